```python
import jax, jax.numpy as jnp
from jax import lax
import numpy as np

D_MODEL = 1024
BATCH = 8
SEQ = 2048
DEPTH = 1

HEAD_DIM = 64
HEADS_PER_GROUP = 8
ATTN_PATTERNS = ((128, 1), (512, 4), (2048, 16))
N_ATTN_GROUPS = len(ATTN_PATTERNS)
ATTN_WIDTH = N_ATTN_GROUPS * HEADS_PER_GROUP * HEAD_DIM
ATTN_OUT = HEADS_PER_GROUP * HEAD_DIM
ATTN_BLOCK = 128
ROPE_THETA = 10000.0
GMLP_CHUNK = 128
GMLP_GROUPS = 8
GMLP_WIDTH = D_MODEL
GMLP_GROUP_DIM = GMLP_WIDTH // GMLP_GROUPS
N_BRANCHES = 2
IN_WIDTH = 3 * ATTN_WIDTH + 2 * GMLP_WIDTH + N_BRANCHES * D_MODEL
D_FF = 2816
ALPHA = (2 * DEPTH) ** 0.25
BETA = (8 * DEPTH) ** -0.25
LN_EPS = 1e-5

kernel_name = 'hybrid_dilated_attn_gmlp_macaron_deepnorm'


def layer_norm(x, g, b):
    xf = x.astype(jnp.float32)
    mu = jnp.mean(xf, -1, keepdims=True)
    var = jnp.mean(jnp.square(xf - mu), -1, keepdims=True)
    y = (xf - mu) * lax.rsqrt(var + LN_EPS) * g.astype(jnp.float32) + b.astype(jnp.float32)
    return y.astype(x.dtype)


def swiglu_ffn(x, w_gate, w_up, w_down):
    return (jax.nn.silu(x @ w_gate) * (x @ w_up)) @ w_down


def rotary(t, cos, sin):
    tf = t.astype(jnp.float32)
    t1, t2 = jnp.split(tf, 2, axis=-1)
    c = cos[:, :, None, None, :]
    s = sin[:, :, None, None, :]
    return jnp.concatenate([t1 * c - t2 * s, t2 * c + t1 * s], axis=-1).astype(t.dtype)


def dilated_window_attention(q, k, v, window, dilation):
    b, s, h, dh = q.shape
    w = window // dilation
    sub_len = s // dilation
    n_blk = -(-sub_len // ATTN_BLOCK)
    pad = n_blk * ATTN_BLOCK - sub_len

    def to_blocks(t):
        t = t.reshape(b, sub_len, dilation, h, dh).transpose(0, 2, 1, 3, 4)
        t = jnp.pad(t, ((0, 0), (0, 0), (0, pad), (0, 0), (0, 0)))
        return t.reshape(b, dilation, n_blk, ATTN_BLOCK, h, dh)

    def with_prev(t):
        prev = jnp.pad(t, ((0, 0), (0, 0), (1, 0), (0, 0), (0, 0), (0, 0)))[:, :, :-1]
        return jnp.concatenate([prev, t], axis=3)

    qb = to_blocks(q)
    kc = with_prev(to_blocks(k))
    vc = with_prev(to_blocks(v))
    scores = jnp.einsum('brnqhd,brnkhd->brnhqk', qb, kc,
                        preferred_element_type=jnp.float32) * (dh ** -0.5)
    blk = np.arange(n_blk)[:, None, None]
    qi = np.arange(ATTN_BLOCK)[None, :, None]
    kj = np.arange(2 * ATTN_BLOCK)[None, None, :]
    dist = qi + ATTN_BLOCK - kj
    kpos = (blk - 1) * ATTN_BLOCK + kj
    mask = (dist >= 0) & (dist <= w) & (kpos >= 0)
    scores = jnp.where(jnp.asarray(mask)[None, None, :, None], scores, -jnp.inf)
    m = jnp.max(scores, axis=-1, keepdims=True)
    p = jnp.exp(scores - m)
    l = jnp.sum(p, axis=-1, keepdims=True)
    o = jnp.einsum('brnhqk,brnkhd->brnqhd', p / l, vc.astype(jnp.float32))
    lse = (m + jnp.log(l))[..., 0].transpose(0, 1, 2, 4, 3)

    def from_blocks(t):
        t = t.reshape((b, dilation, n_blk * ATTN_BLOCK) + t.shape[4:])[:, :, :sub_len]
        t = jnp.swapaxes(t, 1, 2)
        return t.reshape((b, s) + t.shape[3:])

    return from_blocks(o), from_blocks(lse)


def hybrid_mixer(h, cos, sin, w_in, b_gates, gmlp_ln_g, gmlp_ln_b, gmlp_w_s, gmlp_b_s,
                 w_attn_branch, w_gmlp_branch, w_out):
    b, s, _ = h.shape
    proj = h @ w_in
    qkv, z, g = jnp.split(proj, [3 * ATTN_WIDTH, 3 * ATTN_WIDTH + 2 * GMLP_WIDTH], axis=-1)

    qkv = qkv.reshape(b, s, 3, N_ATTN_GROUPS, HEADS_PER_GROUP, HEAD_DIM)
    q = rotary(qkv[:, :, 0], cos, sin)
    k = rotary(qkv[:, :, 1], cos, sin)
    v = qkv[:, :, 2]
    outs, lses = [], []
    for gi, (window, dilation) in enumerate(ATTN_PATTERNS):
        o, lse = dilated_window_attention(q[:, :, gi], k[:, :, gi], v[:, :, gi], window, dilation)
        outs.append(o)
        lses.append(lse)
    wts = jax.nn.softmax(jnp.stack(lses), axis=0)
    y_attn = jnp.sum(wts[..., None] * jnp.stack(outs), axis=0).reshape(b, s, ATTN_OUT).astype(h.dtype)

    u, vg = jnp.split(jax.nn.gelu(z, approximate=False), 2, axis=-1)
    vg = layer_norm(vg, gmlp_ln_g, gmlp_ln_b)
    n_chunk = s // GMLP_CHUNK
    vg = vg.reshape(b, n_chunk, GMLP_CHUNK, GMLP_GROUPS, GMLP_GROUP_DIM)
    w_s = gmlp_w_s * jnp.tril(jnp.ones((GMLP_CHUNK, GMLP_CHUNK), gmlp_w_s.dtype))
    mixed = jnp.einsum('gts,bnsgc->bntgc', w_s, vg) + gmlp_b_s.T[:, :, None]
    y_gmlp = u * mixed.reshape(b, s, GMLP_WIDTH)

    branches = jnp.stack([y_attn @ w_attn_branch, y_gmlp @ w_gmlp_branch], axis=2)
    gates = jax.nn.sigmoid(g.reshape(b, s, N_BRANCHES, D_MODEL) + b_gates.reshape(N_BRANCHES, D_MODEL))
    return jnp.sum(gates * branches, axis=2) @ w_out


def _fwd_setup_inputs(seed: int = 0) -> dict:
    key = jax.random.key(seed)
    ks = jax.random.split(key, 32)

    def nrm(k, shape, scale):
        return jax.random.normal(k, shape, jnp.float32) * scale

    d_s = D_MODEL ** -0.5
    w_in = jnp.concatenate([
        nrm(ks[2], (DEPTH, D_MODEL, ATTN_WIDTH), d_s),
        nrm(ks[3], (DEPTH, D_MODEL, ATTN_WIDTH), d_s),
        nrm(ks[4], (DEPTH, D_MODEL, ATTN_WIDTH), BETA * d_s),
        nrm(ks[5], (DEPTH, D_MODEL, 2 * GMLP_WIDTH), d_s),
        nrm(ks[6], (DEPTH, D_MODEL, N_BRANCHES * D_MODEL), d_s),
    ], axis=-1)
    return {
        'x': jax.random.normal(ks[0], (BATCH, SEQ, D_MODEL), jnp.float32),
        'positions': jnp.broadcast_to(jnp.arange(SEQ, dtype=jnp.int32), (BATCH, SEQ)),
        'ffn1_w_gate': nrm(ks[7], (DEPTH, D_MODEL, D_FF), d_s),
        'ffn1_w_up': nrm(ks[8], (DEPTH, D_MODEL, D_FF), d_s),
        'ffn1_w_down': nrm(ks[9], (DEPTH, D_FF, D_MODEL), BETA * D_FF ** -0.5),
        'ln1_g': 1.0 + nrm(ks[10], (DEPTH, D_MODEL), 0.02),
        'ln1_b': nrm(ks[11], (DEPTH, D_MODEL), 0.02),
        'w_in': w_in,
        'b_gates': nrm(ks[12], (DEPTH, N_BRANCHES * D_MODEL), 0.02),
        'gmlp_ln_g': 1.0 + nrm(ks[13], (DEPTH, GMLP_WIDTH), 0.02),
        'gmlp_ln_b': nrm(ks[14], (DEPTH, GMLP_WIDTH), 0.02),
        'gmlp_w_s': nrm(ks[15], (DEPTH, GMLP_GROUPS, GMLP_CHUNK, GMLP_CHUNK), 0.5 * GMLP_CHUNK ** -0.5),
        'gmlp_b_s': 1.0 + nrm(ks[16], (DEPTH, GMLP_GROUPS, GMLP_CHUNK), 0.02),
        'w_attn_branch': nrm(ks[17], (DEPTH, ATTN_OUT, D_MODEL), BETA * ATTN_OUT ** -0.5),
        'w_gmlp_branch': nrm(ks[18], (DEPTH, GMLP_WIDTH, D_MODEL), BETA * GMLP_WIDTH ** -0.5),
        'w_out': nrm(ks[19], (DEPTH, D_MODEL, D_MODEL), BETA * d_s),
        'ln2_g': 1.0 + nrm(ks[20], (DEPTH, D_MODEL), 0.02),
        'ln2_b': nrm(ks[21], (DEPTH, D_MODEL), 0.02),
        'ffn2_w_gate': nrm(ks[22], (DEPTH, D_MODEL, D_FF), d_s),
        'ffn2_w_up': nrm(ks[23], (DEPTH, D_MODEL, D_FF), d_s),
        'ffn2_w_down': nrm(ks[24], (DEPTH, D_FF, D_MODEL), BETA * D_FF ** -0.5),
        'ln3_g': 1.0 + nrm(ks[25], (DEPTH, D_MODEL), 0.02),
        'ln3_b': nrm(ks[26], (DEPTH, D_MODEL), 0.02),
    }


def _fwd_reference(x, positions, ffn1_w_gate, ffn1_w_up, ffn1_w_down, ln1_g, ln1_b, w_in, b_gates,
              gmlp_ln_g, gmlp_ln_b, gmlp_w_s, gmlp_b_s, w_attn_branch, w_gmlp_branch, w_out,
              ln2_g, ln2_b, ffn2_w_gate, ffn2_w_up, ffn2_w_down, ln3_g, ln3_b):
    inv_freq = ROPE_THETA ** (-jnp.arange(0, HEAD_DIM, 2, dtype=jnp.float32) / HEAD_DIM)
    ang = positions.astype(jnp.float32)[..., None] * inv_freq
    cos, sin = jnp.cos(ang), jnp.sin(ang)
    h = x
    for l in range(DEPTH):
        h = layer_norm(ALPHA * h + 0.5 * swiglu_ffn(h, ffn1_w_gate[l], ffn1_w_up[l], ffn1_w_down[l]),
                       ln1_g[l], ln1_b[l])
        mix = hybrid_mixer(h, cos, sin, w_in[l], b_gates[l], gmlp_ln_g[l], gmlp_ln_b[l], gmlp_w_s[l],
                           gmlp_b_s[l], w_attn_branch[l], w_gmlp_branch[l], w_out[l])
        h = layer_norm(ALPHA * h + mix, ln2_g[l], ln2_b[l])
        h = layer_norm(ALPHA * h + 0.5 * swiglu_ffn(h, ffn2_w_gate[l], ffn2_w_up[l], ffn2_w_down[l]),
                       ln3_g[l], ln3_b[l])
    return h


import jax as _jax
import jax.numpy as _jnp

TWIN_FORMAT = 'train_step'
FWD_PARAMS = ['x', 'positions', 'ffn1_w_gate', 'ffn1_w_up', 'ffn1_w_down', 'ln1_g', 'ln1_b', 'w_in', 'b_gates', 'gmlp_ln_g', 'gmlp_ln_b', 'gmlp_w_s', 'gmlp_b_s', 'w_attn_branch', 'w_gmlp_branch', 'w_out', 'ln2_g', 'ln2_b', 'ffn2_w_gate', 'ffn2_w_up', 'ffn2_w_down', 'ln3_g', 'ln3_b']
TWIN_WEIGHTS = ['ffn1_w_gate', 'ffn1_w_up', 'ffn1_w_down', 'ln1_g', 'ln1_b', 'w_in', 'b_gates', 'gmlp_ln_g', 'gmlp_ln_b', 'gmlp_w_s', 'gmlp_b_s', 'w_attn_branch', 'w_gmlp_branch', 'w_out', 'ln2_g', 'ln2_b', 'ffn2_w_gate', 'ffn2_w_up', 'ffn2_w_down', 'ln3_g', 'ln3_b']
TWIN_DIFF_INPUT = 'x'
TWIN_INPUTS = ['x', 'positions', 'ffn1_w_gate', 'ffn1_w_up', 'ffn1_w_down', 'ln1_g', 'ln1_b', 'w_in', 'b_gates', 'gmlp_ln_g', 'gmlp_ln_b', 'gmlp_w_s', 'gmlp_b_s', 'w_attn_branch', 'w_gmlp_branch', 'w_out', 'ln2_g', 'ln2_b', 'ffn2_w_gate', 'ffn2_w_up', 'ffn2_w_down', 'ln3_g', 'ln3_b', 'loss_target', 'm_ffn1_w_gate', 'm_ffn1_w_up', 'm_ffn1_w_down', 'm_ln1_g', 'm_ln1_b', 'm_w_in', 'm_b_gates', 'm_gmlp_ln_g', 'm_gmlp_ln_b', 'm_gmlp_w_s', 'm_gmlp_b_s', 'm_w_attn_branch', 'm_w_gmlp_branch', 'm_w_out', 'm_ln2_g', 'm_ln2_b', 'm_ffn2_w_gate', 'm_ffn2_w_up', 'm_ffn2_w_down', 'm_ln3_g', 'm_ln3_b', 'v_ffn1_w_gate', 'v_ffn1_w_up', 'v_ffn1_w_down', 'v_ln1_g', 'v_ln1_b', 'v_w_in', 'v_b_gates', 'v_gmlp_ln_g', 'v_gmlp_ln_b', 'v_gmlp_w_s', 'v_gmlp_b_s', 'v_w_attn_branch', 'v_w_gmlp_branch', 'v_w_out', 'v_ln2_g', 'v_ln2_b', 'v_ffn2_w_gate', 'v_ffn2_w_up', 'v_ffn2_w_down', 'v_ln3_g', 'v_ln3_b']
TWIN_OUTPUTS = ['loss', 'grad_x', 'grad_ffn1_w_gate', 'grad_ffn1_w_up', 'grad_ffn1_w_down', 'grad_ln1_g', 'grad_ln1_b', 'grad_w_in', 'grad_b_gates', 'grad_gmlp_ln_g', 'grad_gmlp_ln_b', 'grad_gmlp_w_s', 'grad_gmlp_b_s', 'grad_w_attn_branch', 'grad_w_gmlp_branch', 'grad_w_out', 'grad_ln2_g', 'grad_ln2_b', 'grad_ffn2_w_gate', 'grad_ffn2_w_up', 'grad_ffn2_w_down', 'grad_ln3_g', 'grad_ln3_b', 'delta_ffn1_w_gate', 'delta_ffn1_w_up', 'delta_ffn1_w_down', 'delta_ln1_g', 'delta_ln1_b', 'delta_w_in', 'delta_b_gates', 'delta_gmlp_ln_g', 'delta_gmlp_ln_b', 'delta_gmlp_w_s', 'delta_gmlp_b_s', 'delta_w_attn_branch', 'delta_w_gmlp_branch', 'delta_w_out', 'delta_ln2_g', 'delta_ln2_b', 'delta_ffn2_w_gate', 'delta_ffn2_w_up', 'delta_ffn2_w_down', 'delta_ln3_g', 'delta_ln3_b', 'new_m_ffn1_w_gate', 'new_m_ffn1_w_up', 'new_m_ffn1_w_down', 'new_m_ln1_g', 'new_m_ln1_b', 'new_m_w_in', 'new_m_b_gates', 'new_m_gmlp_ln_g', 'new_m_gmlp_ln_b', 'new_m_gmlp_w_s', 'new_m_gmlp_b_s', 'new_m_w_attn_branch', 'new_m_w_gmlp_branch', 'new_m_w_out', 'new_m_ln2_g', 'new_m_ln2_b', 'new_m_ffn2_w_gate', 'new_m_ffn2_w_up', 'new_m_ffn2_w_down', 'new_m_ln3_g', 'new_m_ln3_b', 'new_v_ffn1_w_gate', 'new_v_ffn1_w_up', 'new_v_ffn1_w_down', 'new_v_ln1_g', 'new_v_ln1_b', 'new_v_w_in', 'new_v_b_gates', 'new_v_gmlp_ln_g', 'new_v_gmlp_ln_b', 'new_v_gmlp_w_s', 'new_v_gmlp_b_s', 'new_v_w_attn_branch', 'new_v_w_gmlp_branch', 'new_v_w_out', 'new_v_ln2_g', 'new_v_ln2_b', 'new_v_ffn2_w_gate', 'new_v_ffn2_w_up', 'new_v_ffn2_w_down', 'new_v_ln3_g', 'new_v_ln3_b']
TWIN_LEAF_KINDS = {'loss': 'loss', 'grad_x': 'grad_x', 'grad_ffn1_w_gate': 'grad_w', 'grad_ffn1_w_up': 'grad_w', 'grad_ffn1_w_down': 'grad_w', 'grad_ln1_g': 'grad_w', 'grad_ln1_b': 'grad_w', 'grad_w_in': 'grad_w', 'grad_b_gates': 'grad_w', 'grad_gmlp_ln_g': 'grad_w', 'grad_gmlp_ln_b': 'grad_w', 'grad_gmlp_w_s': 'grad_w', 'grad_gmlp_b_s': 'grad_w', 'grad_w_attn_branch': 'grad_w', 'grad_w_gmlp_branch': 'grad_w', 'grad_w_out': 'grad_w', 'grad_ln2_g': 'grad_w', 'grad_ln2_b': 'grad_w', 'grad_ffn2_w_gate': 'grad_w', 'grad_ffn2_w_up': 'grad_w', 'grad_ffn2_w_down': 'grad_w', 'grad_ln3_g': 'grad_w', 'grad_ln3_b': 'grad_w', 'delta_ffn1_w_gate': 'delta_w', 'delta_ffn1_w_up': 'delta_w', 'delta_ffn1_w_down': 'delta_w', 'delta_ln1_g': 'delta_w', 'delta_ln1_b': 'delta_w', 'delta_w_in': 'delta_w', 'delta_b_gates': 'delta_w', 'delta_gmlp_ln_g': 'delta_w', 'delta_gmlp_ln_b': 'delta_w', 'delta_gmlp_w_s': 'delta_w', 'delta_gmlp_b_s': 'delta_w', 'delta_w_attn_branch': 'delta_w', 'delta_w_gmlp_branch': 'delta_w', 'delta_w_out': 'delta_w', 'delta_ln2_g': 'delta_w', 'delta_ln2_b': 'delta_w', 'delta_ffn2_w_gate': 'delta_w', 'delta_ffn2_w_up': 'delta_w', 'delta_ffn2_w_down': 'delta_w', 'delta_ln3_g': 'delta_w', 'delta_ln3_b': 'delta_w', 'new_m_ffn1_w_gate': 'new_m', 'new_m_ffn1_w_up': 'new_m', 'new_m_ffn1_w_down': 'new_m', 'new_m_ln1_g': 'new_m', 'new_m_ln1_b': 'new_m', 'new_m_w_in': 'new_m', 'new_m_b_gates': 'new_m', 'new_m_gmlp_ln_g': 'new_m', 'new_m_gmlp_ln_b': 'new_m', 'new_m_gmlp_w_s': 'new_m', 'new_m_gmlp_b_s': 'new_m', 'new_m_w_attn_branch': 'new_m', 'new_m_w_gmlp_branch': 'new_m', 'new_m_w_out': 'new_m', 'new_m_ln2_g': 'new_m', 'new_m_ln2_b': 'new_m', 'new_m_ffn2_w_gate': 'new_m', 'new_m_ffn2_w_up': 'new_m', 'new_m_ffn2_w_down': 'new_m', 'new_m_ln3_g': 'new_m', 'new_m_ln3_b': 'new_m', 'new_v_ffn1_w_gate': 'new_v', 'new_v_ffn1_w_up': 'new_v', 'new_v_ffn1_w_down': 'new_v', 'new_v_ln1_g': 'new_v', 'new_v_ln1_b': 'new_v', 'new_v_w_in': 'new_v', 'new_v_b_gates': 'new_v', 'new_v_gmlp_ln_g': 'new_v', 'new_v_gmlp_ln_b': 'new_v', 'new_v_gmlp_w_s': 'new_v', 'new_v_gmlp_b_s': 'new_v', 'new_v_w_attn_branch': 'new_v', 'new_v_w_gmlp_branch': 'new_v', 'new_v_w_out': 'new_v', 'new_v_ln2_g': 'new_v', 'new_v_ln2_b': 'new_v', 'new_v_ffn2_w_gate': 'new_v', 'new_v_ffn2_w_up': 'new_v', 'new_v_ffn2_w_down': 'new_v', 'new_v_ln3_g': 'new_v', 'new_v_ln3_b': 'new_v'}


def _forward(args):
    return _fwd_reference(*[args[k] for k in FWD_PARAMS])


def _output_shape():
    out = _jax.eval_shape(lambda: _forward(_fwd_setup_inputs(0)))
    return out.shape, out.dtype

N_MICROBATCH = 1
ADAM_LR = 0.001
ADAM_B1 = 0.9
ADAM_B2 = 0.999
ADAM_EPS = 1e-08
ADAM_WD = 0.01
ADAM_STEP = 10
PER_EXAMPLE_BATCH_AXIS = {'x': 0, 'positions': 0, 'loss_target': 0}
SHARED_INPUTS = []
_WEIGHT_DTYPES = {'ffn1_w_gate': _jnp.float32, 'ffn1_w_up': _jnp.float32, 'ffn1_w_down': _jnp.float32, 'ln1_g': _jnp.float32, 'ln1_b': _jnp.float32, 'w_in': _jnp.float32, 'b_gates': _jnp.float32, 'gmlp_ln_g': _jnp.float32, 'gmlp_ln_b': _jnp.float32, 'gmlp_w_s': _jnp.float32, 'gmlp_b_s': _jnp.float32, 'w_attn_branch': _jnp.float32, 'w_gmlp_branch': _jnp.float32, 'w_out': _jnp.float32, 'ln2_g': _jnp.float32, 'ln2_b': _jnp.float32, 'ffn2_w_gate': _jnp.float32, 'ffn2_w_up': _jnp.float32, 'ffn2_w_down': _jnp.float32, 'ln3_g': _jnp.float32, 'ln3_b': _jnp.float32}
MOMENT_SCALE = {'ffn1_w_gate': 1.165712e-02, 'ffn1_w_up': 1.130968e-02, 'ffn1_w_down': 3.151204e-02, 'ln1_g': 5.901029e-01, 'ln1_b': 2.737507e-01, 'w_in': 5.887807e-03, 'b_gates': 4.515999e-03, 'gmlp_ln_g': 4.753456e-03, 'gmlp_ln_b': 4.744638e-03, 'gmlp_w_s': 9.458294e-03, 'gmlp_b_s': 1.400461e-02, 'w_attn_branch': 3.907336e-03, 'w_gmlp_branch': 2.807626e-02, 'w_out': 2.790652e-02, 'ln2_g': 5.973805e-01, 'ln2_b': 2.753592e-01, 'ffn2_w_gate': 1.151275e-02, 'ffn2_w_up': 1.113444e-02, 'ffn2_w_down': 3.109757e-02, 'ln3_g': 1.601883e+01, 'ln3_b': 7.665500e-01}


def _to_microbatches(a, axis):
    t = _jnp.moveaxis(a, axis, 0)
    t = t.reshape((N_MICROBATCH, t.shape[0] // N_MICROBATCH) + t.shape[1:])
    return _jnp.moveaxis(t, 1, axis + 1)


def setup_inputs(seed: int = 0) -> dict:
    inp = _fwd_setup_inputs(seed)
    key = _jax.random.fold_in(_jax.random.key(seed), 7919)
    shape, _ = _output_shape()
    out = dict(inp)
    out["loss_target"] = _jax.random.normal(_jax.random.fold_in(key, 0), shape, _jnp.float32)
    for i, name in enumerate(TWIN_WEIGHTS):
        w = inp[name].astype(_jnp.float32)
        if MOMENT_SCALE is None:
            s = _jnp.sqrt(_jnp.mean(_jnp.square(w)) + 1e-30)
        else:
            s = MOMENT_SCALE[name]
        km, kv = _jax.random.split(_jax.random.fold_in(key, i + 1))
        out[name] = w
        out["m_" + name] = s * _jax.random.normal(km, w.shape, _jnp.float32)
        out["v_" + name] = (s * s) * _jax.random.uniform(kv, w.shape, _jnp.float32, 0.5, 1.5)
    if N_MICROBATCH > 1:
        for name, axis in PER_EXAMPLE_BATCH_AXIS.items():
            out[name] = _to_microbatches(out[name], axis)
    return {'x': out['x'], 'positions': out['positions'], 'ffn1_w_gate': out['ffn1_w_gate'], 'ffn1_w_up': out['ffn1_w_up'], 'ffn1_w_down': out['ffn1_w_down'], 'ln1_g': out['ln1_g'], 'ln1_b': out['ln1_b'], 'w_in': out['w_in'], 'b_gates': out['b_gates'], 'gmlp_ln_g': out['gmlp_ln_g'], 'gmlp_ln_b': out['gmlp_ln_b'], 'gmlp_w_s': out['gmlp_w_s'], 'gmlp_b_s': out['gmlp_b_s'], 'w_attn_branch': out['w_attn_branch'], 'w_gmlp_branch': out['w_gmlp_branch'], 'w_out': out['w_out'], 'ln2_g': out['ln2_g'], 'ln2_b': out['ln2_b'], 'ffn2_w_gate': out['ffn2_w_gate'], 'ffn2_w_up': out['ffn2_w_up'], 'ffn2_w_down': out['ffn2_w_down'], 'ln3_g': out['ln3_g'], 'ln3_b': out['ln3_b'], 'loss_target': out['loss_target'], 'm_ffn1_w_gate': out['m_ffn1_w_gate'], 'm_ffn1_w_up': out['m_ffn1_w_up'], 'm_ffn1_w_down': out['m_ffn1_w_down'], 'm_ln1_g': out['m_ln1_g'], 'm_ln1_b': out['m_ln1_b'], 'm_w_in': out['m_w_in'], 'm_b_gates': out['m_b_gates'], 'm_gmlp_ln_g': out['m_gmlp_ln_g'], 'm_gmlp_ln_b': out['m_gmlp_ln_b'], 'm_gmlp_w_s': out['m_gmlp_w_s'], 'm_gmlp_b_s': out['m_gmlp_b_s'], 'm_w_attn_branch': out['m_w_attn_branch'], 'm_w_gmlp_branch': out['m_w_gmlp_branch'], 'm_w_out': out['m_w_out'], 'm_ln2_g': out['m_ln2_g'], 'm_ln2_b': out['m_ln2_b'], 'm_ffn2_w_gate': out['m_ffn2_w_gate'], 'm_ffn2_w_up': out['m_ffn2_w_up'], 'm_ffn2_w_down': out['m_ffn2_w_down'], 'm_ln3_g': out['m_ln3_g'], 'm_ln3_b': out['m_ln3_b'], 'v_ffn1_w_gate': out['v_ffn1_w_gate'], 'v_ffn1_w_up': out['v_ffn1_w_up'], 'v_ffn1_w_down': out['v_ffn1_w_down'], 'v_ln1_g': out['v_ln1_g'], 'v_ln1_b': out['v_ln1_b'], 'v_w_in': out['v_w_in'], 'v_b_gates': out['v_b_gates'], 'v_gmlp_ln_g': out['v_gmlp_ln_g'], 'v_gmlp_ln_b': out['v_gmlp_ln_b'], 'v_gmlp_w_s': out['v_gmlp_w_s'], 'v_gmlp_b_s': out['v_gmlp_b_s'], 'v_w_attn_branch': out['v_w_attn_branch'], 'v_w_gmlp_branch': out['v_w_gmlp_branch'], 'v_w_out': out['v_w_out'], 'v_ln2_g': out['v_ln2_g'], 'v_ln2_b': out['v_ln2_b'], 'v_ffn2_w_gate': out['v_ffn2_w_gate'], 'v_ffn2_w_up': out['v_ffn2_w_up'], 'v_ffn2_w_down': out['v_ffn2_w_down'], 'v_ln3_g': out['v_ln3_g'], 'v_ln3_b': out['v_ln3_b']}


def _loss(weights, diff, rest, loss_target):
    with _jax.named_scope("forward"):
        args = {**rest, TWIN_DIFF_INPUT: diff, **{k: w.astype(_WEIGHT_DTYPES[k]) for k, w in weights.items()}}
        y = _forward(args)
    with _jax.named_scope("loss_head"):
        err = _jnp.square(y.astype(_jnp.float32) - loss_target)
        return 0.5 * _jnp.sum(_jnp.mean(err, axis=-1)) if err.ndim else 0.5 * err


def _adamw(w, g, m, v):
    m = ADAM_B1 * m + (1.0 - ADAM_B1) * g
    v = ADAM_B2 * v + (1.0 - ADAM_B2) * _jnp.square(g)
    m_hat = m / (1.0 - ADAM_B1 ** ADAM_STEP)
    v_hat = v / (1.0 - ADAM_B2 ** ADAM_STEP)
    delta = -ADAM_LR * (m_hat / (_jnp.sqrt(v_hat) + ADAM_EPS) + ADAM_WD * w)
    return delta, m, v


def reference(x, positions, ffn1_w_gate, ffn1_w_up, ffn1_w_down, ln1_g, ln1_b, w_in, b_gates, gmlp_ln_g, gmlp_ln_b, gmlp_w_s, gmlp_b_s, w_attn_branch, w_gmlp_branch, w_out, ln2_g, ln2_b, ffn2_w_gate, ffn2_w_up, ffn2_w_down, ln3_g, ln3_b, loss_target, m_ffn1_w_gate, m_ffn1_w_up, m_ffn1_w_down, m_ln1_g, m_ln1_b, m_w_in, m_b_gates, m_gmlp_ln_g, m_gmlp_ln_b, m_gmlp_w_s, m_gmlp_b_s, m_w_attn_branch, m_w_gmlp_branch, m_w_out, m_ln2_g, m_ln2_b, m_ffn2_w_gate, m_ffn2_w_up, m_ffn2_w_down, m_ln3_g, m_ln3_b, v_ffn1_w_gate, v_ffn1_w_up, v_ffn1_w_down, v_ln1_g, v_ln1_b, v_w_in, v_b_gates, v_gmlp_ln_g, v_gmlp_ln_b, v_gmlp_w_s, v_gmlp_b_s, v_w_attn_branch, v_w_gmlp_branch, v_w_out, v_ln2_g, v_ln2_b, v_ffn2_w_gate, v_ffn2_w_up, v_ffn2_w_down, v_ln3_g, v_ln3_b):
    given = dict(x=x, positions=positions, ffn1_w_gate=ffn1_w_gate, ffn1_w_up=ffn1_w_up, ffn1_w_down=ffn1_w_down, ln1_g=ln1_g, ln1_b=ln1_b, w_in=w_in, b_gates=b_gates, gmlp_ln_g=gmlp_ln_g, gmlp_ln_b=gmlp_ln_b, gmlp_w_s=gmlp_w_s, gmlp_b_s=gmlp_b_s, w_attn_branch=w_attn_branch, w_gmlp_branch=w_gmlp_branch, w_out=w_out, ln2_g=ln2_g, ln2_b=ln2_b, ffn2_w_gate=ffn2_w_gate, ffn2_w_up=ffn2_w_up, ffn2_w_down=ffn2_w_down, ln3_g=ln3_g, ln3_b=ln3_b, loss_target=loss_target, m_ffn1_w_gate=m_ffn1_w_gate, m_ffn1_w_up=m_ffn1_w_up, m_ffn1_w_down=m_ffn1_w_down, m_ln1_g=m_ln1_g, m_ln1_b=m_ln1_b, m_w_in=m_w_in, m_b_gates=m_b_gates, m_gmlp_ln_g=m_gmlp_ln_g, m_gmlp_ln_b=m_gmlp_ln_b, m_gmlp_w_s=m_gmlp_w_s, m_gmlp_b_s=m_gmlp_b_s, m_w_attn_branch=m_w_attn_branch, m_w_gmlp_branch=m_w_gmlp_branch, m_w_out=m_w_out, m_ln2_g=m_ln2_g, m_ln2_b=m_ln2_b, m_ffn2_w_gate=m_ffn2_w_gate, m_ffn2_w_up=m_ffn2_w_up, m_ffn2_w_down=m_ffn2_w_down, m_ln3_g=m_ln3_g, m_ln3_b=m_ln3_b, v_ffn1_w_gate=v_ffn1_w_gate, v_ffn1_w_up=v_ffn1_w_up, v_ffn1_w_down=v_ffn1_w_down, v_ln1_g=v_ln1_g, v_ln1_b=v_ln1_b, v_w_in=v_w_in, v_b_gates=v_b_gates, v_gmlp_ln_g=v_gmlp_ln_g, v_gmlp_ln_b=v_gmlp_ln_b, v_gmlp_w_s=v_gmlp_w_s, v_gmlp_b_s=v_gmlp_b_s, v_w_attn_branch=v_w_attn_branch, v_w_gmlp_branch=v_w_gmlp_branch, v_w_out=v_w_out, v_ln2_g=v_ln2_g, v_ln2_b=v_ln2_b, v_ffn2_w_gate=v_ffn2_w_gate, v_ffn2_w_up=v_ffn2_w_up, v_ffn2_w_down=v_ffn2_w_down, v_ln3_g=v_ln3_g, v_ln3_b=v_ln3_b)
    weights = {n: given[n] for n in TWIN_WEIGHTS}
    shared = {n: given[n] for n in SHARED_INPUTS}
    per_example = {n: given[n] for n in ['x', 'positions']}
    grad_fn = _jax.value_and_grad(_loss, argnums=(0, 1))

    def one_microbatch(ex, loss_target):
        ex = dict(ex)
        diff = ex.pop(TWIN_DIFF_INPUT)
        return grad_fn(weights, diff, {**shared, **ex}, loss_target)

    if N_MICROBATCH == 1:
        loss, (grad_w, grad_x) = one_microbatch(per_example, given["loss_target"])
    else:
        def body(carry, xs):
            loss_sum, grad_sum = carry
            l_k, (gw_k, gx_k) = one_microbatch(xs[0], xs[1])
            with _jax.named_scope("update"):
                return (loss_sum + l_k, _jax.tree.map(_jnp.add, grad_sum, gw_k)), gx_k

        init = (_jnp.zeros((), _jnp.float32), _jax.tree.map(_jnp.zeros_like, weights))
        (loss, grad_w), grad_x = _jax.lax.scan(body, init, (per_example, given["loss_target"]))
    with _jax.named_scope("update"):
        delta_w, new_m, new_v = {}, {}, {}
        for n in TWIN_WEIGHTS:
            delta_w[n], new_m[n], new_v[n] = _adamw(weights[n], grad_w[n], given["m_" + n], given["v_" + n])
    return (loss, grad_x, *[grad_w[n] for n in TWIN_WEIGHTS], *[delta_w[n] for n in TWIN_WEIGHTS],
            *[new_m[n] for n in TWIN_WEIGHTS], *[new_v[n] for n in TWIN_WEIGHTS])
```

```python
import functools
import math

import jax
import jax.numpy as jnp
from jax import lax
from jax.experimental import pallas as pl
from jax.experimental.pallas import tpu as pltpu

F32 = jnp.float32
BF16 = jnp.bfloat16

S = 2048
D = 1024
NSH = 4
FSH = 704
ATT_W = 1536
GRP_W = 512
NG = 3
NH = 8
DH = 64
BLK = 128
NBLK = S // BLK
GW = 1024
IN_W = 8704
IN_SH = IN_W // NSH
ALPHA = 2.0 ** 0.25
LN_EPS = 1e-5
ROPE_THETA = 10000.0
DILATIONS = (1, 4, 16)
ADAM_LR, ADAM_B1, ADAM_B2, ADAM_EPS, ADAM_WD, ADAM_STEP = 0.001, 0.9, 0.999, 1e-08, 0.01, 10
SMALL_ROWS = 144
MESH_T = pl.DeviceIdType.MESH
MIB = 1024 * 1024
NEG_INF = float("-inf")


def _cp(sem, vmem_mib=48):
    return pltpu.CompilerParams(dimension_semantics=sem, vmem_limit_bytes=vmem_mib * MIB)


def _ln_stats(r):
    mu = jnp.mean(r, axis=-1, keepdims=True)
    xc = r - mu
    var = jnp.mean(xc * xc, axis=-1, keepdims=True)
    rstd = lax.rsqrt(var + LN_EPS)
    return xc * rstd, rstd


def _ln_dx(dxh, xh, rstd):
    m1 = jnp.mean(dxh, axis=-1, keepdims=True)
    m2 = jnp.mean(dxh * xh, axis=-1, keepdims=True)
    return rstd * (dxh - m1 - xh * m2)


def _dot_nt(a, b):
    return lax.dot_general(a, b, (((1,), (1,)), ((), ())), preferred_element_type=F32)


def _dot_tn(a, b):
    return lax.dot_general(a, b, (((0,), (0,)), ((), ())), preferred_element_type=F32)


def _dot(a, b):
    return jnp.dot(a, b, preferred_element_type=F32)


def _ffn_fwd(xin, wg, wu, wd, ln_g, ln_b, name):
    tm = 512

    def body(x_ref, wg_ref, wu_ref, wd_ref, g_ref, b_ref, h_ref, hb_ref, xh_ref, rstd_ref, a_ref, bb_ref, acc_ref):
        j = pl.program_id(1)
        xb = x_ref[...].astype(BF16)
        a = _dot(xb, wg_ref[...])
        b = _dot(xb, wu_ref[...])
        a_ref[...] = a
        bb_ref[...] = b
        s = (a * jax.nn.sigmoid(a)) * b
        f = _dot(s.astype(BF16), wd_ref[...])

        @pl.when(j == 0)
        def _():
            acc_ref[...] = f

        @pl.when(j > 0)
        def _():
            acc_ref[...] += f

        @pl.when(j == NSH - 1)
        def _():
            r = ALPHA * x_ref[...] + 0.5 * acc_ref[...]
            xh, rstd = _ln_stats(r)
            h = xh * g_ref[...] + b_ref[...]
            h_ref[...] = h
            hb_ref[...] = h.astype(BF16)
            xh_ref[...] = xh
            rstd_ref[...] = rstd

    row = pl.BlockSpec((tm, D), lambda i, j: (i, 0))
    vec = pl.BlockSpec((1, D), lambda i, j: (0, 0))
    wcol = pl.BlockSpec((None, D, FSH), lambda i, j: (j, 0, 0))
    ab = pl.BlockSpec((None, tm, FSH), lambda i, j: (j, i, 0))
    return pl.pallas_call(
        body, name=name, grid=(S // tm, NSH),
        in_specs=[row, wcol, wcol, pl.BlockSpec((None, FSH, D), lambda i, j: (j, 0, 0)), vec, vec],
        out_specs=[row, row, row, pl.BlockSpec((tm, 1), lambda i, j: (i, 0)), ab, ab],
        out_shape=[jax.ShapeDtypeStruct((S, D), F32), jax.ShapeDtypeStruct((S, D), BF16),
                   jax.ShapeDtypeStruct((S, D), F32), jax.ShapeDtypeStruct((S, 1), F32),
                   jax.ShapeDtypeStruct((NSH, S, FSH), F32), jax.ShapeDtypeStruct((NSH, S, FSH), F32)],
        scratch_shapes=[pltpu.VMEM((tm, D), F32)],
        compiler_params=_cp(("parallel", "arbitrary")),
    )(xin, wg, wu, wd, ln_g, ln_b)


def _ffn_bwd_act(dr, a, b, wg, wu, wd, name):
    tm = 512

    def body(dr_ref, a_ref, b_ref, wg_ref, wu_ref, wd_ref, da_ref, db_ref, s_ref, dx_ref, acc_ref):
        j = pl.program_id(1)
        df = (0.5 * dr_ref[...]).astype(BF16)
        ds = _dot_nt(df, wd_ref[...])
        av = a_ref[...]
        bv = b_ref[...]
        sig = jax.nn.sigmoid(av)
        sl = av * sig
        da = (ds * bv * (sig * (1.0 + av * (1.0 - sig)))).astype(BF16)
        db = (ds * sl).astype(BF16)
        da_ref[...] = da
        db_ref[...] = db
        s_ref[...] = (sl * bv).astype(BF16)
        dx = _dot_nt(da, wg_ref[...]) + _dot_nt(db, wu_ref[...])

        @pl.when(j == 0)
        def _():
            acc_ref[...] = ALPHA * dr_ref[...] + dx

        @pl.when(j > 0)
        def _():
            acc_ref[...] += dx

        @pl.when(j == NSH - 1)
        def _():
            dx_ref[...] = acc_ref[...]

    row = pl.BlockSpec((tm, D), lambda i, j: (i, 0))
    wcol = pl.BlockSpec((None, D, FSH), lambda i, j: (j, 0, 0))
    ab = pl.BlockSpec((None, tm, FSH), lambda i, j: (j, i, 0))
    sh = jax.ShapeDtypeStruct((NSH, S, FSH), BF16)
    return pl.pallas_call(
        body, name=name, grid=(S // tm, NSH),
        in_specs=[row, ab, ab, wcol, wcol, pl.BlockSpec((None, FSH, D), lambda i, j: (j, 0, 0))],
        out_specs=[ab, ab, ab, row],
        out_shape=[sh, sh, sh, jax.ShapeDtypeStruct((S, D), F32)],
        scratch_shapes=[pltpu.VMEM((tm, D), F32)],
        compiler_params=_cp(("parallel", "arbitrary")),
    )(dr, a, b, wg, wu, wd)


def _matmul(a, b, mode, name, *, n, tm=512, tn=512, tk=None, b_col0=0, add=None, add_scale=1.0, out_dtype=F32):
    m, ka = a.shape
    tk = ka if tk is None else tk
    nk = ka // tk
    assert m % tm == 0 and n % tn == 0 and ka % tk == 0 and b_col0 % tn == 0
    off = b_col0 // tn

    def body(*refs):
        if add is None:
            a_ref, b_ref, o_ref = refs[:3]
            add_ref = None
            rest = refs[3:]
        else:
            a_ref, b_ref, add_ref, o_ref = refs[:4]
            rest = refs[4:]
        k = pl.program_id(2)
        av = a_ref[...].astype(BF16)
        bv = b_ref[...].astype(BF16)
        p = _dot(av, bv) if mode == "nn" else _dot_nt(av, bv)

        def finish(acc):
            if add_ref is not None:
                acc = acc + add_scale * add_ref[...]
            o_ref[...] = acc.astype(out_dtype)

        if nk == 1:
            finish(p)
        else:
            acc_ref = rest[0]

            @pl.when(k == 0)
            def _():
                acc_ref[...] = p

            @pl.when(k > 0)
            def _():
                acc_ref[...] += p

            @pl.when(k == nk - 1)
            def _():
                finish(acc_ref[...])

    a_spec = pl.BlockSpec((tm, tk), lambda i, j, k: (i, k))
    if mode == "nn":
        b_spec = pl.BlockSpec((tk, tn), lambda i, j, k: (k, j + off))
    else:
        b_spec = pl.BlockSpec((tn, tk), lambda i, j, k: (j, k))
    o_spec = pl.BlockSpec((tm, tn), lambda i, j, k: (i, j))
    in_specs = [a_spec, b_spec] + ([o_spec] if add is not None else [])
    args = (a, b) + ((add,) if add is not None else ())
    return pl.pallas_call(
        body, name=name, grid=(m // tm, n // tn, nk),
        in_specs=in_specs, out_specs=o_spec,
        out_shape=jax.ShapeDtypeStruct((m, n), out_dtype),
        scratch_shapes=[pltpu.VMEM((tm, tn), F32)] if nk > 1 else [],
        compiler_params=_cp(("parallel", "parallel", "arbitrary")),
    )(*args)


def _wgrad_cols(xb, y, rh, c, name, y_stacked=False):
    ts = 512

    def body(x_ref, y_ref, o_ref, acc_ref):
        k = pl.program_id(2)
        p = _dot_tn(x_ref[...].astype(BF16), y_ref[...].astype(BF16))

        @pl.when(k == 0)
        def _():
            acc_ref[...] = p

        @pl.when(k > 0)
        def _():
            acc_ref[...] += p

        @pl.when(k == S // ts - 1)
        def _():
            o_ref[...] = acc_ref[...]

    if y_stacked:
        y_spec = pl.BlockSpec((None, ts, c), lambda h, j, k: (j, k, 0))
    else:
        y_spec = pl.BlockSpec((ts, c), lambda h, j, k: (k, j))
    return pl.pallas_call(
        body, name=name, grid=(2, NSH, S // ts),
        in_specs=[pl.BlockSpec((ts, rh), lambda h, j, k: (k, h)), y_spec],
        out_specs=pl.BlockSpec((None, None, rh, c), lambda h, j, k: (h, j, 0, 0)),
        out_shape=jax.ShapeDtypeStruct((2, NSH, rh, c), F32),
        scratch_shapes=[pltpu.VMEM((rh, c), F32)],
        compiler_params=_cp(("parallel", "parallel", "arbitrary")),
    )(xb, y)


def _wgrad_rows(xb, y, r, name, x_stacked=False, y_scale=1.0):
    ts = 512
    rh = r // 2
    c = y.shape[-1]

    def body(x_ref, y_ref, o_ref, acc_ref):
        k = pl.program_id(1)
        p = _dot_tn(x_ref[...].astype(BF16), (y_ref[...] * y_scale).astype(BF16))

        @pl.when(k == 0)
        def _():
            acc_ref[...] = p

        @pl.when(k > 0)
        def _():
            acc_ref[...] += p

        @pl.when(k == S // ts - 1)
        def _():
            o_ref[0] = acc_ref[0:rh, :]
            o_ref[1] = acc_ref[rh:r, :]

    if x_stacked:
        x_spec = pl.BlockSpec((None, ts, r), lambda j, k: (j, k, 0))
    else:
        x_spec = pl.BlockSpec((ts, r), lambda j, k: (k, j))
    return pl.pallas_call(
        body, name=name, grid=(NSH, S // ts),
        in_specs=[x_spec, pl.BlockSpec((ts, c), lambda j, k: (k, 0))],
        out_specs=pl.BlockSpec((2, None, rh, c), lambda j, k: (0, j, 0, 0)),
        out_shape=jax.ShapeDtypeStruct((2, NSH, rh, c), F32),
        scratch_shapes=[pltpu.VMEM((r, c), F32)],
        compiler_params=_cp(("parallel", "arbitrary")),
    )(xb, y)


def _resid_ln(res, f, ln_g, ln_b, name):
    tm = 256

    def body(res_ref, f_ref, g_ref, b_ref, h_ref, hb_ref, xh_ref, rstd_ref):
        r = ALPHA * res_ref[...] + f_ref[...]
        xh, rstd = _ln_stats(r)
        h = xh * g_ref[...] + b_ref[...]
        h_ref[...] = h
        hb_ref[...] = h.astype(BF16)
        xh_ref[...] = xh
        rstd_ref[...] = rstd

    row = pl.BlockSpec((tm, D), lambda i: (i, 0))
    vec = pl.BlockSpec((1, D), lambda i: (0, 0))
    return pl.pallas_call(
        body, name=name, grid=(S // tm,),
        in_specs=[row, row, vec, vec],
        out_specs=[row, row, row, pl.BlockSpec((tm, 1), lambda i: (i, 0))],
        out_shape=[jax.ShapeDtypeStruct((S, D), F32), jax.ShapeDtypeStruct((S, D), BF16),
                   jax.ShapeDtypeStruct((S, D), F32), jax.ShapeDtypeStruct((S, 1), F32)],
        compiler_params=_cp(("parallel",)),
    )(res, f, ln_g, ln_b)


def _ln_bwd(dout, xh, rstd, ln_g, name, target=None):
    tm = 256
    with_loss = target is not None

    def body(*refs):
        if with_loss:
            y_ref, t_ref, xh_ref, rstd_ref, g_ref, dr_ref, dg_ref, db_ref, loss_ref = refs
            err = y_ref[...] - t_ref[...]
            dy = err * (1.0 / D)
        else:
            y_ref, xh_ref, rstd_ref, g_ref, dr_ref, dg_ref, db_ref = refs
            dy = y_ref[...]
        i = pl.program_id(0)
        xh = xh_ref[...]
        dr_ref[...] = _ln_dx(dy * g_ref[...], xh, rstd_ref[...])
        dg = jnp.sum(dy * xh, axis=0, keepdims=True)
        db = jnp.sum(dy, axis=0, keepdims=True)

        @pl.when(i == 0)
        def _():
            dg_ref[...] = dg
            db_ref[...] = db

        @pl.when(i > 0)
        def _():
            dg_ref[...] += dg
            db_ref[...] += db

        if with_loss:
            part = 0.5 * jnp.sum(jnp.mean(err * err, axis=-1, keepdims=True), axis=0, keepdims=True)
            part = jnp.broadcast_to(part, (8, 128))

            @pl.when(i == 0)
            def _():
                loss_ref[...] = part

            @pl.when(i > 0)
            def _():
                loss_ref[...] += part

    row = pl.BlockSpec((tm, D), lambda i: (i, 0))
    vec = pl.BlockSpec((1, D), lambda i: (0, 0))
    col = pl.BlockSpec((tm, 1), lambda i: (i, 0))
    in_specs = [row] + ([row] if with_loss else []) + [row, col, vec]
    out_specs = [row, vec, vec] + ([pl.BlockSpec((8, 128), lambda i: (0, 0))] if with_loss else [])
    out_shape = [jax.ShapeDtypeStruct((S, D), F32), jax.ShapeDtypeStruct((1, D), F32),
                 jax.ShapeDtypeStruct((1, D), F32)] + ([jax.ShapeDtypeStruct((8, 128), F32)] if with_loss else [])
    args = (dout,) + ((target,) if with_loss else ()) + (xh, rstd, ln_g)
    return pl.pallas_call(
        body, name=name, grid=(S // tm,), in_specs=in_specs, out_specs=out_specs, out_shape=out_shape,
        compiler_params=_cp(("arbitrary",)),
    )(*args)


def _rope(t, pos_f, invf, sign, name):
    tm = 256
    nch = ATT_W // 128

    def body(t_ref, pos_ref, invf_ref, o_ref):
        sec = pl.program_id(1)
        ang = pos_ref[...] * invf_ref[...]
        lane = lax.broadcasted_iota(jnp.int32, (tm, 128), 1)
        first = (lane % DH) < (DH // 2)
        rot = sec < 2
        cosf = jnp.where(rot, jnp.cos(ang), 1.0)
        sinv = jnp.where(rot, jnp.sin(ang), 0.0) * sign
        sinf = jnp.where(first, -sinv, sinv)
        for ch in range(nch):
            x = t_ref[:, ch * 128:(ch + 1) * 128]
            sw = jnp.where(first, pltpu.roll(x, 96, 1), pltpu.roll(x, 32, 1))
            o_ref[:, ch * 128:(ch + 1) * 128] = (x * cosf + sw * sinf).astype(BF16)

    return pl.pallas_call(
        body, name=name, grid=(S // tm, 3),
        in_specs=[pl.BlockSpec((tm, ATT_W), lambda i, s: (i, s)), pl.BlockSpec((tm, 1), lambda i, s: (i, 0)),
                  pl.BlockSpec((1, 128), lambda i, s: (0, 0))],
        out_specs=pl.BlockSpec((tm, ATT_W), lambda i, s: (i, s)),
        out_shape=jax.ShapeDtypeStruct((S, 3 * ATT_W), BF16),
        compiler_params=_cp(("parallel", "parallel")),
    )(t, pos_f, invf)


def _blocks_per_class(g):
    return jnp.where(g == 0, NBLK // DILATIONS[0], jnp.where(g == 1, NBLK // DILATIONS[1], NBLK // DILATIONS[2]))


def _attn_fwd(qp, kp, vp, name):
    def body(q_ref, kc_ref, kp_ref, vc_ref, vp_ref, o_ref, lse_ref):
        g = pl.program_id(0)
        b = pl.program_id(1)
        has_prev = lax.rem(b, _blocks_per_class(g)) != 0
        qi = lax.broadcasted_iota(jnp.int32, (BLK, BLK), 0)
        kj = lax.broadcasted_iota(jnp.int32, (BLK, BLK), 1)
        mask_c = kj <= qi
        mask_p = jnp.logical_and(kj >= qi, has_prev)
        for h in range(NH):
            sl = slice(h * DH, (h + 1) * DH)
            q = q_ref[:, sl]
            sc = jnp.where(mask_c, _dot_nt(q, kc_ref[:, sl]) * 0.125, NEG_INF)
            sp = jnp.where(mask_p, _dot_nt(q, kp_ref[:, sl]) * 0.125, NEG_INF)
            m = jnp.maximum(jnp.max(sc, axis=-1, keepdims=True), jnp.max(sp, axis=-1, keepdims=True))
            pc = jnp.exp(sc - m)
            pp = jnp.exp(sp - m)
            l = jnp.sum(pc, axis=-1, keepdims=True) + jnp.sum(pp, axis=-1, keepdims=True)
            o = _dot(pc.astype(BF16), vc_ref[:, sl]) + _dot(pp.astype(BF16), vp_ref[:, sl])
            o_ref[:, sl] = o / l
            lse_ref[:, sl] = jnp.broadcast_to(m + jnp.log(l), (BLK, DH))

    cur = pl.BlockSpec((None, BLK, GRP_W), lambda g, b: (g, b, 0))
    prev = pl.BlockSpec((None, BLK, GRP_W), lambda g, b: (g, jnp.maximum(b - 1, 0), 0))
    shp = jax.ShapeDtypeStruct((NG, S, GRP_W), F32)
    return pl.pallas_call(
        body, name=name, grid=(NG, NBLK),
        in_specs=[cur, cur, prev, cur, prev], out_specs=[cur, cur], out_shape=[shp, shp],
        compiler_params=_cp(("parallel", "parallel")),
    )(qp, kp, kp, vp, vp)


def _attn_combine(o3, lse3, name):
    tm = 256

    def body(o_ref, lse_ref, y_ref, l_ref):
        l0, l1, l2 = lse_ref[0], lse_ref[1], lse_ref[2]
        m = jnp.maximum(jnp.maximum(l0, l1), l2)
        e0, e1, e2 = jnp.exp(l0 - m), jnp.exp(l1 - m), jnp.exp(l2 - m)
        den = e0 + e1 + e2
        y_ref[...] = (e0 * o_ref[0] + e1 * o_ref[1] + e2 * o_ref[2]) / den
        l_ref[...] = m + jnp.log(den)

    blk3 = pl.BlockSpec((NG, tm, GRP_W), lambda i: (0, i, 0))
    blk = pl.BlockSpec((tm, GRP_W), lambda i: (i, 0))
    shp = jax.ShapeDtypeStruct((S, GRP_W), F32)
    return pl.pallas_call(
        body, name=name, grid=(S // tm,), in_specs=[blk3, blk3], out_specs=[blk, blk], out_shape=[shp, shp],
        compiler_params=_cp(("parallel",)),
    )(o3, lse3)


def _attn_bwd(qp, kp, vp, dyp, yp, lp, name):
    def body(q_ref, qn_ref, k_ref, kp_ref, v_ref, vp_ref, dy_ref, dyn_ref, y_ref, yn_ref, l_ref, ln_ref,
             dq_ref, dk_ref, dv_ref):
        g = pl.program_id(0)
        b = pl.program_id(1)
        per = _blocks_per_class(g)
        has_prev = lax.rem(b, per) != 0
        has_next = lax.rem(b + 1, per) != 0
        qi = lax.broadcasted_iota(jnp.int32, (BLK, BLK), 0)
        kj = lax.broadcasted_iota(jnp.int32, (BLK, BLK), 1)
        mask_c = kj <= qi
        mask_p = jnp.logical_and(kj >= qi, has_prev)
        mask_n = jnp.logical_and(kj >= qi, has_next)
        for h in range(NH):
            sl = slice(h * DH, (h + 1) * DH)
            q, qn, k, kpv, v, vpv = q_ref[:, sl], qn_ref[:, sl], k_ref[:, sl], kp_ref[:, sl], v_ref[:, sl], vp_ref[:, sl]
            dy, dyn = dy_ref[:, sl], dyn_ref[:, sl]
            dd = jnp.sum(dy * y_ref[:, sl], axis=-1, keepdims=True)
            ddn = jnp.sum(dyn * yn_ref[:, sl], axis=-1, keepdims=True)
            lcol = l_ref[:, h * DH:h * DH + 1]
            lncol = ln_ref[:, h * DH:h * DH + 1]
            dyb, dynb = dy.astype(BF16), dyn.astype(BF16)
            p = jnp.exp(jnp.where(mask_c, _dot_nt(q, k) * 0.125, NEG_INF) - lcol)
            ds = (p * (_dot_nt(dyb, v) - dd)).astype(BF16)
            dq = _dot(ds, k)
            dk = _dot_tn(ds, q)
            dv = _dot_tn(p.astype(BF16), dyb)
            pp = jnp.exp(jnp.where(mask_p, _dot_nt(q, kpv) * 0.125, NEG_INF) - lcol)
            dsp = (pp * (_dot_nt(dyb, vpv) - dd)).astype(BF16)
            dq = dq + _dot(dsp, kpv)
            pn = jnp.exp(jnp.where(mask_n, _dot_nt(qn, k) * 0.125, NEG_INF) - lncol)
            dsn = (pn * (_dot_nt(dynb, v) - ddn)).astype(BF16)
            dk = dk + _dot_tn(dsn, qn)
            dv = dv + _dot_tn(pn.astype(BF16), dynb)
            dq_ref[:, sl] = dq * 0.125
            dk_ref[:, sl] = dk * 0.125
            dv_ref[:, sl] = dv

    cur = pl.BlockSpec((None, BLK, GRP_W), lambda g, b: (g, b, 0))
    prev = pl.BlockSpec((None, BLK, GRP_W), lambda g, b: (g, jnp.maximum(b - 1, 0), 0))
    nxt = pl.BlockSpec((None, BLK, GRP_W), lambda g, b: (g, jnp.minimum(b + 1, NBLK - 1), 0))
    shp = jax.ShapeDtypeStruct((NG, S, GRP_W), F32)
    return pl.pallas_call(
        body, name=name, grid=(NG, NBLK),
        in_specs=[cur, nxt, cur, prev, cur, prev, cur, nxt, cur, nxt, cur, nxt],
        out_specs=[cur, cur, cur], out_shape=[shp, shp, shp],
        compiler_params=_cp(("parallel", "parallel")),
    )(qp, qp, kp, kp, vp, vp, dyp, dyp, yp, yp, lp, lp)


_SQRT_HALF = 0.7071067811865476
_INV_SQRT_2PI = 0.3989422804014327


def _gelu(z):
    return 0.5 * z * (1.0 + lax.erf(z * _SQRT_HALF))


def _gelu_grad(z):
    return 0.5 * (1.0 + lax.erf(z * _SQRT_HALF)) + z * (jnp.exp(-0.5 * z * z) * _INV_SQRT_2PI)


def _tril_mask():
    t = lax.broadcasted_iota(jnp.int32, (BLK, BLK), 0)
    s = lax.broadcasted_iota(jnp.int32, (BLK, BLK), 1)
    return s <= t


def _gmlp_fwd(z, ln_g, ln_b, w_s, b_s_t, name):
    def body(z_ref, g_ref, b_ref, ws_ref, bs_ref, y_ref):
        zg = _gelu(z_ref[...])
        u = zg[:, :GW]
        xh, _ = _ln_stats(zg[:, GW:])
        vn = (xh * g_ref[...] + b_ref[...]).astype(BF16)
        tril = _tril_mask()
        for gg in range(8):
            sl = slice(gg * BLK, (gg + 1) * BLK)
            wt = jnp.where(tril, ws_ref[gg], 0.0).astype(BF16)
            mixed = _dot(wt, vn[:, sl]) + bs_ref[:, gg:gg + 1]
            y_ref[:, sl] = u[:, sl] * mixed

    vec = pl.BlockSpec((1, GW), lambda n: (0, 0))
    return pl.pallas_call(
        body, name=name, grid=(NBLK,),
        in_specs=[pl.BlockSpec((BLK, 2 * GW), lambda n: (n, 0)), vec, vec,
                  pl.BlockSpec((8, BLK, BLK), lambda n: (0, 0, 0)), pl.BlockSpec((BLK, 8), lambda n: (0, 0))],
        out_specs=pl.BlockSpec((BLK, GW), lambda n: (n, 0)),
        out_shape=jax.ShapeDtypeStruct((S, GW), F32),
        compiler_params=_cp(("parallel",)),
    )(z, ln_g, ln_b, w_s, b_s_t)


def _gmlp_bwd(z, dy, ln_g, ln_b, w_s, b_s_t, name):
    def body(z_ref, dy_ref, g_ref, b_ref, ws_ref, bs_ref, dz_ref, dws_ref, dbs_ref, dg_ref, db_ref, dvn_ref):
        n = pl.program_id(0)
        zv = z_ref[...]
        zg = _gelu(zv)
        u = zg[:, :GW]
        xh, rstd = _ln_stats(zg[:, GW:])
        vn = (xh * g_ref[...] + b_ref[...]).astype(BF16)
        tril = _tril_mask()

        @pl.when(n == 0)
        def _():
            dws_ref[...] = jnp.zeros_like(dws_ref)
            dbs_ref[...] = jnp.zeros_like(dbs_ref)
            dg_ref[...] = jnp.zeros_like(dg_ref)
            db_ref[...] = jnp.zeros_like(db_ref)

        for gg in range(8):
            sl = slice(gg * BLK, (gg + 1) * BLK)
            wt = jnp.where(tril, ws_ref[gg], 0.0).astype(BF16)
            dyg = dy_ref[:, sl]
            mixed = _dot(wt, vn[:, sl]) + bs_ref[:, gg:gg + 1]
            dz_ref[:, sl] = (dyg * mixed * _gelu_grad(zv[:, sl])).astype(BF16)
            dmix = dyg * u[:, sl]
            dmb = dmix.astype(BF16)
            dws_ref[gg] += jnp.where(tril, _dot_nt(dmb, vn[:, sl]), 0.0)
            dbs_ref[:, gg:gg + 1] += jnp.sum(dmix, axis=-1, keepdims=True)
            dvn_ref[:, sl] = _dot_tn(wt, dmb)

        dvn = dvn_ref[...]
        dg_ref[...] += jnp.sum(dvn * xh, axis=0, keepdims=True)
        db_ref[...] += jnp.sum(dvn, axis=0, keepdims=True)
        dvg = _ln_dx(dvn * g_ref[...], xh, rstd)
        dz_ref[:, GW:] = (dvg * _gelu_grad(zv[:, GW:])).astype(BF16)

    vec = pl.BlockSpec((1, GW), lambda n: (0, 0))
    ws = pl.BlockSpec((8, BLK, BLK), lambda n: (0, 0, 0))
    bs = pl.BlockSpec((BLK, 8), lambda n: (0, 0))
    return pl.pallas_call(
        body, name=name, grid=(NBLK,),
        in_specs=[pl.BlockSpec((BLK, 2 * GW), lambda n: (n, 0)), pl.BlockSpec((BLK, GW), lambda n: (n, 0)),
                  vec, vec, ws, bs],
        out_specs=[pl.BlockSpec((BLK, 2 * GW), lambda n: (n, 0)), ws, bs, vec, vec],
        out_shape=[jax.ShapeDtypeStruct((S, 2 * GW), BF16), jax.ShapeDtypeStruct((8, BLK, BLK), F32),
                   jax.ShapeDtypeStruct((BLK, 8), F32), jax.ShapeDtypeStruct((1, GW), F32),
                   jax.ShapeDtypeStruct((1, GW), F32)],
        scratch_shapes=[pltpu.VMEM((BLK, GW), F32)],
        compiler_params=_cp(("arbitrary",)),
    )(z, dy, ln_g, ln_b, w_s, b_s_t)


def _merge_fwd(a, b, gl, b_gates, name):
    tm = 256

    def body(a_ref, b_ref, g0_ref, g1_ref, bg_ref, o_ref):
        g0 = jax.nn.sigmoid(g0_ref[...] + bg_ref[:, :D])
        g1 = jax.nn.sigmoid(g1_ref[...] + bg_ref[:, D:])
        o_ref[...] = (g0 * a_ref[...] + g1 * b_ref[...]).astype(BF16)

    row = pl.BlockSpec((tm, D), lambda i: (i, 0))
    return pl.pallas_call(
        body, name=name, grid=(S // tm,),
        in_specs=[row, row, row, pl.BlockSpec((tm, D), lambda i: (i, 1)), pl.BlockSpec((1, 2 * D), lambda i: (0, 0))],
        out_specs=row, out_shape=jax.ShapeDtypeStruct((S, D), BF16),
        compiler_params=_cp(("parallel",)),
    )(a, b, gl, gl, b_gates)


def _merge_bwd(dm, a, b, gl, b_gates, name):
    tm = 256

    def body(dm_ref, a_ref, b_ref, g0_ref, g1_ref, bg_ref, da_ref, db_ref, dgl_ref, dbg_ref):
        i = pl.program_id(0)
        dmv = dm_ref[...]
        g0 = jax.nn.sigmoid(g0_ref[...] + bg_ref[:, :D])
        g1 = jax.nn.sigmoid(g1_ref[...] + bg_ref[:, D:])
        da_ref[...] = (dmv * g0).astype(BF16)
        db_ref[...] = (dmv * g1).astype(BF16)
        d0 = dmv * a_ref[...] * g0 * (1.0 - g0)
        d1 = dmv * b_ref[...] * g1 * (1.0 - g1)
        dgl_ref[:, :D] = d0.astype(BF16)
        dgl_ref[:, D:] = d1.astype(BF16)
        s0 = jnp.sum(d0, axis=0, keepdims=True)
        s1 = jnp.sum(d1, axis=0, keepdims=True)

        @pl.when(i == 0)
        def _():
            dbg_ref[:, :D] = s0
            dbg_ref[:, D:] = s1

        @pl.when(i > 0)
        def _():
            dbg_ref[:, :D] += s0
            dbg_ref[:, D:] += s1

    row = pl.BlockSpec((tm, D), lambda i: (i, 0))
    wide = pl.BlockSpec((tm, 2 * D), lambda i: (i, 0))
    bg = pl.BlockSpec((1, 2 * D), lambda i: (0, 0))
    return pl.pallas_call(
        body, name=name, grid=(S // tm,),
        in_specs=[row, row, row, row, pl.BlockSpec((tm, D), lambda i: (i, 1)), bg],
        out_specs=[row, row, wide, bg],
        out_shape=[jax.ShapeDtypeStruct((S, D), BF16), jax.ShapeDtypeStruct((S, D), BF16),
                   jax.ShapeDtypeStruct((S, 2 * D), BF16), jax.ShapeDtypeStruct((1, 2 * D), F32)],
        compiler_params=_cp(("arbitrary",)),
    )(dm, a, b, gl, gl, b_gates)


def _adam_math(w, g, m, v):
    m2 = ADAM_B1 * m + (1.0 - ADAM_B1) * g
    v2 = ADAM_B2 * v + (1.0 - ADAM_B2) * (g * g)
    m_hat = m2 / (1.0 - ADAM_B1 ** ADAM_STEP)
    v_hat = v2 / (1.0 - ADAM_B2 ** ADAM_STEP)
    delta = -ADAM_LR * (m_hat / (jnp.sqrt(v_hat) + ADAM_EPS) + ADAM_WD * w)
    return delta, m2, v2


def _pick_rows(rows, cols, unit=16, budget=MIB):
    best = unit
    for t in range(unit, rows + 1, unit):
        if rows % t == 0 and t * cols * 4 <= budget:
            best = t
    assert rows % best == 0
    return best


def _adamw(w, g, m, v, name):
    r, c = w.shape
    tr = _pick_rows(r, c, unit=8)

    def body(w_ref, g_ref, m_ref, v_ref, go_ref, d_ref, mo_ref, vo_ref):
        gv = g_ref[...]
        delta, m2, v2 = _adam_math(w_ref[...], gv, m_ref[...], v_ref[...])
        go_ref[...] = gv
        d_ref[...] = delta
        mo_ref[...] = m2
        vo_ref[...] = v2

    blk = pl.BlockSpec((tr, c), lambda i: (i, 0))
    shp = jax.ShapeDtypeStruct((r, c), F32)
    return pl.pallas_call(
        body, name=name, grid=(r // tr,), in_specs=[blk] * 4, out_specs=[blk] * 4, out_shape=[shp] * 4,
        compiler_params=_cp(("parallel",)),
    )(w, g, m, v)


def _small_sum_adamw(parts, w, m, v, name):
    tr = 48

    def body(p_ref, w_ref, m_ref, v_ref, g_ref, d_ref, mo_ref, vo_ref):
        gv = p_ref[0]
        for k in range(1, 8):
            gv = gv + p_ref[k]
        delta, m2, v2 = _adam_math(w_ref[...], gv, m_ref[...], v_ref[...])
        g_ref[...] = gv
        d_ref[...] = delta
        mo_ref[...] = m2
        vo_ref[...] = v2

    blk = pl.BlockSpec((tr, D), lambda i: (i, 0))
    shp = jax.ShapeDtypeStruct((SMALL_ROWS, D), F32)
    return pl.pallas_call(
        body, name=name, grid=(SMALL_ROWS // tr,),
        in_specs=[pl.BlockSpec((8, tr, D), lambda i: (0, i, 0)), blk, blk, blk],
        out_specs=[blk] * 4, out_shape=[shp] * 4,
        compiler_params=_cp(("parallel",)),
    )(parts, w, m, v)


ANY = pl.BlockSpec(memory_space=pl.ANY)


def _mesh_pos():
    x, y, c = lax.axis_index("x"), lax.axis_index("y"), lax.axis_index("c")
    chips = [(1 - x, y), (x, 1 - y), (1 - x, 1 - y)]
    return x, y, c, chips


def _gather_weights(shards, kinds):
    n = len(shards)
    dims = [s.shape for s in shards]

    def window(ref, kind, j, h, r, c):
        rows = pl.ds(pl.multiple_of(h * (r // 2), 16), r // 2)
        if kind == "stack":
            return ref.at[j, rows, :]
        return ref.at[rows, pl.ds(pl.multiple_of(j * c, 128), c)]

    def whole(ref, kind, j, r, c):
        if kind == "stack":
            return ref.at[j]
        return ref.at[:, pl.ds(pl.multiple_of(j * c, 128), c)]

    def body(*refs):
        ins, outs = refs[:n], refs[n:2 * n]
        send_sems, recv_sems, loc_sems = refs[2 * n:]
        x, y, c, chips = _mesh_pos()
        me = 2 * x + y
        sib = (x, y, 1 - c)

        def rc(a, k, j, h, to, src=None):
            r, cc = dims[a]
            dst = window(outs[a], kinds[a], j, h, r, cc)
            return pltpu.make_async_remote_copy(
                src_ref=dst if src is None else src, dst_ref=dst,
                send_sem=send_sems.at[a * 6 + k], recv_sem=recv_sems.at[a * 6 + k],
                device_id=to, device_id_type=MESH_T)

        local, first, passed = [], [], []
        for a in range(n):
            r, cc = dims[a]
            lc = pltpu.make_async_copy(ins[a], whole(outs[a], kinds[a], me, r, cc), loc_sems.at[a])
            lc.start()
            local.append(lc)
            half = ins[a].at[pl.ds(pl.multiple_of(c * (r // 2), 16), r // 2), :]
            for k, chip in enumerate(chips):
                cp = rc(a, k, me, c, (chip[0], chip[1], c), src=half)
                cp.start()
                first.append(cp)
        for a in range(n):
            for k, chip in enumerate(chips):
                j = 2 * chip[0] + chip[1]
                rc(a, k, j, c, sib).wait_recv()
                fw = rc(a, 3 + k, j, c, sib)
                fw.start()
                passed.append(fw)
        for a in range(n):
            for k, chip in enumerate(chips):
                j = 2 * chip[0] + chip[1]
                rc(a, 3 + k, j, 1 - c, sib).wait_recv()
        for cp in first + passed:
            cp.wait_send()
        for lc in local:
            lc.wait()

    out_shape = []
    for (r, cc), kind in zip(dims, kinds):
        out_shape.append(jax.ShapeDtypeStruct((NSH, r, cc) if kind == "stack" else (r, NSH * cc), BF16))
    return pl.pallas_call(
        body, name="gather_weights", in_specs=[ANY] * n, out_specs=[ANY] * n, out_shape=out_shape,
        scratch_shapes=[pltpu.SemaphoreType.DMA((6 * n,)), pltpu.SemaphoreType.DMA((6 * n,)),
                        pltpu.SemaphoreType.DMA((n,))],
    )(*shards)


def _pair_exchange(grads):
    n = len(grads)

    def body(*refs):
        ins, outs = refs[:n], refs[n:2 * n]
        send_sems, recv_sems = refs[2 * n:]
        x, y, c, _ = _mesh_pos()
        cps = []
        for a in range(n):
            cp = pltpu.make_async_remote_copy(
                src_ref=ins[a].at[1 - c], dst_ref=outs[a], send_sem=send_sems.at[a], recv_sem=recv_sems.at[a],
                device_id=(x, y, 1 - c), device_id_type=MESH_T)
            cp.start()
            cps.append(cp)
        for cp in cps:
            cp.wait()

    return pl.pallas_call(
        body, name="rs_pair_exchange", in_specs=[ANY] * n, out_specs=[ANY] * n,
        out_shape=[jax.ShapeDtypeStruct(g.shape[1:], F32) for g in grads],
        scratch_shapes=[pltpu.SemaphoreType.DMA((n,)), pltpu.SemaphoreType.DMA((n,))],
    )(*grads)


def _pair_sum(g, recv, pos, name):
    _, _, rh, c = g.shape
    tr = _pick_rows(rh, c)

    def body(pos_ref, g_ref, r_ref, o_ref):
        o_ref[...] = (g_ref[...] + r_ref[...]).astype(BF16)

    return pl.pallas_call(
        body, name=name,
        grid_spec=pltpu.PrefetchScalarGridSpec(
            num_scalar_prefetch=1, grid=(NSH, rh // tr),
            in_specs=[pl.BlockSpec((None, None, tr, c), lambda j, r, p: (p[0], j, r, 0)),
                      pl.BlockSpec((None, tr, c), lambda j, r, p: (j, r, 0))],
            out_specs=pl.BlockSpec((None, tr, c), lambda j, r, p: (j, r, 0))),
        out_shape=jax.ShapeDtypeStruct((NSH, rh, c), BF16),
        compiler_params=_cp(("parallel", "parallel")),
    )(pos, g, recv)


def _chip_exchange(psums):
    n = len(psums)

    def body(*refs):
        ins, outs = refs[:n], refs[n:2 * n]
        send_sems, recv_sems = refs[2 * n:]
        x, y, c, chips = _mesh_pos()
        cps = []
        for a in range(n):
            for k, chip in enumerate(chips):
                j = 2 * chip[0] + chip[1]
                cp = pltpu.make_async_remote_copy(
                    src_ref=ins[a].at[j], dst_ref=outs[a].at[k],
                    send_sem=send_sems.at[a * 3 + k], recv_sem=recv_sems.at[a * 3 + k],
                    device_id=(chip[0], chip[1], c), device_id_type=MESH_T)
                cp.start()
                cps.append(cp)
        for cp in cps:
            cp.wait()

    return pl.pallas_call(
        body, name="rs_chip_exchange", in_specs=[ANY] * n, out_specs=[ANY] * n,
        out_shape=[jax.ShapeDtypeStruct((3,) + p.shape[1:], BF16) for p in psums],
        scratch_shapes=[pltpu.SemaphoreType.DMA((3 * n,)), pltpu.SemaphoreType.DMA((3 * n,))],
    )(*psums)


def _owner_sum(g, recv_a, recv_b, pos, name):
    _, _, rh, c = g.shape
    tr = _pick_rows(rh, c)

    def body(pos_ref, g_ref, ra_ref, rb_ref, o_ref):
        acc = g_ref[...] + ra_ref[...]
        for k in range(3):
            acc = acc + rb_ref[k].astype(F32)
        o_ref[...] = acc

    return pl.pallas_call(
        body, name=name,
        grid_spec=pltpu.PrefetchScalarGridSpec(
            num_scalar_prefetch=1, grid=(rh // tr,),
            in_specs=[pl.BlockSpec((None, None, tr, c), lambda r, p: (p[0], p[1], r, 0)),
                      pl.BlockSpec((None, tr, c), lambda r, p: (p[1], r, 0)),
                      pl.BlockSpec((3, tr, c), lambda r, p: (0, r, 0))],
            out_specs=pl.BlockSpec((tr, c), lambda r, p: (r, 0))),
        out_shape=jax.ShapeDtypeStruct((rh, c), F32),
        compiler_params=_cp(("parallel",)),
    )(pos, g, recv_a, recv_b)


def _sibling_allgather(halves):
    n = len(halves)

    def body(*refs):
        ins, outs = refs[:n], refs[n:2 * n]
        send_sems, recv_sems, loc_sems = refs[2 * n:]
        x, y, c, _ = _mesh_pos()
        cps, lcs = [], []
        for a in range(n):
            lc = pltpu.make_async_copy(ins[a], outs[a].at[c], loc_sems.at[a])
            lc.start()
            lcs.append(lc)
            cp = pltpu.make_async_remote_copy(
                src_ref=ins[a], dst_ref=outs[a].at[c], send_sem=send_sems.at[a], recv_sem=recv_sems.at[a],
                device_id=(x, y, 1 - c), device_id_type=MESH_T)
            cp.start()
            cps.append(cp)
        for a in range(n):
            cps[a].wait_send()
            pltpu.make_async_remote_copy(
                src_ref=ins[a], dst_ref=outs[a].at[1 - c], send_sem=send_sems.at[a], recv_sem=recv_sems.at[a],
                device_id=(x, y, 1 - c), device_id_type=MESH_T).wait_recv()
            lcs[a].wait()

    return pl.pallas_call(
        body, name="rs_sibling_allgather", in_specs=[ANY] * n, out_specs=[ANY] * n,
        out_shape=[jax.ShapeDtypeStruct((2,) + h.shape, F32) for h in halves],
        scratch_shapes=[pltpu.SemaphoreType.DMA((n,)), pltpu.SemaphoreType.DMA((n,)), pltpu.SemaphoreType.DMA((n,))],
    )(*halves)


def _small_allgather(part):
    m_per = SMALL_ROWS

    def body(x_ref, out_ref, send_sems, recv_sems, local_sem):
        x, y, c, chips = _mesh_pos()
        me, sibling = (x, y, c), (x, y, 1 - c)

        def rows(px, py, pc):
            return out_ref.at[pl.ds((4 * px + 2 * py + pc) * m_per, m_per), :]

        def copy(k, block, to, src=None):
            return pltpu.make_async_remote_copy(
                src_ref=rows(*block) if src is None else src, dst_ref=rows(*block),
                send_sem=send_sems.at[k], recv_sem=recv_sems.at[k], device_id=to, device_id_type=MESH_T)

        mine = pltpu.make_async_copy(x_ref, rows(*me), local_sem)
        mine.start()
        first = [copy(0, me, sibling, src=x_ref)]
        first += [copy(1 + j, me, (*chip, c), src=x_ref) for j, chip in enumerate(chips)]
        for cp in first:
            cp.start()
        passed = [copy(4 + j, (*chip, c), sibling) for j, chip in enumerate(chips)]
        for j, chip in enumerate(chips):
            copy(1 + j, (*chip, c), me).wait_recv()
            passed[j].start()
        copy(0, sibling, me).wait_recv()
        for j, chip in enumerate(chips):
            copy(4 + j, (*chip, 1 - c), me).wait_recv()
        for cp in first + passed:
            cp.wait_send()
        mine.wait()

    return pl.pallas_call(
        body, name="small_allgather",
        out_shape=jax.ShapeDtypeStruct((8 * m_per, D), F32),
        in_specs=[pl.BlockSpec(memory_space=pltpu.VMEM)], out_specs=pl.BlockSpec(memory_space=pltpu.VMEM),
        scratch_shapes=[pltpu.SemaphoreType.DMA((7,)), pltpu.SemaphoreType.DMA((7,)), pltpu.SemaphoreType.DMA],
    )(part)


def _to_classes(t, d):
    if d == 1:
        return t
    return t.reshape(S // d, d, t.shape[-1]).transpose(1, 0, 2).reshape(S, t.shape[-1])


def _from_classes(t, d):
    if d == 1:
        return t
    return t.reshape(d, S // d, t.shape[-1]).transpose(1, 0, 2).reshape(S, t.shape[-1])


def _group_stack(t, col0):
    return jnp.stack([_to_classes(t[:, col0 + gi * GRP_W: col0 + (gi + 1) * GRP_W], d)
                      for gi, d in enumerate(DILATIONS)])


def _group_perm(t):
    return jnp.stack([_to_classes(t, d) for d in DILATIONS])


def _group_unstack(t3):
    return jnp.concatenate([_from_classes(t3[gi], d) for gi, d in enumerate(DILATIONS)], axis=1)


def _pack_small(ln1_g, ln1_b, gln_g, gln_b, ln2_g, ln2_b, ln3_g, ln3_b, b_gates, b_s, w_s):
    rows = [ln1_g, ln1_b, gln_g, gln_b, ln2_g, ln2_b, ln3_g, ln3_b]
    rows = [r.reshape(1, D) for r in rows] + [b_gates.reshape(2, D), b_s.reshape(1, D), jnp.zeros((5, D), F32),
                                             w_s.reshape(128, D)]
    return jnp.concatenate(rows, axis=0)


def _unpack_small(p):
    out = [p[i:i + 1] for i in range(8)]
    return out + [p[8:10].reshape(1, 2 * D), p[10:11].reshape(1, 8, BLK), p[16:144].reshape(1, 8, BLK, BLK)]


def _local_step(x, pos_f, target, W, P):
    invf = ROPE_THETA ** (-jnp.arange(0, DH, 2, dtype=F32) / DH)
    invf = jnp.tile(invf, 4).reshape(1, 128)
    b_s_t = P["gmlp_b_s"].T

    h1, h1b, xh1, rstd1, a1, b1 = _ffn_fwd(x, W["f1g"], W["f1u"], W["f1d"], P["ln1_g"], P["ln1_b"], "ffn1_fwd")
    qkv = _matmul(h1b, W["w_in"], "nn", "proj_qkv", n=3 * ATT_W, b_col0=0)
    z = _matmul(h1b, W["w_in"], "nn", "proj_z", n=2 * GW, b_col0=3 * ATT_W)
    gl = _matmul(h1b, W["w_in"], "nn", "proj_gates", n=2 * D, b_col0=3 * ATT_W + 2 * GW)
    qkvb = _rope(qkv, pos_f, invf, 1.0, "rope_fwd")
    qp, kp, vp = _group_stack(qkvb, 0), _group_stack(qkvb, ATT_W), _group_stack(qkvb, 2 * ATT_W)
    o3, lse3 = _attn_fwd(qp, kp, vp, "attn_fwd")
    o3t = jnp.stack([_from_classes(o3[gi], d) for gi, d in enumerate(DILATIONS)])
    lse3t = jnp.stack([_from_classes(lse3[gi], d) for gi, d in enumerate(DILATIONS)])
    y_attn, lse = _attn_combine(o3t, lse3t, "attn_combine")
    y_gmlp = _gmlp_fwd(z, P["gmlp_ln_g"], P["gmlp_ln_b"], P["gmlp_w_s"], b_s_t, "gmlp_fwd")
    br_a = _matmul(y_attn, W["w_ab"], "nn", "branch_attn", n=D)
    br_b = _matmul(y_gmlp, W["w_gb"], "nn", "branch_gmlp", n=D)
    merged = _merge_fwd(br_a, br_b, gl, P["b_gates"], "merge_fwd")
    mix = _matmul(merged, W["w_out"], "nn", "mix_out", n=D)
    h2, h2b, xh2, rstd2 = _resid_ln(h1, mix, P["ln2_g"], P["ln2_b"], "resid_ln2")
    y, _, xh3, rstd3, a2, b2 = _ffn_fwd(h2, W["f2g"], W["f2u"], W["f2d"], P["ln3_g"], P["ln3_b"], "ffn2_fwd")

    dr3, dg3, db3, loss = _ln_bwd(y, xh3, rstd3, P["ln3_g"], "loss_ln3_bwd", target=target)
    da2, dbb2, s2, dh2 = _ffn_bwd_act(dr3, a2, b2, W["f2g"], W["f2u"], W["f2d"], "ffn2_bwd")
    g_f2g = _wgrad_cols(h2b, da2, D // 2, FSH, "dw_ffn2_gate", y_stacked=True)
    g_f2u = _wgrad_cols(h2b, dbb2, D // 2, FSH, "dw_ffn2_up", y_stacked=True)
    g_f2d = _wgrad_rows(s2, dr3, FSH, "dw_ffn2_down", x_stacked=True, y_scale=0.5)
    dr2, dg2, db2 = _ln_bwd(dh2, xh2, rstd2, P["ln2_g"], "ln2_bwd")
    g_wout = _wgrad_rows(merged, dr2, 256, "dw_out")
    dmerged = _matmul(dr2, W["w_out"], "nt", "dmerged", n=D)
    dab, dbb, dglb, dbg = _merge_bwd(dmerged, br_a, br_b, gl, P["b_gates"], "merge_bwd")
    g_wab = _wgrad_cols(y_attn, dab, GRP_W // 2, 256, "dw_attn_branch")
    g_wgb = _wgrad_rows(y_gmlp, dbb, 256, "dw_gmlp_branch")
    dy_attn = _matmul(dab, W["w_ab"], "nt", "dy_attn", n=GRP_W)
    dy_gmlp = _matmul(dbb, W["w_gb"], "nt", "dy_gmlp", n=GW)
    dzb, dws, dbs_t, dgln_g, dgln_b = _gmlp_bwd(z, dy_gmlp, P["gmlp_ln_g"], P["gmlp_ln_b"], P["gmlp_w_s"], b_s_t,
                                                 "gmlp_bwd")
    dyp, yp, lp = _group_perm(dy_attn), _group_perm(y_attn), _group_perm(lse)
    dq3, dk3, dv3 = _attn_bwd(qp, kp, vp, dyp, yp, lp, "attn_bwd")
    dqkv_rot = jnp.concatenate([_group_unstack(dq3), _group_unstack(dk3), _group_unstack(dv3)], axis=1)
    dqkvb = _rope(dqkv_rot, pos_f, invf, -1.0, "rope_bwd")
    dproj = jnp.concatenate([dqkvb, dzb, dglb], axis=1)
    g_win = _wgrad_cols(h1b, dproj, D // 2, IN_SH, "dw_in")
    dh1 = _matmul(dproj, W["w_in"], "nt", "dh1", n=D, tn=D, tk=IN_SH,
                  add=dr2, add_scale=ALPHA)
    dr1, dg1, db1 = _ln_bwd(dh1, xh1, rstd1, P["ln1_g"], "ln1_bwd")
    da1, dbb1, s1, dx = _ffn_bwd_act(dr1, a1, b1, W["f1g"], W["f1u"], W["f1d"], "ffn1_bwd")
    g_f1g = _wgrad_cols(x, da1, D // 2, FSH, "dw_ffn1_gate", y_stacked=True)
    g_f1u = _wgrad_cols(x, dbb1, D // 2, FSH, "dw_ffn1_up", y_stacked=True)
    g_f1d = _wgrad_rows(s1, dr1, FSH, "dw_ffn1_down", x_stacked=True, y_scale=0.5)

    big = dict(f1g=g_f1g, f1u=g_f1u, f1d=g_f1d, w_in=g_win, w_ab=g_wab, w_gb=g_wgb, w_out=g_wout,
               f2g=g_f2g, f2u=g_f2u, f2d=g_f2d)
    small = _pack_small(dg1, db1, dgln_g, dgln_b, dg2, db2, dg3, db3, dbg, dbs_t.T, dws)
    return loss, dx, big, small


BIG = ("f1g", "f1u", "f1d", "w_in", "w_ab", "w_gb", "w_out", "f2g", "f2u", "f2d")
KIND = dict(f1g="stack", f1u="stack", f1d="stack", w_in="col", w_ab="col", w_gb="stack", w_out="stack",
            f2g="stack", f2u="stack", f2d="stack")


def kernel(x, positions, ffn1_w_gate, ffn1_w_up, ffn1_w_down, ln1_g, ln1_b, w_in, b_gates, gmlp_ln_g, gmlp_ln_b, gmlp_w_s, gmlp_b_s, w_attn_branch, w_gmlp_branch, w_out, ln2_g, ln2_b, ffn2_w_gate, ffn2_w_up, ffn2_w_down, ln3_g, ln3_b, loss_target, m_ffn1_w_gate, m_ffn1_w_up, m_ffn1_w_down, m_ln1_g, m_ln1_b, m_w_in, m_b_gates, m_gmlp_ln_g, m_gmlp_ln_b, m_gmlp_w_s, m_gmlp_b_s, m_w_attn_branch, m_w_gmlp_branch, m_w_out, m_ln2_g, m_ln2_b, m_ffn2_w_gate, m_ffn2_w_up, m_ffn2_w_down, m_ln3_g, m_ln3_b, v_ffn1_w_gate, v_ffn1_w_up, v_ffn1_w_down, v_ln1_g, v_ln1_b, v_w_in, v_b_gates, v_gmlp_ln_g, v_gmlp_ln_b, v_gmlp_w_s, v_gmlp_b_s, v_w_attn_branch, v_w_gmlp_branch, v_w_out, v_ln2_g, v_ln2_b, v_ffn2_w_gate, v_ffn2_w_up, v_ffn2_w_down, v_ln3_g, v_ln3_b):
    cx, cy, cc = lax.axis_index("x"), lax.axis_index("y"), lax.axis_index("c")
    pos = jnp.stack([cc, 2 * cx + cy]).astype(jnp.int32)

    w_sh = dict(f1g=ffn1_w_gate, f1u=ffn1_w_up, f1d=ffn1_w_down, w_in=w_in, w_ab=w_attn_branch,
                w_gb=w_gmlp_branch, w_out=w_out, f2g=ffn2_w_gate, f2u=ffn2_w_up, f2d=ffn2_w_down)
    m_sh = dict(f1g=m_ffn1_w_gate, f1u=m_ffn1_w_up, f1d=m_ffn1_w_down, w_in=m_w_in, w_ab=m_w_attn_branch,
                w_gb=m_w_gmlp_branch, w_out=m_w_out, f2g=m_ffn2_w_gate, f2u=m_ffn2_w_up, f2d=m_ffn2_w_down)
    v_sh = dict(f1g=v_ffn1_w_gate, f1u=v_ffn1_w_up, f1d=v_ffn1_w_down, w_in=v_w_in, w_ab=v_w_attn_branch,
                w_gb=v_w_gmlp_branch, w_out=v_w_out, f2g=v_ffn2_w_gate, f2u=v_ffn2_w_up, f2d=v_ffn2_w_down)
    w_sh = {k: v[0] for k, v in w_sh.items()}
    m_sh = {k: v[0] for k, v in m_sh.items()}
    v_sh = {k: v[0] for k, v in v_sh.items()}

    full = _gather_weights([w_sh[k].astype(BF16) for k in BIG], [KIND[k] for k in BIG])
    W = dict(zip(BIG, full))
    W["w_gb"] = W["w_gb"].reshape(D, D)
    W["w_out"] = W["w_out"].reshape(D, D)
    P = dict(ln1_g=ln1_g, ln1_b=ln1_b, ln2_g=ln2_g, ln2_b=ln2_b, ln3_g=ln3_g, ln3_b=ln3_b, b_gates=b_gates,
             gmlp_ln_g=gmlp_ln_g, gmlp_ln_b=gmlp_ln_b, gmlp_w_s=gmlp_w_s[0], gmlp_b_s=gmlp_b_s[0])

    pos_f = positions.reshape(S, 1).astype(F32)
    loss_part, dx, big, small = _local_step(x[0], pos_f, loss_target[0], W, P)
    loss = lax.psum(loss_part[0, 0], ("x", "y", "c"))

    grads = [big[k] for k in BIG]
    recv_a = _pair_exchange(grads)
    psums = [_pair_sum(g, r, pos, "rs_pair_sum_" + k) for g, r, k in zip(grads, recv_a, BIG)]
    recv_b = _chip_exchange(psums)
    halves = [_owner_sum(g, ra, rb, pos, "rs_owner_sum_" + k) for g, ra, rb, k in zip(grads, recv_a, recv_b, BIG)]
    reduced = _sibling_allgather(halves)

    g_out, d_out, m_out, v_out = {}, {}, {}, {}
    for k, gfull in zip(BIG, reduced):
        shp = w_sh[k].shape
        g2, dl, mn, vn = _adamw(w_sh[k], gfull.reshape(shp), m_sh[k], v_sh[k], "adamw_" + k)
        g_out[k], d_out[k], m_out[k], v_out[k] = g2[None], dl[None], mn[None], vn[None]

    parts = _small_allgather(small).reshape(8, SMALL_ROWS, D)
    sp = (ln1_g, ln1_b, gmlp_ln_g, gmlp_ln_b, ln2_g, ln2_b, ln3_g, ln3_b, b_gates, gmlp_b_s, gmlp_w_s)
    sm = (m_ln1_g, m_ln1_b, m_gmlp_ln_g, m_gmlp_ln_b, m_ln2_g, m_ln2_b, m_ln3_g, m_ln3_b, m_b_gates, m_gmlp_b_s,
          m_gmlp_w_s)
    sv = (v_ln1_g, v_ln1_b, v_gmlp_ln_g, v_gmlp_ln_b, v_ln2_g, v_ln2_b, v_ln3_g, v_ln3_b, v_b_gates, v_gmlp_b_s,
          v_gmlp_w_s)
    sg, sd, smn, svn = _small_sum_adamw(parts, _pack_small(*sp), _pack_small(*sm), _pack_small(*sv), "small_adamw")
    names = ("ln1_g", "ln1_b", "gmlp_ln_g", "gmlp_ln_b", "ln2_g", "ln2_b", "ln3_g", "ln3_b", "b_gates", "gmlp_b_s",
             "gmlp_w_s")
    for dst, packed in ((g_out, sg), (d_out, sd), (m_out, smn), (v_out, svn)):
        for nm, val in zip(names, _unpack_small(packed)):
            dst[nm] = val

    order = ("f1g", "f1u", "f1d", "ln1_g", "ln1_b", "w_in", "b_gates", "gmlp_ln_g", "gmlp_ln_b", "gmlp_w_s", "gmlp_b_s",
             "w_ab", "w_gb", "w_out", "ln2_g", "ln2_b", "f2g", "f2u", "f2d", "ln3_g", "ln3_b")
    outs = [loss, dx[None]]
    for dst in (g_out, d_out, m_out, v_out):
        outs += [dst[k] for k in order]
    return tuple(outs)
```

```python
import functools
import math

import jax
import jax.numpy as jnp
from jax import lax
from jax.experimental import pallas as pl
from jax.experimental.pallas import tpu as pltpu

F32 = jnp.float32
BF16 = jnp.bfloat16

S = 2048
D = 1024
NSH = 4
FSH = 704
ATT_W = 1536
GRP_W = 512
NG = 3
NH = 8
DH = 64
BLK = 128
NBLK = S // BLK
GW = 1024
IN_W = 8704
IN_SH = IN_W // NSH
ALPHA = 2.0 ** 0.25
LN_EPS = 1e-5
ROPE_THETA = 10000.0
DILATIONS = (1, 4, 16)
ADAM_LR, ADAM_B1, ADAM_B2, ADAM_EPS, ADAM_WD, ADAM_STEP = 0.001, 0.9, 0.999, 1e-08, 0.01, 10
SMALL_ROWS = 144
MESH_T = pl.DeviceIdType.MESH
MIB = 1024 * 1024
NEG_INF = float("-inf")


def _cp(sem, vmem_mib=48):
    return pltpu.CompilerParams(dimension_semantics=sem, vmem_limit_bytes=vmem_mib * MIB)


def _ln_stats(r):
    mu = jnp.mean(r, axis=-1, keepdims=True)
    xc = r - mu
    var = jnp.mean(xc * xc, axis=-1, keepdims=True)
    rstd = lax.rsqrt(var + LN_EPS)
    return xc * rstd, rstd


def _ln_dx(dxh, xh, rstd):
    m1 = jnp.mean(dxh, axis=-1, keepdims=True)
    m2 = jnp.mean(dxh * xh, axis=-1, keepdims=True)
    return rstd * (dxh - m1 - xh * m2)


def _dot_nt(a, b):
    return lax.dot_general(a, b, (((1,), (1,)), ((), ())), preferred_element_type=F32)


def _dot_tn(a, b):
    return lax.dot_general(a, b, (((0,), (0,)), ((), ())), preferred_element_type=F32)


def _dot(a, b):
    return jnp.dot(a, b, preferred_element_type=F32)


def _ffn_fwd(xin, wg, wu, wd, ln_g, ln_b, name):
    tm = 512

    def body(x_ref, wg_ref, wu_ref, wd_ref, g_ref, b_ref, h_ref, hb_ref, xh_ref, rstd_ref, a_ref, bb_ref, acc_ref):
        j = pl.program_id(1)
        xb = x_ref[...].astype(BF16)
        a = _dot(xb, wg_ref[...])
        b = _dot(xb, wu_ref[...])
        a_ref[...] = a
        bb_ref[...] = b
        s = (a * jax.nn.sigmoid(a)) * b
        f = _dot(s.astype(BF16), wd_ref[...])

        @pl.when(j == 0)
        def _():
            acc_ref[...] = f

        @pl.when(j > 0)
        def _():
            acc_ref[...] += f

        @pl.when(j == NSH - 1)
        def _():
            r = ALPHA * x_ref[...] + 0.5 * acc_ref[...]
            xh, rstd = _ln_stats(r)
            h = xh * g_ref[...] + b_ref[...]
            h_ref[...] = h
            hb_ref[...] = h.astype(BF16)
            xh_ref[...] = xh
            rstd_ref[...] = rstd

    row = pl.BlockSpec((tm, D), lambda i, j: (i, 0))
    vec = pl.BlockSpec((1, D), lambda i, j: (0, 0))
    wcol = pl.BlockSpec((None, D, FSH), lambda i, j: (j, 0, 0))
    ab = pl.BlockSpec((None, tm, FSH), lambda i, j: (j, i, 0))
    return pl.pallas_call(
        body, name=name, grid=(S // tm, NSH),
        in_specs=[row, wcol, wcol, pl.BlockSpec((None, FSH, D), lambda i, j: (j, 0, 0)), vec, vec],
        out_specs=[row, row, row, pl.BlockSpec((tm, 1), lambda i, j: (i, 0)), ab, ab],
        out_shape=[jax.ShapeDtypeStruct((S, D), F32), jax.ShapeDtypeStruct((S, D), BF16),
                   jax.ShapeDtypeStruct((S, D), F32), jax.ShapeDtypeStruct((S, 1), F32),
                   jax.ShapeDtypeStruct((NSH, S, FSH), F32), jax.ShapeDtypeStruct((NSH, S, FSH), F32)],
        scratch_shapes=[pltpu.VMEM((tm, D), F32)],
        compiler_params=_cp(("parallel", "arbitrary")),
    )(xin, wg, wu, wd, ln_g, ln_b)


def _ffn_bwd_act(dr, a, b, wg, wu, wd, name):
    tm = 512

    def body(dr_ref, a_ref, b_ref, wg_ref, wu_ref, wd_ref, da_ref, db_ref, s_ref, dx_ref, acc_ref):
        j = pl.program_id(1)
        df = (0.5 * dr_ref[...]).astype(BF16)
        ds = _dot_nt(df, wd_ref[...])
        av = a_ref[...]
        bv = b_ref[...]
        sig = jax.nn.sigmoid(av)
        sl = av * sig
        da = (ds * bv * (sig * (1.0 + av * (1.0 - sig)))).astype(BF16)
        db = (ds * sl).astype(BF16)
        da_ref[...] = da
        db_ref[...] = db
        s_ref[...] = (sl * bv).astype(BF16)
        dx = _dot_nt(da, wg_ref[...]) + _dot_nt(db, wu_ref[...])

        @pl.when(j == 0)
        def _():
            acc_ref[...] = ALPHA * dr_ref[...] + dx

        @pl.when(j > 0)
        def _():
            acc_ref[...] += dx

        @pl.when(j == NSH - 1)
        def _():
            dx_ref[...] = acc_ref[...]

    row = pl.BlockSpec((tm, D), lambda i, j: (i, 0))
    wcol = pl.BlockSpec((None, D, FSH), lambda i, j: (j, 0, 0))
    ab = pl.BlockSpec((None, tm, FSH), lambda i, j: (j, i, 0))
    sh = jax.ShapeDtypeStruct((NSH, S, FSH), BF16)
    return pl.pallas_call(
        body, name=name, grid=(S // tm, NSH),
        in_specs=[row, ab, ab, wcol, wcol, pl.BlockSpec((None, FSH, D), lambda i, j: (j, 0, 0))],
        out_specs=[ab, ab, ab, row],
        out_shape=[sh, sh, sh, jax.ShapeDtypeStruct((S, D), F32)],
        scratch_shapes=[pltpu.VMEM((tm, D), F32)],
        compiler_params=_cp(("parallel", "arbitrary")),
    )(dr, a, b, wg, wu, wd)


def _matmul(a, b, mode, name, *, n, tm=512, tn=512, tk=None, b_col0=0, add=None, add_scale=1.0, out_dtype=F32):
    m, ka = a.shape
    tk = ka if tk is None else tk
    nk = ka // tk
    assert m % tm == 0 and n % tn == 0 and ka % tk == 0 and b_col0 % tn == 0
    off = b_col0 // tn

    def body(*refs):
        if add is None:
            a_ref, b_ref, o_ref = refs[:3]
            add_ref = None
            rest = refs[3:]
        else:
            a_ref, b_ref, add_ref, o_ref = refs[:4]
            rest = refs[4:]
        k = pl.program_id(2)
        av = a_ref[...].astype(BF16)
        bv = b_ref[...].astype(BF16)
        p = _dot(av, bv) if mode == "nn" else _dot_nt(av, bv)

        def finish(acc):
            if add_ref is not None:
                acc = acc + add_scale * add_ref[...]
            o_ref[...] = acc.astype(out_dtype)

        if nk == 1:
            finish(p)
        else:
            acc_ref = rest[0]

            @pl.when(k == 0)
            def _():
                acc_ref[...] = p

            @pl.when(k > 0)
            def _():
                acc_ref[...] += p

            @pl.when(k == nk - 1)
            def _():
                finish(acc_ref[...])

    a_spec = pl.BlockSpec((tm, tk), lambda i, j, k: (i, k))
    if mode == "nn":
        b_spec = pl.BlockSpec((tk, tn), lambda i, j, k: (k, j + off))
    else:
        b_spec = pl.BlockSpec((tn, tk), lambda i, j, k: (j, k))
    o_spec = pl.BlockSpec((tm, tn), lambda i, j, k: (i, j))
    in_specs = [a_spec, b_spec] + ([o_spec] if add is not None else [])
    args = (a, b) + ((add,) if add is not None else ())
    return pl.pallas_call(
        body, name=name, grid=(m // tm, n // tn, nk),
        in_specs=in_specs, out_specs=o_spec,
        out_shape=jax.ShapeDtypeStruct((m, n), out_dtype),
        scratch_shapes=[pltpu.VMEM((tm, tn), F32)] if nk > 1 else [],
        compiler_params=_cp(("parallel", "parallel", "arbitrary")),
    )(*args)


def _wgrad_cols(xb, y, rh, c, name, y_stacked=False):
    ts = 512

    def body(x_ref, y_ref, o_ref, acc_ref):
        k = pl.program_id(2)
        p = _dot_tn(x_ref[...].astype(BF16), y_ref[...].astype(BF16))

        @pl.when(k == 0)
        def _():
            acc_ref[...] = p

        @pl.when(k > 0)
        def _():
            acc_ref[...] += p

        @pl.when(k == S // ts - 1)
        def _():
            o_ref[...] = acc_ref[...]

    if y_stacked:
        y_spec = pl.BlockSpec((None, ts, c), lambda h, j, k: (j, k, 0))
    else:
        y_spec = pl.BlockSpec((ts, c), lambda h, j, k: (k, j))
    return pl.pallas_call(
        body, name=name, grid=(2, NSH, S // ts),
        in_specs=[pl.BlockSpec((ts, rh), lambda h, j, k: (k, h)), y_spec],
        out_specs=pl.BlockSpec((None, None, rh, c), lambda h, j, k: (h, j, 0, 0)),
        out_shape=jax.ShapeDtypeStruct((2, NSH, rh, c), F32),
        scratch_shapes=[pltpu.VMEM((rh, c), F32)],
        compiler_params=_cp(("parallel", "parallel", "arbitrary")),
    )(xb, y)


def _wgrad_rows(xb, y, r, name, x_stacked=False, y_scale=1.0):
    ts = 512
    rh = r // 2
    c = y.shape[-1]

    def body(x_ref, y_ref, o_ref, acc_ref):
        k = pl.program_id(1)
        p = _dot_tn(x_ref[...].astype(BF16), (y_ref[...] * y_scale).astype(BF16))

        @pl.when(k == 0)
        def _():
            acc_ref[...] = p

        @pl.when(k > 0)
        def _():
            acc_ref[...] += p

        @pl.when(k == S // ts - 1)
        def _():
            o_ref[0] = acc_ref[0:rh, :]
            o_ref[1] = acc_ref[rh:r, :]

    if x_stacked:
        x_spec = pl.BlockSpec((None, ts, r), lambda j, k: (j, k, 0))
    else:
        x_spec = pl.BlockSpec((ts, r), lambda j, k: (k, j))
    return pl.pallas_call(
        body, name=name, grid=(NSH, S // ts),
        in_specs=[x_spec, pl.BlockSpec((ts, c), lambda j, k: (k, 0))],
        out_specs=pl.BlockSpec((2, None, rh, c), lambda j, k: (0, j, 0, 0)),
        out_shape=jax.ShapeDtypeStruct((2, NSH, rh, c), F32),
        scratch_shapes=[pltpu.VMEM((r, c), F32)],
        compiler_params=_cp(("parallel", "arbitrary")),
    )(xb, y)


def _resid_ln(res, f, ln_g, ln_b, name):
    tm = 256

    def body(res_ref, f_ref, g_ref, b_ref, h_ref, hb_ref, xh_ref, rstd_ref):
        r = ALPHA * res_ref[...] + f_ref[...]
        xh, rstd = _ln_stats(r)
        h = xh * g_ref[...] + b_ref[...]
        h_ref[...] = h
        hb_ref[...] = h.astype(BF16)
        xh_ref[...] = xh
        rstd_ref[...] = rstd

    row = pl.BlockSpec((tm, D), lambda i: (i, 0))
    vec = pl.BlockSpec((1, D), lambda i: (0, 0))
    return pl.pallas_call(
        body, name=name, grid=(S // tm,),
        in_specs=[row, row, vec, vec],
        out_specs=[row, row, row, pl.BlockSpec((tm, 1), lambda i: (i, 0))],
        out_shape=[jax.ShapeDtypeStruct((S, D), F32), jax.ShapeDtypeStruct((S, D), BF16),
                   jax.ShapeDtypeStruct((S, D), F32), jax.ShapeDtypeStruct((S, 1), F32)],
        compiler_params=_cp(("parallel",)),
    )(res, f, ln_g, ln_b)


def _ln_bwd(dout, xh, rstd, ln_g, name, target=None):
    tm = 256
    with_loss = target is not None

    def body(*refs):
        if with_loss:
            y_ref, t_ref, xh_ref, rstd_ref, g_ref, dr_ref, dg_ref, db_ref, loss_ref = refs
            err = y_ref[...] - t_ref[...]
            dy = err * (1.0 / D)
        else:
            y_ref, xh_ref, rstd_ref, g_ref, dr_ref, dg_ref, db_ref = refs
            dy = y_ref[...]
        i = pl.program_id(0)
        xh = xh_ref[...]
        dr_ref[...] = _ln_dx(dy * g_ref[...], xh, rstd_ref[...])
        dg = jnp.sum(dy * xh, axis=0, keepdims=True)
        db = jnp.sum(dy, axis=0, keepdims=True)

        @pl.when(i == 0)
        def _():
            dg_ref[...] = dg
            db_ref[...] = db

        @pl.when(i > 0)
        def _():
            dg_ref[...] += dg
            db_ref[...] += db

        if with_loss:
            part = 0.5 * jnp.sum(jnp.mean(err * err, axis=-1, keepdims=True), axis=0, keepdims=True)
            part = jnp.broadcast_to(part, (8, 128))

            @pl.when(i == 0)
            def _():
                loss_ref[...] = part

            @pl.when(i > 0)
            def _():
                loss_ref[...] += part

    row = pl.BlockSpec((tm, D), lambda i: (i, 0))
    vec = pl.BlockSpec((1, D), lambda i: (0, 0))
    col = pl.BlockSpec((tm, 1), lambda i: (i, 0))
    in_specs = [row] + ([row] if with_loss else []) + [row, col, vec]
    out_specs = [row, vec, vec] + ([pl.BlockSpec((8, 128), lambda i: (0, 0))] if with_loss else [])
    out_shape = [jax.ShapeDtypeStruct((S, D), F32), jax.ShapeDtypeStruct((1, D), F32),
                 jax.ShapeDtypeStruct((1, D), F32)] + ([jax.ShapeDtypeStruct((8, 128), F32)] if with_loss else [])
    args = (dout,) + ((target,) if with_loss else ()) + (xh, rstd, ln_g)
    return pl.pallas_call(
        body, name=name, grid=(S // tm,), in_specs=in_specs, out_specs=out_specs, out_shape=out_shape,
        compiler_params=_cp(("arbitrary",)),
    )(*args)


def _rope(t, pos_f, invf, sign, name):
    tm = 256
    nch = ATT_W // 128

    def body(t_ref, pos_ref, invf_ref, o_ref):
        sec = pl.program_id(1)
        ang = pos_ref[...] * invf_ref[...]
        lane = lax.broadcasted_iota(jnp.int32, (tm, 128), 1)
        first = (lane % DH) < (DH // 2)
        rot = sec < 2
        cosf = jnp.where(rot, jnp.cos(ang), 1.0)
        sinv = jnp.where(rot, jnp.sin(ang), 0.0) * sign
        sinf = jnp.where(first, -sinv, sinv)
        for ch in range(nch):
            x = t_ref[:, ch * 128:(ch + 1) * 128]
            sw = jnp.where(first, pltpu.roll(x, 96, 1), pltpu.roll(x, 32, 1))
            o_ref[:, ch * 128:(ch + 1) * 128] = (x * cosf + sw * sinf).astype(BF16)

    return pl.pallas_call(
        body, name=name, grid=(S // tm, 3),
        in_specs=[pl.BlockSpec((tm, ATT_W), lambda i, s: (i, s)), pl.BlockSpec((tm, 1), lambda i, s: (i, 0)),
                  pl.BlockSpec((1, 128), lambda i, s: (0, 0))],
        out_specs=pl.BlockSpec((tm, ATT_W), lambda i, s: (i, s)),
        out_shape=jax.ShapeDtypeStruct((S, 3 * ATT_W), BF16),
        compiler_params=_cp(("parallel", "parallel")),
    )(t, pos_f, invf)


def _blocks_per_class(g):
    return jnp.where(g == 0, NBLK // DILATIONS[0], jnp.where(g == 1, NBLK // DILATIONS[1], NBLK // DILATIONS[2]))


def _attn_fwd(qp, kp, vp, name):
    def body(q_ref, kc_ref, kp_ref, vc_ref, vp_ref, o_ref, lse_ref):
        g = pl.program_id(0)
        b = pl.program_id(1)
        has_prev = lax.rem(b, _blocks_per_class(g)) != 0
        qi = lax.broadcasted_iota(jnp.int32, (BLK, BLK), 0)
        kj = lax.broadcasted_iota(jnp.int32, (BLK, BLK), 1)
        mask_c = kj <= qi
        mask_p = jnp.logical_and(kj >= qi, has_prev)
        for h in range(NH):
            sl = slice(h * DH, (h + 1) * DH)
            q = q_ref[:, sl]
            sc = jnp.where(mask_c, _dot_nt(q, kc_ref[:, sl]) * 0.125, NEG_INF)
            sp = jnp.where(mask_p, _dot_nt(q, kp_ref[:, sl]) * 0.125, NEG_INF)
            m = jnp.maximum(jnp.max(sc, axis=-1, keepdims=True), jnp.max(sp, axis=-1, keepdims=True))
            pc = jnp.exp(sc - m)
            pp = jnp.exp(sp - m)
            l = jnp.sum(pc, axis=-1, keepdims=True) + jnp.sum(pp, axis=-1, keepdims=True)
            o = _dot(pc.astype(BF16), vc_ref[:, sl]) + _dot(pp.astype(BF16), vp_ref[:, sl])
            o_ref[:, sl] = o / l
            lse_ref[:, sl] = jnp.broadcast_to(m + jnp.log(l), (BLK, DH))

    cur = pl.BlockSpec((None, BLK, GRP_W), lambda g, b: (g, b, 0))
    prev = pl.BlockSpec((None, BLK, GRP_W), lambda g, b: (g, jnp.maximum(b - 1, 0), 0))
    shp = jax.ShapeDtypeStruct((NG, S, GRP_W), F32)
    return pl.pallas_call(
        body, name=name, grid=(NG, NBLK),
        in_specs=[cur, cur, prev, cur, prev], out_specs=[cur, cur], out_shape=[shp, shp],
        compiler_params=_cp(("parallel", "parallel")),
    )(qp, kp, kp, vp, vp)


def _attn_combine(o3, lse3, name):
    tm = 256

    def body(o_ref, lse_ref, y_ref, l_ref):
        l0, l1, l2 = lse_ref[0], lse_ref[1], lse_ref[2]
        m = jnp.maximum(jnp.maximum(l0, l1), l2)
        e0, e1, e2 = jnp.exp(l0 - m), jnp.exp(l1 - m), jnp.exp(l2 - m)
        den = e0 + e1 + e2
        y_ref[...] = (e0 * o_ref[0] + e1 * o_ref[1] + e2 * o_ref[2]) / den
        l_ref[...] = m + jnp.log(den)

    blk3 = pl.BlockSpec((NG, tm, GRP_W), lambda i: (0, i, 0))
    blk = pl.BlockSpec((tm, GRP_W), lambda i: (i, 0))
    shp = jax.ShapeDtypeStruct((S, GRP_W), F32)
    return pl.pallas_call(
        body, name=name, grid=(S // tm,), in_specs=[blk3, blk3], out_specs=[blk, blk], out_shape=[shp, shp],
        compiler_params=_cp(("parallel",)),
    )(o3, lse3)


def _attn_bwd(qp, kp, vp, dyp, yp, lp, name):
    def body(q_ref, qn_ref, k_ref, kp_ref, v_ref, vp_ref, dy_ref, dyn_ref, y_ref, yn_ref, l_ref, ln_ref,
             dq_ref, dk_ref, dv_ref):
        g = pl.program_id(0)
        b = pl.program_id(1)
        per = _blocks_per_class(g)
        has_prev = lax.rem(b, per) != 0
        has_next = lax.rem(b + 1, per) != 0
        qi = lax.broadcasted_iota(jnp.int32, (BLK, BLK), 0)
        kj = lax.broadcasted_iota(jnp.int32, (BLK, BLK), 1)
        mask_c = kj <= qi
        mask_p = jnp.logical_and(kj >= qi, has_prev)
        mask_n = jnp.logical_and(kj >= qi, has_next)
        for h in range(NH):
            sl = slice(h * DH, (h + 1) * DH)
            q, qn, k, kpv, v, vpv = q_ref[:, sl], qn_ref[:, sl], k_ref[:, sl], kp_ref[:, sl], v_ref[:, sl], vp_ref[:, sl]
            dy, dyn = dy_ref[:, sl], dyn_ref[:, sl]
            dd = jnp.sum(dy * y_ref[:, sl], axis=-1, keepdims=True)
            ddn = jnp.sum(dyn * yn_ref[:, sl], axis=-1, keepdims=True)
            lcol = l_ref[:, h * DH:h * DH + 1]
            lncol = ln_ref[:, h * DH:h * DH + 1]
            dyb, dynb = dy.astype(BF16), dyn.astype(BF16)
            p = jnp.exp(jnp.where(mask_c, _dot_nt(q, k) * 0.125, NEG_INF) - lcol)
            ds = (p * (_dot_nt(dyb, v) - dd)).astype(BF16)
            dq = _dot(ds, k)
            dk = _dot_tn(ds, q)
            dv = _dot_tn(p.astype(BF16), dyb)
            pp = jnp.exp(jnp.where(mask_p, _dot_nt(q, kpv) * 0.125, NEG_INF) - lcol)
            dsp = (pp * (_dot_nt(dyb, vpv) - dd)).astype(BF16)
            dq = dq + _dot(dsp, kpv)
            pn = jnp.exp(jnp.where(mask_n, _dot_nt(qn, k) * 0.125, NEG_INF) - lncol)
            dsn = (pn * (_dot_nt(dynb, v) - ddn)).astype(BF16)
            dk = dk + _dot_tn(dsn, qn)
            dv = dv + _dot_tn(pn.astype(BF16), dynb)
            dq_ref[:, sl] = dq * 0.125
            dk_ref[:, sl] = dk * 0.125
            dv_ref[:, sl] = dv

    cur = pl.BlockSpec((None, BLK, GRP_W), lambda g, b: (g, b, 0))
    prev = pl.BlockSpec((None, BLK, GRP_W), lambda g, b: (g, jnp.maximum(b - 1, 0), 0))
    nxt = pl.BlockSpec((None, BLK, GRP_W), lambda g, b: (g, jnp.minimum(b + 1, NBLK - 1), 0))
    shp = jax.ShapeDtypeStruct((NG, S, GRP_W), F32)
    return pl.pallas_call(
        body, name=name, grid=(NG, NBLK),
        in_specs=[cur, nxt, cur, prev, cur, prev, cur, nxt, cur, nxt, cur, nxt],
        out_specs=[cur, cur, cur], out_shape=[shp, shp, shp],
        compiler_params=_cp(("parallel", "parallel")),
    )(qp, qp, kp, kp, vp, vp, dyp, dyp, yp, yp, lp, lp)


_SQRT_HALF = 0.7071067811865476
_INV_SQRT_2PI = 0.3989422804014327


def _gelu(z):
    return 0.5 * z * (1.0 + lax.erf(z * _SQRT_HALF))


def _gelu_grad(z):
    return 0.5 * (1.0 + lax.erf(z * _SQRT_HALF)) + z * (jnp.exp(-0.5 * z * z) * _INV_SQRT_2PI)


def _tril_mask():
    t = lax.broadcasted_iota(jnp.int32, (BLK, BLK), 0)
    s = lax.broadcasted_iota(jnp.int32, (BLK, BLK), 1)
    return s <= t


def _gmlp_fwd(z, ln_g, ln_b, w_s, b_s_t, name):
    def body(z_ref, g_ref, b_ref, ws_ref, bs_ref, y_ref):
        zg = _gelu(z_ref[...])
        u = zg[:, :GW]
        xh, _ = _ln_stats(zg[:, GW:])
        vn = (xh * g_ref[...] + b_ref[...]).astype(BF16)
        tril = _tril_mask()
        for gg in range(8):
            sl = slice(gg * BLK, (gg + 1) * BLK)
            wt = jnp.where(tril, ws_ref[gg], 0.0).astype(BF16)
            mixed = _dot(wt, vn[:, sl]) + bs_ref[:, gg:gg + 1]
            y_ref[:, sl] = u[:, sl] * mixed

    vec = pl.BlockSpec((1, GW), lambda n: (0, 0))
    return pl.pallas_call(
        body, name=name, grid=(NBLK,),
        in_specs=[pl.BlockSpec((BLK, 2 * GW), lambda n: (n, 0)), vec, vec,
                  pl.BlockSpec((8, BLK, BLK), lambda n: (0, 0, 0)), pl.BlockSpec((BLK, 8), lambda n: (0, 0))],
        out_specs=pl.BlockSpec((BLK, GW), lambda n: (n, 0)),
        out_shape=jax.ShapeDtypeStruct((S, GW), F32),
        compiler_params=_cp(("parallel",)),
    )(z, ln_g, ln_b, w_s, b_s_t)


def _gmlp_bwd(z, dy, ln_g, ln_b, w_s, b_s_t, name):
    def body(z_ref, dy_ref, g_ref, b_ref, ws_ref, bs_ref, dz_ref, dws_ref, dbs_ref, dg_ref, db_ref, dvn_ref):
        n = pl.program_id(0)
        zv = z_ref[...]
        zg = _gelu(zv)
        u = zg[:, :GW]
        xh, rstd = _ln_stats(zg[:, GW:])
        vn = (xh * g_ref[...] + b_ref[...]).astype(BF16)
        tril = _tril_mask()

        @pl.when(n == 0)
        def _():
            dws_ref[...] = jnp.zeros_like(dws_ref)
            dbs_ref[...] = jnp.zeros_like(dbs_ref)
            dg_ref[...] = jnp.zeros_like(dg_ref)
            db_ref[...] = jnp.zeros_like(db_ref)

        for gg in range(8):
            sl = slice(gg * BLK, (gg + 1) * BLK)
            wt = jnp.where(tril, ws_ref[gg], 0.0).astype(BF16)
            dyg = dy_ref[:, sl]
            mixed = _dot(wt, vn[:, sl]) + bs_ref[:, gg:gg + 1]
            dz_ref[:, sl] = (dyg * mixed * _gelu_grad(zv[:, sl])).astype(BF16)
            dmix = dyg * u[:, sl]
            dmb = dmix.astype(BF16)
            dws_ref[gg] += jnp.where(tril, _dot_nt(dmb, vn[:, sl]), 0.0)
            dbs_ref[:, gg:gg + 1] += jnp.sum(dmix, axis=-1, keepdims=True)
            dvn_ref[:, sl] = _dot_tn(wt, dmb)

        dvn = dvn_ref[...]
        dg_ref[...] += jnp.sum(dvn * xh, axis=0, keepdims=True)
        db_ref[...] += jnp.sum(dvn, axis=0, keepdims=True)
        dvg = _ln_dx(dvn * g_ref[...], xh, rstd)
        dz_ref[:, GW:] = (dvg * _gelu_grad(zv[:, GW:])).astype(BF16)

    vec = pl.BlockSpec((1, GW), lambda n: (0, 0))
    ws = pl.BlockSpec((8, BLK, BLK), lambda n: (0, 0, 0))
    bs = pl.BlockSpec((BLK, 8), lambda n: (0, 0))
    return pl.pallas_call(
        body, name=name, grid=(NBLK,),
        in_specs=[pl.BlockSpec((BLK, 2 * GW), lambda n: (n, 0)), pl.BlockSpec((BLK, GW), lambda n: (n, 0)),
                  vec, vec, ws, bs],
        out_specs=[pl.BlockSpec((BLK, 2 * GW), lambda n: (n, 0)), ws, bs, vec, vec],
        out_shape=[jax.ShapeDtypeStruct((S, 2 * GW), BF16), jax.ShapeDtypeStruct((8, BLK, BLK), F32),
                   jax.ShapeDtypeStruct((BLK, 8), F32), jax.ShapeDtypeStruct((1, GW), F32),
                   jax.ShapeDtypeStruct((1, GW), F32)],
        scratch_shapes=[pltpu.VMEM((BLK, GW), F32)],
        compiler_params=_cp(("arbitrary",)),
    )(z, dy, ln_g, ln_b, w_s, b_s_t)


def _merge_fwd(a, b, gl, b_gates, name):
    tm = 256

    def body(a_ref, b_ref, g0_ref, g1_ref, bg_ref, o_ref):
        g0 = jax.nn.sigmoid(g0_ref[...] + bg_ref[:, :D])
        g1 = jax.nn.sigmoid(g1_ref[...] + bg_ref[:, D:])
        o_ref[...] = (g0 * a_ref[...] + g1 * b_ref[...]).astype(BF16)

    row = pl.BlockSpec((tm, D), lambda i: (i, 0))
    return pl.pallas_call(
        body, name=name, grid=(S // tm,),
        in_specs=[row, row, row, pl.BlockSpec((tm, D), lambda i: (i, 1)), pl.BlockSpec((1, 2 * D), lambda i: (0, 0))],
        out_specs=row, out_shape=jax.ShapeDtypeStruct((S, D), BF16),
        compiler_params=_cp(("parallel",)),
    )(a, b, gl, gl, b_gates)


def _merge_bwd(dm, a, b, gl, b_gates, name):
    tm = 256

    def body(dm_ref, a_ref, b_ref, g0_ref, g1_ref, bg_ref, da_ref, db_ref, dgl_ref, dbg_ref):
        i = pl.program_id(0)
        dmv = dm_ref[...]
        g0 = jax.nn.sigmoid(g0_ref[...] + bg_ref[:, :D])
        g1 = jax.nn.sigmoid(g1_ref[...] + bg_ref[:, D:])
        da_ref[...] = (dmv * g0).astype(BF16)
        db_ref[...] = (dmv * g1).astype(BF16)
        d0 = dmv * a_ref[...] * g0 * (1.0 - g0)
        d1 = dmv * b_ref[...] * g1 * (1.0 - g1)
        dgl_ref[:, :D] = d0.astype(BF16)
        dgl_ref[:, D:] = d1.astype(BF16)
        s0 = jnp.sum(d0, axis=0, keepdims=True)
        s1 = jnp.sum(d1, axis=0, keepdims=True)

        @pl.when(i == 0)
        def _():
            dbg_ref[:, :D] = s0
            dbg_ref[:, D:] = s1

        @pl.when(i > 0)
        def _():
            dbg_ref[:, :D] += s0
            dbg_ref[:, D:] += s1

    row = pl.BlockSpec((tm, D), lambda i: (i, 0))
    wide = pl.BlockSpec((tm, 2 * D), lambda i: (i, 0))
    bg = pl.BlockSpec((1, 2 * D), lambda i: (0, 0))
    return pl.pallas_call(
        body, name=name, grid=(S // tm,),
        in_specs=[row, row, row, row, pl.BlockSpec((tm, D), lambda i: (i, 1)), bg],
        out_specs=[row, row, wide, bg],
        out_shape=[jax.ShapeDtypeStruct((S, D), BF16), jax.ShapeDtypeStruct((S, D), BF16),
                   jax.ShapeDtypeStruct((S, 2 * D), BF16), jax.ShapeDtypeStruct((1, 2 * D), F32)],
        compiler_params=_cp(("arbitrary",)),
    )(dm, a, b, gl, gl, b_gates)


def _adam_math(w, g, m, v):
    m2 = ADAM_B1 * m + (1.0 - ADAM_B1) * g
    v2 = ADAM_B2 * v + (1.0 - ADAM_B2) * (g * g)
    m_hat = m2 / (1.0 - ADAM_B1 ** ADAM_STEP)
    v_hat = v2 / (1.0 - ADAM_B2 ** ADAM_STEP)
    delta = -ADAM_LR * (m_hat / (jnp.sqrt(v_hat) + ADAM_EPS) + ADAM_WD * w)
    return delta, m2, v2


def _pick_rows(rows, cols, unit=16, budget=MIB):
    best = unit
    for t in range(unit, rows + 1, unit):
        if rows % t == 0 and t * cols * 4 <= budget:
            best = t
    assert rows % best == 0
    return best


def _adamw(w, g, m, v, name):
    r, c = w.shape
    tr = _pick_rows(r, c, unit=8)

    def body(w_ref, g_ref, m_ref, v_ref, go_ref, d_ref, mo_ref, vo_ref):
        gv = g_ref[...]
        delta, m2, v2 = _adam_math(w_ref[...], gv, m_ref[...], v_ref[...])
        go_ref[...] = gv
        d_ref[...] = delta
        mo_ref[...] = m2
        vo_ref[...] = v2

    blk = pl.BlockSpec((tr, c), lambda i: (i, 0))
    shp = jax.ShapeDtypeStruct((r, c), F32)
    return pl.pallas_call(
        body, name=name, grid=(r // tr,), in_specs=[blk] * 4, out_specs=[blk] * 4, out_shape=[shp] * 4,
        compiler_params=_cp(("parallel",)),
    )(w, g, m, v)


def _small_sum_adamw(parts, w, m, v, name):
    tr = 48

    def body(p_ref, w_ref, m_ref, v_ref, g_ref, d_ref, mo_ref, vo_ref):
        gv = p_ref[0]
        for k in range(1, 8):
            gv = gv + p_ref[k]
        delta, m2, v2 = _adam_math(w_ref[...], gv, m_ref[...], v_ref[...])
        g_ref[...] = gv
        d_ref[...] = delta
        mo_ref[...] = m2
        vo_ref[...] = v2

    blk = pl.BlockSpec((tr, D), lambda i: (i, 0))
    shp = jax.ShapeDtypeStruct((SMALL_ROWS, D), F32)
    return pl.pallas_call(
        body, name=name, grid=(SMALL_ROWS // tr,),
        in_specs=[pl.BlockSpec((8, tr, D), lambda i: (0, i, 0)), blk, blk, blk],
        out_specs=[blk] * 4, out_shape=[shp] * 4,
        compiler_params=_cp(("parallel",)),
    )(parts, w, m, v)


ANY = pl.BlockSpec(memory_space=pl.ANY)


def _mesh_pos():
    x, y, c = lax.axis_index("x"), lax.axis_index("y"), lax.axis_index("c")
    chips = [(1 - x, y), (x, 1 - y), (1 - x, 1 - y)]
    return x, y, c, chips


def _place_shard(w, kind, pos, name):
    r, c = w.shape
    tr = _pick_rows(r, c)

    def body(pos_ref, w_ref, o_ref):
        o_ref[...] = w_ref[...].astype(BF16)

    if kind == "stack":
        o_spec = pl.BlockSpec((None, tr, c), lambda i, p: (p[1], i, 0))
        shape = (NSH, r, c)
    else:
        o_spec = pl.BlockSpec((tr, c), lambda i, p: (i, p[1]))
        shape = (r, NSH * c)
    return pl.pallas_call(
        body, name=name,
        grid_spec=pltpu.PrefetchScalarGridSpec(
            num_scalar_prefetch=1, grid=(r // tr,),
            in_specs=[pl.BlockSpec((tr, c), lambda i, p: (i, 0))], out_specs=o_spec),
        out_shape=jax.ShapeDtypeStruct(shape, BF16),
        compiler_params=_cp(("parallel",)),
    )(pos, w)


def _gather_weights(fulls, kinds, dims):
    n = len(fulls)

    def window(ref, kind, j, h, r, c):
        rows = pl.ds(pl.multiple_of(h * (r // 2), 16), r // 2)
        if kind == "stack":
            return ref.at[j, rows, :]
        return ref.at[rows, pl.ds(pl.multiple_of(j * c, 128), c)]

    def body(*refs):
        outs = refs[n:2 * n]
        send_sems, recv_sems = refs[2 * n:]
        x, y, c, chips = _mesh_pos()
        me = 2 * x + y
        sib = (x, y, 1 - c)

        def rc(a, k, j, h, to):
            r, cc = dims[a]
            win = window(outs[a], kinds[a], j, h, r, cc)
            return pltpu.make_async_remote_copy(
                src_ref=win, dst_ref=win, send_sem=send_sems.at[a * 6 + k], recv_sem=recv_sems.at[a * 6 + k],
                device_id=to, device_id_type=MESH_T)

        first, passed = [], []
        for a in range(n):
            for k, chip in enumerate(chips):
                cp = rc(a, k, me, c, (chip[0], chip[1], c))
                cp.start()
                first.append(cp)
        for a in range(n):
            for k, chip in enumerate(chips):
                j = 2 * chip[0] + chip[1]
                rc(a, k, j, c, sib).wait_recv()
                fw = rc(a, 3 + k, j, c, sib)
                fw.start()
                passed.append(fw)
        for a in range(n):
            for k, chip in enumerate(chips):
                j = 2 * chip[0] + chip[1]
                rc(a, 3 + k, j, 1 - c, sib).wait_recv()
        for cp in first + passed:
            cp.wait_send()

    return pl.pallas_call(
        body, name="gather_weights", in_specs=[ANY] * n, out_specs=[ANY] * n,
        out_shape=[jax.ShapeDtypeStruct(f.shape, BF16) for f in fulls],
        input_output_aliases={i: i for i in range(n)},
        scratch_shapes=[pltpu.SemaphoreType.DMA((6 * n,)), pltpu.SemaphoreType.DMA((6 * n,))],
    )(*fulls)


def _pair_exchange(grads):
    n = len(grads)

    def body(*refs):
        ins, outs = refs[:n], refs[n:2 * n]
        send_sems, recv_sems = refs[2 * n:]
        x, y, c, _ = _mesh_pos()
        cps = []
        for a in range(n):
            cp = pltpu.make_async_remote_copy(
                src_ref=ins[a].at[1 - c], dst_ref=outs[a], send_sem=send_sems.at[a], recv_sem=recv_sems.at[a],
                device_id=(x, y, 1 - c), device_id_type=MESH_T)
            cp.start()
            cps.append(cp)
        for cp in cps:
            cp.wait()

    return pl.pallas_call(
        body, name="rs_pair_exchange", in_specs=[ANY] * n, out_specs=[ANY] * n,
        out_shape=[jax.ShapeDtypeStruct(g.shape[1:], F32) for g in grads],
        scratch_shapes=[pltpu.SemaphoreType.DMA((n,)), pltpu.SemaphoreType.DMA((n,))],
    )(*grads)


def _pair_sum(g, recv, pos, name):
    _, _, rh, c = g.shape
    tr = _pick_rows(rh, c)

    def body(pos_ref, g_ref, r_ref, o_ref):
        o_ref[...] = (g_ref[...] + r_ref[...]).astype(BF16)

    return pl.pallas_call(
        body, name=name,
        grid_spec=pltpu.PrefetchScalarGridSpec(
            num_scalar_prefetch=1, grid=(NSH, rh // tr),
            in_specs=[pl.BlockSpec((None, None, tr, c), lambda j, r, p: (p[0], j, r, 0)),
                      pl.BlockSpec((None, tr, c), lambda j, r, p: (j, r, 0))],
            out_specs=pl.BlockSpec((None, tr, c), lambda j, r, p: (j, r, 0))),
        out_shape=jax.ShapeDtypeStruct((NSH, rh, c), BF16),
        compiler_params=_cp(("parallel", "parallel")),
    )(pos, g, recv)


def _chip_exchange(psums):
    n = len(psums)

    def body(*refs):
        ins, outs = refs[:n], refs[n:2 * n]
        send_sems, recv_sems = refs[2 * n:]
        x, y, c, chips = _mesh_pos()
        cps = []
        for a in range(n):
            for k, chip in enumerate(chips):
                j = 2 * chip[0] + chip[1]
                cp = pltpu.make_async_remote_copy(
                    src_ref=ins[a].at[j], dst_ref=outs[a].at[k],
                    send_sem=send_sems.at[a * 3 + k], recv_sem=recv_sems.at[a * 3 + k],
                    device_id=(chip[0], chip[1], c), device_id_type=MESH_T)
                cp.start()
                cps.append(cp)
        for cp in cps:
            cp.wait()

    return pl.pallas_call(
        body, name="rs_chip_exchange", in_specs=[ANY] * n, out_specs=[ANY] * n,
        out_shape=[jax.ShapeDtypeStruct((3,) + p.shape[1:], BF16) for p in psums],
        scratch_shapes=[pltpu.SemaphoreType.DMA((3 * n,)), pltpu.SemaphoreType.DMA((3 * n,))],
    )(*psums)


def _owner_sum(g, recv_a, recv_b, pos, name):
    _, _, rh, c = g.shape
    tr = _pick_rows(rh, c)

    def body(pos_ref, g_ref, ra_ref, rb_ref, o_ref):
        acc = g_ref[...] + ra_ref[...]
        for k in range(3):
            acc = acc + rb_ref[k].astype(F32)
        o_ref[...] = acc

    return pl.pallas_call(
        body, name=name,
        grid_spec=pltpu.PrefetchScalarGridSpec(
            num_scalar_prefetch=1, grid=(rh // tr,),
            in_specs=[pl.BlockSpec((None, None, tr, c), lambda r, p: (p[0], p[1], r, 0)),
                      pl.BlockSpec((None, tr, c), lambda r, p: (p[1], r, 0)),
                      pl.BlockSpec((3, tr, c), lambda r, p: (0, r, 0))],
            out_specs=pl.BlockSpec((None, tr, c), lambda r, p: (p[0], r, 0))),
        out_shape=jax.ShapeDtypeStruct((2, rh, c), F32),
        compiler_params=_cp(("parallel",)),
    )(pos, g, recv_a, recv_b)


def _sibling_allgather(halves):
    n = len(halves)

    def body(*refs):
        outs = refs[n:2 * n]
        send_sems, recv_sems = refs[2 * n:]
        x, y, c, _ = _mesh_pos()
        cps = []
        for a in range(n):
            cp = pltpu.make_async_remote_copy(
                src_ref=outs[a].at[c], dst_ref=outs[a].at[c], send_sem=send_sems.at[a], recv_sem=recv_sems.at[a],
                device_id=(x, y, 1 - c), device_id_type=MESH_T)
            cp.start()
            cps.append(cp)
        for a in range(n):
            cps[a].wait_send()
            pltpu.make_async_remote_copy(
                src_ref=outs[a].at[1 - c], dst_ref=outs[a].at[1 - c], send_sem=send_sems.at[a],
                recv_sem=recv_sems.at[a], device_id=(x, y, 1 - c), device_id_type=MESH_T).wait_recv()

    return pl.pallas_call(
        body, name="rs_sibling_allgather", in_specs=[ANY] * n, out_specs=[ANY] * n,
        out_shape=[jax.ShapeDtypeStruct(h.shape, F32) for h in halves],
        input_output_aliases={i: i for i in range(n)},
        scratch_shapes=[pltpu.SemaphoreType.DMA((n,)), pltpu.SemaphoreType.DMA((n,))],
    )(*halves)


def _small_allgather(part):
    m_per = SMALL_ROWS

    def body(x_ref, out_ref, send_sems, recv_sems, local_sem):
        x, y, c, chips = _mesh_pos()
        me, sibling = (x, y, c), (x, y, 1 - c)

        def rows(px, py, pc):
            return out_ref.at[pl.ds((4 * px + 2 * py + pc) * m_per, m_per), :]

        def copy(k, block, to, src=None):
            return pltpu.make_async_remote_copy(
                src_ref=rows(*block) if src is None else src, dst_ref=rows(*block),
                send_sem=send_sems.at[k], recv_sem=recv_sems.at[k], device_id=to, device_id_type=MESH_T)

        mine = pltpu.make_async_copy(x_ref, rows(*me), local_sem)
        mine.start()
        first = [copy(0, me, sibling, src=x_ref)]
        first += [copy(1 + j, me, (*chip, c), src=x_ref) for j, chip in enumerate(chips)]
        for cp in first:
            cp.start()
        passed = [copy(4 + j, (*chip, c), sibling) for j, chip in enumerate(chips)]
        for j, chip in enumerate(chips):
            copy(1 + j, (*chip, c), me).wait_recv()
            passed[j].start()
        copy(0, sibling, me).wait_recv()
        for j, chip in enumerate(chips):
            copy(4 + j, (*chip, 1 - c), me).wait_recv()
        for cp in first + passed:
            cp.wait_send()
        mine.wait()

    return pl.pallas_call(
        body, name="small_allgather",
        out_shape=jax.ShapeDtypeStruct((8 * m_per, D), F32),
        in_specs=[pl.BlockSpec(memory_space=pltpu.VMEM)], out_specs=pl.BlockSpec(memory_space=pltpu.VMEM),
        scratch_shapes=[pltpu.SemaphoreType.DMA((7,)), pltpu.SemaphoreType.DMA((7,)), pltpu.SemaphoreType.DMA],
    )(part)


def _to_classes(t, d):
    if d == 1:
        return t
    return t.reshape(S // d, d, t.shape[-1]).transpose(1, 0, 2).reshape(S, t.shape[-1])


def _from_classes(t, d):
    if d == 1:
        return t
    return t.reshape(d, S // d, t.shape[-1]).transpose(1, 0, 2).reshape(S, t.shape[-1])


def _group_stack(t, col0):
    return jnp.stack([_to_classes(t[:, col0 + gi * GRP_W: col0 + (gi + 1) * GRP_W], d)
                      for gi, d in enumerate(DILATIONS)])


def _group_perm(t):
    return jnp.stack([_to_classes(t, d) for d in DILATIONS])


def _group_unstack(t3):
    return jnp.concatenate([_from_classes(t3[gi], d) for gi, d in enumerate(DILATIONS)], axis=1)


def _pack_small(ln1_g, ln1_b, gln_g, gln_b, ln2_g, ln2_b, ln3_g, ln3_b, b_gates, b_s, w_s):
    rows = [ln1_g, ln1_b, gln_g, gln_b, ln2_g, ln2_b, ln3_g, ln3_b]
    rows = [r.reshape(1, D) for r in rows] + [b_gates.reshape(2, D), b_s.reshape(1, D), jnp.zeros((5, D), F32),
                                             w_s.reshape(128, D)]
    return jnp.concatenate(rows, axis=0)


def _unpack_small(p):
    out = [p[i:i + 1] for i in range(8)]
    return out + [p[8:10].reshape(1, 2 * D), p[10:11].reshape(1, 8, BLK), p[16:144].reshape(1, 8, BLK, BLK)]


def _local_step(x, pos_f, target, W, P):
    invf = ROPE_THETA ** (-jnp.arange(0, DH, 2, dtype=F32) / DH)
    invf = jnp.tile(invf, 4).reshape(1, 128)
    b_s_t = P["gmlp_b_s"].T

    h1, h1b, xh1, rstd1, a1, b1 = _ffn_fwd(x, W["f1g"], W["f1u"], W["f1d"], P["ln1_g"], P["ln1_b"], "ffn1_fwd")
    qkv = _matmul(h1b, W["w_in"], "nn", "proj_qkv", n=3 * ATT_W, b_col0=0)
    z = _matmul(h1b, W["w_in"], "nn", "proj_z", n=2 * GW, b_col0=3 * ATT_W)
    gl = _matmul(h1b, W["w_in"], "nn", "proj_gates", n=2 * D, b_col0=3 * ATT_W + 2 * GW)
    qkvb = _rope(qkv, pos_f, invf, 1.0, "rope_fwd")
    qp, kp, vp = _group_stack(qkvb, 0), _group_stack(qkvb, ATT_W), _group_stack(qkvb, 2 * ATT_W)
    o3, lse3 = _attn_fwd(qp, kp, vp, "attn_fwd")
    o3t = jnp.stack([_from_classes(o3[gi], d) for gi, d in enumerate(DILATIONS)])
    lse3t = jnp.stack([_from_classes(lse3[gi], d) for gi, d in enumerate(DILATIONS)])
    y_attn, lse = _attn_combine(o3t, lse3t, "attn_combine")
    y_gmlp = _gmlp_fwd(z, P["gmlp_ln_g"], P["gmlp_ln_b"], P["gmlp_w_s"], b_s_t, "gmlp_fwd")
    br_a = _matmul(y_attn, W["w_ab"], "nn", "branch_attn", n=D)
    br_b = _matmul(y_gmlp, W["w_gb"], "nn", "branch_gmlp", n=D)
    merged = _merge_fwd(br_a, br_b, gl, P["b_gates"], "merge_fwd")
    mix = _matmul(merged, W["w_out"], "nn", "mix_out", n=D)
    h2, h2b, xh2, rstd2 = _resid_ln(h1, mix, P["ln2_g"], P["ln2_b"], "resid_ln2")
    y, _, xh3, rstd3, a2, b2 = _ffn_fwd(h2, W["f2g"], W["f2u"], W["f2d"], P["ln3_g"], P["ln3_b"], "ffn2_fwd")

    dr3, dg3, db3, loss = _ln_bwd(y, xh3, rstd3, P["ln3_g"], "loss_ln3_bwd", target=target)
    da2, dbb2, s2, dh2 = _ffn_bwd_act(dr3, a2, b2, W["f2g"], W["f2u"], W["f2d"], "ffn2_bwd")
    g_f2g = _wgrad_cols(h2b, da2, D // 2, FSH, "dw_ffn2_gate", y_stacked=True)
    g_f2u = _wgrad_cols(h2b, dbb2, D // 2, FSH, "dw_ffn2_up", y_stacked=True)
    g_f2d = _wgrad_rows(s2, dr3, FSH, "dw_ffn2_down", x_stacked=True, y_scale=0.5)
    dr2, dg2, db2 = _ln_bwd(dh2, xh2, rstd2, P["ln2_g"], "ln2_bwd")
    g_wout = _wgrad_rows(merged, dr2, 256, "dw_out")
    dmerged = _matmul(dr2, W["w_out"], "nt", "dmerged", n=D)
    dab, dbb, dglb, dbg = _merge_bwd(dmerged, br_a, br_b, gl, P["b_gates"], "merge_bwd")
    g_wab = _wgrad_cols(y_attn, dab, GRP_W // 2, 256, "dw_attn_branch")
    g_wgb = _wgrad_rows(y_gmlp, dbb, 256, "dw_gmlp_branch")
    dy_attn = _matmul(dab, W["w_ab"], "nt", "dy_attn", n=GRP_W)
    dy_gmlp = _matmul(dbb, W["w_gb"], "nt", "dy_gmlp", n=GW)
    dzb, dws, dbs_t, dgln_g, dgln_b = _gmlp_bwd(z, dy_gmlp, P["gmlp_ln_g"], P["gmlp_ln_b"], P["gmlp_w_s"], b_s_t,
                                                 "gmlp_bwd")
    dyp, yp, lp = _group_perm(dy_attn), _group_perm(y_attn), _group_perm(lse)
    dq3, dk3, dv3 = _attn_bwd(qp, kp, vp, dyp, yp, lp, "attn_bwd")
    dqkv_rot = jnp.concatenate([_group_unstack(dq3), _group_unstack(dk3), _group_unstack(dv3)], axis=1)
    dqkvb = _rope(dqkv_rot, pos_f, invf, -1.0, "rope_bwd")
    dproj = jnp.concatenate([dqkvb, dzb, dglb], axis=1)
    g_win = _wgrad_cols(h1b, dproj, D // 2, IN_SH, "dw_in")
    dh1 = _matmul(dproj, W["w_in"], "nt", "dh1", n=D, tn=D, tk=IN_SH,
                  add=dr2, add_scale=ALPHA)
    dr1, dg1, db1 = _ln_bwd(dh1, xh1, rstd1, P["ln1_g"], "ln1_bwd")
    da1, dbb1, s1, dx = _ffn_bwd_act(dr1, a1, b1, W["f1g"], W["f1u"], W["f1d"], "ffn1_bwd")
    g_f1g = _wgrad_cols(x, da1, D // 2, FSH, "dw_ffn1_gate", y_stacked=True)
    g_f1u = _wgrad_cols(x, dbb1, D // 2, FSH, "dw_ffn1_up", y_stacked=True)
    g_f1d = _wgrad_rows(s1, dr1, FSH, "dw_ffn1_down", x_stacked=True, y_scale=0.5)

    big = dict(f1g=g_f1g, f1u=g_f1u, f1d=g_f1d, w_in=g_win, w_ab=g_wab, w_gb=g_wgb, w_out=g_wout,
               f2g=g_f2g, f2u=g_f2u, f2d=g_f2d)
    small = _pack_small(dg1, db1, dgln_g, dgln_b, dg2, db2, dg3, db3, dbg, dbs_t.T, dws)
    return loss, dx, big, small


BIG = ("f1g", "f1u", "f1d", "w_in", "w_ab", "w_gb", "w_out", "f2g", "f2u", "f2d")
KIND = dict(f1g="stack", f1u="stack", f1d="stack", w_in="col", w_ab="col", w_gb="stack", w_out="stack",
            f2g="stack", f2u="stack", f2d="stack")


def kernel(x, positions, ffn1_w_gate, ffn1_w_up, ffn1_w_down, ln1_g, ln1_b, w_in, b_gates, gmlp_ln_g, gmlp_ln_b, gmlp_w_s, gmlp_b_s, w_attn_branch, w_gmlp_branch, w_out, ln2_g, ln2_b, ffn2_w_gate, ffn2_w_up, ffn2_w_down, ln3_g, ln3_b, loss_target, m_ffn1_w_gate, m_ffn1_w_up, m_ffn1_w_down, m_ln1_g, m_ln1_b, m_w_in, m_b_gates, m_gmlp_ln_g, m_gmlp_ln_b, m_gmlp_w_s, m_gmlp_b_s, m_w_attn_branch, m_w_gmlp_branch, m_w_out, m_ln2_g, m_ln2_b, m_ffn2_w_gate, m_ffn2_w_up, m_ffn2_w_down, m_ln3_g, m_ln3_b, v_ffn1_w_gate, v_ffn1_w_up, v_ffn1_w_down, v_ln1_g, v_ln1_b, v_w_in, v_b_gates, v_gmlp_ln_g, v_gmlp_ln_b, v_gmlp_w_s, v_gmlp_b_s, v_w_attn_branch, v_w_gmlp_branch, v_w_out, v_ln2_g, v_ln2_b, v_ffn2_w_gate, v_ffn2_w_up, v_ffn2_w_down, v_ln3_g, v_ln3_b):
    cx, cy, cc = lax.axis_index("x"), lax.axis_index("y"), lax.axis_index("c")
    pos = jnp.stack([cc, 2 * cx + cy]).astype(jnp.int32)

    w_sh = dict(f1g=ffn1_w_gate, f1u=ffn1_w_up, f1d=ffn1_w_down, w_in=w_in, w_ab=w_attn_branch,
                w_gb=w_gmlp_branch, w_out=w_out, f2g=ffn2_w_gate, f2u=ffn2_w_up, f2d=ffn2_w_down)
    m_sh = dict(f1g=m_ffn1_w_gate, f1u=m_ffn1_w_up, f1d=m_ffn1_w_down, w_in=m_w_in, w_ab=m_w_attn_branch,
                w_gb=m_w_gmlp_branch, w_out=m_w_out, f2g=m_ffn2_w_gate, f2u=m_ffn2_w_up, f2d=m_ffn2_w_down)
    v_sh = dict(f1g=v_ffn1_w_gate, f1u=v_ffn1_w_up, f1d=v_ffn1_w_down, w_in=v_w_in, w_ab=v_w_attn_branch,
                w_gb=v_w_gmlp_branch, w_out=v_w_out, f2g=v_ffn2_w_gate, f2u=v_ffn2_w_up, f2d=v_ffn2_w_down)
    w_sh = {k: v[0] for k, v in w_sh.items()}
    m_sh = {k: v[0] for k, v in m_sh.items()}
    v_sh = {k: v[0] for k, v in v_sh.items()}

    placed = [_place_shard(w_sh[k], KIND[k], pos, "place_" + k) for k in BIG]
    full = _gather_weights(placed, [KIND[k] for k in BIG], [w_sh[k].shape for k in BIG])
    W = dict(zip(BIG, full))
    W["w_gb"] = W["w_gb"].reshape(D, D)
    W["w_out"] = W["w_out"].reshape(D, D)
    P = dict(ln1_g=ln1_g, ln1_b=ln1_b, ln2_g=ln2_g, ln2_b=ln2_b, ln3_g=ln3_g, ln3_b=ln3_b, b_gates=b_gates,
             gmlp_ln_g=gmlp_ln_g, gmlp_ln_b=gmlp_ln_b, gmlp_w_s=gmlp_w_s[0], gmlp_b_s=gmlp_b_s[0])

    pos_f = positions.reshape(S, 1).astype(F32)
    loss_part, dx, big, small = _local_step(x[0], pos_f, loss_target[0], W, P)
    loss = lax.psum(loss_part[0, 0], ("x", "y", "c"))

    grads = [big[k] for k in BIG]
    recv_a = _pair_exchange(grads)
    psums = [_pair_sum(g, r, pos, "rs_pair_sum_" + k) for g, r, k in zip(grads, recv_a, BIG)]
    recv_b = _chip_exchange(psums)
    halves = [_owner_sum(g, ra, rb, pos, "rs_owner_sum_" + k) for g, ra, rb, k in zip(grads, recv_a, recv_b, BIG)]
    reduced = _sibling_allgather(halves)

    g_out, d_out, m_out, v_out = {}, {}, {}, {}
    for k, gfull in zip(BIG, reduced):
        shp = w_sh[k].shape
        g2, dl, mn, vn = _adamw(w_sh[k], gfull.reshape(shp), m_sh[k], v_sh[k], "adamw_" + k)
        g_out[k], d_out[k], m_out[k], v_out[k] = g2[None], dl[None], mn[None], vn[None]

    parts = _small_allgather(small).reshape(8, SMALL_ROWS, D)
    sp = (ln1_g, ln1_b, gmlp_ln_g, gmlp_ln_b, ln2_g, ln2_b, ln3_g, ln3_b, b_gates, gmlp_b_s, gmlp_w_s)
    sm = (m_ln1_g, m_ln1_b, m_gmlp_ln_g, m_gmlp_ln_b, m_ln2_g, m_ln2_b, m_ln3_g, m_ln3_b, m_b_gates, m_gmlp_b_s,
          m_gmlp_w_s)
    sv = (v_ln1_g, v_ln1_b, v_gmlp_ln_g, v_gmlp_ln_b, v_ln2_g, v_ln2_b, v_ln3_g, v_ln3_b, v_b_gates, v_gmlp_b_s,
          v_gmlp_w_s)
    sg, sd, smn, svn = _small_sum_adamw(parts, _pack_small(*sp), _pack_small(*sm), _pack_small(*sv), "small_adamw")
    names = ("ln1_g", "ln1_b", "gmlp_ln_g", "gmlp_ln_b", "ln2_g", "ln2_b", "ln3_g", "ln3_b", "b_gates", "gmlp_b_s",
             "gmlp_w_s")
    for dst, packed in ((g_out, sg), (d_out, sd), (m_out, smn), (v_out, svn)):
        for nm, val in zip(names, _unpack_small(packed)):
            dst[nm] = val

    order = ("f1g", "f1u", "f1d", "ln1_g", "ln1_b", "w_in", "b_gates", "gmlp_ln_g", "gmlp_ln_b", "gmlp_w_s", "gmlp_b_s",
             "w_ab", "w_gb", "w_out", "ln2_g", "ln2_b", "f2g", "f2u", "f2d", "ln3_g", "ln3_b")
    outs = [loss, dx[None]]
    for dst in (g_out, d_out, m_out, v_out):
        outs += [dst[k] for k in order]
    return tuple(outs)
```

```python
import functools
import math

import jax
import jax.numpy as jnp
from jax import lax
from jax.experimental import pallas as pl
from jax.experimental.pallas import tpu as pltpu

F32 = jnp.float32
BF16 = jnp.bfloat16

S = 2048
D = 1024
NSH = 4
FSH = 704
ATT_W = 1536
GRP_W = 512
NG = 3
NH = 8
DH = 64
BLK = 128
NBLK = S // BLK
GW = 1024
IN_W = 8704
IN_SH = IN_W // NSH
ALPHA = 2.0 ** 0.25
LN_EPS = 1e-5
ROPE_THETA = 10000.0
DILATIONS = (1, 4, 16)
ADAM_LR, ADAM_B1, ADAM_B2, ADAM_EPS, ADAM_WD, ADAM_STEP = 0.001, 0.9, 0.999, 1e-08, 0.01, 10
SMALL_ROWS = 144
MESH_T = pl.DeviceIdType.MESH
MIB = 1024 * 1024
NEG_INF = float("-inf")


def _cp(sem, vmem_mib=48):
    return pltpu.CompilerParams(dimension_semantics=sem, vmem_limit_bytes=vmem_mib * MIB)


def _ln_stats(r):
    mu = jnp.mean(r, axis=-1, keepdims=True)
    xc = r - mu
    var = jnp.mean(xc * xc, axis=-1, keepdims=True)
    rstd = lax.rsqrt(var + LN_EPS)
    return xc * rstd, rstd


def _ln_dx(dxh, xh, rstd):
    m1 = jnp.mean(dxh, axis=-1, keepdims=True)
    m2 = jnp.mean(dxh * xh, axis=-1, keepdims=True)
    return rstd * (dxh - m1 - xh * m2)


def _dot_nt(a, b):
    return lax.dot_general(a, b, (((1,), (1,)), ((), ())), preferred_element_type=F32)


def _dot_tn(a, b):
    return lax.dot_general(a, b, (((0,), (0,)), ((), ())), preferred_element_type=F32)


def _dot(a, b):
    return jnp.dot(a, b, preferred_element_type=F32)


def _ffn_fwd(xin, wgt, wut, wd, ln_g, ln_b, name, emit_t=False):
    tm = 512

    def body(x_ref, wg_ref, wu_ref, wd_ref, g_ref, b_ref, *rest):
        if emit_t:
            h_ref, hb_ref, xh_ref, rstd_ref, a_ref, bb_ref, ht_ref, acc_ref = rest
        else:
            h_ref, hb_ref, xh_ref, rstd_ref, a_ref, bb_ref, acc_ref = rest
        j = pl.program_id(1)
        xb = x_ref[...].astype(BF16)
        a = _dot_nt(xb, wg_ref[...])
        b = _dot_nt(xb, wu_ref[...])
        a_ref[...] = a
        bb_ref[...] = b
        s = (a * jax.nn.sigmoid(a)) * b
        f = _dot(s.astype(BF16), wd_ref[...])

        @pl.when(j == 0)
        def _():
            acc_ref[...] = f

        @pl.when(j > 0)
        def _():
            acc_ref[...] += f

        @pl.when(j == NSH - 1)
        def _():
            r = ALPHA * x_ref[...] + 0.5 * acc_ref[...]
            xh, rstd = _ln_stats(r)
            h = xh * g_ref[...] + b_ref[...]
            h_ref[...] = h
            hb_ref[...] = h.astype(BF16)
            xh_ref[...] = xh
            rstd_ref[...] = rstd
            if emit_t:
                ht_ref[...] = h.T.astype(BF16)

    row = pl.BlockSpec((tm, D), lambda i, j: (i, 0))
    vec = pl.BlockSpec((1, D), lambda i, j: (0, 0))
    wsp = pl.BlockSpec((None, FSH, D), lambda i, j: (j, 0, 0))
    ab = pl.BlockSpec((None, tm, FSH), lambda i, j: (j, i, 0))
    out_specs = [row, row, row, pl.BlockSpec((tm, 1), lambda i, j: (i, 0)), ab, ab]
    out_shape = [jax.ShapeDtypeStruct((S, D), F32), jax.ShapeDtypeStruct((S, D), BF16),
                 jax.ShapeDtypeStruct((S, D), F32), jax.ShapeDtypeStruct((S, 1), F32),
                 jax.ShapeDtypeStruct((NSH, S, FSH), F32), jax.ShapeDtypeStruct((NSH, S, FSH), F32)]
    if emit_t:
        out_specs.append(pl.BlockSpec((D, tm), lambda i, j: (0, i)))
        out_shape.append(jax.ShapeDtypeStruct((D, S), BF16))
    return pl.pallas_call(
        body, name=name, grid=(S // tm, NSH),
        in_specs=[row, wsp, wsp, wsp, vec, vec], out_specs=out_specs, out_shape=out_shape,
        scratch_shapes=[pltpu.VMEM((tm, D), F32)],
        compiler_params=_cp(("parallel", "arbitrary")),
    )(xin, wgt, wut, wd, ln_g, ln_b)


def _ffn_bwd(dr, xin_b, a, b, wgt, wut, wd, name):
    tm = 256
    ni = S // tm
    hr = FSH // 2

    def body(dr_ref, x_ref, a_ref, b_ref, wg_ref, wu_ref, wd_ref, dwg_ref, dwu_ref, dwd_ref, dx_hbm, dx_acc, sem):
        j = pl.program_id(0)
        i = pl.program_id(1)
        drv = dr_ref[...]
        df = (0.5 * drv).astype(BF16)
        ds = _dot_nt(df, wd_ref[...])
        av = a_ref[...]
        bv = b_ref[...]
        sig = jax.nn.sigmoid(av)
        sl = av * sig
        da = (ds * bv * (sig * (1.0 + av * (1.0 - sig)))).astype(BF16)
        db = (ds * sl).astype(BF16)
        sv = (sl * bv).astype(BF16)
        dx = _dot(da, wg_ref[...]) + _dot(db, wu_ref[...])
        rows = pl.ds(pl.multiple_of(i * tm, tm), tm)

        @pl.when(j == 0)
        def _():
            dx_acc[rows, :] = ALPHA * drv + dx

        @pl.when(j > 0)
        def _():
            dx_acc[rows, :] += dx

        xb = x_ref[...].astype(BF16)
        for o_ref, lhs, rhs in ((dwg_ref, da, xb), (dwu_ref, db, xb), (dwd_ref, sv, df)):
            p = _dot_tn(lhs, rhs)

            @pl.when(i == 0)
            def _(o_ref=o_ref, p=p):
                o_ref[0] = p[0:hr]
                o_ref[1] = p[hr:FSH]

            @pl.when(i > 0)
            def _(o_ref=o_ref, p=p):
                o_ref[0] += p[0:hr]
                o_ref[1] += p[hr:FSH]

        @pl.when(jnp.logical_and(j == NSH - 1, i == ni - 1))
        def _():
            cp = pltpu.make_async_copy(dx_acc, dx_hbm, sem)
            cp.start()
            cp.wait()

    row = pl.BlockSpec((tm, D), lambda j, i: (i, 0))
    wsp = pl.BlockSpec((None, FSH, D), lambda j, i: (j, 0, 0))
    ab = pl.BlockSpec((None, tm, FSH), lambda j, i: (j, i, 0))
    dw = pl.BlockSpec((2, None, hr, D), lambda j, i: (0, j, 0, 0))
    dwshape = jax.ShapeDtypeStruct((2, NSH, hr, D), F32)
    return pl.pallas_call(
        body, name=name, grid=(NSH, ni),
        in_specs=[row, row, ab, ab, wsp, wsp, wsp],
        out_specs=[dw, dw, dw, pl.BlockSpec(memory_space=pl.ANY)],
        out_shape=[dwshape, dwshape, dwshape, jax.ShapeDtypeStruct((S, D), F32)],
        scratch_shapes=[pltpu.VMEM((S, D), F32), pltpu.SemaphoreType.DMA],
        compiler_params=_cp(("arbitrary", "arbitrary"), vmem_mib=60),
    )(dr, xin_b, a, b, wgt, wut, wd)


def _matmul(a, b, mode, name, *, n, tm=512, tn=512, tk=None, b_col0=0, add=None, add_scale=1.0, out_dtype=F32):
    m, ka = a.shape
    tk = ka if tk is None else tk
    nk = ka // tk
    assert m % tm == 0 and n % tn == 0 and ka % tk == 0 and b_col0 % tn == 0
    off = b_col0 // tn

    def body(*refs):
        if add is None:
            a_ref, b_ref, o_ref = refs[:3]
            add_ref = None
            rest = refs[3:]
        else:
            a_ref, b_ref, add_ref, o_ref = refs[:4]
            rest = refs[4:]
        k = pl.program_id(2)
        av = a_ref[...].astype(BF16)
        bv = b_ref[...].astype(BF16)
        p = _dot(av, bv) if mode == "nn" else _dot_nt(av, bv)

        def finish(acc):
            if add_ref is not None:
                acc = acc + add_scale * add_ref[...]
            o_ref[...] = acc.astype(out_dtype)

        if nk == 1:
            finish(p)
        else:
            acc_ref = rest[0]

            @pl.when(k == 0)
            def _():
                acc_ref[...] = p

            @pl.when(k > 0)
            def _():
                acc_ref[...] += p

            @pl.when(k == nk - 1)
            def _():
                finish(acc_ref[...])

    a_spec = pl.BlockSpec((tm, tk), lambda i, j, k: (i, k))
    if mode == "nn":
        b_spec = pl.BlockSpec((tk, tn), lambda i, j, k: (k, j + off))
    else:
        b_spec = pl.BlockSpec((tn, tk), lambda i, j, k: (j, k))
    o_spec = pl.BlockSpec((tm, tn), lambda i, j, k: (i, j))
    in_specs = [a_spec, b_spec] + ([o_spec] if add is not None else [])
    args = (a, b) + ((add,) if add is not None else ())
    return pl.pallas_call(
        body, name=name, grid=(m // tm, n // tn, nk),
        in_specs=in_specs, out_specs=o_spec,
        out_shape=jax.ShapeDtypeStruct((m, n), out_dtype),
        scratch_shapes=[pltpu.VMEM((tm, tn), F32)] if nk > 1 else [],
        compiler_params=_cp(("parallel", "parallel", "arbitrary")),
    )(*args)


def _wgrad(xt, y, rh, c, name, row_sharded):
    tk = 512
    nk = S // tk

    def body(x_ref, y_ref, o_ref, acc_ref):
        k = pl.program_id(2)
        p = _dot(x_ref[...], y_ref[...].astype(BF16))

        @pl.when(k == 0)
        def _():
            acc_ref[...] = p

        @pl.when(k > 0)
        def _():
            acc_ref[...] += p

        @pl.when(k == nk - 1)
        def _():
            o_ref[...] = acc_ref[...]

    if row_sharded:
        x_spec = pl.BlockSpec((rh, tk), lambda h, j, k: (2 * j + h, k))
        y_spec = pl.BlockSpec((tk, c), lambda h, j, k: (k, 0))
    else:
        x_spec = pl.BlockSpec((rh, tk), lambda h, j, k: (h, k))
        y_spec = pl.BlockSpec((tk, c), lambda h, j, k: (k, j))
    return pl.pallas_call(
        body, name=name, grid=(2, NSH, nk),
        in_specs=[x_spec, y_spec],
        out_specs=pl.BlockSpec((None, None, rh, c), lambda h, j, k: (h, j, 0, 0)),
        out_shape=jax.ShapeDtypeStruct((2, NSH, rh, c), F32),
        scratch_shapes=[pltpu.VMEM((rh, c), F32)],
        compiler_params=_cp(("parallel", "parallel", "arbitrary")),
    )(xt, y)


def _resid_ln(res, f, ln_g, ln_b, name):
    tm = 256

    def body(res_ref, f_ref, g_ref, b_ref, h_ref, hb_ref, xh_ref, rstd_ref):
        r = ALPHA * res_ref[...] + f_ref[...]
        xh, rstd = _ln_stats(r)
        h = xh * g_ref[...] + b_ref[...]
        h_ref[...] = h
        hb_ref[...] = h.astype(BF16)
        xh_ref[...] = xh
        rstd_ref[...] = rstd

    row = pl.BlockSpec((tm, D), lambda i: (i, 0))
    vec = pl.BlockSpec((1, D), lambda i: (0, 0))
    return pl.pallas_call(
        body, name=name, grid=(S // tm,),
        in_specs=[row, row, vec, vec],
        out_specs=[row, row, row, pl.BlockSpec((tm, 1), lambda i: (i, 0))],
        out_shape=[jax.ShapeDtypeStruct((S, D), F32), jax.ShapeDtypeStruct((S, D), BF16),
                   jax.ShapeDtypeStruct((S, D), F32), jax.ShapeDtypeStruct((S, 1), F32)],
        compiler_params=_cp(("parallel",)),
    )(res, f, ln_g, ln_b)


def _ln_bwd(dout, xh, rstd, ln_g, name, target=None):
    tm = 256
    with_loss = target is not None

    def body(*refs):
        if with_loss:
            y_ref, t_ref, xh_ref, rstd_ref, g_ref, dr_ref, dg_ref, db_ref, loss_ref = refs
            err = y_ref[...] - t_ref[...]
            dy = err * (1.0 / D)
        else:
            y_ref, xh_ref, rstd_ref, g_ref, dr_ref, dg_ref, db_ref = refs
            dy = y_ref[...]
        i = pl.program_id(0)
        xh = xh_ref[...]
        dr_ref[...] = _ln_dx(dy * g_ref[...], xh, rstd_ref[...])
        dg = jnp.sum(dy * xh, axis=0, keepdims=True)
        db = jnp.sum(dy, axis=0, keepdims=True)

        @pl.when(i == 0)
        def _():
            dg_ref[...] = dg
            db_ref[...] = db

        @pl.when(i > 0)
        def _():
            dg_ref[...] += dg
            db_ref[...] += db

        if with_loss:
            part = 0.5 * jnp.sum(jnp.mean(err * err, axis=-1, keepdims=True), axis=0, keepdims=True)
            part = jnp.broadcast_to(part, (8, 128))

            @pl.when(i == 0)
            def _():
                loss_ref[...] = part

            @pl.when(i > 0)
            def _():
                loss_ref[...] += part

    row = pl.BlockSpec((tm, D), lambda i: (i, 0))
    vec = pl.BlockSpec((1, D), lambda i: (0, 0))
    col = pl.BlockSpec((tm, 1), lambda i: (i, 0))
    in_specs = [row] + ([row] if with_loss else []) + [row, col, vec]
    out_specs = [row, vec, vec] + ([pl.BlockSpec((8, 128), lambda i: (0, 0))] if with_loss else [])
    out_shape = [jax.ShapeDtypeStruct((S, D), F32), jax.ShapeDtypeStruct((1, D), F32),
                 jax.ShapeDtypeStruct((1, D), F32)] + ([jax.ShapeDtypeStruct((8, 128), F32)] if with_loss else [])
    args = (dout,) + ((target,) if with_loss else ()) + (xh, rstd, ln_g)
    return pl.pallas_call(
        body, name=name, grid=(S // tm,), in_specs=in_specs, out_specs=out_specs, out_shape=out_shape,
        compiler_params=_cp(("arbitrary",)),
    )(*args)


def _rope(ts, pos_f, invf, sign, name):
    tm = 128
    nch = ATT_W // 128
    nin = len(ts)

    def body(*refs):
        t_refs, (pos_ref, invf_ref, o_ref) = refs[:nin], refs[nin:]
        ang = pos_ref[...] * invf_ref[...]
        lane = lax.broadcasted_iota(jnp.int32, (tm, 128), 1)
        first = (lane % DH) < (DH // 2)
        cosf = jnp.cos(ang)
        sinv = jnp.sin(ang) * sign
        sinf = jnp.where(first, -sinv, sinv)
        for sec in range(3):
            t_ref, base = (t_refs[0], sec * ATT_W) if nin == 1 else (t_refs[sec], 0)
            for ch in range(nch):
                x = t_ref[:, base + ch * 128: base + (ch + 1) * 128]
                if sec < 2:
                    sw = jnp.where(first, pltpu.roll(x, 96, 1), pltpu.roll(x, 32, 1))
                    x = x * cosf + sw * sinf
                o_ref[:, sec * ATT_W + ch * 128: sec * ATT_W + (ch + 1) * 128] = x.astype(BF16)

    t_specs = [pl.BlockSpec((tm, t.shape[1]), lambda i: (i, 0)) for t in ts]
    return pl.pallas_call(
        body, name=name, grid=(S // tm,),
        in_specs=t_specs + [pl.BlockSpec((tm, 1), lambda i: (i, 0)), pl.BlockSpec((1, 128), lambda i: (0, 0))],
        out_specs=pl.BlockSpec((tm, 3 * ATT_W), lambda i: (i, 0)),
        out_shape=jax.ShapeDtypeStruct((S, 3 * ATT_W), BF16),
        compiler_params=_cp(("parallel",)),
    )(*ts, pos_f, invf)


def _class_view(t, d):
    return t.reshape(S // d, d * t.shape[1])


def _attn_fwd(gi, qkvb, name):
    d = DILATIONS[gi]
    nblk = S // d // BLK
    nsec = 3 * NG

    def body(*refs):
        if nblk > 1:
            q_ref, kc_ref, kp_ref, vc_ref, vp_ref, o_ref, lse_ref = refs
            has_prev = pl.program_id(1) != 0
        else:
            q_ref, kc_ref, vc_ref, o_ref, lse_ref = refs
        qi = lax.broadcasted_iota(jnp.int32, (BLK, BLK), 0)
        kj = lax.broadcasted_iota(jnp.int32, (BLK, BLK), 1)
        mask_c = kj <= qi
        if nblk > 1:
            mask_p = jnp.logical_and(kj >= qi, has_prev)
        for h in range(NH):
            sl = slice(h * DH, (h + 1) * DH)
            q = q_ref[:, sl]
            sc = jnp.where(mask_c, _dot_nt(q, kc_ref[:, sl]) * 0.125, NEG_INF)
            m = jnp.max(sc, axis=-1, keepdims=True)
            if nblk > 1:
                sp = jnp.where(mask_p, _dot_nt(q, kp_ref[:, sl]) * 0.125, NEG_INF)
                m = jnp.maximum(m, jnp.max(sp, axis=-1, keepdims=True))
            pc = jnp.exp(sc - m)
            l = jnp.sum(pc, axis=-1, keepdims=True)
            o = _dot(pc.astype(BF16), vc_ref[:, sl])
            if nblk > 1:
                pp = jnp.exp(sp - m)
                l = l + jnp.sum(pp, axis=-1, keepdims=True)
                o = o + _dot(pp.astype(BF16), vp_ref[:, sl])
            o_ref[:, sl] = o / l
            lse_ref[:, sl] = jnp.broadcast_to(m + jnp.log(l), (BLK, DH))

    def cur(sec):
        return pl.BlockSpec((BLK, GRP_W), lambda r, n: (n, r * nsec + sec * NG + gi))

    def prev(sec):
        return pl.BlockSpec((BLK, GRP_W), lambda r, n: (jnp.maximum(n - 1, 0), r * nsec + sec * NG + gi))

    out = pl.BlockSpec((BLK, GRP_W), lambda r, n: (n, r))
    shp = jax.ShapeDtypeStruct((S // d, d * GRP_W), F32)
    qv = _class_view(qkvb, d)
    if nblk > 1:
        in_specs, args = [cur(0), cur(1), prev(1), cur(2), prev(2)], (qv, qv, qv, qv, qv)
    else:
        in_specs, args = [cur(0), cur(1), cur(2)], (qv, qv, qv)
    o, lse = pl.pallas_call(
        body, name=name, grid=(d, nblk), in_specs=in_specs, out_specs=[out, out], out_shape=[shp, shp],
        compiler_params=_cp(("parallel", "parallel")),
    )(*args)
    return o.reshape(S, GRP_W), lse.reshape(S, GRP_W)


def _attn_combine(os, lses, name):
    tm = 256

    def body(o0_ref, o1_ref, o2_ref, l0_ref, l1_ref, l2_ref, y_ref, yt_ref, l_ref):
        l0, l1, l2 = l0_ref[...], l1_ref[...], l2_ref[...]
        m = jnp.maximum(jnp.maximum(l0, l1), l2)
        e0, e1, e2 = jnp.exp(l0 - m), jnp.exp(l1 - m), jnp.exp(l2 - m)
        den = e0 + e1 + e2
        y = (e0 * o0_ref[...] + e1 * o1_ref[...] + e2 * o2_ref[...]) / den
        y_ref[...] = y
        yt_ref[...] = y.T.astype(BF16)
        l_ref[...] = m + jnp.log(den)

    blk = pl.BlockSpec((tm, GRP_W), lambda i: (i, 0))
    shp = jax.ShapeDtypeStruct((S, GRP_W), F32)
    return pl.pallas_call(
        body, name=name, grid=(S // tm,), in_specs=[blk] * 6,
        out_specs=[blk, pl.BlockSpec((GRP_W, tm), lambda i: (0, i)), blk],
        out_shape=[shp, jax.ShapeDtypeStruct((GRP_W, S), BF16), shp],
        compiler_params=_cp(("parallel",)),
    )(*os, *lses)


def _attn_bwd(gi, qkvb, dy, y, lse, grads, name):
    d = DILATIONS[gi]
    nblk = S // d // BLK
    nsec = 3 * NG
    nal = 0 if grads is None else 3

    def body(*refs):
        refs = refs[nal:]
        if nblk > 1:
            (q_ref, qn_ref, k_ref, kp_ref, v_ref, vp_ref, dy_ref, dyn_ref, y_ref, yn_ref, l_ref, ln_ref,
             dq_ref, dk_ref, dv_ref) = refs
            n = pl.program_id(1)
            has_prev = n != 0
            has_next = n != nblk - 1
        else:
            q_ref, k_ref, v_ref, dy_ref, y_ref, l_ref, dq_ref, dk_ref, dv_ref = refs
        qi = lax.broadcasted_iota(jnp.int32, (BLK, BLK), 0)
        kj = lax.broadcasted_iota(jnp.int32, (BLK, BLK), 1)
        mask_c = kj <= qi
        if nblk > 1:
            mask_p = jnp.logical_and(kj >= qi, has_prev)
            mask_n = jnp.logical_and(kj >= qi, has_next)
        for h in range(NH):
            sl = slice(h * DH, (h + 1) * DH)
            q, k, v = q_ref[:, sl], k_ref[:, sl], v_ref[:, sl]
            dy_h = dy_ref[:, sl]
            dd = jnp.sum(dy_h * y_ref[:, sl], axis=-1, keepdims=True)
            lcol = l_ref[:, h * DH:h * DH + 1]
            dyb = dy_h.astype(BF16)
            p = jnp.exp(jnp.where(mask_c, _dot_nt(q, k) * 0.125, NEG_INF) - lcol)
            ds = (p * (_dot_nt(dyb, v) - dd)).astype(BF16)
            dq = _dot(ds, k)
            dk = _dot_tn(ds, q)
            dv = _dot_tn(p.astype(BF16), dyb)
            if nblk > 1:
                qn, kpv, vpv = qn_ref[:, sl], kp_ref[:, sl], vp_ref[:, sl]
                dyn = dyn_ref[:, sl]
                ddn = jnp.sum(dyn * yn_ref[:, sl], axis=-1, keepdims=True)
                lncol = ln_ref[:, h * DH:h * DH + 1]
                dynb = dyn.astype(BF16)
                pp = jnp.exp(jnp.where(mask_p, _dot_nt(q, kpv) * 0.125, NEG_INF) - lcol)
                dsp = (pp * (_dot_nt(dyb, vpv) - dd)).astype(BF16)
                dq = dq + _dot(dsp, kpv)
                pn = jnp.exp(jnp.where(mask_n, _dot_nt(qn, k) * 0.125, NEG_INF) - lncol)
                dsn = (pn * (_dot_nt(dynb, v) - ddn)).astype(BF16)
                dk = dk + _dot_tn(dsn, qn)
                dv = dv + _dot_tn(pn.astype(BF16), dynb)
            dq_ref[:, sl] = dq * 0.125
            dk_ref[:, sl] = dk * 0.125
            dv_ref[:, sl] = dv

    def spec(col, shift):
        def idx(r, n):
            return (jnp.clip(n + shift, 0, nblk - 1), col(r))
        return pl.BlockSpec((BLK, GRP_W), idx)

    def qkv_col(sec):
        return lambda r: r * nsec + sec * NG + gi

    tok = lambda r: r
    qv, dyv, yv, lv = (_class_view(t, d) for t in (qkvb, dy, y, lse))
    if nblk > 1:
        in_specs = [spec(qkv_col(0), 0), spec(qkv_col(0), 1), spec(qkv_col(1), 0), spec(qkv_col(1), -1),
                    spec(qkv_col(2), 0), spec(qkv_col(2), -1), spec(tok, 0), spec(tok, 1), spec(tok, 0), spec(tok, 1),
                    spec(tok, 0), spec(tok, 1)]
        args = (qv, qv, qv, qv, qv, qv, dyv, dyv, yv, yv, lv, lv)
    else:
        in_specs = [spec(qkv_col(0), 0), spec(qkv_col(1), 0), spec(qkv_col(2), 0), spec(tok, 0), spec(tok, 0),
                    spec(tok, 0)]
        args = (qv, qv, qv, dyv, yv, lv)
    out = pl.BlockSpec((BLK, GRP_W), lambda r, n: (n, r * NG + gi))
    shp = jax.ShapeDtypeStruct((S // d, d * ATT_W), F32)
    aliased = () if grads is None else tuple(_class_view(g, d) for g in grads)
    res = pl.pallas_call(
        body, name=name, grid=(d, nblk),
        in_specs=[ANY] * nal + in_specs, out_specs=[out, out, out], out_shape=[shp, shp, shp],
        input_output_aliases={i: i for i in range(nal)},
        compiler_params=_cp(("parallel", "parallel")),
    )(*aliased, *args)
    return tuple(g.reshape(S, ATT_W) for g in res)


_SQRT_HALF = 0.7071067811865476
_INV_SQRT_2PI = 0.3989422804014327


def _gelu(z):
    return 0.5 * z * (1.0 + lax.erf(z * _SQRT_HALF))


def _gelu_grad(z):
    return 0.5 * (1.0 + lax.erf(z * _SQRT_HALF)) + z * (jnp.exp(-0.5 * z * z) * _INV_SQRT_2PI)


def _tril_mask():
    t = lax.broadcasted_iota(jnp.int32, (BLK, BLK), 0)
    s = lax.broadcasted_iota(jnp.int32, (BLK, BLK), 1)
    return s <= t


def _gmlp_fwd(z, ln_g, ln_b, w_s, b_s_t, name):
    def body(z_ref, g_ref, b_ref, ws_ref, bs_ref, y_ref, yt_ref):
        zg = _gelu(z_ref[...])
        u = zg[:, :GW]
        xh, _ = _ln_stats(zg[:, GW:])
        vn = (xh * g_ref[...] + b_ref[...]).astype(BF16)
        tril = _tril_mask()
        for gg in range(8):
            sl = slice(gg * BLK, (gg + 1) * BLK)
            wt = jnp.where(tril, ws_ref[gg], 0.0).astype(BF16)
            mixed = _dot(wt, vn[:, sl]) + bs_ref[:, gg:gg + 1]
            yv = u[:, sl] * mixed
            y_ref[:, sl] = yv.astype(BF16)
            yt_ref[sl, :] = yv.T.astype(BF16)

    vec = pl.BlockSpec((1, GW), lambda n: (0, 0))
    return pl.pallas_call(
        body, name=name, grid=(NBLK,),
        in_specs=[pl.BlockSpec((BLK, 2 * GW), lambda n: (n, 0)), vec, vec,
                  pl.BlockSpec((8, BLK, BLK), lambda n: (0, 0, 0)), pl.BlockSpec((BLK, 8), lambda n: (0, 0))],
        out_specs=[pl.BlockSpec((BLK, GW), lambda n: (n, 0)), pl.BlockSpec((GW, BLK), lambda n: (0, n))],
        out_shape=[jax.ShapeDtypeStruct((S, GW), BF16), jax.ShapeDtypeStruct((GW, S), BF16)],
        compiler_params=_cp(("parallel",)),
    )(z, ln_g, ln_b, w_s, b_s_t)


def _gmlp_bwd(z, dy, ln_g, ln_b, w_s, b_s_t, name):
    def body(z_ref, dy_ref, g_ref, b_ref, ws_ref, bs_ref, dz_ref, dws_ref, dbs_ref, dg_ref, db_ref, dvn_ref):
        n = pl.program_id(0)
        zv = z_ref[...]
        zg = _gelu(zv)
        u = zg[:, :GW]
        xh, rstd = _ln_stats(zg[:, GW:])
        vn = (xh * g_ref[...] + b_ref[...]).astype(BF16)
        tril = _tril_mask()

        @pl.when(n == 0)
        def _():
            dws_ref[...] = jnp.zeros_like(dws_ref)
            dbs_ref[...] = jnp.zeros_like(dbs_ref)
            dg_ref[...] = jnp.zeros_like(dg_ref)
            db_ref[...] = jnp.zeros_like(db_ref)

        for gg in range(8):
            sl = slice(gg * BLK, (gg + 1) * BLK)
            wt = jnp.where(tril, ws_ref[gg], 0.0).astype(BF16)
            dyg = dy_ref[:, sl]
            mixed = _dot(wt, vn[:, sl]) + bs_ref[:, gg:gg + 1]
            dz_ref[:, sl] = (dyg * mixed * _gelu_grad(zv[:, sl])).astype(BF16)
            dmix = dyg * u[:, sl]
            dmb = dmix.astype(BF16)
            dws_ref[gg] += jnp.where(tril, _dot_nt(dmb, vn[:, sl]), 0.0)
            dbs_ref[:, gg:gg + 1] += jnp.sum(dmix, axis=-1, keepdims=True)
            dvn_ref[:, sl] = _dot_tn(wt, dmb)

        dvn = dvn_ref[...]
        dg_ref[...] += jnp.sum(dvn * xh, axis=0, keepdims=True)
        db_ref[...] += jnp.sum(dvn, axis=0, keepdims=True)
        dvg = _ln_dx(dvn * g_ref[...], xh, rstd)
        dz_ref[:, GW:] = (dvg * _gelu_grad(zv[:, GW:])).astype(BF16)

    vec = pl.BlockSpec((1, GW), lambda n: (0, 0))
    ws = pl.BlockSpec((8, BLK, BLK), lambda n: (0, 0, 0))
    bs = pl.BlockSpec((BLK, 8), lambda n: (0, 0))
    return pl.pallas_call(
        body, name=name, grid=(NBLK,),
        in_specs=[pl.BlockSpec((BLK, 2 * GW), lambda n: (n, 0)), pl.BlockSpec((BLK, GW), lambda n: (n, 0)),
                  vec, vec, ws, bs],
        out_specs=[pl.BlockSpec((BLK, 2 * GW), lambda n: (n, 0)), ws, bs, vec, vec],
        out_shape=[jax.ShapeDtypeStruct((S, 2 * GW), BF16), jax.ShapeDtypeStruct((8, BLK, BLK), F32),
                   jax.ShapeDtypeStruct((BLK, 8), F32), jax.ShapeDtypeStruct((1, GW), F32),
                   jax.ShapeDtypeStruct((1, GW), F32)],
        scratch_shapes=[pltpu.VMEM((BLK, GW), F32)],
        compiler_params=_cp(("arbitrary",)),
    )(z, dy, ln_g, ln_b, w_s, b_s_t)


def _merge_fwd(a, b, gl, b_gates, name):
    tm = 256

    def body(a_ref, b_ref, g0_ref, g1_ref, bg_ref, o_ref, ot_ref):
        g0 = jax.nn.sigmoid(g0_ref[...] + bg_ref[:, :D])
        g1 = jax.nn.sigmoid(g1_ref[...] + bg_ref[:, D:])
        mg = g0 * a_ref[...] + g1 * b_ref[...]
        o_ref[...] = mg.astype(BF16)
        ot_ref[...] = mg.T.astype(BF16)

    row = pl.BlockSpec((tm, D), lambda i: (i, 0))
    return pl.pallas_call(
        body, name=name, grid=(S // tm,),
        in_specs=[row, row, row, pl.BlockSpec((tm, D), lambda i: (i, 1)), pl.BlockSpec((1, 2 * D), lambda i: (0, 0))],
        out_specs=[row, pl.BlockSpec((D, tm), lambda i: (0, i))],
        out_shape=[jax.ShapeDtypeStruct((S, D), BF16), jax.ShapeDtypeStruct((D, S), BF16)],
        compiler_params=_cp(("parallel",)),
    )(a, b, gl, gl, b_gates)


def _merge_bwd(dm, a, b, gl, b_gates, name):
    tm = 256

    def body(dm_ref, a_ref, b_ref, g0_ref, g1_ref, bg_ref, da_ref, db_ref, dgl_ref, dbg_ref):
        i = pl.program_id(0)
        dmv = dm_ref[...]
        g0 = jax.nn.sigmoid(g0_ref[...] + bg_ref[:, :D])
        g1 = jax.nn.sigmoid(g1_ref[...] + bg_ref[:, D:])
        da_ref[...] = (dmv * g0).astype(BF16)
        db_ref[...] = (dmv * g1).astype(BF16)
        d0 = dmv * a_ref[...] * g0 * (1.0 - g0)
        d1 = dmv * b_ref[...] * g1 * (1.0 - g1)
        dgl_ref[:, :D] = d0.astype(BF16)
        dgl_ref[:, D:] = d1.astype(BF16)
        s0 = jnp.sum(d0, axis=0, keepdims=True)
        s1 = jnp.sum(d1, axis=0, keepdims=True)

        @pl.when(i == 0)
        def _():
            dbg_ref[:, :D] = s0
            dbg_ref[:, D:] = s1

        @pl.when(i > 0)
        def _():
            dbg_ref[:, :D] += s0
            dbg_ref[:, D:] += s1

    row = pl.BlockSpec((tm, D), lambda i: (i, 0))
    wide = pl.BlockSpec((tm, 2 * D), lambda i: (i, 0))
    bg = pl.BlockSpec((1, 2 * D), lambda i: (0, 0))
    return pl.pallas_call(
        body, name=name, grid=(S // tm,),
        in_specs=[row, row, row, row, pl.BlockSpec((tm, D), lambda i: (i, 1)), bg],
        out_specs=[row, row, wide, bg],
        out_shape=[jax.ShapeDtypeStruct((S, D), BF16), jax.ShapeDtypeStruct((S, D), BF16),
                   jax.ShapeDtypeStruct((S, 2 * D), BF16), jax.ShapeDtypeStruct((1, 2 * D), F32)],
        compiler_params=_cp(("arbitrary",)),
    )(dm, a, b, gl, gl, b_gates)


def _adam_math(w, g, m, v):
    m2 = ADAM_B1 * m + (1.0 - ADAM_B1) * g
    v2 = ADAM_B2 * v + (1.0 - ADAM_B2) * (g * g)
    m_hat = m2 / (1.0 - ADAM_B1 ** ADAM_STEP)
    v_hat = v2 / (1.0 - ADAM_B2 ** ADAM_STEP)
    delta = -ADAM_LR * (m_hat / (jnp.sqrt(v_hat) + ADAM_EPS) + ADAM_WD * w)
    return delta, m2, v2


def _pick_rows(rows, cols, unit=16, budget=MIB):
    best = unit
    for t in range(unit, rows + 1, unit):
        if rows % t == 0 and t * cols * 4 <= budget:
            best = t
    assert rows % best == 0
    return best


def _adamw(w, g, m, v, name):
    r, c = w.shape
    tr = _pick_rows(r, c, unit=8)

    def body(w_ref, g_ref, m_ref, v_ref, go_ref, d_ref, mo_ref, vo_ref):
        gv = g_ref[...]
        delta, m2, v2 = _adam_math(w_ref[...], gv, m_ref[...], v_ref[...])
        go_ref[...] = gv
        d_ref[...] = delta
        mo_ref[...] = m2
        vo_ref[...] = v2

    blk = pl.BlockSpec((tr, c), lambda i: (i, 0))
    shp = jax.ShapeDtypeStruct((r, c), F32)
    return pl.pallas_call(
        body, name=name, grid=(r // tr,), in_specs=[blk] * 4, out_specs=[blk] * 4, out_shape=[shp] * 4,
        compiler_params=_cp(("parallel",)),
    )(w, g, m, v)


def _small_sum_adamw(parts, w, m, v, name):
    tr = 48

    def body(p_ref, w_ref, m_ref, v_ref, g_ref, d_ref, mo_ref, vo_ref):
        gv = p_ref[0]
        for k in range(1, 8):
            gv = gv + p_ref[k]
        delta, m2, v2 = _adam_math(w_ref[...], gv, m_ref[...], v_ref[...])
        g_ref[...] = gv
        d_ref[...] = delta
        mo_ref[...] = m2
        vo_ref[...] = v2

    blk = pl.BlockSpec((tr, D), lambda i: (i, 0))
    shp = jax.ShapeDtypeStruct((SMALL_ROWS, D), F32)
    return pl.pallas_call(
        body, name=name, grid=(SMALL_ROWS // tr,),
        in_specs=[pl.BlockSpec((8, tr, D), lambda i: (0, i, 0)), blk, blk, blk],
        out_specs=[blk] * 4, out_shape=[shp] * 4,
        compiler_params=_cp(("parallel",)),
    )(parts, w, m, v)


ANY = pl.BlockSpec(memory_space=pl.ANY)


def _mesh_pos():
    x, y, c = lax.axis_index("x"), lax.axis_index("y"), lax.axis_index("c")
    chips = [(1 - x, y), (x, 1 - y), (1 - x, 1 - y)]
    return x, y, c, chips


def _place_shard(w, kind, pos, name):
    r, c = w.shape
    tr = _pick_rows(r, c)

    def body(pos_ref, w_ref, o_ref):
        o_ref[...] = w_ref[...].astype(BF16)

    if kind == "stack":
        o_spec = pl.BlockSpec((None, tr, c), lambda i, p: (p[1], i, 0))
        shape = (NSH, r, c)
    else:
        o_spec = pl.BlockSpec((tr, c), lambda i, p: (i, p[1]))
        shape = (r, NSH * c)
    return pl.pallas_call(
        body, name=name,
        grid_spec=pltpu.PrefetchScalarGridSpec(
            num_scalar_prefetch=1, grid=(r // tr,),
            in_specs=[pl.BlockSpec((tr, c), lambda i, p: (i, 0))], out_specs=o_spec),
        out_shape=jax.ShapeDtypeStruct(shape, BF16),
        compiler_params=_cp(("parallel",)),
    )(pos, w)


def _gather_weights(fulls, kinds, dims):
    n = len(fulls)

    def window(ref, kind, j, h, r, c):
        rows = pl.ds(pl.multiple_of(h * (r // 2), 16), r // 2)
        if kind == "stack":
            return ref.at[j, rows, :]
        return ref.at[rows, pl.ds(pl.multiple_of(j * c, 128), c)]

    def body(*refs):
        outs = refs[n:2 * n]
        send_sems, recv_sems = refs[2 * n:]
        x, y, c, chips = _mesh_pos()
        me = 2 * x + y
        sib = (x, y, 1 - c)

        def rc(a, k, j, h, to):
            r, cc = dims[a]
            win = window(outs[a], kinds[a], j, h, r, cc)
            return pltpu.make_async_remote_copy(
                src_ref=win, dst_ref=win, send_sem=send_sems.at[a * 6 + k], recv_sem=recv_sems.at[a * 6 + k],
                device_id=to, device_id_type=MESH_T)

        first, passed = [], []
        for a in range(n):
            for k, chip in enumerate(chips):
                cp = rc(a, k, me, c, (chip[0], chip[1], c))
                cp.start()
                first.append(cp)
        for a in range(n):
            for k, chip in enumerate(chips):
                j = 2 * chip[0] + chip[1]
                rc(a, k, j, c, sib).wait_recv()
                fw = rc(a, 3 + k, j, c, sib)
                fw.start()
                passed.append(fw)
        for a in range(n):
            for k, chip in enumerate(chips):
                j = 2 * chip[0] + chip[1]
                rc(a, 3 + k, j, 1 - c, sib).wait_recv()
        for cp in first + passed:
            cp.wait_send()

    return pl.pallas_call(
        body, name="gather_weights", in_specs=[ANY] * n, out_specs=[ANY] * n,
        out_shape=[jax.ShapeDtypeStruct(f.shape, BF16) for f in fulls],
        input_output_aliases={i: i for i in range(n)},
        scratch_shapes=[pltpu.SemaphoreType.DMA((6 * n,)), pltpu.SemaphoreType.DMA((6 * n,))],
    )(*fulls)


def _pair_exchange(grads):
    n = len(grads)

    def body(*refs):
        ins, outs = refs[:n], refs[n:2 * n]
        send_sems, recv_sems = refs[2 * n:]
        x, y, c, _ = _mesh_pos()
        cps = []
        for a in range(n):
            cp = pltpu.make_async_remote_copy(
                src_ref=ins[a].at[1 - c], dst_ref=outs[a], send_sem=send_sems.at[a], recv_sem=recv_sems.at[a],
                device_id=(x, y, 1 - c), device_id_type=MESH_T)
            cp.start()
            cps.append(cp)
        for cp in cps:
            cp.wait()

    return pl.pallas_call(
        body, name="rs_pair_exchange", in_specs=[ANY] * n, out_specs=[ANY] * n,
        out_shape=[jax.ShapeDtypeStruct(g.shape[1:], F32) for g in grads],
        scratch_shapes=[pltpu.SemaphoreType.DMA((n,)), pltpu.SemaphoreType.DMA((n,))],
    )(*grads)


def _pair_sum(g, recv, pos, name):
    _, _, rh, c = g.shape
    tr = _pick_rows(rh, c)

    def body(pos_ref, g_ref, r_ref, o_ref):
        o_ref[...] = (g_ref[...] + r_ref[...]).astype(BF16)

    return pl.pallas_call(
        body, name=name,
        grid_spec=pltpu.PrefetchScalarGridSpec(
            num_scalar_prefetch=1, grid=(NSH, rh // tr),
            in_specs=[pl.BlockSpec((None, None, tr, c), lambda j, r, p: (p[0], j, r, 0)),
                      pl.BlockSpec((None, tr, c), lambda j, r, p: (j, r, 0))],
            out_specs=pl.BlockSpec((None, tr, c), lambda j, r, p: (j, r, 0))),
        out_shape=jax.ShapeDtypeStruct((NSH, rh, c), BF16),
        compiler_params=_cp(("parallel", "parallel")),
    )(pos, g, recv)


def _chip_exchange(psums):
    n = len(psums)

    def body(*refs):
        ins, outs = refs[:n], refs[n:2 * n]
        send_sems, recv_sems = refs[2 * n:]
        x, y, c, chips = _mesh_pos()
        cps = []
        for a in range(n):
            for k, chip in enumerate(chips):
                j = 2 * chip[0] + chip[1]
                cp = pltpu.make_async_remote_copy(
                    src_ref=ins[a].at[j], dst_ref=outs[a].at[k],
                    send_sem=send_sems.at[a * 3 + k], recv_sem=recv_sems.at[a * 3 + k],
                    device_id=(chip[0], chip[1], c), device_id_type=MESH_T)
                cp.start()
                cps.append(cp)
        for cp in cps:
            cp.wait()

    return pl.pallas_call(
        body, name="rs_chip_exchange", in_specs=[ANY] * n, out_specs=[ANY] * n,
        out_shape=[jax.ShapeDtypeStruct((3,) + p.shape[1:], BF16) for p in psums],
        scratch_shapes=[pltpu.SemaphoreType.DMA((3 * n,)), pltpu.SemaphoreType.DMA((3 * n,))],
    )(*psums)


def _owner_sum(g, recv_a, recv_b, pos, name):
    _, _, rh, c = g.shape
    tr = _pick_rows(rh, c)

    def body(pos_ref, g_ref, ra_ref, rb_ref, o_ref):
        acc = g_ref[...] + ra_ref[...]
        for k in range(3):
            acc = acc + rb_ref[k].astype(F32)
        o_ref[...] = acc

    return pl.pallas_call(
        body, name=name,
        grid_spec=pltpu.PrefetchScalarGridSpec(
            num_scalar_prefetch=1, grid=(rh // tr,),
            in_specs=[pl.BlockSpec((None, None, tr, c), lambda r, p: (p[0], p[1], r, 0)),
                      pl.BlockSpec((None, tr, c), lambda r, p: (p[1], r, 0)),
                      pl.BlockSpec((3, tr, c), lambda r, p: (0, r, 0))],
            out_specs=pl.BlockSpec((None, tr, c), lambda r, p: (p[0], r, 0))),
        out_shape=jax.ShapeDtypeStruct((2, rh, c), F32),
        compiler_params=_cp(("parallel",)),
    )(pos, g, recv_a, recv_b)


def _sibling_allgather(halves):
    n = len(halves)

    def body(*refs):
        outs = refs[n:2 * n]
        send_sems, recv_sems = refs[2 * n:]
        x, y, c, _ = _mesh_pos()
        cps = []
        for a in range(n):
            cp = pltpu.make_async_remote_copy(
                src_ref=outs[a].at[c], dst_ref=outs[a].at[c], send_sem=send_sems.at[a], recv_sem=recv_sems.at[a],
                device_id=(x, y, 1 - c), device_id_type=MESH_T)
            cp.start()
            cps.append(cp)
        for a in range(n):
            cps[a].wait_send()
            pltpu.make_async_remote_copy(
                src_ref=outs[a].at[1 - c], dst_ref=outs[a].at[1 - c], send_sem=send_sems.at[a],
                recv_sem=recv_sems.at[a], device_id=(x, y, 1 - c), device_id_type=MESH_T).wait_recv()

    return pl.pallas_call(
        body, name="rs_sibling_allgather", in_specs=[ANY] * n, out_specs=[ANY] * n,
        out_shape=[jax.ShapeDtypeStruct(h.shape, F32) for h in halves],
        input_output_aliases={i: i for i in range(n)},
        scratch_shapes=[pltpu.SemaphoreType.DMA((n,)), pltpu.SemaphoreType.DMA((n,))],
    )(*halves)


def _small_allgather(part):
    m_per = SMALL_ROWS

    def body(x_ref, out_ref, send_sems, recv_sems, local_sem):
        x, y, c, chips = _mesh_pos()
        me, sibling = (x, y, c), (x, y, 1 - c)

        def rows(px, py, pc):
            return out_ref.at[pl.ds((4 * px + 2 * py + pc) * m_per, m_per), :]

        def copy(k, block, to, src=None):
            return pltpu.make_async_remote_copy(
                src_ref=rows(*block) if src is None else src, dst_ref=rows(*block),
                send_sem=send_sems.at[k], recv_sem=recv_sems.at[k], device_id=to, device_id_type=MESH_T)

        mine = pltpu.make_async_copy(x_ref, rows(*me), local_sem)
        mine.start()
        first = [copy(0, me, sibling, src=x_ref)]
        first += [copy(1 + j, me, (*chip, c), src=x_ref) for j, chip in enumerate(chips)]
        for cp in first:
            cp.start()
        passed = [copy(4 + j, (*chip, c), sibling) for j, chip in enumerate(chips)]
        for j, chip in enumerate(chips):
            copy(1 + j, (*chip, c), me).wait_recv()
            passed[j].start()
        copy(0, sibling, me).wait_recv()
        for j, chip in enumerate(chips):
            copy(4 + j, (*chip, 1 - c), me).wait_recv()
        for cp in first + passed:
            cp.wait_send()
        mine.wait()

    return pl.pallas_call(
        body, name="small_allgather",
        out_shape=jax.ShapeDtypeStruct((8 * m_per, D), F32),
        in_specs=[pl.BlockSpec(memory_space=pltpu.VMEM)], out_specs=pl.BlockSpec(memory_space=pltpu.VMEM),
        scratch_shapes=[pltpu.SemaphoreType.DMA((7,)), pltpu.SemaphoreType.DMA((7,)), pltpu.SemaphoreType.DMA],
    )(part)


def _pack_small(ln1_g, ln1_b, gln_g, gln_b, ln2_g, ln2_b, ln3_g, ln3_b, b_gates, b_s, w_s):
    rows = [ln1_g, ln1_b, gln_g, gln_b, ln2_g, ln2_b, ln3_g, ln3_b]
    rows = [r.reshape(1, D) for r in rows] + [b_gates.reshape(2, D), b_s.reshape(1, D), jnp.zeros((5, D), F32),
                                             w_s.reshape(128, D)]
    return jnp.concatenate(rows, axis=0)


def _unpack_small(p):
    out = [p[i:i + 1] for i in range(8)]
    return out + [p[8:10].reshape(1, 2 * D), p[10:11].reshape(1, 8, BLK), p[16:144].reshape(1, 8, BLK, BLK)]


def _local_step(x, pos_f, target, W, P):
    invf = ROPE_THETA ** (-jnp.arange(0, DH, 2, dtype=F32) / DH)
    invf = jnp.tile(invf, 4).reshape(1, 128)
    b_s_t = P["gmlp_b_s"].T

    h1, h1b, xh1, rstd1, a1, b1, h1t = _ffn_fwd(x, W["f1g"], W["f1u"], W["f1d"], P["ln1_g"], P["ln1_b"], "ffn1_fwd",
                                                emit_t=True)
    qkv = _matmul(h1b, W["w_in"], "nn", "proj_qkv", n=3 * ATT_W, b_col0=0)
    z = _matmul(h1b, W["w_in"], "nn", "proj_z", n=2 * GW, b_col0=3 * ATT_W)
    gl = _matmul(h1b, W["w_in"], "nn", "proj_gates", n=2 * D, b_col0=3 * ATT_W + 2 * GW)
    qkvb = _rope([qkv], pos_f, invf, 1.0, "rope_fwd")
    og = [_attn_fwd(gi, qkvb, "attn_fwd_g%d" % gi) for gi in range(NG)]
    y_attn, y_attn_t, lse = _attn_combine([o for o, _ in og], [l for _, l in og], "attn_combine")
    y_gmlp, y_gmlp_t = _gmlp_fwd(z, P["gmlp_ln_g"], P["gmlp_ln_b"], P["gmlp_w_s"], b_s_t, "gmlp_fwd")
    br_a = _matmul(y_attn, W["w_ab"], "nn", "branch_attn", n=D)
    br_b = _matmul(y_gmlp, W["w_gb"], "nn", "branch_gmlp", n=D)
    merged, merged_t = _merge_fwd(br_a, br_b, gl, P["b_gates"], "merge_fwd")
    mix = _matmul(merged, W["w_out"], "nn", "mix_out", n=D)
    h2, h2b, xh2, rstd2 = _resid_ln(h1, mix, P["ln2_g"], P["ln2_b"], "resid_ln2")
    y, _, xh3, rstd3, a2, b2 = _ffn_fwd(h2, W["f2g"], W["f2u"], W["f2d"], P["ln3_g"], P["ln3_b"], "ffn2_fwd")

    dr3, dg3, db3, loss = _ln_bwd(y, xh3, rstd3, P["ln3_g"], "loss_ln3_bwd", target=target)
    g_f2g, g_f2u, g_f2d, dh2 = _ffn_bwd(dr3, h2b, a2, b2, W["f2g"], W["f2u"], W["f2d"], "ffn2_bwd")
    dr2, dg2, db2 = _ln_bwd(dh2, xh2, rstd2, P["ln2_g"], "ln2_bwd")
    g_wout = _wgrad(merged_t, dr2, 128, D, "dw_out", row_sharded=True)
    dmerged = _matmul(dr2, W["w_out"], "nt", "dmerged", n=D)
    dab, dbb, dglb, dbg = _merge_bwd(dmerged, br_a, br_b, gl, P["b_gates"], "merge_bwd")
    g_wab = _wgrad(y_attn_t, dab, GRP_W // 2, 256, "dw_attn_branch", row_sharded=False)
    g_wgb = _wgrad(y_gmlp_t, dbb, 128, D, "dw_gmlp_branch", row_sharded=True)
    dy_attn = _matmul(dab, W["w_ab"], "nt", "dy_attn", n=GRP_W)
    dy_gmlp = _matmul(dbb, W["w_gb"], "nt", "dy_gmlp", n=GW)
    dzb, dws, dbs_t, dgln_g, dgln_b = _gmlp_bwd(z, dy_gmlp, P["gmlp_ln_g"], P["gmlp_ln_b"], P["gmlp_w_s"], b_s_t,
                                                 "gmlp_bwd")
    dqkv = None
    for gi in range(NG):
        dqkv = _attn_bwd(gi, qkvb, dy_attn, y_attn, lse, dqkv, "attn_bwd_g%d" % gi)
    dqkvb = _rope(list(dqkv), pos_f, invf, -1.0, "rope_bwd")
    dproj = jnp.concatenate([dqkvb, dzb, dglb], axis=1)
    g_win = _wgrad(h1t, dproj, D // 2, IN_SH, "dw_in", row_sharded=False)
    dh1 = _matmul(dproj, W["w_in"], "nt", "dh1", n=D, tn=D, tk=IN_SH, add=dr2, add_scale=ALPHA)
    dr1, dg1, db1 = _ln_bwd(dh1, xh1, rstd1, P["ln1_g"], "ln1_bwd")
    g_f1g, g_f1u, g_f1d, dx = _ffn_bwd(dr1, x, a1, b1, W["f1g"], W["f1u"], W["f1d"], "ffn1_bwd")

    big = dict(f1g=g_f1g, f1u=g_f1u, f1d=g_f1d, w_in=g_win, w_ab=g_wab, w_gb=g_wgb, w_out=g_wout,
               f2g=g_f2g, f2u=g_f2u, f2d=g_f2d)
    small = _pack_small(dg1, db1, dgln_g, dgln_b, dg2, db2, dg3, db3, dbg, dbs_t.T, dws)
    return loss, dx, big, small


BIG = ("f1g", "f1u", "f1d", "w_in", "w_ab", "w_gb", "w_out", "f2g", "f2u", "f2d")
TRANSPOSED = ("f1g", "f1u", "f2g", "f2u")
KIND = dict(f1g="stack", f1u="stack", f1d="stack", w_in="col", w_ab="col", w_gb="stack", w_out="stack",
            f2g="stack", f2u="stack", f2d="stack")


def kernel(x, positions, ffn1_w_gate, ffn1_w_up, ffn1_w_down, ln1_g, ln1_b, w_in, b_gates, gmlp_ln_g, gmlp_ln_b, gmlp_w_s, gmlp_b_s, w_attn_branch, w_gmlp_branch, w_out, ln2_g, ln2_b, ffn2_w_gate, ffn2_w_up, ffn2_w_down, ln3_g, ln3_b, loss_target, m_ffn1_w_gate, m_ffn1_w_up, m_ffn1_w_down, m_ln1_g, m_ln1_b, m_w_in, m_b_gates, m_gmlp_ln_g, m_gmlp_ln_b, m_gmlp_w_s, m_gmlp_b_s, m_w_attn_branch, m_w_gmlp_branch, m_w_out, m_ln2_g, m_ln2_b, m_ffn2_w_gate, m_ffn2_w_up, m_ffn2_w_down, m_ln3_g, m_ln3_b, v_ffn1_w_gate, v_ffn1_w_up, v_ffn1_w_down, v_ln1_g, v_ln1_b, v_w_in, v_b_gates, v_gmlp_ln_g, v_gmlp_ln_b, v_gmlp_w_s, v_gmlp_b_s, v_w_attn_branch, v_w_gmlp_branch, v_w_out, v_ln2_g, v_ln2_b, v_ffn2_w_gate, v_ffn2_w_up, v_ffn2_w_down, v_ln3_g, v_ln3_b):
    cx, cy, cc = lax.axis_index("x"), lax.axis_index("y"), lax.axis_index("c")
    pos = jnp.stack([cc, 2 * cx + cy]).astype(jnp.int32)

    w_sh = dict(f1g=ffn1_w_gate, f1u=ffn1_w_up, f1d=ffn1_w_down, w_in=w_in, w_ab=w_attn_branch,
                w_gb=w_gmlp_branch, w_out=w_out, f2g=ffn2_w_gate, f2u=ffn2_w_up, f2d=ffn2_w_down)
    m_sh = dict(f1g=m_ffn1_w_gate, f1u=m_ffn1_w_up, f1d=m_ffn1_w_down, w_in=m_w_in, w_ab=m_w_attn_branch,
                w_gb=m_w_gmlp_branch, w_out=m_w_out, f2g=m_ffn2_w_gate, f2u=m_ffn2_w_up, f2d=m_ffn2_w_down)
    v_sh = dict(f1g=v_ffn1_w_gate, f1u=v_ffn1_w_up, f1d=v_ffn1_w_down, w_in=v_w_in, w_ab=v_w_attn_branch,
                w_gb=v_w_gmlp_branch, w_out=v_w_out, f2g=v_ffn2_w_gate, f2u=v_ffn2_w_up, f2d=v_ffn2_w_down)
    w_sh = {k: (v[0].T if k in TRANSPOSED else v[0]) for k, v in w_sh.items()}
    m_sh = {k: (v[0].T if k in TRANSPOSED else v[0]) for k, v in m_sh.items()}
    v_sh = {k: (v[0].T if k in TRANSPOSED else v[0]) for k, v in v_sh.items()}

    placed = [_place_shard(w_sh[k], KIND[k], pos, "place_" + k) for k in BIG]
    full = _gather_weights(placed, [KIND[k] for k in BIG], [w_sh[k].shape for k in BIG])
    W = dict(zip(BIG, full))
    W["w_gb"] = W["w_gb"].reshape(D, D)
    W["w_out"] = W["w_out"].reshape(D, D)
    P = dict(ln1_g=ln1_g, ln1_b=ln1_b, ln2_g=ln2_g, ln2_b=ln2_b, ln3_g=ln3_g, ln3_b=ln3_b, b_gates=b_gates,
             gmlp_ln_g=gmlp_ln_g, gmlp_ln_b=gmlp_ln_b, gmlp_w_s=gmlp_w_s[0], gmlp_b_s=gmlp_b_s[0])

    pos_f = positions.reshape(S, 1).astype(F32)
    loss_part, dx, big, small = _local_step(x[0], pos_f, loss_target[0], W, P)
    loss = lax.psum(loss_part[0, 0], ("x", "y", "c"))

    grads = [big[k] for k in BIG]
    recv_a = _pair_exchange(grads)
    psums = [_pair_sum(g, r, pos, "rs_pair_sum_" + k) for g, r, k in zip(grads, recv_a, BIG)]
    recv_b = _chip_exchange(psums)
    halves = [_owner_sum(g, ra, rb, pos, "rs_owner_sum_" + k) for g, ra, rb, k in zip(grads, recv_a, recv_b, BIG)]
    reduced = _sibling_allgather(halves)

    g_out, d_out, m_out, v_out = {}, {}, {}, {}
    for k, gfull in zip(BIG, reduced):
        shp = w_sh[k].shape
        res = _adamw(w_sh[k], gfull.reshape(shp), m_sh[k], v_sh[k], "adamw_" + k)
        if k in TRANSPOSED:
            res = [r.T for r in res]
        g_out[k], d_out[k], m_out[k], v_out[k] = [r[None] for r in res]

    parts = _small_allgather(small).reshape(8, SMALL_ROWS, D)
    sp = (ln1_g, ln1_b, gmlp_ln_g, gmlp_ln_b, ln2_g, ln2_b, ln3_g, ln3_b, b_gates, gmlp_b_s, gmlp_w_s)
    sm = (m_ln1_g, m_ln1_b, m_gmlp_ln_g, m_gmlp_ln_b, m_ln2_g, m_ln2_b, m_ln3_g, m_ln3_b, m_b_gates, m_gmlp_b_s,
          m_gmlp_w_s)
    sv = (v_ln1_g, v_ln1_b, v_gmlp_ln_g, v_gmlp_ln_b, v_ln2_g, v_ln2_b, v_ln3_g, v_ln3_b, v_b_gates, v_gmlp_b_s,
          v_gmlp_w_s)
    sg, sd, smn, svn = _small_sum_adamw(parts, _pack_small(*sp), _pack_small(*sm), _pack_small(*sv), "small_adamw")
    names = ("ln1_g", "ln1_b", "gmlp_ln_g", "gmlp_ln_b", "ln2_g", "ln2_b", "ln3_g", "ln3_b", "b_gates", "gmlp_b_s",
             "gmlp_w_s")
    for dst, packed in ((g_out, sg), (d_out, sd), (m_out, smn), (v_out, svn)):
        for nm, val in zip(names, _unpack_small(packed)):
            dst[nm] = val

    order = ("f1g", "f1u", "f1d", "ln1_g", "ln1_b", "w_in", "b_gates", "gmlp_ln_g", "gmlp_ln_b", "gmlp_w_s", "gmlp_b_s",
             "w_ab", "w_gb", "w_out", "ln2_g", "ln2_b", "f2g", "f2u", "f2d", "ln3_g", "ln3_b")
    outs = [loss, dx[None]]
    for dst in (g_out, d_out, m_out, v_out):
        outs += [dst[k] for k in order]
    return tuple(outs)
```

```python
import functools
import math

import jax
import jax.numpy as jnp
from jax import lax
from jax.experimental import pallas as pl
from jax.experimental.pallas import tpu as pltpu

F32 = jnp.float32
BF16 = jnp.bfloat16

S = 2048
D = 1024
NSH = 4
FSH = 704
ATT_W = 1536
GRP_W = 512
NG = 3
NH = 8
DH = 64
BLK = 128
NBLK = S // BLK
GW = 1024
IN_W = 8704
IN_SH = IN_W // NSH
ALPHA = 2.0 ** 0.25
LN_EPS = 1e-5
ROPE_THETA = 10000.0
DILATIONS = (1, 4, 16)
ADAM_LR, ADAM_B1, ADAM_B2, ADAM_EPS, ADAM_WD, ADAM_STEP = 0.001, 0.9, 0.999, 1e-08, 0.01, 10
SMALL_ROWS = 144
MESH_T = pl.DeviceIdType.MESH
MIB = 1024 * 1024
NEG_INF = float("-inf")


def _cp(sem, vmem_mib=48):
    return pltpu.CompilerParams(dimension_semantics=sem, vmem_limit_bytes=vmem_mib * MIB)


def _ln_stats(r):
    mu = jnp.mean(r, axis=-1, keepdims=True)
    xc = r - mu
    var = jnp.mean(xc * xc, axis=-1, keepdims=True)
    rstd = lax.rsqrt(var + LN_EPS)
    return xc * rstd, rstd


def _ln_dx(dxh, xh, rstd):
    m1 = jnp.mean(dxh, axis=-1, keepdims=True)
    m2 = jnp.mean(dxh * xh, axis=-1, keepdims=True)
    return rstd * (dxh - m1 - xh * m2)


def _dot_nt(a, b):
    return lax.dot_general(a, b, (((1,), (1,)), ((), ())), preferred_element_type=F32)


def _dot_tn(a, b):
    return lax.dot_general(a, b, (((0,), (0,)), ((), ())), preferred_element_type=F32)


def _dot(a, b):
    return jnp.dot(a, b, preferred_element_type=F32)


def _ffn_fwd(xin, wgt, wut, wd, ln_g, ln_b, name, emit_t=False):
    tm = 512

    def body(x_ref, wg_ref, wu_ref, wd_ref, g_ref, b_ref, *rest):
        if emit_t:
            h_ref, hb_ref, xh_ref, rstd_ref, a_ref, bb_ref, ht_ref, acc_ref = rest
        else:
            h_ref, hb_ref, xh_ref, rstd_ref, a_ref, bb_ref, acc_ref = rest
        j = pl.program_id(1)
        xb = x_ref[...].astype(BF16)
        a = _dot_nt(xb, wg_ref[...])
        b = _dot_nt(xb, wu_ref[...])
        a_ref[...] = a
        bb_ref[...] = b
        s = (a * jax.nn.sigmoid(a)) * b
        f = _dot(s.astype(BF16), wd_ref[...])

        @pl.when(j == 0)
        def _():
            acc_ref[...] = f

        @pl.when(j > 0)
        def _():
            acc_ref[...] += f

        @pl.when(j == NSH - 1)
        def _():
            r = ALPHA * x_ref[...] + 0.5 * acc_ref[...]
            xh, rstd = _ln_stats(r)
            h = xh * g_ref[...] + b_ref[...]
            h_ref[...] = h
            hb_ref[...] = h.astype(BF16)
            xh_ref[...] = xh
            rstd_ref[...] = rstd
            if emit_t:
                ht_ref[...] = h.T.astype(BF16)

    row = pl.BlockSpec((tm, D), lambda i, j: (i, 0))
    vec = pl.BlockSpec((1, D), lambda i, j: (0, 0))
    wsp = pl.BlockSpec((None, FSH, D), lambda i, j: (j, 0, 0))
    ab = pl.BlockSpec((None, tm, FSH), lambda i, j: (j, i, 0))
    out_specs = [row, row, row, pl.BlockSpec((tm, 1), lambda i, j: (i, 0)), ab, ab]
    out_shape = [jax.ShapeDtypeStruct((S, D), F32), jax.ShapeDtypeStruct((S, D), BF16),
                 jax.ShapeDtypeStruct((S, D), F32), jax.ShapeDtypeStruct((S, 1), F32),
                 jax.ShapeDtypeStruct((NSH, S, FSH), F32), jax.ShapeDtypeStruct((NSH, S, FSH), F32)]
    if emit_t:
        out_specs.append(pl.BlockSpec((D, tm), lambda i, j: (0, i)))
        out_shape.append(jax.ShapeDtypeStruct((D, S), BF16))
    return pl.pallas_call(
        body, name=name, grid=(S // tm, NSH),
        in_specs=[row, wsp, wsp, wsp, vec, vec], out_specs=out_specs, out_shape=out_shape,
        scratch_shapes=[pltpu.VMEM((tm, D), F32)],
        compiler_params=_cp(("parallel", "arbitrary")),
    )(xin, wgt, wut, wd, ln_g, ln_b)


def _ffn_bwd(dr, xin_b, a, b, wgt, wut, wd, name):
    tm = 256
    ni = S // tm
    hr = FSH // 2

    def body(dr_ref, a_ref, b_ref, wg_ref, wu_ref, wd_ref, x_hbm, dwg_hbm, dwu_hbm, dwd_hbm, dx_hbm,
             dx_acc, da_all, db_all, s_all, df_all, x_all, res_buf, sems):
        j = pl.program_id(0)
        i = pl.program_id(1)
        rows = pl.ds(pl.multiple_of(i * tm, tm), tm)

        @pl.when(jnp.logical_and(j == 0, i == 0))
        def _():
            cp = pltpu.make_async_copy(x_hbm, x_all, sems.at[0])
            cp.start()
            cp.wait()

        drv = dr_ref[...]
        df = (0.5 * drv).astype(BF16)

        @pl.when(j == 0)
        def _():
            df_all[rows, :] = df

        ds = _dot_nt(df, wd_ref[...])
        av = a_ref[...]
        bv = b_ref[...]
        sig = jax.nn.sigmoid(av)
        sl = av * sig
        da = (ds * bv * (sig * (1.0 + av * (1.0 - sig)))).astype(BF16)
        db = (ds * sl).astype(BF16)
        da_all[rows, :] = da
        db_all[rows, :] = db
        s_all[rows, :] = (sl * bv).astype(BF16)
        dx = _dot(da, wg_ref[...]) + _dot(db, wu_ref[...])

        @pl.when(j == 0)
        def _():
            dx_acc[rows, :] = ALPHA * drv + dx

        @pl.when(j > 0)
        def _():
            dx_acc[rows, :] += dx

        @pl.when(i == ni - 1)
        def _():
            copies = []
            for n, (lhs, rhs, out) in enumerate(((da_all, x_all, dwg_hbm), (db_all, x_all, dwu_hbm),
                                                 (s_all, df_all, dwd_hbm))):
                slot = n % 2
                if n >= 2:
                    for cp in copies[2 * (n - 2): 2 * (n - 2) + 2]:
                        cp.wait()
                res_buf[slot] = _dot_tn(lhs[...], rhs[...])
                for h in range(2):
                    cp = pltpu.make_async_copy(res_buf.at[slot, pl.ds(h * hr, hr), :], out.at[h, j],
                                               sems.at[1 + 2 * slot + h])
                    cp.start()
                    copies.append(cp)
            for cp in copies[2:]:
                cp.wait()

        @pl.when(jnp.logical_and(j == NSH - 1, i == ni - 1))
        def _():
            cp = pltpu.make_async_copy(dx_acc, dx_hbm, sems.at[0])
            cp.start()
            cp.wait()

    row = pl.BlockSpec((tm, D), lambda j, i: (i, 0))
    wsp = pl.BlockSpec((None, FSH, D), lambda j, i: (j, 0, 0))
    ab = pl.BlockSpec((None, tm, FSH), lambda j, i: (j, i, 0))
    dwshape = jax.ShapeDtypeStruct((2, NSH, hr, D), F32)
    return pl.pallas_call(
        body, name=name, grid=(NSH, ni),
        in_specs=[row, ab, ab, wsp, wsp, wsp, ANY],
        out_specs=[ANY, ANY, ANY, ANY],
        out_shape=[dwshape, dwshape, dwshape, jax.ShapeDtypeStruct((S, D), F32)],
        scratch_shapes=[pltpu.VMEM((S, D), F32), pltpu.VMEM((S, FSH), BF16), pltpu.VMEM((S, FSH), BF16),
                        pltpu.VMEM((S, FSH), BF16), pltpu.VMEM((S, D), BF16), pltpu.VMEM((S, D), BF16),
                        pltpu.VMEM((2, FSH, D), F32), pltpu.SemaphoreType.DMA((5,))],
        compiler_params=_cp(("arbitrary", "arbitrary"), vmem_mib=58),
    )(dr, a, b, wgt, wut, wd, xin_b)


def _matmul(a, b, mode, name, *, n, tm=512, tn=512, tk=None, b_col0=0, add=None, add_scale=1.0, out_dtype=F32):
    m, ka = a.shape
    tk = ka if tk is None else tk
    nk = ka // tk
    assert m % tm == 0 and n % tn == 0 and ka % tk == 0 and b_col0 % tn == 0
    off = b_col0 // tn

    def body(*refs):
        if add is None:
            a_ref, b_ref, o_ref = refs[:3]
            add_ref = None
            rest = refs[3:]
        else:
            a_ref, b_ref, add_ref, o_ref = refs[:4]
            rest = refs[4:]
        k = pl.program_id(2)
        av = a_ref[...].astype(BF16)
        bv = b_ref[...].astype(BF16)
        p = _dot(av, bv) if mode == "nn" else _dot_nt(av, bv)

        def finish(acc):
            if add_ref is not None:
                acc = acc + add_scale * add_ref[...]
            o_ref[...] = acc.astype(out_dtype)

        if nk == 1:
            finish(p)
        else:
            acc_ref = rest[0]

            @pl.when(k == 0)
            def _():
                acc_ref[...] = p

            @pl.when(k > 0)
            def _():
                acc_ref[...] += p

            @pl.when(k == nk - 1)
            def _():
                finish(acc_ref[...])

    a_spec = pl.BlockSpec((tm, tk), lambda i, j, k: (i, k))
    if mode == "nn":
        b_spec = pl.BlockSpec((tk, tn), lambda i, j, k: (k, j + off))
    else:
        b_spec = pl.BlockSpec((tn, tk), lambda i, j, k: (j, k))
    o_spec = pl.BlockSpec((tm, tn), lambda i, j, k: (i, j))
    in_specs = [a_spec, b_spec] + ([o_spec] if add is not None else [])
    args = (a, b) + ((add,) if add is not None else ())
    return pl.pallas_call(
        body, name=name, grid=(m // tm, n // tn, nk),
        in_specs=in_specs, out_specs=o_spec,
        out_shape=jax.ShapeDtypeStruct((m, n), out_dtype),
        scratch_shapes=[pltpu.VMEM((tm, tn), F32)] if nk > 1 else [],
        compiler_params=_cp(("parallel", "parallel", "arbitrary")),
    )(*args)


def _wgrad(xt, y, rh, c, name, row_sharded):
    if row_sharded:
        def body(x_ref, y_ref, o_ref):
            res = _dot(x_ref[...], y_ref[...].astype(BF16))
            for j in range(NSH):
                for h in range(2):
                    o_ref[h, j] = res[(2 * j + h) * rh:(2 * j + h + 1) * rh, :]

        grid = (1,)
        in_specs = [pl.BlockSpec((2 * NSH * rh, S), lambda g: (0, 0)), pl.BlockSpec((S, c), lambda g: (0, 0))]
        out_specs = pl.BlockSpec((2, NSH, rh, c), lambda g: (0, 0, 0, 0))
        sem = ("arbitrary",)
    else:
        def body(x_ref, y_ref, o_ref):
            o_ref[...] = _dot(x_ref[...], y_ref[...].astype(BF16))

        grid = (2, NSH)
        in_specs = [pl.BlockSpec((rh, S), lambda h, j: (h, 0)), pl.BlockSpec((S, c), lambda h, j: (0, j))]
        out_specs = pl.BlockSpec((None, None, rh, c), lambda h, j: (h, j, 0, 0))
        sem = ("parallel", "parallel")
    return pl.pallas_call(
        body, name=name, grid=grid, in_specs=in_specs, out_specs=out_specs,
        out_shape=jax.ShapeDtypeStruct((2, NSH, rh, c), F32),
        compiler_params=_cp(sem, vmem_mib=56),
    )(xt, y)


def _resid_ln(res, f, ln_g, ln_b, name):
    tm = 256

    def body(res_ref, f_ref, g_ref, b_ref, h_ref, hb_ref, xh_ref, rstd_ref):
        r = ALPHA * res_ref[...] + f_ref[...]
        xh, rstd = _ln_stats(r)
        h = xh * g_ref[...] + b_ref[...]
        h_ref[...] = h
        hb_ref[...] = h.astype(BF16)
        xh_ref[...] = xh
        rstd_ref[...] = rstd

    row = pl.BlockSpec((tm, D), lambda i: (i, 0))
    vec = pl.BlockSpec((1, D), lambda i: (0, 0))
    return pl.pallas_call(
        body, name=name, grid=(S // tm,),
        in_specs=[row, row, vec, vec],
        out_specs=[row, row, row, pl.BlockSpec((tm, 1), lambda i: (i, 0))],
        out_shape=[jax.ShapeDtypeStruct((S, D), F32), jax.ShapeDtypeStruct((S, D), BF16),
                   jax.ShapeDtypeStruct((S, D), F32), jax.ShapeDtypeStruct((S, 1), F32)],
        compiler_params=_cp(("parallel",)),
    )(res, f, ln_g, ln_b)


def _ln_bwd(dout, xh, rstd, ln_g, name, target=None):
    tm = 256
    with_loss = target is not None

    def body(*refs):
        if with_loss:
            y_ref, t_ref, xh_ref, rstd_ref, g_ref, dr_ref, dg_ref, db_ref, loss_ref = refs
            err = y_ref[...] - t_ref[...]
            dy = err * (1.0 / D)
        else:
            y_ref, xh_ref, rstd_ref, g_ref, dr_ref, dg_ref, db_ref = refs
            dy = y_ref[...]
        i = pl.program_id(0)
        xh = xh_ref[...]
        dr_ref[...] = _ln_dx(dy * g_ref[...], xh, rstd_ref[...])
        dg = jnp.sum(dy * xh, axis=0, keepdims=True)
        db = jnp.sum(dy, axis=0, keepdims=True)

        @pl.when(i == 0)
        def _():
            dg_ref[...] = dg
            db_ref[...] = db

        @pl.when(i > 0)
        def _():
            dg_ref[...] += dg
            db_ref[...] += db

        if with_loss:
            part = 0.5 * jnp.sum(jnp.mean(err * err, axis=-1, keepdims=True), axis=0, keepdims=True)
            part = jnp.broadcast_to(part, (8, 128))

            @pl.when(i == 0)
            def _():
                loss_ref[...] = part

            @pl.when(i > 0)
            def _():
                loss_ref[...] += part

    row = pl.BlockSpec((tm, D), lambda i: (i, 0))
    vec = pl.BlockSpec((1, D), lambda i: (0, 0))
    col = pl.BlockSpec((tm, 1), lambda i: (i, 0))
    in_specs = [row] + ([row] if with_loss else []) + [row, col, vec]
    out_specs = [row, vec, vec] + ([pl.BlockSpec((8, 128), lambda i: (0, 0))] if with_loss else [])
    out_shape = [jax.ShapeDtypeStruct((S, D), F32), jax.ShapeDtypeStruct((1, D), F32),
                 jax.ShapeDtypeStruct((1, D), F32)] + ([jax.ShapeDtypeStruct((8, 128), F32)] if with_loss else [])
    args = (dout,) + ((target,) if with_loss else ()) + (xh, rstd, ln_g)
    return pl.pallas_call(
        body, name=name, grid=(S // tm,), in_specs=in_specs, out_specs=out_specs, out_shape=out_shape,
        compiler_params=_cp(("arbitrary",)),
    )(*args)


def _rope(ts, pos_f, invf, sign, name):
    tm = 128
    nch = ATT_W // 128
    nin = len(ts)

    def body(*refs):
        t_refs, (pos_ref, invf_ref, o_ref) = refs[:nin], refs[nin:]
        ang = pos_ref[...] * invf_ref[...]
        lane = lax.broadcasted_iota(jnp.int32, (tm, 128), 1)
        first = (lane % DH) < (DH // 2)
        cosf = jnp.cos(ang)
        sinv = jnp.sin(ang) * sign
        sinf = jnp.where(first, -sinv, sinv)
        for sec in range(3):
            t_ref, base = (t_refs[0], sec * ATT_W) if nin == 1 else (t_refs[sec], 0)
            for ch in range(nch):
                x = t_ref[:, base + ch * 128: base + (ch + 1) * 128]
                if sec < 2:
                    sw = jnp.where(first, pltpu.roll(x, 96, 1), pltpu.roll(x, 32, 1))
                    x = x * cosf + sw * sinf
                o_ref[:, sec * ATT_W + ch * 128: sec * ATT_W + (ch + 1) * 128] = x.astype(BF16)

    t_specs = [pl.BlockSpec((tm, t.shape[1]), lambda i: (i, 0)) for t in ts]
    return pl.pallas_call(
        body, name=name, grid=(S // tm,),
        in_specs=t_specs + [pl.BlockSpec((tm, 1), lambda i: (i, 0)), pl.BlockSpec((1, 128), lambda i: (0, 0))],
        out_specs=pl.BlockSpec((tm, 3 * ATT_W), lambda i: (i, 0)),
        out_shape=jax.ShapeDtypeStruct((S, 3 * ATT_W), BF16),
        compiler_params=_cp(("parallel",)),
    )(*ts, pos_f, invf)


def _class_view(t, d):
    return t.reshape(S // d, d * t.shape[1])


def _attn_fwd(gi, qkvb, name):
    d = DILATIONS[gi]
    nblk = S // d // BLK
    nsec = 3 * NG

    def body(*refs):
        if nblk > 1:
            q_ref, kc_ref, kp_ref, vc_ref, vp_ref, o_ref, lse_ref = refs
            has_prev = pl.program_id(1) != 0
        else:
            q_ref, kc_ref, vc_ref, o_ref, lse_ref = refs
        qi = lax.broadcasted_iota(jnp.int32, (BLK, BLK), 0)
        kj = lax.broadcasted_iota(jnp.int32, (BLK, BLK), 1)
        mask_c = kj <= qi
        if nblk > 1:
            mask_p = jnp.logical_and(kj >= qi, has_prev)
        for h in range(NH):
            sl = slice(h * DH, (h + 1) * DH)
            q = q_ref[:, sl]
            sc = jnp.where(mask_c, _dot_nt(q, kc_ref[:, sl]) * 0.125, NEG_INF)
            m = jnp.max(sc, axis=-1, keepdims=True)
            if nblk > 1:
                sp = jnp.where(mask_p, _dot_nt(q, kp_ref[:, sl]) * 0.125, NEG_INF)
                m = jnp.maximum(m, jnp.max(sp, axis=-1, keepdims=True))
            pc = jnp.exp(sc - m)
            l = jnp.sum(pc, axis=-1, keepdims=True)
            o = _dot(pc.astype(BF16), vc_ref[:, sl])
            if nblk > 1:
                pp = jnp.exp(sp - m)
                l = l + jnp.sum(pp, axis=-1, keepdims=True)
                o = o + _dot(pp.astype(BF16), vp_ref[:, sl])
            o_ref[:, sl] = o / l
            lse_ref[:, sl] = jnp.broadcast_to(m + jnp.log(l), (BLK, DH))

    def cur(sec):
        return pl.BlockSpec((BLK, GRP_W), lambda r, n: (n, r * nsec + sec * NG + gi))

    def prev(sec):
        return pl.BlockSpec((BLK, GRP_W), lambda r, n: (jnp.maximum(n - 1, 0), r * nsec + sec * NG + gi))

    out = pl.BlockSpec((BLK, GRP_W), lambda r, n: (n, r))
    shp = jax.ShapeDtypeStruct((S // d, d * GRP_W), F32)
    qv = _class_view(qkvb, d)
    if nblk > 1:
        in_specs, args = [cur(0), cur(1), prev(1), cur(2), prev(2)], (qv, qv, qv, qv, qv)
    else:
        in_specs, args = [cur(0), cur(1), cur(2)], (qv, qv, qv)
    o, lse = pl.pallas_call(
        body, name=name, grid=(d, nblk), in_specs=in_specs, out_specs=[out, out], out_shape=[shp, shp],
        compiler_params=_cp(("parallel", "parallel")),
    )(*args)
    return o.reshape(S, GRP_W), lse.reshape(S, GRP_W)


def _attn_combine(os, lses, name):
    tm = 256

    def body(o0_ref, o1_ref, o2_ref, l0_ref, l1_ref, l2_ref, y_ref, yt_ref, l_ref):
        l0, l1, l2 = l0_ref[...], l1_ref[...], l2_ref[...]
        m = jnp.maximum(jnp.maximum(l0, l1), l2)
        e0, e1, e2 = jnp.exp(l0 - m), jnp.exp(l1 - m), jnp.exp(l2 - m)
        den = e0 + e1 + e2
        y = (e0 * o0_ref[...] + e1 * o1_ref[...] + e2 * o2_ref[...]) / den
        y_ref[...] = y
        yt_ref[...] = y.T.astype(BF16)
        l_ref[...] = m + jnp.log(den)

    blk = pl.BlockSpec((tm, GRP_W), lambda i: (i, 0))
    shp = jax.ShapeDtypeStruct((S, GRP_W), F32)
    return pl.pallas_call(
        body, name=name, grid=(S // tm,), in_specs=[blk] * 6,
        out_specs=[blk, pl.BlockSpec((GRP_W, tm), lambda i: (0, i)), blk],
        out_shape=[shp, jax.ShapeDtypeStruct((GRP_W, S), BF16), shp],
        compiler_params=_cp(("parallel",)),
    )(*os, *lses)


def _attn_bwd(gi, qkvb, dy, y, lse, grads, name):
    d = DILATIONS[gi]
    nblk = S // d // BLK
    nsec = 3 * NG
    nal = 0 if grads is None else 3

    def body(*refs):
        refs = refs[nal:]
        if nblk > 1:
            (q_ref, qn_ref, k_ref, kp_ref, v_ref, vp_ref, dy_ref, dyn_ref, y_ref, yn_ref, l_ref, ln_ref,
             dq_ref, dk_ref, dv_ref) = refs
            n = pl.program_id(1)
            has_prev = n != 0
            has_next = n != nblk - 1
        else:
            q_ref, k_ref, v_ref, dy_ref, y_ref, l_ref, dq_ref, dk_ref, dv_ref = refs
        qi = lax.broadcasted_iota(jnp.int32, (BLK, BLK), 0)
        kj = lax.broadcasted_iota(jnp.int32, (BLK, BLK), 1)
        mask_c = kj <= qi
        if nblk > 1:
            mask_p = jnp.logical_and(kj >= qi, has_prev)
            mask_n = jnp.logical_and(kj >= qi, has_next)
        for h in range(NH):
            sl = slice(h * DH, (h + 1) * DH)
            q, k, v = q_ref[:, sl], k_ref[:, sl], v_ref[:, sl]
            dy_h = dy_ref[:, sl]
            dd = jnp.sum(dy_h * y_ref[:, sl], axis=-1, keepdims=True)
            lcol = l_ref[:, h * DH:h * DH + 1]
            dyb = dy_h.astype(BF16)
            p = jnp.exp(jnp.where(mask_c, _dot_nt(q, k) * 0.125, NEG_INF) - lcol)
            ds = (p * (_dot_nt(dyb, v) - dd)).astype(BF16)
            dq = _dot(ds, k)
            dk = _dot_tn(ds, q)
            dv = _dot_tn(p.astype(BF16), dyb)
            if nblk > 1:
                qn, kpv, vpv = qn_ref[:, sl], kp_ref[:, sl], vp_ref[:, sl]
                dyn = dyn_ref[:, sl]
                ddn = jnp.sum(dyn * yn_ref[:, sl], axis=-1, keepdims=True)
                lncol = ln_ref[:, h * DH:h * DH + 1]
                dynb = dyn.astype(BF16)
                pp = jnp.exp(jnp.where(mask_p, _dot_nt(q, kpv) * 0.125, NEG_INF) - lcol)
                dsp = (pp * (_dot_nt(dyb, vpv) - dd)).astype(BF16)
                dq = dq + _dot(dsp, kpv)
                pn = jnp.exp(jnp.where(mask_n, _dot_nt(qn, k) * 0.125, NEG_INF) - lncol)
                dsn = (pn * (_dot_nt(dynb, v) - ddn)).astype(BF16)
                dk = dk + _dot_tn(dsn, qn)
                dv = dv + _dot_tn(pn.astype(BF16), dynb)
            dq_ref[:, sl] = dq * 0.125
            dk_ref[:, sl] = dk * 0.125
            dv_ref[:, sl] = dv

    def spec(col, shift):
        def idx(r, n):
            return (jnp.clip(n + shift, 0, nblk - 1), col(r))
        return pl.BlockSpec((BLK, GRP_W), idx)

    def qkv_col(sec):
        return lambda r: r * nsec + sec * NG + gi

    tok = lambda r: r
    qv, dyv, yv, lv = (_class_view(t, d) for t in (qkvb, dy, y, lse))
    if nblk > 1:
        in_specs = [spec(qkv_col(0), 0), spec(qkv_col(0), 1), spec(qkv_col(1), 0), spec(qkv_col(1), -1),
                    spec(qkv_col(2), 0), spec(qkv_col(2), -1), spec(tok, 0), spec(tok, 1), spec(tok, 0), spec(tok, 1),
                    spec(tok, 0), spec(tok, 1)]
        args = (qv, qv, qv, qv, qv, qv, dyv, dyv, yv, yv, lv, lv)
    else:
        in_specs = [spec(qkv_col(0), 0), spec(qkv_col(1), 0), spec(qkv_col(2), 0), spec(tok, 0), spec(tok, 0),
                    spec(tok, 0)]
        args = (qv, qv, qv, dyv, yv, lv)
    out = pl.BlockSpec((BLK, GRP_W), lambda r, n: (n, r * NG + gi))
    shp = jax.ShapeDtypeStruct((S // d, d * ATT_W), F32)
    aliased = () if grads is None else tuple(_class_view(g, d) for g in grads)
    res = pl.pallas_call(
        body, name=name, grid=(d, nblk),
        in_specs=[ANY] * nal + in_specs, out_specs=[out, out, out], out_shape=[shp, shp, shp],
        input_output_aliases={i: i for i in range(nal)},
        compiler_params=_cp(("parallel", "parallel")),
    )(*aliased, *args)
    return tuple(g.reshape(S, ATT_W) for g in res)


_SQRT_HALF = 0.7071067811865476
_INV_SQRT_2PI = 0.3989422804014327


def _gelu(z):
    return 0.5 * z * (1.0 + lax.erf(z * _SQRT_HALF))


def _gelu_grad(z):
    return 0.5 * (1.0 + lax.erf(z * _SQRT_HALF)) + z * (jnp.exp(-0.5 * z * z) * _INV_SQRT_2PI)


def _tril_mask():
    t = lax.broadcasted_iota(jnp.int32, (BLK, BLK), 0)
    s = lax.broadcasted_iota(jnp.int32, (BLK, BLK), 1)
    return s <= t


def _gmlp_fwd(z, ln_g, ln_b, w_s, b_s_t, name):
    def body(z_ref, g_ref, b_ref, ws_ref, bs_ref, y_ref, yt_ref):
        zg = _gelu(z_ref[...])
        u = zg[:, :GW]
        xh, _ = _ln_stats(zg[:, GW:])
        vn = (xh * g_ref[...] + b_ref[...]).astype(BF16)
        tril = _tril_mask()
        for gg in range(8):
            sl = slice(gg * BLK, (gg + 1) * BLK)
            wt = jnp.where(tril, ws_ref[gg], 0.0).astype(BF16)
            mixed = _dot(wt, vn[:, sl]) + bs_ref[:, gg:gg + 1]
            yv = u[:, sl] * mixed
            y_ref[:, sl] = yv.astype(BF16)
            yt_ref[sl, :] = yv.T.astype(BF16)

    vec = pl.BlockSpec((1, GW), lambda n: (0, 0))
    return pl.pallas_call(
        body, name=name, grid=(NBLK,),
        in_specs=[pl.BlockSpec((BLK, 2 * GW), lambda n: (n, 0)), vec, vec,
                  pl.BlockSpec((8, BLK, BLK), lambda n: (0, 0, 0)), pl.BlockSpec((BLK, 8), lambda n: (0, 0))],
        out_specs=[pl.BlockSpec((BLK, GW), lambda n: (n, 0)), pl.BlockSpec((GW, BLK), lambda n: (0, n))],
        out_shape=[jax.ShapeDtypeStruct((S, GW), BF16), jax.ShapeDtypeStruct((GW, S), BF16)],
        compiler_params=_cp(("parallel",)),
    )(z, ln_g, ln_b, w_s, b_s_t)


def _gmlp_bwd(z, dy, ln_g, ln_b, w_s, b_s_t, name):
    def body(z_ref, dy_ref, g_ref, b_ref, ws_ref, bs_ref, dz_ref, dws_ref, dbs_ref, dg_ref, db_ref, dvn_ref):
        n = pl.program_id(0)
        zv = z_ref[...]
        zg = _gelu(zv)
        u = zg[:, :GW]
        xh, rstd = _ln_stats(zg[:, GW:])
        vn = (xh * g_ref[...] + b_ref[...]).astype(BF16)
        tril = _tril_mask()

        @pl.when(n == 0)
        def _():
            dws_ref[...] = jnp.zeros_like(dws_ref)
            dbs_ref[...] = jnp.zeros_like(dbs_ref)
            dg_ref[...] = jnp.zeros_like(dg_ref)
            db_ref[...] = jnp.zeros_like(db_ref)

        for gg in range(8):
            sl = slice(gg * BLK, (gg + 1) * BLK)
            wt = jnp.where(tril, ws_ref[gg], 0.0).astype(BF16)
            dyg = dy_ref[:, sl]
            mixed = _dot(wt, vn[:, sl]) + bs_ref[:, gg:gg + 1]
            dz_ref[:, sl] = (dyg * mixed * _gelu_grad(zv[:, sl])).astype(BF16)
            dmix = dyg * u[:, sl]
            dmb = dmix.astype(BF16)
            dws_ref[gg] += jnp.where(tril, _dot_nt(dmb, vn[:, sl]), 0.0)
            dbs_ref[:, gg:gg + 1] += jnp.sum(dmix, axis=-1, keepdims=True)
            dvn_ref[:, sl] = _dot_tn(wt, dmb)

        dvn = dvn_ref[...]
        dg_ref[...] += jnp.sum(dvn * xh, axis=0, keepdims=True)
        db_ref[...] += jnp.sum(dvn, axis=0, keepdims=True)
        dvg = _ln_dx(dvn * g_ref[...], xh, rstd)
        dz_ref[:, GW:] = (dvg * _gelu_grad(zv[:, GW:])).astype(BF16)

    vec = pl.BlockSpec((1, GW), lambda n: (0, 0))
    ws = pl.BlockSpec((8, BLK, BLK), lambda n: (0, 0, 0))
    bs = pl.BlockSpec((BLK, 8), lambda n: (0, 0))
    return pl.pallas_call(
        body, name=name, grid=(NBLK,),
        in_specs=[pl.BlockSpec((BLK, 2 * GW), lambda n: (n, 0)), pl.BlockSpec((BLK, GW), lambda n: (n, 0)),
                  vec, vec, ws, bs],
        out_specs=[pl.BlockSpec((BLK, 2 * GW), lambda n: (n, 0)), ws, bs, vec, vec],
        out_shape=[jax.ShapeDtypeStruct((S, 2 * GW), BF16), jax.ShapeDtypeStruct((8, BLK, BLK), F32),
                   jax.ShapeDtypeStruct((BLK, 8), F32), jax.ShapeDtypeStruct((1, GW), F32),
                   jax.ShapeDtypeStruct((1, GW), F32)],
        scratch_shapes=[pltpu.VMEM((BLK, GW), F32)],
        compiler_params=_cp(("arbitrary",)),
    )(z, dy, ln_g, ln_b, w_s, b_s_t)


def _merge_fwd(a, b, gl, b_gates, name):
    tm = 256

    def body(a_ref, b_ref, g0_ref, g1_ref, bg_ref, o_ref, ot_ref):
        g0 = jax.nn.sigmoid(g0_ref[...] + bg_ref[:, :D])
        g1 = jax.nn.sigmoid(g1_ref[...] + bg_ref[:, D:])
        mg = g0 * a_ref[...] + g1 * b_ref[...]
        o_ref[...] = mg.astype(BF16)
        ot_ref[...] = mg.T.astype(BF16)

    row = pl.BlockSpec((tm, D), lambda i: (i, 0))
    return pl.pallas_call(
        body, name=name, grid=(S // tm,),
        in_specs=[row, row, row, pl.BlockSpec((tm, D), lambda i: (i, 1)), pl.BlockSpec((1, 2 * D), lambda i: (0, 0))],
        out_specs=[row, pl.BlockSpec((D, tm), lambda i: (0, i))],
        out_shape=[jax.ShapeDtypeStruct((S, D), BF16), jax.ShapeDtypeStruct((D, S), BF16)],
        compiler_params=_cp(("parallel",)),
    )(a, b, gl, gl, b_gates)


def _merge_bwd(dm, a, b, gl, b_gates, name):
    tm = 256

    def body(dm_ref, a_ref, b_ref, g0_ref, g1_ref, bg_ref, da_ref, db_ref, dgl_ref, dbg_ref):
        i = pl.program_id(0)
        dmv = dm_ref[...]
        g0 = jax.nn.sigmoid(g0_ref[...] + bg_ref[:, :D])
        g1 = jax.nn.sigmoid(g1_ref[...] + bg_ref[:, D:])
        da_ref[...] = (dmv * g0).astype(BF16)
        db_ref[...] = (dmv * g1).astype(BF16)
        d0 = dmv * a_ref[...] * g0 * (1.0 - g0)
        d1 = dmv * b_ref[...] * g1 * (1.0 - g1)
        dgl_ref[:, :D] = d0.astype(BF16)
        dgl_ref[:, D:] = d1.astype(BF16)
        s0 = jnp.sum(d0, axis=0, keepdims=True)
        s1 = jnp.sum(d1, axis=0, keepdims=True)

        @pl.when(i == 0)
        def _():
            dbg_ref[:, :D] = s0
            dbg_ref[:, D:] = s1

        @pl.when(i > 0)
        def _():
            dbg_ref[:, :D] += s0
            dbg_ref[:, D:] += s1

    row = pl.BlockSpec((tm, D), lambda i: (i, 0))
    wide = pl.BlockSpec((tm, 2 * D), lambda i: (i, 0))
    bg = pl.BlockSpec((1, 2 * D), lambda i: (0, 0))
    return pl.pallas_call(
        body, name=name, grid=(S // tm,),
        in_specs=[row, row, row, row, pl.BlockSpec((tm, D), lambda i: (i, 1)), bg],
        out_specs=[row, row, wide, bg],
        out_shape=[jax.ShapeDtypeStruct((S, D), BF16), jax.ShapeDtypeStruct((S, D), BF16),
                   jax.ShapeDtypeStruct((S, 2 * D), BF16), jax.ShapeDtypeStruct((1, 2 * D), F32)],
        compiler_params=_cp(("arbitrary",)),
    )(dm, a, b, gl, gl, b_gates)


def _adam_math(w, g, m, v):
    m2 = ADAM_B1 * m + (1.0 - ADAM_B1) * g
    v2 = ADAM_B2 * v + (1.0 - ADAM_B2) * (g * g)
    m_hat = m2 / (1.0 - ADAM_B1 ** ADAM_STEP)
    v_hat = v2 / (1.0 - ADAM_B2 ** ADAM_STEP)
    delta = -ADAM_LR * (m_hat / (jnp.sqrt(v_hat) + ADAM_EPS) + ADAM_WD * w)
    return delta, m2, v2


def _pick_rows(rows, cols, unit=16, budget=MIB):
    best = unit
    for t in range(unit, rows + 1, unit):
        if rows % t == 0 and t * cols * 4 <= budget:
            best = t
    assert rows % best == 0
    return best


def _adamw(w, g, m, v, name):
    r, c = w.shape
    tr = _pick_rows(r, c, unit=8)

    def body(w_ref, g_ref, m_ref, v_ref, go_ref, d_ref, mo_ref, vo_ref):
        gv = g_ref[...]
        delta, m2, v2 = _adam_math(w_ref[...], gv, m_ref[...], v_ref[...])
        go_ref[...] = gv
        d_ref[...] = delta
        mo_ref[...] = m2
        vo_ref[...] = v2

    blk = pl.BlockSpec((tr, c), lambda i: (i, 0))
    shp = jax.ShapeDtypeStruct((r, c), F32)
    return pl.pallas_call(
        body, name=name, grid=(r // tr,), in_specs=[blk] * 4, out_specs=[blk] * 4, out_shape=[shp] * 4,
        compiler_params=_cp(("parallel",)),
    )(w, g, m, v)


def _small_sum_adamw(parts, w, m, v, name):
    tr = 48

    def body(p_ref, w_ref, m_ref, v_ref, g_ref, d_ref, mo_ref, vo_ref):
        gv = p_ref[0]
        for k in range(1, 8):
            gv = gv + p_ref[k]
        delta, m2, v2 = _adam_math(w_ref[...], gv, m_ref[...], v_ref[...])
        g_ref[...] = gv
        d_ref[...] = delta
        mo_ref[...] = m2
        vo_ref[...] = v2

    blk = pl.BlockSpec((tr, D), lambda i: (i, 0))
    shp = jax.ShapeDtypeStruct((SMALL_ROWS, D), F32)
    return pl.pallas_call(
        body, name=name, grid=(SMALL_ROWS // tr,),
        in_specs=[pl.BlockSpec((8, tr, D), lambda i: (0, i, 0)), blk, blk, blk],
        out_specs=[blk] * 4, out_shape=[shp] * 4,
        compiler_params=_cp(("parallel",)),
    )(parts, w, m, v)


ANY = pl.BlockSpec(memory_space=pl.ANY)


def _mesh_pos():
    x, y, c = lax.axis_index("x"), lax.axis_index("y"), lax.axis_index("c")
    chips = [(1 - x, y), (x, 1 - y), (1 - x, 1 - y)]
    return x, y, c, chips


def _place_shard(w, kind, pos, name):
    r, c = w.shape
    tr = _pick_rows(r, c)

    def body(pos_ref, w_ref, o_ref):
        o_ref[...] = w_ref[...].astype(BF16)

    if kind == "stack":
        o_spec = pl.BlockSpec((None, tr, c), lambda i, p: (p[1], i, 0))
        shape = (NSH, r, c)
    else:
        o_spec = pl.BlockSpec((tr, c), lambda i, p: (i, p[1]))
        shape = (r, NSH * c)
    return pl.pallas_call(
        body, name=name,
        grid_spec=pltpu.PrefetchScalarGridSpec(
            num_scalar_prefetch=1, grid=(r // tr,),
            in_specs=[pl.BlockSpec((tr, c), lambda i, p: (i, 0))], out_specs=o_spec),
        out_shape=jax.ShapeDtypeStruct(shape, BF16),
        compiler_params=_cp(("parallel",)),
    )(pos, w)


def _gather_weights(fulls, kinds, dims):
    n = len(fulls)

    def window(ref, kind, j, h, r, c):
        rows = pl.ds(pl.multiple_of(h * (r // 2), 16), r // 2)
        if kind == "stack":
            return ref.at[j, rows, :]
        return ref.at[rows, pl.ds(pl.multiple_of(j * c, 128), c)]

    def body(*refs):
        outs = refs[n:2 * n]
        send_sems, recv_sems = refs[2 * n:]
        x, y, c, chips = _mesh_pos()
        me = 2 * x + y
        sib = (x, y, 1 - c)

        def rc(a, k, j, h, to):
            r, cc = dims[a]
            win = window(outs[a], kinds[a], j, h, r, cc)
            return pltpu.make_async_remote_copy(
                src_ref=win, dst_ref=win, send_sem=send_sems.at[a * 6 + k], recv_sem=recv_sems.at[a * 6 + k],
                device_id=to, device_id_type=MESH_T)

        first, passed = [], []
        for a in range(n):
            for k, chip in enumerate(chips):
                cp = rc(a, k, me, c, (chip[0], chip[1], c))
                cp.start()
                first.append(cp)
        for a in range(n):
            for k, chip in enumerate(chips):
                j = 2 * chip[0] + chip[1]
                rc(a, k, j, c, sib).wait_recv()
                fw = rc(a, 3 + k, j, c, sib)
                fw.start()
                passed.append(fw)
        for a in range(n):
            for k, chip in enumerate(chips):
                j = 2 * chip[0] + chip[1]
                rc(a, 3 + k, j, 1 - c, sib).wait_recv()
        for cp in first + passed:
            cp.wait_send()

    return pl.pallas_call(
        body, name="gather_weights", in_specs=[ANY] * n, out_specs=[ANY] * n,
        out_shape=[jax.ShapeDtypeStruct(f.shape, BF16) for f in fulls],
        input_output_aliases={i: i for i in range(n)},
        scratch_shapes=[pltpu.SemaphoreType.DMA((6 * n,)), pltpu.SemaphoreType.DMA((6 * n,))],
    )(*fulls)


def _pair_exchange(grads):
    n = len(grads)

    def body(*refs):
        ins, outs = refs[:n], refs[n:2 * n]
        send_sems, recv_sems = refs[2 * n:]
        x, y, c, _ = _mesh_pos()
        cps = []
        for a in range(n):
            cp = pltpu.make_async_remote_copy(
                src_ref=ins[a].at[1 - c], dst_ref=outs[a], send_sem=send_sems.at[a], recv_sem=recv_sems.at[a],
                device_id=(x, y, 1 - c), device_id_type=MESH_T)
            cp.start()
            cps.append(cp)
        for cp in cps:
            cp.wait()

    return pl.pallas_call(
        body, name="rs_pair_exchange", in_specs=[ANY] * n, out_specs=[ANY] * n,
        out_shape=[jax.ShapeDtypeStruct(g.shape[1:], F32) for g in grads],
        scratch_shapes=[pltpu.SemaphoreType.DMA((n,)), pltpu.SemaphoreType.DMA((n,))],
    )(*grads)


def _pair_sum(g, recv, pos, name):
    _, _, rh, c = g.shape
    tr = _pick_rows(rh, c)

    def body(pos_ref, g_ref, r_ref, o_ref):
        o_ref[...] = (g_ref[...] + r_ref[...]).astype(BF16)

    return pl.pallas_call(
        body, name=name,
        grid_spec=pltpu.PrefetchScalarGridSpec(
            num_scalar_prefetch=1, grid=(NSH, rh // tr),
            in_specs=[pl.BlockSpec((None, None, tr, c), lambda j, r, p: (p[0], j, r, 0)),
                      pl.BlockSpec((None, tr, c), lambda j, r, p: (j, r, 0))],
            out_specs=pl.BlockSpec((None, tr, c), lambda j, r, p: (j, r, 0))),
        out_shape=jax.ShapeDtypeStruct((NSH, rh, c), BF16),
        compiler_params=_cp(("parallel", "parallel")),
    )(pos, g, recv)


def _chip_exchange(psums):
    n = len(psums)

    def body(*refs):
        ins, outs = refs[:n], refs[n:2 * n]
        send_sems, recv_sems = refs[2 * n:]
        x, y, c, chips = _mesh_pos()
        cps = []
        for a in range(n):
            for k, chip in enumerate(chips):
                j = 2 * chip[0] + chip[1]
                cp = pltpu.make_async_remote_copy(
                    src_ref=ins[a].at[j], dst_ref=outs[a].at[k],
                    send_sem=send_sems.at[a * 3 + k], recv_sem=recv_sems.at[a * 3 + k],
                    device_id=(chip[0], chip[1], c), device_id_type=MESH_T)
                cp.start()
                cps.append(cp)
        for cp in cps:
            cp.wait()

    return pl.pallas_call(
        body, name="rs_chip_exchange", in_specs=[ANY] * n, out_specs=[ANY] * n,
        out_shape=[jax.ShapeDtypeStruct((3,) + p.shape[1:], BF16) for p in psums],
        scratch_shapes=[pltpu.SemaphoreType.DMA((3 * n,)), pltpu.SemaphoreType.DMA((3 * n,))],
    )(*psums)


def _owner_sum(g, recv_a, recv_b, pos, name):
    _, _, rh, c = g.shape
    tr = _pick_rows(rh, c)

    def body(pos_ref, g_ref, ra_ref, rb_ref, o_ref):
        acc = g_ref[...] + ra_ref[...]
        for k in range(3):
            acc = acc + rb_ref[k].astype(F32)
        o_ref[...] = acc

    return pl.pallas_call(
        body, name=name,
        grid_spec=pltpu.PrefetchScalarGridSpec(
            num_scalar_prefetch=1, grid=(rh // tr,),
            in_specs=[pl.BlockSpec((None, None, tr, c), lambda r, p: (p[0], p[1], r, 0)),
                      pl.BlockSpec((None, tr, c), lambda r, p: (p[1], r, 0)),
                      pl.BlockSpec((3, tr, c), lambda r, p: (0, r, 0))],
            out_specs=pl.BlockSpec((None, tr, c), lambda r, p: (p[0], r, 0))),
        out_shape=jax.ShapeDtypeStruct((2, rh, c), F32),
        compiler_params=_cp(("parallel",)),
    )(pos, g, recv_a, recv_b)


def _sibling_allgather(halves):
    n = len(halves)

    def body(*refs):
        outs = refs[n:2 * n]
        send_sems, recv_sems = refs[2 * n:]
        x, y, c, _ = _mesh_pos()
        cps = []
        for a in range(n):
            cp = pltpu.make_async_remote_copy(
                src_ref=outs[a].at[c], dst_ref=outs[a].at[c], send_sem=send_sems.at[a], recv_sem=recv_sems.at[a],
                device_id=(x, y, 1 - c), device_id_type=MESH_T)
            cp.start()
            cps.append(cp)
        for a in range(n):
            cps[a].wait_send()
            pltpu.make_async_remote_copy(
                src_ref=outs[a].at[1 - c], dst_ref=outs[a].at[1 - c], send_sem=send_sems.at[a],
                recv_sem=recv_sems.at[a], device_id=(x, y, 1 - c), device_id_type=MESH_T).wait_recv()

    return pl.pallas_call(
        body, name="rs_sibling_allgather", in_specs=[ANY] * n, out_specs=[ANY] * n,
        out_shape=[jax.ShapeDtypeStruct(h.shape, F32) for h in halves],
        input_output_aliases={i: i for i in range(n)},
        scratch_shapes=[pltpu.SemaphoreType.DMA((n,)), pltpu.SemaphoreType.DMA((n,))],
    )(*halves)


def _small_allgather(part):
    m_per = SMALL_ROWS

    def body(x_ref, out_ref, send_sems, recv_sems, local_sem):
        x, y, c, chips = _mesh_pos()
        me, sibling = (x, y, c), (x, y, 1 - c)

        def rows(px, py, pc):
            return out_ref.at[pl.ds((4 * px + 2 * py + pc) * m_per, m_per), :]

        def copy(k, block, to, src=None):
            return pltpu.make_async_remote_copy(
                src_ref=rows(*block) if src is None else src, dst_ref=rows(*block),
                send_sem=send_sems.at[k], recv_sem=recv_sems.at[k], device_id=to, device_id_type=MESH_T)

        mine = pltpu.make_async_copy(x_ref, rows(*me), local_sem)
        mine.start()
        first = [copy(0, me, sibling, src=x_ref)]
        first += [copy(1 + j, me, (*chip, c), src=x_ref) for j, chip in enumerate(chips)]
        for cp in first:
            cp.start()
        passed = [copy(4 + j, (*chip, c), sibling) for j, chip in enumerate(chips)]
        for j, chip in enumerate(chips):
            copy(1 + j, (*chip, c), me).wait_recv()
            passed[j].start()
        copy(0, sibling, me).wait_recv()
        for j, chip in enumerate(chips):
            copy(4 + j, (*chip, 1 - c), me).wait_recv()
        for cp in first + passed:
            cp.wait_send()
        mine.wait()

    return pl.pallas_call(
        body, name="small_allgather",
        out_shape=jax.ShapeDtypeStruct((8 * m_per, D), F32),
        in_specs=[pl.BlockSpec(memory_space=pltpu.VMEM)], out_specs=pl.BlockSpec(memory_space=pltpu.VMEM),
        scratch_shapes=[pltpu.SemaphoreType.DMA((7,)), pltpu.SemaphoreType.DMA((7,)), pltpu.SemaphoreType.DMA],
    )(part)


def _pack_small(ln1_g, ln1_b, gln_g, gln_b, ln2_g, ln2_b, ln3_g, ln3_b, b_gates, b_s, w_s):
    rows = [ln1_g, ln1_b, gln_g, gln_b, ln2_g, ln2_b, ln3_g, ln3_b]
    rows = [r.reshape(1, D) for r in rows] + [b_gates.reshape(2, D), b_s.reshape(1, D), jnp.zeros((5, D), F32),
                                             w_s.reshape(128, D)]
    return jnp.concatenate(rows, axis=0)


def _unpack_small(p):
    out = [p[i:i + 1] for i in range(8)]
    return out + [p[8:10].reshape(1, 2 * D), p[10:11].reshape(1, 8, BLK), p[16:144].reshape(1, 8, BLK, BLK)]


def _local_step(x, pos_f, target, W, P):
    invf = ROPE_THETA ** (-jnp.arange(0, DH, 2, dtype=F32) / DH)
    invf = jnp.tile(invf, 4).reshape(1, 128)
    b_s_t = P["gmlp_b_s"].T

    h1, h1b, xh1, rstd1, a1, b1, h1t = _ffn_fwd(x, W["f1g"], W["f1u"], W["f1d"], P["ln1_g"], P["ln1_b"], "ffn1_fwd",
                                                emit_t=True)
    qkv = _matmul(h1b, W["w_in"], "nn", "proj_qkv", n=3 * ATT_W, b_col0=0)
    z = _matmul(h1b, W["w_in"], "nn", "proj_z", n=2 * GW, b_col0=3 * ATT_W)
    gl = _matmul(h1b, W["w_in"], "nn", "proj_gates", n=2 * D, b_col0=3 * ATT_W + 2 * GW)
    qkvb = _rope([qkv], pos_f, invf, 1.0, "rope_fwd")
    og = [_attn_fwd(gi, qkvb, "attn_fwd_g%d" % gi) for gi in range(NG)]
    y_attn, y_attn_t, lse = _attn_combine([o for o, _ in og], [l for _, l in og], "attn_combine")
    y_gmlp, y_gmlp_t = _gmlp_fwd(z, P["gmlp_ln_g"], P["gmlp_ln_b"], P["gmlp_w_s"], b_s_t, "gmlp_fwd")
    br_a = _matmul(y_attn, W["w_ab"], "nn", "branch_attn", n=D)
    br_b = _matmul(y_gmlp, W["w_gb"], "nn", "branch_gmlp", n=D)
    merged, merged_t = _merge_fwd(br_a, br_b, gl, P["b_gates"], "merge_fwd")
    mix = _matmul(merged, W["w_out"], "nn", "mix_out", n=D)
    h2, h2b, xh2, rstd2 = _resid_ln(h1, mix, P["ln2_g"], P["ln2_b"], "resid_ln2")
    y, _, xh3, rstd3, a2, b2 = _ffn_fwd(h2, W["f2g"], W["f2u"], W["f2d"], P["ln3_g"], P["ln3_b"], "ffn2_fwd")

    dr3, dg3, db3, loss = _ln_bwd(y, xh3, rstd3, P["ln3_g"], "loss_ln3_bwd", target=target)
    g_f2g, g_f2u, g_f2d, dh2 = _ffn_bwd(dr3, h2b, a2, b2, W["f2g"], W["f2u"], W["f2d"], "ffn2_bwd")
    dr2, dg2, db2 = _ln_bwd(dh2, xh2, rstd2, P["ln2_g"], "ln2_bwd")
    g_wout = _wgrad(merged_t, dr2, 128, D, "dw_out", row_sharded=True)
    dmerged = _matmul(dr2, W["w_out"], "nt", "dmerged", n=D)
    dab, dbb, dglb, dbg = _merge_bwd(dmerged, br_a, br_b, gl, P["b_gates"], "merge_bwd")
    g_wab = _wgrad(y_attn_t, dab, GRP_W // 2, 256, "dw_attn_branch", row_sharded=False)
    g_wgb = _wgrad(y_gmlp_t, dbb, 128, D, "dw_gmlp_branch", row_sharded=True)
    dy_attn = _matmul(dab, W["w_ab"], "nt", "dy_attn", n=GRP_W)
    dy_gmlp = _matmul(dbb, W["w_gb"], "nt", "dy_gmlp", n=GW)
    dzb, dws, dbs_t, dgln_g, dgln_b = _gmlp_bwd(z, dy_gmlp, P["gmlp_ln_g"], P["gmlp_ln_b"], P["gmlp_w_s"], b_s_t,
                                                 "gmlp_bwd")
    dqkv = None
    for gi in range(NG):
        dqkv = _attn_bwd(gi, qkvb, dy_attn, y_attn, lse, dqkv, "attn_bwd_g%d" % gi)
    dqkvb = _rope(list(dqkv), pos_f, invf, -1.0, "rope_bwd")
    dproj = jnp.concatenate([dqkvb, dzb, dglb], axis=1)
    g_win = _wgrad(h1t, dproj, D // 2, IN_SH, "dw_in", row_sharded=False)
    dh1 = _matmul(dproj, W["w_in"], "nt", "dh1", n=D, tn=D, tk=IN_SH, add=dr2, add_scale=ALPHA)
    dr1, dg1, db1 = _ln_bwd(dh1, xh1, rstd1, P["ln1_g"], "ln1_bwd")
    g_f1g, g_f1u, g_f1d, dx = _ffn_bwd(dr1, x.astype(BF16), a1, b1, W["f1g"], W["f1u"], W["f1d"], "ffn1_bwd")

    big = dict(f1g=g_f1g, f1u=g_f1u, f1d=g_f1d, w_in=g_win, w_ab=g_wab, w_gb=g_wgb, w_out=g_wout,
               f2g=g_f2g, f2u=g_f2u, f2d=g_f2d)
    small = _pack_small(dg1, db1, dgln_g, dgln_b, dg2, db2, dg3, db3, dbg, dbs_t.T, dws)
    return loss, dx, big, small


BIG = ("f1g", "f1u", "f1d", "w_in", "w_ab", "w_gb", "w_out", "f2g", "f2u", "f2d")
TRANSPOSED = ("f1g", "f1u", "f2g", "f2u")
KIND = dict(f1g="stack", f1u="stack", f1d="stack", w_in="col", w_ab="col", w_gb="stack", w_out="stack",
            f2g="stack", f2u="stack", f2d="stack")


def kernel(x, positions, ffn1_w_gate, ffn1_w_up, ffn1_w_down, ln1_g, ln1_b, w_in, b_gates, gmlp_ln_g, gmlp_ln_b, gmlp_w_s, gmlp_b_s, w_attn_branch, w_gmlp_branch, w_out, ln2_g, ln2_b, ffn2_w_gate, ffn2_w_up, ffn2_w_down, ln3_g, ln3_b, loss_target, m_ffn1_w_gate, m_ffn1_w_up, m_ffn1_w_down, m_ln1_g, m_ln1_b, m_w_in, m_b_gates, m_gmlp_ln_g, m_gmlp_ln_b, m_gmlp_w_s, m_gmlp_b_s, m_w_attn_branch, m_w_gmlp_branch, m_w_out, m_ln2_g, m_ln2_b, m_ffn2_w_gate, m_ffn2_w_up, m_ffn2_w_down, m_ln3_g, m_ln3_b, v_ffn1_w_gate, v_ffn1_w_up, v_ffn1_w_down, v_ln1_g, v_ln1_b, v_w_in, v_b_gates, v_gmlp_ln_g, v_gmlp_ln_b, v_gmlp_w_s, v_gmlp_b_s, v_w_attn_branch, v_w_gmlp_branch, v_w_out, v_ln2_g, v_ln2_b, v_ffn2_w_gate, v_ffn2_w_up, v_ffn2_w_down, v_ln3_g, v_ln3_b):
    cx, cy, cc = lax.axis_index("x"), lax.axis_index("y"), lax.axis_index("c")
    pos = jnp.stack([cc, 2 * cx + cy]).astype(jnp.int32)

    w_sh = dict(f1g=ffn1_w_gate, f1u=ffn1_w_up, f1d=ffn1_w_down, w_in=w_in, w_ab=w_attn_branch,
                w_gb=w_gmlp_branch, w_out=w_out, f2g=ffn2_w_gate, f2u=ffn2_w_up, f2d=ffn2_w_down)
    m_sh = dict(f1g=m_ffn1_w_gate, f1u=m_ffn1_w_up, f1d=m_ffn1_w_down, w_in=m_w_in, w_ab=m_w_attn_branch,
                w_gb=m_w_gmlp_branch, w_out=m_w_out, f2g=m_ffn2_w_gate, f2u=m_ffn2_w_up, f2d=m_ffn2_w_down)
    v_sh = dict(f1g=v_ffn1_w_gate, f1u=v_ffn1_w_up, f1d=v_ffn1_w_down, w_in=v_w_in, w_ab=v_w_attn_branch,
                w_gb=v_w_gmlp_branch, w_out=v_w_out, f2g=v_ffn2_w_gate, f2u=v_ffn2_w_up, f2d=v_ffn2_w_down)
    w_sh = {k: (v[0].T if k in TRANSPOSED else v[0]) for k, v in w_sh.items()}
    m_sh = {k: (v[0].T if k in TRANSPOSED else v[0]) for k, v in m_sh.items()}
    v_sh = {k: (v[0].T if k in TRANSPOSED else v[0]) for k, v in v_sh.items()}

    placed = [_place_shard(w_sh[k], KIND[k], pos, "place_" + k) for k in BIG]
    full = _gather_weights(placed, [KIND[k] for k in BIG], [w_sh[k].shape for k in BIG])
    W = dict(zip(BIG, full))
    W["w_gb"] = W["w_gb"].reshape(D, D)
    W["w_out"] = W["w_out"].reshape(D, D)
    P = dict(ln1_g=ln1_g, ln1_b=ln1_b, ln2_g=ln2_g, ln2_b=ln2_b, ln3_g=ln3_g, ln3_b=ln3_b, b_gates=b_gates,
             gmlp_ln_g=gmlp_ln_g, gmlp_ln_b=gmlp_ln_b, gmlp_w_s=gmlp_w_s[0], gmlp_b_s=gmlp_b_s[0])

    pos_f = positions.reshape(S, 1).astype(F32)
    loss_part, dx, big, small = _local_step(x[0], pos_f, loss_target[0], W, P)
    loss = lax.psum(loss_part[0, 0], ("x", "y", "c"))

    grads = [big[k] for k in BIG]
    recv_a = _pair_exchange(grads)
    psums = [_pair_sum(g, r, pos, "rs_pair_sum_" + k) for g, r, k in zip(grads, recv_a, BIG)]
    recv_b = _chip_exchange(psums)
    halves = [_owner_sum(g, ra, rb, pos, "rs_owner_sum_" + k) for g, ra, rb, k in zip(grads, recv_a, recv_b, BIG)]
    reduced = _sibling_allgather(halves)

    g_out, d_out, m_out, v_out = {}, {}, {}, {}
    for k, gfull in zip(BIG, reduced):
        shp = w_sh[k].shape
        res = _adamw(w_sh[k], gfull.reshape(shp), m_sh[k], v_sh[k], "adamw_" + k)
        if k in TRANSPOSED:
            res = [r.T for r in res]
        g_out[k], d_out[k], m_out[k], v_out[k] = [r[None] for r in res]

    parts = _small_allgather(small).reshape(8, SMALL_ROWS, D)
    sp = (ln1_g, ln1_b, gmlp_ln_g, gmlp_ln_b, ln2_g, ln2_b, ln3_g, ln3_b, b_gates, gmlp_b_s, gmlp_w_s)
    sm = (m_ln1_g, m_ln1_b, m_gmlp_ln_g, m_gmlp_ln_b, m_ln2_g, m_ln2_b, m_ln3_g, m_ln3_b, m_b_gates, m_gmlp_b_s,
          m_gmlp_w_s)
    sv = (v_ln1_g, v_ln1_b, v_gmlp_ln_g, v_gmlp_ln_b, v_ln2_g, v_ln2_b, v_ln3_g, v_ln3_b, v_b_gates, v_gmlp_b_s,
          v_gmlp_w_s)
    sg, sd, smn, svn = _small_sum_adamw(parts, _pack_small(*sp), _pack_small(*sm), _pack_small(*sv), "small_adamw")
    names = ("ln1_g", "ln1_b", "gmlp_ln_g", "gmlp_ln_b", "ln2_g", "ln2_b", "ln3_g", "ln3_b", "b_gates", "gmlp_b_s",
             "gmlp_w_s")
    for dst, packed in ((g_out, sg), (d_out, sd), (m_out, smn), (v_out, svn)):
        for nm, val in zip(names, _unpack_small(packed)):
            dst[nm] = val

    order = ("f1g", "f1u", "f1d", "ln1_g", "ln1_b", "w_in", "b_gates", "gmlp_ln_g", "gmlp_ln_b", "gmlp_w_s", "gmlp_b_s",
             "w_ab", "w_gb", "w_out", "ln2_g", "ln2_b", "f2g", "f2u", "f2d", "ln3_g", "ln3_b")
    outs = [loss, dx[None]]
    for dst in (g_out, d_out, m_out, v_out):
        outs += [dst[k] for k in order]
    return tuple(outs)
```

```python
import functools
import math

import jax
import jax.numpy as jnp
from jax import lax
from jax.experimental import pallas as pl
from jax.experimental.pallas import tpu as pltpu

F32 = jnp.float32
BF16 = jnp.bfloat16

S = 2048
D = 1024
NSH = 4
FSH = 704
ATT_W = 1536
GRP_W = 512
NG = 3
NH = 8
DH = 64
BLK = 128
NBLK = S // BLK
GW = 1024
IN_W = 8704
IN_SH = IN_W // NSH
ALPHA = 2.0 ** 0.25
LN_EPS = 1e-5
ROPE_THETA = 10000.0
DILATIONS = (1, 4, 16)
ADAM_LR, ADAM_B1, ADAM_B2, ADAM_EPS, ADAM_WD, ADAM_STEP = 0.001, 0.9, 0.999, 1e-08, 0.01, 10
SMALL_ROWS = 144
MESH_T = pl.DeviceIdType.MESH
MIB = 1024 * 1024
NEG_INF = float("-inf")


def _cp(sem, vmem_mib=48):
    return pltpu.CompilerParams(dimension_semantics=sem, vmem_limit_bytes=vmem_mib * MIB)


def _ln_stats(r):
    mu = jnp.mean(r, axis=-1, keepdims=True)
    xc = r - mu
    var = jnp.mean(xc * xc, axis=-1, keepdims=True)
    rstd = lax.rsqrt(var + LN_EPS)
    return xc * rstd, rstd


def _ln_dx(dxh, xh, rstd):
    m1 = jnp.mean(dxh, axis=-1, keepdims=True)
    m2 = jnp.mean(dxh * xh, axis=-1, keepdims=True)
    return rstd * (dxh - m1 - xh * m2)


def _dot_nt(a, b):
    return lax.dot_general(a, b, (((1,), (1,)), ((), ())), preferred_element_type=F32)


def _dot_tn(a, b):
    return lax.dot_general(a, b, (((0,), (0,)), ((), ())), preferred_element_type=F32)


def _dot(a, b):
    return jnp.dot(a, b, preferred_element_type=F32)


def _ffn_fwd(xin, wgt, wut, wd, ln_g, ln_b, name, emit_t=False):
    tm = 512

    def body(x_ref, wg_ref, wu_ref, wd_ref, g_ref, b_ref, *rest):
        if emit_t:
            h_ref, hb_ref, xh_ref, rstd_ref, a_ref, bb_ref, ht_ref, acc_ref = rest
        else:
            h_ref, hb_ref, xh_ref, rstd_ref, a_ref, bb_ref, acc_ref = rest
        j = pl.program_id(1)
        xb = x_ref[...].astype(BF16)
        a = _dot_nt(xb, wg_ref[...])
        b = _dot_nt(xb, wu_ref[...])
        a_ref[...] = a
        bb_ref[...] = b
        s = (a * jax.nn.sigmoid(a)) * b
        f = _dot(s.astype(BF16), wd_ref[...])

        @pl.when(j == 0)
        def _():
            acc_ref[...] = f

        @pl.when(j > 0)
        def _():
            acc_ref[...] += f

        @pl.when(j == NSH - 1)
        def _():
            r = ALPHA * x_ref[...] + 0.5 * acc_ref[...]
            xh, rstd = _ln_stats(r)
            h = xh * g_ref[...] + b_ref[...]
            h_ref[...] = h
            hb_ref[...] = h.astype(BF16)
            xh_ref[...] = xh
            rstd_ref[...] = rstd
            if emit_t:
                ht_ref[...] = h.T.astype(BF16)

    row = pl.BlockSpec((tm, D), lambda i, j: (i, 0))
    vec = pl.BlockSpec((1, D), lambda i, j: (0, 0))
    wsp = pl.BlockSpec((None, FSH, D), lambda i, j: (j, 0, 0))
    ab = pl.BlockSpec((None, tm, FSH), lambda i, j: (j, i, 0))
    out_specs = [row, row, row, pl.BlockSpec((tm, 1), lambda i, j: (i, 0)), ab, ab]
    out_shape = [jax.ShapeDtypeStruct((S, D), F32), jax.ShapeDtypeStruct((S, D), BF16),
                 jax.ShapeDtypeStruct((S, D), F32), jax.ShapeDtypeStruct((S, 1), F32),
                 jax.ShapeDtypeStruct((NSH, S, FSH), F32), jax.ShapeDtypeStruct((NSH, S, FSH), F32)]
    if emit_t:
        out_specs.append(pl.BlockSpec((D, tm), lambda i, j: (0, i)))
        out_shape.append(jax.ShapeDtypeStruct((D, S), BF16))
    return pl.pallas_call(
        body, name=name, grid=(S // tm, NSH),
        in_specs=[row, wsp, wsp, wsp, vec, vec], out_specs=out_specs, out_shape=out_shape,
        scratch_shapes=[pltpu.VMEM((tm, D), F32)],
        compiler_params=_cp(("parallel", "arbitrary")),
    )(xin, wgt, wut, wd, ln_g, ln_b)


def _ffn_bwd(dr, xin_b, a, b, wgt, wut, wd, name):
    tm = 256
    ni = S // tm
    hr = FSH // 2

    def body(dr_ref, a_ref, b_ref, wg_ref, wu_ref, wd_ref, x_hbm, dwg_hbm, dwu_hbm, dwd_hbm, dx_hbm,
             dx_acc, da_all, db_all, s_all, df_all, x_all, res_buf, sems):
        j = pl.program_id(0)
        i = pl.program_id(1)
        rows = pl.ds(pl.multiple_of(i * tm, tm), tm)

        @pl.when(jnp.logical_and(j == 0, i == 0))
        def _():
            cp = pltpu.make_async_copy(x_hbm, x_all, sems.at[0])
            cp.start()
            cp.wait()

        drv = dr_ref[...]
        df = (0.5 * drv).astype(BF16)

        @pl.when(j == 0)
        def _():
            df_all[rows, :] = df

        ds = _dot_nt(df, wd_ref[...])
        av = a_ref[...]
        bv = b_ref[...]
        sig = jax.nn.sigmoid(av)
        sl = av * sig
        da = (ds * bv * (sig * (1.0 + av * (1.0 - sig)))).astype(BF16)
        db = (ds * sl).astype(BF16)
        da_all[rows, :] = da
        db_all[rows, :] = db
        s_all[rows, :] = (sl * bv).astype(BF16)
        dx = _dot(da, wg_ref[...]) + _dot(db, wu_ref[...])

        @pl.when(j == 0)
        def _():
            dx_acc[rows, :] = ALPHA * drv + dx

        @pl.when(j > 0)
        def _():
            dx_acc[rows, :] += dx

        @pl.when(i == ni - 1)
        def _():
            copies = []
            for n, (lhs, rhs, out) in enumerate(((da_all, x_all, dwg_hbm), (db_all, x_all, dwu_hbm),
                                                 (s_all, df_all, dwd_hbm))):
                slot = n % 2
                if n >= 2:
                    for cp in copies[2 * (n - 2): 2 * (n - 2) + 2]:
                        cp.wait()
                res_buf[slot] = _dot_tn(lhs[...], rhs[...])
                for h in range(2):
                    cp = pltpu.make_async_copy(res_buf.at[slot, pl.ds(h * hr, hr), :], out.at[h, j],
                                               sems.at[1 + 2 * slot + h])
                    cp.start()
                    copies.append(cp)
            for cp in copies[2:]:
                cp.wait()

        @pl.when(jnp.logical_and(j == NSH - 1, i == ni - 1))
        def _():
            cp = pltpu.make_async_copy(dx_acc, dx_hbm, sems.at[0])
            cp.start()
            cp.wait()

    row = pl.BlockSpec((tm, D), lambda j, i: (i, 0))
    wsp = pl.BlockSpec((None, FSH, D), lambda j, i: (j, 0, 0))
    ab = pl.BlockSpec((None, tm, FSH), lambda j, i: (j, i, 0))
    dwshape = jax.ShapeDtypeStruct((2, NSH, hr, D), F32)
    return pl.pallas_call(
        body, name=name, grid=(NSH, ni),
        in_specs=[row, ab, ab, wsp, wsp, wsp, ANY],
        out_specs=[ANY, ANY, ANY, ANY],
        out_shape=[dwshape, dwshape, dwshape, jax.ShapeDtypeStruct((S, D), F32)],
        scratch_shapes=[pltpu.VMEM((S, D), F32), pltpu.VMEM((S, FSH), BF16), pltpu.VMEM((S, FSH), BF16),
                        pltpu.VMEM((S, FSH), BF16), pltpu.VMEM((S, D), BF16), pltpu.VMEM((S, D), BF16),
                        pltpu.VMEM((2, FSH, D), F32), pltpu.SemaphoreType.DMA((5,))],
        compiler_params=_cp(("arbitrary", "arbitrary"), vmem_mib=58),
    )(dr, a, b, wgt, wut, wd, xin_b)


def _matmul(a, b, mode, name, *, n, tm=512, tn=512, tk=None, b_col0=0, add=None, add_scale=1.0, out_dtype=F32):
    m, ka = a.shape
    tk = ka if tk is None else tk
    nk = ka // tk
    assert m % tm == 0 and n % tn == 0 and ka % tk == 0 and b_col0 % tn == 0
    off = b_col0 // tn

    def body(*refs):
        if add is None:
            a_ref, b_ref, o_ref = refs[:3]
            add_ref = None
            rest = refs[3:]
        else:
            a_ref, b_ref, add_ref, o_ref = refs[:4]
            rest = refs[4:]
        k = pl.program_id(2)
        av = a_ref[...].astype(BF16)
        bv = b_ref[...].astype(BF16)
        p = _dot(av, bv) if mode == "nn" else _dot_nt(av, bv)

        def finish(acc):
            if add_ref is not None:
                acc = acc + add_scale * add_ref[...]
            o_ref[...] = acc.astype(out_dtype)

        if nk == 1:
            finish(p)
        else:
            acc_ref = rest[0]

            @pl.when(k == 0)
            def _():
                acc_ref[...] = p

            @pl.when(k > 0)
            def _():
                acc_ref[...] += p

            @pl.when(k == nk - 1)
            def _():
                finish(acc_ref[...])

    a_spec = pl.BlockSpec((tm, tk), lambda i, j, k: (i, k))
    if mode == "nn":
        b_spec = pl.BlockSpec((tk, tn), lambda i, j, k: (k, j + off))
    else:
        b_spec = pl.BlockSpec((tn, tk), lambda i, j, k: (j, k))
    o_spec = pl.BlockSpec((tm, tn), lambda i, j, k: (i, j))
    in_specs = [a_spec, b_spec] + ([o_spec] if add is not None else [])
    args = (a, b) + ((add,) if add is not None else ())
    return pl.pallas_call(
        body, name=name, grid=(m // tm, n // tn, nk),
        in_specs=in_specs, out_specs=o_spec,
        out_shape=jax.ShapeDtypeStruct((m, n), out_dtype),
        scratch_shapes=[pltpu.VMEM((tm, tn), F32)] if nk > 1 else [],
        compiler_params=_cp(("parallel", "parallel", "arbitrary")),
    )(*args)


def _wgrad(xt, y, rh, c, name, row_sharded):
    if row_sharded:
        def body(x_ref, y_ref, o_ref):
            res = _dot(x_ref[...], y_ref[...].astype(BF16))
            for j in range(NSH):
                for h in range(2):
                    o_ref[h, j] = res[(2 * j + h) * rh:(2 * j + h + 1) * rh, :]

        grid = (1,)
        in_specs = [pl.BlockSpec((2 * NSH * rh, S), lambda g: (0, 0)), pl.BlockSpec((S, c), lambda g: (0, 0))]
        out_specs = pl.BlockSpec((2, NSH, rh, c), lambda g: (0, 0, 0, 0))
        sem = ("arbitrary",)
    else:
        def body(x_ref, y_ref, o_ref):
            o_ref[...] = _dot(x_ref[...], y_ref[...].astype(BF16))

        grid = (2, NSH)
        in_specs = [pl.BlockSpec((rh, S), lambda h, j: (h, 0)), pl.BlockSpec((S, c), lambda h, j: (0, j))]
        out_specs = pl.BlockSpec((None, None, rh, c), lambda h, j: (h, j, 0, 0))
        sem = ("parallel", "parallel")
    return pl.pallas_call(
        body, name=name, grid=grid, in_specs=in_specs, out_specs=out_specs,
        out_shape=jax.ShapeDtypeStruct((2, NSH, rh, c), F32),
        compiler_params=_cp(sem, vmem_mib=56),
    )(xt, y)


def _resid_ln(res, f, ln_g, ln_b, name):
    tm = 256

    def body(res_ref, f_ref, g_ref, b_ref, h_ref, hb_ref, xh_ref, rstd_ref):
        r = ALPHA * res_ref[...] + f_ref[...]
        xh, rstd = _ln_stats(r)
        h = xh * g_ref[...] + b_ref[...]
        h_ref[...] = h
        hb_ref[...] = h.astype(BF16)
        xh_ref[...] = xh
        rstd_ref[...] = rstd

    row = pl.BlockSpec((tm, D), lambda i: (i, 0))
    vec = pl.BlockSpec((1, D), lambda i: (0, 0))
    return pl.pallas_call(
        body, name=name, grid=(S // tm,),
        in_specs=[row, row, vec, vec],
        out_specs=[row, row, row, pl.BlockSpec((tm, 1), lambda i: (i, 0))],
        out_shape=[jax.ShapeDtypeStruct((S, D), F32), jax.ShapeDtypeStruct((S, D), BF16),
                   jax.ShapeDtypeStruct((S, D), F32), jax.ShapeDtypeStruct((S, 1), F32)],
        compiler_params=_cp(("parallel",)),
    )(res, f, ln_g, ln_b)


def _ln_bwd(dout, xh, rstd, ln_g, name, target=None):
    tm = 256
    with_loss = target is not None

    def body(*refs):
        if with_loss:
            y_ref, t_ref, xh_ref, rstd_ref, g_ref, dr_ref, dg_ref, db_ref, loss_ref = refs
            err = y_ref[...] - t_ref[...]
            dy = err * (1.0 / D)
        else:
            y_ref, xh_ref, rstd_ref, g_ref, dr_ref, dg_ref, db_ref = refs
            dy = y_ref[...]
        i = pl.program_id(0)
        xh = xh_ref[...]
        dr_ref[...] = _ln_dx(dy * g_ref[...], xh, rstd_ref[...])
        dg = jnp.sum(dy * xh, axis=0, keepdims=True)
        db = jnp.sum(dy, axis=0, keepdims=True)

        @pl.when(i == 0)
        def _():
            dg_ref[...] = dg
            db_ref[...] = db

        @pl.when(i > 0)
        def _():
            dg_ref[...] += dg
            db_ref[...] += db

        if with_loss:
            part = 0.5 * jnp.sum(jnp.mean(err * err, axis=-1, keepdims=True), axis=0, keepdims=True)
            part = jnp.broadcast_to(part, (8, 128))

            @pl.when(i == 0)
            def _():
                loss_ref[...] = part

            @pl.when(i > 0)
            def _():
                loss_ref[...] += part

    row = pl.BlockSpec((tm, D), lambda i: (i, 0))
    vec = pl.BlockSpec((1, D), lambda i: (0, 0))
    col = pl.BlockSpec((tm, 1), lambda i: (i, 0))
    in_specs = [row] + ([row] if with_loss else []) + [row, col, vec]
    out_specs = [row, vec, vec] + ([pl.BlockSpec((8, 128), lambda i: (0, 0))] if with_loss else [])
    out_shape = [jax.ShapeDtypeStruct((S, D), F32), jax.ShapeDtypeStruct((1, D), F32),
                 jax.ShapeDtypeStruct((1, D), F32)] + ([jax.ShapeDtypeStruct((8, 128), F32)] if with_loss else [])
    args = (dout,) + ((target,) if with_loss else ()) + (xh, rstd, ln_g)
    return pl.pallas_call(
        body, name=name, grid=(S // tm,), in_specs=in_specs, out_specs=out_specs, out_shape=out_shape,
        compiler_params=_cp(("arbitrary",)),
    )(*args)


def _rope(ts, pos_f, invf, sign, name):
    tm = 128
    nch = ATT_W // 128
    nin = len(ts)

    def body(*refs):
        t_refs, (pos_ref, invf_ref, o_ref) = refs[:nin], refs[nin:]
        ang = pos_ref[...] * invf_ref[...]
        lane = lax.broadcasted_iota(jnp.int32, (tm, 128), 1)
        first = (lane % DH) < (DH // 2)
        cosf = jnp.cos(ang)
        sinv = jnp.sin(ang) * sign
        sinf = jnp.where(first, -sinv, sinv)
        for sec in range(3):
            t_ref, base = (t_refs[0], sec * ATT_W) if nin == 1 else (t_refs[sec], 0)
            for ch in range(nch):
                x = t_ref[:, base + ch * 128: base + (ch + 1) * 128]
                if sec < 2:
                    sw = jnp.where(first, pltpu.roll(x, 96, 1), pltpu.roll(x, 32, 1))
                    x = x * cosf + sw * sinf
                o_ref[:, sec * ATT_W + ch * 128: sec * ATT_W + (ch + 1) * 128] = x.astype(BF16)

    t_specs = [pl.BlockSpec((tm, t.shape[1]), lambda i: (i, 0)) for t in ts]
    return pl.pallas_call(
        body, name=name, grid=(S // tm,),
        in_specs=t_specs + [pl.BlockSpec((tm, 1), lambda i: (i, 0)), pl.BlockSpec((1, 128), lambda i: (0, 0))],
        out_specs=pl.BlockSpec((tm, 3 * ATT_W), lambda i: (i, 0)),
        out_shape=jax.ShapeDtypeStruct((S, 3 * ATT_W), BF16),
        compiler_params=_cp(("parallel",)),
    )(*ts, pos_f, invf)


def _class_view(t, d):
    return t.reshape(S // d, d * t.shape[1])


def _attn_fwd(gi, qkvb, name):
    d = DILATIONS[gi]
    nblk = S // d // BLK
    nsec = 3 * NG

    def body(*refs):
        if nblk > 1:
            q_ref, kc_ref, kp_ref, vc_ref, vp_ref, o_ref, lse_ref = refs
            has_prev = pl.program_id(1) != 0
        else:
            q_ref, kc_ref, vc_ref, o_ref, lse_ref = refs
        qi = lax.broadcasted_iota(jnp.int32, (BLK, BLK), 0)
        kj = lax.broadcasted_iota(jnp.int32, (BLK, BLK), 1)
        mask_c = kj <= qi
        if nblk > 1:
            mask_p = jnp.logical_and(kj >= qi, has_prev)
        for h in range(NH):
            sl = slice(h * DH, (h + 1) * DH)
            q = q_ref[:, sl]
            sc = jnp.where(mask_c, _dot_nt(q, kc_ref[:, sl]) * 0.125, NEG_INF)
            m = jnp.max(sc, axis=-1, keepdims=True)
            if nblk > 1:
                sp = jnp.where(mask_p, _dot_nt(q, kp_ref[:, sl]) * 0.125, NEG_INF)
                m = jnp.maximum(m, jnp.max(sp, axis=-1, keepdims=True))
            pc = jnp.exp(sc - m)
            l = jnp.sum(pc, axis=-1, keepdims=True)
            o = _dot(pc.astype(BF16), vc_ref[:, sl])
            if nblk > 1:
                pp = jnp.exp(sp - m)
                l = l + jnp.sum(pp, axis=-1, keepdims=True)
                o = o + _dot(pp.astype(BF16), vp_ref[:, sl])
            o_ref[:, sl] = o / l
            lse_ref[:, sl] = jnp.broadcast_to(m + jnp.log(l), (BLK, DH))

    def cur(sec):
        return pl.BlockSpec((BLK, GRP_W), lambda r, n: (n, r * nsec + sec * NG + gi))

    def prev(sec):
        return pl.BlockSpec((BLK, GRP_W), lambda r, n: (jnp.maximum(n - 1, 0), r * nsec + sec * NG + gi))

    out = pl.BlockSpec((BLK, GRP_W), lambda r, n: (n, r))
    shp = jax.ShapeDtypeStruct((S // d, d * GRP_W), F32)
    qv = _class_view(qkvb, d)
    if nblk > 1:
        in_specs, args = [cur(0), cur(1), prev(1), cur(2), prev(2)], (qv, qv, qv, qv, qv)
    else:
        in_specs, args = [cur(0), cur(1), cur(2)], (qv, qv, qv)
    o, lse = pl.pallas_call(
        body, name=name, grid=(d, nblk), in_specs=in_specs, out_specs=[out, out], out_shape=[shp, shp],
        compiler_params=_cp(("parallel", "parallel")),
    )(*args)
    return o.reshape(S, GRP_W), lse.reshape(S, GRP_W)


def _attn_combine(os, lses, name):
    tm = 256

    def body(o0_ref, o1_ref, o2_ref, l0_ref, l1_ref, l2_ref, y_ref, yt_ref, l_ref):
        l0, l1, l2 = l0_ref[...], l1_ref[...], l2_ref[...]
        m = jnp.maximum(jnp.maximum(l0, l1), l2)
        e0, e1, e2 = jnp.exp(l0 - m), jnp.exp(l1 - m), jnp.exp(l2 - m)
        den = e0 + e1 + e2
        y = (e0 * o0_ref[...] + e1 * o1_ref[...] + e2 * o2_ref[...]) / den
        y_ref[...] = y
        yt_ref[...] = y.T.astype(BF16)
        l_ref[...] = m + jnp.log(den)

    blk = pl.BlockSpec((tm, GRP_W), lambda i: (i, 0))
    shp = jax.ShapeDtypeStruct((S, GRP_W), F32)
    return pl.pallas_call(
        body, name=name, grid=(S // tm,), in_specs=[blk] * 6,
        out_specs=[blk, pl.BlockSpec((GRP_W, tm), lambda i: (0, i)), blk],
        out_shape=[shp, jax.ShapeDtypeStruct((GRP_W, S), BF16), shp],
        compiler_params=_cp(("parallel",)),
    )(*os, *lses)


def _attn_bwd(gi, qkvb, dy, y, lse, grads, name):
    d = DILATIONS[gi]
    nblk = S // d // BLK
    nsec = 3 * NG
    nal = 0 if grads is None else 3

    def body(*refs):
        refs = refs[nal:]
        if nblk > 1:
            (q_ref, qn_ref, k_ref, kp_ref, v_ref, vp_ref, dy_ref, dyn_ref, y_ref, yn_ref, l_ref, ln_ref,
             dq_ref, dk_ref, dv_ref) = refs
            n = pl.program_id(1)
            has_prev = n != 0
            has_next = n != nblk - 1
        else:
            q_ref, k_ref, v_ref, dy_ref, y_ref, l_ref, dq_ref, dk_ref, dv_ref = refs
        qi = lax.broadcasted_iota(jnp.int32, (BLK, BLK), 0)
        kj = lax.broadcasted_iota(jnp.int32, (BLK, BLK), 1)
        mask_c = kj <= qi
        if nblk > 1:
            mask_p = jnp.logical_and(kj >= qi, has_prev)
            mask_n = jnp.logical_and(kj >= qi, has_next)
        for h in range(NH):
            sl = slice(h * DH, (h + 1) * DH)
            q, k, v = q_ref[:, sl], k_ref[:, sl], v_ref[:, sl]
            dy_h = dy_ref[:, sl]
            dd = jnp.sum(dy_h * y_ref[:, sl], axis=-1, keepdims=True)
            lcol = l_ref[:, h * DH:h * DH + 1]
            dyb = dy_h.astype(BF16)
            p = jnp.exp(jnp.where(mask_c, _dot_nt(q, k) * 0.125, NEG_INF) - lcol)
            ds = (p * (_dot_nt(dyb, v) - dd)).astype(BF16)
            dq = _dot(ds, k)
            dk = _dot_tn(ds, q)
            dv = _dot_tn(p.astype(BF16), dyb)
            if nblk > 1:
                qn, kpv, vpv = qn_ref[:, sl], kp_ref[:, sl], vp_ref[:, sl]
                dyn = dyn_ref[:, sl]
                ddn = jnp.sum(dyn * yn_ref[:, sl], axis=-1, keepdims=True)
                lncol = ln_ref[:, h * DH:h * DH + 1]
                dynb = dyn.astype(BF16)
                pp = jnp.exp(jnp.where(mask_p, _dot_nt(q, kpv) * 0.125, NEG_INF) - lcol)
                dsp = (pp * (_dot_nt(dyb, vpv) - dd)).astype(BF16)
                dq = dq + _dot(dsp, kpv)
                pn = jnp.exp(jnp.where(mask_n, _dot_nt(qn, k) * 0.125, NEG_INF) - lncol)
                dsn = (pn * (_dot_nt(dynb, v) - ddn)).astype(BF16)
                dk = dk + _dot_tn(dsn, qn)
                dv = dv + _dot_tn(pn.astype(BF16), dynb)
            dq_ref[:, sl] = dq * 0.125
            dk_ref[:, sl] = dk * 0.125
            dv_ref[:, sl] = dv

    def spec(col, shift):
        def idx(r, n):
            return (jnp.clip(n + shift, 0, nblk - 1), col(r))
        return pl.BlockSpec((BLK, GRP_W), idx)

    def qkv_col(sec):
        return lambda r: r * nsec + sec * NG + gi

    tok = lambda r: r
    qv, dyv, yv, lv = (_class_view(t, d) for t in (qkvb, dy, y, lse))
    if nblk > 1:
        in_specs = [spec(qkv_col(0), 0), spec(qkv_col(0), 1), spec(qkv_col(1), 0), spec(qkv_col(1), -1),
                    spec(qkv_col(2), 0), spec(qkv_col(2), -1), spec(tok, 0), spec(tok, 1), spec(tok, 0), spec(tok, 1),
                    spec(tok, 0), spec(tok, 1)]
        args = (qv, qv, qv, qv, qv, qv, dyv, dyv, yv, yv, lv, lv)
    else:
        in_specs = [spec(qkv_col(0), 0), spec(qkv_col(1), 0), spec(qkv_col(2), 0), spec(tok, 0), spec(tok, 0),
                    spec(tok, 0)]
        args = (qv, qv, qv, dyv, yv, lv)
    out = pl.BlockSpec((BLK, GRP_W), lambda r, n: (n, r * NG + gi))
    shp = jax.ShapeDtypeStruct((S // d, d * ATT_W), F32)
    aliased = () if grads is None else tuple(_class_view(g, d) for g in grads)
    res = pl.pallas_call(
        body, name=name, grid=(d, nblk),
        in_specs=[ANY] * nal + in_specs, out_specs=[out, out, out], out_shape=[shp, shp, shp],
        input_output_aliases={i: i for i in range(nal)},
        compiler_params=_cp(("parallel", "parallel")),
    )(*aliased, *args)
    return tuple(g.reshape(S, ATT_W) for g in res)


_SQRT_HALF = 0.7071067811865476
_INV_SQRT_2PI = 0.3989422804014327


def _gelu(z):
    return 0.5 * z * (1.0 + lax.erf(z * _SQRT_HALF))


def _gelu_grad(z):
    return 0.5 * (1.0 + lax.erf(z * _SQRT_HALF)) + z * (jnp.exp(-0.5 * z * z) * _INV_SQRT_2PI)


def _tril_mask():
    t = lax.broadcasted_iota(jnp.int32, (BLK, BLK), 0)
    s = lax.broadcasted_iota(jnp.int32, (BLK, BLK), 1)
    return s <= t


def _gmlp_fwd(z, ln_g, ln_b, w_s, b_s_t, name):
    def body(z_ref, g_ref, b_ref, ws_ref, bs_ref, y_ref, yt_ref):
        zg = _gelu(z_ref[...])
        u = zg[:, :GW]
        xh, _ = _ln_stats(zg[:, GW:])
        vn = (xh * g_ref[...] + b_ref[...]).astype(BF16)
        tril = _tril_mask()
        for gg in range(8):
            sl = slice(gg * BLK, (gg + 1) * BLK)
            wt = jnp.where(tril, ws_ref[gg], 0.0).astype(BF16)
            mixed = _dot(wt, vn[:, sl]) + bs_ref[:, gg:gg + 1]
            yv = u[:, sl] * mixed
            y_ref[:, sl] = yv.astype(BF16)
            yt_ref[sl, :] = yv.T.astype(BF16)

    vec = pl.BlockSpec((1, GW), lambda n: (0, 0))
    return pl.pallas_call(
        body, name=name, grid=(NBLK,),
        in_specs=[pl.BlockSpec((BLK, 2 * GW), lambda n: (n, 0)), vec, vec,
                  pl.BlockSpec((8, BLK, BLK), lambda n: (0, 0, 0)), pl.BlockSpec((BLK, 8), lambda n: (0, 0))],
        out_specs=[pl.BlockSpec((BLK, GW), lambda n: (n, 0)), pl.BlockSpec((GW, BLK), lambda n: (0, n))],
        out_shape=[jax.ShapeDtypeStruct((S, GW), BF16), jax.ShapeDtypeStruct((GW, S), BF16)],
        compiler_params=_cp(("parallel",)),
    )(z, ln_g, ln_b, w_s, b_s_t)


def _gmlp_bwd(z, dy, ln_g, ln_b, w_s, b_s_t, name):
    def body(z_ref, dy_ref, g_ref, b_ref, ws_ref, bs_ref, dz_ref, dws_ref, dbs_ref, dg_ref, db_ref, dvn_ref):
        n = pl.program_id(0)
        zv = z_ref[...]
        zg = _gelu(zv)
        u = zg[:, :GW]
        xh, rstd = _ln_stats(zg[:, GW:])
        vn = (xh * g_ref[...] + b_ref[...]).astype(BF16)
        tril = _tril_mask()

        @pl.when(n == 0)
        def _():
            dws_ref[...] = jnp.zeros_like(dws_ref)
            dbs_ref[...] = jnp.zeros_like(dbs_ref)
            dg_ref[...] = jnp.zeros_like(dg_ref)
            db_ref[...] = jnp.zeros_like(db_ref)

        for gg in range(8):
            sl = slice(gg * BLK, (gg + 1) * BLK)
            wt = jnp.where(tril, ws_ref[gg], 0.0).astype(BF16)
            dyg = dy_ref[:, sl]
            mixed = _dot(wt, vn[:, sl]) + bs_ref[:, gg:gg + 1]
            dz_ref[:, sl] = (dyg * mixed * _gelu_grad(zv[:, sl])).astype(BF16)
            dmix = dyg * u[:, sl]
            dmb = dmix.astype(BF16)
            dws_ref[gg] += jnp.where(tril, _dot_nt(dmb, vn[:, sl]), 0.0)
            dbs_ref[:, gg:gg + 1] += jnp.sum(dmix, axis=-1, keepdims=True)
            dvn_ref[:, sl] = _dot_tn(wt, dmb)

        dvn = dvn_ref[...]
        dg_ref[...] += jnp.sum(dvn * xh, axis=0, keepdims=True)
        db_ref[...] += jnp.sum(dvn, axis=0, keepdims=True)
        dvg = _ln_dx(dvn * g_ref[...], xh, rstd)
        dz_ref[:, GW:] = (dvg * _gelu_grad(zv[:, GW:])).astype(BF16)

    vec = pl.BlockSpec((1, GW), lambda n: (0, 0))
    ws = pl.BlockSpec((8, BLK, BLK), lambda n: (0, 0, 0))
    bs = pl.BlockSpec((BLK, 8), lambda n: (0, 0))
    return pl.pallas_call(
        body, name=name, grid=(NBLK,),
        in_specs=[pl.BlockSpec((BLK, 2 * GW), lambda n: (n, 0)), pl.BlockSpec((BLK, GW), lambda n: (n, 0)),
                  vec, vec, ws, bs],
        out_specs=[pl.BlockSpec((BLK, 2 * GW), lambda n: (n, 0)), ws, bs, vec, vec],
        out_shape=[jax.ShapeDtypeStruct((S, 2 * GW), BF16), jax.ShapeDtypeStruct((8, BLK, BLK), F32),
                   jax.ShapeDtypeStruct((BLK, 8), F32), jax.ShapeDtypeStruct((1, GW), F32),
                   jax.ShapeDtypeStruct((1, GW), F32)],
        scratch_shapes=[pltpu.VMEM((BLK, GW), F32)],
        compiler_params=_cp(("arbitrary",)),
    )(z, dy, ln_g, ln_b, w_s, b_s_t)


def _merge_fwd(a, b, gl, b_gates, name):
    tm = 256

    def body(a_ref, b_ref, g0_ref, g1_ref, bg_ref, o_ref, ot_ref):
        g0 = jax.nn.sigmoid(g0_ref[...] + bg_ref[:, :D])
        g1 = jax.nn.sigmoid(g1_ref[...] + bg_ref[:, D:])
        mg = g0 * a_ref[...] + g1 * b_ref[...]
        o_ref[...] = mg.astype(BF16)
        ot_ref[...] = mg.T.astype(BF16)

    row = pl.BlockSpec((tm, D), lambda i: (i, 0))
    return pl.pallas_call(
        body, name=name, grid=(S // tm,),
        in_specs=[row, row, row, pl.BlockSpec((tm, D), lambda i: (i, 1)), pl.BlockSpec((1, 2 * D), lambda i: (0, 0))],
        out_specs=[row, pl.BlockSpec((D, tm), lambda i: (0, i))],
        out_shape=[jax.ShapeDtypeStruct((S, D), BF16), jax.ShapeDtypeStruct((D, S), BF16)],
        compiler_params=_cp(("parallel",)),
    )(a, b, gl, gl, b_gates)


def _merge_bwd(dm, a, b, gl, b_gates, name):
    tm = 256

    def body(dm_ref, a_ref, b_ref, g0_ref, g1_ref, bg_ref, da_ref, db_ref, dgl_ref, dbg_ref):
        i = pl.program_id(0)
        dmv = dm_ref[...]
        g0 = jax.nn.sigmoid(g0_ref[...] + bg_ref[:, :D])
        g1 = jax.nn.sigmoid(g1_ref[...] + bg_ref[:, D:])
        da_ref[...] = (dmv * g0).astype(BF16)
        db_ref[...] = (dmv * g1).astype(BF16)
        d0 = dmv * a_ref[...] * g0 * (1.0 - g0)
        d1 = dmv * b_ref[...] * g1 * (1.0 - g1)
        dgl_ref[:, :D] = d0.astype(BF16)
        dgl_ref[:, D:] = d1.astype(BF16)
        s0 = jnp.sum(d0, axis=0, keepdims=True)
        s1 = jnp.sum(d1, axis=0, keepdims=True)

        @pl.when(i == 0)
        def _():
            dbg_ref[:, :D] = s0
            dbg_ref[:, D:] = s1

        @pl.when(i > 0)
        def _():
            dbg_ref[:, :D] += s0
            dbg_ref[:, D:] += s1

    row = pl.BlockSpec((tm, D), lambda i: (i, 0))
    wide = pl.BlockSpec((tm, 2 * D), lambda i: (i, 0))
    bg = pl.BlockSpec((1, 2 * D), lambda i: (0, 0))
    return pl.pallas_call(
        body, name=name, grid=(S // tm,),
        in_specs=[row, row, row, row, pl.BlockSpec((tm, D), lambda i: (i, 1)), bg],
        out_specs=[row, row, wide, bg],
        out_shape=[jax.ShapeDtypeStruct((S, D), BF16), jax.ShapeDtypeStruct((S, D), BF16),
                   jax.ShapeDtypeStruct((S, 2 * D), BF16), jax.ShapeDtypeStruct((1, 2 * D), F32)],
        compiler_params=_cp(("arbitrary",)),
    )(dm, a, b, gl, gl, b_gates)


def _adam_math(w, g, m, v):
    m2 = ADAM_B1 * m + (1.0 - ADAM_B1) * g
    v2 = ADAM_B2 * v + (1.0 - ADAM_B2) * (g * g)
    m_hat = m2 / (1.0 - ADAM_B1 ** ADAM_STEP)
    v_hat = v2 / (1.0 - ADAM_B2 ** ADAM_STEP)
    delta = -ADAM_LR * (m_hat / (jnp.sqrt(v_hat) + ADAM_EPS) + ADAM_WD * w)
    return delta, m2, v2


def _pick_rows(rows, cols, unit=16, budget=MIB):
    best = unit
    for t in range(unit, rows + 1, unit):
        if rows % t == 0 and t * cols * 4 <= budget:
            best = t
    assert rows % best == 0
    return best


def _adamw(w, g, m, v, name):
    r, c = w.shape
    tr = _pick_rows(r, c, unit=8)

    def body(w_ref, g_ref, m_ref, v_ref, go_ref, d_ref, mo_ref, vo_ref):
        gv = g_ref[...]
        delta, m2, v2 = _adam_math(w_ref[...], gv, m_ref[...], v_ref[...])
        go_ref[...] = gv
        d_ref[...] = delta
        mo_ref[...] = m2
        vo_ref[...] = v2

    blk = pl.BlockSpec((tr, c), lambda i: (i, 0))
    shp = jax.ShapeDtypeStruct((r, c), F32)
    return pl.pallas_call(
        body, name=name, grid=(r // tr,), in_specs=[blk] * 4, out_specs=[blk] * 4, out_shape=[shp] * 4,
        compiler_params=_cp(("parallel",)),
    )(w, g, m, v)


def _small_sum_adamw(parts, w, m, v, name):
    tr = 48

    def body(p_ref, w_ref, m_ref, v_ref, g_ref, d_ref, mo_ref, vo_ref):
        gv = p_ref[0]
        for k in range(1, 8):
            gv = gv + p_ref[k]
        delta, m2, v2 = _adam_math(w_ref[...], gv, m_ref[...], v_ref[...])
        g_ref[...] = gv
        d_ref[...] = delta
        mo_ref[...] = m2
        vo_ref[...] = v2

    blk = pl.BlockSpec((tr, D), lambda i: (i, 0))
    shp = jax.ShapeDtypeStruct((SMALL_ROWS, D), F32)
    return pl.pallas_call(
        body, name=name, grid=(SMALL_ROWS // tr,),
        in_specs=[pl.BlockSpec((8, tr, D), lambda i: (0, i, 0)), blk, blk, blk],
        out_specs=[blk] * 4, out_shape=[shp] * 4,
        compiler_params=_cp(("parallel",)),
    )(parts, w, m, v)


ANY = pl.BlockSpec(memory_space=pl.ANY)


def _mesh_pos():
    x, y, c = lax.axis_index("x"), lax.axis_index("y"), lax.axis_index("c")
    chips = [(1 - x, y), (x, 1 - y), (1 - x, 1 - y)]
    return x, y, c, chips


def _place_shard(w, kind, pos, name):
    r, c = w.shape
    tr = _pick_rows(r, c)

    def body(pos_ref, w_ref, o_ref):
        o_ref[...] = w_ref[...].astype(BF16)

    if kind == "stack":
        o_spec = pl.BlockSpec((None, tr, c), lambda i, p: (p[1], i, 0))
        shape = (NSH, r, c)
    else:
        o_spec = pl.BlockSpec((tr, c), lambda i, p: (i, p[1]))
        shape = (r, NSH * c)
    return pl.pallas_call(
        body, name=name,
        grid_spec=pltpu.PrefetchScalarGridSpec(
            num_scalar_prefetch=1, grid=(r // tr,),
            in_specs=[pl.BlockSpec((tr, c), lambda i, p: (i, 0))], out_specs=o_spec),
        out_shape=jax.ShapeDtypeStruct(shape, BF16),
        compiler_params=_cp(("parallel",)),
    )(pos, w)


SEM = pl.BlockSpec(memory_space=pltpu.SEMAPHORE)
SPLIT_COPY = pltpu.CompilerParams(has_side_effects=pltpu.SideEffectType.DATAFLOW_SIDE_EFFECTING)


def _shard_window(ref, kind, j, h, dims):
    r, c = dims
    rows = pl.ds(pl.multiple_of(h * (r // 2), 16), r // 2)
    if kind == "stack":
        return ref.at[j, rows, :]
    return ref.at[rows, pl.ds(pl.multiple_of(j * c, 128), c)]


def _ici_copy(ref, kind, dims, j, c, sems, idx, to):
    win = _shard_window(ref, kind, j, c, dims)
    return pltpu.make_async_remote_copy(src_ref=win, dst_ref=win, send_sem=sems[0].at[idx], recv_sem=sems[1].at[idx],
                                        device_id=to, device_id_type=MESH_T)


def _gather_start(fulls, kinds, dims, after, name):
    n, na = len(fulls), len(after)

    def body(*refs):
        outs = refs[n + na:2 * n + na]
        send_sems, recv_sems, token = refs[2 * n + na:]
        x, y, c, chips = _mesh_pos()
        for a in range(n):
            for k, chip in enumerate(chips):
                _ici_copy(outs[a], kinds[a], dims[a], 2 * x + y, c, (send_sems, recv_sems), 3 * a + k,
                          (chip[0], chip[1], c)).start()
        token[...] = jnp.zeros_like(token)

    res = pl.pallas_call(
        body, name=name, in_specs=[ANY] * (n + na),
        out_specs=[ANY] * n + [SEM, SEM, pl.BlockSpec(memory_space=pltpu.VMEM)],
        out_shape=[jax.ShapeDtypeStruct(f.shape, BF16) for f in fulls]
        + [pltpu.SemaphoreType.DMA((3 * n,)), pltpu.SemaphoreType.DMA((3 * n,)), jax.ShapeDtypeStruct((8, 128), F32)],
        input_output_aliases={i: i for i in range(n)},
        compiler_params=SPLIT_COPY,
    )(*fulls, *after)
    return res[:n], res[n], res[n + 1], res[n + 2]


def _gather_wait(fulls, send_sems, recv_sems, kinds, dims, after, name):
    n, na = len(fulls), len(after)

    def body(*refs):
        ssem, rsem = refs[n], refs[n + 1]
        outs = refs[n + 2 + na:]
        x, y, c, chips = _mesh_pos()
        for a in range(n):
            for k, chip in enumerate(chips):
                to = (chip[0], chip[1], c)
                _ici_copy(outs[a], kinds[a], dims[a], 2 * x + y, c, (ssem, rsem), 3 * a + k, to).wait_send()
                _ici_copy(outs[a], kinds[a], dims[a], 2 * chip[0] + chip[1], c, (ssem, rsem), 3 * a + k, to).wait_recv()

    return pl.pallas_call(
        body, name=name, in_specs=[ANY] * n + [SEM, SEM] + [ANY] * na, out_specs=[ANY] * n,
        out_shape=[jax.ShapeDtypeStruct(f.shape, BF16) for f in fulls],
        input_output_aliases={i: i for i in range(n)},
        compiler_params=SPLIT_COPY,
    )(*fulls, send_sems, recv_sems, *after)


def _gather_forward(fulls, kinds, dims, name):
    n = len(fulls)

    def body(*refs):
        outs = refs[n:2 * n]
        sems = refs[2 * n:]
        x, y, c, chips = _mesh_pos()
        sib = (x, y, 1 - c)
        cps = []
        for a in range(n):
            for k, chip in enumerate(chips):
                cp = _ici_copy(outs[a], kinds[a], dims[a], 2 * chip[0] + chip[1], c, sems, 3 * a + k, sib)
                cp.start()
                cps.append(cp)
        for a in range(n):
            for k, chip in enumerate(chips):
                _ici_copy(outs[a], kinds[a], dims[a], 2 * chip[0] + chip[1], 1 - c, sems, 3 * a + k, sib).wait_recv()
        for cp in cps:
            cp.wait_send()

    return pl.pallas_call(
        body, name=name, in_specs=[ANY] * n, out_specs=[ANY] * n,
        out_shape=[jax.ShapeDtypeStruct(f.shape, BF16) for f in fulls],
        input_output_aliases={i: i for i in range(n)},
        scratch_shapes=[pltpu.SemaphoreType.DMA((3 * n,)), pltpu.SemaphoreType.DMA((3 * n,))],
    )(*fulls)


def _pair_exchange(grads, name):
    n = len(grads)

    def body(*refs):
        ins, outs = refs[:n], refs[n:2 * n]
        send_sems, recv_sems = refs[2 * n:]
        x, y, c, _ = _mesh_pos()
        cps = []
        for a in range(n):
            cp = pltpu.make_async_remote_copy(
                src_ref=ins[a].at[1 - c], dst_ref=outs[a], send_sem=send_sems.at[a], recv_sem=recv_sems.at[a],
                device_id=(x, y, 1 - c), device_id_type=MESH_T)
            cp.start()
            cps.append(cp)
        for cp in cps:
            cp.wait()

    return pl.pallas_call(
        body, name=name, in_specs=[ANY] * n, out_specs=[ANY] * n,
        out_shape=[jax.ShapeDtypeStruct(g.shape[1:], F32) for g in grads],
        scratch_shapes=[pltpu.SemaphoreType.DMA((n,)), pltpu.SemaphoreType.DMA((n,))],
    )(*grads)


def _pair_sum(g, recv, pos, name):
    _, _, rh, c = g.shape
    tr = _pick_rows(rh, c)

    def body(pos_ref, g_ref, r_ref, o_ref):
        o_ref[...] = (g_ref[...] + r_ref[...]).astype(BF16)

    return pl.pallas_call(
        body, name=name,
        grid_spec=pltpu.PrefetchScalarGridSpec(
            num_scalar_prefetch=1, grid=(NSH, rh // tr),
            in_specs=[pl.BlockSpec((None, None, tr, c), lambda j, r, p: (p[0], j, r, 0)),
                      pl.BlockSpec((None, tr, c), lambda j, r, p: (j, r, 0))],
            out_specs=pl.BlockSpec((None, tr, c), lambda j, r, p: (j, r, 0))),
        out_shape=jax.ShapeDtypeStruct((NSH, rh, c), BF16),
        compiler_params=_cp(("parallel", "parallel")),
    )(pos, g, recv)


def _chip_copy(src, land, a, k, chip, c, sems):
    return pltpu.make_async_remote_copy(
        src_ref=src.at[2 * chip[0] + chip[1]], dst_ref=land.at[k], send_sem=sems[0].at[3 * a + k],
        recv_sem=sems[1].at[3 * a + k], device_id=(chip[0], chip[1], c), device_id_type=MESH_T)


def _chip_start(psums, lands, name):
    n = len(psums)

    def body(*refs):
        srcs, dsts = refs[2 * n:3 * n], refs[3 * n:4 * n]
        send_sems, recv_sems, token = refs[4 * n:]
        x, y, c, chips = _mesh_pos()
        for a in range(n):
            for k, chip in enumerate(chips):
                _chip_copy(srcs[a], dsts[a], a, k, chip, c, (send_sems, recv_sems)).start()
        token[...] = jnp.zeros_like(token)

    res = pl.pallas_call(
        body, name=name, in_specs=[ANY] * (2 * n),
        out_specs=[ANY] * (2 * n) + [SEM, SEM, pl.BlockSpec(memory_space=pltpu.VMEM)],
        out_shape=[jax.ShapeDtypeStruct(p.shape, BF16) for p in psums]
        + [jax.ShapeDtypeStruct(l.shape, BF16) for l in lands]
        + [pltpu.SemaphoreType.DMA((3 * n,)), pltpu.SemaphoreType.DMA((3 * n,)), jax.ShapeDtypeStruct((8, 128), F32)],
        input_output_aliases={i: i for i in range(2 * n)},
        compiler_params=SPLIT_COPY,
    )(*psums, *lands)
    return res[:n], res[n:2 * n], res[2 * n], res[2 * n + 1], res[2 * n + 2]


def _chip_wait(psums, lands, send_sems, recv_sems, after, name):
    n, na = len(psums), len(after)

    def body(*refs):
        ssem, rsem = refs[2 * n], refs[2 * n + 1]
        outs = refs[2 * n + 2 + na:]
        srcs, dsts = outs[:n], outs[n:]
        x, y, c, chips = _mesh_pos()
        for a in range(n):
            for k, chip in enumerate(chips):
                cp = _chip_copy(srcs[a], dsts[a], a, k, chip, c, (ssem, rsem))
                cp.wait_send()
                cp.wait_recv()

    res = pl.pallas_call(
        body, name=name, in_specs=[ANY] * (2 * n) + [SEM, SEM] + [ANY] * na, out_specs=[ANY] * (2 * n),
        out_shape=[jax.ShapeDtypeStruct(p.shape, BF16) for p in psums]
        + [jax.ShapeDtypeStruct(l.shape, BF16) for l in lands],
        input_output_aliases={i: i for i in range(2 * n)},
        compiler_params=SPLIT_COPY,
    )(*psums, *lands, send_sems, recv_sems, *after)
    return res[n:]


def _owner_sum(g, recv_a, recv_b, pos, name):
    _, _, rh, c = g.shape
    tr = _pick_rows(rh, c)

    def body(pos_ref, g_ref, ra_ref, rb_ref, o_ref):
        acc = g_ref[...] + ra_ref[...]
        for k in range(3):
            acc = acc + rb_ref[k].astype(F32)
        o_ref[...] = acc

    return pl.pallas_call(
        body, name=name,
        grid_spec=pltpu.PrefetchScalarGridSpec(
            num_scalar_prefetch=1, grid=(rh // tr,),
            in_specs=[pl.BlockSpec((None, None, tr, c), lambda r, p: (p[0], p[1], r, 0)),
                      pl.BlockSpec((None, tr, c), lambda r, p: (p[1], r, 0)),
                      pl.BlockSpec((3, tr, c), lambda r, p: (0, r, 0))],
            out_specs=pl.BlockSpec((None, tr, c), lambda r, p: (p[0], r, 0))),
        out_shape=jax.ShapeDtypeStruct((2, rh, c), F32),
        compiler_params=_cp(("parallel",)),
    )(pos, g, recv_a, recv_b)


def _sibling_allgather(halves, name):
    n = len(halves)

    def body(*refs):
        outs = refs[n:2 * n]
        send_sems, recv_sems = refs[2 * n:]
        x, y, c, _ = _mesh_pos()
        cps = []
        for a in range(n):
            cp = pltpu.make_async_remote_copy(
                src_ref=outs[a].at[c], dst_ref=outs[a].at[c], send_sem=send_sems.at[a], recv_sem=recv_sems.at[a],
                device_id=(x, y, 1 - c), device_id_type=MESH_T)
            cp.start()
            cps.append(cp)
        for a in range(n):
            cps[a].wait_send()
            pltpu.make_async_remote_copy(
                src_ref=outs[a].at[1 - c], dst_ref=outs[a].at[1 - c], send_sem=send_sems.at[a],
                recv_sem=recv_sems.at[a], device_id=(x, y, 1 - c), device_id_type=MESH_T).wait_recv()

    return pl.pallas_call(
        body, name=name, in_specs=[ANY] * n, out_specs=[ANY] * n,
        out_shape=[jax.ShapeDtypeStruct(h.shape, F32) for h in halves],
        input_output_aliases={i: i for i in range(n)},
        scratch_shapes=[pltpu.SemaphoreType.DMA((n,)), pltpu.SemaphoreType.DMA((n,))],
    )(*halves)


def _small_allgather(part):
    m_per = SMALL_ROWS

    def body(x_ref, out_ref, send_sems, recv_sems, local_sem):
        x, y, c, chips = _mesh_pos()
        me, sibling = (x, y, c), (x, y, 1 - c)

        def rows(px, py, pc):
            return out_ref.at[pl.ds((4 * px + 2 * py + pc) * m_per, m_per), :]

        def copy(k, block, to, src=None):
            return pltpu.make_async_remote_copy(
                src_ref=rows(*block) if src is None else src, dst_ref=rows(*block),
                send_sem=send_sems.at[k], recv_sem=recv_sems.at[k], device_id=to, device_id_type=MESH_T)

        mine = pltpu.make_async_copy(x_ref, rows(*me), local_sem)
        mine.start()
        first = [copy(0, me, sibling, src=x_ref)]
        first += [copy(1 + j, me, (*chip, c), src=x_ref) for j, chip in enumerate(chips)]
        for cp in first:
            cp.start()
        passed = [copy(4 + j, (*chip, c), sibling) for j, chip in enumerate(chips)]
        for j, chip in enumerate(chips):
            copy(1 + j, (*chip, c), me).wait_recv()
            passed[j].start()
        copy(0, sibling, me).wait_recv()
        for j, chip in enumerate(chips):
            copy(4 + j, (*chip, 1 - c), me).wait_recv()
        for cp in first + passed:
            cp.wait_send()
        mine.wait()

    return pl.pallas_call(
        body, name="small_allgather",
        out_shape=jax.ShapeDtypeStruct((8 * m_per, D), F32),
        in_specs=[pl.BlockSpec(memory_space=pltpu.VMEM)], out_specs=pl.BlockSpec(memory_space=pltpu.VMEM),
        scratch_shapes=[pltpu.SemaphoreType.DMA((7,)), pltpu.SemaphoreType.DMA((7,)), pltpu.SemaphoreType.DMA],
    )(part)


def _pack_small(ln1_g, ln1_b, gln_g, gln_b, ln2_g, ln2_b, ln3_g, ln3_b, b_gates, b_s, w_s):
    rows = [ln1_g, ln1_b, gln_g, gln_b, ln2_g, ln2_b, ln3_g, ln3_b]
    rows = [r.reshape(1, D) for r in rows] + [b_gates.reshape(2, D), b_s.reshape(1, D), jnp.zeros((5, D), F32),
                                             w_s.reshape(128, D)]
    return jnp.concatenate(rows, axis=0)


def _unpack_small(p):
    out = [p[i:i + 1] for i in range(8)]
    return out + [p[8:10].reshape(1, 2 * D), p[10:11].reshape(1, 8, BLK), p[16:144].reshape(1, 8, BLK, BLK)]


GROUPS = (("f1g", "f1u", "f1d"), ("w_in",), ("w_ab", "w_gb", "w_out"), ("f2g", "f2u", "f2d"))


def _local_step(x, pos_f, target, P, weights_of, grads_ready):
    invf = ROPE_THETA ** (-jnp.arange(0, DH, 2, dtype=F32) / DH)
    invf = jnp.tile(invf, 4).reshape(1, 128)
    b_s_t = P["gmlp_b_s"].T

    W = dict(weights_of(0, []))
    h1, h1b, xh1, rstd1, a1, b1, h1t = _ffn_fwd(x, W["f1g"], W["f1u"], W["f1d"], P["ln1_g"], P["ln1_b"], "ffn1_fwd",
                                                emit_t=True)
    W.update(weights_of(1, [h1b]))
    qkv = _matmul(h1b, W["w_in"], "nn", "proj_qkv", n=3 * ATT_W, b_col0=0)
    z = _matmul(h1b, W["w_in"], "nn", "proj_z", n=2 * GW, b_col0=3 * ATT_W)
    gl = _matmul(h1b, W["w_in"], "nn", "proj_gates", n=2 * D, b_col0=3 * ATT_W + 2 * GW)
    qkvb = _rope([qkv], pos_f, invf, 1.0, "rope_fwd")
    og = [_attn_fwd(gi, qkvb, "attn_fwd_g%d" % gi) for gi in range(NG)]
    y_attn, y_attn_t, lse = _attn_combine([o for o, _ in og], [l for _, l in og], "attn_combine")
    y_gmlp, y_gmlp_t = _gmlp_fwd(z, P["gmlp_ln_g"], P["gmlp_ln_b"], P["gmlp_w_s"], b_s_t, "gmlp_fwd")
    W.update(weights_of(2, [y_gmlp]))
    br_a = _matmul(y_attn, W["w_ab"], "nn", "branch_attn", n=D)
    br_b = _matmul(y_gmlp, W["w_gb"], "nn", "branch_gmlp", n=D)
    merged, merged_t = _merge_fwd(br_a, br_b, gl, P["b_gates"], "merge_fwd")
    mix = _matmul(merged, W["w_out"], "nn", "mix_out", n=D)
    h2, h2b, xh2, rstd2 = _resid_ln(h1, mix, P["ln2_g"], P["ln2_b"], "resid_ln2")
    W.update(weights_of(3, [h2b]))
    y, _, xh3, rstd3, a2, b2 = _ffn_fwd(h2, W["f2g"], W["f2u"], W["f2d"], P["ln3_g"], P["ln3_b"], "ffn2_fwd")

    dr3, dg3, db3, loss = _ln_bwd(y, xh3, rstd3, P["ln3_g"], "loss_ln3_bwd", target=target)
    g_f2g, g_f2u, g_f2d, dh2 = _ffn_bwd(dr3, h2b, a2, b2, W["f2g"], W["f2u"], W["f2d"], "ffn2_bwd")
    grads_ready(3, dict(f2g=g_f2g, f2u=g_f2u, f2d=g_f2d))
    dr2, dg2, db2 = _ln_bwd(dh2, xh2, rstd2, P["ln2_g"], "ln2_bwd")
    g_wout = _wgrad(merged_t, dr2, 128, D, "dw_out", row_sharded=True)
    dmerged = _matmul(dr2, W["w_out"], "nt", "dmerged", n=D)
    dab, dbb, dglb, dbg = _merge_bwd(dmerged, br_a, br_b, gl, P["b_gates"], "merge_bwd")
    g_wab = _wgrad(y_attn_t, dab, GRP_W // 2, 256, "dw_attn_branch", row_sharded=False)
    g_wgb = _wgrad(y_gmlp_t, dbb, 128, D, "dw_gmlp_branch", row_sharded=True)
    grads_ready(2, dict(w_ab=g_wab, w_gb=g_wgb, w_out=g_wout))
    dy_attn = _matmul(dab, W["w_ab"], "nt", "dy_attn", n=GRP_W)
    dy_gmlp = _matmul(dbb, W["w_gb"], "nt", "dy_gmlp", n=GW)
    dzb, dws, dbs_t, dgln_g, dgln_b = _gmlp_bwd(z, dy_gmlp, P["gmlp_ln_g"], P["gmlp_ln_b"], P["gmlp_w_s"], b_s_t,
                                                 "gmlp_bwd")
    dqkv = None
    for gi in range(NG):
        dqkv = _attn_bwd(gi, qkvb, dy_attn, y_attn, lse, dqkv, "attn_bwd_g%d" % gi)
    dqkvb = _rope(list(dqkv), pos_f, invf, -1.0, "rope_bwd")
    dproj = jnp.concatenate([dqkvb, dzb, dglb], axis=1)
    g_win = _wgrad(h1t, dproj, D // 2, IN_SH, "dw_in", row_sharded=False)
    grads_ready(1, dict(w_in=g_win))
    dh1 = _matmul(dproj, W["w_in"], "nt", "dh1", n=D, tn=D, tk=IN_SH, add=dr2, add_scale=ALPHA)
    dr1, dg1, db1 = _ln_bwd(dh1, xh1, rstd1, P["ln1_g"], "ln1_bwd")
    g_f1g, g_f1u, g_f1d, dx = _ffn_bwd(dr1, x.astype(BF16), a1, b1, W["f1g"], W["f1u"], W["f1d"], "ffn1_bwd")
    grads_ready(0, dict(f1g=g_f1g, f1u=g_f1u, f1d=g_f1d))

    small = _pack_small(dg1, db1, dgln_g, dgln_b, dg2, db2, dg3, db3, dbg, dbs_t.T, dws)
    return loss, dx, small


BIG = ("f1g", "f1u", "f1d", "w_in", "w_ab", "w_gb", "w_out", "f2g", "f2u", "f2d")
TRANSPOSED = ("f1g", "f1u", "f2g", "f2u")
KIND = dict(f1g="stack", f1u="stack", f1d="stack", w_in="col", w_ab="col", w_gb="stack", w_out="stack",
            f2g="stack", f2u="stack", f2d="stack")


def kernel(x, positions, ffn1_w_gate, ffn1_w_up, ffn1_w_down, ln1_g, ln1_b, w_in, b_gates, gmlp_ln_g, gmlp_ln_b, gmlp_w_s, gmlp_b_s, w_attn_branch, w_gmlp_branch, w_out, ln2_g, ln2_b, ffn2_w_gate, ffn2_w_up, ffn2_w_down, ln3_g, ln3_b, loss_target, m_ffn1_w_gate, m_ffn1_w_up, m_ffn1_w_down, m_ln1_g, m_ln1_b, m_w_in, m_b_gates, m_gmlp_ln_g, m_gmlp_ln_b, m_gmlp_w_s, m_gmlp_b_s, m_w_attn_branch, m_w_gmlp_branch, m_w_out, m_ln2_g, m_ln2_b, m_ffn2_w_gate, m_ffn2_w_up, m_ffn2_w_down, m_ln3_g, m_ln3_b, v_ffn1_w_gate, v_ffn1_w_up, v_ffn1_w_down, v_ln1_g, v_ln1_b, v_w_in, v_b_gates, v_gmlp_ln_g, v_gmlp_ln_b, v_gmlp_w_s, v_gmlp_b_s, v_w_attn_branch, v_w_gmlp_branch, v_w_out, v_ln2_g, v_ln2_b, v_ffn2_w_gate, v_ffn2_w_up, v_ffn2_w_down, v_ln3_g, v_ln3_b):
    cx, cy, cc = lax.axis_index("x"), lax.axis_index("y"), lax.axis_index("c")
    pos = jnp.stack([cc, 2 * cx + cy]).astype(jnp.int32)

    w_sh = dict(f1g=ffn1_w_gate, f1u=ffn1_w_up, f1d=ffn1_w_down, w_in=w_in, w_ab=w_attn_branch,
                w_gb=w_gmlp_branch, w_out=w_out, f2g=ffn2_w_gate, f2u=ffn2_w_up, f2d=ffn2_w_down)
    m_sh = dict(f1g=m_ffn1_w_gate, f1u=m_ffn1_w_up, f1d=m_ffn1_w_down, w_in=m_w_in, w_ab=m_w_attn_branch,
                w_gb=m_w_gmlp_branch, w_out=m_w_out, f2g=m_ffn2_w_gate, f2u=m_ffn2_w_up, f2d=m_ffn2_w_down)
    v_sh = dict(f1g=v_ffn1_w_gate, f1u=v_ffn1_w_up, f1d=v_ffn1_w_down, w_in=v_w_in, w_ab=v_w_attn_branch,
                w_gb=v_w_gmlp_branch, w_out=v_w_out, f2g=v_ffn2_w_gate, f2u=v_ffn2_w_up, f2d=v_ffn2_w_down)
    w_sh = {k: (v[0].T if k in TRANSPOSED else v[0]) for k, v in w_sh.items()}
    m_sh = {k: (v[0].T if k in TRANSPOSED else v[0]) for k, v in m_sh.items()}
    v_sh = {k: (v[0].T if k in TRANSPOSED else v[0]) for k, v in v_sh.items()}

    started, tokens = [], []
    for gi, names in enumerate(GROUPS):
        placed = [_place_shard(w_sh[k], KIND[k], pos, "place_" + k) for k in names]
        fulls, ssem, rsem, token = _gather_start(placed, [KIND[k] for k in names], [w_sh[k].shape for k in names],
                                                 tokens[-1:], "gather_start_g%d" % gi)
        started.append((fulls, ssem, rsem))
        tokens.append(token)

    def weights_of(gi, after):
        names = GROUPS[gi]
        kinds, dims = [KIND[k] for k in names], [w_sh[k].shape for k in names]
        fulls, ssem, rsem = started[gi]
        fulls = _gather_wait(fulls, ssem, rsem, kinds, dims, list(after) + (tokens if gi == 0 else []),
                             "gather_wait_g%d" % gi)
        fulls = _gather_forward(fulls, kinds, dims, "gather_forward_g%d" % gi)
        return {k: (f.reshape(D, D) if k in ("w_gb", "w_out") else f) for k, f in zip(names, fulls)}

    inflight = {}

    def grads_ready(gi, gd):
        names = GROUPS[gi]
        grads = [gd[k] for k in names]
        recv_a = _pair_exchange(grads, "rs_pair_exchange_g%d" % gi)
        psums = [_pair_sum(g, r, pos, "rs_pair_sum_" + k) for g, r, k in zip(grads, recv_a, names)]
        lands = [lax.empty((3,) + p.shape[1:], BF16) for p in psums]
        psums, lands, ssem, rsem, token = _chip_start(psums, lands, "rs_chip_start_g%d" % gi)
        inflight[gi] = (grads, recv_a, psums, lands, ssem, rsem, token)

    P = dict(ln1_g=ln1_g, ln1_b=ln1_b, ln2_g=ln2_g, ln2_b=ln2_b, ln3_g=ln3_g, ln3_b=ln3_b, b_gates=b_gates,
             gmlp_ln_g=gmlp_ln_g, gmlp_ln_b=gmlp_ln_b, gmlp_w_s=gmlp_w_s[0], gmlp_b_s=gmlp_b_s[0])
    pos_f = positions.reshape(S, 1).astype(F32)
    loss_part, dx, small = _local_step(x[0], pos_f, loss_target[0], P, weights_of, grads_ready)
    loss = lax.psum(loss_part[0, 0], ("x", "y", "c"))

    parts = _small_allgather(small).reshape(8, SMALL_ROWS, D)
    sp = (ln1_g, ln1_b, gmlp_ln_g, gmlp_ln_b, ln2_g, ln2_b, ln3_g, ln3_b, b_gates, gmlp_b_s, gmlp_w_s)
    sm = (m_ln1_g, m_ln1_b, m_gmlp_ln_g, m_gmlp_ln_b, m_ln2_g, m_ln2_b, m_ln3_g, m_ln3_b, m_b_gates, m_gmlp_b_s,
          m_gmlp_w_s)
    sv = (v_ln1_g, v_ln1_b, v_gmlp_ln_g, v_gmlp_ln_b, v_ln2_g, v_ln2_b, v_ln3_g, v_ln3_b, v_b_gates, v_gmlp_b_s,
          v_gmlp_w_s)
    sg, sd, smn, svn = _small_sum_adamw(parts, _pack_small(*sp), _pack_small(*sm), _pack_small(*sv), "small_adamw")
    names = ("ln1_g", "ln1_b", "gmlp_ln_g", "gmlp_ln_b", "ln2_g", "ln2_b", "ln3_g", "ln3_b", "b_gates", "gmlp_b_s",
             "gmlp_w_s")
    g_out, d_out, m_out, v_out = {}, {}, {}, {}
    for dst, packed in ((g_out, sg), (d_out, sd), (m_out, smn), (v_out, svn)):
        for nm, val in zip(names, _unpack_small(packed)):
            dst[nm] = val

    after = [sg]
    for gi in (3, 2, 1, 0):
        grads, recv_a, psums, lands, ssem, rsem, token = inflight[gi]
        recv_b = _chip_wait(psums, lands, ssem, rsem, after + [inflight[0][6]], "rs_chip_wait_g%d" % gi)
        halves = [_owner_sum(g, ra, rb, pos, "rs_owner_sum_" + k)
                  for g, ra, rb, k in zip(grads, recv_a, recv_b, GROUPS[gi])]
        reduced = _sibling_allgather(halves, "rs_sibling_allgather_g%d" % gi)
        for k, gfull in zip(GROUPS[gi], reduced):
            res = _adamw(w_sh[k], gfull.reshape(w_sh[k].shape), m_sh[k], v_sh[k], "adamw_" + k)
            after = [res[1]]
            if k in TRANSPOSED:
                res = [r.T for r in res]
            g_out[k], d_out[k], m_out[k], v_out[k] = [r[None] for r in res]

    order = ("f1g", "f1u", "f1d", "ln1_g", "ln1_b", "w_in", "b_gates", "gmlp_ln_g", "gmlp_ln_b", "gmlp_w_s", "gmlp_b_s",
             "w_ab", "w_gb", "w_out", "ln2_g", "ln2_b", "f2g", "f2u", "f2d", "ln3_g", "ln3_b")
    outs = [loss, dx[None]]
    for dst in (g_out, d_out, m_out, v_out):
        outs += [dst[k] for k in order]
    return tuple(outs)
```

```python
import functools
import math

import jax
import jax.numpy as jnp
from jax import lax
from jax.experimental import pallas as pl
from jax.experimental.pallas import tpu as pltpu

F32 = jnp.float32
BF16 = jnp.bfloat16

S = 2048
D = 1024
NSH = 4
FSH = 704
ATT_W = 1536
GRP_W = 512
NG = 3
NH = 8
DH = 64
BLK = 128
NBLK = S // BLK
GW = 1024
IN_W = 8704
IN_SH = IN_W // NSH
ALPHA = 2.0 ** 0.25
LN_EPS = 1e-5
ROPE_THETA = 10000.0
DILATIONS = (1, 4, 16)
ADAM_LR, ADAM_B1, ADAM_B2, ADAM_EPS, ADAM_WD, ADAM_STEP = 0.001, 0.9, 0.999, 1e-08, 0.01, 10
SMALL_ROWS = 144
MESH_T = pl.DeviceIdType.MESH
MIB = 1024 * 1024
NEG_INF = float("-inf")


def _cp(sem, vmem_mib=48):
    return pltpu.CompilerParams(dimension_semantics=sem, vmem_limit_bytes=vmem_mib * MIB)


def _ln_stats(r):
    mu = jnp.mean(r, axis=-1, keepdims=True)
    xc = r - mu
    var = jnp.mean(xc * xc, axis=-1, keepdims=True)
    rstd = lax.rsqrt(var + LN_EPS)
    return xc * rstd, rstd


def _ln_dx(dxh, xh, rstd):
    m1 = jnp.mean(dxh, axis=-1, keepdims=True)
    m2 = jnp.mean(dxh * xh, axis=-1, keepdims=True)
    return rstd * (dxh - m1 - xh * m2)


def _dot_nt(a, b):
    return lax.dot_general(a, b, (((1,), (1,)), ((), ())), preferred_element_type=F32)


def _dot_tn(a, b):
    return lax.dot_general(a, b, (((0,), (0,)), ((), ())), preferred_element_type=F32)


def _dot(a, b):
    return jnp.dot(a, b, preferred_element_type=F32)


def _ffn_fwd(xin, wgt, wut, wd, ln_g, ln_b, name, emit_t=False):
    tm = 512

    def body(x_ref, wg_ref, wu_ref, wd_ref, g_ref, b_ref, *rest):
        if emit_t:
            h_ref, hb_ref, xh_ref, rstd_ref, a_ref, bb_ref, ht_ref, acc_ref = rest
        else:
            h_ref, hb_ref, xh_ref, rstd_ref, a_ref, bb_ref, acc_ref = rest
        j = pl.program_id(1)
        xb = x_ref[...].astype(BF16)
        a = _dot_nt(xb, wg_ref[...])
        b = _dot_nt(xb, wu_ref[...])
        a_ref[...] = a
        bb_ref[...] = b
        s = (a * jax.nn.sigmoid(a)) * b
        f = _dot(s.astype(BF16), wd_ref[...])

        @pl.when(j == 0)
        def _():
            acc_ref[...] = f

        @pl.when(j > 0)
        def _():
            acc_ref[...] += f

        @pl.when(j == NSH - 1)
        def _():
            r = ALPHA * x_ref[...] + 0.5 * acc_ref[...]
            xh, rstd = _ln_stats(r)
            h = xh * g_ref[...] + b_ref[...]
            h_ref[...] = h
            hb_ref[...] = h.astype(BF16)
            xh_ref[...] = xh
            rstd_ref[...] = rstd
            if emit_t:
                ht_ref[...] = h.T.astype(BF16)

    row = pl.BlockSpec((tm, D), lambda i, j: (i, 0))
    vec = pl.BlockSpec((1, D), lambda i, j: (0, 0))
    wsp = pl.BlockSpec((None, FSH, D), lambda i, j: (j, 0, 0))
    ab = pl.BlockSpec((None, tm, FSH), lambda i, j: (j, i, 0))
    out_specs = [row, row, row, pl.BlockSpec((tm, 1), lambda i, j: (i, 0)), ab, ab]
    out_shape = [jax.ShapeDtypeStruct((S, D), F32), jax.ShapeDtypeStruct((S, D), BF16),
                 jax.ShapeDtypeStruct((S, D), F32), jax.ShapeDtypeStruct((S, 1), F32),
                 jax.ShapeDtypeStruct((NSH, S, FSH), F32), jax.ShapeDtypeStruct((NSH, S, FSH), F32)]
    if emit_t:
        out_specs.append(pl.BlockSpec((D, tm), lambda i, j: (0, i)))
        out_shape.append(jax.ShapeDtypeStruct((D, S), BF16))
    return pl.pallas_call(
        body, name=name, grid=(S // tm, NSH),
        in_specs=[row, wsp, wsp, wsp, vec, vec], out_specs=out_specs, out_shape=out_shape,
        scratch_shapes=[pltpu.VMEM((tm, D), F32)],
        compiler_params=_cp(("parallel", "arbitrary")),
    )(xin, wgt, wut, wd, ln_g, ln_b)


def _ffn_bwd(dr, xin_b, a, b, wgt, wut, wd, name):
    tm = 256
    ni = S // tm
    hr = FSH // 2

    def body(dr_ref, a_ref, b_ref, wg_ref, wu_ref, wd_ref, x_hbm, dwg_hbm, dwu_hbm, dwd_hbm, dx_hbm,
             dx_acc, da_all, db_all, s_all, df_all, x_all, res_buf, sems):
        j = pl.program_id(0)
        i = pl.program_id(1)
        rows = pl.ds(pl.multiple_of(i * tm, tm), tm)

        @pl.when(jnp.logical_and(j == 0, i == 0))
        def _():
            cp = pltpu.make_async_copy(x_hbm, x_all, sems.at[0])
            cp.start()
            cp.wait()

        drv = dr_ref[...]
        df = (0.5 * drv).astype(BF16)

        @pl.when(j == 0)
        def _():
            df_all[rows, :] = df

        ds = _dot_nt(df, wd_ref[...])
        av = a_ref[...]
        bv = b_ref[...]
        sig = jax.nn.sigmoid(av)
        sl = av * sig
        da = (ds * bv * (sig * (1.0 + av * (1.0 - sig)))).astype(BF16)
        db = (ds * sl).astype(BF16)
        da_all[rows, :] = da
        db_all[rows, :] = db
        s_all[rows, :] = (sl * bv).astype(BF16)
        dx = _dot(da, wg_ref[...]) + _dot(db, wu_ref[...])

        @pl.when(j == 0)
        def _():
            dx_acc[rows, :] = ALPHA * drv + dx

        @pl.when(j > 0)
        def _():
            dx_acc[rows, :] += dx

        @pl.when(i == ni - 1)
        def _():
            copies = []
            for n, (lhs, rhs, out) in enumerate(((da_all, x_all, dwg_hbm), (db_all, x_all, dwu_hbm),
                                                 (s_all, df_all, dwd_hbm))):
                slot = n % 2
                if n >= 2:
                    for cp in copies[2 * (n - 2): 2 * (n - 2) + 2]:
                        cp.wait()
                res_buf[slot] = _dot_tn(lhs[...], rhs[...])
                for h in range(2):
                    cp = pltpu.make_async_copy(res_buf.at[slot, pl.ds(h * hr, hr), :], out.at[h, j],
                                               sems.at[1 + 2 * slot + h])
                    cp.start()
                    copies.append(cp)
            for cp in copies[2:]:
                cp.wait()

        @pl.when(jnp.logical_and(j == NSH - 1, i == ni - 1))
        def _():
            cp = pltpu.make_async_copy(dx_acc, dx_hbm, sems.at[0])
            cp.start()
            cp.wait()

    row = pl.BlockSpec((tm, D), lambda j, i: (i, 0))
    wsp = pl.BlockSpec((None, FSH, D), lambda j, i: (j, 0, 0))
    ab = pl.BlockSpec((None, tm, FSH), lambda j, i: (j, i, 0))
    dwshape = jax.ShapeDtypeStruct((2, NSH, hr, D), F32)
    return pl.pallas_call(
        body, name=name, grid=(NSH, ni),
        in_specs=[row, ab, ab, wsp, wsp, wsp, ANY],
        out_specs=[ANY, ANY, ANY, ANY],
        out_shape=[dwshape, dwshape, dwshape, jax.ShapeDtypeStruct((S, D), F32)],
        scratch_shapes=[pltpu.VMEM((S, D), F32), pltpu.VMEM((S, FSH), BF16), pltpu.VMEM((S, FSH), BF16),
                        pltpu.VMEM((S, FSH), BF16), pltpu.VMEM((S, D), BF16), pltpu.VMEM((S, D), BF16),
                        pltpu.VMEM((2, FSH, D), F32), pltpu.SemaphoreType.DMA((5,))],
        compiler_params=_cp(("arbitrary", "arbitrary"), vmem_mib=58),
    )(dr, a, b, wgt, wut, wd, xin_b)


def _matmul(a, b, mode, name, *, n, tm=512, tn=512, tk=None, b_col0=0, add=None, add_scale=1.0, out_dtype=F32):
    m, ka = a.shape
    tk = ka if tk is None else tk
    nk = ka // tk
    assert m % tm == 0 and n % tn == 0 and ka % tk == 0 and b_col0 % tn == 0
    off = b_col0 // tn

    def body(*refs):
        if add is None:
            a_ref, b_ref, o_ref = refs[:3]
            add_ref = None
            rest = refs[3:]
        else:
            a_ref, b_ref, add_ref, o_ref = refs[:4]
            rest = refs[4:]
        k = pl.program_id(2)
        av = a_ref[...].astype(BF16)
        bv = b_ref[...].astype(BF16)
        p = _dot(av, bv) if mode == "nn" else _dot_nt(av, bv)

        def finish(acc):
            if add_ref is not None:
                acc = acc + add_scale * add_ref[...]
            o_ref[...] = acc.astype(out_dtype)

        if nk == 1:
            finish(p)
        else:
            acc_ref = rest[0]

            @pl.when(k == 0)
            def _():
                acc_ref[...] = p

            @pl.when(k > 0)
            def _():
                acc_ref[...] += p

            @pl.when(k == nk - 1)
            def _():
                finish(acc_ref[...])

    a_spec = pl.BlockSpec((tm, tk), lambda i, j, k: (i, k))
    if mode == "nn":
        b_spec = pl.BlockSpec((tk, tn), lambda i, j, k: (k, j + off))
    else:
        b_spec = pl.BlockSpec((tn, tk), lambda i, j, k: (j, k))
    o_spec = pl.BlockSpec((tm, tn), lambda i, j, k: (i, j))
    in_specs = [a_spec, b_spec] + ([o_spec] if add is not None else [])
    args = (a, b) + ((add,) if add is not None else ())
    return pl.pallas_call(
        body, name=name, grid=(m // tm, n // tn, nk),
        in_specs=in_specs, out_specs=o_spec,
        out_shape=jax.ShapeDtypeStruct((m, n), out_dtype),
        scratch_shapes=[pltpu.VMEM((tm, tn), F32)] if nk > 1 else [],
        compiler_params=_cp(("parallel", "parallel", "arbitrary")),
    )(*args)


def _wgrad(xt, y, rh, c, name, row_sharded):
    if row_sharded:
        def body(x_ref, y_ref, o_ref):
            res = _dot(x_ref[...], y_ref[...].astype(BF16))
            for j in range(NSH):
                for h in range(2):
                    o_ref[h, j] = res[(2 * j + h) * rh:(2 * j + h + 1) * rh, :]

        grid = (1,)
        in_specs = [pl.BlockSpec((2 * NSH * rh, S), lambda g: (0, 0)), pl.BlockSpec((S, c), lambda g: (0, 0))]
        out_specs = pl.BlockSpec((2, NSH, rh, c), lambda g: (0, 0, 0, 0))
        sem = ("arbitrary",)
    else:
        def body(x_ref, y_ref, o_ref):
            o_ref[...] = _dot(x_ref[...], y_ref[...].astype(BF16))

        grid = (2, NSH)
        in_specs = [pl.BlockSpec((rh, S), lambda h, j: (h, 0)), pl.BlockSpec((S, c), lambda h, j: (0, j))]
        out_specs = pl.BlockSpec((None, None, rh, c), lambda h, j: (h, j, 0, 0))
        sem = ("parallel", "parallel")
    return pl.pallas_call(
        body, name=name, grid=grid, in_specs=in_specs, out_specs=out_specs,
        out_shape=jax.ShapeDtypeStruct((2, NSH, rh, c), F32),
        compiler_params=_cp(sem, vmem_mib=56),
    )(xt, y)


def _resid_ln(res, f, ln_g, ln_b, name):
    tm = 256

    def body(res_ref, f_ref, g_ref, b_ref, h_ref, hb_ref, xh_ref, rstd_ref):
        r = ALPHA * res_ref[...] + f_ref[...]
        xh, rstd = _ln_stats(r)
        h = xh * g_ref[...] + b_ref[...]
        h_ref[...] = h
        hb_ref[...] = h.astype(BF16)
        xh_ref[...] = xh
        rstd_ref[...] = rstd

    row = pl.BlockSpec((tm, D), lambda i: (i, 0))
    vec = pl.BlockSpec((1, D), lambda i: (0, 0))
    return pl.pallas_call(
        body, name=name, grid=(S // tm,),
        in_specs=[row, row, vec, vec],
        out_specs=[row, row, row, pl.BlockSpec((tm, 1), lambda i: (i, 0))],
        out_shape=[jax.ShapeDtypeStruct((S, D), F32), jax.ShapeDtypeStruct((S, D), BF16),
                   jax.ShapeDtypeStruct((S, D), F32), jax.ShapeDtypeStruct((S, 1), F32)],
        compiler_params=_cp(("parallel",)),
    )(res, f, ln_g, ln_b)


def _ln_bwd(dout, xh, rstd, ln_g, name, target=None):
    tm = 256
    with_loss = target is not None

    def body(*refs):
        if with_loss:
            y_ref, t_ref, xh_ref, rstd_ref, g_ref, dr_ref, dg_ref, db_ref, loss_ref = refs
            err = y_ref[...] - t_ref[...]
            dy = err * (1.0 / D)
        else:
            y_ref, xh_ref, rstd_ref, g_ref, dr_ref, dg_ref, db_ref = refs
            dy = y_ref[...]
        i = pl.program_id(0)
        xh = xh_ref[...]
        dr_ref[...] = _ln_dx(dy * g_ref[...], xh, rstd_ref[...])
        dg = jnp.sum(dy * xh, axis=0, keepdims=True)
        db = jnp.sum(dy, axis=0, keepdims=True)

        @pl.when(i == 0)
        def _():
            dg_ref[...] = dg
            db_ref[...] = db

        @pl.when(i > 0)
        def _():
            dg_ref[...] += dg
            db_ref[...] += db

        if with_loss:
            part = 0.5 * jnp.sum(jnp.mean(err * err, axis=-1, keepdims=True), axis=0, keepdims=True)
            part = jnp.broadcast_to(part, (8, 128))

            @pl.when(i == 0)
            def _():
                loss_ref[...] = part

            @pl.when(i > 0)
            def _():
                loss_ref[...] += part

    row = pl.BlockSpec((tm, D), lambda i: (i, 0))
    vec = pl.BlockSpec((1, D), lambda i: (0, 0))
    col = pl.BlockSpec((tm, 1), lambda i: (i, 0))
    in_specs = [row] + ([row] if with_loss else []) + [row, col, vec]
    out_specs = [row, vec, vec] + ([pl.BlockSpec((8, 128), lambda i: (0, 0))] if with_loss else [])
    out_shape = [jax.ShapeDtypeStruct((S, D), F32), jax.ShapeDtypeStruct((1, D), F32),
                 jax.ShapeDtypeStruct((1, D), F32)] + ([jax.ShapeDtypeStruct((8, 128), F32)] if with_loss else [])
    args = (dout,) + ((target,) if with_loss else ()) + (xh, rstd, ln_g)
    return pl.pallas_call(
        body, name=name, grid=(S // tm,), in_specs=in_specs, out_specs=out_specs, out_shape=out_shape,
        compiler_params=_cp(("arbitrary",)),
    )(*args)


ROPE_TM = 256


def _rope_tables(pos_ref, invf_ref, sign):
    ang = pos_ref[...] * invf_ref[...]
    lane = lax.broadcasted_iota(jnp.int32, ang.shape, 1)
    first = (lane % DH) < (DH // 2)
    sinv = jnp.sin(ang) * sign
    return first, jnp.cos(ang), jnp.where(first, -sinv, sinv)


def _rotate(x, first, cosf, sinf):
    return x * cosf + jnp.where(first, pltpu.roll(x, 96, 1), pltpu.roll(x, 32, 1)) * sinf


def _rope_fwd(qkv, pos_f, invf, name):
    tm = ROPE_TM

    def body(t_ref, pos_ref, invf_ref, o0_ref, o1_ref, o2_ref, buf_ref):
        first, cosf, sinf = _rope_tables(pos_ref, invf_ref, 1.0)
        o_refs = (o0_ref, o1_ref, o2_ref)
        for sec in range(3):
            for gi, d in enumerate(DILATIONS):
                for ch in range(GRP_W // 128):
                    src = sec * ATT_W + gi * GRP_W + ch * 128
                    dst = slice(sec * GRP_W + ch * 128, sec * GRP_W + (ch + 1) * 128)
                    x = t_ref[:, src:src + 128]
                    if sec < 2:
                        x = _rotate(x, first, cosf, sinf)
                    if d == 1:
                        o_refs[gi][0, :, dst] = x.astype(BF16)
                    else:
                        buf_ref[...] = x
                        for r in range(d):
                            o_refs[gi][r, :, dst] = buf_ref[pl.ds(r, tm // d, stride=d), :].astype(BF16)

    return pl.pallas_call(
        body, name=name, grid=(S // tm,),
        in_specs=[pl.BlockSpec((tm, 3 * ATT_W), lambda i: (i, 0)), pl.BlockSpec((tm, 1), lambda i: (i, 0)),
                  pl.BlockSpec((1, 128), lambda i: (0, 0))],
        out_specs=[pl.BlockSpec((d, tm // d, 3 * GRP_W), lambda i: (0, i, 0)) for d in DILATIONS],
        out_shape=[jax.ShapeDtypeStruct((d, S // d, 3 * GRP_W), BF16) for d in DILATIONS],
        scratch_shapes=[pltpu.VMEM((tm, 128), F32)],
        compiler_params=_cp(("parallel",)),
    )(qkv, pos_f, invf)


def _rope_bwd(dqkv_c, pos_f, invf, name):
    tm = ROPE_TM

    def body(*refs):
        g_refs, (pos_ref, invf_ref, o_ref, buf_ref) = refs[:9], refs[9:]
        first, cosf, sinf = _rope_tables(pos_ref, invf_ref, -1.0)
        for sec in range(3):
            for gi, d in enumerate(DILATIONS):
                g_ref = g_refs[3 * gi + sec]
                for ch in range(GRP_W // 128):
                    cols = slice(ch * 128, (ch + 1) * 128)
                    if d == 1:
                        x = g_ref[0, :, cols]
                    else:
                        for r in range(d):
                            buf_ref[pl.ds(r, tm // d, stride=d), :] = g_ref[r, :, cols]
                        x = buf_ref[...]
                    if sec < 2:
                        x = _rotate(x, first, cosf, sinf)
                    dst = sec * ATT_W + gi * GRP_W + ch * 128
                    o_ref[:, dst:dst + 128] = x.astype(BF16)

    g_specs = [pl.BlockSpec((d, tm // d, GRP_W), lambda i: (0, i, 0)) for d in DILATIONS for _ in range(3)]
    return pl.pallas_call(
        body, name=name, grid=(S // tm,),
        in_specs=g_specs + [pl.BlockSpec((tm, 1), lambda i: (i, 0)), pl.BlockSpec((1, 128), lambda i: (0, 0))],
        out_specs=pl.BlockSpec((tm, 3 * ATT_W), lambda i: (i, 0)),
        out_shape=jax.ShapeDtypeStruct((S, 3 * ATT_W), BF16),
        scratch_shapes=[pltpu.VMEM((tm, 128), F32)],
        compiler_params=_cp(("parallel",)),
    )(*[g for grp in dqkv_c for g in grp], pos_f, invf)


def _class_order(ts, name):
    tm = ROPE_TM
    n = len(ts)

    def body(*refs):
        buf_ref = refs[3 * n]
        for a in range(n):
            for ch in range(GRP_W // 128):
                cols = slice(ch * 128, (ch + 1) * 128)
                buf_ref[...] = refs[a][:, cols]
                for b, d in enumerate(DILATIONS[1:]):
                    for r in range(d):
                        refs[n + 2 * a + b][r, :, cols] = buf_ref[pl.ds(r, tm // d, stride=d), :]

    return pl.pallas_call(
        body, name=name, grid=(S // tm,),
        in_specs=[pl.BlockSpec((tm, GRP_W), lambda i: (i, 0))] * n,
        out_specs=[pl.BlockSpec((d, tm // d, GRP_W), lambda i: (0, i, 0)) for _ in range(n) for d in DILATIONS[1:]],
        out_shape=[jax.ShapeDtypeStruct((d, S // d, GRP_W), F32) for _ in range(n) for d in DILATIONS[1:]],
        scratch_shapes=[pltpu.VMEM((tm, 128), F32)],
        compiler_params=_cp(("parallel",)),
    )(*ts)


def _attn_fwd(gi, qkv_c, name):
    d = DILATIONS[gi]
    nblk = S // d // BLK

    def body(*refs):
        if nblk > 1:
            q_ref, kc_ref, kp_ref, vc_ref, vp_ref, o_ref, lse_ref = refs
            has_prev = pl.program_id(1) != 0
        else:
            q_ref, kc_ref, vc_ref, o_ref, lse_ref = refs
        qi = lax.broadcasted_iota(jnp.int32, (BLK, BLK), 0)
        kj = lax.broadcasted_iota(jnp.int32, (BLK, BLK), 1)
        mask_c = kj <= qi
        if nblk > 1:
            mask_p = jnp.logical_and(kj >= qi, has_prev)
        for h in range(NH):
            sl = slice(h * DH, (h + 1) * DH)
            q = q_ref[:, sl]
            sc = jnp.where(mask_c, _dot_nt(q, kc_ref[:, sl]) * 0.125, NEG_INF)
            m = jnp.max(sc, axis=-1, keepdims=True)
            if nblk > 1:
                sp = jnp.where(mask_p, _dot_nt(q, kp_ref[:, sl]) * 0.125, NEG_INF)
                m = jnp.maximum(m, jnp.max(sp, axis=-1, keepdims=True))
            pc = jnp.exp(sc - m)
            l = jnp.sum(pc, axis=-1, keepdims=True)
            o = _dot(pc.astype(BF16), vc_ref[:, sl])
            if nblk > 1:
                pp = jnp.exp(sp - m)
                l = l + jnp.sum(pp, axis=-1, keepdims=True)
                o = o + _dot(pp.astype(BF16), vp_ref[:, sl])
            o_ref[:, sl] = o / l
            lse_ref[:, sl] = jnp.broadcast_to(m + jnp.log(l), (BLK, DH))

    def cur(sec):
        return pl.BlockSpec((None, BLK, GRP_W), lambda r, n: (r, n, sec))

    def prev(sec):
        return pl.BlockSpec((None, BLK, GRP_W), lambda r, n: (r, jnp.maximum(n - 1, 0), sec))

    out = pl.BlockSpec((None, BLK, GRP_W), lambda r, n: (r, n, 0))
    shp = jax.ShapeDtypeStruct((d, S // d, GRP_W), F32)
    if nblk > 1:
        in_specs, args = [cur(0), cur(1), prev(1), cur(2), prev(2)], (qkv_c,) * 5
    else:
        in_specs, args = [cur(0), cur(1), cur(2)], (qkv_c,) * 3
    return pl.pallas_call(
        body, name=name, grid=(d, nblk), in_specs=in_specs, out_specs=[out, out], out_shape=[shp, shp],
        compiler_params=_cp(("parallel", "parallel")),
    )(*args)


def _attn_combine(os, lses, name):
    tm = ROPE_TM

    def body(o0_ref, o1_ref, o2_ref, l0_ref, l1_ref, l2_ref, y_ref, yt_ref, l_ref, buf_ref):
        def token_order(ref, d, cols, slot):
            if d == 1:
                return ref[0, :, cols]
            for r in range(d):
                buf_ref[slot, pl.ds(r, tm // d, stride=d), :] = ref[r, :, cols]
            return buf_ref[slot]

        for ch in range(GRP_W // 128):
            cols = slice(ch * 128, (ch + 1) * 128)
            o = [token_order(ref, d, cols, k) for k, (ref, d) in enumerate(zip((o0_ref, o1_ref, o2_ref), DILATIONS))]
            ls = [token_order(ref, d, cols, 3 + k)
                  for k, (ref, d) in enumerate(zip((l0_ref, l1_ref, l2_ref), DILATIONS))]
            m = jnp.maximum(jnp.maximum(ls[0], ls[1]), ls[2])
            e = [jnp.exp(l - m) for l in ls]
            den = e[0] + e[1] + e[2]
            y = (e[0] * o[0] + e[1] * o[1] + e[2] * o[2]) / den
            y_ref[:, cols] = y
            yt_ref[cols, :] = y.T.astype(BF16)
            l_ref[:, cols] = m + jnp.log(den)

    blk = pl.BlockSpec((tm, GRP_W), lambda i: (i, 0))
    cls = [pl.BlockSpec((d, tm // d, GRP_W), lambda i: (0, i, 0)) for d in DILATIONS]
    shp = jax.ShapeDtypeStruct((S, GRP_W), F32)
    return pl.pallas_call(
        body, name=name, grid=(S // tm,), in_specs=cls + cls,
        out_specs=[blk, pl.BlockSpec((GRP_W, tm), lambda i: (0, i)), blk],
        out_shape=[shp, jax.ShapeDtypeStruct((GRP_W, S), BF16), shp],
        scratch_shapes=[pltpu.VMEM((6, tm, 128), F32)],
        compiler_params=_cp(("parallel",)),
    )(*os, *lses)


def _attn_bwd(gi, qkv_c, dy_c, y_c, lse_c, name):
    d = DILATIONS[gi]
    nblk = S // d // BLK

    def body(*refs):
        if nblk > 1:
            (q_ref, qn_ref, k_ref, kp_ref, v_ref, vp_ref, dy_ref, dyn_ref, y_ref, yn_ref, l_ref, ln_ref,
             dq_ref, dk_ref, dv_ref) = refs
            n = pl.program_id(1)
            has_prev = n != 0
            has_next = n != nblk - 1
        else:
            q_ref, k_ref, v_ref, dy_ref, y_ref, l_ref, dq_ref, dk_ref, dv_ref = refs
        qi = lax.broadcasted_iota(jnp.int32, (BLK, BLK), 0)
        kj = lax.broadcasted_iota(jnp.int32, (BLK, BLK), 1)
        mask_c = kj <= qi
        if nblk > 1:
            mask_p = jnp.logical_and(kj >= qi, has_prev)
            mask_n = jnp.logical_and(kj >= qi, has_next)
        for h in range(NH):
            sl = slice(h * DH, (h + 1) * DH)
            q, k, v = q_ref[:, sl], k_ref[:, sl], v_ref[:, sl]
            dy_h = dy_ref[:, sl]
            dd = jnp.sum(dy_h * y_ref[:, sl], axis=-1, keepdims=True)
            lcol = l_ref[:, h * DH:h * DH + 1]
            dyb = dy_h.astype(BF16)
            p = jnp.exp(jnp.where(mask_c, _dot_nt(q, k) * 0.125, NEG_INF) - lcol)
            ds = (p * (_dot_nt(dyb, v) - dd)).astype(BF16)
            dq = _dot(ds, k)
            dk = _dot_tn(ds, q)
            dv = _dot_tn(p.astype(BF16), dyb)
            if nblk > 1:
                qn, kpv, vpv = qn_ref[:, sl], kp_ref[:, sl], vp_ref[:, sl]
                dyn = dyn_ref[:, sl]
                ddn = jnp.sum(dyn * yn_ref[:, sl], axis=-1, keepdims=True)
                lncol = ln_ref[:, h * DH:h * DH + 1]
                dynb = dyn.astype(BF16)
                pp = jnp.exp(jnp.where(mask_p, _dot_nt(q, kpv) * 0.125, NEG_INF) - lcol)
                dsp = (pp * (_dot_nt(dyb, vpv) - dd)).astype(BF16)
                dq = dq + _dot(dsp, kpv)
                pn = jnp.exp(jnp.where(mask_n, _dot_nt(qn, k) * 0.125, NEG_INF) - lncol)
                dsn = (pn * (_dot_nt(dynb, v) - ddn)).astype(BF16)
                dk = dk + _dot_tn(dsn, qn)
                dv = dv + _dot_tn(pn.astype(BF16), dynb)
            dq_ref[:, sl] = dq * 0.125
            dk_ref[:, sl] = dk * 0.125
            dv_ref[:, sl] = dv

    def spec(sec, shift):
        def idx(r, n):
            return (r, jnp.clip(n + shift, 0, nblk - 1), sec)
        return pl.BlockSpec((None, BLK, GRP_W), idx)

    if nblk > 1:
        in_specs = [spec(0, 0), spec(0, 1), spec(1, 0), spec(1, -1), spec(2, 0), spec(2, -1),
                    spec(0, 0), spec(0, 1), spec(0, 0), spec(0, 1), spec(0, 0), spec(0, 1)]
        args = (qkv_c,) * 6 + (dy_c, dy_c, y_c, y_c, lse_c, lse_c)
    else:
        in_specs = [spec(0, 0), spec(1, 0), spec(2, 0), spec(0, 0), spec(0, 0), spec(0, 0)]
        args = (qkv_c, qkv_c, qkv_c, dy_c, y_c, lse_c)
    out = spec(0, 0)
    shp = jax.ShapeDtypeStruct((d, S // d, GRP_W), F32)
    return pl.pallas_call(
        body, name=name, grid=(d, nblk), in_specs=in_specs, out_specs=[out, out, out], out_shape=[shp, shp, shp],
        compiler_params=_cp(("parallel", "parallel")),
    )(*args)


_SQRT_HALF = 0.7071067811865476
_INV_SQRT_2PI = 0.3989422804014327


def _gelu(z):
    return 0.5 * z * (1.0 + lax.erf(z * _SQRT_HALF))


def _gelu_grad(z):
    return 0.5 * (1.0 + lax.erf(z * _SQRT_HALF)) + z * (jnp.exp(-0.5 * z * z) * _INV_SQRT_2PI)


def _tril_mask():
    t = lax.broadcasted_iota(jnp.int32, (BLK, BLK), 0)
    s = lax.broadcasted_iota(jnp.int32, (BLK, BLK), 1)
    return s <= t


def _gmlp_fwd(z, ln_g, ln_b, w_s, b_s_t, name):
    def body(z_ref, g_ref, b_ref, ws_ref, bs_ref, y_ref, yt_ref):
        zg = _gelu(z_ref[...])
        u = zg[:, :GW]
        xh, _ = _ln_stats(zg[:, GW:])
        vn = (xh * g_ref[...] + b_ref[...]).astype(BF16)
        tril = _tril_mask()
        for gg in range(8):
            sl = slice(gg * BLK, (gg + 1) * BLK)
            wt = jnp.where(tril, ws_ref[gg], 0.0).astype(BF16)
            mixed = _dot(wt, vn[:, sl]) + bs_ref[:, gg:gg + 1]
            yv = u[:, sl] * mixed
            y_ref[:, sl] = yv.astype(BF16)
            yt_ref[sl, :] = yv.T.astype(BF16)

    vec = pl.BlockSpec((1, GW), lambda n: (0, 0))
    return pl.pallas_call(
        body, name=name, grid=(NBLK,),
        in_specs=[pl.BlockSpec((BLK, 2 * GW), lambda n: (n, 0)), vec, vec,
                  pl.BlockSpec((8, BLK, BLK), lambda n: (0, 0, 0)), pl.BlockSpec((BLK, 8), lambda n: (0, 0))],
        out_specs=[pl.BlockSpec((BLK, GW), lambda n: (n, 0)), pl.BlockSpec((GW, BLK), lambda n: (0, n))],
        out_shape=[jax.ShapeDtypeStruct((S, GW), BF16), jax.ShapeDtypeStruct((GW, S), BF16)],
        compiler_params=_cp(("parallel",)),
    )(z, ln_g, ln_b, w_s, b_s_t)


def _gmlp_bwd(z, dy, ln_g, ln_b, w_s, b_s_t, name):
    def body(z_ref, dy_ref, g_ref, b_ref, ws_ref, bs_ref, dz_ref, dws_ref, dbs_ref, dg_ref, db_ref, dvn_ref):
        n = pl.program_id(0)
        zv = z_ref[...]
        zg = _gelu(zv)
        u = zg[:, :GW]
        xh, rstd = _ln_stats(zg[:, GW:])
        vn = (xh * g_ref[...] + b_ref[...]).astype(BF16)
        tril = _tril_mask()

        @pl.when(n == 0)
        def _():
            dws_ref[...] = jnp.zeros_like(dws_ref)
            dbs_ref[...] = jnp.zeros_like(dbs_ref)
            dg_ref[...] = jnp.zeros_like(dg_ref)
            db_ref[...] = jnp.zeros_like(db_ref)

        for gg in range(8):
            sl = slice(gg * BLK, (gg + 1) * BLK)
            wt = jnp.where(tril, ws_ref[gg], 0.0).astype(BF16)
            dyg = dy_ref[:, sl]
            mixed = _dot(wt, vn[:, sl]) + bs_ref[:, gg:gg + 1]
            dz_ref[:, sl] = (dyg * mixed * _gelu_grad(zv[:, sl])).astype(BF16)
            dmix = dyg * u[:, sl]
            dmb = dmix.astype(BF16)
            dws_ref[gg] += jnp.where(tril, _dot_nt(dmb, vn[:, sl]), 0.0)
            dbs_ref[:, gg:gg + 1] += jnp.sum(dmix, axis=-1, keepdims=True)
            dvn_ref[:, sl] = _dot_tn(wt, dmb)

        dvn = dvn_ref[...]
        dg_ref[...] += jnp.sum(dvn * xh, axis=0, keepdims=True)
        db_ref[...] += jnp.sum(dvn, axis=0, keepdims=True)
        dvg = _ln_dx(dvn * g_ref[...], xh, rstd)
        dz_ref[:, GW:] = (dvg * _gelu_grad(zv[:, GW:])).astype(BF16)

    vec = pl.BlockSpec((1, GW), lambda n: (0, 0))
    ws = pl.BlockSpec((8, BLK, BLK), lambda n: (0, 0, 0))
    bs = pl.BlockSpec((BLK, 8), lambda n: (0, 0))
    return pl.pallas_call(
        body, name=name, grid=(NBLK,),
        in_specs=[pl.BlockSpec((BLK, 2 * GW), lambda n: (n, 0)), pl.BlockSpec((BLK, GW), lambda n: (n, 0)),
                  vec, vec, ws, bs],
        out_specs=[pl.BlockSpec((BLK, 2 * GW), lambda n: (n, 0)), ws, bs, vec, vec],
        out_shape=[jax.ShapeDtypeStruct((S, 2 * GW), BF16), jax.ShapeDtypeStruct((8, BLK, BLK), F32),
                   jax.ShapeDtypeStruct((BLK, 8), F32), jax.ShapeDtypeStruct((1, GW), F32),
                   jax.ShapeDtypeStruct((1, GW), F32)],
        scratch_shapes=[pltpu.VMEM((BLK, GW), F32)],
        compiler_params=_cp(("arbitrary",)),
    )(z, dy, ln_g, ln_b, w_s, b_s_t)


def _merge_fwd(a, b, gl, b_gates, name):
    tm = 256

    def body(a_ref, b_ref, g0_ref, g1_ref, bg_ref, o_ref, ot_ref):
        g0 = jax.nn.sigmoid(g0_ref[...] + bg_ref[:, :D])
        g1 = jax.nn.sigmoid(g1_ref[...] + bg_ref[:, D:])
        mg = g0 * a_ref[...] + g1 * b_ref[...]
        o_ref[...] = mg.astype(BF16)
        ot_ref[...] = mg.T.astype(BF16)

    row = pl.BlockSpec((tm, D), lambda i: (i, 0))
    return pl.pallas_call(
        body, name=name, grid=(S // tm,),
        in_specs=[row, row, row, pl.BlockSpec((tm, D), lambda i: (i, 1)), pl.BlockSpec((1, 2 * D), lambda i: (0, 0))],
        out_specs=[row, pl.BlockSpec((D, tm), lambda i: (0, i))],
        out_shape=[jax.ShapeDtypeStruct((S, D), BF16), jax.ShapeDtypeStruct((D, S), BF16)],
        compiler_params=_cp(("parallel",)),
    )(a, b, gl, gl, b_gates)


def _merge_bwd(dm, a, b, gl, b_gates, name):
    tm = 256

    def body(dm_ref, a_ref, b_ref, g0_ref, g1_ref, bg_ref, da_ref, db_ref, dgl_ref, dbg_ref):
        i = pl.program_id(0)
        dmv = dm_ref[...]
        g0 = jax.nn.sigmoid(g0_ref[...] + bg_ref[:, :D])
        g1 = jax.nn.sigmoid(g1_ref[...] + bg_ref[:, D:])
        da_ref[...] = (dmv * g0).astype(BF16)
        db_ref[...] = (dmv * g1).astype(BF16)
        d0 = dmv * a_ref[...] * g0 * (1.0 - g0)
        d1 = dmv * b_ref[...] * g1 * (1.0 - g1)
        dgl_ref[:, :D] = d0.astype(BF16)
        dgl_ref[:, D:] = d1.astype(BF16)
        s0 = jnp.sum(d0, axis=0, keepdims=True)
        s1 = jnp.sum(d1, axis=0, keepdims=True)

        @pl.when(i == 0)
        def _():
            dbg_ref[:, :D] = s0
            dbg_ref[:, D:] = s1

        @pl.when(i > 0)
        def _():
            dbg_ref[:, :D] += s0
            dbg_ref[:, D:] += s1

    row = pl.BlockSpec((tm, D), lambda i: (i, 0))
    wide = pl.BlockSpec((tm, 2 * D), lambda i: (i, 0))
    bg = pl.BlockSpec((1, 2 * D), lambda i: (0, 0))
    return pl.pallas_call(
        body, name=name, grid=(S // tm,),
        in_specs=[row, row, row, row, pl.BlockSpec((tm, D), lambda i: (i, 1)), bg],
        out_specs=[row, row, wide, bg],
        out_shape=[jax.ShapeDtypeStruct((S, D), BF16), jax.ShapeDtypeStruct((S, D), BF16),
                   jax.ShapeDtypeStruct((S, 2 * D), BF16), jax.ShapeDtypeStruct((1, 2 * D), F32)],
        compiler_params=_cp(("arbitrary",)),
    )(dm, a, b, gl, gl, b_gates)


def _adam_math(w, g, m, v):
    m2 = ADAM_B1 * m + (1.0 - ADAM_B1) * g
    v2 = ADAM_B2 * v + (1.0 - ADAM_B2) * (g * g)
    m_hat = m2 / (1.0 - ADAM_B1 ** ADAM_STEP)
    v_hat = v2 / (1.0 - ADAM_B2 ** ADAM_STEP)
    delta = -ADAM_LR * (m_hat / (jnp.sqrt(v_hat) + ADAM_EPS) + ADAM_WD * w)
    return delta, m2, v2


def _pick_rows(rows, cols, unit=16, budget=MIB):
    best = unit
    for t in range(unit, rows + 1, unit):
        if rows % t == 0 and t * cols * 4 <= budget:
            best = t
    assert rows % best == 0
    return best


def _adamw(w, g, m, v, name):
    r, c = w.shape
    tr = _pick_rows(r, c, unit=8)

    def body(w_ref, g_ref, m_ref, v_ref, go_ref, d_ref, mo_ref, vo_ref):
        gv = g_ref[...]
        delta, m2, v2 = _adam_math(w_ref[...], gv, m_ref[...], v_ref[...])
        go_ref[...] = gv
        d_ref[...] = delta
        mo_ref[...] = m2
        vo_ref[...] = v2

    blk = pl.BlockSpec((tr, c), lambda i: (i, 0))
    shp = jax.ShapeDtypeStruct((r, c), F32)
    return pl.pallas_call(
        body, name=name, grid=(r // tr,), in_specs=[blk] * 4, out_specs=[blk] * 4, out_shape=[shp] * 4,
        compiler_params=_cp(("parallel",)),
    )(w, g, m, v)


def _small_sum_adamw(parts, w, m, v, name):
    tr = 48

    def body(p_ref, w_ref, m_ref, v_ref, g_ref, d_ref, mo_ref, vo_ref):
        gv = p_ref[0]
        for k in range(1, 8):
            gv = gv + p_ref[k]
        delta, m2, v2 = _adam_math(w_ref[...], gv, m_ref[...], v_ref[...])
        g_ref[...] = gv
        d_ref[...] = delta
        mo_ref[...] = m2
        vo_ref[...] = v2

    blk = pl.BlockSpec((tr, D), lambda i: (i, 0))
    shp = jax.ShapeDtypeStruct((SMALL_ROWS, D), F32)
    return pl.pallas_call(
        body, name=name, grid=(SMALL_ROWS // tr,),
        in_specs=[pl.BlockSpec((8, tr, D), lambda i: (0, i, 0)), blk, blk, blk],
        out_specs=[blk] * 4, out_shape=[shp] * 4,
        compiler_params=_cp(("parallel",)),
    )(parts, w, m, v)


ANY = pl.BlockSpec(memory_space=pl.ANY)


def _mesh_pos():
    x, y, c = lax.axis_index("x"), lax.axis_index("y"), lax.axis_index("c")
    chips = [(1 - x, y), (x, 1 - y), (1 - x, 1 - y)]
    return x, y, c, chips


def _place_shard(w, kind, pos, name):
    r, c = w.shape
    tr = _pick_rows(r, c)

    def body(pos_ref, w_ref, o_ref):
        o_ref[...] = w_ref[...].astype(BF16)

    if kind == "stack":
        o_spec = pl.BlockSpec((None, tr, c), lambda i, p: (p[1], i, 0))
        shape = (NSH, r, c)
    else:
        o_spec = pl.BlockSpec((tr, c), lambda i, p: (i, p[1]))
        shape = (r, NSH * c)
    return pl.pallas_call(
        body, name=name,
        grid_spec=pltpu.PrefetchScalarGridSpec(
            num_scalar_prefetch=1, grid=(r // tr,),
            in_specs=[pl.BlockSpec((tr, c), lambda i, p: (i, 0))], out_specs=o_spec),
        out_shape=jax.ShapeDtypeStruct(shape, BF16),
        compiler_params=_cp(("parallel",)),
    )(pos, w)


SEM = pl.BlockSpec(memory_space=pltpu.SEMAPHORE)
SPLIT_COPY = pltpu.CompilerParams(has_side_effects=pltpu.SideEffectType.DATAFLOW_SIDE_EFFECTING)


def _shard_window(ref, kind, j, h, dims):
    r, c = dims
    rows = pl.ds(pl.multiple_of(h * (r // 2), 16), r // 2)
    if kind == "stack":
        return ref.at[j, rows, :]
    return ref.at[rows, pl.ds(pl.multiple_of(j * c, 128), c)]


def _ici_copy(ref, kind, dims, j, c, sems, idx, to):
    win = _shard_window(ref, kind, j, c, dims)
    return pltpu.make_async_remote_copy(src_ref=win, dst_ref=win, send_sem=sems[0].at[idx], recv_sem=sems[1].at[idx],
                                        device_id=to, device_id_type=MESH_T)


def _gather_start(fulls, kinds, dims, after, name):
    n, na = len(fulls), len(after)

    def body(*refs):
        outs = refs[n + na:2 * n + na]
        send_sems, recv_sems, token = refs[2 * n + na:]
        x, y, c, chips = _mesh_pos()
        for a in range(n):
            for k, chip in enumerate(chips):
                _ici_copy(outs[a], kinds[a], dims[a], 2 * x + y, c, (send_sems, recv_sems), 3 * a + k,
                          (chip[0], chip[1], c)).start()
        token[...] = jnp.zeros_like(token)

    res = pl.pallas_call(
        body, name=name, in_specs=[ANY] * (n + na),
        out_specs=[ANY] * n + [SEM, SEM, pl.BlockSpec(memory_space=pltpu.VMEM)],
        out_shape=[jax.ShapeDtypeStruct(f.shape, BF16) for f in fulls]
        + [pltpu.SemaphoreType.DMA((3 * n,)), pltpu.SemaphoreType.DMA((3 * n,)), jax.ShapeDtypeStruct((8, 128), F32)],
        input_output_aliases={i: i for i in range(n)},
        compiler_params=SPLIT_COPY,
    )(*fulls, *after)
    return res[:n], res[n], res[n + 1], res[n + 2]


def _gather_wait(fulls, send_sems, recv_sems, kinds, dims, after, name):
    n, na = len(fulls), len(after)

    def body(*refs):
        ssem, rsem = refs[n], refs[n + 1]
        outs = refs[n + 2 + na:]
        x, y, c, chips = _mesh_pos()
        for a in range(n):
            for k, chip in enumerate(chips):
                to = (chip[0], chip[1], c)
                _ici_copy(outs[a], kinds[a], dims[a], 2 * x + y, c, (ssem, rsem), 3 * a + k, to).wait_send()
                _ici_copy(outs[a], kinds[a], dims[a], 2 * chip[0] + chip[1], c, (ssem, rsem), 3 * a + k, to).wait_recv()

    return pl.pallas_call(
        body, name=name, in_specs=[ANY] * n + [SEM, SEM] + [ANY] * na, out_specs=[ANY] * n,
        out_shape=[jax.ShapeDtypeStruct(f.shape, BF16) for f in fulls],
        input_output_aliases={i: i for i in range(n)},
        compiler_params=SPLIT_COPY,
    )(*fulls, send_sems, recv_sems, *after)


def _gather_forward(fulls, kinds, dims, name):
    n = len(fulls)

    def body(*refs):
        outs = refs[n:2 * n]
        sems = refs[2 * n:]
        x, y, c, chips = _mesh_pos()
        sib = (x, y, 1 - c)
        cps = []
        for a in range(n):
            for k, chip in enumerate(chips):
                cp = _ici_copy(outs[a], kinds[a], dims[a], 2 * chip[0] + chip[1], c, sems, 3 * a + k, sib)
                cp.start()
                cps.append(cp)
        for a in range(n):
            for k, chip in enumerate(chips):
                _ici_copy(outs[a], kinds[a], dims[a], 2 * chip[0] + chip[1], 1 - c, sems, 3 * a + k, sib).wait_recv()
        for cp in cps:
            cp.wait_send()

    return pl.pallas_call(
        body, name=name, in_specs=[ANY] * n, out_specs=[ANY] * n,
        out_shape=[jax.ShapeDtypeStruct(f.shape, BF16) for f in fulls],
        input_output_aliases={i: i for i in range(n)},
        scratch_shapes=[pltpu.SemaphoreType.DMA((3 * n,)), pltpu.SemaphoreType.DMA((3 * n,))],
    )(*fulls)


def _pair_exchange(grads, name):
    n = len(grads)

    def body(*refs):
        ins, outs = refs[:n], refs[n:2 * n]
        send_sems, recv_sems = refs[2 * n:]
        x, y, c, _ = _mesh_pos()
        cps = []
        for a in range(n):
            cp = pltpu.make_async_remote_copy(
                src_ref=ins[a].at[1 - c], dst_ref=outs[a], send_sem=send_sems.at[a], recv_sem=recv_sems.at[a],
                device_id=(x, y, 1 - c), device_id_type=MESH_T)
            cp.start()
            cps.append(cp)
        for cp in cps:
            cp.wait()

    return pl.pallas_call(
        body, name=name, in_specs=[ANY] * n, out_specs=[ANY] * n,
        out_shape=[jax.ShapeDtypeStruct(g.shape[1:], F32) for g in grads],
        scratch_shapes=[pltpu.SemaphoreType.DMA((n,)), pltpu.SemaphoreType.DMA((n,))],
    )(*grads)


def _pair_sum(g, recv, pos, name):
    _, _, rh, c = g.shape
    tr = _pick_rows(rh, c)

    def body(pos_ref, g_ref, r_ref, o_ref):
        o_ref[...] = (g_ref[...] + r_ref[...]).astype(BF16)

    return pl.pallas_call(
        body, name=name,
        grid_spec=pltpu.PrefetchScalarGridSpec(
            num_scalar_prefetch=1, grid=(NSH, rh // tr),
            in_specs=[pl.BlockSpec((None, None, tr, c), lambda j, r, p: (p[0], j, r, 0)),
                      pl.BlockSpec((None, tr, c), lambda j, r, p: (j, r, 0))],
            out_specs=pl.BlockSpec((None, tr, c), lambda j, r, p: (j, r, 0))),
        out_shape=jax.ShapeDtypeStruct((NSH, rh, c), BF16),
        compiler_params=_cp(("parallel", "parallel")),
    )(pos, g, recv)


def _chip_copy(src, land, a, k, chip, c, sems):
    return pltpu.make_async_remote_copy(
        src_ref=src.at[2 * chip[0] + chip[1]], dst_ref=land.at[k], send_sem=sems[0].at[3 * a + k],
        recv_sem=sems[1].at[3 * a + k], device_id=(chip[0], chip[1], c), device_id_type=MESH_T)


def _chip_start(psums, lands, name):
    n = len(psums)

    def body(*refs):
        srcs, dsts = refs[2 * n:3 * n], refs[3 * n:4 * n]
        send_sems, recv_sems, token = refs[4 * n:]
        x, y, c, chips = _mesh_pos()
        for a in range(n):
            for k, chip in enumerate(chips):
                _chip_copy(srcs[a], dsts[a], a, k, chip, c, (send_sems, recv_sems)).start()
        token[...] = jnp.zeros_like(token)

    res = pl.pallas_call(
        body, name=name, in_specs=[ANY] * (2 * n),
        out_specs=[ANY] * (2 * n) + [SEM, SEM, pl.BlockSpec(memory_space=pltpu.VMEM)],
        out_shape=[jax.ShapeDtypeStruct(p.shape, BF16) for p in psums]
        + [jax.ShapeDtypeStruct(l.shape, BF16) for l in lands]
        + [pltpu.SemaphoreType.DMA((3 * n,)), pltpu.SemaphoreType.DMA((3 * n,)), jax.ShapeDtypeStruct((8, 128), F32)],
        input_output_aliases={i: i for i in range(2 * n)},
        compiler_params=SPLIT_COPY,
    )(*psums, *lands)
    return res[:n], res[n:2 * n], res[2 * n], res[2 * n + 1], res[2 * n + 2]


def _chip_wait(psums, lands, send_sems, recv_sems, after, name):
    n, na = len(psums), len(after)

    def body(*refs):
        ssem, rsem = refs[2 * n], refs[2 * n + 1]
        outs = refs[2 * n + 2 + na:]
        srcs, dsts = outs[:n], outs[n:]
        x, y, c, chips = _mesh_pos()
        for a in range(n):
            for k, chip in enumerate(chips):
                cp = _chip_copy(srcs[a], dsts[a], a, k, chip, c, (ssem, rsem))
                cp.wait_send()
                cp.wait_recv()

    res = pl.pallas_call(
        body, name=name, in_specs=[ANY] * (2 * n) + [SEM, SEM] + [ANY] * na, out_specs=[ANY] * (2 * n),
        out_shape=[jax.ShapeDtypeStruct(p.shape, BF16) for p in psums]
        + [jax.ShapeDtypeStruct(l.shape, BF16) for l in lands],
        input_output_aliases={i: i for i in range(2 * n)},
        compiler_params=SPLIT_COPY,
    )(*psums, *lands, send_sems, recv_sems, *after)
    return res[n:]


def _owner_sum(g, recv_a, recv_b, pos, name):
    _, _, rh, c = g.shape
    tr = _pick_rows(rh, c)

    def body(pos_ref, g_ref, ra_ref, rb_ref, o_ref):
        acc = g_ref[...] + ra_ref[...]
        for k in range(3):
            acc = acc + rb_ref[k].astype(F32)
        o_ref[...] = acc

    return pl.pallas_call(
        body, name=name,
        grid_spec=pltpu.PrefetchScalarGridSpec(
            num_scalar_prefetch=1, grid=(rh // tr,),
            in_specs=[pl.BlockSpec((None, None, tr, c), lambda r, p: (p[0], p[1], r, 0)),
                      pl.BlockSpec((None, tr, c), lambda r, p: (p[1], r, 0)),
                      pl.BlockSpec((3, tr, c), lambda r, p: (0, r, 0))],
            out_specs=pl.BlockSpec((None, tr, c), lambda r, p: (p[0], r, 0))),
        out_shape=jax.ShapeDtypeStruct((2, rh, c), F32),
        compiler_params=_cp(("parallel",)),
    )(pos, g, recv_a, recv_b)


def _sibling_allgather(halves, name):
    n = len(halves)

    def body(*refs):
        outs = refs[n:2 * n]
        send_sems, recv_sems = refs[2 * n:]
        x, y, c, _ = _mesh_pos()
        cps = []
        for a in range(n):
            cp = pltpu.make_async_remote_copy(
                src_ref=outs[a].at[c], dst_ref=outs[a].at[c], send_sem=send_sems.at[a], recv_sem=recv_sems.at[a],
                device_id=(x, y, 1 - c), device_id_type=MESH_T)
            cp.start()
            cps.append(cp)
        for a in range(n):
            cps[a].wait_send()
            pltpu.make_async_remote_copy(
                src_ref=outs[a].at[1 - c], dst_ref=outs[a].at[1 - c], send_sem=send_sems.at[a],
                recv_sem=recv_sems.at[a], device_id=(x, y, 1 - c), device_id_type=MESH_T).wait_recv()

    return pl.pallas_call(
        body, name=name, in_specs=[ANY] * n, out_specs=[ANY] * n,
        out_shape=[jax.ShapeDtypeStruct(h.shape, F32) for h in halves],
        input_output_aliases={i: i for i in range(n)},
        scratch_shapes=[pltpu.SemaphoreType.DMA((n,)), pltpu.SemaphoreType.DMA((n,))],
    )(*halves)


def _small_allgather(part, after):
    m_per = SMALL_ROWS
    na = len(after)

    def body(x_ref, *refs):
        out_ref, send_sems, recv_sems, local_sem = refs[na:]
        x, y, c, chips = _mesh_pos()
        me, sibling = (x, y, c), (x, y, 1 - c)

        def rows(px, py, pc):
            return out_ref.at[pl.ds((4 * px + 2 * py + pc) * m_per, m_per), :]

        def copy(k, block, to, src=None):
            return pltpu.make_async_remote_copy(
                src_ref=rows(*block) if src is None else src, dst_ref=rows(*block),
                send_sem=send_sems.at[k], recv_sem=recv_sems.at[k], device_id=to, device_id_type=MESH_T)

        mine = pltpu.make_async_copy(x_ref, rows(*me), local_sem)
        mine.start()
        first = [copy(0, me, sibling, src=x_ref)]
        first += [copy(1 + j, me, (*chip, c), src=x_ref) for j, chip in enumerate(chips)]
        for cp in first:
            cp.start()
        passed = [copy(4 + j, (*chip, c), sibling) for j, chip in enumerate(chips)]
        for j, chip in enumerate(chips):
            copy(1 + j, (*chip, c), me).wait_recv()
            passed[j].start()
        copy(0, sibling, me).wait_recv()
        for j, chip in enumerate(chips):
            copy(4 + j, (*chip, 1 - c), me).wait_recv()
        for cp in first + passed:
            cp.wait_send()
        mine.wait()

    return pl.pallas_call(
        body, name="small_allgather",
        out_shape=jax.ShapeDtypeStruct((8 * m_per, D), F32),
        in_specs=[pl.BlockSpec(memory_space=pltpu.VMEM)] + [ANY] * na, out_specs=pl.BlockSpec(memory_space=pltpu.VMEM),
        scratch_shapes=[pltpu.SemaphoreType.DMA((7,)), pltpu.SemaphoreType.DMA((7,)), pltpu.SemaphoreType.DMA],
    )(part, *after)


def _pack_small(ln1_g, ln1_b, gln_g, gln_b, ln2_g, ln2_b, ln3_g, ln3_b, b_gates, b_s, w_s):
    rows = [ln1_g, ln1_b, gln_g, gln_b, ln2_g, ln2_b, ln3_g, ln3_b]
    rows = [r.reshape(1, D) for r in rows] + [b_gates.reshape(2, D), b_s.reshape(1, D), jnp.zeros((5, D), F32),
                                             w_s.reshape(128, D)]
    return jnp.concatenate(rows, axis=0)


def _unpack_small(p):
    out = [p[i:i + 1] for i in range(8)]
    return out + [p[8:10].reshape(1, 2 * D), p[10:11].reshape(1, 8, BLK), p[16:144].reshape(1, 8, BLK, BLK)]


GROUPS = (("f1g", "f1u", "f1d"), ("w_in",), ("w_ab", "w_gb", "w_out"), ("f2g", "f2u", "f2d"))


def _local_step(x, pos_f, target, P, weights_of, grads_ready):
    invf = ROPE_THETA ** (-jnp.arange(0, DH, 2, dtype=F32) / DH)
    invf = jnp.tile(invf, 4).reshape(1, 128)
    b_s_t = P["gmlp_b_s"].T

    W = dict(weights_of(0, []))
    h1, h1b, xh1, rstd1, a1, b1, h1t = _ffn_fwd(x, W["f1g"], W["f1u"], W["f1d"], P["ln1_g"], P["ln1_b"], "ffn1_fwd",
                                                emit_t=True)
    W.update(weights_of(1, [h1b]))
    qkv = _matmul(h1b, W["w_in"], "nn", "proj_qkv", n=3 * ATT_W, b_col0=0)
    z = _matmul(h1b, W["w_in"], "nn", "proj_z", n=2 * GW, b_col0=3 * ATT_W)
    gl = _matmul(h1b, W["w_in"], "nn", "proj_gates", n=2 * D, b_col0=3 * ATT_W + 2 * GW)
    qkv_c = _rope_fwd(qkv, pos_f, invf, "rope_fwd")
    og = [_attn_fwd(gi, qkv_c[gi], "attn_fwd_g%d" % gi) for gi in range(NG)]
    y_attn, y_attn_t, lse = _attn_combine([o for o, _ in og], [l for _, l in og], "attn_combine")
    y_gmlp, y_gmlp_t = _gmlp_fwd(z, P["gmlp_ln_g"], P["gmlp_ln_b"], P["gmlp_w_s"], b_s_t, "gmlp_fwd")
    W.update(weights_of(2, [y_gmlp]))
    br_a = _matmul(y_attn, W["w_ab"], "nn", "branch_attn", n=D)
    br_b = _matmul(y_gmlp, W["w_gb"], "nn", "branch_gmlp", n=D)
    merged, merged_t = _merge_fwd(br_a, br_b, gl, P["b_gates"], "merge_fwd")
    mix = _matmul(merged, W["w_out"], "nn", "mix_out", n=D)
    h2, h2b, xh2, rstd2 = _resid_ln(h1, mix, P["ln2_g"], P["ln2_b"], "resid_ln2")
    W.update(weights_of(3, [h2b]))
    y, _, xh3, rstd3, a2, b2 = _ffn_fwd(h2, W["f2g"], W["f2u"], W["f2d"], P["ln3_g"], P["ln3_b"], "ffn2_fwd")

    dr3, dg3, db3, loss = _ln_bwd(y, xh3, rstd3, P["ln3_g"], "loss_ln3_bwd", target=target)
    g_f2g, g_f2u, g_f2d, dh2 = _ffn_bwd(dr3, h2b, a2, b2, W["f2g"], W["f2u"], W["f2d"], "ffn2_bwd")
    grads_ready(3, dict(f2g=g_f2g, f2u=g_f2u, f2d=g_f2d))
    dr2, dg2, db2 = _ln_bwd(dh2, xh2, rstd2, P["ln2_g"], "ln2_bwd")
    g_wout = _wgrad(merged_t, dr2, 128, D, "dw_out", row_sharded=True)
    dmerged = _matmul(dr2, W["w_out"], "nt", "dmerged", n=D)
    dab, dbb, dglb, dbg = _merge_bwd(dmerged, br_a, br_b, gl, P["b_gates"], "merge_bwd")
    g_wab = _wgrad(y_attn_t, dab, GRP_W // 2, 256, "dw_attn_branch", row_sharded=False)
    g_wgb = _wgrad(y_gmlp_t, dbb, 128, D, "dw_gmlp_branch", row_sharded=True)
    grads_ready(2, dict(w_ab=g_wab, w_gb=g_wgb, w_out=g_wout))
    dy_attn = _matmul(dab, W["w_ab"], "nt", "dy_attn", n=GRP_W)
    dy_gmlp = _matmul(dbb, W["w_gb"], "nt", "dy_gmlp", n=GW)
    dzb, dws, dbs_t, dgln_g, dgln_b = _gmlp_bwd(z, dy_gmlp, P["gmlp_ln_g"], P["gmlp_ln_b"], P["gmlp_w_s"], b_s_t,
                                                 "gmlp_bwd")
    cls = _class_order([dy_attn, y_attn, lse], "attn_class_order")
    dqkv_c = []
    for gi in range(NG):
        dy_c, y_c, lse_c = [t[None] if gi == 0 else cls[2 * a + gi - 1] for a, t in enumerate((dy_attn, y_attn, lse))]
        dqkv_c.append(_attn_bwd(gi, qkv_c[gi], dy_c, y_c, lse_c, "attn_bwd_g%d" % gi))
    dqkvb = _rope_bwd(dqkv_c, pos_f, invf, "rope_bwd")
    dproj = jnp.concatenate([dqkvb, dzb, dglb], axis=1)
    g_win = _wgrad(h1t, dproj, D // 2, IN_SH, "dw_in", row_sharded=False)
    grads_ready(1, dict(w_in=g_win))
    dh1 = _matmul(dproj, W["w_in"], "nt", "dh1", n=D, tn=D, tk=IN_SH, add=dr2, add_scale=ALPHA)
    dr1, dg1, db1 = _ln_bwd(dh1, xh1, rstd1, P["ln1_g"], "ln1_bwd")
    g_f1g, g_f1u, g_f1d, dx = _ffn_bwd(dr1, x.astype(BF16), a1, b1, W["f1g"], W["f1u"], W["f1d"], "ffn1_bwd")
    grads_ready(0, dict(f1g=g_f1g, f1u=g_f1u, f1d=g_f1d))

    small = _pack_small(dg1, db1, dgln_g, dgln_b, dg2, db2, dg3, db3, dbg, dbs_t.T, dws)
    return loss, dx, small


BIG = ("f1g", "f1u", "f1d", "w_in", "w_ab", "w_gb", "w_out", "f2g", "f2u", "f2d")
TRANSPOSED = ("f1g", "f1u", "f2g", "f2u")
KIND = dict(f1g="stack", f1u="stack", f1d="stack", w_in="col", w_ab="col", w_gb="stack", w_out="stack",
            f2g="stack", f2u="stack", f2d="stack")


def kernel(x, positions, ffn1_w_gate, ffn1_w_up, ffn1_w_down, ln1_g, ln1_b, w_in, b_gates, gmlp_ln_g, gmlp_ln_b, gmlp_w_s, gmlp_b_s, w_attn_branch, w_gmlp_branch, w_out, ln2_g, ln2_b, ffn2_w_gate, ffn2_w_up, ffn2_w_down, ln3_g, ln3_b, loss_target, m_ffn1_w_gate, m_ffn1_w_up, m_ffn1_w_down, m_ln1_g, m_ln1_b, m_w_in, m_b_gates, m_gmlp_ln_g, m_gmlp_ln_b, m_gmlp_w_s, m_gmlp_b_s, m_w_attn_branch, m_w_gmlp_branch, m_w_out, m_ln2_g, m_ln2_b, m_ffn2_w_gate, m_ffn2_w_up, m_ffn2_w_down, m_ln3_g, m_ln3_b, v_ffn1_w_gate, v_ffn1_w_up, v_ffn1_w_down, v_ln1_g, v_ln1_b, v_w_in, v_b_gates, v_gmlp_ln_g, v_gmlp_ln_b, v_gmlp_w_s, v_gmlp_b_s, v_w_attn_branch, v_w_gmlp_branch, v_w_out, v_ln2_g, v_ln2_b, v_ffn2_w_gate, v_ffn2_w_up, v_ffn2_w_down, v_ln3_g, v_ln3_b):
    cx, cy, cc = lax.axis_index("x"), lax.axis_index("y"), lax.axis_index("c")
    pos = jnp.stack([cc, 2 * cx + cy]).astype(jnp.int32)

    w_sh = dict(f1g=ffn1_w_gate, f1u=ffn1_w_up, f1d=ffn1_w_down, w_in=w_in, w_ab=w_attn_branch,
                w_gb=w_gmlp_branch, w_out=w_out, f2g=ffn2_w_gate, f2u=ffn2_w_up, f2d=ffn2_w_down)
    m_sh = dict(f1g=m_ffn1_w_gate, f1u=m_ffn1_w_up, f1d=m_ffn1_w_down, w_in=m_w_in, w_ab=m_w_attn_branch,
                w_gb=m_w_gmlp_branch, w_out=m_w_out, f2g=m_ffn2_w_gate, f2u=m_ffn2_w_up, f2d=m_ffn2_w_down)
    v_sh = dict(f1g=v_ffn1_w_gate, f1u=v_ffn1_w_up, f1d=v_ffn1_w_down, w_in=v_w_in, w_ab=v_w_attn_branch,
                w_gb=v_w_gmlp_branch, w_out=v_w_out, f2g=v_ffn2_w_gate, f2u=v_ffn2_w_up, f2d=v_ffn2_w_down)
    w_sh = {k: (v[0].T if k in TRANSPOSED else v[0]) for k, v in w_sh.items()}
    m_sh = {k: (v[0].T if k in TRANSPOSED else v[0]) for k, v in m_sh.items()}
    v_sh = {k: (v[0].T if k in TRANSPOSED else v[0]) for k, v in v_sh.items()}

    started, tokens = [], []
    for gi, names in enumerate(GROUPS):
        placed = [_place_shard(w_sh[k], KIND[k], pos, "place_" + k) for k in names]
        fulls, ssem, rsem, token = _gather_start(placed, [KIND[k] for k in names], [w_sh[k].shape for k in names],
                                                 tokens[-1:], "gather_start_g%d" % gi)
        started.append((fulls, ssem, rsem))
        tokens.append(token)

    def weights_of(gi, after):
        names = GROUPS[gi]
        kinds, dims = [KIND[k] for k in names], [w_sh[k].shape for k in names]
        fulls, ssem, rsem = started[gi]
        fulls = _gather_wait(fulls, ssem, rsem, kinds, dims, list(after) + (tokens if gi == 0 else []),
                             "gather_wait_g%d" % gi)
        fulls = _gather_forward(fulls, kinds, dims, "gather_forward_g%d" % gi)
        return {k: (f.reshape(D, D) if k in ("w_gb", "w_out") else f) for k, f in zip(names, fulls)}

    inflight = {}

    def grads_ready(gi, gd):
        names = GROUPS[gi]
        grads = [gd[k] for k in names]
        recv_a = _pair_exchange(grads, "rs_pair_exchange_g%d" % gi)
        psums = [_pair_sum(g, r, pos, "rs_pair_sum_" + k) for g, r, k in zip(grads, recv_a, names)]
        lands = [lax.empty((3,) + p.shape[1:], BF16) for p in psums]
        psums, lands, ssem, rsem, token = _chip_start(psums, lands, "rs_chip_start_g%d" % gi)
        inflight[gi] = (grads, recv_a, psums, lands, ssem, rsem, token)

    P = dict(ln1_g=ln1_g, ln1_b=ln1_b, ln2_g=ln2_g, ln2_b=ln2_b, ln3_g=ln3_g, ln3_b=ln3_b, b_gates=b_gates,
             gmlp_ln_g=gmlp_ln_g, gmlp_ln_b=gmlp_ln_b, gmlp_w_s=gmlp_w_s[0], gmlp_b_s=gmlp_b_s[0])
    pos_f = positions.reshape(S, 1).astype(F32)
    loss_part, dx, small = _local_step(x[0], pos_f, loss_target[0], P, weights_of, grads_ready)
    loss = lax.psum(loss_part[0, 0], ("x", "y", "c"))

    g_out, d_out, m_out, v_out = {}, {}, {}, {}

    def finish(gi, after):
        grads, recv_a, psums, lands, ssem, rsem, token = inflight[gi]
        recv_b = _chip_wait(psums, lands, ssem, rsem, after + [inflight[0][6]], "rs_chip_wait_g%d" % gi)
        halves = [_owner_sum(g, ra, rb, pos, "rs_owner_sum_" + k)
                  for g, ra, rb, k in zip(grads, recv_a, recv_b, GROUPS[gi])]
        reduced = _sibling_allgather(halves, "rs_sibling_allgather_g%d" % gi)
        for k, gfull in zip(GROUPS[gi], reduced):
            res = _adamw(w_sh[k], gfull.reshape(w_sh[k].shape), m_sh[k], v_sh[k], "adamw_" + k)
            after = [res[1]]
            if k in TRANSPOSED:
                res = [r.T for r in res]
            g_out[k], d_out[k], m_out[k], v_out[k] = [r[None] for r in res]
        return after

    after = []
    for gi in (3, 2, 1):
        after = finish(gi, after)

    parts = _small_allgather(small, after).reshape(8, SMALL_ROWS, D)
    sp = (ln1_g, ln1_b, gmlp_ln_g, gmlp_ln_b, ln2_g, ln2_b, ln3_g, ln3_b, b_gates, gmlp_b_s, gmlp_w_s)
    sm = (m_ln1_g, m_ln1_b, m_gmlp_ln_g, m_gmlp_ln_b, m_ln2_g, m_ln2_b, m_ln3_g, m_ln3_b, m_b_gates, m_gmlp_b_s,
          m_gmlp_w_s)
    sv = (v_ln1_g, v_ln1_b, v_gmlp_ln_g, v_gmlp_ln_b, v_ln2_g, v_ln2_b, v_ln3_g, v_ln3_b, v_b_gates, v_gmlp_b_s,
          v_gmlp_w_s)
    sg, sd, smn, svn = _small_sum_adamw(parts, _pack_small(*sp), _pack_small(*sm), _pack_small(*sv), "small_adamw")
    names = ("ln1_g", "ln1_b", "gmlp_ln_g", "gmlp_ln_b", "ln2_g", "ln2_b", "ln3_g", "ln3_b", "b_gates", "gmlp_b_s",
             "gmlp_w_s")
    for dst, packed in ((g_out, sg), (d_out, sd), (m_out, smn), (v_out, svn)):
        for nm, val in zip(names, _unpack_small(packed)):
            dst[nm] = val
    finish(0, [sg])

    order = ("f1g", "f1u", "f1d", "ln1_g", "ln1_b", "w_in", "b_gates", "gmlp_ln_g", "gmlp_ln_b", "gmlp_w_s", "gmlp_b_s",
             "w_ab", "w_gb", "w_out", "ln2_g", "ln2_b", "f2g", "f2u", "f2d", "ln3_g", "ln3_b")
    outs = [loss, dx[None]]
    for dst in (g_out, d_out, m_out, v_out):
        outs += [dst[k] for k in order]
    return tuple(outs)
```

```python
import functools
import math

import jax
import jax.numpy as jnp
from jax import lax
from jax.experimental import pallas as pl
from jax.experimental.pallas import tpu as pltpu

F32 = jnp.float32
BF16 = jnp.bfloat16

S = 2048
D = 1024
NSH = 4
FSH = 704
ATT_W = 1536
GRP_W = 512
NG = 3
NH = 8
DH = 64
BLK = 128
NBLK = S // BLK
GW = 1024
IN_W = 8704
IN_SH = IN_W // NSH
ALPHA = 2.0 ** 0.25
LN_EPS = 1e-5
ROPE_THETA = 10000.0
DILATIONS = (1, 4, 16)
ADAM_LR, ADAM_B1, ADAM_B2, ADAM_EPS, ADAM_WD, ADAM_STEP = 0.001, 0.9, 0.999, 1e-08, 0.01, 10
SMALL_ROWS = 144
MESH_T = pl.DeviceIdType.MESH
MIB = 1024 * 1024
NEG_INF = float("-inf")


def _cp(sem, vmem_mib=48):
    return pltpu.CompilerParams(dimension_semantics=sem, vmem_limit_bytes=vmem_mib * MIB)


def _ln_stats(r):
    mu = jnp.mean(r, axis=-1, keepdims=True)
    xc = r - mu
    var = jnp.mean(xc * xc, axis=-1, keepdims=True)
    rstd = lax.rsqrt(var + LN_EPS)
    return xc * rstd, rstd


def _ln_dx(dxh, xh, rstd):
    m1 = jnp.mean(dxh, axis=-1, keepdims=True)
    m2 = jnp.mean(dxh * xh, axis=-1, keepdims=True)
    return rstd * (dxh - m1 - xh * m2)


def _dot_nt(a, b):
    return lax.dot_general(a, b, (((1,), (1,)), ((), ())), preferred_element_type=F32)


def _dot_tn(a, b):
    return lax.dot_general(a, b, (((0,), (0,)), ((), ())), preferred_element_type=F32)


def _dot(a, b):
    return jnp.dot(a, b, preferred_element_type=F32)


def _ffn_fwd(xin, wgt, wut, wd, ln_g, ln_b, name, emit_t=False):
    tm = 512

    def body(x_ref, wg_ref, wu_ref, wd_ref, g_ref, b_ref, *rest):
        if emit_t:
            h_ref, hb_ref, xh_ref, rstd_ref, a_ref, bb_ref, ht_ref, acc_ref = rest
        else:
            h_ref, hb_ref, xh_ref, rstd_ref, a_ref, bb_ref, acc_ref = rest
        j = pl.program_id(1)
        xb = x_ref[...].astype(BF16)
        a = _dot_nt(xb, wg_ref[...])
        b = _dot_nt(xb, wu_ref[...])
        a_ref[...] = a
        bb_ref[...] = b
        s = (a * jax.nn.sigmoid(a)) * b
        f = _dot(s.astype(BF16), wd_ref[...])

        @pl.when(j == 0)
        def _():
            acc_ref[...] = f

        @pl.when(j > 0)
        def _():
            acc_ref[...] += f

        @pl.when(j == NSH - 1)
        def _():
            r = ALPHA * x_ref[...] + 0.5 * acc_ref[...]
            xh, rstd = _ln_stats(r)
            h = xh * g_ref[...] + b_ref[...]
            h_ref[...] = h
            hb_ref[...] = h.astype(BF16)
            xh_ref[...] = xh
            rstd_ref[...] = rstd
            if emit_t:
                ht_ref[...] = h.T.astype(BF16)

    row = pl.BlockSpec((tm, D), lambda i, j: (i, 0))
    vec = pl.BlockSpec((1, D), lambda i, j: (0, 0))
    wsp = pl.BlockSpec((None, FSH, D), lambda i, j: (j, 0, 0))
    ab = pl.BlockSpec((None, tm, FSH), lambda i, j: (j, i, 0))
    out_specs = [row, row, row, pl.BlockSpec((tm, 1), lambda i, j: (i, 0)), ab, ab]
    out_shape = [jax.ShapeDtypeStruct((S, D), F32), jax.ShapeDtypeStruct((S, D), BF16),
                 jax.ShapeDtypeStruct((S, D), F32), jax.ShapeDtypeStruct((S, 1), F32),
                 jax.ShapeDtypeStruct((NSH, S, FSH), F32), jax.ShapeDtypeStruct((NSH, S, FSH), F32)]
    if emit_t:
        out_specs.append(pl.BlockSpec((D, tm), lambda i, j: (0, i)))
        out_shape.append(jax.ShapeDtypeStruct((D, S), BF16))
    return pl.pallas_call(
        body, name=name, grid=(S // tm, NSH),
        in_specs=[row, wsp, wsp, wsp, vec, vec], out_specs=out_specs, out_shape=out_shape,
        scratch_shapes=[pltpu.VMEM((tm, D), F32)],
        compiler_params=_cp(("parallel", "arbitrary")),
    )(xin, wgt, wut, wd, ln_g, ln_b)


def _ffn_bwd(dr, xin_b, a, b, wgt, wut, wd, name):
    tm = 256
    ni = S // tm
    hr = FSH // 2

    def body(dr_ref, a_ref, b_ref, wg_ref, wu_ref, wd_ref, x_hbm, dwg_hbm, dwu_hbm, dwd_hbm, dx_hbm,
             dx_acc, da_all, db_all, s_all, df_all, x_all, res_buf, sems):
        j = pl.program_id(0)
        i = pl.program_id(1)
        rows = pl.ds(pl.multiple_of(i * tm, tm), tm)

        @pl.when(jnp.logical_and(j == 0, i == 0))
        def _():
            cp = pltpu.make_async_copy(x_hbm, x_all, sems.at[0])
            cp.start()
            cp.wait()

        drv = dr_ref[...]
        df = (0.5 * drv).astype(BF16)

        @pl.when(j == 0)
        def _():
            df_all[rows, :] = df

        ds = _dot_nt(df, wd_ref[...])
        av = a_ref[...]
        bv = b_ref[...]
        sig = jax.nn.sigmoid(av)
        sl = av * sig
        da = (ds * bv * (sig * (1.0 + av * (1.0 - sig)))).astype(BF16)
        db = (ds * sl).astype(BF16)
        da_all[rows, :] = da
        db_all[rows, :] = db
        s_all[rows, :] = (sl * bv).astype(BF16)
        dx = _dot(da, wg_ref[...]) + _dot(db, wu_ref[...])

        @pl.when(j == 0)
        def _():
            dx_acc[rows, :] = ALPHA * drv + dx

        @pl.when(j > 0)
        def _():
            dx_acc[rows, :] += dx

        @pl.when(i == ni - 1)
        def _():
            copies = []
            for n, (lhs, rhs, out) in enumerate(((da_all, x_all, dwg_hbm), (db_all, x_all, dwu_hbm),
                                                 (s_all, df_all, dwd_hbm))):
                slot = n % 2
                if n >= 2:
                    for cp in copies[2 * (n - 2): 2 * (n - 2) + 2]:
                        cp.wait()
                res_buf[slot] = _dot_tn(lhs[...], rhs[...])
                for h in range(2):
                    cp = pltpu.make_async_copy(res_buf.at[slot, pl.ds(h * hr, hr), :], out.at[h, j],
                                               sems.at[1 + 2 * slot + h])
                    cp.start()
                    copies.append(cp)
            for cp in copies[2:]:
                cp.wait()

        @pl.when(jnp.logical_and(j == NSH - 1, i == ni - 1))
        def _():
            cp = pltpu.make_async_copy(dx_acc, dx_hbm, sems.at[0])
            cp.start()
            cp.wait()

    row = pl.BlockSpec((tm, D), lambda j, i: (i, 0))
    wsp = pl.BlockSpec((None, FSH, D), lambda j, i: (j, 0, 0))
    ab = pl.BlockSpec((None, tm, FSH), lambda j, i: (j, i, 0))
    dwshape = jax.ShapeDtypeStruct((2, NSH, hr, D), F32)
    return pl.pallas_call(
        body, name=name, grid=(NSH, ni),
        in_specs=[row, ab, ab, wsp, wsp, wsp, ANY],
        out_specs=[ANY, ANY, ANY, ANY],
        out_shape=[dwshape, dwshape, dwshape, jax.ShapeDtypeStruct((S, D), F32)],
        scratch_shapes=[pltpu.VMEM((S, D), F32), pltpu.VMEM((S, FSH), BF16), pltpu.VMEM((S, FSH), BF16),
                        pltpu.VMEM((S, FSH), BF16), pltpu.VMEM((S, D), BF16), pltpu.VMEM((S, D), BF16),
                        pltpu.VMEM((2, FSH, D), F32), pltpu.SemaphoreType.DMA((5,))],
        compiler_params=_cp(("arbitrary", "arbitrary"), vmem_mib=58),
    )(dr, a, b, wgt, wut, wd, xin_b)


def _matmul(a, b, mode, name, *, n, tm=512, tn=512, tk=None, b_col0=0, add=None, add_scale=1.0, out_dtype=F32,
            after=()):
    m, ka = a.shape
    tk = ka if tk is None else tk
    nk = ka // tk
    assert m % tm == 0 and n % tn == 0 and ka % tk == 0 and b_col0 % tn == 0
    off = b_col0 // tn
    na = len(after)

    def body(*refs):
        refs = refs[na:]
        if add is None:
            a_ref, b_ref, o_ref = refs[:3]
            add_ref = None
            rest = refs[3:]
        else:
            a_ref, b_ref, add_ref, o_ref = refs[:4]
            rest = refs[4:]
        k = pl.program_id(2)
        av = a_ref[...].astype(BF16)
        bv = b_ref[...].astype(BF16)
        p = _dot(av, bv) if mode == "nn" else _dot_nt(av, bv)

        def finish(acc):
            if add_ref is not None:
                acc = acc + add_scale * add_ref[...]
            o_ref[...] = acc.astype(out_dtype)

        if nk == 1:
            finish(p)
        else:
            acc_ref = rest[0]

            @pl.when(k == 0)
            def _():
                acc_ref[...] = p

            @pl.when(k > 0)
            def _():
                acc_ref[...] += p

            @pl.when(k == nk - 1)
            def _():
                finish(acc_ref[...])

    a_spec = pl.BlockSpec((tm, tk), lambda i, j, k: (i, k))
    if mode == "nn":
        b_spec = pl.BlockSpec((tk, tn), lambda i, j, k: (k, j + off))
    else:
        b_spec = pl.BlockSpec((tn, tk), lambda i, j, k: (j, k))
    o_spec = pl.BlockSpec((tm, tn), lambda i, j, k: (i, j))
    in_specs = [pl.BlockSpec(memory_space=pl.ANY)] * na + [a_spec, b_spec] + ([o_spec] if add is not None else [])
    args = tuple(after) + (a, b) + ((add,) if add is not None else ())
    return pl.pallas_call(
        body, name=name, grid=(m // tm, n // tn, nk),
        in_specs=in_specs, out_specs=o_spec,
        out_shape=jax.ShapeDtypeStruct((m, n), out_dtype),
        scratch_shapes=[pltpu.VMEM((tm, tn), F32)] if nk > 1 else [],
        compiler_params=_cp(("parallel", "parallel", "arbitrary")),
    )(*args)


def _wgrad(xt, y, rh, c, name, row_sharded):
    if row_sharded:
        def body(x_ref, y_ref, o_ref):
            res = _dot(x_ref[...], y_ref[...].astype(BF16))
            for j in range(NSH):
                for h in range(2):
                    o_ref[h, j] = res[(2 * j + h) * rh:(2 * j + h + 1) * rh, :]

        grid = (1,)
        in_specs = [pl.BlockSpec((2 * NSH * rh, S), lambda g: (0, 0)), pl.BlockSpec((S, c), lambda g: (0, 0))]
        out_specs = pl.BlockSpec((2, NSH, rh, c), lambda g: (0, 0, 0, 0))
        sem = ("arbitrary",)
    else:
        def body(x_ref, y_ref, o_ref):
            o_ref[...] = _dot(x_ref[...], y_ref[...].astype(BF16))

        grid = (2, NSH)
        in_specs = [pl.BlockSpec((rh, S), lambda h, j: (h, 0)), pl.BlockSpec((S, c), lambda h, j: (0, j))]
        out_specs = pl.BlockSpec((None, None, rh, c), lambda h, j: (h, j, 0, 0))
        sem = ("parallel", "parallel")
    return pl.pallas_call(
        body, name=name, grid=grid, in_specs=in_specs, out_specs=out_specs,
        out_shape=jax.ShapeDtypeStruct((2, NSH, rh, c), F32),
        compiler_params=_cp(sem, vmem_mib=56),
    )(xt, y)


def _resid_ln(res, f, ln_g, ln_b, name):
    tm = 256

    def body(res_ref, f_ref, g_ref, b_ref, h_ref, hb_ref, xh_ref, rstd_ref):
        r = ALPHA * res_ref[...] + f_ref[...]
        xh, rstd = _ln_stats(r)
        h = xh * g_ref[...] + b_ref[...]
        h_ref[...] = h
        hb_ref[...] = h.astype(BF16)
        xh_ref[...] = xh
        rstd_ref[...] = rstd

    row = pl.BlockSpec((tm, D), lambda i: (i, 0))
    vec = pl.BlockSpec((1, D), lambda i: (0, 0))
    return pl.pallas_call(
        body, name=name, grid=(S // tm,),
        in_specs=[row, row, vec, vec],
        out_specs=[row, row, row, pl.BlockSpec((tm, 1), lambda i: (i, 0))],
        out_shape=[jax.ShapeDtypeStruct((S, D), F32), jax.ShapeDtypeStruct((S, D), BF16),
                   jax.ShapeDtypeStruct((S, D), F32), jax.ShapeDtypeStruct((S, 1), F32)],
        compiler_params=_cp(("parallel",)),
    )(res, f, ln_g, ln_b)


def _ln_bwd(dout, xh, rstd, ln_g, name, target=None, after=()):
    tm = 256
    with_loss = target is not None
    na = len(after)

    def body(*refs):
        refs = refs[na:]
        if with_loss:
            y_ref, t_ref, xh_ref, rstd_ref, g_ref, dr_ref, dg_ref, db_ref, loss_ref = refs
            err = y_ref[...] - t_ref[...]
            dy = err * (1.0 / D)
        else:
            y_ref, xh_ref, rstd_ref, g_ref, dr_ref, dg_ref, db_ref = refs
            dy = y_ref[...]
        i = pl.program_id(0)
        xh = xh_ref[...]
        dr_ref[...] = _ln_dx(dy * g_ref[...], xh, rstd_ref[...])
        dg = jnp.sum(dy * xh, axis=0, keepdims=True)
        db = jnp.sum(dy, axis=0, keepdims=True)

        @pl.when(i == 0)
        def _():
            dg_ref[...] = dg
            db_ref[...] = db

        @pl.when(i > 0)
        def _():
            dg_ref[...] += dg
            db_ref[...] += db

        if with_loss:
            part = 0.5 * jnp.sum(jnp.mean(err * err, axis=-1, keepdims=True), axis=0, keepdims=True)
            part = jnp.broadcast_to(part, (8, 128))

            @pl.when(i == 0)
            def _():
                loss_ref[...] = part

            @pl.when(i > 0)
            def _():
                loss_ref[...] += part

    row = pl.BlockSpec((tm, D), lambda i: (i, 0))
    vec = pl.BlockSpec((1, D), lambda i: (0, 0))
    col = pl.BlockSpec((tm, 1), lambda i: (i, 0))
    in_specs = [pl.BlockSpec(memory_space=pl.ANY)] * na + [row] + ([row] if with_loss else []) + [row, col, vec]
    out_specs = [row, vec, vec] + ([pl.BlockSpec((8, 128), lambda i: (0, 0))] if with_loss else [])
    out_shape = [jax.ShapeDtypeStruct((S, D), F32), jax.ShapeDtypeStruct((1, D), F32),
                 jax.ShapeDtypeStruct((1, D), F32)] + ([jax.ShapeDtypeStruct((8, 128), F32)] if with_loss else [])
    args = tuple(after) + (dout,) + ((target,) if with_loss else ()) + (xh, rstd, ln_g)
    return pl.pallas_call(
        body, name=name, grid=(S // tm,), in_specs=in_specs, out_specs=out_specs, out_shape=out_shape,
        compiler_params=_cp(("arbitrary",)),
    )(*args)


ROPE_TM = 256


def _rope_tables(pos_ref, invf_ref, sign):
    ang = pos_ref[...] * invf_ref[...]
    lane = lax.broadcasted_iota(jnp.int32, ang.shape, 1)
    first = (lane % DH) < (DH // 2)
    sinv = jnp.sin(ang) * sign
    return first, jnp.cos(ang), jnp.where(first, -sinv, sinv)


def _rotate(x, first, cosf, sinf):
    return x * cosf + jnp.where(first, pltpu.roll(x, 96, 1), pltpu.roll(x, 32, 1)) * sinf


def _rope_fwd(qkv, pos_f, invf, name):
    tm = ROPE_TM

    def body(t_ref, pos_ref, invf_ref, o0_ref, o1_ref, o2_ref, buf_ref):
        first, cosf, sinf = _rope_tables(pos_ref, invf_ref, 1.0)
        o_refs = (o0_ref, o1_ref, o2_ref)
        for sec in range(3):
            for gi, d in enumerate(DILATIONS):
                for ch in range(GRP_W // 128):
                    src = sec * ATT_W + gi * GRP_W + ch * 128
                    dst = slice(sec * GRP_W + ch * 128, sec * GRP_W + (ch + 1) * 128)
                    x = t_ref[:, src:src + 128]
                    if sec < 2:
                        x = _rotate(x, first, cosf, sinf)
                    if d == 1:
                        o_refs[gi][0, :, dst] = x.astype(BF16)
                    else:
                        buf_ref[...] = x
                        for r in range(d):
                            o_refs[gi][r, :, dst] = buf_ref[pl.ds(r, tm // d, stride=d), :].astype(BF16)

    return pl.pallas_call(
        body, name=name, grid=(S // tm,),
        in_specs=[pl.BlockSpec((tm, 3 * ATT_W), lambda i: (i, 0)), pl.BlockSpec((tm, 1), lambda i: (i, 0)),
                  pl.BlockSpec((1, 128), lambda i: (0, 0))],
        out_specs=[pl.BlockSpec((d, tm // d, 3 * GRP_W), lambda i: (0, i, 0)) for d in DILATIONS],
        out_shape=[jax.ShapeDtypeStruct((d, S // d, 3 * GRP_W), BF16) for d in DILATIONS],
        scratch_shapes=[pltpu.VMEM((tm, 128), F32)],
        compiler_params=_cp(("parallel",)),
    )(qkv, pos_f, invf)


def _rope_bwd(dqkv_c, pos_f, invf, name):
    tm = ROPE_TM

    def body(*refs):
        g_refs, (pos_ref, invf_ref, o_ref, buf_ref) = refs[:9], refs[9:]
        first, cosf, sinf = _rope_tables(pos_ref, invf_ref, -1.0)
        for sec in range(3):
            for gi, d in enumerate(DILATIONS):
                g_ref = g_refs[3 * gi + sec]
                for ch in range(GRP_W // 128):
                    cols = slice(ch * 128, (ch + 1) * 128)
                    if d == 1:
                        x = g_ref[0, :, cols]
                    else:
                        for r in range(d):
                            buf_ref[pl.ds(r, tm // d, stride=d), :] = g_ref[r, :, cols]
                        x = buf_ref[...]
                    if sec < 2:
                        x = _rotate(x, first, cosf, sinf)
                    dst = sec * ATT_W + gi * GRP_W + ch * 128
                    o_ref[:, dst:dst + 128] = x.astype(BF16)

    g_specs = [pl.BlockSpec((d, tm // d, GRP_W), lambda i: (0, i, 0)) for d in DILATIONS for _ in range(3)]
    return pl.pallas_call(
        body, name=name, grid=(S // tm,),
        in_specs=g_specs + [pl.BlockSpec((tm, 1), lambda i: (i, 0)), pl.BlockSpec((1, 128), lambda i: (0, 0))],
        out_specs=pl.BlockSpec((tm, 3 * ATT_W), lambda i: (i, 0)),
        out_shape=jax.ShapeDtypeStruct((S, 3 * ATT_W), BF16),
        scratch_shapes=[pltpu.VMEM((tm, 128), F32)],
        compiler_params=_cp(("parallel",)),
    )(*[g for grp in dqkv_c for g in grp], pos_f, invf)


def _class_order(ts, name):
    tm = ROPE_TM
    n = len(ts)

    def body(*refs):
        buf_ref = refs[3 * n]
        for a in range(n):
            for ch in range(GRP_W // 128):
                cols = slice(ch * 128, (ch + 1) * 128)
                buf_ref[...] = refs[a][:, cols]
                for b, d in enumerate(DILATIONS[1:]):
                    for r in range(d):
                        refs[n + 2 * a + b][r, :, cols] = buf_ref[pl.ds(r, tm // d, stride=d), :]

    return pl.pallas_call(
        body, name=name, grid=(S // tm,),
        in_specs=[pl.BlockSpec((tm, GRP_W), lambda i: (i, 0))] * n,
        out_specs=[pl.BlockSpec((d, tm // d, GRP_W), lambda i: (0, i, 0)) for _ in range(n) for d in DILATIONS[1:]],
        out_shape=[jax.ShapeDtypeStruct((d, S // d, GRP_W), F32) for _ in range(n) for d in DILATIONS[1:]],
        scratch_shapes=[pltpu.VMEM((tm, 128), F32)],
        compiler_params=_cp(("parallel",)),
    )(*ts)


def _attn_fwd(gi, qkv_c, name):
    d = DILATIONS[gi]
    nblk = S // d // BLK

    def body(*refs):
        if nblk > 1:
            q_ref, kc_ref, kp_ref, vc_ref, vp_ref, o_ref, lse_ref = refs
            has_prev = pl.program_id(1) != 0
        else:
            q_ref, kc_ref, vc_ref, o_ref, lse_ref = refs
        qi = lax.broadcasted_iota(jnp.int32, (BLK, BLK), 0)
        kj = lax.broadcasted_iota(jnp.int32, (BLK, BLK), 1)
        mask_c = kj <= qi
        if nblk > 1:
            mask_p = jnp.logical_and(kj >= qi, has_prev)
        for h in range(NH):
            sl = slice(h * DH, (h + 1) * DH)
            q = q_ref[:, sl]
            sc = jnp.where(mask_c, _dot_nt(q, kc_ref[:, sl]) * 0.125, NEG_INF)
            m = jnp.max(sc, axis=-1, keepdims=True)
            if nblk > 1:
                sp = jnp.where(mask_p, _dot_nt(q, kp_ref[:, sl]) * 0.125, NEG_INF)
                m = jnp.maximum(m, jnp.max(sp, axis=-1, keepdims=True))
            pc = jnp.exp(sc - m)
            l = jnp.sum(pc, axis=-1, keepdims=True)
            o = _dot(pc.astype(BF16), vc_ref[:, sl])
            if nblk > 1:
                pp = jnp.exp(sp - m)
                l = l + jnp.sum(pp, axis=-1, keepdims=True)
                o = o + _dot(pp.astype(BF16), vp_ref[:, sl])
            o_ref[:, sl] = o / l
            lse_ref[:, sl] = jnp.broadcast_to(m + jnp.log(l), (BLK, DH))

    def cur(sec):
        return pl.BlockSpec((None, BLK, GRP_W), lambda r, n: (r, n, sec))

    def prev(sec):
        return pl.BlockSpec((None, BLK, GRP_W), lambda r, n: (r, jnp.maximum(n - 1, 0), sec))

    out = pl.BlockSpec((None, BLK, GRP_W), lambda r, n: (r, n, 0))
    shp = jax.ShapeDtypeStruct((d, S // d, GRP_W), F32)
    if nblk > 1:
        in_specs, args = [cur(0), cur(1), prev(1), cur(2), prev(2)], (qkv_c,) * 5
    else:
        in_specs, args = [cur(0), cur(1), cur(2)], (qkv_c,) * 3
    return pl.pallas_call(
        body, name=name, grid=(d, nblk), in_specs=in_specs, out_specs=[out, out], out_shape=[shp, shp],
        compiler_params=_cp(("parallel", "parallel")),
    )(*args)


def _attn_combine(os, lses, name):
    tm = ROPE_TM

    def body(o0_ref, o1_ref, o2_ref, l0_ref, l1_ref, l2_ref, y_ref, yt_ref, l_ref, buf_ref):
        def token_order(ref, d, cols, slot):
            if d == 1:
                return ref[0, :, cols]
            for r in range(d):
                buf_ref[slot, pl.ds(r, tm // d, stride=d), :] = ref[r, :, cols]
            return buf_ref[slot]

        for ch in range(GRP_W // 128):
            cols = slice(ch * 128, (ch + 1) * 128)
            o = [token_order(ref, d, cols, k) for k, (ref, d) in enumerate(zip((o0_ref, o1_ref, o2_ref), DILATIONS))]
            ls = [token_order(ref, d, cols, 3 + k)
                  for k, (ref, d) in enumerate(zip((l0_ref, l1_ref, l2_ref), DILATIONS))]
            m = jnp.maximum(jnp.maximum(ls[0], ls[1]), ls[2])
            e = [jnp.exp(l - m) for l in ls]
            den = e[0] + e[1] + e[2]
            y = (e[0] * o[0] + e[1] * o[1] + e[2] * o[2]) / den
            y_ref[:, cols] = y
            yt_ref[cols, :] = y.T.astype(BF16)
            l_ref[:, cols] = m + jnp.log(den)

    blk = pl.BlockSpec((tm, GRP_W), lambda i: (i, 0))
    cls = [pl.BlockSpec((d, tm // d, GRP_W), lambda i: (0, i, 0)) for d in DILATIONS]
    shp = jax.ShapeDtypeStruct((S, GRP_W), F32)
    return pl.pallas_call(
        body, name=name, grid=(S // tm,), in_specs=cls + cls,
        out_specs=[blk, pl.BlockSpec((GRP_W, tm), lambda i: (0, i)), blk],
        out_shape=[shp, jax.ShapeDtypeStruct((GRP_W, S), BF16), shp],
        scratch_shapes=[pltpu.VMEM((6, tm, 128), F32)],
        compiler_params=_cp(("parallel",)),
    )(*os, *lses)


def _attn_bwd(gi, qkv_c, dy_c, y_c, lse_c, name):
    d = DILATIONS[gi]
    nblk = S // d // BLK

    def body(*refs):
        if nblk > 1:
            (q_ref, qn_ref, k_ref, kp_ref, v_ref, vp_ref, dy_ref, dyn_ref, y_ref, yn_ref, l_ref, ln_ref,
             dq_ref, dk_ref, dv_ref) = refs
            n = pl.program_id(1)
            has_prev = n != 0
            has_next = n != nblk - 1
        else:
            q_ref, k_ref, v_ref, dy_ref, y_ref, l_ref, dq_ref, dk_ref, dv_ref = refs
        qi = lax.broadcasted_iota(jnp.int32, (BLK, BLK), 0)
        kj = lax.broadcasted_iota(jnp.int32, (BLK, BLK), 1)
        mask_c = kj <= qi
        if nblk > 1:
            mask_p = jnp.logical_and(kj >= qi, has_prev)
            mask_n = jnp.logical_and(kj >= qi, has_next)
        for h in range(NH):
            sl = slice(h * DH, (h + 1) * DH)
            q, k, v = q_ref[:, sl], k_ref[:, sl], v_ref[:, sl]
            dy_h = dy_ref[:, sl]
            dd = jnp.sum(dy_h * y_ref[:, sl], axis=-1, keepdims=True)
            lcol = l_ref[:, h * DH:h * DH + 1]
            dyb = dy_h.astype(BF16)
            p = jnp.exp(jnp.where(mask_c, _dot_nt(q, k) * 0.125, NEG_INF) - lcol)
            ds = (p * (_dot_nt(dyb, v) - dd)).astype(BF16)
            dq = _dot(ds, k)
            dk = _dot_tn(ds, q)
            dv = _dot_tn(p.astype(BF16), dyb)
            if nblk > 1:
                qn, kpv, vpv = qn_ref[:, sl], kp_ref[:, sl], vp_ref[:, sl]
                dyn = dyn_ref[:, sl]
                ddn = jnp.sum(dyn * yn_ref[:, sl], axis=-1, keepdims=True)
                lncol = ln_ref[:, h * DH:h * DH + 1]
                dynb = dyn.astype(BF16)
                pp = jnp.exp(jnp.where(mask_p, _dot_nt(q, kpv) * 0.125, NEG_INF) - lcol)
                dsp = (pp * (_dot_nt(dyb, vpv) - dd)).astype(BF16)
                dq = dq + _dot(dsp, kpv)
                pn = jnp.exp(jnp.where(mask_n, _dot_nt(qn, k) * 0.125, NEG_INF) - lncol)
                dsn = (pn * (_dot_nt(dynb, v) - ddn)).astype(BF16)
                dk = dk + _dot_tn(dsn, qn)
                dv = dv + _dot_tn(pn.astype(BF16), dynb)
            dq_ref[:, sl] = dq * 0.125
            dk_ref[:, sl] = dk * 0.125
            dv_ref[:, sl] = dv

    def spec(sec, shift):
        def idx(r, n):
            return (r, jnp.clip(n + shift, 0, nblk - 1), sec)
        return pl.BlockSpec((None, BLK, GRP_W), idx)

    if nblk > 1:
        in_specs = [spec(0, 0), spec(0, 1), spec(1, 0), spec(1, -1), spec(2, 0), spec(2, -1),
                    spec(0, 0), spec(0, 1), spec(0, 0), spec(0, 1), spec(0, 0), spec(0, 1)]
        args = (qkv_c,) * 6 + (dy_c, dy_c, y_c, y_c, lse_c, lse_c)
    else:
        in_specs = [spec(0, 0), spec(1, 0), spec(2, 0), spec(0, 0), spec(0, 0), spec(0, 0)]
        args = (qkv_c, qkv_c, qkv_c, dy_c, y_c, lse_c)
    out = spec(0, 0)
    shp = jax.ShapeDtypeStruct((d, S // d, GRP_W), F32)
    return pl.pallas_call(
        body, name=name, grid=(d, nblk), in_specs=in_specs, out_specs=[out, out, out], out_shape=[shp, shp, shp],
        compiler_params=_cp(("parallel", "parallel")),
    )(*args)


_SQRT_HALF = 0.7071067811865476
_INV_SQRT_2PI = 0.3989422804014327


def _gelu(z):
    return 0.5 * z * (1.0 + lax.erf(z * _SQRT_HALF))


def _gelu_grad(z):
    return 0.5 * (1.0 + lax.erf(z * _SQRT_HALF)) + z * (jnp.exp(-0.5 * z * z) * _INV_SQRT_2PI)


def _tril_mask():
    t = lax.broadcasted_iota(jnp.int32, (BLK, BLK), 0)
    s = lax.broadcasted_iota(jnp.int32, (BLK, BLK), 1)
    return s <= t


def _gmlp_fwd(z, ln_g, ln_b, w_s, b_s_t, name):
    def body(z_ref, g_ref, b_ref, ws_ref, bs_ref, y_ref, yt_ref):
        zg = _gelu(z_ref[...])
        u = zg[:, :GW]
        xh, _ = _ln_stats(zg[:, GW:])
        vn = (xh * g_ref[...] + b_ref[...]).astype(BF16)
        tril = _tril_mask()
        for gg in range(8):
            sl = slice(gg * BLK, (gg + 1) * BLK)
            wt = jnp.where(tril, ws_ref[gg], 0.0).astype(BF16)
            mixed = _dot(wt, vn[:, sl]) + bs_ref[:, gg:gg + 1]
            yv = u[:, sl] * mixed
            y_ref[:, sl] = yv.astype(BF16)
            yt_ref[sl, :] = yv.T.astype(BF16)

    vec = pl.BlockSpec((1, GW), lambda n: (0, 0))
    return pl.pallas_call(
        body, name=name, grid=(NBLK,),
        in_specs=[pl.BlockSpec((BLK, 2 * GW), lambda n: (n, 0)), vec, vec,
                  pl.BlockSpec((8, BLK, BLK), lambda n: (0, 0, 0)), pl.BlockSpec((BLK, 8), lambda n: (0, 0))],
        out_specs=[pl.BlockSpec((BLK, GW), lambda n: (n, 0)), pl.BlockSpec((GW, BLK), lambda n: (0, n))],
        out_shape=[jax.ShapeDtypeStruct((S, GW), BF16), jax.ShapeDtypeStruct((GW, S), BF16)],
        compiler_params=_cp(("parallel",)),
    )(z, ln_g, ln_b, w_s, b_s_t)


def _gmlp_bwd(z, dy, ln_g, ln_b, w_s, b_s_t, name):
    def body(z_ref, dy_ref, g_ref, b_ref, ws_ref, bs_ref, dz_ref, dws_ref, dbs_ref, dg_ref, db_ref, dvn_ref):
        n = pl.program_id(0)
        zv = z_ref[...]
        zg = _gelu(zv)
        u = zg[:, :GW]
        xh, rstd = _ln_stats(zg[:, GW:])
        vn = (xh * g_ref[...] + b_ref[...]).astype(BF16)
        tril = _tril_mask()

        @pl.when(n == 0)
        def _():
            dws_ref[...] = jnp.zeros_like(dws_ref)
            dbs_ref[...] = jnp.zeros_like(dbs_ref)
            dg_ref[...] = jnp.zeros_like(dg_ref)
            db_ref[...] = jnp.zeros_like(db_ref)

        for gg in range(8):
            sl = slice(gg * BLK, (gg + 1) * BLK)
            wt = jnp.where(tril, ws_ref[gg], 0.0).astype(BF16)
            dyg = dy_ref[:, sl]
            mixed = _dot(wt, vn[:, sl]) + bs_ref[:, gg:gg + 1]
            dz_ref[:, sl] = (dyg * mixed * _gelu_grad(zv[:, sl])).astype(BF16)
            dmix = dyg * u[:, sl]
            dmb = dmix.astype(BF16)
            dws_ref[gg] += jnp.where(tril, _dot_nt(dmb, vn[:, sl]), 0.0)
            dbs_ref[:, gg:gg + 1] += jnp.sum(dmix, axis=-1, keepdims=True)
            dvn_ref[:, sl] = _dot_tn(wt, dmb)

        dvn = dvn_ref[...]
        dg_ref[...] += jnp.sum(dvn * xh, axis=0, keepdims=True)
        db_ref[...] += jnp.sum(dvn, axis=0, keepdims=True)
        dvg = _ln_dx(dvn * g_ref[...], xh, rstd)
        dz_ref[:, GW:] = (dvg * _gelu_grad(zv[:, GW:])).astype(BF16)

    vec = pl.BlockSpec((1, GW), lambda n: (0, 0))
    ws = pl.BlockSpec((8, BLK, BLK), lambda n: (0, 0, 0))
    bs = pl.BlockSpec((BLK, 8), lambda n: (0, 0))
    return pl.pallas_call(
        body, name=name, grid=(NBLK,),
        in_specs=[pl.BlockSpec((BLK, 2 * GW), lambda n: (n, 0)), pl.BlockSpec((BLK, GW), lambda n: (n, 0)),
                  vec, vec, ws, bs],
        out_specs=[pl.BlockSpec((BLK, 2 * GW), lambda n: (n, 0)), ws, bs, vec, vec],
        out_shape=[jax.ShapeDtypeStruct((S, 2 * GW), BF16), jax.ShapeDtypeStruct((8, BLK, BLK), F32),
                   jax.ShapeDtypeStruct((BLK, 8), F32), jax.ShapeDtypeStruct((1, GW), F32),
                   jax.ShapeDtypeStruct((1, GW), F32)],
        scratch_shapes=[pltpu.VMEM((BLK, GW), F32)],
        compiler_params=_cp(("arbitrary",)),
    )(z, dy, ln_g, ln_b, w_s, b_s_t)


def _merge_fwd(a, b, gl, b_gates, name):
    tm = 256

    def body(a_ref, b_ref, g0_ref, g1_ref, bg_ref, o_ref, ot_ref):
        g0 = jax.nn.sigmoid(g0_ref[...] + bg_ref[:, :D])
        g1 = jax.nn.sigmoid(g1_ref[...] + bg_ref[:, D:])
        mg = g0 * a_ref[...] + g1 * b_ref[...]
        o_ref[...] = mg.astype(BF16)
        ot_ref[...] = mg.T.astype(BF16)

    row = pl.BlockSpec((tm, D), lambda i: (i, 0))
    return pl.pallas_call(
        body, name=name, grid=(S // tm,),
        in_specs=[row, row, row, pl.BlockSpec((tm, D), lambda i: (i, 1)), pl.BlockSpec((1, 2 * D), lambda i: (0, 0))],
        out_specs=[row, pl.BlockSpec((D, tm), lambda i: (0, i))],
        out_shape=[jax.ShapeDtypeStruct((S, D), BF16), jax.ShapeDtypeStruct((D, S), BF16)],
        compiler_params=_cp(("parallel",)),
    )(a, b, gl, gl, b_gates)


def _merge_bwd(dm, a, b, gl, b_gates, name):
    tm = 256

    def body(dm_ref, a_ref, b_ref, g0_ref, g1_ref, bg_ref, da_ref, db_ref, dgl_ref, dbg_ref):
        i = pl.program_id(0)
        dmv = dm_ref[...]
        g0 = jax.nn.sigmoid(g0_ref[...] + bg_ref[:, :D])
        g1 = jax.nn.sigmoid(g1_ref[...] + bg_ref[:, D:])
        da_ref[...] = (dmv * g0).astype(BF16)
        db_ref[...] = (dmv * g1).astype(BF16)
        d0 = dmv * a_ref[...] * g0 * (1.0 - g0)
        d1 = dmv * b_ref[...] * g1 * (1.0 - g1)
        dgl_ref[:, :D] = d0.astype(BF16)
        dgl_ref[:, D:] = d1.astype(BF16)
        s0 = jnp.sum(d0, axis=0, keepdims=True)
        s1 = jnp.sum(d1, axis=0, keepdims=True)

        @pl.when(i == 0)
        def _():
            dbg_ref[:, :D] = s0
            dbg_ref[:, D:] = s1

        @pl.when(i > 0)
        def _():
            dbg_ref[:, :D] += s0
            dbg_ref[:, D:] += s1

    row = pl.BlockSpec((tm, D), lambda i: (i, 0))
    wide = pl.BlockSpec((tm, 2 * D), lambda i: (i, 0))
    bg = pl.BlockSpec((1, 2 * D), lambda i: (0, 0))
    return pl.pallas_call(
        body, name=name, grid=(S // tm,),
        in_specs=[row, row, row, row, pl.BlockSpec((tm, D), lambda i: (i, 1)), bg],
        out_specs=[row, row, wide, bg],
        out_shape=[jax.ShapeDtypeStruct((S, D), BF16), jax.ShapeDtypeStruct((S, D), BF16),
                   jax.ShapeDtypeStruct((S, 2 * D), BF16), jax.ShapeDtypeStruct((1, 2 * D), F32)],
        compiler_params=_cp(("arbitrary",)),
    )(dm, a, b, gl, gl, b_gates)


def _adam_math(w, g, m, v):
    m2 = ADAM_B1 * m + (1.0 - ADAM_B1) * g
    v2 = ADAM_B2 * v + (1.0 - ADAM_B2) * (g * g)
    m_hat = m2 / (1.0 - ADAM_B1 ** ADAM_STEP)
    v_hat = v2 / (1.0 - ADAM_B2 ** ADAM_STEP)
    delta = -ADAM_LR * (m_hat / (jnp.sqrt(v_hat) + ADAM_EPS) + ADAM_WD * w)
    return delta, m2, v2


def _pick_rows(rows, cols, unit=16, budget=MIB):
    best = unit
    for t in range(unit, rows + 1, unit):
        if rows % t == 0 and t * cols * 4 <= budget:
            best = t
    assert rows % best == 0
    return best


def _adamw(w, g, m, v, name):
    r, c = w.shape
    tr = _pick_rows(r, c, unit=8)

    def body(w_ref, g_ref, m_ref, v_ref, go_ref, d_ref, mo_ref, vo_ref):
        gv = g_ref[...]
        delta, m2, v2 = _adam_math(w_ref[...], gv, m_ref[...], v_ref[...])
        go_ref[...] = gv
        d_ref[...] = delta
        mo_ref[...] = m2
        vo_ref[...] = v2

    blk = pl.BlockSpec((tr, c), lambda i: (i, 0))
    shp = jax.ShapeDtypeStruct((r, c), F32)
    return pl.pallas_call(
        body, name=name, grid=(r // tr,), in_specs=[blk] * 4, out_specs=[blk] * 4, out_shape=[shp] * 4,
        compiler_params=_cp(("parallel",)),
    )(w, g, m, v)


def _small_sum_adamw(parts, w, m, v, name):
    tr = 48

    def body(p_ref, w_ref, m_ref, v_ref, g_ref, d_ref, mo_ref, vo_ref):
        gv = p_ref[0]
        for k in range(1, 8):
            gv = gv + p_ref[k]
        delta, m2, v2 = _adam_math(w_ref[...], gv, m_ref[...], v_ref[...])
        g_ref[...] = gv
        d_ref[...] = delta
        mo_ref[...] = m2
        vo_ref[...] = v2

    blk = pl.BlockSpec((tr, D), lambda i: (i, 0))
    shp = jax.ShapeDtypeStruct((SMALL_ROWS, D), F32)
    return pl.pallas_call(
        body, name=name, grid=(SMALL_ROWS // tr,),
        in_specs=[pl.BlockSpec((8, tr, D), lambda i: (0, i, 0)), blk, blk, blk],
        out_specs=[blk] * 4, out_shape=[shp] * 4,
        compiler_params=_cp(("parallel",)),
    )(parts, w, m, v)


ANY = pl.BlockSpec(memory_space=pl.ANY)


def _mesh_pos():
    x, y, c = lax.axis_index("x"), lax.axis_index("y"), lax.axis_index("c")
    chips = [(1 - x, y), (x, 1 - y), (1 - x, 1 - y)]
    return x, y, c, chips


def _place_shard(w, kind, pos, name):
    r, c = w.shape
    tr = _pick_rows(r, c)

    def body(pos_ref, w_ref, o_ref):
        o_ref[...] = w_ref[...].astype(BF16)

    if kind == "stack":
        o_spec = pl.BlockSpec((None, tr, c), lambda i, p: (p[1], i, 0))
        shape = (NSH, r, c)
    else:
        o_spec = pl.BlockSpec((tr, c), lambda i, p: (i, p[1]))
        shape = (r, NSH * c)
    return pl.pallas_call(
        body, name=name,
        grid_spec=pltpu.PrefetchScalarGridSpec(
            num_scalar_prefetch=1, grid=(r // tr,),
            in_specs=[pl.BlockSpec((tr, c), lambda i, p: (i, 0))], out_specs=o_spec),
        out_shape=jax.ShapeDtypeStruct(shape, BF16),
        compiler_params=_cp(("parallel",)),
    )(pos, w)


SEM = pl.BlockSpec(memory_space=pltpu.SEMAPHORE)
SPLIT_COPY = pltpu.CompilerParams(has_side_effects=pltpu.SideEffectType.DATAFLOW_SIDE_EFFECTING)


def _shard_window(ref, kind, j, h, dims):
    r, c = dims
    rows = pl.ds(pl.multiple_of(h * (r // 2), 16), r // 2)
    if kind == "stack":
        return ref.at[j, rows, :]
    return ref.at[rows, pl.ds(pl.multiple_of(j * c, 128), c)]


def _ici_copy(ref, kind, dims, j, c, sems, idx, to):
    win = _shard_window(ref, kind, j, c, dims)
    return pltpu.make_async_remote_copy(src_ref=win, dst_ref=win, send_sem=sems[0].at[idx], recv_sem=sems[1].at[idx],
                                        device_id=to, device_id_type=MESH_T)


def _gather_start(fulls, kinds, dims, after, name):
    n, na = len(fulls), len(after)

    def body(*refs):
        outs = refs[n + na:2 * n + na]
        send_sems, recv_sems, token = refs[2 * n + na:]
        x, y, c, chips = _mesh_pos()
        for a in range(n):
            for k, chip in enumerate(chips):
                _ici_copy(outs[a], kinds[a], dims[a], 2 * x + y, c, (send_sems, recv_sems), 3 * a + k,
                          (chip[0], chip[1], c)).start()
        token[...] = jnp.zeros_like(token)

    res = pl.pallas_call(
        body, name=name, in_specs=[ANY] * (n + na),
        out_specs=[ANY] * n + [SEM, SEM, pl.BlockSpec(memory_space=pltpu.VMEM)],
        out_shape=[jax.ShapeDtypeStruct(f.shape, BF16) for f in fulls]
        + [pltpu.SemaphoreType.DMA((3 * n,)), pltpu.SemaphoreType.DMA((3 * n,)), jax.ShapeDtypeStruct((8, 128), F32)],
        input_output_aliases={i: i for i in range(n)},
        compiler_params=SPLIT_COPY,
    )(*fulls, *after)
    return res[:n], res[n], res[n + 1], res[n + 2]


def _gather_wait(fulls, send_sems, recv_sems, kinds, dims, after, name):
    n, na = len(fulls), len(after)

    def body(*refs):
        ssem, rsem = refs[n], refs[n + 1]
        outs = refs[n + 2 + na:]
        x, y, c, chips = _mesh_pos()
        for a in range(n):
            for k, chip in enumerate(chips):
                to = (chip[0], chip[1], c)
                _ici_copy(outs[a], kinds[a], dims[a], 2 * x + y, c, (ssem, rsem), 3 * a + k, to).wait_send()
                _ici_copy(outs[a], kinds[a], dims[a], 2 * chip[0] + chip[1], c, (ssem, rsem), 3 * a + k, to).wait_recv()

    return pl.pallas_call(
        body, name=name, in_specs=[ANY] * n + [SEM, SEM] + [ANY] * na, out_specs=[ANY] * n,
        out_shape=[jax.ShapeDtypeStruct(f.shape, BF16) for f in fulls],
        input_output_aliases={i: i for i in range(n)},
        compiler_params=SPLIT_COPY,
    )(*fulls, send_sems, recv_sems, *after)


def _gather_forward(fulls, kinds, dims, name):
    n = len(fulls)

    def body(*refs):
        outs = refs[n:2 * n]
        sems = refs[2 * n:]
        x, y, c, chips = _mesh_pos()
        sib = (x, y, 1 - c)
        cps = []
        for a in range(n):
            for k, chip in enumerate(chips):
                cp = _ici_copy(outs[a], kinds[a], dims[a], 2 * chip[0] + chip[1], c, sems, 3 * a + k, sib)
                cp.start()
                cps.append(cp)
        for a in range(n):
            for k, chip in enumerate(chips):
                _ici_copy(outs[a], kinds[a], dims[a], 2 * chip[0] + chip[1], 1 - c, sems, 3 * a + k, sib).wait_recv()
        for cp in cps:
            cp.wait_send()

    return pl.pallas_call(
        body, name=name, in_specs=[ANY] * n, out_specs=[ANY] * n,
        out_shape=[jax.ShapeDtypeStruct(f.shape, BF16) for f in fulls],
        input_output_aliases={i: i for i in range(n)},
        scratch_shapes=[pltpu.SemaphoreType.DMA((3 * n,)), pltpu.SemaphoreType.DMA((3 * n,))],
    )(*fulls)


def _pair_exchange(grads, name):
    n = len(grads)

    def body(*refs):
        ins, outs = refs[:n], refs[n:2 * n]
        send_sems, recv_sems = refs[2 * n:]
        x, y, c, _ = _mesh_pos()
        cps = []
        for a in range(n):
            cp = pltpu.make_async_remote_copy(
                src_ref=ins[a].at[1 - c], dst_ref=outs[a], send_sem=send_sems.at[a], recv_sem=recv_sems.at[a],
                device_id=(x, y, 1 - c), device_id_type=MESH_T)
            cp.start()
            cps.append(cp)
        for cp in cps:
            cp.wait()

    return pl.pallas_call(
        body, name=name, in_specs=[ANY] * n, out_specs=[ANY] * n,
        out_shape=[jax.ShapeDtypeStruct(g.shape[1:], F32) for g in grads],
        scratch_shapes=[pltpu.SemaphoreType.DMA((n,)), pltpu.SemaphoreType.DMA((n,))],
    )(*grads)


def _pair_sum(g, recv, pos, name):
    _, _, rh, c = g.shape
    tr = _pick_rows(rh, c)

    def body(pos_ref, g_ref, r_ref, o_ref):
        o_ref[...] = (g_ref[...] + r_ref[...]).astype(BF16)

    return pl.pallas_call(
        body, name=name,
        grid_spec=pltpu.PrefetchScalarGridSpec(
            num_scalar_prefetch=1, grid=(NSH, rh // tr),
            in_specs=[pl.BlockSpec((None, None, tr, c), lambda j, r, p: (p[0], j, r, 0)),
                      pl.BlockSpec((None, tr, c), lambda j, r, p: (j, r, 0))],
            out_specs=pl.BlockSpec((None, tr, c), lambda j, r, p: (j, r, 0))),
        out_shape=jax.ShapeDtypeStruct((NSH, rh, c), BF16),
        compiler_params=_cp(("parallel", "parallel")),
    )(pos, g, recv)


def _chip_copy(src, land, a, k, chip, c, sems):
    return pltpu.make_async_remote_copy(
        src_ref=src.at[2 * chip[0] + chip[1]], dst_ref=land.at[k], send_sem=sems[0].at[3 * a + k],
        recv_sem=sems[1].at[3 * a + k], device_id=(chip[0], chip[1], c), device_id_type=MESH_T)


def _chip_start(psums, lands, name):
    n = len(psums)

    def body(*refs):
        srcs, dsts = refs[2 * n:3 * n], refs[3 * n:4 * n]
        send_sems, recv_sems, token = refs[4 * n:]
        x, y, c, chips = _mesh_pos()
        for a in range(n):
            for k, chip in enumerate(chips):
                _chip_copy(srcs[a], dsts[a], a, k, chip, c, (send_sems, recv_sems)).start()
        token[...] = jnp.zeros_like(token)

    res = pl.pallas_call(
        body, name=name, in_specs=[ANY] * (2 * n),
        out_specs=[ANY] * (2 * n) + [SEM, SEM, pl.BlockSpec(memory_space=pltpu.VMEM)],
        out_shape=[jax.ShapeDtypeStruct(p.shape, BF16) for p in psums]
        + [jax.ShapeDtypeStruct(l.shape, BF16) for l in lands]
        + [pltpu.SemaphoreType.DMA((3 * n,)), pltpu.SemaphoreType.DMA((3 * n,)), jax.ShapeDtypeStruct((8, 128), F32)],
        input_output_aliases={i: i for i in range(2 * n)},
        compiler_params=SPLIT_COPY,
    )(*psums, *lands)
    return res[:n], res[n:2 * n], res[2 * n], res[2 * n + 1], res[2 * n + 2]


def _chip_wait(psums, lands, send_sems, recv_sems, after, name):
    n, na = len(psums), len(after)

    def body(*refs):
        ssem, rsem = refs[2 * n], refs[2 * n + 1]
        outs = refs[2 * n + 2 + na:]
        srcs, dsts = outs[:n], outs[n:]
        x, y, c, chips = _mesh_pos()
        for a in range(n):
            for k, chip in enumerate(chips):
                cp = _chip_copy(srcs[a], dsts[a], a, k, chip, c, (ssem, rsem))
                cp.wait_send()
                cp.wait_recv()

    res = pl.pallas_call(
        body, name=name, in_specs=[ANY] * (2 * n) + [SEM, SEM] + [ANY] * na, out_specs=[ANY] * (2 * n),
        out_shape=[jax.ShapeDtypeStruct(p.shape, BF16) for p in psums]
        + [jax.ShapeDtypeStruct(l.shape, BF16) for l in lands],
        input_output_aliases={i: i for i in range(2 * n)},
        compiler_params=SPLIT_COPY,
    )(*psums, *lands, send_sems, recv_sems, *after)
    return res[n:]


def _owner_sum(g, recv_a, recv_b, pos, name):
    _, _, rh, c = g.shape
    tr = _pick_rows(rh, c)

    def body(pos_ref, g_ref, ra_ref, rb_ref, o_ref):
        acc = g_ref[...] + ra_ref[...]
        for k in range(3):
            acc = acc + rb_ref[k].astype(F32)
        o_ref[...] = acc

    return pl.pallas_call(
        body, name=name,
        grid_spec=pltpu.PrefetchScalarGridSpec(
            num_scalar_prefetch=1, grid=(rh // tr,),
            in_specs=[pl.BlockSpec((None, None, tr, c), lambda r, p: (p[0], p[1], r, 0)),
                      pl.BlockSpec((None, tr, c), lambda r, p: (p[1], r, 0)),
                      pl.BlockSpec((3, tr, c), lambda r, p: (0, r, 0))],
            out_specs=pl.BlockSpec((None, tr, c), lambda r, p: (p[0], r, 0))),
        out_shape=jax.ShapeDtypeStruct((2, rh, c), F32),
        compiler_params=_cp(("parallel",)),
    )(pos, g, recv_a, recv_b)


def _sibling_allgather(halves, name):
    n = len(halves)

    def body(*refs):
        outs = refs[n:2 * n]
        send_sems, recv_sems = refs[2 * n:]
        x, y, c, _ = _mesh_pos()
        cps = []
        for a in range(n):
            cp = pltpu.make_async_remote_copy(
                src_ref=outs[a].at[c], dst_ref=outs[a].at[c], send_sem=send_sems.at[a], recv_sem=recv_sems.at[a],
                device_id=(x, y, 1 - c), device_id_type=MESH_T)
            cp.start()
            cps.append(cp)
        for a in range(n):
            cps[a].wait_send()
            pltpu.make_async_remote_copy(
                src_ref=outs[a].at[1 - c], dst_ref=outs[a].at[1 - c], send_sem=send_sems.at[a],
                recv_sem=recv_sems.at[a], device_id=(x, y, 1 - c), device_id_type=MESH_T).wait_recv()

    return pl.pallas_call(
        body, name=name, in_specs=[ANY] * n, out_specs=[ANY] * n,
        out_shape=[jax.ShapeDtypeStruct(h.shape, F32) for h in halves],
        input_output_aliases={i: i for i in range(n)},
        scratch_shapes=[pltpu.SemaphoreType.DMA((n,)), pltpu.SemaphoreType.DMA((n,))],
    )(*halves)


def _small_allgather(part, after):
    m_per = SMALL_ROWS
    na = len(after)

    def body(x_ref, *refs):
        out_ref, send_sems, recv_sems, local_sem = refs[na:]
        x, y, c, chips = _mesh_pos()
        me, sibling = (x, y, c), (x, y, 1 - c)

        def rows(px, py, pc):
            return out_ref.at[pl.ds((4 * px + 2 * py + pc) * m_per, m_per), :]

        def copy(k, block, to, src=None):
            return pltpu.make_async_remote_copy(
                src_ref=rows(*block) if src is None else src, dst_ref=rows(*block),
                send_sem=send_sems.at[k], recv_sem=recv_sems.at[k], device_id=to, device_id_type=MESH_T)

        mine = pltpu.make_async_copy(x_ref, rows(*me), local_sem)
        mine.start()
        first = [copy(0, me, sibling, src=x_ref)]
        first += [copy(1 + j, me, (*chip, c), src=x_ref) for j, chip in enumerate(chips)]
        for cp in first:
            cp.start()
        passed = [copy(4 + j, (*chip, c), sibling) for j, chip in enumerate(chips)]
        for j, chip in enumerate(chips):
            copy(1 + j, (*chip, c), me).wait_recv()
            passed[j].start()
        copy(0, sibling, me).wait_recv()
        for j, chip in enumerate(chips):
            copy(4 + j, (*chip, 1 - c), me).wait_recv()
        for cp in first + passed:
            cp.wait_send()
        mine.wait()

    return pl.pallas_call(
        body, name="small_allgather",
        out_shape=jax.ShapeDtypeStruct((8 * m_per, D), F32),
        in_specs=[pl.BlockSpec(memory_space=pltpu.VMEM)] + [ANY] * na, out_specs=pl.BlockSpec(memory_space=pltpu.VMEM),
        scratch_shapes=[pltpu.SemaphoreType.DMA((7,)), pltpu.SemaphoreType.DMA((7,)), pltpu.SemaphoreType.DMA],
    )(part, *after)


def _pack_small(ln1_g, ln1_b, gln_g, gln_b, ln2_g, ln2_b, ln3_g, ln3_b, b_gates, b_s, w_s):
    rows = [ln1_g, ln1_b, gln_g, gln_b, ln2_g, ln2_b, ln3_g, ln3_b]
    rows = [r.reshape(1, D) for r in rows] + [b_gates.reshape(2, D), b_s.reshape(1, D), jnp.zeros((5, D), F32),
                                             w_s.reshape(128, D)]
    return jnp.concatenate(rows, axis=0)


def _unpack_small(p):
    out = [p[i:i + 1] for i in range(8)]
    return out + [p[8:10].reshape(1, 2 * D), p[10:11].reshape(1, 8, BLK), p[16:144].reshape(1, 8, BLK, BLK)]


GROUPS = (("f1g", "f1u", "f1d"), ("w_in",), ("w_ab", "w_gb", "w_out"), ("f2g", "f2u", "f2d"))


def _local_step(x, pos_f, target, P, weights_of, grads_ready):
    invf = ROPE_THETA ** (-jnp.arange(0, DH, 2, dtype=F32) / DH)
    invf = jnp.tile(invf, 4).reshape(1, 128)
    b_s_t = P["gmlp_b_s"].T

    W = dict(weights_of(0, []))
    h1, h1b, xh1, rstd1, a1, b1, h1t = _ffn_fwd(x, W["f1g"], W["f1u"], W["f1d"], P["ln1_g"], P["ln1_b"], "ffn1_fwd",
                                                emit_t=True)
    W.update(weights_of(1, [h1b]))
    qkv = _matmul(h1b, W["w_in"], "nn", "proj_qkv", n=3 * ATT_W, b_col0=0)
    z = _matmul(h1b, W["w_in"], "nn", "proj_z", n=2 * GW, b_col0=3 * ATT_W)
    gl = _matmul(h1b, W["w_in"], "nn", "proj_gates", n=2 * D, b_col0=3 * ATT_W + 2 * GW)
    qkv_c = _rope_fwd(qkv, pos_f, invf, "rope_fwd")
    og = [_attn_fwd(gi, qkv_c[gi], "attn_fwd_g%d" % gi) for gi in range(NG)]
    y_attn, y_attn_t, lse = _attn_combine([o for o, _ in og], [l for _, l in og], "attn_combine")
    y_gmlp, y_gmlp_t = _gmlp_fwd(z, P["gmlp_ln_g"], P["gmlp_ln_b"], P["gmlp_w_s"], b_s_t, "gmlp_fwd")
    W.update(weights_of(2, [y_gmlp]))
    br_a = _matmul(y_attn, W["w_ab"], "nn", "branch_attn", n=D)
    br_b = _matmul(y_gmlp, W["w_gb"], "nn", "branch_gmlp", n=D)
    merged, merged_t = _merge_fwd(br_a, br_b, gl, P["b_gates"], "merge_fwd")
    mix = _matmul(merged, W["w_out"], "nn", "mix_out", n=D)
    h2, h2b, xh2, rstd2 = _resid_ln(h1, mix, P["ln2_g"], P["ln2_b"], "resid_ln2")
    W.update(weights_of(3, [h2b]))
    y, _, xh3, rstd3, a2, b2 = _ffn_fwd(h2, W["f2g"], W["f2u"], W["f2d"], P["ln3_g"], P["ln3_b"], "ffn2_fwd")

    dr3, dg3, db3, loss = _ln_bwd(y, xh3, rstd3, P["ln3_g"], "loss_ln3_bwd", target=target)
    g_f2g, g_f2u, g_f2d, dh2 = _ffn_bwd(dr3, h2b, a2, b2, W["f2g"], W["f2u"], W["f2d"], "ffn2_bwd")
    tok = grads_ready(3, dict(f2g=g_f2g, f2u=g_f2u, f2d=g_f2d))
    dr2, dg2, db2 = _ln_bwd(dh2, xh2, rstd2, P["ln2_g"], "ln2_bwd", after=tok)
    g_wout = _wgrad(merged_t, dr2, 128, D, "dw_out", row_sharded=True)
    dmerged = _matmul(dr2, W["w_out"], "nt", "dmerged", n=D)
    dab, dbb, dglb, dbg = _merge_bwd(dmerged, br_a, br_b, gl, P["b_gates"], "merge_bwd")
    g_wab = _wgrad(y_attn_t, dab, GRP_W // 2, 256, "dw_attn_branch", row_sharded=False)
    g_wgb = _wgrad(y_gmlp_t, dbb, 128, D, "dw_gmlp_branch", row_sharded=True)
    tok = grads_ready(2, dict(w_ab=g_wab, w_gb=g_wgb, w_out=g_wout))
    dy_attn = _matmul(dab, W["w_ab"], "nt", "dy_attn", n=GRP_W, after=tok)
    dy_gmlp = _matmul(dbb, W["w_gb"], "nt", "dy_gmlp", n=GW)
    dzb, dws, dbs_t, dgln_g, dgln_b = _gmlp_bwd(z, dy_gmlp, P["gmlp_ln_g"], P["gmlp_ln_b"], P["gmlp_w_s"], b_s_t,
                                                 "gmlp_bwd")
    cls = _class_order([dy_attn, y_attn, lse], "attn_class_order")
    dqkv_c = []
    for gi in range(NG):
        dy_c, y_c, lse_c = [t[None] if gi == 0 else cls[2 * a + gi - 1] for a, t in enumerate((dy_attn, y_attn, lse))]
        dqkv_c.append(_attn_bwd(gi, qkv_c[gi], dy_c, y_c, lse_c, "attn_bwd_g%d" % gi))
    dqkvb = _rope_bwd(dqkv_c, pos_f, invf, "rope_bwd")
    dproj = jnp.concatenate([dqkvb, dzb, dglb], axis=1)
    g_win = _wgrad(h1t, dproj, D // 2, IN_SH, "dw_in", row_sharded=False)
    tok = grads_ready(1, dict(w_in=g_win))
    dh1 = _matmul(dproj, W["w_in"], "nt", "dh1", n=D, tn=D, tk=IN_SH, add=dr2, add_scale=ALPHA, after=tok)
    dr1, dg1, db1 = _ln_bwd(dh1, xh1, rstd1, P["ln1_g"], "ln1_bwd")
    g_f1g, g_f1u, g_f1d, dx = _ffn_bwd(dr1, x.astype(BF16), a1, b1, W["f1g"], W["f1u"], W["f1d"], "ffn1_bwd")
    grads_ready(0, dict(f1g=g_f1g, f1u=g_f1u, f1d=g_f1d))

    small = _pack_small(dg1, db1, dgln_g, dgln_b, dg2, db2, dg3, db3, dbg, dbs_t.T, dws)
    return loss, dx, small


BIG = ("f1g", "f1u", "f1d", "w_in", "w_ab", "w_gb", "w_out", "f2g", "f2u", "f2d")
TRANSPOSED = ("f1g", "f1u", "f2g", "f2u")
KIND = dict(f1g="stack", f1u="stack", f1d="stack", w_in="col", w_ab="col", w_gb="stack", w_out="stack",
            f2g="stack", f2u="stack", f2d="stack")


def kernel(x, positions, ffn1_w_gate, ffn1_w_up, ffn1_w_down, ln1_g, ln1_b, w_in, b_gates, gmlp_ln_g, gmlp_ln_b, gmlp_w_s, gmlp_b_s, w_attn_branch, w_gmlp_branch, w_out, ln2_g, ln2_b, ffn2_w_gate, ffn2_w_up, ffn2_w_down, ln3_g, ln3_b, loss_target, m_ffn1_w_gate, m_ffn1_w_up, m_ffn1_w_down, m_ln1_g, m_ln1_b, m_w_in, m_b_gates, m_gmlp_ln_g, m_gmlp_ln_b, m_gmlp_w_s, m_gmlp_b_s, m_w_attn_branch, m_w_gmlp_branch, m_w_out, m_ln2_g, m_ln2_b, m_ffn2_w_gate, m_ffn2_w_up, m_ffn2_w_down, m_ln3_g, m_ln3_b, v_ffn1_w_gate, v_ffn1_w_up, v_ffn1_w_down, v_ln1_g, v_ln1_b, v_w_in, v_b_gates, v_gmlp_ln_g, v_gmlp_ln_b, v_gmlp_w_s, v_gmlp_b_s, v_w_attn_branch, v_w_gmlp_branch, v_w_out, v_ln2_g, v_ln2_b, v_ffn2_w_gate, v_ffn2_w_up, v_ffn2_w_down, v_ln3_g, v_ln3_b):
    cx, cy, cc = lax.axis_index("x"), lax.axis_index("y"), lax.axis_index("c")
    pos = jnp.stack([cc, 2 * cx + cy]).astype(jnp.int32)

    w_sh = dict(f1g=ffn1_w_gate, f1u=ffn1_w_up, f1d=ffn1_w_down, w_in=w_in, w_ab=w_attn_branch,
                w_gb=w_gmlp_branch, w_out=w_out, f2g=ffn2_w_gate, f2u=ffn2_w_up, f2d=ffn2_w_down)
    m_sh = dict(f1g=m_ffn1_w_gate, f1u=m_ffn1_w_up, f1d=m_ffn1_w_down, w_in=m_w_in, w_ab=m_w_attn_branch,
                w_gb=m_w_gmlp_branch, w_out=m_w_out, f2g=m_ffn2_w_gate, f2u=m_ffn2_w_up, f2d=m_ffn2_w_down)
    v_sh = dict(f1g=v_ffn1_w_gate, f1u=v_ffn1_w_up, f1d=v_ffn1_w_down, w_in=v_w_in, w_ab=v_w_attn_branch,
                w_gb=v_w_gmlp_branch, w_out=v_w_out, f2g=v_ffn2_w_gate, f2u=v_ffn2_w_up, f2d=v_ffn2_w_down)
    w_sh = {k: (v[0].T if k in TRANSPOSED else v[0]) for k, v in w_sh.items()}
    m_sh = {k: (v[0].T if k in TRANSPOSED else v[0]) for k, v in m_sh.items()}
    v_sh = {k: (v[0].T if k in TRANSPOSED else v[0]) for k, v in v_sh.items()}

    started, tokens = [], []
    for gi, names in enumerate(GROUPS):
        placed = [_place_shard(w_sh[k], KIND[k], pos, "place_" + k) for k in names]
        fulls, ssem, rsem, token = _gather_start(placed, [KIND[k] for k in names], [w_sh[k].shape for k in names],
                                                 tokens[-1:], "gather_start_g%d" % gi)
        started.append((fulls, ssem, rsem))
        tokens.append(token)

    def weights_of(gi, after):
        names = GROUPS[gi]
        kinds, dims = [KIND[k] for k in names], [w_sh[k].shape for k in names]
        fulls, ssem, rsem = started[gi]
        fulls = _gather_wait(fulls, ssem, rsem, kinds, dims, list(after) + (tokens if gi == 0 else []),
                             "gather_wait_g%d" % gi)
        fulls = _gather_forward(fulls, kinds, dims, "gather_forward_g%d" % gi)
        return {k: (f.reshape(D, D) if k in ("w_gb", "w_out") else f) for k, f in zip(names, fulls)}

    inflight = {}

    def grads_ready(gi, gd):
        names = GROUPS[gi]
        grads = [gd[k] for k in names]
        recv_a = _pair_exchange(grads, "rs_pair_exchange_g%d" % gi)
        psums = [_pair_sum(g, r, pos, "rs_pair_sum_" + k) for g, r, k in zip(grads, recv_a, names)]
        lands = [lax.empty((3,) + p.shape[1:], BF16) for p in psums]
        psums, lands, ssem, rsem, token = _chip_start(psums, lands, "rs_chip_start_g%d" % gi)
        inflight[gi] = (grads, recv_a, psums, lands, ssem, rsem, token)
        return [token]

    P = dict(ln1_g=ln1_g, ln1_b=ln1_b, ln2_g=ln2_g, ln2_b=ln2_b, ln3_g=ln3_g, ln3_b=ln3_b, b_gates=b_gates,
             gmlp_ln_g=gmlp_ln_g, gmlp_ln_b=gmlp_ln_b, gmlp_w_s=gmlp_w_s[0], gmlp_b_s=gmlp_b_s[0])
    pos_f = positions.reshape(S, 1).astype(F32)
    loss_part, dx, small = _local_step(x[0], pos_f, loss_target[0], P, weights_of, grads_ready)
    loss = lax.psum(loss_part[0, 0], ("x", "y", "c"))

    g_out, d_out, m_out, v_out = {}, {}, {}, {}

    def finish(gi, after):
        grads, recv_a, psums, lands, ssem, rsem, token = inflight[gi]
        recv_b = _chip_wait(psums, lands, ssem, rsem, after + [inflight[0][6]], "rs_chip_wait_g%d" % gi)
        halves = [_owner_sum(g, ra, rb, pos, "rs_owner_sum_" + k)
                  for g, ra, rb, k in zip(grads, recv_a, recv_b, GROUPS[gi])]
        reduced = _sibling_allgather(halves, "rs_sibling_allgather_g%d" % gi)
        for k, gfull in zip(GROUPS[gi], reduced):
            res = _adamw(w_sh[k], gfull.reshape(w_sh[k].shape), m_sh[k], v_sh[k], "adamw_" + k)
            after = [res[1]]
            if k in TRANSPOSED:
                res = [r.T for r in res]
            g_out[k], d_out[k], m_out[k], v_out[k] = [r[None] for r in res]
        return after

    after = []
    for gi in (3, 2, 1):
        after = finish(gi, after)

    parts = _small_allgather(small, after).reshape(8, SMALL_ROWS, D)
    sp = (ln1_g, ln1_b, gmlp_ln_g, gmlp_ln_b, ln2_g, ln2_b, ln3_g, ln3_b, b_gates, gmlp_b_s, gmlp_w_s)
    sm = (m_ln1_g, m_ln1_b, m_gmlp_ln_g, m_gmlp_ln_b, m_ln2_g, m_ln2_b, m_ln3_g, m_ln3_b, m_b_gates, m_gmlp_b_s,
          m_gmlp_w_s)
    sv = (v_ln1_g, v_ln1_b, v_gmlp_ln_g, v_gmlp_ln_b, v_ln2_g, v_ln2_b, v_ln3_g, v_ln3_b, v_b_gates, v_gmlp_b_s,
          v_gmlp_w_s)
    sg, sd, smn, svn = _small_sum_adamw(parts, _pack_small(*sp), _pack_small(*sm), _pack_small(*sv), "small_adamw")
    names = ("ln1_g", "ln1_b", "gmlp_ln_g", "gmlp_ln_b", "ln2_g", "ln2_b", "ln3_g", "ln3_b", "b_gates", "gmlp_b_s",
             "gmlp_w_s")
    for dst, packed in ((g_out, sg), (d_out, sd), (m_out, smn), (v_out, svn)):
        for nm, val in zip(names, _unpack_small(packed)):
            dst[nm] = val
    finish(0, [sg])

    order = ("f1g", "f1u", "f1d", "ln1_g", "ln1_b", "w_in", "b_gates", "gmlp_ln_g", "gmlp_ln_b", "gmlp_w_s", "gmlp_b_s",
             "w_ab", "w_gb", "w_out", "ln2_g", "ln2_b", "f2g", "f2u", "f2d", "ln3_g", "ln3_b")
    outs = [loss, dx[None]]
    for dst in (g_out, d_out, m_out, v_out):
        outs += [dst[k] for k in order]
    return tuple(outs)
```

```python
import functools
import math

import jax
import jax.numpy as jnp
from jax import lax
from jax.experimental import pallas as pl
from jax.experimental.pallas import tpu as pltpu

F32 = jnp.float32
BF16 = jnp.bfloat16

S = 2048
D = 1024
NSH = 4
FSH = 704
ATT_W = 1536
GRP_W = 512
NG = 3
NH = 8
DH = 64
BLK = 128
NBLK = S // BLK
GW = 1024
IN_W = 8704
IN_SH = IN_W // NSH
ALPHA = 2.0 ** 0.25
LN_EPS = 1e-5
ROPE_THETA = 10000.0
DILATIONS = (1, 4, 16)
ADAM_LR, ADAM_B1, ADAM_B2, ADAM_EPS, ADAM_WD, ADAM_STEP = 0.001, 0.9, 0.999, 1e-08, 0.01, 10
SMALL_ROWS = 144
MESH_T = pl.DeviceIdType.MESH
MIB = 1024 * 1024
NEG_INF = float("-inf")


def _cp(sem, vmem_mib=48):
    return pltpu.CompilerParams(dimension_semantics=sem, vmem_limit_bytes=vmem_mib * MIB)


def _ln_stats(r):
    mu = jnp.mean(r, axis=-1, keepdims=True)
    xc = r - mu
    var = jnp.mean(xc * xc, axis=-1, keepdims=True)
    rstd = lax.rsqrt(var + LN_EPS)
    return xc * rstd, rstd


def _ln_dx(dxh, xh, rstd):
    m1 = jnp.mean(dxh, axis=-1, keepdims=True)
    m2 = jnp.mean(dxh * xh, axis=-1, keepdims=True)
    return rstd * (dxh - m1 - xh * m2)


def _dot_nt(a, b):
    return lax.dot_general(a, b, (((1,), (1,)), ((), ())), preferred_element_type=F32)


def _dot_tn(a, b):
    return lax.dot_general(a, b, (((0,), (0,)), ((), ())), preferred_element_type=F32)


def _dot(a, b):
    return jnp.dot(a, b, preferred_element_type=F32)


def _ffn_fwd(xin, wgt, wut, wd, ln_g, ln_b, name, emit_t=False):
    tm = 512

    def body(x_ref, wg_ref, wu_ref, wd_ref, g_ref, b_ref, *rest):
        if emit_t:
            h_ref, hb_ref, xh_ref, rstd_ref, a_ref, bb_ref, ht_ref, acc_ref = rest
        else:
            h_ref, hb_ref, xh_ref, rstd_ref, a_ref, bb_ref, acc_ref = rest
        j = pl.program_id(1)
        xb = x_ref[...].astype(BF16)
        a = _dot_nt(xb, wg_ref[...])
        b = _dot_nt(xb, wu_ref[...])
        a_ref[...] = a.astype(BF16)
        bb_ref[...] = b.astype(BF16)
        s = (a * jax.nn.sigmoid(a)) * b
        f = _dot(s.astype(BF16), wd_ref[...])

        @pl.when(j == 0)
        def _():
            acc_ref[...] = f

        @pl.when(j > 0)
        def _():
            acc_ref[...] += f

        @pl.when(j == NSH - 1)
        def _():
            r = ALPHA * x_ref[...] + 0.5 * acc_ref[...]
            xh, rstd = _ln_stats(r)
            h = xh * g_ref[...] + b_ref[...]
            h_ref[...] = h
            hb_ref[...] = h.astype(BF16)
            xh_ref[...] = xh
            rstd_ref[...] = rstd
            if emit_t:
                ht_ref[...] = h.T.astype(BF16)

    row = pl.BlockSpec((tm, D), lambda i, j: (i, 0))
    vec = pl.BlockSpec((1, D), lambda i, j: (0, 0))
    wsp = pl.BlockSpec((None, FSH, D), lambda i, j: (j, 0, 0))
    ab = pl.BlockSpec((None, tm, FSH), lambda i, j: (j, i, 0))
    out_specs = [row, row, row, pl.BlockSpec((tm, 1), lambda i, j: (i, 0)), ab, ab]
    out_shape = [jax.ShapeDtypeStruct((S, D), F32), jax.ShapeDtypeStruct((S, D), BF16),
                 jax.ShapeDtypeStruct((S, D), F32), jax.ShapeDtypeStruct((S, 1), F32),
                 jax.ShapeDtypeStruct((NSH, S, FSH), BF16), jax.ShapeDtypeStruct((NSH, S, FSH), BF16)]
    if emit_t:
        out_specs.append(pl.BlockSpec((D, tm), lambda i, j: (0, i)))
        out_shape.append(jax.ShapeDtypeStruct((D, S), BF16))
    return pl.pallas_call(
        body, name=name, grid=(S // tm, NSH),
        in_specs=[row, wsp, wsp, wsp, vec, vec], out_specs=out_specs, out_shape=out_shape,
        scratch_shapes=[pltpu.VMEM((tm, D), F32)],
        compiler_params=_cp(("parallel", "arbitrary")),
    )(xin, wgt, wut, wd, ln_g, ln_b)


def _ffn_bwd(dr, xin_b, a, b, wgt, wut, wd, name):
    tm = 512
    ni = S // tm
    hr = FSH // 2

    def body(dr_ref, a_ref, b_ref, wg_ref, wu_ref, wd_ref, x_hbm, dwg_hbm, dwu_hbm, dwd_hbm, dx_hbm,
             dx_acc, da_all, db_all, s_all, df_all, x_all, res_buf, sems):
        j = pl.program_id(0)
        i = pl.program_id(1)
        rows = pl.ds(pl.multiple_of(i * tm, tm), tm)

        @pl.when(jnp.logical_and(j == 0, i == 0))
        def _():
            cp = pltpu.make_async_copy(x_hbm, x_all, sems.at[0])
            cp.start()
            cp.wait()

        drv = dr_ref[...]
        df = (0.5 * drv).astype(BF16)

        @pl.when(j == 0)
        def _():
            df_all[rows, :] = df

        ds = _dot_nt(df, wd_ref[...])
        av = a_ref[...].astype(F32)
        bv = b_ref[...].astype(F32)
        sig = jax.nn.sigmoid(av)
        sl = av * sig
        da = (ds * bv * (sig * (1.0 + av * (1.0 - sig)))).astype(BF16)
        db = (ds * sl).astype(BF16)
        da_all[rows, :] = da
        db_all[rows, :] = db
        s_all[rows, :] = (sl * bv).astype(BF16)
        dx = _dot(da, wg_ref[...]) + _dot(db, wu_ref[...])

        @pl.when(j == 0)
        def _():
            dx_acc[rows, :] = ALPHA * drv + dx

        @pl.when(j > 0)
        def _():
            dx_acc[rows, :] += dx

        @pl.when(i == ni - 1)
        def _():
            copies = []
            for n, (lhs, rhs, out) in enumerate(((da_all, x_all, dwg_hbm), (db_all, x_all, dwu_hbm),
                                                 (s_all, df_all, dwd_hbm))):
                slot = n % 2
                if n >= 2:
                    for cp in copies[2 * (n - 2): 2 * (n - 2) + 2]:
                        cp.wait()
                res_buf[slot] = _dot_tn(lhs[...], rhs[...])
                for h in range(2):
                    cp = pltpu.make_async_copy(res_buf.at[slot, pl.ds(h * hr, hr), :], out.at[h, j],
                                               sems.at[1 + 2 * slot + h])
                    cp.start()
                    copies.append(cp)
            for cp in copies[2:]:
                cp.wait()

        @pl.when(jnp.logical_and(j == NSH - 1, i == ni - 1))
        def _():
            cp = pltpu.make_async_copy(dx_acc, dx_hbm, sems.at[0])
            cp.start()
            cp.wait()

    row = pl.BlockSpec((tm, D), lambda j, i: (i, 0))
    wsp = pl.BlockSpec((None, FSH, D), lambda j, i: (j, 0, 0))
    ab = pl.BlockSpec((None, tm, FSH), lambda j, i: (j, i, 0))
    dwshape = jax.ShapeDtypeStruct((2, NSH, hr, D), F32)
    return pl.pallas_call(
        body, name=name, grid=(NSH, ni),
        in_specs=[row, ab, ab, wsp, wsp, wsp, ANY],
        out_specs=[ANY, ANY, ANY, ANY],
        out_shape=[dwshape, dwshape, dwshape, jax.ShapeDtypeStruct((S, D), F32)],
        scratch_shapes=[pltpu.VMEM((S, D), F32), pltpu.VMEM((S, FSH), BF16), pltpu.VMEM((S, FSH), BF16),
                        pltpu.VMEM((S, FSH), BF16), pltpu.VMEM((S, D), BF16), pltpu.VMEM((S, D), BF16),
                        pltpu.VMEM((2, FSH, D), F32), pltpu.SemaphoreType.DMA((5,))],
        compiler_params=_cp(("arbitrary", "arbitrary"), vmem_mib=58),
    )(dr, a, b, wgt, wut, wd, xin_b)


def _matmul(a, b, mode, name, *, n, tm=512, tn=512, tk=None, b_col0=0, add=None, add_scale=1.0, out_dtype=F32,
            after=()):
    m, ka = a.shape
    tk = ka if tk is None else tk
    nk = ka // tk
    assert m % tm == 0 and n % tn == 0 and ka % tk == 0 and b_col0 % tn == 0
    off = b_col0 // tn
    na = len(after)

    def body(*refs):
        refs = refs[na:]
        if add is None:
            a_ref, b_ref, o_ref = refs[:3]
            add_ref = None
            rest = refs[3:]
        else:
            a_ref, b_ref, add_ref, o_ref = refs[:4]
            rest = refs[4:]
        k = pl.program_id(2)
        av = a_ref[...].astype(BF16)
        bv = b_ref[...].astype(BF16)
        p = _dot(av, bv) if mode == "nn" else _dot_nt(av, bv)

        def finish(acc):
            if add_ref is not None:
                acc = acc + add_scale * add_ref[...]
            o_ref[...] = acc.astype(out_dtype)

        if nk == 1:
            finish(p)
        else:
            acc_ref = rest[0]

            @pl.when(k == 0)
            def _():
                acc_ref[...] = p

            @pl.when(k > 0)
            def _():
                acc_ref[...] += p

            @pl.when(k == nk - 1)
            def _():
                finish(acc_ref[...])

    a_spec = pl.BlockSpec((tm, tk), lambda i, j, k: (i, k))
    if mode == "nn":
        b_spec = pl.BlockSpec((tk, tn), lambda i, j, k: (k, j + off))
    else:
        b_spec = pl.BlockSpec((tn, tk), lambda i, j, k: (j, k))
    o_spec = pl.BlockSpec((tm, tn), lambda i, j, k: (i, j))
    in_specs = [pl.BlockSpec(memory_space=pl.ANY)] * na + [a_spec, b_spec] + ([o_spec] if add is not None else [])
    args = tuple(after) + (a, b) + ((add,) if add is not None else ())
    return pl.pallas_call(
        body, name=name, grid=(m // tm, n // tn, nk),
        in_specs=in_specs, out_specs=o_spec,
        out_shape=jax.ShapeDtypeStruct((m, n), out_dtype),
        scratch_shapes=[pltpu.VMEM((tm, tn), F32)] if nk > 1 else [],
        compiler_params=_cp(("parallel", "parallel", "arbitrary")),
    )(*args)


def _wgrad(xt, y, rh, c, name, row_sharded):
    if row_sharded:
        def body(x_ref, y_ref, o_ref):
            res = _dot(x_ref[...], y_ref[...].astype(BF16))
            for j in range(NSH):
                for h in range(2):
                    o_ref[h, j] = res[(2 * j + h) * rh:(2 * j + h + 1) * rh, :]

        grid = (1,)
        in_specs = [pl.BlockSpec((2 * NSH * rh, S), lambda g: (0, 0)), pl.BlockSpec((S, c), lambda g: (0, 0))]
        out_specs = pl.BlockSpec((2, NSH, rh, c), lambda g: (0, 0, 0, 0))
        sem = ("arbitrary",)
    else:
        def body(x_ref, y_ref, o_ref):
            o_ref[...] = _dot(x_ref[...], y_ref[...].astype(BF16))

        grid = (2, NSH)
        in_specs = [pl.BlockSpec((rh, S), lambda h, j: (h, 0)), pl.BlockSpec((S, c), lambda h, j: (0, j))]
        out_specs = pl.BlockSpec((None, None, rh, c), lambda h, j: (h, j, 0, 0))
        sem = ("parallel", "parallel")
    return pl.pallas_call(
        body, name=name, grid=grid, in_specs=in_specs, out_specs=out_specs,
        out_shape=jax.ShapeDtypeStruct((2, NSH, rh, c), F32),
        compiler_params=_cp(sem, vmem_mib=56),
    )(xt, y)


def _resid_ln(res, f, ln_g, ln_b, name):
    tm = 256

    def body(res_ref, f_ref, g_ref, b_ref, h_ref, hb_ref, xh_ref, rstd_ref):
        r = ALPHA * res_ref[...] + f_ref[...]
        xh, rstd = _ln_stats(r)
        h = xh * g_ref[...] + b_ref[...]
        h_ref[...] = h
        hb_ref[...] = h.astype(BF16)
        xh_ref[...] = xh
        rstd_ref[...] = rstd

    row = pl.BlockSpec((tm, D), lambda i: (i, 0))
    vec = pl.BlockSpec((1, D), lambda i: (0, 0))
    return pl.pallas_call(
        body, name=name, grid=(S // tm,),
        in_specs=[row, row, vec, vec],
        out_specs=[row, row, row, pl.BlockSpec((tm, 1), lambda i: (i, 0))],
        out_shape=[jax.ShapeDtypeStruct((S, D), F32), jax.ShapeDtypeStruct((S, D), BF16),
                   jax.ShapeDtypeStruct((S, D), F32), jax.ShapeDtypeStruct((S, 1), F32)],
        compiler_params=_cp(("parallel",)),
    )(res, f, ln_g, ln_b)


def _ln_bwd(dout, xh, rstd, ln_g, name, target=None, after=()):
    tm = 256
    with_loss = target is not None
    na = len(after)

    def body(*refs):
        refs = refs[na:]
        if with_loss:
            y_ref, t_ref, xh_ref, rstd_ref, g_ref, dr_ref, dg_ref, db_ref, loss_ref = refs
            err = y_ref[...] - t_ref[...]
            dy = err * (1.0 / D)
        else:
            y_ref, xh_ref, rstd_ref, g_ref, dr_ref, dg_ref, db_ref = refs
            dy = y_ref[...]
        i = pl.program_id(0)
        xh = xh_ref[...]
        dr_ref[...] = _ln_dx(dy * g_ref[...], xh, rstd_ref[...])
        dg = jnp.sum(dy * xh, axis=0, keepdims=True)
        db = jnp.sum(dy, axis=0, keepdims=True)

        @pl.when(i == 0)
        def _():
            dg_ref[...] = dg
            db_ref[...] = db

        @pl.when(i > 0)
        def _():
            dg_ref[...] += dg
            db_ref[...] += db

        if with_loss:
            part = 0.5 * jnp.sum(jnp.mean(err * err, axis=-1, keepdims=True), axis=0, keepdims=True)
            part = jnp.broadcast_to(part, (8, 128))

            @pl.when(i == 0)
            def _():
                loss_ref[...] = part

            @pl.when(i > 0)
            def _():
                loss_ref[...] += part

    row = pl.BlockSpec((tm, D), lambda i: (i, 0))
    vec = pl.BlockSpec((1, D), lambda i: (0, 0))
    col = pl.BlockSpec((tm, 1), lambda i: (i, 0))
    in_specs = [pl.BlockSpec(memory_space=pl.ANY)] * na + [row] + ([row] if with_loss else []) + [row, col, vec]
    out_specs = [row, vec, vec] + ([pl.BlockSpec((8, 128), lambda i: (0, 0))] if with_loss else [])
    out_shape = [jax.ShapeDtypeStruct((S, D), F32), jax.ShapeDtypeStruct((1, D), F32),
                 jax.ShapeDtypeStruct((1, D), F32)] + ([jax.ShapeDtypeStruct((8, 128), F32)] if with_loss else [])
    args = tuple(after) + (dout,) + ((target,) if with_loss else ()) + (xh, rstd, ln_g)
    return pl.pallas_call(
        body, name=name, grid=(S // tm,), in_specs=in_specs, out_specs=out_specs, out_shape=out_shape,
        compiler_params=_cp(("arbitrary",)),
    )(*args)


ROPE_TM = 256


def _rope_tables(pos_ref, invf_ref, sign):
    ang = pos_ref[...] * invf_ref[...]
    lane = lax.broadcasted_iota(jnp.int32, ang.shape, 1)
    first = (lane % DH) < (DH // 2)
    sinv = jnp.sin(ang) * sign
    return first, jnp.cos(ang), jnp.where(first, -sinv, sinv)


def _rotate(x, first, cosf, sinf):
    return x * cosf + jnp.where(first, pltpu.roll(x, 96, 1), pltpu.roll(x, 32, 1)) * sinf


def _rope_fwd(qkv, pos_f, invf, name):
    tm = ROPE_TM

    def body(t_ref, pos_ref, invf_ref, o0_ref, o1_ref, o2_ref, buf_ref):
        first, cosf, sinf = _rope_tables(pos_ref, invf_ref, 1.0)
        o_refs = (o0_ref, o1_ref, o2_ref)
        for sec in range(3):
            for gi, d in enumerate(DILATIONS):
                for ch in range(GRP_W // 128):
                    src = sec * ATT_W + gi * GRP_W + ch * 128
                    dst = slice(sec * GRP_W + ch * 128, sec * GRP_W + (ch + 1) * 128)
                    x = t_ref[:, src:src + 128]
                    if sec < 2:
                        x = _rotate(x, first, cosf, sinf)
                    if d == 1:
                        o_refs[gi][0, :, dst] = x.astype(BF16)
                    else:
                        buf_ref[...] = x
                        for r in range(d):
                            o_refs[gi][r, :, dst] = buf_ref[pl.ds(r, tm // d, stride=d), :].astype(BF16)

    return pl.pallas_call(
        body, name=name, grid=(S // tm,),
        in_specs=[pl.BlockSpec((tm, 3 * ATT_W), lambda i: (i, 0)), pl.BlockSpec((tm, 1), lambda i: (i, 0)),
                  pl.BlockSpec((1, 128), lambda i: (0, 0))],
        out_specs=[pl.BlockSpec((d, tm // d, 3 * GRP_W), lambda i: (0, i, 0)) for d in DILATIONS],
        out_shape=[jax.ShapeDtypeStruct((d, S // d, 3 * GRP_W), BF16) for d in DILATIONS],
        scratch_shapes=[pltpu.VMEM((tm, 128), F32)],
        compiler_params=_cp(("parallel",)),
    )(qkv, pos_f, invf)


def _rope_bwd(dqkv_c, pos_f, invf, name):
    tm = ROPE_TM

    def body(*refs):
        g_refs, (pos_ref, invf_ref, o_ref, buf_ref) = refs[:9], refs[9:]
        first, cosf, sinf = _rope_tables(pos_ref, invf_ref, -1.0)
        for sec in range(3):
            for gi, d in enumerate(DILATIONS):
                g_ref = g_refs[3 * gi + sec]
                for ch in range(GRP_W // 128):
                    cols = slice(ch * 128, (ch + 1) * 128)
                    if d == 1:
                        x = g_ref[0, :, cols]
                    else:
                        for r in range(d):
                            buf_ref[pl.ds(r, tm // d, stride=d), :] = g_ref[r, :, cols]
                        x = buf_ref[...]
                    if sec < 2:
                        x = _rotate(x, first, cosf, sinf)
                    dst = sec * ATT_W + gi * GRP_W + ch * 128
                    o_ref[:, dst:dst + 128] = x.astype(BF16)

    g_specs = [pl.BlockSpec((d, tm // d, GRP_W), lambda i: (0, i, 0)) for d in DILATIONS for _ in range(3)]
    return pl.pallas_call(
        body, name=name, grid=(S // tm,),
        in_specs=g_specs + [pl.BlockSpec((tm, 1), lambda i: (i, 0)), pl.BlockSpec((1, 128), lambda i: (0, 0))],
        out_specs=pl.BlockSpec((tm, 3 * ATT_W), lambda i: (i, 0)),
        out_shape=jax.ShapeDtypeStruct((S, 3 * ATT_W), BF16),
        scratch_shapes=[pltpu.VMEM((tm, 128), F32)],
        compiler_params=_cp(("parallel",)),
    )(*[g for grp in dqkv_c for g in grp], pos_f, invf)


def _class_order(ts, name):
    tm = ROPE_TM
    n = len(ts)

    def body(*refs):
        buf_ref = refs[3 * n]
        for a in range(n):
            for ch in range(GRP_W // 128):
                cols = slice(ch * 128, (ch + 1) * 128)
                buf_ref[...] = refs[a][:, cols]
                for b, d in enumerate(DILATIONS[1:]):
                    for r in range(d):
                        refs[n + 2 * a + b][r, :, cols] = buf_ref[pl.ds(r, tm // d, stride=d), :]

    return pl.pallas_call(
        body, name=name, grid=(S // tm,),
        in_specs=[pl.BlockSpec((tm, GRP_W), lambda i: (i, 0))] * n,
        out_specs=[pl.BlockSpec((d, tm // d, GRP_W), lambda i: (0, i, 0)) for _ in range(n) for d in DILATIONS[1:]],
        out_shape=[jax.ShapeDtypeStruct((d, S // d, GRP_W), F32) for _ in range(n) for d in DILATIONS[1:]],
        scratch_shapes=[pltpu.VMEM((tm, 128), F32)],
        compiler_params=_cp(("parallel",)),
    )(*ts)


def _attn_fwd(gi, qkv_c, name):
    d = DILATIONS[gi]
    nblk = S // d // BLK

    def body(*refs):
        if nblk > 1:
            q_ref, kc_ref, kp_ref, vc_ref, vp_ref, o_ref, lse_ref = refs
            has_prev = pl.program_id(1) != 0
        else:
            q_ref, kc_ref, vc_ref, o_ref, lse_ref = refs
        qi = lax.broadcasted_iota(jnp.int32, (BLK, BLK), 0)
        kj = lax.broadcasted_iota(jnp.int32, (BLK, BLK), 1)
        mask_c = kj <= qi
        if nblk > 1:
            mask_p = jnp.logical_and(kj >= qi, has_prev)
        for h in range(NH):
            sl = slice(h * DH, (h + 1) * DH)
            q = q_ref[:, sl]
            sc = jnp.where(mask_c, _dot_nt(q, kc_ref[:, sl]) * 0.125, NEG_INF)
            m = jnp.max(sc, axis=-1, keepdims=True)
            if nblk > 1:
                sp = jnp.where(mask_p, _dot_nt(q, kp_ref[:, sl]) * 0.125, NEG_INF)
                m = jnp.maximum(m, jnp.max(sp, axis=-1, keepdims=True))
            pc = jnp.exp(sc - m)
            l = jnp.sum(pc, axis=-1, keepdims=True)
            o = _dot(pc.astype(BF16), vc_ref[:, sl])
            if nblk > 1:
                pp = jnp.exp(sp - m)
                l = l + jnp.sum(pp, axis=-1, keepdims=True)
                o = o + _dot(pp.astype(BF16), vp_ref[:, sl])
            o_ref[:, sl] = o / l
            lse_ref[:, sl] = jnp.broadcast_to(m + jnp.log(l), (BLK, DH))

    def cur(sec):
        return pl.BlockSpec((None, BLK, GRP_W), lambda r, n: (r, n, sec))

    def prev(sec):
        return pl.BlockSpec((None, BLK, GRP_W), lambda r, n: (r, jnp.maximum(n - 1, 0), sec))

    out = pl.BlockSpec((None, BLK, GRP_W), lambda r, n: (r, n, 0))
    shp = jax.ShapeDtypeStruct((d, S // d, GRP_W), F32)
    if nblk > 1:
        in_specs, args = [cur(0), cur(1), prev(1), cur(2), prev(2)], (qkv_c,) * 5
    else:
        in_specs, args = [cur(0), cur(1), cur(2)], (qkv_c,) * 3
    return pl.pallas_call(
        body, name=name, grid=(d, nblk), in_specs=in_specs, out_specs=[out, out], out_shape=[shp, shp],
        compiler_params=_cp(("parallel", "parallel")),
    )(*args)


def _attn_combine(os, lses, name):
    tm = ROPE_TM

    def body(o0_ref, o1_ref, o2_ref, l0_ref, l1_ref, l2_ref, y_ref, yt_ref, l_ref, buf_ref):
        def token_order(ref, d, cols, slot):
            if d == 1:
                return ref[0, :, cols]
            for r in range(d):
                buf_ref[slot, pl.ds(r, tm // d, stride=d), :] = ref[r, :, cols]
            return buf_ref[slot]

        for ch in range(GRP_W // 128):
            cols = slice(ch * 128, (ch + 1) * 128)
            o = [token_order(ref, d, cols, k) for k, (ref, d) in enumerate(zip((o0_ref, o1_ref, o2_ref), DILATIONS))]
            ls = [token_order(ref, d, cols, 3 + k)
                  for k, (ref, d) in enumerate(zip((l0_ref, l1_ref, l2_ref), DILATIONS))]
            m = jnp.maximum(jnp.maximum(ls[0], ls[1]), ls[2])
            e = [jnp.exp(l - m) for l in ls]
            den = e[0] + e[1] + e[2]
            y = (e[0] * o[0] + e[1] * o[1] + e[2] * o[2]) / den
            y_ref[:, cols] = y
            yt_ref[cols, :] = y.T.astype(BF16)
            l_ref[:, cols] = m + jnp.log(den)

    blk = pl.BlockSpec((tm, GRP_W), lambda i: (i, 0))
    cls = [pl.BlockSpec((d, tm // d, GRP_W), lambda i: (0, i, 0)) for d in DILATIONS]
    shp = jax.ShapeDtypeStruct((S, GRP_W), F32)
    return pl.pallas_call(
        body, name=name, grid=(S // tm,), in_specs=cls + cls,
        out_specs=[blk, pl.BlockSpec((GRP_W, tm), lambda i: (0, i)), blk],
        out_shape=[shp, jax.ShapeDtypeStruct((GRP_W, S), BF16), shp],
        scratch_shapes=[pltpu.VMEM((6, tm, 128), F32)],
        compiler_params=_cp(("parallel",)),
    )(*os, *lses)


def _attn_bwd(gi, qkv_c, dy_c, y_c, lse_c, name):
    d = DILATIONS[gi]
    nblk = S // d // BLK

    def body(*refs):
        if nblk > 1:
            (q_ref, qn_ref, k_ref, kp_ref, v_ref, vp_ref, dy_ref, dyn_ref, y_ref, yn_ref, l_ref, ln_ref,
             dq_ref, dk_ref, dv_ref) = refs
            n = pl.program_id(1)
            has_prev = n != 0
            has_next = n != nblk - 1
        else:
            q_ref, k_ref, v_ref, dy_ref, y_ref, l_ref, dq_ref, dk_ref, dv_ref = refs
        qi = lax.broadcasted_iota(jnp.int32, (BLK, BLK), 0)
        kj = lax.broadcasted_iota(jnp.int32, (BLK, BLK), 1)
        mask_c = kj <= qi
        if nblk > 1:
            mask_p = jnp.logical_and(kj >= qi, has_prev)
            mask_n = jnp.logical_and(kj >= qi, has_next)
        for h in range(NH):
            sl = slice(h * DH, (h + 1) * DH)
            q, k, v = q_ref[:, sl], k_ref[:, sl], v_ref[:, sl]
            dy_h = dy_ref[:, sl]
            dd = jnp.sum(dy_h * y_ref[:, sl], axis=-1, keepdims=True)
            lcol = l_ref[:, h * DH:h * DH + 1]
            dyb = dy_h.astype(BF16)
            p = jnp.exp(jnp.where(mask_c, _dot_nt(q, k) * 0.125, NEG_INF) - lcol)
            ds = (p * (_dot_nt(dyb, v) - dd)).astype(BF16)
            dq = _dot(ds, k)
            dk = _dot_tn(ds, q)
            dv = _dot_tn(p.astype(BF16), dyb)
            if nblk > 1:
                qn, kpv, vpv = qn_ref[:, sl], kp_ref[:, sl], vp_ref[:, sl]
                dyn = dyn_ref[:, sl]
                ddn = jnp.sum(dyn * yn_ref[:, sl], axis=-1, keepdims=True)
                lncol = ln_ref[:, h * DH:h * DH + 1]
                dynb = dyn.astype(BF16)
                pp = jnp.exp(jnp.where(mask_p, _dot_nt(q, kpv) * 0.125, NEG_INF) - lcol)
                dsp = (pp * (_dot_nt(dyb, vpv) - dd)).astype(BF16)
                dq = dq + _dot(dsp, kpv)
                pn = jnp.exp(jnp.where(mask_n, _dot_nt(qn, k) * 0.125, NEG_INF) - lncol)
                dsn = (pn * (_dot_nt(dynb, v) - ddn)).astype(BF16)
                dk = dk + _dot_tn(dsn, qn)
                dv = dv + _dot_tn(pn.astype(BF16), dynb)
            dq_ref[:, sl] = dq * 0.125
            dk_ref[:, sl] = dk * 0.125
            dv_ref[:, sl] = dv

    def spec(sec, shift):
        def idx(r, n):
            return (r, jnp.clip(n + shift, 0, nblk - 1), sec)
        return pl.BlockSpec((None, BLK, GRP_W), idx)

    if nblk > 1:
        in_specs = [spec(0, 0), spec(0, 1), spec(1, 0), spec(1, -1), spec(2, 0), spec(2, -1),
                    spec(0, 0), spec(0, 1), spec(0, 0), spec(0, 1), spec(0, 0), spec(0, 1)]
        args = (qkv_c,) * 6 + (dy_c, dy_c, y_c, y_c, lse_c, lse_c)
    else:
        in_specs = [spec(0, 0), spec(1, 0), spec(2, 0), spec(0, 0), spec(0, 0), spec(0, 0)]
        args = (qkv_c, qkv_c, qkv_c, dy_c, y_c, lse_c)
    out = spec(0, 0)
    shp = jax.ShapeDtypeStruct((d, S // d, GRP_W), F32)
    return pl.pallas_call(
        body, name=name, grid=(d, nblk), in_specs=in_specs, out_specs=[out, out, out], out_shape=[shp, shp, shp],
        compiler_params=_cp(("parallel", "parallel")),
    )(*args)


_SQRT_HALF = 0.7071067811865476
_INV_SQRT_2PI = 0.3989422804014327


def _gelu(z):
    return 0.5 * z * (1.0 + lax.erf(z * _SQRT_HALF))


def _gelu_grad(z):
    return 0.5 * (1.0 + lax.erf(z * _SQRT_HALF)) + z * (jnp.exp(-0.5 * z * z) * _INV_SQRT_2PI)


def _tril_mask():
    t = lax.broadcasted_iota(jnp.int32, (BLK, BLK), 0)
    s = lax.broadcasted_iota(jnp.int32, (BLK, BLK), 1)
    return s <= t


def _gmlp_fwd(z, ln_g, ln_b, w_s, b_s_t, name):
    def body(z_ref, g_ref, b_ref, ws_ref, bs_ref, y_ref, yt_ref):
        zg = _gelu(z_ref[...])
        u = zg[:, :GW]
        xh, _ = _ln_stats(zg[:, GW:])
        vn = (xh * g_ref[...] + b_ref[...]).astype(BF16)
        tril = _tril_mask()
        for gg in range(8):
            sl = slice(gg * BLK, (gg + 1) * BLK)
            wt = jnp.where(tril, ws_ref[gg], 0.0).astype(BF16)
            mixed = _dot(wt, vn[:, sl]) + bs_ref[:, gg:gg + 1]
            yv = u[:, sl] * mixed
            y_ref[:, sl] = yv.astype(BF16)
            yt_ref[sl, :] = yv.T.astype(BF16)

    vec = pl.BlockSpec((1, GW), lambda n: (0, 0))
    return pl.pallas_call(
        body, name=name, grid=(NBLK,),
        in_specs=[pl.BlockSpec((BLK, 2 * GW), lambda n: (n, 0)), vec, vec,
                  pl.BlockSpec((8, BLK, BLK), lambda n: (0, 0, 0)), pl.BlockSpec((BLK, 8), lambda n: (0, 0))],
        out_specs=[pl.BlockSpec((BLK, GW), lambda n: (n, 0)), pl.BlockSpec((GW, BLK), lambda n: (0, n))],
        out_shape=[jax.ShapeDtypeStruct((S, GW), BF16), jax.ShapeDtypeStruct((GW, S), BF16)],
        compiler_params=_cp(("parallel",)),
    )(z, ln_g, ln_b, w_s, b_s_t)


def _gmlp_bwd(z, dy, ln_g, ln_b, w_s, b_s_t, name):
    def body(z_ref, dy_ref, g_ref, b_ref, ws_ref, bs_ref, dz_ref, dws_ref, dbs_ref, dg_ref, db_ref, dvn_ref):
        n = pl.program_id(0)
        zv = z_ref[...]
        zg = _gelu(zv)
        u = zg[:, :GW]
        xh, rstd = _ln_stats(zg[:, GW:])
        vn = (xh * g_ref[...] + b_ref[...]).astype(BF16)
        tril = _tril_mask()

        @pl.when(n == 0)
        def _():
            dws_ref[...] = jnp.zeros_like(dws_ref)
            dbs_ref[...] = jnp.zeros_like(dbs_ref)
            dg_ref[...] = jnp.zeros_like(dg_ref)
            db_ref[...] = jnp.zeros_like(db_ref)

        for gg in range(8):
            sl = slice(gg * BLK, (gg + 1) * BLK)
            wt = jnp.where(tril, ws_ref[gg], 0.0).astype(BF16)
            dyg = dy_ref[:, sl]
            mixed = _dot(wt, vn[:, sl]) + bs_ref[:, gg:gg + 1]
            dz_ref[:, sl] = (dyg * mixed * _gelu_grad(zv[:, sl])).astype(BF16)
            dmix = dyg * u[:, sl]
            dmb = dmix.astype(BF16)
            dws_ref[gg] += jnp.where(tril, _dot_nt(dmb, vn[:, sl]), 0.0)
            dbs_ref[:, gg:gg + 1] += jnp.sum(dmix, axis=-1, keepdims=True)
            dvn_ref[:, sl] = _dot_tn(wt, dmb)

        dvn = dvn_ref[...]
        dg_ref[...] += jnp.sum(dvn * xh, axis=0, keepdims=True)
        db_ref[...] += jnp.sum(dvn, axis=0, keepdims=True)
        dvg = _ln_dx(dvn * g_ref[...], xh, rstd)
        dz_ref[:, GW:] = (dvg * _gelu_grad(zv[:, GW:])).astype(BF16)

    vec = pl.BlockSpec((1, GW), lambda n: (0, 0))
    ws = pl.BlockSpec((8, BLK, BLK), lambda n: (0, 0, 0))
    bs = pl.BlockSpec((BLK, 8), lambda n: (0, 0))
    return pl.pallas_call(
        body, name=name, grid=(NBLK,),
        in_specs=[pl.BlockSpec((BLK, 2 * GW), lambda n: (n, 0)), pl.BlockSpec((BLK, GW), lambda n: (n, 0)),
                  vec, vec, ws, bs],
        out_specs=[pl.BlockSpec((BLK, 2 * GW), lambda n: (n, 0)), ws, bs, vec, vec],
        out_shape=[jax.ShapeDtypeStruct((S, 2 * GW), BF16), jax.ShapeDtypeStruct((8, BLK, BLK), F32),
                   jax.ShapeDtypeStruct((BLK, 8), F32), jax.ShapeDtypeStruct((1, GW), F32),
                   jax.ShapeDtypeStruct((1, GW), F32)],
        scratch_shapes=[pltpu.VMEM((BLK, GW), F32)],
        compiler_params=_cp(("arbitrary",)),
    )(z, dy, ln_g, ln_b, w_s, b_s_t)


def _merge_fwd(a, b, gl, b_gates, name):
    tm = 256

    def body(a_ref, b_ref, g0_ref, g1_ref, bg_ref, o_ref, ot_ref):
        g0 = jax.nn.sigmoid(g0_ref[...] + bg_ref[:, :D])
        g1 = jax.nn.sigmoid(g1_ref[...] + bg_ref[:, D:])
        mg = g0 * a_ref[...] + g1 * b_ref[...]
        o_ref[...] = mg.astype(BF16)
        ot_ref[...] = mg.T.astype(BF16)

    row = pl.BlockSpec((tm, D), lambda i: (i, 0))
    return pl.pallas_call(
        body, name=name, grid=(S // tm,),
        in_specs=[row, row, row, pl.BlockSpec((tm, D), lambda i: (i, 1)), pl.BlockSpec((1, 2 * D), lambda i: (0, 0))],
        out_specs=[row, pl.BlockSpec((D, tm), lambda i: (0, i))],
        out_shape=[jax.ShapeDtypeStruct((S, D), BF16), jax.ShapeDtypeStruct((D, S), BF16)],
        compiler_params=_cp(("parallel",)),
    )(a, b, gl, gl, b_gates)


def _merge_bwd(dm, a, b, gl, b_gates, name):
    tm = 256

    def body(dm_ref, a_ref, b_ref, g0_ref, g1_ref, bg_ref, da_ref, db_ref, dgl_ref, dbg_ref):
        i = pl.program_id(0)
        dmv = dm_ref[...]
        g0 = jax.nn.sigmoid(g0_ref[...] + bg_ref[:, :D])
        g1 = jax.nn.sigmoid(g1_ref[...] + bg_ref[:, D:])
        da_ref[...] = (dmv * g0).astype(BF16)
        db_ref[...] = (dmv * g1).astype(BF16)
        d0 = dmv * a_ref[...] * g0 * (1.0 - g0)
        d1 = dmv * b_ref[...] * g1 * (1.0 - g1)
        dgl_ref[:, :D] = d0.astype(BF16)
        dgl_ref[:, D:] = d1.astype(BF16)
        s0 = jnp.sum(d0, axis=0, keepdims=True)
        s1 = jnp.sum(d1, axis=0, keepdims=True)

        @pl.when(i == 0)
        def _():
            dbg_ref[:, :D] = s0
            dbg_ref[:, D:] = s1

        @pl.when(i > 0)
        def _():
            dbg_ref[:, :D] += s0
            dbg_ref[:, D:] += s1

    row = pl.BlockSpec((tm, D), lambda i: (i, 0))
    wide = pl.BlockSpec((tm, 2 * D), lambda i: (i, 0))
    bg = pl.BlockSpec((1, 2 * D), lambda i: (0, 0))
    return pl.pallas_call(
        body, name=name, grid=(S // tm,),
        in_specs=[row, row, row, row, pl.BlockSpec((tm, D), lambda i: (i, 1)), bg],
        out_specs=[row, row, wide, bg],
        out_shape=[jax.ShapeDtypeStruct((S, D), BF16), jax.ShapeDtypeStruct((S, D), BF16),
                   jax.ShapeDtypeStruct((S, 2 * D), BF16), jax.ShapeDtypeStruct((1, 2 * D), F32)],
        compiler_params=_cp(("arbitrary",)),
    )(dm, a, b, gl, gl, b_gates)


def _adam_math(w, g, m, v):
    m2 = ADAM_B1 * m + (1.0 - ADAM_B1) * g
    v2 = ADAM_B2 * v + (1.0 - ADAM_B2) * (g * g)
    m_hat = m2 / (1.0 - ADAM_B1 ** ADAM_STEP)
    v_hat = v2 / (1.0 - ADAM_B2 ** ADAM_STEP)
    delta = -ADAM_LR * (m_hat / (jnp.sqrt(v_hat) + ADAM_EPS) + ADAM_WD * w)
    return delta, m2, v2


def _pick_rows(rows, cols, unit=16, budget=MIB):
    best = unit
    for t in range(unit, rows + 1, unit):
        if rows % t == 0 and t * cols * 4 <= budget:
            best = t
    assert rows % best == 0
    return best


def _adamw(w, g, m, v, name):
    r, c = w.shape
    tr = _pick_rows(r, c, unit=8)

    def body(w_ref, g_ref, m_ref, v_ref, go_ref, d_ref, mo_ref, vo_ref):
        gv = g_ref[...]
        delta, m2, v2 = _adam_math(w_ref[...], gv, m_ref[...], v_ref[...])
        go_ref[...] = gv
        d_ref[...] = delta
        mo_ref[...] = m2
        vo_ref[...] = v2

    blk = pl.BlockSpec((tr, c), lambda i: (i, 0))
    shp = jax.ShapeDtypeStruct((r, c), F32)
    return pl.pallas_call(
        body, name=name, grid=(r // tr,), in_specs=[blk] * 4, out_specs=[blk] * 4, out_shape=[shp] * 4,
        compiler_params=_cp(("parallel",)),
    )(w, g, m, v)


def _small_sum_adamw(parts, w, m, v, name):
    tr = 48

    def body(p_ref, w_ref, m_ref, v_ref, g_ref, d_ref, mo_ref, vo_ref):
        gv = p_ref[0]
        for k in range(1, 8):
            gv = gv + p_ref[k]
        delta, m2, v2 = _adam_math(w_ref[...], gv, m_ref[...], v_ref[...])
        g_ref[...] = gv
        d_ref[...] = delta
        mo_ref[...] = m2
        vo_ref[...] = v2

    blk = pl.BlockSpec((tr, D), lambda i: (i, 0))
    shp = jax.ShapeDtypeStruct((SMALL_ROWS, D), F32)
    return pl.pallas_call(
        body, name=name, grid=(SMALL_ROWS // tr,),
        in_specs=[pl.BlockSpec((8, tr, D), lambda i: (0, i, 0)), blk, blk, blk],
        out_specs=[blk] * 4, out_shape=[shp] * 4,
        compiler_params=_cp(("parallel",)),
    )(parts, w, m, v)


ANY = pl.BlockSpec(memory_space=pl.ANY)


def _mesh_pos():
    x, y, c = lax.axis_index("x"), lax.axis_index("y"), lax.axis_index("c")
    chips = [(1 - x, y), (x, 1 - y), (1 - x, 1 - y)]
    return x, y, c, chips


def _place_shard(w, kind, pos, name):
    r, c = w.shape
    tr = _pick_rows(r, c)

    def body(pos_ref, w_ref, o_ref):
        o_ref[...] = w_ref[...].astype(BF16)

    if kind == "stack":
        o_spec = pl.BlockSpec((None, tr, c), lambda i, p: (p[1], i, 0))
        shape = (NSH, r, c)
    else:
        o_spec = pl.BlockSpec((tr, c), lambda i, p: (i, p[1]))
        shape = (r, NSH * c)
    return pl.pallas_call(
        body, name=name,
        grid_spec=pltpu.PrefetchScalarGridSpec(
            num_scalar_prefetch=1, grid=(r // tr,),
            in_specs=[pl.BlockSpec((tr, c), lambda i, p: (i, 0))], out_specs=o_spec),
        out_shape=jax.ShapeDtypeStruct(shape, BF16),
        compiler_params=_cp(("parallel",)),
    )(pos, w)


SEM = pl.BlockSpec(memory_space=pltpu.SEMAPHORE)
SPLIT_COPY = pltpu.CompilerParams(has_side_effects=pltpu.SideEffectType.DATAFLOW_SIDE_EFFECTING)


def _shard_window(ref, kind, j, h, dims):
    r, c = dims
    rows = pl.ds(pl.multiple_of(h * (r // 2), 16), r // 2)
    if kind == "stack":
        return ref.at[j, rows, :]
    return ref.at[rows, pl.ds(pl.multiple_of(j * c, 128), c)]


def _ici_copy(ref, kind, dims, j, c, sems, idx, to):
    win = _shard_window(ref, kind, j, c, dims)
    return pltpu.make_async_remote_copy(src_ref=win, dst_ref=win, send_sem=sems[0].at[idx], recv_sem=sems[1].at[idx],
                                        device_id=to, device_id_type=MESH_T)


def _gather_start(fulls, kinds, dims, after, name):
    n, na = len(fulls), len(after)

    def body(*refs):
        outs = refs[n + na:2 * n + na]
        send_sems, recv_sems, token = refs[2 * n + na:]
        x, y, c, chips = _mesh_pos()
        for a in range(n):
            for k, chip in enumerate(chips):
                _ici_copy(outs[a], kinds[a], dims[a], 2 * x + y, c, (send_sems, recv_sems), 3 * a + k,
                          (chip[0], chip[1], c)).start()
        token[...] = jnp.zeros_like(token)

    res = pl.pallas_call(
        body, name=name, in_specs=[ANY] * (n + na),
        out_specs=[ANY] * n + [SEM, SEM, pl.BlockSpec(memory_space=pltpu.VMEM)],
        out_shape=[jax.ShapeDtypeStruct(f.shape, BF16) for f in fulls]
        + [pltpu.SemaphoreType.DMA((3 * n,)), pltpu.SemaphoreType.DMA((3 * n,)), jax.ShapeDtypeStruct((8, 128), F32)],
        input_output_aliases={i: i for i in range(n)},
        compiler_params=SPLIT_COPY,
    )(*fulls, *after)
    return res[:n], res[n], res[n + 1], res[n + 2]


def _gather_wait(fulls, send_sems, recv_sems, kinds, dims, after, name):
    n, na = len(fulls), len(after)

    def body(*refs):
        ssem, rsem = refs[n], refs[n + 1]
        outs = refs[n + 2 + na:]
        x, y, c, chips = _mesh_pos()
        for a in range(n):
            for k, chip in enumerate(chips):
                to = (chip[0], chip[1], c)
                _ici_copy(outs[a], kinds[a], dims[a], 2 * x + y, c, (ssem, rsem), 3 * a + k, to).wait_send()
                _ici_copy(outs[a], kinds[a], dims[a], 2 * chip[0] + chip[1], c, (ssem, rsem), 3 * a + k, to).wait_recv()

    return pl.pallas_call(
        body, name=name, in_specs=[ANY] * n + [SEM, SEM] + [ANY] * na, out_specs=[ANY] * n,
        out_shape=[jax.ShapeDtypeStruct(f.shape, BF16) for f in fulls],
        input_output_aliases={i: i for i in range(n)},
        compiler_params=SPLIT_COPY,
    )(*fulls, send_sems, recv_sems, *after)


def _gather_forward(fulls, kinds, dims, name):
    n = len(fulls)

    def body(*refs):
        outs = refs[n:2 * n]
        sems = refs[2 * n:]
        x, y, c, chips = _mesh_pos()
        sib = (x, y, 1 - c)
        cps = []
        for a in range(n):
            for k, chip in enumerate(chips):
                cp = _ici_copy(outs[a], kinds[a], dims[a], 2 * chip[0] + chip[1], c, sems, 3 * a + k, sib)
                cp.start()
                cps.append(cp)
        for a in range(n):
            for k, chip in enumerate(chips):
                _ici_copy(outs[a], kinds[a], dims[a], 2 * chip[0] + chip[1], 1 - c, sems, 3 * a + k, sib).wait_recv()
        for cp in cps:
            cp.wait_send()

    return pl.pallas_call(
        body, name=name, in_specs=[ANY] * n, out_specs=[ANY] * n,
        out_shape=[jax.ShapeDtypeStruct(f.shape, BF16) for f in fulls],
        input_output_aliases={i: i for i in range(n)},
        scratch_shapes=[pltpu.SemaphoreType.DMA((3 * n,)), pltpu.SemaphoreType.DMA((3 * n,))],
    )(*fulls)


def _pair_exchange(grads, name):
    n = len(grads)

    def body(*refs):
        ins, outs = refs[:n], refs[n:2 * n]
        send_sems, recv_sems = refs[2 * n:]
        x, y, c, _ = _mesh_pos()
        cps = []
        for a in range(n):
            cp = pltpu.make_async_remote_copy(
                src_ref=ins[a].at[1 - c], dst_ref=outs[a], send_sem=send_sems.at[a], recv_sem=recv_sems.at[a],
                device_id=(x, y, 1 - c), device_id_type=MESH_T)
            cp.start()
            cps.append(cp)
        for cp in cps:
            cp.wait()

    return pl.pallas_call(
        body, name=name, in_specs=[ANY] * n, out_specs=[ANY] * n,
        out_shape=[jax.ShapeDtypeStruct(g.shape[1:], F32) for g in grads],
        scratch_shapes=[pltpu.SemaphoreType.DMA((n,)), pltpu.SemaphoreType.DMA((n,))],
    )(*grads)


def _pair_sum(g, recv, pos, name):
    _, _, rh, c = g.shape
    tr = _pick_rows(rh, c)

    def body(pos_ref, g_ref, r_ref, o_ref):
        o_ref[...] = (g_ref[...] + r_ref[...]).astype(BF16)

    return pl.pallas_call(
        body, name=name,
        grid_spec=pltpu.PrefetchScalarGridSpec(
            num_scalar_prefetch=1, grid=(NSH, rh // tr),
            in_specs=[pl.BlockSpec((None, None, tr, c), lambda j, r, p: (p[0], j, r, 0)),
                      pl.BlockSpec((None, tr, c), lambda j, r, p: (j, r, 0))],
            out_specs=pl.BlockSpec((None, tr, c), lambda j, r, p: (j, r, 0))),
        out_shape=jax.ShapeDtypeStruct((NSH, rh, c), BF16),
        compiler_params=_cp(("parallel", "parallel")),
    )(pos, g, recv)


def _chip_copy(src, land, a, k, chip, c, sems):
    return pltpu.make_async_remote_copy(
        src_ref=src.at[2 * chip[0] + chip[1]], dst_ref=land.at[k], send_sem=sems[0].at[3 * a + k],
        recv_sem=sems[1].at[3 * a + k], device_id=(chip[0], chip[1], c), device_id_type=MESH_T)


def _chip_start(psums, lands, name):
    n = len(psums)

    def body(*refs):
        srcs, dsts = refs[2 * n:3 * n], refs[3 * n:4 * n]
        send_sems, recv_sems, token = refs[4 * n:]
        x, y, c, chips = _mesh_pos()
        for a in range(n):
            for k, chip in enumerate(chips):
                _chip_copy(srcs[a], dsts[a], a, k, chip, c, (send_sems, recv_sems)).start()
        token[...] = jnp.zeros_like(token)

    res = pl.pallas_call(
        body, name=name, in_specs=[ANY] * (2 * n),
        out_specs=[ANY] * (2 * n) + [SEM, SEM, pl.BlockSpec(memory_space=pltpu.VMEM)],
        out_shape=[jax.ShapeDtypeStruct(p.shape, BF16) for p in psums]
        + [jax.ShapeDtypeStruct(l.shape, BF16) for l in lands]
        + [pltpu.SemaphoreType.DMA((3 * n,)), pltpu.SemaphoreType.DMA((3 * n,)), jax.ShapeDtypeStruct((8, 128), F32)],
        input_output_aliases={i: i for i in range(2 * n)},
        compiler_params=SPLIT_COPY,
    )(*psums, *lands)
    return res[:n], res[n:2 * n], res[2 * n], res[2 * n + 1], res[2 * n + 2]


def _chip_wait(psums, lands, send_sems, recv_sems, after, name):
    n, na = len(psums), len(after)

    def body(*refs):
        ssem, rsem = refs[2 * n], refs[2 * n + 1]
        outs = refs[2 * n + 2 + na:]
        srcs, dsts = outs[:n], outs[n:]
        x, y, c, chips = _mesh_pos()
        for a in range(n):
            for k, chip in enumerate(chips):
                cp = _chip_copy(srcs[a], dsts[a], a, k, chip, c, (ssem, rsem))
                cp.wait_send()
                cp.wait_recv()

    res = pl.pallas_call(
        body, name=name, in_specs=[ANY] * (2 * n) + [SEM, SEM] + [ANY] * na, out_specs=[ANY] * (2 * n),
        out_shape=[jax.ShapeDtypeStruct(p.shape, BF16) for p in psums]
        + [jax.ShapeDtypeStruct(l.shape, BF16) for l in lands],
        input_output_aliases={i: i for i in range(2 * n)},
        compiler_params=SPLIT_COPY,
    )(*psums, *lands, send_sems, recv_sems, *after)
    return res[n:]


def _owner_sum(g, recv_a, recv_b, pos, name):
    _, _, rh, c = g.shape
    tr = _pick_rows(rh, c)

    def body(pos_ref, g_ref, ra_ref, rb_ref, o_ref):
        acc = g_ref[...] + ra_ref[...]
        for k in range(3):
            acc = acc + rb_ref[k].astype(F32)
        o_ref[...] = acc

    return pl.pallas_call(
        body, name=name,
        grid_spec=pltpu.PrefetchScalarGridSpec(
            num_scalar_prefetch=1, grid=(rh // tr,),
            in_specs=[pl.BlockSpec((None, None, tr, c), lambda r, p: (p[0], p[1], r, 0)),
                      pl.BlockSpec((None, tr, c), lambda r, p: (p[1], r, 0)),
                      pl.BlockSpec((3, tr, c), lambda r, p: (0, r, 0))],
            out_specs=pl.BlockSpec((None, tr, c), lambda r, p: (p[0], r, 0))),
        out_shape=jax.ShapeDtypeStruct((2, rh, c), F32),
        compiler_params=_cp(("parallel",)),
    )(pos, g, recv_a, recv_b)


def _sibling_allgather(halves, name):
    n = len(halves)

    def body(*refs):
        outs = refs[n:2 * n]
        send_sems, recv_sems = refs[2 * n:]
        x, y, c, _ = _mesh_pos()
        cps = []
        for a in range(n):
            cp = pltpu.make_async_remote_copy(
                src_ref=outs[a].at[c], dst_ref=outs[a].at[c], send_sem=send_sems.at[a], recv_sem=recv_sems.at[a],
                device_id=(x, y, 1 - c), device_id_type=MESH_T)
            cp.start()
            cps.append(cp)
        for a in range(n):
            cps[a].wait_send()
            pltpu.make_async_remote_copy(
                src_ref=outs[a].at[1 - c], dst_ref=outs[a].at[1 - c], send_sem=send_sems.at[a],
                recv_sem=recv_sems.at[a], device_id=(x, y, 1 - c), device_id_type=MESH_T).wait_recv()

    return pl.pallas_call(
        body, name=name, in_specs=[ANY] * n, out_specs=[ANY] * n,
        out_shape=[jax.ShapeDtypeStruct(h.shape, F32) for h in halves],
        input_output_aliases={i: i for i in range(n)},
        scratch_shapes=[pltpu.SemaphoreType.DMA((n,)), pltpu.SemaphoreType.DMA((n,))],
    )(*halves)


def _small_allgather(part, after):
    m_per = SMALL_ROWS
    na = len(after)

    def body(x_ref, *refs):
        out_ref, send_sems, recv_sems, local_sem = refs[na:]
        x, y, c, chips = _mesh_pos()
        me, sibling = (x, y, c), (x, y, 1 - c)

        def rows(px, py, pc):
            return out_ref.at[pl.ds((4 * px + 2 * py + pc) * m_per, m_per), :]

        def copy(k, block, to, src=None):
            return pltpu.make_async_remote_copy(
                src_ref=rows(*block) if src is None else src, dst_ref=rows(*block),
                send_sem=send_sems.at[k], recv_sem=recv_sems.at[k], device_id=to, device_id_type=MESH_T)

        mine = pltpu.make_async_copy(x_ref, rows(*me), local_sem)
        mine.start()
        first = [copy(0, me, sibling, src=x_ref)]
        first += [copy(1 + j, me, (*chip, c), src=x_ref) for j, chip in enumerate(chips)]
        for cp in first:
            cp.start()
        passed = [copy(4 + j, (*chip, c), sibling) for j, chip in enumerate(chips)]
        for j, chip in enumerate(chips):
            copy(1 + j, (*chip, c), me).wait_recv()
            passed[j].start()
        copy(0, sibling, me).wait_recv()
        for j, chip in enumerate(chips):
            copy(4 + j, (*chip, 1 - c), me).wait_recv()
        for cp in first + passed:
            cp.wait_send()
        mine.wait()

    return pl.pallas_call(
        body, name="small_allgather",
        out_shape=jax.ShapeDtypeStruct((8 * m_per, D), F32),
        in_specs=[pl.BlockSpec(memory_space=pltpu.VMEM)] + [ANY] * na, out_specs=pl.BlockSpec(memory_space=pltpu.VMEM),
        scratch_shapes=[pltpu.SemaphoreType.DMA((7,)), pltpu.SemaphoreType.DMA((7,)), pltpu.SemaphoreType.DMA],
    )(part, *after)


def _pack_small(ln1_g, ln1_b, gln_g, gln_b, ln2_g, ln2_b, ln3_g, ln3_b, b_gates, b_s, w_s):
    rows = [ln1_g, ln1_b, gln_g, gln_b, ln2_g, ln2_b, ln3_g, ln3_b]
    rows = [r.reshape(1, D) for r in rows] + [b_gates.reshape(2, D), b_s.reshape(1, D), jnp.zeros((5, D), F32),
                                             w_s.reshape(128, D)]
    return jnp.concatenate(rows, axis=0)


def _unpack_small(p):
    out = [p[i:i + 1] for i in range(8)]
    return out + [p[8:10].reshape(1, 2 * D), p[10:11].reshape(1, 8, BLK), p[16:144].reshape(1, 8, BLK, BLK)]


GROUPS = (("f1g", "f1u", "f1d"), ("w_in",), ("w_ab", "w_gb", "w_out"), ("f2g", "f2u", "f2d"))


def _local_step(x, pos_f, target, P, weights_of, grads_ready):
    invf = ROPE_THETA ** (-jnp.arange(0, DH, 2, dtype=F32) / DH)
    invf = jnp.tile(invf, 4).reshape(1, 128)
    b_s_t = P["gmlp_b_s"].T

    W = dict(weights_of(0, []))
    h1, h1b, xh1, rstd1, a1, b1, h1t = _ffn_fwd(x, W["f1g"], W["f1u"], W["f1d"], P["ln1_g"], P["ln1_b"], "ffn1_fwd",
                                                emit_t=True)
    W.update(weights_of(1, [h1b]))
    qkv = _matmul(h1b, W["w_in"], "nn", "proj_qkv", n=3 * ATT_W, b_col0=0, tm=1024, tn=ATT_W)
    z = _matmul(h1b, W["w_in"], "nn", "proj_z", n=2 * GW, b_col0=3 * ATT_W, tm=S, tn=512)
    gl = _matmul(h1b, W["w_in"], "nn", "proj_gates", n=2 * D, b_col0=3 * ATT_W + 2 * GW, tm=S, tn=512)
    qkv_c = _rope_fwd(qkv, pos_f, invf, "rope_fwd")
    og = [_attn_fwd(gi, qkv_c[gi], "attn_fwd_g%d" % gi) for gi in range(NG)]
    y_attn, y_attn_t, lse = _attn_combine([o for o, _ in og], [l for _, l in og], "attn_combine")
    y_gmlp, y_gmlp_t = _gmlp_fwd(z, P["gmlp_ln_g"], P["gmlp_ln_b"], P["gmlp_w_s"], b_s_t, "gmlp_fwd")
    W.update(weights_of(2, [y_gmlp]))
    br_a = _matmul(y_attn, W["w_ab"], "nn", "branch_attn", n=D, tm=1024, tn=D)
    br_b = _matmul(y_gmlp, W["w_gb"], "nn", "branch_gmlp", n=D, tm=1024, tn=D)
    merged, merged_t = _merge_fwd(br_a, br_b, gl, P["b_gates"], "merge_fwd")
    mix = _matmul(merged, W["w_out"], "nn", "mix_out", n=D, tm=1024, tn=D)
    h2, h2b, xh2, rstd2 = _resid_ln(h1, mix, P["ln2_g"], P["ln2_b"], "resid_ln2")
    W.update(weights_of(3, [h2b]))
    y, _, xh3, rstd3, a2, b2 = _ffn_fwd(h2, W["f2g"], W["f2u"], W["f2d"], P["ln3_g"], P["ln3_b"], "ffn2_fwd")

    dr3, dg3, db3, loss = _ln_bwd(y, xh3, rstd3, P["ln3_g"], "loss_ln3_bwd", target=target)
    g_f2g, g_f2u, g_f2d, dh2 = _ffn_bwd(dr3, h2b, a2, b2, W["f2g"], W["f2u"], W["f2d"], "ffn2_bwd")
    tok = grads_ready(3, dict(f2g=g_f2g, f2u=g_f2u, f2d=g_f2d))
    dr2, dg2, db2 = _ln_bwd(dh2, xh2, rstd2, P["ln2_g"], "ln2_bwd", after=tok)
    g_wout = _wgrad(merged_t, dr2, 128, D, "dw_out", row_sharded=True)
    dmerged = _matmul(dr2, W["w_out"], "nt", "dmerged", n=D, tm=1024, tn=D)
    dab, dbb, dglb, dbg = _merge_bwd(dmerged, br_a, br_b, gl, P["b_gates"], "merge_bwd")
    g_wab = _wgrad(y_attn_t, dab, GRP_W // 2, 256, "dw_attn_branch", row_sharded=False)
    g_wgb = _wgrad(y_gmlp_t, dbb, 128, D, "dw_gmlp_branch", row_sharded=True)
    tok = grads_ready(2, dict(w_ab=g_wab, w_gb=g_wgb, w_out=g_wout))
    dy_attn = _matmul(dab, W["w_ab"], "nt", "dy_attn", n=GRP_W, tm=1024, tn=GRP_W, after=tok)
    dy_gmlp = _matmul(dbb, W["w_gb"], "nt", "dy_gmlp", n=GW, tm=1024, tn=GW)
    dzb, dws, dbs_t, dgln_g, dgln_b = _gmlp_bwd(z, dy_gmlp, P["gmlp_ln_g"], P["gmlp_ln_b"], P["gmlp_w_s"], b_s_t,
                                                 "gmlp_bwd")
    cls = _class_order([dy_attn, y_attn, lse], "attn_class_order")
    dqkv_c = []
    for gi in range(NG):
        dy_c, y_c, lse_c = [t[None] if gi == 0 else cls[2 * a + gi - 1] for a, t in enumerate((dy_attn, y_attn, lse))]
        dqkv_c.append(_attn_bwd(gi, qkv_c[gi], dy_c, y_c, lse_c, "attn_bwd_g%d" % gi))
    dqkvb = _rope_bwd(dqkv_c, pos_f, invf, "rope_bwd")
    dproj = jnp.concatenate([dqkvb, dzb, dglb], axis=1)
    g_win = _wgrad(h1t, dproj, D // 2, IN_SH, "dw_in", row_sharded=False)
    tok = grads_ready(1, dict(w_in=g_win))
    dh1 = _matmul(dproj, W["w_in"], "nt", "dh1", n=D, tn=D, tk=IN_SH, add=dr2, add_scale=ALPHA, after=tok)
    dr1, dg1, db1 = _ln_bwd(dh1, xh1, rstd1, P["ln1_g"], "ln1_bwd")
    g_f1g, g_f1u, g_f1d, dx = _ffn_bwd(dr1, x.astype(BF16), a1, b1, W["f1g"], W["f1u"], W["f1d"], "ffn1_bwd")
    grads_ready(0, dict(f1g=g_f1g, f1u=g_f1u, f1d=g_f1d))

    small = _pack_small(dg1, db1, dgln_g, dgln_b, dg2, db2, dg3, db3, dbg, dbs_t.T, dws)
    return loss, dx, small


BIG = ("f1g", "f1u", "f1d", "w_in", "w_ab", "w_gb", "w_out", "f2g", "f2u", "f2d")
TRANSPOSED = ("f1g", "f1u", "f2g", "f2u")
KIND = dict(f1g="stack", f1u="stack", f1d="stack", w_in="col", w_ab="col", w_gb="stack", w_out="stack",
            f2g="stack", f2u="stack", f2d="stack")


def kernel(x, positions, ffn1_w_gate, ffn1_w_up, ffn1_w_down, ln1_g, ln1_b, w_in, b_gates, gmlp_ln_g, gmlp_ln_b, gmlp_w_s, gmlp_b_s, w_attn_branch, w_gmlp_branch, w_out, ln2_g, ln2_b, ffn2_w_gate, ffn2_w_up, ffn2_w_down, ln3_g, ln3_b, loss_target, m_ffn1_w_gate, m_ffn1_w_up, m_ffn1_w_down, m_ln1_g, m_ln1_b, m_w_in, m_b_gates, m_gmlp_ln_g, m_gmlp_ln_b, m_gmlp_w_s, m_gmlp_b_s, m_w_attn_branch, m_w_gmlp_branch, m_w_out, m_ln2_g, m_ln2_b, m_ffn2_w_gate, m_ffn2_w_up, m_ffn2_w_down, m_ln3_g, m_ln3_b, v_ffn1_w_gate, v_ffn1_w_up, v_ffn1_w_down, v_ln1_g, v_ln1_b, v_w_in, v_b_gates, v_gmlp_ln_g, v_gmlp_ln_b, v_gmlp_w_s, v_gmlp_b_s, v_w_attn_branch, v_w_gmlp_branch, v_w_out, v_ln2_g, v_ln2_b, v_ffn2_w_gate, v_ffn2_w_up, v_ffn2_w_down, v_ln3_g, v_ln3_b):
    cx, cy, cc = lax.axis_index("x"), lax.axis_index("y"), lax.axis_index("c")
    pos = jnp.stack([cc, 2 * cx + cy]).astype(jnp.int32)

    w_sh = dict(f1g=ffn1_w_gate, f1u=ffn1_w_up, f1d=ffn1_w_down, w_in=w_in, w_ab=w_attn_branch,
                w_gb=w_gmlp_branch, w_out=w_out, f2g=ffn2_w_gate, f2u=ffn2_w_up, f2d=ffn2_w_down)
    m_sh = dict(f1g=m_ffn1_w_gate, f1u=m_ffn1_w_up, f1d=m_ffn1_w_down, w_in=m_w_in, w_ab=m_w_attn_branch,
                w_gb=m_w_gmlp_branch, w_out=m_w_out, f2g=m_ffn2_w_gate, f2u=m_ffn2_w_up, f2d=m_ffn2_w_down)
    v_sh = dict(f1g=v_ffn1_w_gate, f1u=v_ffn1_w_up, f1d=v_ffn1_w_down, w_in=v_w_in, w_ab=v_w_attn_branch,
                w_gb=v_w_gmlp_branch, w_out=v_w_out, f2g=v_ffn2_w_gate, f2u=v_ffn2_w_up, f2d=v_ffn2_w_down)
    w_sh = {k: (v[0].T if k in TRANSPOSED else v[0]) for k, v in w_sh.items()}
    m_sh = {k: (v[0].T if k in TRANSPOSED else v[0]) for k, v in m_sh.items()}
    v_sh = {k: (v[0].T if k in TRANSPOSED else v[0]) for k, v in v_sh.items()}

    started, tokens = [], []
    for gi, names in enumerate(GROUPS):
        placed = [_place_shard(w_sh[k], KIND[k], pos, "place_" + k) for k in names]
        fulls, ssem, rsem, token = _gather_start(placed, [KIND[k] for k in names], [w_sh[k].shape for k in names],
                                                 tokens[-1:], "gather_start_g%d" % gi)
        started.append((fulls, ssem, rsem))
        tokens.append(token)

    def weights_of(gi, after):
        names = GROUPS[gi]
        kinds, dims = [KIND[k] for k in names], [w_sh[k].shape for k in names]
        fulls, ssem, rsem = started[gi]
        fulls = _gather_wait(fulls, ssem, rsem, kinds, dims, list(after) + (tokens if gi == 0 else []),
                             "gather_wait_g%d" % gi)
        fulls = _gather_forward(fulls, kinds, dims, "gather_forward_g%d" % gi)
        return {k: (f.reshape(D, D) if k in ("w_gb", "w_out") else f) for k, f in zip(names, fulls)}

    inflight = {}

    def grads_ready(gi, gd):
        names = GROUPS[gi]
        grads = [gd[k] for k in names]
        recv_a = _pair_exchange(grads, "rs_pair_exchange_g%d" % gi)
        psums = [_pair_sum(g, r, pos, "rs_pair_sum_" + k) for g, r, k in zip(grads, recv_a, names)]
        lands = [lax.empty((3,) + p.shape[1:], BF16) for p in psums]
        psums, lands, ssem, rsem, token = _chip_start(psums, lands, "rs_chip_start_g%d" % gi)
        inflight[gi] = (grads, recv_a, psums, lands, ssem, rsem, token)
        return [token]

    P = dict(ln1_g=ln1_g, ln1_b=ln1_b, ln2_g=ln2_g, ln2_b=ln2_b, ln3_g=ln3_g, ln3_b=ln3_b, b_gates=b_gates,
             gmlp_ln_g=gmlp_ln_g, gmlp_ln_b=gmlp_ln_b, gmlp_w_s=gmlp_w_s[0], gmlp_b_s=gmlp_b_s[0])
    pos_f = positions.reshape(S, 1).astype(F32)
    loss_part, dx, small = _local_step(x[0], pos_f, loss_target[0], P, weights_of, grads_ready)
    loss = lax.psum(loss_part[0, 0], ("x", "y", "c"))

    g_out, d_out, m_out, v_out = {}, {}, {}, {}

    def finish(gi, after):
        grads, recv_a, psums, lands, ssem, rsem, token = inflight[gi]
        recv_b = _chip_wait(psums, lands, ssem, rsem, after + [inflight[0][6]], "rs_chip_wait_g%d" % gi)
        halves = [_owner_sum(g, ra, rb, pos, "rs_owner_sum_" + k)
                  for g, ra, rb, k in zip(grads, recv_a, recv_b, GROUPS[gi])]
        reduced = _sibling_allgather(halves, "rs_sibling_allgather_g%d" % gi)
        for k, gfull in zip(GROUPS[gi], reduced):
            res = _adamw(w_sh[k], gfull.reshape(w_sh[k].shape), m_sh[k], v_sh[k], "adamw_" + k)
            after = [res[1]]
            if k in TRANSPOSED:
                res = [r.T for r in res]
            g_out[k], d_out[k], m_out[k], v_out[k] = [r[None] for r in res]
        return after

    after = []
    for gi in (3, 2, 1):
        after = finish(gi, after)

    parts = _small_allgather(small, after).reshape(8, SMALL_ROWS, D)
    sp = (ln1_g, ln1_b, gmlp_ln_g, gmlp_ln_b, ln2_g, ln2_b, ln3_g, ln3_b, b_gates, gmlp_b_s, gmlp_w_s)
    sm = (m_ln1_g, m_ln1_b, m_gmlp_ln_g, m_gmlp_ln_b, m_ln2_g, m_ln2_b, m_ln3_g, m_ln3_b, m_b_gates, m_gmlp_b_s,
          m_gmlp_w_s)
    sv = (v_ln1_g, v_ln1_b, v_gmlp_ln_g, v_gmlp_ln_b, v_ln2_g, v_ln2_b, v_ln3_g, v_ln3_b, v_b_gates, v_gmlp_b_s,
          v_gmlp_w_s)
    sg, sd, smn, svn = _small_sum_adamw(parts, _pack_small(*sp), _pack_small(*sm), _pack_small(*sv), "small_adamw")
    names = ("ln1_g", "ln1_b", "gmlp_ln_g", "gmlp_ln_b", "ln2_g", "ln2_b", "ln3_g", "ln3_b", "b_gates", "gmlp_b_s",
             "gmlp_w_s")
    for dst, packed in ((g_out, sg), (d_out, sd), (m_out, smn), (v_out, svn)):
        for nm, val in zip(names, _unpack_small(packed)):
            dst[nm] = val
    finish(0, [sg])

    order = ("f1g", "f1u", "f1d", "ln1_g", "ln1_b", "w_in", "b_gates", "gmlp_ln_g", "gmlp_ln_b", "gmlp_w_s", "gmlp_b_s",
             "w_ab", "w_gb", "w_out", "ln2_g", "ln2_b", "f2g", "f2u", "f2d", "ln3_g", "ln3_b")
    outs = [loss, dx[None]]
    for dst in (g_out, d_out, m_out, v_out):
        outs += [dst[k] for k in order]
    return tuple(outs)
```

```python
import functools
import math

import jax
import jax.numpy as jnp
from jax import lax
from jax.experimental import pallas as pl
from jax.experimental.pallas import tpu as pltpu

F32 = jnp.float32
BF16 = jnp.bfloat16

S = 2048
D = 1024
NSH = 4
FSH = 704
ATT_W = 1536
GRP_W = 512
NG = 3
NH = 8
DH = 64
BLK = 128
NBLK = S // BLK
GW = 1024
IN_W = 8704
IN_SH = IN_W // NSH
ALPHA = 2.0 ** 0.25
LN_EPS = 1e-5
ROPE_THETA = 10000.0
DILATIONS = (1, 4, 16)
ADAM_LR, ADAM_B1, ADAM_B2, ADAM_EPS, ADAM_WD, ADAM_STEP = 0.001, 0.9, 0.999, 1e-08, 0.01, 10
SMALL_ROWS = 144
MESH_T = pl.DeviceIdType.MESH
MIB = 1024 * 1024
NEG_INF = float("-inf")


def _cp(sem, vmem_mib=48):
    return pltpu.CompilerParams(dimension_semantics=sem, vmem_limit_bytes=vmem_mib * MIB)


def _ln_stats(r):
    mu = jnp.mean(r, axis=-1, keepdims=True)
    xc = r - mu
    var = jnp.mean(xc * xc, axis=-1, keepdims=True)
    rstd = lax.rsqrt(var + LN_EPS)
    return xc * rstd, rstd


def _ln_dx(dxh, xh, rstd):
    m1 = jnp.mean(dxh, axis=-1, keepdims=True)
    m2 = jnp.mean(dxh * xh, axis=-1, keepdims=True)
    return rstd * (dxh - m1 - xh * m2)


def _dot_nt(a, b):
    return lax.dot_general(a, b, (((1,), (1,)), ((), ())), preferred_element_type=F32)


def _dot_tn(a, b):
    return lax.dot_general(a, b, (((0,), (0,)), ((), ())), preferred_element_type=F32)


def _dot(a, b):
    return jnp.dot(a, b, preferred_element_type=F32)


def _ffn_fwd(xin, wgt, wut, wd, ln_g, ln_b, name, emit_t=False):
    tm = 512

    def body(x_ref, wg_ref, wu_ref, wd_ref, g_ref, b_ref, *rest):
        if emit_t:
            h_ref, hb_ref, xh_ref, rstd_ref, a_ref, bb_ref, ht_ref, acc_ref = rest
        else:
            h_ref, hb_ref, xh_ref, rstd_ref, a_ref, bb_ref, acc_ref = rest
        j = pl.program_id(1)
        xb = x_ref[...].astype(BF16)
        a = _dot_nt(xb, wg_ref[...])
        b = _dot_nt(xb, wu_ref[...])
        a_ref[...] = a.astype(BF16)
        bb_ref[...] = b.astype(BF16)
        s = (a * jax.nn.sigmoid(a)) * b
        f = _dot(s.astype(BF16), wd_ref[...])

        @pl.when(j == 0)
        def _():
            acc_ref[...] = f

        @pl.when(j > 0)
        def _():
            acc_ref[...] += f

        @pl.when(j == NSH - 1)
        def _():
            r = ALPHA * x_ref[...] + 0.5 * acc_ref[...]
            xh, rstd = _ln_stats(r)
            h = xh * g_ref[...] + b_ref[...]
            h_ref[...] = h
            hb_ref[...] = h.astype(BF16)
            xh_ref[...] = xh
            rstd_ref[...] = rstd
            if emit_t:
                ht_ref[...] = h.T.astype(BF16)

    row = pl.BlockSpec((tm, D), lambda i, j: (i, 0))
    vec = pl.BlockSpec((1, D), lambda i, j: (0, 0))
    wsp = pl.BlockSpec((None, FSH, D), lambda i, j: (j, 0, 0))
    ab = pl.BlockSpec((None, tm, FSH), lambda i, j: (j, i, 0))
    out_specs = [row, row, row, pl.BlockSpec((tm, 1), lambda i, j: (i, 0)), ab, ab]
    out_shape = [jax.ShapeDtypeStruct((S, D), F32), jax.ShapeDtypeStruct((S, D), BF16),
                 jax.ShapeDtypeStruct((S, D), F32), jax.ShapeDtypeStruct((S, 1), F32),
                 jax.ShapeDtypeStruct((NSH, S, FSH), BF16), jax.ShapeDtypeStruct((NSH, S, FSH), BF16)]
    if emit_t:
        out_specs.append(pl.BlockSpec((D, tm), lambda i, j: (0, i)))
        out_shape.append(jax.ShapeDtypeStruct((D, S), BF16))
    return pl.pallas_call(
        body, name=name, grid=(S // tm, NSH),
        in_specs=[row, wsp, wsp, wsp, vec, vec], out_specs=out_specs, out_shape=out_shape,
        scratch_shapes=[pltpu.VMEM((tm, D), F32)],
        compiler_params=_cp(("parallel", "arbitrary")),
    )(xin, wgt, wut, wd, ln_g, ln_b)


def _ffn_bwd(dr, xin_b, a, b, wgt, wut, wd, name, after=()):
    tm = 512
    ni = S // tm
    hr = FSH // 2

    def body(dr_ref, a_ref, b_ref, wg_ref, wu_ref, wd_ref, x_hbm, *rest):
        dwg_hbm, dwu_hbm, dwd_hbm, dx_hbm, dx_acc, da_all, db_all, s_all, df_all, x_all, res_buf, sems = rest[len(after):]
        j = pl.program_id(0)
        i = pl.program_id(1)
        rows = pl.ds(pl.multiple_of(i * tm, tm), tm)

        @pl.when(jnp.logical_and(j == 0, i == 0))
        def _():
            cp = pltpu.make_async_copy(x_hbm, x_all, sems.at[0])
            cp.start()
            cp.wait()

        drv = dr_ref[...]
        df = (0.5 * drv).astype(BF16)

        @pl.when(j == 0)
        def _():
            df_all[rows, :] = df

        ds = _dot_nt(df, wd_ref[...])
        av = a_ref[...].astype(F32)
        bv = b_ref[...].astype(F32)
        sig = jax.nn.sigmoid(av)
        sl = av * sig
        da = (ds * bv * (sig * (1.0 + av * (1.0 - sig)))).astype(BF16)
        db = (ds * sl).astype(BF16)
        da_all[rows, :] = da
        db_all[rows, :] = db
        s_all[rows, :] = (sl * bv).astype(BF16)
        dx = _dot(da, wg_ref[...]) + _dot(db, wu_ref[...])

        @pl.when(j == 0)
        def _():
            dx_acc[rows, :] = ALPHA * drv + dx

        @pl.when(j > 0)
        def _():
            dx_acc[rows, :] += dx

        @pl.when(i == ni - 1)
        def _():
            copies = []
            for n, (lhs, rhs, out) in enumerate(((da_all, x_all, dwg_hbm), (db_all, x_all, dwu_hbm),
                                                 (s_all, df_all, dwd_hbm))):
                slot = n % 2
                if n >= 2:
                    for cp in copies[2 * (n - 2): 2 * (n - 2) + 2]:
                        cp.wait()
                res_buf[slot] = _dot_tn(lhs[...], rhs[...])
                for h in range(2):
                    cp = pltpu.make_async_copy(res_buf.at[slot, pl.ds(h * hr, hr), :], out.at[h, j],
                                               sems.at[1 + 2 * slot + h])
                    cp.start()
                    copies.append(cp)
            for cp in copies[2:]:
                cp.wait()

        @pl.when(jnp.logical_and(j == NSH - 1, i == ni - 1))
        def _():
            cp = pltpu.make_async_copy(dx_acc, dx_hbm, sems.at[0])
            cp.start()
            cp.wait()

    row = pl.BlockSpec((tm, D), lambda j, i: (i, 0))
    wsp = pl.BlockSpec((None, FSH, D), lambda j, i: (j, 0, 0))
    ab = pl.BlockSpec((None, tm, FSH), lambda j, i: (j, i, 0))
    dwshape = jax.ShapeDtypeStruct((2, NSH, hr, D), F32)
    return pl.pallas_call(
        body, name=name, grid=(NSH, ni),
        in_specs=[row, ab, ab, wsp, wsp, wsp, ANY] + [ANY] * len(after),
        out_specs=[ANY, ANY, ANY, ANY],
        out_shape=[dwshape, dwshape, dwshape, jax.ShapeDtypeStruct((S, D), F32)],
        scratch_shapes=[pltpu.VMEM((S, D), F32), pltpu.VMEM((S, FSH), BF16), pltpu.VMEM((S, FSH), BF16),
                        pltpu.VMEM((S, FSH), BF16), pltpu.VMEM((S, D), BF16), pltpu.VMEM((S, D), BF16),
                        pltpu.VMEM((2, FSH, D), F32), pltpu.SemaphoreType.DMA((5,))],
        compiler_params=_cp(("arbitrary", "arbitrary"), vmem_mib=58),
    )(dr, a, b, wgt, wut, wd, xin_b, *after)


def _matmul(a, b, mode, name, *, n, tm=512, tn=512, tk=None, b_col0=0, add=None, add_scale=1.0, out_dtype=F32,
            after=()):
    m, ka = a.shape
    tk = ka if tk is None else tk
    nk = ka // tk
    assert m % tm == 0 and n % tn == 0 and ka % tk == 0 and b_col0 % tn == 0
    off = b_col0 // tn
    na = len(after)

    def body(*refs):
        refs = refs[na:]
        if add is None:
            a_ref, b_ref, o_ref = refs[:3]
            add_ref = None
            rest = refs[3:]
        else:
            a_ref, b_ref, add_ref, o_ref = refs[:4]
            rest = refs[4:]
        k = pl.program_id(2)
        av = a_ref[...].astype(BF16)
        bv = b_ref[...].astype(BF16)
        p = _dot(av, bv) if mode == "nn" else _dot_nt(av, bv)

        def finish(acc):
            if add_ref is not None:
                acc = acc + add_scale * add_ref[...]
            o_ref[...] = acc.astype(out_dtype)

        if nk == 1:
            finish(p)
        else:
            acc_ref = rest[0]

            @pl.when(k == 0)
            def _():
                acc_ref[...] = p

            @pl.when(k > 0)
            def _():
                acc_ref[...] += p

            @pl.when(k == nk - 1)
            def _():
                finish(acc_ref[...])

    a_spec = pl.BlockSpec((tm, tk), lambda i, j, k: (i, k))
    if mode == "nn":
        b_spec = pl.BlockSpec((tk, tn), lambda i, j, k: (k, j + off))
    else:
        b_spec = pl.BlockSpec((tn, tk), lambda i, j, k: (j, k))
    o_spec = pl.BlockSpec((tm, tn), lambda i, j, k: (i, j))
    in_specs = [pl.BlockSpec(memory_space=pl.ANY)] * na + [a_spec, b_spec] + ([o_spec] if add is not None else [])
    args = tuple(after) + (a, b) + ((add,) if add is not None else ())
    return pl.pallas_call(
        body, name=name, grid=(m // tm, n // tn, nk),
        in_specs=in_specs, out_specs=o_spec,
        out_shape=jax.ShapeDtypeStruct((m, n), out_dtype),
        scratch_shapes=[pltpu.VMEM((tm, tn), F32)] if nk > 1 else [],
        compiler_params=_cp(("parallel", "parallel", "arbitrary")),
    )(*args)


def _wgrad(xt, y, rh, c, name, row_sharded, after=()):
    na = len(after)
    if row_sharded:
        def body(x_ref, y_ref, *rest):
            o_ref = rest[na]
            res = _dot(x_ref[...], y_ref[...].astype(BF16))
            for j in range(NSH):
                for h in range(2):
                    o_ref[h, j] = res[(2 * j + h) * rh:(2 * j + h + 1) * rh, :]

        grid = (1,)
        in_specs = [pl.BlockSpec((2 * NSH * rh, S), lambda g: (0, 0)), pl.BlockSpec((S, c), lambda g: (0, 0))]
        out_specs = pl.BlockSpec((2, NSH, rh, c), lambda g: (0, 0, 0, 0))
        sem = ("arbitrary",)
    else:
        def body(x_ref, y_ref, *rest):
            rest[na][...] = _dot(x_ref[...], y_ref[...].astype(BF16))

        grid = (2, NSH)
        in_specs = [pl.BlockSpec((rh, S), lambda h, j: (h, 0)), pl.BlockSpec((S, c), lambda h, j: (0, j))]
        out_specs = pl.BlockSpec((None, None, rh, c), lambda h, j: (h, j, 0, 0))
        sem = ("parallel", "parallel")
    return pl.pallas_call(
        body, name=name, grid=grid, in_specs=in_specs + [pl.BlockSpec(memory_space=pl.ANY)] * na, out_specs=out_specs,
        out_shape=jax.ShapeDtypeStruct((2, NSH, rh, c), F32),
        compiler_params=_cp(sem, vmem_mib=56),
    )(xt, y, *after)


def _resid_ln(res, f, ln_g, ln_b, name):
    tm = 256

    def body(res_ref, f_ref, g_ref, b_ref, h_ref, hb_ref, xh_ref, rstd_ref):
        r = ALPHA * res_ref[...] + f_ref[...]
        xh, rstd = _ln_stats(r)
        h = xh * g_ref[...] + b_ref[...]
        h_ref[...] = h
        hb_ref[...] = h.astype(BF16)
        xh_ref[...] = xh
        rstd_ref[...] = rstd

    row = pl.BlockSpec((tm, D), lambda i: (i, 0))
    vec = pl.BlockSpec((1, D), lambda i: (0, 0))
    return pl.pallas_call(
        body, name=name, grid=(S // tm,),
        in_specs=[row, row, vec, vec],
        out_specs=[row, row, row, pl.BlockSpec((tm, 1), lambda i: (i, 0))],
        out_shape=[jax.ShapeDtypeStruct((S, D), F32), jax.ShapeDtypeStruct((S, D), BF16),
                   jax.ShapeDtypeStruct((S, D), F32), jax.ShapeDtypeStruct((S, 1), F32)],
        compiler_params=_cp(("parallel",)),
    )(res, f, ln_g, ln_b)


def _ln_bwd(dout, xh, rstd, ln_g, name, target=None, after=()):
    tm = 256
    with_loss = target is not None
    na = len(after)

    def body(*refs):
        refs = refs[na:]
        if with_loss:
            y_ref, t_ref, xh_ref, rstd_ref, g_ref, dr_ref, dg_ref, db_ref, loss_ref = refs
            err = y_ref[...] - t_ref[...]
            dy = err * (1.0 / D)
        else:
            y_ref, xh_ref, rstd_ref, g_ref, dr_ref, dg_ref, db_ref = refs
            dy = y_ref[...]
        i = pl.program_id(0)
        xh = xh_ref[...]
        dr_ref[...] = _ln_dx(dy * g_ref[...], xh, rstd_ref[...])
        dg = jnp.sum(dy * xh, axis=0, keepdims=True)
        db = jnp.sum(dy, axis=0, keepdims=True)

        @pl.when(i == 0)
        def _():
            dg_ref[...] = dg
            db_ref[...] = db

        @pl.when(i > 0)
        def _():
            dg_ref[...] += dg
            db_ref[...] += db

        if with_loss:
            part = 0.5 * jnp.sum(jnp.mean(err * err, axis=-1, keepdims=True), axis=0, keepdims=True)
            part = jnp.broadcast_to(part, (8, 128))

            @pl.when(i == 0)
            def _():
                loss_ref[...] = part

            @pl.when(i > 0)
            def _():
                loss_ref[...] += part

    row = pl.BlockSpec((tm, D), lambda i: (i, 0))
    vec = pl.BlockSpec((1, D), lambda i: (0, 0))
    col = pl.BlockSpec((tm, 1), lambda i: (i, 0))
    in_specs = [pl.BlockSpec(memory_space=pl.ANY)] * na + [row] + ([row] if with_loss else []) + [row, col, vec]
    out_specs = [row, vec, vec] + ([pl.BlockSpec((8, 128), lambda i: (0, 0))] if with_loss else [])
    out_shape = [jax.ShapeDtypeStruct((S, D), F32), jax.ShapeDtypeStruct((1, D), F32),
                 jax.ShapeDtypeStruct((1, D), F32)] + ([jax.ShapeDtypeStruct((8, 128), F32)] if with_loss else [])
    args = tuple(after) + (dout,) + ((target,) if with_loss else ()) + (xh, rstd, ln_g)
    return pl.pallas_call(
        body, name=name, grid=(S // tm,), in_specs=in_specs, out_specs=out_specs, out_shape=out_shape,
        compiler_params=_cp(("arbitrary",)),
    )(*args)


ROPE_TM = 256


def _rope_tables(pos_ref, invf_ref, sign):
    ang = pos_ref[...] * invf_ref[...]
    lane = lax.broadcasted_iota(jnp.int32, ang.shape, 1)
    first = (lane % DH) < (DH // 2)
    sinv = jnp.sin(ang) * sign
    return first, jnp.cos(ang), jnp.where(first, -sinv, sinv)


def _rotate(x, first, cosf, sinf):
    return x * cosf + jnp.where(first, pltpu.roll(x, 96, 1), pltpu.roll(x, 32, 1)) * sinf


def _rope_fwd(qkv, pos_f, invf, name):
    tm = ROPE_TM

    def body(t_ref, pos_ref, invf_ref, o0_ref, o1_ref, o2_ref, buf_ref):
        first, cosf, sinf = _rope_tables(pos_ref, invf_ref, 1.0)
        o_refs = (o0_ref, o1_ref, o2_ref)
        for sec in range(3):
            for gi, d in enumerate(DILATIONS):
                for ch in range(GRP_W // 128):
                    src = sec * ATT_W + gi * GRP_W + ch * 128
                    dst = slice(sec * GRP_W + ch * 128, sec * GRP_W + (ch + 1) * 128)
                    x = t_ref[:, src:src + 128]
                    if sec < 2:
                        x = _rotate(x, first, cosf, sinf)
                    if d == 1:
                        o_refs[gi][0, :, dst] = x.astype(BF16)
                    else:
                        buf_ref[...] = x
                        for r in range(d):
                            o_refs[gi][r, :, dst] = buf_ref[pl.ds(r, tm // d, stride=d), :].astype(BF16)

    return pl.pallas_call(
        body, name=name, grid=(S // tm,),
        in_specs=[pl.BlockSpec((tm, 3 * ATT_W), lambda i: (i, 0)), pl.BlockSpec((tm, 1), lambda i: (i, 0)),
                  pl.BlockSpec((1, 128), lambda i: (0, 0))],
        out_specs=[pl.BlockSpec((d, tm // d, 3 * GRP_W), lambda i: (0, i, 0)) for d in DILATIONS],
        out_shape=[jax.ShapeDtypeStruct((d, S // d, 3 * GRP_W), BF16) for d in DILATIONS],
        scratch_shapes=[pltpu.VMEM((tm, 128), F32)],
        compiler_params=_cp(("parallel",)),
    )(qkv, pos_f, invf)


def _rope_bwd(dqkv_c, pos_f, invf, name):
    tm = ROPE_TM

    def body(*refs):
        g_refs, (pos_ref, invf_ref, o_ref, buf_ref) = refs[:9], refs[9:]
        first, cosf, sinf = _rope_tables(pos_ref, invf_ref, -1.0)
        for sec in range(3):
            for gi, d in enumerate(DILATIONS):
                g_ref = g_refs[3 * gi + sec]
                for ch in range(GRP_W // 128):
                    cols = slice(ch * 128, (ch + 1) * 128)
                    if d == 1:
                        x = g_ref[0, :, cols]
                    else:
                        for r in range(d):
                            buf_ref[pl.ds(r, tm // d, stride=d), :] = g_ref[r, :, cols]
                        x = buf_ref[...]
                    if sec < 2:
                        x = _rotate(x, first, cosf, sinf)
                    dst = sec * ATT_W + gi * GRP_W + ch * 128
                    o_ref[:, dst:dst + 128] = x.astype(BF16)

    g_specs = [pl.BlockSpec((d, tm // d, GRP_W), lambda i: (0, i, 0)) for d in DILATIONS for _ in range(3)]
    return pl.pallas_call(
        body, name=name, grid=(S // tm,),
        in_specs=g_specs + [pl.BlockSpec((tm, 1), lambda i: (i, 0)), pl.BlockSpec((1, 128), lambda i: (0, 0))],
        out_specs=pl.BlockSpec((tm, 3 * ATT_W), lambda i: (i, 0)),
        out_shape=jax.ShapeDtypeStruct((S, 3 * ATT_W), BF16),
        scratch_shapes=[pltpu.VMEM((tm, 128), F32)],
        compiler_params=_cp(("parallel",)),
    )(*[g for grp in dqkv_c for g in grp], pos_f, invf)


def _class_order(ts, name):
    tm = ROPE_TM
    n = len(ts)

    def body(*refs):
        buf_ref = refs[3 * n]
        for a in range(n):
            for ch in range(GRP_W // 128):
                cols = slice(ch * 128, (ch + 1) * 128)
                buf_ref[...] = refs[a][:, cols]
                for b, d in enumerate(DILATIONS[1:]):
                    for r in range(d):
                        refs[n + 2 * a + b][r, :, cols] = buf_ref[pl.ds(r, tm // d, stride=d), :]

    return pl.pallas_call(
        body, name=name, grid=(S // tm,),
        in_specs=[pl.BlockSpec((tm, GRP_W), lambda i: (i, 0))] * n,
        out_specs=[pl.BlockSpec((d, tm // d, GRP_W), lambda i: (0, i, 0)) for _ in range(n) for d in DILATIONS[1:]],
        out_shape=[jax.ShapeDtypeStruct((d, S // d, GRP_W), F32) for _ in range(n) for d in DILATIONS[1:]],
        scratch_shapes=[pltpu.VMEM((tm, 128), F32)],
        compiler_params=_cp(("parallel",)),
    )(*ts)


def _attn_fwd(gi, qkv_c, name):
    d = DILATIONS[gi]
    nblk = S // d // BLK

    def body(*refs):
        if nblk > 1:
            q_ref, kc_ref, kp_ref, vc_ref, vp_ref, o_ref, lse_ref = refs
            has_prev = pl.program_id(1) != 0
        else:
            q_ref, kc_ref, vc_ref, o_ref, lse_ref = refs
        qi = lax.broadcasted_iota(jnp.int32, (BLK, BLK), 0)
        kj = lax.broadcasted_iota(jnp.int32, (BLK, BLK), 1)
        mask_c = kj <= qi
        if nblk > 1:
            mask_p = jnp.logical_and(kj >= qi, has_prev)
        for h in range(NH):
            sl = slice(h * DH, (h + 1) * DH)
            q = q_ref[:, sl]
            sc = jnp.where(mask_c, _dot_nt(q, kc_ref[:, sl]) * 0.125, NEG_INF)
            m = jnp.max(sc, axis=-1, keepdims=True)
            if nblk > 1:
                sp = jnp.where(mask_p, _dot_nt(q, kp_ref[:, sl]) * 0.125, NEG_INF)
                m = jnp.maximum(m, jnp.max(sp, axis=-1, keepdims=True))
            pc = jnp.exp(sc - m)
            l = jnp.sum(pc, axis=-1, keepdims=True)
            o = _dot(pc.astype(BF16), vc_ref[:, sl])
            if nblk > 1:
                pp = jnp.exp(sp - m)
                l = l + jnp.sum(pp, axis=-1, keepdims=True)
                o = o + _dot(pp.astype(BF16), vp_ref[:, sl])
            o_ref[:, sl] = o / l
            lse_ref[:, sl] = jnp.broadcast_to(m + jnp.log(l), (BLK, DH))

    def cur(sec):
        return pl.BlockSpec((None, BLK, GRP_W), lambda r, n: (r, n, sec))

    def prev(sec):
        return pl.BlockSpec((None, BLK, GRP_W), lambda r, n: (r, jnp.maximum(n - 1, 0), sec))

    out = pl.BlockSpec((None, BLK, GRP_W), lambda r, n: (r, n, 0))
    shp = jax.ShapeDtypeStruct((d, S // d, GRP_W), F32)
    if nblk > 1:
        in_specs, args = [cur(0), cur(1), prev(1), cur(2), prev(2)], (qkv_c,) * 5
    else:
        in_specs, args = [cur(0), cur(1), cur(2)], (qkv_c,) * 3
    return pl.pallas_call(
        body, name=name, grid=(d, nblk), in_specs=in_specs, out_specs=[out, out], out_shape=[shp, shp],
        compiler_params=_cp(("parallel", "parallel")),
    )(*args)


def _attn_combine(os, lses, name):
    tm = ROPE_TM

    def body(o0_ref, o1_ref, o2_ref, l0_ref, l1_ref, l2_ref, y_ref, yt_ref, l_ref, buf_ref):
        def token_order(ref, d, cols, slot):
            if d == 1:
                return ref[0, :, cols]
            for r in range(d):
                buf_ref[slot, pl.ds(r, tm // d, stride=d), :] = ref[r, :, cols]
            return buf_ref[slot]

        for ch in range(GRP_W // 128):
            cols = slice(ch * 128, (ch + 1) * 128)
            o = [token_order(ref, d, cols, k) for k, (ref, d) in enumerate(zip((o0_ref, o1_ref, o2_ref), DILATIONS))]
            ls = [token_order(ref, d, cols, 3 + k)
                  for k, (ref, d) in enumerate(zip((l0_ref, l1_ref, l2_ref), DILATIONS))]
            m = jnp.maximum(jnp.maximum(ls[0], ls[1]), ls[2])
            e = [jnp.exp(l - m) for l in ls]
            den = e[0] + e[1] + e[2]
            y = (e[0] * o[0] + e[1] * o[1] + e[2] * o[2]) / den
            y_ref[:, cols] = y
            yt_ref[cols, :] = y.T.astype(BF16)
            l_ref[:, cols] = m + jnp.log(den)

    blk = pl.BlockSpec((tm, GRP_W), lambda i: (i, 0))
    cls = [pl.BlockSpec((d, tm // d, GRP_W), lambda i: (0, i, 0)) for d in DILATIONS]
    shp = jax.ShapeDtypeStruct((S, GRP_W), F32)
    return pl.pallas_call(
        body, name=name, grid=(S // tm,), in_specs=cls + cls,
        out_specs=[blk, pl.BlockSpec((GRP_W, tm), lambda i: (0, i)), blk],
        out_shape=[shp, jax.ShapeDtypeStruct((GRP_W, S), BF16), shp],
        scratch_shapes=[pltpu.VMEM((6, tm, 128), F32)],
        compiler_params=_cp(("parallel",)),
    )(*os, *lses)


def _attn_bwd(gi, qkv_c, dy_c, y_c, lse_c, name):
    d = DILATIONS[gi]
    nblk = S // d // BLK

    def body(*refs):
        if nblk > 1:
            (q_ref, qn_ref, k_ref, kp_ref, v_ref, vp_ref, dy_ref, dyn_ref, y_ref, yn_ref, l_ref, ln_ref,
             dq_ref, dk_ref, dv_ref) = refs
            n = pl.program_id(1)
            has_prev = n != 0
            has_next = n != nblk - 1
        else:
            q_ref, k_ref, v_ref, dy_ref, y_ref, l_ref, dq_ref, dk_ref, dv_ref = refs
        qi = lax.broadcasted_iota(jnp.int32, (BLK, BLK), 0)
        kj = lax.broadcasted_iota(jnp.int32, (BLK, BLK), 1)
        mask_c = kj <= qi
        if nblk > 1:
            mask_p = jnp.logical_and(kj >= qi, has_prev)
            mask_n = jnp.logical_and(kj >= qi, has_next)
        for h in range(NH):
            sl = slice(h * DH, (h + 1) * DH)
            q, k, v = q_ref[:, sl], k_ref[:, sl], v_ref[:, sl]
            dy_h = dy_ref[:, sl]
            dd = jnp.sum(dy_h * y_ref[:, sl], axis=-1, keepdims=True)
            lcol = l_ref[:, h * DH:h * DH + 1]
            dyb = dy_h.astype(BF16)
            p = jnp.exp(jnp.where(mask_c, _dot_nt(q, k) * 0.125, NEG_INF) - lcol)
            ds = (p * (_dot_nt(dyb, v) - dd)).astype(BF16)
            dq = _dot(ds, k)
            dk = _dot_tn(ds, q)
            dv = _dot_tn(p.astype(BF16), dyb)
            if nblk > 1:
                qn, kpv, vpv = qn_ref[:, sl], kp_ref[:, sl], vp_ref[:, sl]
                dyn = dyn_ref[:, sl]
                ddn = jnp.sum(dyn * yn_ref[:, sl], axis=-1, keepdims=True)
                lncol = ln_ref[:, h * DH:h * DH + 1]
                dynb = dyn.astype(BF16)
                pp = jnp.exp(jnp.where(mask_p, _dot_nt(q, kpv) * 0.125, NEG_INF) - lcol)
                dsp = (pp * (_dot_nt(dyb, vpv) - dd)).astype(BF16)
                dq = dq + _dot(dsp, kpv)
                pn = jnp.exp(jnp.where(mask_n, _dot_nt(qn, k) * 0.125, NEG_INF) - lncol)
                dsn = (pn * (_dot_nt(dynb, v) - ddn)).astype(BF16)
                dk = dk + _dot_tn(dsn, qn)
                dv = dv + _dot_tn(pn.astype(BF16), dynb)
            dq_ref[:, sl] = dq * 0.125
            dk_ref[:, sl] = dk * 0.125
            dv_ref[:, sl] = dv

    def spec(sec, shift):
        def idx(r, n):
            return (r, jnp.clip(n + shift, 0, nblk - 1), sec)
        return pl.BlockSpec((None, BLK, GRP_W), idx)

    if nblk > 1:
        in_specs = [spec(0, 0), spec(0, 1), spec(1, 0), spec(1, -1), spec(2, 0), spec(2, -1),
                    spec(0, 0), spec(0, 1), spec(0, 0), spec(0, 1), spec(0, 0), spec(0, 1)]
        args = (qkv_c,) * 6 + (dy_c, dy_c, y_c, y_c, lse_c, lse_c)
    else:
        in_specs = [spec(0, 0), spec(1, 0), spec(2, 0), spec(0, 0), spec(0, 0), spec(0, 0)]
        args = (qkv_c, qkv_c, qkv_c, dy_c, y_c, lse_c)
    out = spec(0, 0)
    shp = jax.ShapeDtypeStruct((d, S // d, GRP_W), F32)
    return pl.pallas_call(
        body, name=name, grid=(d, nblk), in_specs=in_specs, out_specs=[out, out, out], out_shape=[shp, shp, shp],
        compiler_params=_cp(("parallel", "parallel")),
    )(*args)


_SQRT_HALF = 0.7071067811865476
_INV_SQRT_2PI = 0.3989422804014327


def _gelu(z):
    return 0.5 * z * (1.0 + lax.erf(z * _SQRT_HALF))


def _gelu_grad(z):
    return 0.5 * (1.0 + lax.erf(z * _SQRT_HALF)) + z * (jnp.exp(-0.5 * z * z) * _INV_SQRT_2PI)


def _tril_mask():
    t = lax.broadcasted_iota(jnp.int32, (BLK, BLK), 0)
    s = lax.broadcasted_iota(jnp.int32, (BLK, BLK), 1)
    return s <= t


def _gmlp_fwd(z, ln_g, ln_b, w_s, b_s_t, name):
    def body(z_ref, g_ref, b_ref, ws_ref, bs_ref, y_ref, yt_ref):
        zg = _gelu(z_ref[...])
        u = zg[:, :GW]
        xh, _ = _ln_stats(zg[:, GW:])
        vn = (xh * g_ref[...] + b_ref[...]).astype(BF16)
        tril = _tril_mask()
        for gg in range(8):
            sl = slice(gg * BLK, (gg + 1) * BLK)
            wt = jnp.where(tril, ws_ref[gg], 0.0).astype(BF16)
            mixed = _dot(wt, vn[:, sl]) + bs_ref[:, gg:gg + 1]
            yv = u[:, sl] * mixed
            y_ref[:, sl] = yv.astype(BF16)
            yt_ref[sl, :] = yv.T.astype(BF16)

    vec = pl.BlockSpec((1, GW), lambda n: (0, 0))
    return pl.pallas_call(
        body, name=name, grid=(NBLK,),
        in_specs=[pl.BlockSpec((BLK, 2 * GW), lambda n: (n, 0)), vec, vec,
                  pl.BlockSpec((8, BLK, BLK), lambda n: (0, 0, 0)), pl.BlockSpec((BLK, 8), lambda n: (0, 0))],
        out_specs=[pl.BlockSpec((BLK, GW), lambda n: (n, 0)), pl.BlockSpec((GW, BLK), lambda n: (0, n))],
        out_shape=[jax.ShapeDtypeStruct((S, GW), BF16), jax.ShapeDtypeStruct((GW, S), BF16)],
        compiler_params=_cp(("parallel",)),
    )(z, ln_g, ln_b, w_s, b_s_t)


def _gmlp_bwd(z, dy, ln_g, ln_b, w_s, b_s_t, name):
    def body(z_ref, dy_ref, g_ref, b_ref, ws_ref, bs_ref, dz_ref, dws_ref, dbs_ref, dg_ref, db_ref, dvn_ref):
        n = pl.program_id(0)
        zv = z_ref[...]
        zg = _gelu(zv)
        u = zg[:, :GW]
        xh, rstd = _ln_stats(zg[:, GW:])
        vn = (xh * g_ref[...] + b_ref[...]).astype(BF16)
        tril = _tril_mask()

        @pl.when(n == 0)
        def _():
            dws_ref[...] = jnp.zeros_like(dws_ref)
            dbs_ref[...] = jnp.zeros_like(dbs_ref)
            dg_ref[...] = jnp.zeros_like(dg_ref)
            db_ref[...] = jnp.zeros_like(db_ref)

        for gg in range(8):
            sl = slice(gg * BLK, (gg + 1) * BLK)
            wt = jnp.where(tril, ws_ref[gg], 0.0).astype(BF16)
            dyg = dy_ref[:, sl]
            mixed = _dot(wt, vn[:, sl]) + bs_ref[:, gg:gg + 1]
            dz_ref[:, sl] = (dyg * mixed * _gelu_grad(zv[:, sl])).astype(BF16)
            dmix = dyg * u[:, sl]
            dmb = dmix.astype(BF16)
            dws_ref[gg] += jnp.where(tril, _dot_nt(dmb, vn[:, sl]), 0.0)
            dbs_ref[:, gg:gg + 1] += jnp.sum(dmix, axis=-1, keepdims=True)
            dvn_ref[:, sl] = _dot_tn(wt, dmb)

        dvn = dvn_ref[...]
        dg_ref[...] += jnp.sum(dvn * xh, axis=0, keepdims=True)
        db_ref[...] += jnp.sum(dvn, axis=0, keepdims=True)
        dvg = _ln_dx(dvn * g_ref[...], xh, rstd)
        dz_ref[:, GW:] = (dvg * _gelu_grad(zv[:, GW:])).astype(BF16)

    vec = pl.BlockSpec((1, GW), lambda n: (0, 0))
    ws = pl.BlockSpec((8, BLK, BLK), lambda n: (0, 0, 0))
    bs = pl.BlockSpec((BLK, 8), lambda n: (0, 0))
    return pl.pallas_call(
        body, name=name, grid=(NBLK,),
        in_specs=[pl.BlockSpec((BLK, 2 * GW), lambda n: (n, 0)), pl.BlockSpec((BLK, GW), lambda n: (n, 0)),
                  vec, vec, ws, bs],
        out_specs=[pl.BlockSpec((BLK, 2 * GW), lambda n: (n, 0)), ws, bs, vec, vec],
        out_shape=[jax.ShapeDtypeStruct((S, 2 * GW), BF16), jax.ShapeDtypeStruct((8, BLK, BLK), F32),
                   jax.ShapeDtypeStruct((BLK, 8), F32), jax.ShapeDtypeStruct((1, GW), F32),
                   jax.ShapeDtypeStruct((1, GW), F32)],
        scratch_shapes=[pltpu.VMEM((BLK, GW), F32)],
        compiler_params=_cp(("arbitrary",)),
    )(z, dy, ln_g, ln_b, w_s, b_s_t)


def _merge_fwd(a, b, gl, b_gates, name):
    tm = 256

    def body(a_ref, b_ref, g0_ref, g1_ref, bg_ref, o_ref, ot_ref):
        g0 = jax.nn.sigmoid(g0_ref[...] + bg_ref[:, :D])
        g1 = jax.nn.sigmoid(g1_ref[...] + bg_ref[:, D:])
        mg = g0 * a_ref[...] + g1 * b_ref[...]
        o_ref[...] = mg.astype(BF16)
        ot_ref[...] = mg.T.astype(BF16)

    row = pl.BlockSpec((tm, D), lambda i: (i, 0))
    return pl.pallas_call(
        body, name=name, grid=(S // tm,),
        in_specs=[row, row, row, pl.BlockSpec((tm, D), lambda i: (i, 1)), pl.BlockSpec((1, 2 * D), lambda i: (0, 0))],
        out_specs=[row, pl.BlockSpec((D, tm), lambda i: (0, i))],
        out_shape=[jax.ShapeDtypeStruct((S, D), BF16), jax.ShapeDtypeStruct((D, S), BF16)],
        compiler_params=_cp(("parallel",)),
    )(a, b, gl, gl, b_gates)


def _merge_bwd(dm, a, b, gl, b_gates, name):
    tm = 256

    def body(dm_ref, a_ref, b_ref, g0_ref, g1_ref, bg_ref, da_ref, db_ref, dgl_ref, dbg_ref):
        i = pl.program_id(0)
        dmv = dm_ref[...]
        g0 = jax.nn.sigmoid(g0_ref[...] + bg_ref[:, :D])
        g1 = jax.nn.sigmoid(g1_ref[...] + bg_ref[:, D:])
        da_ref[...] = (dmv * g0).astype(BF16)
        db_ref[...] = (dmv * g1).astype(BF16)
        d0 = dmv * a_ref[...] * g0 * (1.0 - g0)
        d1 = dmv * b_ref[...] * g1 * (1.0 - g1)
        dgl_ref[:, :D] = d0.astype(BF16)
        dgl_ref[:, D:] = d1.astype(BF16)
        s0 = jnp.sum(d0, axis=0, keepdims=True)
        s1 = jnp.sum(d1, axis=0, keepdims=True)

        @pl.when(i == 0)
        def _():
            dbg_ref[:, :D] = s0
            dbg_ref[:, D:] = s1

        @pl.when(i > 0)
        def _():
            dbg_ref[:, :D] += s0
            dbg_ref[:, D:] += s1

    row = pl.BlockSpec((tm, D), lambda i: (i, 0))
    wide = pl.BlockSpec((tm, 2 * D), lambda i: (i, 0))
    bg = pl.BlockSpec((1, 2 * D), lambda i: (0, 0))
    return pl.pallas_call(
        body, name=name, grid=(S // tm,),
        in_specs=[row, row, row, row, pl.BlockSpec((tm, D), lambda i: (i, 1)), bg],
        out_specs=[row, row, wide, bg],
        out_shape=[jax.ShapeDtypeStruct((S, D), BF16), jax.ShapeDtypeStruct((S, D), BF16),
                   jax.ShapeDtypeStruct((S, 2 * D), BF16), jax.ShapeDtypeStruct((1, 2 * D), F32)],
        compiler_params=_cp(("arbitrary",)),
    )(dm, a, b, gl, gl, b_gates)


def _adam_math(w, g, m, v):
    m2 = ADAM_B1 * m + (1.0 - ADAM_B1) * g
    v2 = ADAM_B2 * v + (1.0 - ADAM_B2) * (g * g)
    m_hat = m2 / (1.0 - ADAM_B1 ** ADAM_STEP)
    v_hat = v2 / (1.0 - ADAM_B2 ** ADAM_STEP)
    delta = -ADAM_LR * (m_hat / (jnp.sqrt(v_hat) + ADAM_EPS) + ADAM_WD * w)
    return delta, m2, v2


def _pick_rows(rows, cols, unit=16, budget=MIB):
    best = unit
    for t in range(unit, rows + 1, unit):
        if rows % t == 0 and t * cols * 4 <= budget:
            best = t
    assert rows % best == 0
    return best


def _adamw(w, g, m, v, name):
    r, c = w.shape
    tr = _pick_rows(r, c, unit=8)

    def body(w_ref, g_ref, m_ref, v_ref, go_ref, d_ref, mo_ref, vo_ref):
        gv = g_ref[...]
        delta, m2, v2 = _adam_math(w_ref[...], gv, m_ref[...], v_ref[...])
        go_ref[...] = gv
        d_ref[...] = delta
        mo_ref[...] = m2
        vo_ref[...] = v2

    blk = pl.BlockSpec((tr, c), lambda i: (i, 0))
    shp = jax.ShapeDtypeStruct((r, c), F32)
    return pl.pallas_call(
        body, name=name, grid=(r // tr,), in_specs=[blk] * 4, out_specs=[blk] * 4, out_shape=[shp] * 4,
        compiler_params=_cp(("parallel",)),
    )(w, g, m, v)


def _small_sum_adamw(parts, w, m, v, name):
    tr = 48

    def body(p_ref, w_ref, m_ref, v_ref, g_ref, d_ref, mo_ref, vo_ref):
        gv = p_ref[0]
        for k in range(1, 8):
            gv = gv + p_ref[k]
        delta, m2, v2 = _adam_math(w_ref[...], gv, m_ref[...], v_ref[...])
        g_ref[...] = gv
        d_ref[...] = delta
        mo_ref[...] = m2
        vo_ref[...] = v2

    blk = pl.BlockSpec((tr, D), lambda i: (i, 0))
    shp = jax.ShapeDtypeStruct((SMALL_ROWS, D), F32)
    return pl.pallas_call(
        body, name=name, grid=(SMALL_ROWS // tr,),
        in_specs=[pl.BlockSpec((8, tr, D), lambda i: (0, i, 0)), blk, blk, blk],
        out_specs=[blk] * 4, out_shape=[shp] * 4,
        compiler_params=_cp(("parallel",)),
    )(parts, w, m, v)


ANY = pl.BlockSpec(memory_space=pl.ANY)


def _mesh_pos():
    x, y, c = lax.axis_index("x"), lax.axis_index("y"), lax.axis_index("c")
    chips = [(1 - x, y), (x, 1 - y), (1 - x, 1 - y)]
    return x, y, c, chips


def _place_shard(w, kind, pos, name):
    r, c = w.shape
    tr = _pick_rows(r, c)

    def body(pos_ref, w_ref, o_ref):
        o_ref[...] = w_ref[...].astype(BF16)

    if kind == "stack":
        o_spec = pl.BlockSpec((None, tr, c), lambda i, p: (p[1], i, 0))
        shape = (NSH, r, c)
    else:
        o_spec = pl.BlockSpec((tr, c), lambda i, p: (i, p[1]))
        shape = (r, NSH * c)
    return pl.pallas_call(
        body, name=name,
        grid_spec=pltpu.PrefetchScalarGridSpec(
            num_scalar_prefetch=1, grid=(r // tr,),
            in_specs=[pl.BlockSpec((tr, c), lambda i, p: (i, 0))], out_specs=o_spec),
        out_shape=jax.ShapeDtypeStruct(shape, BF16),
        compiler_params=_cp(("parallel",)),
    )(pos, w)


SEM = pl.BlockSpec(memory_space=pltpu.SEMAPHORE)
SPLIT_COPY = pltpu.CompilerParams(has_side_effects=pltpu.SideEffectType.DATAFLOW_SIDE_EFFECTING)


def _shard_window(ref, kind, j, h, dims):
    r, c = dims
    rows = pl.ds(pl.multiple_of(h * (r // 2), 16), r // 2)
    if kind == "stack":
        return ref.at[j, rows, :]
    return ref.at[rows, pl.ds(pl.multiple_of(j * c, 128), c)]


def _ici_copy(ref, kind, dims, j, c, sems, idx, to):
    win = _shard_window(ref, kind, j, c, dims)
    return pltpu.make_async_remote_copy(src_ref=win, dst_ref=win, send_sem=sems[0].at[idx], recv_sem=sems[1].at[idx],
                                        device_id=to, device_id_type=MESH_T)


def _gather_start(fulls, kinds, dims, after, name):
    n, na = len(fulls), len(after)

    def body(*refs):
        outs = refs[n + na:2 * n + na]
        send_sems, recv_sems, token = refs[2 * n + na:]
        x, y, c, chips = _mesh_pos()
        for a in range(n):
            for k, chip in enumerate(chips):
                _ici_copy(outs[a], kinds[a], dims[a], 2 * x + y, c, (send_sems, recv_sems), 3 * a + k,
                          (chip[0], chip[1], c)).start()
        token[...] = jnp.zeros_like(token)

    res = pl.pallas_call(
        body, name=name, in_specs=[ANY] * (n + na),
        out_specs=[ANY] * n + [SEM, SEM, pl.BlockSpec(memory_space=pltpu.VMEM)],
        out_shape=[jax.ShapeDtypeStruct(f.shape, BF16) for f in fulls]
        + [pltpu.SemaphoreType.DMA((3 * n,)), pltpu.SemaphoreType.DMA((3 * n,)), jax.ShapeDtypeStruct((8, 128), F32)],
        input_output_aliases={i: i for i in range(n)},
        compiler_params=SPLIT_COPY,
    )(*fulls, *after)
    return res[:n], res[n], res[n + 1], res[n + 2]


def _gather_wait(fulls, send_sems, recv_sems, kinds, dims, after, name):
    n, na = len(fulls), len(after)

    def body(*refs):
        ssem, rsem = refs[n], refs[n + 1]
        outs = refs[n + 2 + na:]
        x, y, c, chips = _mesh_pos()
        for a in range(n):
            for k, chip in enumerate(chips):
                to = (chip[0], chip[1], c)
                _ici_copy(outs[a], kinds[a], dims[a], 2 * x + y, c, (ssem, rsem), 3 * a + k, to).wait_send()
                _ici_copy(outs[a], kinds[a], dims[a], 2 * chip[0] + chip[1], c, (ssem, rsem), 3 * a + k, to).wait_recv()

    return pl.pallas_call(
        body, name=name, in_specs=[ANY] * n + [SEM, SEM] + [ANY] * na, out_specs=[ANY] * n,
        out_shape=[jax.ShapeDtypeStruct(f.shape, BF16) for f in fulls],
        input_output_aliases={i: i for i in range(n)},
        compiler_params=SPLIT_COPY,
    )(*fulls, send_sems, recv_sems, *after)


def _gather_forward(fulls, kinds, dims, name):
    n = len(fulls)

    def body(*refs):
        outs = refs[n:2 * n]
        sems = refs[2 * n:]
        x, y, c, chips = _mesh_pos()
        sib = (x, y, 1 - c)
        cps = []
        for a in range(n):
            for k, chip in enumerate(chips):
                cp = _ici_copy(outs[a], kinds[a], dims[a], 2 * chip[0] + chip[1], c, sems, 3 * a + k, sib)
                cp.start()
                cps.append(cp)
        for a in range(n):
            for k, chip in enumerate(chips):
                _ici_copy(outs[a], kinds[a], dims[a], 2 * chip[0] + chip[1], 1 - c, sems, 3 * a + k, sib).wait_recv()
        for cp in cps:
            cp.wait_send()

    return pl.pallas_call(
        body, name=name, in_specs=[ANY] * n, out_specs=[ANY] * n,
        out_shape=[jax.ShapeDtypeStruct(f.shape, BF16) for f in fulls],
        input_output_aliases={i: i for i in range(n)},
        scratch_shapes=[pltpu.SemaphoreType.DMA((3 * n,)), pltpu.SemaphoreType.DMA((3 * n,))],
    )(*fulls)


def _pair_copy(src, land, a, x, y, c, sems):
    return pltpu.make_async_remote_copy(
        src_ref=src.at[1 - c], dst_ref=land, send_sem=sems[0].at[a], recv_sem=sems[1].at[a],
        device_id=(x, y, 1 - c), device_id_type=MESH_T)


def _pair_start(grads, lands, name):
    n = len(grads)

    def body(*refs):
        srcs, dsts = refs[2 * n:3 * n], refs[3 * n:4 * n]
        send_sems, recv_sems, token = refs[4 * n:]
        x, y, c, _ = _mesh_pos()
        for a in range(n):
            _pair_copy(srcs[a], dsts[a], a, x, y, c, (send_sems, recv_sems)).start()
        token[...] = jnp.zeros_like(token)

    res = pl.pallas_call(
        body, name=name, in_specs=[ANY] * (2 * n),
        out_specs=[ANY] * (2 * n) + [SEM, SEM, pl.BlockSpec(memory_space=pltpu.VMEM)],
        out_shape=[jax.ShapeDtypeStruct(g.shape, F32) for g in grads]
        + [jax.ShapeDtypeStruct(l.shape, F32) for l in lands]
        + [pltpu.SemaphoreType.DMA((n,)), pltpu.SemaphoreType.DMA((n,)), jax.ShapeDtypeStruct((8, 128), F32)],
        input_output_aliases={i: i for i in range(2 * n)},
        compiler_params=SPLIT_COPY,
    )(*grads, *lands)
    return res[:n], res[n:2 * n], res[2 * n], res[2 * n + 1], res[2 * n + 2]


def _pair_wait(grads, lands, send_sems, recv_sems, after, name):
    n, na = len(grads), len(after)

    def body(*refs):
        ssem, rsem = refs[2 * n], refs[2 * n + 1]
        outs = refs[2 * n + 2 + na:]
        x, y, c, _ = _mesh_pos()
        for a in range(n):
            cp = _pair_copy(outs[a], outs[n + a], a, x, y, c, (ssem, rsem))
            cp.wait_send()
            cp.wait_recv()

    res = pl.pallas_call(
        body, name=name, in_specs=[ANY] * (2 * n) + [SEM, SEM] + [ANY] * na, out_specs=[ANY] * (2 * n),
        out_shape=[jax.ShapeDtypeStruct(g.shape, F32) for g in grads]
        + [jax.ShapeDtypeStruct(l.shape, F32) for l in lands],
        input_output_aliases={i: i for i in range(2 * n)},
        compiler_params=SPLIT_COPY,
    )(*grads, *lands, send_sems, recv_sems, *after)
    return res[:n], res[n:]


def _pair_sum(g, recv, pos, name):
    _, _, rh, c = g.shape
    tr = _pick_rows(rh, c)

    def body(pos_ref, g_ref, r_ref, o_ref):
        o_ref[...] = (g_ref[...] + r_ref[...]).astype(BF16)

    return pl.pallas_call(
        body, name=name,
        grid_spec=pltpu.PrefetchScalarGridSpec(
            num_scalar_prefetch=1, grid=(NSH, rh // tr),
            in_specs=[pl.BlockSpec((None, None, tr, c), lambda j, r, p: (p[0], j, r, 0)),
                      pl.BlockSpec((None, tr, c), lambda j, r, p: (j, r, 0))],
            out_specs=pl.BlockSpec((None, tr, c), lambda j, r, p: (j, r, 0))),
        out_shape=jax.ShapeDtypeStruct((NSH, rh, c), BF16),
        compiler_params=_cp(("parallel", "parallel")),
    )(pos, g, recv)


def _chip_copy(src, land, a, k, chip, c, sems):
    return pltpu.make_async_remote_copy(
        src_ref=src.at[2 * chip[0] + chip[1]], dst_ref=land.at[k], send_sem=sems[0].at[3 * a + k],
        recv_sem=sems[1].at[3 * a + k], device_id=(chip[0], chip[1], c), device_id_type=MESH_T)


def _chip_start(psums, lands, name):
    n = len(psums)

    def body(*refs):
        srcs, dsts = refs[2 * n:3 * n], refs[3 * n:4 * n]
        send_sems, recv_sems, token = refs[4 * n:]
        x, y, c, chips = _mesh_pos()
        for a in range(n):
            for k, chip in enumerate(chips):
                _chip_copy(srcs[a], dsts[a], a, k, chip, c, (send_sems, recv_sems)).start()
        token[...] = jnp.zeros_like(token)

    res = pl.pallas_call(
        body, name=name, in_specs=[ANY] * (2 * n),
        out_specs=[ANY] * (2 * n) + [SEM, SEM, pl.BlockSpec(memory_space=pltpu.VMEM)],
        out_shape=[jax.ShapeDtypeStruct(p.shape, BF16) for p in psums]
        + [jax.ShapeDtypeStruct(l.shape, BF16) for l in lands]
        + [pltpu.SemaphoreType.DMA((3 * n,)), pltpu.SemaphoreType.DMA((3 * n,)), jax.ShapeDtypeStruct((8, 128), F32)],
        input_output_aliases={i: i for i in range(2 * n)},
        compiler_params=SPLIT_COPY,
    )(*psums, *lands)
    return res[:n], res[n:2 * n], res[2 * n], res[2 * n + 1], res[2 * n + 2]


def _chip_wait(psums, lands, send_sems, recv_sems, after, name):
    n, na = len(psums), len(after)

    def body(*refs):
        ssem, rsem = refs[2 * n], refs[2 * n + 1]
        outs = refs[2 * n + 2 + na:]
        srcs, dsts = outs[:n], outs[n:]
        x, y, c, chips = _mesh_pos()
        for a in range(n):
            for k, chip in enumerate(chips):
                cp = _chip_copy(srcs[a], dsts[a], a, k, chip, c, (ssem, rsem))
                cp.wait_send()
                cp.wait_recv()

    res = pl.pallas_call(
        body, name=name, in_specs=[ANY] * (2 * n) + [SEM, SEM] + [ANY] * na, out_specs=[ANY] * (2 * n),
        out_shape=[jax.ShapeDtypeStruct(p.shape, BF16) for p in psums]
        + [jax.ShapeDtypeStruct(l.shape, BF16) for l in lands],
        input_output_aliases={i: i for i in range(2 * n)},
        compiler_params=SPLIT_COPY,
    )(*psums, *lands, send_sems, recv_sems, *after)
    return res[n:]


def _owner_sum(g, recv_a, recv_b, pos, name):
    _, _, rh, c = g.shape
    tr = _pick_rows(rh, c)

    def body(pos_ref, g_ref, ra_ref, rb_ref, o_ref):
        acc = g_ref[...] + ra_ref[...]
        for k in range(3):
            acc = acc + rb_ref[k].astype(F32)
        o_ref[...] = acc

    return pl.pallas_call(
        body, name=name,
        grid_spec=pltpu.PrefetchScalarGridSpec(
            num_scalar_prefetch=1, grid=(rh // tr,),
            in_specs=[pl.BlockSpec((None, None, tr, c), lambda r, p: (p[0], p[1], r, 0)),
                      pl.BlockSpec((None, tr, c), lambda r, p: (p[1], r, 0)),
                      pl.BlockSpec((3, tr, c), lambda r, p: (0, r, 0))],
            out_specs=pl.BlockSpec((None, tr, c), lambda r, p: (p[0], r, 0))),
        out_shape=jax.ShapeDtypeStruct((2, rh, c), F32),
        compiler_params=_cp(("parallel",)),
    )(pos, g, recv_a, recv_b)


def _sibling_allgather(halves, name):
    n = len(halves)

    def body(*refs):
        outs = refs[n:2 * n]
        send_sems, recv_sems = refs[2 * n:]
        x, y, c, _ = _mesh_pos()
        cps = []
        for a in range(n):
            cp = pltpu.make_async_remote_copy(
                src_ref=outs[a].at[c], dst_ref=outs[a].at[c], send_sem=send_sems.at[a], recv_sem=recv_sems.at[a],
                device_id=(x, y, 1 - c), device_id_type=MESH_T)
            cp.start()
            cps.append(cp)
        for a in range(n):
            cps[a].wait_send()
            pltpu.make_async_remote_copy(
                src_ref=outs[a].at[1 - c], dst_ref=outs[a].at[1 - c], send_sem=send_sems.at[a],
                recv_sem=recv_sems.at[a], device_id=(x, y, 1 - c), device_id_type=MESH_T).wait_recv()

    return pl.pallas_call(
        body, name=name, in_specs=[ANY] * n, out_specs=[ANY] * n,
        out_shape=[jax.ShapeDtypeStruct(h.shape, F32) for h in halves],
        input_output_aliases={i: i for i in range(n)},
        scratch_shapes=[pltpu.SemaphoreType.DMA((n,)), pltpu.SemaphoreType.DMA((n,))],
    )(*halves)


def _small_allgather(part, after):
    m_per = SMALL_ROWS
    na = len(after)

    def body(x_ref, *refs):
        out_ref, send_sems, recv_sems, local_sem = refs[na:]
        x, y, c, chips = _mesh_pos()
        me, sibling = (x, y, c), (x, y, 1 - c)

        def rows(px, py, pc):
            return out_ref.at[pl.ds((4 * px + 2 * py + pc) * m_per, m_per), :]

        def copy(k, block, to, src=None):
            return pltpu.make_async_remote_copy(
                src_ref=rows(*block) if src is None else src, dst_ref=rows(*block),
                send_sem=send_sems.at[k], recv_sem=recv_sems.at[k], device_id=to, device_id_type=MESH_T)

        mine = pltpu.make_async_copy(x_ref, rows(*me), local_sem)
        mine.start()
        first = [copy(0, me, sibling, src=x_ref)]
        first += [copy(1 + j, me, (*chip, c), src=x_ref) for j, chip in enumerate(chips)]
        for cp in first:
            cp.start()
        passed = [copy(4 + j, (*chip, c), sibling) for j, chip in enumerate(chips)]
        for j, chip in enumerate(chips):
            copy(1 + j, (*chip, c), me).wait_recv()
            passed[j].start()
        copy(0, sibling, me).wait_recv()
        for j, chip in enumerate(chips):
            copy(4 + j, (*chip, 1 - c), me).wait_recv()
        for cp in first + passed:
            cp.wait_send()
        mine.wait()

    return pl.pallas_call(
        body, name="small_allgather",
        out_shape=jax.ShapeDtypeStruct((8 * m_per, D), F32),
        in_specs=[pl.BlockSpec(memory_space=pltpu.VMEM)] + [ANY] * na, out_specs=pl.BlockSpec(memory_space=pltpu.VMEM),
        scratch_shapes=[pltpu.SemaphoreType.DMA((7,)), pltpu.SemaphoreType.DMA((7,)), pltpu.SemaphoreType.DMA],
    )(part, *after)


def _pack_small(ln1_g, ln1_b, gln_g, gln_b, ln2_g, ln2_b, ln3_g, ln3_b, b_gates, b_s, w_s):
    rows = [ln1_g, ln1_b, gln_g, gln_b, ln2_g, ln2_b, ln3_g, ln3_b]
    rows = [r.reshape(1, D) for r in rows] + [b_gates.reshape(2, D), b_s.reshape(1, D), jnp.zeros((5, D), F32),
                                             w_s.reshape(128, D)]
    return jnp.concatenate(rows, axis=0)


def _unpack_small(p):
    out = [p[i:i + 1] for i in range(8)]
    return out + [p[8:10].reshape(1, 2 * D), p[10:11].reshape(1, 8, BLK), p[16:144].reshape(1, 8, BLK, BLK)]


GROUPS = (("f1g", "f1u", "f1d"), ("w_in",), ("w_ab", "w_gb", "w_out"), ("f2g", "f2u", "f2d"))


def _local_step(x, pos_f, target, P, weights_of, grads_ready, flush):
    invf = ROPE_THETA ** (-jnp.arange(0, DH, 2, dtype=F32) / DH)
    invf = jnp.tile(invf, 4).reshape(1, 128)
    b_s_t = P["gmlp_b_s"].T

    W = dict(weights_of(0, []))
    h1, h1b, xh1, rstd1, a1, b1, h1t = _ffn_fwd(x, W["f1g"], W["f1u"], W["f1d"], P["ln1_g"], P["ln1_b"], "ffn1_fwd",
                                                emit_t=True)
    W.update(weights_of(1, [h1b]))
    qkv = _matmul(h1b, W["w_in"], "nn", "proj_qkv", n=3 * ATT_W, b_col0=0, tm=1024, tn=ATT_W)
    z = _matmul(h1b, W["w_in"], "nn", "proj_z", n=2 * GW, b_col0=3 * ATT_W, tm=S, tn=512)
    gl = _matmul(h1b, W["w_in"], "nn", "proj_gates", n=2 * D, b_col0=3 * ATT_W + 2 * GW, tm=S, tn=512)
    qkv_c = _rope_fwd(qkv, pos_f, invf, "rope_fwd")
    og = [_attn_fwd(gi, qkv_c[gi], "attn_fwd_g%d" % gi) for gi in range(NG)]
    y_attn, y_attn_t, lse = _attn_combine([o for o, _ in og], [l for _, l in og], "attn_combine")
    y_gmlp, y_gmlp_t = _gmlp_fwd(z, P["gmlp_ln_g"], P["gmlp_ln_b"], P["gmlp_w_s"], b_s_t, "gmlp_fwd")
    W.update(weights_of(2, [y_gmlp]))
    br_a = _matmul(y_attn, W["w_ab"], "nn", "branch_attn", n=D, tm=1024, tn=D)
    br_b = _matmul(y_gmlp, W["w_gb"], "nn", "branch_gmlp", n=D, tm=1024, tn=D)
    merged, merged_t = _merge_fwd(br_a, br_b, gl, P["b_gates"], "merge_fwd")
    mix = _matmul(merged, W["w_out"], "nn", "mix_out", n=D, tm=1024, tn=D)
    h2, h2b, xh2, rstd2 = _resid_ln(h1, mix, P["ln2_g"], P["ln2_b"], "resid_ln2")
    W.update(weights_of(3, [h2b]))
    y, _, xh3, rstd3, a2, b2 = _ffn_fwd(h2, W["f2g"], W["f2u"], W["f2d"], P["ln3_g"], P["ln3_b"], "ffn2_fwd")

    dr3, dg3, db3, loss = _ln_bwd(y, xh3, rstd3, P["ln3_g"], "loss_ln3_bwd", target=target)
    g_f2g, g_f2u, g_f2d, dh2 = _ffn_bwd(dr3, h2b, a2, b2, W["f2g"], W["f2u"], W["f2d"], "ffn2_bwd")
    tok = grads_ready(3, dict(f2g=g_f2g, f2u=g_f2u, f2d=g_f2d))
    dr2, dg2, db2 = _ln_bwd(dh2, xh2, rstd2, P["ln2_g"], "ln2_bwd", after=tok)
    g_wout = _wgrad(merged_t, dr2, 128, D, "dw_out", row_sharded=True)
    dmerged = _matmul(dr2, W["w_out"], "nt", "dmerged", n=D, tm=1024, tn=D)
    dab, dbb, dglb, dbg = _merge_bwd(dmerged, br_a, br_b, gl, P["b_gates"], "merge_bwd")
    tok = flush([dab])
    g_wab = _wgrad(y_attn_t, dab, GRP_W // 2, 256, "dw_attn_branch", row_sharded=False, after=tok)
    g_wgb = _wgrad(y_gmlp_t, dbb, 128, D, "dw_gmlp_branch", row_sharded=True)
    tok = grads_ready(2, dict(w_ab=g_wab, w_gb=g_wgb, w_out=g_wout))
    dy_attn = _matmul(dab, W["w_ab"], "nt", "dy_attn", n=GRP_W, tm=1024, tn=GRP_W, after=tok)
    dy_gmlp = _matmul(dbb, W["w_gb"], "nt", "dy_gmlp", n=GW, tm=1024, tn=GW)
    dzb, dws, dbs_t, dgln_g, dgln_b = _gmlp_bwd(z, dy_gmlp, P["gmlp_ln_g"], P["gmlp_ln_b"], P["gmlp_w_s"], b_s_t,
                                                 "gmlp_bwd")
    cls = _class_order([dy_attn, y_attn, lse], "attn_class_order")
    dqkv_c = []
    for gi in range(NG):
        dy_c, y_c, lse_c = [t[None] if gi == 0 else cls[2 * a + gi - 1] for a, t in enumerate((dy_attn, y_attn, lse))]
        dqkv_c.append(_attn_bwd(gi, qkv_c[gi], dy_c, y_c, lse_c, "attn_bwd_g%d" % gi))
    dqkvb = _rope_bwd(dqkv_c, pos_f, invf, "rope_bwd")
    dproj = jnp.concatenate([dqkvb, dzb, dglb], axis=1)
    tok = flush([dproj])
    g_win = _wgrad(h1t, dproj, D // 2, IN_SH, "dw_in", row_sharded=False, after=tok)
    tok = grads_ready(1, dict(w_in=g_win))
    dh1 = _matmul(dproj, W["w_in"], "nt", "dh1", n=D, tn=D, tk=IN_SH, add=dr2, add_scale=ALPHA, after=tok)
    dr1, dg1, db1 = _ln_bwd(dh1, xh1, rstd1, P["ln1_g"], "ln1_bwd")
    tok = flush([dr1])
    g_f1g, g_f1u, g_f1d, dx = _ffn_bwd(dr1, x.astype(BF16), a1, b1, W["f1g"], W["f1u"], W["f1d"], "ffn1_bwd",
                                       after=tok)
    grads_ready(0, dict(f1g=g_f1g, f1u=g_f1u, f1d=g_f1d))
    flush([dx])

    small = _pack_small(dg1, db1, dgln_g, dgln_b, dg2, db2, dg3, db3, dbg, dbs_t.T, dws)
    return loss, dx, small


BIG = ("f1g", "f1u", "f1d", "w_in", "w_ab", "w_gb", "w_out", "f2g", "f2u", "f2d")
TRANSPOSED = ("f1g", "f1u", "f2g", "f2u")
KIND = dict(f1g="stack", f1u="stack", f1d="stack", w_in="col", w_ab="col", w_gb="stack", w_out="stack",
            f2g="stack", f2u="stack", f2d="stack")


def kernel(x, positions, ffn1_w_gate, ffn1_w_up, ffn1_w_down, ln1_g, ln1_b, w_in, b_gates, gmlp_ln_g, gmlp_ln_b, gmlp_w_s, gmlp_b_s, w_attn_branch, w_gmlp_branch, w_out, ln2_g, ln2_b, ffn2_w_gate, ffn2_w_up, ffn2_w_down, ln3_g, ln3_b, loss_target, m_ffn1_w_gate, m_ffn1_w_up, m_ffn1_w_down, m_ln1_g, m_ln1_b, m_w_in, m_b_gates, m_gmlp_ln_g, m_gmlp_ln_b, m_gmlp_w_s, m_gmlp_b_s, m_w_attn_branch, m_w_gmlp_branch, m_w_out, m_ln2_g, m_ln2_b, m_ffn2_w_gate, m_ffn2_w_up, m_ffn2_w_down, m_ln3_g, m_ln3_b, v_ffn1_w_gate, v_ffn1_w_up, v_ffn1_w_down, v_ln1_g, v_ln1_b, v_w_in, v_b_gates, v_gmlp_ln_g, v_gmlp_ln_b, v_gmlp_w_s, v_gmlp_b_s, v_w_attn_branch, v_w_gmlp_branch, v_w_out, v_ln2_g, v_ln2_b, v_ffn2_w_gate, v_ffn2_w_up, v_ffn2_w_down, v_ln3_g, v_ln3_b):
    cx, cy, cc = lax.axis_index("x"), lax.axis_index("y"), lax.axis_index("c")
    pos = jnp.stack([cc, 2 * cx + cy]).astype(jnp.int32)

    w_sh = dict(f1g=ffn1_w_gate, f1u=ffn1_w_up, f1d=ffn1_w_down, w_in=w_in, w_ab=w_attn_branch,
                w_gb=w_gmlp_branch, w_out=w_out, f2g=ffn2_w_gate, f2u=ffn2_w_up, f2d=ffn2_w_down)
    m_sh = dict(f1g=m_ffn1_w_gate, f1u=m_ffn1_w_up, f1d=m_ffn1_w_down, w_in=m_w_in, w_ab=m_w_attn_branch,
                w_gb=m_w_gmlp_branch, w_out=m_w_out, f2g=m_ffn2_w_gate, f2u=m_ffn2_w_up, f2d=m_ffn2_w_down)
    v_sh = dict(f1g=v_ffn1_w_gate, f1u=v_ffn1_w_up, f1d=v_ffn1_w_down, w_in=v_w_in, w_ab=v_w_attn_branch,
                w_gb=v_w_gmlp_branch, w_out=v_w_out, f2g=v_ffn2_w_gate, f2u=v_ffn2_w_up, f2d=v_ffn2_w_down)
    w_sh = {k: (v[0].T if k in TRANSPOSED else v[0]) for k, v in w_sh.items()}
    m_sh = {k: (v[0].T if k in TRANSPOSED else v[0]) for k, v in m_sh.items()}
    v_sh = {k: (v[0].T if k in TRANSPOSED else v[0]) for k, v in v_sh.items()}

    started, tokens = [], []
    for gi, names in enumerate(GROUPS):
        placed = [_place_shard(w_sh[k], KIND[k], pos, "place_" + k) for k in names]
        fulls, ssem, rsem, token = _gather_start(placed, [KIND[k] for k in names], [w_sh[k].shape for k in names],
                                                 tokens[-1:], "gather_start_g%d" % gi)
        started.append((fulls, ssem, rsem))
        tokens.append(token)

    def weights_of(gi, after):
        names = GROUPS[gi]
        kinds, dims = [KIND[k] for k in names], [w_sh[k].shape for k in names]
        fulls, ssem, rsem = started[gi]
        fulls = _gather_wait(fulls, ssem, rsem, kinds, dims, list(after) + (tokens if gi == 0 else []),
                             "gather_wait_g%d" % gi)
        fulls = _gather_forward(fulls, kinds, dims, "gather_forward_g%d" % gi)
        return {k: (f.reshape(D, D) if k in ("w_gb", "w_out") else f) for k, f in zip(names, fulls)}

    pending, inflight = [], {}

    def grads_ready(gi, gd):
        grads = [gd[k] for k in GROUPS[gi]]
        lands = [lax.empty(g.shape[1:], F32) for g in grads]
        grads, lands, ssem, rsem, token = _pair_start(grads, lands, "rs_pair_start_g%d" % gi)
        pending.append((gi, grads, lands, ssem, rsem))
        return [token]

    def flush(after):
        gi, grads, lands, ssem, rsem = pending.pop()
        names = GROUPS[gi]
        grads, recv_a = _pair_wait(grads, lands, ssem, rsem, after, "rs_pair_wait_g%d" % gi)
        psums = [_pair_sum(g, r, pos, "rs_pair_sum_" + k) for g, r, k in zip(grads, recv_a, names)]
        lands = [lax.empty((3,) + p.shape[1:], BF16) for p in psums]
        psums, lands, ssem, rsem, token = _chip_start(psums, lands, "rs_chip_start_g%d" % gi)
        inflight[gi] = (grads, recv_a, psums, lands, ssem, rsem, token)
        return [token]

    P = dict(ln1_g=ln1_g, ln1_b=ln1_b, ln2_g=ln2_g, ln2_b=ln2_b, ln3_g=ln3_g, ln3_b=ln3_b, b_gates=b_gates,
             gmlp_ln_g=gmlp_ln_g, gmlp_ln_b=gmlp_ln_b, gmlp_w_s=gmlp_w_s[0], gmlp_b_s=gmlp_b_s[0])
    pos_f = positions.reshape(S, 1).astype(F32)
    loss_part, dx, small = _local_step(x[0], pos_f, loss_target[0], P, weights_of, grads_ready, flush)
    loss = lax.psum(loss_part[0, 0], ("x", "y", "c"))

    g_out, d_out, m_out, v_out = {}, {}, {}, {}

    def finish(gi, after):
        grads, recv_a, psums, lands, ssem, rsem, token = inflight[gi]
        recv_b = _chip_wait(psums, lands, ssem, rsem, after + [inflight[0][6]], "rs_chip_wait_g%d" % gi)
        halves = [_owner_sum(g, ra, rb, pos, "rs_owner_sum_" + k)
                  for g, ra, rb, k in zip(grads, recv_a, recv_b, GROUPS[gi])]
        reduced = _sibling_allgather(halves, "rs_sibling_allgather_g%d" % gi)
        for k, gfull in zip(GROUPS[gi], reduced):
            res = _adamw(w_sh[k], gfull.reshape(w_sh[k].shape), m_sh[k], v_sh[k], "adamw_" + k)
            after = [res[1]]
            if k in TRANSPOSED:
                res = [r.T for r in res]
            g_out[k], d_out[k], m_out[k], v_out[k] = [r[None] for r in res]
        return after

    after = []
    for gi in (3, 2, 1):
        after = finish(gi, after)

    parts = _small_allgather(small, after).reshape(8, SMALL_ROWS, D)
    sp = (ln1_g, ln1_b, gmlp_ln_g, gmlp_ln_b, ln2_g, ln2_b, ln3_g, ln3_b, b_gates, gmlp_b_s, gmlp_w_s)
    sm = (m_ln1_g, m_ln1_b, m_gmlp_ln_g, m_gmlp_ln_b, m_ln2_g, m_ln2_b, m_ln3_g, m_ln3_b, m_b_gates, m_gmlp_b_s,
          m_gmlp_w_s)
    sv = (v_ln1_g, v_ln1_b, v_gmlp_ln_g, v_gmlp_ln_b, v_ln2_g, v_ln2_b, v_ln3_g, v_ln3_b, v_b_gates, v_gmlp_b_s,
          v_gmlp_w_s)
    sg, sd, smn, svn = _small_sum_adamw(parts, _pack_small(*sp), _pack_small(*sm), _pack_small(*sv), "small_adamw")
    names = ("ln1_g", "ln1_b", "gmlp_ln_g", "gmlp_ln_b", "ln2_g", "ln2_b", "ln3_g", "ln3_b", "b_gates", "gmlp_b_s",
             "gmlp_w_s")
    for dst, packed in ((g_out, sg), (d_out, sd), (m_out, smn), (v_out, svn)):
        for nm, val in zip(names, _unpack_small(packed)):
            dst[nm] = val
    finish(0, [sg])

    order = ("f1g", "f1u", "f1d", "ln1_g", "ln1_b", "w_in", "b_gates", "gmlp_ln_g", "gmlp_ln_b", "gmlp_w_s", "gmlp_b_s",
             "w_ab", "w_gb", "w_out", "ln2_g", "ln2_b", "f2g", "f2u", "f2d", "ln3_g", "ln3_b")
    outs = [loss, dx[None]]
    for dst in (g_out, d_out, m_out, v_out):
        outs += [dst[k] for k in order]
    return tuple(outs)
```

```python
import functools
import math

import jax
import jax.numpy as jnp
from jax import lax
from jax.experimental import pallas as pl
from jax.experimental.pallas import tpu as pltpu

F32 = jnp.float32
BF16 = jnp.bfloat16

S = 2048
D = 1024
NSH = 4
FSH = 704
ATT_W = 1536
GRP_W = 512
NG = 3
NH = 8
DH = 64
BLK = 128
NBLK = S // BLK
GW = 1024
IN_W = 8704
IN_SH = IN_W // NSH
ALPHA = 2.0 ** 0.25
LN_EPS = 1e-5
ROPE_THETA = 10000.0
DILATIONS = (1, 4, 16)
ADAM_LR, ADAM_B1, ADAM_B2, ADAM_EPS, ADAM_WD, ADAM_STEP = 0.001, 0.9, 0.999, 1e-08, 0.01, 10
SMALL_ROWS = 144
MESH_T = pl.DeviceIdType.MESH
MIB = 1024 * 1024
NEG_INF = float("-inf")


def _cp(sem, vmem_mib=48):
    return pltpu.CompilerParams(dimension_semantics=sem, vmem_limit_bytes=vmem_mib * MIB)


def _ln_stats(r):
    mu = jnp.mean(r, axis=-1, keepdims=True)
    xc = r - mu
    var = jnp.mean(xc * xc, axis=-1, keepdims=True)
    rstd = lax.rsqrt(var + LN_EPS)
    return xc * rstd, rstd


def _ln_dx(dxh, xh, rstd):
    m1 = jnp.mean(dxh, axis=-1, keepdims=True)
    m2 = jnp.mean(dxh * xh, axis=-1, keepdims=True)
    return rstd * (dxh - m1 - xh * m2)


def _dot_nt(a, b):
    return lax.dot_general(a, b, (((1,), (1,)), ((), ())), preferred_element_type=F32)


def _dot_tn(a, b):
    return lax.dot_general(a, b, (((0,), (0,)), ((), ())), preferred_element_type=F32)


def _dot(a, b):
    return jnp.dot(a, b, preferred_element_type=F32)


def _ffn_fwd(xin, wgt, wut, wd, ln_g, ln_b, name, emit_t=False):
    tm = 512

    def body(x_ref, wg_ref, wu_ref, wd_ref, g_ref, b_ref, *rest):
        if emit_t:
            h_ref, hb_ref, xh_ref, rstd_ref, a_ref, bb_ref, ht_ref, acc_ref = rest
        else:
            h_ref, hb_ref, xh_ref, rstd_ref, a_ref, bb_ref, acc_ref = rest
        j = pl.program_id(1)
        xb = x_ref[...].astype(BF16)
        a = _dot_nt(xb, wg_ref[...])
        b = _dot_nt(xb, wu_ref[...])
        a_ref[...] = a.astype(BF16)
        bb_ref[...] = b.astype(BF16)
        s = (a * jax.nn.sigmoid(a)) * b
        f = _dot(s.astype(BF16), wd_ref[...])

        @pl.when(j == 0)
        def _():
            acc_ref[...] = f

        @pl.when(j > 0)
        def _():
            acc_ref[...] += f

        @pl.when(j == NSH - 1)
        def _():
            r = ALPHA * x_ref[...] + 0.5 * acc_ref[...]
            xh, rstd = _ln_stats(r)
            h = xh * g_ref[...] + b_ref[...]
            h_ref[...] = h
            hb_ref[...] = h.astype(BF16)
            xh_ref[...] = xh
            rstd_ref[...] = rstd
            if emit_t:
                ht_ref[...] = h.T.astype(BF16)

    row = pl.BlockSpec((tm, D), lambda i, j: (i, 0))
    vec = pl.BlockSpec((1, D), lambda i, j: (0, 0))
    wsp = pl.BlockSpec((None, FSH, D), lambda i, j: (j, 0, 0))
    ab = pl.BlockSpec((None, tm, FSH), lambda i, j: (j, i, 0))
    out_specs = [row, row, row, pl.BlockSpec((tm, 1), lambda i, j: (i, 0)), ab, ab]
    out_shape = [jax.ShapeDtypeStruct((S, D), F32), jax.ShapeDtypeStruct((S, D), BF16),
                 jax.ShapeDtypeStruct((S, D), F32), jax.ShapeDtypeStruct((S, 1), F32),
                 jax.ShapeDtypeStruct((NSH, S, FSH), BF16), jax.ShapeDtypeStruct((NSH, S, FSH), BF16)]
    if emit_t:
        out_specs.append(pl.BlockSpec((D, tm), lambda i, j: (0, i)))
        out_shape.append(jax.ShapeDtypeStruct((D, S), BF16))
    return pl.pallas_call(
        body, name=name, grid=(S // tm, NSH),
        in_specs=[row, wsp, wsp, wsp, vec, vec], out_specs=out_specs, out_shape=out_shape,
        scratch_shapes=[pltpu.VMEM((tm, D), F32)],
        compiler_params=_cp(("parallel", "arbitrary")),
    )(xin, wgt, wut, wd, ln_g, ln_b)


def _ffn_bwd(dr, xin_b, a, b, wgt, wut, wd, name, after=()):
    tm = 512
    ni = S // tm
    hr = FSH // 2

    def body(dr_ref, a_ref, b_ref, wg_ref, wu_ref, wd_ref, x_hbm, *rest):
        dwg_hbm, dwu_hbm, dwd_hbm, dx_hbm, dx_acc, da_all, db_all, s_all, df_all, x_all, res_buf, sems = rest[len(after):]
        j = pl.program_id(0)
        i = pl.program_id(1)
        rows = pl.ds(pl.multiple_of(i * tm, tm), tm)

        @pl.when(jnp.logical_and(j == 0, i == 0))
        def _():
            cp = pltpu.make_async_copy(x_hbm, x_all, sems.at[0])
            cp.start()
            cp.wait()

        drv = dr_ref[...]
        df = (0.5 * drv).astype(BF16)

        @pl.when(j == 0)
        def _():
            df_all[rows, :] = df

        ds = _dot_nt(df, wd_ref[...])
        av = a_ref[...].astype(F32)
        bv = b_ref[...].astype(F32)
        sig = jax.nn.sigmoid(av)
        sl = av * sig
        da = (ds * bv * (sig * (1.0 + av * (1.0 - sig)))).astype(BF16)
        db = (ds * sl).astype(BF16)
        da_all[rows, :] = da
        db_all[rows, :] = db
        s_all[rows, :] = (sl * bv).astype(BF16)
        dx = _dot(da, wg_ref[...]) + _dot(db, wu_ref[...])

        @pl.when(j == 0)
        def _():
            dx_acc[rows, :] = ALPHA * drv + dx

        @pl.when(j > 0)
        def _():
            dx_acc[rows, :] += dx

        @pl.when(i == ni - 1)
        def _():
            copies = []
            for n, (lhs, rhs, out) in enumerate(((da_all, x_all, dwg_hbm), (db_all, x_all, dwu_hbm),
                                                 (s_all, df_all, dwd_hbm))):
                slot = n % 2
                if n >= 2:
                    for cp in copies[2 * (n - 2): 2 * (n - 2) + 2]:
                        cp.wait()
                res_buf[slot] = _dot_tn(lhs[...], rhs[...])
                for h in range(2):
                    cp = pltpu.make_async_copy(res_buf.at[slot, pl.ds(h * hr, hr), :], out.at[h, j],
                                               sems.at[1 + 2 * slot + h])
                    cp.start()
                    copies.append(cp)
            for cp in copies[2:]:
                cp.wait()

        @pl.when(jnp.logical_and(j == NSH - 1, i == ni - 1))
        def _():
            cp = pltpu.make_async_copy(dx_acc, dx_hbm, sems.at[0])
            cp.start()
            cp.wait()

    row = pl.BlockSpec((tm, D), lambda j, i: (i, 0))
    wsp = pl.BlockSpec((None, FSH, D), lambda j, i: (j, 0, 0))
    ab = pl.BlockSpec((None, tm, FSH), lambda j, i: (j, i, 0))
    dwshape = jax.ShapeDtypeStruct((2, NSH, hr, D), F32)
    return pl.pallas_call(
        body, name=name, grid=(NSH, ni),
        in_specs=[row, ab, ab, wsp, wsp, wsp, ANY] + [ANY] * len(after),
        out_specs=[ANY, ANY, ANY, ANY],
        out_shape=[dwshape, dwshape, dwshape, jax.ShapeDtypeStruct((S, D), F32)],
        scratch_shapes=[pltpu.VMEM((S, D), F32), pltpu.VMEM((S, FSH), BF16), pltpu.VMEM((S, FSH), BF16),
                        pltpu.VMEM((S, FSH), BF16), pltpu.VMEM((S, D), BF16), pltpu.VMEM((S, D), BF16),
                        pltpu.VMEM((2, FSH, D), F32), pltpu.SemaphoreType.DMA((5,))],
        compiler_params=_cp(("arbitrary", "arbitrary"), vmem_mib=58),
    )(dr, a, b, wgt, wut, wd, xin_b, *after)


def _matmul(a, b, mode, name, *, n, tm=512, tn=512, tk=None, b_col0=0, add=None, add_scale=1.0, out_dtype=F32,
            after=()):
    m, ka = a.shape
    tk = ka if tk is None else tk
    nk = ka // tk
    assert m % tm == 0 and n % tn == 0 and ka % tk == 0 and b_col0 % tn == 0
    off = b_col0 // tn
    na = len(after)

    def body(*refs):
        refs = refs[na:]
        if add is None:
            a_ref, b_ref, o_ref = refs[:3]
            add_ref = None
            rest = refs[3:]
        else:
            a_ref, b_ref, add_ref, o_ref = refs[:4]
            rest = refs[4:]
        k = pl.program_id(2)
        av = a_ref[...].astype(BF16)
        bv = b_ref[...].astype(BF16)
        p = _dot(av, bv) if mode == "nn" else _dot_nt(av, bv)

        def finish(acc):
            if add_ref is not None:
                acc = acc + add_scale * add_ref[...]
            o_ref[...] = acc.astype(out_dtype)

        if nk == 1:
            finish(p)
        else:
            acc_ref = rest[0]

            @pl.when(k == 0)
            def _():
                acc_ref[...] = p

            @pl.when(k > 0)
            def _():
                acc_ref[...] += p

            @pl.when(k == nk - 1)
            def _():
                finish(acc_ref[...])

    a_spec = pl.BlockSpec((tm, tk), lambda i, j, k: (i, k))
    if mode == "nn":
        b_spec = pl.BlockSpec((tk, tn), lambda i, j, k: (k, j + off))
    else:
        b_spec = pl.BlockSpec((tn, tk), lambda i, j, k: (j, k))
    o_spec = pl.BlockSpec((tm, tn), lambda i, j, k: (i, j))
    in_specs = [pl.BlockSpec(memory_space=pl.ANY)] * na + [a_spec, b_spec] + ([o_spec] if add is not None else [])
    args = tuple(after) + (a, b) + ((add,) if add is not None else ())
    return pl.pallas_call(
        body, name=name, grid=(m // tm, n // tn, nk),
        in_specs=in_specs, out_specs=o_spec,
        out_shape=jax.ShapeDtypeStruct((m, n), out_dtype),
        scratch_shapes=[pltpu.VMEM((tm, tn), F32)] if nk > 1 else [],
        compiler_params=_cp(("parallel", "parallel", "arbitrary")),
    )(*args)


def _wgrad(xt, y, rh, c, name, row_sharded, after=()):
    na = len(after)
    if row_sharded:
        def body(x_ref, y_ref, *rest):
            o_ref = rest[na]
            res = _dot(x_ref[...], y_ref[...].astype(BF16))
            for j in range(NSH):
                for h in range(2):
                    o_ref[h, j] = res[(2 * j + h) * rh:(2 * j + h + 1) * rh, :]

        grid = (1,)
        in_specs = [pl.BlockSpec((2 * NSH * rh, S), lambda g: (0, 0)), pl.BlockSpec((S, c), lambda g: (0, 0))]
        out_specs = pl.BlockSpec((2, NSH, rh, c), lambda g: (0, 0, 0, 0))
        sem = ("arbitrary",)
    else:
        def body(x_ref, y_ref, *rest):
            rest[na][...] = _dot(x_ref[...], y_ref[...].astype(BF16))

        grid = (2, NSH)
        in_specs = [pl.BlockSpec((rh, S), lambda h, j: (h, 0)), pl.BlockSpec((S, c), lambda h, j: (0, j))]
        out_specs = pl.BlockSpec((None, None, rh, c), lambda h, j: (h, j, 0, 0))
        sem = ("parallel", "parallel")
    return pl.pallas_call(
        body, name=name, grid=grid, in_specs=in_specs + [pl.BlockSpec(memory_space=pl.ANY)] * na, out_specs=out_specs,
        out_shape=jax.ShapeDtypeStruct((2, NSH, rh, c), F32),
        compiler_params=_cp(sem, vmem_mib=56),
    )(xt, y, *after)


def _resid_ln(res, f, ln_g, ln_b, name):
    tm = 256

    def body(res_ref, f_ref, g_ref, b_ref, h_ref, hb_ref, xh_ref, rstd_ref):
        r = ALPHA * res_ref[...] + f_ref[...]
        xh, rstd = _ln_stats(r)
        h = xh * g_ref[...] + b_ref[...]
        h_ref[...] = h
        hb_ref[...] = h.astype(BF16)
        xh_ref[...] = xh
        rstd_ref[...] = rstd

    row = pl.BlockSpec((tm, D), lambda i: (i, 0))
    vec = pl.BlockSpec((1, D), lambda i: (0, 0))
    return pl.pallas_call(
        body, name=name, grid=(S // tm,),
        in_specs=[row, row, vec, vec],
        out_specs=[row, row, row, pl.BlockSpec((tm, 1), lambda i: (i, 0))],
        out_shape=[jax.ShapeDtypeStruct((S, D), F32), jax.ShapeDtypeStruct((S, D), BF16),
                   jax.ShapeDtypeStruct((S, D), F32), jax.ShapeDtypeStruct((S, 1), F32)],
        compiler_params=_cp(("parallel",)),
    )(res, f, ln_g, ln_b)


def _ln_bwd(dout, xh, rstd, ln_g, name, target=None, after=()):
    tm = 256
    with_loss = target is not None
    na = len(after)

    def body(*refs):
        refs = refs[na:]
        if with_loss:
            y_ref, t_ref, xh_ref, rstd_ref, g_ref, dr_ref, dg_ref, db_ref, loss_ref = refs
            err = y_ref[...] - t_ref[...]
            dy = err * (1.0 / D)
        else:
            y_ref, xh_ref, rstd_ref, g_ref, dr_ref, dg_ref, db_ref = refs
            dy = y_ref[...]
        i = pl.program_id(0)
        xh = xh_ref[...]
        dr_ref[...] = _ln_dx(dy * g_ref[...], xh, rstd_ref[...])
        dg = jnp.sum(dy * xh, axis=0, keepdims=True)
        db = jnp.sum(dy, axis=0, keepdims=True)

        @pl.when(i == 0)
        def _():
            dg_ref[...] = dg
            db_ref[...] = db

        @pl.when(i > 0)
        def _():
            dg_ref[...] += dg
            db_ref[...] += db

        if with_loss:
            part = 0.5 * jnp.sum(jnp.mean(err * err, axis=-1, keepdims=True), axis=0, keepdims=True)
            part = jnp.broadcast_to(part, (8, 128))

            @pl.when(i == 0)
            def _():
                loss_ref[...] = part

            @pl.when(i > 0)
            def _():
                loss_ref[...] += part

    row = pl.BlockSpec((tm, D), lambda i: (i, 0))
    vec = pl.BlockSpec((1, D), lambda i: (0, 0))
    col = pl.BlockSpec((tm, 1), lambda i: (i, 0))
    in_specs = [pl.BlockSpec(memory_space=pl.ANY)] * na + [row] + ([row] if with_loss else []) + [row, col, vec]
    out_specs = [row, vec, vec] + ([pl.BlockSpec((8, 128), lambda i: (0, 0))] if with_loss else [])
    out_shape = [jax.ShapeDtypeStruct((S, D), F32), jax.ShapeDtypeStruct((1, D), F32),
                 jax.ShapeDtypeStruct((1, D), F32)] + ([jax.ShapeDtypeStruct((8, 128), F32)] if with_loss else [])
    args = tuple(after) + (dout,) + ((target,) if with_loss else ()) + (xh, rstd, ln_g)
    return pl.pallas_call(
        body, name=name, grid=(S // tm,), in_specs=in_specs, out_specs=out_specs, out_shape=out_shape,
        compiler_params=_cp(("arbitrary",)),
    )(*args)


ROPE_TM = 256


def _rope_tables(pos_ref, invf_ref, sign):
    ang = pos_ref[...] * invf_ref[...]
    lane = lax.broadcasted_iota(jnp.int32, ang.shape, 1)
    first = (lane % DH) < (DH // 2)
    sinv = jnp.sin(ang) * sign
    return first, jnp.cos(ang), jnp.where(first, -sinv, sinv)


def _rotate(x, first, cosf, sinf):
    return x * cosf + jnp.where(first, pltpu.roll(x, 96, 1), pltpu.roll(x, 32, 1)) * sinf


def _rope_fwd(qkv, pos_f, invf, name):
    tm = ROPE_TM

    def body(t_ref, pos_ref, invf_ref, o0_ref, o1_ref, o2_ref, buf_ref):
        first, cosf, sinf = _rope_tables(pos_ref, invf_ref, 1.0)
        o_refs = (o0_ref, o1_ref, o2_ref)
        for sec in range(3):
            for gi, d in enumerate(DILATIONS):
                for ch in range(GRP_W // 128):
                    src = sec * ATT_W + gi * GRP_W + ch * 128
                    dst = slice(sec * GRP_W + ch * 128, sec * GRP_W + (ch + 1) * 128)
                    x = t_ref[:, src:src + 128]
                    if sec < 2:
                        x = _rotate(x, first, cosf, sinf)
                    if d == 1:
                        o_refs[gi][0, :, dst] = x.astype(BF16)
                    else:
                        buf_ref[...] = x
                        for r in range(d):
                            o_refs[gi][r, :, dst] = buf_ref[pl.ds(r, tm // d, stride=d), :].astype(BF16)

    return pl.pallas_call(
        body, name=name, grid=(S // tm,),
        in_specs=[pl.BlockSpec((tm, 3 * ATT_W), lambda i: (i, 0)), pl.BlockSpec((tm, 1), lambda i: (i, 0)),
                  pl.BlockSpec((1, 128), lambda i: (0, 0))],
        out_specs=[pl.BlockSpec((d, tm // d, 3 * GRP_W), lambda i: (0, i, 0)) for d in DILATIONS],
        out_shape=[jax.ShapeDtypeStruct((d, S // d, 3 * GRP_W), BF16) for d in DILATIONS],
        scratch_shapes=[pltpu.VMEM((tm, 128), F32)],
        compiler_params=_cp(("parallel",)),
    )(qkv, pos_f, invf)


def _rope_bwd(dqkv_c, pos_f, invf, name):
    tm = ROPE_TM

    def body(*refs):
        g_refs, (pos_ref, invf_ref, o_ref, buf_ref) = refs[:9], refs[9:]
        first, cosf, sinf = _rope_tables(pos_ref, invf_ref, -1.0)
        for sec in range(3):
            for gi, d in enumerate(DILATIONS):
                g_ref = g_refs[3 * gi + sec]
                for ch in range(GRP_W // 128):
                    cols = slice(ch * 128, (ch + 1) * 128)
                    if d == 1:
                        x = g_ref[0, :, cols]
                    else:
                        for r in range(d):
                            buf_ref[pl.ds(r, tm // d, stride=d), :] = g_ref[r, :, cols]
                        x = buf_ref[...]
                    if sec < 2:
                        x = _rotate(x, first, cosf, sinf)
                    dst = sec * ATT_W + gi * GRP_W + ch * 128
                    o_ref[:, dst:dst + 128] = x.astype(BF16)

    g_specs = [pl.BlockSpec((d, tm // d, GRP_W), lambda i: (0, i, 0)) for d in DILATIONS for _ in range(3)]
    return pl.pallas_call(
        body, name=name, grid=(S // tm,),
        in_specs=g_specs + [pl.BlockSpec((tm, 1), lambda i: (i, 0)), pl.BlockSpec((1, 128), lambda i: (0, 0))],
        out_specs=pl.BlockSpec((tm, 3 * ATT_W), lambda i: (i, 0)),
        out_shape=jax.ShapeDtypeStruct((S, 3 * ATT_W), BF16),
        scratch_shapes=[pltpu.VMEM((tm, 128), F32)],
        compiler_params=_cp(("parallel",)),
    )(*[g for grp in dqkv_c for g in grp], pos_f, invf)


def _class_order(ts, name):
    tm = ROPE_TM
    n = len(ts)

    def body(*refs):
        buf_ref = refs[3 * n]
        for a in range(n):
            for ch in range(GRP_W // 128):
                cols = slice(ch * 128, (ch + 1) * 128)
                buf_ref[...] = refs[a][:, cols]
                for b, d in enumerate(DILATIONS[1:]):
                    for r in range(d):
                        refs[n + 2 * a + b][r, :, cols] = buf_ref[pl.ds(r, tm // d, stride=d), :]

    return pl.pallas_call(
        body, name=name, grid=(S // tm,),
        in_specs=[pl.BlockSpec((tm, GRP_W), lambda i: (i, 0))] * n,
        out_specs=[pl.BlockSpec((d, tm // d, GRP_W), lambda i: (0, i, 0)) for _ in range(n) for d in DILATIONS[1:]],
        out_shape=[jax.ShapeDtypeStruct((d, S // d, GRP_W), F32) for _ in range(n) for d in DILATIONS[1:]],
        scratch_shapes=[pltpu.VMEM((tm, 128), F32)],
        compiler_params=_cp(("parallel",)),
    )(*ts)


def _heads(ref):
    return jnp.stack([ref[:, h * DH:(h + 1) * DH] for h in range(NH)])


def _bdot_nt(a, b):
    return lax.dot_general(a, b, (((2,), (2,)), ((0,), (0,))), preferred_element_type=F32)


def _bdot(a, b):
    return lax.dot_general(a, b, (((2,), (1,)), ((0,), (0,))), preferred_element_type=F32)


def _bdot_tn(a, b):
    return lax.dot_general(a, b, (((1,), (1,)), ((0,), (0,))), preferred_element_type=F32)


def _attn_fwd(gi, qkv_c, name):
    d = DILATIONS[gi]
    nblk = S // d // BLK

    def body(*refs):
        if nblk > 1:
            q_ref, kc_ref, kp_ref, vc_ref, vp_ref, o_ref, lse_ref = refs
            has_prev = pl.program_id(1) != 0
        else:
            q_ref, kc_ref, vc_ref, o_ref, lse_ref = refs
        qi = lax.broadcasted_iota(jnp.int32, (NH, BLK, BLK), 1)
        kj = lax.broadcasted_iota(jnp.int32, (NH, BLK, BLK), 2)
        q = _heads(q_ref)
        sc = jnp.where(kj <= qi, _bdot_nt(q, _heads(kc_ref)) * 0.125, NEG_INF)
        m = jnp.max(sc, axis=-1, keepdims=True)
        if nblk > 1:
            mask_p = jnp.logical_and(kj >= qi, has_prev)
            sp = jnp.where(mask_p, _bdot_nt(q, _heads(kp_ref)) * 0.125, NEG_INF)
            m = jnp.maximum(m, jnp.max(sp, axis=-1, keepdims=True))
        pc = jnp.exp(sc - m)
        l = jnp.sum(pc, axis=-1, keepdims=True)
        o = _bdot(pc.astype(BF16), _heads(vc_ref))
        if nblk > 1:
            pp = jnp.exp(sp - m)
            l = l + jnp.sum(pp, axis=-1, keepdims=True)
            o = o + _bdot(pp.astype(BF16), _heads(vp_ref))
        o = o / l
        lse = m + jnp.log(l)
        for h in range(NH):
            sl = slice(h * DH, (h + 1) * DH)
            o_ref[:, sl] = o[h]
            lse_ref[:, sl] = jnp.broadcast_to(lse[h], (BLK, DH))

    def cur(sec):
        return pl.BlockSpec((None, BLK, GRP_W), lambda r, n: (r, n, sec))

    def prev(sec):
        return pl.BlockSpec((None, BLK, GRP_W), lambda r, n: (r, jnp.maximum(n - 1, 0), sec))

    out = pl.BlockSpec((None, BLK, GRP_W), lambda r, n: (r, n, 0))
    shp = jax.ShapeDtypeStruct((d, S // d, GRP_W), F32)
    if nblk > 1:
        in_specs, args = [cur(0), cur(1), prev(1), cur(2), prev(2)], (qkv_c,) * 5
    else:
        in_specs, args = [cur(0), cur(1), cur(2)], (qkv_c,) * 3
    return pl.pallas_call(
        body, name=name, grid=(d, nblk), in_specs=in_specs, out_specs=[out, out], out_shape=[shp, shp],
        compiler_params=_cp(("parallel", "parallel")),
    )(*args)


def _attn_combine(os, lses, name):
    tm = ROPE_TM

    def body(o0_ref, o1_ref, o2_ref, l0_ref, l1_ref, l2_ref, y_ref, yt_ref, l_ref, buf_ref):
        def token_order(ref, d, cols, slot):
            if d == 1:
                return ref[0, :, cols]
            for r in range(d):
                buf_ref[slot, pl.ds(r, tm // d, stride=d), :] = ref[r, :, cols]
            return buf_ref[slot]

        for ch in range(GRP_W // 128):
            cols = slice(ch * 128, (ch + 1) * 128)
            o = [token_order(ref, d, cols, k) for k, (ref, d) in enumerate(zip((o0_ref, o1_ref, o2_ref), DILATIONS))]
            ls = [token_order(ref, d, cols, 3 + k)
                  for k, (ref, d) in enumerate(zip((l0_ref, l1_ref, l2_ref), DILATIONS))]
            m = jnp.maximum(jnp.maximum(ls[0], ls[1]), ls[2])
            e = [jnp.exp(l - m) for l in ls]
            den = e[0] + e[1] + e[2]
            y = (e[0] * o[0] + e[1] * o[1] + e[2] * o[2]) / den
            y_ref[:, cols] = y
            yt_ref[cols, :] = y.T.astype(BF16)
            l_ref[:, cols] = m + jnp.log(den)

    blk = pl.BlockSpec((tm, GRP_W), lambda i: (i, 0))
    cls = [pl.BlockSpec((d, tm // d, GRP_W), lambda i: (0, i, 0)) for d in DILATIONS]
    shp = jax.ShapeDtypeStruct((S, GRP_W), F32)
    return pl.pallas_call(
        body, name=name, grid=(S // tm,), in_specs=cls + cls,
        out_specs=[blk, pl.BlockSpec((GRP_W, tm), lambda i: (0, i)), blk],
        out_shape=[shp, jax.ShapeDtypeStruct((GRP_W, S), BF16), shp],
        scratch_shapes=[pltpu.VMEM((6, tm, 128), F32)],
        compiler_params=_cp(("parallel",)),
    )(*os, *lses)


def _attn_bwd(gi, qkv_c, dy_c, y_c, lse_c, name):
    d = DILATIONS[gi]
    nblk = S // d // BLK

    def body(*refs):
        if nblk > 1:
            (q_ref, qn_ref, k_ref, kp_ref, v_ref, vp_ref, dy_ref, dyn_ref, y_ref, yn_ref, l_ref, ln_ref,
             dq_ref, dk_ref, dv_ref) = refs
            n = pl.program_id(1)
            has_prev = n != 0
            has_next = n != nblk - 1
        else:
            q_ref, k_ref, v_ref, dy_ref, y_ref, l_ref, dq_ref, dk_ref, dv_ref = refs
        qi = lax.broadcasted_iota(jnp.int32, (NH, BLK, BLK), 1)
        kj = lax.broadcasted_iota(jnp.int32, (NH, BLK, BLK), 2)

        def lse_col(ref):
            return jnp.stack([ref[:, h * DH:h * DH + 1] for h in range(NH)])

        q, k, v = _heads(q_ref), _heads(k_ref), _heads(v_ref)
        dy = _heads(dy_ref)
        dd = jnp.sum(dy * _heads(y_ref), axis=-1, keepdims=True)
        lcol = lse_col(l_ref)
        dyb = dy.astype(BF16)
        p = jnp.exp(jnp.where(kj <= qi, _bdot_nt(q, k) * 0.125, NEG_INF) - lcol)
        ds = (p * (_bdot_nt(dyb, v) - dd)).astype(BF16)
        dq = _bdot(ds, k)
        dk = _bdot_tn(ds, q)
        dv = _bdot_tn(p.astype(BF16), dyb)
        if nblk > 1:
            qn, kpv, vpv = _heads(qn_ref), _heads(kp_ref), _heads(vp_ref)
            dyn = _heads(dyn_ref)
            ddn = jnp.sum(dyn * _heads(yn_ref), axis=-1, keepdims=True)
            lncol = lse_col(ln_ref)
            dynb = dyn.astype(BF16)
            mask_p = jnp.logical_and(kj >= qi, has_prev)
            pp = jnp.exp(jnp.where(mask_p, _bdot_nt(q, kpv) * 0.125, NEG_INF) - lcol)
            dsp = (pp * (_bdot_nt(dyb, vpv) - dd)).astype(BF16)
            dq = dq + _bdot(dsp, kpv)
            mask_n = jnp.logical_and(kj >= qi, has_next)
            pn = jnp.exp(jnp.where(mask_n, _bdot_nt(qn, k) * 0.125, NEG_INF) - lncol)
            dsn = (pn * (_bdot_nt(dynb, v) - ddn)).astype(BF16)
            dk = dk + _bdot_tn(dsn, qn)
            dv = dv + _bdot_tn(pn.astype(BF16), dynb)
        dq = dq * 0.125
        dk = dk * 0.125
        for h in range(NH):
            sl = slice(h * DH, (h + 1) * DH)
            dq_ref[:, sl] = dq[h]
            dk_ref[:, sl] = dk[h]
            dv_ref[:, sl] = dv[h]

    def spec(sec, shift):
        def idx(r, n):
            return (r, jnp.clip(n + shift, 0, nblk - 1), sec)
        return pl.BlockSpec((None, BLK, GRP_W), idx)

    if nblk > 1:
        in_specs = [spec(0, 0), spec(0, 1), spec(1, 0), spec(1, -1), spec(2, 0), spec(2, -1),
                    spec(0, 0), spec(0, 1), spec(0, 0), spec(0, 1), spec(0, 0), spec(0, 1)]
        args = (qkv_c,) * 6 + (dy_c, dy_c, y_c, y_c, lse_c, lse_c)
    else:
        in_specs = [spec(0, 0), spec(1, 0), spec(2, 0), spec(0, 0), spec(0, 0), spec(0, 0)]
        args = (qkv_c, qkv_c, qkv_c, dy_c, y_c, lse_c)
    out = spec(0, 0)
    shp = jax.ShapeDtypeStruct((d, S // d, GRP_W), F32)
    return pl.pallas_call(
        body, name=name, grid=(d, nblk), in_specs=in_specs, out_specs=[out, out, out], out_shape=[shp, shp, shp],
        compiler_params=_cp(("parallel", "parallel")),
    )(*args)


_SQRT_HALF = 0.7071067811865476
_INV_SQRT_2PI = 0.3989422804014327


def _gelu(z):
    return 0.5 * z * (1.0 + lax.erf(z * _SQRT_HALF))


def _gelu_grad(z):
    return 0.5 * (1.0 + lax.erf(z * _SQRT_HALF)) + z * (jnp.exp(-0.5 * z * z) * _INV_SQRT_2PI)


def _tril_mask():
    t = lax.broadcasted_iota(jnp.int32, (BLK, BLK), 0)
    s = lax.broadcasted_iota(jnp.int32, (BLK, BLK), 1)
    return s <= t


def _gmlp_fwd(z, ln_g, ln_b, w_s, b_s_t, name):
    def body(z_ref, g_ref, b_ref, ws_ref, bs_ref, y_ref, yt_ref):
        zg = _gelu(z_ref[...])
        u = zg[:, :GW]
        xh, _ = _ln_stats(zg[:, GW:])
        vn = (xh * g_ref[...] + b_ref[...]).astype(BF16)
        tril = _tril_mask()
        for gg in range(8):
            sl = slice(gg * BLK, (gg + 1) * BLK)
            wt = jnp.where(tril, ws_ref[gg], 0.0).astype(BF16)
            mixed = _dot(wt, vn[:, sl]) + bs_ref[:, gg:gg + 1]
            yv = u[:, sl] * mixed
            y_ref[:, sl] = yv.astype(BF16)
            yt_ref[sl, :] = yv.T.astype(BF16)

    vec = pl.BlockSpec((1, GW), lambda n: (0, 0))
    return pl.pallas_call(
        body, name=name, grid=(NBLK,),
        in_specs=[pl.BlockSpec((BLK, 2 * GW), lambda n: (n, 0)), vec, vec,
                  pl.BlockSpec((8, BLK, BLK), lambda n: (0, 0, 0)), pl.BlockSpec((BLK, 8), lambda n: (0, 0))],
        out_specs=[pl.BlockSpec((BLK, GW), lambda n: (n, 0)), pl.BlockSpec((GW, BLK), lambda n: (0, n))],
        out_shape=[jax.ShapeDtypeStruct((S, GW), BF16), jax.ShapeDtypeStruct((GW, S), BF16)],
        compiler_params=_cp(("parallel",)),
    )(z, ln_g, ln_b, w_s, b_s_t)


def _gmlp_bwd(z, dy, ln_g, ln_b, w_s, b_s_t, name):
    def body(z_ref, dy_ref, g_ref, b_ref, ws_ref, bs_ref, dz_ref, dws_ref, dbs_ref, dg_ref, db_ref, dvn_ref):
        n = pl.program_id(0)
        zv = z_ref[...]
        zg = _gelu(zv)
        u = zg[:, :GW]
        xh, rstd = _ln_stats(zg[:, GW:])
        vn = (xh * g_ref[...] + b_ref[...]).astype(BF16)
        tril = _tril_mask()

        @pl.when(n == 0)
        def _():
            dws_ref[...] = jnp.zeros_like(dws_ref)
            dbs_ref[...] = jnp.zeros_like(dbs_ref)
            dg_ref[...] = jnp.zeros_like(dg_ref)
            db_ref[...] = jnp.zeros_like(db_ref)

        for gg in range(8):
            sl = slice(gg * BLK, (gg + 1) * BLK)
            wt = jnp.where(tril, ws_ref[gg], 0.0).astype(BF16)
            dyg = dy_ref[:, sl]
            mixed = _dot(wt, vn[:, sl]) + bs_ref[:, gg:gg + 1]
            dz_ref[:, sl] = (dyg * mixed * _gelu_grad(zv[:, sl])).astype(BF16)
            dmix = dyg * u[:, sl]
            dmb = dmix.astype(BF16)
            dws_ref[gg] += jnp.where(tril, _dot_nt(dmb, vn[:, sl]), 0.0)
            dbs_ref[:, gg:gg + 1] += jnp.sum(dmix, axis=-1, keepdims=True)
            dvn_ref[:, sl] = _dot_tn(wt, dmb)

        dvn = dvn_ref[...]
        dg_ref[...] += jnp.sum(dvn * xh, axis=0, keepdims=True)
        db_ref[...] += jnp.sum(dvn, axis=0, keepdims=True)
        dvg = _ln_dx(dvn * g_ref[...], xh, rstd)
        dz_ref[:, GW:] = (dvg * _gelu_grad(zv[:, GW:])).astype(BF16)

    vec = pl.BlockSpec((1, GW), lambda n: (0, 0))
    ws = pl.BlockSpec((8, BLK, BLK), lambda n: (0, 0, 0))
    bs = pl.BlockSpec((BLK, 8), lambda n: (0, 0))
    return pl.pallas_call(
        body, name=name, grid=(NBLK,),
        in_specs=[pl.BlockSpec((BLK, 2 * GW), lambda n: (n, 0)), pl.BlockSpec((BLK, GW), lambda n: (n, 0)),
                  vec, vec, ws, bs],
        out_specs=[pl.BlockSpec((BLK, 2 * GW), lambda n: (n, 0)), ws, bs, vec, vec],
        out_shape=[jax.ShapeDtypeStruct((S, 2 * GW), BF16), jax.ShapeDtypeStruct((8, BLK, BLK), F32),
                   jax.ShapeDtypeStruct((BLK, 8), F32), jax.ShapeDtypeStruct((1, GW), F32),
                   jax.ShapeDtypeStruct((1, GW), F32)],
        scratch_shapes=[pltpu.VMEM((BLK, GW), F32)],
        compiler_params=_cp(("arbitrary",)),
    )(z, dy, ln_g, ln_b, w_s, b_s_t)


def _merge_fwd(a, b, gl, b_gates, name):
    tm = 256

    def body(a_ref, b_ref, g0_ref, g1_ref, bg_ref, o_ref, ot_ref):
        g0 = jax.nn.sigmoid(g0_ref[...] + bg_ref[:, :D])
        g1 = jax.nn.sigmoid(g1_ref[...] + bg_ref[:, D:])
        mg = g0 * a_ref[...] + g1 * b_ref[...]
        o_ref[...] = mg.astype(BF16)
        ot_ref[...] = mg.T.astype(BF16)

    row = pl.BlockSpec((tm, D), lambda i: (i, 0))
    return pl.pallas_call(
        body, name=name, grid=(S // tm,),
        in_specs=[row, row, row, pl.BlockSpec((tm, D), lambda i: (i, 1)), pl.BlockSpec((1, 2 * D), lambda i: (0, 0))],
        out_specs=[row, pl.BlockSpec((D, tm), lambda i: (0, i))],
        out_shape=[jax.ShapeDtypeStruct((S, D), BF16), jax.ShapeDtypeStruct((D, S), BF16)],
        compiler_params=_cp(("parallel",)),
    )(a, b, gl, gl, b_gates)


def _merge_bwd(dm, a, b, gl, b_gates, name):
    tm = 256

    def body(dm_ref, a_ref, b_ref, g0_ref, g1_ref, bg_ref, da_ref, db_ref, dgl_ref, dbg_ref):
        i = pl.program_id(0)
        dmv = dm_ref[...]
        g0 = jax.nn.sigmoid(g0_ref[...] + bg_ref[:, :D])
        g1 = jax.nn.sigmoid(g1_ref[...] + bg_ref[:, D:])
        da_ref[...] = (dmv * g0).astype(BF16)
        db_ref[...] = (dmv * g1).astype(BF16)
        d0 = dmv * a_ref[...] * g0 * (1.0 - g0)
        d1 = dmv * b_ref[...] * g1 * (1.0 - g1)
        dgl_ref[:, :D] = d0.astype(BF16)
        dgl_ref[:, D:] = d1.astype(BF16)
        s0 = jnp.sum(d0, axis=0, keepdims=True)
        s1 = jnp.sum(d1, axis=0, keepdims=True)

        @pl.when(i == 0)
        def _():
            dbg_ref[:, :D] = s0
            dbg_ref[:, D:] = s1

        @pl.when(i > 0)
        def _():
            dbg_ref[:, :D] += s0
            dbg_ref[:, D:] += s1

    row = pl.BlockSpec((tm, D), lambda i: (i, 0))
    wide = pl.BlockSpec((tm, 2 * D), lambda i: (i, 0))
    bg = pl.BlockSpec((1, 2 * D), lambda i: (0, 0))
    return pl.pallas_call(
        body, name=name, grid=(S // tm,),
        in_specs=[row, row, row, row, pl.BlockSpec((tm, D), lambda i: (i, 1)), bg],
        out_specs=[row, row, wide, bg],
        out_shape=[jax.ShapeDtypeStruct((S, D), BF16), jax.ShapeDtypeStruct((S, D), BF16),
                   jax.ShapeDtypeStruct((S, 2 * D), BF16), jax.ShapeDtypeStruct((1, 2 * D), F32)],
        compiler_params=_cp(("arbitrary",)),
    )(dm, a, b, gl, gl, b_gates)


def _adam_math(w, g, m, v):
    m2 = ADAM_B1 * m + (1.0 - ADAM_B1) * g
    v2 = ADAM_B2 * v + (1.0 - ADAM_B2) * (g * g)
    m_hat = m2 / (1.0 - ADAM_B1 ** ADAM_STEP)
    v_hat = v2 / (1.0 - ADAM_B2 ** ADAM_STEP)
    delta = -ADAM_LR * (m_hat / (jnp.sqrt(v_hat) + ADAM_EPS) + ADAM_WD * w)
    return delta, m2, v2


def _pick_rows(rows, cols, unit=16, budget=MIB):
    best = unit
    for t in range(unit, rows + 1, unit):
        if rows % t == 0 and t * cols * 4 <= budget:
            best = t
    assert rows % best == 0
    return best


def _adamw(w, g, m, v, name):
    r, c = w.shape
    tr = _pick_rows(r, c, unit=8)

    def body(w_ref, g_ref, m_ref, v_ref, go_ref, d_ref, mo_ref, vo_ref):
        gv = g_ref[...]
        delta, m2, v2 = _adam_math(w_ref[...], gv, m_ref[...], v_ref[...])
        go_ref[...] = gv
        d_ref[...] = delta
        mo_ref[...] = m2
        vo_ref[...] = v2

    blk = pl.BlockSpec((tr, c), lambda i: (i, 0))
    shp = jax.ShapeDtypeStruct((r, c), F32)
    return pl.pallas_call(
        body, name=name, grid=(r // tr,), in_specs=[blk] * 4, out_specs=[blk] * 4, out_shape=[shp] * 4,
        compiler_params=_cp(("parallel",)),
    )(w, g, m, v)


def _small_sum_adamw(parts, w, m, v, name):
    tr = 48

    def body(p_ref, w_ref, m_ref, v_ref, g_ref, d_ref, mo_ref, vo_ref):
        gv = p_ref[0]
        for k in range(1, 8):
            gv = gv + p_ref[k]
        delta, m2, v2 = _adam_math(w_ref[...], gv, m_ref[...], v_ref[...])
        g_ref[...] = gv
        d_ref[...] = delta
        mo_ref[...] = m2
        vo_ref[...] = v2

    blk = pl.BlockSpec((tr, D), lambda i: (i, 0))
    shp = jax.ShapeDtypeStruct((SMALL_ROWS, D), F32)
    return pl.pallas_call(
        body, name=name, grid=(SMALL_ROWS // tr,),
        in_specs=[pl.BlockSpec((8, tr, D), lambda i: (0, i, 0)), blk, blk, blk],
        out_specs=[blk] * 4, out_shape=[shp] * 4,
        compiler_params=_cp(("parallel",)),
    )(parts, w, m, v)


ANY = pl.BlockSpec(memory_space=pl.ANY)


def _mesh_pos():
    x, y, c = lax.axis_index("x"), lax.axis_index("y"), lax.axis_index("c")
    chips = [(1 - x, y), (x, 1 - y), (1 - x, 1 - y)]
    return x, y, c, chips


def _place_shard(w, kind, pos, name):
    r, c = w.shape
    tr = _pick_rows(r, c)

    def body(pos_ref, w_ref, o_ref):
        o_ref[...] = w_ref[...].astype(BF16)

    if kind == "stack":
        o_spec = pl.BlockSpec((None, tr, c), lambda i, p: (p[1], i, 0))
        shape = (NSH, r, c)
    else:
        o_spec = pl.BlockSpec((tr, c), lambda i, p: (i, p[1]))
        shape = (r, NSH * c)
    return pl.pallas_call(
        body, name=name,
        grid_spec=pltpu.PrefetchScalarGridSpec(
            num_scalar_prefetch=1, grid=(r // tr,),
            in_specs=[pl.BlockSpec((tr, c), lambda i, p: (i, 0))], out_specs=o_spec),
        out_shape=jax.ShapeDtypeStruct(shape, BF16),
        compiler_params=_cp(("parallel",)),
    )(pos, w)


SEM = pl.BlockSpec(memory_space=pltpu.SEMAPHORE)
SPLIT_COPY = pltpu.CompilerParams(has_side_effects=pltpu.SideEffectType.DATAFLOW_SIDE_EFFECTING)


def _shard_window(ref, kind, j, h, dims):
    r, c = dims
    rows = pl.ds(pl.multiple_of(h * (r // 2), 16), r // 2)
    if kind == "stack":
        return ref.at[j, rows, :]
    return ref.at[rows, pl.ds(pl.multiple_of(j * c, 128), c)]


def _ici_copy(ref, kind, dims, j, c, sems, idx, to):
    win = _shard_window(ref, kind, j, c, dims)
    return pltpu.make_async_remote_copy(src_ref=win, dst_ref=win, send_sem=sems[0].at[idx], recv_sem=sems[1].at[idx],
                                        device_id=to, device_id_type=MESH_T)


def _gather_start(fulls, kinds, dims, after, name):
    n, na = len(fulls), len(after)

    def body(*refs):
        outs = refs[n + na:2 * n + na]
        send_sems, recv_sems, token = refs[2 * n + na:]
        x, y, c, chips = _mesh_pos()
        for a in range(n):
            for k, chip in enumerate(chips):
                _ici_copy(outs[a], kinds[a], dims[a], 2 * x + y, c, (send_sems, recv_sems), 3 * a + k,
                          (chip[0], chip[1], c)).start()
        token[...] = jnp.zeros_like(token)

    res = pl.pallas_call(
        body, name=name, in_specs=[ANY] * (n + na),
        out_specs=[ANY] * n + [SEM, SEM, pl.BlockSpec(memory_space=pltpu.VMEM)],
        out_shape=[pltpu.HBM(f.shape, BF16) for f in fulls]
        + [pltpu.SemaphoreType.DMA((3 * n,)), pltpu.SemaphoreType.DMA((3 * n,)), jax.ShapeDtypeStruct((8, 128), F32)],
        input_output_aliases={i: i for i in range(n)},
        compiler_params=SPLIT_COPY,
    )(*fulls, *after)
    return res[:n], res[n], res[n + 1], res[n + 2]


def _gather_wait(fulls, send_sems, recv_sems, kinds, dims, after, name):
    n, na = len(fulls), len(after)

    def body(*refs):
        ssem, rsem = refs[n], refs[n + 1]
        outs = refs[n + 2 + na:]
        x, y, c, chips = _mesh_pos()
        for a in range(n):
            for k, chip in enumerate(chips):
                to = (chip[0], chip[1], c)
                _ici_copy(outs[a], kinds[a], dims[a], 2 * x + y, c, (ssem, rsem), 3 * a + k, to).wait_send()
                _ici_copy(outs[a], kinds[a], dims[a], 2 * chip[0] + chip[1], c, (ssem, rsem), 3 * a + k, to).wait_recv()

    return pl.pallas_call(
        body, name=name, in_specs=[ANY] * n + [SEM, SEM] + [ANY] * na, out_specs=[ANY] * n,
        out_shape=[pltpu.HBM(f.shape, BF16) for f in fulls],
        input_output_aliases={i: i for i in range(n)},
        compiler_params=SPLIT_COPY,
    )(*fulls, send_sems, recv_sems, *after)


def _gather_forward(fulls, kinds, dims, name):
    n = len(fulls)

    def body(*refs):
        outs = refs[n:2 * n]
        sems = refs[2 * n:]
        x, y, c, chips = _mesh_pos()
        sib = (x, y, 1 - c)
        cps = []
        for a in range(n):
            for k, chip in enumerate(chips):
                cp = _ici_copy(outs[a], kinds[a], dims[a], 2 * chip[0] + chip[1], c, sems, 3 * a + k, sib)
                cp.start()
                cps.append(cp)
        for a in range(n):
            for k, chip in enumerate(chips):
                _ici_copy(outs[a], kinds[a], dims[a], 2 * chip[0] + chip[1], 1 - c, sems, 3 * a + k, sib).wait_recv()
        for cp in cps:
            cp.wait_send()

    return pl.pallas_call(
        body, name=name, in_specs=[ANY] * n, out_specs=[ANY] * n,
        out_shape=[pltpu.HBM(f.shape, BF16) for f in fulls],
        input_output_aliases={i: i for i in range(n)},
        scratch_shapes=[pltpu.SemaphoreType.DMA((3 * n,)), pltpu.SemaphoreType.DMA((3 * n,))],
    )(*fulls)


def _pair_copy(src, land, a, x, y, c, sems):
    return pltpu.make_async_remote_copy(
        src_ref=src.at[1 - c], dst_ref=land, send_sem=sems[0].at[a], recv_sem=sems[1].at[a],
        device_id=(x, y, 1 - c), device_id_type=MESH_T)


def _pair_start(grads, lands, name):
    n = len(grads)

    def body(*refs):
        srcs, dsts = refs[2 * n:3 * n], refs[3 * n:4 * n]
        send_sems, recv_sems, token = refs[4 * n:]
        x, y, c, _ = _mesh_pos()
        for a in range(n):
            _pair_copy(srcs[a], dsts[a], a, x, y, c, (send_sems, recv_sems)).start()
        token[...] = jnp.zeros_like(token)

    res = pl.pallas_call(
        body, name=name, in_specs=[ANY] * (2 * n),
        out_specs=[ANY] * (2 * n) + [SEM, SEM, pl.BlockSpec(memory_space=pltpu.VMEM)],
        out_shape=[pltpu.HBM(g.shape, F32) for g in grads]
        + [pltpu.HBM(l.shape, F32) for l in lands]
        + [pltpu.SemaphoreType.DMA((n,)), pltpu.SemaphoreType.DMA((n,)), jax.ShapeDtypeStruct((8, 128), F32)],
        input_output_aliases={i: i for i in range(2 * n)},
        compiler_params=SPLIT_COPY,
    )(*grads, *lands)
    return res[:n], res[n:2 * n], res[2 * n], res[2 * n + 1], res[2 * n + 2]


def _pair_wait(grads, lands, send_sems, recv_sems, after, name):
    n, na = len(grads), len(after)

    def body(*refs):
        ssem, rsem = refs[2 * n], refs[2 * n + 1]
        outs = refs[2 * n + 2 + na:]
        x, y, c, _ = _mesh_pos()
        for a in range(n):
            cp = _pair_copy(outs[a], outs[n + a], a, x, y, c, (ssem, rsem))
            cp.wait_send()
            cp.wait_recv()

    res = pl.pallas_call(
        body, name=name, in_specs=[ANY] * (2 * n) + [SEM, SEM] + [ANY] * na, out_specs=[ANY] * (2 * n),
        out_shape=[pltpu.HBM(g.shape, F32) for g in grads]
        + [pltpu.HBM(l.shape, F32) for l in lands],
        input_output_aliases={i: i for i in range(2 * n)},
        compiler_params=SPLIT_COPY,
    )(*grads, *lands, send_sems, recv_sems, *after)
    return res[:n], res[n:]


def _pair_sum(g, recv, pos, name):
    _, _, rh, c = g.shape
    tr = _pick_rows(rh, c)

    def body(pos_ref, g_ref, r_ref, o_ref):
        o_ref[...] = (g_ref[...] + r_ref[...]).astype(BF16)

    return pl.pallas_call(
        body, name=name,
        grid_spec=pltpu.PrefetchScalarGridSpec(
            num_scalar_prefetch=1, grid=(NSH, rh // tr),
            in_specs=[pl.BlockSpec((None, None, tr, c), lambda j, r, p: (p[0], j, r, 0)),
                      pl.BlockSpec((None, tr, c), lambda j, r, p: (j, r, 0))],
            out_specs=pl.BlockSpec((None, tr, c), lambda j, r, p: (j, r, 0))),
        out_shape=jax.ShapeDtypeStruct((NSH, rh, c), BF16),
        compiler_params=_cp(("parallel", "parallel")),
    )(pos, g, recv)


def _chip_copy(src, land, a, k, chip, c, sems):
    return pltpu.make_async_remote_copy(
        src_ref=src.at[2 * chip[0] + chip[1]], dst_ref=land.at[k], send_sem=sems[0].at[3 * a + k],
        recv_sem=sems[1].at[3 * a + k], device_id=(chip[0], chip[1], c), device_id_type=MESH_T)


def _chip_start(psums, lands, name):
    n = len(psums)

    def body(*refs):
        srcs, dsts = refs[2 * n:3 * n], refs[3 * n:4 * n]
        send_sems, recv_sems, token = refs[4 * n:]
        x, y, c, chips = _mesh_pos()
        for a in range(n):
            for k, chip in enumerate(chips):
                _chip_copy(srcs[a], dsts[a], a, k, chip, c, (send_sems, recv_sems)).start()
        token[...] = jnp.zeros_like(token)

    res = pl.pallas_call(
        body, name=name, in_specs=[ANY] * (2 * n),
        out_specs=[ANY] * (2 * n) + [SEM, SEM, pl.BlockSpec(memory_space=pltpu.VMEM)],
        out_shape=[pltpu.HBM(p.shape, BF16) for p in psums]
        + [pltpu.HBM(l.shape, BF16) for l in lands]
        + [pltpu.SemaphoreType.DMA((3 * n,)), pltpu.SemaphoreType.DMA((3 * n,)), jax.ShapeDtypeStruct((8, 128), F32)],
        input_output_aliases={i: i for i in range(2 * n)},
        compiler_params=SPLIT_COPY,
    )(*psums, *lands)
    return res[:n], res[n:2 * n], res[2 * n], res[2 * n + 1], res[2 * n + 2]


def _chip_wait(psums, lands, send_sems, recv_sems, after, name):
    n, na = len(psums), len(after)

    def body(*refs):
        ssem, rsem = refs[2 * n], refs[2 * n + 1]
        outs = refs[2 * n + 2 + na:]
        srcs, dsts = outs[:n], outs[n:]
        x, y, c, chips = _mesh_pos()
        for a in range(n):
            for k, chip in enumerate(chips):
                cp = _chip_copy(srcs[a], dsts[a], a, k, chip, c, (ssem, rsem))
                cp.wait_send()
                cp.wait_recv()

    res = pl.pallas_call(
        body, name=name, in_specs=[ANY] * (2 * n) + [SEM, SEM] + [ANY] * na, out_specs=[ANY] * (2 * n),
        out_shape=[pltpu.HBM(p.shape, BF16) for p in psums]
        + [pltpu.HBM(l.shape, BF16) for l in lands],
        input_output_aliases={i: i for i in range(2 * n)},
        compiler_params=SPLIT_COPY,
    )(*psums, *lands, send_sems, recv_sems, *after)
    return res[n:]


def _owner_sum(g, recv_a, recv_b, pos, name):
    _, _, rh, c = g.shape
    tr = _pick_rows(rh, c)

    def body(pos_ref, g_ref, ra_ref, rb_ref, o_ref):
        acc = g_ref[...] + ra_ref[...]
        for k in range(3):
            acc = acc + rb_ref[k].astype(F32)
        o_ref[...] = acc

    return pl.pallas_call(
        body, name=name,
        grid_spec=pltpu.PrefetchScalarGridSpec(
            num_scalar_prefetch=1, grid=(rh // tr,),
            in_specs=[pl.BlockSpec((None, None, tr, c), lambda r, p: (p[0], p[1], r, 0)),
                      pl.BlockSpec((None, tr, c), lambda r, p: (p[1], r, 0)),
                      pl.BlockSpec((3, tr, c), lambda r, p: (0, r, 0))],
            out_specs=pl.BlockSpec((None, tr, c), lambda r, p: (p[0], r, 0))),
        out_shape=jax.ShapeDtypeStruct((2, rh, c), F32),
        compiler_params=_cp(("parallel",)),
    )(pos, g, recv_a, recv_b)


def _sibling_allgather(halves, name):
    n = len(halves)

    def body(*refs):
        outs = refs[n:2 * n]
        send_sems, recv_sems = refs[2 * n:]
        x, y, c, _ = _mesh_pos()
        cps = []
        for a in range(n):
            cp = pltpu.make_async_remote_copy(
                src_ref=outs[a].at[c], dst_ref=outs[a].at[c], send_sem=send_sems.at[a], recv_sem=recv_sems.at[a],
                device_id=(x, y, 1 - c), device_id_type=MESH_T)
            cp.start()
            cps.append(cp)
        for a in range(n):
            cps[a].wait_send()
            pltpu.make_async_remote_copy(
                src_ref=outs[a].at[1 - c], dst_ref=outs[a].at[1 - c], send_sem=send_sems.at[a],
                recv_sem=recv_sems.at[a], device_id=(x, y, 1 - c), device_id_type=MESH_T).wait_recv()

    return pl.pallas_call(
        body, name=name, in_specs=[ANY] * n, out_specs=[ANY] * n,
        out_shape=[pltpu.HBM(h.shape, F32) for h in halves],
        input_output_aliases={i: i for i in range(n)},
        scratch_shapes=[pltpu.SemaphoreType.DMA((n,)), pltpu.SemaphoreType.DMA((n,))],
    )(*halves)


def _small_allgather(part, after):
    m_per = SMALL_ROWS
    na = len(after)

    def body(x_ref, *refs):
        out_ref, send_sems, recv_sems, local_sem = refs[na:]
        x, y, c, chips = _mesh_pos()
        me, sibling = (x, y, c), (x, y, 1 - c)

        def rows(px, py, pc):
            return out_ref.at[pl.ds((4 * px + 2 * py + pc) * m_per, m_per), :]

        def copy(k, block, to, src=None):
            return pltpu.make_async_remote_copy(
                src_ref=rows(*block) if src is None else src, dst_ref=rows(*block),
                send_sem=send_sems.at[k], recv_sem=recv_sems.at[k], device_id=to, device_id_type=MESH_T)

        mine = pltpu.make_async_copy(x_ref, rows(*me), local_sem)
        mine.start()
        first = [copy(0, me, sibling, src=x_ref)]
        first += [copy(1 + j, me, (*chip, c), src=x_ref) for j, chip in enumerate(chips)]
        for cp in first:
            cp.start()
        passed = [copy(4 + j, (*chip, c), sibling) for j, chip in enumerate(chips)]
        for j, chip in enumerate(chips):
            copy(1 + j, (*chip, c), me).wait_recv()
            passed[j].start()
        copy(0, sibling, me).wait_recv()
        for j, chip in enumerate(chips):
            copy(4 + j, (*chip, 1 - c), me).wait_recv()
        for cp in first + passed:
            cp.wait_send()
        mine.wait()

    return pl.pallas_call(
        body, name="small_allgather",
        out_shape=jax.ShapeDtypeStruct((8 * m_per, D), F32),
        in_specs=[pl.BlockSpec(memory_space=pltpu.VMEM)] + [ANY] * na, out_specs=pl.BlockSpec(memory_space=pltpu.VMEM),
        scratch_shapes=[pltpu.SemaphoreType.DMA((7,)), pltpu.SemaphoreType.DMA((7,)), pltpu.SemaphoreType.DMA],
    )(part, *after)


def _pack_small(ln1_g, ln1_b, gln_g, gln_b, ln2_g, ln2_b, ln3_g, ln3_b, b_gates, b_s, w_s):
    rows = [ln1_g, ln1_b, gln_g, gln_b, ln2_g, ln2_b, ln3_g, ln3_b]
    rows = [r.reshape(1, D) for r in rows] + [b_gates.reshape(2, D), b_s.reshape(1, D), jnp.zeros((5, D), F32),
                                             w_s.reshape(128, D)]
    return jnp.concatenate(rows, axis=0)


def _unpack_small(p):
    out = [p[i:i + 1] for i in range(8)]
    return out + [p[8:10].reshape(1, 2 * D), p[10:11].reshape(1, 8, BLK), p[16:144].reshape(1, 8, BLK, BLK)]


GROUPS = (("f1g", "f1u", "f1d"), ("w_in",), ("w_ab", "w_gb", "w_out"), ("f2g", "f2u", "f2d"))


def _local_step(x, pos_f, target, P, weights_of, grads_ready, flush):
    invf = ROPE_THETA ** (-jnp.arange(0, DH, 2, dtype=F32) / DH)
    invf = jnp.tile(invf, 4).reshape(1, 128)
    b_s_t = P["gmlp_b_s"].T

    W = dict(weights_of(0, []))
    h1, h1b, xh1, rstd1, a1, b1, h1t = _ffn_fwd(x, W["f1g"], W["f1u"], W["f1d"], P["ln1_g"], P["ln1_b"], "ffn1_fwd",
                                                emit_t=True)
    W.update(weights_of(1, [h1b]))
    qkv = _matmul(h1b, W["w_in"], "nn", "proj_qkv", n=3 * ATT_W, b_col0=0, tm=1024, tn=ATT_W)
    z = _matmul(h1b, W["w_in"], "nn", "proj_z", n=2 * GW, b_col0=3 * ATT_W, tm=S, tn=512)
    gl = _matmul(h1b, W["w_in"], "nn", "proj_gates", n=2 * D, b_col0=3 * ATT_W + 2 * GW, tm=S, tn=512)
    qkv_c = _rope_fwd(qkv, pos_f, invf, "rope_fwd")
    og = [_attn_fwd(gi, qkv_c[gi], "attn_fwd_g%d" % gi) for gi in range(NG)]
    y_attn, y_attn_t, lse = _attn_combine([o for o, _ in og], [l for _, l in og], "attn_combine")
    y_gmlp, y_gmlp_t = _gmlp_fwd(z, P["gmlp_ln_g"], P["gmlp_ln_b"], P["gmlp_w_s"], b_s_t, "gmlp_fwd")
    W.update(weights_of(2, [y_gmlp]))
    br_a = _matmul(y_attn, W["w_ab"], "nn", "branch_attn", n=D, tm=1024, tn=D)
    br_b = _matmul(y_gmlp, W["w_gb"], "nn", "branch_gmlp", n=D, tm=1024, tn=D)
    merged, merged_t = _merge_fwd(br_a, br_b, gl, P["b_gates"], "merge_fwd")
    mix = _matmul(merged, W["w_out"], "nn", "mix_out", n=D, tm=1024, tn=D)
    h2, h2b, xh2, rstd2 = _resid_ln(h1, mix, P["ln2_g"], P["ln2_b"], "resid_ln2")
    W.update(weights_of(3, [h2b]))
    y, _, xh3, rstd3, a2, b2 = _ffn_fwd(h2, W["f2g"], W["f2u"], W["f2d"], P["ln3_g"], P["ln3_b"], "ffn2_fwd")

    dr3, dg3, db3, loss = _ln_bwd(y, xh3, rstd3, P["ln3_g"], "loss_ln3_bwd", target=target)
    g_f2g, g_f2u, g_f2d, dh2 = _ffn_bwd(dr3, h2b, a2, b2, W["f2g"], W["f2u"], W["f2d"], "ffn2_bwd")
    tok = grads_ready(3, dict(f2g=g_f2g, f2u=g_f2u, f2d=g_f2d))
    dr2, dg2, db2 = _ln_bwd(dh2, xh2, rstd2, P["ln2_g"], "ln2_bwd", after=tok)
    g_wout = _wgrad(merged_t, dr2, 128, D, "dw_out", row_sharded=True)
    dmerged = _matmul(dr2, W["w_out"], "nt", "dmerged", n=D, tm=1024, tn=D)
    dab, dbb, dglb, dbg = _merge_bwd(dmerged, br_a, br_b, gl, P["b_gates"], "merge_bwd")
    tok = flush([dab])
    g_wab = _wgrad(y_attn_t, dab, GRP_W // 2, 256, "dw_attn_branch", row_sharded=False, after=tok)
    g_wgb = _wgrad(y_gmlp_t, dbb, 128, D, "dw_gmlp_branch", row_sharded=True)
    tok = grads_ready(2, dict(w_ab=g_wab, w_gb=g_wgb, w_out=g_wout))
    dy_attn = _matmul(dab, W["w_ab"], "nt", "dy_attn", n=GRP_W, tm=1024, tn=GRP_W, after=tok)
    dy_gmlp = _matmul(dbb, W["w_gb"], "nt", "dy_gmlp", n=GW, tm=1024, tn=GW)
    dzb, dws, dbs_t, dgln_g, dgln_b = _gmlp_bwd(z, dy_gmlp, P["gmlp_ln_g"], P["gmlp_ln_b"], P["gmlp_w_s"], b_s_t,
                                                 "gmlp_bwd")
    cls = _class_order([dy_attn, y_attn, lse], "attn_class_order")
    dqkv_c = []
    for gi in range(NG):
        dy_c, y_c, lse_c = [t[None] if gi == 0 else cls[2 * a + gi - 1] for a, t in enumerate((dy_attn, y_attn, lse))]
        dqkv_c.append(_attn_bwd(gi, qkv_c[gi], dy_c, y_c, lse_c, "attn_bwd_g%d" % gi))
    dqkvb = _rope_bwd(dqkv_c, pos_f, invf, "rope_bwd")
    dproj = jnp.concatenate([dqkvb, dzb, dglb], axis=1)
    tok = flush([dproj])
    g_win = _wgrad(h1t, dproj, D // 2, IN_SH, "dw_in", row_sharded=False, after=tok)
    tok = grads_ready(1, dict(w_in=g_win))
    dh1 = _matmul(dproj, W["w_in"], "nt", "dh1", n=D, tn=D, tk=IN_SH, add=dr2, add_scale=ALPHA, after=tok)
    dr1, dg1, db1 = _ln_bwd(dh1, xh1, rstd1, P["ln1_g"], "ln1_bwd")
    tok = flush([dr1])
    g_f1g, g_f1u, g_f1d, dx = _ffn_bwd(dr1, x.astype(BF16), a1, b1, W["f1g"], W["f1u"], W["f1d"], "ffn1_bwd",
                                       after=tok)
    grads_ready(0, dict(f1g=g_f1g, f1u=g_f1u, f1d=g_f1d))
    flush([dx])

    small = _pack_small(dg1, db1, dgln_g, dgln_b, dg2, db2, dg3, db3, dbg, dbs_t.T, dws)
    return loss, dx, small


BIG = ("f1g", "f1u", "f1d", "w_in", "w_ab", "w_gb", "w_out", "f2g", "f2u", "f2d")
TRANSPOSED = ("f1g", "f1u", "f2g", "f2u")
KIND = dict(f1g="stack", f1u="stack", f1d="stack", w_in="col", w_ab="col", w_gb="stack", w_out="stack",
            f2g="stack", f2u="stack", f2d="stack")


def kernel(x, positions, ffn1_w_gate, ffn1_w_up, ffn1_w_down, ln1_g, ln1_b, w_in, b_gates, gmlp_ln_g, gmlp_ln_b, gmlp_w_s, gmlp_b_s, w_attn_branch, w_gmlp_branch, w_out, ln2_g, ln2_b, ffn2_w_gate, ffn2_w_up, ffn2_w_down, ln3_g, ln3_b, loss_target, m_ffn1_w_gate, m_ffn1_w_up, m_ffn1_w_down, m_ln1_g, m_ln1_b, m_w_in, m_b_gates, m_gmlp_ln_g, m_gmlp_ln_b, m_gmlp_w_s, m_gmlp_b_s, m_w_attn_branch, m_w_gmlp_branch, m_w_out, m_ln2_g, m_ln2_b, m_ffn2_w_gate, m_ffn2_w_up, m_ffn2_w_down, m_ln3_g, m_ln3_b, v_ffn1_w_gate, v_ffn1_w_up, v_ffn1_w_down, v_ln1_g, v_ln1_b, v_w_in, v_b_gates, v_gmlp_ln_g, v_gmlp_ln_b, v_gmlp_w_s, v_gmlp_b_s, v_w_attn_branch, v_w_gmlp_branch, v_w_out, v_ln2_g, v_ln2_b, v_ffn2_w_gate, v_ffn2_w_up, v_ffn2_w_down, v_ln3_g, v_ln3_b):
    cx, cy, cc = lax.axis_index("x"), lax.axis_index("y"), lax.axis_index("c")
    pos = jnp.stack([cc, 2 * cx + cy]).astype(jnp.int32)

    w_sh = dict(f1g=ffn1_w_gate, f1u=ffn1_w_up, f1d=ffn1_w_down, w_in=w_in, w_ab=w_attn_branch,
                w_gb=w_gmlp_branch, w_out=w_out, f2g=ffn2_w_gate, f2u=ffn2_w_up, f2d=ffn2_w_down)
    m_sh = dict(f1g=m_ffn1_w_gate, f1u=m_ffn1_w_up, f1d=m_ffn1_w_down, w_in=m_w_in, w_ab=m_w_attn_branch,
                w_gb=m_w_gmlp_branch, w_out=m_w_out, f2g=m_ffn2_w_gate, f2u=m_ffn2_w_up, f2d=m_ffn2_w_down)
    v_sh = dict(f1g=v_ffn1_w_gate, f1u=v_ffn1_w_up, f1d=v_ffn1_w_down, w_in=v_w_in, w_ab=v_w_attn_branch,
                w_gb=v_w_gmlp_branch, w_out=v_w_out, f2g=v_ffn2_w_gate, f2u=v_ffn2_w_up, f2d=v_ffn2_w_down)
    w_sh = {k: (v[0].T if k in TRANSPOSED else v[0]) for k, v in w_sh.items()}
    m_sh = {k: (v[0].T if k in TRANSPOSED else v[0]) for k, v in m_sh.items()}
    v_sh = {k: (v[0].T if k in TRANSPOSED else v[0]) for k, v in v_sh.items()}

    started, tokens = [], []
    for gi, names in enumerate(GROUPS):
        placed = [_place_shard(w_sh[k], KIND[k], pos, "place_" + k) for k in names]
        fulls, ssem, rsem, token = _gather_start(placed, [KIND[k] for k in names], [w_sh[k].shape for k in names],
                                                 tokens[-1:], "gather_start_g%d" % gi)
        started.append((fulls, ssem, rsem))
        tokens.append(token)

    def weights_of(gi, after):
        names = GROUPS[gi]
        kinds, dims = [KIND[k] for k in names], [w_sh[k].shape for k in names]
        fulls, ssem, rsem = started[gi]
        fulls = _gather_wait(fulls, ssem, rsem, kinds, dims, list(after) + (tokens if gi == 0 else []),
                             "gather_wait_g%d" % gi)
        fulls = _gather_forward(fulls, kinds, dims, "gather_forward_g%d" % gi)
        return {k: (f.reshape(D, D) if k in ("w_gb", "w_out") else f) for k, f in zip(names, fulls)}

    pending, inflight = [], {}

    def grads_ready(gi, gd):
        grads = [gd[k] for k in GROUPS[gi]]
        lands = [lax.empty(g.shape[1:], F32) for g in grads]
        grads, lands, ssem, rsem, token = _pair_start(grads, lands, "rs_pair_start_g%d" % gi)
        pending.append((gi, grads, lands, ssem, rsem))
        return [token]

    def flush(after):
        gi, grads, lands, ssem, rsem = pending.pop()
        names = GROUPS[gi]
        grads, recv_a = _pair_wait(grads, lands, ssem, rsem, after, "rs_pair_wait_g%d" % gi)
        psums = [_pair_sum(g, r, pos, "rs_pair_sum_" + k) for g, r, k in zip(grads, recv_a, names)]
        lands = [lax.empty((3,) + p.shape[1:], BF16) for p in psums]
        psums, lands, ssem, rsem, token = _chip_start(psums, lands, "rs_chip_start_g%d" % gi)
        inflight[gi] = (grads, recv_a, psums, lands, ssem, rsem, token)
        return [token]

    P = dict(ln1_g=ln1_g, ln1_b=ln1_b, ln2_g=ln2_g, ln2_b=ln2_b, ln3_g=ln3_g, ln3_b=ln3_b, b_gates=b_gates,
             gmlp_ln_g=gmlp_ln_g, gmlp_ln_b=gmlp_ln_b, gmlp_w_s=gmlp_w_s[0], gmlp_b_s=gmlp_b_s[0])
    pos_f = positions.reshape(S, 1).astype(F32)
    loss_part, dx, small = _local_step(x[0], pos_f, loss_target[0], P, weights_of, grads_ready, flush)
    loss = lax.psum(loss_part[0, 0], ("x", "y", "c"))

    g_out, d_out, m_out, v_out = {}, {}, {}, {}

    def finish(gi, after):
        grads, recv_a, psums, lands, ssem, rsem, token = inflight[gi]
        recv_b = _chip_wait(psums, lands, ssem, rsem, after + [inflight[0][6]], "rs_chip_wait_g%d" % gi)
        halves = [_owner_sum(g, ra, rb, pos, "rs_owner_sum_" + k)
                  for g, ra, rb, k in zip(grads, recv_a, recv_b, GROUPS[gi])]
        reduced = _sibling_allgather(halves, "rs_sibling_allgather_g%d" % gi)
        for k, gfull in zip(GROUPS[gi], reduced):
            res = _adamw(w_sh[k], gfull.reshape(w_sh[k].shape), m_sh[k], v_sh[k], "adamw_" + k)
            after = [res[1]]
            if k in TRANSPOSED:
                res = [r.T for r in res]
            g_out[k], d_out[k], m_out[k], v_out[k] = [r[None] for r in res]
        return after

    after = []
    for gi in (3, 2, 1):
        after = finish(gi, after)

    parts = _small_allgather(small, after).reshape(8, SMALL_ROWS, D)
    sp = (ln1_g, ln1_b, gmlp_ln_g, gmlp_ln_b, ln2_g, ln2_b, ln3_g, ln3_b, b_gates, gmlp_b_s, gmlp_w_s)
    sm = (m_ln1_g, m_ln1_b, m_gmlp_ln_g, m_gmlp_ln_b, m_ln2_g, m_ln2_b, m_ln3_g, m_ln3_b, m_b_gates, m_gmlp_b_s,
          m_gmlp_w_s)
    sv = (v_ln1_g, v_ln1_b, v_gmlp_ln_g, v_gmlp_ln_b, v_ln2_g, v_ln2_b, v_ln3_g, v_ln3_b, v_b_gates, v_gmlp_b_s,
          v_gmlp_w_s)
    sg, sd, smn, svn = _small_sum_adamw(parts, _pack_small(*sp), _pack_small(*sm), _pack_small(*sv), "small_adamw")
    names = ("ln1_g", "ln1_b", "gmlp_ln_g", "gmlp_ln_b", "ln2_g", "ln2_b", "ln3_g", "ln3_b", "b_gates", "gmlp_b_s",
             "gmlp_w_s")
    for dst, packed in ((g_out, sg), (d_out, sd), (m_out, smn), (v_out, svn)):
        for nm, val in zip(names, _unpack_small(packed)):
            dst[nm] = val
    finish(0, [sg])

    order = ("f1g", "f1u", "f1d", "ln1_g", "ln1_b", "w_in", "b_gates", "gmlp_ln_g", "gmlp_ln_b", "gmlp_w_s", "gmlp_b_s",
             "w_ab", "w_gb", "w_out", "ln2_g", "ln2_b", "f2g", "f2u", "f2d", "ln3_g", "ln3_b")
    outs = [loss, dx[None]]
    for dst in (g_out, d_out, m_out, v_out):
        outs += [dst[k] for k in order]
    return tuple(outs)
```

```python
import functools
import math

import jax
import jax.numpy as jnp
from jax import lax
from jax.experimental import pallas as pl
from jax.experimental.pallas import tpu as pltpu

F32 = jnp.float32
BF16 = jnp.bfloat16

S = 2048
D = 1024
NSH = 4
FSH = 704
ATT_W = 1536
GRP_W = 512
NG = 3
NH = 8
DH = 64
BLK = 128
NBLK = S // BLK
GW = 1024
IN_W = 8704
IN_SH = IN_W // NSH
ALPHA = 2.0 ** 0.25
LN_EPS = 1e-5
ROPE_THETA = 10000.0
DILATIONS = (1, 4, 16)
ADAM_LR, ADAM_B1, ADAM_B2, ADAM_EPS, ADAM_WD, ADAM_STEP = 0.001, 0.9, 0.999, 1e-08, 0.01, 10
SMALL_ROWS = 144
MESH_T = pl.DeviceIdType.MESH
MIB = 1024 * 1024
NEG_INF = float("-inf")


def _cp(sem, vmem_mib=48):
    return pltpu.CompilerParams(dimension_semantics=sem, vmem_limit_bytes=vmem_mib * MIB)


def _ln_stats(r):
    mu = jnp.mean(r, axis=-1, keepdims=True)
    xc = r - mu
    var = jnp.mean(xc * xc, axis=-1, keepdims=True)
    rstd = lax.rsqrt(var + LN_EPS)
    return xc * rstd, rstd


def _ln_dx(dxh, xh, rstd):
    m1 = jnp.mean(dxh, axis=-1, keepdims=True)
    m2 = jnp.mean(dxh * xh, axis=-1, keepdims=True)
    return rstd * (dxh - m1 - xh * m2)


def _dot_nt(a, b):
    return lax.dot_general(a, b, (((1,), (1,)), ((), ())), preferred_element_type=F32)


def _dot_tn(a, b):
    return lax.dot_general(a, b, (((0,), (0,)), ((), ())), preferred_element_type=F32)


def _dot(a, b):
    return jnp.dot(a, b, preferred_element_type=F32)


def _ffn_fwd(xin, wgt, wut, wd, ln_g, ln_b, name, emit_t=False):
    tm = 512

    def body(x_ref, wg_ref, wu_ref, wd_ref, g_ref, b_ref, *rest):
        if emit_t:
            h_ref, hb_ref, xh_ref, rstd_ref, a_ref, bb_ref, ht_ref, acc_ref = rest
        else:
            h_ref, hb_ref, xh_ref, rstd_ref, a_ref, bb_ref, acc_ref = rest
        j = pl.program_id(1)
        xb = x_ref[...].astype(BF16)
        a = _dot_nt(xb, wg_ref[...])
        b = _dot_nt(xb, wu_ref[...])
        a_ref[...] = a.astype(BF16)
        bb_ref[...] = b.astype(BF16)
        s = (a * jax.nn.sigmoid(a)) * b
        f = _dot(s.astype(BF16), wd_ref[...])

        @pl.when(j == 0)
        def _():
            acc_ref[...] = f

        @pl.when(j > 0)
        def _():
            acc_ref[...] += f

        @pl.when(j == NSH - 1)
        def _():
            r = ALPHA * x_ref[...] + 0.5 * acc_ref[...]
            xh, rstd = _ln_stats(r)
            h = xh * g_ref[...] + b_ref[...]
            h_ref[...] = h
            hb_ref[...] = h.astype(BF16)
            xh_ref[...] = xh
            rstd_ref[...] = rstd
            if emit_t:
                ht_ref[...] = h.T.astype(BF16)

    row = pl.BlockSpec((tm, D), lambda i, j: (i, 0))
    vec = pl.BlockSpec((1, D), lambda i, j: (0, 0))
    wsp = pl.BlockSpec((None, FSH, D), lambda i, j: (j, 0, 0))
    ab = pl.BlockSpec((None, tm, FSH), lambda i, j: (j, i, 0))
    out_specs = [row, row, row, pl.BlockSpec((tm, 1), lambda i, j: (i, 0)), ab, ab]
    out_shape = [jax.ShapeDtypeStruct((S, D), F32), jax.ShapeDtypeStruct((S, D), BF16),
                 jax.ShapeDtypeStruct((S, D), F32), jax.ShapeDtypeStruct((S, 1), F32),
                 jax.ShapeDtypeStruct((NSH, S, FSH), BF16), jax.ShapeDtypeStruct((NSH, S, FSH), BF16)]
    if emit_t:
        out_specs.append(pl.BlockSpec((D, tm), lambda i, j: (0, i)))
        out_shape.append(jax.ShapeDtypeStruct((D, S), BF16))
    return pl.pallas_call(
        body, name=name, grid=(S // tm, NSH),
        in_specs=[row, wsp, wsp, wsp, vec, vec], out_specs=out_specs, out_shape=out_shape,
        scratch_shapes=[pltpu.VMEM((tm, D), F32)],
        compiler_params=_cp(("parallel", "arbitrary")),
    )(xin, wgt, wut, wd, ln_g, ln_b)


def _ffn_bwd(dr, xin_b, a, b, wgt, wut, wd, name, after=()):
    tm = 512
    ni = S // tm
    hr = FSH // 2

    def body(dr_ref, a_ref, b_ref, wg_ref, wu_ref, wd_ref, x_hbm, *rest):
        dwg_hbm, dwu_hbm, dwd_hbm, dx_hbm, dx_acc, da_all, db_all, s_all, df_all, x_all, res_buf, sems = rest[len(after):]
        j = pl.program_id(0)
        i = pl.program_id(1)
        rows = pl.ds(pl.multiple_of(i * tm, tm), tm)

        @pl.when(jnp.logical_and(j == 0, i == 0))
        def _():
            cp = pltpu.make_async_copy(x_hbm, x_all, sems.at[0])
            cp.start()
            cp.wait()

        drv = dr_ref[...]
        df = (0.5 * drv).astype(BF16)

        @pl.when(j == 0)
        def _():
            df_all[rows, :] = df

        ds = jnp.concatenate([_dot_nt(df, wd_ref[0:384, :]), _dot_nt(df, wd_ref[384:FSH, :])], axis=1)
        av = a_ref[...].astype(F32)
        bv = b_ref[...].astype(F32)
        sig = jax.nn.sigmoid(av)
        sl = av * sig
        da = (ds * bv * (sig * (1.0 + av * (1.0 - sig)))).astype(BF16)
        db = (ds * sl).astype(BF16)
        da_all[rows, :] = da
        db_all[rows, :] = db
        s_all[rows, :] = (sl * bv).astype(BF16)
        dx = _dot(da, wg_ref[...]) + _dot(db, wu_ref[...])

        @pl.when(j == 0)
        def _():
            dx_acc[rows, :] = ALPHA * drv + dx

        @pl.when(j > 0)
        def _():
            dx_acc[rows, :] += dx

        @pl.when(i == ni - 1)
        def _():
            copies = []
            for n, (lhs, rhs, out) in enumerate(((da_all, x_all, dwg_hbm), (db_all, x_all, dwu_hbm),
                                                 (s_all, df_all, dwd_hbm))):
                slot = n % 2
                if n >= 2:
                    for cp in copies[2 * (n - 2): 2 * (n - 2) + 2]:
                        cp.wait()
                res_buf[slot] = _dot_tn(lhs[...], rhs[...])
                for h in range(2):
                    cp = pltpu.make_async_copy(res_buf.at[slot, pl.ds(h * hr, hr), :], out.at[h, j],
                                               sems.at[1 + 2 * slot + h])
                    cp.start()
                    copies.append(cp)
            for cp in copies[2:]:
                cp.wait()

        @pl.when(jnp.logical_and(j == NSH - 1, i == ni - 1))
        def _():
            cp = pltpu.make_async_copy(dx_acc, dx_hbm, sems.at[0])
            cp.start()
            cp.wait()

    row = pl.BlockSpec((tm, D), lambda j, i: (i, 0))
    wsp = pl.BlockSpec((None, FSH, D), lambda j, i: (j, 0, 0))
    ab = pl.BlockSpec((None, tm, FSH), lambda j, i: (j, i, 0))
    dwshape = jax.ShapeDtypeStruct((2, NSH, hr, D), F32)
    return pl.pallas_call(
        body, name=name, grid=(NSH, ni),
        in_specs=[row, ab, ab, wsp, wsp, wsp, ANY] + [ANY] * len(after),
        out_specs=[ANY, ANY, ANY, ANY],
        out_shape=[dwshape, dwshape, dwshape, jax.ShapeDtypeStruct((S, D), F32)],
        scratch_shapes=[pltpu.VMEM((S, D), F32), pltpu.VMEM((S, FSH), BF16), pltpu.VMEM((S, FSH), BF16),
                        pltpu.VMEM((S, FSH), BF16), pltpu.VMEM((S, D), BF16), pltpu.VMEM((S, D), BF16),
                        pltpu.VMEM((2, FSH, D), F32), pltpu.SemaphoreType.DMA((5,))],
        compiler_params=_cp(("arbitrary", "arbitrary"), vmem_mib=58),
    )(dr, a, b, wgt, wut, wd, xin_b, *after)


def _matmul(a, b, mode, name, *, n, tm=512, tn=512, tk=None, b_col0=0, add=None, add_scale=1.0, out_dtype=F32,
            after=()):
    m, ka = a.shape
    tk = ka if tk is None else tk
    nk = ka // tk
    assert m % tm == 0 and n % tn == 0 and ka % tk == 0 and b_col0 % tn == 0
    off = b_col0 // tn
    na = len(after)

    def body(*refs):
        refs = refs[na:]
        if add is None:
            a_ref, b_ref, o_ref = refs[:3]
            add_ref = None
            rest = refs[3:]
        else:
            a_ref, b_ref, add_ref, o_ref = refs[:4]
            rest = refs[4:]
        k = pl.program_id(2)
        av = a_ref[...].astype(BF16)
        bv = b_ref[...].astype(BF16)
        p = _dot(av, bv) if mode == "nn" else _dot_nt(av, bv)

        def finish(acc):
            if add_ref is not None:
                acc = acc + add_scale * add_ref[...]
            o_ref[...] = acc.astype(out_dtype)

        if nk == 1:
            finish(p)
        else:
            acc_ref = rest[0]

            @pl.when(k == 0)
            def _():
                acc_ref[...] = p

            @pl.when(k > 0)
            def _():
                acc_ref[...] += p

            @pl.when(k == nk - 1)
            def _():
                finish(acc_ref[...])

    a_spec = pl.BlockSpec((tm, tk), lambda i, j, k: (i, k))
    if mode == "nn":
        b_spec = pl.BlockSpec((tk, tn), lambda i, j, k: (k, j + off))
    else:
        b_spec = pl.BlockSpec((tn, tk), lambda i, j, k: (j, k))
    o_spec = pl.BlockSpec((tm, tn), lambda i, j, k: (i, j))
    in_specs = [pl.BlockSpec(memory_space=pl.ANY)] * na + [a_spec, b_spec] + ([o_spec] if add is not None else [])
    args = tuple(after) + (a, b) + ((add,) if add is not None else ())
    return pl.pallas_call(
        body, name=name, grid=(m // tm, n // tn, nk),
        in_specs=in_specs, out_specs=o_spec,
        out_shape=jax.ShapeDtypeStruct((m, n), out_dtype),
        scratch_shapes=[pltpu.VMEM((tm, tn), F32)] if nk > 1 else [],
        compiler_params=_cp(("parallel", "parallel", "arbitrary")),
    )(*args)


def _wgrad(xt, y, rh, c, name, row_sharded, after=()):
    na = len(after)
    if row_sharded:
        def body(x_ref, y_ref, *rest):
            o_ref = rest[na]
            res = _dot(x_ref[...], y_ref[...].astype(BF16))
            for j in range(NSH):
                for h in range(2):
                    o_ref[h, j] = res[(2 * j + h) * rh:(2 * j + h + 1) * rh, :]

        grid = (1,)
        in_specs = [pl.BlockSpec((2 * NSH * rh, S), lambda g: (0, 0)), pl.BlockSpec((S, c), lambda g: (0, 0))]
        out_specs = pl.BlockSpec((2, NSH, rh, c), lambda g: (0, 0, 0, 0))
        sem = ("arbitrary",)
    else:
        def body(x_ref, y_ref, *rest):
            rest[na][...] = _dot(x_ref[...], y_ref[...].astype(BF16))

        grid = (2, NSH)
        in_specs = [pl.BlockSpec((rh, S), lambda h, j: (h, 0)), pl.BlockSpec((S, c), lambda h, j: (0, j))]
        out_specs = pl.BlockSpec((None, None, rh, c), lambda h, j: (h, j, 0, 0))
        sem = ("parallel", "parallel")
    return pl.pallas_call(
        body, name=name, grid=grid, in_specs=in_specs + [pl.BlockSpec(memory_space=pl.ANY)] * na, out_specs=out_specs,
        out_shape=jax.ShapeDtypeStruct((2, NSH, rh, c), F32),
        compiler_params=_cp(sem, vmem_mib=56),
    )(xt, y, *after)


def _resid_ln(res, f, ln_g, ln_b, name):
    tm = 256

    def body(res_ref, f_ref, g_ref, b_ref, h_ref, hb_ref, xh_ref, rstd_ref):
        r = ALPHA * res_ref[...] + f_ref[...]
        xh, rstd = _ln_stats(r)
        h = xh * g_ref[...] + b_ref[...]
        h_ref[...] = h
        hb_ref[...] = h.astype(BF16)
        xh_ref[...] = xh
        rstd_ref[...] = rstd

    row = pl.BlockSpec((tm, D), lambda i: (i, 0))
    vec = pl.BlockSpec((1, D), lambda i: (0, 0))
    return pl.pallas_call(
        body, name=name, grid=(S // tm,),
        in_specs=[row, row, vec, vec],
        out_specs=[row, row, row, pl.BlockSpec((tm, 1), lambda i: (i, 0))],
        out_shape=[jax.ShapeDtypeStruct((S, D), F32), jax.ShapeDtypeStruct((S, D), BF16),
                   jax.ShapeDtypeStruct((S, D), F32), jax.ShapeDtypeStruct((S, 1), F32)],
        compiler_params=_cp(("parallel",)),
    )(res, f, ln_g, ln_b)


def _ln_bwd(dout, xh, rstd, ln_g, name, target=None, after=()):
    tm = 256
    with_loss = target is not None
    na = len(after)

    def body(*refs):
        refs = refs[na:]
        if with_loss:
            y_ref, t_ref, xh_ref, rstd_ref, g_ref, dr_ref, dg_ref, db_ref, loss_ref = refs
            err = y_ref[...] - t_ref[...]
            dy = err * (1.0 / D)
        else:
            y_ref, xh_ref, rstd_ref, g_ref, dr_ref, dg_ref, db_ref = refs
            dy = y_ref[...]
        i = pl.program_id(0)
        xh = xh_ref[...]
        dr_ref[...] = _ln_dx(dy * g_ref[...], xh, rstd_ref[...])
        dg = jnp.sum(dy * xh, axis=0, keepdims=True)
        db = jnp.sum(dy, axis=0, keepdims=True)

        @pl.when(i == 0)
        def _():
            dg_ref[...] = dg
            db_ref[...] = db

        @pl.when(i > 0)
        def _():
            dg_ref[...] += dg
            db_ref[...] += db

        if with_loss:
            part = 0.5 * jnp.sum(jnp.mean(err * err, axis=-1, keepdims=True), axis=0, keepdims=True)
            part = jnp.broadcast_to(part, (8, 128))

            @pl.when(i == 0)
            def _():
                loss_ref[...] = part

            @pl.when(i > 0)
            def _():
                loss_ref[...] += part

    row = pl.BlockSpec((tm, D), lambda i: (i, 0))
    vec = pl.BlockSpec((1, D), lambda i: (0, 0))
    col = pl.BlockSpec((tm, 1), lambda i: (i, 0))
    in_specs = [pl.BlockSpec(memory_space=pl.ANY)] * na + [row] + ([row] if with_loss else []) + [row, col, vec]
    out_specs = [row, vec, vec] + ([pl.BlockSpec((8, 128), lambda i: (0, 0))] if with_loss else [])
    out_shape = [jax.ShapeDtypeStruct((S, D), F32), jax.ShapeDtypeStruct((1, D), F32),
                 jax.ShapeDtypeStruct((1, D), F32)] + ([jax.ShapeDtypeStruct((8, 128), F32)] if with_loss else [])
    args = tuple(after) + (dout,) + ((target,) if with_loss else ()) + (xh, rstd, ln_g)
    return pl.pallas_call(
        body, name=name, grid=(S // tm,), in_specs=in_specs, out_specs=out_specs, out_shape=out_shape,
        compiler_params=_cp(("arbitrary",)),
    )(*args)


ROPE_TM = 256


def _rope_tables(pos_ref, invf_ref, sign):
    ang = pos_ref[...] * invf_ref[...]
    lane = lax.broadcasted_iota(jnp.int32, ang.shape, 1)
    first = (lane % DH) < (DH // 2)
    sinv = jnp.sin(ang) * sign
    return first, jnp.cos(ang), jnp.where(first, -sinv, sinv)


def _rotate(x, first, cosf, sinf):
    return x * cosf + jnp.where(first, pltpu.roll(x, 96, 1), pltpu.roll(x, 32, 1)) * sinf


def _rope_fwd(qkv, pos_f, invf, name):
    tm = ROPE_TM

    def body(t_ref, pos_ref, invf_ref, o0_ref, o1_ref, o2_ref, buf_ref):
        first, cosf, sinf = _rope_tables(pos_ref, invf_ref, 1.0)
        o_refs = (o0_ref, o1_ref, o2_ref)
        for sec in range(3):
            for gi, d in enumerate(DILATIONS):
                for ch in range(GRP_W // 128):
                    src = sec * ATT_W + gi * GRP_W + ch * 128
                    dst = slice(sec * GRP_W + ch * 128, sec * GRP_W + (ch + 1) * 128)
                    x = t_ref[:, src:src + 128]
                    if sec < 2:
                        x = _rotate(x, first, cosf, sinf)
                    if d == 1:
                        o_refs[gi][0, :, dst] = x.astype(BF16)
                    else:
                        buf_ref[...] = x
                        for r in range(d):
                            o_refs[gi][r, :, dst] = buf_ref[pl.ds(r, tm // d, stride=d), :].astype(BF16)

    return pl.pallas_call(
        body, name=name, grid=(S // tm,),
        in_specs=[pl.BlockSpec((tm, 3 * ATT_W), lambda i: (i, 0)), pl.BlockSpec((tm, 1), lambda i: (i, 0)),
                  pl.BlockSpec((1, 128), lambda i: (0, 0))],
        out_specs=[pl.BlockSpec((d, tm // d, 3 * GRP_W), lambda i: (0, i, 0)) for d in DILATIONS],
        out_shape=[jax.ShapeDtypeStruct((d, S // d, 3 * GRP_W), BF16) for d in DILATIONS],
        scratch_shapes=[pltpu.VMEM((tm, 128), F32)],
        compiler_params=_cp(("parallel",)),
    )(qkv, pos_f, invf)


def _rope_bwd(dqkv_c, pos_f, invf, name):
    tm = ROPE_TM

    def body(*refs):
        g_refs, (pos_ref, invf_ref, o_ref, buf_ref) = refs[:9], refs[9:]
        first, cosf, sinf = _rope_tables(pos_ref, invf_ref, -1.0)
        for sec in range(3):
            for gi, d in enumerate(DILATIONS):
                g_ref = g_refs[3 * gi + sec]
                for ch in range(GRP_W // 128):
                    cols = slice(ch * 128, (ch + 1) * 128)
                    if d == 1:
                        x = g_ref[0, :, cols]
                    else:
                        for r in range(d):
                            buf_ref[pl.ds(r, tm // d, stride=d), :] = g_ref[r, :, cols]
                        x = buf_ref[...]
                    if sec < 2:
                        x = _rotate(x, first, cosf, sinf)
                    dst = sec * ATT_W + gi * GRP_W + ch * 128
                    o_ref[:, dst:dst + 128] = x.astype(BF16)

    g_specs = [pl.BlockSpec((d, tm // d, GRP_W), lambda i: (0, i, 0)) for d in DILATIONS for _ in range(3)]
    return pl.pallas_call(
        body, name=name, grid=(S // tm,),
        in_specs=g_specs + [pl.BlockSpec((tm, 1), lambda i: (i, 0)), pl.BlockSpec((1, 128), lambda i: (0, 0))],
        out_specs=pl.BlockSpec((tm, 3 * ATT_W), lambda i: (i, 0)),
        out_shape=jax.ShapeDtypeStruct((S, 3 * ATT_W), BF16),
        scratch_shapes=[pltpu.VMEM((tm, 128), F32)],
        compiler_params=_cp(("parallel",)),
    )(*[g for grp in dqkv_c for g in grp], pos_f, invf)


def _class_order(ts, name):
    tm = ROPE_TM
    n = len(ts)

    def body(*refs):
        buf_ref = refs[3 * n]
        for a in range(n):
            for ch in range(GRP_W // 128):
                cols = slice(ch * 128, (ch + 1) * 128)
                buf_ref[...] = refs[a][:, cols]
                for b, d in enumerate(DILATIONS[1:]):
                    for r in range(d):
                        refs[n + 2 * a + b][r, :, cols] = buf_ref[pl.ds(r, tm // d, stride=d), :]

    return pl.pallas_call(
        body, name=name, grid=(S // tm,),
        in_specs=[pl.BlockSpec((tm, GRP_W), lambda i: (i, 0))] * n,
        out_specs=[pl.BlockSpec((d, tm // d, GRP_W), lambda i: (0, i, 0)) for _ in range(n) for d in DILATIONS[1:]],
        out_shape=[jax.ShapeDtypeStruct((d, S // d, GRP_W), F32) for _ in range(n) for d in DILATIONS[1:]],
        scratch_shapes=[pltpu.VMEM((tm, 128), F32)],
        compiler_params=_cp(("parallel",)),
    )(*ts)


def _heads(ref):
    return jnp.stack([ref[:, h * DH:(h + 1) * DH] for h in range(NH)])


def _bdot_nt(a, b):
    return lax.dot_general(a, b, (((2,), (2,)), ((0,), (0,))), preferred_element_type=F32)


def _bdot(a, b):
    return lax.dot_general(a, b, (((2,), (1,)), ((0,), (0,))), preferred_element_type=F32)


def _bdot_tn(a, b):
    return lax.dot_general(a, b, (((1,), (1,)), ((0,), (0,))), preferred_element_type=F32)


def _attn_fwd(gi, qkv_c, name):
    d = DILATIONS[gi]
    nblk = S // d // BLK

    def body(*refs):
        if nblk > 1:
            q_ref, kc_ref, kp_ref, vc_ref, vp_ref, o_ref, lse_ref = refs
            has_prev = pl.program_id(1) != 0
        else:
            q_ref, kc_ref, vc_ref, o_ref, lse_ref = refs
        qi = lax.broadcasted_iota(jnp.int32, (NH, BLK, BLK), 1)
        kj = lax.broadcasted_iota(jnp.int32, (NH, BLK, BLK), 2)
        q = _heads(q_ref)
        sc = jnp.where(kj <= qi, _bdot_nt(q, _heads(kc_ref)) * 0.125, NEG_INF)
        m = jnp.max(sc, axis=-1, keepdims=True)
        if nblk > 1:
            mask_p = jnp.logical_and(kj >= qi, has_prev)
            sp = jnp.where(mask_p, _bdot_nt(q, _heads(kp_ref)) * 0.125, NEG_INF)
            m = jnp.maximum(m, jnp.max(sp, axis=-1, keepdims=True))
        pc = jnp.exp(sc - m)
        l = jnp.sum(pc, axis=-1, keepdims=True)
        o = _bdot(pc.astype(BF16), _heads(vc_ref))
        if nblk > 1:
            pp = jnp.exp(sp - m)
            l = l + jnp.sum(pp, axis=-1, keepdims=True)
            o = o + _bdot(pp.astype(BF16), _heads(vp_ref))
        o = o / l
        lse = m + jnp.log(l)
        for h in range(NH):
            sl = slice(h * DH, (h + 1) * DH)
            o_ref[:, sl] = o[h]
            lse_ref[:, sl] = jnp.broadcast_to(lse[h], (BLK, DH))

    def cur(sec):
        return pl.BlockSpec((None, BLK, GRP_W), lambda r, n: (r, n, sec))

    def prev(sec):
        return pl.BlockSpec((None, BLK, GRP_W), lambda r, n: (r, jnp.maximum(n - 1, 0), sec))

    out = pl.BlockSpec((None, BLK, GRP_W), lambda r, n: (r, n, 0))
    shp = jax.ShapeDtypeStruct((d, S // d, GRP_W), F32)
    if nblk > 1:
        in_specs, args = [cur(0), cur(1), prev(1), cur(2), prev(2)], (qkv_c,) * 5
    else:
        in_specs, args = [cur(0), cur(1), cur(2)], (qkv_c,) * 3
    return pl.pallas_call(
        body, name=name, grid=(d, nblk), in_specs=in_specs, out_specs=[out, out], out_shape=[shp, shp],
        compiler_params=_cp(("parallel", "parallel")),
    )(*args)


def _attn_combine(os, lses, name):
    tm = ROPE_TM

    def body(o0_ref, o1_ref, o2_ref, l0_ref, l1_ref, l2_ref, y_ref, yt_ref, l_ref, buf_ref):
        def token_order(ref, d, cols, slot):
            if d == 1:
                return ref[0, :, cols]
            for r in range(d):
                buf_ref[slot, pl.ds(r, tm // d, stride=d), :] = ref[r, :, cols]
            return buf_ref[slot]

        for ch in range(GRP_W // 128):
            cols = slice(ch * 128, (ch + 1) * 128)
            o = [token_order(ref, d, cols, k) for k, (ref, d) in enumerate(zip((o0_ref, o1_ref, o2_ref), DILATIONS))]
            ls = [token_order(ref, d, cols, 3 + k)
                  for k, (ref, d) in enumerate(zip((l0_ref, l1_ref, l2_ref), DILATIONS))]
            m = jnp.maximum(jnp.maximum(ls[0], ls[1]), ls[2])
            e = [jnp.exp(l - m) for l in ls]
            den = e[0] + e[1] + e[2]
            y = (e[0] * o[0] + e[1] * o[1] + e[2] * o[2]) / den
            y_ref[:, cols] = y
            yt_ref[cols, :] = y.T.astype(BF16)
            l_ref[:, cols] = m + jnp.log(den)

    blk = pl.BlockSpec((tm, GRP_W), lambda i: (i, 0))
    cls = [pl.BlockSpec((d, tm // d, GRP_W), lambda i: (0, i, 0)) for d in DILATIONS]
    shp = jax.ShapeDtypeStruct((S, GRP_W), F32)
    return pl.pallas_call(
        body, name=name, grid=(S // tm,), in_specs=cls + cls,
        out_specs=[blk, pl.BlockSpec((GRP_W, tm), lambda i: (0, i)), blk],
        out_shape=[shp, jax.ShapeDtypeStruct((GRP_W, S), BF16), shp],
        scratch_shapes=[pltpu.VMEM((6, tm, 128), F32)],
        compiler_params=_cp(("parallel",)),
    )(*os, *lses)


def _attn_bwd(gi, qkv_c, dy_c, y_c, lse_c, name):
    d = DILATIONS[gi]
    nblk = S // d // BLK

    def body(*refs):
        if nblk > 1:
            (q_ref, qn_ref, k_ref, kp_ref, v_ref, vp_ref, dy_ref, dyn_ref, y_ref, yn_ref, l_ref, ln_ref,
             dq_ref, dk_ref, dv_ref) = refs
            n = pl.program_id(1)
            has_prev = n != 0
            has_next = n != nblk - 1
        else:
            q_ref, k_ref, v_ref, dy_ref, y_ref, l_ref, dq_ref, dk_ref, dv_ref = refs
        qi = lax.broadcasted_iota(jnp.int32, (NH, BLK, BLK), 1)
        kj = lax.broadcasted_iota(jnp.int32, (NH, BLK, BLK), 2)

        def lse_col(ref):
            return jnp.stack([ref[:, h * DH:h * DH + 1] for h in range(NH)])

        q, k, v = _heads(q_ref), _heads(k_ref), _heads(v_ref)
        dy = _heads(dy_ref)
        dd = jnp.sum(dy * _heads(y_ref), axis=-1, keepdims=True)
        lcol = lse_col(l_ref)
        dyb = dy.astype(BF16)
        p = jnp.exp(jnp.where(kj <= qi, _bdot_nt(q, k) * 0.125, NEG_INF) - lcol)
        ds = (p * (_bdot_nt(dyb, v) - dd)).astype(BF16)
        dq = _bdot(ds, k)
        dk = _bdot_tn(ds, q)
        dv = _bdot_tn(p.astype(BF16), dyb)
        if nblk > 1:
            qn, kpv, vpv = _heads(qn_ref), _heads(kp_ref), _heads(vp_ref)
            dyn = _heads(dyn_ref)
            ddn = jnp.sum(dyn * _heads(yn_ref), axis=-1, keepdims=True)
            lncol = lse_col(ln_ref)
            dynb = dyn.astype(BF16)
            mask_p = jnp.logical_and(kj >= qi, has_prev)
            pp = jnp.exp(jnp.where(mask_p, _bdot_nt(q, kpv) * 0.125, NEG_INF) - lcol)
            dsp = (pp * (_bdot_nt(dyb, vpv) - dd)).astype(BF16)
            dq = dq + _bdot(dsp, kpv)
            mask_n = jnp.logical_and(kj >= qi, has_next)
            pn = jnp.exp(jnp.where(mask_n, _bdot_nt(qn, k) * 0.125, NEG_INF) - lncol)
            dsn = (pn * (_bdot_nt(dynb, v) - ddn)).astype(BF16)
            dk = dk + _bdot_tn(dsn, qn)
            dv = dv + _bdot_tn(pn.astype(BF16), dynb)
        dq = dq * 0.125
        dk = dk * 0.125
        for h in range(NH):
            sl = slice(h * DH, (h + 1) * DH)
            dq_ref[:, sl] = dq[h]
            dk_ref[:, sl] = dk[h]
            dv_ref[:, sl] = dv[h]

    def spec(sec, shift):
        def idx(r, n):
            return (r, jnp.clip(n + shift, 0, nblk - 1), sec)
        return pl.BlockSpec((None, BLK, GRP_W), idx)

    if nblk > 1:
        in_specs = [spec(0, 0), spec(0, 1), spec(1, 0), spec(1, -1), spec(2, 0), spec(2, -1),
                    spec(0, 0), spec(0, 1), spec(0, 0), spec(0, 1), spec(0, 0), spec(0, 1)]
        args = (qkv_c,) * 6 + (dy_c, dy_c, y_c, y_c, lse_c, lse_c)
    else:
        in_specs = [spec(0, 0), spec(1, 0), spec(2, 0), spec(0, 0), spec(0, 0), spec(0, 0)]
        args = (qkv_c, qkv_c, qkv_c, dy_c, y_c, lse_c)
    out = spec(0, 0)
    shp = jax.ShapeDtypeStruct((d, S // d, GRP_W), F32)
    return pl.pallas_call(
        body, name=name, grid=(d, nblk), in_specs=in_specs, out_specs=[out, out, out], out_shape=[shp, shp, shp],
        compiler_params=_cp(("parallel", "parallel")),
    )(*args)


_SQRT_HALF = 0.7071067811865476
_INV_SQRT_2PI = 0.3989422804014327


def _gelu(z):
    return 0.5 * z * (1.0 + lax.erf(z * _SQRT_HALF))


def _gelu_grad(z):
    return 0.5 * (1.0 + lax.erf(z * _SQRT_HALF)) + z * (jnp.exp(-0.5 * z * z) * _INV_SQRT_2PI)


def _tril_mask():
    t = lax.broadcasted_iota(jnp.int32, (BLK, BLK), 0)
    s = lax.broadcasted_iota(jnp.int32, (BLK, BLK), 1)
    return s <= t


def _gmlp_fwd(z, ln_g, ln_b, w_s, b_s_t, name):
    def body(z_ref, g_ref, b_ref, ws_ref, bs_ref, y_ref, yt_ref):
        zg = _gelu(z_ref[...])
        u = zg[:, :GW]
        xh, _ = _ln_stats(zg[:, GW:])
        vn = (xh * g_ref[...] + b_ref[...]).astype(BF16)
        tril = _tril_mask()
        for gg in range(8):
            sl = slice(gg * BLK, (gg + 1) * BLK)
            wt = jnp.where(tril, ws_ref[gg], 0.0).astype(BF16)
            mixed = _dot(wt, vn[:, sl]) + bs_ref[:, gg:gg + 1]
            yv = u[:, sl] * mixed
            y_ref[:, sl] = yv.astype(BF16)
            yt_ref[sl, :] = yv.T.astype(BF16)

    vec = pl.BlockSpec((1, GW), lambda n: (0, 0))
    return pl.pallas_call(
        body, name=name, grid=(NBLK,),
        in_specs=[pl.BlockSpec((BLK, 2 * GW), lambda n: (n, 0)), vec, vec,
                  pl.BlockSpec((8, BLK, BLK), lambda n: (0, 0, 0)), pl.BlockSpec((BLK, 8), lambda n: (0, 0))],
        out_specs=[pl.BlockSpec((BLK, GW), lambda n: (n, 0)), pl.BlockSpec((GW, BLK), lambda n: (0, n))],
        out_shape=[jax.ShapeDtypeStruct((S, GW), BF16), jax.ShapeDtypeStruct((GW, S), BF16)],
        compiler_params=_cp(("parallel",)),
    )(z, ln_g, ln_b, w_s, b_s_t)


def _gmlp_bwd(z, dy, ln_g, ln_b, w_s, b_s_t, name):
    def body(z_ref, dy_ref, g_ref, b_ref, ws_ref, bs_ref, dz_ref, dws_ref, dbs_ref, dg_ref, db_ref, dvn_ref):
        n = pl.program_id(0)
        zv = z_ref[...]
        zg = _gelu(zv)
        u = zg[:, :GW]
        xh, rstd = _ln_stats(zg[:, GW:])
        vn = (xh * g_ref[...] + b_ref[...]).astype(BF16)
        tril = _tril_mask()

        @pl.when(n == 0)
        def _():
            dws_ref[...] = jnp.zeros_like(dws_ref)
            dbs_ref[...] = jnp.zeros_like(dbs_ref)
            dg_ref[...] = jnp.zeros_like(dg_ref)
            db_ref[...] = jnp.zeros_like(db_ref)

        for gg in range(8):
            sl = slice(gg * BLK, (gg + 1) * BLK)
            wt = jnp.where(tril, ws_ref[gg], 0.0).astype(BF16)
            dyg = dy_ref[:, sl]
            mixed = _dot(wt, vn[:, sl]) + bs_ref[:, gg:gg + 1]
            dz_ref[:, sl] = (dyg * mixed * _gelu_grad(zv[:, sl])).astype(BF16)
            dmix = dyg * u[:, sl]
            dmb = dmix.astype(BF16)
            dws_ref[gg] += jnp.where(tril, _dot_nt(dmb, vn[:, sl]), 0.0)
            dbs_ref[:, gg:gg + 1] += jnp.sum(dmix, axis=-1, keepdims=True)
            dvn_ref[:, sl] = _dot_tn(wt, dmb)

        dvn = dvn_ref[...]
        dg_ref[...] += jnp.sum(dvn * xh, axis=0, keepdims=True)
        db_ref[...] += jnp.sum(dvn, axis=0, keepdims=True)
        dvg = _ln_dx(dvn * g_ref[...], xh, rstd)
        dz_ref[:, GW:] = (dvg * _gelu_grad(zv[:, GW:])).astype(BF16)

    vec = pl.BlockSpec((1, GW), lambda n: (0, 0))
    ws = pl.BlockSpec((8, BLK, BLK), lambda n: (0, 0, 0))
    bs = pl.BlockSpec((BLK, 8), lambda n: (0, 0))
    return pl.pallas_call(
        body, name=name, grid=(NBLK,),
        in_specs=[pl.BlockSpec((BLK, 2 * GW), lambda n: (n, 0)), pl.BlockSpec((BLK, GW), lambda n: (n, 0)),
                  vec, vec, ws, bs],
        out_specs=[pl.BlockSpec((BLK, 2 * GW), lambda n: (n, 0)), ws, bs, vec, vec],
        out_shape=[jax.ShapeDtypeStruct((S, 2 * GW), BF16), jax.ShapeDtypeStruct((8, BLK, BLK), F32),
                   jax.ShapeDtypeStruct((BLK, 8), F32), jax.ShapeDtypeStruct((1, GW), F32),
                   jax.ShapeDtypeStruct((1, GW), F32)],
        scratch_shapes=[pltpu.VMEM((BLK, GW), F32)],
        compiler_params=_cp(("arbitrary",)),
    )(z, dy, ln_g, ln_b, w_s, b_s_t)


def _merge_fwd(a, b, gl, b_gates, name):
    tm = 256

    def body(a_ref, b_ref, g0_ref, g1_ref, bg_ref, o_ref, ot_ref):
        g0 = jax.nn.sigmoid(g0_ref[...] + bg_ref[:, :D])
        g1 = jax.nn.sigmoid(g1_ref[...] + bg_ref[:, D:])
        mg = g0 * a_ref[...] + g1 * b_ref[...]
        o_ref[...] = mg.astype(BF16)
        ot_ref[...] = mg.T.astype(BF16)

    row = pl.BlockSpec((tm, D), lambda i: (i, 0))
    return pl.pallas_call(
        body, name=name, grid=(S // tm,),
        in_specs=[row, row, row, pl.BlockSpec((tm, D), lambda i: (i, 1)), pl.BlockSpec((1, 2 * D), lambda i: (0, 0))],
        out_specs=[row, pl.BlockSpec((D, tm), lambda i: (0, i))],
        out_shape=[jax.ShapeDtypeStruct((S, D), BF16), jax.ShapeDtypeStruct((D, S), BF16)],
        compiler_params=_cp(("parallel",)),
    )(a, b, gl, gl, b_gates)


def _merge_bwd(dm, a, b, gl, b_gates, name):
    tm = 256

    def body(dm_ref, a_ref, b_ref, g0_ref, g1_ref, bg_ref, da_ref, db_ref, dgl_ref, dbg_ref):
        i = pl.program_id(0)
        dmv = dm_ref[...]
        g0 = jax.nn.sigmoid(g0_ref[...] + bg_ref[:, :D])
        g1 = jax.nn.sigmoid(g1_ref[...] + bg_ref[:, D:])
        da_ref[...] = (dmv * g0).astype(BF16)
        db_ref[...] = (dmv * g1).astype(BF16)
        d0 = dmv * a_ref[...] * g0 * (1.0 - g0)
        d1 = dmv * b_ref[...] * g1 * (1.0 - g1)
        dgl_ref[:, :D] = d0.astype(BF16)
        dgl_ref[:, D:] = d1.astype(BF16)
        s0 = jnp.sum(d0, axis=0, keepdims=True)
        s1 = jnp.sum(d1, axis=0, keepdims=True)

        @pl.when(i == 0)
        def _():
            dbg_ref[:, :D] = s0
            dbg_ref[:, D:] = s1

        @pl.when(i > 0)
        def _():
            dbg_ref[:, :D] += s0
            dbg_ref[:, D:] += s1

    row = pl.BlockSpec((tm, D), lambda i: (i, 0))
    wide = pl.BlockSpec((tm, 2 * D), lambda i: (i, 0))
    bg = pl.BlockSpec((1, 2 * D), lambda i: (0, 0))
    return pl.pallas_call(
        body, name=name, grid=(S // tm,),
        in_specs=[row, row, row, row, pl.BlockSpec((tm, D), lambda i: (i, 1)), bg],
        out_specs=[row, row, wide, bg],
        out_shape=[jax.ShapeDtypeStruct((S, D), BF16), jax.ShapeDtypeStruct((S, D), BF16),
                   jax.ShapeDtypeStruct((S, 2 * D), BF16), jax.ShapeDtypeStruct((1, 2 * D), F32)],
        compiler_params=_cp(("arbitrary",)),
    )(dm, a, b, gl, gl, b_gates)


def _adam_math(w, g, m, v):
    m2 = ADAM_B1 * m + (1.0 - ADAM_B1) * g
    v2 = ADAM_B2 * v + (1.0 - ADAM_B2) * (g * g)
    m_hat = m2 / (1.0 - ADAM_B1 ** ADAM_STEP)
    v_hat = v2 / (1.0 - ADAM_B2 ** ADAM_STEP)
    delta = -ADAM_LR * (m_hat / (jnp.sqrt(v_hat) + ADAM_EPS) + ADAM_WD * w)
    return delta, m2, v2


def _pick_rows(rows, cols, unit=16, budget=MIB):
    best = unit
    for t in range(unit, rows + 1, unit):
        if rows % t == 0 and t * cols * 4 <= budget:
            best = t
    assert rows % best == 0
    return best


def _adamw(w, g, m, v, name):
    r, c = w.shape
    tr = _pick_rows(r, c, unit=8)

    def body(w_ref, g_ref, m_ref, v_ref, go_ref, d_ref, mo_ref, vo_ref):
        gv = g_ref[...]
        delta, m2, v2 = _adam_math(w_ref[...], gv, m_ref[...], v_ref[...])
        go_ref[...] = gv
        d_ref[...] = delta
        mo_ref[...] = m2
        vo_ref[...] = v2

    blk = pl.BlockSpec((tr, c), lambda i: (i, 0))
    shp = jax.ShapeDtypeStruct((r, c), F32)
    return pl.pallas_call(
        body, name=name, grid=(r // tr,), in_specs=[blk] * 4, out_specs=[blk] * 4, out_shape=[shp] * 4,
        compiler_params=_cp(("parallel",)),
    )(w, g, m, v)


def _small_sum_adamw(parts, w, m, v, name):
    tr = 48

    def body(p_ref, w_ref, m_ref, v_ref, g_ref, d_ref, mo_ref, vo_ref):
        gv = p_ref[0]
        for k in range(1, 8):
            gv = gv + p_ref[k]
        delta, m2, v2 = _adam_math(w_ref[...], gv, m_ref[...], v_ref[...])
        g_ref[...] = gv
        d_ref[...] = delta
        mo_ref[...] = m2
        vo_ref[...] = v2

    blk = pl.BlockSpec((tr, D), lambda i: (i, 0))
    shp = jax.ShapeDtypeStruct((SMALL_ROWS, D), F32)
    return pl.pallas_call(
        body, name=name, grid=(SMALL_ROWS // tr,),
        in_specs=[pl.BlockSpec((8, tr, D), lambda i: (0, i, 0)), blk, blk, blk],
        out_specs=[blk] * 4, out_shape=[shp] * 4,
        compiler_params=_cp(("parallel",)),
    )(parts, w, m, v)


ANY = pl.BlockSpec(memory_space=pl.ANY)


def _mesh_pos():
    x, y, c = lax.axis_index("x"), lax.axis_index("y"), lax.axis_index("c")
    chips = [(1 - x, y), (x, 1 - y), (1 - x, 1 - y)]
    return x, y, c, chips


def _place_shard(w, kind, pos, name):
    r, c = w.shape
    tr = _pick_rows(r, c)

    def body(pos_ref, w_ref, o_ref):
        o_ref[...] = w_ref[...].astype(BF16)

    if kind == "stack":
        o_spec = pl.BlockSpec((None, tr, c), lambda i, p: (p[1], i, 0))
        shape = (NSH, r, c)
    else:
        o_spec = pl.BlockSpec((tr, c), lambda i, p: (i, p[1]))
        shape = (r, NSH * c)
    return pl.pallas_call(
        body, name=name,
        grid_spec=pltpu.PrefetchScalarGridSpec(
            num_scalar_prefetch=1, grid=(r // tr,),
            in_specs=[pl.BlockSpec((tr, c), lambda i, p: (i, 0))], out_specs=o_spec),
        out_shape=jax.ShapeDtypeStruct(shape, BF16),
        compiler_params=_cp(("parallel",)),
    )(pos, w)


SEM = pl.BlockSpec(memory_space=pltpu.SEMAPHORE)
SPLIT_COPY = pltpu.CompilerParams(has_side_effects=pltpu.SideEffectType.DATAFLOW_SIDE_EFFECTING)


def _shard_window(ref, kind, j, h, dims):
    r, c = dims
    rows = pl.ds(pl.multiple_of(h * (r // 2), 16), r // 2)
    if kind == "stack":
        return ref.at[j, rows, :]
    return ref.at[rows, pl.ds(pl.multiple_of(j * c, 128), c)]


def _ici_copy(ref, kind, dims, j, c, sems, idx, to):
    win = _shard_window(ref, kind, j, c, dims)
    return pltpu.make_async_remote_copy(src_ref=win, dst_ref=win, send_sem=sems[0].at[idx], recv_sem=sems[1].at[idx],
                                        device_id=to, device_id_type=MESH_T)


def _gather_start(fulls, kinds, dims, after, name):
    n, na = len(fulls), len(after)

    def body(*refs):
        outs = refs[n + na:2 * n + na]
        send_sems, recv_sems, token = refs[2 * n + na:]
        x, y, c, chips = _mesh_pos()
        for a in range(n):
            for k, chip in enumerate(chips):
                _ici_copy(outs[a], kinds[a], dims[a], 2 * x + y, c, (send_sems, recv_sems), 3 * a + k,
                          (chip[0], chip[1], c)).start()
        token[...] = jnp.zeros_like(token)

    res = pl.pallas_call(
        body, name=name, in_specs=[ANY] * (n + na),
        out_specs=[ANY] * n + [SEM, SEM, pl.BlockSpec(memory_space=pltpu.VMEM)],
        out_shape=[pltpu.HBM(f.shape, BF16) for f in fulls]
        + [pltpu.SemaphoreType.DMA((3 * n,)), pltpu.SemaphoreType.DMA((3 * n,)), jax.ShapeDtypeStruct((8, 128), F32)],
        input_output_aliases={i: i for i in range(n)},
        compiler_params=SPLIT_COPY,
    )(*fulls, *after)
    return res[:n], res[n], res[n + 1], res[n + 2]


def _gather_wait(fulls, send_sems, recv_sems, kinds, dims, after, name):
    n, na = len(fulls), len(after)

    def body(*refs):
        ssem, rsem = refs[n], refs[n + 1]
        outs = refs[n + 2 + na:]
        x, y, c, chips = _mesh_pos()
        for a in range(n):
            for k, chip in enumerate(chips):
                to = (chip[0], chip[1], c)
                _ici_copy(outs[a], kinds[a], dims[a], 2 * x + y, c, (ssem, rsem), 3 * a + k, to).wait_send()
                _ici_copy(outs[a], kinds[a], dims[a], 2 * chip[0] + chip[1], c, (ssem, rsem), 3 * a + k, to).wait_recv()

    return pl.pallas_call(
        body, name=name, in_specs=[ANY] * n + [SEM, SEM] + [ANY] * na, out_specs=[ANY] * n,
        out_shape=[pltpu.HBM(f.shape, BF16) for f in fulls],
        input_output_aliases={i: i for i in range(n)},
        compiler_params=SPLIT_COPY,
    )(*fulls, send_sems, recv_sems, *after)


def _gather_forward(fulls, kinds, dims, name):
    n = len(fulls)

    def body(*refs):
        outs = refs[n:2 * n]
        sems = refs[2 * n:]
        x, y, c, chips = _mesh_pos()
        sib = (x, y, 1 - c)
        cps = []
        for a in range(n):
            for k, chip in enumerate(chips):
                cp = _ici_copy(outs[a], kinds[a], dims[a], 2 * chip[0] + chip[1], c, sems, 3 * a + k, sib)
                cp.start()
                cps.append(cp)
        for a in range(n):
            for k, chip in enumerate(chips):
                _ici_copy(outs[a], kinds[a], dims[a], 2 * chip[0] + chip[1], 1 - c, sems, 3 * a + k, sib).wait_recv()
        for cp in cps:
            cp.wait_send()

    return pl.pallas_call(
        body, name=name, in_specs=[ANY] * n, out_specs=[ANY] * n,
        out_shape=[pltpu.HBM(f.shape, BF16) for f in fulls],
        input_output_aliases={i: i for i in range(n)},
        scratch_shapes=[pltpu.SemaphoreType.DMA((3 * n,)), pltpu.SemaphoreType.DMA((3 * n,))],
    )(*fulls)


def _pair_copy(src, land, a, x, y, c, sems):
    return pltpu.make_async_remote_copy(
        src_ref=src.at[1 - c], dst_ref=land, send_sem=sems[0].at[a], recv_sem=sems[1].at[a],
        device_id=(x, y, 1 - c), device_id_type=MESH_T)


def _pair_start(grads, lands, name):
    n = len(grads)

    def body(*refs):
        srcs, dsts = refs[2 * n:3 * n], refs[3 * n:4 * n]
        send_sems, recv_sems, token = refs[4 * n:]
        x, y, c, _ = _mesh_pos()
        for a in range(n):
            _pair_copy(srcs[a], dsts[a], a, x, y, c, (send_sems, recv_sems)).start()
        token[...] = jnp.zeros_like(token)

    res = pl.pallas_call(
        body, name=name, in_specs=[ANY] * (2 * n),
        out_specs=[ANY] * (2 * n) + [SEM, SEM, pl.BlockSpec(memory_space=pltpu.VMEM)],
        out_shape=[pltpu.HBM(g.shape, F32) for g in grads]
        + [pltpu.HBM(l.shape, F32) for l in lands]
        + [pltpu.SemaphoreType.DMA((n,)), pltpu.SemaphoreType.DMA((n,)), jax.ShapeDtypeStruct((8, 128), F32)],
        input_output_aliases={i: i for i in range(2 * n)},
        compiler_params=SPLIT_COPY,
    )(*grads, *lands)
    return res[:n], res[n:2 * n], res[2 * n], res[2 * n + 1], res[2 * n + 2]


def _pair_wait(grads, lands, send_sems, recv_sems, after, name):
    n, na = len(grads), len(after)

    def body(*refs):
        ssem, rsem = refs[2 * n], refs[2 * n + 1]
        outs = refs[2 * n + 2 + na:]
        x, y, c, _ = _mesh_pos()
        for a in range(n):
            cp = _pair_copy(outs[a], outs[n + a], a, x, y, c, (ssem, rsem))
            cp.wait_send()
            cp.wait_recv()

    res = pl.pallas_call(
        body, name=name, in_specs=[ANY] * (2 * n) + [SEM, SEM] + [ANY] * na, out_specs=[ANY] * (2 * n),
        out_shape=[pltpu.HBM(g.shape, F32) for g in grads]
        + [pltpu.HBM(l.shape, F32) for l in lands],
        input_output_aliases={i: i for i in range(2 * n)},
        compiler_params=SPLIT_COPY,
    )(*grads, *lands, send_sems, recv_sems, *after)
    return res[:n], res[n:]


def _pair_sum(g, recv, pos, name):
    _, _, rh, c = g.shape
    tr = _pick_rows(rh, c)

    def body(pos_ref, g_ref, r_ref, o_ref):
        o_ref[...] = (g_ref[...] + r_ref[...]).astype(BF16)

    return pl.pallas_call(
        body, name=name,
        grid_spec=pltpu.PrefetchScalarGridSpec(
            num_scalar_prefetch=1, grid=(3, rh // tr),
            in_specs=[pl.BlockSpec((None, None, tr, c), lambda k, r, p: (p[0], p[2 + k], r, 0)),
                      pl.BlockSpec((None, tr, c), lambda k, r, p: (p[2 + k], r, 0))],
            out_specs=pl.BlockSpec((None, tr, c), lambda k, r, p: (k, r, 0))),
        out_shape=jax.ShapeDtypeStruct((3, rh, c), BF16),
        compiler_params=_cp(("parallel", "parallel")),
    )(pos, g, recv)


def _chip_copy(src, land, a, k, chip, c, sems):
    return pltpu.make_async_remote_copy(
        src_ref=src.at[k], dst_ref=land.at[k], send_sem=sems[0].at[3 * a + k],
        recv_sem=sems[1].at[3 * a + k], device_id=(chip[0], chip[1], c), device_id_type=MESH_T)


def _chip_start(psums, lands, name):
    n = len(psums)

    def body(*refs):
        srcs, dsts = refs[2 * n:3 * n], refs[3 * n:4 * n]
        send_sems, recv_sems, token = refs[4 * n:]
        x, y, c, chips = _mesh_pos()
        for a in range(n):
            for k, chip in enumerate(chips):
                _chip_copy(srcs[a], dsts[a], a, k, chip, c, (send_sems, recv_sems)).start()
        token[...] = jnp.zeros_like(token)

    res = pl.pallas_call(
        body, name=name, in_specs=[ANY] * (2 * n),
        out_specs=[ANY] * (2 * n) + [SEM, SEM, pl.BlockSpec(memory_space=pltpu.VMEM)],
        out_shape=[pltpu.HBM(p.shape, BF16) for p in psums]
        + [pltpu.HBM(l.shape, BF16) for l in lands]
        + [pltpu.SemaphoreType.DMA((3 * n,)), pltpu.SemaphoreType.DMA((3 * n,)), jax.ShapeDtypeStruct((8, 128), F32)],
        input_output_aliases={i: i for i in range(2 * n)},
        compiler_params=SPLIT_COPY,
    )(*psums, *lands)
    return res[:n], res[n:2 * n], res[2 * n], res[2 * n + 1], res[2 * n + 2]


def _chip_wait(psums, lands, send_sems, recv_sems, after, name):
    n, na = len(psums), len(after)

    def body(*refs):
        ssem, rsem = refs[2 * n], refs[2 * n + 1]
        outs = refs[2 * n + 2 + na:]
        srcs, dsts = outs[:n], outs[n:]
        x, y, c, chips = _mesh_pos()
        for a in range(n):
            for k, chip in enumerate(chips):
                cp = _chip_copy(srcs[a], dsts[a], a, k, chip, c, (ssem, rsem))
                cp.wait_send()
                cp.wait_recv()

    res = pl.pallas_call(
        body, name=name, in_specs=[ANY] * (2 * n) + [SEM, SEM] + [ANY] * na, out_specs=[ANY] * (2 * n),
        out_shape=[pltpu.HBM(p.shape, BF16) for p in psums]
        + [pltpu.HBM(l.shape, BF16) for l in lands],
        input_output_aliases={i: i for i in range(2 * n)},
        compiler_params=SPLIT_COPY,
    )(*psums, *lands, send_sems, recv_sems, *after)
    return res[n:]


def _owner_sum(g, recv_a, recv_b, pos, name):
    _, _, rh, c = g.shape
    tr = _pick_rows(rh, c)

    def body(pos_ref, g_ref, ra_ref, rb_ref, o_ref):
        acc = g_ref[...] + ra_ref[...]
        for k in range(3):
            acc = acc + rb_ref[k].astype(F32)
        o_ref[...] = acc

    return pl.pallas_call(
        body, name=name,
        grid_spec=pltpu.PrefetchScalarGridSpec(
            num_scalar_prefetch=1, grid=(rh // tr,),
            in_specs=[pl.BlockSpec((None, None, tr, c), lambda r, p: (p[0], p[1], r, 0)),
                      pl.BlockSpec((None, tr, c), lambda r, p: (p[1], r, 0)),
                      pl.BlockSpec((3, tr, c), lambda r, p: (0, r, 0))],
            out_specs=pl.BlockSpec((None, tr, c), lambda r, p: (p[0], r, 0))),
        out_shape=jax.ShapeDtypeStruct((2, rh, c), F32),
        compiler_params=_cp(("parallel",)),
    )(pos, g, recv_a, recv_b)


def _sibling_allgather(halves, name):
    n = len(halves)

    def body(*refs):
        outs = refs[n:2 * n]
        send_sems, recv_sems = refs[2 * n:]
        x, y, c, _ = _mesh_pos()
        cps = []
        for a in range(n):
            cp = pltpu.make_async_remote_copy(
                src_ref=outs[a].at[c], dst_ref=outs[a].at[c], send_sem=send_sems.at[a], recv_sem=recv_sems.at[a],
                device_id=(x, y, 1 - c), device_id_type=MESH_T)
            cp.start()
            cps.append(cp)
        for a in range(n):
            cps[a].wait_send()
            pltpu.make_async_remote_copy(
                src_ref=outs[a].at[1 - c], dst_ref=outs[a].at[1 - c], send_sem=send_sems.at[a],
                recv_sem=recv_sems.at[a], device_id=(x, y, 1 - c), device_id_type=MESH_T).wait_recv()

    return pl.pallas_call(
        body, name=name, in_specs=[ANY] * n, out_specs=[ANY] * n,
        out_shape=[pltpu.HBM(h.shape, F32) for h in halves],
        input_output_aliases={i: i for i in range(n)},
        scratch_shapes=[pltpu.SemaphoreType.DMA((n,)), pltpu.SemaphoreType.DMA((n,))],
    )(*halves)


def _small_allgather(part, after):
    m_per = SMALL_ROWS
    na = len(after)

    def body(x_ref, *refs):
        out_ref, send_sems, recv_sems, local_sem = refs[na:]
        x, y, c, chips = _mesh_pos()
        me, sibling = (x, y, c), (x, y, 1 - c)

        def rows(px, py, pc):
            return out_ref.at[pl.ds((4 * px + 2 * py + pc) * m_per, m_per), :]

        def copy(k, block, to, src=None):
            return pltpu.make_async_remote_copy(
                src_ref=rows(*block) if src is None else src, dst_ref=rows(*block),
                send_sem=send_sems.at[k], recv_sem=recv_sems.at[k], device_id=to, device_id_type=MESH_T)

        mine = pltpu.make_async_copy(x_ref, rows(*me), local_sem)
        mine.start()
        first = [copy(0, me, sibling, src=x_ref)]
        first += [copy(1 + j, me, (*chip, c), src=x_ref) for j, chip in enumerate(chips)]
        for cp in first:
            cp.start()
        passed = [copy(4 + j, (*chip, c), sibling) for j, chip in enumerate(chips)]
        for j, chip in enumerate(chips):
            copy(1 + j, (*chip, c), me).wait_recv()
            passed[j].start()
        copy(0, sibling, me).wait_recv()
        for j, chip in enumerate(chips):
            copy(4 + j, (*chip, 1 - c), me).wait_recv()
        for cp in first + passed:
            cp.wait_send()
        mine.wait()

    return pl.pallas_call(
        body, name="small_allgather",
        out_shape=jax.ShapeDtypeStruct((8 * m_per, D), F32),
        in_specs=[pl.BlockSpec(memory_space=pltpu.VMEM)] + [ANY] * na, out_specs=pl.BlockSpec(memory_space=pltpu.VMEM),
        scratch_shapes=[pltpu.SemaphoreType.DMA((7,)), pltpu.SemaphoreType.DMA((7,)), pltpu.SemaphoreType.DMA],
    )(part, *after)


def _pack_small(ln1_g, ln1_b, gln_g, gln_b, ln2_g, ln2_b, ln3_g, ln3_b, b_gates, b_s, w_s):
    rows = [ln1_g, ln1_b, gln_g, gln_b, ln2_g, ln2_b, ln3_g, ln3_b]
    rows = [r.reshape(1, D) for r in rows] + [b_gates.reshape(2, D), b_s.reshape(1, D), jnp.zeros((5, D), F32),
                                             w_s.reshape(128, D)]
    return jnp.concatenate(rows, axis=0)


def _unpack_small(p):
    out = [p[i:i + 1] for i in range(8)]
    return out + [p[8:10].reshape(1, 2 * D), p[10:11].reshape(1, 8, BLK), p[16:144].reshape(1, 8, BLK, BLK)]


GROUPS = (("f1g", "f1u", "f1d"), ("w_in",), ("w_ab", "w_gb", "w_out"), ("f2g", "f2u", "f2d"))


def _local_step(x, pos_f, target, P, weights_of, grads_ready, flush):
    invf = ROPE_THETA ** (-jnp.arange(0, DH, 2, dtype=F32) / DH)
    invf = jnp.tile(invf, 4).reshape(1, 128)
    b_s_t = P["gmlp_b_s"].T

    W = dict(weights_of(0, []))
    h1, h1b, xh1, rstd1, a1, b1, h1t = _ffn_fwd(x, W["f1g"], W["f1u"], W["f1d"], P["ln1_g"], P["ln1_b"], "ffn1_fwd",
                                                emit_t=True)
    W.update(weights_of(1, [h1b]))
    qkv = _matmul(h1b, W["w_in"], "nn", "proj_qkv", n=3 * ATT_W, b_col0=0, tm=1024, tn=ATT_W)
    z = _matmul(h1b, W["w_in"], "nn", "proj_z", n=2 * GW, b_col0=3 * ATT_W, tm=S, tn=512)
    gl = _matmul(h1b, W["w_in"], "nn", "proj_gates", n=2 * D, b_col0=3 * ATT_W + 2 * GW, tm=S, tn=512)
    qkv_c = _rope_fwd(qkv, pos_f, invf, "rope_fwd")
    og = [_attn_fwd(gi, qkv_c[gi], "attn_fwd_g%d" % gi) for gi in range(NG)]
    y_attn, y_attn_t, lse = _attn_combine([o for o, _ in og], [l for _, l in og], "attn_combine")
    y_gmlp, y_gmlp_t = _gmlp_fwd(z, P["gmlp_ln_g"], P["gmlp_ln_b"], P["gmlp_w_s"], b_s_t, "gmlp_fwd")
    W.update(weights_of(2, [y_gmlp]))
    br_a = _matmul(y_attn, W["w_ab"], "nn", "branch_attn", n=D, tm=1024, tn=D)
    br_b = _matmul(y_gmlp, W["w_gb"], "nn", "branch_gmlp", n=D, tm=1024, tn=D)
    merged, merged_t = _merge_fwd(br_a, br_b, gl, P["b_gates"], "merge_fwd")
    mix = _matmul(merged, W["w_out"], "nn", "mix_out", n=D, tm=1024, tn=D)
    h2, h2b, xh2, rstd2 = _resid_ln(h1, mix, P["ln2_g"], P["ln2_b"], "resid_ln2")
    W.update(weights_of(3, [h2b]))
    y, _, xh3, rstd3, a2, b2 = _ffn_fwd(h2, W["f2g"], W["f2u"], W["f2d"], P["ln3_g"], P["ln3_b"], "ffn2_fwd")

    dr3, dg3, db3, loss = _ln_bwd(y, xh3, rstd3, P["ln3_g"], "loss_ln3_bwd", target=target)
    g_f2g, g_f2u, g_f2d, dh2 = _ffn_bwd(dr3, h2b, a2, b2, W["f2g"], W["f2u"], W["f2d"], "ffn2_bwd")
    tok = grads_ready(3, dict(f2g=g_f2g, f2u=g_f2u, f2d=g_f2d))
    dr2, dg2, db2 = _ln_bwd(dh2, xh2, rstd2, P["ln2_g"], "ln2_bwd", after=tok)
    g_wout = _wgrad(merged_t, dr2, 128, D, "dw_out", row_sharded=True)
    dmerged = _matmul(dr2, W["w_out"], "nt", "dmerged", n=D, tm=1024, tn=D)
    dab, dbb, dglb, dbg = _merge_bwd(dmerged, br_a, br_b, gl, P["b_gates"], "merge_bwd")
    tok = flush([dab])
    g_wab = _wgrad(y_attn_t, dab, GRP_W // 2, 256, "dw_attn_branch", row_sharded=False, after=tok)
    g_wgb = _wgrad(y_gmlp_t, dbb, 128, D, "dw_gmlp_branch", row_sharded=True)
    tok = grads_ready(2, dict(w_ab=g_wab, w_gb=g_wgb, w_out=g_wout))
    dy_attn = _matmul(dab, W["w_ab"], "nt", "dy_attn", n=GRP_W, tm=1024, tn=GRP_W, after=tok)
    dy_gmlp = _matmul(dbb, W["w_gb"], "nt", "dy_gmlp", n=GW, tm=1024, tn=GW)
    dzb, dws, dbs_t, dgln_g, dgln_b = _gmlp_bwd(z, dy_gmlp, P["gmlp_ln_g"], P["gmlp_ln_b"], P["gmlp_w_s"], b_s_t,
                                                 "gmlp_bwd")
    cls = _class_order([dy_attn, y_attn, lse], "attn_class_order")
    dqkv_c = []
    for gi in range(NG):
        dy_c, y_c, lse_c = [t[None] if gi == 0 else cls[2 * a + gi - 1] for a, t in enumerate((dy_attn, y_attn, lse))]
        dqkv_c.append(_attn_bwd(gi, qkv_c[gi], dy_c, y_c, lse_c, "attn_bwd_g%d" % gi))
    dqkvb = _rope_bwd(dqkv_c, pos_f, invf, "rope_bwd")
    dproj = jnp.concatenate([dqkvb, dzb, dglb], axis=1)
    tok = flush([dproj])
    g_win = _wgrad(h1t, dproj, D // 2, IN_SH, "dw_in", row_sharded=False, after=tok)
    tok = grads_ready(1, dict(w_in=g_win))
    dh1 = _matmul(dproj, W["w_in"], "nt", "dh1", n=D, tn=D, tk=IN_SH, add=dr2, add_scale=ALPHA, after=tok)
    dr1, dg1, db1 = _ln_bwd(dh1, xh1, rstd1, P["ln1_g"], "ln1_bwd")
    tok = flush([dr1])
    g_f1g, g_f1u, g_f1d, dx = _ffn_bwd(dr1, x.astype(BF16), a1, b1, W["f1g"], W["f1u"], W["f1d"], "ffn1_bwd",
                                       after=tok)
    grads_ready(0, dict(f1g=g_f1g, f1u=g_f1u, f1d=g_f1d))
    flush([dx])

    small = _pack_small(dg1, db1, dgln_g, dgln_b, dg2, db2, dg3, db3, dbg, dbs_t.T, dws)
    return loss, dx, small


BIG = ("f1g", "f1u", "f1d", "w_in", "w_ab", "w_gb", "w_out", "f2g", "f2u", "f2d")
TRANSPOSED = ("f1g", "f1u", "f2g", "f2u")
KIND = dict(f1g="stack", f1u="stack", f1d="stack", w_in="col", w_ab="col", w_gb="stack", w_out="stack",
            f2g="stack", f2u="stack", f2d="stack")


def kernel(x, positions, ffn1_w_gate, ffn1_w_up, ffn1_w_down, ln1_g, ln1_b, w_in, b_gates, gmlp_ln_g, gmlp_ln_b, gmlp_w_s, gmlp_b_s, w_attn_branch, w_gmlp_branch, w_out, ln2_g, ln2_b, ffn2_w_gate, ffn2_w_up, ffn2_w_down, ln3_g, ln3_b, loss_target, m_ffn1_w_gate, m_ffn1_w_up, m_ffn1_w_down, m_ln1_g, m_ln1_b, m_w_in, m_b_gates, m_gmlp_ln_g, m_gmlp_ln_b, m_gmlp_w_s, m_gmlp_b_s, m_w_attn_branch, m_w_gmlp_branch, m_w_out, m_ln2_g, m_ln2_b, m_ffn2_w_gate, m_ffn2_w_up, m_ffn2_w_down, m_ln3_g, m_ln3_b, v_ffn1_w_gate, v_ffn1_w_up, v_ffn1_w_down, v_ln1_g, v_ln1_b, v_w_in, v_b_gates, v_gmlp_ln_g, v_gmlp_ln_b, v_gmlp_w_s, v_gmlp_b_s, v_w_attn_branch, v_w_gmlp_branch, v_w_out, v_ln2_g, v_ln2_b, v_ffn2_w_gate, v_ffn2_w_up, v_ffn2_w_down, v_ln3_g, v_ln3_b):
    cx, cy, cc = lax.axis_index("x"), lax.axis_index("y"), lax.axis_index("c")
    pos = jnp.stack([cc, 2 * cx + cy, 2 * (1 - cx) + cy, 2 * cx + 1 - cy, 2 * (1 - cx) + 1 - cy]).astype(jnp.int32)

    w_sh = dict(f1g=ffn1_w_gate, f1u=ffn1_w_up, f1d=ffn1_w_down, w_in=w_in, w_ab=w_attn_branch,
                w_gb=w_gmlp_branch, w_out=w_out, f2g=ffn2_w_gate, f2u=ffn2_w_up, f2d=ffn2_w_down)
    m_sh = dict(f1g=m_ffn1_w_gate, f1u=m_ffn1_w_up, f1d=m_ffn1_w_down, w_in=m_w_in, w_ab=m_w_attn_branch,
                w_gb=m_w_gmlp_branch, w_out=m_w_out, f2g=m_ffn2_w_gate, f2u=m_ffn2_w_up, f2d=m_ffn2_w_down)
    v_sh = dict(f1g=v_ffn1_w_gate, f1u=v_ffn1_w_up, f1d=v_ffn1_w_down, w_in=v_w_in, w_ab=v_w_attn_branch,
                w_gb=v_w_gmlp_branch, w_out=v_w_out, f2g=v_ffn2_w_gate, f2u=v_ffn2_w_up, f2d=v_ffn2_w_down)
    w_sh = {k: (v[0].T if k in TRANSPOSED else v[0]) for k, v in w_sh.items()}
    m_sh = {k: (v[0].T if k in TRANSPOSED else v[0]) for k, v in m_sh.items()}
    v_sh = {k: (v[0].T if k in TRANSPOSED else v[0]) for k, v in v_sh.items()}

    started, tokens = [], []
    for gi, names in enumerate(GROUPS):
        placed = [_place_shard(w_sh[k], KIND[k], pos, "place_" + k) for k in names]
        fulls, ssem, rsem, token = _gather_start(placed, [KIND[k] for k in names], [w_sh[k].shape for k in names],
                                                 tokens[-1:], "gather_start_g%d" % gi)
        started.append((fulls, ssem, rsem))
        tokens.append(token)

    def weights_of(gi, after):
        names = GROUPS[gi]
        kinds, dims = [KIND[k] for k in names], [w_sh[k].shape for k in names]
        fulls, ssem, rsem = started[gi]
        fulls = _gather_wait(fulls, ssem, rsem, kinds, dims, list(after) + (tokens if gi == 0 else []),
                             "gather_wait_g%d" % gi)
        fulls = _gather_forward(fulls, kinds, dims, "gather_forward_g%d" % gi)
        return {k: (f.reshape(D, D) if k in ("w_gb", "w_out") else f) for k, f in zip(names, fulls)}

    pending, inflight = [], {}

    def grads_ready(gi, gd):
        grads = [gd[k] for k in GROUPS[gi]]
        lands = [lax.empty(g.shape[1:], F32) for g in grads]
        grads, lands, ssem, rsem, token = _pair_start(grads, lands, "rs_pair_start_g%d" % gi)
        pending.append((gi, grads, lands, ssem, rsem))
        return [token]

    def flush(after):
        gi, grads, lands, ssem, rsem = pending.pop()
        names = GROUPS[gi]
        grads, recv_a = _pair_wait(grads, lands, ssem, rsem, after, "rs_pair_wait_g%d" % gi)
        psums = [_pair_sum(g, r, pos, "rs_pair_sum_" + k) for g, r, k in zip(grads, recv_a, names)]
        lands = [lax.empty((3,) + p.shape[1:], BF16) for p in psums]
        psums, lands, ssem, rsem, token = _chip_start(psums, lands, "rs_chip_start_g%d" % gi)
        inflight[gi] = (grads, recv_a, psums, lands, ssem, rsem, token)
        return [token]

    P = dict(ln1_g=ln1_g, ln1_b=ln1_b, ln2_g=ln2_g, ln2_b=ln2_b, ln3_g=ln3_g, ln3_b=ln3_b, b_gates=b_gates,
             gmlp_ln_g=gmlp_ln_g, gmlp_ln_b=gmlp_ln_b, gmlp_w_s=gmlp_w_s[0], gmlp_b_s=gmlp_b_s[0])
    pos_f = positions.reshape(S, 1).astype(F32)
    loss_part, dx, small = _local_step(x[0], pos_f, loss_target[0], P, weights_of, grads_ready, flush)
    loss = lax.psum(loss_part[0, 0], ("x", "y", "c"))

    g_out, d_out, m_out, v_out = {}, {}, {}, {}

    def finish(gi, after):
        grads, recv_a, psums, lands, ssem, rsem, token = inflight[gi]
        recv_b = _chip_wait(psums, lands, ssem, rsem, after + [inflight[0][6]], "rs_chip_wait_g%d" % gi)
        halves = [_owner_sum(g, ra, rb, pos, "rs_owner_sum_" + k)
                  for g, ra, rb, k in zip(grads, recv_a, recv_b, GROUPS[gi])]
        reduced = _sibling_allgather(halves, "rs_sibling_allgather_g%d" % gi)
        for k, gfull in zip(GROUPS[gi], reduced):
            res = _adamw(w_sh[k], gfull.reshape(w_sh[k].shape), m_sh[k], v_sh[k], "adamw_" + k)
            after = [res[1]]
            if k in TRANSPOSED:
                res = [r.T for r in res]
            g_out[k], d_out[k], m_out[k], v_out[k] = [r[None] for r in res]
        return after

    after = []
    for gi in (3, 2, 1):
        after = finish(gi, after)

    parts = _small_allgather(small, after).reshape(8, SMALL_ROWS, D)
    sp = (ln1_g, ln1_b, gmlp_ln_g, gmlp_ln_b, ln2_g, ln2_b, ln3_g, ln3_b, b_gates, gmlp_b_s, gmlp_w_s)
    sm = (m_ln1_g, m_ln1_b, m_gmlp_ln_g, m_gmlp_ln_b, m_ln2_g, m_ln2_b, m_ln3_g, m_ln3_b, m_b_gates, m_gmlp_b_s,
          m_gmlp_w_s)
    sv = (v_ln1_g, v_ln1_b, v_gmlp_ln_g, v_gmlp_ln_b, v_ln2_g, v_ln2_b, v_ln3_g, v_ln3_b, v_b_gates, v_gmlp_b_s,
          v_gmlp_w_s)
    sg, sd, smn, svn = _small_sum_adamw(parts, _pack_small(*sp), _pack_small(*sm), _pack_small(*sv), "small_adamw")
    names = ("ln1_g", "ln1_b", "gmlp_ln_g", "gmlp_ln_b", "ln2_g", "ln2_b", "ln3_g", "ln3_b", "b_gates", "gmlp_b_s",
             "gmlp_w_s")
    for dst, packed in ((g_out, sg), (d_out, sd), (m_out, smn), (v_out, svn)):
        for nm, val in zip(names, _unpack_small(packed)):
            dst[nm] = val
    finish(0, [sg])

    order = ("f1g", "f1u", "f1d", "ln1_g", "ln1_b", "w_in", "b_gates", "gmlp_ln_g", "gmlp_ln_b", "gmlp_w_s", "gmlp_b_s",
             "w_ab", "w_gb", "w_out", "ln2_g", "ln2_b", "f2g", "f2u", "f2d", "ln3_g", "ln3_b")
    outs = [loss, dx[None]]
    for dst in (g_out, d_out, m_out, v_out):
        outs += [dst[k] for k in order]
    return tuple(outs)
```

```python
import functools
import math

import jax
import jax.numpy as jnp
from jax import lax
from jax.experimental import pallas as pl
from jax.experimental.pallas import tpu as pltpu

F32 = jnp.float32
BF16 = jnp.bfloat16

S = 2048
D = 1024
NSH = 4
FSH = 704
ATT_W = 1536
GRP_W = 512
NG = 3
NH = 8
DH = 64
BLK = 128
NBLK = S // BLK
GW = 1024
IN_W = 8704
IN_SH = IN_W // NSH
ALPHA = 2.0 ** 0.25
LN_EPS = 1e-5
ROPE_THETA = 10000.0
DILATIONS = (1, 4, 16)
ADAM_LR, ADAM_B1, ADAM_B2, ADAM_EPS, ADAM_WD, ADAM_STEP = 0.001, 0.9, 0.999, 1e-08, 0.01, 10
SMALL_ROWS = 144
MESH_T = pl.DeviceIdType.MESH
MIB = 1024 * 1024
NEG_INF = float("-inf")


def _cp(sem, vmem_mib=48):
    return pltpu.CompilerParams(dimension_semantics=sem, vmem_limit_bytes=vmem_mib * MIB)


def _ln_stats(r):
    mu = jnp.mean(r, axis=-1, keepdims=True)
    xc = r - mu
    var = jnp.mean(xc * xc, axis=-1, keepdims=True)
    rstd = lax.rsqrt(var + LN_EPS)
    return xc * rstd, rstd


def _ln_dx(dxh, xh, rstd):
    m1 = jnp.mean(dxh, axis=-1, keepdims=True)
    m2 = jnp.mean(dxh * xh, axis=-1, keepdims=True)
    return rstd * (dxh - m1 - xh * m2)


def _dot_nt(a, b):
    return lax.dot_general(a, b, (((1,), (1,)), ((), ())), preferred_element_type=F32)


def _dot_tn(a, b):
    return lax.dot_general(a, b, (((0,), (0,)), ((), ())), preferred_element_type=F32)


def _dot(a, b):
    return jnp.dot(a, b, preferred_element_type=F32)


def _ffn_fwd(xin, wgt, wut, wd, ln_g, ln_b, name, emit_t=False):
    tm = 512

    def body(x_ref, wg_ref, wu_ref, wd_ref, g_ref, b_ref, *rest):
        if emit_t:
            h_ref, hb_ref, xh_ref, rstd_ref, a_ref, bb_ref, ht_ref, acc_ref = rest
        else:
            h_ref, hb_ref, xh_ref, rstd_ref, a_ref, bb_ref, acc_ref = rest
        j = pl.program_id(1)
        xb = x_ref[...].astype(BF16)
        a = _dot_nt(xb, wg_ref[...])
        b = _dot_nt(xb, wu_ref[...])
        a_ref[...] = a.astype(BF16)
        bb_ref[...] = b.astype(BF16)
        s = (a * jax.nn.sigmoid(a)) * b
        f = _dot(s.astype(BF16), wd_ref[...])

        @pl.when(j == 0)
        def _():
            acc_ref[...] = f

        @pl.when(j > 0)
        def _():
            acc_ref[...] += f

        @pl.when(j == NSH - 1)
        def _():
            r = ALPHA * x_ref[...] + 0.5 * acc_ref[...]
            xh, rstd = _ln_stats(r)
            h = xh * g_ref[...] + b_ref[...]
            h_ref[...] = h
            hb_ref[...] = h.astype(BF16)
            xh_ref[...] = xh
            rstd_ref[...] = rstd
            if emit_t:
                ht_ref[...] = h.T.astype(BF16)

    row = pl.BlockSpec((tm, D), lambda i, j: (i, 0))
    vec = pl.BlockSpec((1, D), lambda i, j: (0, 0))
    wsp = pl.BlockSpec((None, FSH, D), lambda i, j: (j, 0, 0))
    ab = pl.BlockSpec((None, tm, FSH), lambda i, j: (j, i, 0))
    out_specs = [row, row, row, pl.BlockSpec((tm, 1), lambda i, j: (i, 0)), ab, ab]
    out_shape = [jax.ShapeDtypeStruct((S, D), F32), jax.ShapeDtypeStruct((S, D), BF16),
                 jax.ShapeDtypeStruct((S, D), F32), jax.ShapeDtypeStruct((S, 1), F32),
                 jax.ShapeDtypeStruct((NSH, S, FSH), BF16), jax.ShapeDtypeStruct((NSH, S, FSH), BF16)]
    if emit_t:
        out_specs.append(pl.BlockSpec((D, tm), lambda i, j: (0, i)))
        out_shape.append(jax.ShapeDtypeStruct((D, S), BF16))
    return pl.pallas_call(
        body, name=name, grid=(S // tm, NSH),
        in_specs=[row, wsp, wsp, wsp, vec, vec], out_specs=out_specs, out_shape=out_shape,
        scratch_shapes=[pltpu.VMEM((tm, D), F32)],
        compiler_params=_cp(("parallel", "arbitrary")),
    )(xin, wgt, wut, wd, ln_g, ln_b)


def _ffn_bwd(dr, xin_b, a, b, wgt, wut, wd, name, after=()):
    tm = 512
    ni = S // tm
    hr = FSH // 2

    def body(dr_ref, a_ref, b_ref, wg_ref, wu_ref, wd_ref, x_hbm, *rest):
        dwg_hbm, dwu_hbm, dwd_hbm, dx_hbm, dx_acc, da_all, db_all, s_all, df_all, x_all, res_buf, sems = rest[len(after):]
        j = pl.program_id(0)
        i = pl.program_id(1)
        rows = pl.ds(pl.multiple_of(i * tm, tm), tm)

        @pl.when(jnp.logical_and(j == 0, i == 0))
        def _():
            cp = pltpu.make_async_copy(x_hbm, x_all, sems.at[0])
            cp.start()
            cp.wait()

        drv = dr_ref[...]
        df = (0.5 * drv).astype(BF16)

        @pl.when(j == 0)
        def _():
            df_all[rows, :] = df

        ds = jnp.concatenate([_dot_nt(df, wd_ref[0:384, :]), _dot_nt(df, wd_ref[384:FSH, :])], axis=1)
        av = a_ref[...].astype(F32)
        bv = b_ref[...].astype(F32)
        sig = jax.nn.sigmoid(av)
        sl = av * sig
        da = (ds * bv * (sig * (1.0 + av * (1.0 - sig)))).astype(BF16)
        db = (ds * sl).astype(BF16)
        da_all[rows, :] = da
        db_all[rows, :] = db
        s_all[rows, :] = (sl * bv).astype(BF16)
        dx = _dot(da, wg_ref[...]) + _dot(db, wu_ref[...])

        @pl.when(j == 0)
        def _():
            dx_acc[rows, :] = ALPHA * drv + dx

        @pl.when(j > 0)
        def _():
            dx_acc[rows, :] += dx

        @pl.when(i == ni - 1)
        def _():
            copies = []
            for n, (lhs, rhs, out) in enumerate(((da_all, x_all, dwg_hbm), (db_all, x_all, dwu_hbm),
                                                 (s_all, df_all, dwd_hbm))):
                slot = n % 2
                if n >= 2:
                    for cp in copies[2 * (n - 2): 2 * (n - 2) + 2]:
                        cp.wait()
                res_buf[slot] = _dot_tn(lhs[...], rhs[...])
                for h in range(2):
                    cp = pltpu.make_async_copy(res_buf.at[slot, pl.ds(h * hr, hr), :], out.at[h, j],
                                               sems.at[1 + 2 * slot + h])
                    cp.start()
                    copies.append(cp)
            for cp in copies[2:]:
                cp.wait()

        @pl.when(jnp.logical_and(j == NSH - 1, i == ni - 1))
        def _():
            cp = pltpu.make_async_copy(dx_acc, dx_hbm, sems.at[0])
            cp.start()
            cp.wait()

    row = pl.BlockSpec((tm, D), lambda j, i: (i, 0))
    wsp = pl.BlockSpec((None, FSH, D), lambda j, i: (j, 0, 0))
    ab = pl.BlockSpec((None, tm, FSH), lambda j, i: (j, i, 0))
    dwshape = jax.ShapeDtypeStruct((2, NSH, hr, D), F32)
    return pl.pallas_call(
        body, name=name, grid=(NSH, ni),
        in_specs=[row, ab, ab, wsp, wsp, wsp, ANY] + [ANY] * len(after),
        out_specs=[ANY, ANY, ANY, ANY],
        out_shape=[dwshape, dwshape, dwshape, jax.ShapeDtypeStruct((S, D), F32)],
        scratch_shapes=[pltpu.VMEM((S, D), F32), pltpu.VMEM((S, FSH), BF16), pltpu.VMEM((S, FSH), BF16),
                        pltpu.VMEM((S, FSH), BF16), pltpu.VMEM((S, D), BF16), pltpu.VMEM((S, D), BF16),
                        pltpu.VMEM((2, FSH, D), F32), pltpu.SemaphoreType.DMA((5,))],
        compiler_params=_cp(("arbitrary", "arbitrary"), vmem_mib=58),
    )(dr, a, b, wgt, wut, wd, xin_b, *after)


def _matmul(a, b, mode, name, *, n, tm=512, tn=512, tk=None, b_col0=0, add=None, add_scale=1.0, out_dtype=F32,
            after=()):
    m, ka = a.shape
    tk = ka if tk is None else tk
    nk = ka // tk
    assert m % tm == 0 and n % tn == 0 and ka % tk == 0 and b_col0 % tn == 0
    off = b_col0 // tn
    na = len(after)

    def body(*refs):
        refs = refs[na:]
        if add is None:
            a_ref, b_ref, o_ref = refs[:3]
            add_ref = None
            rest = refs[3:]
        else:
            a_ref, b_ref, add_ref, o_ref = refs[:4]
            rest = refs[4:]
        k = pl.program_id(2)
        av = a_ref[...].astype(BF16)
        bv = b_ref[...].astype(BF16)
        p = _dot(av, bv) if mode == "nn" else _dot_nt(av, bv)

        def finish(acc):
            if add_ref is not None:
                acc = acc + add_scale * add_ref[...]
            o_ref[...] = acc.astype(out_dtype)

        if nk == 1:
            finish(p)
        else:
            acc_ref = rest[0]

            @pl.when(k == 0)
            def _():
                acc_ref[...] = p

            @pl.when(k > 0)
            def _():
                acc_ref[...] += p

            @pl.when(k == nk - 1)
            def _():
                finish(acc_ref[...])

    a_spec = pl.BlockSpec((tm, tk), lambda i, j, k: (i, k))
    if mode == "nn":
        b_spec = pl.BlockSpec((tk, tn), lambda i, j, k: (k, j + off))
    else:
        b_spec = pl.BlockSpec((tn, tk), lambda i, j, k: (j, k))
    o_spec = pl.BlockSpec((tm, tn), lambda i, j, k: (i, j))
    in_specs = [pl.BlockSpec(memory_space=pl.ANY)] * na + [a_spec, b_spec] + ([o_spec] if add is not None else [])
    args = tuple(after) + (a, b) + ((add,) if add is not None else ())
    return pl.pallas_call(
        body, name=name, grid=(m // tm, n // tn, nk),
        in_specs=in_specs, out_specs=o_spec,
        out_shape=jax.ShapeDtypeStruct((m, n), out_dtype),
        scratch_shapes=[pltpu.VMEM((tm, tn), F32)] if nk > 1 else [],
        compiler_params=_cp(("parallel", "parallel", "arbitrary")),
    )(*args)


def _wgrad(xt, y, rh, c, name, row_sharded, after=()):
    na = len(after)
    if row_sharded:
        def body(x_ref, y_ref, *rest):
            o_ref = rest[na]
            res = _dot(x_ref[...], y_ref[...].astype(BF16))
            for j in range(NSH):
                for h in range(2):
                    o_ref[h, j] = res[(2 * j + h) * rh:(2 * j + h + 1) * rh, :]

        grid = (1,)
        in_specs = [pl.BlockSpec((2 * NSH * rh, S), lambda g: (0, 0)), pl.BlockSpec((S, c), lambda g: (0, 0))]
        out_specs = pl.BlockSpec((2, NSH, rh, c), lambda g: (0, 0, 0, 0))
        sem = ("arbitrary",)
    else:
        def body(x_ref, y_ref, *rest):
            rest[na][...] = _dot(x_ref[...], y_ref[...].astype(BF16))

        grid = (2, NSH)
        in_specs = [pl.BlockSpec((rh, S), lambda h, j: (h, 0)), pl.BlockSpec((S, c), lambda h, j: (0, j))]
        out_specs = pl.BlockSpec((None, None, rh, c), lambda h, j: (h, j, 0, 0))
        sem = ("parallel", "parallel")
    return pl.pallas_call(
        body, name=name, grid=grid, in_specs=in_specs + [pl.BlockSpec(memory_space=pl.ANY)] * na, out_specs=out_specs,
        out_shape=jax.ShapeDtypeStruct((2, NSH, rh, c), F32),
        compiler_params=_cp(sem, vmem_mib=56),
    )(xt, y, *after)


def _resid_ln(res, f, ln_g, ln_b, name):
    tm = 256

    def body(res_ref, f_ref, g_ref, b_ref, h_ref, hb_ref, xh_ref, rstd_ref):
        r = ALPHA * res_ref[...] + f_ref[...]
        xh, rstd = _ln_stats(r)
        h = xh * g_ref[...] + b_ref[...]
        h_ref[...] = h
        hb_ref[...] = h.astype(BF16)
        xh_ref[...] = xh
        rstd_ref[...] = rstd

    row = pl.BlockSpec((tm, D), lambda i: (i, 0))
    vec = pl.BlockSpec((1, D), lambda i: (0, 0))
    return pl.pallas_call(
        body, name=name, grid=(S // tm,),
        in_specs=[row, row, vec, vec],
        out_specs=[row, row, row, pl.BlockSpec((tm, 1), lambda i: (i, 0))],
        out_shape=[jax.ShapeDtypeStruct((S, D), F32), jax.ShapeDtypeStruct((S, D), BF16),
                   jax.ShapeDtypeStruct((S, D), F32), jax.ShapeDtypeStruct((S, 1), F32)],
        compiler_params=_cp(("parallel",)),
    )(res, f, ln_g, ln_b)


def _ln_bwd(dout, xh, rstd, ln_g, name, target=None, after=()):
    tm = 256
    with_loss = target is not None
    na = len(after)

    def body(*refs):
        refs = refs[na:]
        if with_loss:
            y_ref, t_ref, xh_ref, rstd_ref, g_ref, dr_ref, dg_ref, db_ref, loss_ref = refs
            err = y_ref[...] - t_ref[...]
            dy = err * (1.0 / D)
        else:
            y_ref, xh_ref, rstd_ref, g_ref, dr_ref, dg_ref, db_ref = refs
            dy = y_ref[...]
        i = pl.program_id(0)
        xh = xh_ref[...]
        dr_ref[...] = _ln_dx(dy * g_ref[...], xh, rstd_ref[...])
        dg = jnp.sum(dy * xh, axis=0, keepdims=True)
        db = jnp.sum(dy, axis=0, keepdims=True)

        @pl.when(i == 0)
        def _():
            dg_ref[...] = dg
            db_ref[...] = db

        @pl.when(i > 0)
        def _():
            dg_ref[...] += dg
            db_ref[...] += db

        if with_loss:
            part = 0.5 * jnp.sum(jnp.mean(err * err, axis=-1, keepdims=True), axis=0, keepdims=True)
            part = jnp.broadcast_to(part, (8, 128))

            @pl.when(i == 0)
            def _():
                loss_ref[...] = part

            @pl.when(i > 0)
            def _():
                loss_ref[...] += part

    row = pl.BlockSpec((tm, D), lambda i: (i, 0))
    vec = pl.BlockSpec((1, D), lambda i: (0, 0))
    col = pl.BlockSpec((tm, 1), lambda i: (i, 0))
    in_specs = [pl.BlockSpec(memory_space=pl.ANY)] * na + [row] + ([row] if with_loss else []) + [row, col, vec]
    out_specs = [row, vec, vec] + ([pl.BlockSpec((8, 128), lambda i: (0, 0))] if with_loss else [])
    out_shape = [jax.ShapeDtypeStruct((S, D), F32), jax.ShapeDtypeStruct((1, D), F32),
                 jax.ShapeDtypeStruct((1, D), F32)] + ([jax.ShapeDtypeStruct((8, 128), F32)] if with_loss else [])
    args = tuple(after) + (dout,) + ((target,) if with_loss else ()) + (xh, rstd, ln_g)
    return pl.pallas_call(
        body, name=name, grid=(S // tm,), in_specs=in_specs, out_specs=out_specs, out_shape=out_shape,
        compiler_params=_cp(("arbitrary",)),
    )(*args)


ROPE_TM = 256


def _rope_tables(pos_ref, invf_ref, sign):
    ang = pos_ref[...] * invf_ref[...]
    lane = lax.broadcasted_iota(jnp.int32, ang.shape, 1)
    first = (lane % DH) < (DH // 2)
    sinv = jnp.sin(ang) * sign
    return first, jnp.cos(ang), jnp.where(first, -sinv, sinv)


def _rotate(x, first, cosf, sinf):
    return x * cosf + jnp.where(first, pltpu.roll(x, 96, 1), pltpu.roll(x, 32, 1)) * sinf


def _rope_fwd(qkv, pos_f, invf, name):
    tm = ROPE_TM

    def body(t_ref, pos_ref, invf_ref, o0_ref, o1_ref, o2_ref, buf_ref):
        first, cosf, sinf = _rope_tables(pos_ref, invf_ref, 1.0)
        o_refs = (o0_ref, o1_ref, o2_ref)
        for sec in range(3):
            for gi, d in enumerate(DILATIONS):
                for ch in range(GRP_W // 128):
                    src = sec * ATT_W + gi * GRP_W + ch * 128
                    dst = slice(sec * GRP_W + ch * 128, sec * GRP_W + (ch + 1) * 128)
                    x = t_ref[:, src:src + 128]
                    if sec < 2:
                        x = _rotate(x, first, cosf, sinf)
                    if d == 1:
                        o_refs[gi][0, :, dst] = x.astype(BF16)
                    else:
                        buf_ref[...] = x
                        for r in range(d):
                            o_refs[gi][r, :, dst] = buf_ref[pl.ds(r, tm // d, stride=d), :].astype(BF16)

    return pl.pallas_call(
        body, name=name, grid=(S // tm,),
        in_specs=[pl.BlockSpec((tm, 3 * ATT_W), lambda i: (i, 0)), pl.BlockSpec((tm, 1), lambda i: (i, 0)),
                  pl.BlockSpec((1, 128), lambda i: (0, 0))],
        out_specs=[pl.BlockSpec((d, tm // d, 3 * GRP_W), lambda i: (0, i, 0)) for d in DILATIONS],
        out_shape=[jax.ShapeDtypeStruct((d, S // d, 3 * GRP_W), BF16) for d in DILATIONS],
        scratch_shapes=[pltpu.VMEM((tm, 128), F32)],
        compiler_params=_cp(("parallel",)),
    )(qkv, pos_f, invf)


def _rope_bwd(dqkv_c, pos_f, invf, name):
    tm = ROPE_TM

    def body(*refs):
        g_refs, (pos_ref, invf_ref, o_ref, buf_ref) = refs[:9], refs[9:]
        first, cosf, sinf = _rope_tables(pos_ref, invf_ref, -1.0)
        for sec in range(3):
            for gi, d in enumerate(DILATIONS):
                g_ref = g_refs[3 * gi + sec]
                for ch in range(GRP_W // 128):
                    cols = slice(ch * 128, (ch + 1) * 128)
                    if d == 1:
                        x = g_ref[0, :, cols]
                    else:
                        for r in range(d):
                            buf_ref[pl.ds(r, tm // d, stride=d), :] = g_ref[r, :, cols]
                        x = buf_ref[...]
                    if sec < 2:
                        x = _rotate(x, first, cosf, sinf)
                    dst = sec * ATT_W + gi * GRP_W + ch * 128
                    o_ref[:, dst:dst + 128] = x.astype(BF16)

    g_specs = [pl.BlockSpec((d, tm // d, GRP_W), lambda i: (0, i, 0)) for d in DILATIONS for _ in range(3)]
    return pl.pallas_call(
        body, name=name, grid=(S // tm,),
        in_specs=g_specs + [pl.BlockSpec((tm, 1), lambda i: (i, 0)), pl.BlockSpec((1, 128), lambda i: (0, 0))],
        out_specs=pl.BlockSpec((tm, 3 * ATT_W), lambda i: (i, 0)),
        out_shape=jax.ShapeDtypeStruct((S, 3 * ATT_W), BF16),
        scratch_shapes=[pltpu.VMEM((tm, 128), F32)],
        compiler_params=_cp(("parallel",)),
    )(*[g for grp in dqkv_c for g in grp], pos_f, invf)


def _class_order(ts, name):
    tm = ROPE_TM
    n = len(ts)

    def body(*refs):
        buf_ref = refs[3 * n]
        for a in range(n):
            for ch in range(GRP_W // 128):
                cols = slice(ch * 128, (ch + 1) * 128)
                buf_ref[...] = refs[a][:, cols]
                for b, d in enumerate(DILATIONS[1:]):
                    for r in range(d):
                        refs[n + 2 * a + b][r, :, cols] = buf_ref[pl.ds(r, tm // d, stride=d), :]

    return pl.pallas_call(
        body, name=name, grid=(S // tm,),
        in_specs=[pl.BlockSpec((tm, GRP_W), lambda i: (i, 0))] * n,
        out_specs=[pl.BlockSpec((d, tm // d, GRP_W), lambda i: (0, i, 0)) for _ in range(n) for d in DILATIONS[1:]],
        out_shape=[jax.ShapeDtypeStruct((d, S // d, GRP_W), F32) for _ in range(n) for d in DILATIONS[1:]],
        scratch_shapes=[pltpu.VMEM((tm, 128), F32)],
        compiler_params=_cp(("parallel",)),
    )(*ts)


def _heads(ref):
    return jnp.stack([ref[:, h * DH:(h + 1) * DH] for h in range(NH)])


def _bdot_nt(a, b):
    return lax.dot_general(a, b, (((2,), (2,)), ((0,), (0,))), preferred_element_type=F32)


def _bdot(a, b):
    return lax.dot_general(a, b, (((2,), (1,)), ((0,), (0,))), preferred_element_type=F32)


def _bdot_tn(a, b):
    return lax.dot_general(a, b, (((1,), (1,)), ((0,), (0,))), preferred_element_type=F32)


def _attn_fwd(gi, qkv_c, name):
    d = DILATIONS[gi]
    nblk = S // d // BLK

    def body(*refs):
        if nblk > 1:
            q_ref, kc_ref, kp_ref, vc_ref, vp_ref, o_ref, lse_ref = refs
            has_prev = pl.program_id(1) != 0
        else:
            q_ref, kc_ref, vc_ref, o_ref, lse_ref = refs
        qi = lax.broadcasted_iota(jnp.int32, (NH, BLK, BLK), 1)
        kj = lax.broadcasted_iota(jnp.int32, (NH, BLK, BLK), 2)
        q = _heads(q_ref)
        sc = jnp.where(kj <= qi, _bdot_nt(q, _heads(kc_ref)) * 0.125, NEG_INF)
        m = jnp.max(sc, axis=-1, keepdims=True)
        if nblk > 1:
            mask_p = jnp.logical_and(kj >= qi, has_prev)
            sp = jnp.where(mask_p, _bdot_nt(q, _heads(kp_ref)) * 0.125, NEG_INF)
            m = jnp.maximum(m, jnp.max(sp, axis=-1, keepdims=True))
        pc = jnp.exp(sc - m)
        l = jnp.sum(pc, axis=-1, keepdims=True)
        o = _bdot(pc.astype(BF16), _heads(vc_ref))
        if nblk > 1:
            pp = jnp.exp(sp - m)
            l = l + jnp.sum(pp, axis=-1, keepdims=True)
            o = o + _bdot(pp.astype(BF16), _heads(vp_ref))
        o = o / l
        lse = m + jnp.log(l)
        for h in range(NH):
            sl = slice(h * DH, (h + 1) * DH)
            o_ref[:, sl] = o[h]
            lse_ref[:, sl] = jnp.broadcast_to(lse[h], (BLK, DH))

    def cur(sec):
        return pl.BlockSpec((None, BLK, GRP_W), lambda r, n: (r, n, sec))

    def prev(sec):
        return pl.BlockSpec((None, BLK, GRP_W), lambda r, n: (r, jnp.maximum(n - 1, 0), sec))

    out = pl.BlockSpec((None, BLK, GRP_W), lambda r, n: (r, n, 0))
    shp = jax.ShapeDtypeStruct((d, S // d, GRP_W), F32)
    if nblk > 1:
        in_specs, args = [cur(0), cur(1), prev(1), cur(2), prev(2)], (qkv_c,) * 5
    else:
        in_specs, args = [cur(0), cur(1), cur(2)], (qkv_c,) * 3
    return pl.pallas_call(
        body, name=name, grid=(d, nblk), in_specs=in_specs, out_specs=[out, out], out_shape=[shp, shp],
        compiler_params=_cp(("parallel", "parallel")),
    )(*args)


def _attn_combine(os, lses, name):
    tm = ROPE_TM

    def body(o0_ref, o1_ref, o2_ref, l0_ref, l1_ref, l2_ref, y_ref, yt_ref, l_ref, buf_ref):
        def token_order(ref, d, cols, slot):
            if d == 1:
                return ref[0, :, cols]
            for r in range(d):
                buf_ref[slot, pl.ds(r, tm // d, stride=d), :] = ref[r, :, cols]
            return buf_ref[slot]

        for ch in range(GRP_W // 128):
            cols = slice(ch * 128, (ch + 1) * 128)
            o = [token_order(ref, d, cols, k) for k, (ref, d) in enumerate(zip((o0_ref, o1_ref, o2_ref), DILATIONS))]
            ls = [token_order(ref, d, cols, 3 + k)
                  for k, (ref, d) in enumerate(zip((l0_ref, l1_ref, l2_ref), DILATIONS))]
            m = jnp.maximum(jnp.maximum(ls[0], ls[1]), ls[2])
            e = [jnp.exp(l - m) for l in ls]
            den = e[0] + e[1] + e[2]
            y = (e[0] * o[0] + e[1] * o[1] + e[2] * o[2]) / den
            y_ref[:, cols] = y
            yt_ref[cols, :] = y.T.astype(BF16)
            l_ref[:, cols] = m + jnp.log(den)

    blk = pl.BlockSpec((tm, GRP_W), lambda i: (i, 0))
    cls = [pl.BlockSpec((d, tm // d, GRP_W), lambda i: (0, i, 0)) for d in DILATIONS]
    shp = jax.ShapeDtypeStruct((S, GRP_W), F32)
    return pl.pallas_call(
        body, name=name, grid=(S // tm,), in_specs=cls + cls,
        out_specs=[blk, pl.BlockSpec((GRP_W, tm), lambda i: (0, i)), blk],
        out_shape=[shp, jax.ShapeDtypeStruct((GRP_W, S), BF16), shp],
        scratch_shapes=[pltpu.VMEM((6, tm, 128), F32)],
        compiler_params=_cp(("parallel",)),
    )(*os, *lses)


def _attn_bwd(gi, qkv_c, dy_c, y_c, lse_c, name):
    d = DILATIONS[gi]
    nblk = S // d // BLK

    def body(*refs):
        if nblk > 1:
            (q_ref, qn_ref, k_ref, kp_ref, v_ref, vp_ref, dy_ref, dyn_ref, y_ref, yn_ref, l_ref, ln_ref,
             dq_ref, dk_ref, dv_ref) = refs
            n = pl.program_id(1)
            has_prev = n != 0
            has_next = n != nblk - 1
        else:
            q_ref, k_ref, v_ref, dy_ref, y_ref, l_ref, dq_ref, dk_ref, dv_ref = refs
        qi = lax.broadcasted_iota(jnp.int32, (NH, BLK, BLK), 1)
        kj = lax.broadcasted_iota(jnp.int32, (NH, BLK, BLK), 2)

        def lse_col(ref):
            return jnp.stack([ref[:, h * DH:h * DH + 1] for h in range(NH)])

        q, k, v = _heads(q_ref), _heads(k_ref), _heads(v_ref)
        dy = _heads(dy_ref)
        dd = jnp.sum(dy * _heads(y_ref), axis=-1, keepdims=True)
        lcol = lse_col(l_ref)
        dyb = dy.astype(BF16)
        p = jnp.exp(jnp.where(kj <= qi, _bdot_nt(q, k) * 0.125, NEG_INF) - lcol)
        ds = (p * (_bdot_nt(dyb, v) - dd)).astype(BF16)
        dq = _bdot(ds, k)
        dk = _bdot_tn(ds, q)
        dv = _bdot_tn(p.astype(BF16), dyb)
        if nblk > 1:
            qn, kpv, vpv = _heads(qn_ref), _heads(kp_ref), _heads(vp_ref)
            dyn = _heads(dyn_ref)
            ddn = jnp.sum(dyn * _heads(yn_ref), axis=-1, keepdims=True)
            lncol = lse_col(ln_ref)
            dynb = dyn.astype(BF16)
            mask_p = jnp.logical_and(kj >= qi, has_prev)
            pp = jnp.exp(jnp.where(mask_p, _bdot_nt(q, kpv) * 0.125, NEG_INF) - lcol)
            dsp = (pp * (_bdot_nt(dyb, vpv) - dd)).astype(BF16)
            dq = dq + _bdot(dsp, kpv)
            mask_n = jnp.logical_and(kj >= qi, has_next)
            pn = jnp.exp(jnp.where(mask_n, _bdot_nt(qn, k) * 0.125, NEG_INF) - lncol)
            dsn = (pn * (_bdot_nt(dynb, v) - ddn)).astype(BF16)
            dk = dk + _bdot_tn(dsn, qn)
            dv = dv + _bdot_tn(pn.astype(BF16), dynb)
        dq = dq * 0.125
        dk = dk * 0.125
        for h in range(NH):
            sl = slice(h * DH, (h + 1) * DH)
            dq_ref[:, sl] = dq[h]
            dk_ref[:, sl] = dk[h]
            dv_ref[:, sl] = dv[h]

    def spec(sec, shift):
        def idx(r, n):
            return (r, jnp.clip(n + shift, 0, nblk - 1), sec)
        return pl.BlockSpec((None, BLK, GRP_W), idx)

    if nblk > 1:
        in_specs = [spec(0, 0), spec(0, 1), spec(1, 0), spec(1, -1), spec(2, 0), spec(2, -1),
                    spec(0, 0), spec(0, 1), spec(0, 0), spec(0, 1), spec(0, 0), spec(0, 1)]
        args = (qkv_c,) * 6 + (dy_c, dy_c, y_c, y_c, lse_c, lse_c)
    else:
        in_specs = [spec(0, 0), spec(1, 0), spec(2, 0), spec(0, 0), spec(0, 0), spec(0, 0)]
        args = (qkv_c, qkv_c, qkv_c, dy_c, y_c, lse_c)
    out = spec(0, 0)
    shp = jax.ShapeDtypeStruct((d, S // d, GRP_W), F32)
    return pl.pallas_call(
        body, name=name, grid=(d, nblk), in_specs=in_specs, out_specs=[out, out, out], out_shape=[shp, shp, shp],
        compiler_params=_cp(("parallel", "parallel")),
    )(*args)


_SQRT_HALF = 0.7071067811865476
_INV_SQRT_2PI = 0.3989422804014327


def _gelu(z):
    return 0.5 * z * (1.0 + lax.erf(z * _SQRT_HALF))


def _gelu_grad(z):
    return 0.5 * (1.0 + lax.erf(z * _SQRT_HALF)) + z * (jnp.exp(-0.5 * z * z) * _INV_SQRT_2PI)


def _tril_mask():
    t = lax.broadcasted_iota(jnp.int32, (BLK, BLK), 0)
    s = lax.broadcasted_iota(jnp.int32, (BLK, BLK), 1)
    return s <= t


def _gmlp_fwd(z, ln_g, ln_b, w_s, b_s_t, name):
    def body(z_ref, g_ref, b_ref, ws_ref, bs_ref, y_ref, yt_ref):
        zg = _gelu(z_ref[...])
        u = zg[:, :GW]
        xh, _ = _ln_stats(zg[:, GW:])
        vn = (xh * g_ref[...] + b_ref[...]).astype(BF16)
        tril = _tril_mask()
        for gg in range(8):
            sl = slice(gg * BLK, (gg + 1) * BLK)
            wt = jnp.where(tril, ws_ref[gg], 0.0).astype(BF16)
            mixed = _dot(wt, vn[:, sl]) + bs_ref[:, gg:gg + 1]
            yv = u[:, sl] * mixed
            y_ref[:, sl] = yv.astype(BF16)
            yt_ref[sl, :] = yv.T.astype(BF16)

    vec = pl.BlockSpec((1, GW), lambda n: (0, 0))
    return pl.pallas_call(
        body, name=name, grid=(NBLK,),
        in_specs=[pl.BlockSpec((BLK, 2 * GW), lambda n: (n, 0)), vec, vec,
                  pl.BlockSpec((8, BLK, BLK), lambda n: (0, 0, 0)), pl.BlockSpec((BLK, 8), lambda n: (0, 0))],
        out_specs=[pl.BlockSpec((BLK, GW), lambda n: (n, 0)), pl.BlockSpec((GW, BLK), lambda n: (0, n))],
        out_shape=[jax.ShapeDtypeStruct((S, GW), BF16), jax.ShapeDtypeStruct((GW, S), BF16)],
        compiler_params=_cp(("parallel",)),
    )(z, ln_g, ln_b, w_s, b_s_t)


def _gmlp_bwd(z, dy, ln_g, ln_b, w_s, b_s_t, name):
    def body(z_ref, dy_ref, g_ref, b_ref, ws_ref, bs_ref, dz_ref, dws_ref, dbs_ref, dg_ref, db_ref, dvn_ref):
        n = pl.program_id(0)
        zv = z_ref[...]
        zg = _gelu(zv)
        u = zg[:, :GW]
        xh, rstd = _ln_stats(zg[:, GW:])
        vn = (xh * g_ref[...] + b_ref[...]).astype(BF16)
        tril = _tril_mask()

        @pl.when(n == 0)
        def _():
            dws_ref[...] = jnp.zeros_like(dws_ref)
            dbs_ref[...] = jnp.zeros_like(dbs_ref)
            dg_ref[...] = jnp.zeros_like(dg_ref)
            db_ref[...] = jnp.zeros_like(db_ref)

        for gg in range(8):
            sl = slice(gg * BLK, (gg + 1) * BLK)
            wt = jnp.where(tril, ws_ref[gg], 0.0).astype(BF16)
            dyg = dy_ref[:, sl]
            mixed = _dot(wt, vn[:, sl]) + bs_ref[:, gg:gg + 1]
            dz_ref[:, sl] = (dyg * mixed * _gelu_grad(zv[:, sl])).astype(BF16)
            dmix = dyg * u[:, sl]
            dmb = dmix.astype(BF16)
            dws_ref[gg] += jnp.where(tril, _dot_nt(dmb, vn[:, sl]), 0.0)
            dbs_ref[:, gg:gg + 1] += jnp.sum(dmix, axis=-1, keepdims=True)
            dvn_ref[:, sl] = _dot_tn(wt, dmb)

        dvn = dvn_ref[...]
        dg_ref[...] += jnp.sum(dvn * xh, axis=0, keepdims=True)
        db_ref[...] += jnp.sum(dvn, axis=0, keepdims=True)
        dvg = _ln_dx(dvn * g_ref[...], xh, rstd)
        dz_ref[:, GW:] = (dvg * _gelu_grad(zv[:, GW:])).astype(BF16)

    vec = pl.BlockSpec((1, GW), lambda n: (0, 0))
    ws = pl.BlockSpec((8, BLK, BLK), lambda n: (0, 0, 0))
    bs = pl.BlockSpec((BLK, 8), lambda n: (0, 0))
    return pl.pallas_call(
        body, name=name, grid=(NBLK,),
        in_specs=[pl.BlockSpec((BLK, 2 * GW), lambda n: (n, 0)), pl.BlockSpec((BLK, GW), lambda n: (n, 0)),
                  vec, vec, ws, bs],
        out_specs=[pl.BlockSpec((BLK, 2 * GW), lambda n: (n, 0)), ws, bs, vec, vec],
        out_shape=[jax.ShapeDtypeStruct((S, 2 * GW), BF16), jax.ShapeDtypeStruct((8, BLK, BLK), F32),
                   jax.ShapeDtypeStruct((BLK, 8), F32), jax.ShapeDtypeStruct((1, GW), F32),
                   jax.ShapeDtypeStruct((1, GW), F32)],
        scratch_shapes=[pltpu.VMEM((BLK, GW), F32)],
        compiler_params=_cp(("arbitrary",)),
    )(z, dy, ln_g, ln_b, w_s, b_s_t)


def _merge_fwd(a, b, gl, b_gates, name):
    tm = 256

    def body(a_ref, b_ref, g0_ref, g1_ref, bg_ref, o_ref, ot_ref):
        g0 = jax.nn.sigmoid(g0_ref[...] + bg_ref[:, :D])
        g1 = jax.nn.sigmoid(g1_ref[...] + bg_ref[:, D:])
        mg = g0 * a_ref[...] + g1 * b_ref[...]
        o_ref[...] = mg.astype(BF16)
        ot_ref[...] = mg.T.astype(BF16)

    row = pl.BlockSpec((tm, D), lambda i: (i, 0))
    return pl.pallas_call(
        body, name=name, grid=(S // tm,),
        in_specs=[row, row, row, pl.BlockSpec((tm, D), lambda i: (i, 1)), pl.BlockSpec((1, 2 * D), lambda i: (0, 0))],
        out_specs=[row, pl.BlockSpec((D, tm), lambda i: (0, i))],
        out_shape=[jax.ShapeDtypeStruct((S, D), BF16), jax.ShapeDtypeStruct((D, S), BF16)],
        compiler_params=_cp(("parallel",)),
    )(a, b, gl, gl, b_gates)


def _merge_bwd(dm, a, b, gl, b_gates, name):
    tm = 256

    def body(dm_ref, a_ref, b_ref, g0_ref, g1_ref, bg_ref, da_ref, db_ref, dgl_ref, dbg_ref):
        i = pl.program_id(0)
        dmv = dm_ref[...]
        g0 = jax.nn.sigmoid(g0_ref[...] + bg_ref[:, :D])
        g1 = jax.nn.sigmoid(g1_ref[...] + bg_ref[:, D:])
        da_ref[...] = (dmv * g0).astype(BF16)
        db_ref[...] = (dmv * g1).astype(BF16)
        d0 = dmv * a_ref[...] * g0 * (1.0 - g0)
        d1 = dmv * b_ref[...] * g1 * (1.0 - g1)
        dgl_ref[:, :D] = d0.astype(BF16)
        dgl_ref[:, D:] = d1.astype(BF16)
        s0 = jnp.sum(d0, axis=0, keepdims=True)
        s1 = jnp.sum(d1, axis=0, keepdims=True)

        @pl.when(i == 0)
        def _():
            dbg_ref[:, :D] = s0
            dbg_ref[:, D:] = s1

        @pl.when(i > 0)
        def _():
            dbg_ref[:, :D] += s0
            dbg_ref[:, D:] += s1

    row = pl.BlockSpec((tm, D), lambda i: (i, 0))
    wide = pl.BlockSpec((tm, 2 * D), lambda i: (i, 0))
    bg = pl.BlockSpec((1, 2 * D), lambda i: (0, 0))
    return pl.pallas_call(
        body, name=name, grid=(S // tm,),
        in_specs=[row, row, row, row, pl.BlockSpec((tm, D), lambda i: (i, 1)), bg],
        out_specs=[row, row, wide, bg],
        out_shape=[jax.ShapeDtypeStruct((S, D), BF16), jax.ShapeDtypeStruct((S, D), BF16),
                   jax.ShapeDtypeStruct((S, 2 * D), BF16), jax.ShapeDtypeStruct((1, 2 * D), F32)],
        compiler_params=_cp(("arbitrary",)),
    )(dm, a, b, gl, gl, b_gates)


def _adam_math(w, g, m, v):
    m2 = ADAM_B1 * m + (1.0 - ADAM_B1) * g
    v2 = ADAM_B2 * v + (1.0 - ADAM_B2) * (g * g)
    m_hat = m2 / (1.0 - ADAM_B1 ** ADAM_STEP)
    v_hat = v2 / (1.0 - ADAM_B2 ** ADAM_STEP)
    delta = -ADAM_LR * (m_hat / (jnp.sqrt(v_hat) + ADAM_EPS) + ADAM_WD * w)
    return delta, m2, v2


def _pick_rows(rows, cols, unit=16, budget=MIB):
    best = unit
    for t in range(unit, rows + 1, unit):
        if rows % t == 0 and t * cols * 4 <= budget:
            best = t
    assert rows % best == 0
    return best


def _adamw(w, g, m, v, name):
    r, c = w.shape
    tr = _pick_rows(r, c, unit=8)

    def body(w_ref, g_ref, m_ref, v_ref, go_ref, d_ref, mo_ref, vo_ref):
        gv = g_ref[...]
        delta, m2, v2 = _adam_math(w_ref[...], gv, m_ref[...], v_ref[...])
        go_ref[...] = gv
        d_ref[...] = delta
        mo_ref[...] = m2
        vo_ref[...] = v2

    blk = pl.BlockSpec((tr, c), lambda i: (i, 0))
    shp = jax.ShapeDtypeStruct((r, c), F32)
    return pl.pallas_call(
        body, name=name, grid=(r // tr,), in_specs=[blk] * 4, out_specs=[blk] * 4, out_shape=[shp] * 4,
        compiler_params=_cp(("parallel",)),
    )(*[pltpu.with_memory_space_constraint(t, pltpu.HBM) for t in (w, g, m, v)])


def _small_sum_adamw(parts, w, m, v, name):
    tr = 48

    def body(p_ref, w_ref, m_ref, v_ref, g_ref, d_ref, mo_ref, vo_ref):
        gv = p_ref[0]
        for k in range(1, 8):
            gv = gv + p_ref[k]
        delta, m2, v2 = _adam_math(w_ref[...], gv, m_ref[...], v_ref[...])
        g_ref[...] = gv
        d_ref[...] = delta
        mo_ref[...] = m2
        vo_ref[...] = v2

    blk = pl.BlockSpec((tr, D), lambda i: (i, 0))
    shp = jax.ShapeDtypeStruct((SMALL_ROWS, D), F32)
    return pl.pallas_call(
        body, name=name, grid=(SMALL_ROWS // tr,),
        in_specs=[pl.BlockSpec((8, tr, D), lambda i: (0, i, 0)), blk, blk, blk],
        out_specs=[blk] * 4, out_shape=[shp] * 4,
        compiler_params=_cp(("parallel",)),
    )(parts, w, m, v)


ANY = pl.BlockSpec(memory_space=pl.ANY)


def _in_hbm(arrays):
    return [pltpu.with_memory_space_constraint(a, pltpu.HBM) for a in arrays]


def _mesh_pos():
    x, y, c = lax.axis_index("x"), lax.axis_index("y"), lax.axis_index("c")
    chips = [(1 - x, y), (x, 1 - y), (1 - x, 1 - y)]
    return x, y, c, chips


def _place_shard(w, kind, pos, name):
    r, c = w.shape
    tr = _pick_rows(r, c)

    def body(pos_ref, w_ref, o_ref):
        o_ref[...] = w_ref[...].astype(BF16)

    if kind == "stack":
        o_spec = pl.BlockSpec((None, tr, c), lambda i, p: (p[1], i, 0))
        shape = (NSH, r, c)
    else:
        o_spec = pl.BlockSpec((tr, c), lambda i, p: (i, p[1]))
        shape = (r, NSH * c)
    return pl.pallas_call(
        body, name=name,
        grid_spec=pltpu.PrefetchScalarGridSpec(
            num_scalar_prefetch=1, grid=(r // tr,),
            in_specs=[pl.BlockSpec((tr, c), lambda i, p: (i, 0))], out_specs=o_spec),
        out_shape=pltpu.HBM(shape, BF16),
        compiler_params=_cp(("parallel",)),
    )(pos, pltpu.with_memory_space_constraint(w, pltpu.HBM))


SEM = pl.BlockSpec(memory_space=pltpu.SEMAPHORE)
SPLIT_COPY = pltpu.CompilerParams(has_side_effects=pltpu.SideEffectType.DATAFLOW_SIDE_EFFECTING)


def _shard_window(ref, kind, j, h, dims):
    r, c = dims
    rows = pl.ds(pl.multiple_of(h * (r // 2), 16), r // 2)
    if kind == "stack":
        return ref.at[j, rows, :]
    return ref.at[rows, pl.ds(pl.multiple_of(j * c, 128), c)]


def _ici_copy(ref, kind, dims, j, c, sems, idx, to):
    win = _shard_window(ref, kind, j, c, dims)
    return pltpu.make_async_remote_copy(src_ref=win, dst_ref=win, send_sem=sems[0].at[idx], recv_sem=sems[1].at[idx],
                                        device_id=to, device_id_type=MESH_T)


def _gather_start(fulls, kinds, dims, after, name):
    n, na = len(fulls), len(after)

    def body(*refs):
        outs = refs[n + na:2 * n + na]
        send_sems, recv_sems, token = refs[2 * n + na:]
        x, y, c, chips = _mesh_pos()
        for a in range(n):
            for k, chip in enumerate(chips):
                _ici_copy(outs[a], kinds[a], dims[a], 2 * x + y, c, (send_sems, recv_sems), 3 * a + k,
                          (chip[0], chip[1], c)).start()
        token[...] = jnp.zeros_like(token)

    res = pl.pallas_call(
        body, name=name, in_specs=[ANY] * (n + na),
        out_specs=[ANY] * n + [SEM, SEM, pl.BlockSpec(memory_space=pltpu.VMEM)],
        out_shape=[pltpu.HBM(f.shape, BF16) for f in fulls]
        + [pltpu.SemaphoreType.DMA((3 * n,)), pltpu.SemaphoreType.DMA((3 * n,)), jax.ShapeDtypeStruct((8, 128), F32)],
        input_output_aliases={i: i for i in range(n)},
        compiler_params=SPLIT_COPY,
    )(*_in_hbm(fulls), *after)
    return res[:n], res[n], res[n + 1], res[n + 2]


def _gather_wait(fulls, send_sems, recv_sems, kinds, dims, after, name):
    n, na = len(fulls), len(after)

    def body(*refs):
        ssem, rsem = refs[n], refs[n + 1]
        outs = refs[n + 2 + na:]
        x, y, c, chips = _mesh_pos()
        for a in range(n):
            for k, chip in enumerate(chips):
                to = (chip[0], chip[1], c)
                _ici_copy(outs[a], kinds[a], dims[a], 2 * x + y, c, (ssem, rsem), 3 * a + k, to).wait_send()
                _ici_copy(outs[a], kinds[a], dims[a], 2 * chip[0] + chip[1], c, (ssem, rsem), 3 * a + k, to).wait_recv()

    return pl.pallas_call(
        body, name=name, in_specs=[ANY] * n + [SEM, SEM] + [ANY] * na, out_specs=[ANY] * n,
        out_shape=[pltpu.HBM(f.shape, BF16) for f in fulls],
        input_output_aliases={i: i for i in range(n)},
        compiler_params=SPLIT_COPY,
    )(*_in_hbm(fulls), send_sems, recv_sems, *after)


def _gather_forward(fulls, kinds, dims, name):
    n = len(fulls)

    def body(*refs):
        outs = refs[n:2 * n]
        sems = refs[2 * n:]
        x, y, c, chips = _mesh_pos()
        sib = (x, y, 1 - c)
        cps = []
        for a in range(n):
            for k, chip in enumerate(chips):
                cp = _ici_copy(outs[a], kinds[a], dims[a], 2 * chip[0] + chip[1], c, sems, 3 * a + k, sib)
                cp.start()
                cps.append(cp)
        for a in range(n):
            for k, chip in enumerate(chips):
                _ici_copy(outs[a], kinds[a], dims[a], 2 * chip[0] + chip[1], 1 - c, sems, 3 * a + k, sib).wait_recv()
        for cp in cps:
            cp.wait_send()

    return pl.pallas_call(
        body, name=name, in_specs=[ANY] * n, out_specs=[ANY] * n,
        out_shape=[pltpu.HBM(f.shape, BF16) for f in fulls],
        input_output_aliases={i: i for i in range(n)},
        scratch_shapes=[pltpu.SemaphoreType.DMA((3 * n,)), pltpu.SemaphoreType.DMA((3 * n,))],
    )(*_in_hbm(fulls))


def _pair_copy(src, land, a, x, y, c, sems):
    return pltpu.make_async_remote_copy(
        src_ref=src.at[1 - c], dst_ref=land, send_sem=sems[0].at[a], recv_sem=sems[1].at[a],
        device_id=(x, y, 1 - c), device_id_type=MESH_T)


def _pair_start(grads, lands, name):
    n = len(grads)

    def body(*refs):
        srcs, dsts = refs[2 * n:3 * n], refs[3 * n:4 * n]
        send_sems, recv_sems, token = refs[4 * n:]
        x, y, c, _ = _mesh_pos()
        for a in range(n):
            _pair_copy(srcs[a], dsts[a], a, x, y, c, (send_sems, recv_sems)).start()
        token[...] = jnp.zeros_like(token)

    res = pl.pallas_call(
        body, name=name, in_specs=[ANY] * (2 * n),
        out_specs=[ANY] * (2 * n) + [SEM, SEM, pl.BlockSpec(memory_space=pltpu.VMEM)],
        out_shape=[pltpu.HBM(g.shape, F32) for g in grads]
        + [pltpu.HBM(l.shape, F32) for l in lands]
        + [pltpu.SemaphoreType.DMA((n,)), pltpu.SemaphoreType.DMA((n,)), jax.ShapeDtypeStruct((8, 128), F32)],
        input_output_aliases={i: i for i in range(2 * n)},
        compiler_params=SPLIT_COPY,
    )(*_in_hbm(grads), *_in_hbm(lands))
    return res[:n], res[n:2 * n], res[2 * n], res[2 * n + 1], res[2 * n + 2]


def _pair_wait(grads, lands, send_sems, recv_sems, after, name):
    n, na = len(grads), len(after)

    def body(*refs):
        ssem, rsem = refs[2 * n], refs[2 * n + 1]
        outs = refs[2 * n + 2 + na:]
        x, y, c, _ = _mesh_pos()
        for a in range(n):
            cp = _pair_copy(outs[a], outs[n + a], a, x, y, c, (ssem, rsem))
            cp.wait_send()
            cp.wait_recv()

    res = pl.pallas_call(
        body, name=name, in_specs=[ANY] * (2 * n) + [SEM, SEM] + [ANY] * na, out_specs=[ANY] * (2 * n),
        out_shape=[pltpu.HBM(g.shape, F32) for g in grads]
        + [pltpu.HBM(l.shape, F32) for l in lands],
        input_output_aliases={i: i for i in range(2 * n)},
        compiler_params=SPLIT_COPY,
    )(*_in_hbm(grads), *_in_hbm(lands), send_sems, recv_sems, *after)
    return res[:n], res[n:]


def _pair_sum(g, recv, pos, name):
    _, _, rh, c = g.shape
    tr = _pick_rows(rh, c)

    def body(pos_ref, g_ref, r_ref, o_ref):
        o_ref[...] = (g_ref[...] + r_ref[...]).astype(BF16)

    return pl.pallas_call(
        body, name=name,
        grid_spec=pltpu.PrefetchScalarGridSpec(
            num_scalar_prefetch=1, grid=(3, rh // tr),
            in_specs=[pl.BlockSpec((None, None, tr, c), lambda k, r, p: (p[0], p[2 + k], r, 0)),
                      pl.BlockSpec((None, tr, c), lambda k, r, p: (p[2 + k], r, 0))],
            out_specs=pl.BlockSpec((None, tr, c), lambda k, r, p: (k, r, 0))),
        out_shape=pltpu.HBM((3, rh, c), BF16),
        compiler_params=_cp(("parallel", "parallel")),
    )(pos, *_in_hbm([g, recv]))


def _chip_copy(src, land, a, k, chip, c, sems):
    return pltpu.make_async_remote_copy(
        src_ref=src.at[k], dst_ref=land.at[k], send_sem=sems[0].at[3 * a + k],
        recv_sem=sems[1].at[3 * a + k], device_id=(chip[0], chip[1], c), device_id_type=MESH_T)


def _chip_start(psums, lands, name):
    n = len(psums)

    def body(*refs):
        srcs, dsts = refs[2 * n:3 * n], refs[3 * n:4 * n]
        send_sems, recv_sems, token = refs[4 * n:]
        x, y, c, chips = _mesh_pos()
        for a in range(n):
            for k, chip in enumerate(chips):
                _chip_copy(srcs[a], dsts[a], a, k, chip, c, (send_sems, recv_sems)).start()
        token[...] = jnp.zeros_like(token)

    res = pl.pallas_call(
        body, name=name, in_specs=[ANY] * (2 * n),
        out_specs=[ANY] * (2 * n) + [SEM, SEM, pl.BlockSpec(memory_space=pltpu.VMEM)],
        out_shape=[pltpu.HBM(p.shape, BF16) for p in psums]
        + [pltpu.HBM(l.shape, BF16) for l in lands]
        + [pltpu.SemaphoreType.DMA((3 * n,)), pltpu.SemaphoreType.DMA((3 * n,)), jax.ShapeDtypeStruct((8, 128), F32)],
        input_output_aliases={i: i for i in range(2 * n)},
        compiler_params=SPLIT_COPY,
    )(*_in_hbm(psums), *_in_hbm(lands))
    return res[:n], res[n:2 * n], res[2 * n], res[2 * n + 1], res[2 * n + 2]


def _chip_wait(psums, lands, send_sems, recv_sems, after, name):
    n, na = len(psums), len(after)

    def body(*refs):
        ssem, rsem = refs[2 * n], refs[2 * n + 1]
        outs = refs[2 * n + 2 + na:]
        srcs, dsts = outs[:n], outs[n:]
        x, y, c, chips = _mesh_pos()
        for a in range(n):
            for k, chip in enumerate(chips):
                cp = _chip_copy(srcs[a], dsts[a], a, k, chip, c, (ssem, rsem))
                cp.wait_send()
                cp.wait_recv()

    res = pl.pallas_call(
        body, name=name, in_specs=[ANY] * (2 * n) + [SEM, SEM] + [ANY] * na, out_specs=[ANY] * (2 * n),
        out_shape=[pltpu.HBM(p.shape, BF16) for p in psums]
        + [pltpu.HBM(l.shape, BF16) for l in lands],
        input_output_aliases={i: i for i in range(2 * n)},
        compiler_params=SPLIT_COPY,
    )(*_in_hbm(psums), *_in_hbm(lands), send_sems, recv_sems, *after)
    return res[n:]


def _owner_sum(g, recv_a, recv_b, pos, name):
    _, _, rh, c = g.shape
    tr = _pick_rows(rh, c)

    def body(pos_ref, g_ref, ra_ref, rb_ref, o_ref):
        acc = g_ref[...] + ra_ref[...]
        for k in range(3):
            acc = acc + rb_ref[k].astype(F32)
        o_ref[...] = acc

    return pl.pallas_call(
        body, name=name,
        grid_spec=pltpu.PrefetchScalarGridSpec(
            num_scalar_prefetch=1, grid=(rh // tr,),
            in_specs=[pl.BlockSpec((None, None, tr, c), lambda r, p: (p[0], p[1], r, 0)),
                      pl.BlockSpec((None, tr, c), lambda r, p: (p[1], r, 0)),
                      pl.BlockSpec((3, tr, c), lambda r, p: (0, r, 0))],
            out_specs=pl.BlockSpec((None, tr, c), lambda r, p: (p[0], r, 0))),
        out_shape=pltpu.HBM((2, rh, c), F32),
        compiler_params=_cp(("parallel",)),
    )(pos, *_in_hbm([g, recv_a, recv_b]))


def _sibling_allgather(halves, name):
    n = len(halves)

    def body(*refs):
        outs = refs[n:2 * n]
        send_sems, recv_sems = refs[2 * n:]
        x, y, c, _ = _mesh_pos()
        cps = []
        for a in range(n):
            cp = pltpu.make_async_remote_copy(
                src_ref=outs[a].at[c], dst_ref=outs[a].at[c], send_sem=send_sems.at[a], recv_sem=recv_sems.at[a],
                device_id=(x, y, 1 - c), device_id_type=MESH_T)
            cp.start()
            cps.append(cp)
        for a in range(n):
            cps[a].wait_send()
            pltpu.make_async_remote_copy(
                src_ref=outs[a].at[1 - c], dst_ref=outs[a].at[1 - c], send_sem=send_sems.at[a],
                recv_sem=recv_sems.at[a], device_id=(x, y, 1 - c), device_id_type=MESH_T).wait_recv()

    return pl.pallas_call(
        body, name=name, in_specs=[ANY] * n, out_specs=[ANY] * n,
        out_shape=[pltpu.HBM(h.shape, F32) for h in halves],
        input_output_aliases={i: i for i in range(n)},
        scratch_shapes=[pltpu.SemaphoreType.DMA((n,)), pltpu.SemaphoreType.DMA((n,))],
    )(*_in_hbm(halves))


def _small_allgather(part, after):
    m_per = SMALL_ROWS
    na = len(after)

    def body(x_ref, *refs):
        out_ref, send_sems, recv_sems, local_sem = refs[na:]
        x, y, c, chips = _mesh_pos()
        me, sibling = (x, y, c), (x, y, 1 - c)

        def rows(px, py, pc):
            return out_ref.at[pl.ds((4 * px + 2 * py + pc) * m_per, m_per), :]

        def copy(k, block, to, src=None):
            return pltpu.make_async_remote_copy(
                src_ref=rows(*block) if src is None else src, dst_ref=rows(*block),
                send_sem=send_sems.at[k], recv_sem=recv_sems.at[k], device_id=to, device_id_type=MESH_T)

        mine = pltpu.make_async_copy(x_ref, rows(*me), local_sem)
        mine.start()
        first = [copy(0, me, sibling, src=x_ref)]
        first += [copy(1 + j, me, (*chip, c), src=x_ref) for j, chip in enumerate(chips)]
        for cp in first:
            cp.start()
        passed = [copy(4 + j, (*chip, c), sibling) for j, chip in enumerate(chips)]
        for j, chip in enumerate(chips):
            copy(1 + j, (*chip, c), me).wait_recv()
            passed[j].start()
        copy(0, sibling, me).wait_recv()
        for j, chip in enumerate(chips):
            copy(4 + j, (*chip, 1 - c), me).wait_recv()
        for cp in first + passed:
            cp.wait_send()
        mine.wait()

    return pl.pallas_call(
        body, name="small_allgather",
        out_shape=jax.ShapeDtypeStruct((8 * m_per, D), F32),
        in_specs=[pl.BlockSpec(memory_space=pltpu.VMEM)] + [ANY] * na, out_specs=pl.BlockSpec(memory_space=pltpu.VMEM),
        scratch_shapes=[pltpu.SemaphoreType.DMA((7,)), pltpu.SemaphoreType.DMA((7,)), pltpu.SemaphoreType.DMA],
    )(part, *after)


def _pack_small(ln1_g, ln1_b, gln_g, gln_b, ln2_g, ln2_b, ln3_g, ln3_b, b_gates, b_s, w_s):
    rows = [ln1_g, ln1_b, gln_g, gln_b, ln2_g, ln2_b, ln3_g, ln3_b]
    rows = [r.reshape(1, D) for r in rows] + [b_gates.reshape(2, D), b_s.reshape(1, D), jnp.zeros((5, D), F32),
                                             w_s.reshape(128, D)]
    return jnp.concatenate(rows, axis=0)


def _unpack_small(p):
    out = [p[i:i + 1] for i in range(8)]
    return out + [p[8:10].reshape(1, 2 * D), p[10:11].reshape(1, 8, BLK), p[16:144].reshape(1, 8, BLK, BLK)]


GROUPS = (("f1g", "f1u", "f1d"), ("w_in",), ("w_ab", "w_gb", "w_out"), ("f2g", "f2u", "f2d"))


def _local_step(x, pos_f, target, P, weights_of, grads_ready, flush):
    invf = ROPE_THETA ** (-jnp.arange(0, DH, 2, dtype=F32) / DH)
    invf = jnp.tile(invf, 4).reshape(1, 128)
    b_s_t = P["gmlp_b_s"].T

    W = dict(weights_of(0, []))
    h1, h1b, xh1, rstd1, a1, b1, h1t = _ffn_fwd(x, W["f1g"], W["f1u"], W["f1d"], P["ln1_g"], P["ln1_b"], "ffn1_fwd",
                                                emit_t=True)
    W.update(weights_of(1, [h1b]))
    qkv = _matmul(h1b, W["w_in"], "nn", "proj_qkv", n=3 * ATT_W, b_col0=0, tm=1024, tn=ATT_W)
    z = _matmul(h1b, W["w_in"], "nn", "proj_z", n=2 * GW, b_col0=3 * ATT_W, tm=S, tn=512)
    gl = _matmul(h1b, W["w_in"], "nn", "proj_gates", n=2 * D, b_col0=3 * ATT_W + 2 * GW, tm=S, tn=512)
    qkv_c = _rope_fwd(qkv, pos_f, invf, "rope_fwd")
    og = [_attn_fwd(gi, qkv_c[gi], "attn_fwd_g%d" % gi) for gi in range(NG)]
    y_attn, y_attn_t, lse = _attn_combine([o for o, _ in og], [l for _, l in og], "attn_combine")
    y_gmlp, y_gmlp_t = _gmlp_fwd(z, P["gmlp_ln_g"], P["gmlp_ln_b"], P["gmlp_w_s"], b_s_t, "gmlp_fwd")
    W.update(weights_of(2, [y_gmlp]))
    br_a = _matmul(y_attn, W["w_ab"], "nn", "branch_attn", n=D, tm=1024, tn=D)
    br_b = _matmul(y_gmlp, W["w_gb"], "nn", "branch_gmlp", n=D, tm=1024, tn=D)
    merged, merged_t = _merge_fwd(br_a, br_b, gl, P["b_gates"], "merge_fwd")
    mix = _matmul(merged, W["w_out"], "nn", "mix_out", n=D, tm=1024, tn=D)
    h2, h2b, xh2, rstd2 = _resid_ln(h1, mix, P["ln2_g"], P["ln2_b"], "resid_ln2")
    W.update(weights_of(3, [h2b]))
    y, _, xh3, rstd3, a2, b2 = _ffn_fwd(h2, W["f2g"], W["f2u"], W["f2d"], P["ln3_g"], P["ln3_b"], "ffn2_fwd")

    dr3, dg3, db3, loss = _ln_bwd(y, xh3, rstd3, P["ln3_g"], "loss_ln3_bwd", target=target)
    g_f2g, g_f2u, g_f2d, dh2 = _ffn_bwd(dr3, h2b, a2, b2, W["f2g"], W["f2u"], W["f2d"], "ffn2_bwd")
    tok = grads_ready(3, dict(f2g=g_f2g, f2u=g_f2u, f2d=g_f2d))
    dr2, dg2, db2 = _ln_bwd(dh2, xh2, rstd2, P["ln2_g"], "ln2_bwd", after=tok)
    g_wout = _wgrad(merged_t, dr2, 128, D, "dw_out", row_sharded=True)
    dmerged = _matmul(dr2, W["w_out"], "nt", "dmerged", n=D, tm=1024, tn=D)
    dab, dbb, dglb, dbg = _merge_bwd(dmerged, br_a, br_b, gl, P["b_gates"], "merge_bwd")
    tok = flush([dab])
    g_wab = _wgrad(y_attn_t, dab, GRP_W // 2, 256, "dw_attn_branch", row_sharded=False, after=tok)
    g_wgb = _wgrad(y_gmlp_t, dbb, 128, D, "dw_gmlp_branch", row_sharded=True)
    tok = grads_ready(2, dict(w_ab=g_wab, w_gb=g_wgb, w_out=g_wout))
    dy_attn = _matmul(dab, W["w_ab"], "nt", "dy_attn", n=GRP_W, tm=1024, tn=GRP_W, after=tok)
    dy_gmlp = _matmul(dbb, W["w_gb"], "nt", "dy_gmlp", n=GW, tm=1024, tn=GW)
    dzb, dws, dbs_t, dgln_g, dgln_b = _gmlp_bwd(z, dy_gmlp, P["gmlp_ln_g"], P["gmlp_ln_b"], P["gmlp_w_s"], b_s_t,
                                                 "gmlp_bwd")
    cls = _class_order([dy_attn, y_attn, lse], "attn_class_order")
    dqkv_c = []
    for gi in range(NG):
        dy_c, y_c, lse_c = [t[None] if gi == 0 else cls[2 * a + gi - 1] for a, t in enumerate((dy_attn, y_attn, lse))]
        dqkv_c.append(_attn_bwd(gi, qkv_c[gi], dy_c, y_c, lse_c, "attn_bwd_g%d" % gi))
    dqkvb = _rope_bwd(dqkv_c, pos_f, invf, "rope_bwd")
    dproj = jnp.concatenate([dqkvb, dzb, dglb], axis=1)
    tok = flush([dproj])
    g_win = _wgrad(h1t, dproj, D // 2, IN_SH, "dw_in", row_sharded=False, after=tok)
    tok = grads_ready(1, dict(w_in=g_win))
    dh1 = _matmul(dproj, W["w_in"], "nt", "dh1", n=D, tn=D, tk=IN_SH, add=dr2, add_scale=ALPHA, after=tok)
    dr1, dg1, db1 = _ln_bwd(dh1, xh1, rstd1, P["ln1_g"], "ln1_bwd")
    tok = flush([dr1])
    g_f1g, g_f1u, g_f1d, dx = _ffn_bwd(dr1, x.astype(BF16), a1, b1, W["f1g"], W["f1u"], W["f1d"], "ffn1_bwd",
                                       after=tok)
    grads_ready(0, dict(f1g=g_f1g, f1u=g_f1u, f1d=g_f1d))
    flush([dx])

    small = _pack_small(dg1, db1, dgln_g, dgln_b, dg2, db2, dg3, db3, dbg, dbs_t.T, dws)
    return loss, dx, small


BIG = ("f1g", "f1u", "f1d", "w_in", "w_ab", "w_gb", "w_out", "f2g", "f2u", "f2d")
TRANSPOSED = ("f1g", "f1u", "f2g", "f2u")
KIND = dict(f1g="stack", f1u="stack", f1d="stack", w_in="col", w_ab="col", w_gb="stack", w_out="stack",
            f2g="stack", f2u="stack", f2d="stack")


def kernel(x, positions, ffn1_w_gate, ffn1_w_up, ffn1_w_down, ln1_g, ln1_b, w_in, b_gates, gmlp_ln_g, gmlp_ln_b, gmlp_w_s, gmlp_b_s, w_attn_branch, w_gmlp_branch, w_out, ln2_g, ln2_b, ffn2_w_gate, ffn2_w_up, ffn2_w_down, ln3_g, ln3_b, loss_target, m_ffn1_w_gate, m_ffn1_w_up, m_ffn1_w_down, m_ln1_g, m_ln1_b, m_w_in, m_b_gates, m_gmlp_ln_g, m_gmlp_ln_b, m_gmlp_w_s, m_gmlp_b_s, m_w_attn_branch, m_w_gmlp_branch, m_w_out, m_ln2_g, m_ln2_b, m_ffn2_w_gate, m_ffn2_w_up, m_ffn2_w_down, m_ln3_g, m_ln3_b, v_ffn1_w_gate, v_ffn1_w_up, v_ffn1_w_down, v_ln1_g, v_ln1_b, v_w_in, v_b_gates, v_gmlp_ln_g, v_gmlp_ln_b, v_gmlp_w_s, v_gmlp_b_s, v_w_attn_branch, v_w_gmlp_branch, v_w_out, v_ln2_g, v_ln2_b, v_ffn2_w_gate, v_ffn2_w_up, v_ffn2_w_down, v_ln3_g, v_ln3_b):
    cx, cy, cc = lax.axis_index("x"), lax.axis_index("y"), lax.axis_index("c")
    pos = jnp.stack([cc, 2 * cx + cy, 2 * (1 - cx) + cy, 2 * cx + 1 - cy, 2 * (1 - cx) + 1 - cy]).astype(jnp.int32)

    w_sh = dict(f1g=ffn1_w_gate, f1u=ffn1_w_up, f1d=ffn1_w_down, w_in=w_in, w_ab=w_attn_branch,
                w_gb=w_gmlp_branch, w_out=w_out, f2g=ffn2_w_gate, f2u=ffn2_w_up, f2d=ffn2_w_down)
    m_sh = dict(f1g=m_ffn1_w_gate, f1u=m_ffn1_w_up, f1d=m_ffn1_w_down, w_in=m_w_in, w_ab=m_w_attn_branch,
                w_gb=m_w_gmlp_branch, w_out=m_w_out, f2g=m_ffn2_w_gate, f2u=m_ffn2_w_up, f2d=m_ffn2_w_down)
    v_sh = dict(f1g=v_ffn1_w_gate, f1u=v_ffn1_w_up, f1d=v_ffn1_w_down, w_in=v_w_in, w_ab=v_w_attn_branch,
                w_gb=v_w_gmlp_branch, w_out=v_w_out, f2g=v_ffn2_w_gate, f2u=v_ffn2_w_up, f2d=v_ffn2_w_down)
    w_sh = {k: (v[0].T if k in TRANSPOSED else v[0]) for k, v in w_sh.items()}
    m_sh = {k: (v[0].T if k in TRANSPOSED else v[0]) for k, v in m_sh.items()}
    v_sh = {k: (v[0].T if k in TRANSPOSED else v[0]) for k, v in v_sh.items()}

    started, tokens = [], []
    for gi, names in enumerate(GROUPS):
        placed = [_place_shard(w_sh[k], KIND[k], pos, "place_" + k) for k in names]
        fulls, ssem, rsem, token = _gather_start(placed, [KIND[k] for k in names], [w_sh[k].shape for k in names],
                                                 tokens[-1:], "gather_start_g%d" % gi)
        started.append((fulls, ssem, rsem))
        tokens.append(token)

    def weights_of(gi, after):
        names = GROUPS[gi]
        kinds, dims = [KIND[k] for k in names], [w_sh[k].shape for k in names]
        fulls, ssem, rsem = started[gi]
        fulls = _gather_wait(fulls, ssem, rsem, kinds, dims, list(after) + (tokens if gi == 0 else []),
                             "gather_wait_g%d" % gi)
        fulls = _gather_forward(fulls, kinds, dims, "gather_forward_g%d" % gi)
        return {k: (f.reshape(D, D) if k in ("w_gb", "w_out") else f) for k, f in zip(names, fulls)}

    pending, inflight = [], {}

    def grads_ready(gi, gd):
        grads = [gd[k] for k in GROUPS[gi]]
        lands = [lax.empty(g.shape[1:], F32) for g in grads]
        grads, lands, ssem, rsem, token = _pair_start(grads, lands, "rs_pair_start_g%d" % gi)
        pending.append((gi, grads, lands, ssem, rsem))
        return [token]

    def flush(after):
        gi, grads, lands, ssem, rsem = pending.pop()
        names = GROUPS[gi]
        grads, recv_a = _pair_wait(grads, lands, ssem, rsem, after, "rs_pair_wait_g%d" % gi)
        psums = [_pair_sum(g, r, pos, "rs_pair_sum_" + k) for g, r, k in zip(grads, recv_a, names)]
        lands = [lax.empty((3,) + p.shape[1:], BF16) for p in psums]
        psums, lands, ssem, rsem, token = _chip_start(psums, lands, "rs_chip_start_g%d" % gi)
        inflight[gi] = (grads, recv_a, psums, lands, ssem, rsem, token)
        return [token]

    P = dict(ln1_g=ln1_g, ln1_b=ln1_b, ln2_g=ln2_g, ln2_b=ln2_b, ln3_g=ln3_g, ln3_b=ln3_b, b_gates=b_gates,
             gmlp_ln_g=gmlp_ln_g, gmlp_ln_b=gmlp_ln_b, gmlp_w_s=gmlp_w_s[0], gmlp_b_s=gmlp_b_s[0])
    pos_f = positions.reshape(S, 1).astype(F32)
    loss_part, dx, small = _local_step(x[0], pos_f, loss_target[0], P, weights_of, grads_ready, flush)
    loss = lax.psum(loss_part[0, 0], ("x", "y", "c"))

    g_out, d_out, m_out, v_out = {}, {}, {}, {}

    def finish(gi, after):
        grads, recv_a, psums, lands, ssem, rsem, token = inflight[gi]
        recv_b = _chip_wait(psums, lands, ssem, rsem, after + [inflight[0][6]], "rs_chip_wait_g%d" % gi)
        halves = [_owner_sum(g, ra, rb, pos, "rs_owner_sum_" + k)
                  for g, ra, rb, k in zip(grads, recv_a, recv_b, GROUPS[gi])]
        reduced = _sibling_allgather(halves, "rs_sibling_allgather_g%d" % gi)
        for k, gfull in zip(GROUPS[gi], reduced):
            res = _adamw(w_sh[k], gfull.reshape(w_sh[k].shape), m_sh[k], v_sh[k], "adamw_" + k)
            after = [res[1]]
            if k in TRANSPOSED:
                res = [r.T for r in res]
            g_out[k], d_out[k], m_out[k], v_out[k] = [r[None] for r in res]
        return after

    after = []
    for gi in (3, 2, 1):
        after = finish(gi, after)

    parts = _small_allgather(small, after).reshape(8, SMALL_ROWS, D)
    sp = (ln1_g, ln1_b, gmlp_ln_g, gmlp_ln_b, ln2_g, ln2_b, ln3_g, ln3_b, b_gates, gmlp_b_s, gmlp_w_s)
    sm = (m_ln1_g, m_ln1_b, m_gmlp_ln_g, m_gmlp_ln_b, m_ln2_g, m_ln2_b, m_ln3_g, m_ln3_b, m_b_gates, m_gmlp_b_s,
          m_gmlp_w_s)
    sv = (v_ln1_g, v_ln1_b, v_gmlp_ln_g, v_gmlp_ln_b, v_ln2_g, v_ln2_b, v_ln3_g, v_ln3_b, v_b_gates, v_gmlp_b_s,
          v_gmlp_w_s)
    sg, sd, smn, svn = _small_sum_adamw(parts, _pack_small(*sp), _pack_small(*sm), _pack_small(*sv), "small_adamw")
    names = ("ln1_g", "ln1_b", "gmlp_ln_g", "gmlp_ln_b", "ln2_g", "ln2_b", "ln3_g", "ln3_b", "b_gates", "gmlp_b_s",
             "gmlp_w_s")
    for dst, packed in ((g_out, sg), (d_out, sd), (m_out, smn), (v_out, svn)):
        for nm, val in zip(names, _unpack_small(packed)):
            dst[nm] = val
    finish(0, [sg])

    order = ("f1g", "f1u", "f1d", "ln1_g", "ln1_b", "w_in", "b_gates", "gmlp_ln_g", "gmlp_ln_b", "gmlp_w_s", "gmlp_b_s",
             "w_ab", "w_gb", "w_out", "ln2_g", "ln2_b", "f2g", "f2u", "f2d", "ln3_g", "ln3_b")
    outs = [loss, dx[None]]
    for dst in (g_out, d_out, m_out, v_out):
        outs += [dst[k] for k in order]
    return tuple(outs)
```

```python
import functools
import math

import jax
import jax.numpy as jnp
from jax import lax
from jax.experimental import pallas as pl
from jax.experimental.pallas import tpu as pltpu

F32 = jnp.float32
BF16 = jnp.bfloat16

S = 2048
D = 1024
NSH = 4
FSH = 704
ATT_W = 1536
GRP_W = 512
NG = 3
NH = 8
DH = 64
BLK = 128
NBLK = S // BLK
GW = 1024
IN_W = 8704
IN_SH = IN_W // NSH
ALPHA = 2.0 ** 0.25
LN_EPS = 1e-5
ROPE_THETA = 10000.0
DILATIONS = (1, 4, 16)
ADAM_LR, ADAM_B1, ADAM_B2, ADAM_EPS, ADAM_WD, ADAM_STEP = 0.001, 0.9, 0.999, 1e-08, 0.01, 10
SMALL_ROWS = 144
MESH_T = pl.DeviceIdType.MESH
MIB = 1024 * 1024
NEG_INF = float("-inf")


def _cp(sem, vmem_mib=48):
    return pltpu.CompilerParams(dimension_semantics=sem, vmem_limit_bytes=vmem_mib * MIB)


def _ln_stats(r):
    mu = jnp.mean(r, axis=-1, keepdims=True)
    xc = r - mu
    var = jnp.mean(xc * xc, axis=-1, keepdims=True)
    rstd = lax.rsqrt(var + LN_EPS)
    return xc * rstd, rstd


def _ln_dx(dxh, xh, rstd):
    m1 = jnp.mean(dxh, axis=-1, keepdims=True)
    m2 = jnp.mean(dxh * xh, axis=-1, keepdims=True)
    return rstd * (dxh - m1 - xh * m2)


def _dot_nt(a, b):
    return lax.dot_general(a, b, (((1,), (1,)), ((), ())), preferred_element_type=F32)


def _dot_tn(a, b):
    return lax.dot_general(a, b, (((0,), (0,)), ((), ())), preferred_element_type=F32)


def _dot(a, b):
    return jnp.dot(a, b, preferred_element_type=F32)


def _ffn_fwd(xin, wgt, wut, wd, ln_g, ln_b, name, emit_t=False):
    tm = 512

    def body(x_ref, wg_ref, wu_ref, wd_ref, g_ref, b_ref, *rest):
        if emit_t:
            h_ref, hb_ref, xh_ref, rstd_ref, a_ref, bb_ref, ht_ref, acc_ref = rest
        else:
            h_ref, hb_ref, xh_ref, rstd_ref, a_ref, bb_ref, acc_ref = rest
        j = pl.program_id(1)
        xb = x_ref[...].astype(BF16)
        a = _dot_nt(xb, wg_ref[...])
        b = _dot_nt(xb, wu_ref[...])
        a_ref[...] = a.astype(BF16)
        bb_ref[...] = b.astype(BF16)
        s = (a * jax.nn.sigmoid(a)) * b
        f = _dot(s.astype(BF16), wd_ref[...])

        @pl.when(j == 0)
        def _():
            acc_ref[...] = f

        @pl.when(j > 0)
        def _():
            acc_ref[...] += f

        @pl.when(j == NSH - 1)
        def _():
            r = ALPHA * x_ref[...] + 0.5 * acc_ref[...]
            xh, rstd = _ln_stats(r)
            h = xh * g_ref[...] + b_ref[...]
            h_ref[...] = h
            hb_ref[...] = h.astype(BF16)
            xh_ref[...] = xh
            rstd_ref[...] = rstd
            if emit_t:
                ht_ref[...] = h.T.astype(BF16)

    row = pl.BlockSpec((tm, D), lambda i, j: (i, 0))
    vec = pl.BlockSpec((1, D), lambda i, j: (0, 0))
    wsp = pl.BlockSpec((None, FSH, D), lambda i, j: (j, 0, 0))
    ab = pl.BlockSpec((None, tm, FSH), lambda i, j: (j, i, 0))
    out_specs = [row, row, row, pl.BlockSpec((tm, 1), lambda i, j: (i, 0)), ab, ab]
    out_shape = [jax.ShapeDtypeStruct((S, D), F32), jax.ShapeDtypeStruct((S, D), BF16),
                 jax.ShapeDtypeStruct((S, D), F32), jax.ShapeDtypeStruct((S, 1), F32),
                 jax.ShapeDtypeStruct((NSH, S, FSH), BF16), jax.ShapeDtypeStruct((NSH, S, FSH), BF16)]
    if emit_t:
        out_specs.append(pl.BlockSpec((D, tm), lambda i, j: (0, i)))
        out_shape.append(jax.ShapeDtypeStruct((D, S), BF16))
    return pl.pallas_call(
        body, name=name, grid=(S // tm, NSH),
        in_specs=[row, wsp, wsp, wsp, vec, vec], out_specs=out_specs, out_shape=out_shape,
        scratch_shapes=[pltpu.VMEM((tm, D), F32)],
        compiler_params=_cp(("parallel", "arbitrary")),
    )(xin, wgt, wut, wd, ln_g, ln_b)


def _ffn_bwd(dr, xin_b, a, b, wgt, wut, wd, name, after=()):
    tm = 512
    ni = S // tm
    hr = FSH // 2

    def body(dr_ref, a_ref, b_ref, wg_ref, wu_ref, wd_ref, x_hbm, *rest):
        dwg_hbm, dwu_hbm, dwd_hbm, dx_hbm, dx_acc, da_all, db_all, s_all, df_all, x_all, res_buf, sems = rest[len(after):]
        j = pl.program_id(0)
        i = pl.program_id(1)
        rows = pl.ds(pl.multiple_of(i * tm, tm), tm)

        @pl.when(jnp.logical_and(j == 0, i == 0))
        def _():
            cp = pltpu.make_async_copy(x_hbm, x_all, sems.at[0])
            cp.start()
            cp.wait()

        drv = dr_ref[...]
        df = (0.5 * drv).astype(BF16)

        @pl.when(j == 0)
        def _():
            df_all[rows, :] = df

        ds = jnp.concatenate([_dot_nt(df, wd_ref[0:384, :]), _dot_nt(df, wd_ref[384:FSH, :])], axis=1)
        av = a_ref[...].astype(F32)
        bv = b_ref[...].astype(F32)
        sig = jax.nn.sigmoid(av)
        sl = av * sig
        da = (ds * bv * (sig * (1.0 + av * (1.0 - sig)))).astype(BF16)
        db = (ds * sl).astype(BF16)
        da_all[rows, :] = da
        db_all[rows, :] = db
        s_all[rows, :] = (sl * bv).astype(BF16)
        dx = _dot(da, wg_ref[...]) + _dot(db, wu_ref[...])

        @pl.when(j == 0)
        def _():
            dx_acc[rows, :] = ALPHA * drv + dx

        @pl.when(j > 0)
        def _():
            dx_acc[rows, :] += dx

        @pl.when(i == ni - 1)
        def _():
            copies = []
            for n, (lhs, rhs, out) in enumerate(((da_all, x_all, dwg_hbm), (db_all, x_all, dwu_hbm),
                                                 (s_all, df_all, dwd_hbm))):
                slot = n % 2
                if n >= 2:
                    for cp in copies[2 * (n - 2): 2 * (n - 2) + 2]:
                        cp.wait()
                res_buf[slot] = _dot_tn(lhs[...], rhs[...])
                for h in range(2):
                    cp = pltpu.make_async_copy(res_buf.at[slot, pl.ds(h * hr, hr), :], out.at[h, j],
                                               sems.at[1 + 2 * slot + h])
                    cp.start()
                    copies.append(cp)
            for cp in copies[2:]:
                cp.wait()

        @pl.when(jnp.logical_and(j == NSH - 1, i == ni - 1))
        def _():
            cp = pltpu.make_async_copy(dx_acc, dx_hbm, sems.at[0])
            cp.start()
            cp.wait()

    row = pl.BlockSpec((tm, D), lambda j, i: (i, 0))
    wsp = pl.BlockSpec((None, FSH, D), lambda j, i: (j, 0, 0))
    ab = pl.BlockSpec((None, tm, FSH), lambda j, i: (j, i, 0))
    dwshape = jax.ShapeDtypeStruct((2, NSH, hr, D), F32)
    return pl.pallas_call(
        body, name=name, grid=(NSH, ni),
        in_specs=[row, ab, ab, wsp, wsp, wsp, ANY] + [ANY] * len(after),
        out_specs=[ANY, ANY, ANY, ANY],
        out_shape=[dwshape, dwshape, dwshape, jax.ShapeDtypeStruct((S, D), F32)],
        scratch_shapes=[pltpu.VMEM((S, D), F32), pltpu.VMEM((S, FSH), BF16), pltpu.VMEM((S, FSH), BF16),
                        pltpu.VMEM((S, FSH), BF16), pltpu.VMEM((S, D), BF16), pltpu.VMEM((S, D), BF16),
                        pltpu.VMEM((2, FSH, D), F32), pltpu.SemaphoreType.DMA((5,))],
        compiler_params=_cp(("arbitrary", "arbitrary"), vmem_mib=58),
    )(dr, a, b, wgt, wut, wd, xin_b, *after)


def _matmul(a, b, mode, name, *, n, tm=512, tn=512, tk=None, b_col0=0, add=None, add_scale=1.0, out_dtype=F32,
            after=()):
    m, ka = a.shape
    tk = ka if tk is None else tk
    nk = ka // tk
    assert m % tm == 0 and n % tn == 0 and ka % tk == 0 and b_col0 % tn == 0
    off = b_col0 // tn
    na = len(after)

    def body(*refs):
        refs = refs[na:]
        if add is None:
            a_ref, b_ref, o_ref = refs[:3]
            add_ref = None
            rest = refs[3:]
        else:
            a_ref, b_ref, add_ref, o_ref = refs[:4]
            rest = refs[4:]
        k = pl.program_id(2)
        av = a_ref[...].astype(BF16)
        bv = b_ref[...].astype(BF16)
        p = _dot(av, bv) if mode == "nn" else _dot_nt(av, bv)

        def finish(acc):
            if add_ref is not None:
                acc = acc + add_scale * add_ref[...]
            o_ref[...] = acc.astype(out_dtype)

        if nk == 1:
            finish(p)
        else:
            acc_ref = rest[0]

            @pl.when(k == 0)
            def _():
                acc_ref[...] = p

            @pl.when(k > 0)
            def _():
                acc_ref[...] += p

            @pl.when(k == nk - 1)
            def _():
                finish(acc_ref[...])

    a_spec = pl.BlockSpec((tm, tk), lambda i, j, k: (i, k))
    if mode == "nn":
        b_spec = pl.BlockSpec((tk, tn), lambda i, j, k: (k, j + off))
    else:
        b_spec = pl.BlockSpec((tn, tk), lambda i, j, k: (j, k))
    o_spec = pl.BlockSpec((tm, tn), lambda i, j, k: (i, j))
    in_specs = [pl.BlockSpec(memory_space=pl.ANY)] * na + [a_spec, b_spec] + ([o_spec] if add is not None else [])
    args = tuple(after) + (a, b) + ((add,) if add is not None else ())
    return pl.pallas_call(
        body, name=name, grid=(m // tm, n // tn, nk),
        in_specs=in_specs, out_specs=o_spec,
        out_shape=jax.ShapeDtypeStruct((m, n), out_dtype),
        scratch_shapes=[pltpu.VMEM((tm, tn), F32)] if nk > 1 else [],
        compiler_params=_cp(("parallel", "parallel", "arbitrary")),
    )(*args)


def _wgrad(xt, y, rh, c, name, row_sharded, after=()):
    na = len(after)
    if row_sharded:
        def body(x_ref, y_ref, *rest):
            o_ref = rest[na]
            res = _dot(x_ref[...], y_ref[...].astype(BF16))
            for j in range(NSH):
                for h in range(2):
                    o_ref[h, j] = res[(2 * j + h) * rh:(2 * j + h + 1) * rh, :]

        grid = (1,)
        in_specs = [pl.BlockSpec((2 * NSH * rh, S), lambda g: (0, 0)), pl.BlockSpec((S, c), lambda g: (0, 0))]
        out_specs = pl.BlockSpec((2, NSH, rh, c), lambda g: (0, 0, 0, 0))
        sem = ("arbitrary",)
    else:
        def body(x_ref, y_ref, *rest):
            rest[na][...] = _dot(x_ref[...], y_ref[...].astype(BF16))

        grid = (2, NSH)
        in_specs = [pl.BlockSpec((rh, S), lambda h, j: (h, 0)), pl.BlockSpec((S, c), lambda h, j: (0, j))]
        out_specs = pl.BlockSpec((None, None, rh, c), lambda h, j: (h, j, 0, 0))
        sem = ("parallel", "parallel")
    return pl.pallas_call(
        body, name=name, grid=grid, in_specs=in_specs + [pl.BlockSpec(memory_space=pl.ANY)] * na, out_specs=out_specs,
        out_shape=jax.ShapeDtypeStruct((2, NSH, rh, c), F32),
        compiler_params=_cp(sem, vmem_mib=56),
    )(xt, y, *after)


def _resid_ln(res, f, ln_g, ln_b, name):
    tm = 256

    def body(res_ref, f_ref, g_ref, b_ref, h_ref, hb_ref, xh_ref, rstd_ref):
        r = ALPHA * res_ref[...] + f_ref[...]
        xh, rstd = _ln_stats(r)
        h = xh * g_ref[...] + b_ref[...]
        h_ref[...] = h
        hb_ref[...] = h.astype(BF16)
        xh_ref[...] = xh
        rstd_ref[...] = rstd

    row = pl.BlockSpec((tm, D), lambda i: (i, 0))
    vec = pl.BlockSpec((1, D), lambda i: (0, 0))
    return pl.pallas_call(
        body, name=name, grid=(S // tm,),
        in_specs=[row, row, vec, vec],
        out_specs=[row, row, row, pl.BlockSpec((tm, 1), lambda i: (i, 0))],
        out_shape=[jax.ShapeDtypeStruct((S, D), F32), jax.ShapeDtypeStruct((S, D), BF16),
                   jax.ShapeDtypeStruct((S, D), F32), jax.ShapeDtypeStruct((S, 1), F32)],
        compiler_params=_cp(("parallel",)),
    )(res, f, ln_g, ln_b)


def _ln_bwd(dout, xh, rstd, ln_g, name, target=None, after=()):
    tm = 256
    with_loss = target is not None
    na = len(after)

    def body(*refs):
        refs = refs[na:]
        if with_loss:
            y_ref, t_ref, xh_ref, rstd_ref, g_ref, dr_ref, dg_ref, db_ref, loss_ref = refs
            err = y_ref[...] - t_ref[...]
            dy = err * (1.0 / D)
        else:
            y_ref, xh_ref, rstd_ref, g_ref, dr_ref, dg_ref, db_ref = refs
            dy = y_ref[...]
        i = pl.program_id(0)
        xh = xh_ref[...]
        dr_ref[...] = _ln_dx(dy * g_ref[...], xh, rstd_ref[...])
        dg = jnp.sum(dy * xh, axis=0, keepdims=True)
        db = jnp.sum(dy, axis=0, keepdims=True)

        @pl.when(i == 0)
        def _():
            dg_ref[...] = dg
            db_ref[...] = db

        @pl.when(i > 0)
        def _():
            dg_ref[...] += dg
            db_ref[...] += db

        if with_loss:
            part = 0.5 * jnp.sum(jnp.mean(err * err, axis=-1, keepdims=True), axis=0, keepdims=True)
            part = jnp.broadcast_to(part, (8, 128))

            @pl.when(i == 0)
            def _():
                loss_ref[...] = part

            @pl.when(i > 0)
            def _():
                loss_ref[...] += part

    row = pl.BlockSpec((tm, D), lambda i: (i, 0))
    vec = pl.BlockSpec((1, D), lambda i: (0, 0))
    col = pl.BlockSpec((tm, 1), lambda i: (i, 0))
    in_specs = [pl.BlockSpec(memory_space=pl.ANY)] * na + [row] + ([row] if with_loss else []) + [row, col, vec]
    out_specs = [row, vec, vec] + ([pl.BlockSpec((8, 128), lambda i: (0, 0))] if with_loss else [])
    out_shape = [jax.ShapeDtypeStruct((S, D), F32), jax.ShapeDtypeStruct((1, D), F32),
                 jax.ShapeDtypeStruct((1, D), F32)] + ([jax.ShapeDtypeStruct((8, 128), F32)] if with_loss else [])
    args = tuple(after) + (dout,) + ((target,) if with_loss else ()) + (xh, rstd, ln_g)
    return pl.pallas_call(
        body, name=name, grid=(S // tm,), in_specs=in_specs, out_specs=out_specs, out_shape=out_shape,
        compiler_params=_cp(("arbitrary",)),
    )(*args)


ROPE_TM = 256


def _rope_tables(pos_ref, invf_ref, sign):
    ang = pos_ref[...] * invf_ref[...]
    lane = lax.broadcasted_iota(jnp.int32, ang.shape, 1)
    first = (lane % DH) < (DH // 2)
    sinv = jnp.sin(ang) * sign
    return first, jnp.cos(ang), jnp.where(first, -sinv, sinv)


def _rotate(x, first, cosf, sinf):
    return x * cosf + jnp.where(first, pltpu.roll(x, 96, 1), pltpu.roll(x, 32, 1)) * sinf


def _rope_fwd(qkv, pos_f, invf, name):
    tm = ROPE_TM

    def body(t_ref, pos_ref, invf_ref, o0_ref, o1_ref, o2_ref, buf_ref):
        first, cosf, sinf = _rope_tables(pos_ref, invf_ref, 1.0)
        o_refs = (o0_ref, o1_ref, o2_ref)
        for sec in range(3):
            for gi, d in enumerate(DILATIONS):
                for ch in range(GRP_W // 128):
                    src = sec * ATT_W + gi * GRP_W + ch * 128
                    dst = slice(sec * GRP_W + ch * 128, sec * GRP_W + (ch + 1) * 128)
                    x = t_ref[:, src:src + 128]
                    if sec < 2:
                        x = _rotate(x, first, cosf, sinf)
                    if d == 1:
                        o_refs[gi][0, :, dst] = x.astype(BF16)
                    else:
                        buf_ref[...] = x
                        for r in range(d):
                            o_refs[gi][r, :, dst] = buf_ref[pl.ds(r, tm // d, stride=d), :].astype(BF16)

    return pl.pallas_call(
        body, name=name, grid=(S // tm,),
        in_specs=[pl.BlockSpec((tm, 3 * ATT_W), lambda i: (i, 0)), pl.BlockSpec((tm, 1), lambda i: (i, 0)),
                  pl.BlockSpec((1, 128), lambda i: (0, 0))],
        out_specs=[pl.BlockSpec((d, tm // d, 3 * GRP_W), lambda i: (0, i, 0)) for d in DILATIONS],
        out_shape=[jax.ShapeDtypeStruct((d, S // d, 3 * GRP_W), BF16) for d in DILATIONS],
        scratch_shapes=[pltpu.VMEM((tm, 128), F32)],
        compiler_params=_cp(("parallel",)),
    )(qkv, pos_f, invf)


def _rope_bwd(dqkv_c, pos_f, invf, name):
    tm = ROPE_TM

    def body(*refs):
        g_refs, (pos_ref, invf_ref, o_ref, buf_ref) = refs[:9], refs[9:]
        first, cosf, sinf = _rope_tables(pos_ref, invf_ref, -1.0)
        for sec in range(3):
            for gi, d in enumerate(DILATIONS):
                g_ref = g_refs[3 * gi + sec]
                for ch in range(GRP_W // 128):
                    cols = slice(ch * 128, (ch + 1) * 128)
                    if d == 1:
                        x = g_ref[0, :, cols]
                    else:
                        for r in range(d):
                            buf_ref[pl.ds(r, tm // d, stride=d), :] = g_ref[r, :, cols]
                        x = buf_ref[...]
                    if sec < 2:
                        x = _rotate(x, first, cosf, sinf)
                    dst = sec * ATT_W + gi * GRP_W + ch * 128
                    o_ref[:, dst:dst + 128] = x.astype(BF16)

    g_specs = [pl.BlockSpec((d, tm // d, GRP_W), lambda i: (0, i, 0)) for d in DILATIONS for _ in range(3)]
    return pl.pallas_call(
        body, name=name, grid=(S // tm,),
        in_specs=g_specs + [pl.BlockSpec((tm, 1), lambda i: (i, 0)), pl.BlockSpec((1, 128), lambda i: (0, 0))],
        out_specs=pl.BlockSpec((tm, 3 * ATT_W), lambda i: (i, 0)),
        out_shape=jax.ShapeDtypeStruct((S, 3 * ATT_W), BF16),
        scratch_shapes=[pltpu.VMEM((tm, 128), F32)],
        compiler_params=_cp(("parallel",)),
    )(*[g for grp in dqkv_c for g in grp], pos_f, invf)


def _class_order(ts, name):
    tm = ROPE_TM
    n = len(ts)

    def body(*refs):
        buf_ref = refs[3 * n]
        for a in range(n):
            for ch in range(GRP_W // 128):
                cols = slice(ch * 128, (ch + 1) * 128)
                buf_ref[...] = refs[a][:, cols]
                for b, d in enumerate(DILATIONS[1:]):
                    for r in range(d):
                        refs[n + 2 * a + b][r, :, cols] = buf_ref[pl.ds(r, tm // d, stride=d), :]

    return pl.pallas_call(
        body, name=name, grid=(S // tm,),
        in_specs=[pl.BlockSpec((tm, GRP_W), lambda i: (i, 0))] * n,
        out_specs=[pl.BlockSpec((d, tm // d, GRP_W), lambda i: (0, i, 0)) for _ in range(n) for d in DILATIONS[1:]],
        out_shape=[jax.ShapeDtypeStruct((d, S // d, GRP_W), F32) for _ in range(n) for d in DILATIONS[1:]],
        scratch_shapes=[pltpu.VMEM((tm, 128), F32)],
        compiler_params=_cp(("parallel",)),
    )(*ts)


def _heads(ref):
    return jnp.stack([ref[:, h * DH:(h + 1) * DH] for h in range(NH)])


def _bdot_nt(a, b):
    return lax.dot_general(a, b, (((2,), (2,)), ((0,), (0,))), preferred_element_type=F32)


def _bdot(a, b):
    return lax.dot_general(a, b, (((2,), (1,)), ((0,), (0,))), preferred_element_type=F32)


def _bdot_tn(a, b):
    return lax.dot_general(a, b, (((1,), (1,)), ((0,), (0,))), preferred_element_type=F32)


def _attn_fwd(gi, qkv_c, name):
    d = DILATIONS[gi]
    nblk = S // d // BLK

    def body(*refs):
        if nblk > 1:
            q_ref, kc_ref, kp_ref, vc_ref, vp_ref, o_ref, lse_ref = refs
            has_prev = pl.program_id(1) != 0
        else:
            q_ref, kc_ref, vc_ref, o_ref, lse_ref = refs
        qi = lax.broadcasted_iota(jnp.int32, (NH, BLK, BLK), 1)
        kj = lax.broadcasted_iota(jnp.int32, (NH, BLK, BLK), 2)
        q = _heads(q_ref)
        sc = jnp.where(kj <= qi, _bdot_nt(q, _heads(kc_ref)) * 0.125, NEG_INF)
        m = jnp.max(sc, axis=-1, keepdims=True)
        if nblk > 1:
            mask_p = jnp.logical_and(kj >= qi, has_prev)
            sp = jnp.where(mask_p, _bdot_nt(q, _heads(kp_ref)) * 0.125, NEG_INF)
            m = jnp.maximum(m, jnp.max(sp, axis=-1, keepdims=True))
        pc = jnp.exp(sc - m)
        l = jnp.sum(pc, axis=-1, keepdims=True)
        o = _bdot(pc.astype(BF16), _heads(vc_ref))
        if nblk > 1:
            pp = jnp.exp(sp - m)
            l = l + jnp.sum(pp, axis=-1, keepdims=True)
            o = o + _bdot(pp.astype(BF16), _heads(vp_ref))
        o = o / l
        lse = m + jnp.log(l)
        for h in range(NH):
            sl = slice(h * DH, (h + 1) * DH)
            o_ref[:, sl] = o[h]
            lse_ref[:, sl] = jnp.broadcast_to(lse[h], (BLK, DH))

    def cur(sec):
        return pl.BlockSpec((None, BLK, GRP_W), lambda r, n: (r, n, sec))

    def prev(sec):
        return pl.BlockSpec((None, BLK, GRP_W), lambda r, n: (r, jnp.maximum(n - 1, 0), sec))

    out = pl.BlockSpec((None, BLK, GRP_W), lambda r, n: (r, n, 0))
    shp = jax.ShapeDtypeStruct((d, S // d, GRP_W), F32)
    if nblk > 1:
        in_specs, args = [cur(0), cur(1), prev(1), cur(2), prev(2)], (qkv_c,) * 5
    else:
        in_specs, args = [cur(0), cur(1), cur(2)], (qkv_c,) * 3
    return pl.pallas_call(
        body, name=name, grid=(d, nblk), in_specs=in_specs, out_specs=[out, out], out_shape=[shp, shp],
        compiler_params=_cp(("parallel", "parallel")),
    )(*args)


def _attn_combine(os, lses, name):
    tm = ROPE_TM

    def body(o0_ref, o1_ref, o2_ref, l0_ref, l1_ref, l2_ref, y_ref, yt_ref, l_ref, buf_ref):
        def token_order(ref, d, cols, slot):
            if d == 1:
                return ref[0, :, cols]
            for r in range(d):
                buf_ref[slot, pl.ds(r, tm // d, stride=d), :] = ref[r, :, cols]
            return buf_ref[slot]

        for ch in range(GRP_W // 128):
            cols = slice(ch * 128, (ch + 1) * 128)
            o = [token_order(ref, d, cols, k) for k, (ref, d) in enumerate(zip((o0_ref, o1_ref, o2_ref), DILATIONS))]
            ls = [token_order(ref, d, cols, 3 + k)
                  for k, (ref, d) in enumerate(zip((l0_ref, l1_ref, l2_ref), DILATIONS))]
            m = jnp.maximum(jnp.maximum(ls[0], ls[1]), ls[2])
            e = [jnp.exp(l - m) for l in ls]
            den = e[0] + e[1] + e[2]
            y = (e[0] * o[0] + e[1] * o[1] + e[2] * o[2]) / den
            y_ref[:, cols] = y
            yt_ref[cols, :] = y.T.astype(BF16)
            l_ref[:, cols] = m + jnp.log(den)

    blk = pl.BlockSpec((tm, GRP_W), lambda i: (i, 0))
    cls = [pl.BlockSpec((d, tm // d, GRP_W), lambda i: (0, i, 0)) for d in DILATIONS]
    shp = jax.ShapeDtypeStruct((S, GRP_W), F32)
    return pl.pallas_call(
        body, name=name, grid=(S // tm,), in_specs=cls + cls,
        out_specs=[blk, pl.BlockSpec((GRP_W, tm), lambda i: (0, i)), blk],
        out_shape=[shp, jax.ShapeDtypeStruct((GRP_W, S), BF16), shp],
        scratch_shapes=[pltpu.VMEM((6, tm, 128), F32)],
        compiler_params=_cp(("parallel",)),
    )(*os, *lses)


def _attn_bwd(gi, qkv_c, dy_c, y_c, lse_c, name):
    d = DILATIONS[gi]
    nblk = S // d // BLK

    def body(*refs):
        if nblk > 1:
            (q_ref, qn_ref, k_ref, kp_ref, v_ref, vp_ref, dy_ref, dyn_ref, y_ref, yn_ref, l_ref, ln_ref,
             dq_ref, dk_ref, dv_ref) = refs
            n = pl.program_id(1)
            has_prev = n != 0
            has_next = n != nblk - 1
        else:
            q_ref, k_ref, v_ref, dy_ref, y_ref, l_ref, dq_ref, dk_ref, dv_ref = refs
        qi = lax.broadcasted_iota(jnp.int32, (NH, BLK, BLK), 1)
        kj = lax.broadcasted_iota(jnp.int32, (NH, BLK, BLK), 2)

        def lse_col(ref):
            return jnp.stack([ref[:, h * DH:h * DH + 1] for h in range(NH)])

        q, k, v = _heads(q_ref), _heads(k_ref), _heads(v_ref)
        dy = _heads(dy_ref)
        dd = jnp.sum(dy * _heads(y_ref), axis=-1, keepdims=True)
        lcol = lse_col(l_ref)
        dyb = dy.astype(BF16)
        p = jnp.exp(jnp.where(kj <= qi, _bdot_nt(q, k) * 0.125, NEG_INF) - lcol)
        ds = (p * (_bdot_nt(dyb, v) - dd)).astype(BF16)
        dq = _bdot(ds, k)
        dk = _bdot_tn(ds, q)
        dv = _bdot_tn(p.astype(BF16), dyb)
        if nblk > 1:
            qn, kpv, vpv = _heads(qn_ref), _heads(kp_ref), _heads(vp_ref)
            dyn = _heads(dyn_ref)
            ddn = jnp.sum(dyn * _heads(yn_ref), axis=-1, keepdims=True)
            lncol = lse_col(ln_ref)
            dynb = dyn.astype(BF16)
            mask_p = jnp.logical_and(kj >= qi, has_prev)
            pp = jnp.exp(jnp.where(mask_p, _bdot_nt(q, kpv) * 0.125, NEG_INF) - lcol)
            dsp = (pp * (_bdot_nt(dyb, vpv) - dd)).astype(BF16)
            dq = dq + _bdot(dsp, kpv)
            mask_n = jnp.logical_and(kj >= qi, has_next)
            pn = jnp.exp(jnp.where(mask_n, _bdot_nt(qn, k) * 0.125, NEG_INF) - lncol)
            dsn = (pn * (_bdot_nt(dynb, v) - ddn)).astype(BF16)
            dk = dk + _bdot_tn(dsn, qn)
            dv = dv + _bdot_tn(pn.astype(BF16), dynb)
        dq = dq * 0.125
        dk = dk * 0.125
        for h in range(NH):
            sl = slice(h * DH, (h + 1) * DH)
            dq_ref[:, sl] = dq[h]
            dk_ref[:, sl] = dk[h]
            dv_ref[:, sl] = dv[h]

    def spec(sec, shift):
        def idx(r, n):
            return (r, jnp.clip(n + shift, 0, nblk - 1), sec)
        return pl.BlockSpec((None, BLK, GRP_W), idx)

    if nblk > 1:
        in_specs = [spec(0, 0), spec(0, 1), spec(1, 0), spec(1, -1), spec(2, 0), spec(2, -1),
                    spec(0, 0), spec(0, 1), spec(0, 0), spec(0, 1), spec(0, 0), spec(0, 1)]
        args = (qkv_c,) * 6 + (dy_c, dy_c, y_c, y_c, lse_c, lse_c)
    else:
        in_specs = [spec(0, 0), spec(1, 0), spec(2, 0), spec(0, 0), spec(0, 0), spec(0, 0)]
        args = (qkv_c, qkv_c, qkv_c, dy_c, y_c, lse_c)
    out = spec(0, 0)
    shp = jax.ShapeDtypeStruct((d, S // d, GRP_W), F32)
    return pl.pallas_call(
        body, name=name, grid=(d, nblk), in_specs=in_specs, out_specs=[out, out, out], out_shape=[shp, shp, shp],
        compiler_params=_cp(("parallel", "parallel")),
    )(*args)


_SQRT_HALF = 0.7071067811865476
_INV_SQRT_2PI = 0.3989422804014327


def _gelu(z):
    return 0.5 * z * (1.0 + lax.erf(z * _SQRT_HALF))


def _gelu_grad(z):
    return 0.5 * (1.0 + lax.erf(z * _SQRT_HALF)) + z * (jnp.exp(-0.5 * z * z) * _INV_SQRT_2PI)


def _tril_mask():
    t = lax.broadcasted_iota(jnp.int32, (BLK, BLK), 0)
    s = lax.broadcasted_iota(jnp.int32, (BLK, BLK), 1)
    return s <= t


def _gmlp_fwd(z, ln_g, ln_b, w_s, b_s_t, name):
    def body(z_ref, g_ref, b_ref, ws_ref, bs_ref, y_ref, yt_ref):
        zg = _gelu(z_ref[...])
        u = zg[:, :GW]
        xh, _ = _ln_stats(zg[:, GW:])
        vn = (xh * g_ref[...] + b_ref[...]).astype(BF16)
        tril = _tril_mask()
        for gg in range(8):
            sl = slice(gg * BLK, (gg + 1) * BLK)
            wt = jnp.where(tril, ws_ref[gg], 0.0).astype(BF16)
            mixed = _dot(wt, vn[:, sl]) + bs_ref[:, gg:gg + 1]
            yv = u[:, sl] * mixed
            y_ref[:, sl] = yv.astype(BF16)
            yt_ref[sl, :] = yv.T.astype(BF16)

    vec = pl.BlockSpec((1, GW), lambda n: (0, 0))
    return pl.pallas_call(
        body, name=name, grid=(NBLK,),
        in_specs=[pl.BlockSpec((BLK, 2 * GW), lambda n: (n, 0)), vec, vec,
                  pl.BlockSpec((8, BLK, BLK), lambda n: (0, 0, 0)), pl.BlockSpec((BLK, 8), lambda n: (0, 0))],
        out_specs=[pl.BlockSpec((BLK, GW), lambda n: (n, 0)), pl.BlockSpec((GW, BLK), lambda n: (0, n))],
        out_shape=[jax.ShapeDtypeStruct((S, GW), BF16), jax.ShapeDtypeStruct((GW, S), BF16)],
        compiler_params=_cp(("parallel",)),
    )(z, ln_g, ln_b, w_s, b_s_t)


def _gmlp_bwd(z, dy, ln_g, ln_b, w_s, b_s_t, name):
    def body(z_ref, dy_ref, g_ref, b_ref, ws_ref, bs_ref, dz_ref, dws_ref, dbs_ref, dg_ref, db_ref, dvn_ref):
        n = pl.program_id(0)
        zv = z_ref[...]
        zg = _gelu(zv)
        u = zg[:, :GW]
        xh, rstd = _ln_stats(zg[:, GW:])
        vn = (xh * g_ref[...] + b_ref[...]).astype(BF16)
        tril = _tril_mask()

        @pl.when(n == 0)
        def _():
            dws_ref[...] = jnp.zeros_like(dws_ref)
            dbs_ref[...] = jnp.zeros_like(dbs_ref)
            dg_ref[...] = jnp.zeros_like(dg_ref)
            db_ref[...] = jnp.zeros_like(db_ref)

        for gg in range(8):
            sl = slice(gg * BLK, (gg + 1) * BLK)
            wt = jnp.where(tril, ws_ref[gg], 0.0).astype(BF16)
            dyg = dy_ref[:, sl]
            mixed = _dot(wt, vn[:, sl]) + bs_ref[:, gg:gg + 1]
            dz_ref[:, sl] = (dyg * mixed * _gelu_grad(zv[:, sl])).astype(BF16)
            dmix = dyg * u[:, sl]
            dmb = dmix.astype(BF16)
            dws_ref[gg] += jnp.where(tril, _dot_nt(dmb, vn[:, sl]), 0.0)
            dbs_ref[:, gg:gg + 1] += jnp.sum(dmix, axis=-1, keepdims=True)
            dvn_ref[:, sl] = _dot_tn(wt, dmb)

        dvn = dvn_ref[...]
        dg_ref[...] += jnp.sum(dvn * xh, axis=0, keepdims=True)
        db_ref[...] += jnp.sum(dvn, axis=0, keepdims=True)
        dvg = _ln_dx(dvn * g_ref[...], xh, rstd)
        dz_ref[:, GW:] = (dvg * _gelu_grad(zv[:, GW:])).astype(BF16)

    vec = pl.BlockSpec((1, GW), lambda n: (0, 0))
    ws = pl.BlockSpec((8, BLK, BLK), lambda n: (0, 0, 0))
    bs = pl.BlockSpec((BLK, 8), lambda n: (0, 0))
    return pl.pallas_call(
        body, name=name, grid=(NBLK,),
        in_specs=[pl.BlockSpec((BLK, 2 * GW), lambda n: (n, 0)), pl.BlockSpec((BLK, GW), lambda n: (n, 0)),
                  vec, vec, ws, bs],
        out_specs=[pl.BlockSpec((BLK, 2 * GW), lambda n: (n, 0)), ws, bs, vec, vec],
        out_shape=[jax.ShapeDtypeStruct((S, 2 * GW), BF16), jax.ShapeDtypeStruct((8, BLK, BLK), F32),
                   jax.ShapeDtypeStruct((BLK, 8), F32), jax.ShapeDtypeStruct((1, GW), F32),
                   jax.ShapeDtypeStruct((1, GW), F32)],
        scratch_shapes=[pltpu.VMEM((BLK, GW), F32)],
        compiler_params=_cp(("arbitrary",)),
    )(z, dy, ln_g, ln_b, w_s, b_s_t)


def _merge_fwd(a, b, gl, b_gates, name):
    tm = 256

    def body(a_ref, b_ref, g0_ref, g1_ref, bg_ref, o_ref, ot_ref):
        g0 = jax.nn.sigmoid(g0_ref[...] + bg_ref[:, :D])
        g1 = jax.nn.sigmoid(g1_ref[...] + bg_ref[:, D:])
        mg = g0 * a_ref[...] + g1 * b_ref[...]
        o_ref[...] = mg.astype(BF16)
        ot_ref[...] = mg.T.astype(BF16)

    row = pl.BlockSpec((tm, D), lambda i: (i, 0))
    return pl.pallas_call(
        body, name=name, grid=(S // tm,),
        in_specs=[row, row, row, pl.BlockSpec((tm, D), lambda i: (i, 1)), pl.BlockSpec((1, 2 * D), lambda i: (0, 0))],
        out_specs=[row, pl.BlockSpec((D, tm), lambda i: (0, i))],
        out_shape=[jax.ShapeDtypeStruct((S, D), BF16), jax.ShapeDtypeStruct((D, S), BF16)],
        compiler_params=_cp(("parallel",)),
    )(a, b, gl, gl, b_gates)


def _merge_bwd(dm, a, b, gl, b_gates, name):
    tm = 256

    def body(dm_ref, a_ref, b_ref, g0_ref, g1_ref, bg_ref, da_ref, db_ref, dgl_ref, dbg_ref):
        i = pl.program_id(0)
        dmv = dm_ref[...]
        g0 = jax.nn.sigmoid(g0_ref[...] + bg_ref[:, :D])
        g1 = jax.nn.sigmoid(g1_ref[...] + bg_ref[:, D:])
        da_ref[...] = (dmv * g0).astype(BF16)
        db_ref[...] = (dmv * g1).astype(BF16)
        d0 = dmv * a_ref[...] * g0 * (1.0 - g0)
        d1 = dmv * b_ref[...] * g1 * (1.0 - g1)
        dgl_ref[:, :D] = d0.astype(BF16)
        dgl_ref[:, D:] = d1.astype(BF16)
        s0 = jnp.sum(d0, axis=0, keepdims=True)
        s1 = jnp.sum(d1, axis=0, keepdims=True)

        @pl.when(i == 0)
        def _():
            dbg_ref[:, :D] = s0
            dbg_ref[:, D:] = s1

        @pl.when(i > 0)
        def _():
            dbg_ref[:, :D] += s0
            dbg_ref[:, D:] += s1

    row = pl.BlockSpec((tm, D), lambda i: (i, 0))
    wide = pl.BlockSpec((tm, 2 * D), lambda i: (i, 0))
    bg = pl.BlockSpec((1, 2 * D), lambda i: (0, 0))
    return pl.pallas_call(
        body, name=name, grid=(S // tm,),
        in_specs=[row, row, row, row, pl.BlockSpec((tm, D), lambda i: (i, 1)), bg],
        out_specs=[row, row, wide, bg],
        out_shape=[jax.ShapeDtypeStruct((S, D), BF16), jax.ShapeDtypeStruct((S, D), BF16),
                   jax.ShapeDtypeStruct((S, 2 * D), BF16), jax.ShapeDtypeStruct((1, 2 * D), F32)],
        compiler_params=_cp(("arbitrary",)),
    )(dm, a, b, gl, gl, b_gates)


def _adam_math(w, g, m, v):
    m2 = ADAM_B1 * m + (1.0 - ADAM_B1) * g
    v2 = ADAM_B2 * v + (1.0 - ADAM_B2) * (g * g)
    m_hat = m2 / (1.0 - ADAM_B1 ** ADAM_STEP)
    v_hat = v2 / (1.0 - ADAM_B2 ** ADAM_STEP)
    delta = -ADAM_LR * (m_hat / (jnp.sqrt(v_hat) + ADAM_EPS) + ADAM_WD * w)
    return delta, m2, v2


def _pick_rows(rows, cols, unit=16, budget=MIB):
    best = unit
    for t in range(unit, rows + 1, unit):
        if rows % t == 0 and t * cols * 4 <= budget:
            best = t
    assert rows % best == 0
    return best


def _adamw(w, g, m, v, name):
    r, c = w.shape
    tr = _pick_rows(r, c, unit=8)

    def body(w_ref, g_ref, m_ref, v_ref, go_ref, d_ref, mo_ref, vo_ref):
        gv = g_ref[...]
        delta, m2, v2 = _adam_math(w_ref[...], gv, m_ref[...], v_ref[...])
        go_ref[...] = gv
        d_ref[...] = delta
        mo_ref[...] = m2
        vo_ref[...] = v2

    blk = pl.BlockSpec((tr, c), lambda i: (i, 0))
    shp = jax.ShapeDtypeStruct((r, c), F32)
    return pl.pallas_call(
        body, name=name, grid=(r // tr,), in_specs=[blk] * 4, out_specs=[blk] * 4, out_shape=[shp] * 4,
        compiler_params=_cp(("parallel",)),
    )(*[pltpu.with_memory_space_constraint(t, pltpu.HBM) for t in (w, g, m, v)])


def _small_sum_adamw(parts, own, pos, w, m, v, name):
    tr = 48

    def body(pos_ref, p_ref, own_ref, w_ref, m_ref, v_ref, g_ref, d_ref, mo_ref, vo_ref):
        me = 2 * pos_ref[1] + pos_ref[0]
        gv = None
        for k in range(8):
            term = jnp.where(me == k, own_ref[...], p_ref[k])
            gv = term if gv is None else gv + term
        delta, m2, v2 = _adam_math(w_ref[...], gv, m_ref[...], v_ref[...])
        g_ref[...] = gv
        d_ref[...] = delta
        mo_ref[...] = m2
        vo_ref[...] = v2

    blk = pl.BlockSpec((tr, D), lambda i, p: (i, 0))
    shp = jax.ShapeDtypeStruct((SMALL_ROWS, D), F32)
    return pl.pallas_call(
        body, name=name,
        grid_spec=pltpu.PrefetchScalarGridSpec(
            num_scalar_prefetch=1, grid=(SMALL_ROWS // tr,),
            in_specs=[pl.BlockSpec((8, tr, D), lambda i, p: (0, i, 0)), blk, blk, blk, blk],
            out_specs=[blk] * 4),
        out_shape=[shp] * 4,
        compiler_params=_cp(("parallel",)),
    )(pos, parts, own, w, m, v)


ANY = pl.BlockSpec(memory_space=pl.ANY)


def _in_hbm(arrays):
    return [pltpu.with_memory_space_constraint(a, pltpu.HBM) for a in arrays]


def _mesh_pos():
    x, y, c = lax.axis_index("x"), lax.axis_index("y"), lax.axis_index("c")
    chips = [(1 - x, y), (x, 1 - y), (1 - x, 1 - y)]
    return x, y, c, chips


def _place_shard(w, kind, pos, name):
    r, c = w.shape
    tr = _pick_rows(r, c)

    def body(pos_ref, w_ref, o_ref):
        o_ref[...] = w_ref[...].astype(BF16)

    if kind == "stack":
        o_spec = pl.BlockSpec((None, tr, c), lambda i, p: (p[1], i, 0))
        shape = (NSH, r, c)
    else:
        o_spec = pl.BlockSpec((tr, c), lambda i, p: (i, p[1]))
        shape = (r, NSH * c)
    return pl.pallas_call(
        body, name=name,
        grid_spec=pltpu.PrefetchScalarGridSpec(
            num_scalar_prefetch=1, grid=(r // tr,),
            in_specs=[pl.BlockSpec((tr, c), lambda i, p: (i, 0))], out_specs=o_spec),
        out_shape=pltpu.HBM(shape, BF16),
        compiler_params=_cp(("parallel",)),
    )(pos, pltpu.with_memory_space_constraint(w, pltpu.HBM))


SEM = pl.BlockSpec(memory_space=pltpu.SEMAPHORE)
SPLIT_COPY = pltpu.CompilerParams(has_side_effects=pltpu.SideEffectType.DATAFLOW_SIDE_EFFECTING)


def _shard_window(ref, kind, j, h, dims):
    r, c = dims
    rows = pl.ds(pl.multiple_of(h * (r // 2), 16), r // 2)
    if kind == "stack":
        return ref.at[j, rows, :]
    return ref.at[rows, pl.ds(pl.multiple_of(j * c, 128), c)]


def _ici_copy(ref, kind, dims, j, c, sems, idx, to):
    win = _shard_window(ref, kind, j, c, dims)
    return pltpu.make_async_remote_copy(src_ref=win, dst_ref=win, send_sem=sems[0].at[idx], recv_sem=sems[1].at[idx],
                                        device_id=to, device_id_type=MESH_T)


def _gather_start(fulls, kinds, dims, after, name):
    n, na = len(fulls), len(after)

    def body(*refs):
        outs = refs[n + na:2 * n + na]
        send_sems, recv_sems, token = refs[2 * n + na:]
        x, y, c, chips = _mesh_pos()
        for a in range(n):
            for k, chip in enumerate(chips):
                _ici_copy(outs[a], kinds[a], dims[a], 2 * x + y, c, (send_sems, recv_sems), 3 * a + k,
                          (chip[0], chip[1], c)).start()
        token[...] = jnp.zeros_like(token)

    res = pl.pallas_call(
        body, name=name, in_specs=[ANY] * (n + na),
        out_specs=[ANY] * n + [SEM, SEM, pl.BlockSpec(memory_space=pltpu.VMEM)],
        out_shape=[pltpu.HBM(f.shape, BF16) for f in fulls]
        + [pltpu.SemaphoreType.DMA((3 * n,)), pltpu.SemaphoreType.DMA((3 * n,)), jax.ShapeDtypeStruct((8, 128), F32)],
        input_output_aliases={i: i for i in range(n)},
        compiler_params=SPLIT_COPY,
    )(*_in_hbm(fulls), *after)
    return res[:n], res[n], res[n + 1], res[n + 2]


def _gather_wait(fulls, send_sems, recv_sems, kinds, dims, after, name):
    n, na = len(fulls), len(after)

    def body(*refs):
        ssem, rsem = refs[n], refs[n + 1]
        outs = refs[n + 2 + na:]
        x, y, c, chips = _mesh_pos()
        for a in range(n):
            for k, chip in enumerate(chips):
                to = (chip[0], chip[1], c)
                _ici_copy(outs[a], kinds[a], dims[a], 2 * x + y, c, (ssem, rsem), 3 * a + k, to).wait_send()
                _ici_copy(outs[a], kinds[a], dims[a], 2 * chip[0] + chip[1], c, (ssem, rsem), 3 * a + k, to).wait_recv()

    return pl.pallas_call(
        body, name=name, in_specs=[ANY] * n + [SEM, SEM] + [ANY] * na, out_specs=[ANY] * n,
        out_shape=[pltpu.HBM(f.shape, BF16) for f in fulls],
        input_output_aliases={i: i for i in range(n)},
        compiler_params=SPLIT_COPY,
    )(*_in_hbm(fulls), send_sems, recv_sems, *after)


def _gather_forward(fulls, kinds, dims, name):
    n = len(fulls)

    def body(*refs):
        outs = refs[n:2 * n]
        sems = refs[2 * n:]
        x, y, c, chips = _mesh_pos()
        sib = (x, y, 1 - c)
        cps = []
        for a in range(n):
            for k, chip in enumerate(chips):
                cp = _ici_copy(outs[a], kinds[a], dims[a], 2 * chip[0] + chip[1], c, sems, 3 * a + k, sib)
                cp.start()
                cps.append(cp)
        for a in range(n):
            for k, chip in enumerate(chips):
                _ici_copy(outs[a], kinds[a], dims[a], 2 * chip[0] + chip[1], 1 - c, sems, 3 * a + k, sib).wait_recv()
        for cp in cps:
            cp.wait_send()

    return pl.pallas_call(
        body, name=name, in_specs=[ANY] * n, out_specs=[ANY] * n,
        out_shape=[pltpu.HBM(f.shape, BF16) for f in fulls],
        input_output_aliases={i: i for i in range(n)},
        scratch_shapes=[pltpu.SemaphoreType.DMA((3 * n,)), pltpu.SemaphoreType.DMA((3 * n,))],
    )(*_in_hbm(fulls))


def _pair_copy(src, land, a, x, y, c, sems):
    return pltpu.make_async_remote_copy(
        src_ref=src.at[1 - c], dst_ref=land, send_sem=sems[0].at[a], recv_sem=sems[1].at[a],
        device_id=(x, y, 1 - c), device_id_type=MESH_T)


def _pair_start(grads, lands, name):
    n = len(grads)

    def body(*refs):
        srcs, dsts = refs[2 * n:3 * n], refs[3 * n:4 * n]
        send_sems, recv_sems, token = refs[4 * n:]
        x, y, c, _ = _mesh_pos()
        for a in range(n):
            _pair_copy(srcs[a], dsts[a], a, x, y, c, (send_sems, recv_sems)).start()
        token[...] = jnp.zeros_like(token)

    res = pl.pallas_call(
        body, name=name, in_specs=[ANY] * (2 * n),
        out_specs=[ANY] * (2 * n) + [SEM, SEM, pl.BlockSpec(memory_space=pltpu.VMEM)],
        out_shape=[pltpu.HBM(g.shape, F32) for g in grads]
        + [pltpu.HBM(l.shape, F32) for l in lands]
        + [pltpu.SemaphoreType.DMA((n,)), pltpu.SemaphoreType.DMA((n,)), jax.ShapeDtypeStruct((8, 128), F32)],
        input_output_aliases={i: i for i in range(2 * n)},
        compiler_params=SPLIT_COPY,
    )(*_in_hbm(grads), *_in_hbm(lands))
    return res[:n], res[n:2 * n], res[2 * n], res[2 * n + 1], res[2 * n + 2]


def _pair_wait(grads, lands, send_sems, recv_sems, after, name):
    n, na = len(grads), len(after)

    def body(*refs):
        ssem, rsem = refs[2 * n], refs[2 * n + 1]
        outs = refs[2 * n + 2 + na:]
        x, y, c, _ = _mesh_pos()
        for a in range(n):
            cp = _pair_copy(outs[a], outs[n + a], a, x, y, c, (ssem, rsem))
            cp.wait_send()
            cp.wait_recv()

    res = pl.pallas_call(
        body, name=name, in_specs=[ANY] * (2 * n) + [SEM, SEM] + [ANY] * na, out_specs=[ANY] * (2 * n),
        out_shape=[pltpu.HBM(g.shape, F32) for g in grads]
        + [pltpu.HBM(l.shape, F32) for l in lands],
        input_output_aliases={i: i for i in range(2 * n)},
        compiler_params=SPLIT_COPY,
    )(*_in_hbm(grads), *_in_hbm(lands), send_sems, recv_sems, *after)
    return res[:n], res[n:]


def _pair_sum(g, recv, pos, name):
    _, _, rh, c = g.shape
    tr = _pick_rows(rh, c)

    def body(pos_ref, g_ref, r_ref, o_ref):
        o_ref[...] = (g_ref[...] + r_ref[...]).astype(BF16)

    return pl.pallas_call(
        body, name=name,
        grid_spec=pltpu.PrefetchScalarGridSpec(
            num_scalar_prefetch=1, grid=(3, rh // tr),
            in_specs=[pl.BlockSpec((None, None, tr, c), lambda k, r, p: (p[0], p[2 + k], r, 0)),
                      pl.BlockSpec((None, tr, c), lambda k, r, p: (p[2 + k], r, 0))],
            out_specs=pl.BlockSpec((None, tr, c), lambda k, r, p: (k, r, 0))),
        out_shape=pltpu.HBM((3, rh, c), BF16),
        compiler_params=_cp(("parallel", "parallel")),
    )(pos, *_in_hbm([g, recv]))


def _chip_copy(src, land, a, k, chip, c, sems):
    return pltpu.make_async_remote_copy(
        src_ref=src.at[k], dst_ref=land.at[k], send_sem=sems[0].at[3 * a + k],
        recv_sem=sems[1].at[3 * a + k], device_id=(chip[0], chip[1], c), device_id_type=MESH_T)


def _chip_start(psums, lands, name):
    n = len(psums)

    def body(*refs):
        srcs, dsts = refs[2 * n:3 * n], refs[3 * n:4 * n]
        send_sems, recv_sems, token = refs[4 * n:]
        x, y, c, chips = _mesh_pos()
        for a in range(n):
            for k, chip in enumerate(chips):
                _chip_copy(srcs[a], dsts[a], a, k, chip, c, (send_sems, recv_sems)).start()
        token[...] = jnp.zeros_like(token)

    res = pl.pallas_call(
        body, name=name, in_specs=[ANY] * (2 * n),
        out_specs=[ANY] * (2 * n) + [SEM, SEM, pl.BlockSpec(memory_space=pltpu.VMEM)],
        out_shape=[pltpu.HBM(p.shape, BF16) for p in psums]
        + [pltpu.HBM(l.shape, BF16) for l in lands]
        + [pltpu.SemaphoreType.DMA((3 * n,)), pltpu.SemaphoreType.DMA((3 * n,)), jax.ShapeDtypeStruct((8, 128), F32)],
        input_output_aliases={i: i for i in range(2 * n)},
        compiler_params=SPLIT_COPY,
    )(*_in_hbm(psums), *_in_hbm(lands))
    return res[:n], res[n:2 * n], res[2 * n], res[2 * n + 1], res[2 * n + 2]


def _chip_wait(psums, lands, send_sems, recv_sems, after, name):
    n, na = len(psums), len(after)

    def body(*refs):
        ssem, rsem = refs[2 * n], refs[2 * n + 1]
        outs = refs[2 * n + 2 + na:]
        srcs, dsts = outs[:n], outs[n:]
        x, y, c, chips = _mesh_pos()
        for a in range(n):
            for k, chip in enumerate(chips):
                cp = _chip_copy(srcs[a], dsts[a], a, k, chip, c, (ssem, rsem))
                cp.wait_send()
                cp.wait_recv()

    res = pl.pallas_call(
        body, name=name, in_specs=[ANY] * (2 * n) + [SEM, SEM] + [ANY] * na, out_specs=[ANY] * (2 * n),
        out_shape=[pltpu.HBM(p.shape, BF16) for p in psums]
        + [pltpu.HBM(l.shape, BF16) for l in lands],
        input_output_aliases={i: i for i in range(2 * n)},
        compiler_params=SPLIT_COPY,
    )(*_in_hbm(psums), *_in_hbm(lands), send_sems, recv_sems, *after)
    return res[n:]


def _owner_sum(g, recv_a, recv_b, pos, name):
    _, _, rh, c = g.shape
    tr = _pick_rows(rh, c)

    def body(pos_ref, g_ref, ra_ref, rb_ref, o_ref):
        acc = g_ref[...] + ra_ref[...]
        for k in range(3):
            acc = acc + rb_ref[k].astype(F32)
        o_ref[...] = acc

    return pl.pallas_call(
        body, name=name,
        grid_spec=pltpu.PrefetchScalarGridSpec(
            num_scalar_prefetch=1, grid=(rh // tr,),
            in_specs=[pl.BlockSpec((None, None, tr, c), lambda r, p: (p[0], p[1], r, 0)),
                      pl.BlockSpec((None, tr, c), lambda r, p: (p[1], r, 0)),
                      pl.BlockSpec((3, tr, c), lambda r, p: (0, r, 0))],
            out_specs=pl.BlockSpec((None, tr, c), lambda r, p: (p[0], r, 0))),
        out_shape=pltpu.HBM((2, rh, c), F32),
        compiler_params=_cp(("parallel",)),
    )(pos, *_in_hbm([g, recv_a, recv_b]))


def _sibling_allgather(halves, name):
    n = len(halves)

    def body(*refs):
        outs = refs[n:2 * n]
        send_sems, recv_sems = refs[2 * n:]
        x, y, c, _ = _mesh_pos()
        cps = []
        for a in range(n):
            cp = pltpu.make_async_remote_copy(
                src_ref=outs[a].at[c], dst_ref=outs[a].at[c], send_sem=send_sems.at[a], recv_sem=recv_sems.at[a],
                device_id=(x, y, 1 - c), device_id_type=MESH_T)
            cp.start()
            cps.append(cp)
        for a in range(n):
            cps[a].wait_send()
            pltpu.make_async_remote_copy(
                src_ref=outs[a].at[1 - c], dst_ref=outs[a].at[1 - c], send_sem=send_sems.at[a],
                recv_sem=recv_sems.at[a], device_id=(x, y, 1 - c), device_id_type=MESH_T).wait_recv()

    return pl.pallas_call(
        body, name=name, in_specs=[ANY] * n, out_specs=[ANY] * n,
        out_shape=[pltpu.HBM(h.shape, F32) for h in halves],
        input_output_aliases={i: i for i in range(n)},
        scratch_shapes=[pltpu.SemaphoreType.DMA((n,)), pltpu.SemaphoreType.DMA((n,))],
    )(*_in_hbm(halves))


def _peers(x, y, c):
    rel = [(0, 0, 1), (0, 1, 0), (0, 1, 1), (1, 0, 0), (1, 0, 1), (1, 1, 0), (1, 1, 1)]
    return [((1 - x) if dx else x, (1 - y) if dy else y, (1 - c) if dc else c) for dx, dy, dc in rel]


def _small_copy(src, land, k, peer, slot, sems):
    return pltpu.make_async_remote_copy(src_ref=src, dst_ref=land.at[slot], send_sem=sems[0].at[k],
                                        recv_sem=sems[1].at[k], device_id=peer, device_id_type=MESH_T)


def _small_start(part, land, name):
    def body(p_in, l_in, p_ref, l_ref, send_sems, recv_sems, token):
        x, y, c, _ = _mesh_pos()
        for k, peer in enumerate(_peers(x, y, c)):
            _small_copy(p_ref, l_ref, k, peer, 4 * x + 2 * y + c, (send_sems, recv_sems)).start()
        token[...] = jnp.zeros_like(token)

    return pl.pallas_call(
        body, name=name, in_specs=[ANY, ANY],
        out_specs=[ANY, ANY, SEM, SEM, pl.BlockSpec(memory_space=pltpu.VMEM)],
        out_shape=[pltpu.HBM(part.shape, F32), pltpu.HBM(land.shape, F32), pltpu.SemaphoreType.DMA((7,)),
                   pltpu.SemaphoreType.DMA((7,)), jax.ShapeDtypeStruct((8, 128), F32)],
        input_output_aliases={0: 0, 1: 1},
        compiler_params=SPLIT_COPY,
    )(*_in_hbm([part, land]))


def _small_wait(part, land, send_sems, recv_sems, after, name):
    na = len(after)

    def body(*refs):
        ssem, rsem = refs[2], refs[3]
        p_ref, l_ref = refs[4 + na:]
        x, y, c, _ = _mesh_pos()
        for k, peer in enumerate(_peers(x, y, c)):
            cp = _small_copy(p_ref, l_ref, k, peer, 4 * peer[0] + 2 * peer[1] + peer[2], (ssem, rsem))
            cp.wait_send()
            cp.wait_recv()

    return pl.pallas_call(
        body, name=name, in_specs=[ANY, ANY, SEM, SEM] + [ANY] * na, out_specs=[ANY, ANY],
        out_shape=[pltpu.HBM(part.shape, F32), pltpu.HBM(land.shape, F32)],
        input_output_aliases={0: 0, 1: 1},
        compiler_params=SPLIT_COPY,
    )(*_in_hbm([part, land]), send_sems, recv_sems, *after)


def _pack_small(ln1_g, ln1_b, gln_g, gln_b, ln2_g, ln2_b, ln3_g, ln3_b, b_gates, b_s, w_s):
    rows = [ln1_g, ln1_b, gln_g, gln_b, ln2_g, ln2_b, ln3_g, ln3_b]
    rows = [r.reshape(1, D) for r in rows] + [b_gates.reshape(2, D), b_s.reshape(1, D), jnp.zeros((5, D), F32),
                                             w_s.reshape(128, D)]
    return jnp.concatenate(rows, axis=0)


def _unpack_small(p):
    out = [p[i:i + 1] for i in range(8)]
    return out + [p[8:10].reshape(1, 2 * D), p[10:11].reshape(1, 8, BLK), p[16:144].reshape(1, 8, BLK, BLK)]


GROUPS = (("f1g", "f1u", "f1d"), ("w_in",), ("w_ab", "w_gb", "w_out"), ("f2g", "f2u", "f2d"))


def _local_step(x, pos_f, target, P, weights_of, grads_ready, flush, small_ready):
    invf = ROPE_THETA ** (-jnp.arange(0, DH, 2, dtype=F32) / DH)
    invf = jnp.tile(invf, 4).reshape(1, 128)
    b_s_t = P["gmlp_b_s"].T

    W = dict(weights_of(0, []))
    h1, h1b, xh1, rstd1, a1, b1, h1t = _ffn_fwd(x, W["f1g"], W["f1u"], W["f1d"], P["ln1_g"], P["ln1_b"], "ffn1_fwd",
                                                emit_t=True)
    W.update(weights_of(1, [h1b]))
    qkv = _matmul(h1b, W["w_in"], "nn", "proj_qkv", n=3 * ATT_W, b_col0=0, tm=1024, tn=ATT_W)
    z = _matmul(h1b, W["w_in"], "nn", "proj_z", n=2 * GW, b_col0=3 * ATT_W, tm=S, tn=512)
    gl = _matmul(h1b, W["w_in"], "nn", "proj_gates", n=2 * D, b_col0=3 * ATT_W + 2 * GW, tm=S, tn=512)
    qkv_c = _rope_fwd(qkv, pos_f, invf, "rope_fwd")
    og = [_attn_fwd(gi, qkv_c[gi], "attn_fwd_g%d" % gi) for gi in range(NG)]
    y_attn, y_attn_t, lse = _attn_combine([o for o, _ in og], [l for _, l in og], "attn_combine")
    y_gmlp, y_gmlp_t = _gmlp_fwd(z, P["gmlp_ln_g"], P["gmlp_ln_b"], P["gmlp_w_s"], b_s_t, "gmlp_fwd")
    W.update(weights_of(2, [y_gmlp]))
    br_a = _matmul(y_attn, W["w_ab"], "nn", "branch_attn", n=D, tm=1024, tn=D)
    br_b = _matmul(y_gmlp, W["w_gb"], "nn", "branch_gmlp", n=D, tm=1024, tn=D)
    merged, merged_t = _merge_fwd(br_a, br_b, gl, P["b_gates"], "merge_fwd")
    mix = _matmul(merged, W["w_out"], "nn", "mix_out", n=D, tm=1024, tn=D)
    h2, h2b, xh2, rstd2 = _resid_ln(h1, mix, P["ln2_g"], P["ln2_b"], "resid_ln2")
    W.update(weights_of(3, [h2b]))
    y, _, xh3, rstd3, a2, b2 = _ffn_fwd(h2, W["f2g"], W["f2u"], W["f2d"], P["ln3_g"], P["ln3_b"], "ffn2_fwd")

    dr3, dg3, db3, loss = _ln_bwd(y, xh3, rstd3, P["ln3_g"], "loss_ln3_bwd", target=target)
    g_f2g, g_f2u, g_f2d, dh2 = _ffn_bwd(dr3, h2b, a2, b2, W["f2g"], W["f2u"], W["f2d"], "ffn2_bwd")
    tok = grads_ready(3, dict(f2g=g_f2g, f2u=g_f2u, f2d=g_f2d))
    dr2, dg2, db2 = _ln_bwd(dh2, xh2, rstd2, P["ln2_g"], "ln2_bwd", after=tok)
    g_wout = _wgrad(merged_t, dr2, 128, D, "dw_out", row_sharded=True)
    dmerged = _matmul(dr2, W["w_out"], "nt", "dmerged", n=D, tm=1024, tn=D)
    dab, dbb, dglb, dbg = _merge_bwd(dmerged, br_a, br_b, gl, P["b_gates"], "merge_bwd")
    tok = flush([dab])
    g_wab = _wgrad(y_attn_t, dab, GRP_W // 2, 256, "dw_attn_branch", row_sharded=False, after=tok)
    g_wgb = _wgrad(y_gmlp_t, dbb, 128, D, "dw_gmlp_branch", row_sharded=True)
    tok = grads_ready(2, dict(w_ab=g_wab, w_gb=g_wgb, w_out=g_wout))
    dy_attn = _matmul(dab, W["w_ab"], "nt", "dy_attn", n=GRP_W, tm=1024, tn=GRP_W, after=tok)
    dy_gmlp = _matmul(dbb, W["w_gb"], "nt", "dy_gmlp", n=GW, tm=1024, tn=GW)
    dzb, dws, dbs_t, dgln_g, dgln_b = _gmlp_bwd(z, dy_gmlp, P["gmlp_ln_g"], P["gmlp_ln_b"], P["gmlp_w_s"], b_s_t,
                                                 "gmlp_bwd")
    cls = _class_order([dy_attn, y_attn, lse], "attn_class_order")
    dqkv_c = []
    for gi in range(NG):
        dy_c, y_c, lse_c = [t[None] if gi == 0 else cls[2 * a + gi - 1] for a, t in enumerate((dy_attn, y_attn, lse))]
        dqkv_c.append(_attn_bwd(gi, qkv_c[gi], dy_c, y_c, lse_c, "attn_bwd_g%d" % gi))
    dqkvb = _rope_bwd(dqkv_c, pos_f, invf, "rope_bwd")
    dproj = jnp.concatenate([dqkvb, dzb, dglb], axis=1)
    tok = flush([dproj])
    g_win = _wgrad(h1t, dproj, D // 2, IN_SH, "dw_in", row_sharded=False, after=tok)
    tok = grads_ready(1, dict(w_in=g_win))
    dh1 = _matmul(dproj, W["w_in"], "nt", "dh1", n=D, tn=D, tk=IN_SH, add=dr2, add_scale=ALPHA, after=tok)
    dr1, dg1, db1 = _ln_bwd(dh1, xh1, rstd1, P["ln1_g"], "ln1_bwd")
    tok = flush([dr1])
    tok = tok + small_ready(_pack_small(dg1, db1, dgln_g, dgln_b, dg2, db2, dg3, db3, dbg, dbs_t.T, dws))
    g_f1g, g_f1u, g_f1d, dx = _ffn_bwd(dr1, x.astype(BF16), a1, b1, W["f1g"], W["f1u"], W["f1d"], "ffn1_bwd",
                                       after=tok)
    grads_ready(0, dict(f1g=g_f1g, f1u=g_f1u, f1d=g_f1d))
    flush([dx])
    return loss, dx


BIG = ("f1g", "f1u", "f1d", "w_in", "w_ab", "w_gb", "w_out", "f2g", "f2u", "f2d")
TRANSPOSED = ("f1g", "f1u", "f2g", "f2u")
KIND = dict(f1g="stack", f1u="stack", f1d="stack", w_in="col", w_ab="col", w_gb="stack", w_out="stack",
            f2g="stack", f2u="stack", f2d="stack")


def kernel(x, positions, ffn1_w_gate, ffn1_w_up, ffn1_w_down, ln1_g, ln1_b, w_in, b_gates, gmlp_ln_g, gmlp_ln_b, gmlp_w_s, gmlp_b_s, w_attn_branch, w_gmlp_branch, w_out, ln2_g, ln2_b, ffn2_w_gate, ffn2_w_up, ffn2_w_down, ln3_g, ln3_b, loss_target, m_ffn1_w_gate, m_ffn1_w_up, m_ffn1_w_down, m_ln1_g, m_ln1_b, m_w_in, m_b_gates, m_gmlp_ln_g, m_gmlp_ln_b, m_gmlp_w_s, m_gmlp_b_s, m_w_attn_branch, m_w_gmlp_branch, m_w_out, m_ln2_g, m_ln2_b, m_ffn2_w_gate, m_ffn2_w_up, m_ffn2_w_down, m_ln3_g, m_ln3_b, v_ffn1_w_gate, v_ffn1_w_up, v_ffn1_w_down, v_ln1_g, v_ln1_b, v_w_in, v_b_gates, v_gmlp_ln_g, v_gmlp_ln_b, v_gmlp_w_s, v_gmlp_b_s, v_w_attn_branch, v_w_gmlp_branch, v_w_out, v_ln2_g, v_ln2_b, v_ffn2_w_gate, v_ffn2_w_up, v_ffn2_w_down, v_ln3_g, v_ln3_b):
    cx, cy, cc = lax.axis_index("x"), lax.axis_index("y"), lax.axis_index("c")
    pos = jnp.stack([cc, 2 * cx + cy, 2 * (1 - cx) + cy, 2 * cx + 1 - cy, 2 * (1 - cx) + 1 - cy]).astype(jnp.int32)

    w_sh = dict(f1g=ffn1_w_gate, f1u=ffn1_w_up, f1d=ffn1_w_down, w_in=w_in, w_ab=w_attn_branch,
                w_gb=w_gmlp_branch, w_out=w_out, f2g=ffn2_w_gate, f2u=ffn2_w_up, f2d=ffn2_w_down)
    m_sh = dict(f1g=m_ffn1_w_gate, f1u=m_ffn1_w_up, f1d=m_ffn1_w_down, w_in=m_w_in, w_ab=m_w_attn_branch,
                w_gb=m_w_gmlp_branch, w_out=m_w_out, f2g=m_ffn2_w_gate, f2u=m_ffn2_w_up, f2d=m_ffn2_w_down)
    v_sh = dict(f1g=v_ffn1_w_gate, f1u=v_ffn1_w_up, f1d=v_ffn1_w_down, w_in=v_w_in, w_ab=v_w_attn_branch,
                w_gb=v_w_gmlp_branch, w_out=v_w_out, f2g=v_ffn2_w_gate, f2u=v_ffn2_w_up, f2d=v_ffn2_w_down)
    w_sh = {k: (v[0].T if k in TRANSPOSED else v[0]) for k, v in w_sh.items()}
    m_sh = {k: (v[0].T if k in TRANSPOSED else v[0]) for k, v in m_sh.items()}
    v_sh = {k: (v[0].T if k in TRANSPOSED else v[0]) for k, v in v_sh.items()}

    started, tokens = [], []
    for gi, names in enumerate(GROUPS):
        placed = [_place_shard(w_sh[k], KIND[k], pos, "place_" + k) for k in names]
        fulls, ssem, rsem, token = _gather_start(placed, [KIND[k] for k in names], [w_sh[k].shape for k in names],
                                                 tokens[-1:], "gather_start_g%d" % gi)
        started.append((fulls, ssem, rsem))
        tokens.append(token)

    def weights_of(gi, after):
        names = GROUPS[gi]
        kinds, dims = [KIND[k] for k in names], [w_sh[k].shape for k in names]
        fulls, ssem, rsem = started[gi]
        fulls = _gather_wait(fulls, ssem, rsem, kinds, dims, list(after) + (tokens if gi == 0 else []),
                             "gather_wait_g%d" % gi)
        fulls = _gather_forward(fulls, kinds, dims, "gather_forward_g%d" % gi)
        return {k: (f.reshape(D, D) if k in ("w_gb", "w_out") else f) for k, f in zip(names, fulls)}

    pending, inflight = [], {}

    def grads_ready(gi, gd):
        grads = [gd[k] for k in GROUPS[gi]]
        lands = [lax.empty(g.shape[1:], F32) for g in grads]
        grads, lands, ssem, rsem, token = _pair_start(grads, lands, "rs_pair_start_g%d" % gi)
        pending.append((gi, grads, lands, ssem, rsem))
        return [token]

    def flush(after):
        gi, grads, lands, ssem, rsem = pending.pop()
        names = GROUPS[gi]
        grads, recv_a = _pair_wait(grads, lands, ssem, rsem, after, "rs_pair_wait_g%d" % gi)
        psums = [_pair_sum(g, r, pos, "rs_pair_sum_" + k) for g, r, k in zip(grads, recv_a, names)]
        lands = [lax.empty((3,) + p.shape[1:], BF16) for p in psums]
        psums, lands, ssem, rsem, token = _chip_start(psums, lands, "rs_chip_start_g%d" % gi)
        inflight[gi] = (grads, recv_a, psums, lands, ssem, rsem, token)
        return [token]

    P = dict(ln1_g=ln1_g, ln1_b=ln1_b, ln2_g=ln2_g, ln2_b=ln2_b, ln3_g=ln3_g, ln3_b=ln3_b, b_gates=b_gates,
             gmlp_ln_g=gmlp_ln_g, gmlp_ln_b=gmlp_ln_b, gmlp_w_s=gmlp_w_s[0], gmlp_b_s=gmlp_b_s[0])
    pos_f = positions.reshape(S, 1).astype(F32)
    small_state = []

    def small_ready(packed):
        land = jnp.zeros((8, SMALL_ROWS, D), F32)
        packed, land, ssem, rsem, token = _small_start(packed, land, "small_start")
        small_state.append((packed, land, ssem, rsem))
        return [token]

    loss_part, dx = _local_step(x[0], pos_f, loss_target[0], P, weights_of, grads_ready, flush, small_ready)
    loss = lax.psum(loss_part[0, 0], ("x", "y", "c"))

    g_out, d_out, m_out, v_out = {}, {}, {}, {}

    def finish(gi, after):
        grads, recv_a, psums, lands, ssem, rsem, token = inflight[gi]
        recv_b = _chip_wait(psums, lands, ssem, rsem, after + [inflight[0][6]], "rs_chip_wait_g%d" % gi)
        halves = [_owner_sum(g, ra, rb, pos, "rs_owner_sum_" + k)
                  for g, ra, rb, k in zip(grads, recv_a, recv_b, GROUPS[gi])]
        reduced = _sibling_allgather(halves, "rs_sibling_allgather_g%d" % gi)
        for k, gfull in zip(GROUPS[gi], reduced):
            res = _adamw(w_sh[k], gfull.reshape(w_sh[k].shape), m_sh[k], v_sh[k], "adamw_" + k)
            after = [res[1]]
            if k in TRANSPOSED:
                res = [r.T for r in res]
            g_out[k], d_out[k], m_out[k], v_out[k] = [r[None] for r in res]
        return after

    after = []
    for gi in (3, 2, 1):
        after = finish(gi, after)

    small, parts = _small_wait(*small_state[0], after, "small_wait")
    sp = (ln1_g, ln1_b, gmlp_ln_g, gmlp_ln_b, ln2_g, ln2_b, ln3_g, ln3_b, b_gates, gmlp_b_s, gmlp_w_s)
    sm = (m_ln1_g, m_ln1_b, m_gmlp_ln_g, m_gmlp_ln_b, m_ln2_g, m_ln2_b, m_ln3_g, m_ln3_b, m_b_gates, m_gmlp_b_s,
          m_gmlp_w_s)
    sv = (v_ln1_g, v_ln1_b, v_gmlp_ln_g, v_gmlp_ln_b, v_ln2_g, v_ln2_b, v_ln3_g, v_ln3_b, v_b_gates, v_gmlp_b_s,
          v_gmlp_w_s)
    sg, sd, smn, svn = _small_sum_adamw(parts, small, pos, _pack_small(*sp), _pack_small(*sm), _pack_small(*sv),
                                        "small_adamw")
    names = ("ln1_g", "ln1_b", "gmlp_ln_g", "gmlp_ln_b", "ln2_g", "ln2_b", "ln3_g", "ln3_b", "b_gates", "gmlp_b_s",
             "gmlp_w_s")
    for dst, packed in ((g_out, sg), (d_out, sd), (m_out, smn), (v_out, svn)):
        for nm, val in zip(names, _unpack_small(packed)):
            dst[nm] = val
    finish(0, [sg])

    order = ("f1g", "f1u", "f1d", "ln1_g", "ln1_b", "w_in", "b_gates", "gmlp_ln_g", "gmlp_ln_b", "gmlp_w_s", "gmlp_b_s",
             "w_ab", "w_gb", "w_out", "ln2_g", "ln2_b", "f2g", "f2u", "f2d", "ln3_g", "ln3_b")
    outs = [loss, dx[None]]
    for dst in (g_out, d_out, m_out, v_out):
        outs += [dst[k] for k in order]
    return tuple(outs)
```

```python
import functools
import math

import jax
import jax.numpy as jnp
from jax import lax
from jax.experimental import pallas as pl
from jax.experimental.pallas import tpu as pltpu

F32 = jnp.float32
BF16 = jnp.bfloat16

S = 2048
D = 1024
NSH = 4
FSH = 704
ATT_W = 1536
GRP_W = 512
NG = 3
NH = 8
DH = 64
BLK = 128
NBLK = S // BLK
GW = 1024
IN_W = 8704
IN_SH = IN_W // NSH
ALPHA = 2.0 ** 0.25
LN_EPS = 1e-5
ROPE_THETA = 10000.0
DILATIONS = (1, 4, 16)
ADAM_LR, ADAM_B1, ADAM_B2, ADAM_EPS, ADAM_WD, ADAM_STEP = 0.001, 0.9, 0.999, 1e-08, 0.01, 10
SMALL_ROWS = 144
MESH_T = pl.DeviceIdType.MESH
MIB = 1024 * 1024
NEG_INF = float("-inf")


def _cp(sem, vmem_mib=48):
    return pltpu.CompilerParams(dimension_semantics=sem, vmem_limit_bytes=vmem_mib * MIB)


def _ln_stats(r):
    mu = jnp.mean(r, axis=-1, keepdims=True)
    xc = r - mu
    var = jnp.mean(xc * xc, axis=-1, keepdims=True)
    rstd = lax.rsqrt(var + LN_EPS)
    return xc * rstd, rstd


def _ln_dx(dxh, xh, rstd):
    m1 = jnp.mean(dxh, axis=-1, keepdims=True)
    m2 = jnp.mean(dxh * xh, axis=-1, keepdims=True)
    return rstd * (dxh - m1 - xh * m2)


def _dot_nt(a, b):
    return lax.dot_general(a, b, (((1,), (1,)), ((), ())), preferred_element_type=F32)


def _dot_tn(a, b):
    return lax.dot_general(a, b, (((0,), (0,)), ((), ())), preferred_element_type=F32)


def _dot(a, b):
    return jnp.dot(a, b, preferred_element_type=F32)


def _ffn_fwd(xin, wgt, wut, wd, ln_g, ln_b, name, emit_t=False):
    tm = 512

    def body(x_ref, wg_ref, wu_ref, wd_ref, g_ref, b_ref, *rest):
        if emit_t:
            h_ref, hb_ref, xh_ref, rstd_ref, a_ref, bb_ref, ht_ref, acc_ref = rest
        else:
            h_ref, hb_ref, xh_ref, rstd_ref, a_ref, bb_ref, acc_ref = rest
        j = pl.program_id(1)
        xb = x_ref[...].astype(BF16)
        a = _dot_nt(xb, wg_ref[...])
        b = _dot_nt(xb, wu_ref[...])
        a_ref[...] = a.astype(BF16)
        bb_ref[...] = b.astype(BF16)
        s = (a * jax.nn.sigmoid(a)) * b
        f = _dot(s.astype(BF16), wd_ref[...])

        @pl.when(j == 0)
        def _():
            acc_ref[...] = f

        @pl.when(j > 0)
        def _():
            acc_ref[...] += f

        @pl.when(j == NSH - 1)
        def _():
            r = ALPHA * x_ref[...] + 0.5 * acc_ref[...]
            xh, rstd = _ln_stats(r)
            h = xh * g_ref[...] + b_ref[...]
            h_ref[...] = h
            hb_ref[...] = h.astype(BF16)
            xh_ref[...] = xh
            rstd_ref[...] = rstd
            if emit_t:
                ht_ref[...] = h.T.astype(BF16)

    row = pl.BlockSpec((tm, D), lambda i, j: (i, 0))
    vec = pl.BlockSpec((1, D), lambda i, j: (0, 0))
    wsp = pl.BlockSpec((None, FSH, D), lambda i, j: (j, 0, 0))
    ab = pl.BlockSpec((None, tm, FSH), lambda i, j: (j, i, 0))
    out_specs = [row, row, row, pl.BlockSpec((tm, 1), lambda i, j: (i, 0)), ab, ab]
    out_shape = [jax.ShapeDtypeStruct((S, D), F32), jax.ShapeDtypeStruct((S, D), BF16),
                 jax.ShapeDtypeStruct((S, D), F32), jax.ShapeDtypeStruct((S, 1), F32),
                 jax.ShapeDtypeStruct((NSH, S, FSH), BF16), jax.ShapeDtypeStruct((NSH, S, FSH), BF16)]
    if emit_t:
        out_specs.append(pl.BlockSpec((D, tm), lambda i, j: (0, i)))
        out_shape.append(jax.ShapeDtypeStruct((D, S), BF16))
    return pl.pallas_call(
        body, name=name, grid=(S // tm, NSH),
        in_specs=[row, wsp, wsp, wsp, vec, vec], out_specs=out_specs, out_shape=out_shape,
        scratch_shapes=[pltpu.VMEM((tm, D), F32)],
        compiler_params=_cp(("parallel", "arbitrary")),
    )(xin, wgt, wut, wd, ln_g, ln_b)


def _ffn_bwd(dr, xin_b, a, b, wgt, wut, wd, name, after=()):
    tm = 512
    ni = S // tm
    hr = FSH // 2

    def body(dr_ref, a_ref, b_ref, wg_ref, wu_ref, wd_ref, x_hbm, *rest):
        dwg_hbm, dwu_hbm, dwd_hbm, dx_hbm, dx_acc, da_all, db_all, s_all, df_all, x_all, res_buf, sems = rest[len(after):]
        j = pl.program_id(0)
        i = pl.program_id(1)
        rows = pl.ds(pl.multiple_of(i * tm, tm), tm)

        @pl.when(jnp.logical_and(j == 0, i == 0))
        def _():
            cp = pltpu.make_async_copy(x_hbm, x_all, sems.at[0])
            cp.start()
            cp.wait()

        drv = dr_ref[...]
        df = (0.5 * drv).astype(BF16)

        @pl.when(j == 0)
        def _():
            df_all[rows, :] = df

        ds = jnp.concatenate([_dot_nt(df, wd_ref[0:384, :]), _dot_nt(df, wd_ref[384:FSH, :])], axis=1)
        av = a_ref[...].astype(F32)
        bv = b_ref[...].astype(F32)
        sig = jax.nn.sigmoid(av)
        sl = av * sig
        da = (ds * bv * (sig * (1.0 + av * (1.0 - sig)))).astype(BF16)
        db = (ds * sl).astype(BF16)
        da_all[rows, :] = da
        db_all[rows, :] = db
        s_all[rows, :] = (sl * bv).astype(BF16)
        dx = _dot(da, wg_ref[...]) + _dot(db, wu_ref[...])

        @pl.when(j == 0)
        def _():
            dx_acc[rows, :] = ALPHA * drv + dx

        @pl.when(j > 0)
        def _():
            dx_acc[rows, :] += dx

        @pl.when(i == ni - 1)
        def _():
            copies = []
            for n, (lhs, rhs, out) in enumerate(((da_all, x_all, dwg_hbm), (db_all, x_all, dwu_hbm),
                                                 (s_all, df_all, dwd_hbm))):
                slot = n % 2
                if n >= 2:
                    for cp in copies[2 * (n - 2): 2 * (n - 2) + 2]:
                        cp.wait()
                res_buf[slot] = _dot_tn(lhs[...], rhs[...])
                for h in range(2):
                    cp = pltpu.make_async_copy(res_buf.at[slot, pl.ds(h * hr, hr), :], out.at[h, j],
                                               sems.at[1 + 2 * slot + h])
                    cp.start()
                    copies.append(cp)
            for cp in copies[2:]:
                cp.wait()

        @pl.when(jnp.logical_and(j == NSH - 1, i == ni - 1))
        def _():
            cp = pltpu.make_async_copy(dx_acc, dx_hbm, sems.at[0])
            cp.start()
            cp.wait()

    row = pl.BlockSpec((tm, D), lambda j, i: (i, 0))
    wsp = pl.BlockSpec((None, FSH, D), lambda j, i: (j, 0, 0))
    ab = pl.BlockSpec((None, tm, FSH), lambda j, i: (j, i, 0))
    dwshape = jax.ShapeDtypeStruct((2, NSH, hr, D), F32)
    return pl.pallas_call(
        body, name=name, grid=(NSH, ni),
        in_specs=[row, ab, ab, wsp, wsp, wsp, ANY] + [ANY] * len(after),
        out_specs=[ANY, ANY, ANY, ANY],
        out_shape=[dwshape, dwshape, dwshape, jax.ShapeDtypeStruct((S, D), F32)],
        scratch_shapes=[pltpu.VMEM((S, D), F32), pltpu.VMEM((S, FSH), BF16), pltpu.VMEM((S, FSH), BF16),
                        pltpu.VMEM((S, FSH), BF16), pltpu.VMEM((S, D), BF16), pltpu.VMEM((S, D), BF16),
                        pltpu.VMEM((2, FSH, D), F32), pltpu.SemaphoreType.DMA((5,))],
        compiler_params=_cp(("arbitrary", "arbitrary"), vmem_mib=58),
    )(dr, a, b, wgt, wut, wd, xin_b, *after)


def _matmul(a, b, mode, name, *, n, tm=512, tn=512, tk=None, b_col0=0, add=None, add_scale=1.0, out_dtype=F32,
            after=()):
    m, ka = a.shape
    tk = ka if tk is None else tk
    nk = ka // tk
    assert m % tm == 0 and n % tn == 0 and ka % tk == 0 and b_col0 % tn == 0
    off = b_col0 // tn
    na = len(after)

    def body(*refs):
        refs = refs[na:]
        if add is None:
            a_ref, b_ref, o_ref = refs[:3]
            add_ref = None
            rest = refs[3:]
        else:
            a_ref, b_ref, add_ref, o_ref = refs[:4]
            rest = refs[4:]
        k = pl.program_id(2)
        av = a_ref[...].astype(BF16)
        bv = b_ref[...].astype(BF16)
        p = _dot(av, bv) if mode == "nn" else _dot_nt(av, bv)

        def finish(acc):
            if add_ref is not None:
                acc = acc + add_scale * add_ref[...]
            o_ref[...] = acc.astype(out_dtype)

        if nk == 1:
            finish(p)
        else:
            acc_ref = rest[0]

            @pl.when(k == 0)
            def _():
                acc_ref[...] = p

            @pl.when(k > 0)
            def _():
                acc_ref[...] += p

            @pl.when(k == nk - 1)
            def _():
                finish(acc_ref[...])

    a_spec = pl.BlockSpec((tm, tk), lambda i, j, k: (i, k))
    if mode == "nn":
        b_spec = pl.BlockSpec((tk, tn), lambda i, j, k: (k, j + off))
    else:
        b_spec = pl.BlockSpec((tn, tk), lambda i, j, k: (j, k))
    o_spec = pl.BlockSpec((tm, tn), lambda i, j, k: (i, j))
    in_specs = [pl.BlockSpec(memory_space=pl.ANY)] * na + [a_spec, b_spec] + ([o_spec] if add is not None else [])
    args = tuple(after) + (a, b) + ((add,) if add is not None else ())
    return pl.pallas_call(
        body, name=name, grid=(m // tm, n // tn, nk),
        in_specs=in_specs, out_specs=o_spec,
        out_shape=jax.ShapeDtypeStruct((m, n), out_dtype),
        scratch_shapes=[pltpu.VMEM((tm, tn), F32)] if nk > 1 else [],
        compiler_params=_cp(("parallel", "parallel", "arbitrary")),
    )(*args)


def _wgrad(xt, y, rh, c, name, row_sharded, after=()):
    na = len(after)
    if row_sharded:
        def body(x_ref, y_ref, *rest):
            o_ref = rest[na]
            res = _dot(x_ref[...], y_ref[...].astype(BF16))
            for j in range(NSH):
                for h in range(2):
                    o_ref[h, j] = res[(2 * j + h) * rh:(2 * j + h + 1) * rh, :]

        grid = (1,)
        in_specs = [pl.BlockSpec((2 * NSH * rh, S), lambda g: (0, 0)), pl.BlockSpec((S, c), lambda g: (0, 0))]
        out_specs = pl.BlockSpec((2, NSH, rh, c), lambda g: (0, 0, 0, 0))
        sem = ("arbitrary",)
    else:
        def body(x_ref, y_ref, *rest):
            rest[na][...] = _dot(x_ref[...], y_ref[...].astype(BF16))

        grid = (2, NSH)
        in_specs = [pl.BlockSpec((rh, S), lambda h, j: (h, 0)), pl.BlockSpec((S, c), lambda h, j: (0, j))]
        out_specs = pl.BlockSpec((None, None, rh, c), lambda h, j: (h, j, 0, 0))
        sem = ("parallel", "parallel")
    return pl.pallas_call(
        body, name=name, grid=grid, in_specs=in_specs + [pl.BlockSpec(memory_space=pl.ANY)] * na, out_specs=out_specs,
        out_shape=jax.ShapeDtypeStruct((2, NSH, rh, c), F32),
        compiler_params=_cp(sem, vmem_mib=56),
    )(xt, y, *after)


def _resid_ln(res, f, ln_g, ln_b, name):
    tm = 512

    def body(res_ref, f_ref, g_ref, b_ref, h_ref, hb_ref, xh_ref, rstd_ref):
        r = ALPHA * res_ref[...] + f_ref[...]
        xh, rstd = _ln_stats(r)
        h = xh * g_ref[...] + b_ref[...]
        h_ref[...] = h
        hb_ref[...] = h.astype(BF16)
        xh_ref[...] = xh
        rstd_ref[...] = rstd

    row = pl.BlockSpec((tm, D), lambda i: (i, 0))
    vec = pl.BlockSpec((1, D), lambda i: (0, 0))
    return pl.pallas_call(
        body, name=name, grid=(S // tm,),
        in_specs=[row, row, vec, vec],
        out_specs=[row, row, row, pl.BlockSpec((tm, 1), lambda i: (i, 0))],
        out_shape=[jax.ShapeDtypeStruct((S, D), F32), jax.ShapeDtypeStruct((S, D), BF16),
                   jax.ShapeDtypeStruct((S, D), F32), jax.ShapeDtypeStruct((S, 1), F32)],
        compiler_params=_cp(("parallel",)),
    )(res, f, ln_g, ln_b)


def _ln_bwd(dout, xh, rstd, ln_g, name, target=None, after=()):
    tm = 512
    with_loss = target is not None
    na = len(after)

    def body(*refs):
        refs = refs[na:]
        if with_loss:
            y_ref, t_ref, xh_ref, rstd_ref, g_ref, dr_ref, dg_ref, db_ref, loss_ref = refs
            err = y_ref[...] - t_ref[...]
            dy = err * (1.0 / D)
        else:
            y_ref, xh_ref, rstd_ref, g_ref, dr_ref, dg_ref, db_ref = refs
            dy = y_ref[...]
        i = pl.program_id(0)
        xh = xh_ref[...]
        dr_ref[...] = _ln_dx(dy * g_ref[...], xh, rstd_ref[...])
        dg = jnp.sum(dy * xh, axis=0, keepdims=True)
        db = jnp.sum(dy, axis=0, keepdims=True)

        @pl.when(i == 0)
        def _():
            dg_ref[...] = dg
            db_ref[...] = db

        @pl.when(i > 0)
        def _():
            dg_ref[...] += dg
            db_ref[...] += db

        if with_loss:
            part = 0.5 * jnp.sum(jnp.mean(err * err, axis=-1, keepdims=True), axis=0, keepdims=True)
            part = jnp.broadcast_to(part, (8, 128))

            @pl.when(i == 0)
            def _():
                loss_ref[...] = part

            @pl.when(i > 0)
            def _():
                loss_ref[...] += part

    row = pl.BlockSpec((tm, D), lambda i: (i, 0))
    vec = pl.BlockSpec((1, D), lambda i: (0, 0))
    col = pl.BlockSpec((tm, 1), lambda i: (i, 0))
    in_specs = [pl.BlockSpec(memory_space=pl.ANY)] * na + [row] + ([row] if with_loss else []) + [row, col, vec]
    out_specs = [row, vec, vec] + ([pl.BlockSpec((8, 128), lambda i: (0, 0))] if with_loss else [])
    out_shape = [jax.ShapeDtypeStruct((S, D), F32), jax.ShapeDtypeStruct((1, D), F32),
                 jax.ShapeDtypeStruct((1, D), F32)] + ([jax.ShapeDtypeStruct((8, 128), F32)] if with_loss else [])
    args = tuple(after) + (dout,) + ((target,) if with_loss else ()) + (xh, rstd, ln_g)
    return pl.pallas_call(
        body, name=name, grid=(S // tm,), in_specs=in_specs, out_specs=out_specs, out_shape=out_shape,
        compiler_params=_cp(("arbitrary",)),
    )(*args)


ROPE_TM = 256


def _rope_tables(pos_ref, invf_ref, sign):
    ang = pos_ref[...] * invf_ref[...]
    lane = lax.broadcasted_iota(jnp.int32, ang.shape, 1)
    first = (lane % DH) < (DH // 2)
    sinv = jnp.sin(ang) * sign
    return first, jnp.cos(ang), jnp.where(first, -sinv, sinv)


def _rotate(x, first, cosf, sinf):
    return x * cosf + jnp.where(first, pltpu.roll(x, 96, 1), pltpu.roll(x, 32, 1)) * sinf


def _proj_qkv_rope(hb, w_in, pos_f, invf, name):
    tm = 2 * ROPE_TM

    def body(h_ref, w_ref, pos_ref, invf_ref, o0_ref, o1_ref, o2_ref, buf_ref):
        rot = pl.program_id(1) < 2
        first, cosf, sinf = _rope_tables(pos_ref, invf_ref, 1.0)
        cosf = jnp.where(rot, cosf, 1.0)
        sinf = jnp.where(rot, sinf, 0.0)
        acc = _dot(h_ref[...], w_ref[...])
        for gi, (d, o_ref) in enumerate(zip(DILATIONS, (o0_ref, o1_ref, o2_ref))):
            for ch in range(GRP_W // 128):
                cols = slice(ch * 128, (ch + 1) * 128)
                x = _rotate(acc[:, gi * GRP_W + ch * 128: gi * GRP_W + (ch + 1) * 128], first, cosf, sinf)
                if d == 1:
                    o_ref[0, :, cols] = x.astype(BF16)
                else:
                    buf_ref[...] = x
                    for r in range(d):
                        o_ref[r, :, cols] = buf_ref[pl.ds(r, tm // d, stride=d), :].astype(BF16)

    return pl.pallas_call(
        body, name=name, grid=(S // tm, 3),
        in_specs=[pl.BlockSpec((tm, D), lambda i, s: (i, 0)), pl.BlockSpec((D, ATT_W), lambda i, s: (0, s)),
                  pl.BlockSpec((tm, 1), lambda i, s: (i, 0)), pl.BlockSpec((1, 128), lambda i, s: (0, 0))],
        out_specs=[pl.BlockSpec((d, tm // d, GRP_W), lambda i, s: (0, i, s)) for d in DILATIONS],
        out_shape=[jax.ShapeDtypeStruct((d, S // d, 3 * GRP_W), BF16) for d in DILATIONS],
        scratch_shapes=[pltpu.VMEM((tm, 128), F32)],
        compiler_params=_cp(("parallel", "parallel")),
    )(hb, w_in, pos_f, invf)


def _rope_bwd(dqkv_c, pos_f, invf, name):
    tm = ROPE_TM

    def body(*refs):
        g_refs, (pos_ref, invf_ref, o_ref, buf_ref) = refs[:9], refs[9:]
        first, cosf, sinf = _rope_tables(pos_ref, invf_ref, -1.0)
        for sec in range(3):
            for gi, d in enumerate(DILATIONS):
                g_ref = g_refs[3 * gi + sec]
                for ch in range(GRP_W // 128):
                    cols = slice(ch * 128, (ch + 1) * 128)
                    if d == 1:
                        x = g_ref[0, :, cols]
                    else:
                        for r in range(d):
                            buf_ref[pl.ds(r, tm // d, stride=d), :] = g_ref[r, :, cols]
                        x = buf_ref[...]
                    if sec < 2:
                        x = _rotate(x, first, cosf, sinf)
                    dst = sec * ATT_W + gi * GRP_W + ch * 128
                    o_ref[:, dst:dst + 128] = x.astype(BF16)

    g_specs = [pl.BlockSpec((d, tm // d, GRP_W), lambda i: (0, i, 0)) for d in DILATIONS for _ in range(3)]
    return pl.pallas_call(
        body, name=name, grid=(S // tm,),
        in_specs=g_specs + [pl.BlockSpec((tm, 1), lambda i: (i, 0)), pl.BlockSpec((1, 128), lambda i: (0, 0))],
        out_specs=pl.BlockSpec((tm, 3 * ATT_W), lambda i: (i, 0)),
        out_shape=jax.ShapeDtypeStruct((S, 3 * ATT_W), BF16),
        scratch_shapes=[pltpu.VMEM((tm, 128), F32)],
        compiler_params=_cp(("parallel",)),
    )(*[g for grp in dqkv_c for g in grp], pos_f, invf)


def _class_order(ts, name):
    tm = ROPE_TM
    n = len(ts)

    def body(*refs):
        buf_ref = refs[3 * n]
        for a in range(n):
            for ch in range(GRP_W // 128):
                cols = slice(ch * 128, (ch + 1) * 128)
                buf_ref[...] = refs[a][:, cols]
                for b, d in enumerate(DILATIONS[1:]):
                    for r in range(d):
                        refs[n + 2 * a + b][r, :, cols] = buf_ref[pl.ds(r, tm // d, stride=d), :]

    return pl.pallas_call(
        body, name=name, grid=(S // tm,),
        in_specs=[pl.BlockSpec((tm, GRP_W), lambda i: (i, 0))] * n,
        out_specs=[pl.BlockSpec((d, tm // d, GRP_W), lambda i: (0, i, 0)) for _ in range(n) for d in DILATIONS[1:]],
        out_shape=[jax.ShapeDtypeStruct((d, S // d, GRP_W), F32) for _ in range(n) for d in DILATIONS[1:]],
        scratch_shapes=[pltpu.VMEM((tm, 128), F32)],
        compiler_params=_cp(("parallel",)),
    )(*ts)


def _heads(ref):
    return jnp.stack([ref[:, h * DH:(h + 1) * DH] for h in range(NH)])


def _bdot_nt(a, b):
    return lax.dot_general(a, b, (((2,), (2,)), ((0,), (0,))), preferred_element_type=F32)


def _bdot(a, b):
    return lax.dot_general(a, b, (((2,), (1,)), ((0,), (0,))), preferred_element_type=F32)


def _bdot_tn(a, b):
    return lax.dot_general(a, b, (((1,), (1,)), ((0,), (0,))), preferred_element_type=F32)


def _attn_fwd(gi, qkv_c, name):
    d = DILATIONS[gi]
    nblk = S // d // BLK

    def body(*refs):
        if nblk > 1:
            q_ref, kc_ref, kp_ref, vc_ref, vp_ref, o_ref, lse_ref = refs
            has_prev = pl.program_id(1) != 0
        else:
            q_ref, kc_ref, vc_ref, o_ref, lse_ref = refs
        qi = lax.broadcasted_iota(jnp.int32, (NH, BLK, BLK), 1)
        kj = lax.broadcasted_iota(jnp.int32, (NH, BLK, BLK), 2)
        q = _heads(q_ref)
        sc = jnp.where(kj <= qi, _bdot_nt(q, _heads(kc_ref)) * 0.125, NEG_INF)
        m = jnp.max(sc, axis=-1, keepdims=True)
        if nblk > 1:
            mask_p = jnp.logical_and(kj >= qi, has_prev)
            sp = jnp.where(mask_p, _bdot_nt(q, _heads(kp_ref)) * 0.125, NEG_INF)
            m = jnp.maximum(m, jnp.max(sp, axis=-1, keepdims=True))
        pc = jnp.exp(sc - m)
        l = jnp.sum(pc, axis=-1, keepdims=True)
        o = _bdot(pc.astype(BF16), _heads(vc_ref))
        if nblk > 1:
            pp = jnp.exp(sp - m)
            l = l + jnp.sum(pp, axis=-1, keepdims=True)
            o = o + _bdot(pp.astype(BF16), _heads(vp_ref))
        o = o / l
        lse = m + jnp.log(l)
        for h in range(NH):
            sl = slice(h * DH, (h + 1) * DH)
            o_ref[:, sl] = o[h]
            lse_ref[:, sl] = jnp.broadcast_to(lse[h], (BLK, DH))

    def cur(sec):
        return pl.BlockSpec((None, BLK, GRP_W), lambda r, n: (r, n, sec))

    def prev(sec):
        return pl.BlockSpec((None, BLK, GRP_W), lambda r, n: (r, jnp.maximum(n - 1, 0), sec))

    out = pl.BlockSpec((None, BLK, GRP_W), lambda r, n: (r, n, 0))
    shp = jax.ShapeDtypeStruct((d, S // d, GRP_W), F32)
    if nblk > 1:
        in_specs, args = [cur(0), cur(1), prev(1), cur(2), prev(2)], (qkv_c,) * 5
    else:
        in_specs, args = [cur(0), cur(1), cur(2)], (qkv_c,) * 3
    return pl.pallas_call(
        body, name=name, grid=(d, nblk), in_specs=in_specs, out_specs=[out, out], out_shape=[shp, shp],
        compiler_params=_cp(("parallel", "parallel")),
    )(*args)


def _attn_combine(os, lses, name):
    tm = ROPE_TM

    def body(o0_ref, o1_ref, o2_ref, l0_ref, l1_ref, l2_ref, y_ref, yt_ref, l_ref, buf_ref):
        def token_order(ref, d, cols, slot):
            if d == 1:
                return ref[0, :, cols]
            for r in range(d):
                buf_ref[slot, pl.ds(r, tm // d, stride=d), :] = ref[r, :, cols]
            return buf_ref[slot]

        for ch in range(GRP_W // 128):
            cols = slice(ch * 128, (ch + 1) * 128)
            o = [token_order(ref, d, cols, k) for k, (ref, d) in enumerate(zip((o0_ref, o1_ref, o2_ref), DILATIONS))]
            ls = [token_order(ref, d, cols, 3 + k)
                  for k, (ref, d) in enumerate(zip((l0_ref, l1_ref, l2_ref), DILATIONS))]
            m = jnp.maximum(jnp.maximum(ls[0], ls[1]), ls[2])
            e = [jnp.exp(l - m) for l in ls]
            den = e[0] + e[1] + e[2]
            y = (e[0] * o[0] + e[1] * o[1] + e[2] * o[2]) / den
            y_ref[:, cols] = y
            yt_ref[cols, :] = y.T.astype(BF16)
            l_ref[:, cols] = m + jnp.log(den)

    blk = pl.BlockSpec((tm, GRP_W), lambda i: (i, 0))
    cls = [pl.BlockSpec((d, tm // d, GRP_W), lambda i: (0, i, 0)) for d in DILATIONS]
    shp = jax.ShapeDtypeStruct((S, GRP_W), F32)
    return pl.pallas_call(
        body, name=name, grid=(S // tm,), in_specs=cls + cls,
        out_specs=[blk, pl.BlockSpec((GRP_W, tm), lambda i: (0, i)), blk],
        out_shape=[shp, jax.ShapeDtypeStruct((GRP_W, S), BF16), shp],
        scratch_shapes=[pltpu.VMEM((6, tm, 128), F32)],
        compiler_params=_cp(("parallel",)),
    )(*os, *lses)


def _attn_bwd(gi, qkv_c, dy_c, y_c, lse_c, name):
    d = DILATIONS[gi]
    nblk = S // d // BLK

    def body(*refs):
        if nblk > 1:
            (q_ref, qn_ref, k_ref, kp_ref, v_ref, vp_ref, dy_ref, dyn_ref, y_ref, yn_ref, l_ref, ln_ref,
             dq_ref, dk_ref, dv_ref) = refs
            n = pl.program_id(1)
            has_prev = n != 0
            has_next = n != nblk - 1
        else:
            q_ref, k_ref, v_ref, dy_ref, y_ref, l_ref, dq_ref, dk_ref, dv_ref = refs
        qi = lax.broadcasted_iota(jnp.int32, (NH, BLK, BLK), 1)
        kj = lax.broadcasted_iota(jnp.int32, (NH, BLK, BLK), 2)

        def lse_col(ref):
            return jnp.stack([ref[:, h * DH:h * DH + 1] for h in range(NH)])

        q, k, v = _heads(q_ref), _heads(k_ref), _heads(v_ref)
        dy = _heads(dy_ref)
        dd = jnp.sum(dy * _heads(y_ref), axis=-1, keepdims=True)
        lcol = lse_col(l_ref)
        dyb = dy.astype(BF16)
        p = jnp.exp(jnp.where(kj <= qi, _bdot_nt(q, k) * 0.125, NEG_INF) - lcol)
        ds = (p * (_bdot_nt(dyb, v) - dd)).astype(BF16)
        dq = _bdot(ds, k)
        dk = _bdot_tn(ds, q)
        dv = _bdot_tn(p.astype(BF16), dyb)
        if nblk > 1:
            qn, kpv, vpv = _heads(qn_ref), _heads(kp_ref), _heads(vp_ref)
            dyn = _heads(dyn_ref)
            ddn = jnp.sum(dyn * _heads(yn_ref), axis=-1, keepdims=True)
            lncol = lse_col(ln_ref)
            dynb = dyn.astype(BF16)
            mask_p = jnp.logical_and(kj >= qi, has_prev)
            pp = jnp.exp(jnp.where(mask_p, _bdot_nt(q, kpv) * 0.125, NEG_INF) - lcol)
            dsp = (pp * (_bdot_nt(dyb, vpv) - dd)).astype(BF16)
            dq = dq + _bdot(dsp, kpv)
            mask_n = jnp.logical_and(kj >= qi, has_next)
            pn = jnp.exp(jnp.where(mask_n, _bdot_nt(qn, k) * 0.125, NEG_INF) - lncol)
            dsn = (pn * (_bdot_nt(dynb, v) - ddn)).astype(BF16)
            dk = dk + _bdot_tn(dsn, qn)
            dv = dv + _bdot_tn(pn.astype(BF16), dynb)
        dq = dq * 0.125
        dk = dk * 0.125
        for h in range(NH):
            sl = slice(h * DH, (h + 1) * DH)
            dq_ref[:, sl] = dq[h]
            dk_ref[:, sl] = dk[h]
            dv_ref[:, sl] = dv[h]

    def spec(sec, shift):
        def idx(r, n):
            return (r, jnp.clip(n + shift, 0, nblk - 1), sec)
        return pl.BlockSpec((None, BLK, GRP_W), idx)

    if nblk > 1:
        in_specs = [spec(0, 0), spec(0, 1), spec(1, 0), spec(1, -1), spec(2, 0), spec(2, -1),
                    spec(0, 0), spec(0, 1), spec(0, 0), spec(0, 1), spec(0, 0), spec(0, 1)]
        args = (qkv_c,) * 6 + (dy_c, dy_c, y_c, y_c, lse_c, lse_c)
    else:
        in_specs = [spec(0, 0), spec(1, 0), spec(2, 0), spec(0, 0), spec(0, 0), spec(0, 0)]
        args = (qkv_c, qkv_c, qkv_c, dy_c, y_c, lse_c)
    out = spec(0, 0)
    shp = jax.ShapeDtypeStruct((d, S // d, GRP_W), F32)
    return pl.pallas_call(
        body, name=name, grid=(d, nblk), in_specs=in_specs, out_specs=[out, out, out], out_shape=[shp, shp, shp],
        compiler_params=_cp(("parallel", "parallel")),
    )(*args)


_SQRT_HALF = 0.7071067811865476
_INV_SQRT_2PI = 0.3989422804014327


def _gelu(z):
    return 0.5 * z * (1.0 + lax.erf(z * _SQRT_HALF))


def _gelu_grad(z):
    return 0.5 * (1.0 + lax.erf(z * _SQRT_HALF)) + z * (jnp.exp(-0.5 * z * z) * _INV_SQRT_2PI)


def _tril_mask():
    t = lax.broadcasted_iota(jnp.int32, (BLK, BLK), 0)
    s = lax.broadcasted_iota(jnp.int32, (BLK, BLK), 1)
    return s <= t


def _gmlp_fwd(z, ln_g, ln_b, w_s, b_s_t, name):
    def body(z_ref, g_ref, b_ref, ws_ref, bs_ref, y_ref, yt_ref):
        zg = _gelu(z_ref[...])
        u = zg[:, :GW]
        xh, _ = _ln_stats(zg[:, GW:])
        vn = (xh * g_ref[...] + b_ref[...]).astype(BF16)
        tril = _tril_mask()
        for gg in range(8):
            sl = slice(gg * BLK, (gg + 1) * BLK)
            wt = jnp.where(tril, ws_ref[gg], 0.0).astype(BF16)
            mixed = _dot(wt, vn[:, sl]) + bs_ref[:, gg:gg + 1]
            yv = u[:, sl] * mixed
            y_ref[:, sl] = yv.astype(BF16)
            yt_ref[sl, :] = yv.T.astype(BF16)

    vec = pl.BlockSpec((1, GW), lambda n: (0, 0))
    return pl.pallas_call(
        body, name=name, grid=(NBLK,),
        in_specs=[pl.BlockSpec((BLK, 2 * GW), lambda n: (n, 0)), vec, vec,
                  pl.BlockSpec((8, BLK, BLK), lambda n: (0, 0, 0)), pl.BlockSpec((BLK, 8), lambda n: (0, 0))],
        out_specs=[pl.BlockSpec((BLK, GW), lambda n: (n, 0)), pl.BlockSpec((GW, BLK), lambda n: (0, n))],
        out_shape=[jax.ShapeDtypeStruct((S, GW), BF16), jax.ShapeDtypeStruct((GW, S), BF16)],
        compiler_params=_cp(("parallel",)),
    )(z, ln_g, ln_b, w_s, b_s_t)


def _gmlp_bwd(z, dy, ln_g, ln_b, w_s, b_s_t, name):
    def body(z_ref, dy_ref, g_ref, b_ref, ws_ref, bs_ref, dz_ref, dws_ref, dbs_ref, dg_ref, db_ref, dvn_ref):
        n = pl.program_id(0)
        zv = z_ref[...]
        zg = _gelu(zv)
        u = zg[:, :GW]
        xh, rstd = _ln_stats(zg[:, GW:])
        vn = (xh * g_ref[...] + b_ref[...]).astype(BF16)
        tril = _tril_mask()

        @pl.when(n == 0)
        def _():
            dws_ref[...] = jnp.zeros_like(dws_ref)
            dbs_ref[...] = jnp.zeros_like(dbs_ref)
            dg_ref[...] = jnp.zeros_like(dg_ref)
            db_ref[...] = jnp.zeros_like(db_ref)

        for gg in range(8):
            sl = slice(gg * BLK, (gg + 1) * BLK)
            wt = jnp.where(tril, ws_ref[gg], 0.0).astype(BF16)
            dyg = dy_ref[:, sl]
            mixed = _dot(wt, vn[:, sl]) + bs_ref[:, gg:gg + 1]
            dz_ref[:, sl] = (dyg * mixed * _gelu_grad(zv[:, sl])).astype(BF16)
            dmix = dyg * u[:, sl]
            dmb = dmix.astype(BF16)
            dws_ref[gg] += jnp.where(tril, _dot_nt(dmb, vn[:, sl]), 0.0)
            dbs_ref[:, gg:gg + 1] += jnp.sum(dmix, axis=-1, keepdims=True)
            dvn_ref[:, sl] = _dot_tn(wt, dmb)

        dvn = dvn_ref[...]
        dg_ref[...] += jnp.sum(dvn * xh, axis=0, keepdims=True)
        db_ref[...] += jnp.sum(dvn, axis=0, keepdims=True)
        dvg = _ln_dx(dvn * g_ref[...], xh, rstd)
        dz_ref[:, GW:] = (dvg * _gelu_grad(zv[:, GW:])).astype(BF16)

    vec = pl.BlockSpec((1, GW), lambda n: (0, 0))
    ws = pl.BlockSpec((8, BLK, BLK), lambda n: (0, 0, 0))
    bs = pl.BlockSpec((BLK, 8), lambda n: (0, 0))
    return pl.pallas_call(
        body, name=name, grid=(NBLK,),
        in_specs=[pl.BlockSpec((BLK, 2 * GW), lambda n: (n, 0)), pl.BlockSpec((BLK, GW), lambda n: (n, 0)),
                  vec, vec, ws, bs],
        out_specs=[pl.BlockSpec((BLK, 2 * GW), lambda n: (n, 0)), ws, bs, vec, vec],
        out_shape=[jax.ShapeDtypeStruct((S, 2 * GW), BF16), jax.ShapeDtypeStruct((8, BLK, BLK), F32),
                   jax.ShapeDtypeStruct((BLK, 8), F32), jax.ShapeDtypeStruct((1, GW), F32),
                   jax.ShapeDtypeStruct((1, GW), F32)],
        scratch_shapes=[pltpu.VMEM((BLK, GW), F32)],
        compiler_params=_cp(("arbitrary",)),
    )(z, dy, ln_g, ln_b, w_s, b_s_t)


def _merge_fwd(a, b, gl, b_gates, name):
    tm = 512

    def body(a_ref, b_ref, g0_ref, g1_ref, bg_ref, o_ref, ot_ref):
        g0 = jax.nn.sigmoid(g0_ref[...] + bg_ref[:, :D])
        g1 = jax.nn.sigmoid(g1_ref[...] + bg_ref[:, D:])
        mg = g0 * a_ref[...] + g1 * b_ref[...]
        o_ref[...] = mg.astype(BF16)
        ot_ref[...] = mg.T.astype(BF16)

    row = pl.BlockSpec((tm, D), lambda i: (i, 0))
    return pl.pallas_call(
        body, name=name, grid=(S // tm,),
        in_specs=[row, row, row, pl.BlockSpec((tm, D), lambda i: (i, 1)), pl.BlockSpec((1, 2 * D), lambda i: (0, 0))],
        out_specs=[row, pl.BlockSpec((D, tm), lambda i: (0, i))],
        out_shape=[jax.ShapeDtypeStruct((S, D), BF16), jax.ShapeDtypeStruct((D, S), BF16)],
        compiler_params=_cp(("parallel",)),
    )(a, b, gl, gl, b_gates)


def _merge_bwd(dm, a, b, gl, b_gates, name):
    tm = 512

    def body(dm_ref, a_ref, b_ref, g0_ref, g1_ref, bg_ref, da_ref, db_ref, dgl_ref, dbg_ref):
        i = pl.program_id(0)
        dmv = dm_ref[...]
        g0 = jax.nn.sigmoid(g0_ref[...] + bg_ref[:, :D])
        g1 = jax.nn.sigmoid(g1_ref[...] + bg_ref[:, D:])
        da_ref[...] = (dmv * g0).astype(BF16)
        db_ref[...] = (dmv * g1).astype(BF16)
        d0 = dmv * a_ref[...] * g0 * (1.0 - g0)
        d1 = dmv * b_ref[...] * g1 * (1.0 - g1)
        dgl_ref[:, :D] = d0.astype(BF16)
        dgl_ref[:, D:] = d1.astype(BF16)
        s0 = jnp.sum(d0, axis=0, keepdims=True)
        s1 = jnp.sum(d1, axis=0, keepdims=True)

        @pl.when(i == 0)
        def _():
            dbg_ref[:, :D] = s0
            dbg_ref[:, D:] = s1

        @pl.when(i > 0)
        def _():
            dbg_ref[:, :D] += s0
            dbg_ref[:, D:] += s1

    row = pl.BlockSpec((tm, D), lambda i: (i, 0))
    wide = pl.BlockSpec((tm, 2 * D), lambda i: (i, 0))
    bg = pl.BlockSpec((1, 2 * D), lambda i: (0, 0))
    return pl.pallas_call(
        body, name=name, grid=(S // tm,),
        in_specs=[row, row, row, row, pl.BlockSpec((tm, D), lambda i: (i, 1)), bg],
        out_specs=[row, row, wide, bg],
        out_shape=[jax.ShapeDtypeStruct((S, D), BF16), jax.ShapeDtypeStruct((S, D), BF16),
                   jax.ShapeDtypeStruct((S, 2 * D), BF16), jax.ShapeDtypeStruct((1, 2 * D), F32)],
        compiler_params=_cp(("arbitrary",)),
    )(dm, a, b, gl, gl, b_gates)


def _adam_math(w, g, m, v):
    m2 = ADAM_B1 * m + (1.0 - ADAM_B1) * g
    v2 = ADAM_B2 * v + (1.0 - ADAM_B2) * (g * g)
    m_hat = m2 / (1.0 - ADAM_B1 ** ADAM_STEP)
    v_hat = v2 / (1.0 - ADAM_B2 ** ADAM_STEP)
    delta = -ADAM_LR * (m_hat / (jnp.sqrt(v_hat) + ADAM_EPS) + ADAM_WD * w)
    return delta, m2, v2


def _pick_rows(rows, cols, unit=16, budget=MIB):
    best = unit
    for t in range(unit, rows + 1, unit):
        if rows % t == 0 and t * cols * 4 <= budget:
            best = t
    assert rows % best == 0
    return best


def _adamw(w, g, m, v, name):
    r, c = w.shape
    tr = _pick_rows(r, c, unit=8)

    def body(w_ref, g_ref, m_ref, v_ref, go_ref, d_ref, mo_ref, vo_ref):
        gv = g_ref[...]
        delta, m2, v2 = _adam_math(w_ref[...], gv, m_ref[...], v_ref[...])
        go_ref[...] = gv
        d_ref[...] = delta
        mo_ref[...] = m2
        vo_ref[...] = v2

    blk = pl.BlockSpec((tr, c), lambda i: (i, 0))
    shp = jax.ShapeDtypeStruct((r, c), F32)
    return pl.pallas_call(
        body, name=name, grid=(r // tr,), in_specs=[blk] * 4, out_specs=[blk] * 4, out_shape=[shp] * 4,
        compiler_params=_cp(("parallel",)),
    )(*[pltpu.with_memory_space_constraint(t, pltpu.HBM) for t in (w, g, m, v)])


def _small_sum_adamw(parts, own, pos, w, m, v, name):
    tr = 48

    def body(pos_ref, p_ref, own_ref, w_ref, m_ref, v_ref, g_ref, d_ref, mo_ref, vo_ref):
        me = 2 * pos_ref[1] + pos_ref[0]
        gv = None
        for k in range(8):
            term = jnp.where(me == k, own_ref[...], p_ref[k])
            gv = term if gv is None else gv + term
        delta, m2, v2 = _adam_math(w_ref[...], gv, m_ref[...], v_ref[...])
        g_ref[...] = gv
        d_ref[...] = delta
        mo_ref[...] = m2
        vo_ref[...] = v2

    blk = pl.BlockSpec((tr, D), lambda i, p: (i, 0))
    shp = jax.ShapeDtypeStruct((SMALL_ROWS, D), F32)
    return pl.pallas_call(
        body, name=name,
        grid_spec=pltpu.PrefetchScalarGridSpec(
            num_scalar_prefetch=1, grid=(SMALL_ROWS // tr,),
            in_specs=[pl.BlockSpec((8, tr, D), lambda i, p: (0, i, 0)), blk, blk, blk, blk],
            out_specs=[blk] * 4),
        out_shape=[shp] * 4,
        compiler_params=_cp(("parallel",)),
    )(pos, parts, own, w, m, v)


ANY = pl.BlockSpec(memory_space=pl.ANY)


def _in_hbm(arrays):
    return [pltpu.with_memory_space_constraint(a, pltpu.HBM) for a in arrays]


def _mesh_pos():
    x, y, c = lax.axis_index("x"), lax.axis_index("y"), lax.axis_index("c")
    chips = [(1 - x, y), (x, 1 - y), (1 - x, 1 - y)]
    return x, y, c, chips


def _place_shard(w, kind, pos, name):
    r, c = w.shape
    tr = _pick_rows(r, c)

    def body(pos_ref, w_ref, o_ref):
        o_ref[...] = w_ref[...].astype(BF16)

    if kind == "stack":
        o_spec = pl.BlockSpec((None, tr, c), lambda i, p: (p[1], i, 0))
        shape = (NSH, r, c)
    else:
        o_spec = pl.BlockSpec((tr, c), lambda i, p: (i, p[1]))
        shape = (r, NSH * c)
    return pl.pallas_call(
        body, name=name,
        grid_spec=pltpu.PrefetchScalarGridSpec(
            num_scalar_prefetch=1, grid=(r // tr,),
            in_specs=[pl.BlockSpec((tr, c), lambda i, p: (i, 0))], out_specs=o_spec),
        out_shape=pltpu.HBM(shape, BF16),
        compiler_params=_cp(("parallel",)),
    )(pos, pltpu.with_memory_space_constraint(w, pltpu.HBM))


SEM = pl.BlockSpec(memory_space=pltpu.SEMAPHORE)
SPLIT_COPY = pltpu.CompilerParams(has_side_effects=pltpu.SideEffectType.DATAFLOW_SIDE_EFFECTING)


def _shard_window(ref, kind, j, h, dims):
    r, c = dims
    rows = pl.ds(pl.multiple_of(h * (r // 2), 16), r // 2)
    if kind == "stack":
        return ref.at[j, rows, :]
    return ref.at[rows, pl.ds(pl.multiple_of(j * c, 128), c)]


def _ici_copy(ref, kind, dims, j, c, sems, idx, to):
    win = _shard_window(ref, kind, j, c, dims)
    return pltpu.make_async_remote_copy(src_ref=win, dst_ref=win, send_sem=sems[0].at[idx], recv_sem=sems[1].at[idx],
                                        device_id=to, device_id_type=MESH_T)


def _gather_start(fulls, kinds, dims, after, name):
    n, na = len(fulls), len(after)

    def body(*refs):
        outs = refs[n + na:2 * n + na]
        send_sems, recv_sems, token = refs[2 * n + na:]
        x, y, c, chips = _mesh_pos()
        for a in range(n):
            for k, chip in enumerate(chips):
                _ici_copy(outs[a], kinds[a], dims[a], 2 * x + y, c, (send_sems, recv_sems), 3 * a + k,
                          (chip[0], chip[1], c)).start()
        token[...] = jnp.zeros_like(token)

    res = pl.pallas_call(
        body, name=name, in_specs=[ANY] * (n + na),
        out_specs=[ANY] * n + [SEM, SEM, pl.BlockSpec(memory_space=pltpu.VMEM)],
        out_shape=[pltpu.HBM(f.shape, BF16) for f in fulls]
        + [pltpu.SemaphoreType.DMA((3 * n,)), pltpu.SemaphoreType.DMA((3 * n,)), jax.ShapeDtypeStruct((8, 128), F32)],
        input_output_aliases={i: i for i in range(n)},
        compiler_params=SPLIT_COPY,
    )(*_in_hbm(fulls), *after)
    return res[:n], res[n], res[n + 1], res[n + 2]


def _gather_wait(fulls, send_sems, recv_sems, kinds, dims, after, name):
    n, na = len(fulls), len(after)

    def body(*refs):
        ssem, rsem = refs[n], refs[n + 1]
        outs = refs[n + 2 + na:]
        x, y, c, chips = _mesh_pos()
        for a in range(n):
            for k, chip in enumerate(chips):
                to = (chip[0], chip[1], c)
                _ici_copy(outs[a], kinds[a], dims[a], 2 * x + y, c, (ssem, rsem), 3 * a + k, to).wait_send()
                _ici_copy(outs[a], kinds[a], dims[a], 2 * chip[0] + chip[1], c, (ssem, rsem), 3 * a + k, to).wait_recv()

    return pl.pallas_call(
        body, name=name, in_specs=[ANY] * n + [SEM, SEM] + [ANY] * na, out_specs=[ANY] * n,
        out_shape=[pltpu.HBM(f.shape, BF16) for f in fulls],
        input_output_aliases={i: i for i in range(n)},
        compiler_params=SPLIT_COPY,
    )(*_in_hbm(fulls), send_sems, recv_sems, *after)


def _gather_forward(fulls, kinds, dims, name):
    n = len(fulls)

    def body(*refs):
        outs = refs[n:2 * n]
        sems = refs[2 * n:]
        x, y, c, chips = _mesh_pos()
        sib = (x, y, 1 - c)
        cps = []
        for a in range(n):
            for k, chip in enumerate(chips):
                cp = _ici_copy(outs[a], kinds[a], dims[a], 2 * chip[0] + chip[1], c, sems, 3 * a + k, sib)
                cp.start()
                cps.append(cp)
        for a in range(n):
            for k, chip in enumerate(chips):
                _ici_copy(outs[a], kinds[a], dims[a], 2 * chip[0] + chip[1], 1 - c, sems, 3 * a + k, sib).wait_recv()
        for cp in cps:
            cp.wait_send()

    return pl.pallas_call(
        body, name=name, in_specs=[ANY] * n, out_specs=[ANY] * n,
        out_shape=[pltpu.HBM(f.shape, BF16) for f in fulls],
        input_output_aliases={i: i for i in range(n)},
        scratch_shapes=[pltpu.SemaphoreType.DMA((3 * n,)), pltpu.SemaphoreType.DMA((3 * n,))],
    )(*_in_hbm(fulls))


def _pair_copy(src, land, a, x, y, c, sems):
    return pltpu.make_async_remote_copy(
        src_ref=src.at[1 - c], dst_ref=land, send_sem=sems[0].at[a], recv_sem=sems[1].at[a],
        device_id=(x, y, 1 - c), device_id_type=MESH_T)


def _pair_start(grads, lands, name):
    n = len(grads)

    def body(*refs):
        srcs, dsts = refs[2 * n:3 * n], refs[3 * n:4 * n]
        send_sems, recv_sems, token = refs[4 * n:]
        x, y, c, _ = _mesh_pos()
        for a in range(n):
            _pair_copy(srcs[a], dsts[a], a, x, y, c, (send_sems, recv_sems)).start()
        token[...] = jnp.zeros_like(token)

    res = pl.pallas_call(
        body, name=name, in_specs=[ANY] * (2 * n),
        out_specs=[ANY] * (2 * n) + [SEM, SEM, pl.BlockSpec(memory_space=pltpu.VMEM)],
        out_shape=[pltpu.HBM(g.shape, F32) for g in grads]
        + [pltpu.HBM(l.shape, F32) for l in lands]
        + [pltpu.SemaphoreType.DMA((n,)), pltpu.SemaphoreType.DMA((n,)), jax.ShapeDtypeStruct((8, 128), F32)],
        input_output_aliases={i: i for i in range(2 * n)},
        compiler_params=SPLIT_COPY,
    )(*_in_hbm(grads), *_in_hbm(lands))
    return res[:n], res[n:2 * n], res[2 * n], res[2 * n + 1], res[2 * n + 2]


def _pair_wait(grads, lands, send_sems, recv_sems, after, name):
    n, na = len(grads), len(after)

    def body(*refs):
        ssem, rsem = refs[2 * n], refs[2 * n + 1]
        outs = refs[2 * n + 2 + na:]
        x, y, c, _ = _mesh_pos()
        for a in range(n):
            cp = _pair_copy(outs[a], outs[n + a], a, x, y, c, (ssem, rsem))
            cp.wait_send()
            cp.wait_recv()

    res = pl.pallas_call(
        body, name=name, in_specs=[ANY] * (2 * n) + [SEM, SEM] + [ANY] * na, out_specs=[ANY] * (2 * n),
        out_shape=[pltpu.HBM(g.shape, F32) for g in grads]
        + [pltpu.HBM(l.shape, F32) for l in lands],
        input_output_aliases={i: i for i in range(2 * n)},
        compiler_params=SPLIT_COPY,
    )(*_in_hbm(grads), *_in_hbm(lands), send_sems, recv_sems, *after)
    return res[:n], res[n:]


def _pair_sum(g, recv, pos, name):
    _, _, rh, c = g.shape
    tr = _pick_rows(rh, c)

    def body(pos_ref, g_ref, r_ref, o_ref):
        o_ref[...] = (g_ref[...] + r_ref[...]).astype(BF16)

    return pl.pallas_call(
        body, name=name,
        grid_spec=pltpu.PrefetchScalarGridSpec(
            num_scalar_prefetch=1, grid=(3, rh // tr),
            in_specs=[pl.BlockSpec((None, None, tr, c), lambda k, r, p: (p[0], p[2 + k], r, 0)),
                      pl.BlockSpec((None, tr, c), lambda k, r, p: (p[2 + k], r, 0))],
            out_specs=pl.BlockSpec((None, tr, c), lambda k, r, p: (k, r, 0))),
        out_shape=pltpu.HBM((3, rh, c), BF16),
        compiler_params=_cp(("parallel", "parallel")),
    )(pos, *_in_hbm([g, recv]))


def _chip_copy(src, land, a, k, chip, c, sems):
    return pltpu.make_async_remote_copy(
        src_ref=src.at[k], dst_ref=land.at[k], send_sem=sems[0].at[3 * a + k],
        recv_sem=sems[1].at[3 * a + k], device_id=(chip[0], chip[1], c), device_id_type=MESH_T)


def _chip_start(psums, lands, name):
    n = len(psums)

    def body(*refs):
        srcs, dsts = refs[2 * n:3 * n], refs[3 * n:4 * n]
        send_sems, recv_sems, token = refs[4 * n:]
        x, y, c, chips = _mesh_pos()
        for a in range(n):
            for k, chip in enumerate(chips):
                _chip_copy(srcs[a], dsts[a], a, k, chip, c, (send_sems, recv_sems)).start()
        token[...] = jnp.zeros_like(token)

    res = pl.pallas_call(
        body, name=name, in_specs=[ANY] * (2 * n),
        out_specs=[ANY] * (2 * n) + [SEM, SEM, pl.BlockSpec(memory_space=pltpu.VMEM)],
        out_shape=[pltpu.HBM(p.shape, BF16) for p in psums]
        + [pltpu.HBM(l.shape, BF16) for l in lands]
        + [pltpu.SemaphoreType.DMA((3 * n,)), pltpu.SemaphoreType.DMA((3 * n,)), jax.ShapeDtypeStruct((8, 128), F32)],
        input_output_aliases={i: i for i in range(2 * n)},
        compiler_params=SPLIT_COPY,
    )(*_in_hbm(psums), *_in_hbm(lands))
    return res[:n], res[n:2 * n], res[2 * n], res[2 * n + 1], res[2 * n + 2]


def _chip_wait(psums, lands, send_sems, recv_sems, after, name):
    n, na = len(psums), len(after)

    def body(*refs):
        ssem, rsem = refs[2 * n], refs[2 * n + 1]
        outs = refs[2 * n + 2 + na:]
        srcs, dsts = outs[:n], outs[n:]
        x, y, c, chips = _mesh_pos()
        for a in range(n):
            for k, chip in enumerate(chips):
                cp = _chip_copy(srcs[a], dsts[a], a, k, chip, c, (ssem, rsem))
                cp.wait_send()
                cp.wait_recv()

    res = pl.pallas_call(
        body, name=name, in_specs=[ANY] * (2 * n) + [SEM, SEM] + [ANY] * na, out_specs=[ANY] * (2 * n),
        out_shape=[pltpu.HBM(p.shape, BF16) for p in psums]
        + [pltpu.HBM(l.shape, BF16) for l in lands],
        input_output_aliases={i: i for i in range(2 * n)},
        compiler_params=SPLIT_COPY,
    )(*_in_hbm(psums), *_in_hbm(lands), send_sems, recv_sems, *after)
    return res[n:]


def _owner_sum(g, recv_a, recv_b, pos, name):
    _, _, rh, c = g.shape
    tr = _pick_rows(rh, c)

    def body(pos_ref, g_ref, ra_ref, rb_ref, o_ref):
        acc = g_ref[...] + ra_ref[...]
        for k in range(3):
            acc = acc + rb_ref[k].astype(F32)
        o_ref[...] = acc

    return pl.pallas_call(
        body, name=name,
        grid_spec=pltpu.PrefetchScalarGridSpec(
            num_scalar_prefetch=1, grid=(rh // tr,),
            in_specs=[pl.BlockSpec((None, None, tr, c), lambda r, p: (p[0], p[1], r, 0)),
                      pl.BlockSpec((None, tr, c), lambda r, p: (p[1], r, 0)),
                      pl.BlockSpec((3, tr, c), lambda r, p: (0, r, 0))],
            out_specs=pl.BlockSpec((None, tr, c), lambda r, p: (p[0], r, 0))),
        out_shape=pltpu.HBM((2, rh, c), F32),
        compiler_params=_cp(("parallel",)),
    )(pos, *_in_hbm([g, recv_a, recv_b]))


def _sibling_allgather(halves, name):
    n = len(halves)

    def body(*refs):
        outs = refs[n:2 * n]
        send_sems, recv_sems = refs[2 * n:]
        x, y, c, _ = _mesh_pos()
        cps = []
        for a in range(n):
            cp = pltpu.make_async_remote_copy(
                src_ref=outs[a].at[c], dst_ref=outs[a].at[c], send_sem=send_sems.at[a], recv_sem=recv_sems.at[a],
                device_id=(x, y, 1 - c), device_id_type=MESH_T)
            cp.start()
            cps.append(cp)
        for a in range(n):
            cps[a].wait_send()
            pltpu.make_async_remote_copy(
                src_ref=outs[a].at[1 - c], dst_ref=outs[a].at[1 - c], send_sem=send_sems.at[a],
                recv_sem=recv_sems.at[a], device_id=(x, y, 1 - c), device_id_type=MESH_T).wait_recv()

    return pl.pallas_call(
        body, name=name, in_specs=[ANY] * n, out_specs=[ANY] * n,
        out_shape=[pltpu.HBM(h.shape, F32) for h in halves],
        input_output_aliases={i: i for i in range(n)},
        scratch_shapes=[pltpu.SemaphoreType.DMA((n,)), pltpu.SemaphoreType.DMA((n,))],
    )(*_in_hbm(halves))


def _peers(x, y, c):
    rel = [(0, 0, 1), (0, 1, 0), (0, 1, 1), (1, 0, 0), (1, 0, 1), (1, 1, 0), (1, 1, 1)]
    return [((1 - x) if dx else x, (1 - y) if dy else y, (1 - c) if dc else c) for dx, dy, dc in rel]


def _small_copy(src, land, k, peer, slot, sems):
    return pltpu.make_async_remote_copy(src_ref=src, dst_ref=land.at[slot], send_sem=sems[0].at[k],
                                        recv_sem=sems[1].at[k], device_id=peer, device_id_type=MESH_T)


def _small_start(part, land, name):
    def body(p_in, l_in, p_ref, l_ref, send_sems, recv_sems, token):
        x, y, c, _ = _mesh_pos()
        for k, peer in enumerate(_peers(x, y, c)):
            _small_copy(p_ref, l_ref, k, peer, 4 * x + 2 * y + c, (send_sems, recv_sems)).start()
        token[...] = jnp.zeros_like(token)

    return pl.pallas_call(
        body, name=name, in_specs=[ANY, ANY],
        out_specs=[ANY, ANY, SEM, SEM, pl.BlockSpec(memory_space=pltpu.VMEM)],
        out_shape=[pltpu.HBM(part.shape, F32), pltpu.HBM(land.shape, F32), pltpu.SemaphoreType.DMA((7,)),
                   pltpu.SemaphoreType.DMA((7,)), jax.ShapeDtypeStruct((8, 128), F32)],
        input_output_aliases={0: 0, 1: 1},
        compiler_params=SPLIT_COPY,
    )(*_in_hbm([part, land]))


def _small_wait(part, land, send_sems, recv_sems, after, name):
    na = len(after)

    def body(*refs):
        ssem, rsem = refs[2], refs[3]
        p_ref, l_ref = refs[4 + na:]
        x, y, c, _ = _mesh_pos()
        for k, peer in enumerate(_peers(x, y, c)):
            cp = _small_copy(p_ref, l_ref, k, peer, 4 * peer[0] + 2 * peer[1] + peer[2], (ssem, rsem))
            cp.wait_send()
            cp.wait_recv()

    return pl.pallas_call(
        body, name=name, in_specs=[ANY, ANY, SEM, SEM] + [ANY] * na, out_specs=[ANY, ANY],
        out_shape=[pltpu.HBM(part.shape, F32), pltpu.HBM(land.shape, F32)],
        input_output_aliases={0: 0, 1: 1},
        compiler_params=SPLIT_COPY,
    )(*_in_hbm([part, land]), send_sems, recv_sems, *after)


def _pack_small(ln1_g, ln1_b, gln_g, gln_b, ln2_g, ln2_b, ln3_g, ln3_b, b_gates, b_s, w_s):
    rows = [ln1_g, ln1_b, gln_g, gln_b, ln2_g, ln2_b, ln3_g, ln3_b]
    rows = [r.reshape(1, D) for r in rows] + [b_gates.reshape(2, D), b_s.reshape(1, D), jnp.zeros((5, D), F32),
                                             w_s.reshape(128, D)]
    return jnp.concatenate(rows, axis=0)


def _unpack_small(p):
    out = [p[i:i + 1] for i in range(8)]
    return out + [p[8:10].reshape(1, 2 * D), p[10:11].reshape(1, 8, BLK), p[16:144].reshape(1, 8, BLK, BLK)]


GROUPS = (("f1g", "f1u", "f1d"), ("w_in",), ("w_ab", "w_gb", "w_out"), ("f2g", "f2u", "f2d"))


def _local_step(x, pos_f, target, P, weights_of, grads_ready, flush, small_ready):
    invf = ROPE_THETA ** (-jnp.arange(0, DH, 2, dtype=F32) / DH)
    invf = jnp.tile(invf, 4).reshape(1, 128)
    b_s_t = P["gmlp_b_s"].T

    W = dict(weights_of(0, []))
    h1, h1b, xh1, rstd1, a1, b1, h1t = _ffn_fwd(x, W["f1g"], W["f1u"], W["f1d"], P["ln1_g"], P["ln1_b"], "ffn1_fwd",
                                                emit_t=True)
    W.update(weights_of(1, [h1b]))
    qkv_c = _proj_qkv_rope(h1b, W["w_in"], pos_f, invf, "proj_qkv_rope")
    z = _matmul(h1b, W["w_in"], "nn", "proj_z", n=2 * GW, b_col0=3 * ATT_W, tm=S, tn=512)
    gl = _matmul(h1b, W["w_in"], "nn", "proj_gates", n=2 * D, b_col0=3 * ATT_W + 2 * GW, tm=S, tn=512)
    og = [_attn_fwd(gi, qkv_c[gi], "attn_fwd_g%d" % gi) for gi in range(NG)]
    y_attn, y_attn_t, lse = _attn_combine([o for o, _ in og], [l for _, l in og], "attn_combine")
    y_gmlp, y_gmlp_t = _gmlp_fwd(z, P["gmlp_ln_g"], P["gmlp_ln_b"], P["gmlp_w_s"], b_s_t, "gmlp_fwd")
    W.update(weights_of(2, [y_gmlp]))
    br_a = _matmul(y_attn, W["w_ab"], "nn", "branch_attn", n=D, tm=1024, tn=D)
    br_b = _matmul(y_gmlp, W["w_gb"], "nn", "branch_gmlp", n=D, tm=1024, tn=D)
    merged, merged_t = _merge_fwd(br_a, br_b, gl, P["b_gates"], "merge_fwd")
    mix = _matmul(merged, W["w_out"], "nn", "mix_out", n=D, tm=1024, tn=D)
    h2, h2b, xh2, rstd2 = _resid_ln(h1, mix, P["ln2_g"], P["ln2_b"], "resid_ln2")
    W.update(weights_of(3, [h2b]))
    y, _, xh3, rstd3, a2, b2 = _ffn_fwd(h2, W["f2g"], W["f2u"], W["f2d"], P["ln3_g"], P["ln3_b"], "ffn2_fwd")

    dr3, dg3, db3, loss = _ln_bwd(y, xh3, rstd3, P["ln3_g"], "loss_ln3_bwd", target=target)
    g_f2g, g_f2u, g_f2d, dh2 = _ffn_bwd(dr3, h2b, a2, b2, W["f2g"], W["f2u"], W["f2d"], "ffn2_bwd")
    tok = grads_ready(3, dict(f2g=g_f2g, f2u=g_f2u, f2d=g_f2d))
    dr2, dg2, db2 = _ln_bwd(dh2, xh2, rstd2, P["ln2_g"], "ln2_bwd", after=tok)
    g_wout = _wgrad(merged_t, dr2, 128, D, "dw_out", row_sharded=True)
    dmerged = _matmul(dr2, W["w_out"], "nt", "dmerged", n=D, tm=1024, tn=D)
    dab, dbb, dglb, dbg = _merge_bwd(dmerged, br_a, br_b, gl, P["b_gates"], "merge_bwd")
    tok = flush([dab])
    g_wab = _wgrad(y_attn_t, dab, GRP_W // 2, 256, "dw_attn_branch", row_sharded=False, after=tok)
    g_wgb = _wgrad(y_gmlp_t, dbb, 128, D, "dw_gmlp_branch", row_sharded=True)
    tok = grads_ready(2, dict(w_ab=g_wab, w_gb=g_wgb, w_out=g_wout))
    dy_attn = _matmul(dab, W["w_ab"], "nt", "dy_attn", n=GRP_W, tm=1024, tn=GRP_W, after=tok)
    dy_gmlp = _matmul(dbb, W["w_gb"], "nt", "dy_gmlp", n=GW, tm=1024, tn=GW)
    dzb, dws, dbs_t, dgln_g, dgln_b = _gmlp_bwd(z, dy_gmlp, P["gmlp_ln_g"], P["gmlp_ln_b"], P["gmlp_w_s"], b_s_t,
                                                 "gmlp_bwd")
    cls = _class_order([dy_attn, y_attn, lse], "attn_class_order")
    dqkv_c = []
    for gi in range(NG):
        dy_c, y_c, lse_c = [t[None] if gi == 0 else cls[2 * a + gi - 1] for a, t in enumerate((dy_attn, y_attn, lse))]
        dqkv_c.append(_attn_bwd(gi, qkv_c[gi], dy_c, y_c, lse_c, "attn_bwd_g%d" % gi))
    dqkvb = _rope_bwd(dqkv_c, pos_f, invf, "rope_bwd")
    dproj = jnp.concatenate([dqkvb, dzb, dglb], axis=1)
    tok = flush([dproj])
    g_win = _wgrad(h1t, dproj, D // 2, IN_SH, "dw_in", row_sharded=False, after=tok)
    tok = grads_ready(1, dict(w_in=g_win))
    dh1 = _matmul(dproj, W["w_in"], "nt", "dh1", n=D, tm=1024, tn=D, tk=IN_SH, add=dr2, add_scale=ALPHA, after=tok)
    dr1, dg1, db1 = _ln_bwd(dh1, xh1, rstd1, P["ln1_g"], "ln1_bwd")
    tok = flush([dr1])
    tok = tok + small_ready(_pack_small(dg1, db1, dgln_g, dgln_b, dg2, db2, dg3, db3, dbg, dbs_t.T, dws))
    g_f1g, g_f1u, g_f1d, dx = _ffn_bwd(dr1, x.astype(BF16), a1, b1, W["f1g"], W["f1u"], W["f1d"], "ffn1_bwd",
                                       after=tok)
    grads_ready(0, dict(f1g=g_f1g, f1u=g_f1u, f1d=g_f1d))
    flush([dx])
    return loss, dx


BIG = ("f1g", "f1u", "f1d", "w_in", "w_ab", "w_gb", "w_out", "f2g", "f2u", "f2d")
TRANSPOSED = ("f1g", "f1u", "f2g", "f2u")
KIND = dict(f1g="stack", f1u="stack", f1d="stack", w_in="col", w_ab="col", w_gb="stack", w_out="stack",
            f2g="stack", f2u="stack", f2d="stack")


def kernel(x, positions, ffn1_w_gate, ffn1_w_up, ffn1_w_down, ln1_g, ln1_b, w_in, b_gates, gmlp_ln_g, gmlp_ln_b, gmlp_w_s, gmlp_b_s, w_attn_branch, w_gmlp_branch, w_out, ln2_g, ln2_b, ffn2_w_gate, ffn2_w_up, ffn2_w_down, ln3_g, ln3_b, loss_target, m_ffn1_w_gate, m_ffn1_w_up, m_ffn1_w_down, m_ln1_g, m_ln1_b, m_w_in, m_b_gates, m_gmlp_ln_g, m_gmlp_ln_b, m_gmlp_w_s, m_gmlp_b_s, m_w_attn_branch, m_w_gmlp_branch, m_w_out, m_ln2_g, m_ln2_b, m_ffn2_w_gate, m_ffn2_w_up, m_ffn2_w_down, m_ln3_g, m_ln3_b, v_ffn1_w_gate, v_ffn1_w_up, v_ffn1_w_down, v_ln1_g, v_ln1_b, v_w_in, v_b_gates, v_gmlp_ln_g, v_gmlp_ln_b, v_gmlp_w_s, v_gmlp_b_s, v_w_attn_branch, v_w_gmlp_branch, v_w_out, v_ln2_g, v_ln2_b, v_ffn2_w_gate, v_ffn2_w_up, v_ffn2_w_down, v_ln3_g, v_ln3_b):
    cx, cy, cc = lax.axis_index("x"), lax.axis_index("y"), lax.axis_index("c")
    pos = jnp.stack([cc, 2 * cx + cy, 2 * (1 - cx) + cy, 2 * cx + 1 - cy, 2 * (1 - cx) + 1 - cy]).astype(jnp.int32)

    w_sh = dict(f1g=ffn1_w_gate, f1u=ffn1_w_up, f1d=ffn1_w_down, w_in=w_in, w_ab=w_attn_branch,
                w_gb=w_gmlp_branch, w_out=w_out, f2g=ffn2_w_gate, f2u=ffn2_w_up, f2d=ffn2_w_down)
    m_sh = dict(f1g=m_ffn1_w_gate, f1u=m_ffn1_w_up, f1d=m_ffn1_w_down, w_in=m_w_in, w_ab=m_w_attn_branch,
                w_gb=m_w_gmlp_branch, w_out=m_w_out, f2g=m_ffn2_w_gate, f2u=m_ffn2_w_up, f2d=m_ffn2_w_down)
    v_sh = dict(f1g=v_ffn1_w_gate, f1u=v_ffn1_w_up, f1d=v_ffn1_w_down, w_in=v_w_in, w_ab=v_w_attn_branch,
                w_gb=v_w_gmlp_branch, w_out=v_w_out, f2g=v_ffn2_w_gate, f2u=v_ffn2_w_up, f2d=v_ffn2_w_down)
    w_sh = {k: (v[0].T if k in TRANSPOSED else v[0]) for k, v in w_sh.items()}
    m_sh = {k: (v[0].T if k in TRANSPOSED else v[0]) for k, v in m_sh.items()}
    v_sh = {k: (v[0].T if k in TRANSPOSED else v[0]) for k, v in v_sh.items()}

    started, tokens = [], []
    for gi, names in enumerate(GROUPS):
        placed = [_place_shard(w_sh[k], KIND[k], pos, "place_" + k) for k in names]
        fulls, ssem, rsem, token = _gather_start(placed, [KIND[k] for k in names], [w_sh[k].shape for k in names],
                                                 tokens[-1:], "gather_start_g%d" % gi)
        started.append((fulls, ssem, rsem))
        tokens.append(token)

    def weights_of(gi, after):
        names = GROUPS[gi]
        kinds, dims = [KIND[k] for k in names], [w_sh[k].shape for k in names]
        fulls, ssem, rsem = started[gi]
        fulls = _gather_wait(fulls, ssem, rsem, kinds, dims, list(after) + (tokens if gi == 0 else []),
                             "gather_wait_g%d" % gi)
        fulls = _gather_forward(fulls, kinds, dims, "gather_forward_g%d" % gi)
        return {k: (f.reshape(D, D) if k in ("w_gb", "w_out") else f) for k, f in zip(names, fulls)}

    pending, inflight = [], {}

    def grads_ready(gi, gd):
        grads = [gd[k] for k in GROUPS[gi]]
        lands = [lax.empty(g.shape[1:], F32) for g in grads]
        grads, lands, ssem, rsem, token = _pair_start(grads, lands, "rs_pair_start_g%d" % gi)
        pending.append((gi, grads, lands, ssem, rsem))
        return [token]

    def flush(after):
        gi, grads, lands, ssem, rsem = pending.pop()
        names = GROUPS[gi]
        grads, recv_a = _pair_wait(grads, lands, ssem, rsem, after, "rs_pair_wait_g%d" % gi)
        psums = [_pair_sum(g, r, pos, "rs_pair_sum_" + k) for g, r, k in zip(grads, recv_a, names)]
        lands = [lax.empty((3,) + p.shape[1:], BF16) for p in psums]
        psums, lands, ssem, rsem, token = _chip_start(psums, lands, "rs_chip_start_g%d" % gi)
        inflight[gi] = (grads, recv_a, psums, lands, ssem, rsem, token)
        return [token]

    P = dict(ln1_g=ln1_g, ln1_b=ln1_b, ln2_g=ln2_g, ln2_b=ln2_b, ln3_g=ln3_g, ln3_b=ln3_b, b_gates=b_gates,
             gmlp_ln_g=gmlp_ln_g, gmlp_ln_b=gmlp_ln_b, gmlp_w_s=gmlp_w_s[0], gmlp_b_s=gmlp_b_s[0])
    pos_f = positions.reshape(S, 1).astype(F32)
    small_state = []

    def small_ready(packed):
        land = jnp.zeros((8, SMALL_ROWS, D), F32)
        packed, land, ssem, rsem, token = _small_start(packed, land, "small_start")
        small_state.append((packed, land, ssem, rsem))
        return [token]

    loss_part, dx = _local_step(x[0], pos_f, loss_target[0], P, weights_of, grads_ready, flush, small_ready)
    loss = lax.psum(loss_part[0, 0], ("x", "y", "c"))

    g_out, d_out, m_out, v_out = {}, {}, {}, {}

    def finish(gi, after):
        grads, recv_a, psums, lands, ssem, rsem, token = inflight[gi]
        recv_b = _chip_wait(psums, lands, ssem, rsem, after + [inflight[0][6]], "rs_chip_wait_g%d" % gi)
        halves = [_owner_sum(g, ra, rb, pos, "rs_owner_sum_" + k)
                  for g, ra, rb, k in zip(grads, recv_a, recv_b, GROUPS[gi])]
        reduced = _sibling_allgather(halves, "rs_sibling_allgather_g%d" % gi)
        for k, gfull in zip(GROUPS[gi], reduced):
            res = _adamw(w_sh[k], gfull.reshape(w_sh[k].shape), m_sh[k], v_sh[k], "adamw_" + k)
            after = [res[1]]
            if k in TRANSPOSED:
                res = [r.T for r in res]
            g_out[k], d_out[k], m_out[k], v_out[k] = [r[None] for r in res]
        return after

    after = []
    for gi in (3, 2, 1):
        after = finish(gi, after)

    small, parts = _small_wait(*small_state[0], after, "small_wait")
    sp = (ln1_g, ln1_b, gmlp_ln_g, gmlp_ln_b, ln2_g, ln2_b, ln3_g, ln3_b, b_gates, gmlp_b_s, gmlp_w_s)
    sm = (m_ln1_g, m_ln1_b, m_gmlp_ln_g, m_gmlp_ln_b, m_ln2_g, m_ln2_b, m_ln3_g, m_ln3_b, m_b_gates, m_gmlp_b_s,
          m_gmlp_w_s)
    sv = (v_ln1_g, v_ln1_b, v_gmlp_ln_g, v_gmlp_ln_b, v_ln2_g, v_ln2_b, v_ln3_g, v_ln3_b, v_b_gates, v_gmlp_b_s,
          v_gmlp_w_s)
    sg, sd, smn, svn = _small_sum_adamw(parts, small, pos, _pack_small(*sp), _pack_small(*sm), _pack_small(*sv),
                                        "small_adamw")
    names = ("ln1_g", "ln1_b", "gmlp_ln_g", "gmlp_ln_b", "ln2_g", "ln2_b", "ln3_g", "ln3_b", "b_gates", "gmlp_b_s",
             "gmlp_w_s")
    for dst, packed in ((g_out, sg), (d_out, sd), (m_out, smn), (v_out, svn)):
        for nm, val in zip(names, _unpack_small(packed)):
            dst[nm] = val
    finish(0, [sg])

    order = ("f1g", "f1u", "f1d", "ln1_g", "ln1_b", "w_in", "b_gates", "gmlp_ln_g", "gmlp_ln_b", "gmlp_w_s", "gmlp_b_s",
             "w_ab", "w_gb", "w_out", "ln2_g", "ln2_b", "f2g", "f2u", "f2d", "ln3_g", "ln3_b")
    outs = [loss, dx[None]]
    for dst in (g_out, d_out, m_out, v_out):
        outs += [dst[k] for k in order]
    return tuple(outs)
```

```python
import functools
import math

import jax
import jax.numpy as jnp
from jax import lax
from jax.experimental import pallas as pl
from jax.experimental.pallas import tpu as pltpu

F32 = jnp.float32
BF16 = jnp.bfloat16

S = 2048
D = 1024
NSH = 4
FSH = 704
ATT_W = 1536
GRP_W = 512
NG = 3
NH = 8
DH = 64
BLK = 128
NBLK = S // BLK
GW = 1024
IN_W = 8704
IN_SH = IN_W // NSH
ALPHA = 2.0 ** 0.25
LN_EPS = 1e-5
ROPE_THETA = 10000.0
DILATIONS = (1, 4, 16)
ADAM_LR, ADAM_B1, ADAM_B2, ADAM_EPS, ADAM_WD, ADAM_STEP = 0.001, 0.9, 0.999, 1e-08, 0.01, 10
SMALL_ROWS = 144
MESH_T = pl.DeviceIdType.MESH
MIB = 1024 * 1024
NEG_INF = float("-inf")


def _cp(sem, vmem_mib=48):
    return pltpu.CompilerParams(dimension_semantics=sem, vmem_limit_bytes=vmem_mib * MIB)


def _ln_stats(r):
    mu = jnp.mean(r, axis=-1, keepdims=True)
    xc = r - mu
    var = jnp.mean(xc * xc, axis=-1, keepdims=True)
    rstd = lax.rsqrt(var + LN_EPS)
    return xc * rstd, rstd


def _ln_dx(dxh, xh, rstd):
    m1 = jnp.mean(dxh, axis=-1, keepdims=True)
    m2 = jnp.mean(dxh * xh, axis=-1, keepdims=True)
    return rstd * (dxh - m1 - xh * m2)


def _dot_nt(a, b):
    return lax.dot_general(a, b, (((1,), (1,)), ((), ())), preferred_element_type=F32)


def _dot_tn(a, b):
    return lax.dot_general(a, b, (((0,), (0,)), ((), ())), preferred_element_type=F32)


def _dot(a, b):
    return jnp.dot(a, b, preferred_element_type=F32)


def _ffn_fwd(xin, wgt, wut, wd, ln_g, ln_b, name, emit_t=False):
    tm = 512

    def body(x_ref, wg_ref, wu_ref, wd_ref, g_ref, b_ref, *rest):
        if emit_t:
            h_ref, hb_ref, xh_ref, rstd_ref, a_ref, bb_ref, ht_ref, acc_ref = rest
        else:
            h_ref, hb_ref, xh_ref, rstd_ref, a_ref, bb_ref, acc_ref = rest
        j = pl.program_id(1)
        xb = x_ref[...].astype(BF16)
        a = _dot_nt(xb, wg_ref[...])
        b = _dot_nt(xb, wu_ref[...])
        a_ref[...] = a.astype(BF16)
        bb_ref[...] = b.astype(BF16)
        s = (a * jax.nn.sigmoid(a)) * b
        f = _dot(s.astype(BF16), wd_ref[...])

        @pl.when(j == 0)
        def _():
            acc_ref[...] = f

        @pl.when(j > 0)
        def _():
            acc_ref[...] += f

        @pl.when(j == NSH - 1)
        def _():
            r = ALPHA * x_ref[...] + 0.5 * acc_ref[...]
            xh, rstd = _ln_stats(r)
            h = xh * g_ref[...] + b_ref[...]
            h_ref[...] = h
            hb_ref[...] = h.astype(BF16)
            xh_ref[...] = xh
            rstd_ref[...] = rstd
            if emit_t:
                ht_ref[...] = h.T.astype(BF16)

    row = pl.BlockSpec((tm, D), lambda i, j: (i, 0))
    vec = pl.BlockSpec((1, D), lambda i, j: (0, 0))
    wsp = pl.BlockSpec((None, FSH, D), lambda i, j: (j, 0, 0))
    ab = pl.BlockSpec((None, tm, FSH), lambda i, j: (j, i, 0))
    out_specs = [row, row, row, pl.BlockSpec((tm, 1), lambda i, j: (i, 0)), ab, ab]
    out_shape = [jax.ShapeDtypeStruct((S, D), F32), jax.ShapeDtypeStruct((S, D), BF16),
                 jax.ShapeDtypeStruct((S, D), F32), jax.ShapeDtypeStruct((S, 1), F32),
                 jax.ShapeDtypeStruct((NSH, S, FSH), BF16), jax.ShapeDtypeStruct((NSH, S, FSH), BF16)]
    if emit_t:
        out_specs.append(pl.BlockSpec((D, tm), lambda i, j: (0, i)))
        out_shape.append(jax.ShapeDtypeStruct((D, S), BF16))
    return pl.pallas_call(
        body, name=name, grid=(S // tm, NSH),
        in_specs=[row, wsp, wsp, wsp, vec, vec], out_specs=out_specs, out_shape=out_shape,
        scratch_shapes=[pltpu.VMEM((tm, D), F32)],
        compiler_params=_cp(("parallel", "arbitrary")),
    )(xin, wgt, wut, wd, ln_g, ln_b)


def _ffn_bwd(dr, xin_b, a, b, wgt, wut, wd, name, after=()):
    tm = 512
    ni = S // tm
    hr = FSH // 2

    def body(dr_ref, a_ref, b_ref, wg_ref, wu_ref, wd_ref, x_hbm, *rest):
        dwg_hbm, dwu_hbm, dwd_hbm, dx_hbm, dx_acc, da_all, db_all, s_all, df_all, x_all, res_buf, sems = rest[len(after):]
        j = pl.program_id(0)
        i = pl.program_id(1)
        rows = pl.ds(pl.multiple_of(i * tm, tm), tm)

        @pl.when(jnp.logical_and(j == 0, i == 0))
        def _():
            cp = pltpu.make_async_copy(x_hbm, x_all, sems.at[0])
            cp.start()
            cp.wait()

        drv = dr_ref[...]
        df = (0.5 * drv).astype(BF16)

        @pl.when(j == 0)
        def _():
            df_all[rows, :] = df

        ds = jnp.concatenate([_dot_nt(df, wd_ref[0:384, :]), _dot_nt(df, wd_ref[384:FSH, :])], axis=1)
        av = a_ref[...].astype(F32)
        bv = b_ref[...].astype(F32)
        sig = jax.nn.sigmoid(av)
        sl = av * sig
        da = (ds * bv * (sig * (1.0 + av * (1.0 - sig)))).astype(BF16)
        db = (ds * sl).astype(BF16)
        da_all[rows, :] = da
        db_all[rows, :] = db
        s_all[rows, :] = (sl * bv).astype(BF16)
        dx = _dot(da, wg_ref[...]) + _dot(db, wu_ref[...])

        @pl.when(j == 0)
        def _():
            dx_acc[rows, :] = ALPHA * drv + dx

        @pl.when(j > 0)
        def _():
            dx_acc[rows, :] += dx

        @pl.when(i == ni - 1)
        def _():
            copies = []
            for n, (lhs, rhs, out) in enumerate(((da_all, x_all, dwg_hbm), (db_all, x_all, dwu_hbm),
                                                 (s_all, df_all, dwd_hbm))):
                slot = n % 2
                if n >= 2:
                    for cp in copies[2 * (n - 2): 2 * (n - 2) + 2]:
                        cp.wait()
                res_buf[slot] = _dot_tn(lhs[...], rhs[...])
                for h in range(2):
                    cp = pltpu.make_async_copy(res_buf.at[slot, pl.ds(h * hr, hr), :], out.at[h, j],
                                               sems.at[1 + 2 * slot + h])
                    cp.start()
                    copies.append(cp)
            for cp in copies[2:]:
                cp.wait()

        @pl.when(jnp.logical_and(j == NSH - 1, i == ni - 1))
        def _():
            cp = pltpu.make_async_copy(dx_acc, dx_hbm, sems.at[0])
            cp.start()
            cp.wait()

    row = pl.BlockSpec((tm, D), lambda j, i: (i, 0))
    wsp = pl.BlockSpec((None, FSH, D), lambda j, i: (j, 0, 0))
    ab = pl.BlockSpec((None, tm, FSH), lambda j, i: (j, i, 0))
    dwshape = jax.ShapeDtypeStruct((2, NSH, hr, D), F32)
    return pl.pallas_call(
        body, name=name, grid=(NSH, ni),
        in_specs=[row, ab, ab, wsp, wsp, wsp, ANY] + [ANY] * len(after),
        out_specs=[ANY, ANY, ANY, ANY],
        out_shape=[dwshape, dwshape, dwshape, jax.ShapeDtypeStruct((S, D), F32)],
        scratch_shapes=[pltpu.VMEM((S, D), F32), pltpu.VMEM((S, FSH), BF16), pltpu.VMEM((S, FSH), BF16),
                        pltpu.VMEM((S, FSH), BF16), pltpu.VMEM((S, D), BF16), pltpu.VMEM((S, D), BF16),
                        pltpu.VMEM((2, FSH, D), F32), pltpu.SemaphoreType.DMA((5,))],
        compiler_params=_cp(("arbitrary", "arbitrary"), vmem_mib=58),
    )(dr, a, b, wgt, wut, wd, xin_b, *after)


def _matmul(a, b, mode, name, *, n, tm=512, tn=512, tk=None, b_col0=0, add=None, add_scale=1.0, out_dtype=F32,
            after=()):
    m, ka = a.shape
    tk = ka if tk is None else tk
    nk = ka // tk
    assert m % tm == 0 and n % tn == 0 and ka % tk == 0 and b_col0 % tn == 0
    off = b_col0 // tn
    na = len(after)

    def body(*refs):
        refs = refs[na:]
        if add is None:
            a_ref, b_ref, o_ref = refs[:3]
            add_ref = None
            rest = refs[3:]
        else:
            a_ref, b_ref, add_ref, o_ref = refs[:4]
            rest = refs[4:]
        k = pl.program_id(2)
        av = a_ref[...].astype(BF16)
        bv = b_ref[...].astype(BF16)
        p = _dot(av, bv) if mode == "nn" else _dot_nt(av, bv)

        def finish(acc):
            if add_ref is not None:
                acc = acc + add_scale * add_ref[...]
            o_ref[...] = acc.astype(out_dtype)

        if nk == 1:
            finish(p)
        else:
            acc_ref = rest[0]

            @pl.when(k == 0)
            def _():
                acc_ref[...] = p

            @pl.when(k > 0)
            def _():
                acc_ref[...] += p

            @pl.when(k == nk - 1)
            def _():
                finish(acc_ref[...])

    a_spec = pl.BlockSpec((tm, tk), lambda i, j, k: (i, k))
    if mode == "nn":
        b_spec = pl.BlockSpec((tk, tn), lambda i, j, k: (k, j + off))
    else:
        b_spec = pl.BlockSpec((tn, tk), lambda i, j, k: (j, k))
    o_spec = pl.BlockSpec((tm, tn), lambda i, j, k: (i, j))
    in_specs = [pl.BlockSpec(memory_space=pl.ANY)] * na + [a_spec, b_spec] + ([o_spec] if add is not None else [])
    args = tuple(after) + (a, b) + ((add,) if add is not None else ())
    return pl.pallas_call(
        body, name=name, grid=(m // tm, n // tn, nk),
        in_specs=in_specs, out_specs=o_spec,
        out_shape=jax.ShapeDtypeStruct((m, n), out_dtype),
        scratch_shapes=[pltpu.VMEM((tm, tn), F32)] if nk > 1 else [],
        compiler_params=_cp(("parallel", "parallel", "arbitrary")),
    )(*args)


def _wgrad(xt, y, rh, c, name, row_sharded, after=()):
    na = len(after)
    if row_sharded:
        def body(x_ref, y_ref, *rest):
            o_ref = rest[na]
            res = _dot(x_ref[...], y_ref[...].astype(BF16))
            for j in range(NSH):
                for h in range(2):
                    o_ref[h, j] = res[(2 * j + h) * rh:(2 * j + h + 1) * rh, :]

        grid = (1,)
        in_specs = [pl.BlockSpec((2 * NSH * rh, S), lambda g: (0, 0)), pl.BlockSpec((S, c), lambda g: (0, 0))]
        out_specs = pl.BlockSpec((2, NSH, rh, c), lambda g: (0, 0, 0, 0))
        sem = ("arbitrary",)
    else:
        def body(x_ref, y_ref, *rest):
            rest[na][...] = _dot(x_ref[...], y_ref[...].astype(BF16))

        grid = (2, NSH)
        in_specs = [pl.BlockSpec((rh, S), lambda h, j: (h, 0)), pl.BlockSpec((S, c), lambda h, j: (0, j))]
        out_specs = pl.BlockSpec((None, None, rh, c), lambda h, j: (h, j, 0, 0))
        sem = ("parallel", "parallel")
    return pl.pallas_call(
        body, name=name, grid=grid, in_specs=in_specs + [pl.BlockSpec(memory_space=pl.ANY)] * na, out_specs=out_specs,
        out_shape=jax.ShapeDtypeStruct((2, NSH, rh, c), F32),
        compiler_params=_cp(sem, vmem_mib=56),
    )(xt, y, *after)


def _resid_ln(res, f, ln_g, ln_b, name):
    tm = 512

    def body(res_ref, f_ref, g_ref, b_ref, h_ref, hb_ref, xh_ref, rstd_ref):
        r = ALPHA * res_ref[...] + f_ref[...]
        xh, rstd = _ln_stats(r)
        h = xh * g_ref[...] + b_ref[...]
        h_ref[...] = h
        hb_ref[...] = h.astype(BF16)
        xh_ref[...] = xh
        rstd_ref[...] = rstd

    row = pl.BlockSpec((tm, D), lambda i: (i, 0))
    vec = pl.BlockSpec((1, D), lambda i: (0, 0))
    return pl.pallas_call(
        body, name=name, grid=(S // tm,),
        in_specs=[row, row, vec, vec],
        out_specs=[row, row, row, pl.BlockSpec((tm, 1), lambda i: (i, 0))],
        out_shape=[jax.ShapeDtypeStruct((S, D), F32), jax.ShapeDtypeStruct((S, D), BF16),
                   jax.ShapeDtypeStruct((S, D), F32), jax.ShapeDtypeStruct((S, 1), F32)],
        compiler_params=_cp(("parallel",)),
    )(res, f, ln_g, ln_b)


def _ln_bwd(dout, xh, rstd, ln_g, name, target=None, after=()):
    tm = 512
    with_loss = target is not None
    na = len(after)

    def body(*refs):
        refs = refs[na:]
        if with_loss:
            y_ref, t_ref, xh_ref, rstd_ref, g_ref, dr_ref, dg_ref, db_ref, loss_ref = refs
            err = y_ref[...] - t_ref[...]
            dy = err * (1.0 / D)
        else:
            y_ref, xh_ref, rstd_ref, g_ref, dr_ref, dg_ref, db_ref = refs
            dy = y_ref[...]
        i = pl.program_id(0)
        xh = xh_ref[...]
        dr_ref[...] = _ln_dx(dy * g_ref[...], xh, rstd_ref[...])
        dg = jnp.sum(dy * xh, axis=0, keepdims=True)
        db = jnp.sum(dy, axis=0, keepdims=True)

        @pl.when(i == 0)
        def _():
            dg_ref[...] = dg
            db_ref[...] = db

        @pl.when(i > 0)
        def _():
            dg_ref[...] += dg
            db_ref[...] += db

        if with_loss:
            part = 0.5 * jnp.sum(jnp.mean(err * err, axis=-1, keepdims=True), axis=0, keepdims=True)
            part = jnp.broadcast_to(part, (8, 128))

            @pl.when(i == 0)
            def _():
                loss_ref[...] = part

            @pl.when(i > 0)
            def _():
                loss_ref[...] += part

    row = pl.BlockSpec((tm, D), lambda i: (i, 0))
    vec = pl.BlockSpec((1, D), lambda i: (0, 0))
    col = pl.BlockSpec((tm, 1), lambda i: (i, 0))
    in_specs = [pl.BlockSpec(memory_space=pl.ANY)] * na + [row] + ([row] if with_loss else []) + [row, col, vec]
    out_specs = [row, vec, vec] + ([pl.BlockSpec((8, 128), lambda i: (0, 0))] if with_loss else [])
    out_shape = [jax.ShapeDtypeStruct((S, D), F32), jax.ShapeDtypeStruct((1, D), F32),
                 jax.ShapeDtypeStruct((1, D), F32)] + ([jax.ShapeDtypeStruct((8, 128), F32)] if with_loss else [])
    args = tuple(after) + (dout,) + ((target,) if with_loss else ()) + (xh, rstd, ln_g)
    return pl.pallas_call(
        body, name=name, grid=(S // tm,), in_specs=in_specs, out_specs=out_specs, out_shape=out_shape,
        compiler_params=_cp(("arbitrary",)),
    )(*args)


ROPE_TM = 256


def _rope_tables(pos_ref, invf_ref, sign):
    ang = pos_ref[...] * invf_ref[...]
    lane = lax.broadcasted_iota(jnp.int32, ang.shape, 1)
    first = (lane % DH) < (DH // 2)
    sinv = jnp.sin(ang) * sign
    return first, jnp.cos(ang), jnp.where(first, -sinv, sinv)


def _rotate(x, first, cosf, sinf):
    return x * cosf + jnp.where(first, pltpu.roll(x, 96, 1), pltpu.roll(x, 32, 1)) * sinf


def _proj_qkv_rope(hb, w_in, pos_f, invf, name):
    tm = 2 * ROPE_TM

    def body(h_ref, w_ref, pos_ref, invf_ref, o0_ref, o1_ref, o2_ref, buf_ref):
        rot = pl.program_id(1) < 2
        first, cosf, sinf = _rope_tables(pos_ref, invf_ref, 1.0)
        cosf = jnp.where(rot, cosf, 1.0)
        sinf = jnp.where(rot, sinf, 0.0)
        acc = _dot(h_ref[...], w_ref[...])
        for gi, (d, o_ref) in enumerate(zip(DILATIONS, (o0_ref, o1_ref, o2_ref))):
            for ch in range(GRP_W // 128):
                cols = slice(ch * 128, (ch + 1) * 128)
                x = _rotate(acc[:, gi * GRP_W + ch * 128: gi * GRP_W + (ch + 1) * 128], first, cosf, sinf)
                if d == 1:
                    o_ref[0, :, cols] = x.astype(BF16)
                else:
                    buf_ref[...] = x
                    for r in range(d):
                        o_ref[r, :, cols] = buf_ref[pl.ds(r, tm // d, stride=d), :].astype(BF16)

    return pl.pallas_call(
        body, name=name, grid=(S // tm, 3),
        in_specs=[pl.BlockSpec((tm, D), lambda i, s: (i, 0)), pl.BlockSpec((D, ATT_W), lambda i, s: (0, s)),
                  pl.BlockSpec((tm, 1), lambda i, s: (i, 0)), pl.BlockSpec((1, 128), lambda i, s: (0, 0))],
        out_specs=[pl.BlockSpec((d, tm // d, GRP_W), lambda i, s: (0, i, s)) for d in DILATIONS],
        out_shape=[jax.ShapeDtypeStruct((d, S // d, 3 * GRP_W), BF16) for d in DILATIONS],
        scratch_shapes=[pltpu.VMEM((tm, 128), F32)],
        compiler_params=_cp(("parallel", "parallel")),
    )(hb, w_in, pos_f, invf)


def _rope_bwd(dqkv_c, pos_f, invf, name):
    tm = ROPE_TM

    def body(*refs):
        g_refs, (pos_ref, invf_ref, o_ref, buf_ref) = refs[:9], refs[9:]
        first, cosf, sinf = _rope_tables(pos_ref, invf_ref, -1.0)
        for sec in range(3):
            for gi, d in enumerate(DILATIONS):
                g_ref = g_refs[3 * gi + sec]
                for ch in range(GRP_W // 128):
                    cols = slice(ch * 128, (ch + 1) * 128)
                    if d == 1:
                        x = g_ref[0, :, cols]
                    else:
                        for r in range(d):
                            buf_ref[pl.ds(r, tm // d, stride=d), :] = g_ref[r, :, cols]
                        x = buf_ref[...]
                    if sec < 2:
                        x = _rotate(x, first, cosf, sinf)
                    dst = sec * ATT_W + gi * GRP_W + ch * 128
                    o_ref[:, dst:dst + 128] = x.astype(BF16)

    g_specs = [pl.BlockSpec((d, tm // d, GRP_W), lambda i: (0, i, 0)) for d in DILATIONS for _ in range(3)]
    return pl.pallas_call(
        body, name=name, grid=(S // tm,),
        in_specs=g_specs + [pl.BlockSpec((tm, 1), lambda i: (i, 0)), pl.BlockSpec((1, 128), lambda i: (0, 0))],
        out_specs=pl.BlockSpec((tm, 3 * ATT_W), lambda i: (i, 0)),
        out_shape=jax.ShapeDtypeStruct((S, 3 * ATT_W), BF16),
        scratch_shapes=[pltpu.VMEM((tm, 128), F32)],
        compiler_params=_cp(("parallel",)),
    )(*[g for grp in dqkv_c for g in grp], pos_f, invf)


def _class_order(ts, name):
    tm = ROPE_TM
    n = len(ts)

    def body(*refs):
        buf_ref = refs[3 * n]
        for a in range(n):
            for ch in range(GRP_W // 128):
                cols = slice(ch * 128, (ch + 1) * 128)
                buf_ref[...] = refs[a][:, cols]
                for b, d in enumerate(DILATIONS[1:]):
                    for r in range(d):
                        refs[n + 2 * a + b][r, :, cols] = buf_ref[pl.ds(r, tm // d, stride=d), :]

    return pl.pallas_call(
        body, name=name, grid=(S // tm,),
        in_specs=[pl.BlockSpec((tm, GRP_W), lambda i: (i, 0))] * n,
        out_specs=[pl.BlockSpec((d, tm // d, GRP_W), lambda i: (0, i, 0)) for _ in range(n) for d in DILATIONS[1:]],
        out_shape=[jax.ShapeDtypeStruct((d, S // d, GRP_W), F32) for _ in range(n) for d in DILATIONS[1:]],
        scratch_shapes=[pltpu.VMEM((tm, 128), F32)],
        compiler_params=_cp(("parallel",)),
    )(*ts)


def _own_lanes(h):
    return (lax.broadcasted_iota(jnp.int32, (1, 2 * DH), 1) // DH) == (h % 2)


def _heads(ref):
    out = []
    for h in range(NH):
        pair = ref[:, (h // 2) * 2 * DH:(h // 2 + 1) * 2 * DH]
        out.append(jnp.where(_own_lanes(h), pair, jnp.zeros_like(pair)))
    return jnp.stack(out)


def _unheads(t3):
    return jnp.concatenate([t3[2 * p] + t3[2 * p + 1] for p in range(NH // 2)], axis=1)


def _bdot_nt(a, b):
    return lax.dot_general(a, b, (((2,), (2,)), ((0,), (0,))), preferred_element_type=F32)


def _bdot(a, b):
    return lax.dot_general(a, b, (((2,), (1,)), ((0,), (0,))), preferred_element_type=F32)


def _bdot_tn(a, b):
    return lax.dot_general(a, b, (((1,), (1,)), ((0,), (0,))), preferred_element_type=F32)


def _attn_fwd(gi, qkv_c, name):
    d = DILATIONS[gi]
    nblk = S // d // BLK

    def body(*refs):
        if nblk > 1:
            q_ref, kc_ref, kp_ref, vc_ref, vp_ref, o_ref, lse_ref = refs
            has_prev = pl.program_id(1) != 0
        else:
            q_ref, kc_ref, vc_ref, o_ref, lse_ref = refs
        qi = lax.broadcasted_iota(jnp.int32, (NH, BLK, BLK), 1)
        kj = lax.broadcasted_iota(jnp.int32, (NH, BLK, BLK), 2)
        q = _heads(q_ref)
        sc = jnp.where(kj <= qi, _bdot_nt(q, _heads(kc_ref)) * 0.125, NEG_INF)
        m = jnp.max(sc, axis=-1, keepdims=True)
        if nblk > 1:
            mask_p = jnp.logical_and(kj >= qi, has_prev)
            sp = jnp.where(mask_p, _bdot_nt(q, _heads(kp_ref)) * 0.125, NEG_INF)
            m = jnp.maximum(m, jnp.max(sp, axis=-1, keepdims=True))
        pc = jnp.exp(sc - m)
        l = jnp.sum(pc, axis=-1, keepdims=True)
        o = _bdot(pc.astype(BF16), _heads(vc_ref))
        if nblk > 1:
            pp = jnp.exp(sp - m)
            l = l + jnp.sum(pp, axis=-1, keepdims=True)
            o = o + _bdot(pp.astype(BF16), _heads(vp_ref))
        o_ref[...] = _unheads(o / l)
        lse = jnp.broadcast_to(m + jnp.log(l), (NH, BLK, 2 * DH))
        lse_ref[...] = _unheads(jnp.stack([jnp.where(_own_lanes(h), lse[h], 0.0) for h in range(NH)]))

    def cur(sec):
        return pl.BlockSpec((None, BLK, GRP_W), lambda r, n: (r, n, sec))

    def prev(sec):
        return pl.BlockSpec((None, BLK, GRP_W), lambda r, n: (r, jnp.maximum(n - 1, 0), sec))

    out = pl.BlockSpec((None, BLK, GRP_W), lambda r, n: (r, n, 0))
    shp = jax.ShapeDtypeStruct((d, S // d, GRP_W), F32)
    if nblk > 1:
        in_specs, args = [cur(0), cur(1), prev(1), cur(2), prev(2)], (qkv_c,) * 5
    else:
        in_specs, args = [cur(0), cur(1), cur(2)], (qkv_c,) * 3
    return pl.pallas_call(
        body, name=name, grid=(d, nblk), in_specs=in_specs, out_specs=[out, out], out_shape=[shp, shp],
        compiler_params=_cp(("parallel", "parallel")),
    )(*args)


def _attn_combine(os, lses, name):
    tm = ROPE_TM

    def body(o0_ref, o1_ref, o2_ref, l0_ref, l1_ref, l2_ref, y_ref, yt_ref, l_ref, buf_ref):
        def token_order(ref, d, cols, slot):
            if d == 1:
                return ref[0, :, cols]
            for r in range(d):
                buf_ref[slot, pl.ds(r, tm // d, stride=d), :] = ref[r, :, cols]
            return buf_ref[slot]

        for ch in range(GRP_W // 128):
            cols = slice(ch * 128, (ch + 1) * 128)
            o = [token_order(ref, d, cols, k) for k, (ref, d) in enumerate(zip((o0_ref, o1_ref, o2_ref), DILATIONS))]
            ls = [token_order(ref, d, cols, 3 + k)
                  for k, (ref, d) in enumerate(zip((l0_ref, l1_ref, l2_ref), DILATIONS))]
            m = jnp.maximum(jnp.maximum(ls[0], ls[1]), ls[2])
            e = [jnp.exp(l - m) for l in ls]
            den = e[0] + e[1] + e[2]
            y = (e[0] * o[0] + e[1] * o[1] + e[2] * o[2]) / den
            y_ref[:, cols] = y
            yt_ref[cols, :] = y.T.astype(BF16)
            l_ref[:, cols] = m + jnp.log(den)

    blk = pl.BlockSpec((tm, GRP_W), lambda i: (i, 0))
    cls = [pl.BlockSpec((d, tm // d, GRP_W), lambda i: (0, i, 0)) for d in DILATIONS]
    shp = jax.ShapeDtypeStruct((S, GRP_W), F32)
    return pl.pallas_call(
        body, name=name, grid=(S // tm,), in_specs=cls + cls,
        out_specs=[blk, pl.BlockSpec((GRP_W, tm), lambda i: (0, i)), blk],
        out_shape=[shp, jax.ShapeDtypeStruct((GRP_W, S), BF16), shp],
        scratch_shapes=[pltpu.VMEM((6, tm, 128), F32)],
        compiler_params=_cp(("parallel",)),
    )(*os, *lses)


def _attn_bwd(gi, qkv_c, dy_c, y_c, lse_c, name):
    d = DILATIONS[gi]
    nblk = S // d // BLK

    def body(*refs):
        if nblk > 1:
            (q_ref, qn_ref, k_ref, kp_ref, v_ref, vp_ref, dy_ref, dyn_ref, y_ref, yn_ref, l_ref, ln_ref,
             dq_ref, dk_ref, dv_ref) = refs
            n = pl.program_id(1)
            has_prev = n != 0
            has_next = n != nblk - 1
        else:
            q_ref, k_ref, v_ref, dy_ref, y_ref, l_ref, dq_ref, dk_ref, dv_ref = refs
        qi = lax.broadcasted_iota(jnp.int32, (NH, BLK, BLK), 1)
        kj = lax.broadcasted_iota(jnp.int32, (NH, BLK, BLK), 2)

        def lse_col(ref):
            return jnp.stack([ref[:, h * DH:h * DH + 1] for h in range(NH)])

        q, k, v = _heads(q_ref), _heads(k_ref), _heads(v_ref)
        dy = _heads(dy_ref)
        dd = jnp.sum(dy * _heads(y_ref), axis=-1, keepdims=True)
        lcol = lse_col(l_ref)
        dyb = dy.astype(BF16)
        p = jnp.exp(jnp.where(kj <= qi, _bdot_nt(q, k) * 0.125, NEG_INF) - lcol)
        ds = (p * (_bdot_nt(dyb, v) - dd)).astype(BF16)
        dq = _bdot(ds, k)
        dk = _bdot_tn(ds, q)
        dv = _bdot_tn(p.astype(BF16), dyb)
        if nblk > 1:
            qn, kpv, vpv = _heads(qn_ref), _heads(kp_ref), _heads(vp_ref)
            dyn = _heads(dyn_ref)
            ddn = jnp.sum(dyn * _heads(yn_ref), axis=-1, keepdims=True)
            lncol = lse_col(ln_ref)
            dynb = dyn.astype(BF16)
            mask_p = jnp.logical_and(kj >= qi, has_prev)
            pp = jnp.exp(jnp.where(mask_p, _bdot_nt(q, kpv) * 0.125, NEG_INF) - lcol)
            dsp = (pp * (_bdot_nt(dyb, vpv) - dd)).astype(BF16)
            dq = dq + _bdot(dsp, kpv)
            mask_n = jnp.logical_and(kj >= qi, has_next)
            pn = jnp.exp(jnp.where(mask_n, _bdot_nt(qn, k) * 0.125, NEG_INF) - lncol)
            dsn = (pn * (_bdot_nt(dynb, v) - ddn)).astype(BF16)
            dk = dk + _bdot_tn(dsn, qn)
            dv = dv + _bdot_tn(pn.astype(BF16), dynb)
        dq_ref[...] = _unheads(dq) * 0.125
        dk_ref[...] = _unheads(dk) * 0.125
        dv_ref[...] = _unheads(dv)

    def spec(sec, shift):
        def idx(r, n):
            return (r, jnp.clip(n + shift, 0, nblk - 1), sec)
        return pl.BlockSpec((None, BLK, GRP_W), idx)

    if nblk > 1:
        in_specs = [spec(0, 0), spec(0, 1), spec(1, 0), spec(1, -1), spec(2, 0), spec(2, -1),
                    spec(0, 0), spec(0, 1), spec(0, 0), spec(0, 1), spec(0, 0), spec(0, 1)]
        args = (qkv_c,) * 6 + (dy_c, dy_c, y_c, y_c, lse_c, lse_c)
    else:
        in_specs = [spec(0, 0), spec(1, 0), spec(2, 0), spec(0, 0), spec(0, 0), spec(0, 0)]
        args = (qkv_c, qkv_c, qkv_c, dy_c, y_c, lse_c)
    out = spec(0, 0)
    shp = jax.ShapeDtypeStruct((d, S // d, GRP_W), F32)
    return pl.pallas_call(
        body, name=name, grid=(d, nblk), in_specs=in_specs, out_specs=[out, out, out], out_shape=[shp, shp, shp],
        compiler_params=_cp(("parallel", "parallel")),
    )(*args)


_SQRT_HALF = 0.7071067811865476
_INV_SQRT_2PI = 0.3989422804014327


def _gelu(z):
    return 0.5 * z * (1.0 + lax.erf(z * _SQRT_HALF))


def _gelu_grad(z):
    return 0.5 * (1.0 + lax.erf(z * _SQRT_HALF)) + z * (jnp.exp(-0.5 * z * z) * _INV_SQRT_2PI)


def _tril_mask():
    t = lax.broadcasted_iota(jnp.int32, (BLK, BLK), 0)
    s = lax.broadcasted_iota(jnp.int32, (BLK, BLK), 1)
    return s <= t


def _groups(t):
    return jnp.stack([t[:, g * BLK:(g + 1) * BLK] for g in range(8)])


def _ungroup(t3):
    return jnp.concatenate([t3[g] for g in range(8)], axis=1)


def _group_bias(bs_ref):
    return jnp.stack([bs_ref[:, g:g + 1] for g in range(8)])


def _gmlp_fwd(z, ln_g, ln_b, w_s, b_s_t, name):
    def body(z_ref, g_ref, b_ref, ws_ref, bs_ref, y_ref, yt_ref):
        zg = _gelu(z_ref[...])
        u = zg[:, :GW]
        xh, _ = _ln_stats(zg[:, GW:])
        vn = (xh * g_ref[...] + b_ref[...]).astype(BF16)
        wt = jnp.where(_tril_mask(), ws_ref[...], 0.0).astype(BF16)
        yv = u * _ungroup(_bdot(wt, _groups(vn)) + _group_bias(bs_ref))
        y_ref[...] = yv.astype(BF16)
        yt_ref[...] = yv.T.astype(BF16)

    vec = pl.BlockSpec((1, GW), lambda n: (0, 0))
    return pl.pallas_call(
        body, name=name, grid=(NBLK,),
        in_specs=[pl.BlockSpec((BLK, 2 * GW), lambda n: (n, 0)), vec, vec,
                  pl.BlockSpec((8, BLK, BLK), lambda n: (0, 0, 0)), pl.BlockSpec((BLK, 8), lambda n: (0, 0))],
        out_specs=[pl.BlockSpec((BLK, GW), lambda n: (n, 0)), pl.BlockSpec((GW, BLK), lambda n: (0, n))],
        out_shape=[jax.ShapeDtypeStruct((S, GW), BF16), jax.ShapeDtypeStruct((GW, S), BF16)],
        compiler_params=_cp(("parallel",)),
    )(z, ln_g, ln_b, w_s, b_s_t)


def _gmlp_bwd(z, dy, ln_g, ln_b, w_s, b_s_t, name):
    def body(z_ref, dy_ref, g_ref, b_ref, ws_ref, bs_ref, dz_ref, dws_ref, dbs_ref, dg_ref, db_ref, dvn_ref):
        n = pl.program_id(0)
        zv = z_ref[...]
        zg = _gelu(zv)
        u = zg[:, :GW]
        xh, rstd = _ln_stats(zg[:, GW:])
        vn = (xh * g_ref[...] + b_ref[...]).astype(BF16)
        tril = _tril_mask()

        @pl.when(n == 0)
        def _():
            dws_ref[...] = jnp.zeros_like(dws_ref)
            dbs_ref[...] = jnp.zeros_like(dbs_ref)
            dg_ref[...] = jnp.zeros_like(dg_ref)
            db_ref[...] = jnp.zeros_like(db_ref)

        wt = jnp.where(tril, ws_ref[...], 0.0).astype(BF16)
        vn3 = _groups(vn)
        dyv = dy_ref[...]
        mixed = _ungroup(_bdot(wt, vn3) + _group_bias(bs_ref))
        dz_ref[:, :GW] = (dyv * mixed * _gelu_grad(zv[:, :GW])).astype(BF16)
        dmix3 = _groups(dyv * u)
        dmb = dmix3.astype(BF16)
        dws_ref[...] += jnp.where(tril, _bdot_nt(dmb, vn3), 0.0)
        dbsum = jnp.sum(dmix3, axis=-1, keepdims=True)
        for gg in range(8):
            dbs_ref[:, gg:gg + 1] += dbsum[gg]
        dvn_ref[...] = _ungroup(_bdot_tn(wt, dmb))

        dvn = dvn_ref[...]
        dg_ref[...] += jnp.sum(dvn * xh, axis=0, keepdims=True)
        db_ref[...] += jnp.sum(dvn, axis=0, keepdims=True)
        dvg = _ln_dx(dvn * g_ref[...], xh, rstd)
        dz_ref[:, GW:] = (dvg * _gelu_grad(zv[:, GW:])).astype(BF16)

    vec = pl.BlockSpec((1, GW), lambda n: (0, 0))
    ws = pl.BlockSpec((8, BLK, BLK), lambda n: (0, 0, 0))
    bs = pl.BlockSpec((BLK, 8), lambda n: (0, 0))
    return pl.pallas_call(
        body, name=name, grid=(NBLK,),
        in_specs=[pl.BlockSpec((BLK, 2 * GW), lambda n: (n, 0)), pl.BlockSpec((BLK, GW), lambda n: (n, 0)),
                  vec, vec, ws, bs],
        out_specs=[pl.BlockSpec((BLK, 2 * GW), lambda n: (n, 0)), ws, bs, vec, vec],
        out_shape=[jax.ShapeDtypeStruct((S, 2 * GW), BF16), jax.ShapeDtypeStruct((8, BLK, BLK), F32),
                   jax.ShapeDtypeStruct((BLK, 8), F32), jax.ShapeDtypeStruct((1, GW), F32),
                   jax.ShapeDtypeStruct((1, GW), F32)],
        scratch_shapes=[pltpu.VMEM((BLK, GW), F32)],
        compiler_params=_cp(("arbitrary",)),
    )(z, dy, ln_g, ln_b, w_s, b_s_t)


def _merge_fwd(a, b, gl, b_gates, name):
    tm = 512

    def body(a_ref, b_ref, g0_ref, g1_ref, bg_ref, o_ref, ot_ref):
        g0 = jax.nn.sigmoid(g0_ref[...] + bg_ref[:, :D])
        g1 = jax.nn.sigmoid(g1_ref[...] + bg_ref[:, D:])
        mg = g0 * a_ref[...] + g1 * b_ref[...]
        o_ref[...] = mg.astype(BF16)
        ot_ref[...] = mg.T.astype(BF16)

    row = pl.BlockSpec((tm, D), lambda i: (i, 0))
    return pl.pallas_call(
        body, name=name, grid=(S // tm,),
        in_specs=[row, row, row, pl.BlockSpec((tm, D), lambda i: (i, 1)), pl.BlockSpec((1, 2 * D), lambda i: (0, 0))],
        out_specs=[row, pl.BlockSpec((D, tm), lambda i: (0, i))],
        out_shape=[jax.ShapeDtypeStruct((S, D), BF16), jax.ShapeDtypeStruct((D, S), BF16)],
        compiler_params=_cp(("parallel",)),
    )(a, b, gl, gl, b_gates)


def _merge_bwd(dm, a, b, gl, b_gates, name):
    tm = 512

    def body(dm_ref, a_ref, b_ref, g0_ref, g1_ref, bg_ref, da_ref, db_ref, dgl_ref, dbg_ref):
        i = pl.program_id(0)
        dmv = dm_ref[...]
        g0 = jax.nn.sigmoid(g0_ref[...] + bg_ref[:, :D])
        g1 = jax.nn.sigmoid(g1_ref[...] + bg_ref[:, D:])
        da_ref[...] = (dmv * g0).astype(BF16)
        db_ref[...] = (dmv * g1).astype(BF16)
        d0 = dmv * a_ref[...] * g0 * (1.0 - g0)
        d1 = dmv * b_ref[...] * g1 * (1.0 - g1)
        dgl_ref[:, :D] = d0.astype(BF16)
        dgl_ref[:, D:] = d1.astype(BF16)
        s0 = jnp.sum(d0, axis=0, keepdims=True)
        s1 = jnp.sum(d1, axis=0, keepdims=True)

        @pl.when(i == 0)
        def _():
            dbg_ref[:, :D] = s0
            dbg_ref[:, D:] = s1

        @pl.when(i > 0)
        def _():
            dbg_ref[:, :D] += s0
            dbg_ref[:, D:] += s1

    row = pl.BlockSpec((tm, D), lambda i: (i, 0))
    wide = pl.BlockSpec((tm, 2 * D), lambda i: (i, 0))
    bg = pl.BlockSpec((1, 2 * D), lambda i: (0, 0))
    return pl.pallas_call(
        body, name=name, grid=(S // tm,),
        in_specs=[row, row, row, row, pl.BlockSpec((tm, D), lambda i: (i, 1)), bg],
        out_specs=[row, row, wide, bg],
        out_shape=[jax.ShapeDtypeStruct((S, D), BF16), jax.ShapeDtypeStruct((S, D), BF16),
                   jax.ShapeDtypeStruct((S, 2 * D), BF16), jax.ShapeDtypeStruct((1, 2 * D), F32)],
        compiler_params=_cp(("arbitrary",)),
    )(dm, a, b, gl, gl, b_gates)


def _adam_math(w, g, m, v):
    m2 = ADAM_B1 * m + (1.0 - ADAM_B1) * g
    v2 = ADAM_B2 * v + (1.0 - ADAM_B2) * (g * g)
    m_hat = m2 / (1.0 - ADAM_B1 ** ADAM_STEP)
    v_hat = v2 / (1.0 - ADAM_B2 ** ADAM_STEP)
    delta = -ADAM_LR * (m_hat / (jnp.sqrt(v_hat) + ADAM_EPS) + ADAM_WD * w)
    return delta, m2, v2


def _pick_rows(rows, cols, unit=16, budget=MIB):
    best = unit
    for t in range(unit, rows + 1, unit):
        if rows % t == 0 and t * cols * 4 <= budget:
            best = t
    assert rows % best == 0
    return best


def _adamw(w, g, m, v, name):
    r, c = w.shape
    tr = _pick_rows(r, c, unit=8)

    def body(w_ref, g_ref, m_ref, v_ref, go_ref, d_ref, mo_ref, vo_ref):
        gv = g_ref[...]
        delta, m2, v2 = _adam_math(w_ref[...], gv, m_ref[...], v_ref[...])
        go_ref[...] = gv
        d_ref[...] = delta
        mo_ref[...] = m2
        vo_ref[...] = v2

    blk = pl.BlockSpec((tr, c), lambda i: (i, 0))
    shp = jax.ShapeDtypeStruct((r, c), F32)
    return pl.pallas_call(
        body, name=name, grid=(r // tr,), in_specs=[blk] * 4, out_specs=[blk] * 4, out_shape=[shp] * 4,
        compiler_params=_cp(("parallel",)),
    )(*[pltpu.with_memory_space_constraint(t, pltpu.HBM) for t in (w, g, m, v)])


def _small_sum_adamw(parts, own, pos, w, m, v, name):
    tr = 48

    def body(pos_ref, p_ref, own_ref, w_ref, m_ref, v_ref, g_ref, d_ref, mo_ref, vo_ref):
        me = 2 * pos_ref[1] + pos_ref[0]
        gv = None
        for k in range(8):
            term = jnp.where(me == k, own_ref[...], p_ref[k])
            gv = term if gv is None else gv + term
        delta, m2, v2 = _adam_math(w_ref[...], gv, m_ref[...], v_ref[...])
        g_ref[...] = gv
        d_ref[...] = delta
        mo_ref[...] = m2
        vo_ref[...] = v2

    blk = pl.BlockSpec((tr, D), lambda i, p: (i, 0))
    shp = jax.ShapeDtypeStruct((SMALL_ROWS, D), F32)
    return pl.pallas_call(
        body, name=name,
        grid_spec=pltpu.PrefetchScalarGridSpec(
            num_scalar_prefetch=1, grid=(SMALL_ROWS // tr,),
            in_specs=[pl.BlockSpec((8, tr, D), lambda i, p: (0, i, 0)), blk, blk, blk, blk],
            out_specs=[blk] * 4),
        out_shape=[shp] * 4,
        compiler_params=_cp(("parallel",)),
    )(pos, parts, own, w, m, v)


ANY = pl.BlockSpec(memory_space=pl.ANY)


def _in_hbm(arrays):
    return [pltpu.with_memory_space_constraint(a, pltpu.HBM) for a in arrays]


def _mesh_pos():
    x, y, c = lax.axis_index("x"), lax.axis_index("y"), lax.axis_index("c")
    chips = [(1 - x, y), (x, 1 - y), (1 - x, 1 - y)]
    return x, y, c, chips


def _place_shard(w, kind, pos, name):
    r, c = w.shape
    tr = _pick_rows(r, c)

    def body(pos_ref, w_ref, o_ref):
        o_ref[...] = w_ref[...].astype(BF16)

    if kind == "stack":
        o_spec = pl.BlockSpec((None, tr, c), lambda i, p: (p[1], i, 0))
        shape = (NSH, r, c)
    else:
        o_spec = pl.BlockSpec((tr, c), lambda i, p: (i, p[1]))
        shape = (r, NSH * c)
    return pl.pallas_call(
        body, name=name,
        grid_spec=pltpu.PrefetchScalarGridSpec(
            num_scalar_prefetch=1, grid=(r // tr,),
            in_specs=[pl.BlockSpec((tr, c), lambda i, p: (i, 0))], out_specs=o_spec),
        out_shape=pltpu.HBM(shape, BF16),
        compiler_params=_cp(("parallel",)),
    )(pos, pltpu.with_memory_space_constraint(w, pltpu.HBM))


SEM = pl.BlockSpec(memory_space=pltpu.SEMAPHORE)
SPLIT_COPY = pltpu.CompilerParams(has_side_effects=pltpu.SideEffectType.DATAFLOW_SIDE_EFFECTING)


def _shard_window(ref, kind, j, h, dims):
    r, c = dims
    rows = pl.ds(pl.multiple_of(h * (r // 2), 16), r // 2)
    if kind == "stack":
        return ref.at[j, rows, :]
    return ref.at[rows, pl.ds(pl.multiple_of(j * c, 128), c)]


def _ici_copy(ref, kind, dims, j, c, sems, idx, to):
    win = _shard_window(ref, kind, j, c, dims)
    return pltpu.make_async_remote_copy(src_ref=win, dst_ref=win, send_sem=sems[0].at[idx], recv_sem=sems[1].at[idx],
                                        device_id=to, device_id_type=MESH_T)


def _gather_start(fulls, kinds, dims, after, name):
    n, na = len(fulls), len(after)

    def body(*refs):
        outs = refs[n + na:2 * n + na]
        send_sems, recv_sems, token = refs[2 * n + na:]
        x, y, c, chips = _mesh_pos()
        for a in range(n):
            for k, chip in enumerate(chips):
                _ici_copy(outs[a], kinds[a], dims[a], 2 * x + y, c, (send_sems, recv_sems), 3 * a + k,
                          (chip[0], chip[1], c)).start()
        token[...] = jnp.zeros_like(token)

    res = pl.pallas_call(
        body, name=name, in_specs=[ANY] * (n + na),
        out_specs=[ANY] * n + [SEM, SEM, pl.BlockSpec(memory_space=pltpu.VMEM)],
        out_shape=[pltpu.HBM(f.shape, BF16) for f in fulls]
        + [pltpu.SemaphoreType.DMA((3 * n,)), pltpu.SemaphoreType.DMA((3 * n,)), jax.ShapeDtypeStruct((8, 128), F32)],
        input_output_aliases={i: i for i in range(n)},
        compiler_params=SPLIT_COPY,
    )(*_in_hbm(fulls), *after)
    return res[:n], res[n], res[n + 1], res[n + 2]


def _gather_wait(fulls, send_sems, recv_sems, kinds, dims, after, name):
    n, na = len(fulls), len(after)

    def body(*refs):
        ssem, rsem = refs[n], refs[n + 1]
        outs = refs[n + 2 + na:]
        x, y, c, chips = _mesh_pos()
        for a in range(n):
            for k, chip in enumerate(chips):
                to = (chip[0], chip[1], c)
                _ici_copy(outs[a], kinds[a], dims[a], 2 * x + y, c, (ssem, rsem), 3 * a + k, to).wait_send()
                _ici_copy(outs[a], kinds[a], dims[a], 2 * chip[0] + chip[1], c, (ssem, rsem), 3 * a + k, to).wait_recv()

    return pl.pallas_call(
        body, name=name, in_specs=[ANY] * n + [SEM, SEM] + [ANY] * na, out_specs=[ANY] * n,
        out_shape=[pltpu.HBM(f.shape, BF16) for f in fulls],
        input_output_aliases={i: i for i in range(n)},
        compiler_params=SPLIT_COPY,
    )(*_in_hbm(fulls), send_sems, recv_sems, *after)


def _gather_forward(fulls, kinds, dims, name):
    n = len(fulls)

    def body(*refs):
        outs = refs[n:2 * n]
        sems = refs[2 * n:]
        x, y, c, chips = _mesh_pos()
        sib = (x, y, 1 - c)
        cps = []
        for a in range(n):
            for k, chip in enumerate(chips):
                cp = _ici_copy(outs[a], kinds[a], dims[a], 2 * chip[0] + chip[1], c, sems, 3 * a + k, sib)
                cp.start()
                cps.append(cp)
        for a in range(n):
            for k, chip in enumerate(chips):
                _ici_copy(outs[a], kinds[a], dims[a], 2 * chip[0] + chip[1], 1 - c, sems, 3 * a + k, sib).wait_recv()
        for cp in cps:
            cp.wait_send()

    return pl.pallas_call(
        body, name=name, in_specs=[ANY] * n, out_specs=[ANY] * n,
        out_shape=[pltpu.HBM(f.shape, BF16) for f in fulls],
        input_output_aliases={i: i for i in range(n)},
        scratch_shapes=[pltpu.SemaphoreType.DMA((3 * n,)), pltpu.SemaphoreType.DMA((3 * n,))],
    )(*_in_hbm(fulls))


def _pair_copy(src, land, a, x, y, c, sems):
    return pltpu.make_async_remote_copy(
        src_ref=src.at[1 - c], dst_ref=land, send_sem=sems[0].at[a], recv_sem=sems[1].at[a],
        device_id=(x, y, 1 - c), device_id_type=MESH_T)


def _pair_start(grads, lands, name):
    n = len(grads)

    def body(*refs):
        srcs, dsts = refs[2 * n:3 * n], refs[3 * n:4 * n]
        send_sems, recv_sems, token = refs[4 * n:]
        x, y, c, _ = _mesh_pos()
        for a in range(n):
            _pair_copy(srcs[a], dsts[a], a, x, y, c, (send_sems, recv_sems)).start()
        token[...] = jnp.zeros_like(token)

    res = pl.pallas_call(
        body, name=name, in_specs=[ANY] * (2 * n),
        out_specs=[ANY] * (2 * n) + [SEM, SEM, pl.BlockSpec(memory_space=pltpu.VMEM)],
        out_shape=[pltpu.HBM(g.shape, F32) for g in grads]
        + [pltpu.HBM(l.shape, F32) for l in lands]
        + [pltpu.SemaphoreType.DMA((n,)), pltpu.SemaphoreType.DMA((n,)), jax.ShapeDtypeStruct((8, 128), F32)],
        input_output_aliases={i: i for i in range(2 * n)},
        compiler_params=SPLIT_COPY,
    )(*_in_hbm(grads), *_in_hbm(lands))
    return res[:n], res[n:2 * n], res[2 * n], res[2 * n + 1], res[2 * n + 2]


def _pair_wait(grads, lands, send_sems, recv_sems, after, name):
    n, na = len(grads), len(after)

    def body(*refs):
        ssem, rsem = refs[2 * n], refs[2 * n + 1]
        outs = refs[2 * n + 2 + na:]
        x, y, c, _ = _mesh_pos()
        for a in range(n):
            cp = _pair_copy(outs[a], outs[n + a], a, x, y, c, (ssem, rsem))
            cp.wait_send()
            cp.wait_recv()

    res = pl.pallas_call(
        body, name=name, in_specs=[ANY] * (2 * n) + [SEM, SEM] + [ANY] * na, out_specs=[ANY] * (2 * n),
        out_shape=[pltpu.HBM(g.shape, F32) for g in grads]
        + [pltpu.HBM(l.shape, F32) for l in lands],
        input_output_aliases={i: i for i in range(2 * n)},
        compiler_params=SPLIT_COPY,
    )(*_in_hbm(grads), *_in_hbm(lands), send_sems, recv_sems, *after)
    return res[:n], res[n:]


def _pair_sum(g, recv, pos, name):
    _, _, rh, c = g.shape
    tr = _pick_rows(rh, c)

    def body(pos_ref, g_ref, r_ref, o_ref):
        o_ref[...] = (g_ref[...] + r_ref[...]).astype(BF16)

    return pl.pallas_call(
        body, name=name,
        grid_spec=pltpu.PrefetchScalarGridSpec(
            num_scalar_prefetch=1, grid=(3, rh // tr),
            in_specs=[pl.BlockSpec((None, None, tr, c), lambda k, r, p: (p[0], p[2 + k], r, 0)),
                      pl.BlockSpec((None, tr, c), lambda k, r, p: (p[2 + k], r, 0))],
            out_specs=pl.BlockSpec((None, tr, c), lambda k, r, p: (k, r, 0))),
        out_shape=pltpu.HBM((3, rh, c), BF16),
        compiler_params=_cp(("parallel", "parallel")),
    )(pos, *_in_hbm([g, recv]))


def _chip_copy(src, land, a, k, chip, c, sems):
    return pltpu.make_async_remote_copy(
        src_ref=src.at[k], dst_ref=land.at[k], send_sem=sems[0].at[3 * a + k],
        recv_sem=sems[1].at[3 * a + k], device_id=(chip[0], chip[1], c), device_id_type=MESH_T)


def _chip_start(psums, lands, name):
    n = len(psums)

    def body(*refs):
        srcs, dsts = refs[2 * n:3 * n], refs[3 * n:4 * n]
        send_sems, recv_sems, token = refs[4 * n:]
        x, y, c, chips = _mesh_pos()
        for a in range(n):
            for k, chip in enumerate(chips):
                _chip_copy(srcs[a], dsts[a], a, k, chip, c, (send_sems, recv_sems)).start()
        token[...] = jnp.zeros_like(token)

    res = pl.pallas_call(
        body, name=name, in_specs=[ANY] * (2 * n),
        out_specs=[ANY] * (2 * n) + [SEM, SEM, pl.BlockSpec(memory_space=pltpu.VMEM)],
        out_shape=[pltpu.HBM(p.shape, BF16) for p in psums]
        + [pltpu.HBM(l.shape, BF16) for l in lands]
        + [pltpu.SemaphoreType.DMA((3 * n,)), pltpu.SemaphoreType.DMA((3 * n,)), jax.ShapeDtypeStruct((8, 128), F32)],
        input_output_aliases={i: i for i in range(2 * n)},
        compiler_params=SPLIT_COPY,
    )(*_in_hbm(psums), *_in_hbm(lands))
    return res[:n], res[n:2 * n], res[2 * n], res[2 * n + 1], res[2 * n + 2]


def _chip_wait(psums, lands, send_sems, recv_sems, after, name):
    n, na = len(psums), len(after)

    def body(*refs):
        ssem, rsem = refs[2 * n], refs[2 * n + 1]
        outs = refs[2 * n + 2 + na:]
        srcs, dsts = outs[:n], outs[n:]
        x, y, c, chips = _mesh_pos()
        for a in range(n):
            for k, chip in enumerate(chips):
                cp = _chip_copy(srcs[a], dsts[a], a, k, chip, c, (ssem, rsem))
                cp.wait_send()
                cp.wait_recv()

    res = pl.pallas_call(
        body, name=name, in_specs=[ANY] * (2 * n) + [SEM, SEM] + [ANY] * na, out_specs=[ANY] * (2 * n),
        out_shape=[pltpu.HBM(p.shape, BF16) for p in psums]
        + [pltpu.HBM(l.shape, BF16) for l in lands],
        input_output_aliases={i: i for i in range(2 * n)},
        compiler_params=SPLIT_COPY,
    )(*_in_hbm(psums), *_in_hbm(lands), send_sems, recv_sems, *after)
    return res[n:]


def _owner_sum(g, recv_a, recv_b, pos, name):
    _, _, rh, c = g.shape
    tr = _pick_rows(rh, c)

    def body(pos_ref, g_ref, ra_ref, rb_ref, o_ref):
        acc = g_ref[...] + ra_ref[...]
        for k in range(3):
            acc = acc + rb_ref[k].astype(F32)
        o_ref[...] = acc

    return pl.pallas_call(
        body, name=name,
        grid_spec=pltpu.PrefetchScalarGridSpec(
            num_scalar_prefetch=1, grid=(rh // tr,),
            in_specs=[pl.BlockSpec((None, None, tr, c), lambda r, p: (p[0], p[1], r, 0)),
                      pl.BlockSpec((None, tr, c), lambda r, p: (p[1], r, 0)),
                      pl.BlockSpec((3, tr, c), lambda r, p: (0, r, 0))],
            out_specs=pl.BlockSpec((None, tr, c), lambda r, p: (p[0], r, 0))),
        out_shape=pltpu.HBM((2, rh, c), F32),
        compiler_params=_cp(("parallel",)),
    )(pos, *_in_hbm([g, recv_a, recv_b]))


def _sibling_allgather(halves, name):
    n = len(halves)

    def body(*refs):
        outs = refs[n:2 * n]
        send_sems, recv_sems = refs[2 * n:]
        x, y, c, _ = _mesh_pos()
        cps = []
        for a in range(n):
            cp = pltpu.make_async_remote_copy(
                src_ref=outs[a].at[c], dst_ref=outs[a].at[c], send_sem=send_sems.at[a], recv_sem=recv_sems.at[a],
                device_id=(x, y, 1 - c), device_id_type=MESH_T)
            cp.start()
            cps.append(cp)
        for a in range(n):
            cps[a].wait_send()
            pltpu.make_async_remote_copy(
                src_ref=outs[a].at[1 - c], dst_ref=outs[a].at[1 - c], send_sem=send_sems.at[a],
                recv_sem=recv_sems.at[a], device_id=(x, y, 1 - c), device_id_type=MESH_T).wait_recv()

    return pl.pallas_call(
        body, name=name, in_specs=[ANY] * n, out_specs=[ANY] * n,
        out_shape=[pltpu.HBM(h.shape, F32) for h in halves],
        input_output_aliases={i: i for i in range(n)},
        scratch_shapes=[pltpu.SemaphoreType.DMA((n,)), pltpu.SemaphoreType.DMA((n,))],
    )(*_in_hbm(halves))


def _peers(x, y, c):
    rel = [(0, 0, 1), (0, 1, 0), (0, 1, 1), (1, 0, 0), (1, 0, 1), (1, 1, 0), (1, 1, 1)]
    return [((1 - x) if dx else x, (1 - y) if dy else y, (1 - c) if dc else c) for dx, dy, dc in rel]


def _small_copy(src, land, k, peer, slot, sems):
    return pltpu.make_async_remote_copy(src_ref=src, dst_ref=land.at[slot], send_sem=sems[0].at[k],
                                        recv_sem=sems[1].at[k], device_id=peer, device_id_type=MESH_T)


def _small_start(part, land, name):
    def body(p_in, l_in, p_ref, l_ref, send_sems, recv_sems, token):
        x, y, c, _ = _mesh_pos()
        for k, peer in enumerate(_peers(x, y, c)):
            _small_copy(p_ref, l_ref, k, peer, 4 * x + 2 * y + c, (send_sems, recv_sems)).start()
        token[...] = jnp.zeros_like(token)

    return pl.pallas_call(
        body, name=name, in_specs=[ANY, ANY],
        out_specs=[ANY, ANY, SEM, SEM, pl.BlockSpec(memory_space=pltpu.VMEM)],
        out_shape=[pltpu.HBM(part.shape, F32), pltpu.HBM(land.shape, F32), pltpu.SemaphoreType.DMA((7,)),
                   pltpu.SemaphoreType.DMA((7,)), jax.ShapeDtypeStruct((8, 128), F32)],
        input_output_aliases={0: 0, 1: 1},
        compiler_params=SPLIT_COPY,
    )(*_in_hbm([part, land]))


def _small_wait(part, land, send_sems, recv_sems, after, name):
    na = len(after)

    def body(*refs):
        ssem, rsem = refs[2], refs[3]
        p_ref, l_ref = refs[4 + na:]
        x, y, c, _ = _mesh_pos()
        for k, peer in enumerate(_peers(x, y, c)):
            cp = _small_copy(p_ref, l_ref, k, peer, 4 * peer[0] + 2 * peer[1] + peer[2], (ssem, rsem))
            cp.wait_send()
            cp.wait_recv()

    return pl.pallas_call(
        body, name=name, in_specs=[ANY, ANY, SEM, SEM] + [ANY] * na, out_specs=[ANY, ANY],
        out_shape=[pltpu.HBM(part.shape, F32), pltpu.HBM(land.shape, F32)],
        input_output_aliases={0: 0, 1: 1},
        compiler_params=SPLIT_COPY,
    )(*_in_hbm([part, land]), send_sems, recv_sems, *after)


def _pack_small(ln1_g, ln1_b, gln_g, gln_b, ln2_g, ln2_b, ln3_g, ln3_b, b_gates, b_s, w_s):
    rows = [ln1_g, ln1_b, gln_g, gln_b, ln2_g, ln2_b, ln3_g, ln3_b]
    rows = [r.reshape(1, D) for r in rows] + [b_gates.reshape(2, D), b_s.reshape(1, D), jnp.zeros((5, D), F32),
                                             w_s.reshape(128, D)]
    return jnp.concatenate(rows, axis=0)


def _unpack_small(p):
    out = [p[i:i + 1] for i in range(8)]
    return out + [p[8:10].reshape(1, 2 * D), p[10:11].reshape(1, 8, BLK), p[16:144].reshape(1, 8, BLK, BLK)]


GROUPS = (("f1g", "f1u", "f1d"), ("w_in",), ("w_ab", "w_gb", "w_out"), ("f2g", "f2u", "f2d"))


def _local_step(x, pos_f, target, P, weights_of, grads_ready, flush, small_ready):
    invf = ROPE_THETA ** (-jnp.arange(0, DH, 2, dtype=F32) / DH)
    invf = jnp.tile(invf, 4).reshape(1, 128)
    b_s_t = P["gmlp_b_s"].T

    W = dict(weights_of(0, []))
    h1, h1b, xh1, rstd1, a1, b1, h1t = _ffn_fwd(x, W["f1g"], W["f1u"], W["f1d"], P["ln1_g"], P["ln1_b"], "ffn1_fwd",
                                                emit_t=True)
    W.update(weights_of(1, [h1b]))
    qkv_c = _proj_qkv_rope(h1b, W["w_in"], pos_f, invf, "proj_qkv_rope")
    z = _matmul(h1b, W["w_in"], "nn", "proj_z", n=2 * GW, b_col0=3 * ATT_W, tm=S, tn=512)
    gl = _matmul(h1b, W["w_in"], "nn", "proj_gates", n=2 * D, b_col0=3 * ATT_W + 2 * GW, tm=S, tn=512)
    og = [_attn_fwd(gi, qkv_c[gi], "attn_fwd_g%d" % gi) for gi in range(NG)]
    y_attn, y_attn_t, lse = _attn_combine([o for o, _ in og], [l for _, l in og], "attn_combine")
    y_gmlp, y_gmlp_t = _gmlp_fwd(z, P["gmlp_ln_g"], P["gmlp_ln_b"], P["gmlp_w_s"], b_s_t, "gmlp_fwd")
    W.update(weights_of(2, [y_gmlp]))
    br_a = _matmul(y_attn, W["w_ab"], "nn", "branch_attn", n=D, tm=1024, tn=D)
    br_b = _matmul(y_gmlp, W["w_gb"], "nn", "branch_gmlp", n=D, tm=1024, tn=D)
    merged, merged_t = _merge_fwd(br_a, br_b, gl, P["b_gates"], "merge_fwd")
    mix = _matmul(merged, W["w_out"], "nn", "mix_out", n=D, tm=1024, tn=D)
    h2, h2b, xh2, rstd2 = _resid_ln(h1, mix, P["ln2_g"], P["ln2_b"], "resid_ln2")
    W.update(weights_of(3, [h2b]))
    y, _, xh3, rstd3, a2, b2 = _ffn_fwd(h2, W["f2g"], W["f2u"], W["f2d"], P["ln3_g"], P["ln3_b"], "ffn2_fwd")

    dr3, dg3, db3, loss = _ln_bwd(y, xh3, rstd3, P["ln3_g"], "loss_ln3_bwd", target=target)
    g_f2g, g_f2u, g_f2d, dh2 = _ffn_bwd(dr3, h2b, a2, b2, W["f2g"], W["f2u"], W["f2d"], "ffn2_bwd")
    tok = grads_ready(3, dict(f2g=g_f2g, f2u=g_f2u, f2d=g_f2d))
    dr2, dg2, db2 = _ln_bwd(dh2, xh2, rstd2, P["ln2_g"], "ln2_bwd", after=tok)
    g_wout = _wgrad(merged_t, dr2, 128, D, "dw_out", row_sharded=True)
    dmerged = _matmul(dr2, W["w_out"], "nt", "dmerged", n=D, tm=1024, tn=D)
    dab, dbb, dglb, dbg = _merge_bwd(dmerged, br_a, br_b, gl, P["b_gates"], "merge_bwd")
    tok = flush([dab])
    g_wab = _wgrad(y_attn_t, dab, GRP_W // 2, 256, "dw_attn_branch", row_sharded=False, after=tok)
    g_wgb = _wgrad(y_gmlp_t, dbb, 128, D, "dw_gmlp_branch", row_sharded=True)
    tok = grads_ready(2, dict(w_ab=g_wab, w_gb=g_wgb, w_out=g_wout))
    dy_attn = _matmul(dab, W["w_ab"], "nt", "dy_attn", n=GRP_W, tm=1024, tn=GRP_W, after=tok)
    dy_gmlp = _matmul(dbb, W["w_gb"], "nt", "dy_gmlp", n=GW, tm=1024, tn=GW)
    dzb, dws, dbs_t, dgln_g, dgln_b = _gmlp_bwd(z, dy_gmlp, P["gmlp_ln_g"], P["gmlp_ln_b"], P["gmlp_w_s"], b_s_t,
                                                 "gmlp_bwd")
    cls = _class_order([dy_attn, y_attn, lse], "attn_class_order")
    dqkv_c = []
    for gi in range(NG):
        dy_c, y_c, lse_c = [t[None] if gi == 0 else cls[2 * a + gi - 1] for a, t in enumerate((dy_attn, y_attn, lse))]
        dqkv_c.append(_attn_bwd(gi, qkv_c[gi], dy_c, y_c, lse_c, "attn_bwd_g%d" % gi))
    dqkvb = _rope_bwd(dqkv_c, pos_f, invf, "rope_bwd")
    dproj = jnp.concatenate([dqkvb, dzb, dglb], axis=1)
    tok = flush([dproj])
    g_win = _wgrad(h1t, dproj, D // 2, IN_SH, "dw_in", row_sharded=False, after=tok)
    tok = grads_ready(1, dict(w_in=g_win))
    dh1 = _matmul(dproj, W["w_in"], "nt", "dh1", n=D, tm=1024, tn=D, tk=IN_SH, add=dr2, add_scale=ALPHA, after=tok)
    dr1, dg1, db1 = _ln_bwd(dh1, xh1, rstd1, P["ln1_g"], "ln1_bwd")
    tok = flush([dr1])
    tok = tok + small_ready(_pack_small(dg1, db1, dgln_g, dgln_b, dg2, db2, dg3, db3, dbg, dbs_t.T, dws))
    g_f1g, g_f1u, g_f1d, dx = _ffn_bwd(dr1, x.astype(BF16), a1, b1, W["f1g"], W["f1u"], W["f1d"], "ffn1_bwd",
                                       after=tok)
    grads_ready(0, dict(f1g=g_f1g, f1u=g_f1u, f1d=g_f1d))
    flush([dx])
    return loss, dx


BIG = ("f1g", "f1u", "f1d", "w_in", "w_ab", "w_gb", "w_out", "f2g", "f2u", "f2d")
TRANSPOSED = ("f1g", "f1u", "f2g", "f2u")
KIND = dict(f1g="stack", f1u="stack", f1d="stack", w_in="col", w_ab="col", w_gb="stack", w_out="stack",
            f2g="stack", f2u="stack", f2d="stack")


def kernel(x, positions, ffn1_w_gate, ffn1_w_up, ffn1_w_down, ln1_g, ln1_b, w_in, b_gates, gmlp_ln_g, gmlp_ln_b, gmlp_w_s, gmlp_b_s, w_attn_branch, w_gmlp_branch, w_out, ln2_g, ln2_b, ffn2_w_gate, ffn2_w_up, ffn2_w_down, ln3_g, ln3_b, loss_target, m_ffn1_w_gate, m_ffn1_w_up, m_ffn1_w_down, m_ln1_g, m_ln1_b, m_w_in, m_b_gates, m_gmlp_ln_g, m_gmlp_ln_b, m_gmlp_w_s, m_gmlp_b_s, m_w_attn_branch, m_w_gmlp_branch, m_w_out, m_ln2_g, m_ln2_b, m_ffn2_w_gate, m_ffn2_w_up, m_ffn2_w_down, m_ln3_g, m_ln3_b, v_ffn1_w_gate, v_ffn1_w_up, v_ffn1_w_down, v_ln1_g, v_ln1_b, v_w_in, v_b_gates, v_gmlp_ln_g, v_gmlp_ln_b, v_gmlp_w_s, v_gmlp_b_s, v_w_attn_branch, v_w_gmlp_branch, v_w_out, v_ln2_g, v_ln2_b, v_ffn2_w_gate, v_ffn2_w_up, v_ffn2_w_down, v_ln3_g, v_ln3_b):
    cx, cy, cc = lax.axis_index("x"), lax.axis_index("y"), lax.axis_index("c")
    pos = jnp.stack([cc, 2 * cx + cy, 2 * (1 - cx) + cy, 2 * cx + 1 - cy, 2 * (1 - cx) + 1 - cy]).astype(jnp.int32)

    w_sh = dict(f1g=ffn1_w_gate, f1u=ffn1_w_up, f1d=ffn1_w_down, w_in=w_in, w_ab=w_attn_branch,
                w_gb=w_gmlp_branch, w_out=w_out, f2g=ffn2_w_gate, f2u=ffn2_w_up, f2d=ffn2_w_down)
    m_sh = dict(f1g=m_ffn1_w_gate, f1u=m_ffn1_w_up, f1d=m_ffn1_w_down, w_in=m_w_in, w_ab=m_w_attn_branch,
                w_gb=m_w_gmlp_branch, w_out=m_w_out, f2g=m_ffn2_w_gate, f2u=m_ffn2_w_up, f2d=m_ffn2_w_down)
    v_sh = dict(f1g=v_ffn1_w_gate, f1u=v_ffn1_w_up, f1d=v_ffn1_w_down, w_in=v_w_in, w_ab=v_w_attn_branch,
                w_gb=v_w_gmlp_branch, w_out=v_w_out, f2g=v_ffn2_w_gate, f2u=v_ffn2_w_up, f2d=v_ffn2_w_down)
    w_sh = {k: (v[0].T if k in TRANSPOSED else v[0]) for k, v in w_sh.items()}
    m_sh = {k: (v[0].T if k in TRANSPOSED else v[0]) for k, v in m_sh.items()}
    v_sh = {k: (v[0].T if k in TRANSPOSED else v[0]) for k, v in v_sh.items()}

    started, tokens = [], []
    for gi, names in enumerate(GROUPS):
        placed = [_place_shard(w_sh[k], KIND[k], pos, "place_" + k) for k in names]
        fulls, ssem, rsem, token = _gather_start(placed, [KIND[k] for k in names], [w_sh[k].shape for k in names],
                                                 tokens[-1:], "gather_start_g%d" % gi)
        started.append((fulls, ssem, rsem))
        tokens.append(token)

    def weights_of(gi, after):
        names = GROUPS[gi]
        kinds, dims = [KIND[k] for k in names], [w_sh[k].shape for k in names]
        fulls, ssem, rsem = started[gi]
        fulls = _gather_wait(fulls, ssem, rsem, kinds, dims, list(after) + (tokens if gi == 0 else []),
                             "gather_wait_g%d" % gi)
        fulls = _gather_forward(fulls, kinds, dims, "gather_forward_g%d" % gi)
        return {k: (f.reshape(D, D) if k in ("w_gb", "w_out") else f) for k, f in zip(names, fulls)}

    pending, inflight = [], {}

    def grads_ready(gi, gd):
        grads = [gd[k] for k in GROUPS[gi]]
        lands = [lax.empty(g.shape[1:], F32) for g in grads]
        grads, lands, ssem, rsem, token = _pair_start(grads, lands, "rs_pair_start_g%d" % gi)
        pending.append((gi, grads, lands, ssem, rsem))
        return [token]

    def flush(after):
        gi, grads, lands, ssem, rsem = pending.pop()
        names = GROUPS[gi]
        grads, recv_a = _pair_wait(grads, lands, ssem, rsem, after, "rs_pair_wait_g%d" % gi)
        psums = [_pair_sum(g, r, pos, "rs_pair_sum_" + k) for g, r, k in zip(grads, recv_a, names)]
        lands = [lax.empty((3,) + p.shape[1:], BF16) for p in psums]
        psums, lands, ssem, rsem, token = _chip_start(psums, lands, "rs_chip_start_g%d" % gi)
        inflight[gi] = (grads, recv_a, psums, lands, ssem, rsem, token)
        return [token]

    P = dict(ln1_g=ln1_g, ln1_b=ln1_b, ln2_g=ln2_g, ln2_b=ln2_b, ln3_g=ln3_g, ln3_b=ln3_b, b_gates=b_gates,
             gmlp_ln_g=gmlp_ln_g, gmlp_ln_b=gmlp_ln_b, gmlp_w_s=gmlp_w_s[0], gmlp_b_s=gmlp_b_s[0])
    pos_f = positions.reshape(S, 1).astype(F32)
    small_state = []

    def small_ready(packed):
        land = jnp.zeros((8, SMALL_ROWS, D), F32)
        packed, land, ssem, rsem, token = _small_start(packed, land, "small_start")
        small_state.append((packed, land, ssem, rsem))
        return [token]

    loss_part, dx = _local_step(x[0], pos_f, loss_target[0], P, weights_of, grads_ready, flush, small_ready)
    loss = lax.psum(loss_part[0, 0], ("x", "y", "c"))

    g_out, d_out, m_out, v_out = {}, {}, {}, {}

    def finish(gi, after):
        grads, recv_a, psums, lands, ssem, rsem, token = inflight[gi]
        recv_b = _chip_wait(psums, lands, ssem, rsem, after + [inflight[0][6]], "rs_chip_wait_g%d" % gi)
        halves = [_owner_sum(g, ra, rb, pos, "rs_owner_sum_" + k)
                  for g, ra, rb, k in zip(grads, recv_a, recv_b, GROUPS[gi])]
        reduced = _sibling_allgather(halves, "rs_sibling_allgather_g%d" % gi)
        for k, gfull in zip(GROUPS[gi], reduced):
            res = _adamw(w_sh[k], gfull.reshape(w_sh[k].shape), m_sh[k], v_sh[k], "adamw_" + k)
            after = [res[1]]
            if k in TRANSPOSED:
                res = [r.T for r in res]
            g_out[k], d_out[k], m_out[k], v_out[k] = [r[None] for r in res]
        return after

    after = []
    for gi in (3, 2, 1):
        after = finish(gi, after)

    small, parts = _small_wait(*small_state[0], after, "small_wait")
    sp = (ln1_g, ln1_b, gmlp_ln_g, gmlp_ln_b, ln2_g, ln2_b, ln3_g, ln3_b, b_gates, gmlp_b_s, gmlp_w_s)
    sm = (m_ln1_g, m_ln1_b, m_gmlp_ln_g, m_gmlp_ln_b, m_ln2_g, m_ln2_b, m_ln3_g, m_ln3_b, m_b_gates, m_gmlp_b_s,
          m_gmlp_w_s)
    sv = (v_ln1_g, v_ln1_b, v_gmlp_ln_g, v_gmlp_ln_b, v_ln2_g, v_ln2_b, v_ln3_g, v_ln3_b, v_b_gates, v_gmlp_b_s,
          v_gmlp_w_s)
    sg, sd, smn, svn = _small_sum_adamw(parts, small, pos, _pack_small(*sp), _pack_small(*sm), _pack_small(*sv),
                                        "small_adamw")
    names = ("ln1_g", "ln1_b", "gmlp_ln_g", "gmlp_ln_b", "ln2_g", "ln2_b", "ln3_g", "ln3_b", "b_gates", "gmlp_b_s",
             "gmlp_w_s")
    for dst, packed in ((g_out, sg), (d_out, sd), (m_out, smn), (v_out, svn)):
        for nm, val in zip(names, _unpack_small(packed)):
            dst[nm] = val
    finish(0, [sg])

    order = ("f1g", "f1u", "f1d", "ln1_g", "ln1_b", "w_in", "b_gates", "gmlp_ln_g", "gmlp_ln_b", "gmlp_w_s", "gmlp_b_s",
             "w_ab", "w_gb", "w_out", "ln2_g", "ln2_b", "f2g", "f2u", "f2d", "ln3_g", "ln3_b")
    outs = [loss, dx[None]]
    for dst in (g_out, d_out, m_out, v_out):
        outs += [dst[k] for k in order]
    return tuple(outs)
```

```python
import functools
import math

import jax
import jax.numpy as jnp
from jax import lax
from jax.experimental import pallas as pl
from jax.experimental.pallas import tpu as pltpu

F32 = jnp.float32
BF16 = jnp.bfloat16

S = 2048
D = 1024
NSH = 4
FSH = 704
ATT_W = 1536
GRP_W = 512
NG = 3
NH = 8
DH = 64
BLK = 128
NBLK = S // BLK
GW = 1024
IN_W = 8704
IN_SH = IN_W // NSH
ALPHA = 2.0 ** 0.25
LN_EPS = 1e-5
ROPE_THETA = 10000.0
DILATIONS = (1, 4, 16)
ADAM_LR, ADAM_B1, ADAM_B2, ADAM_EPS, ADAM_WD, ADAM_STEP = 0.001, 0.9, 0.999, 1e-08, 0.01, 10
SMALL_ROWS = 144
MESH_T = pl.DeviceIdType.MESH
MIB = 1024 * 1024
NEG_INF = float("-inf")


def _cp(sem, vmem_mib=48):
    return pltpu.CompilerParams(dimension_semantics=sem, vmem_limit_bytes=vmem_mib * MIB)


def _ln_stats(r):
    mu = jnp.mean(r, axis=-1, keepdims=True)
    xc = r - mu
    var = jnp.mean(xc * xc, axis=-1, keepdims=True)
    rstd = lax.rsqrt(var + LN_EPS)
    return xc * rstd, rstd


def _ln_dx(dxh, xh, rstd):
    m1 = jnp.mean(dxh, axis=-1, keepdims=True)
    m2 = jnp.mean(dxh * xh, axis=-1, keepdims=True)
    return rstd * (dxh - m1 - xh * m2)


def _dot_nt(a, b):
    return lax.dot_general(a, b, (((1,), (1,)), ((), ())), preferred_element_type=F32)


def _dot_tn(a, b):
    return lax.dot_general(a, b, (((0,), (0,)), ((), ())), preferred_element_type=F32)


def _dot(a, b):
    return jnp.dot(a, b, preferred_element_type=F32)


def _ffn_fwd(xin, wgt, wut, wd, ln_g, ln_b, name, emit_t=False):
    tm = 1024

    def body(x_ref, wg_ref, wu_ref, wd_ref, g_ref, b_ref, *rest):
        if emit_t:
            hb_ref, xh_ref, rstd_ref, a_ref, bb_ref, ht_ref, acc_ref = rest
        else:
            hb_ref, xh_ref, rstd_ref, a_ref, bb_ref, acc_ref = rest
        j = pl.program_id(1)
        xb = x_ref[...].astype(BF16)
        a = _dot_nt(xb, wg_ref[...])
        b = _dot_nt(xb, wu_ref[...])
        a_ref[...] = a.astype(BF16)
        bb_ref[...] = b.astype(BF16)
        s = (a * jax.nn.sigmoid(a)) * b
        f = _dot(s.astype(BF16), wd_ref[...])

        @pl.when(j == 0)
        def _():
            acc_ref[...] = f

        @pl.when(j > 0)
        def _():
            acc_ref[...] += f

        @pl.when(j == NSH - 1)
        def _():
            r = ALPHA * x_ref[...] + 0.5 * acc_ref[...]
            xh, rstd = _ln_stats(r)
            h = xh * g_ref[...] + b_ref[...]
            hb_ref[...] = h.astype(BF16)
            xh_ref[...] = xh
            rstd_ref[...] = rstd
            if emit_t:
                ht_ref[...] = h.T.astype(BF16)

    row = pl.BlockSpec((tm, D), lambda i, j: (i, 0))
    vec = pl.BlockSpec((1, D), lambda i, j: (0, 0))
    wsp = pl.BlockSpec((None, FSH, D), lambda i, j: (j, 0, 0))
    ab = pl.BlockSpec((None, tm, FSH), lambda i, j: (j, i, 0))
    out_specs = [row, row, pl.BlockSpec((tm, 1), lambda i, j: (i, 0)), ab, ab]
    out_shape = [jax.ShapeDtypeStruct((S, D), BF16),
                 jax.ShapeDtypeStruct((S, D), F32), jax.ShapeDtypeStruct((S, 1), F32),
                 jax.ShapeDtypeStruct((NSH, S, FSH), BF16), jax.ShapeDtypeStruct((NSH, S, FSH), BF16)]
    if emit_t:
        out_specs.append(pl.BlockSpec((D, tm), lambda i, j: (0, i)))
        out_shape.append(jax.ShapeDtypeStruct((D, S), BF16))
    return pl.pallas_call(
        body, name=name, grid=(S // tm, NSH),
        in_specs=[row, wsp, wsp, wsp, vec, vec], out_specs=out_specs, out_shape=out_shape,
        scratch_shapes=[pltpu.VMEM((tm, D), F32)],
        compiler_params=_cp(("parallel", "arbitrary"), vmem_mib=56),
    )(xin, wgt, wut, wd, ln_g, ln_b)


def _ffn_bwd(dr, xin_b, a, b, wgt, wut, wd, name, after=()):
    tm = 512
    ni = S // tm
    hr = FSH // 2

    def body(dr_ref, a_ref, b_ref, wg_ref, wu_ref, wd_ref, x_hbm, *rest):
        dwg_hbm, dwu_hbm, dwd_hbm, dx_hbm, dx_acc, da_all, db_all, s_all, df_all, x_all, res_buf, sems = rest[len(after):]
        j = pl.program_id(0)
        i = pl.program_id(1)
        rows = pl.ds(pl.multiple_of(i * tm, tm), tm)

        @pl.when(jnp.logical_and(j == 0, i == 0))
        def _():
            cp = pltpu.make_async_copy(x_hbm, x_all, sems.at[0])
            cp.start()
            cp.wait()

        drv = dr_ref[...]
        df = (0.5 * drv).astype(BF16)

        @pl.when(j == 0)
        def _():
            df_all[rows, :] = df

        ds = jnp.concatenate([_dot_nt(df, wd_ref[0:384, :]), _dot_nt(df, wd_ref[384:FSH, :])], axis=1)
        av = a_ref[...].astype(F32)
        bv = b_ref[...].astype(F32)
        sig = jax.nn.sigmoid(av)
        sl = av * sig
        da = (ds * bv * (sig * (1.0 + av * (1.0 - sig)))).astype(BF16)
        db = (ds * sl).astype(BF16)
        da_all[rows, :] = da
        db_all[rows, :] = db
        s_all[rows, :] = (sl * bv).astype(BF16)
        dx = _dot(da, wg_ref[...]) + _dot(db, wu_ref[...])

        @pl.when(j == 0)
        def _():
            dx_acc[rows, :] = ALPHA * drv + dx

        @pl.when(j > 0)
        def _():
            dx_acc[rows, :] += dx

        @pl.when(i == ni - 1)
        def _():
            copies = []
            for n, (lhs, rhs, out) in enumerate(((da_all, x_all, dwg_hbm), (db_all, x_all, dwu_hbm),
                                                 (s_all, df_all, dwd_hbm))):
                slot = n % 2
                if n >= 2:
                    for cp in copies[2 * (n - 2): 2 * (n - 2) + 2]:
                        cp.wait()
                res_buf[slot] = _dot_tn(lhs[...], rhs[...])
                for h in range(2):
                    cp = pltpu.make_async_copy(res_buf.at[slot, pl.ds(h * hr, hr), :], out.at[h, j],
                                               sems.at[1 + 2 * slot + h])
                    cp.start()
                    copies.append(cp)
            for cp in copies[2:]:
                cp.wait()

        @pl.when(jnp.logical_and(j == NSH - 1, i == ni - 1))
        def _():
            cp = pltpu.make_async_copy(dx_acc, dx_hbm, sems.at[0])
            cp.start()
            cp.wait()

    row = pl.BlockSpec((tm, D), lambda j, i: (i, 0))
    wsp = pl.BlockSpec((None, FSH, D), lambda j, i: (j, 0, 0))
    ab = pl.BlockSpec((None, tm, FSH), lambda j, i: (j, i, 0))
    dwshape = jax.ShapeDtypeStruct((2, NSH, hr, D), F32)
    return pl.pallas_call(
        body, name=name, grid=(NSH, ni),
        in_specs=[row, ab, ab, wsp, wsp, wsp, ANY] + [ANY] * len(after),
        out_specs=[ANY, ANY, ANY, ANY],
        out_shape=[dwshape, dwshape, dwshape, jax.ShapeDtypeStruct((S, D), F32)],
        scratch_shapes=[pltpu.VMEM((S, D), F32), pltpu.VMEM((S, FSH), BF16), pltpu.VMEM((S, FSH), BF16),
                        pltpu.VMEM((S, FSH), BF16), pltpu.VMEM((S, D), BF16), pltpu.VMEM((S, D), BF16),
                        pltpu.VMEM((2, FSH, D), F32), pltpu.SemaphoreType.DMA((5,))],
        compiler_params=_cp(("arbitrary", "arbitrary"), vmem_mib=58),
    )(dr, a, b, wgt, wut, wd, xin_b, *after)


def _matmul(a, b, mode, name, *, n, tm=512, tn=512, tk=None, b_col0=0, add=None, add_scale=1.0, out_dtype=F32,
            after=()):
    m, ka = a.shape
    tk = ka if tk is None else tk
    nk = ka // tk
    assert m % tm == 0 and n % tn == 0 and ka % tk == 0 and b_col0 % tn == 0
    off = b_col0 // tn
    na = len(after)

    def body(*refs):
        refs = refs[na:]
        if add is None:
            a_ref, b_ref, o_ref = refs[:3]
            add_ref = None
            rest = refs[3:]
        else:
            a_ref, b_ref, add_ref, o_ref = refs[:4]
            rest = refs[4:]
        k = pl.program_id(2)
        av = a_ref[...].astype(BF16)
        bv = b_ref[...].astype(BF16)
        p = _dot(av, bv) if mode == "nn" else _dot_nt(av, bv)

        def finish(acc):
            if add_ref is not None:
                acc = acc + add_scale * add_ref[...]
            o_ref[...] = acc.astype(out_dtype)

        if nk == 1:
            finish(p)
        else:
            acc_ref = rest[0]

            @pl.when(k == 0)
            def _():
                acc_ref[...] = p

            @pl.when(k > 0)
            def _():
                acc_ref[...] += p

            @pl.when(k == nk - 1)
            def _():
                finish(acc_ref[...])

    a_spec = pl.BlockSpec((tm, tk), lambda i, j, k: (i, k))
    if mode == "nn":
        b_spec = pl.BlockSpec((tk, tn), lambda i, j, k: (k, j + off))
    else:
        b_spec = pl.BlockSpec((tn, tk), lambda i, j, k: (j, k))
    o_spec = pl.BlockSpec((tm, tn), lambda i, j, k: (i, j))
    in_specs = [pl.BlockSpec(memory_space=pl.ANY)] * na + [a_spec, b_spec] + ([o_spec] if add is not None else [])
    args = tuple(after) + (a, b) + ((add,) if add is not None else ())
    return pl.pallas_call(
        body, name=name, grid=(m // tm, n // tn, nk),
        in_specs=in_specs, out_specs=o_spec,
        out_shape=jax.ShapeDtypeStruct((m, n), out_dtype),
        scratch_shapes=[pltpu.VMEM((tm, tn), F32)] if nk > 1 else [],
        compiler_params=_cp(("parallel", "parallel", "arbitrary")),
    )(*args)


def _wgrad(xt, y, rh, c, name, row_sharded, after=()):
    na = len(after)
    if row_sharded:
        def body(x_ref, y_ref, *rest):
            o_ref = rest[na]
            res = _dot(x_ref[...], y_ref[...].astype(BF16))
            for j in range(NSH):
                for h in range(2):
                    o_ref[h, j] = res[(2 * j + h) * rh:(2 * j + h + 1) * rh, :]

        grid = (1,)
        in_specs = [pl.BlockSpec((2 * NSH * rh, S), lambda g: (0, 0)), pl.BlockSpec((S, c), lambda g: (0, 0))]
        out_specs = pl.BlockSpec((2, NSH, rh, c), lambda g: (0, 0, 0, 0))
        sem = ("arbitrary",)
    else:
        def body(x_ref, y_ref, *rest):
            rest[na][...] = _dot(x_ref[...], y_ref[...].astype(BF16))

        grid = (2, NSH)
        in_specs = [pl.BlockSpec((rh, S), lambda h, j: (h, 0)), pl.BlockSpec((S, c), lambda h, j: (0, j))]
        out_specs = pl.BlockSpec((None, None, rh, c), lambda h, j: (h, j, 0, 0))
        sem = ("parallel", "parallel")
    return pl.pallas_call(
        body, name=name, grid=grid, in_specs=in_specs + [pl.BlockSpec(memory_space=pl.ANY)] * na, out_specs=out_specs,
        out_shape=jax.ShapeDtypeStruct((2, NSH, rh, c), F32),
        compiler_params=_cp(sem, vmem_mib=56),
    )(xt, y, *after)


def _resid_ln(res_xh, res_g, res_b, f, ln_g, ln_b, name):
    tm = 512

    def body(rx_ref, rg_ref, rb_ref, f_ref, g_ref, b_ref, h_ref, hb_ref, xh_ref, rstd_ref):
        r = ALPHA * (rx_ref[...] * rg_ref[...] + rb_ref[...]) + f_ref[...]
        xh, rstd = _ln_stats(r)
        h = xh * g_ref[...] + b_ref[...]
        h_ref[...] = h
        hb_ref[...] = h.astype(BF16)
        xh_ref[...] = xh
        rstd_ref[...] = rstd

    row = pl.BlockSpec((tm, D), lambda i: (i, 0))
    vec = pl.BlockSpec((1, D), lambda i: (0, 0))
    return pl.pallas_call(
        body, name=name, grid=(S // tm,),
        in_specs=[row, vec, vec, row, vec, vec],
        out_specs=[row, row, row, pl.BlockSpec((tm, 1), lambda i: (i, 0))],
        out_shape=[jax.ShapeDtypeStruct((S, D), F32), jax.ShapeDtypeStruct((S, D), BF16),
                   jax.ShapeDtypeStruct((S, D), F32), jax.ShapeDtypeStruct((S, 1), F32)],
        compiler_params=_cp(("parallel",)),
    )(res_xh, res_g, res_b, f, ln_g, ln_b)


def _ln_bwd(dout, xh, rstd, ln_g, name, target=None, after=()):
    tm = 512
    with_loss = target is not None
    na = len(after)

    def body(*refs):
        refs = refs[na:]
        if with_loss:
            bias_ref, t_ref, xh_ref, rstd_ref, g_ref, dr_ref, dg_ref, db_ref, loss_ref = refs
            err = (xh_ref[...] * g_ref[...] + bias_ref[...]) - t_ref[...]
            dy = err * (1.0 / D)
        else:
            y_ref, xh_ref, rstd_ref, g_ref, dr_ref, dg_ref, db_ref = refs
            dy = y_ref[...]
        i = pl.program_id(0)
        xh = xh_ref[...]
        dr_ref[...] = _ln_dx(dy * g_ref[...], xh, rstd_ref[...])
        dg = jnp.sum(dy * xh, axis=0, keepdims=True)
        db = jnp.sum(dy, axis=0, keepdims=True)

        @pl.when(i == 0)
        def _():
            dg_ref[...] = dg
            db_ref[...] = db

        @pl.when(i > 0)
        def _():
            dg_ref[...] += dg
            db_ref[...] += db

        if with_loss:
            part = 0.5 * jnp.sum(jnp.mean(err * err, axis=-1, keepdims=True), axis=0, keepdims=True)
            part = jnp.broadcast_to(part, (8, 128))

            @pl.when(i == 0)
            def _():
                loss_ref[...] = part

            @pl.when(i > 0)
            def _():
                loss_ref[...] += part

    row = pl.BlockSpec((tm, D), lambda i: (i, 0))
    vec = pl.BlockSpec((1, D), lambda i: (0, 0))
    col = pl.BlockSpec((tm, 1), lambda i: (i, 0))
    in_specs = [pl.BlockSpec(memory_space=pl.ANY)] * na + ([vec, row] if with_loss else [row]) + [row, col, vec]
    out_specs = [row, vec, vec] + ([pl.BlockSpec((8, 128), lambda i: (0, 0))] if with_loss else [])
    out_shape = [jax.ShapeDtypeStruct((S, D), F32), jax.ShapeDtypeStruct((1, D), F32),
                 jax.ShapeDtypeStruct((1, D), F32)] + ([jax.ShapeDtypeStruct((8, 128), F32)] if with_loss else [])
    args = tuple(after) + (dout,) + ((target,) if with_loss else ()) + (xh, rstd, ln_g)
    return pl.pallas_call(
        body, name=name, grid=(S // tm,), in_specs=in_specs, out_specs=out_specs, out_shape=out_shape,
        compiler_params=_cp(("arbitrary",)),
    )(*args)


ROPE_TM = 256


def _rope_tables(pos_ref, invf_ref, sign):
    ang = pos_ref[...] * invf_ref[...]
    lane = lax.broadcasted_iota(jnp.int32, ang.shape, 1)
    first = (lane % DH) < (DH // 2)
    sinv = jnp.sin(ang) * sign
    return first, jnp.cos(ang), jnp.where(first, -sinv, sinv)


def _rotate(x, first, cosf, sinf):
    return x * cosf + jnp.where(first, pltpu.roll(x, 96, 1), pltpu.roll(x, 32, 1)) * sinf


def _proj_qkv_rope(hb, w_in, pos_f, invf, name):
    tm = 2 * ROPE_TM

    def body(h_ref, w_ref, pos_ref, invf_ref, o0_ref, o1_ref, o2_ref, buf_ref):
        rot = pl.program_id(1) < 2
        first, cosf, sinf = _rope_tables(pos_ref, invf_ref, 1.0)
        cosf = jnp.where(rot, cosf, 1.0)
        sinf = jnp.where(rot, sinf, 0.0)
        acc = _dot(h_ref[...], w_ref[...])
        for gi, (d, o_ref) in enumerate(zip(DILATIONS, (o0_ref, o1_ref, o2_ref))):
            for ch in range(GRP_W // 128):
                cols = slice(ch * 128, (ch + 1) * 128)
                x = _rotate(acc[:, gi * GRP_W + ch * 128: gi * GRP_W + (ch + 1) * 128], first, cosf, sinf)
                if d == 1:
                    o_ref[0, :, cols] = x.astype(BF16)
                else:
                    buf_ref[...] = x
                    for r in range(d):
                        o_ref[r, :, cols] = buf_ref[pl.ds(r, tm // d, stride=d), :].astype(BF16)

    return pl.pallas_call(
        body, name=name, grid=(S // tm, 3),
        in_specs=[pl.BlockSpec((tm, D), lambda i, s: (i, 0)), pl.BlockSpec((D, ATT_W), lambda i, s: (0, s)),
                  pl.BlockSpec((tm, 1), lambda i, s: (i, 0)), pl.BlockSpec((1, 128), lambda i, s: (0, 0))],
        out_specs=[pl.BlockSpec((d, tm // d, GRP_W), lambda i, s: (0, i, s)) for d in DILATIONS],
        out_shape=[jax.ShapeDtypeStruct((d, S // d, 3 * GRP_W), BF16) for d in DILATIONS],
        scratch_shapes=[pltpu.VMEM((tm, 128), F32)],
        compiler_params=_cp(("parallel", "parallel")),
    )(hb, w_in, pos_f, invf)


def _rope_bwd(dqkv_c, pos_f, invf, name):
    tm = ROPE_TM

    def body(*refs):
        g_refs, (pos_ref, invf_ref, o_ref, buf_ref) = refs[:9], refs[9:]
        first, cosf, sinf = _rope_tables(pos_ref, invf_ref, -1.0)
        for sec in range(3):
            for gi, d in enumerate(DILATIONS):
                g_ref = g_refs[3 * gi + sec]
                for ch in range(GRP_W // 128):
                    cols = slice(ch * 128, (ch + 1) * 128)
                    if d == 1:
                        x = g_ref[0, :, cols]
                    else:
                        for r in range(d):
                            buf_ref[pl.ds(r, tm // d, stride=d), :] = g_ref[r, :, cols]
                        x = buf_ref[...]
                    if sec < 2:
                        x = _rotate(x, first, cosf, sinf)
                    dst = sec * ATT_W + gi * GRP_W + ch * 128
                    o_ref[:, dst:dst + 128] = x.astype(BF16)

    g_specs = [pl.BlockSpec((d, tm // d, GRP_W), lambda i: (0, i, 0)) for d in DILATIONS for _ in range(3)]
    return pl.pallas_call(
        body, name=name, grid=(S // tm,),
        in_specs=g_specs + [pl.BlockSpec((tm, 1), lambda i: (i, 0)), pl.BlockSpec((1, 128), lambda i: (0, 0))],
        out_specs=pl.BlockSpec((tm, 3 * ATT_W), lambda i: (i, 0)),
        out_shape=jax.ShapeDtypeStruct((S, 3 * ATT_W), BF16),
        scratch_shapes=[pltpu.VMEM((tm, 128), F32)],
        compiler_params=_cp(("parallel",)),
    )(*[g for grp in dqkv_c for g in grp], pos_f, invf)


def _class_order(ts, name):
    tm = ROPE_TM
    n = len(ts)

    def body(*refs):
        buf_ref = refs[3 * n]
        for a in range(n):
            for ch in range(GRP_W // 128):
                cols = slice(ch * 128, (ch + 1) * 128)
                buf_ref[...] = refs[a][:, cols]
                for b, d in enumerate(DILATIONS[1:]):
                    for r in range(d):
                        refs[n + 2 * a + b][r, :, cols] = buf_ref[pl.ds(r, tm // d, stride=d), :]

    return pl.pallas_call(
        body, name=name, grid=(S // tm,),
        in_specs=[pl.BlockSpec((tm, GRP_W), lambda i: (i, 0))] * n,
        out_specs=[pl.BlockSpec((d, tm // d, GRP_W), lambda i: (0, i, 0)) for _ in range(n) for d in DILATIONS[1:]],
        out_shape=[jax.ShapeDtypeStruct((d, S // d, GRP_W), F32) for _ in range(n) for d in DILATIONS[1:]],
        scratch_shapes=[pltpu.VMEM((tm, 128), F32)],
        compiler_params=_cp(("parallel",)),
    )(*ts)


def _own_lanes(h):
    return (lax.broadcasted_iota(jnp.int32, (1, 2 * DH), 1) // DH) == (h % 2)


def _heads(ref):
    out = []
    for h in range(NH):
        pair = ref[:, (h // 2) * 2 * DH:(h // 2 + 1) * 2 * DH]
        out.append(jnp.where(_own_lanes(h), pair, jnp.zeros_like(pair)))
    return jnp.stack(out)


def _unheads(t3):
    return jnp.concatenate([t3[2 * p] + t3[2 * p + 1] for p in range(NH // 2)], axis=1)


def _bdot_nt(a, b):
    return lax.dot_general(a, b, (((2,), (2,)), ((0,), (0,))), preferred_element_type=F32)


def _bdot(a, b):
    return lax.dot_general(a, b, (((2,), (1,)), ((0,), (0,))), preferred_element_type=F32)


def _bdot_tn(a, b):
    return lax.dot_general(a, b, (((1,), (1,)), ((0,), (0,))), preferred_element_type=F32)


def _attn_fwd(gi, qkv_c, name):
    d = DILATIONS[gi]
    nblk = S // d // BLK

    def body(*refs):
        if nblk > 1:
            q_ref, kc_ref, kp_ref, vc_ref, vp_ref, o_ref, lse_ref = refs
            has_prev = pl.program_id(1) != 0
        else:
            q_ref, kc_ref, vc_ref, o_ref, lse_ref = refs
        qi = lax.broadcasted_iota(jnp.int32, (NH, BLK, BLK), 1)
        kj = lax.broadcasted_iota(jnp.int32, (NH, BLK, BLK), 2)
        q = _heads(q_ref)
        sc = jnp.where(kj <= qi, _bdot_nt(q, _heads(kc_ref)) * 0.125, NEG_INF)
        m = jnp.max(sc, axis=-1, keepdims=True)
        if nblk > 1:
            mask_p = jnp.logical_and(kj >= qi, has_prev)
            sp = jnp.where(mask_p, _bdot_nt(q, _heads(kp_ref)) * 0.125, NEG_INF)
            m = jnp.maximum(m, jnp.max(sp, axis=-1, keepdims=True))
        pc = jnp.exp(sc - m)
        l = jnp.sum(pc, axis=-1, keepdims=True)
        o = _bdot(pc.astype(BF16), _heads(vc_ref))
        if nblk > 1:
            pp = jnp.exp(sp - m)
            l = l + jnp.sum(pp, axis=-1, keepdims=True)
            o = o + _bdot(pp.astype(BF16), _heads(vp_ref))
        o_ref[...] = _unheads(o / l)
        lse = jnp.broadcast_to(m + jnp.log(l), (NH, BLK, 2 * DH))
        lse_ref[...] = _unheads(jnp.stack([jnp.where(_own_lanes(h), lse[h], 0.0) for h in range(NH)]))

    def cur(sec):
        return pl.BlockSpec((None, BLK, GRP_W), lambda r, n: (r, n, sec))

    def prev(sec):
        return pl.BlockSpec((None, BLK, GRP_W), lambda r, n: (r, jnp.maximum(n - 1, 0), sec))

    out = pl.BlockSpec((None, BLK, GRP_W), lambda r, n: (r, n, 0))
    shp = jax.ShapeDtypeStruct((d, S // d, GRP_W), F32)
    if nblk > 1:
        in_specs, args = [cur(0), cur(1), prev(1), cur(2), prev(2)], (qkv_c,) * 5
    else:
        in_specs, args = [cur(0), cur(1), cur(2)], (qkv_c,) * 3
    return pl.pallas_call(
        body, name=name, grid=(d, nblk), in_specs=in_specs, out_specs=[out, out], out_shape=[shp, shp],
        compiler_params=_cp(("parallel", "parallel")),
    )(*args)


def _attn_combine(os, lses, name):
    tm = ROPE_TM

    def body(o0_ref, o1_ref, o2_ref, l0_ref, l1_ref, l2_ref, y_ref, yt_ref, l_ref, buf_ref):
        def token_order(ref, d, cols, slot):
            if d == 1:
                return ref[0, :, cols]
            for r in range(d):
                buf_ref[slot, pl.ds(r, tm // d, stride=d), :] = ref[r, :, cols]
            return buf_ref[slot]

        for ch in range(GRP_W // 128):
            cols = slice(ch * 128, (ch + 1) * 128)
            o = [token_order(ref, d, cols, k) for k, (ref, d) in enumerate(zip((o0_ref, o1_ref, o2_ref), DILATIONS))]
            ls = [token_order(ref, d, cols, 3 + k)
                  for k, (ref, d) in enumerate(zip((l0_ref, l1_ref, l2_ref), DILATIONS))]
            m = jnp.maximum(jnp.maximum(ls[0], ls[1]), ls[2])
            e = [jnp.exp(l - m) for l in ls]
            den = e[0] + e[1] + e[2]
            y = (e[0] * o[0] + e[1] * o[1] + e[2] * o[2]) / den
            y_ref[:, cols] = y
            yt_ref[cols, :] = y.T.astype(BF16)
            l_ref[:, cols] = m + jnp.log(den)

    blk = pl.BlockSpec((tm, GRP_W), lambda i: (i, 0))
    cls = [pl.BlockSpec((d, tm // d, GRP_W), lambda i: (0, i, 0)) for d in DILATIONS]
    shp = jax.ShapeDtypeStruct((S, GRP_W), F32)
    return pl.pallas_call(
        body, name=name, grid=(S // tm,), in_specs=cls + cls,
        out_specs=[blk, pl.BlockSpec((GRP_W, tm), lambda i: (0, i)), blk],
        out_shape=[shp, jax.ShapeDtypeStruct((GRP_W, S), BF16), shp],
        scratch_shapes=[pltpu.VMEM((6, tm, 128), F32)],
        compiler_params=_cp(("parallel",)),
    )(*os, *lses)


def _attn_bwd(gi, qkv_c, dy_c, y_c, lse_c, name):
    d = DILATIONS[gi]
    nblk = S // d // BLK

    def body(*refs):
        if nblk > 1:
            (q_ref, qn_ref, k_ref, kp_ref, v_ref, vp_ref, dy_ref, dyn_ref, y_ref, yn_ref, l_ref, ln_ref,
             dq_ref, dk_ref, dv_ref) = refs
            n = pl.program_id(1)
            has_prev = n != 0
            has_next = n != nblk - 1
        else:
            q_ref, k_ref, v_ref, dy_ref, y_ref, l_ref, dq_ref, dk_ref, dv_ref = refs
        qi = lax.broadcasted_iota(jnp.int32, (NH, BLK, BLK), 1)
        kj = lax.broadcasted_iota(jnp.int32, (NH, BLK, BLK), 2)

        def lse_col(ref):
            return jnp.stack([ref[:, h * DH:h * DH + 1] for h in range(NH)])

        q, k, v = _heads(q_ref), _heads(k_ref), _heads(v_ref)
        dy = _heads(dy_ref)
        dd = jnp.sum(dy * _heads(y_ref), axis=-1, keepdims=True)
        lcol = lse_col(l_ref)
        dyb = dy.astype(BF16)
        p = jnp.exp(jnp.where(kj <= qi, _bdot_nt(q, k) * 0.125, NEG_INF) - lcol)
        ds = (p * (_bdot_nt(dyb, v) - dd)).astype(BF16)
        dq = _bdot(ds, k)
        dk = _bdot_tn(ds, q)
        dv = _bdot_tn(p.astype(BF16), dyb)
        if nblk > 1:
            qn, kpv, vpv = _heads(qn_ref), _heads(kp_ref), _heads(vp_ref)
            dyn = _heads(dyn_ref)
            ddn = jnp.sum(dyn * _heads(yn_ref), axis=-1, keepdims=True)
            lncol = lse_col(ln_ref)
            dynb = dyn.astype(BF16)
            mask_p = jnp.logical_and(kj >= qi, has_prev)
            pp = jnp.exp(jnp.where(mask_p, _bdot_nt(q, kpv) * 0.125, NEG_INF) - lcol)
            dsp = (pp * (_bdot_nt(dyb, vpv) - dd)).astype(BF16)
            dq = dq + _bdot(dsp, kpv)
            mask_n = jnp.logical_and(kj >= qi, has_next)
            pn = jnp.exp(jnp.where(mask_n, _bdot_nt(qn, k) * 0.125, NEG_INF) - lncol)
            dsn = (pn * (_bdot_nt(dynb, v) - ddn)).astype(BF16)
            dk = dk + _bdot_tn(dsn, qn)
            dv = dv + _bdot_tn(pn.astype(BF16), dynb)
        dq_ref[...] = _unheads(dq) * 0.125
        dk_ref[...] = _unheads(dk) * 0.125
        dv_ref[...] = _unheads(dv)

    def spec(sec, shift):
        def idx(r, n):
            return (r, jnp.clip(n + shift, 0, nblk - 1), sec)
        return pl.BlockSpec((None, BLK, GRP_W), idx)

    if nblk > 1:
        in_specs = [spec(0, 0), spec(0, 1), spec(1, 0), spec(1, -1), spec(2, 0), spec(2, -1),
                    spec(0, 0), spec(0, 1), spec(0, 0), spec(0, 1), spec(0, 0), spec(0, 1)]
        args = (qkv_c,) * 6 + (dy_c, dy_c, y_c, y_c, lse_c, lse_c)
    else:
        in_specs = [spec(0, 0), spec(1, 0), spec(2, 0), spec(0, 0), spec(0, 0), spec(0, 0)]
        args = (qkv_c, qkv_c, qkv_c, dy_c, y_c, lse_c)
    out = spec(0, 0)
    shp = jax.ShapeDtypeStruct((d, S // d, GRP_W), F32)
    return pl.pallas_call(
        body, name=name, grid=(d, nblk), in_specs=in_specs, out_specs=[out, out, out], out_shape=[shp, shp, shp],
        compiler_params=_cp(("parallel", "parallel")),
    )(*args)


_SQRT_HALF = 0.7071067811865476
_INV_SQRT_2PI = 0.3989422804014327


def _gelu(z):
    return 0.5 * z * (1.0 + lax.erf(z * _SQRT_HALF))


def _gelu_grad(z):
    return 0.5 * (1.0 + lax.erf(z * _SQRT_HALF)) + z * (jnp.exp(-0.5 * z * z) * _INV_SQRT_2PI)


def _tril_mask():
    t = lax.broadcasted_iota(jnp.int32, (BLK, BLK), 0)
    s = lax.broadcasted_iota(jnp.int32, (BLK, BLK), 1)
    return s <= t


def _groups(t):
    return jnp.stack([t[:, g * BLK:(g + 1) * BLK] for g in range(8)])


def _ungroup(t3):
    return jnp.concatenate([t3[g] for g in range(8)], axis=1)


def _group_bias(bs_ref):
    return jnp.stack([bs_ref[:, g:g + 1] for g in range(8)])


def _gmlp_fwd(z, ln_g, ln_b, w_s, b_s_t, name):
    def body(z_ref, g_ref, b_ref, ws_ref, bs_ref, y_ref, yt_ref):
        zg = _gelu(z_ref[...])
        u = zg[:, :GW]
        xh, _ = _ln_stats(zg[:, GW:])
        vn = (xh * g_ref[...] + b_ref[...]).astype(BF16)
        wt = jnp.where(_tril_mask(), ws_ref[...], 0.0).astype(BF16)
        yv = u * _ungroup(_bdot(wt, _groups(vn)) + _group_bias(bs_ref))
        y_ref[...] = yv.astype(BF16)
        yt_ref[...] = yv.T.astype(BF16)

    vec = pl.BlockSpec((1, GW), lambda n: (0, 0))
    return pl.pallas_call(
        body, name=name, grid=(NBLK,),
        in_specs=[pl.BlockSpec((BLK, 2 * GW), lambda n: (n, 0)), vec, vec,
                  pl.BlockSpec((8, BLK, BLK), lambda n: (0, 0, 0)), pl.BlockSpec((BLK, 8), lambda n: (0, 0))],
        out_specs=[pl.BlockSpec((BLK, GW), lambda n: (n, 0)), pl.BlockSpec((GW, BLK), lambda n: (0, n))],
        out_shape=[jax.ShapeDtypeStruct((S, GW), BF16), jax.ShapeDtypeStruct((GW, S), BF16)],
        compiler_params=_cp(("parallel",)),
    )(z, ln_g, ln_b, w_s, b_s_t)


def _gmlp_bwd(z, dy, ln_g, ln_b, w_s, b_s_t, name):
    def body(z_ref, dy_ref, g_ref, b_ref, ws_ref, bs_ref, dz_ref, dws_ref, dbs_ref, dg_ref, db_ref, dvn_ref):
        n = pl.program_id(0)
        zv = z_ref[...]
        zg = _gelu(zv)
        u = zg[:, :GW]
        xh, rstd = _ln_stats(zg[:, GW:])
        vn = (xh * g_ref[...] + b_ref[...]).astype(BF16)
        tril = _tril_mask()

        @pl.when(n == 0)
        def _():
            dws_ref[...] = jnp.zeros_like(dws_ref)
            dbs_ref[...] = jnp.zeros_like(dbs_ref)
            dg_ref[...] = jnp.zeros_like(dg_ref)
            db_ref[...] = jnp.zeros_like(db_ref)

        wt = jnp.where(tril, ws_ref[...], 0.0).astype(BF16)
        vn3 = _groups(vn)
        dyv = dy_ref[...]
        mixed = _ungroup(_bdot(wt, vn3) + _group_bias(bs_ref))
        dz_ref[:, :GW] = (dyv * mixed * _gelu_grad(zv[:, :GW])).astype(BF16)
        dmix3 = _groups(dyv * u)
        dmb = dmix3.astype(BF16)
        dws_ref[...] += jnp.where(tril, _bdot_nt(dmb, vn3), 0.0)
        dbsum = jnp.sum(dmix3, axis=-1, keepdims=True)
        for gg in range(8):
            dbs_ref[:, gg:gg + 1] += dbsum[gg]
        dvn_ref[...] = _ungroup(_bdot_tn(wt, dmb))

        dvn = dvn_ref[...]
        dg_ref[...] += jnp.sum(dvn * xh, axis=0, keepdims=True)
        db_ref[...] += jnp.sum(dvn, axis=0, keepdims=True)
        dvg = _ln_dx(dvn * g_ref[...], xh, rstd)
        dz_ref[:, GW:] = (dvg * _gelu_grad(zv[:, GW:])).astype(BF16)

    vec = pl.BlockSpec((1, GW), lambda n: (0, 0))
    ws = pl.BlockSpec((8, BLK, BLK), lambda n: (0, 0, 0))
    bs = pl.BlockSpec((BLK, 8), lambda n: (0, 0))
    return pl.pallas_call(
        body, name=name, grid=(NBLK,),
        in_specs=[pl.BlockSpec((BLK, 2 * GW), lambda n: (n, 0)), pl.BlockSpec((BLK, GW), lambda n: (n, 0)),
                  vec, vec, ws, bs],
        out_specs=[pl.BlockSpec((BLK, 2 * GW), lambda n: (n, 0)), ws, bs, vec, vec],
        out_shape=[jax.ShapeDtypeStruct((S, 2 * GW), BF16), jax.ShapeDtypeStruct((8, BLK, BLK), F32),
                   jax.ShapeDtypeStruct((BLK, 8), F32), jax.ShapeDtypeStruct((1, GW), F32),
                   jax.ShapeDtypeStruct((1, GW), F32)],
        scratch_shapes=[pltpu.VMEM((BLK, GW), F32)],
        compiler_params=_cp(("arbitrary",)),
    )(z, dy, ln_g, ln_b, w_s, b_s_t)


def _merge_fwd(a, b, gl, b_gates, name):
    tm = 512

    def body(a_ref, b_ref, g0_ref, g1_ref, bg_ref, o_ref, ot_ref):
        g0 = jax.nn.sigmoid(g0_ref[...] + bg_ref[:, :D])
        g1 = jax.nn.sigmoid(g1_ref[...] + bg_ref[:, D:])
        mg = g0 * a_ref[...] + g1 * b_ref[...]
        o_ref[...] = mg.astype(BF16)
        ot_ref[...] = mg.T.astype(BF16)

    row = pl.BlockSpec((tm, D), lambda i: (i, 0))
    return pl.pallas_call(
        body, name=name, grid=(S // tm,),
        in_specs=[row, row, row, pl.BlockSpec((tm, D), lambda i: (i, 1)), pl.BlockSpec((1, 2 * D), lambda i: (0, 0))],
        out_specs=[row, pl.BlockSpec((D, tm), lambda i: (0, i))],
        out_shape=[jax.ShapeDtypeStruct((S, D), BF16), jax.ShapeDtypeStruct((D, S), BF16)],
        compiler_params=_cp(("parallel",)),
    )(a, b, gl, gl, b_gates)


def _merge_bwd(dm, a, b, gl, b_gates, name):
    tm = 512

    def body(dm_ref, a_ref, b_ref, g0_ref, g1_ref, bg_ref, da_ref, db_ref, dgl_ref, dbg_ref):
        i = pl.program_id(0)
        dmv = dm_ref[...]
        g0 = jax.nn.sigmoid(g0_ref[...] + bg_ref[:, :D])
        g1 = jax.nn.sigmoid(g1_ref[...] + bg_ref[:, D:])
        da_ref[...] = (dmv * g0).astype(BF16)
        db_ref[...] = (dmv * g1).astype(BF16)
        d0 = dmv * a_ref[...] * g0 * (1.0 - g0)
        d1 = dmv * b_ref[...] * g1 * (1.0 - g1)
        dgl_ref[:, :D] = d0.astype(BF16)
        dgl_ref[:, D:] = d1.astype(BF16)
        s0 = jnp.sum(d0, axis=0, keepdims=True)
        s1 = jnp.sum(d1, axis=0, keepdims=True)

        @pl.when(i == 0)
        def _():
            dbg_ref[:, :D] = s0
            dbg_ref[:, D:] = s1

        @pl.when(i > 0)
        def _():
            dbg_ref[:, :D] += s0
            dbg_ref[:, D:] += s1

    row = pl.BlockSpec((tm, D), lambda i: (i, 0))
    wide = pl.BlockSpec((tm, 2 * D), lambda i: (i, 0))
    bg = pl.BlockSpec((1, 2 * D), lambda i: (0, 0))
    return pl.pallas_call(
        body, name=name, grid=(S // tm,),
        in_specs=[row, row, row, row, pl.BlockSpec((tm, D), lambda i: (i, 1)), bg],
        out_specs=[row, row, wide, bg],
        out_shape=[jax.ShapeDtypeStruct((S, D), BF16), jax.ShapeDtypeStruct((S, D), BF16),
                   jax.ShapeDtypeStruct((S, 2 * D), BF16), jax.ShapeDtypeStruct((1, 2 * D), F32)],
        compiler_params=_cp(("arbitrary",)),
    )(dm, a, b, gl, gl, b_gates)


def _adam_math(w, g, m, v):
    m2 = ADAM_B1 * m + (1.0 - ADAM_B1) * g
    v2 = ADAM_B2 * v + (1.0 - ADAM_B2) * (g * g)
    m_hat = m2 / (1.0 - ADAM_B1 ** ADAM_STEP)
    v_hat = v2 / (1.0 - ADAM_B2 ** ADAM_STEP)
    delta = -ADAM_LR * (m_hat / (jnp.sqrt(v_hat) + ADAM_EPS) + ADAM_WD * w)
    return delta, m2, v2


def _pick_rows(rows, cols, unit=16, budget=MIB):
    best = unit
    for t in range(unit, rows + 1, unit):
        if rows % t == 0 and t * cols * 4 <= budget:
            best = t
    assert rows % best == 0
    return best


def _adamw(w, g, m, v, name):
    r, c = w.shape
    tr = _pick_rows(r, c, unit=8)

    def body(w_ref, g_ref, m_ref, v_ref, go_ref, d_ref, mo_ref, vo_ref):
        gv = g_ref[...]
        delta, m2, v2 = _adam_math(w_ref[...], gv, m_ref[...], v_ref[...])
        go_ref[...] = gv
        d_ref[...] = delta
        mo_ref[...] = m2
        vo_ref[...] = v2

    blk = pl.BlockSpec((tr, c), lambda i: (i, 0))
    shp = jax.ShapeDtypeStruct((r, c), F32)
    return pl.pallas_call(
        body, name=name, grid=(r // tr,), in_specs=[blk] * 4, out_specs=[blk] * 4, out_shape=[shp] * 4,
        compiler_params=_cp(("parallel",)),
    )(*[pltpu.with_memory_space_constraint(t, pltpu.HBM) for t in (w, g, m, v)])


def _small_sum_adamw(parts, own, pos, w, m, v, name):
    tr = 48

    def body(pos_ref, p_ref, own_ref, w_ref, m_ref, v_ref, g_ref, d_ref, mo_ref, vo_ref):
        me = 2 * pos_ref[1] + pos_ref[0]
        gv = None
        for k in range(8):
            term = jnp.where(me == k, own_ref[...], p_ref[k])
            gv = term if gv is None else gv + term
        delta, m2, v2 = _adam_math(w_ref[...], gv, m_ref[...], v_ref[...])
        g_ref[...] = gv
        d_ref[...] = delta
        mo_ref[...] = m2
        vo_ref[...] = v2

    blk = pl.BlockSpec((tr, D), lambda i, p: (i, 0))
    shp = jax.ShapeDtypeStruct((SMALL_ROWS, D), F32)
    return pl.pallas_call(
        body, name=name,
        grid_spec=pltpu.PrefetchScalarGridSpec(
            num_scalar_prefetch=1, grid=(SMALL_ROWS // tr,),
            in_specs=[pl.BlockSpec((8, tr, D), lambda i, p: (0, i, 0)), blk, blk, blk, blk],
            out_specs=[blk] * 4),
        out_shape=[shp] * 4,
        compiler_params=_cp(("parallel",)),
    )(pos, parts, own, w, m, v)


ANY = pl.BlockSpec(memory_space=pl.ANY)


def _in_hbm(arrays):
    return [pltpu.with_memory_space_constraint(a, pltpu.HBM) for a in arrays]


def _mesh_pos():
    x, y, c = lax.axis_index("x"), lax.axis_index("y"), lax.axis_index("c")
    chips = [(1 - x, y), (x, 1 - y), (1 - x, 1 - y)]
    return x, y, c, chips


def _place_shard(w, kind, pos, name):
    r, c = w.shape
    tr = _pick_rows(r, c)

    def body(pos_ref, w_ref, o_ref):
        o_ref[...] = w_ref[...].astype(BF16)

    if kind == "stack":
        o_spec = pl.BlockSpec((None, tr, c), lambda i, p: (p[1], i, 0))
        shape = (NSH, r, c)
    else:
        o_spec = pl.BlockSpec((tr, c), lambda i, p: (i, p[1]))
        shape = (r, NSH * c)
    return pl.pallas_call(
        body, name=name,
        grid_spec=pltpu.PrefetchScalarGridSpec(
            num_scalar_prefetch=1, grid=(r // tr,),
            in_specs=[pl.BlockSpec((tr, c), lambda i, p: (i, 0))], out_specs=o_spec),
        out_shape=pltpu.HBM(shape, BF16),
        compiler_params=_cp(("parallel",)),
    )(pos, pltpu.with_memory_space_constraint(w, pltpu.HBM))


SEM = pl.BlockSpec(memory_space=pltpu.SEMAPHORE)
SPLIT_COPY = pltpu.CompilerParams(has_side_effects=pltpu.SideEffectType.DATAFLOW_SIDE_EFFECTING)


def _shard_window(ref, kind, j, h, dims):
    r, c = dims
    rows = pl.ds(pl.multiple_of(h * (r // 2), 16), r // 2)
    if kind == "stack":
        return ref.at[j, rows, :]
    return ref.at[rows, pl.ds(pl.multiple_of(j * c, 128), c)]


def _ici_copy(ref, kind, dims, j, c, sems, idx, to):
    win = _shard_window(ref, kind, j, c, dims)
    return pltpu.make_async_remote_copy(src_ref=win, dst_ref=win, send_sem=sems[0].at[idx], recv_sem=sems[1].at[idx],
                                        device_id=to, device_id_type=MESH_T)


def _gather_start(fulls, kinds, dims, after, name):
    n, na = len(fulls), len(after)

    def body(*refs):
        outs = refs[n + na:2 * n + na]
        send_sems, recv_sems, token = refs[2 * n + na:]
        x, y, c, chips = _mesh_pos()
        for a in range(n):
            for k, chip in enumerate(chips):
                _ici_copy(outs[a], kinds[a], dims[a], 2 * x + y, c, (send_sems, recv_sems), 3 * a + k,
                          (chip[0], chip[1], c)).start()
        token[...] = jnp.zeros_like(token)

    res = pl.pallas_call(
        body, name=name, in_specs=[ANY] * (n + na),
        out_specs=[ANY] * n + [SEM, SEM, pl.BlockSpec(memory_space=pltpu.VMEM)],
        out_shape=[pltpu.HBM(f.shape, BF16) for f in fulls]
        + [pltpu.SemaphoreType.DMA((3 * n,)), pltpu.SemaphoreType.DMA((3 * n,)), jax.ShapeDtypeStruct((8, 128), F32)],
        input_output_aliases={i: i for i in range(n)},
        compiler_params=SPLIT_COPY,
    )(*_in_hbm(fulls), *after)
    return res[:n], res[n], res[n + 1], res[n + 2]


def _gather_wait(fulls, send_sems, recv_sems, kinds, dims, after, name):
    n, na = len(fulls), len(after)

    def body(*refs):
        ssem, rsem = refs[n], refs[n + 1]
        outs = refs[n + 2 + na:]
        x, y, c, chips = _mesh_pos()
        for a in range(n):
            for k, chip in enumerate(chips):
                to = (chip[0], chip[1], c)
                _ici_copy(outs[a], kinds[a], dims[a], 2 * x + y, c, (ssem, rsem), 3 * a + k, to).wait_send()
                _ici_copy(outs[a], kinds[a], dims[a], 2 * chip[0] + chip[1], c, (ssem, rsem), 3 * a + k, to).wait_recv()

    return pl.pallas_call(
        body, name=name, in_specs=[ANY] * n + [SEM, SEM] + [ANY] * na, out_specs=[ANY] * n,
        out_shape=[pltpu.HBM(f.shape, BF16) for f in fulls],
        input_output_aliases={i: i for i in range(n)},
        compiler_params=SPLIT_COPY,
    )(*_in_hbm(fulls), send_sems, recv_sems, *after)


def _gather_forward(fulls, kinds, dims, name):
    n = len(fulls)

    def body(*refs):
        outs = refs[n:2 * n]
        sems = refs[2 * n:]
        x, y, c, chips = _mesh_pos()
        sib = (x, y, 1 - c)
        cps = []
        for a in range(n):
            for k, chip in enumerate(chips):
                cp = _ici_copy(outs[a], kinds[a], dims[a], 2 * chip[0] + chip[1], c, sems, 3 * a + k, sib)
                cp.start()
                cps.append(cp)
        for a in range(n):
            for k, chip in enumerate(chips):
                _ici_copy(outs[a], kinds[a], dims[a], 2 * chip[0] + chip[1], 1 - c, sems, 3 * a + k, sib).wait_recv()
        for cp in cps:
            cp.wait_send()

    return pl.pallas_call(
        body, name=name, in_specs=[ANY] * n, out_specs=[ANY] * n,
        out_shape=[pltpu.HBM(f.shape, BF16) for f in fulls],
        input_output_aliases={i: i for i in range(n)},
        scratch_shapes=[pltpu.SemaphoreType.DMA((3 * n,)), pltpu.SemaphoreType.DMA((3 * n,))],
    )(*_in_hbm(fulls))


def _pair_copy(src, land, a, x, y, c, sems):
    return pltpu.make_async_remote_copy(
        src_ref=src.at[1 - c], dst_ref=land, send_sem=sems[0].at[a], recv_sem=sems[1].at[a],
        device_id=(x, y, 1 - c), device_id_type=MESH_T)


def _pair_start(grads, lands, name):
    n = len(grads)

    def body(*refs):
        srcs, dsts = refs[2 * n:3 * n], refs[3 * n:4 * n]
        send_sems, recv_sems, token = refs[4 * n:]
        x, y, c, _ = _mesh_pos()
        for a in range(n):
            _pair_copy(srcs[a], dsts[a], a, x, y, c, (send_sems, recv_sems)).start()
        token[...] = jnp.zeros_like(token)

    res = pl.pallas_call(
        body, name=name, in_specs=[ANY] * (2 * n),
        out_specs=[ANY] * (2 * n) + [SEM, SEM, pl.BlockSpec(memory_space=pltpu.VMEM)],
        out_shape=[pltpu.HBM(g.shape, F32) for g in grads]
        + [pltpu.HBM(l.shape, F32) for l in lands]
        + [pltpu.SemaphoreType.DMA((n,)), pltpu.SemaphoreType.DMA((n,)), jax.ShapeDtypeStruct((8, 128), F32)],
        input_output_aliases={i: i for i in range(2 * n)},
        compiler_params=SPLIT_COPY,
    )(*_in_hbm(grads), *_in_hbm(lands))
    return res[:n], res[n:2 * n], res[2 * n], res[2 * n + 1], res[2 * n + 2]


def _pair_wait(grads, lands, send_sems, recv_sems, after, name):
    n, na = len(grads), len(after)

    def body(*refs):
        ssem, rsem = refs[2 * n], refs[2 * n + 1]
        outs = refs[2 * n + 2 + na:]
        x, y, c, _ = _mesh_pos()
        for a in range(n):
            cp = _pair_copy(outs[a], outs[n + a], a, x, y, c, (ssem, rsem))
            cp.wait_send()
            cp.wait_recv()

    res = pl.pallas_call(
        body, name=name, in_specs=[ANY] * (2 * n) + [SEM, SEM] + [ANY] * na, out_specs=[ANY] * (2 * n),
        out_shape=[pltpu.HBM(g.shape, F32) for g in grads]
        + [pltpu.HBM(l.shape, F32) for l in lands],
        input_output_aliases={i: i for i in range(2 * n)},
        compiler_params=SPLIT_COPY,
    )(*_in_hbm(grads), *_in_hbm(lands), send_sems, recv_sems, *after)
    return res[:n], res[n:]


def _pair_sum(g, recv, pos, name):
    _, _, rh, c = g.shape
    tr = _pick_rows(rh, c)

    def body(pos_ref, g_ref, r_ref, o_ref):
        o_ref[...] = (g_ref[...] + r_ref[...]).astype(BF16)

    return pl.pallas_call(
        body, name=name,
        grid_spec=pltpu.PrefetchScalarGridSpec(
            num_scalar_prefetch=1, grid=(3, rh // tr),
            in_specs=[pl.BlockSpec((None, None, tr, c), lambda k, r, p: (p[0], p[2 + k], r, 0)),
                      pl.BlockSpec((None, tr, c), lambda k, r, p: (p[2 + k], r, 0))],
            out_specs=pl.BlockSpec((None, tr, c), lambda k, r, p: (k, r, 0))),
        out_shape=pltpu.HBM((3, rh, c), BF16),
        compiler_params=_cp(("parallel", "parallel")),
    )(pos, *_in_hbm([g, recv]))


def _chip_copy(src, land, a, k, chip, c, sems):
    return pltpu.make_async_remote_copy(
        src_ref=src.at[k], dst_ref=land.at[k], send_sem=sems[0].at[3 * a + k],
        recv_sem=sems[1].at[3 * a + k], device_id=(chip[0], chip[1], c), device_id_type=MESH_T)


def _chip_start(psums, lands, name):
    n = len(psums)

    def body(*refs):
        srcs, dsts = refs[2 * n:3 * n], refs[3 * n:4 * n]
        send_sems, recv_sems, token = refs[4 * n:]
        x, y, c, chips = _mesh_pos()
        for a in range(n):
            for k, chip in enumerate(chips):
                _chip_copy(srcs[a], dsts[a], a, k, chip, c, (send_sems, recv_sems)).start()
        token[...] = jnp.zeros_like(token)

    res = pl.pallas_call(
        body, name=name, in_specs=[ANY] * (2 * n),
        out_specs=[ANY] * (2 * n) + [SEM, SEM, pl.BlockSpec(memory_space=pltpu.VMEM)],
        out_shape=[pltpu.HBM(p.shape, BF16) for p in psums]
        + [pltpu.HBM(l.shape, BF16) for l in lands]
        + [pltpu.SemaphoreType.DMA((3 * n,)), pltpu.SemaphoreType.DMA((3 * n,)), jax.ShapeDtypeStruct((8, 128), F32)],
        input_output_aliases={i: i for i in range(2 * n)},
        compiler_params=SPLIT_COPY,
    )(*_in_hbm(psums), *_in_hbm(lands))
    return res[:n], res[n:2 * n], res[2 * n], res[2 * n + 1], res[2 * n + 2]


def _chip_wait(psums, lands, send_sems, recv_sems, after, name):
    n, na = len(psums), len(after)

    def body(*refs):
        ssem, rsem = refs[2 * n], refs[2 * n + 1]
        outs = refs[2 * n + 2 + na:]
        srcs, dsts = outs[:n], outs[n:]
        x, y, c, chips = _mesh_pos()
        for a in range(n):
            for k, chip in enumerate(chips):
                cp = _chip_copy(srcs[a], dsts[a], a, k, chip, c, (ssem, rsem))
                cp.wait_send()
                cp.wait_recv()

    res = pl.pallas_call(
        body, name=name, in_specs=[ANY] * (2 * n) + [SEM, SEM] + [ANY] * na, out_specs=[ANY] * (2 * n),
        out_shape=[pltpu.HBM(p.shape, BF16) for p in psums]
        + [pltpu.HBM(l.shape, BF16) for l in lands],
        input_output_aliases={i: i for i in range(2 * n)},
        compiler_params=SPLIT_COPY,
    )(*_in_hbm(psums), *_in_hbm(lands), send_sems, recv_sems, *after)
    return res[n:]


def _owner_sum(g, recv_a, recv_b, pos, name):
    _, _, rh, c = g.shape
    tr = _pick_rows(rh, c)

    def body(pos_ref, g_ref, ra_ref, rb_ref, o_ref):
        acc = g_ref[...] + ra_ref[...]
        for k in range(3):
            acc = acc + rb_ref[k].astype(F32)
        o_ref[...] = acc

    return pl.pallas_call(
        body, name=name,
        grid_spec=pltpu.PrefetchScalarGridSpec(
            num_scalar_prefetch=1, grid=(rh // tr,),
            in_specs=[pl.BlockSpec((None, None, tr, c), lambda r, p: (p[0], p[1], r, 0)),
                      pl.BlockSpec((None, tr, c), lambda r, p: (p[1], r, 0)),
                      pl.BlockSpec((3, tr, c), lambda r, p: (0, r, 0))],
            out_specs=pl.BlockSpec((None, tr, c), lambda r, p: (p[0], r, 0))),
        out_shape=pltpu.HBM((2, rh, c), F32),
        compiler_params=_cp(("parallel",)),
    )(pos, *_in_hbm([g, recv_a, recv_b]))


def _sibling_allgather(halves, name):
    n = len(halves)

    def body(*refs):
        outs = refs[n:2 * n]
        send_sems, recv_sems = refs[2 * n:]
        x, y, c, _ = _mesh_pos()
        cps = []
        for a in range(n):
            cp = pltpu.make_async_remote_copy(
                src_ref=outs[a].at[c], dst_ref=outs[a].at[c], send_sem=send_sems.at[a], recv_sem=recv_sems.at[a],
                device_id=(x, y, 1 - c), device_id_type=MESH_T)
            cp.start()
            cps.append(cp)
        for a in range(n):
            cps[a].wait_send()
            pltpu.make_async_remote_copy(
                src_ref=outs[a].at[1 - c], dst_ref=outs[a].at[1 - c], send_sem=send_sems.at[a],
                recv_sem=recv_sems.at[a], device_id=(x, y, 1 - c), device_id_type=MESH_T).wait_recv()

    return pl.pallas_call(
        body, name=name, in_specs=[ANY] * n, out_specs=[ANY] * n,
        out_shape=[pltpu.HBM(h.shape, F32) for h in halves],
        input_output_aliases={i: i for i in range(n)},
        scratch_shapes=[pltpu.SemaphoreType.DMA((n,)), pltpu.SemaphoreType.DMA((n,))],
    )(*_in_hbm(halves))


def _peers(x, y, c):
    rel = [(0, 0, 1), (0, 1, 0), (0, 1, 1), (1, 0, 0), (1, 0, 1), (1, 1, 0), (1, 1, 1)]
    return [((1 - x) if dx else x, (1 - y) if dy else y, (1 - c) if dc else c) for dx, dy, dc in rel]


def _small_copy(src, land, k, peer, slot, sems):
    return pltpu.make_async_remote_copy(src_ref=src, dst_ref=land.at[slot], send_sem=sems[0].at[k],
                                        recv_sem=sems[1].at[k], device_id=peer, device_id_type=MESH_T)


def _small_start(part, land, name):
    def body(p_in, l_in, p_ref, l_ref, send_sems, recv_sems, token):
        x, y, c, _ = _mesh_pos()
        for k, peer in enumerate(_peers(x, y, c)):
            _small_copy(p_ref, l_ref, k, peer, 4 * x + 2 * y + c, (send_sems, recv_sems)).start()
        token[...] = jnp.zeros_like(token)

    return pl.pallas_call(
        body, name=name, in_specs=[ANY, ANY],
        out_specs=[ANY, ANY, SEM, SEM, pl.BlockSpec(memory_space=pltpu.VMEM)],
        out_shape=[pltpu.HBM(part.shape, F32), pltpu.HBM(land.shape, F32), pltpu.SemaphoreType.DMA((7,)),
                   pltpu.SemaphoreType.DMA((7,)), jax.ShapeDtypeStruct((8, 128), F32)],
        input_output_aliases={0: 0, 1: 1},
        compiler_params=SPLIT_COPY,
    )(*_in_hbm([part, land]))


def _small_wait(part, land, send_sems, recv_sems, after, name):
    na = len(after)

    def body(*refs):
        ssem, rsem = refs[2], refs[3]
        p_ref, l_ref = refs[4 + na:]
        x, y, c, _ = _mesh_pos()
        for k, peer in enumerate(_peers(x, y, c)):
            cp = _small_copy(p_ref, l_ref, k, peer, 4 * peer[0] + 2 * peer[1] + peer[2], (ssem, rsem))
            cp.wait_send()
            cp.wait_recv()

    return pl.pallas_call(
        body, name=name, in_specs=[ANY, ANY, SEM, SEM] + [ANY] * na, out_specs=[ANY, ANY],
        out_shape=[pltpu.HBM(part.shape, F32), pltpu.HBM(land.shape, F32)],
        input_output_aliases={0: 0, 1: 1},
        compiler_params=SPLIT_COPY,
    )(*_in_hbm([part, land]), send_sems, recv_sems, *after)


def _pack_small(ln1_g, ln1_b, gln_g, gln_b, ln2_g, ln2_b, ln3_g, ln3_b, b_gates, b_s, w_s):
    rows = [ln1_g, ln1_b, gln_g, gln_b, ln2_g, ln2_b, ln3_g, ln3_b]
    rows = [r.reshape(1, D) for r in rows] + [b_gates.reshape(2, D), b_s.reshape(1, D), jnp.zeros((5, D), F32),
                                             w_s.reshape(128, D)]
    return jnp.concatenate(rows, axis=0)


def _unpack_small(p):
    out = [p[i:i + 1] for i in range(8)]
    return out + [p[8:10].reshape(1, 2 * D), p[10:11].reshape(1, 8, BLK), p[16:144].reshape(1, 8, BLK, BLK)]


GROUPS = (("f1g", "f1u", "f1d"), ("w_in",), ("w_ab", "w_gb", "w_out"), ("f2g", "f2u", "f2d"))


def _local_step(x, pos_f, target, P, weights_of, grads_ready, flush, small_ready):
    invf = ROPE_THETA ** (-jnp.arange(0, DH, 2, dtype=F32) / DH)
    invf = jnp.tile(invf, 4).reshape(1, 128)
    b_s_t = P["gmlp_b_s"].T

    W = dict(weights_of(0, []))
    h1b, xh1, rstd1, a1, b1, h1t = _ffn_fwd(x, W["f1g"], W["f1u"], W["f1d"], P["ln1_g"], P["ln1_b"], "ffn1_fwd",
                                                emit_t=True)
    W.update(weights_of(1, [h1b]))
    qkv_c = _proj_qkv_rope(h1b, W["w_in"], pos_f, invf, "proj_qkv_rope")
    z = _matmul(h1b, W["w_in"], "nn", "proj_z", n=2 * GW, b_col0=3 * ATT_W, tm=S, tn=512)
    gl = _matmul(h1b, W["w_in"], "nn", "proj_gates", n=2 * D, b_col0=3 * ATT_W + 2 * GW, tm=S, tn=512)
    og = [_attn_fwd(gi, qkv_c[gi], "attn_fwd_g%d" % gi) for gi in range(NG)]
    y_attn, y_attn_t, lse = _attn_combine([o for o, _ in og], [l for _, l in og], "attn_combine")
    y_gmlp, y_gmlp_t = _gmlp_fwd(z, P["gmlp_ln_g"], P["gmlp_ln_b"], P["gmlp_w_s"], b_s_t, "gmlp_fwd")
    W.update(weights_of(2, [y_gmlp]))
    br_a = _matmul(y_attn, W["w_ab"], "nn", "branch_attn", n=D, tm=1024, tn=D)
    br_b = _matmul(y_gmlp, W["w_gb"], "nn", "branch_gmlp", n=D, tm=1024, tn=D)
    merged, merged_t = _merge_fwd(br_a, br_b, gl, P["b_gates"], "merge_fwd")
    mix = _matmul(merged, W["w_out"], "nn", "mix_out", n=D, tm=1024, tn=D)
    h2, h2b, xh2, rstd2 = _resid_ln(xh1, P["ln1_g"], P["ln1_b"], mix, P["ln2_g"], P["ln2_b"], "resid_ln2")
    W.update(weights_of(3, [h2b]))
    _, xh3, rstd3, a2, b2 = _ffn_fwd(h2, W["f2g"], W["f2u"], W["f2d"], P["ln3_g"], P["ln3_b"], "ffn2_fwd")

    dr3, dg3, db3, loss = _ln_bwd(P["ln3_b"], xh3, rstd3, P["ln3_g"], "loss_ln3_bwd", target=target)
    g_f2g, g_f2u, g_f2d, dh2 = _ffn_bwd(dr3, h2b, a2, b2, W["f2g"], W["f2u"], W["f2d"], "ffn2_bwd")
    tok = grads_ready(3, dict(f2g=g_f2g, f2u=g_f2u, f2d=g_f2d))
    dr2, dg2, db2 = _ln_bwd(dh2, xh2, rstd2, P["ln2_g"], "ln2_bwd", after=tok)
    g_wout = _wgrad(merged_t, dr2, 128, D, "dw_out", row_sharded=True)
    dmerged = _matmul(dr2, W["w_out"], "nt", "dmerged", n=D, tm=1024, tn=D)
    dab, dbb, dglb, dbg = _merge_bwd(dmerged, br_a, br_b, gl, P["b_gates"], "merge_bwd")
    tok = flush([dab])
    g_wab = _wgrad(y_attn_t, dab, GRP_W // 2, 256, "dw_attn_branch", row_sharded=False, after=tok)
    g_wgb = _wgrad(y_gmlp_t, dbb, 128, D, "dw_gmlp_branch", row_sharded=True)
    tok = grads_ready(2, dict(w_ab=g_wab, w_gb=g_wgb, w_out=g_wout))
    dy_attn = _matmul(dab, W["w_ab"], "nt", "dy_attn", n=GRP_W, tm=1024, tn=GRP_W, after=tok)
    dy_gmlp = _matmul(dbb, W["w_gb"], "nt", "dy_gmlp", n=GW, tm=1024, tn=GW)
    dzb, dws, dbs_t, dgln_g, dgln_b = _gmlp_bwd(z, dy_gmlp, P["gmlp_ln_g"], P["gmlp_ln_b"], P["gmlp_w_s"], b_s_t,
                                                 "gmlp_bwd")
    cls = _class_order([dy_attn, y_attn, lse], "attn_class_order")
    dqkv_c = []
    for gi in range(NG):
        dy_c, y_c, lse_c = [t[None] if gi == 0 else cls[2 * a + gi - 1] for a, t in enumerate((dy_attn, y_attn, lse))]
        dqkv_c.append(_attn_bwd(gi, qkv_c[gi], dy_c, y_c, lse_c, "attn_bwd_g%d" % gi))
    dqkvb = _rope_bwd(dqkv_c, pos_f, invf, "rope_bwd")
    dproj = jnp.concatenate([dqkvb, dzb, dglb], axis=1)
    tok = flush([dproj])
    g_win = _wgrad(h1t, dproj, D // 2, IN_SH, "dw_in", row_sharded=False, after=tok)
    tok = grads_ready(1, dict(w_in=g_win))
    dh1 = _matmul(dproj, W["w_in"], "nt", "dh1", n=D, tm=1024, tn=D, tk=IN_SH, add=dr2, add_scale=ALPHA, after=tok)
    dr1, dg1, db1 = _ln_bwd(dh1, xh1, rstd1, P["ln1_g"], "ln1_bwd")
    tok = flush([dr1])
    tok = tok + small_ready(_pack_small(dg1, db1, dgln_g, dgln_b, dg2, db2, dg3, db3, dbg, dbs_t.T, dws))
    g_f1g, g_f1u, g_f1d, dx = _ffn_bwd(dr1, x.astype(BF16), a1, b1, W["f1g"], W["f1u"], W["f1d"], "ffn1_bwd",
                                       after=tok)
    grads_ready(0, dict(f1g=g_f1g, f1u=g_f1u, f1d=g_f1d))
    flush([dx])
    return loss, dx


BIG = ("f1g", "f1u", "f1d", "w_in", "w_ab", "w_gb", "w_out", "f2g", "f2u", "f2d")
TRANSPOSED = ("f1g", "f1u", "f2g", "f2u")
KIND = dict(f1g="stack", f1u="stack", f1d="stack", w_in="col", w_ab="col", w_gb="stack", w_out="stack",
            f2g="stack", f2u="stack", f2d="stack")


def kernel(x, positions, ffn1_w_gate, ffn1_w_up, ffn1_w_down, ln1_g, ln1_b, w_in, b_gates, gmlp_ln_g, gmlp_ln_b, gmlp_w_s, gmlp_b_s, w_attn_branch, w_gmlp_branch, w_out, ln2_g, ln2_b, ffn2_w_gate, ffn2_w_up, ffn2_w_down, ln3_g, ln3_b, loss_target, m_ffn1_w_gate, m_ffn1_w_up, m_ffn1_w_down, m_ln1_g, m_ln1_b, m_w_in, m_b_gates, m_gmlp_ln_g, m_gmlp_ln_b, m_gmlp_w_s, m_gmlp_b_s, m_w_attn_branch, m_w_gmlp_branch, m_w_out, m_ln2_g, m_ln2_b, m_ffn2_w_gate, m_ffn2_w_up, m_ffn2_w_down, m_ln3_g, m_ln3_b, v_ffn1_w_gate, v_ffn1_w_up, v_ffn1_w_down, v_ln1_g, v_ln1_b, v_w_in, v_b_gates, v_gmlp_ln_g, v_gmlp_ln_b, v_gmlp_w_s, v_gmlp_b_s, v_w_attn_branch, v_w_gmlp_branch, v_w_out, v_ln2_g, v_ln2_b, v_ffn2_w_gate, v_ffn2_w_up, v_ffn2_w_down, v_ln3_g, v_ln3_b):
    cx, cy, cc = lax.axis_index("x"), lax.axis_index("y"), lax.axis_index("c")
    pos = jnp.stack([cc, 2 * cx + cy, 2 * (1 - cx) + cy, 2 * cx + 1 - cy, 2 * (1 - cx) + 1 - cy]).astype(jnp.int32)

    w_sh = dict(f1g=ffn1_w_gate, f1u=ffn1_w_up, f1d=ffn1_w_down, w_in=w_in, w_ab=w_attn_branch,
                w_gb=w_gmlp_branch, w_out=w_out, f2g=ffn2_w_gate, f2u=ffn2_w_up, f2d=ffn2_w_down)
    m_sh = dict(f1g=m_ffn1_w_gate, f1u=m_ffn1_w_up, f1d=m_ffn1_w_down, w_in=m_w_in, w_ab=m_w_attn_branch,
                w_gb=m_w_gmlp_branch, w_out=m_w_out, f2g=m_ffn2_w_gate, f2u=m_ffn2_w_up, f2d=m_ffn2_w_down)
    v_sh = dict(f1g=v_ffn1_w_gate, f1u=v_ffn1_w_up, f1d=v_ffn1_w_down, w_in=v_w_in, w_ab=v_w_attn_branch,
                w_gb=v_w_gmlp_branch, w_out=v_w_out, f2g=v_ffn2_w_gate, f2u=v_ffn2_w_up, f2d=v_ffn2_w_down)
    w_sh = {k: (v[0].T if k in TRANSPOSED else v[0]) for k, v in w_sh.items()}
    m_sh = {k: (v[0].T if k in TRANSPOSED else v[0]) for k, v in m_sh.items()}
    v_sh = {k: (v[0].T if k in TRANSPOSED else v[0]) for k, v in v_sh.items()}

    started, tokens = [], []
    for gi, names in enumerate(GROUPS):
        placed = [_place_shard(w_sh[k], KIND[k], pos, "place_" + k) for k in names]
        fulls, ssem, rsem, token = _gather_start(placed, [KIND[k] for k in names], [w_sh[k].shape for k in names],
                                                 tokens[-1:], "gather_start_g%d" % gi)
        started.append((fulls, ssem, rsem))
        tokens.append(token)

    def weights_of(gi, after):
        names = GROUPS[gi]
        kinds, dims = [KIND[k] for k in names], [w_sh[k].shape for k in names]
        fulls, ssem, rsem = started[gi]
        fulls = _gather_wait(fulls, ssem, rsem, kinds, dims, list(after) + (tokens if gi == 0 else []),
                             "gather_wait_g%d" % gi)
        fulls = _gather_forward(fulls, kinds, dims, "gather_forward_g%d" % gi)
        return {k: (f.reshape(D, D) if k in ("w_gb", "w_out") else f) for k, f in zip(names, fulls)}

    pending, inflight = [], {}

    def grads_ready(gi, gd):
        grads = [gd[k] for k in GROUPS[gi]]
        lands = [lax.empty(g.shape[1:], F32) for g in grads]
        grads, lands, ssem, rsem, token = _pair_start(grads, lands, "rs_pair_start_g%d" % gi)
        pending.append((gi, grads, lands, ssem, rsem))
        return [token]

    def flush(after):
        gi, grads, lands, ssem, rsem = pending.pop()
        names = GROUPS[gi]
        grads, recv_a = _pair_wait(grads, lands, ssem, rsem, after, "rs_pair_wait_g%d" % gi)
        psums = [_pair_sum(g, r, pos, "rs_pair_sum_" + k) for g, r, k in zip(grads, recv_a, names)]
        lands = [lax.empty((3,) + p.shape[1:], BF16) for p in psums]
        psums, lands, ssem, rsem, token = _chip_start(psums, lands, "rs_chip_start_g%d" % gi)
        inflight[gi] = (grads, recv_a, psums, lands, ssem, rsem, token)
        return [token]

    P = dict(ln1_g=ln1_g, ln1_b=ln1_b, ln2_g=ln2_g, ln2_b=ln2_b, ln3_g=ln3_g, ln3_b=ln3_b, b_gates=b_gates,
             gmlp_ln_g=gmlp_ln_g, gmlp_ln_b=gmlp_ln_b, gmlp_w_s=gmlp_w_s[0], gmlp_b_s=gmlp_b_s[0])
    pos_f = positions.reshape(S, 1).astype(F32)
    small_state = []

    def small_ready(packed):
        land = jnp.zeros((8, SMALL_ROWS, D), F32)
        packed, land, ssem, rsem, token = _small_start(packed, land, "small_start")
        small_state.append((packed, land, ssem, rsem))
        return [token]

    loss_part, dx = _local_step(x[0], pos_f, loss_target[0], P, weights_of, grads_ready, flush, small_ready)
    loss = lax.psum(loss_part[0, 0], ("x", "y", "c"))

    g_out, d_out, m_out, v_out = {}, {}, {}, {}

    def finish(gi, after):
        grads, recv_a, psums, lands, ssem, rsem, token = inflight[gi]
        recv_b = _chip_wait(psums, lands, ssem, rsem, after + [inflight[0][6]], "rs_chip_wait_g%d" % gi)
        halves = [_owner_sum(g, ra, rb, pos, "rs_owner_sum_" + k)
                  for g, ra, rb, k in zip(grads, recv_a, recv_b, GROUPS[gi])]
        reduced = _sibling_allgather(halves, "rs_sibling_allgather_g%d" % gi)
        for k, gfull in zip(GROUPS[gi], reduced):
            res = _adamw(w_sh[k], gfull.reshape(w_sh[k].shape), m_sh[k], v_sh[k], "adamw_" + k)
            after = [res[1]]
            if k in TRANSPOSED:
                res = [r.T for r in res]
            g_out[k], d_out[k], m_out[k], v_out[k] = [r[None] for r in res]
        return after

    after = []
    for gi in (3, 2, 1):
        after = finish(gi, after)

    small, parts = _small_wait(*small_state[0], after, "small_wait")
    sp = (ln1_g, ln1_b, gmlp_ln_g, gmlp_ln_b, ln2_g, ln2_b, ln3_g, ln3_b, b_gates, gmlp_b_s, gmlp_w_s)
    sm = (m_ln1_g, m_ln1_b, m_gmlp_ln_g, m_gmlp_ln_b, m_ln2_g, m_ln2_b, m_ln3_g, m_ln3_b, m_b_gates, m_gmlp_b_s,
          m_gmlp_w_s)
    sv = (v_ln1_g, v_ln1_b, v_gmlp_ln_g, v_gmlp_ln_b, v_ln2_g, v_ln2_b, v_ln3_g, v_ln3_b, v_b_gates, v_gmlp_b_s,
          v_gmlp_w_s)
    sg, sd, smn, svn = _small_sum_adamw(parts, small, pos, _pack_small(*sp), _pack_small(*sm), _pack_small(*sv),
                                        "small_adamw")
    names = ("ln1_g", "ln1_b", "gmlp_ln_g", "gmlp_ln_b", "ln2_g", "ln2_b", "ln3_g", "ln3_b", "b_gates", "gmlp_b_s",
             "gmlp_w_s")
    for dst, packed in ((g_out, sg), (d_out, sd), (m_out, smn), (v_out, svn)):
        for nm, val in zip(names, _unpack_small(packed)):
            dst[nm] = val
    finish(0, [sg])

    order = ("f1g", "f1u", "f1d", "ln1_g", "ln1_b", "w_in", "b_gates", "gmlp_ln_g", "gmlp_ln_b", "gmlp_w_s", "gmlp_b_s",
             "w_ab", "w_gb", "w_out", "ln2_g", "ln2_b", "f2g", "f2u", "f2d", "ln3_g", "ln3_b")
    outs = [loss, dx[None]]
    for dst in (g_out, d_out, m_out, v_out):
        outs += [dst[k] for k in order]
    return tuple(outs)
```

```python
import functools
import math

import jax
import jax.numpy as jnp
from jax import lax
from jax.experimental import pallas as pl
from jax.experimental.pallas import tpu as pltpu

F32 = jnp.float32
BF16 = jnp.bfloat16

S = 2048
D = 1024
NSH = 4
FSH = 704
ATT_W = 1536
GRP_W = 512
NG = 3
NH = 8
DH = 64
BLK = 128
NBLK = S // BLK
GW = 1024
IN_W = 8704
IN_SH = IN_W // NSH
ALPHA = 2.0 ** 0.25
LN_EPS = 1e-5
ROPE_THETA = 10000.0
DILATIONS = (1, 4, 16)
ADAM_LR, ADAM_B1, ADAM_B2, ADAM_EPS, ADAM_WD, ADAM_STEP = 0.001, 0.9, 0.999, 1e-08, 0.01, 10
SMALL_ROWS = 144
MESH_T = pl.DeviceIdType.MESH
MIB = 1024 * 1024
NEG_INF = float("-inf")


def _cp(sem, vmem_mib=48):
    return pltpu.CompilerParams(dimension_semantics=sem, vmem_limit_bytes=vmem_mib * MIB)


def _ln_stats(r):
    mu = jnp.mean(r, axis=-1, keepdims=True)
    xc = r - mu
    var = jnp.mean(xc * xc, axis=-1, keepdims=True)
    rstd = lax.rsqrt(var + LN_EPS)
    return xc * rstd, rstd


def _ln_dx(dxh, xh, rstd):
    m1 = jnp.mean(dxh, axis=-1, keepdims=True)
    m2 = jnp.mean(dxh * xh, axis=-1, keepdims=True)
    return rstd * (dxh - m1 - xh * m2)


def _dot_nt(a, b):
    return lax.dot_general(a, b, (((1,), (1,)), ((), ())), preferred_element_type=F32)


def _dot_tn(a, b):
    return lax.dot_general(a, b, (((0,), (0,)), ((), ())), preferred_element_type=F32)


def _dot(a, b):
    return jnp.dot(a, b, preferred_element_type=F32)


def _ffn_fwd(xin, wgt, wut, wd, ln_g, ln_b, name, emit_t=False):
    tm = 1024

    def body(x_ref, wg_ref, wu_ref, wd_ref, g_ref, b_ref, *rest):
        if emit_t:
            hb_ref, xh_ref, rstd_ref, a_ref, bb_ref, ht_ref, acc_ref = rest
        else:
            hb_ref, xh_ref, rstd_ref, a_ref, bb_ref, acc_ref = rest
        j = pl.program_id(1)
        xb = x_ref[...].astype(BF16)
        a = _dot_nt(xb, wg_ref[...])
        b = _dot_nt(xb, wu_ref[...])
        a_ref[...] = a.astype(BF16)
        bb_ref[...] = b.astype(BF16)
        s = (a * jax.nn.sigmoid(a)) * b
        f = _dot(s.astype(BF16), wd_ref[...])

        @pl.when(j == 0)
        def _():
            acc_ref[...] = f

        @pl.when(j > 0)
        def _():
            acc_ref[...] += f

        @pl.when(j == NSH - 1)
        def _():
            r = ALPHA * x_ref[...] + 0.5 * acc_ref[...]
            xh, rstd = _ln_stats(r)
            h = xh * g_ref[...] + b_ref[...]
            hb_ref[...] = h.astype(BF16)
            xh_ref[...] = xh
            rstd_ref[...] = rstd
            if emit_t:
                ht_ref[...] = h.T.astype(BF16)

    row = pl.BlockSpec((tm, D), lambda i, j: (i, 0))
    vec = pl.BlockSpec((1, D), lambda i, j: (0, 0))
    wsp = pl.BlockSpec((None, FSH, D), lambda i, j: (j, 0, 0))
    ab = pl.BlockSpec((None, tm, FSH), lambda i, j: (j, i, 0))
    out_specs = [row, row, pl.BlockSpec((tm, 1), lambda i, j: (i, 0)), ab, ab]
    out_shape = [jax.ShapeDtypeStruct((S, D), BF16),
                 jax.ShapeDtypeStruct((S, D), F32), jax.ShapeDtypeStruct((S, 1), F32),
                 jax.ShapeDtypeStruct((NSH, S, FSH), BF16), jax.ShapeDtypeStruct((NSH, S, FSH), BF16)]
    if emit_t:
        out_specs.append(pl.BlockSpec((D, tm), lambda i, j: (0, i)))
        out_shape.append(jax.ShapeDtypeStruct((D, S), BF16))
    return pl.pallas_call(
        body, name=name, grid=(S // tm, NSH),
        in_specs=[row, wsp, wsp, wsp, vec, vec], out_specs=out_specs, out_shape=out_shape,
        scratch_shapes=[pltpu.VMEM((tm, D), F32)],
        compiler_params=_cp(("parallel", "arbitrary"), vmem_mib=56),
    )(xin, wgt, wut, wd, ln_g, ln_b)


def _ffn_bwd(dr, xin_b, a, b, wgt, wut, wd, name, after=()):
    tm = 512
    ni = S // tm
    hr = FSH // 2

    def body(dr_ref, a_ref, b_ref, wg_ref, wu_ref, wd_ref, x_hbm, *rest):
        dwg_hbm, dwu_hbm, dwd_hbm, dx_hbm, dx_acc, da_all, db_all, s_all, df_all, x_all, res_buf, sems = rest[len(after):]
        j = pl.program_id(0)
        i = pl.program_id(1)
        rows = pl.ds(pl.multiple_of(i * tm, tm), tm)

        @pl.when(jnp.logical_and(j == 0, i == 0))
        def _():
            cp = pltpu.make_async_copy(x_hbm, x_all, sems.at[0])
            cp.start()
            cp.wait()

        drv = dr_ref[...]
        df = (0.5 * drv).astype(BF16)

        @pl.when(j == 0)
        def _():
            df_all[rows, :] = df

        ds = jnp.concatenate([_dot_nt(df, wd_ref[0:384, :]), _dot_nt(df, wd_ref[384:FSH, :])], axis=1)
        av = a_ref[...].astype(F32)
        bv = b_ref[...].astype(F32)
        sig = jax.nn.sigmoid(av)
        sl = av * sig
        da = (ds * bv * (sig * (1.0 + av * (1.0 - sig)))).astype(BF16)
        db = (ds * sl).astype(BF16)
        da_all[rows, :] = da
        db_all[rows, :] = db
        s_all[rows, :] = (sl * bv).astype(BF16)
        dx = _dot(da, wg_ref[...]) + _dot(db, wu_ref[...])

        @pl.when(j == 0)
        def _():
            dx_acc[rows, :] = ALPHA * drv + dx

        @pl.when(j > 0)
        def _():
            dx_acc[rows, :] += dx

        @pl.when(i == ni - 1)
        def _():
            copies = []
            for n, (lhs, rhs, out) in enumerate(((da_all, x_all, dwg_hbm), (db_all, x_all, dwu_hbm),
                                                 (s_all, df_all, dwd_hbm))):
                slot = n % 2
                if n >= 2:
                    for cp in copies[2 * (n - 2): 2 * (n - 2) + 2]:
                        cp.wait()
                res_buf[slot] = _dot_tn(lhs[...], rhs[...])
                for h in range(2):
                    cp = pltpu.make_async_copy(res_buf.at[slot, pl.ds(h * hr, hr), :], out.at[h, j],
                                               sems.at[1 + 2 * slot + h])
                    cp.start()
                    copies.append(cp)
            for cp in copies[2:]:
                cp.wait()

        @pl.when(jnp.logical_and(j == NSH - 1, i == ni - 1))
        def _():
            cp = pltpu.make_async_copy(dx_acc, dx_hbm, sems.at[0])
            cp.start()
            cp.wait()

    row = pl.BlockSpec((tm, D), lambda j, i: (i, 0))
    wsp = pl.BlockSpec((None, FSH, D), lambda j, i: (j, 0, 0))
    ab = pl.BlockSpec((None, tm, FSH), lambda j, i: (j, i, 0))
    dwshape = jax.ShapeDtypeStruct((2, NSH, hr, D), F32)
    return pl.pallas_call(
        body, name=name, grid=(NSH, ni),
        in_specs=[row, ab, ab, wsp, wsp, wsp, ANY] + [ANY] * len(after),
        out_specs=[ANY, ANY, ANY, ANY],
        out_shape=[dwshape, dwshape, dwshape, jax.ShapeDtypeStruct((S, D), F32)],
        scratch_shapes=[pltpu.VMEM((S, D), F32), pltpu.VMEM((S, FSH), BF16), pltpu.VMEM((S, FSH), BF16),
                        pltpu.VMEM((S, FSH), BF16), pltpu.VMEM((S, D), BF16), pltpu.VMEM((S, D), BF16),
                        pltpu.VMEM((2, FSH, D), F32), pltpu.SemaphoreType.DMA((5,))],
        compiler_params=_cp(("arbitrary", "arbitrary"), vmem_mib=58),
    )(dr, a, b, wgt, wut, wd, xin_b, *after)


def _matmul(a, b, mode, name, *, n, tm=512, tn=512, tk=None, b_col0=0, add=None, add_scale=1.0, out_dtype=F32,
            after=()):
    m, ka = a.shape
    tk = ka if tk is None else tk
    nk = ka // tk
    assert m % tm == 0 and n % tn == 0 and ka % tk == 0 and b_col0 % tn == 0
    off = b_col0 // tn
    na = len(after)

    def body(*refs):
        refs = refs[na:]
        if add is None:
            a_ref, b_ref, o_ref = refs[:3]
            add_ref = None
            rest = refs[3:]
        else:
            a_ref, b_ref, add_ref, o_ref = refs[:4]
            rest = refs[4:]
        k = pl.program_id(2)
        av = a_ref[...].astype(BF16)
        bv = b_ref[...].astype(BF16)
        p = _dot(av, bv) if mode == "nn" else _dot_nt(av, bv)

        def finish(acc):
            if add_ref is not None:
                acc = acc + add_scale * add_ref[...]
            o_ref[...] = acc.astype(out_dtype)

        if nk == 1:
            finish(p)
        else:
            acc_ref = rest[0]

            @pl.when(k == 0)
            def _():
                acc_ref[...] = p

            @pl.when(k > 0)
            def _():
                acc_ref[...] += p

            @pl.when(k == nk - 1)
            def _():
                finish(acc_ref[...])

    a_spec = pl.BlockSpec((tm, tk), lambda i, j, k: (i, k))
    if mode == "nn":
        b_spec = pl.BlockSpec((tk, tn), lambda i, j, k: (k, j + off))
    else:
        b_spec = pl.BlockSpec((tn, tk), lambda i, j, k: (j, k))
    o_spec = pl.BlockSpec((tm, tn), lambda i, j, k: (i, j))
    in_specs = [pl.BlockSpec(memory_space=pl.ANY)] * na + [a_spec, b_spec] + ([o_spec] if add is not None else [])
    args = tuple(after) + (a, b) + ((add,) if add is not None else ())
    return pl.pallas_call(
        body, name=name, grid=(m // tm, n // tn, nk),
        in_specs=in_specs, out_specs=o_spec,
        out_shape=jax.ShapeDtypeStruct((m, n), out_dtype),
        scratch_shapes=[pltpu.VMEM((tm, tn), F32)] if nk > 1 else [],
        compiler_params=_cp(("parallel", "parallel", "arbitrary")),
    )(*args)


def _wgrad(xt, y, rh, c, name, row_sharded, after=()):
    na = len(after)
    if row_sharded:
        def body(x_ref, y_ref, *rest):
            o_ref = rest[na]
            res = _dot(x_ref[...], y_ref[...].astype(BF16))
            for j in range(NSH):
                for h in range(2):
                    o_ref[h, j] = res[(2 * j + h) * rh:(2 * j + h + 1) * rh, :]

        grid = (1,)
        in_specs = [pl.BlockSpec((2 * NSH * rh, S), lambda g: (0, 0)), pl.BlockSpec((S, c), lambda g: (0, 0))]
        out_specs = pl.BlockSpec((2, NSH, rh, c), lambda g: (0, 0, 0, 0))
        sem = ("arbitrary",)
    else:
        def body(x_ref, y_ref, *rest):
            rest[na][...] = _dot(x_ref[...], y_ref[...].astype(BF16))

        grid = (2, NSH)
        in_specs = [pl.BlockSpec((rh, S), lambda h, j: (h, 0)), pl.BlockSpec((S, c), lambda h, j: (0, j))]
        out_specs = pl.BlockSpec((None, None, rh, c), lambda h, j: (h, j, 0, 0))
        sem = ("parallel", "parallel")
    return pl.pallas_call(
        body, name=name, grid=grid, in_specs=in_specs + [pl.BlockSpec(memory_space=pl.ANY)] * na, out_specs=out_specs,
        out_shape=jax.ShapeDtypeStruct((2, NSH, rh, c), F32),
        compiler_params=_cp(sem, vmem_mib=56),
    )(xt, y, *after)


def _resid_ln(res_xh, res_g, res_b, f, ln_g, ln_b, name):
    tm = 512

    def body(rx_ref, rg_ref, rb_ref, f_ref, g_ref, b_ref, h_ref, hb_ref, xh_ref, rstd_ref):
        r = ALPHA * (rx_ref[...] * rg_ref[...] + rb_ref[...]) + f_ref[...]
        xh, rstd = _ln_stats(r)
        h = xh * g_ref[...] + b_ref[...]
        h_ref[...] = h
        hb_ref[...] = h.astype(BF16)
        xh_ref[...] = xh
        rstd_ref[...] = rstd

    row = pl.BlockSpec((tm, D), lambda i: (i, 0))
    vec = pl.BlockSpec((1, D), lambda i: (0, 0))
    return pl.pallas_call(
        body, name=name, grid=(S // tm,),
        in_specs=[row, vec, vec, row, vec, vec],
        out_specs=[row, row, row, pl.BlockSpec((tm, 1), lambda i: (i, 0))],
        out_shape=[jax.ShapeDtypeStruct((S, D), F32), jax.ShapeDtypeStruct((S, D), BF16),
                   jax.ShapeDtypeStruct((S, D), F32), jax.ShapeDtypeStruct((S, 1), F32)],
        compiler_params=_cp(("parallel",)),
    )(res_xh, res_g, res_b, f, ln_g, ln_b)


def _ln_bwd(dout, xh, rstd, ln_g, name, target=None, after=()):
    tm = 512
    with_loss = target is not None
    na = len(after)

    def body(*refs):
        refs = refs[na:]
        if with_loss:
            bias_ref, t_ref, xh_ref, rstd_ref, g_ref, dr_ref, dg_ref, db_ref, loss_ref = refs
            err = (xh_ref[...] * g_ref[...] + bias_ref[...]) - t_ref[...]
            dy = err * (1.0 / D)
        else:
            y_ref, xh_ref, rstd_ref, g_ref, dr_ref, dg_ref, db_ref = refs
            dy = y_ref[...]
        i = pl.program_id(0)
        xh = xh_ref[...]
        dr_ref[...] = _ln_dx(dy * g_ref[...], xh, rstd_ref[...])
        dg = jnp.sum(dy * xh, axis=0, keepdims=True)
        db = jnp.sum(dy, axis=0, keepdims=True)

        @pl.when(i == 0)
        def _():
            dg_ref[...] = dg
            db_ref[...] = db

        @pl.when(i > 0)
        def _():
            dg_ref[...] += dg
            db_ref[...] += db

        if with_loss:
            part = 0.5 * jnp.sum(jnp.mean(err * err, axis=-1, keepdims=True), axis=0, keepdims=True)
            part = jnp.broadcast_to(part, (8, 128))

            @pl.when(i == 0)
            def _():
                loss_ref[...] = part

            @pl.when(i > 0)
            def _():
                loss_ref[...] += part

    row = pl.BlockSpec((tm, D), lambda i: (i, 0))
    vec = pl.BlockSpec((1, D), lambda i: (0, 0))
    col = pl.BlockSpec((tm, 1), lambda i: (i, 0))
    in_specs = [pl.BlockSpec(memory_space=pl.ANY)] * na + ([vec, row] if with_loss else [row]) + [row, col, vec]
    out_specs = [row, vec, vec] + ([pl.BlockSpec((8, 128), lambda i: (0, 0))] if with_loss else [])
    out_shape = [jax.ShapeDtypeStruct((S, D), F32), jax.ShapeDtypeStruct((1, D), F32),
                 jax.ShapeDtypeStruct((1, D), F32)] + ([jax.ShapeDtypeStruct((8, 128), F32)] if with_loss else [])
    args = tuple(after) + (dout,) + ((target,) if with_loss else ()) + (xh, rstd, ln_g)
    return pl.pallas_call(
        body, name=name, grid=(S // tm,), in_specs=in_specs, out_specs=out_specs, out_shape=out_shape,
        compiler_params=_cp(("arbitrary",)),
    )(*args)


ROPE_TM = 256


def _rope_tables(pos_ref, invf_ref, sign):
    ang = pos_ref[...] * invf_ref[...]
    lane = lax.broadcasted_iota(jnp.int32, ang.shape, 1)
    first = (lane % DH) < (DH // 2)
    sinv = jnp.sin(ang) * sign
    return first, jnp.cos(ang), jnp.where(first, -sinv, sinv)


def _rotate(x, first, cosf, sinf):
    return x * cosf + jnp.where(first, pltpu.roll(x, 96, 1), pltpu.roll(x, 32, 1)) * sinf


def _proj_qkv_rope(hb, w_in, pos_f, invf, name):
    tm = 2 * ROPE_TM

    def body(h_ref, w_ref, pos_ref, invf_ref, o0_ref, o1_ref, o2_ref, buf_ref):
        rot = pl.program_id(1) < 2
        first, cosf, sinf = _rope_tables(pos_ref, invf_ref, 1.0)
        cosf = jnp.where(rot, cosf, 1.0)
        sinf = jnp.where(rot, sinf, 0.0)
        acc = _dot(h_ref[...], w_ref[...])
        for gi, (d, o_ref) in enumerate(zip(DILATIONS, (o0_ref, o1_ref, o2_ref))):
            for ch in range(GRP_W // 128):
                cols = slice(ch * 128, (ch + 1) * 128)
                x = _rotate(acc[:, gi * GRP_W + ch * 128: gi * GRP_W + (ch + 1) * 128], first, cosf, sinf)
                if d == 1:
                    o_ref[0, :, cols] = x.astype(BF16)
                else:
                    buf_ref[...] = x
                    for r in range(d):
                        o_ref[r, :, cols] = buf_ref[pl.ds(r, tm // d, stride=d), :].astype(BF16)

    return pl.pallas_call(
        body, name=name, grid=(S // tm, 3),
        in_specs=[pl.BlockSpec((tm, D), lambda i, s: (i, 0)), pl.BlockSpec((D, ATT_W), lambda i, s: (0, s)),
                  pl.BlockSpec((tm, 1), lambda i, s: (i, 0)), pl.BlockSpec((1, 128), lambda i, s: (0, 0))],
        out_specs=[pl.BlockSpec((d, tm // d, GRP_W), lambda i, s: (0, i, s)) for d in DILATIONS],
        out_shape=[jax.ShapeDtypeStruct((d, S // d, 3 * GRP_W), BF16) for d in DILATIONS],
        scratch_shapes=[pltpu.VMEM((tm, 128), F32)],
        compiler_params=_cp(("parallel", "parallel")),
    )(hb, w_in, pos_f, invf)


def _rope_bwd(dqkv_c, pos_f, invf, name):
    tm = ROPE_TM

    def body(*refs):
        g_refs, (pos_ref, invf_ref, o_ref, buf_ref) = refs[:9], refs[9:]
        first, cosf, sinf = _rope_tables(pos_ref, invf_ref, -1.0)
        for sec in range(3):
            for gi, d in enumerate(DILATIONS):
                g_ref = g_refs[3 * gi + sec]
                for ch in range(GRP_W // 128):
                    cols = slice(ch * 128, (ch + 1) * 128)
                    if d == 1:
                        x = g_ref[0, :, cols]
                    else:
                        for r in range(d):
                            buf_ref[pl.ds(r, tm // d, stride=d), :] = g_ref[r, :, cols]
                        x = buf_ref[...]
                    if sec < 2:
                        x = _rotate(x, first, cosf, sinf)
                    dst = sec * ATT_W + gi * GRP_W + ch * 128
                    o_ref[:, dst:dst + 128] = x.astype(BF16)

    g_specs = [pl.BlockSpec((d, tm // d, GRP_W), lambda i: (0, i, 0)) for d in DILATIONS for _ in range(3)]
    return pl.pallas_call(
        body, name=name, grid=(S // tm,),
        in_specs=g_specs + [pl.BlockSpec((tm, 1), lambda i: (i, 0)), pl.BlockSpec((1, 128), lambda i: (0, 0))],
        out_specs=pl.BlockSpec((tm, 3 * ATT_W), lambda i: (i, 0)),
        out_shape=jax.ShapeDtypeStruct((S, 3 * ATT_W), BF16),
        scratch_shapes=[pltpu.VMEM((tm, 128), F32)],
        compiler_params=_cp(("parallel",)),
    )(*[g for grp in dqkv_c for g in grp], pos_f, invf)


def _class_order(ts, name):
    tm = ROPE_TM
    n = len(ts)

    def body(*refs):
        buf_ref = refs[3 * n]
        for a in range(n):
            for ch in range(GRP_W // 128):
                cols = slice(ch * 128, (ch + 1) * 128)
                buf_ref[...] = refs[a][:, cols]
                for b, d in enumerate(DILATIONS[1:]):
                    for r in range(d):
                        refs[n + 2 * a + b][r, :, cols] = buf_ref[pl.ds(r, tm // d, stride=d), :]

    return pl.pallas_call(
        body, name=name, grid=(S // tm,),
        in_specs=[pl.BlockSpec((tm, GRP_W), lambda i: (i, 0))] * n,
        out_specs=[pl.BlockSpec((d, tm // d, GRP_W), lambda i: (0, i, 0)) for _ in range(n) for d in DILATIONS[1:]],
        out_shape=[jax.ShapeDtypeStruct((d, S // d, GRP_W), F32) for _ in range(n) for d in DILATIONS[1:]],
        scratch_shapes=[pltpu.VMEM((tm, 128), F32)],
        compiler_params=_cp(("parallel",)),
    )(*ts)


def _own_lanes(h):
    return (lax.broadcasted_iota(jnp.int32, (1, 2 * DH), 1) // DH) == (h % 2)


def _heads(ref):
    out = []
    for h in range(NH):
        pair = ref[:, (h // 2) * 2 * DH:(h // 2 + 1) * 2 * DH]
        out.append(jnp.where(_own_lanes(h), pair, jnp.zeros_like(pair)))
    return jnp.stack(out)


def _unheads(t3):
    return jnp.concatenate([t3[2 * p] + t3[2 * p + 1] for p in range(NH // 2)], axis=1)


def _bdot_nt(a, b):
    return lax.dot_general(a, b, (((2,), (2,)), ((0,), (0,))), preferred_element_type=F32)


def _bdot(a, b):
    return lax.dot_general(a, b, (((2,), (1,)), ((0,), (0,))), preferred_element_type=F32)


def _bdot_tn(a, b):
    return lax.dot_general(a, b, (((1,), (1,)), ((0,), (0,))), preferred_element_type=F32)


def _attn_fwd(gi, qkv_c, name):
    d = DILATIONS[gi]
    nblk = S // d // BLK

    def body(*refs):
        if nblk > 1:
            q_ref, kc_ref, kp_ref, vc_ref, vp_ref, o_ref, lse_ref = refs
            has_prev = pl.program_id(1) != 0
        else:
            q_ref, kc_ref, vc_ref, o_ref, lse_ref = refs
        qi = lax.broadcasted_iota(jnp.int32, (NH, BLK, BLK), 1)
        kj = lax.broadcasted_iota(jnp.int32, (NH, BLK, BLK), 2)
        q = _heads(q_ref)
        sc = jnp.where(kj <= qi, _bdot_nt(q, _heads(kc_ref)) * 0.125, NEG_INF)
        m = jnp.max(sc, axis=-1, keepdims=True)
        if nblk > 1:
            mask_p = jnp.logical_and(kj >= qi, has_prev)
            sp = jnp.where(mask_p, _bdot_nt(q, _heads(kp_ref)) * 0.125, NEG_INF)
            m = jnp.maximum(m, jnp.max(sp, axis=-1, keepdims=True))
        pc = jnp.exp(sc - m)
        l = jnp.sum(pc, axis=-1, keepdims=True)
        o = _bdot(pc.astype(BF16), _heads(vc_ref))
        if nblk > 1:
            pp = jnp.exp(sp - m)
            l = l + jnp.sum(pp, axis=-1, keepdims=True)
            o = o + _bdot(pp.astype(BF16), _heads(vp_ref))
        o_ref[...] = _unheads(o / l)
        lse = jnp.broadcast_to(m + jnp.log(l), (NH, BLK, 2 * DH))
        lse_ref[...] = _unheads(jnp.stack([jnp.where(_own_lanes(h), lse[h], 0.0) for h in range(NH)]))

    def cur(sec):
        return pl.BlockSpec((None, BLK, GRP_W), lambda r, n: (r, n, sec))

    def prev(sec):
        return pl.BlockSpec((None, BLK, GRP_W), lambda r, n: (r, jnp.maximum(n - 1, 0), sec))

    out = pl.BlockSpec((None, BLK, GRP_W), lambda r, n: (r, n, 0))
    shp = jax.ShapeDtypeStruct((d, S // d, GRP_W), F32)
    if nblk > 1:
        in_specs, args = [cur(0), cur(1), prev(1), cur(2), prev(2)], (qkv_c,) * 5
    else:
        in_specs, args = [cur(0), cur(1), cur(2)], (qkv_c,) * 3
    return pl.pallas_call(
        body, name=name, grid=(d, nblk), in_specs=in_specs, out_specs=[out, out], out_shape=[shp, shp],
        compiler_params=_cp(("parallel", "parallel")),
    )(*args)


def _attn_combine(os, lses, name):
    tm = ROPE_TM

    def body(o0_ref, o1_ref, o2_ref, l0_ref, l1_ref, l2_ref, y_ref, yt_ref, l_ref, buf_ref):
        def token_order(ref, d, cols, slot):
            if d == 1:
                return ref[0, :, cols]
            for r in range(d):
                buf_ref[slot, pl.ds(r, tm // d, stride=d), :] = ref[r, :, cols]
            return buf_ref[slot]

        for ch in range(GRP_W // 128):
            cols = slice(ch * 128, (ch + 1) * 128)
            o = [token_order(ref, d, cols, k) for k, (ref, d) in enumerate(zip((o0_ref, o1_ref, o2_ref), DILATIONS))]
            ls = [token_order(ref, d, cols, 3 + k)
                  for k, (ref, d) in enumerate(zip((l0_ref, l1_ref, l2_ref), DILATIONS))]
            m = jnp.maximum(jnp.maximum(ls[0], ls[1]), ls[2])
            e = [jnp.exp(l - m) for l in ls]
            den = e[0] + e[1] + e[2]
            y = (e[0] * o[0] + e[1] * o[1] + e[2] * o[2]) / den
            y_ref[:, cols] = y
            yt_ref[cols, :] = y.T.astype(BF16)
            l_ref[:, cols] = m + jnp.log(den)

    blk = pl.BlockSpec((tm, GRP_W), lambda i: (i, 0))
    cls = [pl.BlockSpec((d, tm // d, GRP_W), lambda i: (0, i, 0)) for d in DILATIONS]
    shp = jax.ShapeDtypeStruct((S, GRP_W), F32)
    return pl.pallas_call(
        body, name=name, grid=(S // tm,), in_specs=cls + cls,
        out_specs=[blk, pl.BlockSpec((GRP_W, tm), lambda i: (0, i)), blk],
        out_shape=[shp, jax.ShapeDtypeStruct((GRP_W, S), BF16), shp],
        scratch_shapes=[pltpu.VMEM((6, tm, 128), F32)],
        compiler_params=_cp(("parallel",)),
    )(*os, *lses)


def _attn_bwd(gi, qkv_c, dy_c, y_c, lse_c, name):
    d = DILATIONS[gi]
    nblk = S // d // BLK

    def body(*refs):
        if nblk > 1:
            (q_ref, qn_ref, k_ref, kp_ref, v_ref, vp_ref, dy_ref, dyn_ref, y_ref, yn_ref, l_ref, ln_ref,
             dq_ref, dk_ref, dv_ref) = refs
            n = pl.program_id(1)
            has_prev = n != 0
            has_next = n != nblk - 1
        else:
            q_ref, k_ref, v_ref, dy_ref, y_ref, l_ref, dq_ref, dk_ref, dv_ref = refs
        qi = lax.broadcasted_iota(jnp.int32, (NH, BLK, BLK), 1)
        kj = lax.broadcasted_iota(jnp.int32, (NH, BLK, BLK), 2)

        def lse_col(ref):
            return jnp.stack([ref[:, h * DH:h * DH + 1] for h in range(NH)])

        q, k, v = _heads(q_ref), _heads(k_ref), _heads(v_ref)
        dy = _heads(dy_ref)
        dd = jnp.sum(dy * _heads(y_ref), axis=-1, keepdims=True)
        lcol = lse_col(l_ref)
        dyb = dy.astype(BF16)
        p = jnp.exp(jnp.where(kj <= qi, _bdot_nt(q, k) * 0.125, NEG_INF) - lcol)
        ds = (p * (_bdot_nt(dyb, v) - dd)).astype(BF16)
        dq = _bdot(ds, k)
        dk = _bdot_tn(ds, q)
        dv = _bdot_tn(p.astype(BF16), dyb)
        if nblk > 1:
            qn, kpv, vpv = _heads(qn_ref), _heads(kp_ref), _heads(vp_ref)
            dyn = _heads(dyn_ref)
            ddn = jnp.sum(dyn * _heads(yn_ref), axis=-1, keepdims=True)
            lncol = lse_col(ln_ref)
            dynb = dyn.astype(BF16)
            mask_p = jnp.logical_and(kj >= qi, has_prev)
            pp = jnp.exp(jnp.where(mask_p, _bdot_nt(q, kpv) * 0.125, NEG_INF) - lcol)
            dsp = (pp * (_bdot_nt(dyb, vpv) - dd)).astype(BF16)
            dq = dq + _bdot(dsp, kpv)
            mask_n = jnp.logical_and(kj >= qi, has_next)
            pn = jnp.exp(jnp.where(mask_n, _bdot_nt(qn, k) * 0.125, NEG_INF) - lncol)
            dsn = (pn * (_bdot_nt(dynb, v) - ddn)).astype(BF16)
            dk = dk + _bdot_tn(dsn, qn)
            dv = dv + _bdot_tn(pn.astype(BF16), dynb)
        dq_ref[...] = _unheads(dq) * 0.125
        dk_ref[...] = _unheads(dk) * 0.125
        dv_ref[...] = _unheads(dv)

    def spec(sec, shift):
        def idx(r, n):
            return (r, jnp.clip(n + shift, 0, nblk - 1), sec)
        return pl.BlockSpec((None, BLK, GRP_W), idx)

    if nblk > 1:
        in_specs = [spec(0, 0), spec(0, 1), spec(1, 0), spec(1, -1), spec(2, 0), spec(2, -1),
                    spec(0, 0), spec(0, 1), spec(0, 0), spec(0, 1), spec(0, 0), spec(0, 1)]
        args = (qkv_c,) * 6 + (dy_c, dy_c, y_c, y_c, lse_c, lse_c)
    else:
        in_specs = [spec(0, 0), spec(1, 0), spec(2, 0), spec(0, 0), spec(0, 0), spec(0, 0)]
        args = (qkv_c, qkv_c, qkv_c, dy_c, y_c, lse_c)
    out = spec(0, 0)
    shp = jax.ShapeDtypeStruct((d, S // d, GRP_W), F32)
    return pl.pallas_call(
        body, name=name, grid=(d, nblk), in_specs=in_specs, out_specs=[out, out, out], out_shape=[shp, shp, shp],
        compiler_params=_cp(("parallel", "parallel")),
    )(*args)


_SQRT_HALF = 0.7071067811865476
_INV_SQRT_2PI = 0.3989422804014327


def _gelu(z):
    return 0.5 * z * (1.0 + lax.erf(z * _SQRT_HALF))


def _gelu_grad(z):
    return 0.5 * (1.0 + lax.erf(z * _SQRT_HALF)) + z * (jnp.exp(-0.5 * z * z) * _INV_SQRT_2PI)


def _tril_mask():
    t = lax.broadcasted_iota(jnp.int32, (BLK, BLK), 0)
    s = lax.broadcasted_iota(jnp.int32, (BLK, BLK), 1)
    return s <= t


def _groups(t):
    return jnp.stack([t[:, g * BLK:(g + 1) * BLK] for g in range(8)])


def _ungroup(t3):
    return jnp.concatenate([t3[g] for g in range(8)], axis=1)


def _group_bias(bs_ref):
    return jnp.stack([bs_ref[:, g:g + 1] for g in range(8)])


def _gmlp_fwd(z, ln_g, ln_b, w_s, b_s_t, name):
    def body(z_ref, g_ref, b_ref, ws_ref, bs_ref, y_ref, yt_ref):
        zg = _gelu(z_ref[...])
        u = zg[:, :GW]
        xh, _ = _ln_stats(zg[:, GW:])
        vn = (xh * g_ref[...] + b_ref[...]).astype(BF16)
        wt = jnp.where(_tril_mask(), ws_ref[...], 0.0).astype(BF16)
        yv = u * _ungroup(_bdot(wt, _groups(vn)) + _group_bias(bs_ref))
        y_ref[...] = yv.astype(BF16)
        yt_ref[...] = yv.T.astype(BF16)

    vec = pl.BlockSpec((1, GW), lambda n: (0, 0))
    return pl.pallas_call(
        body, name=name, grid=(NBLK,),
        in_specs=[pl.BlockSpec((BLK, 2 * GW), lambda n: (n, 0)), vec, vec,
                  pl.BlockSpec((8, BLK, BLK), lambda n: (0, 0, 0)), pl.BlockSpec((BLK, 8), lambda n: (0, 0))],
        out_specs=[pl.BlockSpec((BLK, GW), lambda n: (n, 0)), pl.BlockSpec((GW, BLK), lambda n: (0, n))],
        out_shape=[jax.ShapeDtypeStruct((S, GW), BF16), jax.ShapeDtypeStruct((GW, S), BF16)],
        compiler_params=_cp(("parallel",)),
    )(z, ln_g, ln_b, w_s, b_s_t)


def _gmlp_bwd(z, dy, ln_g, ln_b, w_s, b_s_t, name):
    def body(z_ref, dy_ref, g_ref, b_ref, ws_ref, bs_ref, dz_ref, dws_ref, dbs_ref, dg_ref, db_ref, dvn_ref):
        n = pl.program_id(0)
        zv = z_ref[...]
        zg = _gelu(zv)
        u = zg[:, :GW]
        xh, rstd = _ln_stats(zg[:, GW:])
        vn = (xh * g_ref[...] + b_ref[...]).astype(BF16)
        tril = _tril_mask()

        @pl.when(n == 0)
        def _():
            dws_ref[...] = jnp.zeros_like(dws_ref)
            dbs_ref[...] = jnp.zeros_like(dbs_ref)
            dg_ref[...] = jnp.zeros_like(dg_ref)
            db_ref[...] = jnp.zeros_like(db_ref)

        wt = jnp.where(tril, ws_ref[...], 0.0).astype(BF16)
        vn3 = _groups(vn)
        dyv = dy_ref[...]
        mixed = _ungroup(_bdot(wt, vn3) + _group_bias(bs_ref))
        dz_ref[:, :GW] = (dyv * mixed * _gelu_grad(zv[:, :GW])).astype(BF16)
        dmix3 = _groups(dyv * u)
        dmb = dmix3.astype(BF16)
        dws_ref[...] += jnp.where(tril, _bdot_nt(dmb, vn3), 0.0)
        dbsum = jnp.sum(dmix3, axis=-1, keepdims=True)
        for gg in range(8):
            dbs_ref[:, gg:gg + 1] += dbsum[gg]
        dvn_ref[...] = _ungroup(_bdot_tn(wt, dmb))

        dvn = dvn_ref[...]
        dg_ref[...] += jnp.sum(dvn * xh, axis=0, keepdims=True)
        db_ref[...] += jnp.sum(dvn, axis=0, keepdims=True)
        dvg = _ln_dx(dvn * g_ref[...], xh, rstd)
        dz_ref[:, GW:] = (dvg * _gelu_grad(zv[:, GW:])).astype(BF16)

    vec = pl.BlockSpec((1, GW), lambda n: (0, 0))
    ws = pl.BlockSpec((8, BLK, BLK), lambda n: (0, 0, 0))
    bs = pl.BlockSpec((BLK, 8), lambda n: (0, 0))
    return pl.pallas_call(
        body, name=name, grid=(NBLK,),
        in_specs=[pl.BlockSpec((BLK, 2 * GW), lambda n: (n, 0)), pl.BlockSpec((BLK, GW), lambda n: (n, 0)),
                  vec, vec, ws, bs],
        out_specs=[pl.BlockSpec((BLK, 2 * GW), lambda n: (n, 0)), ws, bs, vec, vec],
        out_shape=[jax.ShapeDtypeStruct((S, 2 * GW), BF16), jax.ShapeDtypeStruct((8, BLK, BLK), F32),
                   jax.ShapeDtypeStruct((BLK, 8), F32), jax.ShapeDtypeStruct((1, GW), F32),
                   jax.ShapeDtypeStruct((1, GW), F32)],
        scratch_shapes=[pltpu.VMEM((BLK, GW), F32)],
        compiler_params=_cp(("arbitrary",)),
    )(z, dy, ln_g, ln_b, w_s, b_s_t)


def _merge_fwd(a, b, gl, b_gates, name):
    tm = 512

    def body(a_ref, b_ref, g0_ref, g1_ref, bg_ref, o_ref, ot_ref):
        g0 = jax.nn.sigmoid(g0_ref[...] + bg_ref[:, :D])
        g1 = jax.nn.sigmoid(g1_ref[...] + bg_ref[:, D:])
        mg = g0 * a_ref[...] + g1 * b_ref[...]
        o_ref[...] = mg.astype(BF16)
        ot_ref[...] = mg.T.astype(BF16)

    row = pl.BlockSpec((tm, D), lambda i: (i, 0))
    return pl.pallas_call(
        body, name=name, grid=(S // tm,),
        in_specs=[row, row, row, pl.BlockSpec((tm, D), lambda i: (i, 1)), pl.BlockSpec((1, 2 * D), lambda i: (0, 0))],
        out_specs=[row, pl.BlockSpec((D, tm), lambda i: (0, i))],
        out_shape=[jax.ShapeDtypeStruct((S, D), BF16), jax.ShapeDtypeStruct((D, S), BF16)],
        compiler_params=_cp(("parallel",)),
    )(a, b, gl, gl, b_gates)


def _merge_bwd(dm, a, b, gl, b_gates, name):
    tm = 512

    def body(dm_ref, a_ref, b_ref, g0_ref, g1_ref, bg_ref, da_ref, db_ref, dgl_ref, dbg_ref):
        i = pl.program_id(0)
        dmv = dm_ref[...]
        g0 = jax.nn.sigmoid(g0_ref[...] + bg_ref[:, :D])
        g1 = jax.nn.sigmoid(g1_ref[...] + bg_ref[:, D:])
        da_ref[...] = (dmv * g0).astype(BF16)
        db_ref[...] = (dmv * g1).astype(BF16)
        d0 = dmv * a_ref[...] * g0 * (1.0 - g0)
        d1 = dmv * b_ref[...] * g1 * (1.0 - g1)
        dgl_ref[:, :D] = d0.astype(BF16)
        dgl_ref[:, D:] = d1.astype(BF16)
        s0 = jnp.sum(d0, axis=0, keepdims=True)
        s1 = jnp.sum(d1, axis=0, keepdims=True)

        @pl.when(i == 0)
        def _():
            dbg_ref[:, :D] = s0
            dbg_ref[:, D:] = s1

        @pl.when(i > 0)
        def _():
            dbg_ref[:, :D] += s0
            dbg_ref[:, D:] += s1

    row = pl.BlockSpec((tm, D), lambda i: (i, 0))
    wide = pl.BlockSpec((tm, 2 * D), lambda i: (i, 0))
    bg = pl.BlockSpec((1, 2 * D), lambda i: (0, 0))
    return pl.pallas_call(
        body, name=name, grid=(S // tm,),
        in_specs=[row, row, row, row, pl.BlockSpec((tm, D), lambda i: (i, 1)), bg],
        out_specs=[row, row, wide, bg],
        out_shape=[jax.ShapeDtypeStruct((S, D), BF16), jax.ShapeDtypeStruct((S, D), BF16),
                   jax.ShapeDtypeStruct((S, 2 * D), BF16), jax.ShapeDtypeStruct((1, 2 * D), F32)],
        compiler_params=_cp(("arbitrary",)),
    )(dm, a, b, gl, gl, b_gates)


def _adam_math(w, g, m, v):
    m2 = ADAM_B1 * m + (1.0 - ADAM_B1) * g
    v2 = ADAM_B2 * v + (1.0 - ADAM_B2) * (g * g)
    m_hat = m2 / (1.0 - ADAM_B1 ** ADAM_STEP)
    v_hat = v2 / (1.0 - ADAM_B2 ** ADAM_STEP)
    delta = -ADAM_LR * (m_hat / (jnp.sqrt(v_hat) + ADAM_EPS) + ADAM_WD * w)
    return delta, m2, v2


def _pick_rows(rows, cols, unit=16, budget=2 * MIB):
    best = unit
    for t in range(unit, rows + 1, unit):
        if rows % t == 0 and t * cols * 4 <= budget:
            best = t
    assert rows % best == 0
    return best


def _adamw(w, g, m, v, name):
    r, c = w.shape
    tr = _pick_rows(r, c, unit=8)

    def body(w_ref, g_ref, m_ref, v_ref, go_ref, d_ref, mo_ref, vo_ref):
        gv = g_ref[...]
        delta, m2, v2 = _adam_math(w_ref[...], gv, m_ref[...], v_ref[...])
        go_ref[...] = gv
        d_ref[...] = delta
        mo_ref[...] = m2
        vo_ref[...] = v2

    blk = pl.BlockSpec((tr, c), lambda i: (i, 0))
    shp = jax.ShapeDtypeStruct((r, c), F32)
    return pl.pallas_call(
        body, name=name, grid=(r // tr,), in_specs=[blk] * 4, out_specs=[blk] * 4, out_shape=[shp] * 4,
        compiler_params=_cp(("parallel",)),
    )(*[pltpu.with_memory_space_constraint(t, pltpu.HBM) for t in (w, g, m, v)])


def _small_sum_adamw(parts, own, pos, w, m, v, name):
    tr = 48

    def body(pos_ref, p_ref, own_ref, w_ref, m_ref, v_ref, g_ref, d_ref, mo_ref, vo_ref):
        me = 2 * pos_ref[1] + pos_ref[0]
        gv = None
        for k in range(8):
            term = jnp.where(me == k, own_ref[...], p_ref[k])
            gv = term if gv is None else gv + term
        delta, m2, v2 = _adam_math(w_ref[...], gv, m_ref[...], v_ref[...])
        g_ref[...] = gv
        d_ref[...] = delta
        mo_ref[...] = m2
        vo_ref[...] = v2

    blk = pl.BlockSpec((tr, D), lambda i, p: (i, 0))
    shp = jax.ShapeDtypeStruct((SMALL_ROWS, D), F32)
    return pl.pallas_call(
        body, name=name,
        grid_spec=pltpu.PrefetchScalarGridSpec(
            num_scalar_prefetch=1, grid=(SMALL_ROWS // tr,),
            in_specs=[pl.BlockSpec((8, tr, D), lambda i, p: (0, i, 0)), blk, blk, blk, blk],
            out_specs=[blk] * 4),
        out_shape=[shp] * 4,
        compiler_params=_cp(("parallel",)),
    )(pos, parts, own, w, m, v)


ANY = pl.BlockSpec(memory_space=pl.ANY)


def _in_hbm(arrays):
    return [pltpu.with_memory_space_constraint(a, pltpu.HBM) for a in arrays]


def _mesh_pos():
    x, y, c = lax.axis_index("x"), lax.axis_index("y"), lax.axis_index("c")
    chips = [(1 - x, y), (x, 1 - y), (1 - x, 1 - y)]
    return x, y, c, chips


def _place_shard(w, kind, pos, name):
    r, c = w.shape
    tr = _pick_rows(r, c)

    def body(pos_ref, w_ref, o_ref):
        o_ref[...] = w_ref[...].astype(BF16)

    if kind == "stack":
        o_spec = pl.BlockSpec((None, tr, c), lambda i, p: (p[1], i, 0))
        shape = (NSH, r, c)
    else:
        o_spec = pl.BlockSpec((tr, c), lambda i, p: (i, p[1]))
        shape = (r, NSH * c)
    return pl.pallas_call(
        body, name=name,
        grid_spec=pltpu.PrefetchScalarGridSpec(
            num_scalar_prefetch=1, grid=(r // tr,),
            in_specs=[pl.BlockSpec((tr, c), lambda i, p: (i, 0))], out_specs=o_spec),
        out_shape=pltpu.HBM(shape, BF16),
        compiler_params=_cp(("parallel",)),
    )(pos, pltpu.with_memory_space_constraint(w, pltpu.HBM))


SEM = pl.BlockSpec(memory_space=pltpu.SEMAPHORE)
SPLIT_COPY = pltpu.CompilerParams(has_side_effects=pltpu.SideEffectType.DATAFLOW_SIDE_EFFECTING)


def _shard_window(ref, kind, j, h, dims):
    r, c = dims
    rows = pl.ds(pl.multiple_of(h * (r // 2), 16), r // 2)
    if kind == "stack":
        return ref.at[j, rows, :]
    return ref.at[rows, pl.ds(pl.multiple_of(j * c, 128), c)]


def _ici_copy(ref, kind, dims, j, c, sems, idx, to):
    win = _shard_window(ref, kind, j, c, dims)
    return pltpu.make_async_remote_copy(src_ref=win, dst_ref=win, send_sem=sems[0].at[idx], recv_sem=sems[1].at[idx],
                                        device_id=to, device_id_type=MESH_T)


def _gather_start(fulls, kinds, dims, after, name):
    n, na = len(fulls), len(after)

    def body(*refs):
        outs = refs[n + na:2 * n + na]
        send_sems, recv_sems, token = refs[2 * n + na:]
        x, y, c, chips = _mesh_pos()
        for a in range(n):
            for k, chip in enumerate(chips):
                _ici_copy(outs[a], kinds[a], dims[a], 2 * x + y, c, (send_sems, recv_sems), 3 * a + k,
                          (chip[0], chip[1], c)).start()
        token[...] = jnp.zeros_like(token)

    res = pl.pallas_call(
        body, name=name, in_specs=[ANY] * (n + na),
        out_specs=[ANY] * n + [SEM, SEM, pl.BlockSpec(memory_space=pltpu.VMEM)],
        out_shape=[pltpu.HBM(f.shape, BF16) for f in fulls]
        + [pltpu.SemaphoreType.DMA((3 * n,)), pltpu.SemaphoreType.DMA((3 * n,)), jax.ShapeDtypeStruct((8, 128), F32)],
        input_output_aliases={i: i for i in range(n)},
        compiler_params=SPLIT_COPY,
    )(*_in_hbm(fulls), *after)
    return res[:n], res[n], res[n + 1], res[n + 2]


def _gather_wait(fulls, send_sems, recv_sems, kinds, dims, after, name):
    n, na = len(fulls), len(after)

    def body(*refs):
        ssem, rsem = refs[n], refs[n + 1]
        outs = refs[n + 2 + na:]
        x, y, c, chips = _mesh_pos()
        for a in range(n):
            for k, chip in enumerate(chips):
                to = (chip[0], chip[1], c)
                _ici_copy(outs[a], kinds[a], dims[a], 2 * x + y, c, (ssem, rsem), 3 * a + k, to).wait_send()
                _ici_copy(outs[a], kinds[a], dims[a], 2 * chip[0] + chip[1], c, (ssem, rsem), 3 * a + k, to).wait_recv()

    return pl.pallas_call(
        body, name=name, in_specs=[ANY] * n + [SEM, SEM] + [ANY] * na, out_specs=[ANY] * n,
        out_shape=[pltpu.HBM(f.shape, BF16) for f in fulls],
        input_output_aliases={i: i for i in range(n)},
        compiler_params=SPLIT_COPY,
    )(*_in_hbm(fulls), send_sems, recv_sems, *after)


def _gather_forward(fulls, kinds, dims, name):
    n = len(fulls)

    def body(*refs):
        outs = refs[n:2 * n]
        sems = refs[2 * n:]
        x, y, c, chips = _mesh_pos()
        sib = (x, y, 1 - c)
        cps = []
        for a in range(n):
            for k, chip in enumerate(chips):
                cp = _ici_copy(outs[a], kinds[a], dims[a], 2 * chip[0] + chip[1], c, sems, 3 * a + k, sib)
                cp.start()
                cps.append(cp)
        for a in range(n):
            for k, chip in enumerate(chips):
                _ici_copy(outs[a], kinds[a], dims[a], 2 * chip[0] + chip[1], 1 - c, sems, 3 * a + k, sib).wait_recv()
        for cp in cps:
            cp.wait_send()

    return pl.pallas_call(
        body, name=name, in_specs=[ANY] * n, out_specs=[ANY] * n,
        out_shape=[pltpu.HBM(f.shape, BF16) for f in fulls],
        input_output_aliases={i: i for i in range(n)},
        scratch_shapes=[pltpu.SemaphoreType.DMA((3 * n,)), pltpu.SemaphoreType.DMA((3 * n,))],
    )(*_in_hbm(fulls))


def _pair_copy(src, land, a, x, y, c, sems):
    return pltpu.make_async_remote_copy(
        src_ref=src.at[1 - c], dst_ref=land, send_sem=sems[0].at[a], recv_sem=sems[1].at[a],
        device_id=(x, y, 1 - c), device_id_type=MESH_T)


def _pair_start(grads, lands, name):
    n = len(grads)

    def body(*refs):
        srcs, dsts = refs[2 * n:3 * n], refs[3 * n:4 * n]
        send_sems, recv_sems, token = refs[4 * n:]
        x, y, c, _ = _mesh_pos()
        for a in range(n):
            _pair_copy(srcs[a], dsts[a], a, x, y, c, (send_sems, recv_sems)).start()
        token[...] = jnp.zeros_like(token)

    res = pl.pallas_call(
        body, name=name, in_specs=[ANY] * (2 * n),
        out_specs=[ANY] * (2 * n) + [SEM, SEM, pl.BlockSpec(memory_space=pltpu.VMEM)],
        out_shape=[pltpu.HBM(g.shape, F32) for g in grads]
        + [pltpu.HBM(l.shape, F32) for l in lands]
        + [pltpu.SemaphoreType.DMA((n,)), pltpu.SemaphoreType.DMA((n,)), jax.ShapeDtypeStruct((8, 128), F32)],
        input_output_aliases={i: i for i in range(2 * n)},
        compiler_params=SPLIT_COPY,
    )(*_in_hbm(grads), *_in_hbm(lands))
    return res[:n], res[n:2 * n], res[2 * n], res[2 * n + 1], res[2 * n + 2]


def _pair_wait(grads, lands, send_sems, recv_sems, after, name):
    n, na = len(grads), len(after)

    def body(*refs):
        ssem, rsem = refs[2 * n], refs[2 * n + 1]
        outs = refs[2 * n + 2 + na:]
        x, y, c, _ = _mesh_pos()
        for a in range(n):
            cp = _pair_copy(outs[a], outs[n + a], a, x, y, c, (ssem, rsem))
            cp.wait_send()
            cp.wait_recv()

    res = pl.pallas_call(
        body, name=name, in_specs=[ANY] * (2 * n) + [SEM, SEM] + [ANY] * na, out_specs=[ANY] * (2 * n),
        out_shape=[pltpu.HBM(g.shape, F32) for g in grads]
        + [pltpu.HBM(l.shape, F32) for l in lands],
        input_output_aliases={i: i for i in range(2 * n)},
        compiler_params=SPLIT_COPY,
    )(*_in_hbm(grads), *_in_hbm(lands), send_sems, recv_sems, *after)
    return res[:n], res[n:]


def _pair_sum(g, recv, pos, name):
    _, _, rh, c = g.shape
    tr = _pick_rows(rh, c)

    def body(pos_ref, g_ref, r_ref, o_ref):
        o_ref[...] = (g_ref[...] + r_ref[...]).astype(BF16)

    return pl.pallas_call(
        body, name=name,
        grid_spec=pltpu.PrefetchScalarGridSpec(
            num_scalar_prefetch=1, grid=(3, rh // tr),
            in_specs=[pl.BlockSpec((None, None, tr, c), lambda k, r, p: (p[0], p[2 + k], r, 0)),
                      pl.BlockSpec((None, tr, c), lambda k, r, p: (p[2 + k], r, 0))],
            out_specs=pl.BlockSpec((None, tr, c), lambda k, r, p: (k, r, 0))),
        out_shape=pltpu.HBM((3, rh, c), BF16),
        compiler_params=_cp(("parallel", "parallel")),
    )(pos, *_in_hbm([g, recv]))


def _chip_copy(src, land, a, k, chip, c, sems):
    return pltpu.make_async_remote_copy(
        src_ref=src.at[k], dst_ref=land.at[k], send_sem=sems[0].at[3 * a + k],
        recv_sem=sems[1].at[3 * a + k], device_id=(chip[0], chip[1], c), device_id_type=MESH_T)


def _chip_start(psums, lands, name):
    n = len(psums)

    def body(*refs):
        srcs, dsts = refs[2 * n:3 * n], refs[3 * n:4 * n]
        send_sems, recv_sems, token = refs[4 * n:]
        x, y, c, chips = _mesh_pos()
        for a in range(n):
            for k, chip in enumerate(chips):
                _chip_copy(srcs[a], dsts[a], a, k, chip, c, (send_sems, recv_sems)).start()
        token[...] = jnp.zeros_like(token)

    res = pl.pallas_call(
        body, name=name, in_specs=[ANY] * (2 * n),
        out_specs=[ANY] * (2 * n) + [SEM, SEM, pl.BlockSpec(memory_space=pltpu.VMEM)],
        out_shape=[pltpu.HBM(p.shape, BF16) for p in psums]
        + [pltpu.HBM(l.shape, BF16) for l in lands]
        + [pltpu.SemaphoreType.DMA((3 * n,)), pltpu.SemaphoreType.DMA((3 * n,)), jax.ShapeDtypeStruct((8, 128), F32)],
        input_output_aliases={i: i for i in range(2 * n)},
        compiler_params=SPLIT_COPY,
    )(*_in_hbm(psums), *_in_hbm(lands))
    return res[:n], res[n:2 * n], res[2 * n], res[2 * n + 1], res[2 * n + 2]


def _chip_wait(psums, lands, send_sems, recv_sems, after, name):
    n, na = len(psums), len(after)

    def body(*refs):
        ssem, rsem = refs[2 * n], refs[2 * n + 1]
        outs = refs[2 * n + 2 + na:]
        srcs, dsts = outs[:n], outs[n:]
        x, y, c, chips = _mesh_pos()
        for a in range(n):
            for k, chip in enumerate(chips):
                cp = _chip_copy(srcs[a], dsts[a], a, k, chip, c, (ssem, rsem))
                cp.wait_send()
                cp.wait_recv()

    res = pl.pallas_call(
        body, name=name, in_specs=[ANY] * (2 * n) + [SEM, SEM] + [ANY] * na, out_specs=[ANY] * (2 * n),
        out_shape=[pltpu.HBM(p.shape, BF16) for p in psums]
        + [pltpu.HBM(l.shape, BF16) for l in lands],
        input_output_aliases={i: i for i in range(2 * n)},
        compiler_params=SPLIT_COPY,
    )(*_in_hbm(psums), *_in_hbm(lands), send_sems, recv_sems, *after)
    return res[n:]


def _owner_sum(g, recv_a, recv_b, pos, name):
    _, _, rh, c = g.shape
    tr = _pick_rows(rh, c)

    def body(pos_ref, g_ref, ra_ref, rb_ref, o_ref):
        acc = g_ref[...] + ra_ref[...]
        for k in range(3):
            acc = acc + rb_ref[k].astype(F32)
        o_ref[...] = acc

    return pl.pallas_call(
        body, name=name,
        grid_spec=pltpu.PrefetchScalarGridSpec(
            num_scalar_prefetch=1, grid=(rh // tr,),
            in_specs=[pl.BlockSpec((None, None, tr, c), lambda r, p: (p[0], p[1], r, 0)),
                      pl.BlockSpec((None, tr, c), lambda r, p: (p[1], r, 0)),
                      pl.BlockSpec((3, tr, c), lambda r, p: (0, r, 0))],
            out_specs=pl.BlockSpec((None, tr, c), lambda r, p: (p[0], r, 0))),
        out_shape=pltpu.HBM((2, rh, c), F32),
        compiler_params=_cp(("parallel",)),
    )(pos, *_in_hbm([g, recv_a, recv_b]))


def _sibling_allgather(halves, name):
    n = len(halves)

    def body(*refs):
        outs = refs[n:2 * n]
        send_sems, recv_sems = refs[2 * n:]
        x, y, c, _ = _mesh_pos()
        cps = []
        for a in range(n):
            cp = pltpu.make_async_remote_copy(
                src_ref=outs[a].at[c], dst_ref=outs[a].at[c], send_sem=send_sems.at[a], recv_sem=recv_sems.at[a],
                device_id=(x, y, 1 - c), device_id_type=MESH_T)
            cp.start()
            cps.append(cp)
        for a in range(n):
            cps[a].wait_send()
            pltpu.make_async_remote_copy(
                src_ref=outs[a].at[1 - c], dst_ref=outs[a].at[1 - c], send_sem=send_sems.at[a],
                recv_sem=recv_sems.at[a], device_id=(x, y, 1 - c), device_id_type=MESH_T).wait_recv()

    return pl.pallas_call(
        body, name=name, in_specs=[ANY] * n, out_specs=[ANY] * n,
        out_shape=[pltpu.HBM(h.shape, F32) for h in halves],
        input_output_aliases={i: i for i in range(n)},
        scratch_shapes=[pltpu.SemaphoreType.DMA((n,)), pltpu.SemaphoreType.DMA((n,))],
    )(*_in_hbm(halves))


def _peers(x, y, c):
    rel = [(0, 0, 1), (0, 1, 0), (0, 1, 1), (1, 0, 0), (1, 0, 1), (1, 1, 0), (1, 1, 1)]
    return [((1 - x) if dx else x, (1 - y) if dy else y, (1 - c) if dc else c) for dx, dy, dc in rel]


def _small_copy(src, land, k, peer, slot, sems):
    return pltpu.make_async_remote_copy(src_ref=src, dst_ref=land.at[slot], send_sem=sems[0].at[k],
                                        recv_sem=sems[1].at[k], device_id=peer, device_id_type=MESH_T)


def _small_start(part, land, name):
    def body(p_in, l_in, p_ref, l_ref, send_sems, recv_sems, token):
        x, y, c, _ = _mesh_pos()
        for k, peer in enumerate(_peers(x, y, c)):
            _small_copy(p_ref, l_ref, k, peer, 4 * x + 2 * y + c, (send_sems, recv_sems)).start()
        token[...] = jnp.zeros_like(token)

    return pl.pallas_call(
        body, name=name, in_specs=[ANY, ANY],
        out_specs=[ANY, ANY, SEM, SEM, pl.BlockSpec(memory_space=pltpu.VMEM)],
        out_shape=[pltpu.HBM(part.shape, F32), pltpu.HBM(land.shape, F32), pltpu.SemaphoreType.DMA((7,)),
                   pltpu.SemaphoreType.DMA((7,)), jax.ShapeDtypeStruct((8, 128), F32)],
        input_output_aliases={0: 0, 1: 1},
        compiler_params=SPLIT_COPY,
    )(*_in_hbm([part, land]))


def _small_wait(part, land, send_sems, recv_sems, after, name):
    na = len(after)

    def body(*refs):
        ssem, rsem = refs[2], refs[3]
        p_ref, l_ref = refs[4 + na:]
        x, y, c, _ = _mesh_pos()
        for k, peer in enumerate(_peers(x, y, c)):
            cp = _small_copy(p_ref, l_ref, k, peer, 4 * peer[0] + 2 * peer[1] + peer[2], (ssem, rsem))
            cp.wait_send()
            cp.wait_recv()

    return pl.pallas_call(
        body, name=name, in_specs=[ANY, ANY, SEM, SEM] + [ANY] * na, out_specs=[ANY, ANY],
        out_shape=[pltpu.HBM(part.shape, F32), pltpu.HBM(land.shape, F32)],
        input_output_aliases={0: 0, 1: 1},
        compiler_params=SPLIT_COPY,
    )(*_in_hbm([part, land]), send_sems, recv_sems, *after)


def _pack_small(ln1_g, ln1_b, gln_g, gln_b, ln2_g, ln2_b, ln3_g, ln3_b, b_gates, b_s, w_s):
    rows = [ln1_g, ln1_b, gln_g, gln_b, ln2_g, ln2_b, ln3_g, ln3_b]
    rows = [r.reshape(1, D) for r in rows] + [b_gates.reshape(2, D), b_s.reshape(1, D), jnp.zeros((5, D), F32),
                                             w_s.reshape(128, D)]
    return jnp.concatenate(rows, axis=0)


def _unpack_small(p):
    out = [p[i:i + 1] for i in range(8)]
    return out + [p[8:10].reshape(1, 2 * D), p[10:11].reshape(1, 8, BLK), p[16:144].reshape(1, 8, BLK, BLK)]


GROUPS = (("f1g", "f1u", "f1d"), ("w_in",), ("w_ab", "w_gb", "w_out"), ("f2g", "f2u", "f2d"))


def _local_step(x, pos_f, target, P, weights_of, grads_ready, flush, small_ready):
    invf = ROPE_THETA ** (-jnp.arange(0, DH, 2, dtype=F32) / DH)
    invf = jnp.tile(invf, 4).reshape(1, 128)
    b_s_t = P["gmlp_b_s"].T

    W = dict(weights_of(0, []))
    h1b, xh1, rstd1, a1, b1, h1t = _ffn_fwd(x, W["f1g"], W["f1u"], W["f1d"], P["ln1_g"], P["ln1_b"], "ffn1_fwd",
                                                emit_t=True)
    W.update(weights_of(1, [h1b]))
    qkv_c = _proj_qkv_rope(h1b, W["w_in"], pos_f, invf, "proj_qkv_rope")
    z = _matmul(h1b, W["w_in"], "nn", "proj_z", n=2 * GW, b_col0=3 * ATT_W, tm=S, tn=512)
    gl = _matmul(h1b, W["w_in"], "nn", "proj_gates", n=2 * D, b_col0=3 * ATT_W + 2 * GW, tm=S, tn=512)
    og = [_attn_fwd(gi, qkv_c[gi], "attn_fwd_g%d" % gi) for gi in range(NG)]
    y_attn, y_attn_t, lse = _attn_combine([o for o, _ in og], [l for _, l in og], "attn_combine")
    y_gmlp, y_gmlp_t = _gmlp_fwd(z, P["gmlp_ln_g"], P["gmlp_ln_b"], P["gmlp_w_s"], b_s_t, "gmlp_fwd")
    W.update(weights_of(2, [y_gmlp]))
    br_a = _matmul(y_attn, W["w_ab"], "nn", "branch_attn", n=D, tm=1024, tn=D)
    br_b = _matmul(y_gmlp, W["w_gb"], "nn", "branch_gmlp", n=D, tm=1024, tn=D)
    merged, merged_t = _merge_fwd(br_a, br_b, gl, P["b_gates"], "merge_fwd")
    mix = _matmul(merged, W["w_out"], "nn", "mix_out", n=D, tm=1024, tn=D)
    h2, h2b, xh2, rstd2 = _resid_ln(xh1, P["ln1_g"], P["ln1_b"], mix, P["ln2_g"], P["ln2_b"], "resid_ln2")
    W.update(weights_of(3, [h2b]))
    _, xh3, rstd3, a2, b2 = _ffn_fwd(h2, W["f2g"], W["f2u"], W["f2d"], P["ln3_g"], P["ln3_b"], "ffn2_fwd")

    dr3, dg3, db3, loss = _ln_bwd(P["ln3_b"], xh3, rstd3, P["ln3_g"], "loss_ln3_bwd", target=target)
    g_f2g, g_f2u, g_f2d, dh2 = _ffn_bwd(dr3, h2b, a2, b2, W["f2g"], W["f2u"], W["f2d"], "ffn2_bwd")
    tok = grads_ready(3, dict(f2g=g_f2g, f2u=g_f2u, f2d=g_f2d))
    dr2, dg2, db2 = _ln_bwd(dh2, xh2, rstd2, P["ln2_g"], "ln2_bwd", after=tok)
    g_wout = _wgrad(merged_t, dr2, 128, D, "dw_out", row_sharded=True)
    dmerged = _matmul(dr2, W["w_out"], "nt", "dmerged", n=D, tm=1024, tn=D)
    dab, dbb, dglb, dbg = _merge_bwd(dmerged, br_a, br_b, gl, P["b_gates"], "merge_bwd")
    tok = flush([dab])
    g_wab = _wgrad(y_attn_t, dab, GRP_W // 2, 256, "dw_attn_branch", row_sharded=False, after=tok)
    g_wgb = _wgrad(y_gmlp_t, dbb, 128, D, "dw_gmlp_branch", row_sharded=True)
    tok = grads_ready(2, dict(w_ab=g_wab, w_gb=g_wgb, w_out=g_wout))
    dy_attn = _matmul(dab, W["w_ab"], "nt", "dy_attn", n=GRP_W, tm=1024, tn=GRP_W, after=tok)
    dy_gmlp = _matmul(dbb, W["w_gb"], "nt", "dy_gmlp", n=GW, tm=1024, tn=GW)
    dzb, dws, dbs_t, dgln_g, dgln_b = _gmlp_bwd(z, dy_gmlp, P["gmlp_ln_g"], P["gmlp_ln_b"], P["gmlp_w_s"], b_s_t,
                                                 "gmlp_bwd")
    cls = _class_order([dy_attn, y_attn, lse], "attn_class_order")
    dqkv_c = []
    for gi in range(NG):
        dy_c, y_c, lse_c = [t[None] if gi == 0 else cls[2 * a + gi - 1] for a, t in enumerate((dy_attn, y_attn, lse))]
        dqkv_c.append(_attn_bwd(gi, qkv_c[gi], dy_c, y_c, lse_c, "attn_bwd_g%d" % gi))
    dqkvb = _rope_bwd(dqkv_c, pos_f, invf, "rope_bwd")
    dproj = jnp.concatenate([dqkvb, dzb, dglb], axis=1)
    tok = flush([dproj])
    g_win = _wgrad(h1t, dproj, D // 2, IN_SH, "dw_in", row_sharded=False, after=tok)
    tok = grads_ready(1, dict(w_in=g_win))
    dh1 = _matmul(dproj, W["w_in"], "nt", "dh1", n=D, tm=1024, tn=D, tk=IN_SH, add=dr2, add_scale=ALPHA, after=tok)
    dr1, dg1, db1 = _ln_bwd(dh1, xh1, rstd1, P["ln1_g"], "ln1_bwd")
    tok = flush([dr1])
    tok = tok + small_ready(_pack_small(dg1, db1, dgln_g, dgln_b, dg2, db2, dg3, db3, dbg, dbs_t.T, dws))
    g_f1g, g_f1u, g_f1d, dx = _ffn_bwd(dr1, x.astype(BF16), a1, b1, W["f1g"], W["f1u"], W["f1d"], "ffn1_bwd",
                                       after=tok)
    grads_ready(0, dict(f1g=g_f1g, f1u=g_f1u, f1d=g_f1d))
    flush([dx])
    return loss, dx


BIG = ("f1g", "f1u", "f1d", "w_in", "w_ab", "w_gb", "w_out", "f2g", "f2u", "f2d")
TRANSPOSED = ("f1g", "f1u", "f2g", "f2u")
KIND = dict(f1g="stack", f1u="stack", f1d="stack", w_in="col", w_ab="col", w_gb="stack", w_out="stack",
            f2g="stack", f2u="stack", f2d="stack")


def kernel(x, positions, ffn1_w_gate, ffn1_w_up, ffn1_w_down, ln1_g, ln1_b, w_in, b_gates, gmlp_ln_g, gmlp_ln_b, gmlp_w_s, gmlp_b_s, w_attn_branch, w_gmlp_branch, w_out, ln2_g, ln2_b, ffn2_w_gate, ffn2_w_up, ffn2_w_down, ln3_g, ln3_b, loss_target, m_ffn1_w_gate, m_ffn1_w_up, m_ffn1_w_down, m_ln1_g, m_ln1_b, m_w_in, m_b_gates, m_gmlp_ln_g, m_gmlp_ln_b, m_gmlp_w_s, m_gmlp_b_s, m_w_attn_branch, m_w_gmlp_branch, m_w_out, m_ln2_g, m_ln2_b, m_ffn2_w_gate, m_ffn2_w_up, m_ffn2_w_down, m_ln3_g, m_ln3_b, v_ffn1_w_gate, v_ffn1_w_up, v_ffn1_w_down, v_ln1_g, v_ln1_b, v_w_in, v_b_gates, v_gmlp_ln_g, v_gmlp_ln_b, v_gmlp_w_s, v_gmlp_b_s, v_w_attn_branch, v_w_gmlp_branch, v_w_out, v_ln2_g, v_ln2_b, v_ffn2_w_gate, v_ffn2_w_up, v_ffn2_w_down, v_ln3_g, v_ln3_b):
    cx, cy, cc = lax.axis_index("x"), lax.axis_index("y"), lax.axis_index("c")
    pos = jnp.stack([cc, 2 * cx + cy, 2 * (1 - cx) + cy, 2 * cx + 1 - cy, 2 * (1 - cx) + 1 - cy]).astype(jnp.int32)

    w_sh = dict(f1g=ffn1_w_gate, f1u=ffn1_w_up, f1d=ffn1_w_down, w_in=w_in, w_ab=w_attn_branch,
                w_gb=w_gmlp_branch, w_out=w_out, f2g=ffn2_w_gate, f2u=ffn2_w_up, f2d=ffn2_w_down)
    m_sh = dict(f1g=m_ffn1_w_gate, f1u=m_ffn1_w_up, f1d=m_ffn1_w_down, w_in=m_w_in, w_ab=m_w_attn_branch,
                w_gb=m_w_gmlp_branch, w_out=m_w_out, f2g=m_ffn2_w_gate, f2u=m_ffn2_w_up, f2d=m_ffn2_w_down)
    v_sh = dict(f1g=v_ffn1_w_gate, f1u=v_ffn1_w_up, f1d=v_ffn1_w_down, w_in=v_w_in, w_ab=v_w_attn_branch,
                w_gb=v_w_gmlp_branch, w_out=v_w_out, f2g=v_ffn2_w_gate, f2u=v_ffn2_w_up, f2d=v_ffn2_w_down)
    w_sh = {k: (v[0].T if k in TRANSPOSED else v[0]) for k, v in w_sh.items()}
    m_sh = {k: (v[0].T if k in TRANSPOSED else v[0]) for k, v in m_sh.items()}
    v_sh = {k: (v[0].T if k in TRANSPOSED else v[0]) for k, v in v_sh.items()}

    started, tokens = [], []
    for gi, names in enumerate(GROUPS):
        placed = [_place_shard(w_sh[k], KIND[k], pos, "place_" + k) for k in names]
        fulls, ssem, rsem, token = _gather_start(placed, [KIND[k] for k in names], [w_sh[k].shape for k in names],
                                                 tokens[-1:], "gather_start_g%d" % gi)
        started.append((fulls, ssem, rsem))
        tokens.append(token)

    def weights_of(gi, after):
        names = GROUPS[gi]
        kinds, dims = [KIND[k] for k in names], [w_sh[k].shape for k in names]
        fulls, ssem, rsem = started[gi]
        fulls = _gather_wait(fulls, ssem, rsem, kinds, dims, list(after) + (tokens if gi == 0 else []),
                             "gather_wait_g%d" % gi)
        fulls = _gather_forward(fulls, kinds, dims, "gather_forward_g%d" % gi)
        return {k: (f.reshape(D, D) if k in ("w_gb", "w_out") else f) for k, f in zip(names, fulls)}

    pending, inflight = [], {}

    def grads_ready(gi, gd):
        grads = [gd[k] for k in GROUPS[gi]]
        lands = [lax.empty(g.shape[1:], F32) for g in grads]
        grads, lands, ssem, rsem, token = _pair_start(grads, lands, "rs_pair_start_g%d" % gi)
        pending.append((gi, grads, lands, ssem, rsem))
        return [token]

    def flush(after):
        gi, grads, lands, ssem, rsem = pending.pop()
        names = GROUPS[gi]
        grads, recv_a = _pair_wait(grads, lands, ssem, rsem, after, "rs_pair_wait_g%d" % gi)
        psums = [_pair_sum(g, r, pos, "rs_pair_sum_" + k) for g, r, k in zip(grads, recv_a, names)]
        lands = [lax.empty((3,) + p.shape[1:], BF16) for p in psums]
        psums, lands, ssem, rsem, token = _chip_start(psums, lands, "rs_chip_start_g%d" % gi)
        inflight[gi] = (grads, recv_a, psums, lands, ssem, rsem, token)
        return [token]

    P = dict(ln1_g=ln1_g, ln1_b=ln1_b, ln2_g=ln2_g, ln2_b=ln2_b, ln3_g=ln3_g, ln3_b=ln3_b, b_gates=b_gates,
             gmlp_ln_g=gmlp_ln_g, gmlp_ln_b=gmlp_ln_b, gmlp_w_s=gmlp_w_s[0], gmlp_b_s=gmlp_b_s[0])
    pos_f = positions.reshape(S, 1).astype(F32)
    small_state = []

    def small_ready(packed):
        land = jnp.zeros((8, SMALL_ROWS, D), F32)
        packed, land, ssem, rsem, token = _small_start(packed, land, "small_start")
        small_state.append((packed, land, ssem, rsem))
        return [token]

    loss_part, dx = _local_step(x[0], pos_f, loss_target[0], P, weights_of, grads_ready, flush, small_ready)
    loss = lax.psum(loss_part[0, 0], ("x", "y", "c"))

    g_out, d_out, m_out, v_out = {}, {}, {}, {}

    def finish(gis, after, tag):
        names, halves = [], []
        for gi in gis:
            grads, recv_a, psums, lands, ssem, rsem, token = inflight[gi]
            recv_b = _chip_wait(psums, lands, ssem, rsem, after + [inflight[0][6]], "rs_chip_wait_g%d" % gi)
            halves += [_owner_sum(g, ra, rb, pos, "rs_owner_sum_" + k)
                       for g, ra, rb, k in zip(grads, recv_a, recv_b, GROUPS[gi])]
            names += GROUPS[gi]
            after = halves[-1:]
        reduced = _sibling_allgather(halves, "rs_sibling_allgather_" + tag)
        for k, gfull in zip(names, reduced):
            res = _adamw(w_sh[k], gfull.reshape(w_sh[k].shape), m_sh[k], v_sh[k], "adamw_" + k)
            after = [res[1]]
            if k in TRANSPOSED:
                res = [r.T for r in res]
            g_out[k], d_out[k], m_out[k], v_out[k] = [r[None] for r in res]
        return after

    after = finish((3, 2, 1), [], "g321")

    small, parts = _small_wait(*small_state[0], after, "small_wait")
    sp = (ln1_g, ln1_b, gmlp_ln_g, gmlp_ln_b, ln2_g, ln2_b, ln3_g, ln3_b, b_gates, gmlp_b_s, gmlp_w_s)
    sm = (m_ln1_g, m_ln1_b, m_gmlp_ln_g, m_gmlp_ln_b, m_ln2_g, m_ln2_b, m_ln3_g, m_ln3_b, m_b_gates, m_gmlp_b_s,
          m_gmlp_w_s)
    sv = (v_ln1_g, v_ln1_b, v_gmlp_ln_g, v_gmlp_ln_b, v_ln2_g, v_ln2_b, v_ln3_g, v_ln3_b, v_b_gates, v_gmlp_b_s,
          v_gmlp_w_s)
    sg, sd, smn, svn = _small_sum_adamw(parts, small, pos, _pack_small(*sp), _pack_small(*sm), _pack_small(*sv),
                                        "small_adamw")
    names = ("ln1_g", "ln1_b", "gmlp_ln_g", "gmlp_ln_b", "ln2_g", "ln2_b", "ln3_g", "ln3_b", "b_gates", "gmlp_b_s",
             "gmlp_w_s")
    for dst, packed in ((g_out, sg), (d_out, sd), (m_out, smn), (v_out, svn)):
        for nm, val in zip(names, _unpack_small(packed)):
            dst[nm] = val
    finish((0,), [sg], "g0")

    order = ("f1g", "f1u", "f1d", "ln1_g", "ln1_b", "w_in", "b_gates", "gmlp_ln_g", "gmlp_ln_b", "gmlp_w_s", "gmlp_b_s",
             "w_ab", "w_gb", "w_out", "ln2_g", "ln2_b", "f2g", "f2u", "f2d", "ln3_g", "ln3_b")
    outs = [loss, dx[None]]
    for dst in (g_out, d_out, m_out, v_out):
        outs += [dst[k] for k in order]
    return tuple(outs)
```

```python
import functools
import math

import jax
import jax.numpy as jnp
from jax import lax
from jax.experimental import pallas as pl
from jax.experimental.pallas import tpu as pltpu

F32 = jnp.float32
BF16 = jnp.bfloat16

S = 2048
D = 1024
NSH = 4
FSH = 704
ATT_W = 1536
GRP_W = 512
NG = 3
NH = 8
DH = 64
BLK = 128
NBLK = S // BLK
GW = 1024
IN_W = 8704
IN_SH = IN_W // NSH
ALPHA = 2.0 ** 0.25
LN_EPS = 1e-5
ROPE_THETA = 10000.0
DILATIONS = (1, 4, 16)
ADAM_LR, ADAM_B1, ADAM_B2, ADAM_EPS, ADAM_WD, ADAM_STEP = 0.001, 0.9, 0.999, 1e-08, 0.01, 10
SMALL_ROWS = 144
MESH_T = pl.DeviceIdType.MESH
MIB = 1024 * 1024
NEG_INF = float("-inf")


def _cp(sem, vmem_mib=48):
    return pltpu.CompilerParams(dimension_semantics=sem, vmem_limit_bytes=vmem_mib * MIB)


def _ln_stats(r):
    mu = jnp.mean(r, axis=-1, keepdims=True)
    xc = r - mu
    var = jnp.mean(xc * xc, axis=-1, keepdims=True)
    rstd = lax.rsqrt(var + LN_EPS)
    return xc * rstd, rstd


def _ln_dx(dxh, xh, rstd):
    m1 = jnp.mean(dxh, axis=-1, keepdims=True)
    m2 = jnp.mean(dxh * xh, axis=-1, keepdims=True)
    return rstd * (dxh - m1 - xh * m2)


def _dot_nt(a, b):
    return lax.dot_general(a, b, (((1,), (1,)), ((), ())), preferred_element_type=F32)


def _dot_tn(a, b):
    return lax.dot_general(a, b, (((0,), (0,)), ((), ())), preferred_element_type=F32)


def _dot(a, b):
    return jnp.dot(a, b, preferred_element_type=F32)


def _ffn_fwd(xin, wgt, wut, wd, ln_g, ln_b, name, emit_t=False, target=None):
    with_loss = target is not None
    tm = 512 if with_loss else 1024

    def body(x_ref, wg_ref, wu_ref, wd_ref, g_ref, b_ref, *rest):
        if with_loss:
            t_ref, dr_ref, a_ref, bb_ref, dg_ref, db_ref, loss_ref, acc_ref = rest
        elif emit_t:
            hb_ref, xh_ref, rstd_ref, a_ref, bb_ref, ht_ref, acc_ref = rest
        else:
            hb_ref, xh_ref, rstd_ref, a_ref, bb_ref, acc_ref = rest
        i = pl.program_id(0)
        j = pl.program_id(1)
        xb = x_ref[...].astype(BF16)
        a = _dot_nt(xb, wg_ref[...])
        b = _dot_nt(xb, wu_ref[...])
        a_ref[...] = a.astype(BF16)
        bb_ref[...] = b.astype(BF16)
        s = (a * jax.nn.sigmoid(a)) * b
        f = _dot(s.astype(BF16), wd_ref[...])

        @pl.when(j == 0)
        def _():
            acc_ref[...] = f

        @pl.when(j > 0)
        def _():
            acc_ref[...] += f

        @pl.when(j == NSH - 1)
        def _():
            r = ALPHA * x_ref[...] + 0.5 * acc_ref[...]
            xh, rstd = _ln_stats(r)
            h = xh * g_ref[...] + b_ref[...]
            if with_loss:
                err = h - t_ref[...]
                dy = err * (1.0 / D)
                dr_ref[...] = _ln_dx(dy * g_ref[...], xh, rstd)
                dg = jnp.sum(dy * xh, axis=0, keepdims=True)
                db = jnp.sum(dy, axis=0, keepdims=True)
                part = 0.5 * jnp.sum(jnp.mean(err * err, axis=-1, keepdims=True), axis=0, keepdims=True)
                part = jnp.broadcast_to(part, (8, 128))

                @pl.when(i == 0)
                def _():
                    dg_ref[...] = dg
                    db_ref[...] = db
                    loss_ref[...] = part

                @pl.when(i > 0)
                def _():
                    dg_ref[...] += dg
                    db_ref[...] += db
                    loss_ref[...] += part
            else:
                hb_ref[...] = h.astype(BF16)
                xh_ref[...] = xh
                rstd_ref[...] = rstd
                if emit_t:
                    ht_ref[...] = h.T.astype(BF16)

    row = pl.BlockSpec((tm, D), lambda i, j: (i, 0))
    vec = pl.BlockSpec((1, D), lambda i, j: (0, 0))
    wsp = pl.BlockSpec((None, FSH, D), lambda i, j: (j, 0, 0))
    ab = pl.BlockSpec((None, tm, FSH), lambda i, j: (j, i, 0))
    ab_shape = jax.ShapeDtypeStruct((NSH, S, FSH), BF16)
    in_specs, args = [row, wsp, wsp, wsp, vec, vec], (xin, wgt, wut, wd, ln_g, ln_b)
    if with_loss:
        in_specs, args = in_specs + [row], args + (target,)
        out_specs = [row, ab, ab, vec, vec, pl.BlockSpec((8, 128), lambda i, j: (0, 0))]
        out_shape = [jax.ShapeDtypeStruct((S, D), F32), ab_shape, ab_shape, jax.ShapeDtypeStruct((1, D), F32),
                     jax.ShapeDtypeStruct((1, D), F32), jax.ShapeDtypeStruct((8, 128), F32)]
    else:
        out_specs = [row, row, pl.BlockSpec((tm, 1), lambda i, j: (i, 0)), ab, ab]
        out_shape = [jax.ShapeDtypeStruct((S, D), BF16), jax.ShapeDtypeStruct((S, D), F32),
                     jax.ShapeDtypeStruct((S, 1), F32), ab_shape, ab_shape]
        if emit_t:
            out_specs.append(pl.BlockSpec((D, tm), lambda i, j: (0, i)))
            out_shape.append(jax.ShapeDtypeStruct((D, S), BF16))
    return pl.pallas_call(
        body, name=name, grid=(S // tm, NSH), in_specs=in_specs, out_specs=out_specs, out_shape=out_shape,
        scratch_shapes=[pltpu.VMEM((tm, D), F32)],
        compiler_params=_cp(("arbitrary" if with_loss else "parallel", "arbitrary"), vmem_mib=56),
    )(*args)


def _ffn_bwd(dr, xin_b, a, b, wgt, wut, wd, name, after=()):
    tm = 512
    ni = S // tm
    hr = FSH // 2

    def body(dr_ref, a_ref, b_ref, wg_ref, wu_ref, wd_ref, x_hbm, *rest):
        dwg_hbm, dwu_hbm, dwd_hbm, dx_hbm, dx_acc, da_all, db_all, s_all, df_all, x_all, res_buf, sems = rest[len(after):]
        j = pl.program_id(0)
        i = pl.program_id(1)
        rows = pl.ds(pl.multiple_of(i * tm, tm), tm)

        @pl.when(jnp.logical_and(j == 0, i == 0))
        def _():
            cp = pltpu.make_async_copy(x_hbm, x_all, sems.at[0])
            cp.start()
            cp.wait()

        drv = dr_ref[...]
        df = (0.5 * drv).astype(BF16)

        @pl.when(j == 0)
        def _():
            df_all[rows, :] = df

        ds = jnp.concatenate([_dot_nt(df, wd_ref[0:384, :]), _dot_nt(df, wd_ref[384:FSH, :])], axis=1)
        av = a_ref[...].astype(F32)
        bv = b_ref[...].astype(F32)
        sig = jax.nn.sigmoid(av)
        sl = av * sig
        da = (ds * bv * (sig * (1.0 + av * (1.0 - sig)))).astype(BF16)
        db = (ds * sl).astype(BF16)
        da_all[rows, :] = da
        db_all[rows, :] = db
        s_all[rows, :] = (sl * bv).astype(BF16)
        dx = _dot(da, wg_ref[...]) + _dot(db, wu_ref[...])

        @pl.when(j == 0)
        def _():
            dx_acc[rows, :] = ALPHA * drv + dx

        @pl.when(j > 0)
        def _():
            dx_acc[rows, :] += dx

        @pl.when(i == ni - 1)
        def _():
            copies = []
            for n, (lhs, rhs, out) in enumerate(((da_all, x_all, dwg_hbm), (db_all, x_all, dwu_hbm),
                                                 (s_all, df_all, dwd_hbm))):
                slot = n % 2
                if n >= 2:
                    for cp in copies[2 * (n - 2): 2 * (n - 2) + 2]:
                        cp.wait()
                res_buf[slot] = _dot_tn(lhs[...], rhs[...])
                for h in range(2):
                    cp = pltpu.make_async_copy(res_buf.at[slot, pl.ds(h * hr, hr), :], out.at[h, j],
                                               sems.at[1 + 2 * slot + h])
                    cp.start()
                    copies.append(cp)
            for cp in copies[2:]:
                cp.wait()

        @pl.when(jnp.logical_and(j == NSH - 1, i == ni - 1))
        def _():
            cp = pltpu.make_async_copy(dx_acc, dx_hbm, sems.at[0])
            cp.start()
            cp.wait()

    row = pl.BlockSpec((tm, D), lambda j, i: (i, 0))
    wsp = pl.BlockSpec((None, FSH, D), lambda j, i: (j, 0, 0))
    ab = pl.BlockSpec((None, tm, FSH), lambda j, i: (j, i, 0))
    dwshape = jax.ShapeDtypeStruct((2, NSH, hr, D), F32)
    return pl.pallas_call(
        body, name=name, grid=(NSH, ni),
        in_specs=[row, ab, ab, wsp, wsp, wsp, ANY] + [ANY] * len(after),
        out_specs=[ANY, ANY, ANY, ANY],
        out_shape=[dwshape, dwshape, dwshape, jax.ShapeDtypeStruct((S, D), F32)],
        scratch_shapes=[pltpu.VMEM((S, D), F32), pltpu.VMEM((S, FSH), BF16), pltpu.VMEM((S, FSH), BF16),
                        pltpu.VMEM((S, FSH), BF16), pltpu.VMEM((S, D), BF16), pltpu.VMEM((S, D), BF16),
                        pltpu.VMEM((2, FSH, D), F32), pltpu.SemaphoreType.DMA((5,))],
        compiler_params=_cp(("arbitrary", "arbitrary"), vmem_mib=58),
    )(dr, a, b, wgt, wut, wd, xin_b, *after)


def _matmul(a, b, mode, name, *, n, tm=512, tn=512, tk=None, b_col0=0, add=None, add_scale=1.0, out_dtype=F32,
            after=()):
    m, ka = a.shape
    tk = ka if tk is None else tk
    nk = ka // tk
    assert m % tm == 0 and n % tn == 0 and ka % tk == 0 and b_col0 % tn == 0
    off = b_col0 // tn
    na = len(after)

    def body(*refs):
        refs = refs[na:]
        if add is None:
            a_ref, b_ref, o_ref = refs[:3]
            add_ref = None
            rest = refs[3:]
        else:
            a_ref, b_ref, add_ref, o_ref = refs[:4]
            rest = refs[4:]
        k = pl.program_id(2)
        av = a_ref[...].astype(BF16)
        bv = b_ref[...].astype(BF16)
        p = _dot(av, bv) if mode == "nn" else _dot_nt(av, bv)

        def finish(acc):
            if add_ref is not None:
                acc = acc + add_scale * add_ref[...]
            o_ref[...] = acc.astype(out_dtype)

        if nk == 1:
            finish(p)
        else:
            acc_ref = rest[0]

            @pl.when(k == 0)
            def _():
                acc_ref[...] = p

            @pl.when(k > 0)
            def _():
                acc_ref[...] += p

            @pl.when(k == nk - 1)
            def _():
                finish(acc_ref[...])

    a_spec = pl.BlockSpec((tm, tk), lambda i, j, k: (i, k))
    if mode == "nn":
        b_spec = pl.BlockSpec((tk, tn), lambda i, j, k: (k, j + off))
    else:
        b_spec = pl.BlockSpec((tn, tk), lambda i, j, k: (j, k))
    o_spec = pl.BlockSpec((tm, tn), lambda i, j, k: (i, j))
    in_specs = [pl.BlockSpec(memory_space=pl.ANY)] * na + [a_spec, b_spec] + ([o_spec] if add is not None else [])
    args = tuple(after) + (a, b) + ((add,) if add is not None else ())
    return pl.pallas_call(
        body, name=name, grid=(m // tm, n // tn, nk),
        in_specs=in_specs, out_specs=o_spec,
        out_shape=jax.ShapeDtypeStruct((m, n), out_dtype),
        scratch_shapes=[pltpu.VMEM((tm, tn), F32)] if nk > 1 else [],
        compiler_params=_cp(("parallel", "parallel", "arbitrary")),
    )(*args)


def _wgrad(xt, y, rh, c, name, row_sharded, after=()):
    na = len(after)
    if row_sharded:
        def body(x_ref, y_ref, *rest):
            o_ref = rest[na]
            res = _dot(x_ref[...], y_ref[...].astype(BF16))
            for j in range(NSH):
                for h in range(2):
                    o_ref[h, j] = res[(2 * j + h) * rh:(2 * j + h + 1) * rh, :]

        grid = (1,)
        in_specs = [pl.BlockSpec((2 * NSH * rh, S), lambda g: (0, 0)), pl.BlockSpec((S, c), lambda g: (0, 0))]
        out_specs = pl.BlockSpec((2, NSH, rh, c), lambda g: (0, 0, 0, 0))
        sem = ("arbitrary",)
    else:
        def body(x_ref, y_ref, *rest):
            rest[na][...] = _dot(x_ref[...], y_ref[...].astype(BF16))

        grid = (2, NSH)
        in_specs = [pl.BlockSpec((rh, S), lambda h, j: (h, 0)), pl.BlockSpec((S, c), lambda h, j: (0, j))]
        out_specs = pl.BlockSpec((None, None, rh, c), lambda h, j: (h, j, 0, 0))
        sem = ("parallel", "parallel")
    return pl.pallas_call(
        body, name=name, grid=grid, in_specs=in_specs + [pl.BlockSpec(memory_space=pl.ANY)] * na, out_specs=out_specs,
        out_shape=jax.ShapeDtypeStruct((2, NSH, rh, c), F32),
        compiler_params=_cp(sem, vmem_mib=56),
    )(xt, y, *after)


def _resid_ln(res_xh, res_g, res_b, a, w, ln_g, ln_b, name):
    tm = 512

    def body(rx_ref, rg_ref, rb_ref, a_ref, w_ref, g_ref, b_ref, h_ref, hb_ref, xh_ref, rstd_ref):
        r = ALPHA * (rx_ref[...] * rg_ref[...] + rb_ref[...]) + _dot(a_ref[...], w_ref[...])
        xh, rstd = _ln_stats(r)
        h = xh * g_ref[...] + b_ref[...]
        h_ref[...] = h
        hb_ref[...] = h.astype(BF16)
        xh_ref[...] = xh
        rstd_ref[...] = rstd

    row = pl.BlockSpec((tm, D), lambda i: (i, 0))
    vec = pl.BlockSpec((1, D), lambda i: (0, 0))
    return pl.pallas_call(
        body, name=name, grid=(S // tm,),
        in_specs=[row, vec, vec, row, pl.BlockSpec((D, D), lambda i: (0, 0)), vec, vec],
        out_specs=[row, row, row, pl.BlockSpec((tm, 1), lambda i: (i, 0))],
        out_shape=[jax.ShapeDtypeStruct((S, D), F32), jax.ShapeDtypeStruct((S, D), BF16),
                   jax.ShapeDtypeStruct((S, D), F32), jax.ShapeDtypeStruct((S, 1), F32)],
        compiler_params=_cp(("parallel",)),
    )(res_xh, res_g, res_b, a, w, ln_g, ln_b)


def _dh1_ln_bwd(dproj, w_in, dr2, xh, rstd, ln_g, name, after=()):
    tm, tk = 512, IN_SH
    nk = IN_W // tk
    na = len(after)

    def body(*refs):
        a_ref, b_ref, add_ref, xh_ref, rstd_ref, g_ref, dr_ref, dg_ref, db_ref, acc_ref = refs[na:]
        i = pl.program_id(0)
        k = pl.program_id(1)
        p = _dot_nt(a_ref[...], b_ref[...])

        @pl.when(k == 0)
        def _():
            acc_ref[...] = p

        @pl.when(k > 0)
        def _():
            acc_ref[...] += p

        @pl.when(k == nk - 1)
        def _():
            dy = acc_ref[...] + ALPHA * add_ref[...]
            xhv = xh_ref[...]
            dr_ref[...] = _ln_dx(dy * g_ref[...], xhv, rstd_ref[...])
            dg = jnp.sum(dy * xhv, axis=0, keepdims=True)
            db = jnp.sum(dy, axis=0, keepdims=True)

            @pl.when(i == 0)
            def _():
                dg_ref[...] = dg
                db_ref[...] = db

            @pl.when(i > 0)
            def _():
                dg_ref[...] += dg
                db_ref[...] += db

    row = pl.BlockSpec((tm, D), lambda i, k: (i, 0))
    vec = pl.BlockSpec((1, D), lambda i, k: (0, 0))
    return pl.pallas_call(
        body, name=name, grid=(S // tm, nk),
        in_specs=[pl.BlockSpec(memory_space=pl.ANY)] * na
        + [pl.BlockSpec((tm, tk), lambda i, k: (i, k)), pl.BlockSpec((D, tk), lambda i, k: (0, k)), row, row,
           pl.BlockSpec((tm, 1), lambda i, k: (i, 0)), vec],
        out_specs=[row, vec, vec],
        out_shape=[jax.ShapeDtypeStruct((S, D), F32), jax.ShapeDtypeStruct((1, D), F32),
                   jax.ShapeDtypeStruct((1, D), F32)],
        scratch_shapes=[pltpu.VMEM((tm, D), F32)],
        compiler_params=_cp(("arbitrary", "arbitrary"), vmem_mib=56),
    )(*after, dproj, w_in, dr2, xh, rstd, ln_g)


def _ln_bwd(dout, xh, rstd, ln_g, name, after=()):
    tm = 512
    na = len(after)

    def body(*refs):
        y_ref, xh_ref, rstd_ref, g_ref, dr_ref, dg_ref, db_ref = refs[na:]
        dy = y_ref[...]
        i = pl.program_id(0)
        xh = xh_ref[...]
        dr_ref[...] = _ln_dx(dy * g_ref[...], xh, rstd_ref[...])
        dg = jnp.sum(dy * xh, axis=0, keepdims=True)
        db = jnp.sum(dy, axis=0, keepdims=True)

        @pl.when(i == 0)
        def _():
            dg_ref[...] = dg
            db_ref[...] = db

        @pl.when(i > 0)
        def _():
            dg_ref[...] += dg
            db_ref[...] += db

    row = pl.BlockSpec((tm, D), lambda i: (i, 0))
    vec = pl.BlockSpec((1, D), lambda i: (0, 0))
    return pl.pallas_call(
        body, name=name, grid=(S // tm,),
        in_specs=[pl.BlockSpec(memory_space=pl.ANY)] * na + [row, row, pl.BlockSpec((tm, 1), lambda i: (i, 0)), vec],
        out_specs=[row, vec, vec],
        out_shape=[jax.ShapeDtypeStruct((S, D), F32), jax.ShapeDtypeStruct((1, D), F32),
                   jax.ShapeDtypeStruct((1, D), F32)],
        compiler_params=_cp(("arbitrary",)),
    )(*after, dout, xh, rstd, ln_g)


ROPE_TM = 256


def _rope_tables(pos_ref, invf_ref, sign):
    ang = pos_ref[...] * invf_ref[...]
    lane = lax.broadcasted_iota(jnp.int32, ang.shape, 1)
    first = (lane % DH) < (DH // 2)
    sinv = jnp.sin(ang) * sign
    return first, jnp.cos(ang), jnp.where(first, -sinv, sinv)


def _rotate(x, first, cosf, sinf):
    return x * cosf + jnp.where(first, pltpu.roll(x, 96, 1), pltpu.roll(x, 32, 1)) * sinf


def _proj_qkv_rope(hb, w_in, pos_f, invf, name):
    tm = 2 * ROPE_TM

    def body(h_ref, w_ref, pos_ref, invf_ref, o0_ref, o1_ref, o2_ref, buf_ref):
        rot = pl.program_id(1) < 2
        first, cosf, sinf = _rope_tables(pos_ref, invf_ref, 1.0)
        cosf = jnp.where(rot, cosf, 1.0)
        sinf = jnp.where(rot, sinf, 0.0)
        acc = _dot(h_ref[...], w_ref[...])
        for gi, (d, o_ref) in enumerate(zip(DILATIONS, (o0_ref, o1_ref, o2_ref))):
            for ch in range(GRP_W // 128):
                cols = slice(ch * 128, (ch + 1) * 128)
                x = _rotate(acc[:, gi * GRP_W + ch * 128: gi * GRP_W + (ch + 1) * 128], first, cosf, sinf)
                if d == 1:
                    o_ref[0, :, cols] = x.astype(BF16)
                else:
                    buf_ref[...] = x
                    for r in range(d):
                        o_ref[r, :, cols] = buf_ref[pl.ds(r, tm // d, stride=d), :].astype(BF16)

    return pl.pallas_call(
        body, name=name, grid=(S // tm, 3),
        in_specs=[pl.BlockSpec((tm, D), lambda i, s: (i, 0)), pl.BlockSpec((D, ATT_W), lambda i, s: (0, s)),
                  pl.BlockSpec((tm, 1), lambda i, s: (i, 0)), pl.BlockSpec((1, 128), lambda i, s: (0, 0))],
        out_specs=[pl.BlockSpec((d, tm // d, GRP_W), lambda i, s: (0, i, s)) for d in DILATIONS],
        out_shape=[jax.ShapeDtypeStruct((d, S // d, 3 * GRP_W), BF16) for d in DILATIONS],
        scratch_shapes=[pltpu.VMEM((tm, 128), F32)],
        compiler_params=_cp(("parallel", "parallel")),
    )(hb, w_in, pos_f, invf)


def _rope_bwd(dqkv_c, pos_f, invf, name):
    tm = ROPE_TM

    def body(*refs):
        g_refs, (pos_ref, invf_ref, o_ref, buf_ref) = refs[:9], refs[9:]
        first, cosf, sinf = _rope_tables(pos_ref, invf_ref, -1.0)
        for sec in range(3):
            for gi, d in enumerate(DILATIONS):
                g_ref = g_refs[3 * gi + sec]
                for ch in range(GRP_W // 128):
                    cols = slice(ch * 128, (ch + 1) * 128)
                    if d == 1:
                        x = g_ref[0, :, cols]
                    else:
                        for r in range(d):
                            buf_ref[pl.ds(r, tm // d, stride=d), :] = g_ref[r, :, cols]
                        x = buf_ref[...]
                    if sec < 2:
                        x = _rotate(x, first, cosf, sinf)
                    dst = sec * ATT_W + gi * GRP_W + ch * 128
                    o_ref[:, dst:dst + 128] = x.astype(BF16)

    g_specs = [pl.BlockSpec((d, tm // d, GRP_W), lambda i: (0, i, 0)) for d in DILATIONS for _ in range(3)]
    return pl.pallas_call(
        body, name=name, grid=(S // tm,),
        in_specs=g_specs + [pl.BlockSpec((tm, 1), lambda i: (i, 0)), pl.BlockSpec((1, 128), lambda i: (0, 0))],
        out_specs=pl.BlockSpec((tm, 3 * ATT_W), lambda i: (i, 0)),
        out_shape=jax.ShapeDtypeStruct((S, 3 * ATT_W), BF16),
        scratch_shapes=[pltpu.VMEM((tm, 128), F32)],
        compiler_params=_cp(("parallel",)),
    )(*[g for grp in dqkv_c for g in grp], pos_f, invf)


def _class_order(ts, name):
    tm = ROPE_TM
    n = len(ts)

    def body(*refs):
        buf_ref = refs[3 * n]
        for a in range(n):
            for ch in range(GRP_W // 128):
                cols = slice(ch * 128, (ch + 1) * 128)
                buf_ref[...] = refs[a][:, cols]
                for b, d in enumerate(DILATIONS[1:]):
                    for r in range(d):
                        refs[n + 2 * a + b][r, :, cols] = buf_ref[pl.ds(r, tm // d, stride=d), :]

    return pl.pallas_call(
        body, name=name, grid=(S // tm,),
        in_specs=[pl.BlockSpec((tm, GRP_W), lambda i: (i, 0))] * n,
        out_specs=[pl.BlockSpec((d, tm // d, GRP_W), lambda i: (0, i, 0)) for _ in range(n) for d in DILATIONS[1:]],
        out_shape=[jax.ShapeDtypeStruct((d, S // d, GRP_W), F32) for _ in range(n) for d in DILATIONS[1:]],
        scratch_shapes=[pltpu.VMEM((tm, 128), F32)],
        compiler_params=_cp(("parallel",)),
    )(*ts)


def _own_lanes(h):
    return (lax.broadcasted_iota(jnp.int32, (1, 2 * DH), 1) // DH) == (h % 2)


def _heads(ref):
    out = []
    for h in range(NH):
        pair = ref[:, (h // 2) * 2 * DH:(h // 2 + 1) * 2 * DH]
        out.append(jnp.where(_own_lanes(h), pair, jnp.zeros_like(pair)))
    return jnp.stack(out)


def _unheads(t3):
    return jnp.concatenate([t3[2 * p] + t3[2 * p + 1] for p in range(NH // 2)], axis=1)


def _bdot_nt(a, b):
    return lax.dot_general(a, b, (((2,), (2,)), ((0,), (0,))), preferred_element_type=F32)


def _bdot(a, b):
    return lax.dot_general(a, b, (((2,), (1,)), ((0,), (0,))), preferred_element_type=F32)


def _bdot_tn(a, b):
    return lax.dot_general(a, b, (((1,), (1,)), ((0,), (0,))), preferred_element_type=F32)


def _attn_fwd(gi, qkv_c, name):
    d = DILATIONS[gi]
    nblk = S // d // BLK

    def body(*refs):
        if nblk > 1:
            q_ref, kc_ref, kp_ref, vc_ref, vp_ref, o_ref, lse_ref = refs
            has_prev = pl.program_id(1) != 0
        else:
            q_ref, kc_ref, vc_ref, o_ref, lse_ref = refs
        qi = lax.broadcasted_iota(jnp.int32, (NH, BLK, BLK), 1)
        kj = lax.broadcasted_iota(jnp.int32, (NH, BLK, BLK), 2)
        q = _heads(q_ref)
        sc = jnp.where(kj <= qi, _bdot_nt(q, _heads(kc_ref)) * 0.125, NEG_INF)
        m = jnp.max(sc, axis=-1, keepdims=True)
        if nblk > 1:
            mask_p = jnp.logical_and(kj >= qi, has_prev)
            sp = jnp.where(mask_p, _bdot_nt(q, _heads(kp_ref)) * 0.125, NEG_INF)
            m = jnp.maximum(m, jnp.max(sp, axis=-1, keepdims=True))
        pc = jnp.exp(sc - m)
        l = jnp.sum(pc, axis=-1, keepdims=True)
        o = _bdot(pc.astype(BF16), _heads(vc_ref))
        if nblk > 1:
            pp = jnp.exp(sp - m)
            l = l + jnp.sum(pp, axis=-1, keepdims=True)
            o = o + _bdot(pp.astype(BF16), _heads(vp_ref))
        o_ref[...] = _unheads(o / l)
        lse = jnp.broadcast_to(m + jnp.log(l), (NH, BLK, 2 * DH))
        lse_ref[...] = _unheads(jnp.stack([jnp.where(_own_lanes(h), lse[h], 0.0) for h in range(NH)]))

    def cur(sec):
        return pl.BlockSpec((None, BLK, GRP_W), lambda r, n: (r, n, sec))

    def prev(sec):
        return pl.BlockSpec((None, BLK, GRP_W), lambda r, n: (r, jnp.maximum(n - 1, 0), sec))

    out = pl.BlockSpec((None, BLK, GRP_W), lambda r, n: (r, n, 0))
    shp = jax.ShapeDtypeStruct((d, S // d, GRP_W), F32)
    if nblk > 1:
        in_specs, args = [cur(0), cur(1), prev(1), cur(2), prev(2)], (qkv_c,) * 5
    else:
        in_specs, args = [cur(0), cur(1), cur(2)], (qkv_c,) * 3
    return pl.pallas_call(
        body, name=name, grid=(d, nblk), in_specs=in_specs, out_specs=[out, out], out_shape=[shp, shp],
        compiler_params=_cp(("parallel", "parallel")),
    )(*args)


def _attn_combine(os, lses, name):
    tm = ROPE_TM

    def body(o0_ref, o1_ref, o2_ref, l0_ref, l1_ref, l2_ref, y_ref, yt_ref, l_ref, buf_ref):
        def token_order(ref, d, cols, slot):
            if d == 1:
                return ref[0, :, cols]
            for r in range(d):
                buf_ref[slot, pl.ds(r, tm // d, stride=d), :] = ref[r, :, cols]
            return buf_ref[slot]

        for ch in range(GRP_W // 128):
            cols = slice(ch * 128, (ch + 1) * 128)
            o = [token_order(ref, d, cols, k) for k, (ref, d) in enumerate(zip((o0_ref, o1_ref, o2_ref), DILATIONS))]
            ls = [token_order(ref, d, cols, 3 + k)
                  for k, (ref, d) in enumerate(zip((l0_ref, l1_ref, l2_ref), DILATIONS))]
            m = jnp.maximum(jnp.maximum(ls[0], ls[1]), ls[2])
            e = [jnp.exp(l - m) for l in ls]
            den = e[0] + e[1] + e[2]
            y = (e[0] * o[0] + e[1] * o[1] + e[2] * o[2]) / den
            y_ref[:, cols] = y
            yt_ref[cols, :] = y.T.astype(BF16)
            l_ref[:, cols] = m + jnp.log(den)

    blk = pl.BlockSpec((tm, GRP_W), lambda i: (i, 0))
    cls = [pl.BlockSpec((d, tm // d, GRP_W), lambda i: (0, i, 0)) for d in DILATIONS]
    shp = jax.ShapeDtypeStruct((S, GRP_W), F32)
    return pl.pallas_call(
        body, name=name, grid=(S // tm,), in_specs=cls + cls,
        out_specs=[blk, pl.BlockSpec((GRP_W, tm), lambda i: (0, i)), blk],
        out_shape=[shp, jax.ShapeDtypeStruct((GRP_W, S), BF16), shp],
        scratch_shapes=[pltpu.VMEM((6, tm, 128), F32)],
        compiler_params=_cp(("parallel",)),
    )(*os, *lses)


def _attn_bwd(gi, qkv_c, dy_c, y_c, lse_c, name):
    d = DILATIONS[gi]
    nblk = S // d // BLK

    def body(*refs):
        if nblk > 1:
            (q_ref, qn_ref, k_ref, kp_ref, v_ref, vp_ref, dy_ref, dyn_ref, y_ref, yn_ref, l_ref, ln_ref,
             dq_ref, dk_ref, dv_ref) = refs
            n = pl.program_id(1)
            has_prev = n != 0
            has_next = n != nblk - 1
        else:
            q_ref, k_ref, v_ref, dy_ref, y_ref, l_ref, dq_ref, dk_ref, dv_ref = refs
        qi = lax.broadcasted_iota(jnp.int32, (NH, BLK, BLK), 1)
        kj = lax.broadcasted_iota(jnp.int32, (NH, BLK, BLK), 2)

        def lse_col(ref):
            return jnp.stack([ref[:, h * DH:h * DH + 1] for h in range(NH)])

        q, k, v = _heads(q_ref), _heads(k_ref), _heads(v_ref)
        dy = _heads(dy_ref)
        dd = jnp.sum(dy * _heads(y_ref), axis=-1, keepdims=True)
        lcol = lse_col(l_ref)
        dyb = dy.astype(BF16)
        p = jnp.exp(jnp.where(kj <= qi, _bdot_nt(q, k) * 0.125, NEG_INF) - lcol)
        ds = (p * (_bdot_nt(dyb, v) - dd)).astype(BF16)
        dq = _bdot(ds, k)
        dk = _bdot_tn(ds, q)
        dv = _bdot_tn(p.astype(BF16), dyb)
        if nblk > 1:
            qn, kpv, vpv = _heads(qn_ref), _heads(kp_ref), _heads(vp_ref)
            dyn = _heads(dyn_ref)
            ddn = jnp.sum(dyn * _heads(yn_ref), axis=-1, keepdims=True)
            lncol = lse_col(ln_ref)
            dynb = dyn.astype(BF16)
            mask_p = jnp.logical_and(kj >= qi, has_prev)
            pp = jnp.exp(jnp.where(mask_p, _bdot_nt(q, kpv) * 0.125, NEG_INF) - lcol)
            dsp = (pp * (_bdot_nt(dyb, vpv) - dd)).astype(BF16)
            dq = dq + _bdot(dsp, kpv)
            mask_n = jnp.logical_and(kj >= qi, has_next)
            pn = jnp.exp(jnp.where(mask_n, _bdot_nt(qn, k) * 0.125, NEG_INF) - lncol)
            dsn = (pn * (_bdot_nt(dynb, v) - ddn)).astype(BF16)
            dk = dk + _bdot_tn(dsn, qn)
            dv = dv + _bdot_tn(pn.astype(BF16), dynb)
        dq_ref[...] = _unheads(dq) * 0.125
        dk_ref[...] = _unheads(dk) * 0.125
        dv_ref[...] = _unheads(dv)

    def spec(sec, shift):
        def idx(r, n):
            return (r, jnp.clip(n + shift, 0, nblk - 1), sec)
        return pl.BlockSpec((None, BLK, GRP_W), idx)

    if nblk > 1:
        in_specs = [spec(0, 0), spec(0, 1), spec(1, 0), spec(1, -1), spec(2, 0), spec(2, -1),
                    spec(0, 0), spec(0, 1), spec(0, 0), spec(0, 1), spec(0, 0), spec(0, 1)]
        args = (qkv_c,) * 6 + (dy_c, dy_c, y_c, y_c, lse_c, lse_c)
    else:
        in_specs = [spec(0, 0), spec(1, 0), spec(2, 0), spec(0, 0), spec(0, 0), spec(0, 0)]
        args = (qkv_c, qkv_c, qkv_c, dy_c, y_c, lse_c)
    out = spec(0, 0)
    shp = jax.ShapeDtypeStruct((d, S // d, GRP_W), F32)
    return pl.pallas_call(
        body, name=name, grid=(d, nblk), in_specs=in_specs, out_specs=[out, out, out], out_shape=[shp, shp, shp],
        compiler_params=_cp(("parallel", "parallel")),
    )(*args)


_SQRT_HALF = 0.7071067811865476
_INV_SQRT_2PI = 0.3989422804014327


def _gelu(z):
    return 0.5 * z * (1.0 + lax.erf(z * _SQRT_HALF))


def _gelu_grad(z):
    return 0.5 * (1.0 + lax.erf(z * _SQRT_HALF)) + z * (jnp.exp(-0.5 * z * z) * _INV_SQRT_2PI)


def _tril_mask():
    t = lax.broadcasted_iota(jnp.int32, (BLK, BLK), 0)
    s = lax.broadcasted_iota(jnp.int32, (BLK, BLK), 1)
    return s <= t


def _groups(t):
    return jnp.stack([t[:, g * BLK:(g + 1) * BLK] for g in range(8)])


def _ungroup(t3):
    return jnp.concatenate([t3[g] for g in range(8)], axis=1)


def _group_bias(bs_ref):
    return jnp.stack([bs_ref[:, g:g + 1] for g in range(8)])


def _gmlp_fwd(z, ln_g, ln_b, w_s, b_s_t, name):
    def body(z_ref, g_ref, b_ref, ws_ref, bs_ref, y_ref, yt_ref):
        zg = _gelu(z_ref[...])
        u = zg[:, :GW]
        xh, _ = _ln_stats(zg[:, GW:])
        vn = (xh * g_ref[...] + b_ref[...]).astype(BF16)
        wt = jnp.where(_tril_mask(), ws_ref[...], 0.0).astype(BF16)
        yv = u * _ungroup(_bdot(wt, _groups(vn)) + _group_bias(bs_ref))
        y_ref[...] = yv.astype(BF16)
        yt_ref[...] = yv.T.astype(BF16)

    vec = pl.BlockSpec((1, GW), lambda n: (0, 0))
    return pl.pallas_call(
        body, name=name, grid=(NBLK,),
        in_specs=[pl.BlockSpec((BLK, 2 * GW), lambda n: (n, 0)), vec, vec,
                  pl.BlockSpec((8, BLK, BLK), lambda n: (0, 0, 0)), pl.BlockSpec((BLK, 8), lambda n: (0, 0))],
        out_specs=[pl.BlockSpec((BLK, GW), lambda n: (n, 0)), pl.BlockSpec((GW, BLK), lambda n: (0, n))],
        out_shape=[jax.ShapeDtypeStruct((S, GW), BF16), jax.ShapeDtypeStruct((GW, S), BF16)],
        compiler_params=_cp(("parallel",)),
    )(z, ln_g, ln_b, w_s, b_s_t)


def _gmlp_bwd(z, dy, ln_g, ln_b, w_s, b_s_t, name):
    def body(z_ref, dy_ref, g_ref, b_ref, ws_ref, bs_ref, dz_ref, dws_ref, dbs_ref, dg_ref, db_ref, dvn_ref):
        n = pl.program_id(0)
        zv = z_ref[...]
        zg = _gelu(zv)
        u = zg[:, :GW]
        xh, rstd = _ln_stats(zg[:, GW:])
        vn = (xh * g_ref[...] + b_ref[...]).astype(BF16)
        tril = _tril_mask()

        @pl.when(n == 0)
        def _():
            dws_ref[...] = jnp.zeros_like(dws_ref)
            dbs_ref[...] = jnp.zeros_like(dbs_ref)
            dg_ref[...] = jnp.zeros_like(dg_ref)
            db_ref[...] = jnp.zeros_like(db_ref)

        wt = jnp.where(tril, ws_ref[...], 0.0).astype(BF16)
        vn3 = _groups(vn)
        dyv = dy_ref[...]
        mixed = _ungroup(_bdot(wt, vn3) + _group_bias(bs_ref))
        dz_ref[:, :GW] = (dyv * mixed * _gelu_grad(zv[:, :GW])).astype(BF16)
        dmix3 = _groups(dyv * u)
        dmb = dmix3.astype(BF16)
        dws_ref[...] += jnp.where(tril, _bdot_nt(dmb, vn3), 0.0)
        dbsum = jnp.sum(dmix3, axis=-1, keepdims=True)
        for gg in range(8):
            dbs_ref[:, gg:gg + 1] += dbsum[gg]
        dvn_ref[...] = _ungroup(_bdot_tn(wt, dmb))

        dvn = dvn_ref[...]
        dg_ref[...] += jnp.sum(dvn * xh, axis=0, keepdims=True)
        db_ref[...] += jnp.sum(dvn, axis=0, keepdims=True)
        dvg = _ln_dx(dvn * g_ref[...], xh, rstd)
        dz_ref[:, GW:] = (dvg * _gelu_grad(zv[:, GW:])).astype(BF16)

    vec = pl.BlockSpec((1, GW), lambda n: (0, 0))
    ws = pl.BlockSpec((8, BLK, BLK), lambda n: (0, 0, 0))
    bs = pl.BlockSpec((BLK, 8), lambda n: (0, 0))
    return pl.pallas_call(
        body, name=name, grid=(NBLK,),
        in_specs=[pl.BlockSpec((BLK, 2 * GW), lambda n: (n, 0)), pl.BlockSpec((BLK, GW), lambda n: (n, 0)),
                  vec, vec, ws, bs],
        out_specs=[pl.BlockSpec((BLK, 2 * GW), lambda n: (n, 0)), ws, bs, vec, vec],
        out_shape=[jax.ShapeDtypeStruct((S, 2 * GW), BF16), jax.ShapeDtypeStruct((8, BLK, BLK), F32),
                   jax.ShapeDtypeStruct((BLK, 8), F32), jax.ShapeDtypeStruct((1, GW), F32),
                   jax.ShapeDtypeStruct((1, GW), F32)],
        scratch_shapes=[pltpu.VMEM((BLK, GW), F32)],
        compiler_params=_cp(("arbitrary",)),
    )(z, dy, ln_g, ln_b, w_s, b_s_t)


def _merge_fwd(a, b, gl, b_gates, name):
    tm = 512

    def body(a_ref, b_ref, g0_ref, g1_ref, bg_ref, o_ref, ot_ref):
        g0 = jax.nn.sigmoid(g0_ref[...] + bg_ref[:, :D])
        g1 = jax.nn.sigmoid(g1_ref[...] + bg_ref[:, D:])
        mg = g0 * a_ref[...] + g1 * b_ref[...]
        o_ref[...] = mg.astype(BF16)
        ot_ref[...] = mg.T.astype(BF16)

    row = pl.BlockSpec((tm, D), lambda i: (i, 0))
    return pl.pallas_call(
        body, name=name, grid=(S // tm,),
        in_specs=[row, row, row, pl.BlockSpec((tm, D), lambda i: (i, 1)), pl.BlockSpec((1, 2 * D), lambda i: (0, 0))],
        out_specs=[row, pl.BlockSpec((D, tm), lambda i: (0, i))],
        out_shape=[jax.ShapeDtypeStruct((S, D), BF16), jax.ShapeDtypeStruct((D, S), BF16)],
        compiler_params=_cp(("parallel",)),
    )(a, b, gl, gl, b_gates)


def _merge_bwd(dm, a, b, gl, b_gates, name):
    tm = 512

    def body(dm_ref, a_ref, b_ref, g0_ref, g1_ref, bg_ref, da_ref, db_ref, dgl_ref, dbg_ref):
        i = pl.program_id(0)
        dmv = dm_ref[...]
        g0 = jax.nn.sigmoid(g0_ref[...] + bg_ref[:, :D])
        g1 = jax.nn.sigmoid(g1_ref[...] + bg_ref[:, D:])
        da_ref[...] = (dmv * g0).astype(BF16)
        db_ref[...] = (dmv * g1).astype(BF16)
        d0 = dmv * a_ref[...] * g0 * (1.0 - g0)
        d1 = dmv * b_ref[...] * g1 * (1.0 - g1)
        dgl_ref[:, :D] = d0.astype(BF16)
        dgl_ref[:, D:] = d1.astype(BF16)
        s0 = jnp.sum(d0, axis=0, keepdims=True)
        s1 = jnp.sum(d1, axis=0, keepdims=True)

        @pl.when(i == 0)
        def _():
            dbg_ref[:, :D] = s0
            dbg_ref[:, D:] = s1

        @pl.when(i > 0)
        def _():
            dbg_ref[:, :D] += s0
            dbg_ref[:, D:] += s1

    row = pl.BlockSpec((tm, D), lambda i: (i, 0))
    wide = pl.BlockSpec((tm, 2 * D), lambda i: (i, 0))
    bg = pl.BlockSpec((1, 2 * D), lambda i: (0, 0))
    return pl.pallas_call(
        body, name=name, grid=(S // tm,),
        in_specs=[row, row, row, row, pl.BlockSpec((tm, D), lambda i: (i, 1)), bg],
        out_specs=[row, row, wide, bg],
        out_shape=[jax.ShapeDtypeStruct((S, D), BF16), jax.ShapeDtypeStruct((S, D), BF16),
                   jax.ShapeDtypeStruct((S, 2 * D), BF16), jax.ShapeDtypeStruct((1, 2 * D), F32)],
        compiler_params=_cp(("arbitrary",)),
    )(dm, a, b, gl, gl, b_gates)


def _adam_math(w, g, m, v):
    m2 = ADAM_B1 * m + (1.0 - ADAM_B1) * g
    v2 = ADAM_B2 * v + (1.0 - ADAM_B2) * (g * g)
    m_hat = m2 / (1.0 - ADAM_B1 ** ADAM_STEP)
    v_hat = v2 / (1.0 - ADAM_B2 ** ADAM_STEP)
    delta = -ADAM_LR * (m_hat / (jnp.sqrt(v_hat) + ADAM_EPS) + ADAM_WD * w)
    return delta, m2, v2


def _pick_rows(rows, cols, unit=16, budget=2 * MIB):
    best = unit
    for t in range(unit, rows + 1, unit):
        if rows % t == 0 and t * cols * 4 <= budget:
            best = t
    assert rows % best == 0
    return best


def _adamw(w, g, m, v, name):
    r, c = w.shape
    tr = _pick_rows(r, c, unit=8)

    def body(w_ref, g_ref, m_ref, v_ref, go_ref, d_ref, mo_ref, vo_ref):
        gv = g_ref[...]
        delta, m2, v2 = _adam_math(w_ref[...], gv, m_ref[...], v_ref[...])
        go_ref[...] = gv
        d_ref[...] = delta
        mo_ref[...] = m2
        vo_ref[...] = v2

    blk = pl.BlockSpec((tr, c), lambda i: (i, 0))
    shp = jax.ShapeDtypeStruct((r, c), F32)
    return pl.pallas_call(
        body, name=name, grid=(r // tr,), in_specs=[blk] * 4, out_specs=[blk] * 4, out_shape=[shp] * 4,
        compiler_params=_cp(("parallel",)),
    )(*[pltpu.with_memory_space_constraint(t, pltpu.HBM) for t in (w, g, m, v)])


def _small_sum_adamw(parts, own, pos, w, m, v, name):
    tr = 48

    def body(pos_ref, p_ref, own_ref, w_ref, m_ref, v_ref, g_ref, d_ref, mo_ref, vo_ref):
        me = 2 * pos_ref[1] + pos_ref[0]
        gv = None
        for k in range(8):
            term = jnp.where(me == k, own_ref[...], p_ref[k])
            gv = term if gv is None else gv + term
        delta, m2, v2 = _adam_math(w_ref[...], gv, m_ref[...], v_ref[...])
        g_ref[...] = gv
        d_ref[...] = delta
        mo_ref[...] = m2
        vo_ref[...] = v2

    blk = pl.BlockSpec((tr, D), lambda i, p: (i, 0))
    shp = jax.ShapeDtypeStruct((SMALL_ROWS, D), F32)
    return pl.pallas_call(
        body, name=name,
        grid_spec=pltpu.PrefetchScalarGridSpec(
            num_scalar_prefetch=1, grid=(SMALL_ROWS // tr,),
            in_specs=[pl.BlockSpec((8, tr, D), lambda i, p: (0, i, 0)), blk, blk, blk, blk],
            out_specs=[blk] * 4),
        out_shape=[shp] * 4,
        compiler_params=_cp(("parallel",)),
    )(pos, parts, own, w, m, v)


ANY = pl.BlockSpec(memory_space=pl.ANY)


def _in_hbm(arrays):
    return [pltpu.with_memory_space_constraint(a, pltpu.HBM) for a in arrays]


def _mesh_pos():
    x, y, c = lax.axis_index("x"), lax.axis_index("y"), lax.axis_index("c")
    chips = [(1 - x, y), (x, 1 - y), (1 - x, 1 - y)]
    return x, y, c, chips


def _place_shard(w, kind, pos, name):
    r, c = w.shape
    tr = _pick_rows(r, c)

    def body(pos_ref, w_ref, o_ref):
        o_ref[...] = w_ref[...].astype(BF16)

    if kind == "stack":
        o_spec = pl.BlockSpec((None, tr, c), lambda i, p: (p[1], i, 0))
        shape = (NSH, r, c)
    else:
        o_spec = pl.BlockSpec((tr, c), lambda i, p: (i, p[1]))
        shape = (r, NSH * c)
    return pl.pallas_call(
        body, name=name,
        grid_spec=pltpu.PrefetchScalarGridSpec(
            num_scalar_prefetch=1, grid=(r // tr,),
            in_specs=[pl.BlockSpec((tr, c), lambda i, p: (i, 0))], out_specs=o_spec),
        out_shape=pltpu.HBM(shape, BF16),
        compiler_params=_cp(("parallel",)),
    )(pos, pltpu.with_memory_space_constraint(w, pltpu.HBM))


SEM = pl.BlockSpec(memory_space=pltpu.SEMAPHORE)
SPLIT_COPY = pltpu.CompilerParams(has_side_effects=pltpu.SideEffectType.DATAFLOW_SIDE_EFFECTING)


def _shard_window(ref, kind, j, h, dims):
    r, c = dims
    rows = pl.ds(pl.multiple_of(h * (r // 2), 16), r // 2)
    if kind == "stack":
        return ref.at[j, rows, :]
    return ref.at[rows, pl.ds(pl.multiple_of(j * c, 128), c)]


def _ici_copy(ref, kind, dims, j, c, sems, idx, to):
    win = _shard_window(ref, kind, j, c, dims)
    return pltpu.make_async_remote_copy(src_ref=win, dst_ref=win, send_sem=sems[0].at[idx], recv_sem=sems[1].at[idx],
                                        device_id=to, device_id_type=MESH_T)


def _gather_start(fulls, kinds, dims, after, name):
    n, na = len(fulls), len(after)

    def body(*refs):
        outs = refs[n + na:2 * n + na]
        send_sems, recv_sems, token = refs[2 * n + na:]
        x, y, c, chips = _mesh_pos()
        for a in range(n):
            for k, chip in enumerate(chips):
                _ici_copy(outs[a], kinds[a], dims[a], 2 * x + y, c, (send_sems, recv_sems), 3 * a + k,
                          (chip[0], chip[1], c)).start()
        token[...] = jnp.zeros_like(token)

    res = pl.pallas_call(
        body, name=name, in_specs=[ANY] * (n + na),
        out_specs=[ANY] * n + [SEM, SEM, pl.BlockSpec(memory_space=pltpu.VMEM)],
        out_shape=[pltpu.HBM(f.shape, BF16) for f in fulls]
        + [pltpu.SemaphoreType.DMA((3 * n,)), pltpu.SemaphoreType.DMA((3 * n,)), jax.ShapeDtypeStruct((8, 128), F32)],
        input_output_aliases={i: i for i in range(n)},
        compiler_params=SPLIT_COPY,
    )(*_in_hbm(fulls), *after)
    return res[:n], res[n], res[n + 1], res[n + 2]


def _gather_wait(fulls, send_sems, recv_sems, kinds, dims, after, name):
    n, na = len(fulls), len(after)

    def body(*refs):
        ssem, rsem = refs[n], refs[n + 1]
        outs = refs[n + 2 + na:]
        x, y, c, chips = _mesh_pos()
        for a in range(n):
            for k, chip in enumerate(chips):
                to = (chip[0], chip[1], c)
                _ici_copy(outs[a], kinds[a], dims[a], 2 * x + y, c, (ssem, rsem), 3 * a + k, to).wait_send()
                _ici_copy(outs[a], kinds[a], dims[a], 2 * chip[0] + chip[1], c, (ssem, rsem), 3 * a + k, to).wait_recv()

    return pl.pallas_call(
        body, name=name, in_specs=[ANY] * n + [SEM, SEM] + [ANY] * na, out_specs=[ANY] * n,
        out_shape=[pltpu.HBM(f.shape, BF16) for f in fulls],
        input_output_aliases={i: i for i in range(n)},
        compiler_params=SPLIT_COPY,
    )(*_in_hbm(fulls), send_sems, recv_sems, *after)


def _gather_forward(fulls, kinds, dims, name):
    n = len(fulls)

    def body(*refs):
        outs = refs[n:2 * n]
        sems = refs[2 * n:]
        x, y, c, chips = _mesh_pos()
        sib = (x, y, 1 - c)
        cps = []
        for a in range(n):
            for k, chip in enumerate(chips):
                cp = _ici_copy(outs[a], kinds[a], dims[a], 2 * chip[0] + chip[1], c, sems, 3 * a + k, sib)
                cp.start()
                cps.append(cp)
        for a in range(n):
            for k, chip in enumerate(chips):
                _ici_copy(outs[a], kinds[a], dims[a], 2 * chip[0] + chip[1], 1 - c, sems, 3 * a + k, sib).wait_recv()
        for cp in cps:
            cp.wait_send()

    return pl.pallas_call(
        body, name=name, in_specs=[ANY] * n, out_specs=[ANY] * n,
        out_shape=[pltpu.HBM(f.shape, BF16) for f in fulls],
        input_output_aliases={i: i for i in range(n)},
        scratch_shapes=[pltpu.SemaphoreType.DMA((3 * n,)), pltpu.SemaphoreType.DMA((3 * n,))],
    )(*_in_hbm(fulls))


def _pair_copy(src, land, a, x, y, c, sems):
    return pltpu.make_async_remote_copy(
        src_ref=src.at[1 - c], dst_ref=land, send_sem=sems[0].at[a], recv_sem=sems[1].at[a],
        device_id=(x, y, 1 - c), device_id_type=MESH_T)


def _pair_start(grads, lands, name):
    n = len(grads)

    def body(*refs):
        srcs, dsts = refs[2 * n:3 * n], refs[3 * n:4 * n]
        send_sems, recv_sems, token = refs[4 * n:]
        x, y, c, _ = _mesh_pos()
        for a in range(n):
            _pair_copy(srcs[a], dsts[a], a, x, y, c, (send_sems, recv_sems)).start()
        token[...] = jnp.zeros_like(token)

    res = pl.pallas_call(
        body, name=name, in_specs=[ANY] * (2 * n),
        out_specs=[ANY] * (2 * n) + [SEM, SEM, pl.BlockSpec(memory_space=pltpu.VMEM)],
        out_shape=[pltpu.HBM(g.shape, F32) for g in grads]
        + [pltpu.HBM(l.shape, F32) for l in lands]
        + [pltpu.SemaphoreType.DMA((n,)), pltpu.SemaphoreType.DMA((n,)), jax.ShapeDtypeStruct((8, 128), F32)],
        input_output_aliases={i: i for i in range(2 * n)},
        compiler_params=SPLIT_COPY,
    )(*_in_hbm(grads), *_in_hbm(lands))
    return res[:n], res[n:2 * n], res[2 * n], res[2 * n + 1], res[2 * n + 2]


def _pair_wait(grads, lands, send_sems, recv_sems, after, name):
    n, na = len(grads), len(after)

    def body(*refs):
        ssem, rsem = refs[2 * n], refs[2 * n + 1]
        outs = refs[2 * n + 2 + na:]
        x, y, c, _ = _mesh_pos()
        for a in range(n):
            cp = _pair_copy(outs[a], outs[n + a], a, x, y, c, (ssem, rsem))
            cp.wait_send()
            cp.wait_recv()

    res = pl.pallas_call(
        body, name=name, in_specs=[ANY] * (2 * n) + [SEM, SEM] + [ANY] * na, out_specs=[ANY] * (2 * n),
        out_shape=[pltpu.HBM(g.shape, F32) for g in grads]
        + [pltpu.HBM(l.shape, F32) for l in lands],
        input_output_aliases={i: i for i in range(2 * n)},
        compiler_params=SPLIT_COPY,
    )(*_in_hbm(grads), *_in_hbm(lands), send_sems, recv_sems, *after)
    return res[:n], res[n:]


def _pair_sum(g, recv, pos, name):
    _, _, rh, c = g.shape
    tr = _pick_rows(rh, c)

    def body(pos_ref, g_ref, r_ref, o_ref):
        o_ref[...] = (g_ref[...] + r_ref[...]).astype(BF16)

    return pl.pallas_call(
        body, name=name,
        grid_spec=pltpu.PrefetchScalarGridSpec(
            num_scalar_prefetch=1, grid=(3, rh // tr),
            in_specs=[pl.BlockSpec((None, None, tr, c), lambda k, r, p: (p[0], p[2 + k], r, 0)),
                      pl.BlockSpec((None, tr, c), lambda k, r, p: (p[2 + k], r, 0))],
            out_specs=pl.BlockSpec((None, tr, c), lambda k, r, p: (k, r, 0))),
        out_shape=pltpu.HBM((3, rh, c), BF16),
        compiler_params=_cp(("parallel", "parallel")),
    )(pos, *_in_hbm([g, recv]))


def _chip_copy(src, land, a, k, chip, c, sems):
    return pltpu.make_async_remote_copy(
        src_ref=src.at[k], dst_ref=land.at[k], send_sem=sems[0].at[3 * a + k],
        recv_sem=sems[1].at[3 * a + k], device_id=(chip[0], chip[1], c), device_id_type=MESH_T)


def _chip_start(psums, lands, name):
    n = len(psums)

    def body(*refs):
        srcs, dsts = refs[2 * n:3 * n], refs[3 * n:4 * n]
        send_sems, recv_sems, token = refs[4 * n:]
        x, y, c, chips = _mesh_pos()
        for a in range(n):
            for k, chip in enumerate(chips):
                _chip_copy(srcs[a], dsts[a], a, k, chip, c, (send_sems, recv_sems)).start()
        token[...] = jnp.zeros_like(token)

    res = pl.pallas_call(
        body, name=name, in_specs=[ANY] * (2 * n),
        out_specs=[ANY] * (2 * n) + [SEM, SEM, pl.BlockSpec(memory_space=pltpu.VMEM)],
        out_shape=[pltpu.HBM(p.shape, BF16) for p in psums]
        + [pltpu.HBM(l.shape, BF16) for l in lands]
        + [pltpu.SemaphoreType.DMA((3 * n,)), pltpu.SemaphoreType.DMA((3 * n,)), jax.ShapeDtypeStruct((8, 128), F32)],
        input_output_aliases={i: i for i in range(2 * n)},
        compiler_params=SPLIT_COPY,
    )(*_in_hbm(psums), *_in_hbm(lands))
    return res[:n], res[n:2 * n], res[2 * n], res[2 * n + 1], res[2 * n + 2]


def _chip_wait(psums, lands, send_sems, recv_sems, after, name):
    n, na = len(psums), len(after)

    def body(*refs):
        ssem, rsem = refs[2 * n], refs[2 * n + 1]
        outs = refs[2 * n + 2 + na:]
        srcs, dsts = outs[:n], outs[n:]
        x, y, c, chips = _mesh_pos()
        for a in range(n):
            for k, chip in enumerate(chips):
                cp = _chip_copy(srcs[a], dsts[a], a, k, chip, c, (ssem, rsem))
                cp.wait_send()
                cp.wait_recv()

    res = pl.pallas_call(
        body, name=name, in_specs=[ANY] * (2 * n) + [SEM, SEM] + [ANY] * na, out_specs=[ANY] * (2 * n),
        out_shape=[pltpu.HBM(p.shape, BF16) for p in psums]
        + [pltpu.HBM(l.shape, BF16) for l in lands],
        input_output_aliases={i: i for i in range(2 * n)},
        compiler_params=SPLIT_COPY,
    )(*_in_hbm(psums), *_in_hbm(lands), send_sems, recv_sems, *after)
    return res[n:]


def _owner_sum(g, recv_a, recv_b, pos, name):
    _, _, rh, c = g.shape
    tr = _pick_rows(rh, c)

    def body(pos_ref, g_ref, ra_ref, rb_ref, o_ref):
        acc = g_ref[...] + ra_ref[...]
        for k in range(3):
            acc = acc + rb_ref[k].astype(F32)
        o_ref[...] = acc

    return pl.pallas_call(
        body, name=name,
        grid_spec=pltpu.PrefetchScalarGridSpec(
            num_scalar_prefetch=1, grid=(rh // tr,),
            in_specs=[pl.BlockSpec((None, None, tr, c), lambda r, p: (p[0], p[1], r, 0)),
                      pl.BlockSpec((None, tr, c), lambda r, p: (p[1], r, 0)),
                      pl.BlockSpec((3, tr, c), lambda r, p: (0, r, 0))],
            out_specs=pl.BlockSpec((None, tr, c), lambda r, p: (p[0], r, 0))),
        out_shape=pltpu.HBM((2, rh, c), F32),
        compiler_params=_cp(("parallel",)),
    )(pos, *_in_hbm([g, recv_a, recv_b]))


def _sibling_allgather(halves, name):
    n = len(halves)

    def body(*refs):
        outs = refs[n:2 * n]
        send_sems, recv_sems = refs[2 * n:]
        x, y, c, _ = _mesh_pos()
        cps = []
        for a in range(n):
            cp = pltpu.make_async_remote_copy(
                src_ref=outs[a].at[c], dst_ref=outs[a].at[c], send_sem=send_sems.at[a], recv_sem=recv_sems.at[a],
                device_id=(x, y, 1 - c), device_id_type=MESH_T)
            cp.start()
            cps.append(cp)
        for a in range(n):
            cps[a].wait_send()
            pltpu.make_async_remote_copy(
                src_ref=outs[a].at[1 - c], dst_ref=outs[a].at[1 - c], send_sem=send_sems.at[a],
                recv_sem=recv_sems.at[a], device_id=(x, y, 1 - c), device_id_type=MESH_T).wait_recv()

    return pl.pallas_call(
        body, name=name, in_specs=[ANY] * n, out_specs=[ANY] * n,
        out_shape=[pltpu.HBM(h.shape, F32) for h in halves],
        input_output_aliases={i: i for i in range(n)},
        scratch_shapes=[pltpu.SemaphoreType.DMA((n,)), pltpu.SemaphoreType.DMA((n,))],
    )(*_in_hbm(halves))


def _peers(x, y, c):
    rel = [(0, 0, 1), (0, 1, 0), (0, 1, 1), (1, 0, 0), (1, 0, 1), (1, 1, 0), (1, 1, 1)]
    return [((1 - x) if dx else x, (1 - y) if dy else y, (1 - c) if dc else c) for dx, dy, dc in rel]


def _small_copy(src, land, k, peer, slot, sems):
    return pltpu.make_async_remote_copy(src_ref=src, dst_ref=land.at[slot], send_sem=sems[0].at[k],
                                        recv_sem=sems[1].at[k], device_id=peer, device_id_type=MESH_T)


def _small_start(part, land, name):
    def body(p_in, l_in, p_ref, l_ref, send_sems, recv_sems, token):
        x, y, c, _ = _mesh_pos()
        for k, peer in enumerate(_peers(x, y, c)):
            _small_copy(p_ref, l_ref, k, peer, 4 * x + 2 * y + c, (send_sems, recv_sems)).start()
        token[...] = jnp.zeros_like(token)

    return pl.pallas_call(
        body, name=name, in_specs=[ANY, ANY],
        out_specs=[ANY, ANY, SEM, SEM, pl.BlockSpec(memory_space=pltpu.VMEM)],
        out_shape=[pltpu.HBM(part.shape, F32), pltpu.HBM(land.shape, F32), pltpu.SemaphoreType.DMA((7,)),
                   pltpu.SemaphoreType.DMA((7,)), jax.ShapeDtypeStruct((8, 128), F32)],
        input_output_aliases={0: 0, 1: 1},
        compiler_params=SPLIT_COPY,
    )(*_in_hbm([part, land]))


def _small_wait(part, land, send_sems, recv_sems, after, name):
    na = len(after)

    def body(*refs):
        ssem, rsem = refs[2], refs[3]
        p_ref, l_ref = refs[4 + na:]
        x, y, c, _ = _mesh_pos()
        for k, peer in enumerate(_peers(x, y, c)):
            cp = _small_copy(p_ref, l_ref, k, peer, 4 * peer[0] + 2 * peer[1] + peer[2], (ssem, rsem))
            cp.wait_send()
            cp.wait_recv()

    return pl.pallas_call(
        body, name=name, in_specs=[ANY, ANY, SEM, SEM] + [ANY] * na, out_specs=[ANY, ANY],
        out_shape=[pltpu.HBM(part.shape, F32), pltpu.HBM(land.shape, F32)],
        input_output_aliases={0: 0, 1: 1},
        compiler_params=SPLIT_COPY,
    )(*_in_hbm([part, land]), send_sems, recv_sems, *after)


def _pack_small(ln1_g, ln1_b, gln_g, gln_b, ln2_g, ln2_b, ln3_g, ln3_b, b_gates, b_s, w_s):
    rows = [ln1_g, ln1_b, gln_g, gln_b, ln2_g, ln2_b, ln3_g, ln3_b]
    rows = [r.reshape(1, D) for r in rows] + [b_gates.reshape(2, D), b_s.reshape(1, D), jnp.zeros((5, D), F32),
                                             w_s.reshape(128, D)]
    return jnp.concatenate(rows, axis=0)


def _unpack_small(p):
    out = [p[i:i + 1] for i in range(8)]
    return out + [p[8:10].reshape(1, 2 * D), p[10:11].reshape(1, 8, BLK), p[16:144].reshape(1, 8, BLK, BLK)]


GROUPS = (("f1g", "f1u", "f1d"), ("w_in",), ("w_ab", "w_gb", "w_out"), ("f2g", "f2u", "f2d"))


def _local_step(x, pos_f, target, P, weights_of, grads_ready, flush, small_ready):
    invf = ROPE_THETA ** (-jnp.arange(0, DH, 2, dtype=F32) / DH)
    invf = jnp.tile(invf, 4).reshape(1, 128)
    b_s_t = P["gmlp_b_s"].T

    W = dict(weights_of(0, []))
    h1b, xh1, rstd1, a1, b1, h1t = _ffn_fwd(x, W["f1g"], W["f1u"], W["f1d"], P["ln1_g"], P["ln1_b"], "ffn1_fwd",
                                                emit_t=True)
    W.update(weights_of(1, [h1b]))
    qkv_c = _proj_qkv_rope(h1b, W["w_in"], pos_f, invf, "proj_qkv_rope")
    z = _matmul(h1b, W["w_in"], "nn", "proj_z", n=2 * GW, b_col0=3 * ATT_W, tm=S, tn=512)
    gl = _matmul(h1b, W["w_in"], "nn", "proj_gates", n=2 * D, b_col0=3 * ATT_W + 2 * GW, tm=S, tn=512)
    og = [_attn_fwd(gi, qkv_c[gi], "attn_fwd_g%d" % gi) for gi in range(NG)]
    y_attn, y_attn_t, lse = _attn_combine([o for o, _ in og], [l for _, l in og], "attn_combine")
    y_gmlp, y_gmlp_t = _gmlp_fwd(z, P["gmlp_ln_g"], P["gmlp_ln_b"], P["gmlp_w_s"], b_s_t, "gmlp_fwd")
    W.update(weights_of(2, [y_gmlp]))
    br_a = _matmul(y_attn, W["w_ab"], "nn", "branch_attn", n=D, tm=1024, tn=D)
    br_b = _matmul(y_gmlp, W["w_gb"], "nn", "branch_gmlp", n=D, tm=1024, tn=D)
    merged, merged_t = _merge_fwd(br_a, br_b, gl, P["b_gates"], "merge_fwd")
    h2, h2b, xh2, rstd2 = _resid_ln(xh1, P["ln1_g"], P["ln1_b"], merged, W["w_out"], P["ln2_g"], P["ln2_b"],
                                    "mix_resid_ln2")
    W.update(weights_of(3, [h2b]))
    dr3, a2, b2, dg3, db3, loss = _ffn_fwd(h2, W["f2g"], W["f2u"], W["f2d"], P["ln3_g"], P["ln3_b"],
                                           "ffn2_fwd_loss", target=target)

    g_f2g, g_f2u, g_f2d, dh2 = _ffn_bwd(dr3, h2b, a2, b2, W["f2g"], W["f2u"], W["f2d"], "ffn2_bwd")
    tok = grads_ready(3, dict(f2g=g_f2g, f2u=g_f2u, f2d=g_f2d))
    dr2, dg2, db2 = _ln_bwd(dh2, xh2, rstd2, P["ln2_g"], "ln2_bwd", after=tok)
    g_wout = _wgrad(merged_t, dr2, 128, D, "dw_out", row_sharded=True)
    dmerged = _matmul(dr2, W["w_out"], "nt", "dmerged", n=D, tm=1024, tn=D)
    dab, dbb, dglb, dbg = _merge_bwd(dmerged, br_a, br_b, gl, P["b_gates"], "merge_bwd")
    tok = flush([dab])
    g_wab = _wgrad(y_attn_t, dab, GRP_W // 2, 256, "dw_attn_branch", row_sharded=False, after=tok)
    g_wgb = _wgrad(y_gmlp_t, dbb, 128, D, "dw_gmlp_branch", row_sharded=True)
    tok = grads_ready(2, dict(w_ab=g_wab, w_gb=g_wgb, w_out=g_wout))
    dy_attn = _matmul(dab, W["w_ab"], "nt", "dy_attn", n=GRP_W, tm=1024, tn=GRP_W, after=tok)
    dy_gmlp = _matmul(dbb, W["w_gb"], "nt", "dy_gmlp", n=GW, tm=1024, tn=GW)
    dzb, dws, dbs_t, dgln_g, dgln_b = _gmlp_bwd(z, dy_gmlp, P["gmlp_ln_g"], P["gmlp_ln_b"], P["gmlp_w_s"], b_s_t,
                                                 "gmlp_bwd")
    cls = _class_order([dy_attn, y_attn, lse], "attn_class_order")
    dqkv_c = []
    for gi in range(NG):
        dy_c, y_c, lse_c = [t[None] if gi == 0 else cls[2 * a + gi - 1] for a, t in enumerate((dy_attn, y_attn, lse))]
        dqkv_c.append(_attn_bwd(gi, qkv_c[gi], dy_c, y_c, lse_c, "attn_bwd_g%d" % gi))
    dqkvb = _rope_bwd(dqkv_c, pos_f, invf, "rope_bwd")
    dproj = jnp.concatenate([dqkvb, dzb, dglb], axis=1)
    tok = flush([dproj])
    g_win = _wgrad(h1t, dproj, D // 2, IN_SH, "dw_in", row_sharded=False, after=tok)
    tok = grads_ready(1, dict(w_in=g_win))
    dr1, dg1, db1 = _dh1_ln_bwd(dproj, W["w_in"], dr2, xh1, rstd1, P["ln1_g"], "dh1_ln1_bwd", after=tok)
    tok = flush([dr1])
    tok = tok + small_ready(_pack_small(dg1, db1, dgln_g, dgln_b, dg2, db2, dg3, db3, dbg, dbs_t.T, dws))
    g_f1g, g_f1u, g_f1d, dx = _ffn_bwd(dr1, x.astype(BF16), a1, b1, W["f1g"], W["f1u"], W["f1d"], "ffn1_bwd",
                                       after=tok)
    grads_ready(0, dict(f1g=g_f1g, f1u=g_f1u, f1d=g_f1d))
    flush([dx])
    return loss, dx


BIG = ("f1g", "f1u", "f1d", "w_in", "w_ab", "w_gb", "w_out", "f2g", "f2u", "f2d")
TRANSPOSED = ("f1g", "f1u", "f2g", "f2u")
KIND = dict(f1g="stack", f1u="stack", f1d="stack", w_in="col", w_ab="col", w_gb="stack", w_out="stack",
            f2g="stack", f2u="stack", f2d="stack")


def kernel(x, positions, ffn1_w_gate, ffn1_w_up, ffn1_w_down, ln1_g, ln1_b, w_in, b_gates, gmlp_ln_g, gmlp_ln_b, gmlp_w_s, gmlp_b_s, w_attn_branch, w_gmlp_branch, w_out, ln2_g, ln2_b, ffn2_w_gate, ffn2_w_up, ffn2_w_down, ln3_g, ln3_b, loss_target, m_ffn1_w_gate, m_ffn1_w_up, m_ffn1_w_down, m_ln1_g, m_ln1_b, m_w_in, m_b_gates, m_gmlp_ln_g, m_gmlp_ln_b, m_gmlp_w_s, m_gmlp_b_s, m_w_attn_branch, m_w_gmlp_branch, m_w_out, m_ln2_g, m_ln2_b, m_ffn2_w_gate, m_ffn2_w_up, m_ffn2_w_down, m_ln3_g, m_ln3_b, v_ffn1_w_gate, v_ffn1_w_up, v_ffn1_w_down, v_ln1_g, v_ln1_b, v_w_in, v_b_gates, v_gmlp_ln_g, v_gmlp_ln_b, v_gmlp_w_s, v_gmlp_b_s, v_w_attn_branch, v_w_gmlp_branch, v_w_out, v_ln2_g, v_ln2_b, v_ffn2_w_gate, v_ffn2_w_up, v_ffn2_w_down, v_ln3_g, v_ln3_b):
    cx, cy, cc = lax.axis_index("x"), lax.axis_index("y"), lax.axis_index("c")
    pos = jnp.stack([cc, 2 * cx + cy, 2 * (1 - cx) + cy, 2 * cx + 1 - cy, 2 * (1 - cx) + 1 - cy]).astype(jnp.int32)

    w_sh = dict(f1g=ffn1_w_gate, f1u=ffn1_w_up, f1d=ffn1_w_down, w_in=w_in, w_ab=w_attn_branch,
                w_gb=w_gmlp_branch, w_out=w_out, f2g=ffn2_w_gate, f2u=ffn2_w_up, f2d=ffn2_w_down)
    m_sh = dict(f1g=m_ffn1_w_gate, f1u=m_ffn1_w_up, f1d=m_ffn1_w_down, w_in=m_w_in, w_ab=m_w_attn_branch,
                w_gb=m_w_gmlp_branch, w_out=m_w_out, f2g=m_ffn2_w_gate, f2u=m_ffn2_w_up, f2d=m_ffn2_w_down)
    v_sh = dict(f1g=v_ffn1_w_gate, f1u=v_ffn1_w_up, f1d=v_ffn1_w_down, w_in=v_w_in, w_ab=v_w_attn_branch,
                w_gb=v_w_gmlp_branch, w_out=v_w_out, f2g=v_ffn2_w_gate, f2u=v_ffn2_w_up, f2d=v_ffn2_w_down)
    w_sh = {k: (v[0].T if k in TRANSPOSED else v[0]) for k, v in w_sh.items()}
    m_sh = {k: (v[0].T if k in TRANSPOSED else v[0]) for k, v in m_sh.items()}
    v_sh = {k: (v[0].T if k in TRANSPOSED else v[0]) for k, v in v_sh.items()}

    started, tokens = [], []
    for gi, names in enumerate(GROUPS):
        placed = [_place_shard(w_sh[k], KIND[k], pos, "place_" + k) for k in names]
        fulls, ssem, rsem, token = _gather_start(placed, [KIND[k] for k in names], [w_sh[k].shape for k in names],
                                                 tokens[-1:], "gather_start_g%d" % gi)
        started.append((fulls, ssem, rsem))
        tokens.append(token)

    def weights_of(gi, after):
        names = GROUPS[gi]
        kinds, dims = [KIND[k] for k in names], [w_sh[k].shape for k in names]
        fulls, ssem, rsem = started[gi]
        fulls = _gather_wait(fulls, ssem, rsem, kinds, dims, list(after) + (tokens if gi == 0 else []),
                             "gather_wait_g%d" % gi)
        fulls = _gather_forward(fulls, kinds, dims, "gather_forward_g%d" % gi)
        return {k: (f.reshape(D, D) if k in ("w_gb", "w_out") else f) for k, f in zip(names, fulls)}

    pending, inflight = [], {}

    def grads_ready(gi, gd):
        grads = [gd[k] for k in GROUPS[gi]]
        lands = [lax.empty(g.shape[1:], F32) for g in grads]
        grads, lands, ssem, rsem, token = _pair_start(grads, lands, "rs_pair_start_g%d" % gi)
        pending.append((gi, grads, lands, ssem, rsem))
        return [token]

    def flush(after):
        gi, grads, lands, ssem, rsem = pending.pop()
        names = GROUPS[gi]
        grads, recv_a = _pair_wait(grads, lands, ssem, rsem, after, "rs_pair_wait_g%d" % gi)
        psums = [_pair_sum(g, r, pos, "rs_pair_sum_" + k) for g, r, k in zip(grads, recv_a, names)]
        lands = [lax.empty((3,) + p.shape[1:], BF16) for p in psums]
        psums, lands, ssem, rsem, token = _chip_start(psums, lands, "rs_chip_start_g%d" % gi)
        inflight[gi] = (grads, recv_a, psums, lands, ssem, rsem, token)
        return [token]

    P = dict(ln1_g=ln1_g, ln1_b=ln1_b, ln2_g=ln2_g, ln2_b=ln2_b, ln3_g=ln3_g, ln3_b=ln3_b, b_gates=b_gates,
             gmlp_ln_g=gmlp_ln_g, gmlp_ln_b=gmlp_ln_b, gmlp_w_s=gmlp_w_s[0], gmlp_b_s=gmlp_b_s[0])
    pos_f = positions.reshape(S, 1).astype(F32)
    small_state = []

    def small_ready(packed):
        land = jnp.zeros((8, SMALL_ROWS, D), F32)
        packed, land, ssem, rsem, token = _small_start(packed, land, "small_start")
        small_state.append((packed, land, ssem, rsem))
        return [token]

    loss_part, dx = _local_step(x[0], pos_f, loss_target[0], P, weights_of, grads_ready, flush, small_ready)
    loss = lax.psum(loss_part[0, 0], ("x", "y", "c"))

    g_out, d_out, m_out, v_out = {}, {}, {}, {}

    def finish(gis, after, tag):
        names, halves = [], []
        for gi in gis:
            grads, recv_a, psums, lands, ssem, rsem, token = inflight[gi]
            recv_b = _chip_wait(psums, lands, ssem, rsem, after + [inflight[0][6]], "rs_chip_wait_g%d" % gi)
            halves += [_owner_sum(g, ra, rb, pos, "rs_owner_sum_" + k)
                       for g, ra, rb, k in zip(grads, recv_a, recv_b, GROUPS[gi])]
            names += GROUPS[gi]
            after = halves[-1:]
        reduced = _sibling_allgather(halves, "rs_sibling_allgather_" + tag)
        for k, gfull in zip(names, reduced):
            res = _adamw(w_sh[k], gfull.reshape(w_sh[k].shape), m_sh[k], v_sh[k], "adamw_" + k)
            after = [res[1]]
            if k in TRANSPOSED:
                res = [r.T for r in res]
            g_out[k], d_out[k], m_out[k], v_out[k] = [r[None] for r in res]
        return after

    after = finish((3, 2, 1), [], "g321")

    small, parts = _small_wait(*small_state[0], after, "small_wait")
    sp = (ln1_g, ln1_b, gmlp_ln_g, gmlp_ln_b, ln2_g, ln2_b, ln3_g, ln3_b, b_gates, gmlp_b_s, gmlp_w_s)
    sm = (m_ln1_g, m_ln1_b, m_gmlp_ln_g, m_gmlp_ln_b, m_ln2_g, m_ln2_b, m_ln3_g, m_ln3_b, m_b_gates, m_gmlp_b_s,
          m_gmlp_w_s)
    sv = (v_ln1_g, v_ln1_b, v_gmlp_ln_g, v_gmlp_ln_b, v_ln2_g, v_ln2_b, v_ln3_g, v_ln3_b, v_b_gates, v_gmlp_b_s,
          v_gmlp_w_s)
    sg, sd, smn, svn = _small_sum_adamw(parts, small, pos, _pack_small(*sp), _pack_small(*sm), _pack_small(*sv),
                                        "small_adamw")
    names = ("ln1_g", "ln1_b", "gmlp_ln_g", "gmlp_ln_b", "ln2_g", "ln2_b", "ln3_g", "ln3_b", "b_gates", "gmlp_b_s",
             "gmlp_w_s")
    for dst, packed in ((g_out, sg), (d_out, sd), (m_out, smn), (v_out, svn)):
        for nm, val in zip(names, _unpack_small(packed)):
            dst[nm] = val
    finish((0,), [sg], "g0")

    order = ("f1g", "f1u", "f1d", "ln1_g", "ln1_b", "w_in", "b_gates", "gmlp_ln_g", "gmlp_ln_b", "gmlp_w_s", "gmlp_b_s",
             "w_ab", "w_gb", "w_out", "ln2_g", "ln2_b", "f2g", "f2u", "f2d", "ln3_g", "ln3_b")
    outs = [loss, dx[None]]
    for dst in (g_out, d_out, m_out, v_out):
        outs += [dst[k] for k in order]
    return tuple(outs)
```

```python
import jax
import jax.numpy as jnp
from jax import lax
from jax.experimental import pallas as pl
from jax.experimental.pallas import tpu as pltpu

F32 = jnp.float32
BF16 = jnp.bfloat16

S = 2048
D = 1024
NSH = 4
FSH = 704
ATT_W = 1536
GRP_W = 512
NG = 3
NH = 8
DH = 64
BLK = 128
NBLK = S // BLK
GW = 1024
IN_W = 8704
IN_SH = IN_W // NSH
ALPHA = 2.0 ** 0.25
LN_EPS = 1e-5
ROPE_THETA = 10000.0
DILATIONS = (1, 4, 16)
ADAM_LR, ADAM_B1, ADAM_B2, ADAM_EPS, ADAM_WD, ADAM_STEP = 0.001, 0.9, 0.999, 1e-08, 0.01, 10
SMALL_ROWS = 144
MESH_T = pl.DeviceIdType.MESH
MIB = 1024 * 1024
NEG_INF = float("-inf")


def _cp(sem, vmem_mib=48):
    return pltpu.CompilerParams(dimension_semantics=sem, vmem_limit_bytes=vmem_mib * MIB)


def _ln_stats(r):
    mu = jnp.mean(r, axis=-1, keepdims=True)
    xc = r - mu
    var = jnp.mean(xc * xc, axis=-1, keepdims=True)
    rstd = lax.rsqrt(var + LN_EPS)
    return xc * rstd, rstd


def _ln_dx(dxh, xh, rstd):
    m1 = jnp.mean(dxh, axis=-1, keepdims=True)
    m2 = jnp.mean(dxh * xh, axis=-1, keepdims=True)
    return rstd * (dxh - m1 - xh * m2)


def _dot_nt(a, b):
    return lax.dot_general(a, b, (((1,), (1,)), ((), ())), preferred_element_type=F32)


def _dot_tn(a, b):
    return lax.dot_general(a, b, (((0,), (0,)), ((), ())), preferred_element_type=F32)


def _dot(a, b):
    return jnp.dot(a, b, preferred_element_type=F32)


def _ffn_fwd(xin, wgt, wut, wd, ln_g, ln_b, name, emit_t=False, target=None):
    with_loss = target is not None
    tm = 512 if with_loss else 1024

    def body(x_ref, wg_ref, wu_ref, wd_ref, g_ref, b_ref, *rest):
        if with_loss:
            t_ref, dr_ref, a_ref, bb_ref, dg_ref, db_ref, loss_ref, acc_ref = rest
        elif emit_t:
            hb_ref, xh_ref, rstd_ref, a_ref, bb_ref, ht_ref, acc_ref = rest
        else:
            hb_ref, xh_ref, rstd_ref, a_ref, bb_ref, acc_ref = rest
        i = pl.program_id(0)
        j = pl.program_id(1)
        xb = x_ref[...].astype(BF16)
        a = _dot_nt(xb, wg_ref[...])
        b = _dot_nt(xb, wu_ref[...])
        a_ref[...] = a.astype(BF16)
        bb_ref[...] = b.astype(BF16)
        s = (a * jax.nn.sigmoid(a)) * b
        f = _dot(s.astype(BF16), wd_ref[...])

        @pl.when(j == 0)
        def _():
            acc_ref[...] = f

        @pl.when(j > 0)
        def _():
            acc_ref[...] += f

        @pl.when(j == NSH - 1)
        def _():
            r = ALPHA * x_ref[...] + 0.5 * acc_ref[...]
            xh, rstd = _ln_stats(r)
            h = xh * g_ref[...] + b_ref[...]
            if with_loss:
                err = h - t_ref[...]
                dy = err * (1.0 / D)
                dr_ref[...] = _ln_dx(dy * g_ref[...], xh, rstd)
                dg = jnp.sum(dy * xh, axis=0, keepdims=True)
                db = jnp.sum(dy, axis=0, keepdims=True)
                part = 0.5 * jnp.sum(jnp.mean(err * err, axis=-1, keepdims=True), axis=0, keepdims=True)
                part = jnp.broadcast_to(part, (8, 128))

                @pl.when(i == 0)
                def _():
                    dg_ref[...] = dg
                    db_ref[...] = db
                    loss_ref[...] = part

                @pl.when(i > 0)
                def _():
                    dg_ref[...] += dg
                    db_ref[...] += db
                    loss_ref[...] += part
            else:
                hb_ref[...] = h.astype(BF16)
                xh_ref[...] = xh
                rstd_ref[...] = rstd
                if emit_t:
                    ht_ref[...] = h.T.astype(BF16)

    row = pl.BlockSpec((tm, D), lambda i, j: (i, 0))
    vec = pl.BlockSpec((1, D), lambda i, j: (0, 0))
    wsp = pl.BlockSpec((None, FSH, D), lambda i, j: (j, 0, 0))
    ab = pl.BlockSpec((None, tm, FSH), lambda i, j: (j, i, 0))
    ab_shape = jax.ShapeDtypeStruct((NSH, S, FSH), BF16)
    in_specs, args = [row, wsp, wsp, wsp, vec, vec], (xin, wgt, wut, wd, ln_g, ln_b)
    if with_loss:
        in_specs, args = in_specs + [row], args + (target,)
        out_specs = [row, ab, ab, vec, vec, pl.BlockSpec((8, 128), lambda i, j: (0, 0))]
        out_shape = [jax.ShapeDtypeStruct((S, D), F32), ab_shape, ab_shape, jax.ShapeDtypeStruct((1, D), F32),
                     jax.ShapeDtypeStruct((1, D), F32), jax.ShapeDtypeStruct((8, 128), F32)]
    else:
        out_specs = [row, row, pl.BlockSpec((tm, 1), lambda i, j: (i, 0)), ab, ab]
        out_shape = [jax.ShapeDtypeStruct((S, D), BF16), jax.ShapeDtypeStruct((S, D), F32),
                     jax.ShapeDtypeStruct((S, 1), F32), ab_shape, ab_shape]
        if emit_t:
            out_specs.append(pl.BlockSpec((D, tm), lambda i, j: (0, i)))
            out_shape.append(jax.ShapeDtypeStruct((D, S), BF16))
    return pl.pallas_call(
        body, name=name, grid=(S // tm, NSH), in_specs=in_specs, out_specs=out_specs, out_shape=out_shape,
        scratch_shapes=[pltpu.VMEM((tm, D), F32)],
        compiler_params=_cp(("arbitrary" if with_loss else "parallel", "arbitrary"), vmem_mib=56),
    )(*args)


def _ffn_bwd(dr, xin_b, a, b, wgt, wut, wd, name, after=()):
    tm = 512
    ni = S // tm
    hr = FSH // 2

    def body(dr_ref, a_ref, b_ref, wg_ref, wu_ref, wd_ref, x_hbm, *rest):
        dwg_hbm, dwu_hbm, dwd_hbm, dx_hbm, dx_acc, da_all, db_all, s_all, df_all, x_all, res_buf, sems = rest[len(after):]
        j = pl.program_id(0)
        i = pl.program_id(1)
        rows = pl.ds(pl.multiple_of(i * tm, tm), tm)

        @pl.when(jnp.logical_and(j == 0, i == 0))
        def _():
            cp = pltpu.make_async_copy(x_hbm, x_all, sems.at[0])
            cp.start()
            cp.wait()

        drv = dr_ref[...]
        df = (0.5 * drv).astype(BF16)

        @pl.when(j == 0)
        def _():
            df_all[rows, :] = df

        ds = jnp.concatenate([_dot_nt(df, wd_ref[0:384, :]), _dot_nt(df, wd_ref[384:FSH, :])], axis=1)
        av = a_ref[...].astype(F32)
        bv = b_ref[...].astype(F32)
        sig = jax.nn.sigmoid(av)
        sl = av * sig
        da = (ds * bv * (sig * (1.0 + av * (1.0 - sig)))).astype(BF16)
        db = (ds * sl).astype(BF16)
        da_all[rows, :] = da
        db_all[rows, :] = db
        s_all[rows, :] = (sl * bv).astype(BF16)
        dx = _dot(da, wg_ref[...]) + _dot(db, wu_ref[...])

        @pl.when(j == 0)
        def _():
            dx_acc[rows, :] = ALPHA * drv + dx

        @pl.when(j > 0)
        def _():
            dx_acc[rows, :] += dx

        @pl.when(i == ni - 1)
        def _():
            copies = []
            for n, (lhs, rhs, out) in enumerate(((da_all, x_all, dwg_hbm), (db_all, x_all, dwu_hbm),
                                                 (s_all, df_all, dwd_hbm))):
                slot = n % 2
                if n >= 2:
                    for cp in copies[2 * (n - 2): 2 * (n - 2) + 2]:
                        cp.wait()
                res_buf[slot] = _dot_tn(lhs[...], rhs[...])
                for h in range(2):
                    cp = pltpu.make_async_copy(res_buf.at[slot, pl.ds(h * hr, hr), :], out.at[h, j],
                                               sems.at[1 + 2 * slot + h])
                    cp.start()
                    copies.append(cp)
            for cp in copies[2:]:
                cp.wait()

        @pl.when(jnp.logical_and(j == NSH - 1, i == ni - 1))
        def _():
            cp = pltpu.make_async_copy(dx_acc, dx_hbm, sems.at[0])
            cp.start()
            cp.wait()

    row = pl.BlockSpec((tm, D), lambda j, i: (i, 0))
    wsp = pl.BlockSpec((None, FSH, D), lambda j, i: (j, 0, 0))
    ab = pl.BlockSpec((None, tm, FSH), lambda j, i: (j, i, 0))
    dwshape = jax.ShapeDtypeStruct((2, NSH, hr, D), F32)
    return pl.pallas_call(
        body, name=name, grid=(NSH, ni),
        in_specs=[row, ab, ab, wsp, wsp, wsp, ANY] + [ANY] * len(after),
        out_specs=[ANY, ANY, ANY, ANY],
        out_shape=[dwshape, dwshape, dwshape, jax.ShapeDtypeStruct((S, D), F32)],
        scratch_shapes=[pltpu.VMEM((S, D), F32), pltpu.VMEM((S, FSH), BF16), pltpu.VMEM((S, FSH), BF16),
                        pltpu.VMEM((S, FSH), BF16), pltpu.VMEM((S, D), BF16), pltpu.VMEM((S, D), BF16),
                        pltpu.VMEM((2, FSH, D), F32), pltpu.SemaphoreType.DMA((5,))],
        compiler_params=_cp(("arbitrary", "arbitrary"), vmem_mib=58),
    )(dr, a, b, wgt, wut, wd, xin_b, *after)


def _matmul(a, b, mode, name, *, n, tm, tn, b_col0=0, after=()):
    m, k = a.shape
    assert m % tm == 0 and n % tn == 0 and b_col0 % tn == 0
    off = b_col0 // tn
    na = len(after)

    def body(*refs):
        a_ref, b_ref, o_ref = refs[na:]
        av = a_ref[...].astype(BF16)
        o_ref[...] = _dot(av, b_ref[...]) if mode == "nn" else _dot_nt(av, b_ref[...])

    if mode == "nn":
        b_spec = pl.BlockSpec((k, tn), lambda i, j: (0, j + off))
    else:
        b_spec = pl.BlockSpec((tn, k), lambda i, j: (j, 0))
    return pl.pallas_call(
        body, name=name, grid=(m // tm, n // tn),
        in_specs=[pl.BlockSpec(memory_space=pl.ANY)] * na + [pl.BlockSpec((tm, k), lambda i, j: (i, 0)), b_spec],
        out_specs=pl.BlockSpec((tm, tn), lambda i, j: (i, j)),
        out_shape=jax.ShapeDtypeStruct((m, n), F32),
        compiler_params=_cp(("parallel", "parallel")),
    )(*after, a, b)


def _wgrad(xt, y, rh, c, name, row_sharded, after=()):
    na = len(after)
    if row_sharded:
        def body(x_ref, y_ref, *rest):
            o_ref = rest[na]
            res = _dot(x_ref[...], y_ref[...].astype(BF16))
            for j in range(NSH):
                for h in range(2):
                    o_ref[h, j] = res[(2 * j + h) * rh:(2 * j + h + 1) * rh, :]

        grid = (1,)
        in_specs = [pl.BlockSpec((2 * NSH * rh, S), lambda g: (0, 0)), pl.BlockSpec((S, c), lambda g: (0, 0))]
        out_specs = pl.BlockSpec((2, NSH, rh, c), lambda g: (0, 0, 0, 0))
        sem = ("arbitrary",)
    else:
        def body(x_ref, y_ref, *rest):
            rest[na][...] = _dot(x_ref[...], y_ref[...].astype(BF16))

        grid = (2, NSH)
        in_specs = [pl.BlockSpec((rh, S), lambda h, j: (h, 0)), pl.BlockSpec((S, c), lambda h, j: (0, j))]
        out_specs = pl.BlockSpec((None, None, rh, c), lambda h, j: (h, j, 0, 0))
        sem = ("parallel", "parallel")
    return pl.pallas_call(
        body, name=name, grid=grid, in_specs=in_specs + [pl.BlockSpec(memory_space=pl.ANY)] * na, out_specs=out_specs,
        out_shape=jax.ShapeDtypeStruct((2, NSH, rh, c), F32),
        compiler_params=_cp(sem, vmem_mib=56),
    )(xt, y, *after)


def _resid_ln(res_xh, res_g, res_b, a, w, ln_g, ln_b, name):
    tm = 512

    def body(rx_ref, rg_ref, rb_ref, a_ref, w_ref, g_ref, b_ref, h_ref, hb_ref, xh_ref, rstd_ref):
        r = ALPHA * (rx_ref[...] * rg_ref[...] + rb_ref[...]) + _dot(a_ref[...], w_ref[...])
        xh, rstd = _ln_stats(r)
        h = xh * g_ref[...] + b_ref[...]
        h_ref[...] = h
        hb_ref[...] = h.astype(BF16)
        xh_ref[...] = xh
        rstd_ref[...] = rstd

    row = pl.BlockSpec((tm, D), lambda i: (i, 0))
    vec = pl.BlockSpec((1, D), lambda i: (0, 0))
    return pl.pallas_call(
        body, name=name, grid=(S // tm,),
        in_specs=[row, vec, vec, row, pl.BlockSpec((D, D), lambda i: (0, 0)), vec, vec],
        out_specs=[row, row, row, pl.BlockSpec((tm, 1), lambda i: (i, 0))],
        out_shape=[jax.ShapeDtypeStruct((S, D), F32), jax.ShapeDtypeStruct((S, D), BF16),
                   jax.ShapeDtypeStruct((S, D), F32), jax.ShapeDtypeStruct((S, 1), F32)],
        compiler_params=_cp(("parallel",)),
    )(res_xh, res_g, res_b, a, w, ln_g, ln_b)


def _dh1_ln_bwd(dproj, w_in, dr2, xh, rstd, ln_g, name, after=()):
    tm, tk = 512, IN_SH
    nk = IN_W // tk
    na = len(after)

    def body(*refs):
        a_ref, b_ref, add_ref, xh_ref, rstd_ref, g_ref, dr_ref, dg_ref, db_ref, acc_ref = refs[na:]
        i = pl.program_id(0)
        k = pl.program_id(1)
        p = _dot_nt(a_ref[...], b_ref[...])

        @pl.when(k == 0)
        def _():
            acc_ref[...] = p

        @pl.when(k > 0)
        def _():
            acc_ref[...] += p

        @pl.when(k == nk - 1)
        def _():
            dy = acc_ref[...] + ALPHA * add_ref[...]
            xhv = xh_ref[...]
            dr_ref[...] = _ln_dx(dy * g_ref[...], xhv, rstd_ref[...])
            dg = jnp.sum(dy * xhv, axis=0, keepdims=True)
            db = jnp.sum(dy, axis=0, keepdims=True)

            @pl.when(i == 0)
            def _():
                dg_ref[...] = dg
                db_ref[...] = db

            @pl.when(i > 0)
            def _():
                dg_ref[...] += dg
                db_ref[...] += db

    row = pl.BlockSpec((tm, D), lambda i, k: (i, 0))
    vec = pl.BlockSpec((1, D), lambda i, k: (0, 0))
    return pl.pallas_call(
        body, name=name, grid=(S // tm, nk),
        in_specs=[pl.BlockSpec(memory_space=pl.ANY)] * na
        + [pl.BlockSpec((tm, tk), lambda i, k: (i, k)), pl.BlockSpec((D, tk), lambda i, k: (0, k)), row, row,
           pl.BlockSpec((tm, 1), lambda i, k: (i, 0)), vec],
        out_specs=[row, vec, vec],
        out_shape=[jax.ShapeDtypeStruct((S, D), F32), jax.ShapeDtypeStruct((1, D), F32),
                   jax.ShapeDtypeStruct((1, D), F32)],
        scratch_shapes=[pltpu.VMEM((tm, D), F32)],
        compiler_params=_cp(("arbitrary", "arbitrary"), vmem_mib=56),
    )(*after, dproj, w_in, dr2, xh, rstd, ln_g)


def _ln_bwd(dout, xh, rstd, ln_g, name, after=()):
    tm = 512
    na = len(after)

    def body(*refs):
        y_ref, xh_ref, rstd_ref, g_ref, dr_ref, dg_ref, db_ref = refs[na:]
        dy = y_ref[...]
        i = pl.program_id(0)
        xh = xh_ref[...]
        dr_ref[...] = _ln_dx(dy * g_ref[...], xh, rstd_ref[...])
        dg = jnp.sum(dy * xh, axis=0, keepdims=True)
        db = jnp.sum(dy, axis=0, keepdims=True)

        @pl.when(i == 0)
        def _():
            dg_ref[...] = dg
            db_ref[...] = db

        @pl.when(i > 0)
        def _():
            dg_ref[...] += dg
            db_ref[...] += db

    row = pl.BlockSpec((tm, D), lambda i: (i, 0))
    vec = pl.BlockSpec((1, D), lambda i: (0, 0))
    return pl.pallas_call(
        body, name=name, grid=(S // tm,),
        in_specs=[pl.BlockSpec(memory_space=pl.ANY)] * na + [row, row, pl.BlockSpec((tm, 1), lambda i: (i, 0)), vec],
        out_specs=[row, vec, vec],
        out_shape=[jax.ShapeDtypeStruct((S, D), F32), jax.ShapeDtypeStruct((1, D), F32),
                   jax.ShapeDtypeStruct((1, D), F32)],
        compiler_params=_cp(("arbitrary",)),
    )(*after, dout, xh, rstd, ln_g)


ROPE_TM = 256


def _rope_tables(pos_ref, invf_ref, sign):
    ang = pos_ref[...] * invf_ref[...]
    lane = lax.broadcasted_iota(jnp.int32, ang.shape, 1)
    first = (lane % DH) < (DH // 2)
    sinv = jnp.sin(ang) * sign
    return first, jnp.cos(ang), jnp.where(first, -sinv, sinv)


def _rotate(x, first, cosf, sinf):
    return x * cosf + jnp.where(first, pltpu.roll(x, 96, 1), pltpu.roll(x, 32, 1)) * sinf


def _proj_qkv_rope(hb, w_in, pos_f, invf, name):
    tm = 2 * ROPE_TM

    def body(h_ref, w_ref, pos_ref, invf_ref, o0_ref, o1_ref, o2_ref, buf_ref):
        rot = pl.program_id(1) < 2
        first, cosf, sinf = _rope_tables(pos_ref, invf_ref, 1.0)
        cosf = jnp.where(rot, cosf, 1.0)
        sinf = jnp.where(rot, sinf, 0.0)
        acc = _dot(h_ref[...], w_ref[...])
        for gi, (d, o_ref) in enumerate(zip(DILATIONS, (o0_ref, o1_ref, o2_ref))):
            for ch in range(GRP_W // 128):
                cols = slice(ch * 128, (ch + 1) * 128)
                x = _rotate(acc[:, gi * GRP_W + ch * 128: gi * GRP_W + (ch + 1) * 128], first, cosf, sinf)
                if d == 1:
                    o_ref[0, :, cols] = x.astype(BF16)
                else:
                    buf_ref[...] = x
                    for r in range(d):
                        o_ref[r, :, cols] = buf_ref[pl.ds(r, tm // d, stride=d), :].astype(BF16)

    return pl.pallas_call(
        body, name=name, grid=(S // tm, 3),
        in_specs=[pl.BlockSpec((tm, D), lambda i, s: (i, 0)), pl.BlockSpec((D, ATT_W), lambda i, s: (0, s)),
                  pl.BlockSpec((tm, 1), lambda i, s: (i, 0)), pl.BlockSpec((1, 128), lambda i, s: (0, 0))],
        out_specs=[pl.BlockSpec((d, tm // d, GRP_W), lambda i, s: (0, i, s)) for d in DILATIONS],
        out_shape=[jax.ShapeDtypeStruct((d, S // d, 3 * GRP_W), BF16) for d in DILATIONS],
        scratch_shapes=[pltpu.VMEM((tm, 128), F32)],
        compiler_params=_cp(("parallel", "parallel")),
    )(hb, w_in, pos_f, invf)


def _rope_bwd(dqkv_c, pos_f, invf, name):
    tm = ROPE_TM

    def body(*refs):
        g_refs, (pos_ref, invf_ref, o_ref, buf_ref) = refs[:9], refs[9:]
        first, cosf, sinf = _rope_tables(pos_ref, invf_ref, -1.0)
        for sec in range(3):
            for gi, d in enumerate(DILATIONS):
                g_ref = g_refs[3 * gi + sec]
                for ch in range(GRP_W // 128):
                    cols = slice(ch * 128, (ch + 1) * 128)
                    if d == 1:
                        x = g_ref[0, :, cols]
                    else:
                        for r in range(d):
                            buf_ref[pl.ds(r, tm // d, stride=d), :] = g_ref[r, :, cols]
                        x = buf_ref[...]
                    if sec < 2:
                        x = _rotate(x, first, cosf, sinf)
                    dst = sec * ATT_W + gi * GRP_W + ch * 128
                    o_ref[:, dst:dst + 128] = x.astype(BF16)

    g_specs = [pl.BlockSpec((d, tm // d, GRP_W), lambda i: (0, i, 0)) for d in DILATIONS for _ in range(3)]
    return pl.pallas_call(
        body, name=name, grid=(S // tm,),
        in_specs=g_specs + [pl.BlockSpec((tm, 1), lambda i: (i, 0)), pl.BlockSpec((1, 128), lambda i: (0, 0))],
        out_specs=pl.BlockSpec((tm, 3 * ATT_W), lambda i: (i, 0)),
        out_shape=jax.ShapeDtypeStruct((S, 3 * ATT_W), BF16),
        scratch_shapes=[pltpu.VMEM((tm, 128), F32)],
        compiler_params=_cp(("parallel",)),
    )(*[g for grp in dqkv_c for g in grp], pos_f, invf)


def _class_order(ts, name):
    tm = ROPE_TM
    n = len(ts)

    def body(*refs):
        buf_ref = refs[3 * n]
        for a in range(n):
            for ch in range(GRP_W // 128):
                cols = slice(ch * 128, (ch + 1) * 128)
                buf_ref[...] = refs[a][:, cols]
                for b, d in enumerate(DILATIONS[1:]):
                    for r in range(d):
                        refs[n + 2 * a + b][r, :, cols] = buf_ref[pl.ds(r, tm // d, stride=d), :]

    return pl.pallas_call(
        body, name=name, grid=(S // tm,),
        in_specs=[pl.BlockSpec((tm, GRP_W), lambda i: (i, 0))] * n,
        out_specs=[pl.BlockSpec((d, tm // d, GRP_W), lambda i: (0, i, 0)) for _ in range(n) for d in DILATIONS[1:]],
        out_shape=[jax.ShapeDtypeStruct((d, S // d, GRP_W), F32) for _ in range(n) for d in DILATIONS[1:]],
        scratch_shapes=[pltpu.VMEM((tm, 128), F32)],
        compiler_params=_cp(("parallel",)),
    )(*ts)


def _own_lanes(h):
    return (lax.broadcasted_iota(jnp.int32, (1, 2 * DH), 1) // DH) == (h % 2)


def _heads(ref):
    out = []
    for h in range(NH):
        pair = ref[:, (h // 2) * 2 * DH:(h // 2 + 1) * 2 * DH]
        out.append(jnp.where(_own_lanes(h), pair, jnp.zeros_like(pair)))
    return jnp.stack(out)


def _unheads(t3):
    return jnp.concatenate([t3[2 * p] + t3[2 * p + 1] for p in range(NH // 2)], axis=1)


def _bdot_nt(a, b):
    return lax.dot_general(a, b, (((2,), (2,)), ((0,), (0,))), preferred_element_type=F32)


def _bdot(a, b):
    return lax.dot_general(a, b, (((2,), (1,)), ((0,), (0,))), preferred_element_type=F32)


def _bdot_tn(a, b):
    return lax.dot_general(a, b, (((1,), (1,)), ((0,), (0,))), preferred_element_type=F32)


def _attn_fwd(gi, qkv_c, name):
    d = DILATIONS[gi]
    nblk = S // d // BLK

    def body(*refs):
        if nblk > 1:
            q_ref, kc_ref, kp_ref, vc_ref, vp_ref, o_ref, lse_ref = refs
            has_prev = pl.program_id(1) != 0
        else:
            q_ref, kc_ref, vc_ref, o_ref, lse_ref = refs
        qi = lax.broadcasted_iota(jnp.int32, (NH, BLK, BLK), 1)
        kj = lax.broadcasted_iota(jnp.int32, (NH, BLK, BLK), 2)
        q = _heads(q_ref)
        sc = jnp.where(kj <= qi, _bdot_nt(q, _heads(kc_ref)) * 0.125, NEG_INF)
        m = jnp.max(sc, axis=-1, keepdims=True)
        if nblk > 1:
            mask_p = jnp.logical_and(kj >= qi, has_prev)
            sp = jnp.where(mask_p, _bdot_nt(q, _heads(kp_ref)) * 0.125, NEG_INF)
            m = jnp.maximum(m, jnp.max(sp, axis=-1, keepdims=True))
        pc = jnp.exp(sc - m)
        l = jnp.sum(pc, axis=-1, keepdims=True)
        o = _bdot(pc.astype(BF16), _heads(vc_ref))
        if nblk > 1:
            pp = jnp.exp(sp - m)
            l = l + jnp.sum(pp, axis=-1, keepdims=True)
            o = o + _bdot(pp.astype(BF16), _heads(vp_ref))
        o_ref[...] = _unheads(o / l)
        lse = jnp.broadcast_to(m + jnp.log(l), (NH, BLK, 2 * DH))
        lse_ref[...] = _unheads(jnp.stack([jnp.where(_own_lanes(h), lse[h], 0.0) for h in range(NH)]))

    def cur(sec):
        return pl.BlockSpec((None, BLK, GRP_W), lambda r, n: (r, n, sec))

    def prev(sec):
        return pl.BlockSpec((None, BLK, GRP_W), lambda r, n: (r, jnp.maximum(n - 1, 0), sec))

    out = pl.BlockSpec((None, BLK, GRP_W), lambda r, n: (r, n, 0))
    shp = jax.ShapeDtypeStruct((d, S // d, GRP_W), F32)
    if nblk > 1:
        in_specs, args = [cur(0), cur(1), prev(1), cur(2), prev(2)], (qkv_c,) * 5
    else:
        in_specs, args = [cur(0), cur(1), cur(2)], (qkv_c,) * 3
    return pl.pallas_call(
        body, name=name, grid=(d, nblk), in_specs=in_specs, out_specs=[out, out], out_shape=[shp, shp],
        compiler_params=_cp(("parallel", "parallel")),
    )(*args)


def _attn_combine(os, lses, name):
    tm = ROPE_TM

    def body(o0_ref, o1_ref, o2_ref, l0_ref, l1_ref, l2_ref, y_ref, yt_ref, l_ref, buf_ref):
        def token_order(ref, d, cols, slot):
            if d == 1:
                return ref[0, :, cols]
            for r in range(d):
                buf_ref[slot, pl.ds(r, tm // d, stride=d), :] = ref[r, :, cols]
            return buf_ref[slot]

        for ch in range(GRP_W // 128):
            cols = slice(ch * 128, (ch + 1) * 128)
            o = [token_order(ref, d, cols, k) for k, (ref, d) in enumerate(zip((o0_ref, o1_ref, o2_ref), DILATIONS))]
            ls = [token_order(ref, d, cols, 3 + k)
                  for k, (ref, d) in enumerate(zip((l0_ref, l1_ref, l2_ref), DILATIONS))]
            m = jnp.maximum(jnp.maximum(ls[0], ls[1]), ls[2])
            e = [jnp.exp(l - m) for l in ls]
            den = e[0] + e[1] + e[2]
            y = (e[0] * o[0] + e[1] * o[1] + e[2] * o[2]) / den
            y_ref[:, cols] = y
            yt_ref[cols, :] = y.T.astype(BF16)
            l_ref[:, cols] = m + jnp.log(den)

    blk = pl.BlockSpec((tm, GRP_W), lambda i: (i, 0))
    cls = [pl.BlockSpec((d, tm // d, GRP_W), lambda i: (0, i, 0)) for d in DILATIONS]
    shp = jax.ShapeDtypeStruct((S, GRP_W), F32)
    return pl.pallas_call(
        body, name=name, grid=(S // tm,), in_specs=cls + cls,
        out_specs=[blk, pl.BlockSpec((GRP_W, tm), lambda i: (0, i)), blk],
        out_shape=[shp, jax.ShapeDtypeStruct((GRP_W, S), BF16), shp],
        scratch_shapes=[pltpu.VMEM((6, tm, 128), F32)],
        compiler_params=_cp(("parallel",)),
    )(*os, *lses)


def _attn_bwd(gi, qkv_c, dy_c, y_c, lse_c, name):
    d = DILATIONS[gi]
    nblk = S // d // BLK

    def body(*refs):
        if nblk > 1:
            (q_ref, qn_ref, k_ref, kp_ref, v_ref, vp_ref, dy_ref, dyn_ref, y_ref, yn_ref, l_ref, ln_ref,
             dq_ref, dk_ref, dv_ref) = refs
            n = pl.program_id(1)
            has_prev = n != 0
            has_next = n != nblk - 1
        else:
            q_ref, k_ref, v_ref, dy_ref, y_ref, l_ref, dq_ref, dk_ref, dv_ref = refs
        qi = lax.broadcasted_iota(jnp.int32, (NH, BLK, BLK), 1)
        kj = lax.broadcasted_iota(jnp.int32, (NH, BLK, BLK), 2)

        def lse_col(ref):
            return jnp.stack([ref[:, h * DH:h * DH + 1] for h in range(NH)])

        q, k, v = _heads(q_ref), _heads(k_ref), _heads(v_ref)
        dy = _heads(dy_ref)
        dd = jnp.sum(dy * _heads(y_ref), axis=-1, keepdims=True)
        lcol = lse_col(l_ref)
        dyb = dy.astype(BF16)
        p = jnp.exp(jnp.where(kj <= qi, _bdot_nt(q, k) * 0.125, NEG_INF) - lcol)
        ds = (p * (_bdot_nt(dyb, v) - dd)).astype(BF16)
        dq = _bdot(ds, k)
        dk = _bdot_tn(ds, q)
        dv = _bdot_tn(p.astype(BF16), dyb)
        if nblk > 1:
            qn, kpv, vpv = _heads(qn_ref), _heads(kp_ref), _heads(vp_ref)
            dyn = _heads(dyn_ref)
            ddn = jnp.sum(dyn * _heads(yn_ref), axis=-1, keepdims=True)
            lncol = lse_col(ln_ref)
            dynb = dyn.astype(BF16)
            mask_p = jnp.logical_and(kj >= qi, has_prev)
            pp = jnp.exp(jnp.where(mask_p, _bdot_nt(q, kpv) * 0.125, NEG_INF) - lcol)
            dsp = (pp * (_bdot_nt(dyb, vpv) - dd)).astype(BF16)
            dq = dq + _bdot(dsp, kpv)
            mask_n = jnp.logical_and(kj >= qi, has_next)
            pn = jnp.exp(jnp.where(mask_n, _bdot_nt(qn, k) * 0.125, NEG_INF) - lncol)
            dsn = (pn * (_bdot_nt(dynb, v) - ddn)).astype(BF16)
            dk = dk + _bdot_tn(dsn, qn)
            dv = dv + _bdot_tn(pn.astype(BF16), dynb)
        dq_ref[...] = _unheads(dq) * 0.125
        dk_ref[...] = _unheads(dk) * 0.125
        dv_ref[...] = _unheads(dv)

    def spec(sec, shift):
        def idx(r, n):
            return (r, jnp.clip(n + shift, 0, nblk - 1), sec)
        return pl.BlockSpec((None, BLK, GRP_W), idx)

    if nblk > 1:
        in_specs = [spec(0, 0), spec(0, 1), spec(1, 0), spec(1, -1), spec(2, 0), spec(2, -1),
                    spec(0, 0), spec(0, 1), spec(0, 0), spec(0, 1), spec(0, 0), spec(0, 1)]
        args = (qkv_c,) * 6 + (dy_c, dy_c, y_c, y_c, lse_c, lse_c)
    else:
        in_specs = [spec(0, 0), spec(1, 0), spec(2, 0), spec(0, 0), spec(0, 0), spec(0, 0)]
        args = (qkv_c, qkv_c, qkv_c, dy_c, y_c, lse_c)
    out = spec(0, 0)
    shp = jax.ShapeDtypeStruct((d, S // d, GRP_W), F32)
    return pl.pallas_call(
        body, name=name, grid=(d, nblk), in_specs=in_specs, out_specs=[out, out, out], out_shape=[shp, shp, shp],
        compiler_params=_cp(("parallel", "parallel")),
    )(*args)


_SQRT_HALF = 0.7071067811865476
_INV_SQRT_2PI = 0.3989422804014327


def _gelu(z):
    return 0.5 * z * (1.0 + lax.erf(z * _SQRT_HALF))


def _gelu_grad(z):
    return 0.5 * (1.0 + lax.erf(z * _SQRT_HALF)) + z * (jnp.exp(-0.5 * z * z) * _INV_SQRT_2PI)


def _tril_mask():
    t = lax.broadcasted_iota(jnp.int32, (BLK, BLK), 0)
    s = lax.broadcasted_iota(jnp.int32, (BLK, BLK), 1)
    return s <= t


def _groups(t):
    return jnp.stack([t[:, g * BLK:(g + 1) * BLK] for g in range(8)])


def _ungroup(t3):
    return jnp.concatenate([t3[g] for g in range(8)], axis=1)


def _group_bias(bs_ref):
    return jnp.stack([bs_ref[:, g:g + 1] for g in range(8)])


def _gmlp_fwd(z, ln_g, ln_b, w_s, b_s_t, name):
    def body(z_ref, g_ref, b_ref, ws_ref, bs_ref, y_ref, yt_ref):
        zg = _gelu(z_ref[...])
        u = zg[:, :GW]
        xh, _ = _ln_stats(zg[:, GW:])
        vn = (xh * g_ref[...] + b_ref[...]).astype(BF16)
        wt = jnp.where(_tril_mask(), ws_ref[...], 0.0).astype(BF16)
        yv = u * _ungroup(_bdot(wt, _groups(vn)) + _group_bias(bs_ref))
        y_ref[...] = yv.astype(BF16)
        yt_ref[...] = yv.T.astype(BF16)

    vec = pl.BlockSpec((1, GW), lambda n: (0, 0))
    return pl.pallas_call(
        body, name=name, grid=(NBLK,),
        in_specs=[pl.BlockSpec((BLK, 2 * GW), lambda n: (n, 0)), vec, vec,
                  pl.BlockSpec((8, BLK, BLK), lambda n: (0, 0, 0)), pl.BlockSpec((BLK, 8), lambda n: (0, 0))],
        out_specs=[pl.BlockSpec((BLK, GW), lambda n: (n, 0)), pl.BlockSpec((GW, BLK), lambda n: (0, n))],
        out_shape=[jax.ShapeDtypeStruct((S, GW), BF16), jax.ShapeDtypeStruct((GW, S), BF16)],
        compiler_params=_cp(("parallel",)),
    )(z, ln_g, ln_b, w_s, b_s_t)


def _gmlp_bwd(z, dy, ln_g, ln_b, w_s, b_s_t, name):
    def body(z_ref, dy_ref, g_ref, b_ref, ws_ref, bs_ref, dz_ref, dws_ref, dbs_ref, dg_ref, db_ref, dvn_ref):
        n = pl.program_id(0)
        zv = z_ref[...]
        zg = _gelu(zv)
        u = zg[:, :GW]
        xh, rstd = _ln_stats(zg[:, GW:])
        vn = (xh * g_ref[...] + b_ref[...]).astype(BF16)
        tril = _tril_mask()

        @pl.when(n == 0)
        def _():
            dws_ref[...] = jnp.zeros_like(dws_ref)
            dbs_ref[...] = jnp.zeros_like(dbs_ref)
            dg_ref[...] = jnp.zeros_like(dg_ref)
            db_ref[...] = jnp.zeros_like(db_ref)

        wt = jnp.where(tril, ws_ref[...], 0.0).astype(BF16)
        vn3 = _groups(vn)
        dyv = dy_ref[...]
        mixed = _ungroup(_bdot(wt, vn3) + _group_bias(bs_ref))
        dz_ref[:, :GW] = (dyv * mixed * _gelu_grad(zv[:, :GW])).astype(BF16)
        dmix3 = _groups(dyv * u)
        dmb = dmix3.astype(BF16)
        dws_ref[...] += jnp.where(tril, _bdot_nt(dmb, vn3), 0.0)
        dbsum = jnp.sum(dmix3, axis=-1, keepdims=True)
        for gg in range(8):
            dbs_ref[:, gg:gg + 1] += dbsum[gg]
        dvn_ref[...] = _ungroup(_bdot_tn(wt, dmb))

        dvn = dvn_ref[...]
        dg_ref[...] += jnp.sum(dvn * xh, axis=0, keepdims=True)
        db_ref[...] += jnp.sum(dvn, axis=0, keepdims=True)
        dvg = _ln_dx(dvn * g_ref[...], xh, rstd)
        dz_ref[:, GW:] = (dvg * _gelu_grad(zv[:, GW:])).astype(BF16)

    vec = pl.BlockSpec((1, GW), lambda n: (0, 0))
    ws = pl.BlockSpec((8, BLK, BLK), lambda n: (0, 0, 0))
    bs = pl.BlockSpec((BLK, 8), lambda n: (0, 0))
    return pl.pallas_call(
        body, name=name, grid=(NBLK,),
        in_specs=[pl.BlockSpec((BLK, 2 * GW), lambda n: (n, 0)), pl.BlockSpec((BLK, GW), lambda n: (n, 0)),
                  vec, vec, ws, bs],
        out_specs=[pl.BlockSpec((BLK, 2 * GW), lambda n: (n, 0)), ws, bs, vec, vec],
        out_shape=[jax.ShapeDtypeStruct((S, 2 * GW), BF16), jax.ShapeDtypeStruct((8, BLK, BLK), F32),
                   jax.ShapeDtypeStruct((BLK, 8), F32), jax.ShapeDtypeStruct((1, GW), F32),
                   jax.ShapeDtypeStruct((1, GW), F32)],
        scratch_shapes=[pltpu.VMEM((BLK, GW), F32)],
        compiler_params=_cp(("arbitrary",)),
    )(z, dy, ln_g, ln_b, w_s, b_s_t)


def _merge_fwd(a, b, gl, b_gates, name):
    tm = 512

    def body(a_ref, b_ref, g0_ref, g1_ref, bg_ref, o_ref, ot_ref):
        g0 = jax.nn.sigmoid(g0_ref[...] + bg_ref[:, :D])
        g1 = jax.nn.sigmoid(g1_ref[...] + bg_ref[:, D:])
        mg = g0 * a_ref[...] + g1 * b_ref[...]
        o_ref[...] = mg.astype(BF16)
        ot_ref[...] = mg.T.astype(BF16)

    row = pl.BlockSpec((tm, D), lambda i: (i, 0))
    return pl.pallas_call(
        body, name=name, grid=(S // tm,),
        in_specs=[row, row, row, pl.BlockSpec((tm, D), lambda i: (i, 1)), pl.BlockSpec((1, 2 * D), lambda i: (0, 0))],
        out_specs=[row, pl.BlockSpec((D, tm), lambda i: (0, i))],
        out_shape=[jax.ShapeDtypeStruct((S, D), BF16), jax.ShapeDtypeStruct((D, S), BF16)],
        compiler_params=_cp(("parallel",)),
    )(a, b, gl, gl, b_gates)


def _merge_bwd(dm, a, b, gl, b_gates, name):
    tm = 512

    def body(dm_ref, a_ref, b_ref, g0_ref, g1_ref, bg_ref, da_ref, db_ref, dgl_ref, dbg_ref):
        i = pl.program_id(0)
        dmv = dm_ref[...]
        g0 = jax.nn.sigmoid(g0_ref[...] + bg_ref[:, :D])
        g1 = jax.nn.sigmoid(g1_ref[...] + bg_ref[:, D:])
        da_ref[...] = (dmv * g0).astype(BF16)
        db_ref[...] = (dmv * g1).astype(BF16)
        d0 = dmv * a_ref[...] * g0 * (1.0 - g0)
        d1 = dmv * b_ref[...] * g1 * (1.0 - g1)
        dgl_ref[:, :D] = d0.astype(BF16)
        dgl_ref[:, D:] = d1.astype(BF16)
        s0 = jnp.sum(d0, axis=0, keepdims=True)
        s1 = jnp.sum(d1, axis=0, keepdims=True)

        @pl.when(i == 0)
        def _():
            dbg_ref[:, :D] = s0
            dbg_ref[:, D:] = s1

        @pl.when(i > 0)
        def _():
            dbg_ref[:, :D] += s0
            dbg_ref[:, D:] += s1

    row = pl.BlockSpec((tm, D), lambda i: (i, 0))
    wide = pl.BlockSpec((tm, 2 * D), lambda i: (i, 0))
    bg = pl.BlockSpec((1, 2 * D), lambda i: (0, 0))
    return pl.pallas_call(
        body, name=name, grid=(S // tm,),
        in_specs=[row, row, row, row, pl.BlockSpec((tm, D), lambda i: (i, 1)), bg],
        out_specs=[row, row, wide, bg],
        out_shape=[jax.ShapeDtypeStruct((S, D), BF16), jax.ShapeDtypeStruct((S, D), BF16),
                   jax.ShapeDtypeStruct((S, 2 * D), BF16), jax.ShapeDtypeStruct((1, 2 * D), F32)],
        compiler_params=_cp(("arbitrary",)),
    )(dm, a, b, gl, gl, b_gates)


def _adam_math(w, g, m, v):
    m2 = ADAM_B1 * m + (1.0 - ADAM_B1) * g
    v2 = ADAM_B2 * v + (1.0 - ADAM_B2) * (g * g)
    m_hat = m2 / (1.0 - ADAM_B1 ** ADAM_STEP)
    v_hat = v2 / (1.0 - ADAM_B2 ** ADAM_STEP)
    delta = -ADAM_LR * (m_hat / (jnp.sqrt(v_hat) + ADAM_EPS) + ADAM_WD * w)
    return delta, m2, v2


def _pick_rows(rows, cols, unit=16, budget=2 * MIB):
    best = unit
    for t in range(unit, rows + 1, unit):
        if rows % t == 0 and t * cols * 4 <= budget:
            best = t
    assert rows % best == 0
    return best


def _adamw(w, g, m, v, name):
    r, c = w.shape
    tr = _pick_rows(r, c, unit=8)

    def body(w_ref, g_ref, m_ref, v_ref, go_ref, d_ref, mo_ref, vo_ref):
        gv = g_ref[...]
        delta, m2, v2 = _adam_math(w_ref[...], gv, m_ref[...], v_ref[...])
        go_ref[...] = gv
        d_ref[...] = delta
        mo_ref[...] = m2
        vo_ref[...] = v2

    blk = pl.BlockSpec((tr, c), lambda i: (i, 0))
    shp = jax.ShapeDtypeStruct((r, c), F32)
    return pl.pallas_call(
        body, name=name, grid=(r // tr,), in_specs=[blk] * 4, out_specs=[blk] * 4, out_shape=[shp] * 4,
        compiler_params=_cp(("parallel",)),
    )(*[pltpu.with_memory_space_constraint(t, pltpu.HBM) for t in (w, g, m, v)])


def _small_sum_adamw(parts, own, pos, w, m, v, name):
    tr = 48

    def body(pos_ref, p_ref, own_ref, w_ref, m_ref, v_ref, g_ref, d_ref, mo_ref, vo_ref):
        me = 2 * pos_ref[1] + pos_ref[0]
        gv = None
        for k in range(8):
            term = jnp.where(me == k, own_ref[...], p_ref[k])
            gv = term if gv is None else gv + term
        delta, m2, v2 = _adam_math(w_ref[...], gv, m_ref[...], v_ref[...])
        g_ref[...] = gv
        d_ref[...] = delta
        mo_ref[...] = m2
        vo_ref[...] = v2

    blk = pl.BlockSpec((tr, D), lambda i, p: (i, 0))
    shp = jax.ShapeDtypeStruct((SMALL_ROWS, D), F32)
    return pl.pallas_call(
        body, name=name,
        grid_spec=pltpu.PrefetchScalarGridSpec(
            num_scalar_prefetch=1, grid=(SMALL_ROWS // tr,),
            in_specs=[pl.BlockSpec((8, tr, D), lambda i, p: (0, i, 0)), blk, blk, blk, blk],
            out_specs=[blk] * 4),
        out_shape=[shp] * 4,
        compiler_params=_cp(("parallel",)),
    )(pos, parts, own, w, m, v)


ANY = pl.BlockSpec(memory_space=pl.ANY)


def _in_hbm(arrays):
    return [pltpu.with_memory_space_constraint(a, pltpu.HBM) for a in arrays]


def _mesh_pos():
    x, y, c = lax.axis_index("x"), lax.axis_index("y"), lax.axis_index("c")
    chips = [(1 - x, y), (x, 1 - y), (1 - x, 1 - y)]
    return x, y, c, chips


def _place_shard(w, kind, pos, name):
    r, c = w.shape
    tr = _pick_rows(r, c)

    def body(pos_ref, w_ref, o_ref):
        o_ref[...] = w_ref[...].astype(BF16)

    if kind == "stack":
        o_spec = pl.BlockSpec((None, tr, c), lambda i, p: (p[1], i, 0))
        shape = (NSH, r, c)
    else:
        o_spec = pl.BlockSpec((tr, c), lambda i, p: (i, p[1]))
        shape = (r, NSH * c)
    return pl.pallas_call(
        body, name=name,
        grid_spec=pltpu.PrefetchScalarGridSpec(
            num_scalar_prefetch=1, grid=(r // tr,),
            in_specs=[pl.BlockSpec((tr, c), lambda i, p: (i, 0))], out_specs=o_spec),
        out_shape=pltpu.HBM(shape, BF16),
        compiler_params=_cp(("parallel",)),
    )(pos, pltpu.with_memory_space_constraint(w, pltpu.HBM))


SEM = pl.BlockSpec(memory_space=pltpu.SEMAPHORE)
SPLIT_COPY = pltpu.CompilerParams(has_side_effects=pltpu.SideEffectType.DATAFLOW_SIDE_EFFECTING)


def _shard_window(ref, kind, j, h, dims):
    r, c = dims
    rows = pl.ds(pl.multiple_of(h * (r // 2), 16), r // 2)
    if kind == "stack":
        return ref.at[j, rows, :]
    return ref.at[rows, pl.ds(pl.multiple_of(j * c, 128), c)]


def _ici_copy(ref, kind, dims, j, c, sems, idx, to):
    win = _shard_window(ref, kind, j, c, dims)
    return pltpu.make_async_remote_copy(src_ref=win, dst_ref=win, send_sem=sems[0].at[idx], recv_sem=sems[1].at[idx],
                                        device_id=to, device_id_type=MESH_T)


def _gather_start(fulls, kinds, dims, after, name):
    n, na = len(fulls), len(after)

    def body(*refs):
        outs = refs[n + na:2 * n + na]
        send_sems, recv_sems, token = refs[2 * n + na:]
        x, y, c, chips = _mesh_pos()
        for a in range(n):
            for k, chip in enumerate(chips):
                _ici_copy(outs[a], kinds[a], dims[a], 2 * x + y, c, (send_sems, recv_sems), 3 * a + k,
                          (chip[0], chip[1], c)).start()
        token[...] = jnp.zeros_like(token)

    res = pl.pallas_call(
        body, name=name, in_specs=[ANY] * (n + na),
        out_specs=[ANY] * n + [SEM, SEM, pl.BlockSpec(memory_space=pltpu.VMEM)],
        out_shape=[pltpu.HBM(f.shape, BF16) for f in fulls]
        + [pltpu.SemaphoreType.DMA((3 * n,)), pltpu.SemaphoreType.DMA((3 * n,)), jax.ShapeDtypeStruct((8, 128), F32)],
        input_output_aliases={i: i for i in range(n)},
        compiler_params=SPLIT_COPY,
    )(*_in_hbm(fulls), *after)
    return res[:n], res[n], res[n + 1], res[n + 2]


def _gather_wait(fulls, send_sems, recv_sems, kinds, dims, after, name):
    n, na = len(fulls), len(after)

    def body(*refs):
        ssem, rsem = refs[n], refs[n + 1]
        outs = refs[n + 2 + na:]
        x, y, c, chips = _mesh_pos()
        for a in range(n):
            for k, chip in enumerate(chips):
                to = (chip[0], chip[1], c)
                _ici_copy(outs[a], kinds[a], dims[a], 2 * x + y, c, (ssem, rsem), 3 * a + k, to).wait_send()
                _ici_copy(outs[a], kinds[a], dims[a], 2 * chip[0] + chip[1], c, (ssem, rsem), 3 * a + k, to).wait_recv()

    return pl.pallas_call(
        body, name=name, in_specs=[ANY] * n + [SEM, SEM] + [ANY] * na, out_specs=[ANY] * n,
        out_shape=[pltpu.HBM(f.shape, BF16) for f in fulls],
        input_output_aliases={i: i for i in range(n)},
        compiler_params=SPLIT_COPY,
    )(*_in_hbm(fulls), send_sems, recv_sems, *after)


def _gather_forward(fulls, kinds, dims, name):
    n = len(fulls)

    def body(*refs):
        outs = refs[n:2 * n]
        sems = refs[2 * n:]
        x, y, c, chips = _mesh_pos()
        sib = (x, y, 1 - c)
        cps = []
        for a in range(n):
            for k, chip in enumerate(chips):
                cp = _ici_copy(outs[a], kinds[a], dims[a], 2 * chip[0] + chip[1], c, sems, 3 * a + k, sib)
                cp.start()
                cps.append(cp)
        for a in range(n):
            for k, chip in enumerate(chips):
                _ici_copy(outs[a], kinds[a], dims[a], 2 * chip[0] + chip[1], 1 - c, sems, 3 * a + k, sib).wait_recv()
        for cp in cps:
            cp.wait_send()

    return pl.pallas_call(
        body, name=name, in_specs=[ANY] * n, out_specs=[ANY] * n,
        out_shape=[pltpu.HBM(f.shape, BF16) for f in fulls],
        input_output_aliases={i: i for i in range(n)},
        scratch_shapes=[pltpu.SemaphoreType.DMA((3 * n,)), pltpu.SemaphoreType.DMA((3 * n,))],
    )(*_in_hbm(fulls))


def _pair_copy(src, land, a, x, y, c, sems):
    return pltpu.make_async_remote_copy(
        src_ref=src.at[1 - c], dst_ref=land, send_sem=sems[0].at[a], recv_sem=sems[1].at[a],
        device_id=(x, y, 1 - c), device_id_type=MESH_T)


def _pair_start(grads, lands, name):
    n = len(grads)

    def body(*refs):
        srcs, dsts = refs[2 * n:3 * n], refs[3 * n:4 * n]
        send_sems, recv_sems, token = refs[4 * n:]
        x, y, c, _ = _mesh_pos()
        for a in range(n):
            _pair_copy(srcs[a], dsts[a], a, x, y, c, (send_sems, recv_sems)).start()
        token[...] = jnp.zeros_like(token)

    res = pl.pallas_call(
        body, name=name, in_specs=[ANY] * (2 * n),
        out_specs=[ANY] * (2 * n) + [SEM, SEM, pl.BlockSpec(memory_space=pltpu.VMEM)],
        out_shape=[pltpu.HBM(g.shape, F32) for g in grads]
        + [pltpu.HBM(l.shape, F32) for l in lands]
        + [pltpu.SemaphoreType.DMA((n,)), pltpu.SemaphoreType.DMA((n,)), jax.ShapeDtypeStruct((8, 128), F32)],
        input_output_aliases={i: i for i in range(2 * n)},
        compiler_params=SPLIT_COPY,
    )(*_in_hbm(grads), *_in_hbm(lands))
    return res[:n], res[n:2 * n], res[2 * n], res[2 * n + 1], res[2 * n + 2]


def _pair_wait(grads, lands, send_sems, recv_sems, after, name):
    n, na = len(grads), len(after)

    def body(*refs):
        ssem, rsem = refs[2 * n], refs[2 * n + 1]
        outs = refs[2 * n + 2 + na:]
        x, y, c, _ = _mesh_pos()
        for a in range(n):
            cp = _pair_copy(outs[a], outs[n + a], a, x, y, c, (ssem, rsem))
            cp.wait_send()
            cp.wait_recv()

    res = pl.pallas_call(
        body, name=name, in_specs=[ANY] * (2 * n) + [SEM, SEM] + [ANY] * na, out_specs=[ANY] * (2 * n),
        out_shape=[pltpu.HBM(g.shape, F32) for g in grads]
        + [pltpu.HBM(l.shape, F32) for l in lands],
        input_output_aliases={i: i for i in range(2 * n)},
        compiler_params=SPLIT_COPY,
    )(*_in_hbm(grads), *_in_hbm(lands), send_sems, recv_sems, *after)
    return res[:n], res[n:]


def _pair_sum(g, recv, pos, name):
    _, _, rh, c = g.shape
    tr = _pick_rows(rh, c)

    def body(pos_ref, g_ref, r_ref, o_ref):
        o_ref[...] = (g_ref[...] + r_ref[...]).astype(BF16)

    return pl.pallas_call(
        body, name=name,
        grid_spec=pltpu.PrefetchScalarGridSpec(
            num_scalar_prefetch=1, grid=(3, rh // tr),
            in_specs=[pl.BlockSpec((None, None, tr, c), lambda k, r, p: (p[0], p[2 + k], r, 0)),
                      pl.BlockSpec((None, tr, c), lambda k, r, p: (p[2 + k], r, 0))],
            out_specs=pl.BlockSpec((None, tr, c), lambda k, r, p: (k, r, 0))),
        out_shape=pltpu.HBM((3, rh, c), BF16),
        compiler_params=_cp(("parallel", "parallel")),
    )(pos, *_in_hbm([g, recv]))


def _chip_copy(src, land, a, k, chip, c, sems):
    return pltpu.make_async_remote_copy(
        src_ref=src.at[k], dst_ref=land.at[k], send_sem=sems[0].at[3 * a + k],
        recv_sem=sems[1].at[3 * a + k], device_id=(chip[0], chip[1], c), device_id_type=MESH_T)


def _chip_start(psums, lands, name):
    n = len(psums)

    def body(*refs):
        srcs, dsts = refs[2 * n:3 * n], refs[3 * n:4 * n]
        send_sems, recv_sems, token = refs[4 * n:]
        x, y, c, chips = _mesh_pos()
        for a in range(n):
            for k, chip in enumerate(chips):
                _chip_copy(srcs[a], dsts[a], a, k, chip, c, (send_sems, recv_sems)).start()
        token[...] = jnp.zeros_like(token)

    res = pl.pallas_call(
        body, name=name, in_specs=[ANY] * (2 * n),
        out_specs=[ANY] * (2 * n) + [SEM, SEM, pl.BlockSpec(memory_space=pltpu.VMEM)],
        out_shape=[pltpu.HBM(p.shape, BF16) for p in psums]
        + [pltpu.HBM(l.shape, BF16) for l in lands]
        + [pltpu.SemaphoreType.DMA((3 * n,)), pltpu.SemaphoreType.DMA((3 * n,)), jax.ShapeDtypeStruct((8, 128), F32)],
        input_output_aliases={i: i for i in range(2 * n)},
        compiler_params=SPLIT_COPY,
    )(*_in_hbm(psums), *_in_hbm(lands))
    return res[:n], res[n:2 * n], res[2 * n], res[2 * n + 1], res[2 * n + 2]


def _chip_wait(psums, lands, send_sems, recv_sems, after, name):
    n, na = len(psums), len(after)

    def body(*refs):
        ssem, rsem = refs[2 * n], refs[2 * n + 1]
        outs = refs[2 * n + 2 + na:]
        srcs, dsts = outs[:n], outs[n:]
        x, y, c, chips = _mesh_pos()
        for a in range(n):
            for k, chip in enumerate(chips):
                cp = _chip_copy(srcs[a], dsts[a], a, k, chip, c, (ssem, rsem))
                cp.wait_send()
                cp.wait_recv()

    res = pl.pallas_call(
        body, name=name, in_specs=[ANY] * (2 * n) + [SEM, SEM] + [ANY] * na, out_specs=[ANY] * (2 * n),
        out_shape=[pltpu.HBM(p.shape, BF16) for p in psums]
        + [pltpu.HBM(l.shape, BF16) for l in lands],
        input_output_aliases={i: i for i in range(2 * n)},
        compiler_params=SPLIT_COPY,
    )(*_in_hbm(psums), *_in_hbm(lands), send_sems, recv_sems, *after)
    return res[n:]


def _owner_sum(g, recv_a, recv_b, pos, name):
    _, _, rh, c = g.shape
    tr = _pick_rows(rh, c)

    def body(pos_ref, g_ref, ra_ref, rb_ref, o_ref):
        acc = g_ref[...] + ra_ref[...]
        for k in range(3):
            acc = acc + rb_ref[k].astype(F32)
        o_ref[...] = acc

    return pl.pallas_call(
        body, name=name,
        grid_spec=pltpu.PrefetchScalarGridSpec(
            num_scalar_prefetch=1, grid=(rh // tr,),
            in_specs=[pl.BlockSpec((None, None, tr, c), lambda r, p: (p[0], p[1], r, 0)),
                      pl.BlockSpec((None, tr, c), lambda r, p: (p[1], r, 0)),
                      pl.BlockSpec((3, tr, c), lambda r, p: (0, r, 0))],
            out_specs=pl.BlockSpec((None, tr, c), lambda r, p: (p[0], r, 0))),
        out_shape=pltpu.HBM((2, rh, c), F32),
        compiler_params=_cp(("parallel",)),
    )(pos, *_in_hbm([g, recv_a, recv_b]))


def _sibling_allgather(halves, name):
    n = len(halves)

    def body(*refs):
        outs = refs[n:2 * n]
        send_sems, recv_sems = refs[2 * n:]
        x, y, c, _ = _mesh_pos()
        cps = []
        for a in range(n):
            cp = pltpu.make_async_remote_copy(
                src_ref=outs[a].at[c], dst_ref=outs[a].at[c], send_sem=send_sems.at[a], recv_sem=recv_sems.at[a],
                device_id=(x, y, 1 - c), device_id_type=MESH_T)
            cp.start()
            cps.append(cp)
        for a in range(n):
            cps[a].wait_send()
            pltpu.make_async_remote_copy(
                src_ref=outs[a].at[1 - c], dst_ref=outs[a].at[1 - c], send_sem=send_sems.at[a],
                recv_sem=recv_sems.at[a], device_id=(x, y, 1 - c), device_id_type=MESH_T).wait_recv()

    return pl.pallas_call(
        body, name=name, in_specs=[ANY] * n, out_specs=[ANY] * n,
        out_shape=[pltpu.HBM(h.shape, F32) for h in halves],
        input_output_aliases={i: i for i in range(n)},
        scratch_shapes=[pltpu.SemaphoreType.DMA((n,)), pltpu.SemaphoreType.DMA((n,))],
    )(*_in_hbm(halves))


def _peers(x, y, c):
    rel = [(0, 0, 1), (0, 1, 0), (0, 1, 1), (1, 0, 0), (1, 0, 1), (1, 1, 0), (1, 1, 1)]
    return [((1 - x) if dx else x, (1 - y) if dy else y, (1 - c) if dc else c) for dx, dy, dc in rel]


def _small_copy(src, land, k, peer, slot, sems):
    return pltpu.make_async_remote_copy(src_ref=src, dst_ref=land.at[slot], send_sem=sems[0].at[k],
                                        recv_sem=sems[1].at[k], device_id=peer, device_id_type=MESH_T)


def _small_start(part, land, name):
    def body(p_in, l_in, p_ref, l_ref, send_sems, recv_sems, token):
        x, y, c, _ = _mesh_pos()
        for k, peer in enumerate(_peers(x, y, c)):
            _small_copy(p_ref, l_ref, k, peer, 4 * x + 2 * y + c, (send_sems, recv_sems)).start()
        token[...] = jnp.zeros_like(token)

    return pl.pallas_call(
        body, name=name, in_specs=[ANY, ANY],
        out_specs=[ANY, ANY, SEM, SEM, pl.BlockSpec(memory_space=pltpu.VMEM)],
        out_shape=[pltpu.HBM(part.shape, F32), pltpu.HBM(land.shape, F32), pltpu.SemaphoreType.DMA((7,)),
                   pltpu.SemaphoreType.DMA((7,)), jax.ShapeDtypeStruct((8, 128), F32)],
        input_output_aliases={0: 0, 1: 1},
        compiler_params=SPLIT_COPY,
    )(*_in_hbm([part, land]))


def _small_wait(part, land, send_sems, recv_sems, after, name):
    na = len(after)

    def body(*refs):
        ssem, rsem = refs[2], refs[3]
        p_ref, l_ref = refs[4 + na:]
        x, y, c, _ = _mesh_pos()
        for k, peer in enumerate(_peers(x, y, c)):
            cp = _small_copy(p_ref, l_ref, k, peer, 4 * peer[0] + 2 * peer[1] + peer[2], (ssem, rsem))
            cp.wait_send()
            cp.wait_recv()

    return pl.pallas_call(
        body, name=name, in_specs=[ANY, ANY, SEM, SEM] + [ANY] * na, out_specs=[ANY, ANY],
        out_shape=[pltpu.HBM(part.shape, F32), pltpu.HBM(land.shape, F32)],
        input_output_aliases={0: 0, 1: 1},
        compiler_params=SPLIT_COPY,
    )(*_in_hbm([part, land]), send_sems, recv_sems, *after)


def _pack_small(ln1_g, ln1_b, gln_g, gln_b, ln2_g, ln2_b, ln3_g, ln3_b, b_gates, b_s, w_s):
    rows = [ln1_g, ln1_b, gln_g, gln_b, ln2_g, ln2_b, ln3_g, ln3_b]
    rows = [r.reshape(1, D) for r in rows] + [b_gates.reshape(2, D), b_s.reshape(1, D), jnp.zeros((5, D), F32),
                                             w_s.reshape(128, D)]
    return jnp.concatenate(rows, axis=0)


def _unpack_small(p):
    out = [p[i:i + 1] for i in range(8)]
    return out + [p[8:10].reshape(1, 2 * D), p[10:11].reshape(1, 8, BLK), p[16:144].reshape(1, 8, BLK, BLK)]


GROUPS = (("f1g", "f1u", "f1d"), ("w_in",), ("w_ab", "w_gb", "w_out"), ("f2g", "f2u", "f2d"))


def _local_step(x, pos_f, target, P, weights_of, grads_ready, flush, small_ready):
    invf = ROPE_THETA ** (-jnp.arange(0, DH, 2, dtype=F32) / DH)
    invf = jnp.tile(invf, 4).reshape(1, 128)
    b_s_t = P["gmlp_b_s"].T

    W = dict(weights_of(0, []))
    h1b, xh1, rstd1, a1, b1, h1t = _ffn_fwd(x, W["f1g"], W["f1u"], W["f1d"], P["ln1_g"], P["ln1_b"], "ffn1_fwd",
                                                emit_t=True)
    W.update(weights_of(1, [h1b]))
    qkv_c = _proj_qkv_rope(h1b, W["w_in"], pos_f, invf, "proj_qkv_rope")
    z = _matmul(h1b, W["w_in"], "nn", "proj_z", n=2 * GW, b_col0=3 * ATT_W, tm=S, tn=512)
    gl = _matmul(h1b, W["w_in"], "nn", "proj_gates", n=2 * D, b_col0=3 * ATT_W + 2 * GW, tm=S, tn=512)
    og = [_attn_fwd(gi, qkv_c[gi], "attn_fwd_g%d" % gi) for gi in range(NG)]
    y_attn, y_attn_t, lse = _attn_combine([o for o, _ in og], [l for _, l in og], "attn_combine")
    y_gmlp, y_gmlp_t = _gmlp_fwd(z, P["gmlp_ln_g"], P["gmlp_ln_b"], P["gmlp_w_s"], b_s_t, "gmlp_fwd")
    W.update(weights_of(2, [y_gmlp]))
    br_a = _matmul(y_attn, W["w_ab"], "nn", "branch_attn", n=D, tm=1024, tn=D)
    br_b = _matmul(y_gmlp, W["w_gb"], "nn", "branch_gmlp", n=D, tm=1024, tn=D)
    merged, merged_t = _merge_fwd(br_a, br_b, gl, P["b_gates"], "merge_fwd")
    h2, h2b, xh2, rstd2 = _resid_ln(xh1, P["ln1_g"], P["ln1_b"], merged, W["w_out"], P["ln2_g"], P["ln2_b"],
                                    "mix_resid_ln2")
    W.update(weights_of(3, [h2b]))
    dr3, a2, b2, dg3, db3, loss = _ffn_fwd(h2, W["f2g"], W["f2u"], W["f2d"], P["ln3_g"], P["ln3_b"],
                                           "ffn2_fwd_loss", target=target)

    g_f2g, g_f2u, g_f2d, dh2 = _ffn_bwd(dr3, h2b, a2, b2, W["f2g"], W["f2u"], W["f2d"], "ffn2_bwd")
    tok = grads_ready(3, dict(f2g=g_f2g, f2u=g_f2u, f2d=g_f2d))
    dr2, dg2, db2 = _ln_bwd(dh2, xh2, rstd2, P["ln2_g"], "ln2_bwd", after=tok)
    g_wout = _wgrad(merged_t, dr2, 128, D, "dw_out", row_sharded=True)
    dmerged = _matmul(dr2, W["w_out"], "nt", "dmerged", n=D, tm=1024, tn=D)
    dab, dbb, dglb, dbg = _merge_bwd(dmerged, br_a, br_b, gl, P["b_gates"], "merge_bwd")
    tok = flush([dab])
    g_wab = _wgrad(y_attn_t, dab, GRP_W // 2, 256, "dw_attn_branch", row_sharded=False, after=tok)
    g_wgb = _wgrad(y_gmlp_t, dbb, 128, D, "dw_gmlp_branch", row_sharded=True)
    tok = grads_ready(2, dict(w_ab=g_wab, w_gb=g_wgb, w_out=g_wout))
    dy_attn = _matmul(dab, W["w_ab"], "nt", "dy_attn", n=GRP_W, tm=1024, tn=GRP_W, after=tok)
    dy_gmlp = _matmul(dbb, W["w_gb"], "nt", "dy_gmlp", n=GW, tm=1024, tn=GW)
    dzb, dws, dbs_t, dgln_g, dgln_b = _gmlp_bwd(z, dy_gmlp, P["gmlp_ln_g"], P["gmlp_ln_b"], P["gmlp_w_s"], b_s_t,
                                                 "gmlp_bwd")
    cls = _class_order([dy_attn, y_attn, lse], "attn_class_order")
    dqkv_c = []
    for gi in range(NG):
        dy_c, y_c, lse_c = [t[None] if gi == 0 else cls[2 * a + gi - 1] for a, t in enumerate((dy_attn, y_attn, lse))]
        dqkv_c.append(_attn_bwd(gi, qkv_c[gi], dy_c, y_c, lse_c, "attn_bwd_g%d" % gi))
    dqkvb = _rope_bwd(dqkv_c, pos_f, invf, "rope_bwd")
    dproj = jnp.concatenate([dqkvb, dzb, dglb], axis=1)
    tok = flush([dproj])
    g_win = _wgrad(h1t, dproj, D // 2, IN_SH, "dw_in", row_sharded=False, after=tok)
    tok = grads_ready(1, dict(w_in=g_win))
    dr1, dg1, db1 = _dh1_ln_bwd(dproj, W["w_in"], dr2, xh1, rstd1, P["ln1_g"], "dh1_ln1_bwd", after=tok)
    tok = flush([dr1])
    tok = tok + small_ready(_pack_small(dg1, db1, dgln_g, dgln_b, dg2, db2, dg3, db3, dbg, dbs_t.T, dws))
    g_f1g, g_f1u, g_f1d, dx = _ffn_bwd(dr1, x.astype(BF16), a1, b1, W["f1g"], W["f1u"], W["f1d"], "ffn1_bwd",
                                       after=tok)
    grads_ready(0, dict(f1g=g_f1g, f1u=g_f1u, f1d=g_f1d))
    flush([dx])
    return loss, dx


TRANSPOSED = ("f1g", "f1u", "f2g", "f2u")
KIND = dict(f1g="stack", f1u="stack", f1d="stack", w_in="col", w_ab="col", w_gb="stack", w_out="stack",
            f2g="stack", f2u="stack", f2d="stack")


def kernel(x, positions, ffn1_w_gate, ffn1_w_up, ffn1_w_down, ln1_g, ln1_b, w_in, b_gates, gmlp_ln_g, gmlp_ln_b, gmlp_w_s, gmlp_b_s, w_attn_branch, w_gmlp_branch, w_out, ln2_g, ln2_b, ffn2_w_gate, ffn2_w_up, ffn2_w_down, ln3_g, ln3_b, loss_target, m_ffn1_w_gate, m_ffn1_w_up, m_ffn1_w_down, m_ln1_g, m_ln1_b, m_w_in, m_b_gates, m_gmlp_ln_g, m_gmlp_ln_b, m_gmlp_w_s, m_gmlp_b_s, m_w_attn_branch, m_w_gmlp_branch, m_w_out, m_ln2_g, m_ln2_b, m_ffn2_w_gate, m_ffn2_w_up, m_ffn2_w_down, m_ln3_g, m_ln3_b, v_ffn1_w_gate, v_ffn1_w_up, v_ffn1_w_down, v_ln1_g, v_ln1_b, v_w_in, v_b_gates, v_gmlp_ln_g, v_gmlp_ln_b, v_gmlp_w_s, v_gmlp_b_s, v_w_attn_branch, v_w_gmlp_branch, v_w_out, v_ln2_g, v_ln2_b, v_ffn2_w_gate, v_ffn2_w_up, v_ffn2_w_down, v_ln3_g, v_ln3_b):
    cx, cy, cc = lax.axis_index("x"), lax.axis_index("y"), lax.axis_index("c")
    pos = jnp.stack([cc, 2 * cx + cy, 2 * (1 - cx) + cy, 2 * cx + 1 - cy, 2 * (1 - cx) + 1 - cy]).astype(jnp.int32)

    w_sh = dict(f1g=ffn1_w_gate, f1u=ffn1_w_up, f1d=ffn1_w_down, w_in=w_in, w_ab=w_attn_branch,
                w_gb=w_gmlp_branch, w_out=w_out, f2g=ffn2_w_gate, f2u=ffn2_w_up, f2d=ffn2_w_down)
    m_sh = dict(f1g=m_ffn1_w_gate, f1u=m_ffn1_w_up, f1d=m_ffn1_w_down, w_in=m_w_in, w_ab=m_w_attn_branch,
                w_gb=m_w_gmlp_branch, w_out=m_w_out, f2g=m_ffn2_w_gate, f2u=m_ffn2_w_up, f2d=m_ffn2_w_down)
    v_sh = dict(f1g=v_ffn1_w_gate, f1u=v_ffn1_w_up, f1d=v_ffn1_w_down, w_in=v_w_in, w_ab=v_w_attn_branch,
                w_gb=v_w_gmlp_branch, w_out=v_w_out, f2g=v_ffn2_w_gate, f2u=v_ffn2_w_up, f2d=v_ffn2_w_down)
    w_sh = {k: (v[0].T if k in TRANSPOSED else v[0]) for k, v in w_sh.items()}
    m_sh = {k: (v[0].T if k in TRANSPOSED else v[0]) for k, v in m_sh.items()}
    v_sh = {k: (v[0].T if k in TRANSPOSED else v[0]) for k, v in v_sh.items()}

    started, tokens = [], []
    for gi, names in enumerate(GROUPS):
        placed = [_place_shard(w_sh[k], KIND[k], pos, "place_" + k) for k in names]
        fulls, ssem, rsem, token = _gather_start(placed, [KIND[k] for k in names], [w_sh[k].shape for k in names],
                                                 tokens[-1:], "gather_start_g%d" % gi)
        started.append((fulls, ssem, rsem))
        tokens.append(token)

    def weights_of(gi, after):
        names = GROUPS[gi]
        kinds, dims = [KIND[k] for k in names], [w_sh[k].shape for k in names]
        fulls, ssem, rsem = started[gi]
        fulls = _gather_wait(fulls, ssem, rsem, kinds, dims, list(after) + (tokens if gi == 0 else []),
                             "gather_wait_g%d" % gi)
        fulls = _gather_forward(fulls, kinds, dims, "gather_forward_g%d" % gi)
        return {k: (f.reshape(D, D) if k in ("w_gb", "w_out") else f) for k, f in zip(names, fulls)}

    pending, inflight = [], {}

    def grads_ready(gi, gd):
        grads = [gd[k] for k in GROUPS[gi]]
        lands = [lax.empty(g.shape[1:], F32) for g in grads]
        grads, lands, ssem, rsem, token = _pair_start(grads, lands, "rs_pair_start_g%d" % gi)
        pending.append((gi, grads, lands, ssem, rsem))
        return [token]

    def flush(after):
        gi, grads, lands, ssem, rsem = pending.pop()
        names = GROUPS[gi]
        grads, recv_a = _pair_wait(grads, lands, ssem, rsem, after, "rs_pair_wait_g%d" % gi)
        psums = [_pair_sum(g, r, pos, "rs_pair_sum_" + k) for g, r, k in zip(grads, recv_a, names)]
        lands = [lax.empty((3,) + p.shape[1:], BF16) for p in psums]
        psums, lands, ssem, rsem, token = _chip_start(psums, lands, "rs_chip_start_g%d" % gi)
        inflight[gi] = (grads, recv_a, psums, lands, ssem, rsem, token)
        return [token]

    P = dict(ln1_g=ln1_g, ln1_b=ln1_b, ln2_g=ln2_g, ln2_b=ln2_b, ln3_g=ln3_g, ln3_b=ln3_b, b_gates=b_gates,
             gmlp_ln_g=gmlp_ln_g, gmlp_ln_b=gmlp_ln_b, gmlp_w_s=gmlp_w_s[0], gmlp_b_s=gmlp_b_s[0])
    pos_f = positions.reshape(S, 1).astype(F32)
    small_state = []

    def small_ready(packed):
        land = jnp.zeros((8, SMALL_ROWS, D), F32)
        packed, land, ssem, rsem, token = _small_start(packed, land, "small_start")
        small_state.append((packed, land, ssem, rsem))
        return [token]

    loss_part, dx = _local_step(x[0], pos_f, loss_target[0], P, weights_of, grads_ready, flush, small_ready)
    loss = lax.psum(loss_part[0, 0], ("x", "y", "c"))

    g_out, d_out, m_out, v_out = {}, {}, {}, {}

    def finish(gis, after, tag):
        names, halves = [], []
        for gi in gis:
            grads, recv_a, psums, lands, ssem, rsem, token = inflight[gi]
            recv_b = _chip_wait(psums, lands, ssem, rsem, after + [inflight[0][6]], "rs_chip_wait_g%d" % gi)
            halves += [_owner_sum(g, ra, rb, pos, "rs_owner_sum_" + k)
                       for g, ra, rb, k in zip(grads, recv_a, recv_b, GROUPS[gi])]
            names += GROUPS[gi]
            after = halves[-1:]
        reduced = _sibling_allgather(halves, "rs_sibling_allgather_" + tag)
        for k, gfull in zip(names, reduced):
            res = _adamw(w_sh[k], gfull.reshape(w_sh[k].shape), m_sh[k], v_sh[k], "adamw_" + k)
            after = [res[1]]
            if k in TRANSPOSED:
                res = [r.T for r in res]
            g_out[k], d_out[k], m_out[k], v_out[k] = [r[None] for r in res]
        return after

    after = finish((3, 2, 1), [], "g321")

    small, parts = _small_wait(*small_state[0], after, "small_wait")
    sp = (ln1_g, ln1_b, gmlp_ln_g, gmlp_ln_b, ln2_g, ln2_b, ln3_g, ln3_b, b_gates, gmlp_b_s, gmlp_w_s)
    sm = (m_ln1_g, m_ln1_b, m_gmlp_ln_g, m_gmlp_ln_b, m_ln2_g, m_ln2_b, m_ln3_g, m_ln3_b, m_b_gates, m_gmlp_b_s,
          m_gmlp_w_s)
    sv = (v_ln1_g, v_ln1_b, v_gmlp_ln_g, v_gmlp_ln_b, v_ln2_g, v_ln2_b, v_ln3_g, v_ln3_b, v_b_gates, v_gmlp_b_s,
          v_gmlp_w_s)
    sg, sd, smn, svn = _small_sum_adamw(parts, small, pos, _pack_small(*sp), _pack_small(*sm), _pack_small(*sv),
                                        "small_adamw")
    names = ("ln1_g", "ln1_b", "gmlp_ln_g", "gmlp_ln_b", "ln2_g", "ln2_b", "ln3_g", "ln3_b", "b_gates", "gmlp_b_s",
             "gmlp_w_s")
    for dst, packed in ((g_out, sg), (d_out, sd), (m_out, smn), (v_out, svn)):
        for nm, val in zip(names, _unpack_small(packed)):
            dst[nm] = val
    finish((0,), [sg], "g0")

    order = ("f1g", "f1u", "f1d", "ln1_g", "ln1_b", "w_in", "b_gates", "gmlp_ln_g", "gmlp_ln_b", "gmlp_w_s", "gmlp_b_s",
             "w_ab", "w_gb", "w_out", "ln2_g", "ln2_b", "f2g", "f2u", "f2d", "ln3_g", "ln3_b")
    outs = [loss, dx[None]]
    for dst in (g_out, d_out, m_out, v_out):
        outs += [dst[k] for k in order]
    return tuple(outs)
```

```python
import jax
import jax.numpy as jnp
from jax import lax
from jax.experimental import pallas as pl
from jax.experimental.pallas import tpu as pltpu

F32 = jnp.float32
BF16 = jnp.bfloat16

S = 2048
D = 1024
NSH = 4
FSH = 704
ATT_W = 1536
GRP_W = 512
NG = 3
NH = 8
DH = 64
BLK = 128
NBLK = S // BLK
GW = 1024
IN_W = 8704
IN_SH = IN_W // NSH
ALPHA = 2.0 ** 0.25
LN_EPS = 1e-5
ROPE_THETA = 10000.0
DILATIONS = (1, 4, 16)
ADAM_LR, ADAM_B1, ADAM_B2, ADAM_EPS, ADAM_WD, ADAM_STEP = 0.001, 0.9, 0.999, 1e-08, 0.01, 10
SMALL_ROWS = 144
MESH_T = pl.DeviceIdType.MESH
MIB = 1024 * 1024
NEG_INF = float("-inf")


def _cp(sem, vmem_mib=48):
    return pltpu.CompilerParams(dimension_semantics=sem, vmem_limit_bytes=vmem_mib * MIB)


def _ln_stats(r):
    mu = jnp.mean(r, axis=-1, keepdims=True)
    xc = r - mu
    var = jnp.mean(xc * xc, axis=-1, keepdims=True)
    rstd = lax.rsqrt(var + LN_EPS)
    return xc * rstd, rstd


def _ln_dx(dxh, xh, rstd):
    m1 = jnp.mean(dxh, axis=-1, keepdims=True)
    m2 = jnp.mean(dxh * xh, axis=-1, keepdims=True)
    return rstd * (dxh - m1 - xh * m2)


def _dot_nt(a, b):
    return lax.dot_general(a, b, (((1,), (1,)), ((), ())), preferred_element_type=F32)


def _dot_tn(a, b):
    return lax.dot_general(a, b, (((0,), (0,)), ((), ())), preferred_element_type=F32)


def _dot(a, b):
    return jnp.dot(a, b, preferred_element_type=F32)


def _ffn_fwd(xin, wgt, wut, wd, ln_g, ln_b, name, emit_t=False, target=None):
    with_loss = target is not None
    tm = 512 if with_loss else 1024

    def body(x_ref, wg_ref, wu_ref, wd_ref, g_ref, b_ref, *rest):
        if with_loss:
            t_ref, dr_ref, a_ref, bb_ref, dg_ref, db_ref, loss_ref, acc_ref = rest
        elif emit_t:
            hb_ref, xh_ref, rstd_ref, a_ref, bb_ref, ht_ref, acc_ref = rest
        else:
            hb_ref, xh_ref, rstd_ref, a_ref, bb_ref, acc_ref = rest
        i = pl.program_id(0)
        j = pl.program_id(1)
        xb = x_ref[...].astype(BF16)
        a = _dot_nt(xb, wg_ref[...])
        b = _dot_nt(xb, wu_ref[...])
        a_ref[...] = a.astype(BF16)
        bb_ref[...] = b.astype(BF16)
        s = (a * jax.nn.sigmoid(a)) * b
        f = _dot(s.astype(BF16), wd_ref[...])

        @pl.when(j == 0)
        def _():
            acc_ref[...] = f

        @pl.when(j > 0)
        def _():
            acc_ref[...] += f

        @pl.when(j == NSH - 1)
        def _():
            r = ALPHA * x_ref[...] + 0.5 * acc_ref[...]
            xh, rstd = _ln_stats(r)
            h = xh * g_ref[...] + b_ref[...]
            if with_loss:
                err = h - t_ref[...]
                dy = err * (1.0 / D)
                dr_ref[...] = _ln_dx(dy * g_ref[...], xh, rstd)
                dg = jnp.sum(dy * xh, axis=0, keepdims=True)
                db = jnp.sum(dy, axis=0, keepdims=True)
                part = 0.5 * jnp.sum(jnp.mean(err * err, axis=-1, keepdims=True), axis=0, keepdims=True)
                part = jnp.broadcast_to(part, (8, 128))

                @pl.when(i == 0)
                def _():
                    dg_ref[...] = dg
                    db_ref[...] = db
                    loss_ref[...] = part

                @pl.when(i > 0)
                def _():
                    dg_ref[...] += dg
                    db_ref[...] += db
                    loss_ref[...] += part
            else:
                hb_ref[...] = h.astype(BF16)
                xh_ref[...] = xh
                rstd_ref[...] = rstd
                if emit_t:
                    ht_ref[...] = h.T.astype(BF16)

    row = pl.BlockSpec((tm, D), lambda i, j: (i, 0))
    vec = pl.BlockSpec((1, D), lambda i, j: (0, 0))
    wsp = pl.BlockSpec((None, FSH, D), lambda i, j: (j, 0, 0))
    ab = pl.BlockSpec((None, tm, FSH), lambda i, j: (j, i, 0))
    ab_shape = jax.ShapeDtypeStruct((NSH, S, FSH), BF16)
    in_specs, args = [row, wsp, wsp, wsp, vec, vec], (xin, wgt, wut, wd, ln_g, ln_b)
    if with_loss:
        in_specs, args = in_specs + [row], args + (target,)
        out_specs = [row, ab, ab, vec, vec, pl.BlockSpec((8, 128), lambda i, j: (0, 0))]
        out_shape = [jax.ShapeDtypeStruct((S, D), F32), ab_shape, ab_shape, jax.ShapeDtypeStruct((1, D), F32),
                     jax.ShapeDtypeStruct((1, D), F32), jax.ShapeDtypeStruct((8, 128), F32)]
    else:
        out_specs = [row, row, pl.BlockSpec((tm, 1), lambda i, j: (i, 0)), ab, ab]
        out_shape = [jax.ShapeDtypeStruct((S, D), BF16), jax.ShapeDtypeStruct((S, D), F32),
                     jax.ShapeDtypeStruct((S, 1), F32), ab_shape, ab_shape]
        if emit_t:
            out_specs.append(pl.BlockSpec((D, tm), lambda i, j: (0, i)))
            out_shape.append(jax.ShapeDtypeStruct((D, S), BF16))
    return pl.pallas_call(
        body, name=name, grid=(S // tm, NSH), in_specs=in_specs, out_specs=out_specs, out_shape=out_shape,
        scratch_shapes=[pltpu.VMEM((tm, D), F32)],
        compiler_params=_cp(("arbitrary" if with_loss else "parallel", "arbitrary"), vmem_mib=56),
    )(*args)


def _ffn_bwd(dr, xin_b, a, b, wgt, wut, wd, name, after=()):
    tm = 512
    ni = S // tm
    hr = FSH // 2

    def body(dr_ref, a_ref, b_ref, wg_ref, wu_ref, wd_ref, x_hbm, *rest):
        dwg_hbm, dwu_hbm, dwd_hbm, dx_hbm, dx_acc, da_all, db_all, s_all, df_all, x_all, res_buf, sems = rest[len(after):]
        j = pl.program_id(0)
        i = pl.program_id(1)
        rows = pl.ds(pl.multiple_of(i * tm, tm), tm)

        @pl.when(jnp.logical_and(j == 0, i == 0))
        def _():
            cp = pltpu.make_async_copy(x_hbm, x_all, sems.at[0])
            cp.start()
            cp.wait()

        drv = dr_ref[...]
        df = (0.5 * drv).astype(BF16)

        @pl.when(j == 0)
        def _():
            df_all[rows, :] = df

        ds = jnp.concatenate([_dot_nt(df, wd_ref[0:384, :]), _dot_nt(df, wd_ref[384:FSH, :])], axis=1)
        av = a_ref[...].astype(F32)
        bv = b_ref[...].astype(F32)
        sig = jax.nn.sigmoid(av)
        sl = av * sig
        da = (ds * bv * (sig * (1.0 + av * (1.0 - sig)))).astype(BF16)
        db = (ds * sl).astype(BF16)
        da_all[rows, :] = da
        db_all[rows, :] = db
        s_all[rows, :] = (sl * bv).astype(BF16)
        dx = _dot(da, wg_ref[...]) + _dot(db, wu_ref[...])

        @pl.when(j == 0)
        def _():
            dx_acc[rows, :] = ALPHA * drv + dx

        @pl.when(j > 0)
        def _():
            dx_acc[rows, :] += dx

        @pl.when(i == ni - 1)
        def _():
            copies = []
            for n, (lhs, rhs, out) in enumerate(((da_all, x_all, dwg_hbm), (db_all, x_all, dwu_hbm),
                                                 (s_all, df_all, dwd_hbm))):
                slot = n % 2
                if n >= 2:
                    for cp in copies[2 * (n - 2): 2 * (n - 2) + 2]:
                        cp.wait()
                res_buf[slot] = _dot_tn(lhs[...], rhs[...])
                for h in range(2):
                    cp = pltpu.make_async_copy(res_buf.at[slot, pl.ds(h * hr, hr), :], out.at[h, j],
                                               sems.at[1 + 2 * slot + h])
                    cp.start()
                    copies.append(cp)
            for cp in copies[2:]:
                cp.wait()

        @pl.when(jnp.logical_and(j == NSH - 1, i == ni - 1))
        def _():
            cp = pltpu.make_async_copy(dx_acc, dx_hbm, sems.at[0])
            cp.start()
            cp.wait()

    row = pl.BlockSpec((tm, D), lambda j, i: (i, 0))
    wsp = pl.BlockSpec((None, FSH, D), lambda j, i: (j, 0, 0))
    ab = pl.BlockSpec((None, tm, FSH), lambda j, i: (j, i, 0))
    dwshape = jax.ShapeDtypeStruct((2, NSH, hr, D), F32)
    return pl.pallas_call(
        body, name=name, grid=(NSH, ni),
        in_specs=[row, ab, ab, wsp, wsp, wsp, ANY] + [ANY] * len(after),
        out_specs=[ANY, ANY, ANY, ANY],
        out_shape=[dwshape, dwshape, dwshape, jax.ShapeDtypeStruct((S, D), F32)],
        scratch_shapes=[pltpu.VMEM((S, D), F32), pltpu.VMEM((S, FSH), BF16), pltpu.VMEM((S, FSH), BF16),
                        pltpu.VMEM((S, FSH), BF16), pltpu.VMEM((S, D), BF16), pltpu.VMEM((S, D), BF16),
                        pltpu.VMEM((2, FSH, D), F32), pltpu.SemaphoreType.DMA((5,))],
        compiler_params=_cp(("arbitrary", "arbitrary"), vmem_mib=58),
    )(dr, a, b, wgt, wut, wd, xin_b, *after)


def _matmul(a, b, mode, name, *, n, tm, tn, b_col0=0, after=()):
    m, k = a.shape
    assert m % tm == 0 and n % tn == 0 and b_col0 % tn == 0
    off = b_col0 // tn
    na = len(after)

    def body(*refs):
        a_ref, b_ref, o_ref = refs[na:]
        av = a_ref[...].astype(BF16)
        o_ref[...] = _dot(av, b_ref[...]) if mode == "nn" else _dot_nt(av, b_ref[...])

    if mode == "nn":
        b_spec = pl.BlockSpec((k, tn), lambda i, j: (0, j + off))
    else:
        b_spec = pl.BlockSpec((tn, k), lambda i, j: (j, 0))
    return pl.pallas_call(
        body, name=name, grid=(m // tm, n // tn),
        in_specs=[pl.BlockSpec(memory_space=pl.ANY)] * na + [pl.BlockSpec((tm, k), lambda i, j: (i, 0)), b_spec],
        out_specs=pl.BlockSpec((tm, tn), lambda i, j: (i, j)),
        out_shape=jax.ShapeDtypeStruct((m, n), F32),
        compiler_params=_cp(("parallel", "parallel")),
    )(*after, a, b)


def _wgrad(xt, y, rh, c, name, row_sharded, after=()):
    na = len(after)
    if row_sharded:
        def body(x_ref, y_ref, *rest):
            o_ref = rest[na]
            res = _dot(x_ref[...], y_ref[...].astype(BF16))
            for j in range(NSH):
                for h in range(2):
                    o_ref[h, j] = res[(2 * j + h) * rh:(2 * j + h + 1) * rh, :]

        grid = (1,)
        in_specs = [pl.BlockSpec((2 * NSH * rh, S), lambda g: (0, 0)), pl.BlockSpec((S, c), lambda g: (0, 0))]
        out_specs = pl.BlockSpec((2, NSH, rh, c), lambda g: (0, 0, 0, 0))
        sem = ("arbitrary",)
    else:
        def body(x_ref, y_ref, *rest):
            rest[na][...] = _dot(x_ref[...], y_ref[...].astype(BF16))

        grid = (2, NSH)
        in_specs = [pl.BlockSpec((rh, S), lambda h, j: (h, 0)), pl.BlockSpec((S, c), lambda h, j: (0, j))]
        out_specs = pl.BlockSpec((None, None, rh, c), lambda h, j: (h, j, 0, 0))
        sem = ("parallel", "parallel")
    return pl.pallas_call(
        body, name=name, grid=grid, in_specs=in_specs + [pl.BlockSpec(memory_space=pl.ANY)] * na, out_specs=out_specs,
        out_shape=jax.ShapeDtypeStruct((2, NSH, rh, c), F32),
        compiler_params=_cp(sem, vmem_mib=56),
    )(xt, y, *after)


def _resid_ln(res_xh, res_g, res_b, a, w, ln_g, ln_b, name):
    tm = 512

    def body(rx_ref, rg_ref, rb_ref, a_ref, w_ref, g_ref, b_ref, h_ref, hb_ref, xh_ref, rstd_ref):
        r = ALPHA * (rx_ref[...] * rg_ref[...] + rb_ref[...]) + _dot(a_ref[...], w_ref[...])
        xh, rstd = _ln_stats(r)
        h = xh * g_ref[...] + b_ref[...]
        h_ref[...] = h
        hb_ref[...] = h.astype(BF16)
        xh_ref[...] = xh
        rstd_ref[...] = rstd

    row = pl.BlockSpec((tm, D), lambda i: (i, 0))
    vec = pl.BlockSpec((1, D), lambda i: (0, 0))
    return pl.pallas_call(
        body, name=name, grid=(S // tm,),
        in_specs=[row, vec, vec, row, pl.BlockSpec((D, D), lambda i: (0, 0)), vec, vec],
        out_specs=[row, row, row, pl.BlockSpec((tm, 1), lambda i: (i, 0))],
        out_shape=[jax.ShapeDtypeStruct((S, D), F32), jax.ShapeDtypeStruct((S, D), BF16),
                   jax.ShapeDtypeStruct((S, D), F32), jax.ShapeDtypeStruct((S, 1), F32)],
        compiler_params=_cp(("parallel",)),
    )(res_xh, res_g, res_b, a, w, ln_g, ln_b)


def _dh1_ln_bwd(dproj, w_in, dr2, xh, rstd, ln_g, name, after=()):
    tm, tk = 512, IN_SH
    nk = IN_W // tk
    na = len(after)

    def body(*refs):
        a_ref, b_ref, add_ref, xh_ref, rstd_ref, g_ref, dr_ref, dg_ref, db_ref, acc_ref = refs[na:]
        i = pl.program_id(0)
        k = pl.program_id(1)
        p = _dot_nt(a_ref[...], b_ref[...])

        @pl.when(k == 0)
        def _():
            acc_ref[...] = p

        @pl.when(k > 0)
        def _():
            acc_ref[...] += p

        @pl.when(k == nk - 1)
        def _():
            dy = acc_ref[...] + ALPHA * add_ref[...]
            xhv = xh_ref[...]
            dr_ref[...] = _ln_dx(dy * g_ref[...], xhv, rstd_ref[...])
            dg = jnp.sum(dy * xhv, axis=0, keepdims=True)
            db = jnp.sum(dy, axis=0, keepdims=True)

            @pl.when(i == 0)
            def _():
                dg_ref[...] = dg
                db_ref[...] = db

            @pl.when(i > 0)
            def _():
                dg_ref[...] += dg
                db_ref[...] += db

    row = pl.BlockSpec((tm, D), lambda i, k: (i, 0))
    vec = pl.BlockSpec((1, D), lambda i, k: (0, 0))
    return pl.pallas_call(
        body, name=name, grid=(S // tm, nk),
        in_specs=[pl.BlockSpec(memory_space=pl.ANY)] * na
        + [pl.BlockSpec((tm, tk), lambda i, k: (i, k)), pl.BlockSpec((D, tk), lambda i, k: (0, k)), row, row,
           pl.BlockSpec((tm, 1), lambda i, k: (i, 0)), vec],
        out_specs=[row, vec, vec],
        out_shape=[jax.ShapeDtypeStruct((S, D), F32), jax.ShapeDtypeStruct((1, D), F32),
                   jax.ShapeDtypeStruct((1, D), F32)],
        scratch_shapes=[pltpu.VMEM((tm, D), F32)],
        compiler_params=_cp(("arbitrary", "arbitrary"), vmem_mib=56),
    )(*after, dproj, w_in, dr2, xh, rstd, ln_g)


def _ln_bwd(dout, xh, rstd, ln_g, name, after=()):
    tm = 512
    na = len(after)

    def body(*refs):
        y_ref, xh_ref, rstd_ref, g_ref, dr_ref, dg_ref, db_ref = refs[na:]
        dy = y_ref[...]
        i = pl.program_id(0)
        xh = xh_ref[...]
        dr_ref[...] = _ln_dx(dy * g_ref[...], xh, rstd_ref[...])
        dg = jnp.sum(dy * xh, axis=0, keepdims=True)
        db = jnp.sum(dy, axis=0, keepdims=True)

        @pl.when(i == 0)
        def _():
            dg_ref[...] = dg
            db_ref[...] = db

        @pl.when(i > 0)
        def _():
            dg_ref[...] += dg
            db_ref[...] += db

    row = pl.BlockSpec((tm, D), lambda i: (i, 0))
    vec = pl.BlockSpec((1, D), lambda i: (0, 0))
    return pl.pallas_call(
        body, name=name, grid=(S // tm,),
        in_specs=[pl.BlockSpec(memory_space=pl.ANY)] * na + [row, row, pl.BlockSpec((tm, 1), lambda i: (i, 0)), vec],
        out_specs=[row, vec, vec],
        out_shape=[jax.ShapeDtypeStruct((S, D), F32), jax.ShapeDtypeStruct((1, D), F32),
                   jax.ShapeDtypeStruct((1, D), F32)],
        compiler_params=_cp(("arbitrary",)),
    )(*after, dout, xh, rstd, ln_g)


ROPE_TM = 256


def _rope_tables(pos_ref, invf_ref, sign):
    ang = pos_ref[...] * invf_ref[...]
    lane = lax.broadcasted_iota(jnp.int32, ang.shape, 1)
    first = (lane % DH) < (DH // 2)
    sinv = jnp.sin(ang) * sign
    return first, jnp.cos(ang), jnp.where(first, -sinv, sinv)


def _rotate(x, first, cosf, sinf):
    return x * cosf + jnp.where(first, pltpu.roll(x, 96, 1), pltpu.roll(x, 32, 1)) * sinf


def _proj_qkv_rope(hb, w_in, pos_f, invf, name):
    tm = 2 * ROPE_TM

    def body(h_ref, w_ref, pos_ref, invf_ref, o0_ref, o1_ref, o2_ref, buf_ref):
        rot = pl.program_id(1) < 2
        first, cosf, sinf = _rope_tables(pos_ref, invf_ref, 1.0)
        cosf = jnp.where(rot, cosf, 1.0)
        sinf = jnp.where(rot, sinf, 0.0)
        acc = _dot(h_ref[...], w_ref[...])
        for gi, (d, o_ref) in enumerate(zip(DILATIONS, (o0_ref, o1_ref, o2_ref))):
            for ch in range(GRP_W // 128):
                cols = slice(ch * 128, (ch + 1) * 128)
                x = _rotate(acc[:, gi * GRP_W + ch * 128: gi * GRP_W + (ch + 1) * 128], first, cosf, sinf)
                if d == 1:
                    o_ref[0, :, cols] = x.astype(BF16)
                else:
                    buf_ref[...] = x
                    for r in range(d):
                        o_ref[r, :, cols] = buf_ref[pl.ds(r, tm // d, stride=d), :].astype(BF16)

    return pl.pallas_call(
        body, name=name, grid=(S // tm, 3),
        in_specs=[pl.BlockSpec((tm, D), lambda i, s: (i, 0)), pl.BlockSpec((D, ATT_W), lambda i, s: (0, s)),
                  pl.BlockSpec((tm, 1), lambda i, s: (i, 0)), pl.BlockSpec((1, 128), lambda i, s: (0, 0))],
        out_specs=[pl.BlockSpec((d, tm // d, GRP_W), lambda i, s: (0, i, s)) for d in DILATIONS],
        out_shape=[jax.ShapeDtypeStruct((d, S // d, 3 * GRP_W), BF16) for d in DILATIONS],
        scratch_shapes=[pltpu.VMEM((tm, 128), F32)],
        compiler_params=_cp(("parallel", "parallel")),
    )(hb, w_in, pos_f, invf)


def _rope_bwd(dqkv_c, pos_f, invf, name):
    tm = ROPE_TM

    def body(*refs):
        g_refs, (pos_ref, invf_ref, o_ref, buf_ref) = refs[:9], refs[9:]
        first, cosf, sinf = _rope_tables(pos_ref, invf_ref, -1.0)
        for sec in range(3):
            for gi, d in enumerate(DILATIONS):
                g_ref = g_refs[3 * gi + sec]
                for ch in range(GRP_W // 128):
                    cols = slice(ch * 128, (ch + 1) * 128)
                    if d == 1:
                        x = g_ref[0, :, cols]
                    else:
                        for r in range(d):
                            buf_ref[pl.ds(r, tm // d, stride=d), :] = g_ref[r, :, cols]
                        x = buf_ref[...]
                    if sec < 2:
                        x = _rotate(x, first, cosf, sinf)
                    dst = sec * ATT_W + gi * GRP_W + ch * 128
                    o_ref[:, dst:dst + 128] = x.astype(BF16)

    g_specs = [pl.BlockSpec((d, tm // d, GRP_W), lambda i: (0, i, 0)) for d in DILATIONS for _ in range(3)]
    return pl.pallas_call(
        body, name=name, grid=(S // tm,),
        in_specs=g_specs + [pl.BlockSpec((tm, 1), lambda i: (i, 0)), pl.BlockSpec((1, 128), lambda i: (0, 0))],
        out_specs=pl.BlockSpec((tm, 3 * ATT_W), lambda i: (i, 0)),
        out_shape=jax.ShapeDtypeStruct((S, 3 * ATT_W), BF16),
        scratch_shapes=[pltpu.VMEM((tm, 128), F32)],
        compiler_params=_cp(("parallel",)),
    )(*[g for grp in dqkv_c for g in grp], pos_f, invf)


def _class_order(ts, name):
    tm = ROPE_TM
    n = len(ts)

    def body(*refs):
        buf_ref = refs[3 * n]
        for a in range(n):
            for ch in range(GRP_W // 128):
                cols = slice(ch * 128, (ch + 1) * 128)
                buf_ref[...] = refs[a][:, cols]
                for b, d in enumerate(DILATIONS[1:]):
                    for r in range(d):
                        refs[n + 2 * a + b][r, :, cols] = buf_ref[pl.ds(r, tm // d, stride=d), :]

    return pl.pallas_call(
        body, name=name, grid=(S // tm,),
        in_specs=[pl.BlockSpec((tm, GRP_W), lambda i: (i, 0))] * n,
        out_specs=[pl.BlockSpec((d, tm // d, GRP_W), lambda i: (0, i, 0)) for _ in range(n) for d in DILATIONS[1:]],
        out_shape=[jax.ShapeDtypeStruct((d, S // d, GRP_W), F32) for _ in range(n) for d in DILATIONS[1:]],
        scratch_shapes=[pltpu.VMEM((tm, 128), F32)],
        compiler_params=_cp(("parallel",)),
    )(*ts)


def _own_lanes(h):
    return (lax.broadcasted_iota(jnp.int32, (1, 2 * DH), 1) // DH) == (h % 2)


def _heads(ref):
    out = []
    for h in range(NH):
        pair = ref[:, (h // 2) * 2 * DH:(h // 2 + 1) * 2 * DH]
        out.append(jnp.where(_own_lanes(h), pair, jnp.zeros_like(pair)))
    return jnp.stack(out)


def _unheads(t3):
    return jnp.concatenate([t3[2 * p] + t3[2 * p + 1] for p in range(NH // 2)], axis=1)


def _bdot_nt(a, b):
    return lax.dot_general(a, b, (((2,), (2,)), ((0,), (0,))), preferred_element_type=F32)


def _bdot(a, b):
    return lax.dot_general(a, b, (((2,), (1,)), ((0,), (0,))), preferred_element_type=F32)


def _bdot_tn(a, b):
    return lax.dot_general(a, b, (((1,), (1,)), ((0,), (0,))), preferred_element_type=F32)


def _attn_fwd(gi, qkv_c, name):
    d = DILATIONS[gi]
    nblk = S // d // BLK

    def body(*refs):
        if nblk > 1:
            q_ref, kc_ref, kp_ref, vc_ref, vp_ref, o_ref, lse_ref = refs
            has_prev = pl.program_id(1) != 0
        else:
            q_ref, kc_ref, vc_ref, o_ref, lse_ref = refs
        qi = lax.broadcasted_iota(jnp.int32, (NH, BLK, BLK), 1)
        kj = lax.broadcasted_iota(jnp.int32, (NH, BLK, BLK), 2)
        q = _heads(q_ref)
        sc = jnp.where(kj <= qi, _bdot_nt(q, _heads(kc_ref)) * 0.125, NEG_INF)
        m = jnp.max(sc, axis=-1, keepdims=True)
        if nblk > 1:
            mask_p = jnp.logical_and(kj >= qi, has_prev)
            sp = jnp.where(mask_p, _bdot_nt(q, _heads(kp_ref)) * 0.125, NEG_INF)
            m = jnp.maximum(m, jnp.max(sp, axis=-1, keepdims=True))
        pc = jnp.exp(sc - m)
        l = jnp.sum(pc, axis=-1, keepdims=True)
        o = _bdot(pc.astype(BF16), _heads(vc_ref))
        if nblk > 1:
            pp = jnp.exp(sp - m)
            l = l + jnp.sum(pp, axis=-1, keepdims=True)
            o = o + _bdot(pp.astype(BF16), _heads(vp_ref))
        o_ref[...] = _unheads(o / l)
        lse = jnp.broadcast_to(m + jnp.log(l), (NH, BLK, 2 * DH))
        lse_ref[...] = _unheads(jnp.stack([jnp.where(_own_lanes(h), lse[h], 0.0) for h in range(NH)]))

    def cur(sec):
        return pl.BlockSpec((None, BLK, GRP_W), lambda r, n: (r, n, sec))

    def prev(sec):
        return pl.BlockSpec((None, BLK, GRP_W), lambda r, n: (r, jnp.maximum(n - 1, 0), sec))

    out = pl.BlockSpec((None, BLK, GRP_W), lambda r, n: (r, n, 0))
    shp = jax.ShapeDtypeStruct((d, S // d, GRP_W), F32)
    if nblk > 1:
        in_specs, args = [cur(0), cur(1), prev(1), cur(2), prev(2)], (qkv_c,) * 5
    else:
        in_specs, args = [cur(0), cur(1), cur(2)], (qkv_c,) * 3
    return pl.pallas_call(
        body, name=name, grid=(d, nblk), in_specs=in_specs, out_specs=[out, out], out_shape=[shp, shp],
        compiler_params=_cp(("parallel", "parallel")),
    )(*args)


def _attn_combine(os, lses, name):
    tm = ROPE_TM

    def body(o0_ref, o1_ref, o2_ref, l0_ref, l1_ref, l2_ref, y_ref, yt_ref, l_ref, buf_ref):
        def token_order(ref, d, cols, slot):
            if d == 1:
                return ref[0, :, cols]
            for r in range(d):
                buf_ref[slot, pl.ds(r, tm // d, stride=d), :] = ref[r, :, cols]
            return buf_ref[slot]

        for ch in range(GRP_W // 128):
            cols = slice(ch * 128, (ch + 1) * 128)
            o = [token_order(ref, d, cols, k) for k, (ref, d) in enumerate(zip((o0_ref, o1_ref, o2_ref), DILATIONS))]
            ls = [token_order(ref, d, cols, 3 + k)
                  for k, (ref, d) in enumerate(zip((l0_ref, l1_ref, l2_ref), DILATIONS))]
            m = jnp.maximum(jnp.maximum(ls[0], ls[1]), ls[2])
            e = [jnp.exp(l - m) for l in ls]
            den = e[0] + e[1] + e[2]
            y = (e[0] * o[0] + e[1] * o[1] + e[2] * o[2]) / den
            y_ref[:, cols] = y
            yt_ref[cols, :] = y.T.astype(BF16)
            l_ref[:, cols] = m + jnp.log(den)

    blk = pl.BlockSpec((tm, GRP_W), lambda i: (i, 0))
    cls = [pl.BlockSpec((d, tm // d, GRP_W), lambda i: (0, i, 0)) for d in DILATIONS]
    shp = jax.ShapeDtypeStruct((S, GRP_W), F32)
    return pl.pallas_call(
        body, name=name, grid=(S // tm,), in_specs=cls + cls,
        out_specs=[blk, pl.BlockSpec((GRP_W, tm), lambda i: (0, i)), blk],
        out_shape=[shp, jax.ShapeDtypeStruct((GRP_W, S), BF16), shp],
        scratch_shapes=[pltpu.VMEM((6, tm, 128), F32)],
        compiler_params=_cp(("parallel",)),
    )(*os, *lses)


def _attn_bwd(gi, qkv_c, dy_c, y_c, lse_c, name):
    d = DILATIONS[gi]
    nblk = S // d // BLK

    def body(*refs):
        if nblk > 1:
            (q_ref, qn_ref, k_ref, kp_ref, v_ref, vp_ref, dy_ref, dyn_ref, y_ref, yn_ref, l_ref, ln_ref,
             dq_ref, dk_ref, dv_ref) = refs
            n = pl.program_id(1)
            has_prev = n != 0
            has_next = n != nblk - 1
        else:
            q_ref, k_ref, v_ref, dy_ref, y_ref, l_ref, dq_ref, dk_ref, dv_ref = refs
        qi = lax.broadcasted_iota(jnp.int32, (NH, BLK, BLK), 1)
        kj = lax.broadcasted_iota(jnp.int32, (NH, BLK, BLK), 2)

        def lse_col(ref):
            return jnp.stack([ref[:, h * DH:h * DH + 1] for h in range(NH)])

        q, k, v = _heads(q_ref), _heads(k_ref), _heads(v_ref)
        dy = _heads(dy_ref)
        dd = jnp.sum(dy * _heads(y_ref), axis=-1, keepdims=True)
        lcol = lse_col(l_ref)
        dyb = dy.astype(BF16)
        p = jnp.exp(jnp.where(kj <= qi, _bdot_nt(q, k) * 0.125, NEG_INF) - lcol)
        ds = (p * (_bdot_nt(dyb, v) - dd)).astype(BF16)
        dq = _bdot(ds, k)
        dk = _bdot_tn(ds, q)
        dv = _bdot_tn(p.astype(BF16), dyb)
        if nblk > 1:
            qn, kpv, vpv = _heads(qn_ref), _heads(kp_ref), _heads(vp_ref)
            dyn = _heads(dyn_ref)
            ddn = jnp.sum(dyn * _heads(yn_ref), axis=-1, keepdims=True)
            lncol = lse_col(ln_ref)
            dynb = dyn.astype(BF16)
            mask_p = jnp.logical_and(kj >= qi, has_prev)
            pp = jnp.exp(jnp.where(mask_p, _bdot_nt(q, kpv) * 0.125, NEG_INF) - lcol)
            dsp = (pp * (_bdot_nt(dyb, vpv) - dd)).astype(BF16)
            dq = dq + _bdot(dsp, kpv)
            mask_n = jnp.logical_and(kj >= qi, has_next)
            pn = jnp.exp(jnp.where(mask_n, _bdot_nt(qn, k) * 0.125, NEG_INF) - lncol)
            dsn = (pn * (_bdot_nt(dynb, v) - ddn)).astype(BF16)
            dk = dk + _bdot_tn(dsn, qn)
            dv = dv + _bdot_tn(pn.astype(BF16), dynb)
        dq_ref[...] = _unheads(dq) * 0.125
        dk_ref[...] = _unheads(dk) * 0.125
        dv_ref[...] = _unheads(dv)

    def spec(sec, shift):
        def idx(r, n):
            return (r, jnp.clip(n + shift, 0, nblk - 1), sec)
        return pl.BlockSpec((None, BLK, GRP_W), idx)

    if nblk > 1:
        in_specs = [spec(0, 0), spec(0, 1), spec(1, 0), spec(1, -1), spec(2, 0), spec(2, -1),
                    spec(0, 0), spec(0, 1), spec(0, 0), spec(0, 1), spec(0, 0), spec(0, 1)]
        args = (qkv_c,) * 6 + (dy_c, dy_c, y_c, y_c, lse_c, lse_c)
    else:
        in_specs = [spec(0, 0), spec(1, 0), spec(2, 0), spec(0, 0), spec(0, 0), spec(0, 0)]
        args = (qkv_c, qkv_c, qkv_c, dy_c, y_c, lse_c)
    out = spec(0, 0)
    shp = jax.ShapeDtypeStruct((d, S // d, GRP_W), F32)
    return pl.pallas_call(
        body, name=name, grid=(d, nblk), in_specs=in_specs, out_specs=[out, out, out], out_shape=[shp, shp, shp],
        compiler_params=_cp(("parallel", "parallel")),
    )(*args)


_SQRT_HALF = 0.7071067811865476
_INV_SQRT_2PI = 0.3989422804014327


def _gelu(z):
    return 0.5 * z * (1.0 + lax.erf(z * _SQRT_HALF))


def _gelu_grad(z):
    return 0.5 * (1.0 + lax.erf(z * _SQRT_HALF)) + z * (jnp.exp(-0.5 * z * z) * _INV_SQRT_2PI)


def _tril_mask():
    t = lax.broadcasted_iota(jnp.int32, (BLK, BLK), 0)
    s = lax.broadcasted_iota(jnp.int32, (BLK, BLK), 1)
    return s <= t


def _groups(t):
    return jnp.stack([t[:, g * BLK:(g + 1) * BLK] for g in range(8)])


def _ungroup(t3):
    return jnp.concatenate([t3[g] for g in range(8)], axis=1)


def _group_bias(bs_ref):
    return jnp.stack([bs_ref[:, g:g + 1] for g in range(8)])


def _gmlp_fwd(z, ln_g, ln_b, w_s, b_s_t, name):
    def body(z_ref, g_ref, b_ref, ws_ref, bs_ref, y_ref, yt_ref):
        zg = _gelu(z_ref[...])
        u = zg[:, :GW]
        xh, _ = _ln_stats(zg[:, GW:])
        vn = (xh * g_ref[...] + b_ref[...]).astype(BF16)
        wt = jnp.where(_tril_mask(), ws_ref[...], 0.0).astype(BF16)
        yv = u * _ungroup(_bdot(wt, _groups(vn)) + _group_bias(bs_ref))
        y_ref[...] = yv.astype(BF16)
        yt_ref[...] = yv.T.astype(BF16)

    vec = pl.BlockSpec((1, GW), lambda n: (0, 0))
    return pl.pallas_call(
        body, name=name, grid=(NBLK,),
        in_specs=[pl.BlockSpec((BLK, 2 * GW), lambda n: (n, 0)), vec, vec,
                  pl.BlockSpec((8, BLK, BLK), lambda n: (0, 0, 0)), pl.BlockSpec((BLK, 8), lambda n: (0, 0))],
        out_specs=[pl.BlockSpec((BLK, GW), lambda n: (n, 0)), pl.BlockSpec((GW, BLK), lambda n: (0, n))],
        out_shape=[jax.ShapeDtypeStruct((S, GW), BF16), jax.ShapeDtypeStruct((GW, S), BF16)],
        compiler_params=_cp(("parallel",)),
    )(z, ln_g, ln_b, w_s, b_s_t)


def _gmlp_bwd(z, dy, ln_g, ln_b, w_s, b_s_t, name):
    def body(z_ref, dy_ref, g_ref, b_ref, ws_ref, bs_ref, dz_ref, dws_ref, dbs_ref, dg_ref, db_ref, dvn_ref):
        n = pl.program_id(0)
        zv = z_ref[...]
        zg = _gelu(zv)
        u = zg[:, :GW]
        xh, rstd = _ln_stats(zg[:, GW:])
        vn = (xh * g_ref[...] + b_ref[...]).astype(BF16)
        tril = _tril_mask()

        @pl.when(n == 0)
        def _():
            dws_ref[...] = jnp.zeros_like(dws_ref)
            dbs_ref[...] = jnp.zeros_like(dbs_ref)
            dg_ref[...] = jnp.zeros_like(dg_ref)
            db_ref[...] = jnp.zeros_like(db_ref)

        wt = jnp.where(tril, ws_ref[...], 0.0).astype(BF16)
        vn3 = _groups(vn)
        dyv = dy_ref[...]
        mixed = _ungroup(_bdot(wt, vn3) + _group_bias(bs_ref))
        dz_ref[:, :GW] = (dyv * mixed * _gelu_grad(zv[:, :GW])).astype(BF16)
        dmix3 = _groups(dyv * u)
        dmb = dmix3.astype(BF16)
        dws_ref[...] += jnp.where(tril, _bdot_nt(dmb, vn3), 0.0)
        dbsum = jnp.sum(dmix3, axis=-1, keepdims=True)
        for gg in range(8):
            dbs_ref[:, gg:gg + 1] += dbsum[gg]
        dvn_ref[...] = _ungroup(_bdot_tn(wt, dmb))

        dvn = dvn_ref[...]
        dg_ref[...] += jnp.sum(dvn * xh, axis=0, keepdims=True)
        db_ref[...] += jnp.sum(dvn, axis=0, keepdims=True)
        dvg = _ln_dx(dvn * g_ref[...], xh, rstd)
        dz_ref[:, GW:] = (dvg * _gelu_grad(zv[:, GW:])).astype(BF16)

    vec = pl.BlockSpec((1, GW), lambda n: (0, 0))
    ws = pl.BlockSpec((8, BLK, BLK), lambda n: (0, 0, 0))
    bs = pl.BlockSpec((BLK, 8), lambda n: (0, 0))
    return pl.pallas_call(
        body, name=name, grid=(NBLK,),
        in_specs=[pl.BlockSpec((BLK, 2 * GW), lambda n: (n, 0)), pl.BlockSpec((BLK, GW), lambda n: (n, 0)),
                  vec, vec, ws, bs],
        out_specs=[pl.BlockSpec((BLK, 2 * GW), lambda n: (n, 0)), ws, bs, vec, vec],
        out_shape=[jax.ShapeDtypeStruct((S, 2 * GW), BF16), jax.ShapeDtypeStruct((8, BLK, BLK), F32),
                   jax.ShapeDtypeStruct((BLK, 8), F32), jax.ShapeDtypeStruct((1, GW), F32),
                   jax.ShapeDtypeStruct((1, GW), F32)],
        scratch_shapes=[pltpu.VMEM((BLK, GW), F32)],
        compiler_params=_cp(("arbitrary",)),
    )(z, dy, ln_g, ln_b, w_s, b_s_t)


def _merge_fwd(a, b, gl, b_gates, name):
    tm = 512

    def body(a_ref, b_ref, g0_ref, g1_ref, bg_ref, o_ref, ot_ref):
        g0 = jax.nn.sigmoid(g0_ref[...] + bg_ref[:, :D])
        g1 = jax.nn.sigmoid(g1_ref[...] + bg_ref[:, D:])
        mg = g0 * a_ref[...] + g1 * b_ref[...]
        o_ref[...] = mg.astype(BF16)
        ot_ref[...] = mg.T.astype(BF16)

    row = pl.BlockSpec((tm, D), lambda i: (i, 0))
    return pl.pallas_call(
        body, name=name, grid=(S // tm,),
        in_specs=[row, row, row, pl.BlockSpec((tm, D), lambda i: (i, 1)), pl.BlockSpec((1, 2 * D), lambda i: (0, 0))],
        out_specs=[row, pl.BlockSpec((D, tm), lambda i: (0, i))],
        out_shape=[jax.ShapeDtypeStruct((S, D), BF16), jax.ShapeDtypeStruct((D, S), BF16)],
        compiler_params=_cp(("parallel",)),
    )(a, b, gl, gl, b_gates)


def _merge_bwd(dm, a, b, gl, b_gates, name):
    tm = 512

    def body(dm_ref, a_ref, b_ref, g0_ref, g1_ref, bg_ref, da_ref, db_ref, dgl_ref, dbg_ref):
        i = pl.program_id(0)
        dmv = dm_ref[...]
        g0 = jax.nn.sigmoid(g0_ref[...] + bg_ref[:, :D])
        g1 = jax.nn.sigmoid(g1_ref[...] + bg_ref[:, D:])
        da_ref[...] = (dmv * g0).astype(BF16)
        db_ref[...] = (dmv * g1).astype(BF16)
        d0 = dmv * a_ref[...] * g0 * (1.0 - g0)
        d1 = dmv * b_ref[...] * g1 * (1.0 - g1)
        dgl_ref[:, :D] = d0.astype(BF16)
        dgl_ref[:, D:] = d1.astype(BF16)
        s0 = jnp.sum(d0, axis=0, keepdims=True)
        s1 = jnp.sum(d1, axis=0, keepdims=True)

        @pl.when(i == 0)
        def _():
            dbg_ref[:, :D] = s0
            dbg_ref[:, D:] = s1

        @pl.when(i > 0)
        def _():
            dbg_ref[:, :D] += s0
            dbg_ref[:, D:] += s1

    row = pl.BlockSpec((tm, D), lambda i: (i, 0))
    wide = pl.BlockSpec((tm, 2 * D), lambda i: (i, 0))
    bg = pl.BlockSpec((1, 2 * D), lambda i: (0, 0))
    return pl.pallas_call(
        body, name=name, grid=(S // tm,),
        in_specs=[row, row, row, row, pl.BlockSpec((tm, D), lambda i: (i, 1)), bg],
        out_specs=[row, row, wide, bg],
        out_shape=[jax.ShapeDtypeStruct((S, D), BF16), jax.ShapeDtypeStruct((S, D), BF16),
                   jax.ShapeDtypeStruct((S, 2 * D), BF16), jax.ShapeDtypeStruct((1, 2 * D), F32)],
        compiler_params=_cp(("arbitrary",)),
    )(dm, a, b, gl, gl, b_gates)


def _adam_math(w, g, m, v):
    m2 = ADAM_B1 * m + (1.0 - ADAM_B1) * g
    v2 = ADAM_B2 * v + (1.0 - ADAM_B2) * (g * g)
    m_hat = m2 / (1.0 - ADAM_B1 ** ADAM_STEP)
    v_hat = v2 / (1.0 - ADAM_B2 ** ADAM_STEP)
    delta = -ADAM_LR * (m_hat / (jnp.sqrt(v_hat) + ADAM_EPS) + ADAM_WD * w)
    return delta, m2, v2


def _pick_rows(rows, cols, unit=16, budget=2 * MIB):
    best = unit
    for t in range(unit, rows + 1, unit):
        if rows % t == 0 and t * cols * 4 <= budget:
            best = t
    assert rows % best == 0
    return best


def _adamw(w, g, m, v, name):
    r, c = w.shape
    tr = _pick_rows(r, c, unit=8)

    def body(w_ref, g_ref, m_ref, v_ref, go_ref, d_ref, mo_ref, vo_ref):
        gv = g_ref[...]
        delta, m2, v2 = _adam_math(w_ref[...], gv, m_ref[...], v_ref[...])
        go_ref[...] = gv
        d_ref[...] = delta
        mo_ref[...] = m2
        vo_ref[...] = v2

    blk = pl.BlockSpec((tr, c), lambda i: (i, 0))
    shp = jax.ShapeDtypeStruct((r, c), F32)
    return pl.pallas_call(
        body, name=name, grid=(r // tr,), in_specs=[blk] * 4, out_specs=[blk] * 4, out_shape=[shp] * 4,
        compiler_params=_cp(("parallel",)),
    )(*[pltpu.with_memory_space_constraint(t, pltpu.HBM) for t in (w, g, m, v)])


def _small_sum_adamw(parts, own, pos, w, m, v, name):
    tr = 48

    def body(pos_ref, p_ref, own_ref, w_ref, m_ref, v_ref, g_ref, d_ref, mo_ref, vo_ref):
        me = 2 * pos_ref[1] + pos_ref[0]
        gv = None
        for k in range(8):
            term = jnp.where(me == k, own_ref[...], p_ref[k])
            gv = term if gv is None else gv + term
        delta, m2, v2 = _adam_math(w_ref[...], gv, m_ref[...], v_ref[...])
        g_ref[...] = gv
        d_ref[...] = delta
        mo_ref[...] = m2
        vo_ref[...] = v2

    blk = pl.BlockSpec((tr, D), lambda i, p: (i, 0))
    shp = jax.ShapeDtypeStruct((SMALL_ROWS, D), F32)
    return pl.pallas_call(
        body, name=name,
        grid_spec=pltpu.PrefetchScalarGridSpec(
            num_scalar_prefetch=1, grid=(SMALL_ROWS // tr,),
            in_specs=[pl.BlockSpec((8, tr, D), lambda i, p: (0, i, 0)), blk, blk, blk, blk],
            out_specs=[blk] * 4),
        out_shape=[shp] * 4,
        compiler_params=_cp(("parallel",)),
    )(pos, parts, own, w, m, v)


ANY = pl.BlockSpec(memory_space=pl.ANY)


def _in_hbm(arrays):
    return [pltpu.with_memory_space_constraint(a, pltpu.HBM) for a in arrays]


def _mesh_pos():
    x, y, c = lax.axis_index("x"), lax.axis_index("y"), lax.axis_index("c")
    chips = [(1 - x, y), (x, 1 - y), (1 - x, 1 - y)]
    return x, y, c, chips


def _place_shard(w, kind, pos, name):
    r, c = w.shape
    tr = _pick_rows(r, c)

    def body(pos_ref, w_ref, o_ref):
        o_ref[...] = w_ref[...].astype(BF16)

    if kind == "stack":
        o_spec = pl.BlockSpec((None, tr, c), lambda i, p: (p[1], i, 0))
        shape = (NSH, r, c)
    else:
        o_spec = pl.BlockSpec((tr, c), lambda i, p: (i, p[1]))
        shape = (r, NSH * c)
    return pl.pallas_call(
        body, name=name,
        grid_spec=pltpu.PrefetchScalarGridSpec(
            num_scalar_prefetch=1, grid=(r // tr,),
            in_specs=[pl.BlockSpec((tr, c), lambda i, p: (i, 0))], out_specs=o_spec),
        out_shape=pltpu.HBM(shape, BF16),
        compiler_params=_cp(("parallel",)),
    )(pos, pltpu.with_memory_space_constraint(w, pltpu.HBM))


SEM = pl.BlockSpec(memory_space=pltpu.SEMAPHORE)
SPLIT_COPY = pltpu.CompilerParams(has_side_effects=pltpu.SideEffectType.DATAFLOW_SIDE_EFFECTING)


def _shard_window(ref, kind, j, h, dims):
    r, c = dims
    rows = pl.ds(pl.multiple_of(h * (r // 2), 16), r // 2)
    if kind == "stack":
        return ref.at[j, rows, :]
    return ref.at[rows, pl.ds(pl.multiple_of(j * c, 128), c)]


def _ici_copy(ref, kind, dims, j, c, sems, idx, to):
    win = _shard_window(ref, kind, j, c, dims)
    return pltpu.make_async_remote_copy(src_ref=win, dst_ref=win, send_sem=sems[0].at[idx], recv_sem=sems[1].at[idx],
                                        device_id=to, device_id_type=MESH_T)


def _both_copy(ref, kind, dims, a, k, chip, half, tc, sc, sems):
    win = _shard_window(ref, kind, half[0], half[1], dims)
    return pltpu.make_async_remote_copy(src_ref=win, dst_ref=win, send_sem=sems[0].at[6 * a + 2 * k + tc],
                                        recv_sem=sems[1].at[6 * a + 2 * k + sc],
                                        device_id=(chip[0], chip[1], tc), device_id_type=MESH_T)


def _gather_start(fulls, kinds, dims, after, name, both=False):
    n, na = len(fulls), len(after)
    per = 6 if both else 3

    def body(*refs):
        outs = refs[n + na:2 * n + na]
        send_sems, recv_sems, token = refs[2 * n + na:]
        x, y, c, chips = _mesh_pos()
        for a in range(n):
            for k, chip in enumerate(chips):
                if both:
                    for tc in range(2):
                        _both_copy(outs[a], kinds[a], dims[a], a, k, chip, (2 * x + y, c), tc, c,
                                   (send_sems, recv_sems)).start()
                else:
                    _ici_copy(outs[a], kinds[a], dims[a], 2 * x + y, c, (send_sems, recv_sems), 3 * a + k,
                              (chip[0], chip[1], c)).start()
        token[...] = jnp.zeros_like(token)

    res = pl.pallas_call(
        body, name=name, in_specs=[ANY] * (n + na),
        out_specs=[ANY] * n + [SEM, SEM, pl.BlockSpec(memory_space=pltpu.VMEM)],
        out_shape=[pltpu.HBM(f.shape, BF16) for f in fulls]
        + [pltpu.SemaphoreType.DMA((per * n,)), pltpu.SemaphoreType.DMA((per * n,)),
           jax.ShapeDtypeStruct((8, 128), F32)],
        input_output_aliases={i: i for i in range(n)},
        compiler_params=SPLIT_COPY,
    )(*_in_hbm(fulls), *after)
    return res[:n], res[n], res[n + 1], res[n + 2]


def _gather_wait(fulls, send_sems, recv_sems, kinds, dims, after, name, both=False):
    n, na = len(fulls), len(after)

    def body(*refs):
        ssem, rsem = refs[n], refs[n + 1]
        outs = refs[n + 2 + na:]
        x, y, c, chips = _mesh_pos()
        for a in range(n):
            for k, chip in enumerate(chips):
                if both:
                    for oc in range(2):
                        _both_copy(outs[a], kinds[a], dims[a], a, k, chip, (2 * x + y, c), oc, c,
                                   (ssem, rsem)).wait_send()
                        _both_copy(outs[a], kinds[a], dims[a], a, k, chip, (2 * chip[0] + chip[1], oc), c, oc,
                                   (ssem, rsem)).wait_recv()
                    continue
                to = (chip[0], chip[1], c)
                _ici_copy(outs[a], kinds[a], dims[a], 2 * x + y, c, (ssem, rsem), 3 * a + k, to).wait_send()
                _ici_copy(outs[a], kinds[a], dims[a], 2 * chip[0] + chip[1], c, (ssem, rsem), 3 * a + k, to).wait_recv()

    return pl.pallas_call(
        body, name=name, in_specs=[ANY] * n + [SEM, SEM] + [ANY] * na, out_specs=[ANY] * n,
        out_shape=[pltpu.HBM(f.shape, BF16) for f in fulls],
        input_output_aliases={i: i for i in range(n)},
        compiler_params=SPLIT_COPY,
    )(*_in_hbm(fulls), send_sems, recv_sems, *after)


def _gather_forward(fulls, kinds, dims, name):
    n = len(fulls)

    def body(*refs):
        outs = refs[n:2 * n]
        sems = refs[2 * n:]
        x, y, c, chips = _mesh_pos()
        sib = (x, y, 1 - c)
        cps = []
        for a in range(n):
            for k, chip in enumerate(chips):
                cp = _ici_copy(outs[a], kinds[a], dims[a], 2 * chip[0] + chip[1], c, sems, 3 * a + k, sib)
                cp.start()
                cps.append(cp)
        for a in range(n):
            for k, chip in enumerate(chips):
                _ici_copy(outs[a], kinds[a], dims[a], 2 * chip[0] + chip[1], 1 - c, sems, 3 * a + k, sib).wait_recv()
        for cp in cps:
            cp.wait_send()

    return pl.pallas_call(
        body, name=name, in_specs=[ANY] * n, out_specs=[ANY] * n,
        out_shape=[pltpu.HBM(f.shape, BF16) for f in fulls],
        input_output_aliases={i: i for i in range(n)},
        scratch_shapes=[pltpu.SemaphoreType.DMA((3 * n,)), pltpu.SemaphoreType.DMA((3 * n,))],
    )(*_in_hbm(fulls))


def _pair_copy(src, land, a, x, y, c, sems):
    return pltpu.make_async_remote_copy(
        src_ref=src.at[1 - c], dst_ref=land, send_sem=sems[0].at[a], recv_sem=sems[1].at[a],
        device_id=(x, y, 1 - c), device_id_type=MESH_T)


def _pair_start(grads, lands, name):
    n = len(grads)

    def body(*refs):
        srcs, dsts = refs[2 * n:3 * n], refs[3 * n:4 * n]
        send_sems, recv_sems, token = refs[4 * n:]
        x, y, c, _ = _mesh_pos()
        for a in range(n):
            _pair_copy(srcs[a], dsts[a], a, x, y, c, (send_sems, recv_sems)).start()
        token[...] = jnp.zeros_like(token)

    res = pl.pallas_call(
        body, name=name, in_specs=[ANY] * (2 * n),
        out_specs=[ANY] * (2 * n) + [SEM, SEM, pl.BlockSpec(memory_space=pltpu.VMEM)],
        out_shape=[pltpu.HBM(g.shape, F32) for g in grads]
        + [pltpu.HBM(l.shape, F32) for l in lands]
        + [pltpu.SemaphoreType.DMA((n,)), pltpu.SemaphoreType.DMA((n,)), jax.ShapeDtypeStruct((8, 128), F32)],
        input_output_aliases={i: i for i in range(2 * n)},
        compiler_params=SPLIT_COPY,
    )(*_in_hbm(grads), *_in_hbm(lands))
    return res[:n], res[n:2 * n], res[2 * n], res[2 * n + 1], res[2 * n + 2]


def _pair_wait(grads, lands, send_sems, recv_sems, after, name):
    n, na = len(grads), len(after)

    def body(*refs):
        ssem, rsem = refs[2 * n], refs[2 * n + 1]
        outs = refs[2 * n + 2 + na:]
        x, y, c, _ = _mesh_pos()
        for a in range(n):
            cp = _pair_copy(outs[a], outs[n + a], a, x, y, c, (ssem, rsem))
            cp.wait_send()
            cp.wait_recv()

    res = pl.pallas_call(
        body, name=name, in_specs=[ANY] * (2 * n) + [SEM, SEM] + [ANY] * na, out_specs=[ANY] * (2 * n),
        out_shape=[pltpu.HBM(g.shape, F32) for g in grads]
        + [pltpu.HBM(l.shape, F32) for l in lands],
        input_output_aliases={i: i for i in range(2 * n)},
        compiler_params=SPLIT_COPY,
    )(*_in_hbm(grads), *_in_hbm(lands), send_sems, recv_sems, *after)
    return res[:n], res[n:]


def _pair_sum(g, recv, pos, name):
    _, _, rh, c = g.shape
    tr = _pick_rows(rh, c)

    def body(pos_ref, g_ref, r_ref, o_ref):
        o_ref[...] = (g_ref[...] + r_ref[...]).astype(BF16)

    return pl.pallas_call(
        body, name=name,
        grid_spec=pltpu.PrefetchScalarGridSpec(
            num_scalar_prefetch=1, grid=(3, rh // tr),
            in_specs=[pl.BlockSpec((None, None, tr, c), lambda k, r, p: (p[0], p[2 + k], r, 0)),
                      pl.BlockSpec((None, tr, c), lambda k, r, p: (p[2 + k], r, 0))],
            out_specs=pl.BlockSpec((None, tr, c), lambda k, r, p: (k, r, 0))),
        out_shape=pltpu.HBM((3, rh, c), BF16),
        compiler_params=_cp(("parallel", "parallel")),
    )(pos, *_in_hbm([g, recv]))


def _chip_copy(src, land, a, k, chip, c, sems):
    return pltpu.make_async_remote_copy(
        src_ref=src.at[k], dst_ref=land.at[k], send_sem=sems[0].at[3 * a + k],
        recv_sem=sems[1].at[3 * a + k], device_id=(chip[0], chip[1], c), device_id_type=MESH_T)


def _chip_start(psums, lands, name):
    n = len(psums)

    def body(*refs):
        srcs, dsts = refs[2 * n:3 * n], refs[3 * n:4 * n]
        send_sems, recv_sems, token = refs[4 * n:]
        x, y, c, chips = _mesh_pos()
        for a in range(n):
            for k, chip in enumerate(chips):
                _chip_copy(srcs[a], dsts[a], a, k, chip, c, (send_sems, recv_sems)).start()
        token[...] = jnp.zeros_like(token)

    res = pl.pallas_call(
        body, name=name, in_specs=[ANY] * (2 * n),
        out_specs=[ANY] * (2 * n) + [SEM, SEM, pl.BlockSpec(memory_space=pltpu.VMEM)],
        out_shape=[pltpu.HBM(p.shape, BF16) for p in psums]
        + [pltpu.HBM(l.shape, BF16) for l in lands]
        + [pltpu.SemaphoreType.DMA((3 * n,)), pltpu.SemaphoreType.DMA((3 * n,)), jax.ShapeDtypeStruct((8, 128), F32)],
        input_output_aliases={i: i for i in range(2 * n)},
        compiler_params=SPLIT_COPY,
    )(*_in_hbm(psums), *_in_hbm(lands))
    return res[:n], res[n:2 * n], res[2 * n], res[2 * n + 1], res[2 * n + 2]


def _chip_wait(psums, lands, send_sems, recv_sems, after, name):
    n, na = len(psums), len(after)

    def body(*refs):
        ssem, rsem = refs[2 * n], refs[2 * n + 1]
        outs = refs[2 * n + 2 + na:]
        srcs, dsts = outs[:n], outs[n:]
        x, y, c, chips = _mesh_pos()
        for a in range(n):
            for k, chip in enumerate(chips):
                cp = _chip_copy(srcs[a], dsts[a], a, k, chip, c, (ssem, rsem))
                cp.wait_send()
                cp.wait_recv()

    res = pl.pallas_call(
        body, name=name, in_specs=[ANY] * (2 * n) + [SEM, SEM] + [ANY] * na, out_specs=[ANY] * (2 * n),
        out_shape=[pltpu.HBM(p.shape, BF16) for p in psums]
        + [pltpu.HBM(l.shape, BF16) for l in lands],
        input_output_aliases={i: i for i in range(2 * n)},
        compiler_params=SPLIT_COPY,
    )(*_in_hbm(psums), *_in_hbm(lands), send_sems, recv_sems, *after)
    return res[n:]


def _owner_sum(g, recv_a, recv_b, pos, name):
    _, _, rh, c = g.shape
    tr = _pick_rows(rh, c)

    def body(pos_ref, g_ref, ra_ref, rb_ref, o_ref):
        acc = g_ref[...] + ra_ref[...]
        for k in range(3):
            acc = acc + rb_ref[k].astype(F32)
        o_ref[...] = acc

    return pl.pallas_call(
        body, name=name,
        grid_spec=pltpu.PrefetchScalarGridSpec(
            num_scalar_prefetch=1, grid=(rh // tr,),
            in_specs=[pl.BlockSpec((None, None, tr, c), lambda r, p: (p[0], p[1], r, 0)),
                      pl.BlockSpec((None, tr, c), lambda r, p: (p[1], r, 0)),
                      pl.BlockSpec((3, tr, c), lambda r, p: (0, r, 0))],
            out_specs=pl.BlockSpec((None, tr, c), lambda r, p: (p[0], r, 0))),
        out_shape=pltpu.HBM((2, rh, c), F32),
        compiler_params=_cp(("parallel",)),
    )(pos, *_in_hbm([g, recv_a, recv_b]))


def _sibling_allgather(halves, name):
    n = len(halves)

    def body(*refs):
        outs = refs[n:2 * n]
        send_sems, recv_sems = refs[2 * n:]
        x, y, c, _ = _mesh_pos()
        cps = []
        for a in range(n):
            cp = pltpu.make_async_remote_copy(
                src_ref=outs[a].at[c], dst_ref=outs[a].at[c], send_sem=send_sems.at[a], recv_sem=recv_sems.at[a],
                device_id=(x, y, 1 - c), device_id_type=MESH_T)
            cp.start()
            cps.append(cp)
        for a in range(n):
            cps[a].wait_send()
            pltpu.make_async_remote_copy(
                src_ref=outs[a].at[1 - c], dst_ref=outs[a].at[1 - c], send_sem=send_sems.at[a],
                recv_sem=recv_sems.at[a], device_id=(x, y, 1 - c), device_id_type=MESH_T).wait_recv()

    return pl.pallas_call(
        body, name=name, in_specs=[ANY] * n, out_specs=[ANY] * n,
        out_shape=[pltpu.HBM(h.shape, F32) for h in halves],
        input_output_aliases={i: i for i in range(n)},
        scratch_shapes=[pltpu.SemaphoreType.DMA((n,)), pltpu.SemaphoreType.DMA((n,))],
    )(*_in_hbm(halves))


def _peers(x, y, c):
    rel = [(0, 0, 1), (0, 1, 0), (0, 1, 1), (1, 0, 0), (1, 0, 1), (1, 1, 0), (1, 1, 1)]
    return [((1 - x) if dx else x, (1 - y) if dy else y, (1 - c) if dc else c) for dx, dy, dc in rel]


def _small_copy(src, land, k, peer, slot, sems):
    return pltpu.make_async_remote_copy(src_ref=src, dst_ref=land.at[slot], send_sem=sems[0].at[k],
                                        recv_sem=sems[1].at[k], device_id=peer, device_id_type=MESH_T)


def _small_start(part, land, name):
    def body(p_in, l_in, p_ref, l_ref, send_sems, recv_sems, token):
        x, y, c, _ = _mesh_pos()
        for k, peer in enumerate(_peers(x, y, c)):
            _small_copy(p_ref, l_ref, k, peer, 4 * x + 2 * y + c, (send_sems, recv_sems)).start()
        token[...] = jnp.zeros_like(token)

    return pl.pallas_call(
        body, name=name, in_specs=[ANY, ANY],
        out_specs=[ANY, ANY, SEM, SEM, pl.BlockSpec(memory_space=pltpu.VMEM)],
        out_shape=[pltpu.HBM(part.shape, F32), pltpu.HBM(land.shape, F32), pltpu.SemaphoreType.DMA((7,)),
                   pltpu.SemaphoreType.DMA((7,)), jax.ShapeDtypeStruct((8, 128), F32)],
        input_output_aliases={0: 0, 1: 1},
        compiler_params=SPLIT_COPY,
    )(*_in_hbm([part, land]))


def _small_wait(part, land, send_sems, recv_sems, after, name):
    na = len(after)

    def body(*refs):
        ssem, rsem = refs[2], refs[3]
        p_ref, l_ref = refs[4 + na:]
        x, y, c, _ = _mesh_pos()
        for k, peer in enumerate(_peers(x, y, c)):
            cp = _small_copy(p_ref, l_ref, k, peer, 4 * peer[0] + 2 * peer[1] + peer[2], (ssem, rsem))
            cp.wait_send()
            cp.wait_recv()

    return pl.pallas_call(
        body, name=name, in_specs=[ANY, ANY, SEM, SEM] + [ANY] * na, out_specs=[ANY, ANY],
        out_shape=[pltpu.HBM(part.shape, F32), pltpu.HBM(land.shape, F32)],
        input_output_aliases={0: 0, 1: 1},
        compiler_params=SPLIT_COPY,
    )(*_in_hbm([part, land]), send_sems, recv_sems, *after)


def _pack_small(ln1_g, ln1_b, gln_g, gln_b, ln2_g, ln2_b, ln3_g, ln3_b, b_gates, b_s, w_s):
    rows = [ln1_g, ln1_b, gln_g, gln_b, ln2_g, ln2_b, ln3_g, ln3_b]
    rows = [r.reshape(1, D) for r in rows] + [b_gates.reshape(2, D), b_s.reshape(1, D), jnp.zeros((5, D), F32),
                                             w_s.reshape(128, D)]
    return jnp.concatenate(rows, axis=0)


def _unpack_small(p):
    out = [p[i:i + 1] for i in range(8)]
    return out + [p[8:10].reshape(1, 2 * D), p[10:11].reshape(1, 8, BLK), p[16:144].reshape(1, 8, BLK, BLK)]


GROUPS = (("f1g", "f1u", "f1d"), ("w_in",), ("w_ab", "w_gb", "w_out"), ("f2g", "f2u", "f2d"))
LATE_GROUPS = (2, 3)


def _local_step(x, pos_f, target, P, weights_of, grads_ready, flush, small_ready):
    invf = ROPE_THETA ** (-jnp.arange(0, DH, 2, dtype=F32) / DH)
    invf = jnp.tile(invf, 4).reshape(1, 128)
    b_s_t = P["gmlp_b_s"].T

    W = dict(weights_of(0, []))
    h1b, xh1, rstd1, a1, b1, h1t = _ffn_fwd(x, W["f1g"], W["f1u"], W["f1d"], P["ln1_g"], P["ln1_b"], "ffn1_fwd",
                                                emit_t=True)
    W.update(weights_of(1, [h1b]))
    qkv_c = _proj_qkv_rope(h1b, W["w_in"], pos_f, invf, "proj_qkv_rope")
    z = _matmul(h1b, W["w_in"], "nn", "proj_z", n=2 * GW, b_col0=3 * ATT_W, tm=S, tn=512)
    gl = _matmul(h1b, W["w_in"], "nn", "proj_gates", n=2 * D, b_col0=3 * ATT_W + 2 * GW, tm=S, tn=512)
    og = [_attn_fwd(gi, qkv_c[gi], "attn_fwd_g%d" % gi) for gi in range(NG)]
    y_attn, y_attn_t, lse = _attn_combine([o for o, _ in og], [l for _, l in og], "attn_combine")
    y_gmlp, y_gmlp_t = _gmlp_fwd(z, P["gmlp_ln_g"], P["gmlp_ln_b"], P["gmlp_w_s"], b_s_t, "gmlp_fwd")
    W.update(weights_of(2, [y_gmlp]))
    br_a = _matmul(y_attn, W["w_ab"], "nn", "branch_attn", n=D, tm=1024, tn=D)
    br_b = _matmul(y_gmlp, W["w_gb"], "nn", "branch_gmlp", n=D, tm=1024, tn=D)
    merged, merged_t = _merge_fwd(br_a, br_b, gl, P["b_gates"], "merge_fwd")
    h2, h2b, xh2, rstd2 = _resid_ln(xh1, P["ln1_g"], P["ln1_b"], merged, W["w_out"], P["ln2_g"], P["ln2_b"],
                                    "mix_resid_ln2")
    W.update(weights_of(3, [h2b]))
    dr3, a2, b2, dg3, db3, loss = _ffn_fwd(h2, W["f2g"], W["f2u"], W["f2d"], P["ln3_g"], P["ln3_b"],
                                           "ffn2_fwd_loss", target=target)

    g_f2g, g_f2u, g_f2d, dh2 = _ffn_bwd(dr3, h2b, a2, b2, W["f2g"], W["f2u"], W["f2d"], "ffn2_bwd")
    tok = grads_ready(3, dict(f2g=g_f2g, f2u=g_f2u, f2d=g_f2d))
    dr2, dg2, db2 = _ln_bwd(dh2, xh2, rstd2, P["ln2_g"], "ln2_bwd", after=tok)
    g_wout = _wgrad(merged_t, dr2, 128, D, "dw_out", row_sharded=True)
    dmerged = _matmul(dr2, W["w_out"], "nt", "dmerged", n=D, tm=1024, tn=D)
    dab, dbb, dglb, dbg = _merge_bwd(dmerged, br_a, br_b, gl, P["b_gates"], "merge_bwd")
    tok = flush([dab])
    g_wab = _wgrad(y_attn_t, dab, GRP_W // 2, 256, "dw_attn_branch", row_sharded=False, after=tok)
    g_wgb = _wgrad(y_gmlp_t, dbb, 128, D, "dw_gmlp_branch", row_sharded=True)
    tok = grads_ready(2, dict(w_ab=g_wab, w_gb=g_wgb, w_out=g_wout))
    dy_attn = _matmul(dab, W["w_ab"], "nt", "dy_attn", n=GRP_W, tm=1024, tn=GRP_W, after=tok)
    dy_gmlp = _matmul(dbb, W["w_gb"], "nt", "dy_gmlp", n=GW, tm=1024, tn=GW)
    dzb, dws, dbs_t, dgln_g, dgln_b = _gmlp_bwd(z, dy_gmlp, P["gmlp_ln_g"], P["gmlp_ln_b"], P["gmlp_w_s"], b_s_t,
                                                 "gmlp_bwd")
    cls = _class_order([dy_attn, y_attn, lse], "attn_class_order")
    dqkv_c = []
    for gi in range(NG):
        dy_c, y_c, lse_c = [t[None] if gi == 0 else cls[2 * a + gi - 1] for a, t in enumerate((dy_attn, y_attn, lse))]
        dqkv_c.append(_attn_bwd(gi, qkv_c[gi], dy_c, y_c, lse_c, "attn_bwd_g%d" % gi))
    dqkvb = _rope_bwd(dqkv_c, pos_f, invf, "rope_bwd")
    dproj = jnp.concatenate([dqkvb, dzb, dglb], axis=1)
    tok = flush([dproj])
    g_win = _wgrad(h1t, dproj, D // 2, IN_SH, "dw_in", row_sharded=False, after=tok)
    tok = grads_ready(1, dict(w_in=g_win))
    dr1, dg1, db1 = _dh1_ln_bwd(dproj, W["w_in"], dr2, xh1, rstd1, P["ln1_g"], "dh1_ln1_bwd", after=tok)
    tok = flush([dr1])
    tok = tok + small_ready(_pack_small(dg1, db1, dgln_g, dgln_b, dg2, db2, dg3, db3, dbg, dbs_t.T, dws))
    g_f1g, g_f1u, g_f1d, dx = _ffn_bwd(dr1, x.astype(BF16), a1, b1, W["f1g"], W["f1u"], W["f1d"], "ffn1_bwd",
                                       after=tok)
    grads_ready(0, dict(f1g=g_f1g, f1u=g_f1u, f1d=g_f1d))
    flush([dx])
    return loss, dx


TRANSPOSED = ("f1g", "f1u", "f2g", "f2u")
KIND = dict(f1g="stack", f1u="stack", f1d="stack", w_in="col", w_ab="col", w_gb="stack", w_out="stack",
            f2g="stack", f2u="stack", f2d="stack")


def kernel(x, positions, ffn1_w_gate, ffn1_w_up, ffn1_w_down, ln1_g, ln1_b, w_in, b_gates, gmlp_ln_g, gmlp_ln_b, gmlp_w_s, gmlp_b_s, w_attn_branch, w_gmlp_branch, w_out, ln2_g, ln2_b, ffn2_w_gate, ffn2_w_up, ffn2_w_down, ln3_g, ln3_b, loss_target, m_ffn1_w_gate, m_ffn1_w_up, m_ffn1_w_down, m_ln1_g, m_ln1_b, m_w_in, m_b_gates, m_gmlp_ln_g, m_gmlp_ln_b, m_gmlp_w_s, m_gmlp_b_s, m_w_attn_branch, m_w_gmlp_branch, m_w_out, m_ln2_g, m_ln2_b, m_ffn2_w_gate, m_ffn2_w_up, m_ffn2_w_down, m_ln3_g, m_ln3_b, v_ffn1_w_gate, v_ffn1_w_up, v_ffn1_w_down, v_ln1_g, v_ln1_b, v_w_in, v_b_gates, v_gmlp_ln_g, v_gmlp_ln_b, v_gmlp_w_s, v_gmlp_b_s, v_w_attn_branch, v_w_gmlp_branch, v_w_out, v_ln2_g, v_ln2_b, v_ffn2_w_gate, v_ffn2_w_up, v_ffn2_w_down, v_ln3_g, v_ln3_b):
    cx, cy, cc = lax.axis_index("x"), lax.axis_index("y"), lax.axis_index("c")
    pos = jnp.stack([cc, 2 * cx + cy, 2 * (1 - cx) + cy, 2 * cx + 1 - cy, 2 * (1 - cx) + 1 - cy]).astype(jnp.int32)

    w_sh = dict(f1g=ffn1_w_gate, f1u=ffn1_w_up, f1d=ffn1_w_down, w_in=w_in, w_ab=w_attn_branch,
                w_gb=w_gmlp_branch, w_out=w_out, f2g=ffn2_w_gate, f2u=ffn2_w_up, f2d=ffn2_w_down)
    m_sh = dict(f1g=m_ffn1_w_gate, f1u=m_ffn1_w_up, f1d=m_ffn1_w_down, w_in=m_w_in, w_ab=m_w_attn_branch,
                w_gb=m_w_gmlp_branch, w_out=m_w_out, f2g=m_ffn2_w_gate, f2u=m_ffn2_w_up, f2d=m_ffn2_w_down)
    v_sh = dict(f1g=v_ffn1_w_gate, f1u=v_ffn1_w_up, f1d=v_ffn1_w_down, w_in=v_w_in, w_ab=v_w_attn_branch,
                w_gb=v_w_gmlp_branch, w_out=v_w_out, f2g=v_ffn2_w_gate, f2u=v_ffn2_w_up, f2d=v_ffn2_w_down)
    w_sh = {k: (v[0].T if k in TRANSPOSED else v[0]) for k, v in w_sh.items()}
    m_sh = {k: (v[0].T if k in TRANSPOSED else v[0]) for k, v in m_sh.items()}
    v_sh = {k: (v[0].T if k in TRANSPOSED else v[0]) for k, v in v_sh.items()}

    started, tokens = [], []
    for gi, names in enumerate(GROUPS):
        placed = [_place_shard(w_sh[k], KIND[k], pos, "place_" + k) for k in names]
        fulls, ssem, rsem, token = _gather_start(placed, [KIND[k] for k in names], [w_sh[k].shape for k in names],
                                                 tokens[-1:], "gather_start_g%d" % gi, both=gi in LATE_GROUPS)
        started.append((fulls, ssem, rsem))
        tokens.append(token)

    def weights_of(gi, after):
        names = GROUPS[gi]
        kinds, dims = [KIND[k] for k in names], [w_sh[k].shape for k in names]
        fulls, ssem, rsem = started[gi]
        fulls = _gather_wait(fulls, ssem, rsem, kinds, dims, list(after) + (tokens if gi == 0 else []),
                             "gather_wait_g%d" % gi, both=gi in LATE_GROUPS)
        if gi not in LATE_GROUPS:
            fulls = _gather_forward(fulls, kinds, dims, "gather_forward_g%d" % gi)
        return {k: (f.reshape(D, D) if k in ("w_gb", "w_out") else f) for k, f in zip(names, fulls)}

    pending, inflight = [], {}

    def grads_ready(gi, gd):
        grads = [gd[k] for k in GROUPS[gi]]
        lands = [lax.empty(g.shape[1:], F32) for g in grads]
        grads, lands, ssem, rsem, token = _pair_start(grads, lands, "rs_pair_start_g%d" % gi)
        pending.append((gi, grads, lands, ssem, rsem))
        return [token]

    def flush(after):
        gi, grads, lands, ssem, rsem = pending.pop()
        names = GROUPS[gi]
        grads, recv_a = _pair_wait(grads, lands, ssem, rsem, after, "rs_pair_wait_g%d" % gi)
        psums = [_pair_sum(g, r, pos, "rs_pair_sum_" + k) for g, r, k in zip(grads, recv_a, names)]
        lands = [lax.empty((3,) + p.shape[1:], BF16) for p in psums]
        psums, lands, ssem, rsem, token = _chip_start(psums, lands, "rs_chip_start_g%d" % gi)
        inflight[gi] = (grads, recv_a, psums, lands, ssem, rsem, token)
        return [token]

    P = dict(ln1_g=ln1_g, ln1_b=ln1_b, ln2_g=ln2_g, ln2_b=ln2_b, ln3_g=ln3_g, ln3_b=ln3_b, b_gates=b_gates,
             gmlp_ln_g=gmlp_ln_g, gmlp_ln_b=gmlp_ln_b, gmlp_w_s=gmlp_w_s[0], gmlp_b_s=gmlp_b_s[0])
    pos_f = positions.reshape(S, 1).astype(F32)
    small_state = []

    def small_ready(packed):
        land = jnp.zeros((8, SMALL_ROWS, D), F32)
        packed, land, ssem, rsem, token = _small_start(packed, land, "small_start")
        small_state.append((packed, land, ssem, rsem))
        return [token]

    loss_part, dx = _local_step(x[0], pos_f, loss_target[0], P, weights_of, grads_ready, flush, small_ready)
    loss = lax.psum(loss_part[0, 0], ("x", "y", "c"))

    g_out, d_out, m_out, v_out = {}, {}, {}, {}

    def finish(gis, after, tag):
        names, halves = [], []
        for gi in gis:
            grads, recv_a, psums, lands, ssem, rsem, token = inflight[gi]
            recv_b = _chip_wait(psums, lands, ssem, rsem, after + [inflight[0][6]], "rs_chip_wait_g%d" % gi)
            halves += [_owner_sum(g, ra, rb, pos, "rs_owner_sum_" + k)
                       for g, ra, rb, k in zip(grads, recv_a, recv_b, GROUPS[gi])]
            names += GROUPS[gi]
            after = halves[-1:]
        reduced = _sibling_allgather(halves, "rs_sibling_allgather_" + tag)
        for k, gfull in zip(names, reduced):
            res = _adamw(w_sh[k], gfull.reshape(w_sh[k].shape), m_sh[k], v_sh[k], "adamw_" + k)
            after = [res[1]]
            if k in TRANSPOSED:
                res = [r.T for r in res]
            g_out[k], d_out[k], m_out[k], v_out[k] = [r[None] for r in res]
        return after

    after = finish((3, 2, 1), [], "g321")

    small, parts = _small_wait(*small_state[0], after, "small_wait")
    sp = (ln1_g, ln1_b, gmlp_ln_g, gmlp_ln_b, ln2_g, ln2_b, ln3_g, ln3_b, b_gates, gmlp_b_s, gmlp_w_s)
    sm = (m_ln1_g, m_ln1_b, m_gmlp_ln_g, m_gmlp_ln_b, m_ln2_g, m_ln2_b, m_ln3_g, m_ln3_b, m_b_gates, m_gmlp_b_s,
          m_gmlp_w_s)
    sv = (v_ln1_g, v_ln1_b, v_gmlp_ln_g, v_gmlp_ln_b, v_ln2_g, v_ln2_b, v_ln3_g, v_ln3_b, v_b_gates, v_gmlp_b_s,
          v_gmlp_w_s)
    sg, sd, smn, svn = _small_sum_adamw(parts, small, pos, _pack_small(*sp), _pack_small(*sm), _pack_small(*sv),
                                        "small_adamw")
    names = ("ln1_g", "ln1_b", "gmlp_ln_g", "gmlp_ln_b", "ln2_g", "ln2_b", "ln3_g", "ln3_b", "b_gates", "gmlp_b_s",
             "gmlp_w_s")
    for dst, packed in ((g_out, sg), (d_out, sd), (m_out, smn), (v_out, svn)):
        for nm, val in zip(names, _unpack_small(packed)):
            dst[nm] = val
    finish((0,), [sg], "g0")

    order = ("f1g", "f1u", "f1d", "ln1_g", "ln1_b", "w_in", "b_gates", "gmlp_ln_g", "gmlp_ln_b", "gmlp_w_s", "gmlp_b_s",
             "w_ab", "w_gb", "w_out", "ln2_g", "ln2_b", "f2g", "f2u", "f2d", "ln3_g", "ln3_b")
    outs = [loss, dx[None]]
    for dst in (g_out, d_out, m_out, v_out):
        outs += [dst[k] for k in order]
    return tuple(outs)
```

```python
import jax
import jax.numpy as jnp
from jax import lax
from jax.experimental import pallas as pl
from jax.experimental.pallas import tpu as pltpu

F32 = jnp.float32
BF16 = jnp.bfloat16

S = 2048
D = 1024
NSH = 4
FSH = 704
ATT_W = 1536
GRP_W = 512
NG = 3
NH = 8
DH = 64
BLK = 128
NBLK = S // BLK
GW = 1024
IN_W = 8704
IN_SH = IN_W // NSH
ALPHA = 2.0 ** 0.25
LN_EPS = 1e-5
ROPE_THETA = 10000.0
DILATIONS = (1, 4, 16)
ADAM_LR, ADAM_B1, ADAM_B2, ADAM_EPS, ADAM_WD, ADAM_STEP = 0.001, 0.9, 0.999, 1e-08, 0.01, 10
SMALL_ROWS = 144
EPI_ROWS = 256
MESH_T = pl.DeviceIdType.MESH
MIB = 1024 * 1024
NEG_INF = float("-inf")


def _cp(sem, vmem_mib=48):
    return pltpu.CompilerParams(dimension_semantics=sem, vmem_limit_bytes=vmem_mib * MIB)


def _ln_stats(r):
    mu = jnp.mean(r, axis=-1, keepdims=True)
    xc = r - mu
    var = jnp.mean(xc * xc, axis=-1, keepdims=True)
    rstd = lax.rsqrt(var + LN_EPS)
    return xc * rstd, rstd


def _ln_dx(dxh, xh, rstd):
    m1 = jnp.mean(dxh, axis=-1, keepdims=True)
    m2 = jnp.mean(dxh * xh, axis=-1, keepdims=True)
    return rstd * (dxh - m1 - xh * m2)


def _dot_nt(a, b):
    return lax.dot_general(a, b, (((1,), (1,)), ((), ())), preferred_element_type=F32)


def _dot_tn(a, b):
    return lax.dot_general(a, b, (((0,), (0,)), ((), ())), preferred_element_type=F32)


def _dot(a, b):
    return jnp.dot(a, b, preferred_element_type=F32)


def _ffn_fwd(xin, wgt, wut, wd, ln_g, ln_b, name, emit_t=False, target=None):
    with_loss = target is not None
    tm = 1024

    def body(x_ref, wg_ref, wu_ref, wd_ref, g_ref, b_ref, *rest):
        if with_loss:
            t_ref, dr_ref, a_ref, bb_ref, dg_ref, db_ref, loss_ref, acc_ref = rest
        elif emit_t:
            hb_ref, xh_ref, rstd_ref, a_ref, bb_ref, ht_ref, acc_ref = rest
        else:
            hb_ref, xh_ref, rstd_ref, a_ref, bb_ref, acc_ref = rest
        i = pl.program_id(0)
        j = pl.program_id(1)
        xb = x_ref[...].astype(BF16)
        a = _dot_nt(xb, wg_ref[...])
        b = _dot_nt(xb, wu_ref[...])
        a_ref[...] = a.astype(BF16)
        bb_ref[...] = b.astype(BF16)
        s = (a * jax.nn.sigmoid(a)) * b
        f = _dot(s.astype(BF16), wd_ref[...])

        @pl.when(j == 0)
        def _():
            acc_ref[...] = f

        @pl.when(j > 0)
        def _():
            acc_ref[...] += f

        if with_loss:
            @pl.when(jnp.logical_and(j == NSH - 1, i == 0))
            def _():
                dg_ref[...] = jnp.zeros_like(dg_ref)
                db_ref[...] = jnp.zeros_like(db_ref)
                loss_ref[...] = jnp.zeros_like(loss_ref)

        @pl.when(j == NSH - 1)
        def _():
            for c0 in range(0, tm, EPI_ROWS):
                rows = slice(c0, c0 + EPI_ROWS)
                r = ALPHA * x_ref[rows, :] + 0.5 * acc_ref[rows, :]
                xh, rstd = _ln_stats(r)
                h = xh * g_ref[...] + b_ref[...]
                if with_loss:
                    err = h - t_ref[rows, :]
                    dy = err * (1.0 / D)
                    dr_ref[rows, :] = _ln_dx(dy * g_ref[...], xh, rstd)
                    dg_ref[...] += jnp.sum(dy * xh, axis=0, keepdims=True)
                    db_ref[...] += jnp.sum(dy, axis=0, keepdims=True)
                    part = 0.5 * jnp.sum(jnp.mean(err * err, axis=-1, keepdims=True), axis=0, keepdims=True)
                    loss_ref[...] += jnp.broadcast_to(part, (8, 128))
                else:
                    hb_ref[rows, :] = h.astype(BF16)
                    xh_ref[rows, :] = xh
                    rstd_ref[rows, :] = rstd
                    if emit_t:
                        ht_ref[:, rows] = h.T.astype(BF16)

    row = pl.BlockSpec((tm, D), lambda i, j: (i, 0))
    vec = pl.BlockSpec((1, D), lambda i, j: (0, 0))
    wsp = pl.BlockSpec((None, FSH, D), lambda i, j: (j, 0, 0))
    ab = pl.BlockSpec((None, tm, FSH), lambda i, j: (j, i, 0))
    ab_shape = jax.ShapeDtypeStruct((NSH, S, FSH), BF16)
    in_specs, args = [row, wsp, wsp, wsp, vec, vec], (xin, wgt, wut, wd, ln_g, ln_b)
    if with_loss:
        in_specs, args = in_specs + [row], args + (target,)
        out_specs = [row, ab, ab, vec, vec, pl.BlockSpec((8, 128), lambda i, j: (0, 0))]
        out_shape = [jax.ShapeDtypeStruct((S, D), F32), ab_shape, ab_shape, jax.ShapeDtypeStruct((1, D), F32),
                     jax.ShapeDtypeStruct((1, D), F32), jax.ShapeDtypeStruct((8, 128), F32)]
    else:
        out_specs = [row, row, pl.BlockSpec((tm, 1), lambda i, j: (i, 0)), ab, ab]
        out_shape = [jax.ShapeDtypeStruct((S, D), BF16), jax.ShapeDtypeStruct((S, D), F32),
                     jax.ShapeDtypeStruct((S, 1), F32), ab_shape, ab_shape]
        if emit_t:
            out_specs.append(pl.BlockSpec((D, tm), lambda i, j: (0, i)))
            out_shape.append(jax.ShapeDtypeStruct((D, S), BF16))
    return pl.pallas_call(
        body, name=name, grid=(S // tm, NSH), in_specs=in_specs, out_specs=out_specs, out_shape=out_shape,
        scratch_shapes=[pltpu.VMEM((tm, D), F32)],
        compiler_params=_cp(("arbitrary" if with_loss else "parallel", "arbitrary"), vmem_mib=56),
    )(*args)


def _ffn_bwd(dr, xin_b, a, b, wgt, wut, wd, name, after=()):
    tm = 512
    ni = S // tm
    hr = FSH // 2

    def body(dr_ref, a_ref, b_ref, wg_ref, wu_ref, wd_ref, x_hbm, *rest):
        dwg_hbm, dwu_hbm, dwd_hbm, dx_hbm, dx_acc, da_all, db_all, s_all, df_all, x_all, res_buf, sems = rest[len(after):]
        j = pl.program_id(0)
        i = pl.program_id(1)
        rows = pl.ds(pl.multiple_of(i * tm, tm), tm)

        @pl.when(jnp.logical_and(j == 0, i == 0))
        def _():
            cp = pltpu.make_async_copy(x_hbm, x_all, sems.at[0])
            cp.start()
            cp.wait()

        drv = dr_ref[...]
        df = (0.5 * drv).astype(BF16)

        @pl.when(j == 0)
        def _():
            df_all[rows, :] = df

        ds = jnp.concatenate([_dot_nt(df, wd_ref[0:384, :]), _dot_nt(df, wd_ref[384:FSH, :])], axis=1)
        av = a_ref[...].astype(F32)
        bv = b_ref[...].astype(F32)
        sig = jax.nn.sigmoid(av)
        sl = av * sig
        da = (ds * bv * (sig * (1.0 + av * (1.0 - sig)))).astype(BF16)
        db = (ds * sl).astype(BF16)
        da_all[rows, :] = da
        db_all[rows, :] = db
        s_all[rows, :] = (sl * bv).astype(BF16)
        dx = _dot(da, wg_ref[...]) + _dot(db, wu_ref[...])

        @pl.when(j == 0)
        def _():
            dx_acc[rows, :] = ALPHA * drv + dx

        @pl.when(j > 0)
        def _():
            dx_acc[rows, :] += dx

        @pl.when(i == ni - 1)
        def _():
            copies = []
            for n, (lhs, rhs, out) in enumerate(((da_all, x_all, dwg_hbm), (db_all, x_all, dwu_hbm),
                                                 (s_all, df_all, dwd_hbm))):
                slot = n % 2
                if n >= 2:
                    for cp in copies[2 * (n - 2): 2 * (n - 2) + 2]:
                        cp.wait()
                res_buf[slot] = _dot_tn(lhs[...], rhs[...])
                for h in range(2):
                    cp = pltpu.make_async_copy(res_buf.at[slot, pl.ds(h * hr, hr), :], out.at[h, j],
                                               sems.at[1 + 2 * slot + h])
                    cp.start()
                    copies.append(cp)
            for cp in copies[2:]:
                cp.wait()

        @pl.when(jnp.logical_and(j == NSH - 1, i == ni - 1))
        def _():
            cp = pltpu.make_async_copy(dx_acc, dx_hbm, sems.at[0])
            cp.start()
            cp.wait()

    row = pl.BlockSpec((tm, D), lambda j, i: (i, 0))
    wsp = pl.BlockSpec((None, FSH, D), lambda j, i: (j, 0, 0))
    ab = pl.BlockSpec((None, tm, FSH), lambda j, i: (j, i, 0))
    dwshape = jax.ShapeDtypeStruct((2, NSH, hr, D), F32)
    return pl.pallas_call(
        body, name=name, grid=(NSH, ni),
        in_specs=[row, ab, ab, wsp, wsp, wsp, ANY] + [ANY] * len(after),
        out_specs=[ANY, ANY, ANY, ANY],
        out_shape=[dwshape, dwshape, dwshape, jax.ShapeDtypeStruct((S, D), F32)],
        scratch_shapes=[pltpu.VMEM((S, D), F32), pltpu.VMEM((S, FSH), BF16), pltpu.VMEM((S, FSH), BF16),
                        pltpu.VMEM((S, FSH), BF16), pltpu.VMEM((S, D), BF16), pltpu.VMEM((S, D), BF16),
                        pltpu.VMEM((2, FSH, D), F32), pltpu.SemaphoreType.DMA((5,))],
        compiler_params=_cp(("arbitrary", "arbitrary"), vmem_mib=58),
    )(dr, a, b, wgt, wut, wd, xin_b, *after)


def _matmul(a, b, mode, name, *, n, tm, tn, b_col0=0, after=()):
    m, k = a.shape
    assert m % tm == 0 and n % tn == 0 and b_col0 % tn == 0
    off = b_col0 // tn
    na = len(after)

    def body(*refs):
        a_ref, b_ref, o_ref = refs[na:]
        av = a_ref[...].astype(BF16)
        o_ref[...] = _dot(av, b_ref[...]) if mode == "nn" else _dot_nt(av, b_ref[...])

    if mode == "nn":
        b_spec = pl.BlockSpec((k, tn), lambda i, j: (0, j + off))
    else:
        b_spec = pl.BlockSpec((tn, k), lambda i, j: (j, 0))
    return pl.pallas_call(
        body, name=name, grid=(m // tm, n // tn),
        in_specs=[pl.BlockSpec(memory_space=pl.ANY)] * na + [pl.BlockSpec((tm, k), lambda i, j: (i, 0)), b_spec],
        out_specs=pl.BlockSpec((tm, tn), lambda i, j: (i, j)),
        out_shape=jax.ShapeDtypeStruct((m, n), F32),
        compiler_params=_cp(("parallel", "parallel")),
    )(*after, a, b)


def _wgrad(xt, y, rh, c, name, row_sharded, after=()):
    na = len(after)
    if row_sharded:
        def body(x_ref, y_ref, *rest):
            o_ref = rest[na]
            res = _dot(x_ref[...], y_ref[...].astype(BF16))
            for j in range(NSH):
                for h in range(2):
                    o_ref[h, j] = res[(2 * j + h) * rh:(2 * j + h + 1) * rh, :]

        grid = (1,)
        in_specs = [pl.BlockSpec((2 * NSH * rh, S), lambda g: (0, 0)), pl.BlockSpec((S, c), lambda g: (0, 0))]
        out_specs = pl.BlockSpec((2, NSH, rh, c), lambda g: (0, 0, 0, 0))
        sem = ("arbitrary",)
    else:
        def body(x_ref, y_ref, *rest):
            rest[na][...] = _dot(x_ref[...], y_ref[...].astype(BF16))

        grid = (2, NSH)
        in_specs = [pl.BlockSpec((rh, S), lambda h, j: (h, 0)), pl.BlockSpec((S, c), lambda h, j: (0, j))]
        out_specs = pl.BlockSpec((None, None, rh, c), lambda h, j: (h, j, 0, 0))
        sem = ("parallel", "parallel")
    return pl.pallas_call(
        body, name=name, grid=grid, in_specs=in_specs + [pl.BlockSpec(memory_space=pl.ANY)] * na, out_specs=out_specs,
        out_shape=jax.ShapeDtypeStruct((2, NSH, rh, c), F32),
        compiler_params=_cp(sem, vmem_mib=56),
    )(xt, y, *after)


def _resid_ln(res_xh, res_g, res_b, a, w, ln_g, ln_b, name):
    tm = 512

    def body(rx_ref, rg_ref, rb_ref, a_ref, w_ref, g_ref, b_ref, h_ref, hb_ref, xh_ref, rstd_ref):
        r = ALPHA * (rx_ref[...] * rg_ref[...] + rb_ref[...]) + _dot(a_ref[...], w_ref[...])
        xh, rstd = _ln_stats(r)
        h = xh * g_ref[...] + b_ref[...]
        h_ref[...] = h
        hb_ref[...] = h.astype(BF16)
        xh_ref[...] = xh
        rstd_ref[...] = rstd

    row = pl.BlockSpec((tm, D), lambda i: (i, 0))
    vec = pl.BlockSpec((1, D), lambda i: (0, 0))
    return pl.pallas_call(
        body, name=name, grid=(S // tm,),
        in_specs=[row, vec, vec, row, pl.BlockSpec((D, D), lambda i: (0, 0)), vec, vec],
        out_specs=[row, row, row, pl.BlockSpec((tm, 1), lambda i: (i, 0))],
        out_shape=[jax.ShapeDtypeStruct((S, D), F32), jax.ShapeDtypeStruct((S, D), BF16),
                   jax.ShapeDtypeStruct((S, D), F32), jax.ShapeDtypeStruct((S, 1), F32)],
        compiler_params=_cp(("parallel",)),
    )(res_xh, res_g, res_b, a, w, ln_g, ln_b)


def _dh1_ln_bwd(dproj, w_in, dr2, xh, rstd, ln_g, name, after=()):
    tm, tk, ch = 1024, IN_SH, EPI_ROWS
    nk = IN_W // tk
    na = len(after)

    def body(*refs):
        a_ref, b_ref, add_ref, xh_ref, rstd_ref, g_ref, dr_ref, dg_ref, db_ref, acc_ref = refs[na:]
        i = pl.program_id(0)
        k = pl.program_id(1)
        p = _dot_nt(a_ref[...], b_ref[...])

        @pl.when(k == 0)
        def _():
            acc_ref[...] = p

        @pl.when(k > 0)
        def _():
            acc_ref[...] += p

        @pl.when(jnp.logical_and(k == nk - 1, i == 0))
        def _():
            dg_ref[...] = jnp.zeros_like(dg_ref)
            db_ref[...] = jnp.zeros_like(db_ref)

        @pl.when(k == nk - 1)
        def _():
            for c0 in range(0, tm, ch):
                rows = slice(c0, c0 + ch)
                dy = acc_ref[rows, :] + ALPHA * add_ref[rows, :]
                xhv = xh_ref[rows, :]
                dr_ref[rows, :] = _ln_dx(dy * g_ref[...], xhv, rstd_ref[rows, :])
                dg_ref[...] += jnp.sum(dy * xhv, axis=0, keepdims=True)
                db_ref[...] += jnp.sum(dy, axis=0, keepdims=True)

    row = pl.BlockSpec((tm, D), lambda i, k: (i, 0))
    vec = pl.BlockSpec((1, D), lambda i, k: (0, 0))
    return pl.pallas_call(
        body, name=name, grid=(S // tm, nk),
        in_specs=[pl.BlockSpec(memory_space=pl.ANY)] * na
        + [pl.BlockSpec((tm, tk), lambda i, k: (i, k)), pl.BlockSpec((D, tk), lambda i, k: (0, k)), row, row,
           pl.BlockSpec((tm, 1), lambda i, k: (i, 0)), vec],
        out_specs=[row, vec, vec],
        out_shape=[jax.ShapeDtypeStruct((S, D), F32), jax.ShapeDtypeStruct((1, D), F32),
                   jax.ShapeDtypeStruct((1, D), F32)],
        scratch_shapes=[pltpu.VMEM((tm, D), F32)],
        compiler_params=_cp(("arbitrary", "arbitrary"), vmem_mib=56),
    )(*after, dproj, w_in, dr2, xh, rstd, ln_g)


def _ln_bwd(dout, xh, rstd, ln_g, name, after=()):
    tm = 512
    na = len(after)

    def body(*refs):
        y_ref, xh_ref, rstd_ref, g_ref, dr_ref, dg_ref, db_ref = refs[na:]
        dy = y_ref[...]
        i = pl.program_id(0)
        xh = xh_ref[...]
        dr_ref[...] = _ln_dx(dy * g_ref[...], xh, rstd_ref[...])
        dg = jnp.sum(dy * xh, axis=0, keepdims=True)
        db = jnp.sum(dy, axis=0, keepdims=True)

        @pl.when(i == 0)
        def _():
            dg_ref[...] = dg
            db_ref[...] = db

        @pl.when(i > 0)
        def _():
            dg_ref[...] += dg
            db_ref[...] += db

    row = pl.BlockSpec((tm, D), lambda i: (i, 0))
    vec = pl.BlockSpec((1, D), lambda i: (0, 0))
    return pl.pallas_call(
        body, name=name, grid=(S // tm,),
        in_specs=[pl.BlockSpec(memory_space=pl.ANY)] * na + [row, row, pl.BlockSpec((tm, 1), lambda i: (i, 0)), vec],
        out_specs=[row, vec, vec],
        out_shape=[jax.ShapeDtypeStruct((S, D), F32), jax.ShapeDtypeStruct((1, D), F32),
                   jax.ShapeDtypeStruct((1, D), F32)],
        compiler_params=_cp(("arbitrary",)),
    )(*after, dout, xh, rstd, ln_g)


ROPE_TM = 256


def _rope_tables(pos_ref, invf_ref, sign):
    ang = pos_ref[...] * invf_ref[...]
    lane = lax.broadcasted_iota(jnp.int32, ang.shape, 1)
    first = (lane % DH) < (DH // 2)
    sinv = jnp.sin(ang) * sign
    return first, jnp.cos(ang), jnp.where(first, -sinv, sinv)


def _rotate(x, first, cosf, sinf):
    return x * cosf + jnp.where(first, pltpu.roll(x, 96, 1), pltpu.roll(x, 32, 1)) * sinf


def _proj_qkv_rope(hb, w_in, pos_f, invf, name):
    tm = 2 * ROPE_TM

    def body(h_ref, w_ref, pos_ref, invf_ref, o0_ref, o1_ref, o2_ref, buf_ref):
        rot = pl.program_id(1) < 2
        first, cosf, sinf = _rope_tables(pos_ref, invf_ref, 1.0)
        cosf = jnp.where(rot, cosf, 1.0)
        sinf = jnp.where(rot, sinf, 0.0)
        acc = _dot(h_ref[...], w_ref[...])
        for gi, (d, o_ref) in enumerate(zip(DILATIONS, (o0_ref, o1_ref, o2_ref))):
            for ch in range(GRP_W // 128):
                cols = slice(ch * 128, (ch + 1) * 128)
                x = _rotate(acc[:, gi * GRP_W + ch * 128: gi * GRP_W + (ch + 1) * 128], first, cosf, sinf)
                if d == 1:
                    o_ref[0, :, cols] = x.astype(BF16)
                else:
                    buf_ref[...] = x
                    for r in range(d):
                        o_ref[r, :, cols] = buf_ref[pl.ds(r, tm // d, stride=d), :].astype(BF16)

    return pl.pallas_call(
        body, name=name, grid=(S // tm, 3),
        in_specs=[pl.BlockSpec((tm, D), lambda i, s: (i, 0)), pl.BlockSpec((D, ATT_W), lambda i, s: (0, s)),
                  pl.BlockSpec((tm, 1), lambda i, s: (i, 0)), pl.BlockSpec((1, 128), lambda i, s: (0, 0))],
        out_specs=[pl.BlockSpec((d, tm // d, GRP_W), lambda i, s: (0, i, s)) for d in DILATIONS],
        out_shape=[jax.ShapeDtypeStruct((d, S // d, 3 * GRP_W), BF16) for d in DILATIONS],
        scratch_shapes=[pltpu.VMEM((tm, 128), F32)],
        compiler_params=_cp(("parallel", "parallel")),
    )(hb, w_in, pos_f, invf)


def _rope_bwd(dqkv_c, pos_f, invf, name):
    tm = ROPE_TM

    def body(*refs):
        g_refs, (pos_ref, invf_ref, o_ref, buf_ref) = refs[:9], refs[9:]
        first, cosf, sinf = _rope_tables(pos_ref, invf_ref, -1.0)
        for sec in range(3):
            for gi, d in enumerate(DILATIONS):
                g_ref = g_refs[3 * gi + sec]
                for ch in range(GRP_W // 128):
                    cols = slice(ch * 128, (ch + 1) * 128)
                    if d == 1:
                        x = g_ref[0, :, cols]
                    else:
                        for r in range(d):
                            buf_ref[pl.ds(r, tm // d, stride=d), :] = g_ref[r, :, cols]
                        x = buf_ref[...]
                    if sec < 2:
                        x = _rotate(x, first, cosf, sinf)
                    dst = sec * ATT_W + gi * GRP_W + ch * 128
                    o_ref[:, dst:dst + 128] = x.astype(BF16)

    g_specs = [pl.BlockSpec((d, tm // d, GRP_W), lambda i: (0, i, 0)) for d in DILATIONS for _ in range(3)]
    return pl.pallas_call(
        body, name=name, grid=(S // tm,),
        in_specs=g_specs + [pl.BlockSpec((tm, 1), lambda i: (i, 0)), pl.BlockSpec((1, 128), lambda i: (0, 0))],
        out_specs=pl.BlockSpec((tm, 3 * ATT_W), lambda i: (i, 0)),
        out_shape=jax.ShapeDtypeStruct((S, 3 * ATT_W), BF16),
        scratch_shapes=[pltpu.VMEM((tm, 128), F32)],
        compiler_params=_cp(("parallel",)),
    )(*[g for grp in dqkv_c for g in grp], pos_f, invf)


def _class_order(ts, name):
    tm = ROPE_TM
    n = len(ts)

    def body(*refs):
        buf_ref = refs[3 * n]
        for a in range(n):
            for ch in range(GRP_W // 128):
                cols = slice(ch * 128, (ch + 1) * 128)
                buf_ref[...] = refs[a][:, cols]
                for b, d in enumerate(DILATIONS[1:]):
                    for r in range(d):
                        refs[n + 2 * a + b][r, :, cols] = buf_ref[pl.ds(r, tm // d, stride=d), :]

    return pl.pallas_call(
        body, name=name, grid=(S // tm,),
        in_specs=[pl.BlockSpec((tm, GRP_W), lambda i: (i, 0))] * n,
        out_specs=[pl.BlockSpec((d, tm // d, GRP_W), lambda i: (0, i, 0)) for _ in range(n) for d in DILATIONS[1:]],
        out_shape=[jax.ShapeDtypeStruct((d, S // d, GRP_W), F32) for _ in range(n) for d in DILATIONS[1:]],
        scratch_shapes=[pltpu.VMEM((tm, 128), F32)],
        compiler_params=_cp(("parallel",)),
    )(*ts)


def _own_lanes(h):
    return (lax.broadcasted_iota(jnp.int32, (1, 2 * DH), 1) // DH) == (h % 2)


def _heads(ref):
    out = []
    for h in range(NH):
        pair = ref[:, (h // 2) * 2 * DH:(h // 2 + 1) * 2 * DH]
        out.append(jnp.where(_own_lanes(h), pair, jnp.zeros_like(pair)))
    return jnp.stack(out)


def _unheads(t3):
    return jnp.concatenate([t3[2 * p] + t3[2 * p + 1] for p in range(NH // 2)], axis=1)


def _bdot_nt(a, b):
    return lax.dot_general(a, b, (((2,), (2,)), ((0,), (0,))), preferred_element_type=F32)


def _bdot(a, b):
    return lax.dot_general(a, b, (((2,), (1,)), ((0,), (0,))), preferred_element_type=F32)


def _bdot_tn(a, b):
    return lax.dot_general(a, b, (((1,), (1,)), ((0,), (0,))), preferred_element_type=F32)


def _attn_fwd(gi, qkv_c, name):
    d = DILATIONS[gi]
    nblk = S // d // BLK

    def body(*refs):
        if nblk > 1:
            q_ref, kc_ref, kp_ref, vc_ref, vp_ref, o_ref, lse_ref = refs
            has_prev = pl.program_id(1) != 0
        else:
            q_ref, kc_ref, vc_ref, o_ref, lse_ref = refs
        qi = lax.broadcasted_iota(jnp.int32, (NH, BLK, BLK), 1)
        kj = lax.broadcasted_iota(jnp.int32, (NH, BLK, BLK), 2)
        q = _heads(q_ref)
        sc = jnp.where(kj <= qi, _bdot_nt(q, _heads(kc_ref)) * 0.125, NEG_INF)
        m = jnp.max(sc, axis=-1, keepdims=True)
        if nblk > 1:
            mask_p = jnp.logical_and(kj >= qi, has_prev)
            sp = jnp.where(mask_p, _bdot_nt(q, _heads(kp_ref)) * 0.125, NEG_INF)
            m = jnp.maximum(m, jnp.max(sp, axis=-1, keepdims=True))
        pc = jnp.exp(sc - m)
        l = jnp.sum(pc, axis=-1, keepdims=True)
        o = _bdot(pc.astype(BF16), _heads(vc_ref))
        if nblk > 1:
            pp = jnp.exp(sp - m)
            l = l + jnp.sum(pp, axis=-1, keepdims=True)
            o = o + _bdot(pp.astype(BF16), _heads(vp_ref))
        o_ref[...] = _unheads(o / l)
        lse = jnp.broadcast_to(m + jnp.log(l), (NH, BLK, 2 * DH))
        lse_ref[...] = _unheads(jnp.stack([jnp.where(_own_lanes(h), lse[h], 0.0) for h in range(NH)]))

    def cur(sec):
        return pl.BlockSpec((None, BLK, GRP_W), lambda r, n: (r, n, sec))

    def prev(sec):
        return pl.BlockSpec((None, BLK, GRP_W), lambda r, n: (r, jnp.maximum(n - 1, 0), sec))

    out = pl.BlockSpec((None, BLK, GRP_W), lambda r, n: (r, n, 0))
    shp = jax.ShapeDtypeStruct((d, S // d, GRP_W), F32)
    if nblk > 1:
        in_specs, args = [cur(0), cur(1), prev(1), cur(2), prev(2)], (qkv_c,) * 5
    else:
        in_specs, args = [cur(0), cur(1), cur(2)], (qkv_c,) * 3
    return pl.pallas_call(
        body, name=name, grid=(d, nblk), in_specs=in_specs, out_specs=[out, out], out_shape=[shp, shp],
        compiler_params=_cp(("parallel", "parallel")),
    )(*args)


def _attn_combine(os, lses, name):
    tm = ROPE_TM

    def body(o0_ref, o1_ref, o2_ref, l0_ref, l1_ref, l2_ref, y_ref, yt_ref, l_ref, buf_ref):
        def token_order(ref, d, cols, slot):
            if d == 1:
                return ref[0, :, cols]
            for r in range(d):
                buf_ref[slot, pl.ds(r, tm // d, stride=d), :] = ref[r, :, cols]
            return buf_ref[slot]

        for ch in range(GRP_W // 128):
            cols = slice(ch * 128, (ch + 1) * 128)
            o = [token_order(ref, d, cols, k) for k, (ref, d) in enumerate(zip((o0_ref, o1_ref, o2_ref), DILATIONS))]
            ls = [token_order(ref, d, cols, 3 + k)
                  for k, (ref, d) in enumerate(zip((l0_ref, l1_ref, l2_ref), DILATIONS))]
            m = jnp.maximum(jnp.maximum(ls[0], ls[1]), ls[2])
            e = [jnp.exp(l - m) for l in ls]
            den = e[0] + e[1] + e[2]
            y = (e[0] * o[0] + e[1] * o[1] + e[2] * o[2]) / den
            y_ref[:, cols] = y
            yt_ref[cols, :] = y.T.astype(BF16)
            l_ref[:, cols] = m + jnp.log(den)

    blk = pl.BlockSpec((tm, GRP_W), lambda i: (i, 0))
    cls = [pl.BlockSpec((d, tm // d, GRP_W), lambda i: (0, i, 0)) for d in DILATIONS]
    shp = jax.ShapeDtypeStruct((S, GRP_W), F32)
    return pl.pallas_call(
        body, name=name, grid=(S // tm,), in_specs=cls + cls,
        out_specs=[blk, pl.BlockSpec((GRP_W, tm), lambda i: (0, i)), blk],
        out_shape=[shp, jax.ShapeDtypeStruct((GRP_W, S), BF16), shp],
        scratch_shapes=[pltpu.VMEM((6, tm, 128), F32)],
        compiler_params=_cp(("parallel",)),
    )(*os, *lses)


def _attn_bwd(gi, qkv_c, dy_c, y_c, lse_c, name):
    d = DILATIONS[gi]
    nblk = S // d // BLK

    def body(*refs):
        if nblk > 1:
            (q_ref, qn_ref, k_ref, kp_ref, v_ref, vp_ref, dy_ref, dyn_ref, y_ref, yn_ref, l_ref, ln_ref,
             dq_ref, dk_ref, dv_ref) = refs
            n = pl.program_id(1)
            has_prev = n != 0
            has_next = n != nblk - 1
        else:
            q_ref, k_ref, v_ref, dy_ref, y_ref, l_ref, dq_ref, dk_ref, dv_ref = refs
        qi = lax.broadcasted_iota(jnp.int32, (NH, BLK, BLK), 1)
        kj = lax.broadcasted_iota(jnp.int32, (NH, BLK, BLK), 2)

        def lse_col(ref):
            return jnp.stack([ref[:, h * DH:h * DH + 1] for h in range(NH)])

        q, k, v = _heads(q_ref), _heads(k_ref), _heads(v_ref)
        dy = _heads(dy_ref)
        dd = jnp.sum(dy * _heads(y_ref), axis=-1, keepdims=True)
        lcol = lse_col(l_ref)
        dyb = dy.astype(BF16)
        p = jnp.exp(jnp.where(kj <= qi, _bdot_nt(q, k) * 0.125, NEG_INF) - lcol)
        ds = (p * (_bdot_nt(dyb, v) - dd)).astype(BF16)
        dq = _bdot(ds, k)
        dk = _bdot_tn(ds, q)
        dv = _bdot_tn(p.astype(BF16), dyb)
        if nblk > 1:
            qn, kpv, vpv = _heads(qn_ref), _heads(kp_ref), _heads(vp_ref)
            dyn = _heads(dyn_ref)
            ddn = jnp.sum(dyn * _heads(yn_ref), axis=-1, keepdims=True)
            lncol = lse_col(ln_ref)
            dynb = dyn.astype(BF16)
            mask_p = jnp.logical_and(kj >= qi, has_prev)
            pp = jnp.exp(jnp.where(mask_p, _bdot_nt(q, kpv) * 0.125, NEG_INF) - lcol)
            dsp = (pp * (_bdot_nt(dyb, vpv) - dd)).astype(BF16)
            dq = dq + _bdot(dsp, kpv)
            mask_n = jnp.logical_and(kj >= qi, has_next)
            pn = jnp.exp(jnp.where(mask_n, _bdot_nt(qn, k) * 0.125, NEG_INF) - lncol)
            dsn = (pn * (_bdot_nt(dynb, v) - ddn)).astype(BF16)
            dk = dk + _bdot_tn(dsn, qn)
            dv = dv + _bdot_tn(pn.astype(BF16), dynb)
        dq_ref[...] = _unheads(dq) * 0.125
        dk_ref[...] = _unheads(dk) * 0.125
        dv_ref[...] = _unheads(dv)

    def spec(sec, shift):
        def idx(r, n):
            return (r, jnp.clip(n + shift, 0, nblk - 1), sec)
        return pl.BlockSpec((None, BLK, GRP_W), idx)

    if nblk > 1:
        in_specs = [spec(0, 0), spec(0, 1), spec(1, 0), spec(1, -1), spec(2, 0), spec(2, -1),
                    spec(0, 0), spec(0, 1), spec(0, 0), spec(0, 1), spec(0, 0), spec(0, 1)]
        args = (qkv_c,) * 6 + (dy_c, dy_c, y_c, y_c, lse_c, lse_c)
    else:
        in_specs = [spec(0, 0), spec(1, 0), spec(2, 0), spec(0, 0), spec(0, 0), spec(0, 0)]
        args = (qkv_c, qkv_c, qkv_c, dy_c, y_c, lse_c)
    out = spec(0, 0)
    shp = jax.ShapeDtypeStruct((d, S // d, GRP_W), F32)
    return pl.pallas_call(
        body, name=name, grid=(d, nblk), in_specs=in_specs, out_specs=[out, out, out], out_shape=[shp, shp, shp],
        compiler_params=_cp(("parallel", "parallel")),
    )(*args)


_SQRT_HALF = 0.7071067811865476
_INV_SQRT_2PI = 0.3989422804014327


def _gelu(z):
    return 0.5 * z * (1.0 + lax.erf(z * _SQRT_HALF))


def _gelu_grad(z):
    return 0.5 * (1.0 + lax.erf(z * _SQRT_HALF)) + z * (jnp.exp(-0.5 * z * z) * _INV_SQRT_2PI)


def _tril_mask():
    t = lax.broadcasted_iota(jnp.int32, (BLK, BLK), 0)
    s = lax.broadcasted_iota(jnp.int32, (BLK, BLK), 1)
    return s <= t


def _groups(t):
    return jnp.stack([t[:, g * BLK:(g + 1) * BLK] for g in range(8)])


def _ungroup(t3):
    return jnp.concatenate([t3[g] for g in range(8)], axis=1)


def _group_bias(bs_ref):
    return jnp.stack([bs_ref[:, g:g + 1] for g in range(8)])


def _gmlp_fwd(z, ln_g, ln_b, w_s, b_s_t, name):
    def body(z_ref, g_ref, b_ref, ws_ref, bs_ref, y_ref, yt_ref):
        zg = _gelu(z_ref[...])
        u = zg[:, :GW]
        xh, _ = _ln_stats(zg[:, GW:])
        vn = (xh * g_ref[...] + b_ref[...]).astype(BF16)
        wt = jnp.where(_tril_mask(), ws_ref[...], 0.0).astype(BF16)
        yv = u * _ungroup(_bdot(wt, _groups(vn)) + _group_bias(bs_ref))
        y_ref[...] = yv.astype(BF16)
        yt_ref[...] = yv.T.astype(BF16)

    vec = pl.BlockSpec((1, GW), lambda n: (0, 0))
    return pl.pallas_call(
        body, name=name, grid=(NBLK,),
        in_specs=[pl.BlockSpec((BLK, 2 * GW), lambda n: (n, 0)), vec, vec,
                  pl.BlockSpec((8, BLK, BLK), lambda n: (0, 0, 0)), pl.BlockSpec((BLK, 8), lambda n: (0, 0))],
        out_specs=[pl.BlockSpec((BLK, GW), lambda n: (n, 0)), pl.BlockSpec((GW, BLK), lambda n: (0, n))],
        out_shape=[jax.ShapeDtypeStruct((S, GW), BF16), jax.ShapeDtypeStruct((GW, S), BF16)],
        compiler_params=_cp(("parallel",)),
    )(z, ln_g, ln_b, w_s, b_s_t)


def _gmlp_bwd(z, dy, ln_g, ln_b, w_s, b_s_t, name):
    def body(z_ref, dy_ref, g_ref, b_ref, ws_ref, bs_ref, dz_ref, dws_ref, dbs_ref, dg_ref, db_ref, dvn_ref):
        n = pl.program_id(0)
        zv = z_ref[...]
        zg = _gelu(zv)
        u = zg[:, :GW]
        xh, rstd = _ln_stats(zg[:, GW:])
        vn = (xh * g_ref[...] + b_ref[...]).astype(BF16)
        tril = _tril_mask()

        @pl.when(n == 0)
        def _():
            dws_ref[...] = jnp.zeros_like(dws_ref)
            dbs_ref[...] = jnp.zeros_like(dbs_ref)
            dg_ref[...] = jnp.zeros_like(dg_ref)
            db_ref[...] = jnp.zeros_like(db_ref)

        wt = jnp.where(tril, ws_ref[...], 0.0).astype(BF16)
        vn3 = _groups(vn)
        dyv = dy_ref[...]
        mixed = _ungroup(_bdot(wt, vn3) + _group_bias(bs_ref))
        dz_ref[:, :GW] = (dyv * mixed * _gelu_grad(zv[:, :GW])).astype(BF16)
        dmix3 = _groups(dyv * u)
        dmb = dmix3.astype(BF16)
        dws_ref[...] += jnp.where(tril, _bdot_nt(dmb, vn3), 0.0)
        dbsum = jnp.sum(dmix3, axis=-1, keepdims=True)
        for gg in range(8):
            dbs_ref[:, gg:gg + 1] += dbsum[gg]
        dvn_ref[...] = _ungroup(_bdot_tn(wt, dmb))

        dvn = dvn_ref[...]
        dg_ref[...] += jnp.sum(dvn * xh, axis=0, keepdims=True)
        db_ref[...] += jnp.sum(dvn, axis=0, keepdims=True)
        dvg = _ln_dx(dvn * g_ref[...], xh, rstd)
        dz_ref[:, GW:] = (dvg * _gelu_grad(zv[:, GW:])).astype(BF16)

    vec = pl.BlockSpec((1, GW), lambda n: (0, 0))
    ws = pl.BlockSpec((8, BLK, BLK), lambda n: (0, 0, 0))
    bs = pl.BlockSpec((BLK, 8), lambda n: (0, 0))
    return pl.pallas_call(
        body, name=name, grid=(NBLK,),
        in_specs=[pl.BlockSpec((BLK, 2 * GW), lambda n: (n, 0)), pl.BlockSpec((BLK, GW), lambda n: (n, 0)),
                  vec, vec, ws, bs],
        out_specs=[pl.BlockSpec((BLK, 2 * GW), lambda n: (n, 0)), ws, bs, vec, vec],
        out_shape=[jax.ShapeDtypeStruct((S, 2 * GW), BF16), jax.ShapeDtypeStruct((8, BLK, BLK), F32),
                   jax.ShapeDtypeStruct((BLK, 8), F32), jax.ShapeDtypeStruct((1, GW), F32),
                   jax.ShapeDtypeStruct((1, GW), F32)],
        scratch_shapes=[pltpu.VMEM((BLK, GW), F32)],
        compiler_params=_cp(("arbitrary",)),
    )(z, dy, ln_g, ln_b, w_s, b_s_t)


def _merge_fwd(a, b, gl, b_gates, name):
    tm = 512

    def body(a_ref, b_ref, g0_ref, g1_ref, bg_ref, o_ref, ot_ref):
        g0 = jax.nn.sigmoid(g0_ref[...] + bg_ref[:, :D])
        g1 = jax.nn.sigmoid(g1_ref[...] + bg_ref[:, D:])
        mg = g0 * a_ref[...] + g1 * b_ref[...]
        o_ref[...] = mg.astype(BF16)
        ot_ref[...] = mg.T.astype(BF16)

    row = pl.BlockSpec((tm, D), lambda i: (i, 0))
    return pl.pallas_call(
        body, name=name, grid=(S // tm,),
        in_specs=[row, row, row, pl.BlockSpec((tm, D), lambda i: (i, 1)), pl.BlockSpec((1, 2 * D), lambda i: (0, 0))],
        out_specs=[row, pl.BlockSpec((D, tm), lambda i: (0, i))],
        out_shape=[jax.ShapeDtypeStruct((S, D), BF16), jax.ShapeDtypeStruct((D, S), BF16)],
        compiler_params=_cp(("parallel",)),
    )(a, b, gl, gl, b_gates)


def _merge_bwd(dm, a, b, gl, b_gates, name):
    tm = 512

    def body(dm_ref, a_ref, b_ref, g0_ref, g1_ref, bg_ref, da_ref, db_ref, dgl_ref, dbg_ref):
        i = pl.program_id(0)
        dmv = dm_ref[...]
        g0 = jax.nn.sigmoid(g0_ref[...] + bg_ref[:, :D])
        g1 = jax.nn.sigmoid(g1_ref[...] + bg_ref[:, D:])
        da_ref[...] = (dmv * g0).astype(BF16)
        db_ref[...] = (dmv * g1).astype(BF16)
        d0 = dmv * a_ref[...] * g0 * (1.0 - g0)
        d1 = dmv * b_ref[...] * g1 * (1.0 - g1)
        dgl_ref[:, :D] = d0.astype(BF16)
        dgl_ref[:, D:] = d1.astype(BF16)
        s0 = jnp.sum(d0, axis=0, keepdims=True)
        s1 = jnp.sum(d1, axis=0, keepdims=True)

        @pl.when(i == 0)
        def _():
            dbg_ref[:, :D] = s0
            dbg_ref[:, D:] = s1

        @pl.when(i > 0)
        def _():
            dbg_ref[:, :D] += s0
            dbg_ref[:, D:] += s1

    row = pl.BlockSpec((tm, D), lambda i: (i, 0))
    wide = pl.BlockSpec((tm, 2 * D), lambda i: (i, 0))
    bg = pl.BlockSpec((1, 2 * D), lambda i: (0, 0))
    return pl.pallas_call(
        body, name=name, grid=(S // tm,),
        in_specs=[row, row, row, row, pl.BlockSpec((tm, D), lambda i: (i, 1)), bg],
        out_specs=[row, row, wide, bg],
        out_shape=[jax.ShapeDtypeStruct((S, D), BF16), jax.ShapeDtypeStruct((S, D), BF16),
                   jax.ShapeDtypeStruct((S, 2 * D), BF16), jax.ShapeDtypeStruct((1, 2 * D), F32)],
        compiler_params=_cp(("arbitrary",)),
    )(dm, a, b, gl, gl, b_gates)


def _adam_math(w, g, m, v):
    m2 = ADAM_B1 * m + (1.0 - ADAM_B1) * g
    v2 = ADAM_B2 * v + (1.0 - ADAM_B2) * (g * g)
    m_hat = m2 / (1.0 - ADAM_B1 ** ADAM_STEP)
    v_hat = v2 / (1.0 - ADAM_B2 ** ADAM_STEP)
    delta = -ADAM_LR * (m_hat / (jnp.sqrt(v_hat) + ADAM_EPS) + ADAM_WD * w)
    return delta, m2, v2


def _pick_rows(rows, cols, unit=16, budget=2 * MIB):
    best = unit
    for t in range(unit, rows + 1, unit):
        if rows % t == 0 and t * cols * 4 <= budget:
            best = t
    assert rows % best == 0
    return best


def _adamw(w, g, m, v, name):
    r, c = w.shape
    tr = _pick_rows(r, c, unit=8)

    def body(w_ref, g_ref, m_ref, v_ref, go_ref, d_ref, mo_ref, vo_ref):
        gv = g_ref[...]
        delta, m2, v2 = _adam_math(w_ref[...], gv, m_ref[...], v_ref[...])
        go_ref[...] = gv
        d_ref[...] = delta
        mo_ref[...] = m2
        vo_ref[...] = v2

    blk = pl.BlockSpec((tr, c), lambda i: (i, 0))
    shp = jax.ShapeDtypeStruct((r, c), F32)
    return pl.pallas_call(
        body, name=name, grid=(r // tr,), in_specs=[blk] * 4, out_specs=[blk] * 4, out_shape=[shp] * 4,
        compiler_params=_cp(("parallel",)),
    )(*[pltpu.with_memory_space_constraint(t, pltpu.HBM) for t in (w, g, m, v)])


def _small_sum_adamw(parts, own, pos, w, m, v, name):
    tr = 48

    def body(pos_ref, p_ref, own_ref, w_ref, m_ref, v_ref, g_ref, d_ref, mo_ref, vo_ref):
        me = 2 * pos_ref[1] + pos_ref[0]
        gv = None
        for k in range(8):
            term = jnp.where(me == k, own_ref[...], p_ref[k])
            gv = term if gv is None else gv + term
        delta, m2, v2 = _adam_math(w_ref[...], gv, m_ref[...], v_ref[...])
        g_ref[...] = gv
        d_ref[...] = delta
        mo_ref[...] = m2
        vo_ref[...] = v2

    blk = pl.BlockSpec((tr, D), lambda i, p: (i, 0))
    shp = jax.ShapeDtypeStruct((SMALL_ROWS, D), F32)
    return pl.pallas_call(
        body, name=name,
        grid_spec=pltpu.PrefetchScalarGridSpec(
            num_scalar_prefetch=1, grid=(SMALL_ROWS // tr,),
            in_specs=[pl.BlockSpec((8, tr, D), lambda i, p: (0, i, 0)), blk, blk, blk, blk],
            out_specs=[blk] * 4),
        out_shape=[shp] * 4,
        compiler_params=_cp(("parallel",)),
    )(pos, parts, own, w, m, v)


ANY = pl.BlockSpec(memory_space=pl.ANY)


def _in_hbm(arrays):
    return [pltpu.with_memory_space_constraint(a, pltpu.HBM) for a in arrays]


def _mesh_pos():
    x, y, c = lax.axis_index("x"), lax.axis_index("y"), lax.axis_index("c")
    chips = [(1 - x, y), (x, 1 - y), (1 - x, 1 - y)]
    return x, y, c, chips


def _place_shard(w, kind, pos, name):
    r, c = w.shape
    tr = _pick_rows(r, c)

    def body(pos_ref, w_ref, o_ref):
        o_ref[...] = w_ref[...].astype(BF16)

    if kind == "stack":
        o_spec = pl.BlockSpec((None, tr, c), lambda i, p: (p[1], i, 0))
        shape = (NSH, r, c)
    else:
        o_spec = pl.BlockSpec((tr, c), lambda i, p: (i, p[1]))
        shape = (r, NSH * c)
    return pl.pallas_call(
        body, name=name,
        grid_spec=pltpu.PrefetchScalarGridSpec(
            num_scalar_prefetch=1, grid=(r // tr,),
            in_specs=[pl.BlockSpec((tr, c), lambda i, p: (i, 0))], out_specs=o_spec),
        out_shape=pltpu.HBM(shape, BF16),
        compiler_params=_cp(("parallel",)),
    )(pos, pltpu.with_memory_space_constraint(w, pltpu.HBM))


SEM = pl.BlockSpec(memory_space=pltpu.SEMAPHORE)
SPLIT_COPY = pltpu.CompilerParams(has_side_effects=pltpu.SideEffectType.DATAFLOW_SIDE_EFFECTING)


def _shard_window(ref, kind, j, h, dims):
    r, c = dims
    rows = pl.ds(pl.multiple_of(h * (r // 2), 16), r // 2)
    if kind == "stack":
        return ref.at[j, rows, :]
    return ref.at[rows, pl.ds(pl.multiple_of(j * c, 128), c)]


def _ici_copy(ref, kind, dims, j, c, sems, idx, to):
    win = _shard_window(ref, kind, j, c, dims)
    return pltpu.make_async_remote_copy(src_ref=win, dst_ref=win, send_sem=sems[0].at[idx], recv_sem=sems[1].at[idx],
                                        device_id=to, device_id_type=MESH_T)


def _both_copy(ref, kind, dims, a, k, chip, half, tc, sc, sems):
    win = _shard_window(ref, kind, half[0], half[1], dims)
    return pltpu.make_async_remote_copy(src_ref=win, dst_ref=win, send_sem=sems[0].at[6 * a + 2 * k + tc],
                                        recv_sem=sems[1].at[6 * a + 2 * k + sc],
                                        device_id=(chip[0], chip[1], tc), device_id_type=MESH_T)


def _gather_start(fulls, kinds, dims, after, name, both=False):
    n, na = len(fulls), len(after)
    per = 6 if both else 3

    def body(*refs):
        outs = refs[n + na:2 * n + na]
        send_sems, recv_sems, token = refs[2 * n + na:]
        x, y, c, chips = _mesh_pos()
        for a in range(n):
            for k, chip in enumerate(chips):
                if both:
                    for tc in range(2):
                        _both_copy(outs[a], kinds[a], dims[a], a, k, chip, (2 * x + y, c), tc, c,
                                   (send_sems, recv_sems)).start()
                else:
                    _ici_copy(outs[a], kinds[a], dims[a], 2 * x + y, c, (send_sems, recv_sems), 3 * a + k,
                              (chip[0], chip[1], c)).start()
        token[...] = jnp.zeros_like(token)

    res = pl.pallas_call(
        body, name=name, in_specs=[ANY] * (n + na),
        out_specs=[ANY] * n + [SEM, SEM, pl.BlockSpec(memory_space=pltpu.VMEM)],
        out_shape=[pltpu.HBM(f.shape, BF16) for f in fulls]
        + [pltpu.SemaphoreType.DMA((per * n,)), pltpu.SemaphoreType.DMA((per * n,)),
           jax.ShapeDtypeStruct((8, 128), F32)],
        input_output_aliases={i: i for i in range(n)},
        compiler_params=SPLIT_COPY,
    )(*_in_hbm(fulls), *after)
    return res[:n], res[n], res[n + 1], res[n + 2]


def _gather_wait(fulls, send_sems, recv_sems, kinds, dims, after, name, both=False):
    n, na = len(fulls), len(after)

    def body(*refs):
        ssem, rsem = refs[n], refs[n + 1]
        outs = refs[n + 2 + na:]
        x, y, c, chips = _mesh_pos()
        for a in range(n):
            for k, chip in enumerate(chips):
                if both:
                    for oc in range(2):
                        _both_copy(outs[a], kinds[a], dims[a], a, k, chip, (2 * x + y, c), oc, c,
                                   (ssem, rsem)).wait_send()
                        _both_copy(outs[a], kinds[a], dims[a], a, k, chip, (2 * chip[0] + chip[1], oc), c, oc,
                                   (ssem, rsem)).wait_recv()
                    continue
                to = (chip[0], chip[1], c)
                _ici_copy(outs[a], kinds[a], dims[a], 2 * x + y, c, (ssem, rsem), 3 * a + k, to).wait_send()
                _ici_copy(outs[a], kinds[a], dims[a], 2 * chip[0] + chip[1], c, (ssem, rsem), 3 * a + k, to).wait_recv()

    return pl.pallas_call(
        body, name=name, in_specs=[ANY] * n + [SEM, SEM] + [ANY] * na, out_specs=[ANY] * n,
        out_shape=[pltpu.HBM(f.shape, BF16) for f in fulls],
        input_output_aliases={i: i for i in range(n)},
        compiler_params=SPLIT_COPY,
    )(*_in_hbm(fulls), send_sems, recv_sems, *after)


def _gather_forward(fulls, kinds, dims, name):
    n = len(fulls)

    def body(*refs):
        outs = refs[n:2 * n]
        sems = refs[2 * n:]
        x, y, c, chips = _mesh_pos()
        sib = (x, y, 1 - c)
        cps = []
        for a in range(n):
            for k, chip in enumerate(chips):
                cp = _ici_copy(outs[a], kinds[a], dims[a], 2 * chip[0] + chip[1], c, sems, 3 * a + k, sib)
                cp.start()
                cps.append(cp)
        for a in range(n):
            for k, chip in enumerate(chips):
                _ici_copy(outs[a], kinds[a], dims[a], 2 * chip[0] + chip[1], 1 - c, sems, 3 * a + k, sib).wait_recv()
        for cp in cps:
            cp.wait_send()

    return pl.pallas_call(
        body, name=name, in_specs=[ANY] * n, out_specs=[ANY] * n,
        out_shape=[pltpu.HBM(f.shape, BF16) for f in fulls],
        input_output_aliases={i: i for i in range(n)},
        scratch_shapes=[pltpu.SemaphoreType.DMA((3 * n,)), pltpu.SemaphoreType.DMA((3 * n,))],
    )(*_in_hbm(fulls))


def _pair_copy(src, land, a, x, y, c, sems):
    return pltpu.make_async_remote_copy(
        src_ref=src.at[1 - c], dst_ref=land, send_sem=sems[0].at[a], recv_sem=sems[1].at[a],
        device_id=(x, y, 1 - c), device_id_type=MESH_T)


def _pair_start(grads, lands, name):
    n = len(grads)

    def body(*refs):
        srcs, dsts = refs[2 * n:3 * n], refs[3 * n:4 * n]
        send_sems, recv_sems, token = refs[4 * n:]
        x, y, c, _ = _mesh_pos()
        for a in range(n):
            _pair_copy(srcs[a], dsts[a], a, x, y, c, (send_sems, recv_sems)).start()
        token[...] = jnp.zeros_like(token)

    res = pl.pallas_call(
        body, name=name, in_specs=[ANY] * (2 * n),
        out_specs=[ANY] * (2 * n) + [SEM, SEM, pl.BlockSpec(memory_space=pltpu.VMEM)],
        out_shape=[pltpu.HBM(g.shape, F32) for g in grads]
        + [pltpu.HBM(l.shape, F32) for l in lands]
        + [pltpu.SemaphoreType.DMA((n,)), pltpu.SemaphoreType.DMA((n,)), jax.ShapeDtypeStruct((8, 128), F32)],
        input_output_aliases={i: i for i in range(2 * n)},
        compiler_params=SPLIT_COPY,
    )(*_in_hbm(grads), *_in_hbm(lands))
    return res[:n], res[n:2 * n], res[2 * n], res[2 * n + 1], res[2 * n + 2]


def _pair_wait(grads, lands, send_sems, recv_sems, after, name):
    n, na = len(grads), len(after)

    def body(*refs):
        ssem, rsem = refs[2 * n], refs[2 * n + 1]
        outs = refs[2 * n + 2 + na:]
        x, y, c, _ = _mesh_pos()
        for a in range(n):
            cp = _pair_copy(outs[a], outs[n + a], a, x, y, c, (ssem, rsem))
            cp.wait_send()
            cp.wait_recv()

    res = pl.pallas_call(
        body, name=name, in_specs=[ANY] * (2 * n) + [SEM, SEM] + [ANY] * na, out_specs=[ANY] * (2 * n),
        out_shape=[pltpu.HBM(g.shape, F32) for g in grads]
        + [pltpu.HBM(l.shape, F32) for l in lands],
        input_output_aliases={i: i for i in range(2 * n)},
        compiler_params=SPLIT_COPY,
    )(*_in_hbm(grads), *_in_hbm(lands), send_sems, recv_sems, *after)
    return res[:n], res[n:]


def _pair_sum(g, recv, pos, name):
    _, _, rh, c = g.shape
    tr = _pick_rows(rh, c)

    def body(pos_ref, g_ref, r_ref, o_ref):
        o_ref[...] = (g_ref[...] + r_ref[...]).astype(BF16)

    return pl.pallas_call(
        body, name=name,
        grid_spec=pltpu.PrefetchScalarGridSpec(
            num_scalar_prefetch=1, grid=(3, rh // tr),
            in_specs=[pl.BlockSpec((None, None, tr, c), lambda k, r, p: (p[0], p[2 + k], r, 0)),
                      pl.BlockSpec((None, tr, c), lambda k, r, p: (p[2 + k], r, 0))],
            out_specs=pl.BlockSpec((None, tr, c), lambda k, r, p: (k, r, 0))),
        out_shape=pltpu.HBM((3, rh, c), BF16),
        compiler_params=_cp(("parallel", "parallel")),
    )(pos, *_in_hbm([g, recv]))


def _chip_copy(src, land, a, k, chip, c, sems):
    return pltpu.make_async_remote_copy(
        src_ref=src.at[k], dst_ref=land.at[k], send_sem=sems[0].at[3 * a + k],
        recv_sem=sems[1].at[3 * a + k], device_id=(chip[0], chip[1], c), device_id_type=MESH_T)


def _chip_start(psums, lands, name):
    n = len(psums)

    def body(*refs):
        srcs, dsts = refs[2 * n:3 * n], refs[3 * n:4 * n]
        send_sems, recv_sems, token = refs[4 * n:]
        x, y, c, chips = _mesh_pos()
        for a in range(n):
            for k, chip in enumerate(chips):
                _chip_copy(srcs[a], dsts[a], a, k, chip, c, (send_sems, recv_sems)).start()
        token[...] = jnp.zeros_like(token)

    res = pl.pallas_call(
        body, name=name, in_specs=[ANY] * (2 * n),
        out_specs=[ANY] * (2 * n) + [SEM, SEM, pl.BlockSpec(memory_space=pltpu.VMEM)],
        out_shape=[pltpu.HBM(p.shape, BF16) for p in psums]
        + [pltpu.HBM(l.shape, BF16) for l in lands]
        + [pltpu.SemaphoreType.DMA((3 * n,)), pltpu.SemaphoreType.DMA((3 * n,)), jax.ShapeDtypeStruct((8, 128), F32)],
        input_output_aliases={i: i for i in range(2 * n)},
        compiler_params=SPLIT_COPY,
    )(*_in_hbm(psums), *_in_hbm(lands))
    return res[:n], res[n:2 * n], res[2 * n], res[2 * n + 1], res[2 * n + 2]


def _chip_wait(psums, lands, send_sems, recv_sems, after, name):
    n, na = len(psums), len(after)

    def body(*refs):
        ssem, rsem = refs[2 * n], refs[2 * n + 1]
        outs = refs[2 * n + 2 + na:]
        srcs, dsts = outs[:n], outs[n:]
        x, y, c, chips = _mesh_pos()
        for a in range(n):
            for k, chip in enumerate(chips):
                cp = _chip_copy(srcs[a], dsts[a], a, k, chip, c, (ssem, rsem))
                cp.wait_send()
                cp.wait_recv()

    res = pl.pallas_call(
        body, name=name, in_specs=[ANY] * (2 * n) + [SEM, SEM] + [ANY] * na, out_specs=[ANY] * (2 * n),
        out_shape=[pltpu.HBM(p.shape, BF16) for p in psums]
        + [pltpu.HBM(l.shape, BF16) for l in lands],
        input_output_aliases={i: i for i in range(2 * n)},
        compiler_params=SPLIT_COPY,
    )(*_in_hbm(psums), *_in_hbm(lands), send_sems, recv_sems, *after)
    return res[n:]


def _owner_sum(g, recv_a, recv_b, pos, name):
    _, _, rh, c = g.shape
    tr = _pick_rows(rh, c)

    def body(pos_ref, g_ref, ra_ref, rb_ref, o_ref):
        acc = g_ref[...] + ra_ref[...]
        for k in range(3):
            acc = acc + rb_ref[k].astype(F32)
        o_ref[...] = acc

    return pl.pallas_call(
        body, name=name,
        grid_spec=pltpu.PrefetchScalarGridSpec(
            num_scalar_prefetch=1, grid=(rh // tr,),
            in_specs=[pl.BlockSpec((None, None, tr, c), lambda r, p: (p[0], p[1], r, 0)),
                      pl.BlockSpec((None, tr, c), lambda r, p: (p[1], r, 0)),
                      pl.BlockSpec((3, tr, c), lambda r, p: (0, r, 0))],
            out_specs=pl.BlockSpec((None, tr, c), lambda r, p: (p[0], r, 0))),
        out_shape=pltpu.HBM((2, rh, c), F32),
        compiler_params=_cp(("parallel",)),
    )(pos, *_in_hbm([g, recv_a, recv_b]))


def _sibling_allgather(halves, name):
    n = len(halves)

    def body(*refs):
        outs = refs[n:2 * n]
        send_sems, recv_sems = refs[2 * n:]
        x, y, c, _ = _mesh_pos()
        cps = []
        for a in range(n):
            cp = pltpu.make_async_remote_copy(
                src_ref=outs[a].at[c], dst_ref=outs[a].at[c], send_sem=send_sems.at[a], recv_sem=recv_sems.at[a],
                device_id=(x, y, 1 - c), device_id_type=MESH_T)
            cp.start()
            cps.append(cp)
        for a in range(n):
            cps[a].wait_send()
            pltpu.make_async_remote_copy(
                src_ref=outs[a].at[1 - c], dst_ref=outs[a].at[1 - c], send_sem=send_sems.at[a],
                recv_sem=recv_sems.at[a], device_id=(x, y, 1 - c), device_id_type=MESH_T).wait_recv()

    return pl.pallas_call(
        body, name=name, in_specs=[ANY] * n, out_specs=[ANY] * n,
        out_shape=[pltpu.HBM(h.shape, F32) for h in halves],
        input_output_aliases={i: i for i in range(n)},
        scratch_shapes=[pltpu.SemaphoreType.DMA((n,)), pltpu.SemaphoreType.DMA((n,))],
    )(*_in_hbm(halves))


def _peers(x, y, c):
    rel = [(0, 0, 1), (0, 1, 0), (0, 1, 1), (1, 0, 0), (1, 0, 1), (1, 1, 0), (1, 1, 1)]
    return [((1 - x) if dx else x, (1 - y) if dy else y, (1 - c) if dc else c) for dx, dy, dc in rel]


def _small_copy(src, land, k, peer, slot, sems):
    return pltpu.make_async_remote_copy(src_ref=src, dst_ref=land.at[slot], send_sem=sems[0].at[k],
                                        recv_sem=sems[1].at[k], device_id=peer, device_id_type=MESH_T)


def _small_start(part, land, name):
    def body(p_in, l_in, p_ref, l_ref, send_sems, recv_sems, token):
        x, y, c, _ = _mesh_pos()
        for k, peer in enumerate(_peers(x, y, c)):
            _small_copy(p_ref, l_ref, k, peer, 4 * x + 2 * y + c, (send_sems, recv_sems)).start()
        token[...] = jnp.zeros_like(token)

    return pl.pallas_call(
        body, name=name, in_specs=[ANY, ANY],
        out_specs=[ANY, ANY, SEM, SEM, pl.BlockSpec(memory_space=pltpu.VMEM)],
        out_shape=[pltpu.HBM(part.shape, F32), pltpu.HBM(land.shape, F32), pltpu.SemaphoreType.DMA((7,)),
                   pltpu.SemaphoreType.DMA((7,)), jax.ShapeDtypeStruct((8, 128), F32)],
        input_output_aliases={0: 0, 1: 1},
        compiler_params=SPLIT_COPY,
    )(*_in_hbm([part, land]))


def _small_wait(part, land, send_sems, recv_sems, after, name):
    na = len(after)

    def body(*refs):
        ssem, rsem = refs[2], refs[3]
        p_ref, l_ref = refs[4 + na:]
        x, y, c, _ = _mesh_pos()
        for k, peer in enumerate(_peers(x, y, c)):
            cp = _small_copy(p_ref, l_ref, k, peer, 4 * peer[0] + 2 * peer[1] + peer[2], (ssem, rsem))
            cp.wait_send()
            cp.wait_recv()

    return pl.pallas_call(
        body, name=name, in_specs=[ANY, ANY, SEM, SEM] + [ANY] * na, out_specs=[ANY, ANY],
        out_shape=[pltpu.HBM(part.shape, F32), pltpu.HBM(land.shape, F32)],
        input_output_aliases={0: 0, 1: 1},
        compiler_params=SPLIT_COPY,
    )(*_in_hbm([part, land]), send_sems, recv_sems, *after)


def _pack_small(ln1_g, ln1_b, gln_g, gln_b, ln2_g, ln2_b, ln3_g, ln3_b, b_gates, b_s, w_s):
    rows = [ln1_g, ln1_b, gln_g, gln_b, ln2_g, ln2_b, ln3_g, ln3_b]
    rows = [r.reshape(1, D) for r in rows] + [b_gates.reshape(2, D), b_s.reshape(1, D), jnp.zeros((5, D), F32),
                                             w_s.reshape(128, D)]
    return jnp.concatenate(rows, axis=0)


def _unpack_small(p):
    out = [p[i:i + 1] for i in range(8)]
    return out + [p[8:10].reshape(1, 2 * D), p[10:11].reshape(1, 8, BLK), p[16:144].reshape(1, 8, BLK, BLK)]


GROUPS = (("f1g", "f1u", "f1d"), ("w_in",), ("w_ab", "w_gb", "w_out"), ("f2g", "f2u", "f2d"))
LATE_GROUPS = (2, 3)


def _local_step(x, pos_f, target, P, weights_of, grads_ready, flush, small_ready):
    invf = ROPE_THETA ** (-jnp.arange(0, DH, 2, dtype=F32) / DH)
    invf = jnp.tile(invf, 4).reshape(1, 128)
    b_s_t = P["gmlp_b_s"].T

    W = dict(weights_of(0, []))
    h1b, xh1, rstd1, a1, b1, h1t = _ffn_fwd(x, W["f1g"], W["f1u"], W["f1d"], P["ln1_g"], P["ln1_b"], "ffn1_fwd",
                                                emit_t=True)
    W.update(weights_of(1, [h1b]))
    qkv_c = _proj_qkv_rope(h1b, W["w_in"], pos_f, invf, "proj_qkv_rope")
    z = _matmul(h1b, W["w_in"], "nn", "proj_z", n=2 * GW, b_col0=3 * ATT_W, tm=S, tn=512)
    gl = _matmul(h1b, W["w_in"], "nn", "proj_gates", n=2 * D, b_col0=3 * ATT_W + 2 * GW, tm=S, tn=512)
    og = [_attn_fwd(gi, qkv_c[gi], "attn_fwd_g%d" % gi) for gi in range(NG)]
    y_attn, y_attn_t, lse = _attn_combine([o for o, _ in og], [l for _, l in og], "attn_combine")
    y_gmlp, y_gmlp_t = _gmlp_fwd(z, P["gmlp_ln_g"], P["gmlp_ln_b"], P["gmlp_w_s"], b_s_t, "gmlp_fwd")
    W.update(weights_of(2, [y_gmlp]))
    br_a = _matmul(y_attn, W["w_ab"], "nn", "branch_attn", n=D, tm=1024, tn=D)
    br_b = _matmul(y_gmlp, W["w_gb"], "nn", "branch_gmlp", n=D, tm=1024, tn=D)
    merged, merged_t = _merge_fwd(br_a, br_b, gl, P["b_gates"], "merge_fwd")
    h2, h2b, xh2, rstd2 = _resid_ln(xh1, P["ln1_g"], P["ln1_b"], merged, W["w_out"], P["ln2_g"], P["ln2_b"],
                                    "mix_resid_ln2")
    W.update(weights_of(3, [h2b]))
    dr3, a2, b2, dg3, db3, loss = _ffn_fwd(h2, W["f2g"], W["f2u"], W["f2d"], P["ln3_g"], P["ln3_b"],
                                           "ffn2_fwd_loss", target=target)

    g_f2g, g_f2u, g_f2d, dh2 = _ffn_bwd(dr3, h2b, a2, b2, W["f2g"], W["f2u"], W["f2d"], "ffn2_bwd")
    tok = grads_ready(3, dict(f2g=g_f2g, f2u=g_f2u, f2d=g_f2d))
    dr2, dg2, db2 = _ln_bwd(dh2, xh2, rstd2, P["ln2_g"], "ln2_bwd", after=tok)
    g_wout = _wgrad(merged_t, dr2, 128, D, "dw_out", row_sharded=True)
    dmerged = _matmul(dr2, W["w_out"], "nt", "dmerged", n=D, tm=1024, tn=D)
    dab, dbb, dglb, dbg = _merge_bwd(dmerged, br_a, br_b, gl, P["b_gates"], "merge_bwd")
    tok = flush([dab])
    g_wab = _wgrad(y_attn_t, dab, GRP_W // 2, 256, "dw_attn_branch", row_sharded=False, after=tok)
    g_wgb = _wgrad(y_gmlp_t, dbb, 128, D, "dw_gmlp_branch", row_sharded=True)
    tok = grads_ready(2, dict(w_ab=g_wab, w_gb=g_wgb, w_out=g_wout))
    dy_attn = _matmul(dab, W["w_ab"], "nt", "dy_attn", n=GRP_W, tm=1024, tn=GRP_W, after=tok)
    dy_gmlp = _matmul(dbb, W["w_gb"], "nt", "dy_gmlp", n=GW, tm=1024, tn=GW)
    dzb, dws, dbs_t, dgln_g, dgln_b = _gmlp_bwd(z, dy_gmlp, P["gmlp_ln_g"], P["gmlp_ln_b"], P["gmlp_w_s"], b_s_t,
                                                 "gmlp_bwd")
    cls = _class_order([dy_attn, y_attn, lse], "attn_class_order")
    dqkv_c = []
    for gi in range(NG):
        dy_c, y_c, lse_c = [t[None] if gi == 0 else cls[2 * a + gi - 1] for a, t in enumerate((dy_attn, y_attn, lse))]
        dqkv_c.append(_attn_bwd(gi, qkv_c[gi], dy_c, y_c, lse_c, "attn_bwd_g%d" % gi))
    dqkvb = _rope_bwd(dqkv_c, pos_f, invf, "rope_bwd")
    dproj = jnp.concatenate([dqkvb, dzb, dglb], axis=1)
    tok = flush([dproj])
    g_win = _wgrad(h1t, dproj, D // 2, IN_SH, "dw_in", row_sharded=False, after=tok)
    tok = grads_ready(1, dict(w_in=g_win))
    dr1, dg1, db1 = _dh1_ln_bwd(dproj, W["w_in"], dr2, xh1, rstd1, P["ln1_g"], "dh1_ln1_bwd", after=tok)
    tok = flush([dr1])
    tok = tok + small_ready(_pack_small(dg1, db1, dgln_g, dgln_b, dg2, db2, dg3, db3, dbg, dbs_t.T, dws))
    g_f1g, g_f1u, g_f1d, dx = _ffn_bwd(dr1, x.astype(BF16), a1, b1, W["f1g"], W["f1u"], W["f1d"], "ffn1_bwd",
                                       after=tok)
    grads_ready(0, dict(f1g=g_f1g, f1u=g_f1u, f1d=g_f1d))
    flush([dx])
    return loss, dx


TRANSPOSED = ("f1g", "f1u", "f2g", "f2u")
KIND = dict(f1g="stack", f1u="stack", f1d="stack", w_in="col", w_ab="col", w_gb="stack", w_out="stack",
            f2g="stack", f2u="stack", f2d="stack")


def kernel(x, positions, ffn1_w_gate, ffn1_w_up, ffn1_w_down, ln1_g, ln1_b, w_in, b_gates, gmlp_ln_g, gmlp_ln_b, gmlp_w_s, gmlp_b_s, w_attn_branch, w_gmlp_branch, w_out, ln2_g, ln2_b, ffn2_w_gate, ffn2_w_up, ffn2_w_down, ln3_g, ln3_b, loss_target, m_ffn1_w_gate, m_ffn1_w_up, m_ffn1_w_down, m_ln1_g, m_ln1_b, m_w_in, m_b_gates, m_gmlp_ln_g, m_gmlp_ln_b, m_gmlp_w_s, m_gmlp_b_s, m_w_attn_branch, m_w_gmlp_branch, m_w_out, m_ln2_g, m_ln2_b, m_ffn2_w_gate, m_ffn2_w_up, m_ffn2_w_down, m_ln3_g, m_ln3_b, v_ffn1_w_gate, v_ffn1_w_up, v_ffn1_w_down, v_ln1_g, v_ln1_b, v_w_in, v_b_gates, v_gmlp_ln_g, v_gmlp_ln_b, v_gmlp_w_s, v_gmlp_b_s, v_w_attn_branch, v_w_gmlp_branch, v_w_out, v_ln2_g, v_ln2_b, v_ffn2_w_gate, v_ffn2_w_up, v_ffn2_w_down, v_ln3_g, v_ln3_b):
    cx, cy, cc = lax.axis_index("x"), lax.axis_index("y"), lax.axis_index("c")
    pos = jnp.stack([cc, 2 * cx + cy, 2 * (1 - cx) + cy, 2 * cx + 1 - cy, 2 * (1 - cx) + 1 - cy]).astype(jnp.int32)

    w_sh = dict(f1g=ffn1_w_gate, f1u=ffn1_w_up, f1d=ffn1_w_down, w_in=w_in, w_ab=w_attn_branch,
                w_gb=w_gmlp_branch, w_out=w_out, f2g=ffn2_w_gate, f2u=ffn2_w_up, f2d=ffn2_w_down)
    m_sh = dict(f1g=m_ffn1_w_gate, f1u=m_ffn1_w_up, f1d=m_ffn1_w_down, w_in=m_w_in, w_ab=m_w_attn_branch,
                w_gb=m_w_gmlp_branch, w_out=m_w_out, f2g=m_ffn2_w_gate, f2u=m_ffn2_w_up, f2d=m_ffn2_w_down)
    v_sh = dict(f1g=v_ffn1_w_gate, f1u=v_ffn1_w_up, f1d=v_ffn1_w_down, w_in=v_w_in, w_ab=v_w_attn_branch,
                w_gb=v_w_gmlp_branch, w_out=v_w_out, f2g=v_ffn2_w_gate, f2u=v_ffn2_w_up, f2d=v_ffn2_w_down)
    w_sh = {k: (v[0].T if k in TRANSPOSED else v[0]) for k, v in w_sh.items()}
    m_sh = {k: (v[0].T if k in TRANSPOSED else v[0]) for k, v in m_sh.items()}
    v_sh = {k: (v[0].T if k in TRANSPOSED else v[0]) for k, v in v_sh.items()}

    started, tokens = [], []
    for gi, names in enumerate(GROUPS):
        placed = [_place_shard(w_sh[k], KIND[k], pos, "place_" + k) for k in names]
        fulls, ssem, rsem, token = _gather_start(placed, [KIND[k] for k in names], [w_sh[k].shape for k in names],
                                                 tokens[-1:], "gather_start_g%d" % gi, both=gi in LATE_GROUPS)
        started.append((fulls, ssem, rsem))
        tokens.append(token)

    def weights_of(gi, after):
        names = GROUPS[gi]
        kinds, dims = [KIND[k] for k in names], [w_sh[k].shape for k in names]
        fulls, ssem, rsem = started[gi]
        fulls = _gather_wait(fulls, ssem, rsem, kinds, dims, list(after) + (tokens if gi == 0 else []),
                             "gather_wait_g%d" % gi, both=gi in LATE_GROUPS)
        if gi not in LATE_GROUPS:
            fulls = _gather_forward(fulls, kinds, dims, "gather_forward_g%d" % gi)
        return {k: (f.reshape(D, D) if k in ("w_gb", "w_out") else f) for k, f in zip(names, fulls)}

    pending, inflight = [], {}

    def grads_ready(gi, gd):
        grads = [gd[k] for k in GROUPS[gi]]
        lands = [lax.empty(g.shape[1:], F32) for g in grads]
        grads, lands, ssem, rsem, token = _pair_start(grads, lands, "rs_pair_start_g%d" % gi)
        pending.append((gi, grads, lands, ssem, rsem))
        return [token]

    def flush(after):
        gi, grads, lands, ssem, rsem = pending.pop()
        names = GROUPS[gi]
        grads, recv_a = _pair_wait(grads, lands, ssem, rsem, after, "rs_pair_wait_g%d" % gi)
        psums = [_pair_sum(g, r, pos, "rs_pair_sum_" + k) for g, r, k in zip(grads, recv_a, names)]
        lands = [lax.empty((3,) + p.shape[1:], BF16) for p in psums]
        psums, lands, ssem, rsem, token = _chip_start(psums, lands, "rs_chip_start_g%d" % gi)
        inflight[gi] = (grads, recv_a, psums, lands, ssem, rsem, token)
        return [token]

    P = dict(ln1_g=ln1_g, ln1_b=ln1_b, ln2_g=ln2_g, ln2_b=ln2_b, ln3_g=ln3_g, ln3_b=ln3_b, b_gates=b_gates,
             gmlp_ln_g=gmlp_ln_g, gmlp_ln_b=gmlp_ln_b, gmlp_w_s=gmlp_w_s[0], gmlp_b_s=gmlp_b_s[0])
    pos_f = positions.reshape(S, 1).astype(F32)
    small_state = []

    def small_ready(packed):
        land = jnp.zeros((8, SMALL_ROWS, D), F32)
        packed, land, ssem, rsem, token = _small_start(packed, land, "small_start")
        small_state.append((packed, land, ssem, rsem))
        return [token]

    loss_part, dx = _local_step(x[0], pos_f, loss_target[0], P, weights_of, grads_ready, flush, small_ready)
    loss = lax.psum(loss_part[0, 0], ("x", "y", "c"))

    g_out, d_out, m_out, v_out = {}, {}, {}, {}

    def finish(gis, after, tag):
        names, halves = [], []
        for gi in gis:
            grads, recv_a, psums, lands, ssem, rsem, token = inflight[gi]
            recv_b = _chip_wait(psums, lands, ssem, rsem, after + [inflight[0][6]], "rs_chip_wait_g%d" % gi)
            halves += [_owner_sum(g, ra, rb, pos, "rs_owner_sum_" + k)
                       for g, ra, rb, k in zip(grads, recv_a, recv_b, GROUPS[gi])]
            names += GROUPS[gi]
            after = halves[-1:]
        reduced = _sibling_allgather(halves, "rs_sibling_allgather_" + tag)
        for k, gfull in zip(names, reduced):
            res = _adamw(w_sh[k], gfull.reshape(w_sh[k].shape), m_sh[k], v_sh[k], "adamw_" + k)
            after = [res[1]]
            if k in TRANSPOSED:
                res = [r.T for r in res]
            g_out[k], d_out[k], m_out[k], v_out[k] = [r[None] for r in res]
        return after

    after = finish((3, 2, 1), [], "g321")

    small, parts = _small_wait(*small_state[0], after, "small_wait")
    sp = (ln1_g, ln1_b, gmlp_ln_g, gmlp_ln_b, ln2_g, ln2_b, ln3_g, ln3_b, b_gates, gmlp_b_s, gmlp_w_s)
    sm = (m_ln1_g, m_ln1_b, m_gmlp_ln_g, m_gmlp_ln_b, m_ln2_g, m_ln2_b, m_ln3_g, m_ln3_b, m_b_gates, m_gmlp_b_s,
          m_gmlp_w_s)
    sv = (v_ln1_g, v_ln1_b, v_gmlp_ln_g, v_gmlp_ln_b, v_ln2_g, v_ln2_b, v_ln3_g, v_ln3_b, v_b_gates, v_gmlp_b_s,
          v_gmlp_w_s)
    sg, sd, smn, svn = _small_sum_adamw(parts, small, pos, _pack_small(*sp), _pack_small(*sm), _pack_small(*sv),
                                        "small_adamw")
    names = ("ln1_g", "ln1_b", "gmlp_ln_g", "gmlp_ln_b", "ln2_g", "ln2_b", "ln3_g", "ln3_b", "b_gates", "gmlp_b_s",
             "gmlp_w_s")
    for dst, packed in ((g_out, sg), (d_out, sd), (m_out, smn), (v_out, svn)):
        for nm, val in zip(names, _unpack_small(packed)):
            dst[nm] = val
    finish((0,), [sg], "g0")

    order = ("f1g", "f1u", "f1d", "ln1_g", "ln1_b", "w_in", "b_gates", "gmlp_ln_g", "gmlp_ln_b", "gmlp_w_s", "gmlp_b_s",
             "w_ab", "w_gb", "w_out", "ln2_g", "ln2_b", "f2g", "f2u", "f2d", "ln3_g", "ln3_b")
    outs = [loss, dx[None]]
    for dst in (g_out, d_out, m_out, v_out):
        outs += [dst[k] for k in order]
    return tuple(outs)
```

```python
import jax
import jax.numpy as jnp
from jax import lax
from jax.experimental import pallas as pl
from jax.experimental.pallas import tpu as pltpu

F32 = jnp.float32
BF16 = jnp.bfloat16

S = 2048
D = 1024
NSH = 4
FSH = 704
ATT_W = 1536
GRP_W = 512
NG = 3
NH = 8
DH = 64
BLK = 128
NBLK = S // BLK
GW = 1024
IN_W = 8704
IN_SH = IN_W // NSH
ALPHA = 2.0 ** 0.25
LN_EPS = 1e-5
ROPE_THETA = 10000.0
DILATIONS = (1, 4, 16)
ADAM_LR, ADAM_B1, ADAM_B2, ADAM_EPS, ADAM_WD, ADAM_STEP = 0.001, 0.9, 0.999, 1e-08, 0.01, 10
SMALL_ROWS = 144
EPI_ROWS = 256
MESH_T = pl.DeviceIdType.MESH
MIB = 1024 * 1024
NEG_INF = float("-inf")


def _cp(sem, vmem_mib=48):
    return pltpu.CompilerParams(dimension_semantics=sem, vmem_limit_bytes=vmem_mib * MIB)


def _ln_stats(r):
    mu = jnp.mean(r, axis=-1, keepdims=True)
    xc = r - mu
    var = jnp.mean(xc * xc, axis=-1, keepdims=True)
    rstd = lax.rsqrt(var + LN_EPS)
    return xc * rstd, rstd


def _ln_dx(dxh, xh, rstd):
    m1 = jnp.mean(dxh, axis=-1, keepdims=True)
    m2 = jnp.mean(dxh * xh, axis=-1, keepdims=True)
    return rstd * (dxh - m1 - xh * m2)


def _dot_nt(a, b):
    return lax.dot_general(a, b, (((1,), (1,)), ((), ())), preferred_element_type=F32)


def _dot_tn(a, b):
    return lax.dot_general(a, b, (((0,), (0,)), ((), ())), preferred_element_type=F32)


def _dot(a, b):
    return jnp.dot(a, b, preferred_element_type=F32)


def _ffn_fwd(xin, wgt, wut, wd, ln_g, ln_b, name, emit_t=False, target=None):
    with_loss = target is not None
    tm = 1024

    def body(x_ref, wg_ref, wu_ref, wd_ref, g_ref, b_ref, *rest):
        if with_loss:
            t_ref, dr_ref, a_ref, bb_ref, dg_ref, db_ref, loss_ref, acc_ref = rest
        elif emit_t:
            hb_ref, xh_ref, rstd_ref, a_ref, bb_ref, ht_ref, acc_ref = rest
        else:
            hb_ref, xh_ref, rstd_ref, a_ref, bb_ref, acc_ref = rest
        i = pl.program_id(0)
        j = pl.program_id(1)
        xb = x_ref[...].astype(BF16)
        a = _dot_nt(xb, wg_ref[...])
        b = _dot_nt(xb, wu_ref[...])
        a_ref[...] = a.astype(BF16)
        bb_ref[...] = b.astype(BF16)
        s = (a * jax.nn.sigmoid(a)) * b
        f = _dot(s.astype(BF16), wd_ref[...])

        @pl.when(j == 0)
        def _():
            acc_ref[...] = f

        @pl.when(j > 0)
        def _():
            acc_ref[...] += f

        if with_loss:
            @pl.when(jnp.logical_and(j == NSH - 1, i == 0))
            def _():
                dg_ref[...] = jnp.zeros_like(dg_ref)
                db_ref[...] = jnp.zeros_like(db_ref)
                loss_ref[...] = jnp.zeros_like(loss_ref)

        @pl.when(j == NSH - 1)
        def _():
            for c0 in range(0, tm, EPI_ROWS):
                rows = slice(c0, c0 + EPI_ROWS)
                r = ALPHA * x_ref[rows, :] + 0.5 * acc_ref[rows, :]
                xh, rstd = _ln_stats(r)
                h = xh * g_ref[...] + b_ref[...]
                if with_loss:
                    err = h - t_ref[rows, :]
                    dy = err * (1.0 / D)
                    dr_ref[rows, :] = _ln_dx(dy * g_ref[...], xh, rstd)
                    dg_ref[...] += jnp.sum(dy * xh, axis=0, keepdims=True)
                    db_ref[...] += jnp.sum(dy, axis=0, keepdims=True)
                    part = 0.5 * jnp.sum(jnp.mean(err * err, axis=-1, keepdims=True), axis=0, keepdims=True)
                    loss_ref[...] += jnp.broadcast_to(part, (8, 128))
                else:
                    hb_ref[rows, :] = h.astype(BF16)
                    xh_ref[rows, :] = xh
                    rstd_ref[rows, :] = rstd
                    if emit_t:
                        ht_ref[:, rows] = h.T.astype(BF16)

    row = pl.BlockSpec((tm, D), lambda i, j: (i, 0))
    vec = pl.BlockSpec((1, D), lambda i, j: (0, 0))
    wsp = pl.BlockSpec((None, FSH, D), lambda i, j: (j, 0, 0))
    ab = pl.BlockSpec((None, tm, FSH), lambda i, j: (j, i, 0))
    ab_shape = jax.ShapeDtypeStruct((NSH, S, FSH), BF16)
    in_specs, args = [row, wsp, wsp, wsp, vec, vec], (xin, wgt, wut, wd, ln_g, ln_b)
    if with_loss:
        in_specs, args = in_specs + [row], args + (target,)
        out_specs = [row, ab, ab, vec, vec, pl.BlockSpec((8, 128), lambda i, j: (0, 0))]
        out_shape = [jax.ShapeDtypeStruct((S, D), F32), ab_shape, ab_shape, jax.ShapeDtypeStruct((1, D), F32),
                     jax.ShapeDtypeStruct((1, D), F32), jax.ShapeDtypeStruct((8, 128), F32)]
    else:
        out_specs = [row, row, pl.BlockSpec((tm, 1), lambda i, j: (i, 0)), ab, ab]
        out_shape = [jax.ShapeDtypeStruct((S, D), BF16), jax.ShapeDtypeStruct((S, D), F32),
                     jax.ShapeDtypeStruct((S, 1), F32), ab_shape, ab_shape]
        if emit_t:
            out_specs.append(pl.BlockSpec((D, tm), lambda i, j: (0, i)))
            out_shape.append(jax.ShapeDtypeStruct((D, S), BF16))
    return pl.pallas_call(
        body, name=name, grid=(S // tm, NSH), in_specs=in_specs, out_specs=out_specs, out_shape=out_shape,
        scratch_shapes=[pltpu.VMEM((tm, D), F32)],
        compiler_params=_cp(("arbitrary" if with_loss else "parallel", "arbitrary"), vmem_mib=56),
    )(*args)


def _ffn_bwd(dr, xin_b, a, b, wgt, wut, wd, name, after=()):
    tm = 512
    ni = S // tm
    hr = FSH // 2

    def body(dr_ref, a_ref, b_ref, wg_ref, wu_ref, wd_ref, x_hbm, *rest):
        dwg_hbm, dwu_hbm, dwd_hbm, dx_hbm, dx_acc, da_all, db_all, s_all, df_all, x_all, res_buf, sems = rest[len(after):]
        j = pl.program_id(0)
        i = pl.program_id(1)
        rows = pl.ds(pl.multiple_of(i * tm, tm), tm)

        @pl.when(jnp.logical_and(j == 0, i == 0))
        def _():
            cp = pltpu.make_async_copy(x_hbm, x_all, sems.at[0])
            cp.start()
            cp.wait()

        drv = dr_ref[...]
        df = (0.5 * drv).astype(BF16)

        @pl.when(j == 0)
        def _():
            df_all[rows, :] = df

        ds = jnp.concatenate([_dot_nt(df, wd_ref[0:384, :]), _dot_nt(df, wd_ref[384:FSH, :])], axis=1)
        av = a_ref[...].astype(F32)
        bv = b_ref[...].astype(F32)
        sig = jax.nn.sigmoid(av)
        sl = av * sig
        da = (ds * bv * (sig * (1.0 + av * (1.0 - sig)))).astype(BF16)
        db = (ds * sl).astype(BF16)
        da_all[rows, :] = da
        db_all[rows, :] = db
        s_all[rows, :] = (sl * bv).astype(BF16)
        dx = _dot(da, wg_ref[...]) + _dot(db, wu_ref[...])

        @pl.when(j == 0)
        def _():
            dx_acc[rows, :] = ALPHA * drv + dx

        @pl.when(j > 0)
        def _():
            dx_acc[rows, :] += dx

        @pl.when(i == ni - 1)
        def _():
            copies = []
            for n, (lhs, rhs, out) in enumerate(((da_all, x_all, dwg_hbm), (db_all, x_all, dwu_hbm),
                                                 (s_all, df_all, dwd_hbm))):
                slot = n % 2
                if n >= 2:
                    for cp in copies[2 * (n - 2): 2 * (n - 2) + 2]:
                        cp.wait()
                res_buf[slot] = _dot_tn(lhs[...], rhs[...])
                for h in range(2):
                    cp = pltpu.make_async_copy(res_buf.at[slot, pl.ds(h * hr, hr), :], out.at[h, j],
                                               sems.at[1 + 2 * slot + h])
                    cp.start()
                    copies.append(cp)
            for cp in copies[2:]:
                cp.wait()

        @pl.when(jnp.logical_and(j == NSH - 1, i == ni - 1))
        def _():
            cp = pltpu.make_async_copy(dx_acc, dx_hbm, sems.at[0])
            cp.start()
            cp.wait()

    row = pl.BlockSpec((tm, D), lambda j, i: (i, 0))
    wsp = pl.BlockSpec((None, FSH, D), lambda j, i: (j, 0, 0))
    ab = pl.BlockSpec((None, tm, FSH), lambda j, i: (j, i, 0))
    dwshape = jax.ShapeDtypeStruct((2, NSH, hr, D), F32)
    return pl.pallas_call(
        body, name=name, grid=(NSH, ni),
        in_specs=[row, ab, ab, wsp, wsp, wsp, ANY] + [ANY] * len(after),
        out_specs=[ANY, ANY, ANY, ANY],
        out_shape=[dwshape, dwshape, dwshape, jax.ShapeDtypeStruct((S, D), F32)],
        scratch_shapes=[pltpu.VMEM((S, D), F32), pltpu.VMEM((S, FSH), BF16), pltpu.VMEM((S, FSH), BF16),
                        pltpu.VMEM((S, FSH), BF16), pltpu.VMEM((S, D), BF16), pltpu.VMEM((S, D), BF16),
                        pltpu.VMEM((2, FSH, D), F32), pltpu.SemaphoreType.DMA((5,))],
        compiler_params=_cp(("arbitrary", "arbitrary"), vmem_mib=58),
    )(dr, a, b, wgt, wut, wd, xin_b, *after)


def _matmul(a, b, mode, name, *, n, tm, tn, b_col0=0, after=()):
    m, k = a.shape
    assert m % tm == 0 and n % tn == 0 and b_col0 % tn == 0
    off = b_col0 // tn
    na = len(after)

    def body(*refs):
        a_ref, b_ref, o_ref = refs[na:]
        av = a_ref[...].astype(BF16)
        o_ref[...] = _dot(av, b_ref[...]) if mode == "nn" else _dot_nt(av, b_ref[...])

    if mode == "nn":
        b_spec = pl.BlockSpec((k, tn), lambda i, j: (0, j + off))
    else:
        b_spec = pl.BlockSpec((tn, k), lambda i, j: (j, 0))
    return pl.pallas_call(
        body, name=name, grid=(m // tm, n // tn),
        in_specs=[pl.BlockSpec(memory_space=pl.ANY)] * na + [pl.BlockSpec((tm, k), lambda i, j: (i, 0)), b_spec],
        out_specs=pl.BlockSpec((tm, tn), lambda i, j: (i, j)),
        out_shape=jax.ShapeDtypeStruct((m, n), F32),
        compiler_params=_cp(("parallel", "parallel")),
    )(*after, a, b)


def _wgrad(xt, y, rh, c, name, row_sharded, after=()):
    na = len(after)
    if row_sharded:
        def body(x_ref, y_ref, *rest):
            o_ref = rest[na]
            res = _dot(x_ref[...], y_ref[...].astype(BF16))
            for j in range(NSH):
                for h in range(2):
                    o_ref[h, j] = res[(2 * j + h) * rh:(2 * j + h + 1) * rh, :]

        grid = (1,)
        in_specs = [pl.BlockSpec((2 * NSH * rh, S), lambda g: (0, 0)), pl.BlockSpec((S, c), lambda g: (0, 0))]
        out_specs = pl.BlockSpec((2, NSH, rh, c), lambda g: (0, 0, 0, 0))
        sem = ("arbitrary",)
    else:
        def body(x_ref, y_ref, *rest):
            rest[na][...] = _dot(x_ref[...], y_ref[...].astype(BF16))

        grid = (2, NSH)
        in_specs = [pl.BlockSpec((rh, S), lambda h, j: (h, 0)), pl.BlockSpec((S, c), lambda h, j: (0, j))]
        out_specs = pl.BlockSpec((None, None, rh, c), lambda h, j: (h, j, 0, 0))
        sem = ("parallel", "parallel")
    return pl.pallas_call(
        body, name=name, grid=grid, in_specs=in_specs + [pl.BlockSpec(memory_space=pl.ANY)] * na, out_specs=out_specs,
        out_shape=jax.ShapeDtypeStruct((2, NSH, rh, c), F32),
        compiler_params=_cp(sem, vmem_mib=56),
    )(xt, y, *after)


def _resid_ln(res_xh, res_g, res_b, a, w, ln_g, ln_b, name):
    tm = 512

    def body(rx_ref, rg_ref, rb_ref, a_ref, w_ref, g_ref, b_ref, h_ref, hb_ref, xh_ref, rstd_ref):
        r = ALPHA * (rx_ref[...] * rg_ref[...] + rb_ref[...]) + _dot(a_ref[...], w_ref[...])
        xh, rstd = _ln_stats(r)
        h = xh * g_ref[...] + b_ref[...]
        h_ref[...] = h
        hb_ref[...] = h.astype(BF16)
        xh_ref[...] = xh
        rstd_ref[...] = rstd

    row = pl.BlockSpec((tm, D), lambda i: (i, 0))
    vec = pl.BlockSpec((1, D), lambda i: (0, 0))
    return pl.pallas_call(
        body, name=name, grid=(S // tm,),
        in_specs=[row, vec, vec, row, pl.BlockSpec((D, D), lambda i: (0, 0)), vec, vec],
        out_specs=[row, row, row, pl.BlockSpec((tm, 1), lambda i: (i, 0))],
        out_shape=[jax.ShapeDtypeStruct((S, D), F32), jax.ShapeDtypeStruct((S, D), BF16),
                   jax.ShapeDtypeStruct((S, D), F32), jax.ShapeDtypeStruct((S, 1), F32)],
        compiler_params=_cp(("parallel",)),
    )(res_xh, res_g, res_b, a, w, ln_g, ln_b)


def _dh1_ln_bwd(dproj, w_in, dr2, xh, rstd, ln_g, name, after=()):
    tm, tk, ch = 1024, IN_SH, EPI_ROWS
    nk = IN_W // tk
    na = len(after)

    def body(*refs):
        a_ref, b_ref, add_ref, xh_ref, rstd_ref, g_ref, dr_ref, dg_ref, db_ref, acc_ref = refs[na:]
        i = pl.program_id(0)
        k = pl.program_id(1)
        p = _dot_nt(a_ref[...], b_ref[...])

        @pl.when(k == 0)
        def _():
            acc_ref[...] = p

        @pl.when(k > 0)
        def _():
            acc_ref[...] += p

        @pl.when(jnp.logical_and(k == nk - 1, i == 0))
        def _():
            dg_ref[...] = jnp.zeros_like(dg_ref)
            db_ref[...] = jnp.zeros_like(db_ref)

        @pl.when(k == nk - 1)
        def _():
            for c0 in range(0, tm, ch):
                rows = slice(c0, c0 + ch)
                dy = acc_ref[rows, :] + ALPHA * add_ref[rows, :]
                xhv = xh_ref[rows, :]
                dr_ref[rows, :] = _ln_dx(dy * g_ref[...], xhv, rstd_ref[rows, :])
                dg_ref[...] += jnp.sum(dy * xhv, axis=0, keepdims=True)
                db_ref[...] += jnp.sum(dy, axis=0, keepdims=True)

    row = pl.BlockSpec((tm, D), lambda i, k: (i, 0))
    vec = pl.BlockSpec((1, D), lambda i, k: (0, 0))
    return pl.pallas_call(
        body, name=name, grid=(S // tm, nk),
        in_specs=[pl.BlockSpec(memory_space=pl.ANY)] * na
        + [pl.BlockSpec((tm, tk), lambda i, k: (i, k)), pl.BlockSpec((D, tk), lambda i, k: (0, k)), row, row,
           pl.BlockSpec((tm, 1), lambda i, k: (i, 0)), vec],
        out_specs=[row, vec, vec],
        out_shape=[jax.ShapeDtypeStruct((S, D), F32), jax.ShapeDtypeStruct((1, D), F32),
                   jax.ShapeDtypeStruct((1, D), F32)],
        scratch_shapes=[pltpu.VMEM((tm, D), F32)],
        compiler_params=_cp(("arbitrary", "arbitrary"), vmem_mib=56),
    )(*after, dproj, w_in, dr2, xh, rstd, ln_g)


def _ln_bwd(dout, xh, rstd, ln_g, name, after=()):
    tm = 512
    na = len(after)

    def body(*refs):
        y_ref, xh_ref, rstd_ref, g_ref, dr_ref, dg_ref, db_ref = refs[na:]
        dy = y_ref[...]
        i = pl.program_id(0)
        xh = xh_ref[...]
        dr_ref[...] = _ln_dx(dy * g_ref[...], xh, rstd_ref[...])
        dg = jnp.sum(dy * xh, axis=0, keepdims=True)
        db = jnp.sum(dy, axis=0, keepdims=True)

        @pl.when(i == 0)
        def _():
            dg_ref[...] = dg
            db_ref[...] = db

        @pl.when(i > 0)
        def _():
            dg_ref[...] += dg
            db_ref[...] += db

    row = pl.BlockSpec((tm, D), lambda i: (i, 0))
    vec = pl.BlockSpec((1, D), lambda i: (0, 0))
    return pl.pallas_call(
        body, name=name, grid=(S // tm,),
        in_specs=[pl.BlockSpec(memory_space=pl.ANY)] * na + [row, row, pl.BlockSpec((tm, 1), lambda i: (i, 0)), vec],
        out_specs=[row, vec, vec],
        out_shape=[jax.ShapeDtypeStruct((S, D), F32), jax.ShapeDtypeStruct((1, D), F32),
                   jax.ShapeDtypeStruct((1, D), F32)],
        compiler_params=_cp(("arbitrary",)),
    )(*after, dout, xh, rstd, ln_g)


ROPE_TM = 256


def _rope_tables(pos_ref, invf_ref, sign):
    ang = pos_ref[...] * invf_ref[...]
    lane = lax.broadcasted_iota(jnp.int32, ang.shape, 1)
    first = (lane % DH) < (DH // 2)
    sinv = jnp.sin(ang) * sign
    return first, jnp.cos(ang), jnp.where(first, -sinv, sinv)


def _rotate(x, first, cosf, sinf):
    return x * cosf + jnp.where(first, pltpu.roll(x, 96, 1), pltpu.roll(x, 32, 1)) * sinf


def _proj_qkv_rope(hb, w_in, pos_f, invf, name):
    tm = 2 * ROPE_TM

    def body(h_ref, w_ref, pos_ref, invf_ref, o0_ref, o1_ref, o2_ref, buf_ref):
        rot = pl.program_id(1) < 2
        first, cosf, sinf = _rope_tables(pos_ref, invf_ref, 1.0)
        cosf = jnp.where(rot, cosf, 1.0)
        sinf = jnp.where(rot, sinf, 0.0)
        acc = _dot(h_ref[...], w_ref[...])
        for gi, (d, o_ref) in enumerate(zip(DILATIONS, (o0_ref, o1_ref, o2_ref))):
            for ch in range(GRP_W // 128):
                cols = slice(ch * 128, (ch + 1) * 128)
                x = _rotate(acc[:, gi * GRP_W + ch * 128: gi * GRP_W + (ch + 1) * 128], first, cosf, sinf)
                if d == 1:
                    o_ref[0, :, cols] = x.astype(BF16)
                else:
                    buf_ref[...] = x
                    for r in range(d):
                        o_ref[r, :, cols] = buf_ref[pl.ds(r, tm // d, stride=d), :].astype(BF16)

    return pl.pallas_call(
        body, name=name, grid=(S // tm, 3),
        in_specs=[pl.BlockSpec((tm, D), lambda i, s: (i, 0)), pl.BlockSpec((D, ATT_W), lambda i, s: (0, s)),
                  pl.BlockSpec((tm, 1), lambda i, s: (i, 0)), pl.BlockSpec((1, 128), lambda i, s: (0, 0))],
        out_specs=[pl.BlockSpec((d, tm // d, GRP_W), lambda i, s: (0, i, s)) for d in DILATIONS],
        out_shape=[jax.ShapeDtypeStruct((d, S // d, 3 * GRP_W), BF16) for d in DILATIONS],
        scratch_shapes=[pltpu.VMEM((tm, 128), F32)],
        compiler_params=_cp(("parallel", "parallel")),
    )(hb, w_in, pos_f, invf)


def _rope_bwd(dqkv_c, pos_f, invf, name):
    tm = ROPE_TM

    def body(*refs):
        g_refs, (pos_ref, invf_ref, o_ref, buf_ref) = refs[:9], refs[9:]
        first, cosf, sinf = _rope_tables(pos_ref, invf_ref, -1.0)
        for sec in range(3):
            for gi, d in enumerate(DILATIONS):
                g_ref = g_refs[3 * gi + sec]
                for ch in range(GRP_W // 128):
                    cols = slice(ch * 128, (ch + 1) * 128)
                    if d == 1:
                        x = g_ref[0, :, cols]
                    else:
                        for r in range(d):
                            buf_ref[pl.ds(r, tm // d, stride=d), :] = g_ref[r, :, cols]
                        x = buf_ref[...]
                    if sec < 2:
                        x = _rotate(x, first, cosf, sinf)
                    dst = sec * ATT_W + gi * GRP_W + ch * 128
                    o_ref[:, dst:dst + 128] = x.astype(BF16)

    g_specs = [pl.BlockSpec((d, tm // d, GRP_W), lambda i: (0, i, 0)) for d in DILATIONS for _ in range(3)]
    return pl.pallas_call(
        body, name=name, grid=(S // tm,),
        in_specs=g_specs + [pl.BlockSpec((tm, 1), lambda i: (i, 0)), pl.BlockSpec((1, 128), lambda i: (0, 0))],
        out_specs=pl.BlockSpec((tm, 3 * ATT_W), lambda i: (i, 0)),
        out_shape=jax.ShapeDtypeStruct((S, 3 * ATT_W), BF16),
        scratch_shapes=[pltpu.VMEM((tm, 128), F32)],
        compiler_params=_cp(("parallel",)),
    )(*[g for grp in dqkv_c for g in grp], pos_f, invf)


def _class_order(ts, name):
    tm = ROPE_TM
    n = len(ts)

    def body(*refs):
        buf_ref = refs[3 * n]
        for a in range(n):
            for ch in range(GRP_W // 128):
                cols = slice(ch * 128, (ch + 1) * 128)
                buf_ref[...] = refs[a][:, cols]
                for b, d in enumerate(DILATIONS[1:]):
                    for r in range(d):
                        refs[n + 2 * a + b][r, :, cols] = buf_ref[pl.ds(r, tm // d, stride=d), :]

    return pl.pallas_call(
        body, name=name, grid=(S // tm,),
        in_specs=[pl.BlockSpec((tm, GRP_W), lambda i: (i, 0))] * n,
        out_specs=[pl.BlockSpec((d, tm // d, GRP_W), lambda i: (0, i, 0)) for _ in range(n) for d in DILATIONS[1:]],
        out_shape=[jax.ShapeDtypeStruct((d, S // d, GRP_W), F32) for _ in range(n) for d in DILATIONS[1:]],
        scratch_shapes=[pltpu.VMEM((tm, 128), F32)],
        compiler_params=_cp(("parallel",)),
    )(*ts)


def _own_lanes(h):
    return (lax.broadcasted_iota(jnp.int32, (1, 2 * DH), 1) // DH) == (h % 2)


def _heads(ref):
    out = []
    for h in range(NH):
        pair = ref[:, (h // 2) * 2 * DH:(h // 2 + 1) * 2 * DH]
        out.append(jnp.where(_own_lanes(h), pair, jnp.zeros_like(pair)))
    return jnp.stack(out)


def _unheads(t3):
    return jnp.concatenate([t3[2 * p] + t3[2 * p + 1] for p in range(NH // 2)], axis=1)


def _bdot_nt(a, b):
    return lax.dot_general(a, b, (((2,), (2,)), ((0,), (0,))), preferred_element_type=F32)


def _bdot(a, b):
    return lax.dot_general(a, b, (((2,), (1,)), ((0,), (0,))), preferred_element_type=F32)


def _bdot_tn(a, b):
    return lax.dot_general(a, b, (((1,), (1,)), ((0,), (0,))), preferred_element_type=F32)


def _attn_fwd(gi, qkv_c, name):
    d = DILATIONS[gi]
    nblk = S // d // BLK

    def body(*refs):
        if nblk > 1:
            q_ref, kc_ref, kp_ref, vc_ref, vp_ref, o_ref, lse_ref = refs
            has_prev = pl.program_id(1) != 0
        else:
            q_ref, kc_ref, vc_ref, o_ref, lse_ref = refs
        qi = lax.broadcasted_iota(jnp.int32, (NH, BLK, BLK), 1)
        kj = lax.broadcasted_iota(jnp.int32, (NH, BLK, BLK), 2)
        q = _heads(q_ref)
        sc = jnp.where(kj <= qi, _bdot_nt(q, _heads(kc_ref)) * 0.125, NEG_INF)
        m = jnp.max(sc, axis=-1, keepdims=True)
        if nblk > 1:
            mask_p = jnp.logical_and(kj >= qi, has_prev)
            sp = jnp.where(mask_p, _bdot_nt(q, _heads(kp_ref)) * 0.125, NEG_INF)
            m = jnp.maximum(m, jnp.max(sp, axis=-1, keepdims=True))
        pc = jnp.exp(sc - m)
        l = jnp.sum(pc, axis=-1, keepdims=True)
        o = _bdot(pc.astype(BF16), _heads(vc_ref))
        if nblk > 1:
            pp = jnp.exp(sp - m)
            l = l + jnp.sum(pp, axis=-1, keepdims=True)
            o = o + _bdot(pp.astype(BF16), _heads(vp_ref))
        o_ref[...] = _unheads(o / l)
        lse = jnp.broadcast_to(m + jnp.log(l), (NH, BLK, 2 * DH))
        lse_ref[...] = _unheads(jnp.stack([jnp.where(_own_lanes(h), lse[h], 0.0) for h in range(NH)]))

    def cur(sec):
        return pl.BlockSpec((None, BLK, GRP_W), lambda r, n: (r, n, sec))

    def prev(sec):
        return pl.BlockSpec((None, BLK, GRP_W), lambda r, n: (r, jnp.maximum(n - 1, 0), sec))

    out = pl.BlockSpec((None, BLK, GRP_W), lambda r, n: (r, n, 0))
    shp = jax.ShapeDtypeStruct((d, S // d, GRP_W), F32)
    if nblk > 1:
        in_specs, args = [cur(0), cur(1), prev(1), cur(2), prev(2)], (qkv_c,) * 5
    else:
        in_specs, args = [cur(0), cur(1), cur(2)], (qkv_c,) * 3
    return pl.pallas_call(
        body, name=name, grid=(d, nblk), in_specs=in_specs, out_specs=[out, out], out_shape=[shp, shp],
        compiler_params=_cp(("parallel", "parallel")),
    )(*args)


def _attn_combine(os, lses, name):
    tm = ROPE_TM

    def body(o0_ref, o1_ref, o2_ref, l0_ref, l1_ref, l2_ref, y_ref, yt_ref, l_ref, buf_ref):
        def token_order(ref, d, cols, slot):
            if d == 1:
                return ref[0, :, cols]
            for r in range(d):
                buf_ref[slot, pl.ds(r, tm // d, stride=d), :] = ref[r, :, cols]
            return buf_ref[slot]

        for ch in range(GRP_W // 128):
            cols = slice(ch * 128, (ch + 1) * 128)
            o = [token_order(ref, d, cols, k) for k, (ref, d) in enumerate(zip((o0_ref, o1_ref, o2_ref), DILATIONS))]
            ls = [token_order(ref, d, cols, 3 + k)
                  for k, (ref, d) in enumerate(zip((l0_ref, l1_ref, l2_ref), DILATIONS))]
            m = jnp.maximum(jnp.maximum(ls[0], ls[1]), ls[2])
            e = [jnp.exp(l - m) for l in ls]
            den = e[0] + e[1] + e[2]
            y = (e[0] * o[0] + e[1] * o[1] + e[2] * o[2]) / den
            y_ref[:, cols] = y
            yt_ref[cols, :] = y.T.astype(BF16)
            l_ref[:, cols] = m + jnp.log(den)

    blk = pl.BlockSpec((tm, GRP_W), lambda i: (i, 0))
    cls = [pl.BlockSpec((d, tm // d, GRP_W), lambda i: (0, i, 0)) for d in DILATIONS]
    shp = jax.ShapeDtypeStruct((S, GRP_W), F32)
    return pl.pallas_call(
        body, name=name, grid=(S // tm,), in_specs=cls + cls,
        out_specs=[blk, pl.BlockSpec((GRP_W, tm), lambda i: (0, i)), blk],
        out_shape=[shp, jax.ShapeDtypeStruct((GRP_W, S), BF16), shp],
        scratch_shapes=[pltpu.VMEM((6, tm, 128), F32)],
        compiler_params=_cp(("parallel",)),
    )(*os, *lses)


def _attn_bwd(gi, qkv_c, dy_c, y_c, lse_c, name):
    d = DILATIONS[gi]
    nblk = S // d // BLK

    def body(*refs):
        if nblk > 1:
            (q_ref, qn_ref, k_ref, kp_ref, v_ref, vp_ref, dy_ref, dyn_ref, y_ref, yn_ref, l_ref, ln_ref,
             dq_ref, dk_ref, dv_ref) = refs
            n = pl.program_id(1)
            has_prev = n != 0
            has_next = n != nblk - 1
        else:
            q_ref, k_ref, v_ref, dy_ref, y_ref, l_ref, dq_ref, dk_ref, dv_ref = refs
        qi = lax.broadcasted_iota(jnp.int32, (NH, BLK, BLK), 1)
        kj = lax.broadcasted_iota(jnp.int32, (NH, BLK, BLK), 2)

        def lse_col(ref):
            return jnp.stack([ref[:, h * DH:h * DH + 1] for h in range(NH)])

        q, k, v = _heads(q_ref), _heads(k_ref), _heads(v_ref)
        dy = _heads(dy_ref)
        dd = jnp.sum(dy * _heads(y_ref), axis=-1, keepdims=True)
        lcol = lse_col(l_ref)
        dyb = dy.astype(BF16)
        p = jnp.exp(jnp.where(kj <= qi, _bdot_nt(q, k) * 0.125, NEG_INF) - lcol)
        ds = (p * (_bdot_nt(dyb, v) - dd)).astype(BF16)
        dq = _bdot(ds, k)
        dk = _bdot_tn(ds, q)
        dv = _bdot_tn(p.astype(BF16), dyb)
        if nblk > 1:
            qn, kpv, vpv = _heads(qn_ref), _heads(kp_ref), _heads(vp_ref)
            dyn = _heads(dyn_ref)
            ddn = jnp.sum(dyn * _heads(yn_ref), axis=-1, keepdims=True)
            lncol = lse_col(ln_ref)
            dynb = dyn.astype(BF16)
            mask_p = jnp.logical_and(kj >= qi, has_prev)
            pp = jnp.exp(jnp.where(mask_p, _bdot_nt(q, kpv) * 0.125, NEG_INF) - lcol)
            dsp = (pp * (_bdot_nt(dyb, vpv) - dd)).astype(BF16)
            dq = dq + _bdot(dsp, kpv)
            mask_n = jnp.logical_and(kj >= qi, has_next)
            pn = jnp.exp(jnp.where(mask_n, _bdot_nt(qn, k) * 0.125, NEG_INF) - lncol)
            dsn = (pn * (_bdot_nt(dynb, v) - ddn)).astype(BF16)
            dk = dk + _bdot_tn(dsn, qn)
            dv = dv + _bdot_tn(pn.astype(BF16), dynb)
        dq_ref[...] = _unheads(dq) * 0.125
        dk_ref[...] = _unheads(dk) * 0.125
        dv_ref[...] = _unheads(dv)

    def spec(sec, shift):
        def idx(r, n):
            return (r, jnp.clip(n + shift, 0, nblk - 1), sec)
        return pl.BlockSpec((None, BLK, GRP_W), idx)

    if nblk > 1:
        in_specs = [spec(0, 0), spec(0, 1), spec(1, 0), spec(1, -1), spec(2, 0), spec(2, -1),
                    spec(0, 0), spec(0, 1), spec(0, 0), spec(0, 1), spec(0, 0), spec(0, 1)]
        args = (qkv_c,) * 6 + (dy_c, dy_c, y_c, y_c, lse_c, lse_c)
    else:
        in_specs = [spec(0, 0), spec(1, 0), spec(2, 0), spec(0, 0), spec(0, 0), spec(0, 0)]
        args = (qkv_c, qkv_c, qkv_c, dy_c, y_c, lse_c)
    out = spec(0, 0)
    shp = jax.ShapeDtypeStruct((d, S // d, GRP_W), F32)
    return pl.pallas_call(
        body, name=name, grid=(d, nblk), in_specs=in_specs, out_specs=[out, out, out], out_shape=[shp, shp, shp],
        compiler_params=_cp(("parallel", "parallel")),
    )(*args)


_SQRT_HALF = 0.7071067811865476
_INV_SQRT_2PI = 0.3989422804014327


def _gelu(z):
    return 0.5 * z * (1.0 + lax.erf(z * _SQRT_HALF))


def _gelu_grad(z):
    return 0.5 * (1.0 + lax.erf(z * _SQRT_HALF)) + z * (jnp.exp(-0.5 * z * z) * _INV_SQRT_2PI)


def _tril_mask():
    t = lax.broadcasted_iota(jnp.int32, (BLK, BLK), 0)
    s = lax.broadcasted_iota(jnp.int32, (BLK, BLK), 1)
    return s <= t


def _groups(t):
    return jnp.stack([t[:, g * BLK:(g + 1) * BLK] for g in range(8)])


def _ungroup(t3):
    return jnp.concatenate([t3[g] for g in range(8)], axis=1)


def _group_bias(bs_ref):
    return jnp.stack([bs_ref[:, g:g + 1] for g in range(8)])


def _gmlp_fwd(z, ln_g, ln_b, w_s, b_s_t, name):
    def body(z_ref, g_ref, b_ref, ws_ref, bs_ref, y_ref, yt_ref):
        zg = _gelu(z_ref[...])
        u = zg[:, :GW]
        xh, _ = _ln_stats(zg[:, GW:])
        vn = (xh * g_ref[...] + b_ref[...]).astype(BF16)
        wt = jnp.where(_tril_mask(), ws_ref[...], 0.0).astype(BF16)
        yv = u * _ungroup(_bdot(wt, _groups(vn)) + _group_bias(bs_ref))
        y_ref[...] = yv.astype(BF16)
        yt_ref[...] = yv.T.astype(BF16)

    vec = pl.BlockSpec((1, GW), lambda n: (0, 0))
    return pl.pallas_call(
        body, name=name, grid=(NBLK,),
        in_specs=[pl.BlockSpec((BLK, 2 * GW), lambda n: (n, 0)), vec, vec,
                  pl.BlockSpec((8, BLK, BLK), lambda n: (0, 0, 0)), pl.BlockSpec((BLK, 8), lambda n: (0, 0))],
        out_specs=[pl.BlockSpec((BLK, GW), lambda n: (n, 0)), pl.BlockSpec((GW, BLK), lambda n: (0, n))],
        out_shape=[jax.ShapeDtypeStruct((S, GW), BF16), jax.ShapeDtypeStruct((GW, S), BF16)],
        compiler_params=_cp(("parallel",)),
    )(z, ln_g, ln_b, w_s, b_s_t)


def _gmlp_bwd(z, dy, ln_g, ln_b, w_s, b_s_t, name):
    def body(z_ref, dy_ref, g_ref, b_ref, ws_ref, bs_ref, dz_ref, dws_ref, dbs_ref, dg_ref, db_ref, dvn_ref):
        n = pl.program_id(0)
        zv = z_ref[...]
        zg = _gelu(zv)
        u = zg[:, :GW]
        xh, rstd = _ln_stats(zg[:, GW:])
        vn = (xh * g_ref[...] + b_ref[...]).astype(BF16)
        tril = _tril_mask()

        @pl.when(n == 0)
        def _():
            dws_ref[...] = jnp.zeros_like(dws_ref)
            dbs_ref[...] = jnp.zeros_like(dbs_ref)
            dg_ref[...] = jnp.zeros_like(dg_ref)
            db_ref[...] = jnp.zeros_like(db_ref)

        wt = jnp.where(tril, ws_ref[...], 0.0).astype(BF16)
        vn3 = _groups(vn)
        dyv = dy_ref[...]
        mixed = _ungroup(_bdot(wt, vn3) + _group_bias(bs_ref))
        dz_ref[:, :GW] = (dyv * mixed * _gelu_grad(zv[:, :GW])).astype(BF16)
        dmix3 = _groups(dyv * u)
        dmb = dmix3.astype(BF16)
        dws_ref[...] += jnp.where(tril, _bdot_nt(dmb, vn3), 0.0)
        dbsum = jnp.sum(dmix3, axis=-1, keepdims=True)
        for gg in range(8):
            dbs_ref[:, gg:gg + 1] += dbsum[gg]
        dvn_ref[...] = _ungroup(_bdot_tn(wt, dmb))

        dvn = dvn_ref[...]
        dg_ref[...] += jnp.sum(dvn * xh, axis=0, keepdims=True)
        db_ref[...] += jnp.sum(dvn, axis=0, keepdims=True)
        dvg = _ln_dx(dvn * g_ref[...], xh, rstd)
        dz_ref[:, GW:] = (dvg * _gelu_grad(zv[:, GW:])).astype(BF16)

    vec = pl.BlockSpec((1, GW), lambda n: (0, 0))
    ws = pl.BlockSpec((8, BLK, BLK), lambda n: (0, 0, 0))
    bs = pl.BlockSpec((BLK, 8), lambda n: (0, 0))
    return pl.pallas_call(
        body, name=name, grid=(NBLK,),
        in_specs=[pl.BlockSpec((BLK, 2 * GW), lambda n: (n, 0)), pl.BlockSpec((BLK, GW), lambda n: (n, 0)),
                  vec, vec, ws, bs],
        out_specs=[pl.BlockSpec((BLK, 2 * GW), lambda n: (n, 0)), ws, bs, vec, vec],
        out_shape=[jax.ShapeDtypeStruct((S, 2 * GW), BF16), jax.ShapeDtypeStruct((8, BLK, BLK), F32),
                   jax.ShapeDtypeStruct((BLK, 8), F32), jax.ShapeDtypeStruct((1, GW), F32),
                   jax.ShapeDtypeStruct((1, GW), F32)],
        scratch_shapes=[pltpu.VMEM((BLK, GW), F32)],
        compiler_params=_cp(("arbitrary",)),
    )(z, dy, ln_g, ln_b, w_s, b_s_t)


def _merge_fwd(a, b, gl, b_gates, name):
    tm = 512

    def body(a_ref, b_ref, g0_ref, g1_ref, bg_ref, o_ref, ot_ref):
        g0 = jax.nn.sigmoid(g0_ref[...] + bg_ref[:, :D])
        g1 = jax.nn.sigmoid(g1_ref[...] + bg_ref[:, D:])
        mg = g0 * a_ref[...] + g1 * b_ref[...]
        o_ref[...] = mg.astype(BF16)
        ot_ref[...] = mg.T.astype(BF16)

    row = pl.BlockSpec((tm, D), lambda i: (i, 0))
    return pl.pallas_call(
        body, name=name, grid=(S // tm,),
        in_specs=[row, row, row, pl.BlockSpec((tm, D), lambda i: (i, 1)), pl.BlockSpec((1, 2 * D), lambda i: (0, 0))],
        out_specs=[row, pl.BlockSpec((D, tm), lambda i: (0, i))],
        out_shape=[jax.ShapeDtypeStruct((S, D), BF16), jax.ShapeDtypeStruct((D, S), BF16)],
        compiler_params=_cp(("parallel",)),
    )(a, b, gl, gl, b_gates)


def _merge_bwd(dm, a, b, gl, b_gates, name):
    tm = 512

    def body(dm_ref, a_ref, b_ref, g0_ref, g1_ref, bg_ref, da_ref, db_ref, dgl_ref, dbg_ref):
        i = pl.program_id(0)
        dmv = dm_ref[...]
        g0 = jax.nn.sigmoid(g0_ref[...] + bg_ref[:, :D])
        g1 = jax.nn.sigmoid(g1_ref[...] + bg_ref[:, D:])
        da_ref[...] = (dmv * g0).astype(BF16)
        db_ref[...] = (dmv * g1).astype(BF16)
        d0 = dmv * a_ref[...] * g0 * (1.0 - g0)
        d1 = dmv * b_ref[...] * g1 * (1.0 - g1)
        dgl_ref[:, :D] = d0.astype(BF16)
        dgl_ref[:, D:] = d1.astype(BF16)
        s0 = jnp.sum(d0, axis=0, keepdims=True)
        s1 = jnp.sum(d1, axis=0, keepdims=True)

        @pl.when(i == 0)
        def _():
            dbg_ref[:, :D] = s0
            dbg_ref[:, D:] = s1

        @pl.when(i > 0)
        def _():
            dbg_ref[:, :D] += s0
            dbg_ref[:, D:] += s1

    row = pl.BlockSpec((tm, D), lambda i: (i, 0))
    wide = pl.BlockSpec((tm, 2 * D), lambda i: (i, 0))
    bg = pl.BlockSpec((1, 2 * D), lambda i: (0, 0))
    return pl.pallas_call(
        body, name=name, grid=(S // tm,),
        in_specs=[row, row, row, row, pl.BlockSpec((tm, D), lambda i: (i, 1)), bg],
        out_specs=[row, row, wide, bg],
        out_shape=[jax.ShapeDtypeStruct((S, D), BF16), jax.ShapeDtypeStruct((S, D), BF16),
                   jax.ShapeDtypeStruct((S, 2 * D), BF16), jax.ShapeDtypeStruct((1, 2 * D), F32)],
        compiler_params=_cp(("arbitrary",)),
    )(dm, a, b, gl, gl, b_gates)


def _adam_math(w, g, m, v):
    m2 = ADAM_B1 * m + (1.0 - ADAM_B1) * g
    v2 = ADAM_B2 * v + (1.0 - ADAM_B2) * (g * g)
    m_hat = m2 / (1.0 - ADAM_B1 ** ADAM_STEP)
    v_hat = v2 / (1.0 - ADAM_B2 ** ADAM_STEP)
    delta = -ADAM_LR * (m_hat / (jnp.sqrt(v_hat) + ADAM_EPS) + ADAM_WD * w)
    return delta, m2, v2


def _pick_rows(rows, cols, unit=16, budget=2 * MIB):
    best = unit
    for t in range(unit, rows + 1, unit):
        if rows % t == 0 and t * cols * 4 <= budget:
            best = t
    assert rows % best == 0
    return best


def _adamw(w, g, m, v, name):
    r, c = w.shape
    tr = _pick_rows(r, c, unit=8)

    def body(w_ref, g_ref, m_ref, v_ref, go_ref, d_ref, mo_ref, vo_ref):
        gv = g_ref[...]
        delta, m2, v2 = _adam_math(w_ref[...], gv, m_ref[...], v_ref[...])
        go_ref[...] = gv
        d_ref[...] = delta
        mo_ref[...] = m2
        vo_ref[...] = v2

    blk = pl.BlockSpec((tr, c), lambda i: (i, 0))
    shp = jax.ShapeDtypeStruct((r, c), F32)
    return pl.pallas_call(
        body, name=name, grid=(r // tr,), in_specs=[blk] * 4, out_specs=[blk] * 4, out_shape=[shp] * 4,
        compiler_params=_cp(("parallel",)),
    )(*[pltpu.with_memory_space_constraint(t, pltpu.HBM) for t in (w, g, m, v)])


def _small_sum_adamw(parts, own, pos, w, m, v, name):
    tr = 48

    def body(pos_ref, p_ref, own_ref, w_ref, m_ref, v_ref, g_ref, d_ref, mo_ref, vo_ref):
        me = 2 * pos_ref[1] + pos_ref[0]
        gv = None
        for k in range(8):
            term = jnp.where(me == k, own_ref[...], p_ref[k])
            gv = term if gv is None else gv + term
        delta, m2, v2 = _adam_math(w_ref[...], gv, m_ref[...], v_ref[...])
        g_ref[...] = gv
        d_ref[...] = delta
        mo_ref[...] = m2
        vo_ref[...] = v2

    blk = pl.BlockSpec((tr, D), lambda i, p: (i, 0))
    shp = jax.ShapeDtypeStruct((SMALL_ROWS, D), F32)
    return pl.pallas_call(
        body, name=name,
        grid_spec=pltpu.PrefetchScalarGridSpec(
            num_scalar_prefetch=1, grid=(SMALL_ROWS // tr,),
            in_specs=[pl.BlockSpec((8, tr, D), lambda i, p: (0, i, 0)), blk, blk, blk, blk],
            out_specs=[blk] * 4),
        out_shape=[shp] * 4,
        compiler_params=_cp(("parallel",)),
    )(pos, parts, own, w, m, v)


ANY = pl.BlockSpec(memory_space=pl.ANY)


def _in_hbm(arrays):
    return [pltpu.with_memory_space_constraint(a, pltpu.HBM) for a in arrays]


def _mesh_pos():
    x, y, c = lax.axis_index("x"), lax.axis_index("y"), lax.axis_index("c")
    chips = [(1 - x, y), (x, 1 - y), (1 - x, 1 - y)]
    return x, y, c, chips


def _place_shard(w, kind, pos, name):
    r, c = w.shape
    tr = _pick_rows(r, c)

    def body(pos_ref, w_ref, o_ref):
        o_ref[...] = w_ref[...].astype(BF16)

    if kind == "stack":
        o_spec = pl.BlockSpec((None, tr, c), lambda i, p: (p[1], i, 0))
        shape = (NSH, r, c)
    else:
        o_spec = pl.BlockSpec((tr, c), lambda i, p: (i, p[1]))
        shape = (r, NSH * c)
    return pl.pallas_call(
        body, name=name,
        grid_spec=pltpu.PrefetchScalarGridSpec(
            num_scalar_prefetch=1, grid=(r // tr,),
            in_specs=[pl.BlockSpec((tr, c), lambda i, p: (i, 0))], out_specs=o_spec),
        out_shape=pltpu.HBM(shape, BF16),
        compiler_params=_cp(("parallel",)),
    )(pos, pltpu.with_memory_space_constraint(w, pltpu.HBM))


SEM = pl.BlockSpec(memory_space=pltpu.SEMAPHORE)
SPLIT_COPY = pltpu.CompilerParams(has_side_effects=pltpu.SideEffectType.DATAFLOW_SIDE_EFFECTING)


def _shard_window(ref, kind, j, h, dims):
    r, c = dims
    rows = pl.ds(pl.multiple_of(h * (r // 2), 16), r // 2)
    if kind == "stack":
        return ref.at[j, rows, :]
    return ref.at[rows, pl.ds(pl.multiple_of(j * c, 128), c)]


def _ici_copy(ref, kind, dims, j, c, sems, idx, to):
    win = _shard_window(ref, kind, j, c, dims)
    return pltpu.make_async_remote_copy(src_ref=win, dst_ref=win, send_sem=sems[0].at[idx], recv_sem=sems[1].at[idx],
                                        device_id=to, device_id_type=MESH_T)


def _both_copy(ref, kind, dims, a, k, chip, half, tc, sc, sems):
    win = _shard_window(ref, kind, half[0], half[1], dims)
    return pltpu.make_async_remote_copy(src_ref=win, dst_ref=win, send_sem=sems[0].at[6 * a + 2 * k + tc],
                                        recv_sem=sems[1].at[6 * a + 2 * k + sc],
                                        device_id=(chip[0], chip[1], tc), device_id_type=MESH_T)


def _gather_start(fulls, kinds, dims, after, name, both=False):
    n, na = len(fulls), len(after)
    per = 6 if both else 3

    def body(*refs):
        outs = refs[n + na:2 * n + na]
        send_sems, recv_sems, token = refs[2 * n + na:]
        x, y, c, chips = _mesh_pos()
        for a in range(n):
            for k, chip in enumerate(chips):
                if both:
                    for tc in range(2):
                        _both_copy(outs[a], kinds[a], dims[a], a, k, chip, (2 * x + y, c), tc, c,
                                   (send_sems, recv_sems)).start()
                else:
                    _ici_copy(outs[a], kinds[a], dims[a], 2 * x + y, c, (send_sems, recv_sems), 3 * a + k,
                              (chip[0], chip[1], c)).start()
        token[...] = jnp.zeros_like(token)

    res = pl.pallas_call(
        body, name=name, in_specs=[ANY] * (n + na),
        out_specs=[ANY] * n + [SEM, SEM, pl.BlockSpec(memory_space=pltpu.VMEM)],
        out_shape=[pltpu.HBM(f.shape, BF16) for f in fulls]
        + [pltpu.SemaphoreType.DMA((per * n,)), pltpu.SemaphoreType.DMA((per * n,)),
           jax.ShapeDtypeStruct((8, 128), F32)],
        input_output_aliases={i: i for i in range(n)},
        compiler_params=SPLIT_COPY,
    )(*_in_hbm(fulls), *after)
    return res[:n], res[n], res[n + 1], res[n + 2]


def _gather_wait(fulls, send_sems, recv_sems, kinds, dims, after, name, both=False):
    n, na = len(fulls), len(after)

    def body(*refs):
        ssem, rsem = refs[n], refs[n + 1]
        outs = refs[n + 2 + na:]
        x, y, c, chips = _mesh_pos()
        for a in range(n):
            for k, chip in enumerate(chips):
                if both:
                    for oc in range(2):
                        _both_copy(outs[a], kinds[a], dims[a], a, k, chip, (2 * x + y, c), oc, c,
                                   (ssem, rsem)).wait_send()
                        _both_copy(outs[a], kinds[a], dims[a], a, k, chip, (2 * chip[0] + chip[1], oc), c, oc,
                                   (ssem, rsem)).wait_recv()
                    continue
                to = (chip[0], chip[1], c)
                _ici_copy(outs[a], kinds[a], dims[a], 2 * x + y, c, (ssem, rsem), 3 * a + k, to).wait_send()
                _ici_copy(outs[a], kinds[a], dims[a], 2 * chip[0] + chip[1], c, (ssem, rsem), 3 * a + k, to).wait_recv()

    return pl.pallas_call(
        body, name=name, in_specs=[ANY] * n + [SEM, SEM] + [ANY] * na, out_specs=[ANY] * n,
        out_shape=[pltpu.HBM(f.shape, BF16) for f in fulls],
        input_output_aliases={i: i for i in range(n)},
        compiler_params=SPLIT_COPY,
    )(*_in_hbm(fulls), send_sems, recv_sems, *after)


def _gather_forward(fulls, kinds, dims, name):
    n = len(fulls)

    def body(*refs):
        outs = refs[n:2 * n]
        sems = refs[2 * n:]
        x, y, c, chips = _mesh_pos()
        sib = (x, y, 1 - c)
        cps = []
        for a in range(n):
            for k, chip in enumerate(chips):
                cp = _ici_copy(outs[a], kinds[a], dims[a], 2 * chip[0] + chip[1], c, sems, 3 * a + k, sib)
                cp.start()
                cps.append(cp)
        for a in range(n):
            for k, chip in enumerate(chips):
                _ici_copy(outs[a], kinds[a], dims[a], 2 * chip[0] + chip[1], 1 - c, sems, 3 * a + k, sib).wait_recv()
        for cp in cps:
            cp.wait_send()

    return pl.pallas_call(
        body, name=name, in_specs=[ANY] * n, out_specs=[ANY] * n,
        out_shape=[pltpu.HBM(f.shape, BF16) for f in fulls],
        input_output_aliases={i: i for i in range(n)},
        scratch_shapes=[pltpu.SemaphoreType.DMA((3 * n,)), pltpu.SemaphoreType.DMA((3 * n,))],
    )(*_in_hbm(fulls))


def _pair_copy(src, land, a, x, y, c, sems):
    return pltpu.make_async_remote_copy(
        src_ref=src.at[1 - c], dst_ref=land, send_sem=sems[0].at[a], recv_sem=sems[1].at[a],
        device_id=(x, y, 1 - c), device_id_type=MESH_T)


def _pair_start(grads, lands, name):
    n = len(grads)

    def body(*refs):
        srcs, dsts = refs[2 * n:3 * n], refs[3 * n:4 * n]
        send_sems, recv_sems, token = refs[4 * n:]
        x, y, c, _ = _mesh_pos()
        for a in range(n):
            _pair_copy(srcs[a], dsts[a], a, x, y, c, (send_sems, recv_sems)).start()
        token[...] = jnp.zeros_like(token)

    res = pl.pallas_call(
        body, name=name, in_specs=[ANY] * (2 * n),
        out_specs=[ANY] * (2 * n) + [SEM, SEM, pl.BlockSpec(memory_space=pltpu.VMEM)],
        out_shape=[pltpu.HBM(g.shape, F32) for g in grads]
        + [pltpu.HBM(l.shape, F32) for l in lands]
        + [pltpu.SemaphoreType.DMA((n,)), pltpu.SemaphoreType.DMA((n,)), jax.ShapeDtypeStruct((8, 128), F32)],
        input_output_aliases={i: i for i in range(2 * n)},
        compiler_params=SPLIT_COPY,
    )(*_in_hbm(grads), *_in_hbm(lands))
    return res[:n], res[n:2 * n], res[2 * n], res[2 * n + 1], res[2 * n + 2]


def _pair_wait(grads, lands, send_sems, recv_sems, after, name):
    n, na = len(grads), len(after)

    def body(*refs):
        ssem, rsem = refs[2 * n], refs[2 * n + 1]
        outs = refs[2 * n + 2 + na:]
        x, y, c, _ = _mesh_pos()
        for a in range(n):
            cp = _pair_copy(outs[a], outs[n + a], a, x, y, c, (ssem, rsem))
            cp.wait_send()
            cp.wait_recv()

    res = pl.pallas_call(
        body, name=name, in_specs=[ANY] * (2 * n) + [SEM, SEM] + [ANY] * na, out_specs=[ANY] * (2 * n),
        out_shape=[pltpu.HBM(g.shape, F32) for g in grads]
        + [pltpu.HBM(l.shape, F32) for l in lands],
        input_output_aliases={i: i for i in range(2 * n)},
        compiler_params=SPLIT_COPY,
    )(*_in_hbm(grads), *_in_hbm(lands), send_sems, recv_sems, *after)
    return res[:n], res[n:]


def _pair_sum(g, recv, pos, name):
    _, _, rh, c = g.shape
    tr = _pick_rows(rh, c)

    def body(pos_ref, g_ref, r_ref, o_ref):
        o_ref[...] = (g_ref[...] + r_ref[...]).astype(BF16)

    return pl.pallas_call(
        body, name=name,
        grid_spec=pltpu.PrefetchScalarGridSpec(
            num_scalar_prefetch=1, grid=(3, rh // tr),
            in_specs=[pl.BlockSpec((None, None, tr, c), lambda k, r, p: (p[0], p[2 + k], r, 0)),
                      pl.BlockSpec((None, tr, c), lambda k, r, p: (p[2 + k], r, 0))],
            out_specs=pl.BlockSpec((None, tr, c), lambda k, r, p: (k, r, 0))),
        out_shape=pltpu.HBM((3, rh, c), BF16),
        compiler_params=_cp(("parallel", "parallel")),
    )(pos, *_in_hbm([g, recv]))


def _chip_copy(src, land, a, k, chip, c, sems):
    return pltpu.make_async_remote_copy(
        src_ref=src.at[k], dst_ref=land.at[k], send_sem=sems[0].at[3 * a + k],
        recv_sem=sems[1].at[3 * a + k], device_id=(chip[0], chip[1], c), device_id_type=MESH_T)


def _chip_start(psums, lands, name):
    n = len(psums)

    def body(*refs):
        srcs, dsts = refs[2 * n:3 * n], refs[3 * n:4 * n]
        send_sems, recv_sems, token = refs[4 * n:]
        x, y, c, chips = _mesh_pos()
        for a in range(n):
            for k, chip in enumerate(chips):
                _chip_copy(srcs[a], dsts[a], a, k, chip, c, (send_sems, recv_sems)).start()
        token[...] = jnp.zeros_like(token)

    res = pl.pallas_call(
        body, name=name, in_specs=[ANY] * (2 * n),
        out_specs=[ANY] * (2 * n) + [SEM, SEM, pl.BlockSpec(memory_space=pltpu.VMEM)],
        out_shape=[pltpu.HBM(p.shape, BF16) for p in psums]
        + [pltpu.HBM(l.shape, BF16) for l in lands]
        + [pltpu.SemaphoreType.DMA((3 * n,)), pltpu.SemaphoreType.DMA((3 * n,)), jax.ShapeDtypeStruct((8, 128), F32)],
        input_output_aliases={i: i for i in range(2 * n)},
        compiler_params=SPLIT_COPY,
    )(*_in_hbm(psums), *_in_hbm(lands))
    return res[:n], res[n:2 * n], res[2 * n], res[2 * n + 1], res[2 * n + 2]


def _chip_wait(psums, lands, send_sems, recv_sems, after, name):
    n, na = len(psums), len(after)

    def body(*refs):
        ssem, rsem = refs[2 * n], refs[2 * n + 1]
        outs = refs[2 * n + 2 + na:]
        srcs, dsts = outs[:n], outs[n:]
        x, y, c, chips = _mesh_pos()
        for a in range(n):
            for k, chip in enumerate(chips):
                cp = _chip_copy(srcs[a], dsts[a], a, k, chip, c, (ssem, rsem))
                cp.wait_send()
                cp.wait_recv()

    res = pl.pallas_call(
        body, name=name, in_specs=[ANY] * (2 * n) + [SEM, SEM] + [ANY] * na, out_specs=[ANY] * (2 * n),
        out_shape=[pltpu.HBM(p.shape, BF16) for p in psums]
        + [pltpu.HBM(l.shape, BF16) for l in lands],
        input_output_aliases={i: i for i in range(2 * n)},
        compiler_params=SPLIT_COPY,
    )(*_in_hbm(psums), *_in_hbm(lands), send_sems, recv_sems, *after)
    return res[n:]


def _owner_sum(g, recv_a, recv_b, pos, name):
    _, _, rh, c = g.shape
    tr = _pick_rows(rh, c)

    def body(pos_ref, g_ref, ra_ref, rb_ref, o_ref):
        acc = g_ref[...] + ra_ref[...]
        for k in range(3):
            acc = acc + rb_ref[k].astype(F32)
        o_ref[...] = acc

    return pl.pallas_call(
        body, name=name,
        grid_spec=pltpu.PrefetchScalarGridSpec(
            num_scalar_prefetch=1, grid=(rh // tr,),
            in_specs=[pl.BlockSpec((None, None, tr, c), lambda r, p: (p[0], p[1], r, 0)),
                      pl.BlockSpec((None, tr, c), lambda r, p: (p[1], r, 0)),
                      pl.BlockSpec((3, tr, c), lambda r, p: (0, r, 0))],
            out_specs=pl.BlockSpec((None, tr, c), lambda r, p: (p[0], r, 0))),
        out_shape=pltpu.HBM((2, rh, c), F32),
        compiler_params=_cp(("parallel",)),
    )(pos, *_in_hbm([g, recv_a, recv_b]))


def _sibling_allgather(halves, name):
    n = len(halves)

    def body(*refs):
        outs = refs[n:2 * n]
        send_sems, recv_sems = refs[2 * n:]
        x, y, c, _ = _mesh_pos()
        cps = []
        for a in range(n):
            cp = pltpu.make_async_remote_copy(
                src_ref=outs[a].at[c], dst_ref=outs[a].at[c], send_sem=send_sems.at[a], recv_sem=recv_sems.at[a],
                device_id=(x, y, 1 - c), device_id_type=MESH_T)
            cp.start()
            cps.append(cp)
        for a in range(n):
            cps[a].wait_send()
            pltpu.make_async_remote_copy(
                src_ref=outs[a].at[1 - c], dst_ref=outs[a].at[1 - c], send_sem=send_sems.at[a],
                recv_sem=recv_sems.at[a], device_id=(x, y, 1 - c), device_id_type=MESH_T).wait_recv()

    return pl.pallas_call(
        body, name=name, in_specs=[ANY] * n, out_specs=[ANY] * n,
        out_shape=[pltpu.HBM(h.shape, F32) for h in halves],
        input_output_aliases={i: i for i in range(n)},
        scratch_shapes=[pltpu.SemaphoreType.DMA((n,)), pltpu.SemaphoreType.DMA((n,))],
    )(*_in_hbm(halves))


def _peers(x, y, c):
    rel = [(0, 0, 1), (0, 1, 0), (0, 1, 1), (1, 0, 0), (1, 0, 1), (1, 1, 0), (1, 1, 1)]
    return [((1 - x) if dx else x, (1 - y) if dy else y, (1 - c) if dc else c) for dx, dy, dc in rel]


def _small_copy(src, land, k, peer, slot, sems):
    return pltpu.make_async_remote_copy(src_ref=src, dst_ref=land.at[slot], send_sem=sems[0].at[k],
                                        recv_sem=sems[1].at[k], device_id=peer, device_id_type=MESH_T)


def _small_start(part, land, name):
    def body(p_in, l_in, p_ref, l_ref, send_sems, recv_sems, token):
        x, y, c, _ = _mesh_pos()
        for k, peer in enumerate(_peers(x, y, c)):
            _small_copy(p_ref, l_ref, k, peer, 4 * x + 2 * y + c, (send_sems, recv_sems)).start()
        token[...] = jnp.zeros_like(token)

    return pl.pallas_call(
        body, name=name, in_specs=[ANY, ANY],
        out_specs=[ANY, ANY, SEM, SEM, pl.BlockSpec(memory_space=pltpu.VMEM)],
        out_shape=[pltpu.HBM(part.shape, F32), pltpu.HBM(land.shape, F32), pltpu.SemaphoreType.DMA((7,)),
                   pltpu.SemaphoreType.DMA((7,)), jax.ShapeDtypeStruct((8, 128), F32)],
        input_output_aliases={0: 0, 1: 1},
        compiler_params=SPLIT_COPY,
    )(*_in_hbm([part, land]))


def _small_wait(part, land, send_sems, recv_sems, after, name):
    na = len(after)

    def body(*refs):
        ssem, rsem = refs[2], refs[3]
        p_ref, l_ref = refs[4 + na:]
        x, y, c, _ = _mesh_pos()
        for k, peer in enumerate(_peers(x, y, c)):
            cp = _small_copy(p_ref, l_ref, k, peer, 4 * peer[0] + 2 * peer[1] + peer[2], (ssem, rsem))
            cp.wait_send()
            cp.wait_recv()

    return pl.pallas_call(
        body, name=name, in_specs=[ANY, ANY, SEM, SEM] + [ANY] * na, out_specs=[ANY, ANY],
        out_shape=[pltpu.HBM(part.shape, F32), pltpu.HBM(land.shape, F32)],
        input_output_aliases={0: 0, 1: 1},
        compiler_params=SPLIT_COPY,
    )(*_in_hbm([part, land]), send_sems, recv_sems, *after)


def _pack_small(ln1_g, ln1_b, gln_g, gln_b, ln2_g, ln2_b, ln3_g, ln3_b, b_gates, b_s, w_s):
    rows = [ln1_g, ln1_b, gln_g, gln_b, ln2_g, ln2_b, ln3_g, ln3_b]
    rows = [r.reshape(1, D) for r in rows] + [b_gates.reshape(2, D), b_s.reshape(1, D), jnp.zeros((5, D), F32),
                                             w_s.reshape(128, D)]
    return jnp.concatenate(rows, axis=0)


def _unpack_small(p):
    out = [p[i:i + 1] for i in range(8)]
    return out + [p[8:10].reshape(1, 2 * D), p[10:11].reshape(1, 8, BLK), p[16:144].reshape(1, 8, BLK, BLK)]


GROUPS = (("f1g", "f1u", "f1d"), ("w_in",), ("w_ab", "w_gb", "w_out"), ("f2g", "f2u", "f2d"))
LATE_GROUPS = (2, 3)


def _local_step(x, pos_f, target, P, weights_of, grads_ready, flush, small_ready):
    invf = ROPE_THETA ** (-jnp.arange(0, DH, 2, dtype=F32) / DH)
    invf = jnp.tile(invf, 4).reshape(1, 128)
    b_s_t = P["gmlp_b_s"].T

    W = dict(weights_of(0, []))
    h1b, xh1, rstd1, a1, b1, h1t = _ffn_fwd(x, W["f1g"], W["f1u"], W["f1d"], P["ln1_g"], P["ln1_b"], "ffn1_fwd",
                                                emit_t=True)
    W.update(weights_of(1, [h1b]))
    qkv_c = _proj_qkv_rope(h1b, W["w_in"], pos_f, invf, "proj_qkv_rope")
    z = _matmul(h1b, W["w_in"], "nn", "proj_z", n=2 * GW, b_col0=3 * ATT_W, tm=S, tn=512)
    gl = _matmul(h1b, W["w_in"], "nn", "proj_gates", n=2 * D, b_col0=3 * ATT_W + 2 * GW, tm=S, tn=512)
    og = [_attn_fwd(gi, qkv_c[gi], "attn_fwd_g%d" % gi) for gi in range(NG)]
    y_attn, y_attn_t, lse = _attn_combine([o for o, _ in og], [l for _, l in og], "attn_combine")
    y_gmlp, y_gmlp_t = _gmlp_fwd(z, P["gmlp_ln_g"], P["gmlp_ln_b"], P["gmlp_w_s"], b_s_t, "gmlp_fwd")
    W.update(weights_of(2, [y_gmlp]))
    br_a = _matmul(y_attn, W["w_ab"], "nn", "branch_attn", n=D, tm=1024, tn=D)
    br_b = _matmul(y_gmlp, W["w_gb"], "nn", "branch_gmlp", n=D, tm=1024, tn=D)
    merged, merged_t = _merge_fwd(br_a, br_b, gl, P["b_gates"], "merge_fwd")
    h2, h2b, xh2, rstd2 = _resid_ln(xh1, P["ln1_g"], P["ln1_b"], merged, W["w_out"], P["ln2_g"], P["ln2_b"],
                                    "mix_resid_ln2")
    W.update(weights_of(3, [h2b]))
    dr3, a2, b2, dg3, db3, loss = _ffn_fwd(h2, W["f2g"], W["f2u"], W["f2d"], P["ln3_g"], P["ln3_b"],
                                           "ffn2_fwd_loss", target=target)

    g_f2g, g_f2u, g_f2d, dh2 = _ffn_bwd(dr3, h2b, a2, b2, W["f2g"], W["f2u"], W["f2d"], "ffn2_bwd")
    tok = grads_ready(3, dict(f2g=g_f2g, f2u=g_f2u, f2d=g_f2d))
    dr2, dg2, db2 = _ln_bwd(dh2, xh2, rstd2, P["ln2_g"], "ln2_bwd", after=tok)
    g_wout = _wgrad(merged_t, dr2, 128, D, "dw_out", row_sharded=True)
    dmerged = _matmul(dr2, W["w_out"], "nt", "dmerged", n=D, tm=1024, tn=D)
    dab, dbb, dglb, dbg = _merge_bwd(dmerged, br_a, br_b, gl, P["b_gates"], "merge_bwd")
    tok = flush([dab])
    g_wab = _wgrad(y_attn_t, dab, GRP_W // 2, 256, "dw_attn_branch", row_sharded=False, after=tok)
    g_wgb = _wgrad(y_gmlp_t, dbb, 128, D, "dw_gmlp_branch", row_sharded=True)
    tok = grads_ready(2, dict(w_ab=g_wab, w_gb=g_wgb, w_out=g_wout))
    dy_attn = _matmul(dab, W["w_ab"], "nt", "dy_attn", n=GRP_W, tm=1024, tn=GRP_W, after=tok)
    dy_gmlp = _matmul(dbb, W["w_gb"], "nt", "dy_gmlp", n=GW, tm=1024, tn=GW)
    dzb, dws, dbs_t, dgln_g, dgln_b = _gmlp_bwd(z, dy_gmlp, P["gmlp_ln_g"], P["gmlp_ln_b"], P["gmlp_w_s"], b_s_t,
                                                 "gmlp_bwd")
    cls = _class_order([dy_attn, y_attn, lse], "attn_class_order")
    dqkv_c = []
    for gi in range(NG):
        dy_c, y_c, lse_c = [t[None] if gi == 0 else cls[2 * a + gi - 1] for a, t in enumerate((dy_attn, y_attn, lse))]
        dqkv_c.append(_attn_bwd(gi, qkv_c[gi], dy_c, y_c, lse_c, "attn_bwd_g%d" % gi))
    dqkvb = _rope_bwd(dqkv_c, pos_f, invf, "rope_bwd")
    dproj = jnp.concatenate([dqkvb, dzb, dglb], axis=1)
    tok = flush([dproj])
    g_win = _wgrad(h1t, dproj, D // 2, IN_SH, "dw_in", row_sharded=False, after=tok)
    tok = grads_ready(1, dict(w_in=g_win))
    dr1, dg1, db1 = _dh1_ln_bwd(dproj, W["w_in"], dr2, xh1, rstd1, P["ln1_g"], "dh1_ln1_bwd", after=tok)
    tok = flush([dr1])
    tok = tok + small_ready(_pack_small(dg1, db1, dgln_g, dgln_b, dg2, db2, dg3, db3, dbg, dbs_t.T, dws))
    g_f1g, g_f1u, g_f1d, dx = _ffn_bwd(dr1, x.astype(BF16), a1, b1, W["f1g"], W["f1u"], W["f1d"], "ffn1_bwd",
                                       after=tok)
    tok = grads_ready(0, dict(f1g=g_f1g, f1u=g_f1u, f1d=g_f1d))
    return loss, dx, tok


TRANSPOSED = ("f1g", "f1u", "f2g", "f2u")
KIND = dict(f1g="stack", f1u="stack", f1d="stack", w_in="col", w_ab="col", w_gb="stack", w_out="stack",
            f2g="stack", f2u="stack", f2d="stack")


def kernel(x, positions, ffn1_w_gate, ffn1_w_up, ffn1_w_down, ln1_g, ln1_b, w_in, b_gates, gmlp_ln_g, gmlp_ln_b, gmlp_w_s, gmlp_b_s, w_attn_branch, w_gmlp_branch, w_out, ln2_g, ln2_b, ffn2_w_gate, ffn2_w_up, ffn2_w_down, ln3_g, ln3_b, loss_target, m_ffn1_w_gate, m_ffn1_w_up, m_ffn1_w_down, m_ln1_g, m_ln1_b, m_w_in, m_b_gates, m_gmlp_ln_g, m_gmlp_ln_b, m_gmlp_w_s, m_gmlp_b_s, m_w_attn_branch, m_w_gmlp_branch, m_w_out, m_ln2_g, m_ln2_b, m_ffn2_w_gate, m_ffn2_w_up, m_ffn2_w_down, m_ln3_g, m_ln3_b, v_ffn1_w_gate, v_ffn1_w_up, v_ffn1_w_down, v_ln1_g, v_ln1_b, v_w_in, v_b_gates, v_gmlp_ln_g, v_gmlp_ln_b, v_gmlp_w_s, v_gmlp_b_s, v_w_attn_branch, v_w_gmlp_branch, v_w_out, v_ln2_g, v_ln2_b, v_ffn2_w_gate, v_ffn2_w_up, v_ffn2_w_down, v_ln3_g, v_ln3_b):
    cx, cy, cc = lax.axis_index("x"), lax.axis_index("y"), lax.axis_index("c")
    pos = jnp.stack([cc, 2 * cx + cy, 2 * (1 - cx) + cy, 2 * cx + 1 - cy, 2 * (1 - cx) + 1 - cy]).astype(jnp.int32)

    w_sh = dict(f1g=ffn1_w_gate, f1u=ffn1_w_up, f1d=ffn1_w_down, w_in=w_in, w_ab=w_attn_branch,
                w_gb=w_gmlp_branch, w_out=w_out, f2g=ffn2_w_gate, f2u=ffn2_w_up, f2d=ffn2_w_down)
    m_sh = dict(f1g=m_ffn1_w_gate, f1u=m_ffn1_w_up, f1d=m_ffn1_w_down, w_in=m_w_in, w_ab=m_w_attn_branch,
                w_gb=m_w_gmlp_branch, w_out=m_w_out, f2g=m_ffn2_w_gate, f2u=m_ffn2_w_up, f2d=m_ffn2_w_down)
    v_sh = dict(f1g=v_ffn1_w_gate, f1u=v_ffn1_w_up, f1d=v_ffn1_w_down, w_in=v_w_in, w_ab=v_w_attn_branch,
                w_gb=v_w_gmlp_branch, w_out=v_w_out, f2g=v_ffn2_w_gate, f2u=v_ffn2_w_up, f2d=v_ffn2_w_down)
    w_sh = {k: (v[0].T if k in TRANSPOSED else v[0]) for k, v in w_sh.items()}
    m_sh = {k: (v[0].T if k in TRANSPOSED else v[0]) for k, v in m_sh.items()}
    v_sh = {k: (v[0].T if k in TRANSPOSED else v[0]) for k, v in v_sh.items()}

    started, tokens = [], []
    for gi, names in enumerate(GROUPS):
        placed = [_place_shard(w_sh[k], KIND[k], pos, "place_" + k) for k in names]
        fulls, ssem, rsem, token = _gather_start(placed, [KIND[k] for k in names], [w_sh[k].shape for k in names],
                                                 tokens[-1:], "gather_start_g%d" % gi, both=gi in LATE_GROUPS)
        started.append((fulls, ssem, rsem))
        tokens.append(token)

    def weights_of(gi, after):
        names = GROUPS[gi]
        kinds, dims = [KIND[k] for k in names], [w_sh[k].shape for k in names]
        fulls, ssem, rsem = started[gi]
        fulls = _gather_wait(fulls, ssem, rsem, kinds, dims, list(after) + (tokens if gi == 0 else []),
                             "gather_wait_g%d" % gi, both=gi in LATE_GROUPS)
        if gi not in LATE_GROUPS:
            fulls = _gather_forward(fulls, kinds, dims, "gather_forward_g%d" % gi)
        return {k: (f.reshape(D, D) if k in ("w_gb", "w_out") else f) for k, f in zip(names, fulls)}

    pending, inflight = [], {}

    def grads_ready(gi, gd):
        grads = [gd[k] for k in GROUPS[gi]]
        lands = [lax.empty(g.shape[1:], F32) for g in grads]
        grads, lands, ssem, rsem, token = _pair_start(grads, lands, "rs_pair_start_g%d" % gi)
        pending.append((gi, grads, lands, ssem, rsem))
        return [token]

    def flush(after):
        gi, grads, lands, ssem, rsem = pending.pop()
        names = GROUPS[gi]
        grads, recv_a = _pair_wait(grads, lands, ssem, rsem, after, "rs_pair_wait_g%d" % gi)
        psums = [_pair_sum(g, r, pos, "rs_pair_sum_" + k) for g, r, k in zip(grads, recv_a, names)]
        lands = [lax.empty((3,) + p.shape[1:], BF16) for p in psums]
        psums, lands, ssem, rsem, token = _chip_start(psums, lands, "rs_chip_start_g%d" % gi)
        inflight[gi] = (grads, recv_a, psums, lands, ssem, rsem, token)
        return [token]

    P = dict(ln1_g=ln1_g, ln1_b=ln1_b, ln2_g=ln2_g, ln2_b=ln2_b, ln3_g=ln3_g, ln3_b=ln3_b, b_gates=b_gates,
             gmlp_ln_g=gmlp_ln_g, gmlp_ln_b=gmlp_ln_b, gmlp_w_s=gmlp_w_s[0], gmlp_b_s=gmlp_b_s[0])
    pos_f = positions.reshape(S, 1).astype(F32)
    small_state = []

    def small_ready(packed):
        land = jnp.zeros((8, SMALL_ROWS, D), F32)
        packed, land, ssem, rsem, token = _small_start(packed, land, "small_start")
        small_state.append((packed, land, ssem, rsem))
        return [token]

    loss_part, dx, tok_last = _local_step(x[0], pos_f, loss_target[0], P, weights_of, grads_ready, flush,
                                          small_ready)
    loss = lax.psum(loss_part[0, 0], ("x", "y", "c"))

    g_out, d_out, m_out, v_out = {}, {}, {}, {}

    def finish(gis, after, tag):
        names, halves = [], []
        for gi in gis:
            grads, recv_a, psums, lands, ssem, rsem, token = inflight[gi]
            recv_b = _chip_wait(psums, lands, ssem, rsem, after, "rs_chip_wait_g%d" % gi)
            halves += [_owner_sum(g, ra, rb, pos, "rs_owner_sum_" + k)
                       for g, ra, rb, k in zip(grads, recv_a, recv_b, GROUPS[gi])]
            names += GROUPS[gi]
            after = halves[-1:]
        reduced = _sibling_allgather(halves, "rs_sibling_allgather_" + tag)
        for k, gfull in zip(names, reduced):
            res = _adamw(w_sh[k], gfull.reshape(w_sh[k].shape), m_sh[k], v_sh[k], "adamw_" + k)
            after = [res[1]]
            if k in TRANSPOSED:
                res = [r.T for r in res]
            g_out[k], d_out[k], m_out[k], v_out[k] = [r[None] for r in res]
        return after

    after = finish((3,), tok_last, "g3")
    after = finish((2, 1), flush(after), "g21")

    small, parts = _small_wait(*small_state[0], after, "small_wait")
    sp = (ln1_g, ln1_b, gmlp_ln_g, gmlp_ln_b, ln2_g, ln2_b, ln3_g, ln3_b, b_gates, gmlp_b_s, gmlp_w_s)
    sm = (m_ln1_g, m_ln1_b, m_gmlp_ln_g, m_gmlp_ln_b, m_ln2_g, m_ln2_b, m_ln3_g, m_ln3_b, m_b_gates, m_gmlp_b_s,
          m_gmlp_w_s)
    sv = (v_ln1_g, v_ln1_b, v_gmlp_ln_g, v_gmlp_ln_b, v_ln2_g, v_ln2_b, v_ln3_g, v_ln3_b, v_b_gates, v_gmlp_b_s,
          v_gmlp_w_s)
    sg, sd, smn, svn = _small_sum_adamw(parts, small, pos, _pack_small(*sp), _pack_small(*sm), _pack_small(*sv),
                                        "small_adamw")
    names = ("ln1_g", "ln1_b", "gmlp_ln_g", "gmlp_ln_b", "ln2_g", "ln2_b", "ln3_g", "ln3_b", "b_gates", "gmlp_b_s",
             "gmlp_w_s")
    for dst, packed in ((g_out, sg), (d_out, sd), (m_out, smn), (v_out, svn)):
        for nm, val in zip(names, _unpack_small(packed)):
            dst[nm] = val
    finish((0,), [sg], "g0")

    order = ("f1g", "f1u", "f1d", "ln1_g", "ln1_b", "w_in", "b_gates", "gmlp_ln_g", "gmlp_ln_b", "gmlp_w_s", "gmlp_b_s",
             "w_ab", "w_gb", "w_out", "ln2_g", "ln2_b", "f2g", "f2u", "f2d", "ln3_g", "ln3_b")
    outs = [loss, dx[None]]
    for dst in (g_out, d_out, m_out, v_out):
        outs += [dst[k] for k in order]
    return tuple(outs)
```

```python
import jax
import jax.numpy as jnp
from jax import lax
from jax.experimental import pallas as pl
from jax.experimental.pallas import tpu as pltpu

F32 = jnp.float32
BF16 = jnp.bfloat16

S = 2048
D = 1024
NSH = 4
FSH = 704
ATT_W = 1536
GRP_W = 512
NG = 3
NH = 8
DH = 64
BLK = 128
NBLK = S // BLK
GW = 1024
IN_W = 8704
IN_SH = IN_W // NSH
ALPHA = 2.0 ** 0.25
LN_EPS = 1e-5
ROPE_THETA = 10000.0
DILATIONS = (1, 4, 16)
ADAM_LR, ADAM_B1, ADAM_B2, ADAM_EPS, ADAM_WD, ADAM_STEP = 0.001, 0.9, 0.999, 1e-08, 0.01, 10
SMALL_ROWS = 144
EPI_ROWS = 256
MESH_T = pl.DeviceIdType.MESH
MIB = 1024 * 1024
NEG_INF = float("-inf")


def _cp(sem, vmem_mib=48):
    return pltpu.CompilerParams(dimension_semantics=sem, vmem_limit_bytes=vmem_mib * MIB)


def _ln_stats(r):
    mu = jnp.mean(r, axis=-1, keepdims=True)
    xc = r - mu
    var = jnp.mean(xc * xc, axis=-1, keepdims=True)
    rstd = lax.rsqrt(var + LN_EPS)
    return xc * rstd, rstd


def _ln_dx(dxh, xh, rstd):
    m1 = jnp.mean(dxh, axis=-1, keepdims=True)
    m2 = jnp.mean(dxh * xh, axis=-1, keepdims=True)
    return rstd * (dxh - m1 - xh * m2)


def _dot_nt(a, b):
    return lax.dot_general(a, b, (((1,), (1,)), ((), ())), preferred_element_type=F32)


def _dot_tn(a, b):
    return lax.dot_general(a, b, (((0,), (0,)), ((), ())), preferred_element_type=F32)


def _dot(a, b):
    return jnp.dot(a, b, preferred_element_type=F32)


def _ffn_fwd(xin, wgt, wut, wd, ln_g, ln_b, name, emit_t=False, target=None):
    with_loss = target is not None
    tm = 1024

    def body(x_ref, wg_ref, wu_ref, wd_ref, g_ref, b_ref, *rest):
        if with_loss:
            t_ref, dr_ref, a_ref, bb_ref, dg_ref, db_ref, loss_ref, acc_ref = rest
        elif emit_t:
            hb_ref, xh_ref, rstd_ref, a_ref, bb_ref, ht_ref, acc_ref = rest
        else:
            hb_ref, xh_ref, rstd_ref, a_ref, bb_ref, acc_ref = rest
        i = pl.program_id(0)
        j = pl.program_id(1)
        xb = x_ref[...].astype(BF16)
        a = _dot_nt(xb, wg_ref[...])
        b = _dot_nt(xb, wu_ref[...])
        a_ref[...] = a.astype(BF16)
        bb_ref[...] = b.astype(BF16)
        s = (a * jax.nn.sigmoid(a)) * b
        f = _dot(s.astype(BF16), wd_ref[...])

        @pl.when(j == 0)
        def _():
            acc_ref[...] = f

        @pl.when(j > 0)
        def _():
            acc_ref[...] += f

        if with_loss:
            @pl.when(jnp.logical_and(j == NSH - 1, i == 0))
            def _():
                dg_ref[...] = jnp.zeros_like(dg_ref)
                db_ref[...] = jnp.zeros_like(db_ref)
                loss_ref[...] = jnp.zeros_like(loss_ref)

        @pl.when(j == NSH - 1)
        def _():
            for c0 in range(0, tm, EPI_ROWS):
                rows = slice(c0, c0 + EPI_ROWS)
                r = ALPHA * x_ref[rows, :] + 0.5 * acc_ref[rows, :]
                xh, rstd = _ln_stats(r)
                h = xh * g_ref[...] + b_ref[...]
                if with_loss:
                    err = h - t_ref[rows, :]
                    dy = err * (1.0 / D)
                    dr_ref[rows, :] = _ln_dx(dy * g_ref[...], xh, rstd)
                    dg_ref[...] += jnp.sum(dy * xh, axis=0, keepdims=True)
                    db_ref[...] += jnp.sum(dy, axis=0, keepdims=True)
                    part = 0.5 * jnp.sum(jnp.mean(err * err, axis=-1, keepdims=True), axis=0, keepdims=True)
                    loss_ref[...] += jnp.broadcast_to(part, (8, 128))
                else:
                    hb_ref[rows, :] = h.astype(BF16)
                    xh_ref[rows, :] = xh
                    rstd_ref[rows, :] = rstd
                    if emit_t:
                        ht_ref[:, rows] = h.T.astype(BF16)

    row = pl.BlockSpec((tm, D), lambda i, j: (i, 0))
    vec = pl.BlockSpec((1, D), lambda i, j: (0, 0))
    wsp = pl.BlockSpec((None, FSH, D), lambda i, j: (j, 0, 0))
    ab = pl.BlockSpec((None, tm, FSH), lambda i, j: (j, i, 0))
    ab_shape = jax.ShapeDtypeStruct((NSH, S, FSH), BF16)
    in_specs, args = [row, wsp, wsp, wsp, vec, vec], (xin, wgt, wut, wd, ln_g, ln_b)
    if with_loss:
        in_specs, args = in_specs + [row], args + (target,)
        out_specs = [row, ab, ab, vec, vec, pl.BlockSpec((8, 128), lambda i, j: (0, 0))]
        out_shape = [jax.ShapeDtypeStruct((S, D), F32), ab_shape, ab_shape, jax.ShapeDtypeStruct((1, D), F32),
                     jax.ShapeDtypeStruct((1, D), F32), jax.ShapeDtypeStruct((8, 128), F32)]
    else:
        out_specs = [row, row, pl.BlockSpec((tm, 1), lambda i, j: (i, 0)), ab, ab]
        out_shape = [jax.ShapeDtypeStruct((S, D), BF16), jax.ShapeDtypeStruct((S, D), F32),
                     jax.ShapeDtypeStruct((S, 1), F32), ab_shape, ab_shape]
        if emit_t:
            out_specs.append(pl.BlockSpec((D, tm), lambda i, j: (0, i)))
            out_shape.append(jax.ShapeDtypeStruct((D, S), BF16))
    return pl.pallas_call(
        body, name=name, grid=(S // tm, NSH), in_specs=in_specs, out_specs=out_specs, out_shape=out_shape,
        scratch_shapes=[pltpu.VMEM((tm, D), F32)],
        compiler_params=_cp(("arbitrary" if with_loss else "parallel", "arbitrary"), vmem_mib=56),
    )(*args)


def _ffn_bwd(dr, xin_b, a, b, wgt, wut, wd, name, after=()):
    tm = 512
    ni = S // tm
    hr = FSH // 2

    def body(dr_ref, a_ref, b_ref, wg_ref, wu_ref, wd_ref, x_hbm, *rest):
        dwg_hbm, dwu_hbm, dwd_hbm, dx_hbm, dx_acc, da_all, db_all, s_all, df_all, x_all, res_buf, sems = rest[len(after):]
        j = pl.program_id(0)
        i = pl.program_id(1)
        rows = pl.ds(pl.multiple_of(i * tm, tm), tm)

        @pl.when(jnp.logical_and(j == 0, i == 0))
        def _():
            cp = pltpu.make_async_copy(x_hbm, x_all, sems.at[0])
            cp.start()
            cp.wait()

        drv = dr_ref[...]
        df = (0.5 * drv).astype(BF16)

        @pl.when(j == 0)
        def _():
            df_all[rows, :] = df

        ds = jnp.concatenate([_dot_nt(df, wd_ref[0:384, :]), _dot_nt(df, wd_ref[384:FSH, :])], axis=1)
        av = a_ref[...].astype(F32)
        bv = b_ref[...].astype(F32)
        sig = jax.nn.sigmoid(av)
        sl = av * sig
        da = (ds * bv * (sig * (1.0 + av * (1.0 - sig)))).astype(BF16)
        db = (ds * sl).astype(BF16)
        da_all[rows, :] = da
        db_all[rows, :] = db
        s_all[rows, :] = (sl * bv).astype(BF16)
        dx = _dot(da, wg_ref[...]) + _dot(db, wu_ref[...])

        @pl.when(j == 0)
        def _():
            dx_acc[rows, :] = ALPHA * drv + dx

        @pl.when(j > 0)
        def _():
            dx_acc[rows, :] += dx

        @pl.when(i == ni - 1)
        def _():
            copies = []
            for n, (lhs, rhs, out) in enumerate(((da_all, x_all, dwg_hbm), (db_all, x_all, dwu_hbm),
                                                 (s_all, df_all, dwd_hbm))):
                slot = n % 2
                if n >= 2:
                    for cp in copies[2 * (n - 2): 2 * (n - 2) + 2]:
                        cp.wait()
                res_buf[slot] = _dot_tn(lhs[...], rhs[...])
                for h in range(2):
                    cp = pltpu.make_async_copy(res_buf.at[slot, pl.ds(h * hr, hr), :], out.at[h, j],
                                               sems.at[1 + 2 * slot + h])
                    cp.start()
                    copies.append(cp)
            for cp in copies[2:]:
                cp.wait()

        @pl.when(jnp.logical_and(j == NSH - 1, i == ni - 1))
        def _():
            cp = pltpu.make_async_copy(dx_acc, dx_hbm, sems.at[0])
            cp.start()
            cp.wait()

    row = pl.BlockSpec((tm, D), lambda j, i: (i, 0))
    wsp = pl.BlockSpec((None, FSH, D), lambda j, i: (j, 0, 0))
    ab = pl.BlockSpec((None, tm, FSH), lambda j, i: (j, i, 0))
    dwshape = jax.ShapeDtypeStruct((2, NSH, hr, D), F32)
    return pl.pallas_call(
        body, name=name, grid=(NSH, ni),
        in_specs=[row, ab, ab, wsp, wsp, wsp, ANY] + [ANY] * len(after),
        out_specs=[ANY, ANY, ANY, ANY],
        out_shape=[dwshape, dwshape, dwshape, jax.ShapeDtypeStruct((S, D), F32)],
        scratch_shapes=[pltpu.VMEM((S, D), F32), pltpu.VMEM((S, FSH), BF16), pltpu.VMEM((S, FSH), BF16),
                        pltpu.VMEM((S, FSH), BF16), pltpu.VMEM((S, D), BF16), pltpu.VMEM((S, D), BF16),
                        pltpu.VMEM((2, FSH, D), F32), pltpu.SemaphoreType.DMA((5,))],
        compiler_params=_cp(("arbitrary", "arbitrary"), vmem_mib=58),
    )(dr, a, b, wgt, wut, wd, xin_b, *after)


def _matmul(a, b, mode, name, *, n, tm, tn, b_col0=0, after=()):
    m, k = a.shape
    assert m % tm == 0 and n % tn == 0 and b_col0 % tn == 0
    off = b_col0 // tn
    na = len(after)

    def body(*refs):
        a_ref, b_ref, o_ref = refs[na:]
        av = a_ref[...].astype(BF16)
        o_ref[...] = _dot(av, b_ref[...]) if mode == "nn" else _dot_nt(av, b_ref[...])

    if mode == "nn":
        b_spec = pl.BlockSpec((k, tn), lambda i, j: (0, j + off))
    else:
        b_spec = pl.BlockSpec((tn, k), lambda i, j: (j, 0))
    return pl.pallas_call(
        body, name=name, grid=(m // tm, n // tn),
        in_specs=[pl.BlockSpec(memory_space=pl.ANY)] * na + [pl.BlockSpec((tm, k), lambda i, j: (i, 0)), b_spec],
        out_specs=pl.BlockSpec((tm, tn), lambda i, j: (i, j)),
        out_shape=jax.ShapeDtypeStruct((m, n), F32),
        compiler_params=_cp(("parallel", "parallel")),
    )(*after, a, b)


def _wgrad(xt, y, rh, c, name, row_sharded, after=()):
    na = len(after)
    if row_sharded:
        def body(x_ref, y_ref, *rest):
            o_ref = rest[na]
            res = _dot(x_ref[...], y_ref[...].astype(BF16))
            for j in range(NSH):
                for h in range(2):
                    o_ref[h, j] = res[(2 * j + h) * rh:(2 * j + h + 1) * rh, :]

        grid = (1,)
        in_specs = [pl.BlockSpec((2 * NSH * rh, S), lambda g: (0, 0)), pl.BlockSpec((S, c), lambda g: (0, 0))]
        out_specs = pl.BlockSpec((2, NSH, rh, c), lambda g: (0, 0, 0, 0))
        sem = ("arbitrary",)
    else:
        def body(x_ref, y_ref, *rest):
            rest[na][...] = _dot(x_ref[...], y_ref[...].astype(BF16))

        grid = (2, NSH)
        in_specs = [pl.BlockSpec((rh, S), lambda h, j: (h, 0)), pl.BlockSpec((S, c), lambda h, j: (0, j))]
        out_specs = pl.BlockSpec((None, None, rh, c), lambda h, j: (h, j, 0, 0))
        sem = ("parallel", "parallel")
    return pl.pallas_call(
        body, name=name, grid=grid, in_specs=in_specs + [pl.BlockSpec(memory_space=pl.ANY)] * na, out_specs=out_specs,
        out_shape=jax.ShapeDtypeStruct((2, NSH, rh, c), F32),
        compiler_params=_cp(sem, vmem_mib=56),
    )(xt, y, *after)


def _resid_ln(res_xh, res_g, res_b, a, w, ln_g, ln_b, name):
    tm = 512

    def body(rx_ref, rg_ref, rb_ref, a_ref, w_ref, g_ref, b_ref, h_ref, hb_ref, xh_ref, rstd_ref):
        r = ALPHA * (rx_ref[...] * rg_ref[...] + rb_ref[...]) + _dot(a_ref[...], w_ref[...])
        xh, rstd = _ln_stats(r)
        h = xh * g_ref[...] + b_ref[...]
        h_ref[...] = h
        hb_ref[...] = h.astype(BF16)
        xh_ref[...] = xh
        rstd_ref[...] = rstd

    row = pl.BlockSpec((tm, D), lambda i: (i, 0))
    vec = pl.BlockSpec((1, D), lambda i: (0, 0))
    return pl.pallas_call(
        body, name=name, grid=(S // tm,),
        in_specs=[row, vec, vec, row, pl.BlockSpec((D, D), lambda i: (0, 0)), vec, vec],
        out_specs=[row, row, row, pl.BlockSpec((tm, 1), lambda i: (i, 0))],
        out_shape=[jax.ShapeDtypeStruct((S, D), F32), jax.ShapeDtypeStruct((S, D), BF16),
                   jax.ShapeDtypeStruct((S, D), F32), jax.ShapeDtypeStruct((S, 1), F32)],
        compiler_params=_cp(("parallel",)),
    )(res_xh, res_g, res_b, a, w, ln_g, ln_b)


def _dh1_ln_bwd(dproj, w_in, dr2, xh, rstd, ln_g, name, after=()):
    tm, tk, ch = 1024, IN_SH, EPI_ROWS
    nk = IN_W // tk
    na = len(after)

    def body(*refs):
        a_ref, b_ref, add_ref, xh_ref, rstd_ref, g_ref, dr_ref, dg_ref, db_ref, acc_ref = refs[na:]
        i = pl.program_id(0)
        k = pl.program_id(1)
        p = _dot_nt(a_ref[...], b_ref[...])

        @pl.when(k == 0)
        def _():
            acc_ref[...] = p

        @pl.when(k > 0)
        def _():
            acc_ref[...] += p

        @pl.when(jnp.logical_and(k == nk - 1, i == 0))
        def _():
            dg_ref[...] = jnp.zeros_like(dg_ref)
            db_ref[...] = jnp.zeros_like(db_ref)

        @pl.when(k == nk - 1)
        def _():
            for c0 in range(0, tm, ch):
                rows = slice(c0, c0 + ch)
                dy = acc_ref[rows, :] + ALPHA * add_ref[rows, :]
                xhv = xh_ref[rows, :]
                dr_ref[rows, :] = _ln_dx(dy * g_ref[...], xhv, rstd_ref[rows, :])
                dg_ref[...] += jnp.sum(dy * xhv, axis=0, keepdims=True)
                db_ref[...] += jnp.sum(dy, axis=0, keepdims=True)

    row = pl.BlockSpec((tm, D), lambda i, k: (i, 0))
    vec = pl.BlockSpec((1, D), lambda i, k: (0, 0))
    return pl.pallas_call(
        body, name=name, grid=(S // tm, nk),
        in_specs=[pl.BlockSpec(memory_space=pl.ANY)] * na
        + [pl.BlockSpec((tm, tk), lambda i, k: (i, k)), pl.BlockSpec((D, tk), lambda i, k: (0, k)), row, row,
           pl.BlockSpec((tm, 1), lambda i, k: (i, 0)), vec],
        out_specs=[row, vec, vec],
        out_shape=[jax.ShapeDtypeStruct((S, D), F32), jax.ShapeDtypeStruct((1, D), F32),
                   jax.ShapeDtypeStruct((1, D), F32)],
        scratch_shapes=[pltpu.VMEM((tm, D), F32)],
        compiler_params=_cp(("arbitrary", "arbitrary"), vmem_mib=56),
    )(*after, dproj, w_in, dr2, xh, rstd, ln_g)


def _ln_bwd(dout, xh, rstd, ln_g, name, after=()):
    tm = 512
    na = len(after)

    def body(*refs):
        y_ref, xh_ref, rstd_ref, g_ref, dr_ref, dg_ref, db_ref = refs[na:]
        dy = y_ref[...]
        i = pl.program_id(0)
        xh = xh_ref[...]
        dr_ref[...] = _ln_dx(dy * g_ref[...], xh, rstd_ref[...])
        dg = jnp.sum(dy * xh, axis=0, keepdims=True)
        db = jnp.sum(dy, axis=0, keepdims=True)

        @pl.when(i == 0)
        def _():
            dg_ref[...] = dg
            db_ref[...] = db

        @pl.when(i > 0)
        def _():
            dg_ref[...] += dg
            db_ref[...] += db

    row = pl.BlockSpec((tm, D), lambda i: (i, 0))
    vec = pl.BlockSpec((1, D), lambda i: (0, 0))
    return pl.pallas_call(
        body, name=name, grid=(S // tm,),
        in_specs=[pl.BlockSpec(memory_space=pl.ANY)] * na + [row, row, pl.BlockSpec((tm, 1), lambda i: (i, 0)), vec],
        out_specs=[row, vec, vec],
        out_shape=[jax.ShapeDtypeStruct((S, D), F32), jax.ShapeDtypeStruct((1, D), F32),
                   jax.ShapeDtypeStruct((1, D), F32)],
        compiler_params=_cp(("arbitrary",)),
    )(*after, dout, xh, rstd, ln_g)


ROPE_TM = 256


def _rope_tables(pos_ref, invf_ref, sign):
    ang = pos_ref[...] * invf_ref[...]
    lane = lax.broadcasted_iota(jnp.int32, ang.shape, 1)
    first = (lane % DH) < (DH // 2)
    sinv = jnp.sin(ang) * sign
    return first, jnp.cos(ang), jnp.where(first, -sinv, sinv)


def _rotate(x, first, cosf, sinf):
    return x * cosf + jnp.where(first, pltpu.roll(x, 96, 1), pltpu.roll(x, 32, 1)) * sinf


def _proj_qkv_rope(hb, w_in, pos_f, invf, name):
    tm = 2 * ROPE_TM

    def body(h_ref, w_ref, pos_ref, invf_ref, o0_ref, o1_ref, o2_ref, buf_ref):
        rot = pl.program_id(1) < 2
        first, cosf, sinf = _rope_tables(pos_ref, invf_ref, 1.0)
        cosf = jnp.where(rot, cosf, 1.0)
        sinf = jnp.where(rot, sinf, 0.0)
        acc = _dot(h_ref[...], w_ref[...])
        for gi, (d, o_ref) in enumerate(zip(DILATIONS, (o0_ref, o1_ref, o2_ref))):
            for ch in range(GRP_W // 128):
                cols = slice(ch * 128, (ch + 1) * 128)
                x = _rotate(acc[:, gi * GRP_W + ch * 128: gi * GRP_W + (ch + 1) * 128], first, cosf, sinf)
                if d == 1:
                    o_ref[0, :, cols] = x.astype(BF16)
                else:
                    buf_ref[...] = x
                    for r in range(d):
                        o_ref[r, :, cols] = buf_ref[pl.ds(r, tm // d, stride=d), :].astype(BF16)

    return pl.pallas_call(
        body, name=name, grid=(S // tm, 3),
        in_specs=[pl.BlockSpec((tm, D), lambda i, s: (i, 0)), pl.BlockSpec((D, ATT_W), lambda i, s: (0, s)),
                  pl.BlockSpec((tm, 1), lambda i, s: (i, 0)), pl.BlockSpec((1, 128), lambda i, s: (0, 0))],
        out_specs=[pl.BlockSpec((d, tm // d, GRP_W), lambda i, s: (0, i, s)) for d in DILATIONS],
        out_shape=[jax.ShapeDtypeStruct((d, S // d, 3 * GRP_W), BF16) for d in DILATIONS],
        scratch_shapes=[pltpu.VMEM((tm, 128), F32)],
        compiler_params=_cp(("parallel", "parallel")),
    )(hb, w_in, pos_f, invf)


def _rope_bwd(dqkv_c, pos_f, invf, name):
    tm = ROPE_TM

    def body(*refs):
        g_refs, (pos_ref, invf_ref, o_ref, buf_ref) = refs[:9], refs[9:]
        first, cosf, sinf = _rope_tables(pos_ref, invf_ref, -1.0)
        for sec in range(3):
            for gi, d in enumerate(DILATIONS):
                g_ref = g_refs[3 * gi + sec]
                for ch in range(GRP_W // 128):
                    cols = slice(ch * 128, (ch + 1) * 128)
                    if d == 1:
                        x = g_ref[0, :, cols]
                    else:
                        for r in range(d):
                            buf_ref[pl.ds(r, tm // d, stride=d), :] = g_ref[r, :, cols]
                        x = buf_ref[...]
                    if sec < 2:
                        x = _rotate(x, first, cosf, sinf)
                    dst = sec * ATT_W + gi * GRP_W + ch * 128
                    o_ref[:, dst:dst + 128] = x.astype(BF16)

    g_specs = [pl.BlockSpec((d, tm // d, GRP_W), lambda i: (0, i, 0)) for d in DILATIONS for _ in range(3)]
    return pl.pallas_call(
        body, name=name, grid=(S // tm,),
        in_specs=g_specs + [pl.BlockSpec((tm, 1), lambda i: (i, 0)), pl.BlockSpec((1, 128), lambda i: (0, 0))],
        out_specs=pl.BlockSpec((tm, 3 * ATT_W), lambda i: (i, 0)),
        out_shape=jax.ShapeDtypeStruct((S, 3 * ATT_W), BF16),
        scratch_shapes=[pltpu.VMEM((tm, 128), F32)],
        compiler_params=_cp(("parallel",)),
    )(*[g for grp in dqkv_c for g in grp], pos_f, invf)


def _class_order(ts, name):
    tm = ROPE_TM
    n = len(ts)

    def body(*refs):
        buf_ref = refs[3 * n]
        for a in range(n):
            for ch in range(GRP_W // 128):
                cols = slice(ch * 128, (ch + 1) * 128)
                buf_ref[...] = refs[a][:, cols]
                for b, d in enumerate(DILATIONS[1:]):
                    for r in range(d):
                        refs[n + 2 * a + b][r, :, cols] = buf_ref[pl.ds(r, tm // d, stride=d), :]

    return pl.pallas_call(
        body, name=name, grid=(S // tm,),
        in_specs=[pl.BlockSpec((tm, GRP_W), lambda i: (i, 0))] * n,
        out_specs=[pl.BlockSpec((d, tm // d, GRP_W), lambda i: (0, i, 0)) for _ in range(n) for d in DILATIONS[1:]],
        out_shape=[jax.ShapeDtypeStruct((d, S // d, GRP_W), F32) for _ in range(n) for d in DILATIONS[1:]],
        scratch_shapes=[pltpu.VMEM((tm, 128), F32)],
        compiler_params=_cp(("parallel",)),
    )(*ts)


def _own_lanes(h):
    return (lax.broadcasted_iota(jnp.int32, (1, 2 * DH), 1) // DH) == (h % 2)


def _heads(ref):
    out = []
    for h in range(NH):
        pair = ref[:, (h // 2) * 2 * DH:(h // 2 + 1) * 2 * DH]
        out.append(jnp.where(_own_lanes(h), pair, jnp.zeros_like(pair)))
    return jnp.stack(out)


def _unheads(t3):
    return jnp.concatenate([t3[2 * p] + t3[2 * p + 1] for p in range(NH // 2)], axis=1)


def _bdot_nt(a, b):
    return lax.dot_general(a, b, (((2,), (2,)), ((0,), (0,))), preferred_element_type=F32)


def _bdot(a, b):
    return lax.dot_general(a, b, (((2,), (1,)), ((0,), (0,))), preferred_element_type=F32)


def _bdot_tn(a, b):
    return lax.dot_general(a, b, (((1,), (1,)), ((0,), (0,))), preferred_element_type=F32)


def _attn_fwd(gi, qkv_c, name):
    d = DILATIONS[gi]
    nblk = S // d // BLK

    def body(*refs):
        if nblk > 1:
            q_ref, kc_ref, kp_ref, vc_ref, vp_ref, o_ref, lse_ref = refs
            has_prev = pl.program_id(1) != 0
        else:
            q_ref, kc_ref, vc_ref, o_ref, lse_ref = refs
        qi = lax.broadcasted_iota(jnp.int32, (NH, BLK, BLK), 1)
        kj = lax.broadcasted_iota(jnp.int32, (NH, BLK, BLK), 2)
        q = _heads(q_ref)
        sc = jnp.where(kj <= qi, _bdot_nt(q, _heads(kc_ref)) * 0.125, NEG_INF)
        m = jnp.max(sc, axis=-1, keepdims=True)
        if nblk > 1:
            mask_p = jnp.logical_and(kj >= qi, has_prev)
            sp = jnp.where(mask_p, _bdot_nt(q, _heads(kp_ref)) * 0.125, NEG_INF)
            m = jnp.maximum(m, jnp.max(sp, axis=-1, keepdims=True))
        pc = jnp.exp(sc - m)
        l = jnp.sum(pc, axis=-1, keepdims=True)
        o = _bdot(pc.astype(BF16), _heads(vc_ref))
        if nblk > 1:
            pp = jnp.exp(sp - m)
            l = l + jnp.sum(pp, axis=-1, keepdims=True)
            o = o + _bdot(pp.astype(BF16), _heads(vp_ref))
        o_ref[...] = _unheads(o / l)
        lse = jnp.broadcast_to(m + jnp.log(l), (NH, BLK, 2 * DH))
        lse_ref[...] = _unheads(jnp.stack([jnp.where(_own_lanes(h), lse[h], 0.0) for h in range(NH)]))

    def cur(sec):
        return pl.BlockSpec((None, BLK, GRP_W), lambda r, n: (r, n, sec))

    def prev(sec):
        return pl.BlockSpec((None, BLK, GRP_W), lambda r, n: (r, jnp.maximum(n - 1, 0), sec))

    out = pl.BlockSpec((None, BLK, GRP_W), lambda r, n: (r, n, 0))
    shp = jax.ShapeDtypeStruct((d, S // d, GRP_W), F32)
    if nblk > 1:
        in_specs, args = [cur(0), cur(1), prev(1), cur(2), prev(2)], (qkv_c,) * 5
    else:
        in_specs, args = [cur(0), cur(1), cur(2)], (qkv_c,) * 3
    return pl.pallas_call(
        body, name=name, grid=(d, nblk), in_specs=in_specs, out_specs=[out, out], out_shape=[shp, shp],
        compiler_params=_cp(("parallel", "parallel")),
    )(*args)


def _attn_combine(os, lses, name):
    tm = ROPE_TM

    def body(o0_ref, o1_ref, o2_ref, l0_ref, l1_ref, l2_ref, y_ref, yt_ref, l_ref, buf_ref):
        def token_order(ref, d, cols, slot):
            if d == 1:
                return ref[0, :, cols]
            for r in range(d):
                buf_ref[slot, pl.ds(r, tm // d, stride=d), :] = ref[r, :, cols]
            return buf_ref[slot]

        for ch in range(GRP_W // 128):
            cols = slice(ch * 128, (ch + 1) * 128)
            o = [token_order(ref, d, cols, k) for k, (ref, d) in enumerate(zip((o0_ref, o1_ref, o2_ref), DILATIONS))]
            ls = [token_order(ref, d, cols, 3 + k)
                  for k, (ref, d) in enumerate(zip((l0_ref, l1_ref, l2_ref), DILATIONS))]
            m = jnp.maximum(jnp.maximum(ls[0], ls[1]), ls[2])
            e = [jnp.exp(l - m) for l in ls]
            den = e[0] + e[1] + e[2]
            y = (e[0] * o[0] + e[1] * o[1] + e[2] * o[2]) / den
            y_ref[:, cols] = y
            yt_ref[cols, :] = y.T.astype(BF16)
            l_ref[:, cols] = m + jnp.log(den)

    blk = pl.BlockSpec((tm, GRP_W), lambda i: (i, 0))
    cls = [pl.BlockSpec((d, tm // d, GRP_W), lambda i: (0, i, 0)) for d in DILATIONS]
    shp = jax.ShapeDtypeStruct((S, GRP_W), F32)
    return pl.pallas_call(
        body, name=name, grid=(S // tm,), in_specs=cls + cls,
        out_specs=[blk, pl.BlockSpec((GRP_W, tm), lambda i: (0, i)), blk],
        out_shape=[shp, jax.ShapeDtypeStruct((GRP_W, S), BF16), shp],
        scratch_shapes=[pltpu.VMEM((6, tm, 128), F32)],
        compiler_params=_cp(("parallel",)),
    )(*os, *lses)


def _attn_bwd(gi, qkv_c, dy_c, y_c, lse_c, name):
    d = DILATIONS[gi]
    nblk = S // d // BLK

    def body(*refs):
        if nblk > 1:
            (q_ref, qn_ref, k_ref, kp_ref, v_ref, vp_ref, dy_ref, dyn_ref, y_ref, yn_ref, l_ref, ln_ref,
             dq_ref, dk_ref, dv_ref) = refs
            n = pl.program_id(1)
            has_prev = n != 0
            has_next = n != nblk - 1
        else:
            q_ref, k_ref, v_ref, dy_ref, y_ref, l_ref, dq_ref, dk_ref, dv_ref = refs
        qi = lax.broadcasted_iota(jnp.int32, (NH, BLK, BLK), 1)
        kj = lax.broadcasted_iota(jnp.int32, (NH, BLK, BLK), 2)

        def lse_col(ref):
            return jnp.stack([ref[:, h * DH:h * DH + 1] for h in range(NH)])

        q, k, v = _heads(q_ref), _heads(k_ref), _heads(v_ref)
        dy = _heads(dy_ref)
        dd = jnp.sum(dy * _heads(y_ref), axis=-1, keepdims=True)
        lcol = lse_col(l_ref)
        dyb = dy.astype(BF16)
        p = jnp.exp(jnp.where(kj <= qi, _bdot_nt(q, k) * 0.125, NEG_INF) - lcol)
        ds = (p * (_bdot_nt(dyb, v) - dd)).astype(BF16)
        dq = _bdot(ds, k)
        dk = _bdot_tn(ds, q)
        dv = _bdot_tn(p.astype(BF16), dyb)
        if nblk > 1:
            qn, kpv, vpv = _heads(qn_ref), _heads(kp_ref), _heads(vp_ref)
            dyn = _heads(dyn_ref)
            ddn = jnp.sum(dyn * _heads(yn_ref), axis=-1, keepdims=True)
            lncol = lse_col(ln_ref)
            dynb = dyn.astype(BF16)
            mask_p = jnp.logical_and(kj >= qi, has_prev)
            pp = jnp.exp(jnp.where(mask_p, _bdot_nt(q, kpv) * 0.125, NEG_INF) - lcol)
            dsp = (pp * (_bdot_nt(dyb, vpv) - dd)).astype(BF16)
            dq = dq + _bdot(dsp, kpv)
            mask_n = jnp.logical_and(kj >= qi, has_next)
            pn = jnp.exp(jnp.where(mask_n, _bdot_nt(qn, k) * 0.125, NEG_INF) - lncol)
            dsn = (pn * (_bdot_nt(dynb, v) - ddn)).astype(BF16)
            dk = dk + _bdot_tn(dsn, qn)
            dv = dv + _bdot_tn(pn.astype(BF16), dynb)
        dq_ref[...] = _unheads(dq) * 0.125
        dk_ref[...] = _unheads(dk) * 0.125
        dv_ref[...] = _unheads(dv)

    def spec(sec, shift):
        def idx(r, n):
            return (r, jnp.clip(n + shift, 0, nblk - 1), sec)
        return pl.BlockSpec((None, BLK, GRP_W), idx)

    if nblk > 1:
        in_specs = [spec(0, 0), spec(0, 1), spec(1, 0), spec(1, -1), spec(2, 0), spec(2, -1),
                    spec(0, 0), spec(0, 1), spec(0, 0), spec(0, 1), spec(0, 0), spec(0, 1)]
        args = (qkv_c,) * 6 + (dy_c, dy_c, y_c, y_c, lse_c, lse_c)
    else:
        in_specs = [spec(0, 0), spec(1, 0), spec(2, 0), spec(0, 0), spec(0, 0), spec(0, 0)]
        args = (qkv_c, qkv_c, qkv_c, dy_c, y_c, lse_c)
    out = spec(0, 0)
    shp = jax.ShapeDtypeStruct((d, S // d, GRP_W), F32)
    return pl.pallas_call(
        body, name=name, grid=(d, nblk), in_specs=in_specs, out_specs=[out, out, out], out_shape=[shp, shp, shp],
        compiler_params=_cp(("parallel", "parallel")),
    )(*args)


_SQRT_HALF = 0.7071067811865476
_INV_SQRT_2PI = 0.3989422804014327


def _gelu(z):
    return 0.5 * z * (1.0 + lax.erf(z * _SQRT_HALF))


def _gelu_grad(z):
    return 0.5 * (1.0 + lax.erf(z * _SQRT_HALF)) + z * (jnp.exp(-0.5 * z * z) * _INV_SQRT_2PI)


def _tril_mask():
    t = lax.broadcasted_iota(jnp.int32, (BLK, BLK), 0)
    s = lax.broadcasted_iota(jnp.int32, (BLK, BLK), 1)
    return s <= t


def _groups(t):
    return jnp.stack([t[:, g * BLK:(g + 1) * BLK] for g in range(8)])


def _ungroup(t3):
    return jnp.concatenate([t3[g] for g in range(8)], axis=1)


def _group_bias(bs_ref):
    return jnp.stack([bs_ref[:, g:g + 1] for g in range(8)])


def _gmlp_fwd(z, ln_g, ln_b, w_s, b_s_t, name):
    def body(z_ref, g_ref, b_ref, ws_ref, bs_ref, y_ref, yt_ref):
        zg = _gelu(z_ref[...])
        u = zg[:, :GW]
        xh, _ = _ln_stats(zg[:, GW:])
        vn = (xh * g_ref[...] + b_ref[...]).astype(BF16)
        wt = jnp.where(_tril_mask(), ws_ref[...], 0.0).astype(BF16)
        yv = u * _ungroup(_bdot(wt, _groups(vn)) + _group_bias(bs_ref))
        y_ref[...] = yv.astype(BF16)
        yt_ref[...] = yv.T.astype(BF16)

    vec = pl.BlockSpec((1, GW), lambda n: (0, 0))
    return pl.pallas_call(
        body, name=name, grid=(NBLK,),
        in_specs=[pl.BlockSpec((BLK, 2 * GW), lambda n: (n, 0)), vec, vec,
                  pl.BlockSpec((8, BLK, BLK), lambda n: (0, 0, 0)), pl.BlockSpec((BLK, 8), lambda n: (0, 0))],
        out_specs=[pl.BlockSpec((BLK, GW), lambda n: (n, 0)), pl.BlockSpec((GW, BLK), lambda n: (0, n))],
        out_shape=[jax.ShapeDtypeStruct((S, GW), BF16), jax.ShapeDtypeStruct((GW, S), BF16)],
        compiler_params=_cp(("parallel",)),
    )(z, ln_g, ln_b, w_s, b_s_t)


def _gmlp_bwd(z, dy, ln_g, ln_b, w_s, b_s_t, name):
    def body(z_ref, dy_ref, g_ref, b_ref, ws_ref, bs_ref, dz_ref, dws_ref, dbs_ref, dg_ref, db_ref, dvn_ref):
        n = pl.program_id(0)
        zv = z_ref[...]
        zg = _gelu(zv)
        u = zg[:, :GW]
        xh, rstd = _ln_stats(zg[:, GW:])
        vn = (xh * g_ref[...] + b_ref[...]).astype(BF16)
        tril = _tril_mask()

        @pl.when(n == 0)
        def _():
            dws_ref[...] = jnp.zeros_like(dws_ref)
            dbs_ref[...] = jnp.zeros_like(dbs_ref)
            dg_ref[...] = jnp.zeros_like(dg_ref)
            db_ref[...] = jnp.zeros_like(db_ref)

        wt = jnp.where(tril, ws_ref[...], 0.0).astype(BF16)
        vn3 = _groups(vn)
        dyv = dy_ref[...]
        mixed = _ungroup(_bdot(wt, vn3) + _group_bias(bs_ref))
        dz_ref[:, :GW] = (dyv * mixed * _gelu_grad(zv[:, :GW])).astype(BF16)
        dmix3 = _groups(dyv * u)
        dmb = dmix3.astype(BF16)
        dws_ref[...] += jnp.where(tril, _bdot_nt(dmb, vn3), 0.0)
        dbsum = jnp.sum(dmix3, axis=-1, keepdims=True)
        for gg in range(8):
            dbs_ref[:, gg:gg + 1] += dbsum[gg]
        dvn_ref[...] = _ungroup(_bdot_tn(wt, dmb))

        dvn = dvn_ref[...]
        dg_ref[...] += jnp.sum(dvn * xh, axis=0, keepdims=True)
        db_ref[...] += jnp.sum(dvn, axis=0, keepdims=True)
        dvg = _ln_dx(dvn * g_ref[...], xh, rstd)
        dz_ref[:, GW:] = (dvg * _gelu_grad(zv[:, GW:])).astype(BF16)

    vec = pl.BlockSpec((1, GW), lambda n: (0, 0))
    ws = pl.BlockSpec((8, BLK, BLK), lambda n: (0, 0, 0))
    bs = pl.BlockSpec((BLK, 8), lambda n: (0, 0))
    return pl.pallas_call(
        body, name=name, grid=(NBLK,),
        in_specs=[pl.BlockSpec((BLK, 2 * GW), lambda n: (n, 0)), pl.BlockSpec((BLK, GW), lambda n: (n, 0)),
                  vec, vec, ws, bs],
        out_specs=[pl.BlockSpec((BLK, 2 * GW), lambda n: (n, 0)), ws, bs, vec, vec],
        out_shape=[jax.ShapeDtypeStruct((S, 2 * GW), BF16), jax.ShapeDtypeStruct((8, BLK, BLK), F32),
                   jax.ShapeDtypeStruct((BLK, 8), F32), jax.ShapeDtypeStruct((1, GW), F32),
                   jax.ShapeDtypeStruct((1, GW), F32)],
        scratch_shapes=[pltpu.VMEM((BLK, GW), F32)],
        compiler_params=_cp(("arbitrary",)),
    )(z, dy, ln_g, ln_b, w_s, b_s_t)


def _merge_fwd(a, b, gl, b_gates, name):
    tm = 512

    def body(a_ref, b_ref, g0_ref, g1_ref, bg_ref, o_ref, ot_ref):
        g0 = jax.nn.sigmoid(g0_ref[...] + bg_ref[:, :D])
        g1 = jax.nn.sigmoid(g1_ref[...] + bg_ref[:, D:])
        mg = g0 * a_ref[...] + g1 * b_ref[...]
        o_ref[...] = mg.astype(BF16)
        ot_ref[...] = mg.T.astype(BF16)

    row = pl.BlockSpec((tm, D), lambda i: (i, 0))
    return pl.pallas_call(
        body, name=name, grid=(S // tm,),
        in_specs=[row, row, row, pl.BlockSpec((tm, D), lambda i: (i, 1)), pl.BlockSpec((1, 2 * D), lambda i: (0, 0))],
        out_specs=[row, pl.BlockSpec((D, tm), lambda i: (0, i))],
        out_shape=[jax.ShapeDtypeStruct((S, D), BF16), jax.ShapeDtypeStruct((D, S), BF16)],
        compiler_params=_cp(("parallel",)),
    )(a, b, gl, gl, b_gates)


def _merge_bwd(dm, a, b, gl, b_gates, name):
    tm = 512

    def body(dm_ref, a_ref, b_ref, g0_ref, g1_ref, bg_ref, da_ref, db_ref, dgl_ref, dbg_ref):
        i = pl.program_id(0)
        dmv = dm_ref[...]
        g0 = jax.nn.sigmoid(g0_ref[...] + bg_ref[:, :D])
        g1 = jax.nn.sigmoid(g1_ref[...] + bg_ref[:, D:])
        da_ref[...] = (dmv * g0).astype(BF16)
        db_ref[...] = (dmv * g1).astype(BF16)
        d0 = dmv * a_ref[...] * g0 * (1.0 - g0)
        d1 = dmv * b_ref[...] * g1 * (1.0 - g1)
        dgl_ref[:, :D] = d0.astype(BF16)
        dgl_ref[:, D:] = d1.astype(BF16)
        s0 = jnp.sum(d0, axis=0, keepdims=True)
        s1 = jnp.sum(d1, axis=0, keepdims=True)

        @pl.when(i == 0)
        def _():
            dbg_ref[:, :D] = s0
            dbg_ref[:, D:] = s1

        @pl.when(i > 0)
        def _():
            dbg_ref[:, :D] += s0
            dbg_ref[:, D:] += s1

    row = pl.BlockSpec((tm, D), lambda i: (i, 0))
    wide = pl.BlockSpec((tm, 2 * D), lambda i: (i, 0))
    bg = pl.BlockSpec((1, 2 * D), lambda i: (0, 0))
    return pl.pallas_call(
        body, name=name, grid=(S // tm,),
        in_specs=[row, row, row, row, pl.BlockSpec((tm, D), lambda i: (i, 1)), bg],
        out_specs=[row, row, wide, bg],
        out_shape=[jax.ShapeDtypeStruct((S, D), BF16), jax.ShapeDtypeStruct((S, D), BF16),
                   jax.ShapeDtypeStruct((S, 2 * D), BF16), jax.ShapeDtypeStruct((1, 2 * D), F32)],
        compiler_params=_cp(("arbitrary",)),
    )(dm, a, b, gl, gl, b_gates)


def _adam_math(w, g, m, v):
    m2 = ADAM_B1 * m + (1.0 - ADAM_B1) * g
    v2 = ADAM_B2 * v + (1.0 - ADAM_B2) * (g * g)
    m_hat = m2 / (1.0 - ADAM_B1 ** ADAM_STEP)
    v_hat = v2 / (1.0 - ADAM_B2 ** ADAM_STEP)
    delta = -ADAM_LR * (m_hat / (jnp.sqrt(v_hat) + ADAM_EPS) + ADAM_WD * w)
    return delta, m2, v2


def _pick_rows(rows, cols, unit=16, budget=2 * MIB):
    best = unit
    for t in range(unit, rows + 1, unit):
        if rows % t == 0 and t * cols * 4 <= budget:
            best = t
    assert rows % best == 0
    return best


def _adamw(w, g, m, v, name):
    r, c = w.shape
    tr = _pick_rows(r, c, unit=8)

    def body(w_ref, g_ref, m_ref, v_ref, go_ref, d_ref, mo_ref, vo_ref):
        gv = g_ref[...]
        delta, m2, v2 = _adam_math(w_ref[...], gv, m_ref[...], v_ref[...])
        go_ref[...] = gv
        d_ref[...] = delta
        mo_ref[...] = m2
        vo_ref[...] = v2

    blk = pl.BlockSpec((tr, c), lambda i: (i, 0))
    shp = jax.ShapeDtypeStruct((r, c), F32)
    return pl.pallas_call(
        body, name=name, grid=(r // tr,), in_specs=[blk] * 4, out_specs=[blk] * 4, out_shape=[shp] * 4,
        compiler_params=_cp(("parallel",)),
    )(*[pltpu.with_memory_space_constraint(t, pltpu.HBM) for t in (w, g, m, v)])


def _small_sum_adamw(parts, own, pos, w, m, v, name):
    tr = 48

    def body(pos_ref, p_ref, own_ref, w_ref, m_ref, v_ref, g_ref, d_ref, mo_ref, vo_ref):
        me = 2 * pos_ref[1] + pos_ref[0]
        gv = None
        for k in range(8):
            term = jnp.where(me == k, own_ref[...], p_ref[k])
            gv = term if gv is None else gv + term
        delta, m2, v2 = _adam_math(w_ref[...], gv, m_ref[...], v_ref[...])
        g_ref[...] = gv
        d_ref[...] = delta
        mo_ref[...] = m2
        vo_ref[...] = v2

    blk = pl.BlockSpec((tr, D), lambda i, p: (i, 0))
    shp = jax.ShapeDtypeStruct((SMALL_ROWS, D), F32)
    return pl.pallas_call(
        body, name=name,
        grid_spec=pltpu.PrefetchScalarGridSpec(
            num_scalar_prefetch=1, grid=(SMALL_ROWS // tr,),
            in_specs=[pl.BlockSpec((8, tr, D), lambda i, p: (0, i, 0)), blk, blk, blk, blk],
            out_specs=[blk] * 4),
        out_shape=[shp] * 4,
        compiler_params=_cp(("parallel",)),
    )(pos, parts, own, w, m, v)


ANY = pl.BlockSpec(memory_space=pl.ANY)


def _in_hbm(arrays):
    return [pltpu.with_memory_space_constraint(a, pltpu.HBM) for a in arrays]


def _mesh_pos():
    x, y, c = lax.axis_index("x"), lax.axis_index("y"), lax.axis_index("c")
    chips = [(1 - x, y), (x, 1 - y), (1 - x, 1 - y)]
    return x, y, c, chips


def _place_shard(w, kind, pos, name):
    r, c = w.shape
    tr = _pick_rows(r, c)

    def body(pos_ref, w_ref, o_ref):
        o_ref[...] = w_ref[...].astype(BF16)

    if kind == "stack":
        o_spec = pl.BlockSpec((None, tr, c), lambda i, p: (p[1], i, 0))
        shape = (NSH, r, c)
    else:
        o_spec = pl.BlockSpec((tr, c), lambda i, p: (i, p[1]))
        shape = (r, NSH * c)
    return pl.pallas_call(
        body, name=name,
        grid_spec=pltpu.PrefetchScalarGridSpec(
            num_scalar_prefetch=1, grid=(r // tr,),
            in_specs=[pl.BlockSpec((tr, c), lambda i, p: (i, 0))], out_specs=o_spec),
        out_shape=pltpu.HBM(shape, BF16),
        compiler_params=_cp(("parallel",)),
    )(pos, pltpu.with_memory_space_constraint(w, pltpu.HBM))


SEM = pl.BlockSpec(memory_space=pltpu.SEMAPHORE)
SPLIT_COPY = pltpu.CompilerParams(has_side_effects=pltpu.SideEffectType.DATAFLOW_SIDE_EFFECTING)


def _shard_window(ref, kind, j, h, dims):
    r, c = dims
    rows = pl.ds(pl.multiple_of(h * (r // 2), 16), r // 2)
    if kind == "stack":
        return ref.at[j, rows, :]
    return ref.at[rows, pl.ds(pl.multiple_of(j * c, 128), c)]


def _ici_copy(ref, kind, dims, j, c, sems, idx, to):
    win = _shard_window(ref, kind, j, c, dims)
    return pltpu.make_async_remote_copy(src_ref=win, dst_ref=win, send_sem=sems[0].at[idx], recv_sem=sems[1].at[idx],
                                        device_id=to, device_id_type=MESH_T)


def _both_copy(ref, kind, dims, a, k, chip, half, tc, sc, sems):
    win = _shard_window(ref, kind, half[0], half[1], dims)
    return pltpu.make_async_remote_copy(src_ref=win, dst_ref=win, send_sem=sems[0].at[6 * a + 2 * k + tc],
                                        recv_sem=sems[1].at[6 * a + 2 * k + sc],
                                        device_id=(chip[0], chip[1], tc), device_id_type=MESH_T)


def _gather_start(fulls, kinds, dims, after, name, both=False):
    n, na = len(fulls), len(after)
    per = 6 if both else 3

    def body(*refs):
        outs = refs[n + na:2 * n + na]
        send_sems, recv_sems, token = refs[2 * n + na:]
        x, y, c, chips = _mesh_pos()
        for a in range(n):
            for k, chip in enumerate(chips):
                if both:
                    for tc in range(2):
                        _both_copy(outs[a], kinds[a], dims[a], a, k, chip, (2 * x + y, c), tc, c,
                                   (send_sems, recv_sems)).start()
                else:
                    _ici_copy(outs[a], kinds[a], dims[a], 2 * x + y, c, (send_sems, recv_sems), 3 * a + k,
                              (chip[0], chip[1], c)).start()
        token[...] = jnp.zeros_like(token)

    res = pl.pallas_call(
        body, name=name, in_specs=[ANY] * (n + na),
        out_specs=[ANY] * n + [SEM, SEM, pl.BlockSpec(memory_space=pltpu.VMEM)],
        out_shape=[pltpu.HBM(f.shape, BF16) for f in fulls]
        + [pltpu.SemaphoreType.DMA((per * n,)), pltpu.SemaphoreType.DMA((per * n,)),
           jax.ShapeDtypeStruct((8, 128), F32)],
        input_output_aliases={i: i for i in range(n)},
        compiler_params=SPLIT_COPY,
    )(*_in_hbm(fulls), *after)
    return res[:n], res[n], res[n + 1], res[n + 2]


def _gather_wait(fulls, send_sems, recv_sems, kinds, dims, after, name, both=False):
    n, na = len(fulls), len(after)

    def body(*refs):
        ssem, rsem = refs[n], refs[n + 1]
        outs = refs[n + 2 + na:]
        x, y, c, chips = _mesh_pos()
        for a in range(n):
            for k, chip in enumerate(chips):
                if both:
                    for oc in range(2):
                        _both_copy(outs[a], kinds[a], dims[a], a, k, chip, (2 * x + y, c), oc, c,
                                   (ssem, rsem)).wait_send()
                        _both_copy(outs[a], kinds[a], dims[a], a, k, chip, (2 * chip[0] + chip[1], oc), c, oc,
                                   (ssem, rsem)).wait_recv()
                    continue
                to = (chip[0], chip[1], c)
                _ici_copy(outs[a], kinds[a], dims[a], 2 * x + y, c, (ssem, rsem), 3 * a + k, to).wait_send()
                _ici_copy(outs[a], kinds[a], dims[a], 2 * chip[0] + chip[1], c, (ssem, rsem), 3 * a + k, to).wait_recv()

    return pl.pallas_call(
        body, name=name, in_specs=[ANY] * n + [SEM, SEM] + [ANY] * na, out_specs=[ANY] * n,
        out_shape=[pltpu.HBM(f.shape, BF16) for f in fulls],
        input_output_aliases={i: i for i in range(n)},
        compiler_params=SPLIT_COPY,
    )(*_in_hbm(fulls), send_sems, recv_sems, *after)


def _gather_forward(fulls, kinds, dims, name):
    n = len(fulls)

    def body(*refs):
        outs = refs[n:2 * n]
        sems = refs[2 * n:]
        x, y, c, chips = _mesh_pos()
        sib = (x, y, 1 - c)
        cps = []
        for a in range(n):
            for k, chip in enumerate(chips):
                cp = _ici_copy(outs[a], kinds[a], dims[a], 2 * chip[0] + chip[1], c, sems, 3 * a + k, sib)
                cp.start()
                cps.append(cp)
        for a in range(n):
            for k, chip in enumerate(chips):
                _ici_copy(outs[a], kinds[a], dims[a], 2 * chip[0] + chip[1], 1 - c, sems, 3 * a + k, sib).wait_recv()
        for cp in cps:
            cp.wait_send()

    return pl.pallas_call(
        body, name=name, in_specs=[ANY] * n, out_specs=[ANY] * n,
        out_shape=[pltpu.HBM(f.shape, BF16) for f in fulls],
        input_output_aliases={i: i for i in range(n)},
        scratch_shapes=[pltpu.SemaphoreType.DMA((3 * n,)), pltpu.SemaphoreType.DMA((3 * n,))],
    )(*_in_hbm(fulls))


def _pair_copy(src, land, a, x, y, c, sems):
    return pltpu.make_async_remote_copy(
        src_ref=src.at[1 - c], dst_ref=land, send_sem=sems[0].at[a], recv_sem=sems[1].at[a],
        device_id=(x, y, 1 - c), device_id_type=MESH_T)


def _pair_start(grads, lands, name):
    n = len(grads)

    def body(*refs):
        srcs, dsts = refs[2 * n:3 * n], refs[3 * n:4 * n]
        send_sems, recv_sems, token = refs[4 * n:]
        x, y, c, _ = _mesh_pos()
        for a in range(n):
            _pair_copy(srcs[a], dsts[a], a, x, y, c, (send_sems, recv_sems)).start()
        token[...] = jnp.zeros_like(token)

    res = pl.pallas_call(
        body, name=name, in_specs=[ANY] * (2 * n),
        out_specs=[ANY] * (2 * n) + [SEM, SEM, pl.BlockSpec(memory_space=pltpu.VMEM)],
        out_shape=[pltpu.HBM(g.shape, F32) for g in grads]
        + [pltpu.HBM(l.shape, F32) for l in lands]
        + [pltpu.SemaphoreType.DMA((n,)), pltpu.SemaphoreType.DMA((n,)), jax.ShapeDtypeStruct((8, 128), F32)],
        input_output_aliases={i: i for i in range(2 * n)},
        compiler_params=SPLIT_COPY,
    )(*_in_hbm(grads), *_in_hbm(lands))
    return res[:n], res[n:2 * n], res[2 * n], res[2 * n + 1], res[2 * n + 2]


def _pair_wait(grads, lands, send_sems, recv_sems, after, name):
    n, na = len(grads), len(after)

    def body(*refs):
        ssem, rsem = refs[2 * n], refs[2 * n + 1]
        outs = refs[2 * n + 2 + na:]
        x, y, c, _ = _mesh_pos()
        for a in range(n):
            cp = _pair_copy(outs[a], outs[n + a], a, x, y, c, (ssem, rsem))
            cp.wait_send()
            cp.wait_recv()

    res = pl.pallas_call(
        body, name=name, in_specs=[ANY] * (2 * n) + [SEM, SEM] + [ANY] * na, out_specs=[ANY] * (2 * n),
        out_shape=[pltpu.HBM(g.shape, F32) for g in grads]
        + [pltpu.HBM(l.shape, F32) for l in lands],
        input_output_aliases={i: i for i in range(2 * n)},
        compiler_params=SPLIT_COPY,
    )(*_in_hbm(grads), *_in_hbm(lands), send_sems, recv_sems, *after)
    return res[:n], res[n:]


def _pair_sum(g, recv, pos, name):
    _, _, rh, c = g.shape
    tr = _pick_rows(rh, c)

    def body(pos_ref, g_ref, r_ref, o_ref):
        o_ref[...] = (g_ref[...] + r_ref[...]).astype(BF16)

    return pl.pallas_call(
        body, name=name,
        grid_spec=pltpu.PrefetchScalarGridSpec(
            num_scalar_prefetch=1, grid=(3, rh // tr),
            in_specs=[pl.BlockSpec((None, None, tr, c), lambda k, r, p: (p[0], p[2 + k], r, 0)),
                      pl.BlockSpec((None, tr, c), lambda k, r, p: (p[2 + k], r, 0))],
            out_specs=pl.BlockSpec((None, tr, c), lambda k, r, p: (k, r, 0))),
        out_shape=pltpu.HBM((3, rh, c), BF16),
        compiler_params=_cp(("parallel", "parallel")),
    )(pos, *_in_hbm([g, recv]))


def _chip_copy(src, land, a, k, chip, c, sems):
    return pltpu.make_async_remote_copy(
        src_ref=src.at[k], dst_ref=land.at[k], send_sem=sems[0].at[3 * a + k],
        recv_sem=sems[1].at[3 * a + k], device_id=(chip[0], chip[1], c), device_id_type=MESH_T)


def _chip_start(psums, lands, name):
    n = len(psums)

    def body(*refs):
        srcs, dsts = refs[2 * n:3 * n], refs[3 * n:4 * n]
        send_sems, recv_sems, token = refs[4 * n:]
        x, y, c, chips = _mesh_pos()
        for a in range(n):
            for k, chip in enumerate(chips):
                _chip_copy(srcs[a], dsts[a], a, k, chip, c, (send_sems, recv_sems)).start()
        token[...] = jnp.zeros_like(token)

    res = pl.pallas_call(
        body, name=name, in_specs=[ANY] * (2 * n),
        out_specs=[ANY] * (2 * n) + [SEM, SEM, pl.BlockSpec(memory_space=pltpu.VMEM)],
        out_shape=[pltpu.HBM(p.shape, BF16) for p in psums]
        + [pltpu.HBM(l.shape, BF16) for l in lands]
        + [pltpu.SemaphoreType.DMA((3 * n,)), pltpu.SemaphoreType.DMA((3 * n,)), jax.ShapeDtypeStruct((8, 128), F32)],
        input_output_aliases={i: i for i in range(2 * n)},
        compiler_params=SPLIT_COPY,
    )(*_in_hbm(psums), *_in_hbm(lands))
    return res[:n], res[n:2 * n], res[2 * n], res[2 * n + 1], res[2 * n + 2]


def _chip_wait(psums, lands, send_sems, recv_sems, after, name):
    n, na = len(psums), len(after)

    def body(*refs):
        ssem, rsem = refs[2 * n], refs[2 * n + 1]
        outs = refs[2 * n + 2 + na:]
        srcs, dsts = outs[:n], outs[n:]
        x, y, c, chips = _mesh_pos()
        for a in range(n):
            for k, chip in enumerate(chips):
                cp = _chip_copy(srcs[a], dsts[a], a, k, chip, c, (ssem, rsem))
                cp.wait_send()
                cp.wait_recv()

    res = pl.pallas_call(
        body, name=name, in_specs=[ANY] * (2 * n) + [SEM, SEM] + [ANY] * na, out_specs=[ANY] * (2 * n),
        out_shape=[pltpu.HBM(p.shape, BF16) for p in psums]
        + [pltpu.HBM(l.shape, BF16) for l in lands],
        input_output_aliases={i: i for i in range(2 * n)},
        compiler_params=SPLIT_COPY,
    )(*_in_hbm(psums), *_in_hbm(lands), send_sems, recv_sems, *after)
    return res[n:]


def _owner_sum(g, recv_a, recv_b, pos, name):
    _, _, rh, c = g.shape
    tr = _pick_rows(rh, c)

    def body(pos_ref, g_ref, ra_ref, rb_ref, o_ref):
        acc = g_ref[...] + ra_ref[...]
        for k in range(3):
            acc = acc + rb_ref[k].astype(F32)
        o_ref[...] = acc

    return pl.pallas_call(
        body, name=name,
        grid_spec=pltpu.PrefetchScalarGridSpec(
            num_scalar_prefetch=1, grid=(rh // tr,),
            in_specs=[pl.BlockSpec((None, None, tr, c), lambda r, p: (p[0], p[1], r, 0)),
                      pl.BlockSpec((None, tr, c), lambda r, p: (p[1], r, 0)),
                      pl.BlockSpec((3, tr, c), lambda r, p: (0, r, 0))],
            out_specs=pl.BlockSpec((None, tr, c), lambda r, p: (p[0], r, 0))),
        out_shape=pltpu.HBM((2, rh, c), F32),
        compiler_params=_cp(("parallel",)),
    )(pos, *_in_hbm([g, recv_a, recv_b]))


def _sibling_allgather(halves, name):
    n = len(halves)

    def body(*refs):
        outs = refs[n:2 * n]
        send_sems, recv_sems = refs[2 * n:]
        x, y, c, _ = _mesh_pos()
        cps = []
        for a in range(n):
            cp = pltpu.make_async_remote_copy(
                src_ref=outs[a].at[c], dst_ref=outs[a].at[c], send_sem=send_sems.at[a], recv_sem=recv_sems.at[a],
                device_id=(x, y, 1 - c), device_id_type=MESH_T)
            cp.start()
            cps.append(cp)
        for a in range(n):
            cps[a].wait_send()
            pltpu.make_async_remote_copy(
                src_ref=outs[a].at[1 - c], dst_ref=outs[a].at[1 - c], send_sem=send_sems.at[a],
                recv_sem=recv_sems.at[a], device_id=(x, y, 1 - c), device_id_type=MESH_T).wait_recv()

    return pl.pallas_call(
        body, name=name, in_specs=[ANY] * n, out_specs=[ANY] * n,
        out_shape=[pltpu.HBM(h.shape, F32) for h in halves],
        input_output_aliases={i: i for i in range(n)},
        scratch_shapes=[pltpu.SemaphoreType.DMA((n,)), pltpu.SemaphoreType.DMA((n,))],
    )(*_in_hbm(halves))


def _peers(x, y, c):
    rel = [(0, 0, 1), (0, 1, 0), (0, 1, 1), (1, 0, 0), (1, 0, 1), (1, 1, 0), (1, 1, 1)]
    return [((1 - x) if dx else x, (1 - y) if dy else y, (1 - c) if dc else c) for dx, dy, dc in rel]


def _small_copy(src, land, k, peer, slot, sems):
    return pltpu.make_async_remote_copy(src_ref=src, dst_ref=land.at[slot], send_sem=sems[0].at[k],
                                        recv_sem=sems[1].at[k], device_id=peer, device_id_type=MESH_T)


def _small_start(part, land, name):
    def body(p_in, l_in, p_ref, l_ref, send_sems, recv_sems, token):
        x, y, c, _ = _mesh_pos()
        for k, peer in enumerate(_peers(x, y, c)):
            _small_copy(p_ref, l_ref, k, peer, 4 * x + 2 * y + c, (send_sems, recv_sems)).start()
        token[...] = jnp.zeros_like(token)

    return pl.pallas_call(
        body, name=name, in_specs=[ANY, ANY],
        out_specs=[ANY, ANY, SEM, SEM, pl.BlockSpec(memory_space=pltpu.VMEM)],
        out_shape=[pltpu.HBM(part.shape, F32), pltpu.HBM(land.shape, F32), pltpu.SemaphoreType.DMA((7,)),
                   pltpu.SemaphoreType.DMA((7,)), jax.ShapeDtypeStruct((8, 128), F32)],
        input_output_aliases={0: 0, 1: 1},
        compiler_params=SPLIT_COPY,
    )(*_in_hbm([part, land]))


def _small_wait(part, land, send_sems, recv_sems, after, name):
    na = len(after)

    def body(*refs):
        ssem, rsem = refs[2], refs[3]
        p_ref, l_ref = refs[4 + na:]
        x, y, c, _ = _mesh_pos()
        for k, peer in enumerate(_peers(x, y, c)):
            cp = _small_copy(p_ref, l_ref, k, peer, 4 * peer[0] + 2 * peer[1] + peer[2], (ssem, rsem))
            cp.wait_send()
            cp.wait_recv()

    return pl.pallas_call(
        body, name=name, in_specs=[ANY, ANY, SEM, SEM] + [ANY] * na, out_specs=[ANY, ANY],
        out_shape=[pltpu.HBM(part.shape, F32), pltpu.HBM(land.shape, F32)],
        input_output_aliases={0: 0, 1: 1},
        compiler_params=SPLIT_COPY,
    )(*_in_hbm([part, land]), send_sems, recv_sems, *after)


def _pack_small(ln1_g, ln1_b, gln_g, gln_b, ln2_g, ln2_b, ln3_g, ln3_b, b_gates, b_s, w_s):
    rows = [ln1_g, ln1_b, gln_g, gln_b, ln2_g, ln2_b, ln3_g, ln3_b]
    rows = [r.reshape(1, D) for r in rows] + [b_gates.reshape(2, D), b_s.reshape(1, D), jnp.zeros((5, D), F32),
                                             w_s.reshape(128, D)]
    return jnp.concatenate(rows, axis=0)


def _unpack_small(p):
    out = [p[i:i + 1] for i in range(8)]
    return out + [p[8:10].reshape(1, 2 * D), p[10:11].reshape(1, 8, BLK), p[16:144].reshape(1, 8, BLK, BLK)]


GROUPS = (("f1g", "f1u", "f1d"), ("w_in",), ("w_ab", "w_gb", "w_out"), ("f2g", "f2u", "f2d"))
LATE_GROUPS = (2, 3)


def _local_step(x, pos_f, target, P, weights_of, grads_ready, flush, small_ready):
    invf = ROPE_THETA ** (-jnp.arange(0, DH, 2, dtype=F32) / DH)
    invf = jnp.tile(invf, 4).reshape(1, 128)
    b_s_t = P["gmlp_b_s"].T

    W = dict(weights_of(0, []))
    h1b, xh1, rstd1, a1, b1, h1t = _ffn_fwd(x, W["f1g"], W["f1u"], W["f1d"], P["ln1_g"], P["ln1_b"], "ffn1_fwd",
                                                emit_t=True)
    W.update(weights_of(1, [h1b]))
    qkv_c = _proj_qkv_rope(h1b, W["w_in"], pos_f, invf, "proj_qkv_rope")
    z = _matmul(h1b, W["w_in"], "nn", "proj_z", n=2 * GW, b_col0=3 * ATT_W, tm=S, tn=512)
    gl = _matmul(h1b, W["w_in"], "nn", "proj_gates", n=2 * D, b_col0=3 * ATT_W + 2 * GW, tm=S, tn=512)
    og = [_attn_fwd(gi, qkv_c[gi], "attn_fwd_g%d" % gi) for gi in range(NG)]
    y_attn, y_attn_t, lse = _attn_combine([o for o, _ in og], [l for _, l in og], "attn_combine")
    y_gmlp, y_gmlp_t = _gmlp_fwd(z, P["gmlp_ln_g"], P["gmlp_ln_b"], P["gmlp_w_s"], b_s_t, "gmlp_fwd")
    W.update(weights_of(2, [y_gmlp]))
    br_a = _matmul(y_attn, W["w_ab"], "nn", "branch_attn", n=D, tm=1024, tn=D)
    br_b = _matmul(y_gmlp, W["w_gb"], "nn", "branch_gmlp", n=D, tm=1024, tn=D)
    merged, merged_t = _merge_fwd(br_a, br_b, gl, P["b_gates"], "merge_fwd")
    h2, h2b, xh2, rstd2 = _resid_ln(xh1, P["ln1_g"], P["ln1_b"], merged, W["w_out"], P["ln2_g"], P["ln2_b"],
                                    "mix_resid_ln2")
    W.update(weights_of(3, [h2b]))
    dr3, a2, b2, dg3, db3, loss = _ffn_fwd(h2, W["f2g"], W["f2u"], W["f2d"], P["ln3_g"], P["ln3_b"],
                                           "ffn2_fwd_loss", target=target)

    g_f2g, g_f2u, g_f2d, dh2 = _ffn_bwd(dr3, h2b, a2, b2, W["f2g"], W["f2u"], W["f2d"], "ffn2_bwd")
    tok = grads_ready(3, dict(f2g=g_f2g, f2u=g_f2u, f2d=g_f2d))
    dr2, dg2, db2 = _ln_bwd(dh2, xh2, rstd2, P["ln2_g"], "ln2_bwd", after=tok)
    g_wout = _wgrad(merged_t, dr2, 128, D, "dw_out", row_sharded=True)
    dmerged = _matmul(dr2, W["w_out"], "nt", "dmerged", n=D, tm=1024, tn=D)
    dab, dbb, dglb, dbg = _merge_bwd(dmerged, br_a, br_b, gl, P["b_gates"], "merge_bwd")
    tok = flush([dab])
    g_wab = _wgrad(y_attn_t, dab, GRP_W // 2, 256, "dw_attn_branch", row_sharded=False, after=tok)
    g_wgb = _wgrad(y_gmlp_t, dbb, 128, D, "dw_gmlp_branch", row_sharded=True)
    tok = grads_ready(2, dict(w_ab=g_wab, w_gb=g_wgb, w_out=g_wout))
    dy_attn = _matmul(dab, W["w_ab"], "nt", "dy_attn", n=GRP_W, tm=1024, tn=GRP_W, after=tok)
    dy_gmlp = _matmul(dbb, W["w_gb"], "nt", "dy_gmlp", n=GW, tm=1024, tn=GW)
    dzb, dws, dbs_t, dgln_g, dgln_b = _gmlp_bwd(z, dy_gmlp, P["gmlp_ln_g"], P["gmlp_ln_b"], P["gmlp_w_s"], b_s_t,
                                                 "gmlp_bwd")
    cls = _class_order([dy_attn, y_attn, lse], "attn_class_order")
    dqkv_c = []
    for gi in range(NG):
        dy_c, y_c, lse_c = [t[None] if gi == 0 else cls[2 * a + gi - 1] for a, t in enumerate((dy_attn, y_attn, lse))]
        dqkv_c.append(_attn_bwd(gi, qkv_c[gi], dy_c, y_c, lse_c, "attn_bwd_g%d" % gi))
    dqkvb = _rope_bwd(dqkv_c, pos_f, invf, "rope_bwd")
    dproj = jnp.concatenate([dqkvb, dzb, dglb], axis=1)
    tok = flush([dproj])
    g_win = _wgrad(h1t, dproj, D // 2, IN_SH, "dw_in", row_sharded=False, after=tok)
    tok = grads_ready(1, dict(w_in=g_win))
    dr1, dg1, db1 = _dh1_ln_bwd(dproj, W["w_in"], dr2, xh1, rstd1, P["ln1_g"], "dh1_ln1_bwd", after=tok)
    tok = flush([dr1])
    tok = tok + small_ready(_pack_small(dg1, db1, dgln_g, dgln_b, dg2, db2, dg3, db3, dbg, dbs_t.T, dws))
    g_f1g, g_f1u, g_f1d, dx = _ffn_bwd(dr1, x.astype(BF16), a1, b1, W["f1g"], W["f1u"], W["f1d"], "ffn1_bwd",
                                       after=tok)
    tok = grads_ready(0, dict(f1g=g_f1g, f1u=g_f1u, f1d=g_f1d))
    return loss, dx, tok


TRANSPOSED = ("f1g", "f1u", "f2g", "f2u")
KIND = dict(f1g="stack", f1u="stack", f1d="stack", w_in="col", w_ab="col", w_gb="stack", w_out="stack",
            f2g="stack", f2u="stack", f2d="stack")


def kernel(x, positions, ffn1_w_gate, ffn1_w_up, ffn1_w_down, ln1_g, ln1_b, w_in, b_gates, gmlp_ln_g, gmlp_ln_b, gmlp_w_s, gmlp_b_s, w_attn_branch, w_gmlp_branch, w_out, ln2_g, ln2_b, ffn2_w_gate, ffn2_w_up, ffn2_w_down, ln3_g, ln3_b, loss_target, m_ffn1_w_gate, m_ffn1_w_up, m_ffn1_w_down, m_ln1_g, m_ln1_b, m_w_in, m_b_gates, m_gmlp_ln_g, m_gmlp_ln_b, m_gmlp_w_s, m_gmlp_b_s, m_w_attn_branch, m_w_gmlp_branch, m_w_out, m_ln2_g, m_ln2_b, m_ffn2_w_gate, m_ffn2_w_up, m_ffn2_w_down, m_ln3_g, m_ln3_b, v_ffn1_w_gate, v_ffn1_w_up, v_ffn1_w_down, v_ln1_g, v_ln1_b, v_w_in, v_b_gates, v_gmlp_ln_g, v_gmlp_ln_b, v_gmlp_w_s, v_gmlp_b_s, v_w_attn_branch, v_w_gmlp_branch, v_w_out, v_ln2_g, v_ln2_b, v_ffn2_w_gate, v_ffn2_w_up, v_ffn2_w_down, v_ln3_g, v_ln3_b):
    cx, cy, cc = lax.axis_index("x"), lax.axis_index("y"), lax.axis_index("c")
    pos = jnp.stack([cc, 2 * cx + cy, 2 * (1 - cx) + cy, 2 * cx + 1 - cy, 2 * (1 - cx) + 1 - cy]).astype(jnp.int32)

    w_sh = dict(f1g=ffn1_w_gate, f1u=ffn1_w_up, f1d=ffn1_w_down, w_in=w_in, w_ab=w_attn_branch,
                w_gb=w_gmlp_branch, w_out=w_out, f2g=ffn2_w_gate, f2u=ffn2_w_up, f2d=ffn2_w_down)
    m_sh = dict(f1g=m_ffn1_w_gate, f1u=m_ffn1_w_up, f1d=m_ffn1_w_down, w_in=m_w_in, w_ab=m_w_attn_branch,
                w_gb=m_w_gmlp_branch, w_out=m_w_out, f2g=m_ffn2_w_gate, f2u=m_ffn2_w_up, f2d=m_ffn2_w_down)
    v_sh = dict(f1g=v_ffn1_w_gate, f1u=v_ffn1_w_up, f1d=v_ffn1_w_down, w_in=v_w_in, w_ab=v_w_attn_branch,
                w_gb=v_w_gmlp_branch, w_out=v_w_out, f2g=v_ffn2_w_gate, f2u=v_ffn2_w_up, f2d=v_ffn2_w_down)
    w_sh = {k: (v[0].T if k in TRANSPOSED else v[0]) for k, v in w_sh.items()}
    m_sh = {k: (v[0].T if k in TRANSPOSED else v[0]) for k, v in m_sh.items()}
    v_sh = {k: (v[0].T if k in TRANSPOSED else v[0]) for k, v in v_sh.items()}

    started, tokens = [], []
    for gi, names in enumerate(GROUPS):
        placed = [_place_shard(w_sh[k], KIND[k], pos, "place_" + k) for k in names]
        fulls, ssem, rsem, token = _gather_start(placed, [KIND[k] for k in names], [w_sh[k].shape for k in names],
                                                 tokens[-1:], "gather_start_g%d" % gi, both=gi in LATE_GROUPS)
        started.append((fulls, ssem, rsem))
        tokens.append(token)

    def weights_of(gi, after):
        names = GROUPS[gi]
        kinds, dims = [KIND[k] for k in names], [w_sh[k].shape for k in names]
        fulls, ssem, rsem = started[gi]
        fulls = _gather_wait(fulls, ssem, rsem, kinds, dims, list(after) + (tokens if gi == 0 else []),
                             "gather_wait_g%d" % gi, both=gi in LATE_GROUPS)
        if gi not in LATE_GROUPS:
            fulls = _gather_forward(fulls, kinds, dims, "gather_forward_g%d" % gi)
        return {k: (f.reshape(D, D) if k in ("w_gb", "w_out") else f) for k, f in zip(names, fulls)}

    pending, inflight = [], {}

    def grads_ready(gi, gd):
        grads = [gd[k] for k in GROUPS[gi]]
        lands = [lax.empty(g.shape[1:], F32) for g in grads]
        grads, lands, ssem, rsem, token = _pair_start(grads, lands, "rs_pair_start_g%d" % gi)
        pending.append((gi, grads, lands, ssem, rsem))
        return [token]

    def flush(after):
        gi, grads, lands, ssem, rsem = pending.pop()
        names = GROUPS[gi]
        grads, recv_a = _pair_wait(grads, lands, ssem, rsem, after, "rs_pair_wait_g%d" % gi)
        psums = [_pair_sum(g, r, pos, "rs_pair_sum_" + k) for g, r, k in zip(grads, recv_a, names)]
        lands = [lax.empty((3,) + p.shape[1:], BF16) for p in psums]
        psums, lands, ssem, rsem, token = _chip_start(psums, lands, "rs_chip_start_g%d" % gi)
        inflight[gi] = (grads, recv_a, psums, lands, ssem, rsem, token)
        return [token]

    P = dict(ln1_g=ln1_g, ln1_b=ln1_b, ln2_g=ln2_g, ln2_b=ln2_b, ln3_g=ln3_g, ln3_b=ln3_b, b_gates=b_gates,
             gmlp_ln_g=gmlp_ln_g, gmlp_ln_b=gmlp_ln_b, gmlp_w_s=gmlp_w_s[0], gmlp_b_s=gmlp_b_s[0])
    pos_f = positions.reshape(S, 1).astype(F32)
    small_state = []

    def small_ready(packed):
        land = jnp.zeros((8, SMALL_ROWS, D), F32)
        packed, land, ssem, rsem, token = _small_start(packed, land, "small_start")
        small_state.append((packed, land, ssem, rsem))
        return [token]

    loss_part, dx, tok_last = _local_step(x[0], pos_f, loss_target[0], P, weights_of, grads_ready, flush,
                                          small_ready)
    loss = lax.psum(loss_part[0, 0], ("x", "y", "c"))

    g_out, d_out, m_out, v_out = {}, {}, {}, {}

    def finish(gis, after, tag):
        names, halves = [], []
        for gi in gis:
            grads, recv_a, psums, lands, ssem, rsem, token = inflight[gi]
            recv_b = _chip_wait(psums, lands, ssem, rsem, after + [inflight[0][6]], "rs_chip_wait_g%d" % gi)
            halves += [_owner_sum(g, ra, rb, pos, "rs_owner_sum_" + k)
                       for g, ra, rb, k in zip(grads, recv_a, recv_b, GROUPS[gi])]
            names += GROUPS[gi]
            after = halves[-1:]
        reduced = _sibling_allgather(halves, "rs_sibling_allgather_" + tag)
        for k, gfull in zip(names, reduced):
            res = _adamw(w_sh[k], gfull.reshape(w_sh[k].shape), m_sh[k], v_sh[k], "adamw_" + k)
            after = [res[1]]
            if k in TRANSPOSED:
                res = [r.T for r in res]
            g_out[k], d_out[k], m_out[k], v_out[k] = [r[None] for r in res]
        return after

    small, parts = _small_wait(*small_state[0], tok_last, "small_wait")
    sp = (ln1_g, ln1_b, gmlp_ln_g, gmlp_ln_b, ln2_g, ln2_b, ln3_g, ln3_b, b_gates, gmlp_b_s, gmlp_w_s)
    sm = (m_ln1_g, m_ln1_b, m_gmlp_ln_g, m_gmlp_ln_b, m_ln2_g, m_ln2_b, m_ln3_g, m_ln3_b, m_b_gates, m_gmlp_b_s,
          m_gmlp_w_s)
    sv = (v_ln1_g, v_ln1_b, v_gmlp_ln_g, v_gmlp_ln_b, v_ln2_g, v_ln2_b, v_ln3_g, v_ln3_b, v_b_gates, v_gmlp_b_s,
          v_gmlp_w_s)
    sg, sd, smn, svn = _small_sum_adamw(parts, small, pos, _pack_small(*sp), _pack_small(*sm), _pack_small(*sv),
                                        "small_adamw")
    names = ("ln1_g", "ln1_b", "gmlp_ln_g", "gmlp_ln_b", "ln2_g", "ln2_b", "ln3_g", "ln3_b", "b_gates", "gmlp_b_s",
             "gmlp_w_s")
    for dst, packed in ((g_out, sg), (d_out, sd), (m_out, smn), (v_out, svn)):
        for nm, val in zip(names, _unpack_small(packed)):
            dst[nm] = val
    flush([sg])
    after = finish((3, 2, 1), [], "g321")
    finish((0,), after, "g0")

    order = ("f1g", "f1u", "f1d", "ln1_g", "ln1_b", "w_in", "b_gates", "gmlp_ln_g", "gmlp_ln_b", "gmlp_w_s", "gmlp_b_s",
             "w_ab", "w_gb", "w_out", "ln2_g", "ln2_b", "f2g", "f2u", "f2d", "ln3_g", "ln3_b")
    outs = [loss, dx[None]]
    for dst in (g_out, d_out, m_out, v_out):
        outs += [dst[k] for k in order]
    return tuple(outs)
```

```python
import jax
import jax.numpy as jnp
from jax import lax
from jax.experimental import pallas as pl
from jax.experimental.pallas import tpu as pltpu

F32 = jnp.float32
BF16 = jnp.bfloat16

S = 2048
D = 1024
NSH = 4
FSH = 704
ATT_W = 1536
GRP_W = 512
NG = 3
NH = 8
DH = 64
BLK = 128
NBLK = S // BLK
GW = 1024
IN_W = 8704
IN_SH = IN_W // NSH
ALPHA = 2.0 ** 0.25
LN_EPS = 1e-5
ROPE_THETA = 10000.0
DILATIONS = (1, 4, 16)
ADAM_LR, ADAM_B1, ADAM_B2, ADAM_EPS, ADAM_WD, ADAM_STEP = 0.001, 0.9, 0.999, 1e-08, 0.01, 10
SMALL_ROWS = 144
EPI_ROWS = 256
MESH_T = pl.DeviceIdType.MESH
MIB = 1024 * 1024
NEG_INF = float("-inf")


def _cp(sem, vmem_mib=48):
    return pltpu.CompilerParams(dimension_semantics=sem, vmem_limit_bytes=vmem_mib * MIB)


def _ln_stats(r):
    mu = jnp.mean(r, axis=-1, keepdims=True)
    xc = r - mu
    var = jnp.mean(xc * xc, axis=-1, keepdims=True)
    rstd = lax.rsqrt(var + LN_EPS)
    return xc * rstd, rstd


def _ln_dx(dxh, xh, rstd):
    m1 = jnp.mean(dxh, axis=-1, keepdims=True)
    m2 = jnp.mean(dxh * xh, axis=-1, keepdims=True)
    return rstd * (dxh - m1 - xh * m2)


def _dot_nt(a, b):
    return lax.dot_general(a, b, (((1,), (1,)), ((), ())), preferred_element_type=F32)


def _dot_tn(a, b):
    return lax.dot_general(a, b, (((0,), (0,)), ((), ())), preferred_element_type=F32)


def _dot(a, b):
    return jnp.dot(a, b, preferred_element_type=F32)


def _ffn_fwd(xin, wgt, wut, wd, ln_g, ln_b, name, emit_t=False, target=None):
    with_loss = target is not None
    tm = 1024

    def body(x_ref, wg_ref, wu_ref, wd_ref, g_ref, b_ref, *rest):
        if with_loss:
            t_ref, dr_ref, a_ref, bb_ref, dg_ref, db_ref, loss_ref, acc_ref = rest
        elif emit_t:
            hb_ref, xh_ref, rstd_ref, a_ref, bb_ref, ht_ref, acc_ref = rest
        else:
            hb_ref, xh_ref, rstd_ref, a_ref, bb_ref, acc_ref = rest
        i = pl.program_id(0)
        j = pl.program_id(1)
        xb = x_ref[...].astype(BF16)
        a = _dot_nt(xb, wg_ref[...])
        b = _dot_nt(xb, wu_ref[...])
        a_ref[...] = a.astype(BF16)
        bb_ref[...] = b.astype(BF16)
        s = (a * jax.nn.sigmoid(a)) * b
        f = _dot(s.astype(BF16), wd_ref[...])

        @pl.when(j == 0)
        def _():
            acc_ref[...] = f

        @pl.when(j > 0)
        def _():
            acc_ref[...] += f

        if with_loss:
            @pl.when(jnp.logical_and(j == NSH - 1, i == 0))
            def _():
                dg_ref[...] = jnp.zeros_like(dg_ref)
                db_ref[...] = jnp.zeros_like(db_ref)
                loss_ref[...] = jnp.zeros_like(loss_ref)

        @pl.when(j == NSH - 1)
        def _():
            for c0 in range(0, tm, EPI_ROWS):
                rows = slice(c0, c0 + EPI_ROWS)
                r = ALPHA * x_ref[rows, :] + 0.5 * acc_ref[rows, :]
                xh, rstd = _ln_stats(r)
                h = xh * g_ref[...] + b_ref[...]
                if with_loss:
                    err = h - t_ref[rows, :]
                    dy = err * (1.0 / D)
                    dr_ref[rows, :] = _ln_dx(dy * g_ref[...], xh, rstd)
                    dg_ref[...] += jnp.sum(dy * xh, axis=0, keepdims=True)
                    db_ref[...] += jnp.sum(dy, axis=0, keepdims=True)
                    part = 0.5 * jnp.sum(jnp.mean(err * err, axis=-1, keepdims=True), axis=0, keepdims=True)
                    loss_ref[...] += jnp.broadcast_to(part, (8, 128))
                else:
                    hb_ref[rows, :] = h.astype(BF16)
                    xh_ref[rows, :] = xh
                    rstd_ref[rows, :] = rstd
                    if emit_t:
                        ht_ref[:, rows] = h.T.astype(BF16)

    row = pl.BlockSpec((tm, D), lambda i, j: (i, 0))
    vec = pl.BlockSpec((1, D), lambda i, j: (0, 0))
    wsp = pl.BlockSpec((None, FSH, D), lambda i, j: (j, 0, 0))
    ab = pl.BlockSpec((None, tm, FSH), lambda i, j: (j, i, 0))
    ab_shape = jax.ShapeDtypeStruct((NSH, S, FSH), BF16)
    in_specs, args = [row, wsp, wsp, wsp, vec, vec], (xin, wgt, wut, wd, ln_g, ln_b)
    if with_loss:
        in_specs, args = in_specs + [row], args + (target,)
        out_specs = [row, ab, ab, vec, vec, pl.BlockSpec((8, 128), lambda i, j: (0, 0))]
        out_shape = [jax.ShapeDtypeStruct((S, D), F32), ab_shape, ab_shape, jax.ShapeDtypeStruct((1, D), F32),
                     jax.ShapeDtypeStruct((1, D), F32), jax.ShapeDtypeStruct((8, 128), F32)]
    else:
        out_specs = [row, row, pl.BlockSpec((tm, 1), lambda i, j: (i, 0)), ab, ab]
        out_shape = [jax.ShapeDtypeStruct((S, D), BF16), jax.ShapeDtypeStruct((S, D), F32),
                     jax.ShapeDtypeStruct((S, 1), F32), ab_shape, ab_shape]
        if emit_t:
            out_specs.append(pl.BlockSpec((D, tm), lambda i, j: (0, i)))
            out_shape.append(jax.ShapeDtypeStruct((D, S), BF16))
    return pl.pallas_call(
        body, name=name, grid=(S // tm, NSH), in_specs=in_specs, out_specs=out_specs, out_shape=out_shape,
        scratch_shapes=[pltpu.VMEM((tm, D), F32)],
        compiler_params=_cp(("arbitrary" if with_loss else "parallel", "arbitrary"), vmem_mib=56),
    )(*args)


def _ffn_bwd(dr, xin_b, a, b, wgt, wut, wd, name, after=()):
    tm = 512
    ni = S // tm
    hr = FSH // 2

    def body(dr_ref, a_ref, b_ref, wg_ref, wu_ref, wd_ref, x_hbm, *rest):
        dwg_hbm, dwu_hbm, dwd_hbm, dx_hbm, dx_acc, da_all, db_all, s_all, df_all, x_all, res_buf, sems = rest[len(after):]
        j = pl.program_id(0)
        i = pl.program_id(1)
        rows = pl.ds(pl.multiple_of(i * tm, tm), tm)

        @pl.when(jnp.logical_and(j == 0, i == 0))
        def _():
            cp = pltpu.make_async_copy(x_hbm, x_all, sems.at[0])
            cp.start()
            cp.wait()

        drv = dr_ref[...]
        df = (0.5 * drv).astype(BF16)

        @pl.when(j == 0)
        def _():
            df_all[rows, :] = df

        ds = jnp.concatenate([_dot_nt(df, wd_ref[0:384, :]), _dot_nt(df, wd_ref[384:FSH, :])], axis=1)
        av = a_ref[...].astype(F32)
        bv = b_ref[...].astype(F32)
        sig = jax.nn.sigmoid(av)
        sl = av * sig
        da = (ds * bv * (sig * (1.0 + av * (1.0 - sig)))).astype(BF16)
        db = (ds * sl).astype(BF16)
        da_all[rows, :] = da
        db_all[rows, :] = db
        s_all[rows, :] = (sl * bv).astype(BF16)
        dx = _dot(da, wg_ref[...]) + _dot(db, wu_ref[...])

        @pl.when(j == 0)
        def _():
            dx_acc[rows, :] = ALPHA * drv + dx

        @pl.when(j > 0)
        def _():
            dx_acc[rows, :] += dx

        @pl.when(i == ni - 1)
        def _():
            copies = []
            for n, (lhs, rhs, out) in enumerate(((da_all, x_all, dwg_hbm), (db_all, x_all, dwu_hbm),
                                                 (s_all, df_all, dwd_hbm))):
                slot = n % 2
                if n >= 2:
                    for cp in copies[2 * (n - 2): 2 * (n - 2) + 2]:
                        cp.wait()
                res_buf[slot] = _dot_tn(lhs[...], rhs[...])
                for h in range(2):
                    cp = pltpu.make_async_copy(res_buf.at[slot, pl.ds(h * hr, hr), :], out.at[h, j],
                                               sems.at[1 + 2 * slot + h])
                    cp.start()
                    copies.append(cp)
            for cp in copies[2:]:
                cp.wait()

        @pl.when(jnp.logical_and(j == NSH - 1, i == ni - 1))
        def _():
            cp = pltpu.make_async_copy(dx_acc, dx_hbm, sems.at[0])
            cp.start()
            cp.wait()

    row = pl.BlockSpec((tm, D), lambda j, i: (i, 0))
    wsp = pl.BlockSpec((None, FSH, D), lambda j, i: (j, 0, 0))
    ab = pl.BlockSpec((None, tm, FSH), lambda j, i: (j, i, 0))
    dwshape = jax.ShapeDtypeStruct((2, NSH, hr, D), F32)
    return pl.pallas_call(
        body, name=name, grid=(NSH, ni),
        in_specs=[row, ab, ab, wsp, wsp, wsp, ANY] + [ANY] * len(after),
        out_specs=[ANY, ANY, ANY, ANY],
        out_shape=[dwshape, dwshape, dwshape, jax.ShapeDtypeStruct((S, D), F32)],
        scratch_shapes=[pltpu.VMEM((S, D), F32), pltpu.VMEM((S, FSH), BF16), pltpu.VMEM((S, FSH), BF16),
                        pltpu.VMEM((S, FSH), BF16), pltpu.VMEM((S, D), BF16), pltpu.VMEM((S, D), BF16),
                        pltpu.VMEM((2, FSH, D), F32), pltpu.SemaphoreType.DMA((5,))],
        compiler_params=_cp(("arbitrary", "arbitrary"), vmem_mib=58),
    )(dr, a, b, wgt, wut, wd, xin_b, *after)


def _matmul(a, b, mode, name, *, n, tm, tn, b_col0=0, after=()):
    m, k = a.shape
    assert m % tm == 0 and n % tn == 0 and b_col0 % tn == 0
    off = b_col0 // tn
    na = len(after)

    def body(*refs):
        a_ref, b_ref, o_ref = refs[na:]
        av = a_ref[...].astype(BF16)
        o_ref[...] = _dot(av, b_ref[...]) if mode == "nn" else _dot_nt(av, b_ref[...])

    if mode == "nn":
        b_spec = pl.BlockSpec((k, tn), lambda i, j: (0, j + off))
    else:
        b_spec = pl.BlockSpec((tn, k), lambda i, j: (j, 0))
    return pl.pallas_call(
        body, name=name, grid=(m // tm, n // tn),
        in_specs=[pl.BlockSpec(memory_space=pl.ANY)] * na + [pl.BlockSpec((tm, k), lambda i, j: (i, 0)), b_spec],
        out_specs=pl.BlockSpec((tm, tn), lambda i, j: (i, j)),
        out_shape=jax.ShapeDtypeStruct((m, n), F32),
        compiler_params=_cp(("parallel", "parallel")),
    )(*after, a, b)


def _wgrad(xt, y, rh, c, name, row_sharded, after=()):
    na = len(after)
    if row_sharded:
        def body(x_ref, y_ref, *rest):
            o_ref = rest[na]
            res = _dot(x_ref[...], y_ref[...].astype(BF16))
            for j in range(NSH):
                for h in range(2):
                    o_ref[h, j] = res[(2 * j + h) * rh:(2 * j + h + 1) * rh, :]

        grid = (1,)
        in_specs = [pl.BlockSpec((2 * NSH * rh, S), lambda g: (0, 0)), pl.BlockSpec((S, c), lambda g: (0, 0))]
        out_specs = pl.BlockSpec((2, NSH, rh, c), lambda g: (0, 0, 0, 0))
        sem = ("arbitrary",)
    else:
        def body(x_ref, y_ref, *rest):
            rest[na][...] = _dot(x_ref[...], y_ref[...].astype(BF16))

        grid = (2, NSH)
        in_specs = [pl.BlockSpec((rh, S), lambda h, j: (h, 0)), pl.BlockSpec((S, c), lambda h, j: (0, j))]
        out_specs = pl.BlockSpec((None, None, rh, c), lambda h, j: (h, j, 0, 0))
        sem = ("parallel", "parallel")
    return pl.pallas_call(
        body, name=name, grid=grid, in_specs=in_specs + [pl.BlockSpec(memory_space=pl.ANY)] * na, out_specs=out_specs,
        out_shape=jax.ShapeDtypeStruct((2, NSH, rh, c), F32),
        compiler_params=_cp(sem, vmem_mib=56),
    )(xt, y, *after)


def _resid_ln(res_xh, res_g, res_b, a, w, ln_g, ln_b, name):
    tm = 512

    def body(rx_ref, rg_ref, rb_ref, a_ref, w_ref, g_ref, b_ref, h_ref, hb_ref, xh_ref, rstd_ref):
        r = ALPHA * (rx_ref[...] * rg_ref[...] + rb_ref[...]) + _dot(a_ref[...], w_ref[...])
        xh, rstd = _ln_stats(r)
        h = xh * g_ref[...] + b_ref[...]
        h_ref[...] = h
        hb_ref[...] = h.astype(BF16)
        xh_ref[...] = xh
        rstd_ref[...] = rstd

    row = pl.BlockSpec((tm, D), lambda i: (i, 0))
    vec = pl.BlockSpec((1, D), lambda i: (0, 0))
    return pl.pallas_call(
        body, name=name, grid=(S // tm,),
        in_specs=[row, vec, vec, row, pl.BlockSpec((D, D), lambda i: (0, 0)), vec, vec],
        out_specs=[row, row, row, pl.BlockSpec((tm, 1), lambda i: (i, 0))],
        out_shape=[jax.ShapeDtypeStruct((S, D), F32), jax.ShapeDtypeStruct((S, D), BF16),
                   jax.ShapeDtypeStruct((S, D), F32), jax.ShapeDtypeStruct((S, 1), F32)],
        compiler_params=_cp(("parallel",)),
    )(res_xh, res_g, res_b, a, w, ln_g, ln_b)


def _dh1_ln_bwd(dproj, w_in, dr2, xh, rstd, ln_g, name, after=()):
    tm, tk, ch = 1024, IN_SH, EPI_ROWS
    nk = IN_W // tk
    na = len(after)

    def body(*refs):
        a_ref, b_ref, add_ref, xh_ref, rstd_ref, g_ref, dr_ref, dg_ref, db_ref, acc_ref = refs[na:]
        i = pl.program_id(0)
        k = pl.program_id(1)
        p = _dot_nt(a_ref[...], b_ref[...])

        @pl.when(k == 0)
        def _():
            acc_ref[...] = p

        @pl.when(k > 0)
        def _():
            acc_ref[...] += p

        @pl.when(jnp.logical_and(k == nk - 1, i == 0))
        def _():
            dg_ref[...] = jnp.zeros_like(dg_ref)
            db_ref[...] = jnp.zeros_like(db_ref)

        @pl.when(k == nk - 1)
        def _():
            for c0 in range(0, tm, ch):
                rows = slice(c0, c0 + ch)
                dy = acc_ref[rows, :] + ALPHA * add_ref[rows, :]
                xhv = xh_ref[rows, :]
                dr_ref[rows, :] = _ln_dx(dy * g_ref[...], xhv, rstd_ref[rows, :])
                dg_ref[...] += jnp.sum(dy * xhv, axis=0, keepdims=True)
                db_ref[...] += jnp.sum(dy, axis=0, keepdims=True)

    row = pl.BlockSpec((tm, D), lambda i, k: (i, 0))
    vec = pl.BlockSpec((1, D), lambda i, k: (0, 0))
    return pl.pallas_call(
        body, name=name, grid=(S // tm, nk),
        in_specs=[pl.BlockSpec(memory_space=pl.ANY)] * na
        + [pl.BlockSpec((tm, tk), lambda i, k: (i, k)), pl.BlockSpec((D, tk), lambda i, k: (0, k)), row, row,
           pl.BlockSpec((tm, 1), lambda i, k: (i, 0)), vec],
        out_specs=[row, vec, vec],
        out_shape=[jax.ShapeDtypeStruct((S, D), F32), jax.ShapeDtypeStruct((1, D), F32),
                   jax.ShapeDtypeStruct((1, D), F32)],
        scratch_shapes=[pltpu.VMEM((tm, D), F32)],
        compiler_params=_cp(("arbitrary", "arbitrary"), vmem_mib=56),
    )(*after, dproj, w_in, dr2, xh, rstd, ln_g)


def _ln_bwd(dout, xh, rstd, ln_g, name, after=()):
    tm = 512
    na = len(after)

    def body(*refs):
        y_ref, xh_ref, rstd_ref, g_ref, dr_ref, dg_ref, db_ref = refs[na:]
        dy = y_ref[...]
        i = pl.program_id(0)
        xh = xh_ref[...]
        dr_ref[...] = _ln_dx(dy * g_ref[...], xh, rstd_ref[...])
        dg = jnp.sum(dy * xh, axis=0, keepdims=True)
        db = jnp.sum(dy, axis=0, keepdims=True)

        @pl.when(i == 0)
        def _():
            dg_ref[...] = dg
            db_ref[...] = db

        @pl.when(i > 0)
        def _():
            dg_ref[...] += dg
            db_ref[...] += db

    row = pl.BlockSpec((tm, D), lambda i: (i, 0))
    vec = pl.BlockSpec((1, D), lambda i: (0, 0))
    return pl.pallas_call(
        body, name=name, grid=(S // tm,),
        in_specs=[pl.BlockSpec(memory_space=pl.ANY)] * na + [row, row, pl.BlockSpec((tm, 1), lambda i: (i, 0)), vec],
        out_specs=[row, vec, vec],
        out_shape=[jax.ShapeDtypeStruct((S, D), F32), jax.ShapeDtypeStruct((1, D), F32),
                   jax.ShapeDtypeStruct((1, D), F32)],
        compiler_params=_cp(("arbitrary",)),
    )(*after, dout, xh, rstd, ln_g)


ROPE_TM = 256


def _rope_tables(pos_ref, invf_ref, sign):
    ang = pos_ref[...] * invf_ref[...]
    lane = lax.broadcasted_iota(jnp.int32, ang.shape, 1)
    first = (lane % DH) < (DH // 2)
    sinv = jnp.sin(ang) * sign
    return first, jnp.cos(ang), jnp.where(first, -sinv, sinv)


def _rotate(x, first, cosf, sinf):
    return x * cosf + jnp.where(first, pltpu.roll(x, 96, 1), pltpu.roll(x, 32, 1)) * sinf


def _proj_qkv_rope(hb, w_in, pos_f, invf, name):
    tm = 2 * ROPE_TM

    def body(h_ref, w_ref, pos_ref, invf_ref, o0_ref, o1_ref, o2_ref, buf_ref):
        rot = pl.program_id(1) < 2
        first, cosf, sinf = _rope_tables(pos_ref, invf_ref, 1.0)
        cosf = jnp.where(rot, cosf, 1.0)
        sinf = jnp.where(rot, sinf, 0.0)
        acc = _dot(h_ref[...], w_ref[...])
        for gi, (d, o_ref) in enumerate(zip(DILATIONS, (o0_ref, o1_ref, o2_ref))):
            for ch in range(GRP_W // 128):
                cols = slice(ch * 128, (ch + 1) * 128)
                x = _rotate(acc[:, gi * GRP_W + ch * 128: gi * GRP_W + (ch + 1) * 128], first, cosf, sinf)
                if d == 1:
                    o_ref[0, :, cols] = x.astype(BF16)
                else:
                    buf_ref[...] = x
                    for r in range(d):
                        o_ref[r, :, cols] = buf_ref[pl.ds(r, tm // d, stride=d), :].astype(BF16)

    return pl.pallas_call(
        body, name=name, grid=(S // tm, 3),
        in_specs=[pl.BlockSpec((tm, D), lambda i, s: (i, 0)), pl.BlockSpec((D, ATT_W), lambda i, s: (0, s)),
                  pl.BlockSpec((tm, 1), lambda i, s: (i, 0)), pl.BlockSpec((1, 128), lambda i, s: (0, 0))],
        out_specs=[pl.BlockSpec((d, tm // d, GRP_W), lambda i, s: (0, i, s)) for d in DILATIONS],
        out_shape=[jax.ShapeDtypeStruct((d, S // d, 3 * GRP_W), BF16) for d in DILATIONS],
        scratch_shapes=[pltpu.VMEM((tm, 128), F32)],
        compiler_params=_cp(("parallel", "parallel")),
    )(hb, w_in, pos_f, invf)


def _rope_bwd(dqkv_c, dz, dgl, pos_f, invf, name):
    tm = ROPE_TM

    def body(*refs):
        g_refs, (dz_ref, dgl_ref, pos_ref, invf_ref, o_ref, buf_ref) = refs[:9], refs[9:]
        o_ref[:, 3 * ATT_W:3 * ATT_W + 2 * GW] = dz_ref[...]
        o_ref[:, 3 * ATT_W + 2 * GW:IN_W] = dgl_ref[...]
        first, cosf, sinf = _rope_tables(pos_ref, invf_ref, -1.0)
        for sec in range(3):
            for gi, d in enumerate(DILATIONS):
                g_ref = g_refs[3 * gi + sec]
                for ch in range(GRP_W // 128):
                    cols = slice(ch * 128, (ch + 1) * 128)
                    if d == 1:
                        x = g_ref[0, :, cols]
                    else:
                        for r in range(d):
                            buf_ref[pl.ds(r, tm // d, stride=d), :] = g_ref[r, :, cols]
                        x = buf_ref[...]
                    if sec < 2:
                        x = _rotate(x, first, cosf, sinf)
                    dst = sec * ATT_W + gi * GRP_W + ch * 128
                    o_ref[:, dst:dst + 128] = x.astype(BF16)

    g_specs = [pl.BlockSpec((d, tm // d, GRP_W), lambda i: (0, i, 0)) for d in DILATIONS for _ in range(3)]
    return pl.pallas_call(
        body, name=name, grid=(S // tm,),
        in_specs=g_specs + [pl.BlockSpec((tm, 2 * GW), lambda i: (i, 0)), pl.BlockSpec((tm, 2 * D), lambda i: (i, 0)),
                            pl.BlockSpec((tm, 1), lambda i: (i, 0)), pl.BlockSpec((1, 128), lambda i: (0, 0))],
        out_specs=pl.BlockSpec((tm, IN_W), lambda i: (i, 0)),
        out_shape=jax.ShapeDtypeStruct((S, IN_W), BF16),
        scratch_shapes=[pltpu.VMEM((tm, 128), F32)],
        compiler_params=_cp(("parallel",)),
    )(*[g for grp in dqkv_c for g in grp], dz, dgl, pos_f, invf)


def _class_order(ts, name):
    tm = ROPE_TM
    n = len(ts)

    def body(*refs):
        buf_ref = refs[3 * n]
        for a in range(n):
            for ch in range(GRP_W // 128):
                cols = slice(ch * 128, (ch + 1) * 128)
                buf_ref[...] = refs[a][:, cols]
                for b, d in enumerate(DILATIONS[1:]):
                    for r in range(d):
                        refs[n + 2 * a + b][r, :, cols] = buf_ref[pl.ds(r, tm // d, stride=d), :]

    return pl.pallas_call(
        body, name=name, grid=(S // tm,),
        in_specs=[pl.BlockSpec((tm, GRP_W), lambda i: (i, 0))] * n,
        out_specs=[pl.BlockSpec((d, tm // d, GRP_W), lambda i: (0, i, 0)) for _ in range(n) for d in DILATIONS[1:]],
        out_shape=[jax.ShapeDtypeStruct((d, S // d, GRP_W), F32) for _ in range(n) for d in DILATIONS[1:]],
        scratch_shapes=[pltpu.VMEM((tm, 128), F32)],
        compiler_params=_cp(("parallel",)),
    )(*ts)


def _own_lanes(h):
    return (lax.broadcasted_iota(jnp.int32, (1, 2 * DH), 1) // DH) == (h % 2)


def _heads(ref):
    out = []
    for h in range(NH):
        pair = ref[:, (h // 2) * 2 * DH:(h // 2 + 1) * 2 * DH]
        out.append(jnp.where(_own_lanes(h), pair, jnp.zeros_like(pair)))
    return jnp.stack(out)


def _unheads(t3):
    return jnp.concatenate([t3[2 * p] + t3[2 * p + 1] for p in range(NH // 2)], axis=1)


def _bdot_nt(a, b):
    return lax.dot_general(a, b, (((2,), (2,)), ((0,), (0,))), preferred_element_type=F32)


def _bdot(a, b):
    return lax.dot_general(a, b, (((2,), (1,)), ((0,), (0,))), preferred_element_type=F32)


def _bdot_tn(a, b):
    return lax.dot_general(a, b, (((1,), (1,)), ((0,), (0,))), preferred_element_type=F32)


def _attn_fwd(gi, qkv_c, name):
    d = DILATIONS[gi]
    nblk = S // d // BLK

    def body(*refs):
        if nblk > 1:
            q_ref, kc_ref, kp_ref, vc_ref, vp_ref, o_ref, lse_ref = refs
            has_prev = pl.program_id(1) != 0
        else:
            q_ref, kc_ref, vc_ref, o_ref, lse_ref = refs
        qi = lax.broadcasted_iota(jnp.int32, (NH, BLK, BLK), 1)
        kj = lax.broadcasted_iota(jnp.int32, (NH, BLK, BLK), 2)
        q = _heads(q_ref)
        sc = jnp.where(kj <= qi, _bdot_nt(q, _heads(kc_ref)) * 0.125, NEG_INF)
        m = jnp.max(sc, axis=-1, keepdims=True)
        if nblk > 1:
            mask_p = jnp.logical_and(kj >= qi, has_prev)
            sp = jnp.where(mask_p, _bdot_nt(q, _heads(kp_ref)) * 0.125, NEG_INF)
            m = jnp.maximum(m, jnp.max(sp, axis=-1, keepdims=True))
        pc = jnp.exp(sc - m)
        l = jnp.sum(pc, axis=-1, keepdims=True)
        o = _bdot(pc.astype(BF16), _heads(vc_ref))
        if nblk > 1:
            pp = jnp.exp(sp - m)
            l = l + jnp.sum(pp, axis=-1, keepdims=True)
            o = o + _bdot(pp.astype(BF16), _heads(vp_ref))
        o_ref[...] = _unheads(o / l)
        lse = jnp.broadcast_to(m + jnp.log(l), (NH, BLK, 2 * DH))
        lse_ref[...] = _unheads(jnp.stack([jnp.where(_own_lanes(h), lse[h], 0.0) for h in range(NH)]))

    def cur(sec):
        return pl.BlockSpec((None, BLK, GRP_W), lambda r, n: (r, n, sec))

    def prev(sec):
        return pl.BlockSpec((None, BLK, GRP_W), lambda r, n: (r, jnp.maximum(n - 1, 0), sec))

    out = pl.BlockSpec((None, BLK, GRP_W), lambda r, n: (r, n, 0))
    shp = jax.ShapeDtypeStruct((d, S // d, GRP_W), F32)
    if nblk > 1:
        in_specs, args = [cur(0), cur(1), prev(1), cur(2), prev(2)], (qkv_c,) * 5
    else:
        in_specs, args = [cur(0), cur(1), cur(2)], (qkv_c,) * 3
    return pl.pallas_call(
        body, name=name, grid=(d, nblk), in_specs=in_specs, out_specs=[out, out], out_shape=[shp, shp],
        compiler_params=_cp(("parallel", "parallel")),
    )(*args)


def _attn_combine(os, lses, name):
    tm = ROPE_TM

    def body(o0_ref, o1_ref, o2_ref, l0_ref, l1_ref, l2_ref, y_ref, yt_ref, l_ref, buf_ref):
        def token_order(ref, d, cols, slot):
            if d == 1:
                return ref[0, :, cols]
            for r in range(d):
                buf_ref[slot, pl.ds(r, tm // d, stride=d), :] = ref[r, :, cols]
            return buf_ref[slot]

        for ch in range(GRP_W // 128):
            cols = slice(ch * 128, (ch + 1) * 128)
            o = [token_order(ref, d, cols, k) for k, (ref, d) in enumerate(zip((o0_ref, o1_ref, o2_ref), DILATIONS))]
            ls = [token_order(ref, d, cols, 3 + k)
                  for k, (ref, d) in enumerate(zip((l0_ref, l1_ref, l2_ref), DILATIONS))]
            m = jnp.maximum(jnp.maximum(ls[0], ls[1]), ls[2])
            e = [jnp.exp(l - m) for l in ls]
            den = e[0] + e[1] + e[2]
            y = (e[0] * o[0] + e[1] * o[1] + e[2] * o[2]) / den
            y_ref[:, cols] = y
            yt_ref[cols, :] = y.T.astype(BF16)
            l_ref[:, cols] = m + jnp.log(den)

    blk = pl.BlockSpec((tm, GRP_W), lambda i: (i, 0))
    cls = [pl.BlockSpec((d, tm // d, GRP_W), lambda i: (0, i, 0)) for d in DILATIONS]
    shp = jax.ShapeDtypeStruct((S, GRP_W), F32)
    return pl.pallas_call(
        body, name=name, grid=(S // tm,), in_specs=cls + cls,
        out_specs=[blk, pl.BlockSpec((GRP_W, tm), lambda i: (0, i)), blk],
        out_shape=[shp, jax.ShapeDtypeStruct((GRP_W, S), BF16), shp],
        scratch_shapes=[pltpu.VMEM((6, tm, 128), F32)],
        compiler_params=_cp(("parallel",)),
    )(*os, *lses)


def _attn_bwd(gi, qkv_c, dy_c, y_c, lse_c, name):
    d = DILATIONS[gi]
    nblk = S // d // BLK

    def body(*refs):
        if nblk > 1:
            (q_ref, qn_ref, k_ref, kp_ref, v_ref, vp_ref, dy_ref, dyn_ref, y_ref, yn_ref, l_ref, ln_ref,
             dq_ref, dk_ref, dv_ref) = refs
            n = pl.program_id(1)
            has_prev = n != 0
            has_next = n != nblk - 1
        else:
            q_ref, k_ref, v_ref, dy_ref, y_ref, l_ref, dq_ref, dk_ref, dv_ref = refs
        qi = lax.broadcasted_iota(jnp.int32, (NH, BLK, BLK), 1)
        kj = lax.broadcasted_iota(jnp.int32, (NH, BLK, BLK), 2)

        def lse_col(ref):
            return jnp.stack([ref[:, h * DH:h * DH + 1] for h in range(NH)])

        q, k, v = _heads(q_ref), _heads(k_ref), _heads(v_ref)
        dy = _heads(dy_ref)
        dd = jnp.sum(dy * _heads(y_ref), axis=-1, keepdims=True)
        lcol = lse_col(l_ref)
        dyb = dy.astype(BF16)
        p = jnp.exp(jnp.where(kj <= qi, _bdot_nt(q, k) * 0.125, NEG_INF) - lcol)
        ds = (p * (_bdot_nt(dyb, v) - dd)).astype(BF16)
        dq = _bdot(ds, k)
        dk = _bdot_tn(ds, q)
        dv = _bdot_tn(p.astype(BF16), dyb)
        if nblk > 1:
            qn, kpv, vpv = _heads(qn_ref), _heads(kp_ref), _heads(vp_ref)
            dyn = _heads(dyn_ref)
            ddn = jnp.sum(dyn * _heads(yn_ref), axis=-1, keepdims=True)
            lncol = lse_col(ln_ref)
            dynb = dyn.astype(BF16)
            mask_p = jnp.logical_and(kj >= qi, has_prev)
            pp = jnp.exp(jnp.where(mask_p, _bdot_nt(q, kpv) * 0.125, NEG_INF) - lcol)
            dsp = (pp * (_bdot_nt(dyb, vpv) - dd)).astype(BF16)
            dq = dq + _bdot(dsp, kpv)
            mask_n = jnp.logical_and(kj >= qi, has_next)
            pn = jnp.exp(jnp.where(mask_n, _bdot_nt(qn, k) * 0.125, NEG_INF) - lncol)
            dsn = (pn * (_bdot_nt(dynb, v) - ddn)).astype(BF16)
            dk = dk + _bdot_tn(dsn, qn)
            dv = dv + _bdot_tn(pn.astype(BF16), dynb)
        dq_ref[...] = _unheads(dq) * 0.125
        dk_ref[...] = _unheads(dk) * 0.125
        dv_ref[...] = _unheads(dv)

    def spec(sec, shift):
        def idx(r, n):
            return (r, jnp.clip(n + shift, 0, nblk - 1), sec)
        return pl.BlockSpec((None, BLK, GRP_W), idx)

    if nblk > 1:
        in_specs = [spec(0, 0), spec(0, 1), spec(1, 0), spec(1, -1), spec(2, 0), spec(2, -1),
                    spec(0, 0), spec(0, 1), spec(0, 0), spec(0, 1), spec(0, 0), spec(0, 1)]
        args = (qkv_c,) * 6 + (dy_c, dy_c, y_c, y_c, lse_c, lse_c)
    else:
        in_specs = [spec(0, 0), spec(1, 0), spec(2, 0), spec(0, 0), spec(0, 0), spec(0, 0)]
        args = (qkv_c, qkv_c, qkv_c, dy_c, y_c, lse_c)
    out = spec(0, 0)
    shp = jax.ShapeDtypeStruct((d, S // d, GRP_W), F32)
    return pl.pallas_call(
        body, name=name, grid=(d, nblk), in_specs=in_specs, out_specs=[out, out, out], out_shape=[shp, shp, shp],
        compiler_params=_cp(("parallel", "parallel")),
    )(*args)


_SQRT_HALF = 0.7071067811865476
_INV_SQRT_2PI = 0.3989422804014327


def _gelu(z):
    return 0.5 * z * (1.0 + lax.erf(z * _SQRT_HALF))


def _gelu_grad(z):
    return 0.5 * (1.0 + lax.erf(z * _SQRT_HALF)) + z * (jnp.exp(-0.5 * z * z) * _INV_SQRT_2PI)


def _tril_mask():
    t = lax.broadcasted_iota(jnp.int32, (BLK, BLK), 0)
    s = lax.broadcasted_iota(jnp.int32, (BLK, BLK), 1)
    return s <= t


def _groups(t):
    return jnp.stack([t[:, g * BLK:(g + 1) * BLK] for g in range(8)])


def _ungroup(t3):
    return jnp.concatenate([t3[g] for g in range(8)], axis=1)


def _group_bias(bs_ref):
    return jnp.stack([bs_ref[:, g:g + 1] for g in range(8)])


def _gmlp_fwd(z, ln_g, ln_b, w_s, b_s_t, name):
    def body(z_ref, g_ref, b_ref, ws_ref, bs_ref, y_ref, yt_ref):
        zg = _gelu(z_ref[...])
        u = zg[:, :GW]
        xh, _ = _ln_stats(zg[:, GW:])
        vn = (xh * g_ref[...] + b_ref[...]).astype(BF16)
        wt = jnp.where(_tril_mask(), ws_ref[...], 0.0).astype(BF16)
        yv = u * _ungroup(_bdot(wt, _groups(vn)) + _group_bias(bs_ref))
        y_ref[...] = yv.astype(BF16)
        yt_ref[...] = yv.T.astype(BF16)

    vec = pl.BlockSpec((1, GW), lambda n: (0, 0))
    return pl.pallas_call(
        body, name=name, grid=(NBLK,),
        in_specs=[pl.BlockSpec((BLK, 2 * GW), lambda n: (n, 0)), vec, vec,
                  pl.BlockSpec((8, BLK, BLK), lambda n: (0, 0, 0)), pl.BlockSpec((BLK, 8), lambda n: (0, 0))],
        out_specs=[pl.BlockSpec((BLK, GW), lambda n: (n, 0)), pl.BlockSpec((GW, BLK), lambda n: (0, n))],
        out_shape=[jax.ShapeDtypeStruct((S, GW), BF16), jax.ShapeDtypeStruct((GW, S), BF16)],
        compiler_params=_cp(("parallel",)),
    )(z, ln_g, ln_b, w_s, b_s_t)


def _gmlp_bwd(z, dy, ln_g, ln_b, w_s, b_s_t, name):
    def body(z_ref, dy_ref, g_ref, b_ref, ws_ref, bs_ref, dz_ref, dws_ref, dbs_ref, dg_ref, db_ref, dvn_ref):
        n = pl.program_id(0)
        zv = z_ref[...]
        zg = _gelu(zv)
        u = zg[:, :GW]
        xh, rstd = _ln_stats(zg[:, GW:])
        vn = (xh * g_ref[...] + b_ref[...]).astype(BF16)
        tril = _tril_mask()

        @pl.when(n == 0)
        def _():
            dws_ref[...] = jnp.zeros_like(dws_ref)
            dbs_ref[...] = jnp.zeros_like(dbs_ref)
            dg_ref[...] = jnp.zeros_like(dg_ref)
            db_ref[...] = jnp.zeros_like(db_ref)

        wt = jnp.where(tril, ws_ref[...], 0.0).astype(BF16)
        vn3 = _groups(vn)
        dyv = dy_ref[...]
        mixed = _ungroup(_bdot(wt, vn3) + _group_bias(bs_ref))
        dz_ref[:, :GW] = (dyv * mixed * _gelu_grad(zv[:, :GW])).astype(BF16)
        dmix3 = _groups(dyv * u)
        dmb = dmix3.astype(BF16)
        dws_ref[...] += jnp.where(tril, _bdot_nt(dmb, vn3), 0.0)
        dbsum = jnp.sum(dmix3, axis=-1, keepdims=True)
        for gg in range(8):
            dbs_ref[:, gg:gg + 1] += dbsum[gg]
        dvn_ref[...] = _ungroup(_bdot_tn(wt, dmb))

        dvn = dvn_ref[...]
        dg_ref[...] += jnp.sum(dvn * xh, axis=0, keepdims=True)
        db_ref[...] += jnp.sum(dvn, axis=0, keepdims=True)
        dvg = _ln_dx(dvn * g_ref[...], xh, rstd)
        dz_ref[:, GW:] = (dvg * _gelu_grad(zv[:, GW:])).astype(BF16)

    vec = pl.BlockSpec((1, GW), lambda n: (0, 0))
    ws = pl.BlockSpec((8, BLK, BLK), lambda n: (0, 0, 0))
    bs = pl.BlockSpec((BLK, 8), lambda n: (0, 0))
    return pl.pallas_call(
        body, name=name, grid=(NBLK,),
        in_specs=[pl.BlockSpec((BLK, 2 * GW), lambda n: (n, 0)), pl.BlockSpec((BLK, GW), lambda n: (n, 0)),
                  vec, vec, ws, bs],
        out_specs=[pl.BlockSpec((BLK, 2 * GW), lambda n: (n, 0)), ws, bs, vec, vec],
        out_shape=[jax.ShapeDtypeStruct((S, 2 * GW), BF16), jax.ShapeDtypeStruct((8, BLK, BLK), F32),
                   jax.ShapeDtypeStruct((BLK, 8), F32), jax.ShapeDtypeStruct((1, GW), F32),
                   jax.ShapeDtypeStruct((1, GW), F32)],
        scratch_shapes=[pltpu.VMEM((BLK, GW), F32)],
        compiler_params=_cp(("arbitrary",)),
    )(z, dy, ln_g, ln_b, w_s, b_s_t)


def _merge_fwd(a, b, gl, b_gates, name):
    tm = 512

    def body(a_ref, b_ref, g0_ref, g1_ref, bg_ref, o_ref, ot_ref):
        g0 = jax.nn.sigmoid(g0_ref[...] + bg_ref[:, :D])
        g1 = jax.nn.sigmoid(g1_ref[...] + bg_ref[:, D:])
        mg = g0 * a_ref[...] + g1 * b_ref[...]
        o_ref[...] = mg.astype(BF16)
        ot_ref[...] = mg.T.astype(BF16)

    row = pl.BlockSpec((tm, D), lambda i: (i, 0))
    return pl.pallas_call(
        body, name=name, grid=(S // tm,),
        in_specs=[row, row, row, pl.BlockSpec((tm, D), lambda i: (i, 1)), pl.BlockSpec((1, 2 * D), lambda i: (0, 0))],
        out_specs=[row, pl.BlockSpec((D, tm), lambda i: (0, i))],
        out_shape=[jax.ShapeDtypeStruct((S, D), BF16), jax.ShapeDtypeStruct((D, S), BF16)],
        compiler_params=_cp(("parallel",)),
    )(a, b, gl, gl, b_gates)


def _merge_bwd(dm, a, b, gl, b_gates, name):
    tm = 512

    def body(dm_ref, a_ref, b_ref, g0_ref, g1_ref, bg_ref, da_ref, db_ref, dgl_ref, dbg_ref):
        i = pl.program_id(0)
        dmv = dm_ref[...]
        g0 = jax.nn.sigmoid(g0_ref[...] + bg_ref[:, :D])
        g1 = jax.nn.sigmoid(g1_ref[...] + bg_ref[:, D:])
        da_ref[...] = (dmv * g0).astype(BF16)
        db_ref[...] = (dmv * g1).astype(BF16)
        d0 = dmv * a_ref[...] * g0 * (1.0 - g0)
        d1 = dmv * b_ref[...] * g1 * (1.0 - g1)
        dgl_ref[:, :D] = d0.astype(BF16)
        dgl_ref[:, D:] = d1.astype(BF16)
        s0 = jnp.sum(d0, axis=0, keepdims=True)
        s1 = jnp.sum(d1, axis=0, keepdims=True)

        @pl.when(i == 0)
        def _():
            dbg_ref[:, :D] = s0
            dbg_ref[:, D:] = s1

        @pl.when(i > 0)
        def _():
            dbg_ref[:, :D] += s0
            dbg_ref[:, D:] += s1

    row = pl.BlockSpec((tm, D), lambda i: (i, 0))
    wide = pl.BlockSpec((tm, 2 * D), lambda i: (i, 0))
    bg = pl.BlockSpec((1, 2 * D), lambda i: (0, 0))
    return pl.pallas_call(
        body, name=name, grid=(S // tm,),
        in_specs=[row, row, row, row, pl.BlockSpec((tm, D), lambda i: (i, 1)), bg],
        out_specs=[row, row, wide, bg],
        out_shape=[jax.ShapeDtypeStruct((S, D), BF16), jax.ShapeDtypeStruct((S, D), BF16),
                   jax.ShapeDtypeStruct((S, 2 * D), BF16), jax.ShapeDtypeStruct((1, 2 * D), F32)],
        compiler_params=_cp(("arbitrary",)),
    )(dm, a, b, gl, gl, b_gates)


def _adam_math(w, g, m, v):
    m2 = ADAM_B1 * m + (1.0 - ADAM_B1) * g
    v2 = ADAM_B2 * v + (1.0 - ADAM_B2) * (g * g)
    m_hat = m2 / (1.0 - ADAM_B1 ** ADAM_STEP)
    v_hat = v2 / (1.0 - ADAM_B2 ** ADAM_STEP)
    delta = -ADAM_LR * (m_hat / (jnp.sqrt(v_hat) + ADAM_EPS) + ADAM_WD * w)
    return delta, m2, v2


def _pick_rows(rows, cols, unit=16, budget=2 * MIB):
    best = unit
    for t in range(unit, rows + 1, unit):
        if rows % t == 0 and t * cols * 4 <= budget:
            best = t
    assert rows % best == 0
    return best


def _adamw(w, g, m, v, name):
    r, c = w.shape
    tr = _pick_rows(r, c, unit=8)

    def body(w_ref, g_ref, m_ref, v_ref, go_ref, d_ref, mo_ref, vo_ref):
        gv = g_ref[...]
        delta, m2, v2 = _adam_math(w_ref[...], gv, m_ref[...], v_ref[...])
        go_ref[...] = gv
        d_ref[...] = delta
        mo_ref[...] = m2
        vo_ref[...] = v2

    blk = pl.BlockSpec((tr, c), lambda i: (i, 0))
    shp = jax.ShapeDtypeStruct((r, c), F32)
    return pl.pallas_call(
        body, name=name, grid=(r // tr,), in_specs=[blk] * 4, out_specs=[blk] * 4, out_shape=[shp] * 4,
        compiler_params=_cp(("parallel",)),
    )(*[pltpu.with_memory_space_constraint(t, pltpu.HBM) for t in (w, g, m, v)])


def _small_sum_adamw(parts, own, pos, w, m, v, name):
    tr = 48

    def body(pos_ref, p_ref, own_ref, w_ref, m_ref, v_ref, g_ref, d_ref, mo_ref, vo_ref):
        me = 2 * pos_ref[1] + pos_ref[0]
        gv = None
        for k in range(8):
            term = jnp.where(me == k, own_ref[...], p_ref[k])
            gv = term if gv is None else gv + term
        delta, m2, v2 = _adam_math(w_ref[...], gv, m_ref[...], v_ref[...])
        g_ref[...] = gv
        d_ref[...] = delta
        mo_ref[...] = m2
        vo_ref[...] = v2

    blk = pl.BlockSpec((tr, D), lambda i, p: (i, 0))
    shp = jax.ShapeDtypeStruct((SMALL_ROWS, D), F32)
    return pl.pallas_call(
        body, name=name,
        grid_spec=pltpu.PrefetchScalarGridSpec(
            num_scalar_prefetch=1, grid=(SMALL_ROWS // tr,),
            in_specs=[pl.BlockSpec((8, tr, D), lambda i, p: (0, i, 0)), blk, blk, blk, blk],
            out_specs=[blk] * 4),
        out_shape=[shp] * 4,
        compiler_params=_cp(("parallel",)),
    )(pos, parts, own, w, m, v)


ANY = pl.BlockSpec(memory_space=pl.ANY)


def _in_hbm(arrays):
    return [pltpu.with_memory_space_constraint(a, pltpu.HBM) for a in arrays]


def _mesh_pos():
    x, y, c = lax.axis_index("x"), lax.axis_index("y"), lax.axis_index("c")
    chips = [(1 - x, y), (x, 1 - y), (1 - x, 1 - y)]
    return x, y, c, chips


def _place_shard(w, kind, pos, name):
    r, c = w.shape
    tr = _pick_rows(r, c)

    def body(pos_ref, w_ref, o_ref):
        o_ref[...] = w_ref[...].astype(BF16)

    if kind == "stack":
        o_spec = pl.BlockSpec((None, tr, c), lambda i, p: (p[1], i, 0))
        shape = (NSH, r, c)
    else:
        o_spec = pl.BlockSpec((tr, c), lambda i, p: (i, p[1]))
        shape = (r, NSH * c)
    return pl.pallas_call(
        body, name=name,
        grid_spec=pltpu.PrefetchScalarGridSpec(
            num_scalar_prefetch=1, grid=(r // tr,),
            in_specs=[pl.BlockSpec((tr, c), lambda i, p: (i, 0))], out_specs=o_spec),
        out_shape=pltpu.HBM(shape, BF16),
        compiler_params=_cp(("parallel",)),
    )(pos, pltpu.with_memory_space_constraint(w, pltpu.HBM))


SEM = pl.BlockSpec(memory_space=pltpu.SEMAPHORE)
SPLIT_COPY = pltpu.CompilerParams(has_side_effects=pltpu.SideEffectType.DATAFLOW_SIDE_EFFECTING)


def _shard_window(ref, kind, j, h, dims):
    r, c = dims
    rows = pl.ds(pl.multiple_of(h * (r // 2), 16), r // 2)
    if kind == "stack":
        return ref.at[j, rows, :]
    return ref.at[rows, pl.ds(pl.multiple_of(j * c, 128), c)]


def _ici_copy(ref, kind, dims, j, c, sems, idx, to):
    win = _shard_window(ref, kind, j, c, dims)
    return pltpu.make_async_remote_copy(src_ref=win, dst_ref=win, send_sem=sems[0].at[idx], recv_sem=sems[1].at[idx],
                                        device_id=to, device_id_type=MESH_T)


def _both_copy(ref, kind, dims, a, k, chip, half, tc, sc, sems):
    win = _shard_window(ref, kind, half[0], half[1], dims)
    return pltpu.make_async_remote_copy(src_ref=win, dst_ref=win, send_sem=sems[0].at[6 * a + 2 * k + tc],
                                        recv_sem=sems[1].at[6 * a + 2 * k + sc],
                                        device_id=(chip[0], chip[1], tc), device_id_type=MESH_T)


def _gather_start(fulls, kinds, dims, after, name, both=False):
    n, na = len(fulls), len(after)
    per = 6 if both else 3

    def body(*refs):
        outs = refs[n + na:2 * n + na]
        send_sems, recv_sems, token = refs[2 * n + na:]
        x, y, c, chips = _mesh_pos()
        for a in range(n):
            for k, chip in enumerate(chips):
                if both:
                    for tc in range(2):
                        _both_copy(outs[a], kinds[a], dims[a], a, k, chip, (2 * x + y, c), tc, c,
                                   (send_sems, recv_sems)).start()
                else:
                    _ici_copy(outs[a], kinds[a], dims[a], 2 * x + y, c, (send_sems, recv_sems), 3 * a + k,
                              (chip[0], chip[1], c)).start()
        token[...] = jnp.zeros_like(token)

    res = pl.pallas_call(
        body, name=name, in_specs=[ANY] * (n + na),
        out_specs=[ANY] * n + [SEM, SEM, pl.BlockSpec(memory_space=pltpu.VMEM)],
        out_shape=[pltpu.HBM(f.shape, BF16) for f in fulls]
        + [pltpu.SemaphoreType.DMA((per * n,)), pltpu.SemaphoreType.DMA((per * n,)),
           jax.ShapeDtypeStruct((8, 128), F32)],
        input_output_aliases={i: i for i in range(n)},
        compiler_params=SPLIT_COPY,
    )(*_in_hbm(fulls), *after)
    return res[:n], res[n], res[n + 1], res[n + 2]


def _gather_wait(fulls, send_sems, recv_sems, kinds, dims, after, name, both=False):
    n, na = len(fulls), len(after)

    def body(*refs):
        ssem, rsem = refs[n], refs[n + 1]
        outs = refs[n + 2 + na:]
        x, y, c, chips = _mesh_pos()
        for a in range(n):
            for k, chip in enumerate(chips):
                if both:
                    for oc in range(2):
                        _both_copy(outs[a], kinds[a], dims[a], a, k, chip, (2 * x + y, c), oc, c,
                                   (ssem, rsem)).wait_send()
                        _both_copy(outs[a], kinds[a], dims[a], a, k, chip, (2 * chip[0] + chip[1], oc), c, oc,
                                   (ssem, rsem)).wait_recv()
                    continue
                to = (chip[0], chip[1], c)
                _ici_copy(outs[a], kinds[a], dims[a], 2 * x + y, c, (ssem, rsem), 3 * a + k, to).wait_send()
                _ici_copy(outs[a], kinds[a], dims[a], 2 * chip[0] + chip[1], c, (ssem, rsem), 3 * a + k, to).wait_recv()

    return pl.pallas_call(
        body, name=name, in_specs=[ANY] * n + [SEM, SEM] + [ANY] * na, out_specs=[ANY] * n,
        out_shape=[pltpu.HBM(f.shape, BF16) for f in fulls],
        input_output_aliases={i: i for i in range(n)},
        compiler_params=SPLIT_COPY,
    )(*_in_hbm(fulls), send_sems, recv_sems, *after)


def _gather_forward(fulls, kinds, dims, name):
    n = len(fulls)

    def body(*refs):
        outs = refs[n:2 * n]
        sems = refs[2 * n:]
        x, y, c, chips = _mesh_pos()
        sib = (x, y, 1 - c)
        cps = []
        for a in range(n):
            for k, chip in enumerate(chips):
                cp = _ici_copy(outs[a], kinds[a], dims[a], 2 * chip[0] + chip[1], c, sems, 3 * a + k, sib)
                cp.start()
                cps.append(cp)
        for a in range(n):
            for k, chip in enumerate(chips):
                _ici_copy(outs[a], kinds[a], dims[a], 2 * chip[0] + chip[1], 1 - c, sems, 3 * a + k, sib).wait_recv()
        for cp in cps:
            cp.wait_send()

    return pl.pallas_call(
        body, name=name, in_specs=[ANY] * n, out_specs=[ANY] * n,
        out_shape=[pltpu.HBM(f.shape, BF16) for f in fulls],
        input_output_aliases={i: i for i in range(n)},
        scratch_shapes=[pltpu.SemaphoreType.DMA((3 * n,)), pltpu.SemaphoreType.DMA((3 * n,))],
    )(*_in_hbm(fulls))


def _pair_copy(src, land, a, x, y, c, sems):
    return pltpu.make_async_remote_copy(
        src_ref=src.at[1 - c], dst_ref=land, send_sem=sems[0].at[a], recv_sem=sems[1].at[a],
        device_id=(x, y, 1 - c), device_id_type=MESH_T)


def _pair_start(grads, lands, name):
    n = len(grads)

    def body(*refs):
        srcs, dsts = refs[2 * n:3 * n], refs[3 * n:4 * n]
        send_sems, recv_sems, token = refs[4 * n:]
        x, y, c, _ = _mesh_pos()
        for a in range(n):
            _pair_copy(srcs[a], dsts[a], a, x, y, c, (send_sems, recv_sems)).start()
        token[...] = jnp.zeros_like(token)

    res = pl.pallas_call(
        body, name=name, in_specs=[ANY] * (2 * n),
        out_specs=[ANY] * (2 * n) + [SEM, SEM, pl.BlockSpec(memory_space=pltpu.VMEM)],
        out_shape=[pltpu.HBM(g.shape, F32) for g in grads]
        + [pltpu.HBM(l.shape, F32) for l in lands]
        + [pltpu.SemaphoreType.DMA((n,)), pltpu.SemaphoreType.DMA((n,)), jax.ShapeDtypeStruct((8, 128), F32)],
        input_output_aliases={i: i for i in range(2 * n)},
        compiler_params=SPLIT_COPY,
    )(*_in_hbm(grads), *_in_hbm(lands))
    return res[:n], res[n:2 * n], res[2 * n], res[2 * n + 1], res[2 * n + 2]


def _pair_wait(grads, lands, send_sems, recv_sems, after, name):
    n, na = len(grads), len(after)

    def body(*refs):
        ssem, rsem = refs[2 * n], refs[2 * n + 1]
        outs = refs[2 * n + 2 + na:]
        x, y, c, _ = _mesh_pos()
        for a in range(n):
            cp = _pair_copy(outs[a], outs[n + a], a, x, y, c, (ssem, rsem))
            cp.wait_send()
            cp.wait_recv()

    res = pl.pallas_call(
        body, name=name, in_specs=[ANY] * (2 * n) + [SEM, SEM] + [ANY] * na, out_specs=[ANY] * (2 * n),
        out_shape=[pltpu.HBM(g.shape, F32) for g in grads]
        + [pltpu.HBM(l.shape, F32) for l in lands],
        input_output_aliases={i: i for i in range(2 * n)},
        compiler_params=SPLIT_COPY,
    )(*_in_hbm(grads), *_in_hbm(lands), send_sems, recv_sems, *after)
    return res[:n], res[n:]


def _pair_sum(g, recv, pos, name):
    _, _, rh, c = g.shape
    tr = _pick_rows(rh, c)

    def body(pos_ref, g_ref, r_ref, o_ref):
        o_ref[...] = (g_ref[...] + r_ref[...]).astype(BF16)

    return pl.pallas_call(
        body, name=name,
        grid_spec=pltpu.PrefetchScalarGridSpec(
            num_scalar_prefetch=1, grid=(3, rh // tr),
            in_specs=[pl.BlockSpec((None, None, tr, c), lambda k, r, p: (p[0], p[2 + k], r, 0)),
                      pl.BlockSpec((None, tr, c), lambda k, r, p: (p[2 + k], r, 0))],
            out_specs=pl.BlockSpec((None, tr, c), lambda k, r, p: (k, r, 0))),
        out_shape=pltpu.HBM((3, rh, c), BF16),
        compiler_params=_cp(("parallel", "parallel")),
    )(pos, *_in_hbm([g, recv]))


def _chip_copy(src, land, a, k, chip, c, sems):
    return pltpu.make_async_remote_copy(
        src_ref=src.at[k], dst_ref=land.at[k], send_sem=sems[0].at[3 * a + k],
        recv_sem=sems[1].at[3 * a + k], device_id=(chip[0], chip[1], c), device_id_type=MESH_T)


def _chip_start(psums, lands, name):
    n = len(psums)

    def body(*refs):
        srcs, dsts = refs[2 * n:3 * n], refs[3 * n:4 * n]
        send_sems, recv_sems, token = refs[4 * n:]
        x, y, c, chips = _mesh_pos()
        for a in range(n):
            for k, chip in enumerate(chips):
                _chip_copy(srcs[a], dsts[a], a, k, chip, c, (send_sems, recv_sems)).start()
        token[...] = jnp.zeros_like(token)

    res = pl.pallas_call(
        body, name=name, in_specs=[ANY] * (2 * n),
        out_specs=[ANY] * (2 * n) + [SEM, SEM, pl.BlockSpec(memory_space=pltpu.VMEM)],
        out_shape=[pltpu.HBM(p.shape, BF16) for p in psums]
        + [pltpu.HBM(l.shape, BF16) for l in lands]
        + [pltpu.SemaphoreType.DMA((3 * n,)), pltpu.SemaphoreType.DMA((3 * n,)), jax.ShapeDtypeStruct((8, 128), F32)],
        input_output_aliases={i: i for i in range(2 * n)},
        compiler_params=SPLIT_COPY,
    )(*_in_hbm(psums), *_in_hbm(lands))
    return res[:n], res[n:2 * n], res[2 * n], res[2 * n + 1], res[2 * n + 2]


def _chip_wait(psums, lands, send_sems, recv_sems, after, name):
    n, na = len(psums), len(after)

    def body(*refs):
        ssem, rsem = refs[2 * n], refs[2 * n + 1]
        outs = refs[2 * n + 2 + na:]
        srcs, dsts = outs[:n], outs[n:]
        x, y, c, chips = _mesh_pos()
        for a in range(n):
            for k, chip in enumerate(chips):
                cp = _chip_copy(srcs[a], dsts[a], a, k, chip, c, (ssem, rsem))
                cp.wait_send()
                cp.wait_recv()

    res = pl.pallas_call(
        body, name=name, in_specs=[ANY] * (2 * n) + [SEM, SEM] + [ANY] * na, out_specs=[ANY] * (2 * n),
        out_shape=[pltpu.HBM(p.shape, BF16) for p in psums]
        + [pltpu.HBM(l.shape, BF16) for l in lands],
        input_output_aliases={i: i for i in range(2 * n)},
        compiler_params=SPLIT_COPY,
    )(*_in_hbm(psums), *_in_hbm(lands), send_sems, recv_sems, *after)
    return res[n:]


def _owner_sum(g, recv_a, recv_b, pos, name):
    _, _, rh, c = g.shape
    tr = _pick_rows(rh, c)

    def body(pos_ref, g_ref, ra_ref, rb_ref, o_ref):
        acc = g_ref[...] + ra_ref[...]
        for k in range(3):
            acc = acc + rb_ref[k].astype(F32)
        o_ref[...] = acc

    return pl.pallas_call(
        body, name=name,
        grid_spec=pltpu.PrefetchScalarGridSpec(
            num_scalar_prefetch=1, grid=(rh // tr,),
            in_specs=[pl.BlockSpec((None, None, tr, c), lambda r, p: (p[0], p[1], r, 0)),
                      pl.BlockSpec((None, tr, c), lambda r, p: (p[1], r, 0)),
                      pl.BlockSpec((3, tr, c), lambda r, p: (0, r, 0))],
            out_specs=pl.BlockSpec((None, tr, c), lambda r, p: (p[0], r, 0))),
        out_shape=pltpu.HBM((2, rh, c), F32),
        compiler_params=_cp(("parallel",)),
    )(pos, *_in_hbm([g, recv_a, recv_b]))


def _sibling_allgather(halves, name):
    n = len(halves)

    def body(*refs):
        outs = refs[n:2 * n]
        send_sems, recv_sems = refs[2 * n:]
        x, y, c, _ = _mesh_pos()
        cps = []
        for a in range(n):
            cp = pltpu.make_async_remote_copy(
                src_ref=outs[a].at[c], dst_ref=outs[a].at[c], send_sem=send_sems.at[a], recv_sem=recv_sems.at[a],
                device_id=(x, y, 1 - c), device_id_type=MESH_T)
            cp.start()
            cps.append(cp)
        for a in range(n):
            cps[a].wait_send()
            pltpu.make_async_remote_copy(
                src_ref=outs[a].at[1 - c], dst_ref=outs[a].at[1 - c], send_sem=send_sems.at[a],
                recv_sem=recv_sems.at[a], device_id=(x, y, 1 - c), device_id_type=MESH_T).wait_recv()

    return pl.pallas_call(
        body, name=name, in_specs=[ANY] * n, out_specs=[ANY] * n,
        out_shape=[pltpu.HBM(h.shape, F32) for h in halves],
        input_output_aliases={i: i for i in range(n)},
        scratch_shapes=[pltpu.SemaphoreType.DMA((n,)), pltpu.SemaphoreType.DMA((n,))],
    )(*_in_hbm(halves))


def _peers(x, y, c):
    rel = [(0, 0, 1), (0, 1, 0), (0, 1, 1), (1, 0, 0), (1, 0, 1), (1, 1, 0), (1, 1, 1)]
    return [((1 - x) if dx else x, (1 - y) if dy else y, (1 - c) if dc else c) for dx, dy, dc in rel]


def _small_copy(src, land, k, peer, slot, sems):
    return pltpu.make_async_remote_copy(src_ref=src, dst_ref=land.at[slot], send_sem=sems[0].at[k],
                                        recv_sem=sems[1].at[k], device_id=peer, device_id_type=MESH_T)


def _small_start(part, land, name):
    def body(p_in, l_in, p_ref, l_ref, send_sems, recv_sems, token):
        x, y, c, _ = _mesh_pos()
        for k, peer in enumerate(_peers(x, y, c)):
            _small_copy(p_ref, l_ref, k, peer, 4 * x + 2 * y + c, (send_sems, recv_sems)).start()
        token[...] = jnp.zeros_like(token)

    return pl.pallas_call(
        body, name=name, in_specs=[ANY, ANY],
        out_specs=[ANY, ANY, SEM, SEM, pl.BlockSpec(memory_space=pltpu.VMEM)],
        out_shape=[pltpu.HBM(part.shape, F32), pltpu.HBM(land.shape, F32), pltpu.SemaphoreType.DMA((7,)),
                   pltpu.SemaphoreType.DMA((7,)), jax.ShapeDtypeStruct((8, 128), F32)],
        input_output_aliases={0: 0, 1: 1},
        compiler_params=SPLIT_COPY,
    )(*_in_hbm([part, land]))


def _small_wait(part, land, send_sems, recv_sems, after, name):
    na = len(after)

    def body(*refs):
        ssem, rsem = refs[2], refs[3]
        p_ref, l_ref = refs[4 + na:]
        x, y, c, _ = _mesh_pos()
        for k, peer in enumerate(_peers(x, y, c)):
            cp = _small_copy(p_ref, l_ref, k, peer, 4 * peer[0] + 2 * peer[1] + peer[2], (ssem, rsem))
            cp.wait_send()
            cp.wait_recv()

    return pl.pallas_call(
        body, name=name, in_specs=[ANY, ANY, SEM, SEM] + [ANY] * na, out_specs=[ANY, ANY],
        out_shape=[pltpu.HBM(part.shape, F32), pltpu.HBM(land.shape, F32)],
        input_output_aliases={0: 0, 1: 1},
        compiler_params=SPLIT_COPY,
    )(*_in_hbm([part, land]), send_sems, recv_sems, *after)


def _pack_small(ln1_g, ln1_b, gln_g, gln_b, ln2_g, ln2_b, ln3_g, ln3_b, b_gates, b_s, w_s):
    rows = [ln1_g, ln1_b, gln_g, gln_b, ln2_g, ln2_b, ln3_g, ln3_b]
    rows = [r.reshape(1, D) for r in rows] + [b_gates.reshape(2, D), b_s.reshape(1, D), jnp.zeros((5, D), F32),
                                             w_s.reshape(128, D)]
    return jnp.concatenate(rows, axis=0)


def _unpack_small(p):
    out = [p[i:i + 1] for i in range(8)]
    return out + [p[8:10].reshape(1, 2 * D), p[10:11].reshape(1, 8, BLK), p[16:144].reshape(1, 8, BLK, BLK)]


GROUPS = (("f1g", "f1u", "f1d"), ("w_in",), ("w_ab", "w_gb", "w_out"), ("f2g", "f2u", "f2d"))
LATE_GROUPS = (2, 3)


def _local_step(x, pos_f, target, P, weights_of, grads_ready, flush, small_ready):
    invf = ROPE_THETA ** (-jnp.arange(0, DH, 2, dtype=F32) / DH)
    invf = jnp.tile(invf, 4).reshape(1, 128)
    b_s_t = P["gmlp_b_s"].T

    W = dict(weights_of(0, []))
    h1b, xh1, rstd1, a1, b1, h1t = _ffn_fwd(x, W["f1g"], W["f1u"], W["f1d"], P["ln1_g"], P["ln1_b"], "ffn1_fwd",
                                                emit_t=True)
    W.update(weights_of(1, [h1b]))
    qkv_c = _proj_qkv_rope(h1b, W["w_in"], pos_f, invf, "proj_qkv_rope")
    z = _matmul(h1b, W["w_in"], "nn", "proj_z", n=2 * GW, b_col0=3 * ATT_W, tm=S, tn=512)
    gl = _matmul(h1b, W["w_in"], "nn", "proj_gates", n=2 * D, b_col0=3 * ATT_W + 2 * GW, tm=S, tn=512)
    og = [_attn_fwd(gi, qkv_c[gi], "attn_fwd_g%d" % gi) for gi in range(NG)]
    y_attn, y_attn_t, lse = _attn_combine([o for o, _ in og], [l for _, l in og], "attn_combine")
    y_gmlp, y_gmlp_t = _gmlp_fwd(z, P["gmlp_ln_g"], P["gmlp_ln_b"], P["gmlp_w_s"], b_s_t, "gmlp_fwd")
    W.update(weights_of(2, [y_gmlp]))
    br_a = _matmul(y_attn, W["w_ab"], "nn", "branch_attn", n=D, tm=1024, tn=D)
    br_b = _matmul(y_gmlp, W["w_gb"], "nn", "branch_gmlp", n=D, tm=1024, tn=D)
    merged, merged_t = _merge_fwd(br_a, br_b, gl, P["b_gates"], "merge_fwd")
    h2, h2b, xh2, rstd2 = _resid_ln(xh1, P["ln1_g"], P["ln1_b"], merged, W["w_out"], P["ln2_g"], P["ln2_b"],
                                    "mix_resid_ln2")
    W.update(weights_of(3, [h2b]))
    dr3, a2, b2, dg3, db3, loss = _ffn_fwd(h2, W["f2g"], W["f2u"], W["f2d"], P["ln3_g"], P["ln3_b"],
                                           "ffn2_fwd_loss", target=target)

    g_f2g, g_f2u, g_f2d, dh2 = _ffn_bwd(dr3, h2b, a2, b2, W["f2g"], W["f2u"], W["f2d"], "ffn2_bwd")
    tok = grads_ready(3, dict(f2g=g_f2g, f2u=g_f2u, f2d=g_f2d))
    dr2, dg2, db2 = _ln_bwd(dh2, xh2, rstd2, P["ln2_g"], "ln2_bwd", after=tok)
    g_wout = _wgrad(merged_t, dr2, 128, D, "dw_out", row_sharded=True)
    dmerged = _matmul(dr2, W["w_out"], "nt", "dmerged", n=D, tm=1024, tn=D)
    dab, dbb, dglb, dbg = _merge_bwd(dmerged, br_a, br_b, gl, P["b_gates"], "merge_bwd")
    tok = flush([dab])
    g_wab = _wgrad(y_attn_t, dab, GRP_W // 2, 256, "dw_attn_branch", row_sharded=False, after=tok)
    g_wgb = _wgrad(y_gmlp_t, dbb, 128, D, "dw_gmlp_branch", row_sharded=True)
    tok = grads_ready(2, dict(w_ab=g_wab, w_gb=g_wgb, w_out=g_wout))
    dy_attn = _matmul(dab, W["w_ab"], "nt", "dy_attn", n=GRP_W, tm=1024, tn=GRP_W, after=tok)
    dy_gmlp = _matmul(dbb, W["w_gb"], "nt", "dy_gmlp", n=GW, tm=1024, tn=GW)
    dzb, dws, dbs_t, dgln_g, dgln_b = _gmlp_bwd(z, dy_gmlp, P["gmlp_ln_g"], P["gmlp_ln_b"], P["gmlp_w_s"], b_s_t,
                                                 "gmlp_bwd")
    cls = _class_order([dy_attn, y_attn, lse], "attn_class_order")
    dqkv_c = []
    for gi in range(NG):
        dy_c, y_c, lse_c = [t[None] if gi == 0 else cls[2 * a + gi - 1] for a, t in enumerate((dy_attn, y_attn, lse))]
        dqkv_c.append(_attn_bwd(gi, qkv_c[gi], dy_c, y_c, lse_c, "attn_bwd_g%d" % gi))
    dproj = _rope_bwd(dqkv_c, dzb, dglb, pos_f, invf, "rope_bwd")
    tok = flush([dproj])
    g_win = _wgrad(h1t, dproj, D // 2, IN_SH, "dw_in", row_sharded=False, after=tok)
    tok = grads_ready(1, dict(w_in=g_win))
    dr1, dg1, db1 = _dh1_ln_bwd(dproj, W["w_in"], dr2, xh1, rstd1, P["ln1_g"], "dh1_ln1_bwd", after=tok)
    tok = flush([dr1])
    tok = tok + small_ready(_pack_small(dg1, db1, dgln_g, dgln_b, dg2, db2, dg3, db3, dbg, dbs_t.T, dws))
    g_f1g, g_f1u, g_f1d, dx = _ffn_bwd(dr1, x.astype(BF16), a1, b1, W["f1g"], W["f1u"], W["f1d"], "ffn1_bwd",
                                       after=tok)
    grads_ready(0, dict(f1g=g_f1g, f1u=g_f1u, f1d=g_f1d))
    flush([dx])
    return loss, dx


TRANSPOSED = ("f1g", "f1u", "f2g", "f2u")
KIND = dict(f1g="stack", f1u="stack", f1d="stack", w_in="col", w_ab="col", w_gb="stack", w_out="stack",
            f2g="stack", f2u="stack", f2d="stack")


def kernel(x, positions, ffn1_w_gate, ffn1_w_up, ffn1_w_down, ln1_g, ln1_b, w_in, b_gates, gmlp_ln_g, gmlp_ln_b, gmlp_w_s, gmlp_b_s, w_attn_branch, w_gmlp_branch, w_out, ln2_g, ln2_b, ffn2_w_gate, ffn2_w_up, ffn2_w_down, ln3_g, ln3_b, loss_target, m_ffn1_w_gate, m_ffn1_w_up, m_ffn1_w_down, m_ln1_g, m_ln1_b, m_w_in, m_b_gates, m_gmlp_ln_g, m_gmlp_ln_b, m_gmlp_w_s, m_gmlp_b_s, m_w_attn_branch, m_w_gmlp_branch, m_w_out, m_ln2_g, m_ln2_b, m_ffn2_w_gate, m_ffn2_w_up, m_ffn2_w_down, m_ln3_g, m_ln3_b, v_ffn1_w_gate, v_ffn1_w_up, v_ffn1_w_down, v_ln1_g, v_ln1_b, v_w_in, v_b_gates, v_gmlp_ln_g, v_gmlp_ln_b, v_gmlp_w_s, v_gmlp_b_s, v_w_attn_branch, v_w_gmlp_branch, v_w_out, v_ln2_g, v_ln2_b, v_ffn2_w_gate, v_ffn2_w_up, v_ffn2_w_down, v_ln3_g, v_ln3_b):
    cx, cy, cc = lax.axis_index("x"), lax.axis_index("y"), lax.axis_index("c")
    pos = jnp.stack([cc, 2 * cx + cy, 2 * (1 - cx) + cy, 2 * cx + 1 - cy, 2 * (1 - cx) + 1 - cy]).astype(jnp.int32)

    w_sh = dict(f1g=ffn1_w_gate, f1u=ffn1_w_up, f1d=ffn1_w_down, w_in=w_in, w_ab=w_attn_branch,
                w_gb=w_gmlp_branch, w_out=w_out, f2g=ffn2_w_gate, f2u=ffn2_w_up, f2d=ffn2_w_down)
    m_sh = dict(f1g=m_ffn1_w_gate, f1u=m_ffn1_w_up, f1d=m_ffn1_w_down, w_in=m_w_in, w_ab=m_w_attn_branch,
                w_gb=m_w_gmlp_branch, w_out=m_w_out, f2g=m_ffn2_w_gate, f2u=m_ffn2_w_up, f2d=m_ffn2_w_down)
    v_sh = dict(f1g=v_ffn1_w_gate, f1u=v_ffn1_w_up, f1d=v_ffn1_w_down, w_in=v_w_in, w_ab=v_w_attn_branch,
                w_gb=v_w_gmlp_branch, w_out=v_w_out, f2g=v_ffn2_w_gate, f2u=v_ffn2_w_up, f2d=v_ffn2_w_down)
    w_sh = {k: (v[0].T if k in TRANSPOSED else v[0]) for k, v in w_sh.items()}
    m_sh = {k: (v[0].T if k in TRANSPOSED else v[0]) for k, v in m_sh.items()}
    v_sh = {k: (v[0].T if k in TRANSPOSED else v[0]) for k, v in v_sh.items()}

    started, tokens = [], []
    for gi, names in enumerate(GROUPS):
        placed = [_place_shard(w_sh[k], KIND[k], pos, "place_" + k) for k in names]
        fulls, ssem, rsem, token = _gather_start(placed, [KIND[k] for k in names], [w_sh[k].shape for k in names],
                                                 tokens[-1:], "gather_start_g%d" % gi, both=gi in LATE_GROUPS)
        started.append((fulls, ssem, rsem))
        tokens.append(token)

    def weights_of(gi, after):
        names = GROUPS[gi]
        kinds, dims = [KIND[k] for k in names], [w_sh[k].shape for k in names]
        fulls, ssem, rsem = started[gi]
        fulls = _gather_wait(fulls, ssem, rsem, kinds, dims, list(after) + (tokens if gi == 0 else []),
                             "gather_wait_g%d" % gi, both=gi in LATE_GROUPS)
        if gi not in LATE_GROUPS:
            fulls = _gather_forward(fulls, kinds, dims, "gather_forward_g%d" % gi)
        return {k: (f.reshape(D, D) if k in ("w_gb", "w_out") else f) for k, f in zip(names, fulls)}

    pending, inflight = [], {}

    def grads_ready(gi, gd):
        grads = [gd[k] for k in GROUPS[gi]]
        lands = [lax.empty(g.shape[1:], F32) for g in grads]
        grads, lands, ssem, rsem, token = _pair_start(grads, lands, "rs_pair_start_g%d" % gi)
        pending.append((gi, grads, lands, ssem, rsem))
        return [token]

    def flush(after):
        gi, grads, lands, ssem, rsem = pending.pop()
        names = GROUPS[gi]
        grads, recv_a = _pair_wait(grads, lands, ssem, rsem, after, "rs_pair_wait_g%d" % gi)
        psums = [_pair_sum(g, r, pos, "rs_pair_sum_" + k) for g, r, k in zip(grads, recv_a, names)]
        lands = [lax.empty((3,) + p.shape[1:], BF16) for p in psums]
        psums, lands, ssem, rsem, token = _chip_start(psums, lands, "rs_chip_start_g%d" % gi)
        inflight[gi] = (grads, recv_a, psums, lands, ssem, rsem, token)
        return [token]

    P = dict(ln1_g=ln1_g, ln1_b=ln1_b, ln2_g=ln2_g, ln2_b=ln2_b, ln3_g=ln3_g, ln3_b=ln3_b, b_gates=b_gates,
             gmlp_ln_g=gmlp_ln_g, gmlp_ln_b=gmlp_ln_b, gmlp_w_s=gmlp_w_s[0], gmlp_b_s=gmlp_b_s[0])
    pos_f = positions.reshape(S, 1).astype(F32)
    small_state = []

    def small_ready(packed):
        land = jnp.zeros((8, SMALL_ROWS, D), F32)
        packed, land, ssem, rsem, token = _small_start(packed, land, "small_start")
        small_state.append((packed, land, ssem, rsem))
        return [token]

    loss_part, dx = _local_step(x[0], pos_f, loss_target[0], P, weights_of, grads_ready, flush, small_ready)
    loss = lax.psum(loss_part[0, 0], ("x", "y", "c"))

    g_out, d_out, m_out, v_out = {}, {}, {}, {}

    def finish(gis, after, tag):
        names, halves = [], []
        for gi in gis:
            grads, recv_a, psums, lands, ssem, rsem, token = inflight[gi]
            recv_b = _chip_wait(psums, lands, ssem, rsem, after + [inflight[0][6]], "rs_chip_wait_g%d" % gi)
            halves += [_owner_sum(g, ra, rb, pos, "rs_owner_sum_" + k)
                       for g, ra, rb, k in zip(grads, recv_a, recv_b, GROUPS[gi])]
            names += GROUPS[gi]
            after = halves[-1:]
        reduced = _sibling_allgather(halves, "rs_sibling_allgather_" + tag)
        for k, gfull in zip(names, reduced):
            res = _adamw(w_sh[k], gfull.reshape(w_sh[k].shape), m_sh[k], v_sh[k], "adamw_" + k)
            after = [res[1]]
            if k in TRANSPOSED:
                res = [r.T for r in res]
            g_out[k], d_out[k], m_out[k], v_out[k] = [r[None] for r in res]
        return after

    after = finish((3, 2, 1), [], "g321")

    small, parts = _small_wait(*small_state[0], after, "small_wait")
    sp = (ln1_g, ln1_b, gmlp_ln_g, gmlp_ln_b, ln2_g, ln2_b, ln3_g, ln3_b, b_gates, gmlp_b_s, gmlp_w_s)
    sm = (m_ln1_g, m_ln1_b, m_gmlp_ln_g, m_gmlp_ln_b, m_ln2_g, m_ln2_b, m_ln3_g, m_ln3_b, m_b_gates, m_gmlp_b_s,
          m_gmlp_w_s)
    sv = (v_ln1_g, v_ln1_b, v_gmlp_ln_g, v_gmlp_ln_b, v_ln2_g, v_ln2_b, v_ln3_g, v_ln3_b, v_b_gates, v_gmlp_b_s,
          v_gmlp_w_s)
    sg, sd, smn, svn = _small_sum_adamw(parts, small, pos, _pack_small(*sp), _pack_small(*sm), _pack_small(*sv),
                                        "small_adamw")
    names = ("ln1_g", "ln1_b", "gmlp_ln_g", "gmlp_ln_b", "ln2_g", "ln2_b", "ln3_g", "ln3_b", "b_gates", "gmlp_b_s",
             "gmlp_w_s")
    for dst, packed in ((g_out, sg), (d_out, sd), (m_out, smn), (v_out, svn)):
        for nm, val in zip(names, _unpack_small(packed)):
            dst[nm] = val
    finish((0,), [sg], "g0")

    order = ("f1g", "f1u", "f1d", "ln1_g", "ln1_b", "w_in", "b_gates", "gmlp_ln_g", "gmlp_ln_b", "gmlp_w_s", "gmlp_b_s",
             "w_ab", "w_gb", "w_out", "ln2_g", "ln2_b", "f2g", "f2u", "f2d", "ln3_g", "ln3_b")
    outs = [loss, dx[None]]
    for dst in (g_out, d_out, m_out, v_out):
        outs += [dst[k] for k in order]
    return tuple(outs)
```

```python
import jax
import jax.numpy as jnp
from jax import lax
from jax.experimental import pallas as pl
from jax.experimental.pallas import tpu as pltpu

F32 = jnp.float32
BF16 = jnp.bfloat16

S = 2048
D = 1024
NSH = 4
FSH = 704
ATT_W = 1536
GRP_W = 512
NG = 3
NH = 8
DH = 64
BLK = 128
NBLK = S // BLK
GW = 1024
IN_W = 8704
IN_SH = IN_W // NSH
ALPHA = 2.0 ** 0.25
LN_EPS = 1e-5
ROPE_THETA = 10000.0
DILATIONS = (1, 4, 16)
ADAM_LR, ADAM_B1, ADAM_B2, ADAM_EPS, ADAM_WD, ADAM_STEP = 0.001, 0.9, 0.999, 1e-08, 0.01, 10
SMALL_ROWS = 144
EPI_ROWS = 256
MESH_T = pl.DeviceIdType.MESH
MIB = 1024 * 1024
NEG_INF = float("-inf")


def _cp(sem, vmem_mib=48):
    return pltpu.CompilerParams(dimension_semantics=sem, vmem_limit_bytes=vmem_mib * MIB)


def _ln_stats(r):
    mu = jnp.mean(r, axis=-1, keepdims=True)
    xc = r - mu
    var = jnp.mean(xc * xc, axis=-1, keepdims=True)
    rstd = lax.rsqrt(var + LN_EPS)
    return xc * rstd, rstd


def _ln_dx(dxh, xh, rstd):
    m1 = jnp.mean(dxh, axis=-1, keepdims=True)
    m2 = jnp.mean(dxh * xh, axis=-1, keepdims=True)
    return rstd * (dxh - m1 - xh * m2)


def _dot_nt(a, b):
    return lax.dot_general(a, b, (((1,), (1,)), ((), ())), preferred_element_type=F32)


def _dot_tn(a, b):
    return lax.dot_general(a, b, (((0,), (0,)), ((), ())), preferred_element_type=F32)


def _dot(a, b):
    return jnp.dot(a, b, preferred_element_type=F32)


def _ffn_fwd(xin, wgt, wut, wd, ln_g, ln_b, name, emit_t=False, target=None):
    with_loss = target is not None
    tm = 1024

    def body(x_ref, wg_ref, wu_ref, wd_ref, g_ref, b_ref, *rest):
        if with_loss:
            t_ref, dr_ref, a_ref, bb_ref, dg_ref, db_ref, loss_ref, acc_ref = rest
        elif emit_t:
            hb_ref, xh_ref, rstd_ref, a_ref, bb_ref, ht_ref, acc_ref = rest
        else:
            hb_ref, xh_ref, rstd_ref, a_ref, bb_ref, acc_ref = rest
        i = pl.program_id(0)
        j = pl.program_id(1)
        xb = x_ref[...].astype(BF16)
        a = _dot_nt(xb, wg_ref[...])
        b = _dot_nt(xb, wu_ref[...])
        a_ref[...] = a.astype(BF16)
        bb_ref[...] = b.astype(BF16)
        s = (a * jax.nn.sigmoid(a)) * b
        f = _dot(s.astype(BF16), wd_ref[...])

        @pl.when(j == 0)
        def _():
            acc_ref[...] = f

        @pl.when(j > 0)
        def _():
            acc_ref[...] += f

        if with_loss:
            @pl.when(jnp.logical_and(j == NSH - 1, i == 0))
            def _():
                dg_ref[...] = jnp.zeros_like(dg_ref)
                db_ref[...] = jnp.zeros_like(db_ref)
                loss_ref[...] = jnp.zeros_like(loss_ref)

        @pl.when(j == NSH - 1)
        def _():
            for c0 in range(0, tm, EPI_ROWS):
                rows = slice(c0, c0 + EPI_ROWS)
                r = ALPHA * x_ref[rows, :] + 0.5 * acc_ref[rows, :]
                xh, rstd = _ln_stats(r)
                h = xh * g_ref[...] + b_ref[...]
                if with_loss:
                    err = h - t_ref[rows, :]
                    dy = err * (1.0 / D)
                    dr_ref[rows, :] = _ln_dx(dy * g_ref[...], xh, rstd)
                    dg_ref[...] += jnp.sum(dy * xh, axis=0, keepdims=True)
                    db_ref[...] += jnp.sum(dy, axis=0, keepdims=True)
                    part = 0.5 * jnp.sum(jnp.mean(err * err, axis=-1, keepdims=True), axis=0, keepdims=True)
                    loss_ref[...] += jnp.broadcast_to(part, (8, 128))
                else:
                    hb_ref[rows, :] = h.astype(BF16)
                    xh_ref[rows, :] = xh
                    rstd_ref[rows, :] = rstd
                    if emit_t:
                        ht_ref[:, rows] = h.T.astype(BF16)

    row = pl.BlockSpec((tm, D), lambda i, j: (i, 0))
    vec = pl.BlockSpec((1, D), lambda i, j: (0, 0))
    wsp = pl.BlockSpec((None, FSH, D), lambda i, j: (j, 0, 0))
    ab = pl.BlockSpec((None, tm, FSH), lambda i, j: (j, i, 0))
    ab_shape = jax.ShapeDtypeStruct((NSH, S, FSH), BF16)
    in_specs, args = [row, wsp, wsp, wsp, vec, vec], (xin, wgt, wut, wd, ln_g, ln_b)
    if with_loss:
        in_specs, args = in_specs + [row], args + (target,)
        out_specs = [row, ab, ab, vec, vec, pl.BlockSpec((8, 128), lambda i, j: (0, 0))]
        out_shape = [jax.ShapeDtypeStruct((S, D), F32), ab_shape, ab_shape, jax.ShapeDtypeStruct((1, D), F32),
                     jax.ShapeDtypeStruct((1, D), F32), jax.ShapeDtypeStruct((8, 128), F32)]
    else:
        out_specs = [row, row, pl.BlockSpec((tm, 1), lambda i, j: (i, 0)), ab, ab]
        out_shape = [jax.ShapeDtypeStruct((S, D), BF16), jax.ShapeDtypeStruct((S, D), F32),
                     jax.ShapeDtypeStruct((S, 1), F32), ab_shape, ab_shape]
        if emit_t:
            out_specs.append(pl.BlockSpec((D, tm), lambda i, j: (0, i)))
            out_shape.append(jax.ShapeDtypeStruct((D, S), BF16))
    return pl.pallas_call(
        body, name=name, grid=(S // tm, NSH), in_specs=in_specs, out_specs=out_specs, out_shape=out_shape,
        scratch_shapes=[pltpu.VMEM((tm, D), F32)],
        compiler_params=_cp(("arbitrary" if with_loss else "parallel", "arbitrary"), vmem_mib=56),
    )(*args)


def _ffn_bwd(dr, xin_b, a, b, wgt, wut, wd, name, after=()):
    tm = 512
    ni = S // tm
    hr = FSH // 2

    def body(dr_ref, a_ref, b_ref, wg_ref, wu_ref, wd_ref, x_hbm, *rest):
        dwg_hbm, dwu_hbm, dwd_hbm, dx_hbm, dx_acc, da_all, db_all, s_all, df_all, x_all, res_buf, sems = rest[len(after):]
        j = pl.program_id(0)
        i = pl.program_id(1)
        rows = pl.ds(pl.multiple_of(i * tm, tm), tm)

        @pl.when(jnp.logical_and(j == 0, i == 0))
        def _():
            cp = pltpu.make_async_copy(x_hbm, x_all, sems.at[0])
            cp.start()
            cp.wait()

        drv = dr_ref[...]
        df = (0.5 * drv).astype(BF16)

        @pl.when(j == 0)
        def _():
            df_all[rows, :] = df

        ds = jnp.concatenate([_dot_nt(df, wd_ref[0:384, :]), _dot_nt(df, wd_ref[384:FSH, :])], axis=1)
        av = a_ref[...].astype(F32)
        bv = b_ref[...].astype(F32)
        sig = jax.nn.sigmoid(av)
        sl = av * sig
        da = (ds * bv * (sig * (1.0 + av * (1.0 - sig)))).astype(BF16)
        db = (ds * sl).astype(BF16)
        da_all[rows, :] = da
        db_all[rows, :] = db
        s_all[rows, :] = (sl * bv).astype(BF16)
        dx = _dot(da, wg_ref[...]) + _dot(db, wu_ref[...])

        @pl.when(j == 0)
        def _():
            dx_acc[rows, :] = ALPHA * drv + dx

        @pl.when(j > 0)
        def _():
            dx_acc[rows, :] += dx

        @pl.when(i == ni - 1)
        def _():
            copies = []
            for n, (lhs, rhs, out) in enumerate(((da_all, x_all, dwg_hbm), (db_all, x_all, dwu_hbm),
                                                 (s_all, df_all, dwd_hbm))):
                slot = n % 2
                if n >= 2:
                    for cp in copies[2 * (n - 2): 2 * (n - 2) + 2]:
                        cp.wait()
                res_buf[slot] = _dot_tn(lhs[...], rhs[...])
                for h in range(2):
                    cp = pltpu.make_async_copy(res_buf.at[slot, pl.ds(h * hr, hr), :], out.at[h, j],
                                               sems.at[1 + 2 * slot + h])
                    cp.start()
                    copies.append(cp)
            for cp in copies[2:]:
                cp.wait()

        @pl.when(jnp.logical_and(j == NSH - 1, i == ni - 1))
        def _():
            cp = pltpu.make_async_copy(dx_acc, dx_hbm, sems.at[0])
            cp.start()
            cp.wait()

    row = pl.BlockSpec((tm, D), lambda j, i: (i, 0))
    wsp = pl.BlockSpec((None, FSH, D), lambda j, i: (j, 0, 0))
    ab = pl.BlockSpec((None, tm, FSH), lambda j, i: (j, i, 0))
    dwshape = jax.ShapeDtypeStruct((2, NSH, hr, D), F32)
    return pl.pallas_call(
        body, name=name, grid=(NSH, ni),
        in_specs=[row, ab, ab, wsp, wsp, wsp, ANY] + [ANY] * len(after),
        out_specs=[ANY, ANY, ANY, ANY],
        out_shape=[dwshape, dwshape, dwshape, jax.ShapeDtypeStruct((S, D), F32)],
        scratch_shapes=[pltpu.VMEM((S, D), F32), pltpu.VMEM((S, FSH), BF16), pltpu.VMEM((S, FSH), BF16),
                        pltpu.VMEM((S, FSH), BF16), pltpu.VMEM((S, D), BF16), pltpu.VMEM((S, D), BF16),
                        pltpu.VMEM((2, FSH, D), F32), pltpu.SemaphoreType.DMA((5,))],
        compiler_params=_cp(("arbitrary", "arbitrary"), vmem_mib=58),
    )(dr, a, b, wgt, wut, wd, xin_b, *after)


def _matmul(a, b, mode, name, *, n, tm, tn, b_col0=0, after=()):
    m, k = a.shape
    assert m % tm == 0 and n % tn == 0 and b_col0 % tn == 0
    off = b_col0 // tn
    na = len(after)

    def body(*refs):
        a_ref, b_ref, o_ref = refs[na:]
        av = a_ref[...].astype(BF16)
        o_ref[...] = _dot(av, b_ref[...]) if mode == "nn" else _dot_nt(av, b_ref[...])

    if mode == "nn":
        b_spec = pl.BlockSpec((k, tn), lambda i, j: (0, j + off))
    else:
        b_spec = pl.BlockSpec((tn, k), lambda i, j: (j, 0))
    return pl.pallas_call(
        body, name=name, grid=(m // tm, n // tn),
        in_specs=[pl.BlockSpec(memory_space=pl.ANY)] * na + [pl.BlockSpec((tm, k), lambda i, j: (i, 0)), b_spec],
        out_specs=pl.BlockSpec((tm, tn), lambda i, j: (i, j)),
        out_shape=jax.ShapeDtypeStruct((m, n), F32),
        compiler_params=_cp(("parallel", "parallel")),
    )(*after, a, b)


def _wgrad(xt, y, rh, c, name, row_sharded, after=()):
    na = len(after)
    if row_sharded:
        def body(x_ref, y_ref, *rest):
            o_ref = rest[na]
            res = _dot(x_ref[...], y_ref[...].astype(BF16))
            for j in range(NSH):
                for h in range(2):
                    o_ref[h, j] = res[(2 * j + h) * rh:(2 * j + h + 1) * rh, :]

        grid = (1,)
        in_specs = [pl.BlockSpec((2 * NSH * rh, S), lambda g: (0, 0)), pl.BlockSpec((S, c), lambda g: (0, 0))]
        out_specs = pl.BlockSpec((2, NSH, rh, c), lambda g: (0, 0, 0, 0))
        sem = ("arbitrary",)
    else:
        def body(x_ref, y_ref, *rest):
            rest[na][...] = _dot(x_ref[...], y_ref[...].astype(BF16))

        grid = (2, NSH)
        in_specs = [pl.BlockSpec((rh, S), lambda h, j: (h, 0)), pl.BlockSpec((S, c), lambda h, j: (0, j))]
        out_specs = pl.BlockSpec((None, None, rh, c), lambda h, j: (h, j, 0, 0))
        sem = ("parallel", "parallel")
    return pl.pallas_call(
        body, name=name, grid=grid, in_specs=in_specs + [pl.BlockSpec(memory_space=pl.ANY)] * na, out_specs=out_specs,
        out_shape=jax.ShapeDtypeStruct((2, NSH, rh, c), F32),
        compiler_params=_cp(sem, vmem_mib=56),
    )(xt, y, *after)


def _resid_ln(res_xh, res_g, res_b, a, w, ln_g, ln_b, name):
    tm = 512

    def body(rx_ref, rg_ref, rb_ref, a_ref, w_ref, g_ref, b_ref, h_ref, hb_ref, xh_ref, rstd_ref):
        r = ALPHA * (rx_ref[...] * rg_ref[...] + rb_ref[...]) + _dot(a_ref[...], w_ref[...])
        xh, rstd = _ln_stats(r)
        h = xh * g_ref[...] + b_ref[...]
        h_ref[...] = h
        hb_ref[...] = h.astype(BF16)
        xh_ref[...] = xh
        rstd_ref[...] = rstd

    row = pl.BlockSpec((tm, D), lambda i: (i, 0))
    vec = pl.BlockSpec((1, D), lambda i: (0, 0))
    return pl.pallas_call(
        body, name=name, grid=(S // tm,),
        in_specs=[row, vec, vec, row, pl.BlockSpec((D, D), lambda i: (0, 0)), vec, vec],
        out_specs=[row, row, row, pl.BlockSpec((tm, 1), lambda i: (i, 0))],
        out_shape=[jax.ShapeDtypeStruct((S, D), F32), jax.ShapeDtypeStruct((S, D), BF16),
                   jax.ShapeDtypeStruct((S, D), F32), jax.ShapeDtypeStruct((S, 1), F32)],
        compiler_params=_cp(("parallel",)),
    )(res_xh, res_g, res_b, a, w, ln_g, ln_b)


def _dh1_ln_bwd(dproj, w_in, dr2, xh, rstd, ln_g, name, after=()):
    tm, tk, ch = 1024, IN_SH, EPI_ROWS
    nk = IN_W // tk
    na = len(after)

    def body(*refs):
        a_ref, b_ref, add_ref, xh_ref, rstd_ref, g_ref, dr_ref, dg_ref, db_ref, acc_ref = refs[na:]
        i = pl.program_id(0)
        k = pl.program_id(1)
        p = _dot_nt(a_ref[...], b_ref[...])

        @pl.when(k == 0)
        def _():
            acc_ref[...] = p

        @pl.when(k > 0)
        def _():
            acc_ref[...] += p

        @pl.when(jnp.logical_and(k == nk - 1, i == 0))
        def _():
            dg_ref[...] = jnp.zeros_like(dg_ref)
            db_ref[...] = jnp.zeros_like(db_ref)

        @pl.when(k == nk - 1)
        def _():
            for c0 in range(0, tm, ch):
                rows = slice(c0, c0 + ch)
                dy = acc_ref[rows, :] + ALPHA * add_ref[rows, :]
                xhv = xh_ref[rows, :]
                dr_ref[rows, :] = _ln_dx(dy * g_ref[...], xhv, rstd_ref[rows, :])
                dg_ref[...] += jnp.sum(dy * xhv, axis=0, keepdims=True)
                db_ref[...] += jnp.sum(dy, axis=0, keepdims=True)

    row = pl.BlockSpec((tm, D), lambda i, k: (i, 0))
    vec = pl.BlockSpec((1, D), lambda i, k: (0, 0))
    return pl.pallas_call(
        body, name=name, grid=(S // tm, nk),
        in_specs=[pl.BlockSpec(memory_space=pl.ANY)] * na
        + [pl.BlockSpec((tm, tk), lambda i, k: (i, k)), pl.BlockSpec((D, tk), lambda i, k: (0, k)), row, row,
           pl.BlockSpec((tm, 1), lambda i, k: (i, 0)), vec],
        out_specs=[row, vec, vec],
        out_shape=[jax.ShapeDtypeStruct((S, D), F32), jax.ShapeDtypeStruct((1, D), F32),
                   jax.ShapeDtypeStruct((1, D), F32)],
        scratch_shapes=[pltpu.VMEM((tm, D), F32)],
        compiler_params=_cp(("arbitrary", "arbitrary"), vmem_mib=56),
    )(*after, dproj, w_in, dr2, xh, rstd, ln_g)


def _ln_bwd(dout, xh, rstd, ln_g, name, after=()):
    tm = 512
    na = len(after)

    def body(*refs):
        y_ref, xh_ref, rstd_ref, g_ref, dr_ref, dg_ref, db_ref = refs[na:]
        dy = y_ref[...]
        i = pl.program_id(0)
        xh = xh_ref[...]
        dr_ref[...] = _ln_dx(dy * g_ref[...], xh, rstd_ref[...])
        dg = jnp.sum(dy * xh, axis=0, keepdims=True)
        db = jnp.sum(dy, axis=0, keepdims=True)

        @pl.when(i == 0)
        def _():
            dg_ref[...] = dg
            db_ref[...] = db

        @pl.when(i > 0)
        def _():
            dg_ref[...] += dg
            db_ref[...] += db

    row = pl.BlockSpec((tm, D), lambda i: (i, 0))
    vec = pl.BlockSpec((1, D), lambda i: (0, 0))
    return pl.pallas_call(
        body, name=name, grid=(S // tm,),
        in_specs=[pl.BlockSpec(memory_space=pl.ANY)] * na + [row, row, pl.BlockSpec((tm, 1), lambda i: (i, 0)), vec],
        out_specs=[row, vec, vec],
        out_shape=[jax.ShapeDtypeStruct((S, D), F32), jax.ShapeDtypeStruct((1, D), F32),
                   jax.ShapeDtypeStruct((1, D), F32)],
        compiler_params=_cp(("arbitrary",)),
    )(*after, dout, xh, rstd, ln_g)


ROPE_TM = 256


def _rope_tables(pos_ref, invf_ref, sign):
    ang = pos_ref[...] * invf_ref[...]
    lane = lax.broadcasted_iota(jnp.int32, ang.shape, 1)
    first = (lane % DH) < (DH // 2)
    sinv = jnp.sin(ang) * sign
    return first, jnp.cos(ang), jnp.where(first, -sinv, sinv)


def _rotate(x, first, cosf, sinf):
    return x * cosf + jnp.where(first, pltpu.roll(x, 96, 1), pltpu.roll(x, 32, 1)) * sinf


def _proj_qkv_rope(hb, w_in, pos_f, invf, name):
    tm = 2 * ROPE_TM

    def body(h_ref, w_ref, pos_ref, invf_ref, o0_ref, o1_ref, o2_ref, buf_ref):
        rot = pl.program_id(1) < 2
        first, cosf, sinf = _rope_tables(pos_ref, invf_ref, 1.0)
        cosf = jnp.where(rot, cosf, 1.0)
        sinf = jnp.where(rot, sinf, 0.0)
        acc = _dot(h_ref[...], w_ref[...])
        for gi, (d, o_ref) in enumerate(zip(DILATIONS, (o0_ref, o1_ref, o2_ref))):
            for ch in range(GRP_W // 128):
                cols = slice(ch * 128, (ch + 1) * 128)
                x = _rotate(acc[:, gi * GRP_W + ch * 128: gi * GRP_W + (ch + 1) * 128], first, cosf, sinf)
                if d == 1:
                    o_ref[0, :, cols] = x.astype(BF16)
                else:
                    buf_ref[...] = x
                    for r in range(d):
                        o_ref[r, :, cols] = buf_ref[pl.ds(r, tm // d, stride=d), :].astype(BF16)

    return pl.pallas_call(
        body, name=name, grid=(S // tm, 3),
        in_specs=[pl.BlockSpec((tm, D), lambda i, s: (i, 0)), pl.BlockSpec((D, ATT_W), lambda i, s: (0, s)),
                  pl.BlockSpec((tm, 1), lambda i, s: (i, 0)), pl.BlockSpec((1, 128), lambda i, s: (0, 0))],
        out_specs=[pl.BlockSpec((d, tm // d, GRP_W), lambda i, s: (0, i, s)) for d in DILATIONS],
        out_shape=[jax.ShapeDtypeStruct((d, S // d, 3 * GRP_W), BF16) for d in DILATIONS],
        scratch_shapes=[pltpu.VMEM((tm, 128), F32)],
        compiler_params=_cp(("parallel", "parallel")),
    )(hb, w_in, pos_f, invf)


def _rope_bwd(dqkv_c, dz, dgl, pos_f, invf, name):
    tm = ROPE_TM

    def body(*refs):
        g_refs, (dz_ref, dgl_ref, pos_ref, invf_ref, o_ref, buf_ref) = refs[:9], refs[9:]
        o_ref[:, 3 * ATT_W:3 * ATT_W + 2 * GW] = dz_ref[...]
        o_ref[:, 3 * ATT_W + 2 * GW:IN_W] = dgl_ref[...]
        first, cosf, sinf = _rope_tables(pos_ref, invf_ref, -1.0)
        for sec in range(3):
            for gi, d in enumerate(DILATIONS):
                g_ref = g_refs[3 * gi + sec]
                for ch in range(GRP_W // 128):
                    cols = slice(ch * 128, (ch + 1) * 128)
                    if d == 1:
                        x = g_ref[0, :, cols]
                    else:
                        for r in range(d):
                            buf_ref[pl.ds(r, tm // d, stride=d), :] = g_ref[r, :, cols]
                        x = buf_ref[...]
                    if sec < 2:
                        x = _rotate(x, first, cosf, sinf)
                    dst = sec * ATT_W + gi * GRP_W + ch * 128
                    o_ref[:, dst:dst + 128] = x.astype(BF16)

    g_specs = [pl.BlockSpec((d, tm // d, GRP_W), lambda i: (0, i, 0)) for d in DILATIONS for _ in range(3)]
    return pl.pallas_call(
        body, name=name, grid=(S // tm,),
        in_specs=g_specs + [pl.BlockSpec((tm, 2 * GW), lambda i: (i, 0)), pl.BlockSpec((tm, 2 * D), lambda i: (i, 0)),
                            pl.BlockSpec((tm, 1), lambda i: (i, 0)), pl.BlockSpec((1, 128), lambda i: (0, 0))],
        out_specs=pl.BlockSpec((tm, IN_W), lambda i: (i, 0)),
        out_shape=jax.ShapeDtypeStruct((S, IN_W), BF16),
        scratch_shapes=[pltpu.VMEM((tm, 128), F32)],
        compiler_params=_cp(("parallel",)),
    )(*[g for grp in dqkv_c for g in grp], dz, dgl, pos_f, invf)


def _class_order(ts, name):
    tm = ROPE_TM
    n = len(ts)

    def body(*refs):
        buf_ref = refs[3 * n]
        for a in range(n):
            for ch in range(GRP_W // 128):
                cols = slice(ch * 128, (ch + 1) * 128)
                buf_ref[...] = refs[a][:, cols]
                for b, d in enumerate(DILATIONS[1:]):
                    for r in range(d):
                        refs[n + 2 * a + b][r, :, cols] = buf_ref[pl.ds(r, tm // d, stride=d), :]

    return pl.pallas_call(
        body, name=name, grid=(S // tm,),
        in_specs=[pl.BlockSpec((tm, GRP_W), lambda i: (i, 0))] * n,
        out_specs=[pl.BlockSpec((d, tm // d, GRP_W), lambda i: (0, i, 0)) for _ in range(n) for d in DILATIONS[1:]],
        out_shape=[jax.ShapeDtypeStruct((d, S // d, GRP_W), F32) for _ in range(n) for d in DILATIONS[1:]],
        scratch_shapes=[pltpu.VMEM((tm, 128), F32)],
        compiler_params=_cp(("parallel",)),
    )(*ts)


def _own_lanes(h):
    return (lax.broadcasted_iota(jnp.int32, (1, 2 * DH), 1) // DH) == (h % 2)


def _heads(ref):
    out = []
    for h in range(NH):
        pair = ref[:, (h // 2) * 2 * DH:(h // 2 + 1) * 2 * DH]
        out.append(jnp.where(_own_lanes(h), pair, jnp.zeros_like(pair)))
    return jnp.stack(out)


def _unheads(t3):
    return jnp.concatenate([t3[2 * p] + t3[2 * p + 1] for p in range(NH // 2)], axis=1)


def _bdot_nt(a, b):
    return lax.dot_general(a, b, (((2,), (2,)), ((0,), (0,))), preferred_element_type=F32)


def _bdot(a, b):
    return lax.dot_general(a, b, (((2,), (1,)), ((0,), (0,))), preferred_element_type=F32)


def _bdot_tn(a, b):
    return lax.dot_general(a, b, (((1,), (1,)), ((0,), (0,))), preferred_element_type=F32)


def _attn_fwd(gi, qkv_c, name):
    d = DILATIONS[gi]
    nblk = S // d // BLK

    def body(*refs):
        if nblk > 1:
            q_ref, kc_ref, kp_ref, vc_ref, vp_ref, o_ref, lse_ref = refs
            has_prev = pl.program_id(1) != 0
        else:
            q_ref, kc_ref, vc_ref, o_ref, lse_ref = refs
        qi = lax.broadcasted_iota(jnp.int32, (NH, BLK, BLK), 1)
        kj = lax.broadcasted_iota(jnp.int32, (NH, BLK, BLK), 2)
        q = _heads(q_ref)
        sc = jnp.where(kj <= qi, _bdot_nt(q, _heads(kc_ref)) * 0.125, NEG_INF)
        m = jnp.max(sc, axis=-1, keepdims=True)
        if nblk > 1:
            mask_p = jnp.logical_and(kj >= qi, has_prev)
            sp = jnp.where(mask_p, _bdot_nt(q, _heads(kp_ref)) * 0.125, NEG_INF)
            m = jnp.maximum(m, jnp.max(sp, axis=-1, keepdims=True))
        pc = jnp.exp(sc - m)
        l = jnp.sum(pc, axis=-1, keepdims=True)
        o = _bdot(pc.astype(BF16), _heads(vc_ref))
        if nblk > 1:
            pp = jnp.exp(sp - m)
            l = l + jnp.sum(pp, axis=-1, keepdims=True)
            o = o + _bdot(pp.astype(BF16), _heads(vp_ref))
        o_ref[...] = _unheads(o / l)
        lse = jnp.broadcast_to(m + jnp.log(l), (NH, BLK, 2 * DH))
        lse_ref[...] = _unheads(jnp.stack([jnp.where(_own_lanes(h), lse[h], 0.0) for h in range(NH)]))

    def cur(sec):
        return pl.BlockSpec((None, BLK, GRP_W), lambda r, n: (r, n, sec))

    def prev(sec):
        return pl.BlockSpec((None, BLK, GRP_W), lambda r, n: (r, jnp.maximum(n - 1, 0), sec))

    out = pl.BlockSpec((None, BLK, GRP_W), lambda r, n: (r, n, 0))
    shp = jax.ShapeDtypeStruct((d, S // d, GRP_W), F32)
    if nblk > 1:
        in_specs, args = [cur(0), cur(1), prev(1), cur(2), prev(2)], (qkv_c,) * 5
    else:
        in_specs, args = [cur(0), cur(1), cur(2)], (qkv_c,) * 3
    return pl.pallas_call(
        body, name=name, grid=(d, nblk), in_specs=in_specs, out_specs=[out, out], out_shape=[shp, shp],
        compiler_params=_cp(("parallel", "parallel")),
    )(*args)


def _attn_combine(os, lses, name):
    tm = ROPE_TM

    def body(o0_ref, o1_ref, o2_ref, l0_ref, l1_ref, l2_ref, y_ref, yt_ref, l_ref, buf_ref):
        def token_order(ref, d, cols, slot):
            if d == 1:
                return ref[0, :, cols]
            for r in range(d):
                buf_ref[slot, pl.ds(r, tm // d, stride=d), :] = ref[r, :, cols]
            return buf_ref[slot]

        for ch in range(GRP_W // 128):
            cols = slice(ch * 128, (ch + 1) * 128)
            o = [token_order(ref, d, cols, k) for k, (ref, d) in enumerate(zip((o0_ref, o1_ref, o2_ref), DILATIONS))]
            ls = [token_order(ref, d, cols, 3 + k)
                  for k, (ref, d) in enumerate(zip((l0_ref, l1_ref, l2_ref), DILATIONS))]
            m = jnp.maximum(jnp.maximum(ls[0], ls[1]), ls[2])
            e = [jnp.exp(l - m) for l in ls]
            den = e[0] + e[1] + e[2]
            y = (e[0] * o[0] + e[1] * o[1] + e[2] * o[2]) / den
            y_ref[:, cols] = y
            yt_ref[cols, :] = y.T.astype(BF16)
            l_ref[:, cols] = m + jnp.log(den)

    blk = pl.BlockSpec((tm, GRP_W), lambda i: (i, 0))
    cls = [pl.BlockSpec((d, tm // d, GRP_W), lambda i: (0, i, 0)) for d in DILATIONS]
    shp = jax.ShapeDtypeStruct((S, GRP_W), F32)
    return pl.pallas_call(
        body, name=name, grid=(S // tm,), in_specs=cls + cls,
        out_specs=[blk, pl.BlockSpec((GRP_W, tm), lambda i: (0, i)), blk],
        out_shape=[shp, jax.ShapeDtypeStruct((GRP_W, S), BF16), shp],
        scratch_shapes=[pltpu.VMEM((6, tm, 128), F32)],
        compiler_params=_cp(("parallel",)),
    )(*os, *lses)


def _attn_bwd(gi, qkv_c, dy_c, y_c, lse_c, name):
    d = DILATIONS[gi]
    nblk = S // d // BLK

    def body(*refs):
        if nblk > 1:
            (q_ref, qn_ref, k_ref, kp_ref, v_ref, vp_ref, dy_ref, dyn_ref, y_ref, yn_ref, l_ref, ln_ref,
             dq_ref, dk_ref, dv_ref) = refs
            n = pl.program_id(1)
            has_prev = n != 0
            has_next = n != nblk - 1
        else:
            q_ref, k_ref, v_ref, dy_ref, y_ref, l_ref, dq_ref, dk_ref, dv_ref = refs
        qi = lax.broadcasted_iota(jnp.int32, (NH, BLK, BLK), 1)
        kj = lax.broadcasted_iota(jnp.int32, (NH, BLK, BLK), 2)

        def lse_col(ref):
            return jnp.stack([ref[:, h * DH:h * DH + 1] for h in range(NH)])

        q, k, v = _heads(q_ref), _heads(k_ref), _heads(v_ref)
        dy = _heads(dy_ref)
        dd = jnp.sum(dy * _heads(y_ref), axis=-1, keepdims=True)
        lcol = lse_col(l_ref)
        dyb = dy.astype(BF16)
        p = jnp.exp(jnp.where(kj <= qi, _bdot_nt(q, k) * 0.125, NEG_INF) - lcol)
        ds = (p * (_bdot_nt(dyb, v) - dd)).astype(BF16)
        dq = _bdot(ds, k)
        dk = _bdot_tn(ds, q)
        dv = _bdot_tn(p.astype(BF16), dyb)
        if nblk > 1:
            qn, kpv, vpv = _heads(qn_ref), _heads(kp_ref), _heads(vp_ref)
            dyn = _heads(dyn_ref)
            ddn = jnp.sum(dyn * _heads(yn_ref), axis=-1, keepdims=True)
            lncol = lse_col(ln_ref)
            dynb = dyn.astype(BF16)
            mask_p = jnp.logical_and(kj >= qi, has_prev)
            pp = jnp.exp(jnp.where(mask_p, _bdot_nt(q, kpv) * 0.125, NEG_INF) - lcol)
            dsp = (pp * (_bdot_nt(dyb, vpv) - dd)).astype(BF16)
            dq = dq + _bdot(dsp, kpv)
            mask_n = jnp.logical_and(kj >= qi, has_next)
            pn = jnp.exp(jnp.where(mask_n, _bdot_nt(qn, k) * 0.125, NEG_INF) - lncol)
            dsn = (pn * (_bdot_nt(dynb, v) - ddn)).astype(BF16)
            dk = dk + _bdot_tn(dsn, qn)
            dv = dv + _bdot_tn(pn.astype(BF16), dynb)
        dq_ref[...] = _unheads(dq) * 0.125
        dk_ref[...] = _unheads(dk) * 0.125
        dv_ref[...] = _unheads(dv)

    def spec(sec, shift):
        def idx(r, n):
            return (r, jnp.clip(n + shift, 0, nblk - 1), sec)
        return pl.BlockSpec((None, BLK, GRP_W), idx)

    if nblk > 1:
        in_specs = [spec(0, 0), spec(0, 1), spec(1, 0), spec(1, -1), spec(2, 0), spec(2, -1),
                    spec(0, 0), spec(0, 1), spec(0, 0), spec(0, 1), spec(0, 0), spec(0, 1)]
        args = (qkv_c,) * 6 + (dy_c, dy_c, y_c, y_c, lse_c, lse_c)
    else:
        in_specs = [spec(0, 0), spec(1, 0), spec(2, 0), spec(0, 0), spec(0, 0), spec(0, 0)]
        args = (qkv_c, qkv_c, qkv_c, dy_c, y_c, lse_c)
    out = spec(0, 0)
    shp = jax.ShapeDtypeStruct((d, S // d, GRP_W), F32)
    return pl.pallas_call(
        body, name=name, grid=(d, nblk), in_specs=in_specs, out_specs=[out, out, out], out_shape=[shp, shp, shp],
        compiler_params=_cp(("parallel", "parallel")),
    )(*args)


_SQRT_HALF = 0.7071067811865476
_INV_SQRT_2PI = 0.3989422804014327


def _gelu(z):
    return 0.5 * z * (1.0 + lax.erf(z * _SQRT_HALF))


def _gelu_grad(z):
    return 0.5 * (1.0 + lax.erf(z * _SQRT_HALF)) + z * (jnp.exp(-0.5 * z * z) * _INV_SQRT_2PI)


def _tril_mask():
    t = lax.broadcasted_iota(jnp.int32, (BLK, BLK), 0)
    s = lax.broadcasted_iota(jnp.int32, (BLK, BLK), 1)
    return s <= t


def _groups(t):
    return jnp.stack([t[:, g * BLK:(g + 1) * BLK] for g in range(8)])


def _ungroup(t3):
    return jnp.concatenate([t3[g] for g in range(8)], axis=1)


def _group_bias(bs_ref):
    return jnp.stack([bs_ref[:, g:g + 1] for g in range(8)])


def _gmlp_fwd(z, ln_g, ln_b, w_s, b_s_t, name):
    def body(z_ref, g_ref, b_ref, ws_ref, bs_ref, y_ref, yt_ref):
        zg = _gelu(z_ref[...])
        u = zg[:, :GW]
        xh, _ = _ln_stats(zg[:, GW:])
        vn = (xh * g_ref[...] + b_ref[...]).astype(BF16)
        wt = jnp.where(_tril_mask(), ws_ref[...], 0.0).astype(BF16)
        yv = u * _ungroup(_bdot(wt, _groups(vn)) + _group_bias(bs_ref))
        y_ref[...] = yv.astype(BF16)
        yt_ref[...] = yv.T.astype(BF16)

    vec = pl.BlockSpec((1, GW), lambda n: (0, 0))
    return pl.pallas_call(
        body, name=name, grid=(NBLK,),
        in_specs=[pl.BlockSpec((BLK, 2 * GW), lambda n: (n, 0)), vec, vec,
                  pl.BlockSpec((8, BLK, BLK), lambda n: (0, 0, 0)), pl.BlockSpec((BLK, 8), lambda n: (0, 0))],
        out_specs=[pl.BlockSpec((BLK, GW), lambda n: (n, 0)), pl.BlockSpec((GW, BLK), lambda n: (0, n))],
        out_shape=[jax.ShapeDtypeStruct((S, GW), BF16), jax.ShapeDtypeStruct((GW, S), BF16)],
        compiler_params=_cp(("parallel",)),
    )(z, ln_g, ln_b, w_s, b_s_t)


def _gmlp_bwd(z, dy, ln_g, ln_b, w_s, b_s_t, name):
    def body(z_ref, dy_ref, g_ref, b_ref, ws_ref, bs_ref, dz_ref, dws_ref, dbs_ref, dg_ref, db_ref, dvn_ref):
        n = pl.program_id(0)
        zv = z_ref[...]
        zg = _gelu(zv)
        u = zg[:, :GW]
        xh, rstd = _ln_stats(zg[:, GW:])
        vn = (xh * g_ref[...] + b_ref[...]).astype(BF16)
        tril = _tril_mask()

        @pl.when(n == 0)
        def _():
            dws_ref[...] = jnp.zeros_like(dws_ref)
            dbs_ref[...] = jnp.zeros_like(dbs_ref)
            dg_ref[...] = jnp.zeros_like(dg_ref)
            db_ref[...] = jnp.zeros_like(db_ref)

        wt = jnp.where(tril, ws_ref[...], 0.0).astype(BF16)
        vn3 = _groups(vn)
        dyv = dy_ref[...]
        mixed = _ungroup(_bdot(wt, vn3) + _group_bias(bs_ref))
        dz_ref[:, :GW] = (dyv * mixed * _gelu_grad(zv[:, :GW])).astype(BF16)
        dmix3 = _groups(dyv * u)
        dmb = dmix3.astype(BF16)
        dws_ref[...] += jnp.where(tril, _bdot_nt(dmb, vn3), 0.0)
        dbsum = jnp.sum(dmix3, axis=-1, keepdims=True)
        for gg in range(8):
            dbs_ref[:, gg:gg + 1] += dbsum[gg]
        dvn_ref[...] = _ungroup(_bdot_tn(wt, dmb))

        dvn = dvn_ref[...]
        dg_ref[...] += jnp.sum(dvn * xh, axis=0, keepdims=True)
        db_ref[...] += jnp.sum(dvn, axis=0, keepdims=True)
        dvg = _ln_dx(dvn * g_ref[...], xh, rstd)
        dz_ref[:, GW:] = (dvg * _gelu_grad(zv[:, GW:])).astype(BF16)

    vec = pl.BlockSpec((1, GW), lambda n: (0, 0))
    ws = pl.BlockSpec((8, BLK, BLK), lambda n: (0, 0, 0))
    bs = pl.BlockSpec((BLK, 8), lambda n: (0, 0))
    return pl.pallas_call(
        body, name=name, grid=(NBLK,),
        in_specs=[pl.BlockSpec((BLK, 2 * GW), lambda n: (n, 0)), pl.BlockSpec((BLK, GW), lambda n: (n, 0)),
                  vec, vec, ws, bs],
        out_specs=[pl.BlockSpec((BLK, 2 * GW), lambda n: (n, 0)), ws, bs, vec, vec],
        out_shape=[jax.ShapeDtypeStruct((S, 2 * GW), BF16), jax.ShapeDtypeStruct((8, BLK, BLK), F32),
                   jax.ShapeDtypeStruct((BLK, 8), F32), jax.ShapeDtypeStruct((1, GW), F32),
                   jax.ShapeDtypeStruct((1, GW), F32)],
        scratch_shapes=[pltpu.VMEM((BLK, GW), F32)],
        compiler_params=_cp(("arbitrary",)),
    )(z, dy, ln_g, ln_b, w_s, b_s_t)


def _merge_fwd(a, b, gl, b_gates, name):
    tm = 512

    def body(a_ref, b_ref, g0_ref, g1_ref, bg_ref, o_ref, ot_ref):
        g0 = jax.nn.sigmoid(g0_ref[...] + bg_ref[:, :D])
        g1 = jax.nn.sigmoid(g1_ref[...] + bg_ref[:, D:])
        mg = g0 * a_ref[...] + g1 * b_ref[...]
        o_ref[...] = mg.astype(BF16)
        ot_ref[...] = mg.T.astype(BF16)

    row = pl.BlockSpec((tm, D), lambda i: (i, 0))
    return pl.pallas_call(
        body, name=name, grid=(S // tm,),
        in_specs=[row, row, row, pl.BlockSpec((tm, D), lambda i: (i, 1)), pl.BlockSpec((1, 2 * D), lambda i: (0, 0))],
        out_specs=[row, pl.BlockSpec((D, tm), lambda i: (0, i))],
        out_shape=[jax.ShapeDtypeStruct((S, D), BF16), jax.ShapeDtypeStruct((D, S), BF16)],
        compiler_params=_cp(("parallel",)),
    )(a, b, gl, gl, b_gates)


def _merge_bwd(dm, a, b, gl, b_gates, name):
    tm = 512

    def body(dm_ref, a_ref, b_ref, g0_ref, g1_ref, bg_ref, da_ref, db_ref, dgl_ref, dbg_ref):
        i = pl.program_id(0)
        dmv = dm_ref[...]
        g0 = jax.nn.sigmoid(g0_ref[...] + bg_ref[:, :D])
        g1 = jax.nn.sigmoid(g1_ref[...] + bg_ref[:, D:])
        da_ref[...] = (dmv * g0).astype(BF16)
        db_ref[...] = (dmv * g1).astype(BF16)
        d0 = dmv * a_ref[...] * g0 * (1.0 - g0)
        d1 = dmv * b_ref[...] * g1 * (1.0 - g1)
        dgl_ref[:, :D] = d0.astype(BF16)
        dgl_ref[:, D:] = d1.astype(BF16)
        s0 = jnp.sum(d0, axis=0, keepdims=True)
        s1 = jnp.sum(d1, axis=0, keepdims=True)

        @pl.when(i == 0)
        def _():
            dbg_ref[:, :D] = s0
            dbg_ref[:, D:] = s1

        @pl.when(i > 0)
        def _():
            dbg_ref[:, :D] += s0
            dbg_ref[:, D:] += s1

    row = pl.BlockSpec((tm, D), lambda i: (i, 0))
    wide = pl.BlockSpec((tm, 2 * D), lambda i: (i, 0))
    bg = pl.BlockSpec((1, 2 * D), lambda i: (0, 0))
    return pl.pallas_call(
        body, name=name, grid=(S // tm,),
        in_specs=[row, row, row, row, pl.BlockSpec((tm, D), lambda i: (i, 1)), bg],
        out_specs=[row, row, wide, bg],
        out_shape=[jax.ShapeDtypeStruct((S, D), BF16), jax.ShapeDtypeStruct((S, D), BF16),
                   jax.ShapeDtypeStruct((S, 2 * D), BF16), jax.ShapeDtypeStruct((1, 2 * D), F32)],
        compiler_params=_cp(("arbitrary",)),
    )(dm, a, b, gl, gl, b_gates)


def _adam_math(w, g, m, v):
    m2 = ADAM_B1 * m + (1.0 - ADAM_B1) * g
    v2 = ADAM_B2 * v + (1.0 - ADAM_B2) * (g * g)
    m_hat = m2 / (1.0 - ADAM_B1 ** ADAM_STEP)
    v_hat = v2 / (1.0 - ADAM_B2 ** ADAM_STEP)
    delta = -ADAM_LR * (m_hat / (jnp.sqrt(v_hat) + ADAM_EPS) + ADAM_WD * w)
    return delta, m2, v2


def _pick_rows(rows, cols, unit=16, budget=2 * MIB):
    best = unit
    for t in range(unit, rows + 1, unit):
        if rows % t == 0 and t * cols * 4 <= budget:
            best = t
    assert rows % best == 0
    return best


def _adamw(w, g, m, v, name):
    r, c = w.shape
    tr = _pick_rows(r, c, unit=8)

    def body(w_ref, g_ref, m_ref, v_ref, go_ref, d_ref, mo_ref, vo_ref):
        gv = g_ref[...]
        delta, m2, v2 = _adam_math(w_ref[...], gv, m_ref[...], v_ref[...])
        go_ref[...] = gv
        d_ref[...] = delta
        mo_ref[...] = m2
        vo_ref[...] = v2

    blk = pl.BlockSpec((tr, c), lambda i: (i, 0))
    shp = jax.ShapeDtypeStruct((r, c), F32)
    return pl.pallas_call(
        body, name=name, grid=(r // tr,), in_specs=[blk] * 4, out_specs=[blk] * 4, out_shape=[shp] * 4,
        compiler_params=_cp(("parallel",)),
    )(*[pltpu.with_memory_space_constraint(t, pltpu.HBM) for t in (w, g, m, v)])


def _small_sum_adamw(parts, own, pos, w, m, v, name):
    tr = 48

    def body(pos_ref, p_ref, own_ref, w_ref, m_ref, v_ref, g_ref, d_ref, mo_ref, vo_ref):
        me = 2 * pos_ref[1] + pos_ref[0]
        gv = None
        for k in range(8):
            term = jnp.where(me == k, own_ref[...], p_ref[k])
            gv = term if gv is None else gv + term
        delta, m2, v2 = _adam_math(w_ref[...], gv, m_ref[...], v_ref[...])
        g_ref[...] = gv
        d_ref[...] = delta
        mo_ref[...] = m2
        vo_ref[...] = v2

    blk = pl.BlockSpec((tr, D), lambda i, p: (i, 0))
    shp = jax.ShapeDtypeStruct((SMALL_ROWS, D), F32)
    return pl.pallas_call(
        body, name=name,
        grid_spec=pltpu.PrefetchScalarGridSpec(
            num_scalar_prefetch=1, grid=(SMALL_ROWS // tr,),
            in_specs=[pl.BlockSpec((8, tr, D), lambda i, p: (0, i, 0)), blk, blk, blk, blk],
            out_specs=[blk] * 4),
        out_shape=[shp] * 4,
        compiler_params=_cp(("parallel",)),
    )(pos, parts, own, w, m, v)


ANY = pl.BlockSpec(memory_space=pl.ANY)


def _in_hbm(arrays):
    return [pltpu.with_memory_space_constraint(a, pltpu.HBM) for a in arrays]


def _mesh_pos():
    x, y, c = lax.axis_index("x"), lax.axis_index("y"), lax.axis_index("c")
    chips = [(1 - x, y), (x, 1 - y), (1 - x, 1 - y)]
    return x, y, c, chips


def _place_shard(w, kind, pos, name):
    r, c = w.shape
    tr = _pick_rows(r, c)

    def body(pos_ref, w_ref, o_ref):
        o_ref[...] = w_ref[...].astype(BF16)

    if kind == "stack":
        o_spec = pl.BlockSpec((None, tr, c), lambda i, p: (p[1], i, 0))
        shape = (NSH, r, c)
    else:
        o_spec = pl.BlockSpec((tr, c), lambda i, p: (i, p[1]))
        shape = (r, NSH * c)
    return pl.pallas_call(
        body, name=name,
        grid_spec=pltpu.PrefetchScalarGridSpec(
            num_scalar_prefetch=1, grid=(r // tr,),
            in_specs=[pl.BlockSpec((tr, c), lambda i, p: (i, 0))], out_specs=o_spec),
        out_shape=pltpu.HBM(shape, BF16),
        compiler_params=_cp(("parallel",)),
    )(pos, pltpu.with_memory_space_constraint(w, pltpu.HBM))


SEM = pl.BlockSpec(memory_space=pltpu.SEMAPHORE)
SPLIT_COPY = pltpu.CompilerParams(has_side_effects=pltpu.SideEffectType.DATAFLOW_SIDE_EFFECTING)


def _shard_window(ref, kind, j, h, dims):
    r, c = dims
    rows = pl.ds(pl.multiple_of(h * (r // 2), 16), r // 2)
    if kind == "stack":
        return ref.at[j, rows, :]
    return ref.at[rows, pl.ds(pl.multiple_of(j * c, 128), c)]


def _ici_copy(ref, kind, dims, j, c, sems, idx, to):
    win = _shard_window(ref, kind, j, c, dims)
    return pltpu.make_async_remote_copy(src_ref=win, dst_ref=win, send_sem=sems[0].at[idx], recv_sem=sems[1].at[idx],
                                        device_id=to, device_id_type=MESH_T)


def _both_copy(ref, kind, dims, a, k, chip, half, tc, sc, sems):
    win = _shard_window(ref, kind, half[0], half[1], dims)
    return pltpu.make_async_remote_copy(src_ref=win, dst_ref=win, send_sem=sems[0].at[6 * a + 2 * k + tc],
                                        recv_sem=sems[1].at[6 * a + 2 * k + sc],
                                        device_id=(chip[0], chip[1], tc), device_id_type=MESH_T)


def _gather_start(fulls, kinds, dims, after, name, both=False):
    n, na = len(fulls), len(after)
    per = 6 if both else 3

    def body(*refs):
        outs = refs[n + na:2 * n + na]
        send_sems, recv_sems, token = refs[2 * n + na:]
        x, y, c, chips = _mesh_pos()
        for a in range(n):
            for k, chip in enumerate(chips):
                if both:
                    for tc in range(2):
                        _both_copy(outs[a], kinds[a], dims[a], a, k, chip, (2 * x + y, c), tc, c,
                                   (send_sems, recv_sems)).start()
                else:
                    _ici_copy(outs[a], kinds[a], dims[a], 2 * x + y, c, (send_sems, recv_sems), 3 * a + k,
                              (chip[0], chip[1], c)).start()
        token[...] = jnp.zeros_like(token)

    res = pl.pallas_call(
        body, name=name, in_specs=[ANY] * (n + na),
        out_specs=[ANY] * n + [SEM, SEM, pl.BlockSpec(memory_space=pltpu.VMEM)],
        out_shape=[pltpu.HBM(f.shape, BF16) for f in fulls]
        + [pltpu.SemaphoreType.DMA((per * n,)), pltpu.SemaphoreType.DMA((per * n,)),
           jax.ShapeDtypeStruct((8, 128), F32)],
        input_output_aliases={i: i for i in range(n)},
        compiler_params=SPLIT_COPY,
    )(*_in_hbm(fulls), *after)
    return res[:n], res[n], res[n + 1], res[n + 2]


def _gather_wait(fulls, send_sems, recv_sems, kinds, dims, after, name, both=False):
    n, na = len(fulls), len(after)

    def body(*refs):
        ssem, rsem = refs[n], refs[n + 1]
        outs = refs[n + 2 + na:]
        x, y, c, chips = _mesh_pos()
        for a in range(n):
            for k, chip in enumerate(chips):
                if both:
                    for oc in range(2):
                        _both_copy(outs[a], kinds[a], dims[a], a, k, chip, (2 * x + y, c), oc, c,
                                   (ssem, rsem)).wait_send()
                        _both_copy(outs[a], kinds[a], dims[a], a, k, chip, (2 * chip[0] + chip[1], oc), c, oc,
                                   (ssem, rsem)).wait_recv()
                    continue
                to = (chip[0], chip[1], c)
                _ici_copy(outs[a], kinds[a], dims[a], 2 * x + y, c, (ssem, rsem), 3 * a + k, to).wait_send()
                _ici_copy(outs[a], kinds[a], dims[a], 2 * chip[0] + chip[1], c, (ssem, rsem), 3 * a + k, to).wait_recv()

    return pl.pallas_call(
        body, name=name, in_specs=[ANY] * n + [SEM, SEM] + [ANY] * na, out_specs=[ANY] * n,
        out_shape=[pltpu.HBM(f.shape, BF16) for f in fulls],
        input_output_aliases={i: i for i in range(n)},
        compiler_params=SPLIT_COPY,
    )(*_in_hbm(fulls), send_sems, recv_sems, *after)


def _gather_forward(fulls, kinds, dims, name):
    n = len(fulls)

    def body(*refs):
        outs = refs[n:2 * n]
        sems = refs[2 * n:]
        x, y, c, chips = _mesh_pos()
        sib = (x, y, 1 - c)
        cps = []
        for a in range(n):
            for k, chip in enumerate(chips):
                cp = _ici_copy(outs[a], kinds[a], dims[a], 2 * chip[0] + chip[1], c, sems, 3 * a + k, sib)
                cp.start()
                cps.append(cp)
        for a in range(n):
            for k, chip in enumerate(chips):
                _ici_copy(outs[a], kinds[a], dims[a], 2 * chip[0] + chip[1], 1 - c, sems, 3 * a + k, sib).wait_recv()
        for cp in cps:
            cp.wait_send()

    return pl.pallas_call(
        body, name=name, in_specs=[ANY] * n, out_specs=[ANY] * n,
        out_shape=[pltpu.HBM(f.shape, BF16) for f in fulls],
        input_output_aliases={i: i for i in range(n)},
        scratch_shapes=[pltpu.SemaphoreType.DMA((3 * n,)), pltpu.SemaphoreType.DMA((3 * n,))],
    )(*_in_hbm(fulls))


def _pair_copy(src, land, a, x, y, c, sems):
    return pltpu.make_async_remote_copy(
        src_ref=src.at[1 - c], dst_ref=land, send_sem=sems[0].at[a], recv_sem=sems[1].at[a],
        device_id=(x, y, 1 - c), device_id_type=MESH_T)


def _pair_start(grads, lands, name):
    n = len(grads)

    def body(*refs):
        srcs, dsts = refs[2 * n:3 * n], refs[3 * n:4 * n]
        send_sems, recv_sems, token = refs[4 * n:]
        x, y, c, _ = _mesh_pos()
        for a in range(n):
            _pair_copy(srcs[a], dsts[a], a, x, y, c, (send_sems, recv_sems)).start()
        token[...] = jnp.zeros_like(token)

    res = pl.pallas_call(
        body, name=name, in_specs=[ANY] * (2 * n),
        out_specs=[ANY] * (2 * n) + [SEM, SEM, pl.BlockSpec(memory_space=pltpu.VMEM)],
        out_shape=[pltpu.HBM(g.shape, F32) for g in grads]
        + [pltpu.HBM(l.shape, F32) for l in lands]
        + [pltpu.SemaphoreType.DMA((n,)), pltpu.SemaphoreType.DMA((n,)), jax.ShapeDtypeStruct((8, 128), F32)],
        input_output_aliases={i: i for i in range(2 * n)},
        compiler_params=SPLIT_COPY,
    )(*_in_hbm(grads), *_in_hbm(lands))
    return res[:n], res[n:2 * n], res[2 * n], res[2 * n + 1], res[2 * n + 2]


def _pair_wait(grads, lands, send_sems, recv_sems, after, name):
    n, na = len(grads), len(after)

    def body(*refs):
        ssem, rsem = refs[2 * n], refs[2 * n + 1]
        outs = refs[2 * n + 2 + na:]
        x, y, c, _ = _mesh_pos()
        for a in range(n):
            cp = _pair_copy(outs[a], outs[n + a], a, x, y, c, (ssem, rsem))
            cp.wait_send()
            cp.wait_recv()

    res = pl.pallas_call(
        body, name=name, in_specs=[ANY] * (2 * n) + [SEM, SEM] + [ANY] * na, out_specs=[ANY] * (2 * n),
        out_shape=[pltpu.HBM(g.shape, F32) for g in grads]
        + [pltpu.HBM(l.shape, F32) for l in lands],
        input_output_aliases={i: i for i in range(2 * n)},
        compiler_params=SPLIT_COPY,
    )(*_in_hbm(grads), *_in_hbm(lands), send_sems, recv_sems, *after)
    return res[:n], res[n:]


def _pair_sum(g, recv, pos, name):
    _, _, rh, c = g.shape
    tr = _pick_rows(rh, c)

    def body(pos_ref, g_ref, r_ref, o_ref):
        o_ref[...] = (g_ref[...] + r_ref[...]).astype(BF16)

    return pl.pallas_call(
        body, name=name,
        grid_spec=pltpu.PrefetchScalarGridSpec(
            num_scalar_prefetch=1, grid=(3, rh // tr),
            in_specs=[pl.BlockSpec((None, None, tr, c), lambda k, r, p: (p[0], p[2 + k], r, 0)),
                      pl.BlockSpec((None, tr, c), lambda k, r, p: (p[2 + k], r, 0))],
            out_specs=pl.BlockSpec((None, tr, c), lambda k, r, p: (k, r, 0))),
        out_shape=pltpu.HBM((3, rh, c), BF16),
        compiler_params=_cp(("parallel", "parallel")),
    )(pos, *_in_hbm([g, recv]))


def _pair_sum_group(gs, recvs, pos, name):
    n = len(gs)
    _, _, rh, c = gs[0].shape

    def body(pos_ref, *refs):
        a = pl.program_id(0)
        for t in range(n):
            @pl.when(a == t)
            def _(t=t):
                refs[2 * n + t][...] = (refs[t][...] + refs[n + t][...]).astype(BF16)

    def slot(t, a, k):
        return jnp.where(a == t, k, jnp.where(a < t, 0, 2))

    g_specs = [pl.BlockSpec((None, None, rh, c), lambda a, k, p, t=t: (p[0], p[2 + slot(t, a, k)], 0, 0))
               for t in range(n)]
    r_specs = [pl.BlockSpec((None, rh, c), lambda a, k, p, t=t: (p[2 + slot(t, a, k)], 0, 0)) for t in range(n)]
    o_specs = [pl.BlockSpec((None, rh, c), lambda a, k, p, t=t: (slot(t, a, k), 0, 0)) for t in range(n)]
    return pl.pallas_call(
        body, name=name,
        grid_spec=pltpu.PrefetchScalarGridSpec(num_scalar_prefetch=1, grid=(n, 3), in_specs=g_specs + r_specs,
                                               out_specs=o_specs),
        out_shape=[pltpu.HBM((3, rh, c), BF16)] * n,
        compiler_params=_cp(("arbitrary", "arbitrary")),
    )(pos, *_in_hbm(list(gs) + list(recvs)))


def _chip_copy(src, land, a, k, chip, c, sems):
    return pltpu.make_async_remote_copy(
        src_ref=src.at[k], dst_ref=land.at[k], send_sem=sems[0].at[3 * a + k],
        recv_sem=sems[1].at[3 * a + k], device_id=(chip[0], chip[1], c), device_id_type=MESH_T)


def _chip_start(psums, lands, name):
    n = len(psums)

    def body(*refs):
        srcs, dsts = refs[2 * n:3 * n], refs[3 * n:4 * n]
        send_sems, recv_sems, token = refs[4 * n:]
        x, y, c, chips = _mesh_pos()
        for a in range(n):
            for k, chip in enumerate(chips):
                _chip_copy(srcs[a], dsts[a], a, k, chip, c, (send_sems, recv_sems)).start()
        token[...] = jnp.zeros_like(token)

    res = pl.pallas_call(
        body, name=name, in_specs=[ANY] * (2 * n),
        out_specs=[ANY] * (2 * n) + [SEM, SEM, pl.BlockSpec(memory_space=pltpu.VMEM)],
        out_shape=[pltpu.HBM(p.shape, BF16) for p in psums]
        + [pltpu.HBM(l.shape, BF16) for l in lands]
        + [pltpu.SemaphoreType.DMA((3 * n,)), pltpu.SemaphoreType.DMA((3 * n,)), jax.ShapeDtypeStruct((8, 128), F32)],
        input_output_aliases={i: i for i in range(2 * n)},
        compiler_params=SPLIT_COPY,
    )(*_in_hbm(psums), *_in_hbm(lands))
    return res[:n], res[n:2 * n], res[2 * n], res[2 * n + 1], res[2 * n + 2]


def _chip_wait(psums, lands, send_sems, recv_sems, after, name):
    n, na = len(psums), len(after)

    def body(*refs):
        ssem, rsem = refs[2 * n], refs[2 * n + 1]
        outs = refs[2 * n + 2 + na:]
        srcs, dsts = outs[:n], outs[n:]
        x, y, c, chips = _mesh_pos()
        for a in range(n):
            for k, chip in enumerate(chips):
                cp = _chip_copy(srcs[a], dsts[a], a, k, chip, c, (ssem, rsem))
                cp.wait_send()
                cp.wait_recv()

    res = pl.pallas_call(
        body, name=name, in_specs=[ANY] * (2 * n) + [SEM, SEM] + [ANY] * na, out_specs=[ANY] * (2 * n),
        out_shape=[pltpu.HBM(p.shape, BF16) for p in psums]
        + [pltpu.HBM(l.shape, BF16) for l in lands],
        input_output_aliases={i: i for i in range(2 * n)},
        compiler_params=SPLIT_COPY,
    )(*_in_hbm(psums), *_in_hbm(lands), send_sems, recv_sems, *after)
    return res[n:]


def _owner_sum(g, recv_a, recv_b, pos, name):
    _, _, rh, c = g.shape
    tr = _pick_rows(rh, c)

    def body(pos_ref, g_ref, ra_ref, rb_ref, o_ref):
        acc = g_ref[...] + ra_ref[...]
        for k in range(3):
            acc = acc + rb_ref[k].astype(F32)
        o_ref[...] = acc

    return pl.pallas_call(
        body, name=name,
        grid_spec=pltpu.PrefetchScalarGridSpec(
            num_scalar_prefetch=1, grid=(rh // tr,),
            in_specs=[pl.BlockSpec((None, None, tr, c), lambda r, p: (p[0], p[1], r, 0)),
                      pl.BlockSpec((None, tr, c), lambda r, p: (p[1], r, 0)),
                      pl.BlockSpec((3, tr, c), lambda r, p: (0, r, 0))],
            out_specs=pl.BlockSpec((None, tr, c), lambda r, p: (p[0], r, 0))),
        out_shape=pltpu.HBM((2, rh, c), F32),
        compiler_params=_cp(("parallel",)),
    )(pos, *_in_hbm([g, recv_a, recv_b]))


def _sibling_allgather(halves, name):
    n = len(halves)

    def body(*refs):
        outs = refs[n:2 * n]
        send_sems, recv_sems = refs[2 * n:]
        x, y, c, _ = _mesh_pos()
        cps = []
        for a in range(n):
            cp = pltpu.make_async_remote_copy(
                src_ref=outs[a].at[c], dst_ref=outs[a].at[c], send_sem=send_sems.at[a], recv_sem=recv_sems.at[a],
                device_id=(x, y, 1 - c), device_id_type=MESH_T)
            cp.start()
            cps.append(cp)
        for a in range(n):
            cps[a].wait_send()
            pltpu.make_async_remote_copy(
                src_ref=outs[a].at[1 - c], dst_ref=outs[a].at[1 - c], send_sem=send_sems.at[a],
                recv_sem=recv_sems.at[a], device_id=(x, y, 1 - c), device_id_type=MESH_T).wait_recv()

    return pl.pallas_call(
        body, name=name, in_specs=[ANY] * n, out_specs=[ANY] * n,
        out_shape=[pltpu.HBM(h.shape, F32) for h in halves],
        input_output_aliases={i: i for i in range(n)},
        scratch_shapes=[pltpu.SemaphoreType.DMA((n,)), pltpu.SemaphoreType.DMA((n,))],
    )(*_in_hbm(halves))


def _peers(x, y, c):
    rel = [(0, 0, 1), (0, 1, 0), (0, 1, 1), (1, 0, 0), (1, 0, 1), (1, 1, 0), (1, 1, 1)]
    return [((1 - x) if dx else x, (1 - y) if dy else y, (1 - c) if dc else c) for dx, dy, dc in rel]


def _small_copy(src, land, k, peer, slot, sems):
    return pltpu.make_async_remote_copy(src_ref=src, dst_ref=land.at[slot], send_sem=sems[0].at[k],
                                        recv_sem=sems[1].at[k], device_id=peer, device_id_type=MESH_T)


def _small_start(part, land, name):
    def body(p_in, l_in, p_ref, l_ref, send_sems, recv_sems, token):
        x, y, c, _ = _mesh_pos()
        for k, peer in enumerate(_peers(x, y, c)):
            _small_copy(p_ref, l_ref, k, peer, 4 * x + 2 * y + c, (send_sems, recv_sems)).start()
        token[...] = jnp.zeros_like(token)

    return pl.pallas_call(
        body, name=name, in_specs=[ANY, ANY],
        out_specs=[ANY, ANY, SEM, SEM, pl.BlockSpec(memory_space=pltpu.VMEM)],
        out_shape=[pltpu.HBM(part.shape, F32), pltpu.HBM(land.shape, F32), pltpu.SemaphoreType.DMA((7,)),
                   pltpu.SemaphoreType.DMA((7,)), jax.ShapeDtypeStruct((8, 128), F32)],
        input_output_aliases={0: 0, 1: 1},
        compiler_params=SPLIT_COPY,
    )(*_in_hbm([part, land]))


def _small_wait(part, land, send_sems, recv_sems, after, name):
    na = len(after)

    def body(*refs):
        ssem, rsem = refs[2], refs[3]
        p_ref, l_ref = refs[4 + na:]
        x, y, c, _ = _mesh_pos()
        for k, peer in enumerate(_peers(x, y, c)):
            cp = _small_copy(p_ref, l_ref, k, peer, 4 * peer[0] + 2 * peer[1] + peer[2], (ssem, rsem))
            cp.wait_send()
            cp.wait_recv()

    return pl.pallas_call(
        body, name=name, in_specs=[ANY, ANY, SEM, SEM] + [ANY] * na, out_specs=[ANY, ANY],
        out_shape=[pltpu.HBM(part.shape, F32), pltpu.HBM(land.shape, F32)],
        input_output_aliases={0: 0, 1: 1},
        compiler_params=SPLIT_COPY,
    )(*_in_hbm([part, land]), send_sems, recv_sems, *after)


def _pack_small(ln1_g, ln1_b, gln_g, gln_b, ln2_g, ln2_b, ln3_g, ln3_b, b_gates, b_s, w_s):
    rows = [ln1_g, ln1_b, gln_g, gln_b, ln2_g, ln2_b, ln3_g, ln3_b]
    rows = [r.reshape(1, D) for r in rows] + [b_gates.reshape(2, D), b_s.reshape(1, D), jnp.zeros((5, D), F32),
                                             w_s.reshape(128, D)]
    return jnp.concatenate(rows, axis=0)


def _unpack_small(p):
    out = [p[i:i + 1] for i in range(8)]
    return out + [p[8:10].reshape(1, 2 * D), p[10:11].reshape(1, 8, BLK), p[16:144].reshape(1, 8, BLK, BLK)]


GROUPS = (("f1g", "f1u", "f1d"), ("w_in",), ("w_ab", "w_gb", "w_out"), ("f2g", "f2u", "f2d"))
LATE_GROUPS = (2, 3)


def _local_step(x, pos_f, target, P, weights_of, grads_ready, flush, small_ready):
    invf = ROPE_THETA ** (-jnp.arange(0, DH, 2, dtype=F32) / DH)
    invf = jnp.tile(invf, 4).reshape(1, 128)
    b_s_t = P["gmlp_b_s"].T

    W = dict(weights_of(0, []))
    h1b, xh1, rstd1, a1, b1, h1t = _ffn_fwd(x, W["f1g"], W["f1u"], W["f1d"], P["ln1_g"], P["ln1_b"], "ffn1_fwd",
                                                emit_t=True)
    W.update(weights_of(1, [h1b]))
    qkv_c = _proj_qkv_rope(h1b, W["w_in"], pos_f, invf, "proj_qkv_rope")
    z = _matmul(h1b, W["w_in"], "nn", "proj_z", n=2 * GW, b_col0=3 * ATT_W, tm=S, tn=512)
    gl = _matmul(h1b, W["w_in"], "nn", "proj_gates", n=2 * D, b_col0=3 * ATT_W + 2 * GW, tm=S, tn=512)
    og = [_attn_fwd(gi, qkv_c[gi], "attn_fwd_g%d" % gi) for gi in range(NG)]
    y_attn, y_attn_t, lse = _attn_combine([o for o, _ in og], [l for _, l in og], "attn_combine")
    y_gmlp, y_gmlp_t = _gmlp_fwd(z, P["gmlp_ln_g"], P["gmlp_ln_b"], P["gmlp_w_s"], b_s_t, "gmlp_fwd")
    W.update(weights_of(2, [y_gmlp]))
    br_a = _matmul(y_attn, W["w_ab"], "nn", "branch_attn", n=D, tm=1024, tn=D)
    br_b = _matmul(y_gmlp, W["w_gb"], "nn", "branch_gmlp", n=D, tm=1024, tn=D)
    merged, merged_t = _merge_fwd(br_a, br_b, gl, P["b_gates"], "merge_fwd")
    h2, h2b, xh2, rstd2 = _resid_ln(xh1, P["ln1_g"], P["ln1_b"], merged, W["w_out"], P["ln2_g"], P["ln2_b"],
                                    "mix_resid_ln2")
    W.update(weights_of(3, [h2b]))
    dr3, a2, b2, dg3, db3, loss = _ffn_fwd(h2, W["f2g"], W["f2u"], W["f2d"], P["ln3_g"], P["ln3_b"],
                                           "ffn2_fwd_loss", target=target)

    g_f2g, g_f2u, g_f2d, dh2 = _ffn_bwd(dr3, h2b, a2, b2, W["f2g"], W["f2u"], W["f2d"], "ffn2_bwd")
    tok = grads_ready(3, dict(f2g=g_f2g, f2u=g_f2u, f2d=g_f2d))
    dr2, dg2, db2 = _ln_bwd(dh2, xh2, rstd2, P["ln2_g"], "ln2_bwd", after=tok)
    g_wout = _wgrad(merged_t, dr2, 128, D, "dw_out", row_sharded=True)
    dmerged = _matmul(dr2, W["w_out"], "nt", "dmerged", n=D, tm=1024, tn=D)
    dab, dbb, dglb, dbg = _merge_bwd(dmerged, br_a, br_b, gl, P["b_gates"], "merge_bwd")
    tok = flush([dab])
    g_wab = _wgrad(y_attn_t, dab, GRP_W // 2, 256, "dw_attn_branch", row_sharded=False, after=tok)
    g_wgb = _wgrad(y_gmlp_t, dbb, 128, D, "dw_gmlp_branch", row_sharded=True)
    tok = grads_ready(2, dict(w_ab=g_wab, w_gb=g_wgb, w_out=g_wout))
    dy_attn = _matmul(dab, W["w_ab"], "nt", "dy_attn", n=GRP_W, tm=1024, tn=GRP_W, after=tok)
    dy_gmlp = _matmul(dbb, W["w_gb"], "nt", "dy_gmlp", n=GW, tm=1024, tn=GW)
    dzb, dws, dbs_t, dgln_g, dgln_b = _gmlp_bwd(z, dy_gmlp, P["gmlp_ln_g"], P["gmlp_ln_b"], P["gmlp_w_s"], b_s_t,
                                                 "gmlp_bwd")
    cls = _class_order([dy_attn, y_attn, lse], "attn_class_order")
    dqkv_c = []
    for gi in range(NG):
        dy_c, y_c, lse_c = [t[None] if gi == 0 else cls[2 * a + gi - 1] for a, t in enumerate((dy_attn, y_attn, lse))]
        dqkv_c.append(_attn_bwd(gi, qkv_c[gi], dy_c, y_c, lse_c, "attn_bwd_g%d" % gi))
    dproj = _rope_bwd(dqkv_c, dzb, dglb, pos_f, invf, "rope_bwd")
    tok = flush([dproj])
    g_win = _wgrad(h1t, dproj, D // 2, IN_SH, "dw_in", row_sharded=False, after=tok)
    tok = grads_ready(1, dict(w_in=g_win))
    dr1, dg1, db1 = _dh1_ln_bwd(dproj, W["w_in"], dr2, xh1, rstd1, P["ln1_g"], "dh1_ln1_bwd", after=tok)
    tok = flush([dr1])
    tok = tok + small_ready(_pack_small(dg1, db1, dgln_g, dgln_b, dg2, db2, dg3, db3, dbg, dbs_t.T, dws))
    g_f1g, g_f1u, g_f1d, dx = _ffn_bwd(dr1, x.astype(BF16), a1, b1, W["f1g"], W["f1u"], W["f1d"], "ffn1_bwd",
                                       after=tok)
    grads_ready(0, dict(f1g=g_f1g, f1u=g_f1u, f1d=g_f1d))
    flush([dx])
    return loss, dx


TRANSPOSED = ("f1g", "f1u", "f2g", "f2u")
KIND = dict(f1g="stack", f1u="stack", f1d="stack", w_in="col", w_ab="col", w_gb="stack", w_out="stack",
            f2g="stack", f2u="stack", f2d="stack")


def kernel(x, positions, ffn1_w_gate, ffn1_w_up, ffn1_w_down, ln1_g, ln1_b, w_in, b_gates, gmlp_ln_g, gmlp_ln_b, gmlp_w_s, gmlp_b_s, w_attn_branch, w_gmlp_branch, w_out, ln2_g, ln2_b, ffn2_w_gate, ffn2_w_up, ffn2_w_down, ln3_g, ln3_b, loss_target, m_ffn1_w_gate, m_ffn1_w_up, m_ffn1_w_down, m_ln1_g, m_ln1_b, m_w_in, m_b_gates, m_gmlp_ln_g, m_gmlp_ln_b, m_gmlp_w_s, m_gmlp_b_s, m_w_attn_branch, m_w_gmlp_branch, m_w_out, m_ln2_g, m_ln2_b, m_ffn2_w_gate, m_ffn2_w_up, m_ffn2_w_down, m_ln3_g, m_ln3_b, v_ffn1_w_gate, v_ffn1_w_up, v_ffn1_w_down, v_ln1_g, v_ln1_b, v_w_in, v_b_gates, v_gmlp_ln_g, v_gmlp_ln_b, v_gmlp_w_s, v_gmlp_b_s, v_w_attn_branch, v_w_gmlp_branch, v_w_out, v_ln2_g, v_ln2_b, v_ffn2_w_gate, v_ffn2_w_up, v_ffn2_w_down, v_ln3_g, v_ln3_b):
    cx, cy, cc = lax.axis_index("x"), lax.axis_index("y"), lax.axis_index("c")
    pos = jnp.stack([cc, 2 * cx + cy, 2 * (1 - cx) + cy, 2 * cx + 1 - cy, 2 * (1 - cx) + 1 - cy]).astype(jnp.int32)

    w_sh = dict(f1g=ffn1_w_gate, f1u=ffn1_w_up, f1d=ffn1_w_down, w_in=w_in, w_ab=w_attn_branch,
                w_gb=w_gmlp_branch, w_out=w_out, f2g=ffn2_w_gate, f2u=ffn2_w_up, f2d=ffn2_w_down)
    m_sh = dict(f1g=m_ffn1_w_gate, f1u=m_ffn1_w_up, f1d=m_ffn1_w_down, w_in=m_w_in, w_ab=m_w_attn_branch,
                w_gb=m_w_gmlp_branch, w_out=m_w_out, f2g=m_ffn2_w_gate, f2u=m_ffn2_w_up, f2d=m_ffn2_w_down)
    v_sh = dict(f1g=v_ffn1_w_gate, f1u=v_ffn1_w_up, f1d=v_ffn1_w_down, w_in=v_w_in, w_ab=v_w_attn_branch,
                w_gb=v_w_gmlp_branch, w_out=v_w_out, f2g=v_ffn2_w_gate, f2u=v_ffn2_w_up, f2d=v_ffn2_w_down)
    w_sh = {k: (v[0].T if k in TRANSPOSED else v[0]) for k, v in w_sh.items()}
    m_sh = {k: (v[0].T if k in TRANSPOSED else v[0]) for k, v in m_sh.items()}
    v_sh = {k: (v[0].T if k in TRANSPOSED else v[0]) for k, v in v_sh.items()}

    started, tokens = [], []
    for gi, names in enumerate(GROUPS):
        placed = [_place_shard(w_sh[k], KIND[k], pos, "place_" + k) for k in names]
        fulls, ssem, rsem, token = _gather_start(placed, [KIND[k] for k in names], [w_sh[k].shape for k in names],
                                                 tokens[-1:], "gather_start_g%d" % gi, both=gi in LATE_GROUPS)
        started.append((fulls, ssem, rsem))
        tokens.append(token)

    def weights_of(gi, after):
        names = GROUPS[gi]
        kinds, dims = [KIND[k] for k in names], [w_sh[k].shape for k in names]
        fulls, ssem, rsem = started[gi]
        fulls = _gather_wait(fulls, ssem, rsem, kinds, dims, list(after) + (tokens if gi == 0 else []),
                             "gather_wait_g%d" % gi, both=gi in LATE_GROUPS)
        if gi not in LATE_GROUPS:
            fulls = _gather_forward(fulls, kinds, dims, "gather_forward_g%d" % gi)
        return {k: (f.reshape(D, D) if k in ("w_gb", "w_out") else f) for k, f in zip(names, fulls)}

    pending, inflight = [], {}

    def grads_ready(gi, gd):
        grads = [gd[k] for k in GROUPS[gi]]
        lands = [lax.empty(g.shape[1:], F32) for g in grads]
        grads, lands, ssem, rsem, token = _pair_start(grads, lands, "rs_pair_start_g%d" % gi)
        pending.append((gi, grads, lands, ssem, rsem))
        return [token]

    def flush(after):
        gi, grads, lands, ssem, rsem = pending.pop()
        names = GROUPS[gi]
        grads, recv_a = _pair_wait(grads, lands, ssem, rsem, after, "rs_pair_wait_g%d" % gi)
        if len(names) > 1 and len({g.shape for g in grads}) == 1:
            psums = _pair_sum_group(grads, recv_a, pos, "rs_pair_sum_g%d" % gi)
        else:
            psums = [_pair_sum(g, r, pos, "rs_pair_sum_" + k) for g, r, k in zip(grads, recv_a, names)]
        lands = [lax.empty((3,) + p.shape[1:], BF16) for p in psums]
        psums, lands, ssem, rsem, token = _chip_start(psums, lands, "rs_chip_start_g%d" % gi)
        inflight[gi] = (grads, recv_a, psums, lands, ssem, rsem, token)
        return [token]

    P = dict(ln1_g=ln1_g, ln1_b=ln1_b, ln2_g=ln2_g, ln2_b=ln2_b, ln3_g=ln3_g, ln3_b=ln3_b, b_gates=b_gates,
             gmlp_ln_g=gmlp_ln_g, gmlp_ln_b=gmlp_ln_b, gmlp_w_s=gmlp_w_s[0], gmlp_b_s=gmlp_b_s[0])
    pos_f = positions.reshape(S, 1).astype(F32)
    small_state = []

    def small_ready(packed):
        land = jnp.zeros((8, SMALL_ROWS, D), F32)
        packed, land, ssem, rsem, token = _small_start(packed, land, "small_start")
        small_state.append((packed, land, ssem, rsem))
        return [token]

    loss_part, dx = _local_step(x[0], pos_f, loss_target[0], P, weights_of, grads_ready, flush, small_ready)
    loss = lax.psum(loss_part[0, 0], ("x", "y", "c"))

    g_out, d_out, m_out, v_out = {}, {}, {}, {}

    def finish(gis, after, tag):
        names, halves = [], []
        for gi in gis:
            grads, recv_a, psums, lands, ssem, rsem, token = inflight[gi]
            recv_b = _chip_wait(psums, lands, ssem, rsem, after + [inflight[0][6]], "rs_chip_wait_g%d" % gi)
            halves += [_owner_sum(g, ra, rb, pos, "rs_owner_sum_" + k)
                       for g, ra, rb, k in zip(grads, recv_a, recv_b, GROUPS[gi])]
            names += GROUPS[gi]
            after = halves[-1:]
        reduced = _sibling_allgather(halves, "rs_sibling_allgather_" + tag)
        for k, gfull in zip(names, reduced):
            res = _adamw(w_sh[k], gfull.reshape(w_sh[k].shape), m_sh[k], v_sh[k], "adamw_" + k)
            after = [res[1]]
            if k in TRANSPOSED:
                res = [r.T for r in res]
            g_out[k], d_out[k], m_out[k], v_out[k] = [r[None] for r in res]
        return after

    after = finish((3, 2, 1), [], "g321")

    small, parts = _small_wait(*small_state[0], after, "small_wait")
    sp = (ln1_g, ln1_b, gmlp_ln_g, gmlp_ln_b, ln2_g, ln2_b, ln3_g, ln3_b, b_gates, gmlp_b_s, gmlp_w_s)
    sm = (m_ln1_g, m_ln1_b, m_gmlp_ln_g, m_gmlp_ln_b, m_ln2_g, m_ln2_b, m_ln3_g, m_ln3_b, m_b_gates, m_gmlp_b_s,
          m_gmlp_w_s)
    sv = (v_ln1_g, v_ln1_b, v_gmlp_ln_g, v_gmlp_ln_b, v_ln2_g, v_ln2_b, v_ln3_g, v_ln3_b, v_b_gates, v_gmlp_b_s,
          v_gmlp_w_s)
    sg, sd, smn, svn = _small_sum_adamw(parts, small, pos, _pack_small(*sp), _pack_small(*sm), _pack_small(*sv),
                                        "small_adamw")
    names = ("ln1_g", "ln1_b", "gmlp_ln_g", "gmlp_ln_b", "ln2_g", "ln2_b", "ln3_g", "ln3_b", "b_gates", "gmlp_b_s",
             "gmlp_w_s")
    for dst, packed in ((g_out, sg), (d_out, sd), (m_out, smn), (v_out, svn)):
        for nm, val in zip(names, _unpack_small(packed)):
            dst[nm] = val
    finish((0,), [sg], "g0")

    order = ("f1g", "f1u", "f1d", "ln1_g", "ln1_b", "w_in", "b_gates", "gmlp_ln_g", "gmlp_ln_b", "gmlp_w_s", "gmlp_b_s",
             "w_ab", "w_gb", "w_out", "ln2_g", "ln2_b", "f2g", "f2u", "f2d", "ln3_g", "ln3_b")
    outs = [loss, dx[None]]
    for dst in (g_out, d_out, m_out, v_out):
        outs += [dst[k] for k in order]
    return tuple(outs)
```

```python
import jax
import jax.numpy as jnp
from jax import lax
from jax.experimental import pallas as pl
from jax.experimental.pallas import tpu as pltpu

F32 = jnp.float32
BF16 = jnp.bfloat16

S = 2048
D = 1024
NSH = 4
FSH = 704
ATT_W = 1536
GRP_W = 512
NG = 3
NH = 8
DH = 64
BLK = 128
NBLK = S // BLK
GW = 1024
IN_W = 8704
IN_SH = IN_W // NSH
ALPHA = 2.0 ** 0.25
LN_EPS = 1e-5
ROPE_THETA = 10000.0
DILATIONS = (1, 4, 16)
ADAM_LR, ADAM_B1, ADAM_B2, ADAM_EPS, ADAM_WD, ADAM_STEP = 0.001, 0.9, 0.999, 1e-08, 0.01, 10
SMALL_ROWS = 144
EPI_ROWS = 256
MESH_T = pl.DeviceIdType.MESH
MIB = 1024 * 1024
NEG_INF = float("-inf")


def _cp(sem, vmem_mib=48):
    return pltpu.CompilerParams(dimension_semantics=sem, vmem_limit_bytes=vmem_mib * MIB)


def _ln_stats(r):
    mu = jnp.mean(r, axis=-1, keepdims=True)
    xc = r - mu
    var = jnp.mean(xc * xc, axis=-1, keepdims=True)
    rstd = lax.rsqrt(var + LN_EPS)
    return xc * rstd, rstd


def _ln_dx(dxh, xh, rstd):
    m1 = jnp.mean(dxh, axis=-1, keepdims=True)
    m2 = jnp.mean(dxh * xh, axis=-1, keepdims=True)
    return rstd * (dxh - m1 - xh * m2)


def _dot_nt(a, b):
    return lax.dot_general(a, b, (((1,), (1,)), ((), ())), preferred_element_type=F32)


def _dot_tn(a, b):
    return lax.dot_general(a, b, (((0,), (0,)), ((), ())), preferred_element_type=F32)


def _dot(a, b):
    return jnp.dot(a, b, preferred_element_type=F32)


def _ffn_fwd(xin, wgt, wut, wd, ln_g, ln_b, name, emit_t=False, target=None):
    with_loss = target is not None
    tm = 1024

    def body(x_ref, wg_ref, wu_ref, wd_ref, g_ref, b_ref, *rest):
        if with_loss:
            t_ref, dr_ref, a_ref, bb_ref, dg_ref, db_ref, loss_ref, acc_ref = rest
        elif emit_t:
            hb_ref, xh_ref, rstd_ref, a_ref, bb_ref, ht_ref, acc_ref = rest
        else:
            hb_ref, xh_ref, rstd_ref, a_ref, bb_ref, acc_ref = rest
        i = pl.program_id(0)
        j = pl.program_id(1)
        xb = x_ref[...].astype(BF16)
        a = _dot_nt(xb, wg_ref[...])
        b = _dot_nt(xb, wu_ref[...])
        a_ref[...] = a.astype(BF16)
        bb_ref[...] = b.astype(BF16)
        s = (a * jax.nn.sigmoid(a)) * b
        f = _dot(s.astype(BF16), wd_ref[...])

        @pl.when(j == 0)
        def _():
            acc_ref[...] = f

        @pl.when(j > 0)
        def _():
            acc_ref[...] += f

        if with_loss:
            @pl.when(jnp.logical_and(j == NSH - 1, i == 0))
            def _():
                dg_ref[...] = jnp.zeros_like(dg_ref)
                db_ref[...] = jnp.zeros_like(db_ref)
                loss_ref[...] = jnp.zeros_like(loss_ref)

        @pl.when(j == NSH - 1)
        def _():
            for c0 in range(0, tm, EPI_ROWS):
                rows = slice(c0, c0 + EPI_ROWS)
                r = ALPHA * x_ref[rows, :] + 0.5 * acc_ref[rows, :]
                xh, rstd = _ln_stats(r)
                h = xh * g_ref[...] + b_ref[...]
                if with_loss:
                    err = h - t_ref[rows, :]
                    dy = err * (1.0 / D)
                    dr_ref[rows, :] = _ln_dx(dy * g_ref[...], xh, rstd)
                    dg_ref[...] += jnp.sum(dy * xh, axis=0, keepdims=True)
                    db_ref[...] += jnp.sum(dy, axis=0, keepdims=True)
                    part = 0.5 * jnp.sum(jnp.mean(err * err, axis=-1, keepdims=True), axis=0, keepdims=True)
                    loss_ref[...] += jnp.broadcast_to(part, (8, 128))
                else:
                    hb_ref[rows, :] = h.astype(BF16)
                    xh_ref[rows, :] = xh
                    rstd_ref[rows, :] = rstd
                    if emit_t:
                        ht_ref[:, rows] = h.T.astype(BF16)

    row = pl.BlockSpec((tm, D), lambda i, j: (i, 0))
    vec = pl.BlockSpec((1, D), lambda i, j: (0, 0))
    wsp = pl.BlockSpec((None, FSH, D), lambda i, j: (j, 0, 0))
    ab = pl.BlockSpec((None, tm, FSH), lambda i, j: (j, i, 0))
    ab_shape = jax.ShapeDtypeStruct((NSH, S, FSH), BF16)
    in_specs, args = [row, wsp, wsp, wsp, vec, vec], (xin, wgt, wut, wd, ln_g, ln_b)
    if with_loss:
        in_specs, args = in_specs + [row], args + (target,)
        out_specs = [row, ab, ab, vec, vec, pl.BlockSpec((8, 128), lambda i, j: (0, 0))]
        out_shape = [jax.ShapeDtypeStruct((S, D), F32), ab_shape, ab_shape, jax.ShapeDtypeStruct((1, D), F32),
                     jax.ShapeDtypeStruct((1, D), F32), jax.ShapeDtypeStruct((8, 128), F32)]
    else:
        out_specs = [row, row, pl.BlockSpec((tm, 1), lambda i, j: (i, 0)), ab, ab]
        out_shape = [jax.ShapeDtypeStruct((S, D), BF16), jax.ShapeDtypeStruct((S, D), F32),
                     jax.ShapeDtypeStruct((S, 1), F32), ab_shape, ab_shape]
        if emit_t:
            out_specs.append(pl.BlockSpec((D, tm), lambda i, j: (0, i)))
            out_shape.append(jax.ShapeDtypeStruct((D, S), BF16))
    return pl.pallas_call(
        body, name=name, grid=(S // tm, NSH), in_specs=in_specs, out_specs=out_specs, out_shape=out_shape,
        scratch_shapes=[pltpu.VMEM((tm, D), F32)],
        compiler_params=_cp(("arbitrary" if with_loss else "parallel", "arbitrary"), vmem_mib=56),
    )(*args)


def _ffn_bwd(dr, xin_b, a, b, wgt, wut, wd, name, after=()):
    tm = 512
    ni = S // tm
    hr = FSH // 2

    def body(dr_ref, a_ref, b_ref, wg_ref, wu_ref, wd_ref, x_hbm, *rest):
        dwg_hbm, dwu_hbm, dwd_hbm, dx_hbm, dx_acc, da_all, db_all, s_all, df_all, x_all, res_buf, sems = rest[len(after):]
        j = pl.program_id(0)
        i = pl.program_id(1)
        rows = pl.ds(pl.multiple_of(i * tm, tm), tm)

        @pl.when(jnp.logical_and(j == 0, i == 0))
        def _():
            cp = pltpu.make_async_copy(x_hbm, x_all, sems.at[0])
            cp.start()
            cp.wait()

        drv = dr_ref[...]
        df = (0.5 * drv).astype(BF16)

        @pl.when(j == 0)
        def _():
            df_all[rows, :] = df

        ds = jnp.concatenate([_dot_nt(df, wd_ref[0:384, :]), _dot_nt(df, wd_ref[384:FSH, :])], axis=1)
        av = a_ref[...].astype(F32)
        bv = b_ref[...].astype(F32)
        sig = jax.nn.sigmoid(av)
        sl = av * sig
        da = (ds * bv * (sig * (1.0 + av * (1.0 - sig)))).astype(BF16)
        db = (ds * sl).astype(BF16)
        da_all[rows, :] = da
        db_all[rows, :] = db
        s_all[rows, :] = (sl * bv).astype(BF16)
        dx = _dot(da, wg_ref[...]) + _dot(db, wu_ref[...])

        @pl.when(j == 0)
        def _():
            dx_acc[rows, :] = ALPHA * drv + dx

        @pl.when(j > 0)
        def _():
            dx_acc[rows, :] += dx

        @pl.when(i == ni - 1)
        def _():
            copies = []
            for n, (lhs, rhs, out) in enumerate(((da_all, x_all, dwg_hbm), (db_all, x_all, dwu_hbm),
                                                 (s_all, df_all, dwd_hbm))):
                slot = n % 2
                if n >= 2:
                    for cp in copies[2 * (n - 2): 2 * (n - 2) + 2]:
                        cp.wait()
                res_buf[slot] = _dot_tn(lhs[...], rhs[...])
                for h in range(2):
                    cp = pltpu.make_async_copy(res_buf.at[slot, pl.ds(h * hr, hr), :], out.at[h, j],
                                               sems.at[1 + 2 * slot + h])
                    cp.start()
                    copies.append(cp)
            for cp in copies[2:]:
                cp.wait()

        @pl.when(jnp.logical_and(j == NSH - 1, i == ni - 1))
        def _():
            cp = pltpu.make_async_copy(dx_acc, dx_hbm, sems.at[0])
            cp.start()
            cp.wait()

    row = pl.BlockSpec((tm, D), lambda j, i: (i, 0))
    wsp = pl.BlockSpec((None, FSH, D), lambda j, i: (j, 0, 0))
    ab = pl.BlockSpec((None, tm, FSH), lambda j, i: (j, i, 0))
    dwshape = jax.ShapeDtypeStruct((2, NSH, hr, D), F32)
    return pl.pallas_call(
        body, name=name, grid=(NSH, ni),
        in_specs=[row, ab, ab, wsp, wsp, wsp, ANY] + [ANY] * len(after),
        out_specs=[ANY, ANY, ANY, ANY],
        out_shape=[dwshape, dwshape, dwshape, jax.ShapeDtypeStruct((S, D), F32)],
        scratch_shapes=[pltpu.VMEM((S, D), F32), pltpu.VMEM((S, FSH), BF16), pltpu.VMEM((S, FSH), BF16),
                        pltpu.VMEM((S, FSH), BF16), pltpu.VMEM((S, D), BF16), pltpu.VMEM((S, D), BF16),
                        pltpu.VMEM((2, FSH, D), F32), pltpu.SemaphoreType.DMA((5,))],
        compiler_params=_cp(("arbitrary", "arbitrary"), vmem_mib=58),
    )(dr, a, b, wgt, wut, wd, xin_b, *after)


def _matmul(a, b, mode, name, *, n, tm, tn, b_col0=0, after=()):
    m, k = a.shape
    assert m % tm == 0 and n % tn == 0 and b_col0 % tn == 0
    off = b_col0 // tn
    na = len(after)

    def body(*refs):
        a_ref, b_ref, o_ref = refs[na:]
        av = a_ref[...].astype(BF16)
        o_ref[...] = _dot(av, b_ref[...]) if mode == "nn" else _dot_nt(av, b_ref[...])

    if mode == "nn":
        b_spec = pl.BlockSpec((k, tn), lambda i, j: (0, j + off))
    else:
        b_spec = pl.BlockSpec((tn, k), lambda i, j: (j, 0))
    return pl.pallas_call(
        body, name=name, grid=(m // tm, n // tn),
        in_specs=[pl.BlockSpec(memory_space=pl.ANY)] * na + [pl.BlockSpec((tm, k), lambda i, j: (i, 0)), b_spec],
        out_specs=pl.BlockSpec((tm, tn), lambda i, j: (i, j)),
        out_shape=jax.ShapeDtypeStruct((m, n), F32),
        compiler_params=_cp(("parallel", "parallel")),
    )(*after, a, b)


def _wgrad(xt, y, rh, c, name, row_sharded, after=()):
    na = len(after)
    if row_sharded:
        def body(x_ref, y_ref, *rest):
            o_ref = rest[na]
            res = _dot(x_ref[...], y_ref[...].astype(BF16))
            for j in range(NSH):
                for h in range(2):
                    o_ref[h, j] = res[(2 * j + h) * rh:(2 * j + h + 1) * rh, :]

        grid = (1,)
        in_specs = [pl.BlockSpec((2 * NSH * rh, S), lambda g: (0, 0)), pl.BlockSpec((S, c), lambda g: (0, 0))]
        out_specs = pl.BlockSpec((2, NSH, rh, c), lambda g: (0, 0, 0, 0))
        sem = ("arbitrary",)
    else:
        def body(x_ref, y_ref, *rest):
            rest[na][...] = _dot(x_ref[...], y_ref[...].astype(BF16))

        grid = (2, NSH)
        in_specs = [pl.BlockSpec((rh, S), lambda h, j: (h, 0)), pl.BlockSpec((S, c), lambda h, j: (0, j))]
        out_specs = pl.BlockSpec((None, None, rh, c), lambda h, j: (h, j, 0, 0))
        sem = ("parallel", "parallel")
    return pl.pallas_call(
        body, name=name, grid=grid, in_specs=in_specs + [pl.BlockSpec(memory_space=pl.ANY)] * na, out_specs=out_specs,
        out_shape=jax.ShapeDtypeStruct((2, NSH, rh, c), F32),
        compiler_params=_cp(sem, vmem_mib=56),
    )(xt, y, *after)


def _resid_ln(res_xh, res_g, res_b, a, w, ln_g, ln_b, name):
    tm = 512

    def body(rx_ref, rg_ref, rb_ref, a_ref, w_ref, g_ref, b_ref, h_ref, hb_ref, xh_ref, rstd_ref):
        r = ALPHA * (rx_ref[...] * rg_ref[...] + rb_ref[...]) + _dot(a_ref[...], w_ref[...])
        xh, rstd = _ln_stats(r)
        h = xh * g_ref[...] + b_ref[...]
        h_ref[...] = h
        hb_ref[...] = h.astype(BF16)
        xh_ref[...] = xh
        rstd_ref[...] = rstd

    row = pl.BlockSpec((tm, D), lambda i: (i, 0))
    vec = pl.BlockSpec((1, D), lambda i: (0, 0))
    return pl.pallas_call(
        body, name=name, grid=(S // tm,),
        in_specs=[row, vec, vec, row, pl.BlockSpec((D, D), lambda i: (0, 0)), vec, vec],
        out_specs=[row, row, row, pl.BlockSpec((tm, 1), lambda i: (i, 0))],
        out_shape=[jax.ShapeDtypeStruct((S, D), F32), jax.ShapeDtypeStruct((S, D), BF16),
                   jax.ShapeDtypeStruct((S, D), F32), jax.ShapeDtypeStruct((S, 1), F32)],
        compiler_params=_cp(("parallel",)),
    )(res_xh, res_g, res_b, a, w, ln_g, ln_b)


def _dh1_ln_bwd(dproj, w_in, dr2, xh, rstd, ln_g, name, after=()):
    tm, tk, ch = 1024, IN_SH, EPI_ROWS
    nk = IN_W // tk
    na = len(after)

    def body(*refs):
        a_ref, b_ref, add_ref, xh_ref, rstd_ref, g_ref, dr_ref, dg_ref, db_ref, acc_ref = refs[na:]
        i = pl.program_id(0)
        k = pl.program_id(1)
        p = _dot_nt(a_ref[...], b_ref[...])

        @pl.when(k == 0)
        def _():
            acc_ref[...] = p

        @pl.when(k > 0)
        def _():
            acc_ref[...] += p

        @pl.when(jnp.logical_and(k == nk - 1, i == 0))
        def _():
            dg_ref[...] = jnp.zeros_like(dg_ref)
            db_ref[...] = jnp.zeros_like(db_ref)

        @pl.when(k == nk - 1)
        def _():
            for c0 in range(0, tm, ch):
                rows = slice(c0, c0 + ch)
                dy = acc_ref[rows, :] + ALPHA * add_ref[rows, :]
                xhv = xh_ref[rows, :]
                dr_ref[rows, :] = _ln_dx(dy * g_ref[...], xhv, rstd_ref[rows, :])
                dg_ref[...] += jnp.sum(dy * xhv, axis=0, keepdims=True)
                db_ref[...] += jnp.sum(dy, axis=0, keepdims=True)

    row = pl.BlockSpec((tm, D), lambda i, k: (i, 0))
    vec = pl.BlockSpec((1, D), lambda i, k: (0, 0))
    return pl.pallas_call(
        body, name=name, grid=(S // tm, nk),
        in_specs=[pl.BlockSpec(memory_space=pl.ANY)] * na
        + [pl.BlockSpec((tm, tk), lambda i, k: (i, k)), pl.BlockSpec((D, tk), lambda i, k: (0, k)), row, row,
           pl.BlockSpec((tm, 1), lambda i, k: (i, 0)), vec],
        out_specs=[row, vec, vec],
        out_shape=[jax.ShapeDtypeStruct((S, D), F32), jax.ShapeDtypeStruct((1, D), F32),
                   jax.ShapeDtypeStruct((1, D), F32)],
        scratch_shapes=[pltpu.VMEM((tm, D), F32)],
        compiler_params=_cp(("arbitrary", "arbitrary"), vmem_mib=56),
    )(*after, dproj, w_in, dr2, xh, rstd, ln_g)


def _ln_bwd(dout, xh, rstd, ln_g, name, after=()):
    tm = 512
    na = len(after)

    def body(*refs):
        y_ref, xh_ref, rstd_ref, g_ref, dr_ref, dg_ref, db_ref = refs[na:]
        dy = y_ref[...]
        i = pl.program_id(0)
        xh = xh_ref[...]
        dr_ref[...] = _ln_dx(dy * g_ref[...], xh, rstd_ref[...])
        dg = jnp.sum(dy * xh, axis=0, keepdims=True)
        db = jnp.sum(dy, axis=0, keepdims=True)

        @pl.when(i == 0)
        def _():
            dg_ref[...] = dg
            db_ref[...] = db

        @pl.when(i > 0)
        def _():
            dg_ref[...] += dg
            db_ref[...] += db

    row = pl.BlockSpec((tm, D), lambda i: (i, 0))
    vec = pl.BlockSpec((1, D), lambda i: (0, 0))
    return pl.pallas_call(
        body, name=name, grid=(S // tm,),
        in_specs=[pl.BlockSpec(memory_space=pl.ANY)] * na + [row, row, pl.BlockSpec((tm, 1), lambda i: (i, 0)), vec],
        out_specs=[row, vec, vec],
        out_shape=[jax.ShapeDtypeStruct((S, D), F32), jax.ShapeDtypeStruct((1, D), F32),
                   jax.ShapeDtypeStruct((1, D), F32)],
        compiler_params=_cp(("arbitrary",)),
    )(*after, dout, xh, rstd, ln_g)


ROPE_TM = 256


def _rope_tables(pos_ref, invf_ref, sign):
    ang = pos_ref[...] * invf_ref[...]
    lane = lax.broadcasted_iota(jnp.int32, ang.shape, 1)
    first = (lane % DH) < (DH // 2)
    sinv = jnp.sin(ang) * sign
    return first, jnp.cos(ang), jnp.where(first, -sinv, sinv)


def _rotate(x, first, cosf, sinf):
    return x * cosf + jnp.where(first, pltpu.roll(x, 96, 1), pltpu.roll(x, 32, 1)) * sinf


def _proj_qkv_rope(hb, w_in, pos_f, invf, name):
    tm = 2 * ROPE_TM

    def body(h_ref, w_ref, pos_ref, invf_ref, o0_ref, o1_ref, o2_ref, buf_ref):
        rot = pl.program_id(1) < 2
        first, cosf, sinf = _rope_tables(pos_ref, invf_ref, 1.0)
        cosf = jnp.where(rot, cosf, 1.0)
        sinf = jnp.where(rot, sinf, 0.0)
        acc = _dot(h_ref[...], w_ref[...])
        for gi, (d, o_ref) in enumerate(zip(DILATIONS, (o0_ref, o1_ref, o2_ref))):
            for ch in range(GRP_W // 128):
                cols = slice(ch * 128, (ch + 1) * 128)
                x = _rotate(acc[:, gi * GRP_W + ch * 128: gi * GRP_W + (ch + 1) * 128], first, cosf, sinf)
                if d == 1:
                    o_ref[0, :, cols] = x.astype(BF16)
                else:
                    buf_ref[...] = x
                    for r in range(d):
                        o_ref[r, :, cols] = buf_ref[pl.ds(r, tm // d, stride=d), :].astype(BF16)

    return pl.pallas_call(
        body, name=name, grid=(S // tm, 3),
        in_specs=[pl.BlockSpec((tm, D), lambda i, s: (i, 0)), pl.BlockSpec((D, ATT_W), lambda i, s: (0, s)),
                  pl.BlockSpec((tm, 1), lambda i, s: (i, 0)), pl.BlockSpec((1, 128), lambda i, s: (0, 0))],
        out_specs=[pl.BlockSpec((d, tm // d, GRP_W), lambda i, s: (0, i, s)) for d in DILATIONS],
        out_shape=[jax.ShapeDtypeStruct((d, S // d, 3 * GRP_W), BF16) for d in DILATIONS],
        scratch_shapes=[pltpu.VMEM((tm, 128), F32)],
        compiler_params=_cp(("parallel", "parallel")),
    )(hb, w_in, pos_f, invf)


def _rope_bwd(dqkv_c, dz, dgl, pos_f, invf, name):
    tm = ROPE_TM

    def body(*refs):
        g_refs, (dz_ref, dgl_ref, pos_ref, invf_ref, o_ref, buf_ref) = refs[:9], refs[9:]
        o_ref[:, 3 * ATT_W:3 * ATT_W + 2 * GW] = dz_ref[...]
        o_ref[:, 3 * ATT_W + 2 * GW:IN_W] = dgl_ref[...]
        first, cosf, sinf = _rope_tables(pos_ref, invf_ref, -1.0)
        for sec in range(3):
            for gi, d in enumerate(DILATIONS):
                g_ref = g_refs[3 * gi + sec]
                for ch in range(GRP_W // 128):
                    cols = slice(ch * 128, (ch + 1) * 128)
                    if d == 1:
                        x = g_ref[0, :, cols]
                    else:
                        for r in range(d):
                            buf_ref[pl.ds(r, tm // d, stride=d), :] = g_ref[r, :, cols]
                        x = buf_ref[...]
                    if sec < 2:
                        x = _rotate(x, first, cosf, sinf)
                    dst = sec * ATT_W + gi * GRP_W + ch * 128
                    o_ref[:, dst:dst + 128] = x.astype(BF16)

    g_specs = [pl.BlockSpec((d, tm // d, GRP_W), lambda i: (0, i, 0)) for d in DILATIONS for _ in range(3)]
    return pl.pallas_call(
        body, name=name, grid=(S // tm,),
        in_specs=g_specs + [pl.BlockSpec((tm, 2 * GW), lambda i: (i, 0)), pl.BlockSpec((tm, 2 * D), lambda i: (i, 0)),
                            pl.BlockSpec((tm, 1), lambda i: (i, 0)), pl.BlockSpec((1, 128), lambda i: (0, 0))],
        out_specs=pl.BlockSpec((tm, IN_W), lambda i: (i, 0)),
        out_shape=jax.ShapeDtypeStruct((S, IN_W), BF16),
        scratch_shapes=[pltpu.VMEM((tm, 128), F32)],
        compiler_params=_cp(("parallel",)),
    )(*[g for grp in dqkv_c for g in grp], dz, dgl, pos_f, invf)


def _class_order(ts, name):
    tm = ROPE_TM
    n = len(ts)

    def body(*refs):
        buf_ref = refs[3 * n]
        for a in range(n):
            for ch in range(GRP_W // 128):
                cols = slice(ch * 128, (ch + 1) * 128)
                buf_ref[...] = refs[a][:, cols]
                for b, d in enumerate(DILATIONS[1:]):
                    for r in range(d):
                        refs[n + 2 * a + b][r, :, cols] = buf_ref[pl.ds(r, tm // d, stride=d), :]

    return pl.pallas_call(
        body, name=name, grid=(S // tm,),
        in_specs=[pl.BlockSpec((tm, GRP_W), lambda i: (i, 0))] * n,
        out_specs=[pl.BlockSpec((d, tm // d, GRP_W), lambda i: (0, i, 0)) for _ in range(n) for d in DILATIONS[1:]],
        out_shape=[jax.ShapeDtypeStruct((d, S // d, GRP_W), F32) for _ in range(n) for d in DILATIONS[1:]],
        scratch_shapes=[pltpu.VMEM((tm, 128), F32)],
        compiler_params=_cp(("parallel",)),
    )(*ts)


def _own_lanes(h):
    return (lax.broadcasted_iota(jnp.int32, (1, 2 * DH), 1) // DH) == (h % 2)


def _heads(ref):
    out = []
    for h in range(NH):
        pair = ref[:, (h // 2) * 2 * DH:(h // 2 + 1) * 2 * DH]
        out.append(jnp.where(_own_lanes(h), pair, jnp.zeros_like(pair)))
    return jnp.stack(out)


def _unheads(t3):
    return jnp.concatenate([t3[2 * p] + t3[2 * p + 1] for p in range(NH // 2)], axis=1)


def _bdot_nt(a, b):
    return lax.dot_general(a, b, (((2,), (2,)), ((0,), (0,))), preferred_element_type=F32)


def _bdot(a, b):
    return lax.dot_general(a, b, (((2,), (1,)), ((0,), (0,))), preferred_element_type=F32)


def _bdot_tn(a, b):
    return lax.dot_general(a, b, (((1,), (1,)), ((0,), (0,))), preferred_element_type=F32)


def _attn_fwd(gi, qkv_c, name):
    d = DILATIONS[gi]
    nblk = S // d // BLK

    def body(*refs):
        if nblk > 1:
            q_ref, kc_ref, kp_ref, vc_ref, vp_ref, o_ref, lse_ref = refs
            has_prev = pl.program_id(1) != 0
        else:
            q_ref, kc_ref, vc_ref, o_ref, lse_ref = refs
        qi = lax.broadcasted_iota(jnp.int32, (NH, BLK, BLK), 1)
        kj = lax.broadcasted_iota(jnp.int32, (NH, BLK, BLK), 2)
        q = _heads(q_ref)
        sc = jnp.where(kj <= qi, _bdot_nt(q, _heads(kc_ref)) * 0.125, NEG_INF)
        m = jnp.max(sc, axis=-1, keepdims=True)
        if nblk > 1:
            mask_p = jnp.logical_and(kj >= qi, has_prev)
            sp = jnp.where(mask_p, _bdot_nt(q, _heads(kp_ref)) * 0.125, NEG_INF)
            m = jnp.maximum(m, jnp.max(sp, axis=-1, keepdims=True))
        pc = jnp.exp(sc - m)
        l = jnp.sum(pc, axis=-1, keepdims=True)
        o = _bdot(pc.astype(BF16), _heads(vc_ref))
        if nblk > 1:
            pp = jnp.exp(sp - m)
            l = l + jnp.sum(pp, axis=-1, keepdims=True)
            o = o + _bdot(pp.astype(BF16), _heads(vp_ref))
        o_ref[...] = _unheads(o / l)
        lse = jnp.broadcast_to(m + jnp.log(l), (NH, BLK, 2 * DH))
        lse_ref[...] = _unheads(jnp.stack([jnp.where(_own_lanes(h), lse[h], 0.0) for h in range(NH)]))

    def cur(sec):
        return pl.BlockSpec((None, BLK, GRP_W), lambda r, n: (r, n, sec))

    def prev(sec):
        return pl.BlockSpec((None, BLK, GRP_W), lambda r, n: (r, jnp.maximum(n - 1, 0), sec))

    out = pl.BlockSpec((None, BLK, GRP_W), lambda r, n: (r, n, 0))
    shp = jax.ShapeDtypeStruct((d, S // d, GRP_W), F32)
    if nblk > 1:
        in_specs, args = [cur(0), cur(1), prev(1), cur(2), prev(2)], (qkv_c,) * 5
    else:
        in_specs, args = [cur(0), cur(1), cur(2)], (qkv_c,) * 3
    return pl.pallas_call(
        body, name=name, grid=(d, nblk), in_specs=in_specs, out_specs=[out, out], out_shape=[shp, shp],
        compiler_params=_cp(("parallel", "parallel")),
    )(*args)


def _attn_combine(os, lses, name):
    tm = ROPE_TM

    def body(o0_ref, o1_ref, o2_ref, l0_ref, l1_ref, l2_ref, y_ref, yt_ref, l_ref, buf_ref):
        def token_order(ref, d, cols, slot):
            if d == 1:
                return ref[0, :, cols]
            for r in range(d):
                buf_ref[slot, pl.ds(r, tm // d, stride=d), :] = ref[r, :, cols]
            return buf_ref[slot]

        for ch in range(GRP_W // 128):
            cols = slice(ch * 128, (ch + 1) * 128)
            o = [token_order(ref, d, cols, k) for k, (ref, d) in enumerate(zip((o0_ref, o1_ref, o2_ref), DILATIONS))]
            ls = [token_order(ref, d, cols, 3 + k)
                  for k, (ref, d) in enumerate(zip((l0_ref, l1_ref, l2_ref), DILATIONS))]
            m = jnp.maximum(jnp.maximum(ls[0], ls[1]), ls[2])
            e = [jnp.exp(l - m) for l in ls]
            den = e[0] + e[1] + e[2]
            y = (e[0] * o[0] + e[1] * o[1] + e[2] * o[2]) / den
            y_ref[:, cols] = y
            yt_ref[cols, :] = y.T.astype(BF16)
            l_ref[:, cols] = m + jnp.log(den)

    blk = pl.BlockSpec((tm, GRP_W), lambda i: (i, 0))
    cls = [pl.BlockSpec((d, tm // d, GRP_W), lambda i: (0, i, 0)) for d in DILATIONS]
    shp = jax.ShapeDtypeStruct((S, GRP_W), F32)
    return pl.pallas_call(
        body, name=name, grid=(S // tm,), in_specs=cls + cls,
        out_specs=[blk, pl.BlockSpec((GRP_W, tm), lambda i: (0, i)), blk],
        out_shape=[shp, jax.ShapeDtypeStruct((GRP_W, S), BF16), shp],
        scratch_shapes=[pltpu.VMEM((6, tm, 128), F32)],
        compiler_params=_cp(("parallel",)),
    )(*os, *lses)


def _attn_bwd(gi, qkv_c, dy_c, y_c, lse_c, name):
    d = DILATIONS[gi]
    nblk = S // d // BLK

    def body(*refs):
        if nblk > 1:
            (q_ref, qn_ref, k_ref, kp_ref, v_ref, vp_ref, dy_ref, dyn_ref, y_ref, yn_ref, l_ref, ln_ref,
             dq_ref, dk_ref, dv_ref) = refs
            n = pl.program_id(1)
            has_prev = n != 0
            has_next = n != nblk - 1
        else:
            q_ref, k_ref, v_ref, dy_ref, y_ref, l_ref, dq_ref, dk_ref, dv_ref = refs
        qi = lax.broadcasted_iota(jnp.int32, (NH, BLK, BLK), 1)
        kj = lax.broadcasted_iota(jnp.int32, (NH, BLK, BLK), 2)

        def lse_col(ref):
            return jnp.stack([ref[:, h * DH:h * DH + 1] for h in range(NH)])

        q, k, v = _heads(q_ref), _heads(k_ref), _heads(v_ref)
        dy = _heads(dy_ref)
        dd = jnp.sum(dy * _heads(y_ref), axis=-1, keepdims=True)
        lcol = lse_col(l_ref)
        dyb = dy.astype(BF16)
        p = jnp.exp(jnp.where(kj <= qi, _bdot_nt(q, k) * 0.125, NEG_INF) - lcol)
        ds = (p * (_bdot_nt(dyb, v) - dd)).astype(BF16)
        dq = _bdot(ds, k)
        dk = _bdot_tn(ds, q)
        dv = _bdot_tn(p.astype(BF16), dyb)
        if nblk > 1:
            qn, kpv, vpv = _heads(qn_ref), _heads(kp_ref), _heads(vp_ref)
            dyn = _heads(dyn_ref)
            ddn = jnp.sum(dyn * _heads(yn_ref), axis=-1, keepdims=True)
            lncol = lse_col(ln_ref)
            dynb = dyn.astype(BF16)
            mask_p = jnp.logical_and(kj >= qi, has_prev)
            pp = jnp.exp(jnp.where(mask_p, _bdot_nt(q, kpv) * 0.125, NEG_INF) - lcol)
            dsp = (pp * (_bdot_nt(dyb, vpv) - dd)).astype(BF16)
            dq = dq + _bdot(dsp, kpv)
            mask_n = jnp.logical_and(kj >= qi, has_next)
            pn = jnp.exp(jnp.where(mask_n, _bdot_nt(qn, k) * 0.125, NEG_INF) - lncol)
            dsn = (pn * (_bdot_nt(dynb, v) - ddn)).astype(BF16)
            dk = dk + _bdot_tn(dsn, qn)
            dv = dv + _bdot_tn(pn.astype(BF16), dynb)
        dq_ref[...] = _unheads(dq) * 0.125
        dk_ref[...] = _unheads(dk) * 0.125
        dv_ref[...] = _unheads(dv)

    def spec(sec, shift):
        def idx(r, n):
            return (r, jnp.clip(n + shift, 0, nblk - 1), sec)
        return pl.BlockSpec((None, BLK, GRP_W), idx)

    if nblk > 1:
        in_specs = [spec(0, 0), spec(0, 1), spec(1, 0), spec(1, -1), spec(2, 0), spec(2, -1),
                    spec(0, 0), spec(0, 1), spec(0, 0), spec(0, 1), spec(0, 0), spec(0, 1)]
        args = (qkv_c,) * 6 + (dy_c, dy_c, y_c, y_c, lse_c, lse_c)
    else:
        in_specs = [spec(0, 0), spec(1, 0), spec(2, 0), spec(0, 0), spec(0, 0), spec(0, 0)]
        args = (qkv_c, qkv_c, qkv_c, dy_c, y_c, lse_c)
    out = spec(0, 0)
    shp = jax.ShapeDtypeStruct((d, S // d, GRP_W), F32)
    return pl.pallas_call(
        body, name=name, grid=(d, nblk), in_specs=in_specs, out_specs=[out, out, out], out_shape=[shp, shp, shp],
        compiler_params=_cp(("parallel", "parallel")),
    )(*args)


_SQRT_HALF = 0.7071067811865476
_INV_SQRT_2PI = 0.3989422804014327


def _gelu(z):
    return 0.5 * z * (1.0 + lax.erf(z * _SQRT_HALF))


def _gelu_grad(z):
    return 0.5 * (1.0 + lax.erf(z * _SQRT_HALF)) + z * (jnp.exp(-0.5 * z * z) * _INV_SQRT_2PI)


def _tril_mask():
    t = lax.broadcasted_iota(jnp.int32, (BLK, BLK), 0)
    s = lax.broadcasted_iota(jnp.int32, (BLK, BLK), 1)
    return s <= t


def _groups(t):
    return jnp.stack([t[:, g * BLK:(g + 1) * BLK] for g in range(8)])


def _ungroup(t3):
    return jnp.concatenate([t3[g] for g in range(8)], axis=1)


def _group_bias(bs_ref):
    return jnp.stack([bs_ref[:, g:g + 1] for g in range(8)])


def _gmlp_fwd(z, ln_g, ln_b, w_s, b_s_t, name):
    def body(z_ref, g_ref, b_ref, ws_ref, bs_ref, y_ref, yt_ref):
        zg = _gelu(z_ref[...])
        u = zg[:, :GW]
        xh, _ = _ln_stats(zg[:, GW:])
        vn = (xh * g_ref[...] + b_ref[...]).astype(BF16)
        wt = jnp.where(_tril_mask(), ws_ref[...], 0.0).astype(BF16)
        yv = u * _ungroup(_bdot(wt, _groups(vn)) + _group_bias(bs_ref))
        y_ref[...] = yv.astype(BF16)
        yt_ref[...] = yv.T.astype(BF16)

    vec = pl.BlockSpec((1, GW), lambda n: (0, 0))
    return pl.pallas_call(
        body, name=name, grid=(NBLK,),
        in_specs=[pl.BlockSpec((BLK, 2 * GW), lambda n: (n, 0)), vec, vec,
                  pl.BlockSpec((8, BLK, BLK), lambda n: (0, 0, 0)), pl.BlockSpec((BLK, 8), lambda n: (0, 0))],
        out_specs=[pl.BlockSpec((BLK, GW), lambda n: (n, 0)), pl.BlockSpec((GW, BLK), lambda n: (0, n))],
        out_shape=[jax.ShapeDtypeStruct((S, GW), BF16), jax.ShapeDtypeStruct((GW, S), BF16)],
        compiler_params=_cp(("parallel",)),
    )(z, ln_g, ln_b, w_s, b_s_t)


def _gmlp_bwd(z, dy, ln_g, ln_b, w_s, b_s_t, name):
    def body(z_ref, dy_ref, g_ref, b_ref, ws_ref, bs_ref, dz_ref, dws_ref, dbs_ref, dg_ref, db_ref, dvn_ref):
        n = pl.program_id(0)
        zv = z_ref[...]
        zg = _gelu(zv)
        u = zg[:, :GW]
        xh, rstd = _ln_stats(zg[:, GW:])
        vn = (xh * g_ref[...] + b_ref[...]).astype(BF16)
        tril = _tril_mask()

        @pl.when(n == 0)
        def _():
            dws_ref[...] = jnp.zeros_like(dws_ref)
            dbs_ref[...] = jnp.zeros_like(dbs_ref)
            dg_ref[...] = jnp.zeros_like(dg_ref)
            db_ref[...] = jnp.zeros_like(db_ref)

        wt = jnp.where(tril, ws_ref[...], 0.0).astype(BF16)
        vn3 = _groups(vn)
        dyv = dy_ref[...]
        mixed = _ungroup(_bdot(wt, vn3) + _group_bias(bs_ref))
        dz_ref[:, :GW] = (dyv * mixed * _gelu_grad(zv[:, :GW])).astype(BF16)
        dmix3 = _groups(dyv * u)
        dmb = dmix3.astype(BF16)
        dws_ref[...] += jnp.where(tril, _bdot_nt(dmb, vn3), 0.0)
        dbsum = jnp.sum(dmix3, axis=-1, keepdims=True)
        for gg in range(8):
            dbs_ref[:, gg:gg + 1] += dbsum[gg]
        dvn_ref[...] = _ungroup(_bdot_tn(wt, dmb))

        dvn = dvn_ref[...]
        dg_ref[...] += jnp.sum(dvn * xh, axis=0, keepdims=True)
        db_ref[...] += jnp.sum(dvn, axis=0, keepdims=True)
        dvg = _ln_dx(dvn * g_ref[...], xh, rstd)
        dz_ref[:, GW:] = (dvg * _gelu_grad(zv[:, GW:])).astype(BF16)

    vec = pl.BlockSpec((1, GW), lambda n: (0, 0))
    ws = pl.BlockSpec((8, BLK, BLK), lambda n: (0, 0, 0))
    bs = pl.BlockSpec((BLK, 8), lambda n: (0, 0))
    return pl.pallas_call(
        body, name=name, grid=(NBLK,),
        in_specs=[pl.BlockSpec((BLK, 2 * GW), lambda n: (n, 0)), pl.BlockSpec((BLK, GW), lambda n: (n, 0)),
                  vec, vec, ws, bs],
        out_specs=[pl.BlockSpec((BLK, 2 * GW), lambda n: (n, 0)), ws, bs, vec, vec],
        out_shape=[jax.ShapeDtypeStruct((S, 2 * GW), BF16), jax.ShapeDtypeStruct((8, BLK, BLK), F32),
                   jax.ShapeDtypeStruct((BLK, 8), F32), jax.ShapeDtypeStruct((1, GW), F32),
                   jax.ShapeDtypeStruct((1, GW), F32)],
        scratch_shapes=[pltpu.VMEM((BLK, GW), F32)],
        compiler_params=_cp(("arbitrary",)),
    )(z, dy, ln_g, ln_b, w_s, b_s_t)


def _merge_fwd(a, b, gl, b_gates, name):
    tm = 512

    def body(a_ref, b_ref, g0_ref, g1_ref, bg_ref, o_ref, ot_ref):
        g0 = jax.nn.sigmoid(g0_ref[...] + bg_ref[:, :D])
        g1 = jax.nn.sigmoid(g1_ref[...] + bg_ref[:, D:])
        mg = g0 * a_ref[...] + g1 * b_ref[...]
        o_ref[...] = mg.astype(BF16)
        ot_ref[...] = mg.T.astype(BF16)

    row = pl.BlockSpec((tm, D), lambda i: (i, 0))
    return pl.pallas_call(
        body, name=name, grid=(S // tm,),
        in_specs=[row, row, row, pl.BlockSpec((tm, D), lambda i: (i, 1)), pl.BlockSpec((1, 2 * D), lambda i: (0, 0))],
        out_specs=[row, pl.BlockSpec((D, tm), lambda i: (0, i))],
        out_shape=[jax.ShapeDtypeStruct((S, D), BF16), jax.ShapeDtypeStruct((D, S), BF16)],
        compiler_params=_cp(("parallel",)),
    )(a, b, gl, gl, b_gates)


def _merge_bwd(dm, a, b, gl, b_gates, name):
    tm = 512

    def body(dm_ref, a_ref, b_ref, g0_ref, g1_ref, bg_ref, da_ref, db_ref, dgl_ref, dbg_ref):
        i = pl.program_id(0)
        dmv = dm_ref[...]
        g0 = jax.nn.sigmoid(g0_ref[...] + bg_ref[:, :D])
        g1 = jax.nn.sigmoid(g1_ref[...] + bg_ref[:, D:])
        da_ref[...] = (dmv * g0).astype(BF16)
        db_ref[...] = (dmv * g1).astype(BF16)
        d0 = dmv * a_ref[...] * g0 * (1.0 - g0)
        d1 = dmv * b_ref[...] * g1 * (1.0 - g1)
        dgl_ref[:, :D] = d0.astype(BF16)
        dgl_ref[:, D:] = d1.astype(BF16)
        s0 = jnp.sum(d0, axis=0, keepdims=True)
        s1 = jnp.sum(d1, axis=0, keepdims=True)

        @pl.when(i == 0)
        def _():
            dbg_ref[:, :D] = s0
            dbg_ref[:, D:] = s1

        @pl.when(i > 0)
        def _():
            dbg_ref[:, :D] += s0
            dbg_ref[:, D:] += s1

    row = pl.BlockSpec((tm, D), lambda i: (i, 0))
    wide = pl.BlockSpec((tm, 2 * D), lambda i: (i, 0))
    bg = pl.BlockSpec((1, 2 * D), lambda i: (0, 0))
    return pl.pallas_call(
        body, name=name, grid=(S // tm,),
        in_specs=[row, row, row, row, pl.BlockSpec((tm, D), lambda i: (i, 1)), bg],
        out_specs=[row, row, wide, bg],
        out_shape=[jax.ShapeDtypeStruct((S, D), BF16), jax.ShapeDtypeStruct((S, D), BF16),
                   jax.ShapeDtypeStruct((S, 2 * D), BF16), jax.ShapeDtypeStruct((1, 2 * D), F32)],
        compiler_params=_cp(("arbitrary",)),
    )(dm, a, b, gl, gl, b_gates)


def _adam_math(w, g, m, v):
    m2 = ADAM_B1 * m + (1.0 - ADAM_B1) * g
    v2 = ADAM_B2 * v + (1.0 - ADAM_B2) * (g * g)
    m_hat = m2 / (1.0 - ADAM_B1 ** ADAM_STEP)
    v_hat = v2 / (1.0 - ADAM_B2 ** ADAM_STEP)
    delta = -ADAM_LR * (m_hat / (jnp.sqrt(v_hat) + ADAM_EPS) + ADAM_WD * w)
    return delta, m2, v2


def _pick_rows(rows, cols, unit=16, budget=2 * MIB):
    best = unit
    for t in range(unit, rows + 1, unit):
        if rows % t == 0 and t * cols * 4 <= budget:
            best = t
    assert rows % best == 0
    return best


def _adamw(w, g, m, v, name):
    r, c = w.shape
    tr = _pick_rows(r, c, unit=8)

    def body(w_ref, g_ref, m_ref, v_ref, go_ref, d_ref, mo_ref, vo_ref):
        gv = g_ref[...]
        delta, m2, v2 = _adam_math(w_ref[...], gv, m_ref[...], v_ref[...])
        go_ref[...] = gv
        d_ref[...] = delta
        mo_ref[...] = m2
        vo_ref[...] = v2

    blk = pl.BlockSpec((tr, c), lambda i: (i, 0))
    shp = jax.ShapeDtypeStruct((r, c), F32)
    return pl.pallas_call(
        body, name=name, grid=(r // tr,), in_specs=[blk] * 4, out_specs=[blk] * 4, out_shape=[shp] * 4,
        compiler_params=_cp(("parallel",)),
    )(*[pltpu.with_memory_space_constraint(t, pltpu.HBM) for t in (w, g, m, v)])


def _adamw_group(ws, gs, ms, vs, name):
    n = len(ws)
    r, c = ws[0].shape
    tr = _pick_rows(r, c, unit=8, budget=3 * MIB // 4)
    nt = r // tr

    def body(*refs):
        a = pl.program_id(0)
        for t in range(n):
            @pl.when(a == t)
            def _(t=t):
                w_ref, g_ref, m_ref, v_ref = refs[4 * t:4 * t + 4]
                go_ref, d_ref, mo_ref, vo_ref = refs[4 * n + 4 * t:4 * n + 4 * t + 4]
                gv = g_ref[...]
                delta, m2, v2 = _adam_math(w_ref[...], gv, m_ref[...], v_ref[...])
                go_ref[...] = gv
                d_ref[...] = delta
                mo_ref[...] = m2
                vo_ref[...] = v2

    def spec(t):
        return pl.BlockSpec((tr, c), lambda a, i, t=t: (jnp.where(a == t, i, jnp.where(a < t, 0, nt - 1)), 0))

    specs = [spec(t) for t in range(n) for _ in range(4)]
    args = [x for t in range(n) for x in (ws[t], gs[t], ms[t], vs[t])]
    res = pl.pallas_call(
        body, name=name, grid=(n, nt), in_specs=specs, out_specs=specs,
        out_shape=[jax.ShapeDtypeStruct((r, c), F32)] * (4 * n),
        compiler_params=_cp(("arbitrary", "arbitrary")),
    )(*[pltpu.with_memory_space_constraint(x, pltpu.HBM) for x in args])
    return [res[4 * t:4 * t + 4] for t in range(n)]


def _small_sum_adamw(parts, own, pos, w, m, v, name):
    tr = 48

    def body(pos_ref, p_ref, own_ref, w_ref, m_ref, v_ref, g_ref, d_ref, mo_ref, vo_ref):
        me = 2 * pos_ref[1] + pos_ref[0]
        gv = None
        for k in range(8):
            term = jnp.where(me == k, own_ref[...], p_ref[k])
            gv = term if gv is None else gv + term
        delta, m2, v2 = _adam_math(w_ref[...], gv, m_ref[...], v_ref[...])
        g_ref[...] = gv
        d_ref[...] = delta
        mo_ref[...] = m2
        vo_ref[...] = v2

    blk = pl.BlockSpec((tr, D), lambda i, p: (i, 0))
    shp = jax.ShapeDtypeStruct((SMALL_ROWS, D), F32)
    return pl.pallas_call(
        body, name=name,
        grid_spec=pltpu.PrefetchScalarGridSpec(
            num_scalar_prefetch=1, grid=(SMALL_ROWS // tr,),
            in_specs=[pl.BlockSpec((8, tr, D), lambda i, p: (0, i, 0)), blk, blk, blk, blk],
            out_specs=[blk] * 4),
        out_shape=[shp] * 4,
        compiler_params=_cp(("parallel",)),
    )(pos, parts, own, w, m, v)


ANY = pl.BlockSpec(memory_space=pl.ANY)


def _in_hbm(arrays):
    return [pltpu.with_memory_space_constraint(a, pltpu.HBM) for a in arrays]


def _mesh_pos():
    x, y, c = lax.axis_index("x"), lax.axis_index("y"), lax.axis_index("c")
    chips = [(1 - x, y), (x, 1 - y), (1 - x, 1 - y)]
    return x, y, c, chips


def _place_shard(w, kind, pos, name):
    r, c = w.shape
    tr = _pick_rows(r, c)

    def body(pos_ref, w_ref, o_ref):
        o_ref[...] = w_ref[...].astype(BF16)

    if kind == "stack":
        o_spec = pl.BlockSpec((None, tr, c), lambda i, p: (p[1], i, 0))
        shape = (NSH, r, c)
    else:
        o_spec = pl.BlockSpec((tr, c), lambda i, p: (i, p[1]))
        shape = (r, NSH * c)
    return pl.pallas_call(
        body, name=name,
        grid_spec=pltpu.PrefetchScalarGridSpec(
            num_scalar_prefetch=1, grid=(r // tr,),
            in_specs=[pl.BlockSpec((tr, c), lambda i, p: (i, 0))], out_specs=o_spec),
        out_shape=pltpu.HBM(shape, BF16),
        compiler_params=_cp(("parallel",)),
    )(pos, pltpu.with_memory_space_constraint(w, pltpu.HBM))


SEM = pl.BlockSpec(memory_space=pltpu.SEMAPHORE)
SPLIT_COPY = pltpu.CompilerParams(has_side_effects=pltpu.SideEffectType.DATAFLOW_SIDE_EFFECTING)


def _shard_window(ref, kind, j, h, dims):
    r, c = dims
    rows = pl.ds(pl.multiple_of(h * (r // 2), 16), r // 2)
    if kind == "stack":
        return ref.at[j, rows, :]
    return ref.at[rows, pl.ds(pl.multiple_of(j * c, 128), c)]


def _ici_copy(ref, kind, dims, j, c, sems, idx, to):
    win = _shard_window(ref, kind, j, c, dims)
    return pltpu.make_async_remote_copy(src_ref=win, dst_ref=win, send_sem=sems[0].at[idx], recv_sem=sems[1].at[idx],
                                        device_id=to, device_id_type=MESH_T)


def _both_copy(ref, kind, dims, a, k, chip, half, tc, sc, sems):
    win = _shard_window(ref, kind, half[0], half[1], dims)
    return pltpu.make_async_remote_copy(src_ref=win, dst_ref=win, send_sem=sems[0].at[6 * a + 2 * k + tc],
                                        recv_sem=sems[1].at[6 * a + 2 * k + sc],
                                        device_id=(chip[0], chip[1], tc), device_id_type=MESH_T)


def _gather_start(fulls, kinds, dims, after, name, both=False):
    n, na = len(fulls), len(after)
    per = 6 if both else 3

    def body(*refs):
        outs = refs[n + na:2 * n + na]
        send_sems, recv_sems, token = refs[2 * n + na:]
        x, y, c, chips = _mesh_pos()
        for a in range(n):
            for k, chip in enumerate(chips):
                if both:
                    for tc in range(2):
                        _both_copy(outs[a], kinds[a], dims[a], a, k, chip, (2 * x + y, c), tc, c,
                                   (send_sems, recv_sems)).start()
                else:
                    _ici_copy(outs[a], kinds[a], dims[a], 2 * x + y, c, (send_sems, recv_sems), 3 * a + k,
                              (chip[0], chip[1], c)).start()
        token[...] = jnp.zeros_like(token)

    res = pl.pallas_call(
        body, name=name, in_specs=[ANY] * (n + na),
        out_specs=[ANY] * n + [SEM, SEM, pl.BlockSpec(memory_space=pltpu.VMEM)],
        out_shape=[pltpu.HBM(f.shape, BF16) for f in fulls]
        + [pltpu.SemaphoreType.DMA((per * n,)), pltpu.SemaphoreType.DMA((per * n,)),
           jax.ShapeDtypeStruct((8, 128), F32)],
        input_output_aliases={i: i for i in range(n)},
        compiler_params=SPLIT_COPY,
    )(*_in_hbm(fulls), *after)
    return res[:n], res[n], res[n + 1], res[n + 2]


def _gather_wait(fulls, send_sems, recv_sems, kinds, dims, after, name, both=False):
    n, na = len(fulls), len(after)

    def body(*refs):
        ssem, rsem = refs[n], refs[n + 1]
        outs = refs[n + 2 + na:]
        x, y, c, chips = _mesh_pos()
        for a in range(n):
            for k, chip in enumerate(chips):
                if both:
                    for oc in range(2):
                        _both_copy(outs[a], kinds[a], dims[a], a, k, chip, (2 * x + y, c), oc, c,
                                   (ssem, rsem)).wait_send()
                        _both_copy(outs[a], kinds[a], dims[a], a, k, chip, (2 * chip[0] + chip[1], oc), c, oc,
                                   (ssem, rsem)).wait_recv()
                    continue
                to = (chip[0], chip[1], c)
                _ici_copy(outs[a], kinds[a], dims[a], 2 * x + y, c, (ssem, rsem), 3 * a + k, to).wait_send()
                _ici_copy(outs[a], kinds[a], dims[a], 2 * chip[0] + chip[1], c, (ssem, rsem), 3 * a + k, to).wait_recv()

    return pl.pallas_call(
        body, name=name, in_specs=[ANY] * n + [SEM, SEM] + [ANY] * na, out_specs=[ANY] * n,
        out_shape=[pltpu.HBM(f.shape, BF16) for f in fulls],
        input_output_aliases={i: i for i in range(n)},
        compiler_params=SPLIT_COPY,
    )(*_in_hbm(fulls), send_sems, recv_sems, *after)


def _gather_forward(fulls, kinds, dims, name):
    n = len(fulls)

    def body(*refs):
        outs = refs[n:2 * n]
        sems = refs[2 * n:]
        x, y, c, chips = _mesh_pos()
        sib = (x, y, 1 - c)
        cps = []
        for a in range(n):
            for k, chip in enumerate(chips):
                cp = _ici_copy(outs[a], kinds[a], dims[a], 2 * chip[0] + chip[1], c, sems, 3 * a + k, sib)
                cp.start()
                cps.append(cp)
        for a in range(n):
            for k, chip in enumerate(chips):
                _ici_copy(outs[a], kinds[a], dims[a], 2 * chip[0] + chip[1], 1 - c, sems, 3 * a + k, sib).wait_recv()
        for cp in cps:
            cp.wait_send()

    return pl.pallas_call(
        body, name=name, in_specs=[ANY] * n, out_specs=[ANY] * n,
        out_shape=[pltpu.HBM(f.shape, BF16) for f in fulls],
        input_output_aliases={i: i for i in range(n)},
        scratch_shapes=[pltpu.SemaphoreType.DMA((3 * n,)), pltpu.SemaphoreType.DMA((3 * n,))],
    )(*_in_hbm(fulls))


def _pair_copy(src, land, a, x, y, c, sems):
    return pltpu.make_async_remote_copy(
        src_ref=src.at[1 - c], dst_ref=land, send_sem=sems[0].at[a], recv_sem=sems[1].at[a],
        device_id=(x, y, 1 - c), device_id_type=MESH_T)


def _pair_start(grads, lands, name):
    n = len(grads)

    def body(*refs):
        srcs, dsts = refs[2 * n:3 * n], refs[3 * n:4 * n]
        send_sems, recv_sems, token = refs[4 * n:]
        x, y, c, _ = _mesh_pos()
        for a in range(n):
            _pair_copy(srcs[a], dsts[a], a, x, y, c, (send_sems, recv_sems)).start()
        token[...] = jnp.zeros_like(token)

    res = pl.pallas_call(
        body, name=name, in_specs=[ANY] * (2 * n),
        out_specs=[ANY] * (2 * n) + [SEM, SEM, pl.BlockSpec(memory_space=pltpu.VMEM)],
        out_shape=[pltpu.HBM(g.shape, F32) for g in grads]
        + [pltpu.HBM(l.shape, F32) for l in lands]
        + [pltpu.SemaphoreType.DMA((n,)), pltpu.SemaphoreType.DMA((n,)), jax.ShapeDtypeStruct((8, 128), F32)],
        input_output_aliases={i: i for i in range(2 * n)},
        compiler_params=SPLIT_COPY,
    )(*_in_hbm(grads), *_in_hbm(lands))
    return res[:n], res[n:2 * n], res[2 * n], res[2 * n + 1], res[2 * n + 2]


def _pair_wait(grads, lands, send_sems, recv_sems, after, name):
    n, na = len(grads), len(after)

    def body(*refs):
        ssem, rsem = refs[2 * n], refs[2 * n + 1]
        outs = refs[2 * n + 2 + na:]
        x, y, c, _ = _mesh_pos()
        for a in range(n):
            cp = _pair_copy(outs[a], outs[n + a], a, x, y, c, (ssem, rsem))
            cp.wait_send()
            cp.wait_recv()

    res = pl.pallas_call(
        body, name=name, in_specs=[ANY] * (2 * n) + [SEM, SEM] + [ANY] * na, out_specs=[ANY] * (2 * n),
        out_shape=[pltpu.HBM(g.shape, F32) for g in grads]
        + [pltpu.HBM(l.shape, F32) for l in lands],
        input_output_aliases={i: i for i in range(2 * n)},
        compiler_params=SPLIT_COPY,
    )(*_in_hbm(grads), *_in_hbm(lands), send_sems, recv_sems, *after)
    return res[:n], res[n:]


def _pair_sum(g, recv, pos, name):
    _, _, rh, c = g.shape
    tr = _pick_rows(rh, c)

    def body(pos_ref, g_ref, r_ref, o_ref):
        o_ref[...] = (g_ref[...] + r_ref[...]).astype(BF16)

    return pl.pallas_call(
        body, name=name,
        grid_spec=pltpu.PrefetchScalarGridSpec(
            num_scalar_prefetch=1, grid=(3, rh // tr),
            in_specs=[pl.BlockSpec((None, None, tr, c), lambda k, r, p: (p[0], p[2 + k], r, 0)),
                      pl.BlockSpec((None, tr, c), lambda k, r, p: (p[2 + k], r, 0))],
            out_specs=pl.BlockSpec((None, tr, c), lambda k, r, p: (k, r, 0))),
        out_shape=pltpu.HBM((3, rh, c), BF16),
        compiler_params=_cp(("parallel", "parallel")),
    )(pos, *_in_hbm([g, recv]))


def _pair_sum_group(gs, recvs, pos, name):
    n = len(gs)
    _, _, rh, c = gs[0].shape

    def body(pos_ref, *refs):
        a = pl.program_id(0)
        for t in range(n):
            @pl.when(a == t)
            def _(t=t):
                refs[2 * n + t][...] = (refs[t][...] + refs[n + t][...]).astype(BF16)

    def slot(t, a, k):
        return jnp.where(a == t, k, jnp.where(a < t, 0, 2))

    g_specs = [pl.BlockSpec((None, None, rh, c), lambda a, k, p, t=t: (p[0], p[2 + slot(t, a, k)], 0, 0))
               for t in range(n)]
    r_specs = [pl.BlockSpec((None, rh, c), lambda a, k, p, t=t: (p[2 + slot(t, a, k)], 0, 0)) for t in range(n)]
    o_specs = [pl.BlockSpec((None, rh, c), lambda a, k, p, t=t: (slot(t, a, k), 0, 0)) for t in range(n)]
    return pl.pallas_call(
        body, name=name,
        grid_spec=pltpu.PrefetchScalarGridSpec(num_scalar_prefetch=1, grid=(n, 3), in_specs=g_specs + r_specs,
                                               out_specs=o_specs),
        out_shape=[pltpu.HBM((3, rh, c), BF16)] * n,
        compiler_params=_cp(("arbitrary", "arbitrary")),
    )(pos, *_in_hbm(list(gs) + list(recvs)))


def _chip_copy(src, land, a, k, chip, c, sems):
    return pltpu.make_async_remote_copy(
        src_ref=src.at[k], dst_ref=land.at[k], send_sem=sems[0].at[3 * a + k],
        recv_sem=sems[1].at[3 * a + k], device_id=(chip[0], chip[1], c), device_id_type=MESH_T)


def _chip_start(psums, lands, name):
    n = len(psums)

    def body(*refs):
        srcs, dsts = refs[2 * n:3 * n], refs[3 * n:4 * n]
        send_sems, recv_sems, token = refs[4 * n:]
        x, y, c, chips = _mesh_pos()
        for a in range(n):
            for k, chip in enumerate(chips):
                _chip_copy(srcs[a], dsts[a], a, k, chip, c, (send_sems, recv_sems)).start()
        token[...] = jnp.zeros_like(token)

    res = pl.pallas_call(
        body, name=name, in_specs=[ANY] * (2 * n),
        out_specs=[ANY] * (2 * n) + [SEM, SEM, pl.BlockSpec(memory_space=pltpu.VMEM)],
        out_shape=[pltpu.HBM(p.shape, BF16) for p in psums]
        + [pltpu.HBM(l.shape, BF16) for l in lands]
        + [pltpu.SemaphoreType.DMA((3 * n,)), pltpu.SemaphoreType.DMA((3 * n,)), jax.ShapeDtypeStruct((8, 128), F32)],
        input_output_aliases={i: i for i in range(2 * n)},
        compiler_params=SPLIT_COPY,
    )(*_in_hbm(psums), *_in_hbm(lands))
    return res[:n], res[n:2 * n], res[2 * n], res[2 * n + 1], res[2 * n + 2]


def _chip_wait(psums, lands, send_sems, recv_sems, after, name):
    n, na = len(psums), len(after)

    def body(*refs):
        ssem, rsem = refs[2 * n], refs[2 * n + 1]
        outs = refs[2 * n + 2 + na:]
        srcs, dsts = outs[:n], outs[n:]
        x, y, c, chips = _mesh_pos()
        for a in range(n):
            for k, chip in enumerate(chips):
                cp = _chip_copy(srcs[a], dsts[a], a, k, chip, c, (ssem, rsem))
                cp.wait_send()
                cp.wait_recv()

    res = pl.pallas_call(
        body, name=name, in_specs=[ANY] * (2 * n) + [SEM, SEM] + [ANY] * na, out_specs=[ANY] * (2 * n),
        out_shape=[pltpu.HBM(p.shape, BF16) for p in psums]
        + [pltpu.HBM(l.shape, BF16) for l in lands],
        input_output_aliases={i: i for i in range(2 * n)},
        compiler_params=SPLIT_COPY,
    )(*_in_hbm(psums), *_in_hbm(lands), send_sems, recv_sems, *after)
    return res[n:]


def _owner_sum(g, recv_a, recv_b, pos, name):
    _, _, rh, c = g.shape
    tr = _pick_rows(rh, c)

    def body(pos_ref, g_ref, ra_ref, rb_ref, o_ref):
        acc = g_ref[...] + ra_ref[...]
        for k in range(3):
            acc = acc + rb_ref[k].astype(F32)
        o_ref[...] = acc

    return pl.pallas_call(
        body, name=name,
        grid_spec=pltpu.PrefetchScalarGridSpec(
            num_scalar_prefetch=1, grid=(rh // tr,),
            in_specs=[pl.BlockSpec((None, None, tr, c), lambda r, p: (p[0], p[1], r, 0)),
                      pl.BlockSpec((None, tr, c), lambda r, p: (p[1], r, 0)),
                      pl.BlockSpec((3, tr, c), lambda r, p: (0, r, 0))],
            out_specs=pl.BlockSpec((None, tr, c), lambda r, p: (p[0], r, 0))),
        out_shape=pltpu.HBM((2, rh, c), F32),
        compiler_params=_cp(("parallel",)),
    )(pos, *_in_hbm([g, recv_a, recv_b]))


def _sibling_allgather(halves, name):
    n = len(halves)

    def body(*refs):
        outs = refs[n:2 * n]
        send_sems, recv_sems = refs[2 * n:]
        x, y, c, _ = _mesh_pos()
        cps = []
        for a in range(n):
            cp = pltpu.make_async_remote_copy(
                src_ref=outs[a].at[c], dst_ref=outs[a].at[c], send_sem=send_sems.at[a], recv_sem=recv_sems.at[a],
                device_id=(x, y, 1 - c), device_id_type=MESH_T)
            cp.start()
            cps.append(cp)
        for a in range(n):
            cps[a].wait_send()
            pltpu.make_async_remote_copy(
                src_ref=outs[a].at[1 - c], dst_ref=outs[a].at[1 - c], send_sem=send_sems.at[a],
                recv_sem=recv_sems.at[a], device_id=(x, y, 1 - c), device_id_type=MESH_T).wait_recv()

    return pl.pallas_call(
        body, name=name, in_specs=[ANY] * n, out_specs=[ANY] * n,
        out_shape=[pltpu.HBM(h.shape, F32) for h in halves],
        input_output_aliases={i: i for i in range(n)},
        scratch_shapes=[pltpu.SemaphoreType.DMA((n,)), pltpu.SemaphoreType.DMA((n,))],
    )(*_in_hbm(halves))


def _peers(x, y, c):
    rel = [(0, 0, 1), (0, 1, 0), (0, 1, 1), (1, 0, 0), (1, 0, 1), (1, 1, 0), (1, 1, 1)]
    return [((1 - x) if dx else x, (1 - y) if dy else y, (1 - c) if dc else c) for dx, dy, dc in rel]


def _small_copy(src, land, k, peer, slot, sems):
    return pltpu.make_async_remote_copy(src_ref=src, dst_ref=land.at[slot], send_sem=sems[0].at[k],
                                        recv_sem=sems[1].at[k], device_id=peer, device_id_type=MESH_T)


def _small_start(part, land, name):
    def body(p_in, l_in, p_ref, l_ref, send_sems, recv_sems, token):
        x, y, c, _ = _mesh_pos()
        for k, peer in enumerate(_peers(x, y, c)):
            _small_copy(p_ref, l_ref, k, peer, 4 * x + 2 * y + c, (send_sems, recv_sems)).start()
        token[...] = jnp.zeros_like(token)

    return pl.pallas_call(
        body, name=name, in_specs=[ANY, ANY],
        out_specs=[ANY, ANY, SEM, SEM, pl.BlockSpec(memory_space=pltpu.VMEM)],
        out_shape=[pltpu.HBM(part.shape, F32), pltpu.HBM(land.shape, F32), pltpu.SemaphoreType.DMA((7,)),
                   pltpu.SemaphoreType.DMA((7,)), jax.ShapeDtypeStruct((8, 128), F32)],
        input_output_aliases={0: 0, 1: 1},
        compiler_params=SPLIT_COPY,
    )(*_in_hbm([part, land]))


def _small_wait(part, land, send_sems, recv_sems, after, name):
    na = len(after)

    def body(*refs):
        ssem, rsem = refs[2], refs[3]
        p_ref, l_ref = refs[4 + na:]
        x, y, c, _ = _mesh_pos()
        for k, peer in enumerate(_peers(x, y, c)):
            cp = _small_copy(p_ref, l_ref, k, peer, 4 * peer[0] + 2 * peer[1] + peer[2], (ssem, rsem))
            cp.wait_send()
            cp.wait_recv()

    return pl.pallas_call(
        body, name=name, in_specs=[ANY, ANY, SEM, SEM] + [ANY] * na, out_specs=[ANY, ANY],
        out_shape=[pltpu.HBM(part.shape, F32), pltpu.HBM(land.shape, F32)],
        input_output_aliases={0: 0, 1: 1},
        compiler_params=SPLIT_COPY,
    )(*_in_hbm([part, land]), send_sems, recv_sems, *after)


def _pack_small(ln1_g, ln1_b, gln_g, gln_b, ln2_g, ln2_b, ln3_g, ln3_b, b_gates, b_s, w_s):
    rows = [ln1_g, ln1_b, gln_g, gln_b, ln2_g, ln2_b, ln3_g, ln3_b]
    rows = [r.reshape(1, D) for r in rows] + [b_gates.reshape(2, D), b_s.reshape(1, D), jnp.zeros((5, D), F32),
                                             w_s.reshape(128, D)]
    return jnp.concatenate(rows, axis=0)


def _unpack_small(p):
    out = [p[i:i + 1] for i in range(8)]
    return out + [p[8:10].reshape(1, 2 * D), p[10:11].reshape(1, 8, BLK), p[16:144].reshape(1, 8, BLK, BLK)]


GROUPS = (("f1g", "f1u", "f1d"), ("w_in",), ("w_ab", "w_gb", "w_out"), ("f2g", "f2u", "f2d"))
LATE_GROUPS = (2, 3)


def _local_step(x, pos_f, target, P, weights_of, grads_ready, flush, small_ready):
    invf = ROPE_THETA ** (-jnp.arange(0, DH, 2, dtype=F32) / DH)
    invf = jnp.tile(invf, 4).reshape(1, 128)
    b_s_t = P["gmlp_b_s"].T

    W = dict(weights_of(0, []))
    h1b, xh1, rstd1, a1, b1, h1t = _ffn_fwd(x, W["f1g"], W["f1u"], W["f1d"], P["ln1_g"], P["ln1_b"], "ffn1_fwd",
                                                emit_t=True)
    W.update(weights_of(1, [h1b]))
    qkv_c = _proj_qkv_rope(h1b, W["w_in"], pos_f, invf, "proj_qkv_rope")
    z = _matmul(h1b, W["w_in"], "nn", "proj_z", n=2 * GW, b_col0=3 * ATT_W, tm=S, tn=512)
    gl = _matmul(h1b, W["w_in"], "nn", "proj_gates", n=2 * D, b_col0=3 * ATT_W + 2 * GW, tm=S, tn=512)
    og = [_attn_fwd(gi, qkv_c[gi], "attn_fwd_g%d" % gi) for gi in range(NG)]
    y_attn, y_attn_t, lse = _attn_combine([o for o, _ in og], [l for _, l in og], "attn_combine")
    y_gmlp, y_gmlp_t = _gmlp_fwd(z, P["gmlp_ln_g"], P["gmlp_ln_b"], P["gmlp_w_s"], b_s_t, "gmlp_fwd")
    W.update(weights_of(2, [y_gmlp]))
    br_a = _matmul(y_attn, W["w_ab"], "nn", "branch_attn", n=D, tm=1024, tn=D)
    br_b = _matmul(y_gmlp, W["w_gb"], "nn", "branch_gmlp", n=D, tm=1024, tn=D)
    merged, merged_t = _merge_fwd(br_a, br_b, gl, P["b_gates"], "merge_fwd")
    h2, h2b, xh2, rstd2 = _resid_ln(xh1, P["ln1_g"], P["ln1_b"], merged, W["w_out"], P["ln2_g"], P["ln2_b"],
                                    "mix_resid_ln2")
    W.update(weights_of(3, [h2b]))
    dr3, a2, b2, dg3, db3, loss = _ffn_fwd(h2, W["f2g"], W["f2u"], W["f2d"], P["ln3_g"], P["ln3_b"],
                                           "ffn2_fwd_loss", target=target)

    g_f2g, g_f2u, g_f2d, dh2 = _ffn_bwd(dr3, h2b, a2, b2, W["f2g"], W["f2u"], W["f2d"], "ffn2_bwd")
    tok = grads_ready(3, dict(f2g=g_f2g, f2u=g_f2u, f2d=g_f2d))
    dr2, dg2, db2 = _ln_bwd(dh2, xh2, rstd2, P["ln2_g"], "ln2_bwd", after=tok)
    g_wout = _wgrad(merged_t, dr2, 128, D, "dw_out", row_sharded=True)
    dmerged = _matmul(dr2, W["w_out"], "nt", "dmerged", n=D, tm=1024, tn=D)
    dab, dbb, dglb, dbg = _merge_bwd(dmerged, br_a, br_b, gl, P["b_gates"], "merge_bwd")
    tok = flush([dab])
    g_wab = _wgrad(y_attn_t, dab, GRP_W // 2, 256, "dw_attn_branch", row_sharded=False, after=tok)
    g_wgb = _wgrad(y_gmlp_t, dbb, 128, D, "dw_gmlp_branch", row_sharded=True)
    tok = grads_ready(2, dict(w_ab=g_wab, w_gb=g_wgb, w_out=g_wout))
    dy_attn = _matmul(dab, W["w_ab"], "nt", "dy_attn", n=GRP_W, tm=1024, tn=GRP_W, after=tok)
    dy_gmlp = _matmul(dbb, W["w_gb"], "nt", "dy_gmlp", n=GW, tm=1024, tn=GW)
    dzb, dws, dbs_t, dgln_g, dgln_b = _gmlp_bwd(z, dy_gmlp, P["gmlp_ln_g"], P["gmlp_ln_b"], P["gmlp_w_s"], b_s_t,
                                                 "gmlp_bwd")
    cls = _class_order([dy_attn, y_attn, lse], "attn_class_order")
    dqkv_c = []
    for gi in range(NG):
        dy_c, y_c, lse_c = [t[None] if gi == 0 else cls[2 * a + gi - 1] for a, t in enumerate((dy_attn, y_attn, lse))]
        dqkv_c.append(_attn_bwd(gi, qkv_c[gi], dy_c, y_c, lse_c, "attn_bwd_g%d" % gi))
    dproj = _rope_bwd(dqkv_c, dzb, dglb, pos_f, invf, "rope_bwd")
    tok = flush([dproj])
    g_win = _wgrad(h1t, dproj, D // 2, IN_SH, "dw_in", row_sharded=False, after=tok)
    tok = grads_ready(1, dict(w_in=g_win))
    dr1, dg1, db1 = _dh1_ln_bwd(dproj, W["w_in"], dr2, xh1, rstd1, P["ln1_g"], "dh1_ln1_bwd", after=tok)
    tok = flush([dr1])
    tok = tok + small_ready(_pack_small(dg1, db1, dgln_g, dgln_b, dg2, db2, dg3, db3, dbg, dbs_t.T, dws))
    g_f1g, g_f1u, g_f1d, dx = _ffn_bwd(dr1, x.astype(BF16), a1, b1, W["f1g"], W["f1u"], W["f1d"], "ffn1_bwd",
                                       after=tok)
    grads_ready(0, dict(f1g=g_f1g, f1u=g_f1u, f1d=g_f1d))
    flush([dx])
    return loss, dx


TRANSPOSED = ("f1g", "f1u", "f2g", "f2u")
KIND = dict(f1g="stack", f1u="stack", f1d="stack", w_in="col", w_ab="col", w_gb="stack", w_out="stack",
            f2g="stack", f2u="stack", f2d="stack")


def kernel(x, positions, ffn1_w_gate, ffn1_w_up, ffn1_w_down, ln1_g, ln1_b, w_in, b_gates, gmlp_ln_g, gmlp_ln_b, gmlp_w_s, gmlp_b_s, w_attn_branch, w_gmlp_branch, w_out, ln2_g, ln2_b, ffn2_w_gate, ffn2_w_up, ffn2_w_down, ln3_g, ln3_b, loss_target, m_ffn1_w_gate, m_ffn1_w_up, m_ffn1_w_down, m_ln1_g, m_ln1_b, m_w_in, m_b_gates, m_gmlp_ln_g, m_gmlp_ln_b, m_gmlp_w_s, m_gmlp_b_s, m_w_attn_branch, m_w_gmlp_branch, m_w_out, m_ln2_g, m_ln2_b, m_ffn2_w_gate, m_ffn2_w_up, m_ffn2_w_down, m_ln3_g, m_ln3_b, v_ffn1_w_gate, v_ffn1_w_up, v_ffn1_w_down, v_ln1_g, v_ln1_b, v_w_in, v_b_gates, v_gmlp_ln_g, v_gmlp_ln_b, v_gmlp_w_s, v_gmlp_b_s, v_w_attn_branch, v_w_gmlp_branch, v_w_out, v_ln2_g, v_ln2_b, v_ffn2_w_gate, v_ffn2_w_up, v_ffn2_w_down, v_ln3_g, v_ln3_b):
    cx, cy, cc = lax.axis_index("x"), lax.axis_index("y"), lax.axis_index("c")
    pos = jnp.stack([cc, 2 * cx + cy, 2 * (1 - cx) + cy, 2 * cx + 1 - cy, 2 * (1 - cx) + 1 - cy]).astype(jnp.int32)

    w_sh = dict(f1g=ffn1_w_gate, f1u=ffn1_w_up, f1d=ffn1_w_down, w_in=w_in, w_ab=w_attn_branch,
                w_gb=w_gmlp_branch, w_out=w_out, f2g=ffn2_w_gate, f2u=ffn2_w_up, f2d=ffn2_w_down)
    m_sh = dict(f1g=m_ffn1_w_gate, f1u=m_ffn1_w_up, f1d=m_ffn1_w_down, w_in=m_w_in, w_ab=m_w_attn_branch,
                w_gb=m_w_gmlp_branch, w_out=m_w_out, f2g=m_ffn2_w_gate, f2u=m_ffn2_w_up, f2d=m_ffn2_w_down)
    v_sh = dict(f1g=v_ffn1_w_gate, f1u=v_ffn1_w_up, f1d=v_ffn1_w_down, w_in=v_w_in, w_ab=v_w_attn_branch,
                w_gb=v_w_gmlp_branch, w_out=v_w_out, f2g=v_ffn2_w_gate, f2u=v_ffn2_w_up, f2d=v_ffn2_w_down)
    w_sh = {k: (v[0].T if k in TRANSPOSED else v[0]) for k, v in w_sh.items()}
    m_sh = {k: (v[0].T if k in TRANSPOSED else v[0]) for k, v in m_sh.items()}
    v_sh = {k: (v[0].T if k in TRANSPOSED else v[0]) for k, v in v_sh.items()}

    started, tokens = [], []
    for gi, names in enumerate(GROUPS):
        placed = [_place_shard(w_sh[k], KIND[k], pos, "place_" + k) for k in names]
        fulls, ssem, rsem, token = _gather_start(placed, [KIND[k] for k in names], [w_sh[k].shape for k in names],
                                                 tokens[-1:], "gather_start_g%d" % gi, both=gi in LATE_GROUPS)
        started.append((fulls, ssem, rsem))
        tokens.append(token)

    def weights_of(gi, after):
        names = GROUPS[gi]
        kinds, dims = [KIND[k] for k in names], [w_sh[k].shape for k in names]
        fulls, ssem, rsem = started[gi]
        fulls = _gather_wait(fulls, ssem, rsem, kinds, dims, list(after) + (tokens if gi == 0 else []),
                             "gather_wait_g%d" % gi, both=gi in LATE_GROUPS)
        if gi not in LATE_GROUPS:
            fulls = _gather_forward(fulls, kinds, dims, "gather_forward_g%d" % gi)
        return {k: (f.reshape(D, D) if k in ("w_gb", "w_out") else f) for k, f in zip(names, fulls)}

    pending, inflight = [], {}

    def grads_ready(gi, gd):
        grads = [gd[k] for k in GROUPS[gi]]
        lands = [lax.empty(g.shape[1:], F32) for g in grads]
        grads, lands, ssem, rsem, token = _pair_start(grads, lands, "rs_pair_start_g%d" % gi)
        pending.append((gi, grads, lands, ssem, rsem))
        return [token]

    def flush(after):
        gi, grads, lands, ssem, rsem = pending.pop()
        names = GROUPS[gi]
        grads, recv_a = _pair_wait(grads, lands, ssem, rsem, after, "rs_pair_wait_g%d" % gi)
        if len(names) > 1 and len({g.shape for g in grads}) == 1:
            psums = _pair_sum_group(grads, recv_a, pos, "rs_pair_sum_g%d" % gi)
        else:
            psums = [_pair_sum(g, r, pos, "rs_pair_sum_" + k) for g, r, k in zip(grads, recv_a, names)]
        lands = [lax.empty((3,) + p.shape[1:], BF16) for p in psums]
        psums, lands, ssem, rsem, token = _chip_start(psums, lands, "rs_chip_start_g%d" % gi)
        inflight[gi] = (grads, recv_a, psums, lands, ssem, rsem, token)
        return [token]

    P = dict(ln1_g=ln1_g, ln1_b=ln1_b, ln2_g=ln2_g, ln2_b=ln2_b, ln3_g=ln3_g, ln3_b=ln3_b, b_gates=b_gates,
             gmlp_ln_g=gmlp_ln_g, gmlp_ln_b=gmlp_ln_b, gmlp_w_s=gmlp_w_s[0], gmlp_b_s=gmlp_b_s[0])
    pos_f = positions.reshape(S, 1).astype(F32)
    small_state = []

    def small_ready(packed):
        land = jnp.zeros((8, SMALL_ROWS, D), F32)
        packed, land, ssem, rsem, token = _small_start(packed, land, "small_start")
        small_state.append((packed, land, ssem, rsem))
        return [token]

    loss_part, dx = _local_step(x[0], pos_f, loss_target[0], P, weights_of, grads_ready, flush, small_ready)
    loss = lax.psum(loss_part[0, 0], ("x", "y", "c"))

    g_out, d_out, m_out, v_out = {}, {}, {}, {}

    def finish(gis, after, tag):
        names, halves = [], []
        for gi in gis:
            grads, recv_a, psums, lands, ssem, rsem, token = inflight[gi]
            recv_b = _chip_wait(psums, lands, ssem, rsem, after + [inflight[0][6]], "rs_chip_wait_g%d" % gi)
            halves += [_owner_sum(g, ra, rb, pos, "rs_owner_sum_" + k)
                       for g, ra, rb, k in zip(grads, recv_a, recv_b, GROUPS[gi])]
            names += GROUPS[gi]
            after = halves[-1:]
        reduced = dict(zip(names, _sibling_allgather(halves, "rs_sibling_allgather_" + tag)))
        for gi in gis:
            ks = GROUPS[gi]
            gfulls = [reduced[k].reshape(w_sh[k].shape) for k in ks]
            if len(ks) > 1 and len({w_sh[k].shape for k in ks}) == 1:
                results = _adamw_group([w_sh[k] for k in ks], gfulls, [m_sh[k] for k in ks], [v_sh[k] for k in ks],
                                       "adamw_g%d" % gi)
            else:
                results = [_adamw(w_sh[k], gf, m_sh[k], v_sh[k], "adamw_" + k) for k, gf in zip(ks, gfulls)]
            for k, res in zip(ks, results):
                after = [res[1]]
                if k in TRANSPOSED:
                    res = [r.T for r in res]
                g_out[k], d_out[k], m_out[k], v_out[k] = [r[None] for r in res]
        return after

    after = finish((3, 2, 1), [], "g321")

    small, parts = _small_wait(*small_state[0], after, "small_wait")
    sp = (ln1_g, ln1_b, gmlp_ln_g, gmlp_ln_b, ln2_g, ln2_b, ln3_g, ln3_b, b_gates, gmlp_b_s, gmlp_w_s)
    sm = (m_ln1_g, m_ln1_b, m_gmlp_ln_g, m_gmlp_ln_b, m_ln2_g, m_ln2_b, m_ln3_g, m_ln3_b, m_b_gates, m_gmlp_b_s,
          m_gmlp_w_s)
    sv = (v_ln1_g, v_ln1_b, v_gmlp_ln_g, v_gmlp_ln_b, v_ln2_g, v_ln2_b, v_ln3_g, v_ln3_b, v_b_gates, v_gmlp_b_s,
          v_gmlp_w_s)
    sg, sd, smn, svn = _small_sum_adamw(parts, small, pos, _pack_small(*sp), _pack_small(*sm), _pack_small(*sv),
                                        "small_adamw")
    names = ("ln1_g", "ln1_b", "gmlp_ln_g", "gmlp_ln_b", "ln2_g", "ln2_b", "ln3_g", "ln3_b", "b_gates", "gmlp_b_s",
             "gmlp_w_s")
    for dst, packed in ((g_out, sg), (d_out, sd), (m_out, smn), (v_out, svn)):
        for nm, val in zip(names, _unpack_small(packed)):
            dst[nm] = val
    finish((0,), [sg], "g0")

    order = ("f1g", "f1u", "f1d", "ln1_g", "ln1_b", "w_in", "b_gates", "gmlp_ln_g", "gmlp_ln_b", "gmlp_w_s", "gmlp_b_s",
             "w_ab", "w_gb", "w_out", "ln2_g", "ln2_b", "f2g", "f2u", "f2d", "ln3_g", "ln3_b")
    outs = [loss, dx[None]]
    for dst in (g_out, d_out, m_out, v_out):
        outs += [dst[k] for k in order]
    return tuple(outs)
```

```python
import jax
import jax.numpy as jnp
from jax import lax
from jax.experimental import pallas as pl
from jax.experimental.pallas import tpu as pltpu

F32 = jnp.float32
BF16 = jnp.bfloat16

S = 2048
D = 1024
NSH = 4
FSH = 704
ATT_W = 1536
GRP_W = 512
NG = 3
NH = 8
DH = 64
BLK = 128
NBLK = S // BLK
GW = 1024
IN_W = 8704
IN_SH = IN_W // NSH
ALPHA = 2.0 ** 0.25
LN_EPS = 1e-5
ROPE_THETA = 10000.0
DILATIONS = (1, 4, 16)
ADAM_LR, ADAM_B1, ADAM_B2, ADAM_EPS, ADAM_WD, ADAM_STEP = 0.001, 0.9, 0.999, 1e-08, 0.01, 10
SMALL_ROWS = 144
EPI_ROWS = 256
MESH_T = pl.DeviceIdType.MESH
MIB = 1024 * 1024
NEG_INF = float("-inf")


def _cp(sem, vmem_mib=48):
    return pltpu.CompilerParams(dimension_semantics=sem, vmem_limit_bytes=vmem_mib * MIB)


def _ln_stats(r):
    mu = jnp.mean(r, axis=-1, keepdims=True)
    xc = r - mu
    var = jnp.mean(xc * xc, axis=-1, keepdims=True)
    rstd = lax.rsqrt(var + LN_EPS)
    return xc * rstd, rstd


def _ln_dx(dxh, xh, rstd):
    m1 = jnp.mean(dxh, axis=-1, keepdims=True)
    m2 = jnp.mean(dxh * xh, axis=-1, keepdims=True)
    return rstd * (dxh - m1 - xh * m2)


def _dot_nt(a, b):
    return lax.dot_general(a, b, (((1,), (1,)), ((), ())), preferred_element_type=F32)


def _dot_tn(a, b):
    return lax.dot_general(a, b, (((0,), (0,)), ((), ())), preferred_element_type=F32)


def _dot(a, b):
    return jnp.dot(a, b, preferred_element_type=F32)


def _ffn_fwd(xin, wgt, wut, wd, ln_g, ln_b, name, emit_t=False, target=None):
    with_loss = target is not None
    tm = 1024

    def body(x_ref, wg_ref, wu_ref, wd_ref, g_ref, b_ref, *rest):
        if with_loss:
            t_ref, dr_ref, a_ref, bb_ref, dg_ref, db_ref, loss_ref, acc_ref = rest
        elif emit_t:
            hb_ref, xh_ref, rstd_ref, a_ref, bb_ref, ht_ref, acc_ref = rest
        else:
            hb_ref, xh_ref, rstd_ref, a_ref, bb_ref, acc_ref = rest
        i = pl.program_id(0)
        j = pl.program_id(1)
        xb = x_ref[...].astype(BF16)
        a = _dot_nt(xb, wg_ref[...])
        b = _dot_nt(xb, wu_ref[...])
        a_ref[...] = a.astype(BF16)
        bb_ref[...] = b.astype(BF16)
        s = (a * jax.nn.sigmoid(a)) * b
        f = _dot(s.astype(BF16), wd_ref[...])

        @pl.when(j == 0)
        def _():
            acc_ref[...] = f

        @pl.when(j > 0)
        def _():
            acc_ref[...] += f

        if with_loss:
            @pl.when(jnp.logical_and(j == NSH - 1, i == 0))
            def _():
                dg_ref[...] = jnp.zeros_like(dg_ref)
                db_ref[...] = jnp.zeros_like(db_ref)
                loss_ref[...] = jnp.zeros_like(loss_ref)

        @pl.when(j == NSH - 1)
        def _():
            for c0 in range(0, tm, EPI_ROWS):
                rows = slice(c0, c0 + EPI_ROWS)
                r = ALPHA * x_ref[rows, :] + 0.5 * acc_ref[rows, :]
                xh, rstd = _ln_stats(r)
                h = xh * g_ref[...] + b_ref[...]
                if with_loss:
                    err = h - t_ref[rows, :]
                    dy = err * (1.0 / D)
                    dr_ref[rows, :] = _ln_dx(dy * g_ref[...], xh, rstd)
                    dg_ref[...] += jnp.sum(dy * xh, axis=0, keepdims=True)
                    db_ref[...] += jnp.sum(dy, axis=0, keepdims=True)
                    part = 0.5 * jnp.sum(jnp.mean(err * err, axis=-1, keepdims=True), axis=0, keepdims=True)
                    loss_ref[...] += jnp.broadcast_to(part, (8, 128))
                else:
                    hb_ref[rows, :] = h.astype(BF16)
                    xh_ref[rows, :] = xh
                    rstd_ref[rows, :] = rstd
                    if emit_t:
                        ht_ref[:, rows] = h.T.astype(BF16)

    row = pl.BlockSpec((tm, D), lambda i, j: (i, 0))
    vec = pl.BlockSpec((1, D), lambda i, j: (0, 0))
    wsp = pl.BlockSpec((None, FSH, D), lambda i, j: (j, 0, 0))
    ab = pl.BlockSpec((None, tm, FSH), lambda i, j: (j, i, 0))
    ab_shape = jax.ShapeDtypeStruct((NSH, S, FSH), BF16)
    in_specs, args = [row, wsp, wsp, wsp, vec, vec], (xin, wgt, wut, wd, ln_g, ln_b)
    if with_loss:
        in_specs, args = in_specs + [row], args + (target,)
        out_specs = [row, ab, ab, vec, vec, pl.BlockSpec((8, 128), lambda i, j: (0, 0))]
        out_shape = [jax.ShapeDtypeStruct((S, D), F32), ab_shape, ab_shape, jax.ShapeDtypeStruct((1, D), F32),
                     jax.ShapeDtypeStruct((1, D), F32), jax.ShapeDtypeStruct((8, 128), F32)]
    else:
        out_specs = [row, row, pl.BlockSpec((tm, 1), lambda i, j: (i, 0)), ab, ab]
        out_shape = [jax.ShapeDtypeStruct((S, D), BF16), jax.ShapeDtypeStruct((S, D), F32),
                     jax.ShapeDtypeStruct((S, 1), F32), ab_shape, ab_shape]
        if emit_t:
            out_specs.append(pl.BlockSpec((D, tm), lambda i, j: (0, i)))
            out_shape.append(jax.ShapeDtypeStruct((D, S), BF16))
    return pl.pallas_call(
        body, name=name, grid=(S // tm, NSH), in_specs=in_specs, out_specs=out_specs, out_shape=out_shape,
        scratch_shapes=[pltpu.VMEM((tm, D), F32)],
        compiler_params=_cp(("arbitrary" if with_loss else "parallel", "arbitrary"), vmem_mib=56),
    )(*args)


def _ffn_bwd(dr, xin_b, a, b, wgt, wut, wd, name, after=()):
    tm = 512
    ni = S // tm
    hr = FSH // 2

    def body(dr_ref, a_ref, b_ref, wg_ref, wu_ref, wd_ref, x_hbm, *rest):
        dwg_hbm, dwu_hbm, dwd_hbm, dx_hbm, dx_acc, da_all, db_all, s_all, df_all, x_all, res_buf, sems = rest[len(after):]
        j = pl.program_id(0)
        i = pl.program_id(1)
        rows = pl.ds(pl.multiple_of(i * tm, tm), tm)

        @pl.when(jnp.logical_and(j == 0, i == 0))
        def _():
            cp = pltpu.make_async_copy(x_hbm, x_all, sems.at[0])
            cp.start()
            cp.wait()

        drv = dr_ref[...]
        df = (0.5 * drv).astype(BF16)

        @pl.when(j == 0)
        def _():
            df_all[rows, :] = df

        ds = jnp.concatenate([_dot_nt(df, wd_ref[0:384, :]), _dot_nt(df, wd_ref[384:FSH, :])], axis=1)
        av = a_ref[...].astype(F32)
        bv = b_ref[...].astype(F32)
        sig = jax.nn.sigmoid(av)
        sl = av * sig
        da = (ds * bv * (sig * (1.0 + av * (1.0 - sig)))).astype(BF16)
        db = (ds * sl).astype(BF16)
        da_all[rows, :] = da
        db_all[rows, :] = db
        s_all[rows, :] = (sl * bv).astype(BF16)
        dx = _dot(da, wg_ref[...]) + _dot(db, wu_ref[...])

        @pl.when(j == 0)
        def _():
            dx_acc[rows, :] = ALPHA * drv + dx

        @pl.when(j > 0)
        def _():
            dx_acc[rows, :] += dx

        @pl.when(i == ni - 1)
        def _():
            copies = []
            for n, (lhs, rhs, out) in enumerate(((da_all, x_all, dwg_hbm), (db_all, x_all, dwu_hbm),
                                                 (s_all, df_all, dwd_hbm))):
                slot = n % 2
                if n >= 2:
                    for cp in copies[2 * (n - 2): 2 * (n - 2) + 2]:
                        cp.wait()
                res_buf[slot] = _dot_tn(lhs[...], rhs[...])
                for h in range(2):
                    cp = pltpu.make_async_copy(res_buf.at[slot, pl.ds(h * hr, hr), :], out.at[h, j],
                                               sems.at[1 + 2 * slot + h])
                    cp.start()
                    copies.append(cp)
            for cp in copies[2:]:
                cp.wait()

        @pl.when(jnp.logical_and(j == NSH - 1, i == ni - 1))
        def _():
            cp = pltpu.make_async_copy(dx_acc, dx_hbm, sems.at[0])
            cp.start()
            cp.wait()

    row = pl.BlockSpec((tm, D), lambda j, i: (i, 0))
    wsp = pl.BlockSpec((None, FSH, D), lambda j, i: (j, 0, 0))
    ab = pl.BlockSpec((None, tm, FSH), lambda j, i: (j, i, 0))
    dwshape = jax.ShapeDtypeStruct((2, NSH, hr, D), F32)
    return pl.pallas_call(
        body, name=name, grid=(NSH, ni),
        in_specs=[row, ab, ab, wsp, wsp, wsp, ANY] + [ANY] * len(after),
        out_specs=[ANY, ANY, ANY, ANY],
        out_shape=[dwshape, dwshape, dwshape, jax.ShapeDtypeStruct((S, D), F32)],
        scratch_shapes=[pltpu.VMEM((S, D), F32), pltpu.VMEM((S, FSH), BF16), pltpu.VMEM((S, FSH), BF16),
                        pltpu.VMEM((S, FSH), BF16), pltpu.VMEM((S, D), BF16), pltpu.VMEM((S, D), BF16),
                        pltpu.VMEM((2, FSH, D), F32), pltpu.SemaphoreType.DMA((5,))],
        compiler_params=_cp(("arbitrary", "arbitrary"), vmem_mib=58),
    )(dr, a, b, wgt, wut, wd, xin_b, *after)


def _matmul(a, b, mode, name, *, n, tm, tn, b_col0=0, after=()):
    m, k = a.shape
    assert m % tm == 0 and n % tn == 0 and b_col0 % tn == 0
    off = b_col0 // tn
    na = len(after)

    def body(*refs):
        a_ref, b_ref, o_ref = refs[na:]
        av = a_ref[...].astype(BF16)
        o_ref[...] = _dot(av, b_ref[...]) if mode == "nn" else _dot_nt(av, b_ref[...])

    if mode == "nn":
        b_spec = pl.BlockSpec((k, tn), lambda i, j: (0, j + off))
    else:
        b_spec = pl.BlockSpec((tn, k), lambda i, j: (j, 0))
    return pl.pallas_call(
        body, name=name, grid=(m // tm, n // tn),
        in_specs=[pl.BlockSpec(memory_space=pl.ANY)] * na + [pl.BlockSpec((tm, k), lambda i, j: (i, 0)), b_spec],
        out_specs=pl.BlockSpec((tm, tn), lambda i, j: (i, j)),
        out_shape=jax.ShapeDtypeStruct((m, n), F32),
        compiler_params=_cp(("parallel", "parallel")),
    )(*after, a, b)


def _wgrad(xt, y, rh, c, name, row_sharded, after=()):
    na = len(after)
    if row_sharded:
        tc = 512

        def body(x_ref, y_ref, *rest):
            o_ref = rest[na]
            res = _dot(x_ref[...], y_ref[...].astype(BF16))
            for j in range(NSH):
                for h in range(2):
                    o_ref[h, j] = res[(2 * j + h) * rh:(2 * j + h + 1) * rh, :]

        grid = (c // tc,)
        in_specs = [pl.BlockSpec((2 * NSH * rh, S), lambda g: (0, 0)), pl.BlockSpec((S, tc), lambda g: (0, g))]
        out_specs = pl.BlockSpec((2, NSH, rh, tc), lambda g: (0, 0, 0, g))
        sem = ("parallel",)
    else:
        def body(x_ref, y_ref, *rest):
            rest[na][...] = _dot(x_ref[...], y_ref[...].astype(BF16))

        grid = (2, NSH)
        in_specs = [pl.BlockSpec((rh, S), lambda h, j: (h, 0)), pl.BlockSpec((S, c), lambda h, j: (0, j))]
        out_specs = pl.BlockSpec((None, None, rh, c), lambda h, j: (h, j, 0, 0))
        sem = ("parallel", "parallel")
    return pl.pallas_call(
        body, name=name, grid=grid, in_specs=in_specs + [pl.BlockSpec(memory_space=pl.ANY)] * na, out_specs=out_specs,
        out_shape=jax.ShapeDtypeStruct((2, NSH, rh, c), F32),
        compiler_params=_cp(sem, vmem_mib=56),
    )(xt, y, *after)


def _resid_ln(res_xh, res_g, res_b, a, w, ln_g, ln_b, name):
    tm = 512

    def body(rx_ref, rg_ref, rb_ref, a_ref, w_ref, g_ref, b_ref, h_ref, hb_ref, xh_ref, rstd_ref):
        r = ALPHA * (rx_ref[...] * rg_ref[...] + rb_ref[...]) + _dot(a_ref[...], w_ref[...])
        xh, rstd = _ln_stats(r)
        h = xh * g_ref[...] + b_ref[...]
        h_ref[...] = h
        hb_ref[...] = h.astype(BF16)
        xh_ref[...] = xh
        rstd_ref[...] = rstd

    row = pl.BlockSpec((tm, D), lambda i: (i, 0))
    vec = pl.BlockSpec((1, D), lambda i: (0, 0))
    return pl.pallas_call(
        body, name=name, grid=(S // tm,),
        in_specs=[row, vec, vec, row, pl.BlockSpec((D, D), lambda i: (0, 0)), vec, vec],
        out_specs=[row, row, row, pl.BlockSpec((tm, 1), lambda i: (i, 0))],
        out_shape=[jax.ShapeDtypeStruct((S, D), F32), jax.ShapeDtypeStruct((S, D), BF16),
                   jax.ShapeDtypeStruct((S, D), F32), jax.ShapeDtypeStruct((S, 1), F32)],
        compiler_params=_cp(("parallel",)),
    )(res_xh, res_g, res_b, a, w, ln_g, ln_b)


def _dh1_ln_bwd(dproj, w_in, dr2, xh, rstd, ln_g, name, after=()):
    tm, tk, ch = 1024, IN_SH, EPI_ROWS
    nk = IN_W // tk
    na = len(after)

    def body(*refs):
        a_ref, b_ref, add_ref, xh_ref, rstd_ref, g_ref, dr_ref, dg_ref, db_ref, acc_ref = refs[na:]
        i = pl.program_id(0)
        k = pl.program_id(1)
        p = _dot_nt(a_ref[...], b_ref[...])

        @pl.when(k == 0)
        def _():
            acc_ref[...] = p

        @pl.when(k > 0)
        def _():
            acc_ref[...] += p

        @pl.when(jnp.logical_and(k == nk - 1, i == 0))
        def _():
            dg_ref[...] = jnp.zeros_like(dg_ref)
            db_ref[...] = jnp.zeros_like(db_ref)

        @pl.when(k == nk - 1)
        def _():
            for c0 in range(0, tm, ch):
                rows = slice(c0, c0 + ch)
                dy = acc_ref[rows, :] + ALPHA * add_ref[rows, :]
                xhv = xh_ref[rows, :]
                dr_ref[rows, :] = _ln_dx(dy * g_ref[...], xhv, rstd_ref[rows, :])
                dg_ref[...] += jnp.sum(dy * xhv, axis=0, keepdims=True)
                db_ref[...] += jnp.sum(dy, axis=0, keepdims=True)

    row = pl.BlockSpec((tm, D), lambda i, k: (i, 0))
    vec = pl.BlockSpec((1, D), lambda i, k: (0, 0))
    return pl.pallas_call(
        body, name=name, grid=(S // tm, nk),
        in_specs=[pl.BlockSpec(memory_space=pl.ANY)] * na
        + [pl.BlockSpec((tm, tk), lambda i, k: (i, k)), pl.BlockSpec((D, tk), lambda i, k: (0, k)), row, row,
           pl.BlockSpec((tm, 1), lambda i, k: (i, 0)), vec],
        out_specs=[row, vec, vec],
        out_shape=[jax.ShapeDtypeStruct((S, D), F32), jax.ShapeDtypeStruct((1, D), F32),
                   jax.ShapeDtypeStruct((1, D), F32)],
        scratch_shapes=[pltpu.VMEM((tm, D), F32)],
        compiler_params=_cp(("arbitrary", "arbitrary"), vmem_mib=56),
    )(*after, dproj, w_in, dr2, xh, rstd, ln_g)


def _ln_bwd(dout, xh, rstd, ln_g, name, after=()):
    tm = 512
    na = len(after)

    def body(*refs):
        y_ref, xh_ref, rstd_ref, g_ref, dr_ref, dg_ref, db_ref = refs[na:]
        dy = y_ref[...]
        i = pl.program_id(0)
        xh = xh_ref[...]
        dr_ref[...] = _ln_dx(dy * g_ref[...], xh, rstd_ref[...])
        dg = jnp.sum(dy * xh, axis=0, keepdims=True)
        db = jnp.sum(dy, axis=0, keepdims=True)

        @pl.when(i == 0)
        def _():
            dg_ref[...] = dg
            db_ref[...] = db

        @pl.when(i > 0)
        def _():
            dg_ref[...] += dg
            db_ref[...] += db

    row = pl.BlockSpec((tm, D), lambda i: (i, 0))
    vec = pl.BlockSpec((1, D), lambda i: (0, 0))
    return pl.pallas_call(
        body, name=name, grid=(S // tm,),
        in_specs=[pl.BlockSpec(memory_space=pl.ANY)] * na + [row, row, pl.BlockSpec((tm, 1), lambda i: (i, 0)), vec],
        out_specs=[row, vec, vec],
        out_shape=[jax.ShapeDtypeStruct((S, D), F32), jax.ShapeDtypeStruct((1, D), F32),
                   jax.ShapeDtypeStruct((1, D), F32)],
        compiler_params=_cp(("arbitrary",)),
    )(*after, dout, xh, rstd, ln_g)


ROPE_TM = 256


def _rope_tables(pos_ref, invf_ref, sign):
    ang = pos_ref[...] * invf_ref[...]
    lane = lax.broadcasted_iota(jnp.int32, ang.shape, 1)
    first = (lane % DH) < (DH // 2)
    sinv = jnp.sin(ang) * sign
    return first, jnp.cos(ang), jnp.where(first, -sinv, sinv)


def _rotate(x, first, cosf, sinf):
    return x * cosf + jnp.where(first, pltpu.roll(x, 96, 1), pltpu.roll(x, 32, 1)) * sinf


def _proj_qkv_rope(hb, w_in, pos_f, invf, name):
    tm = 2 * ROPE_TM

    def body(h_ref, w_ref, pos_ref, invf_ref, o0_ref, o1_ref, o2_ref, buf_ref):
        rot = pl.program_id(1) < 2
        first, cosf, sinf = _rope_tables(pos_ref, invf_ref, 1.0)
        cosf = jnp.where(rot, cosf, 1.0)
        sinf = jnp.where(rot, sinf, 0.0)
        acc = _dot(h_ref[...], w_ref[...])
        for gi, (d, o_ref) in enumerate(zip(DILATIONS, (o0_ref, o1_ref, o2_ref))):
            for ch in range(GRP_W // 128):
                cols = slice(ch * 128, (ch + 1) * 128)
                x = _rotate(acc[:, gi * GRP_W + ch * 128: gi * GRP_W + (ch + 1) * 128], first, cosf, sinf)
                if d == 1:
                    o_ref[0, :, cols] = x.astype(BF16)
                else:
                    buf_ref[...] = x
                    for r in range(d):
                        o_ref[r, :, cols] = buf_ref[pl.ds(r, tm // d, stride=d), :].astype(BF16)

    return pl.pallas_call(
        body, name=name, grid=(S // tm, 3),
        in_specs=[pl.BlockSpec((tm, D), lambda i, s: (i, 0)), pl.BlockSpec((D, ATT_W), lambda i, s: (0, s)),
                  pl.BlockSpec((tm, 1), lambda i, s: (i, 0)), pl.BlockSpec((1, 128), lambda i, s: (0, 0))],
        out_specs=[pl.BlockSpec((d, tm // d, GRP_W), lambda i, s: (0, i, s)) for d in DILATIONS],
        out_shape=[jax.ShapeDtypeStruct((d, S // d, 3 * GRP_W), BF16) for d in DILATIONS],
        scratch_shapes=[pltpu.VMEM((tm, 128), F32)],
        compiler_params=_cp(("parallel", "parallel")),
    )(hb, w_in, pos_f, invf)


def _rope_bwd(dqkv_c, dz, dgl, pos_f, invf, name):
    tm = ROPE_TM

    def body(*refs):
        g_refs, (dz_ref, dgl_ref, pos_ref, invf_ref, o_ref, buf_ref) = refs[:9], refs[9:]
        o_ref[:, 3 * ATT_W:3 * ATT_W + 2 * GW] = dz_ref[...]
        o_ref[:, 3 * ATT_W + 2 * GW:IN_W] = dgl_ref[...]
        first, cosf, sinf = _rope_tables(pos_ref, invf_ref, -1.0)
        for sec in range(3):
            for gi, d in enumerate(DILATIONS):
                g_ref = g_refs[3 * gi + sec]
                for ch in range(GRP_W // 128):
                    cols = slice(ch * 128, (ch + 1) * 128)
                    if d == 1:
                        x = g_ref[0, :, cols]
                    else:
                        for r in range(d):
                            buf_ref[pl.ds(r, tm // d, stride=d), :] = g_ref[r, :, cols]
                        x = buf_ref[...]
                    if sec < 2:
                        x = _rotate(x, first, cosf, sinf)
                    dst = sec * ATT_W + gi * GRP_W + ch * 128
                    o_ref[:, dst:dst + 128] = x.astype(BF16)

    g_specs = [pl.BlockSpec((d, tm // d, GRP_W), lambda i: (0, i, 0)) for d in DILATIONS for _ in range(3)]
    return pl.pallas_call(
        body, name=name, grid=(S // tm,),
        in_specs=g_specs + [pl.BlockSpec((tm, 2 * GW), lambda i: (i, 0)), pl.BlockSpec((tm, 2 * D), lambda i: (i, 0)),
                            pl.BlockSpec((tm, 1), lambda i: (i, 0)), pl.BlockSpec((1, 128), lambda i: (0, 0))],
        out_specs=pl.BlockSpec((tm, IN_W), lambda i: (i, 0)),
        out_shape=jax.ShapeDtypeStruct((S, IN_W), BF16),
        scratch_shapes=[pltpu.VMEM((tm, 128), F32)],
        compiler_params=_cp(("parallel",)),
    )(*[g for grp in dqkv_c for g in grp], dz, dgl, pos_f, invf)


def _class_order(ts, name):
    tm = ROPE_TM
    n = len(ts)

    def body(*refs):
        buf_ref = refs[3 * n]
        for a in range(n):
            for ch in range(GRP_W // 128):
                cols = slice(ch * 128, (ch + 1) * 128)
                buf_ref[...] = refs[a][:, cols]
                for b, d in enumerate(DILATIONS[1:]):
                    for r in range(d):
                        refs[n + 2 * a + b][r, :, cols] = buf_ref[pl.ds(r, tm // d, stride=d), :]

    return pl.pallas_call(
        body, name=name, grid=(S // tm,),
        in_specs=[pl.BlockSpec((tm, GRP_W), lambda i: (i, 0))] * n,
        out_specs=[pl.BlockSpec((d, tm // d, GRP_W), lambda i: (0, i, 0)) for _ in range(n) for d in DILATIONS[1:]],
        out_shape=[jax.ShapeDtypeStruct((d, S // d, GRP_W), F32) for _ in range(n) for d in DILATIONS[1:]],
        scratch_shapes=[pltpu.VMEM((tm, 128), F32)],
        compiler_params=_cp(("parallel",)),
    )(*ts)


def _own_lanes(h):
    return (lax.broadcasted_iota(jnp.int32, (1, 2 * DH), 1) // DH) == (h % 2)


def _heads(ref):
    out = []
    for h in range(NH):
        pair = ref[:, (h // 2) * 2 * DH:(h // 2 + 1) * 2 * DH]
        out.append(jnp.where(_own_lanes(h), pair, jnp.zeros_like(pair)))
    return jnp.stack(out)


def _unheads(t3):
    return jnp.concatenate([t3[2 * p] + t3[2 * p + 1] for p in range(NH // 2)], axis=1)


def _bdot_nt(a, b):
    return lax.dot_general(a, b, (((2,), (2,)), ((0,), (0,))), preferred_element_type=F32)


def _bdot(a, b):
    return lax.dot_general(a, b, (((2,), (1,)), ((0,), (0,))), preferred_element_type=F32)


def _bdot_tn(a, b):
    return lax.dot_general(a, b, (((1,), (1,)), ((0,), (0,))), preferred_element_type=F32)


def _attn_fwd(gi, qkv_c, name):
    d = DILATIONS[gi]
    nblk = S // d // BLK

    def body(*refs):
        if nblk > 1:
            q_ref, kc_ref, kp_ref, vc_ref, vp_ref, o_ref, lse_ref = refs
            has_prev = pl.program_id(1) != 0
        else:
            q_ref, kc_ref, vc_ref, o_ref, lse_ref = refs
        qi = lax.broadcasted_iota(jnp.int32, (NH, BLK, BLK), 1)
        kj = lax.broadcasted_iota(jnp.int32, (NH, BLK, BLK), 2)
        q = _heads(q_ref)
        sc = jnp.where(kj <= qi, _bdot_nt(q, _heads(kc_ref)) * 0.125, NEG_INF)
        m = jnp.max(sc, axis=-1, keepdims=True)
        if nblk > 1:
            mask_p = jnp.logical_and(kj >= qi, has_prev)
            sp = jnp.where(mask_p, _bdot_nt(q, _heads(kp_ref)) * 0.125, NEG_INF)
            m = jnp.maximum(m, jnp.max(sp, axis=-1, keepdims=True))
        pc = jnp.exp(sc - m)
        l = jnp.sum(pc, axis=-1, keepdims=True)
        o = _bdot(pc.astype(BF16), _heads(vc_ref))
        if nblk > 1:
            pp = jnp.exp(sp - m)
            l = l + jnp.sum(pp, axis=-1, keepdims=True)
            o = o + _bdot(pp.astype(BF16), _heads(vp_ref))
        o_ref[...] = _unheads(o / l)
        lse = jnp.broadcast_to(m + jnp.log(l), (NH, BLK, 2 * DH))
        lse_ref[...] = _unheads(jnp.stack([jnp.where(_own_lanes(h), lse[h], 0.0) for h in range(NH)]))

    def cur(sec):
        return pl.BlockSpec((None, BLK, GRP_W), lambda r, n: (r, n, sec))

    def prev(sec):
        return pl.BlockSpec((None, BLK, GRP_W), lambda r, n: (r, jnp.maximum(n - 1, 0), sec))

    out = pl.BlockSpec((None, BLK, GRP_W), lambda r, n: (r, n, 0))
    shp = jax.ShapeDtypeStruct((d, S // d, GRP_W), F32)
    if nblk > 1:
        in_specs, args = [cur(0), cur(1), prev(1), cur(2), prev(2)], (qkv_c,) * 5
    else:
        in_specs, args = [cur(0), cur(1), cur(2)], (qkv_c,) * 3
    return pl.pallas_call(
        body, name=name, grid=(d, nblk), in_specs=in_specs, out_specs=[out, out], out_shape=[shp, shp],
        compiler_params=_cp(("parallel", "parallel")),
    )(*args)


def _attn_combine(os, lses, name):
    tm = ROPE_TM

    def body(o0_ref, o1_ref, o2_ref, l0_ref, l1_ref, l2_ref, y_ref, yt_ref, l_ref, buf_ref):
        def token_order(ref, d, cols, slot):
            if d == 1:
                return ref[0, :, cols]
            for r in range(d):
                buf_ref[slot, pl.ds(r, tm // d, stride=d), :] = ref[r, :, cols]
            return buf_ref[slot]

        for ch in range(GRP_W // 128):
            cols = slice(ch * 128, (ch + 1) * 128)
            o = [token_order(ref, d, cols, k) for k, (ref, d) in enumerate(zip((o0_ref, o1_ref, o2_ref), DILATIONS))]
            ls = [token_order(ref, d, cols, 3 + k)
                  for k, (ref, d) in enumerate(zip((l0_ref, l1_ref, l2_ref), DILATIONS))]
            m = jnp.maximum(jnp.maximum(ls[0], ls[1]), ls[2])
            e = [jnp.exp(l - m) for l in ls]
            den = e[0] + e[1] + e[2]
            y = (e[0] * o[0] + e[1] * o[1] + e[2] * o[2]) / den
            y_ref[:, cols] = y
            yt_ref[cols, :] = y.T.astype(BF16)
            l_ref[:, cols] = m + jnp.log(den)

    blk = pl.BlockSpec((tm, GRP_W), lambda i: (i, 0))
    cls = [pl.BlockSpec((d, tm // d, GRP_W), lambda i: (0, i, 0)) for d in DILATIONS]
    shp = jax.ShapeDtypeStruct((S, GRP_W), F32)
    return pl.pallas_call(
        body, name=name, grid=(S // tm,), in_specs=cls + cls,
        out_specs=[blk, pl.BlockSpec((GRP_W, tm), lambda i: (0, i)), blk],
        out_shape=[shp, jax.ShapeDtypeStruct((GRP_W, S), BF16), shp],
        scratch_shapes=[pltpu.VMEM((6, tm, 128), F32)],
        compiler_params=_cp(("parallel",)),
    )(*os, *lses)


def _attn_bwd(gi, qkv_c, dy_c, y_c, lse_c, name):
    d = DILATIONS[gi]
    nblk = S // d // BLK

    def body(*refs):
        if nblk > 1:
            (q_ref, qn_ref, k_ref, kp_ref, v_ref, vp_ref, dy_ref, dyn_ref, y_ref, yn_ref, l_ref, ln_ref,
             dq_ref, dk_ref, dv_ref) = refs
            n = pl.program_id(1)
            has_prev = n != 0
            has_next = n != nblk - 1
        else:
            q_ref, k_ref, v_ref, dy_ref, y_ref, l_ref, dq_ref, dk_ref, dv_ref = refs
        qi = lax.broadcasted_iota(jnp.int32, (NH, BLK, BLK), 1)
        kj = lax.broadcasted_iota(jnp.int32, (NH, BLK, BLK), 2)

        def lse_col(ref):
            return jnp.stack([ref[:, h * DH:h * DH + 1] for h in range(NH)])

        q, k, v = _heads(q_ref), _heads(k_ref), _heads(v_ref)
        dy = _heads(dy_ref)
        dd = jnp.sum(dy * _heads(y_ref), axis=-1, keepdims=True)
        lcol = lse_col(l_ref)
        dyb = dy.astype(BF16)
        p = jnp.exp(jnp.where(kj <= qi, _bdot_nt(q, k) * 0.125, NEG_INF) - lcol)
        ds = (p * (_bdot_nt(dyb, v) - dd)).astype(BF16)
        dq = _bdot(ds, k)
        dk = _bdot_tn(ds, q)
        dv = _bdot_tn(p.astype(BF16), dyb)
        if nblk > 1:
            qn, kpv, vpv = _heads(qn_ref), _heads(kp_ref), _heads(vp_ref)
            dyn = _heads(dyn_ref)
            ddn = jnp.sum(dyn * _heads(yn_ref), axis=-1, keepdims=True)
            lncol = lse_col(ln_ref)
            dynb = dyn.astype(BF16)
            mask_p = jnp.logical_and(kj >= qi, has_prev)
            pp = jnp.exp(jnp.where(mask_p, _bdot_nt(q, kpv) * 0.125, NEG_INF) - lcol)
            dsp = (pp * (_bdot_nt(dyb, vpv) - dd)).astype(BF16)
            dq = dq + _bdot(dsp, kpv)
            mask_n = jnp.logical_and(kj >= qi, has_next)
            pn = jnp.exp(jnp.where(mask_n, _bdot_nt(qn, k) * 0.125, NEG_INF) - lncol)
            dsn = (pn * (_bdot_nt(dynb, v) - ddn)).astype(BF16)
            dk = dk + _bdot_tn(dsn, qn)
            dv = dv + _bdot_tn(pn.astype(BF16), dynb)
        dq_ref[...] = _unheads(dq) * 0.125
        dk_ref[...] = _unheads(dk) * 0.125
        dv_ref[...] = _unheads(dv)

    def spec(sec, shift):
        def idx(r, n):
            return (r, jnp.clip(n + shift, 0, nblk - 1), sec)
        return pl.BlockSpec((None, BLK, GRP_W), idx)

    if nblk > 1:
        in_specs = [spec(0, 0), spec(0, 1), spec(1, 0), spec(1, -1), spec(2, 0), spec(2, -1),
                    spec(0, 0), spec(0, 1), spec(0, 0), spec(0, 1), spec(0, 0), spec(0, 1)]
        args = (qkv_c,) * 6 + (dy_c, dy_c, y_c, y_c, lse_c, lse_c)
    else:
        in_specs = [spec(0, 0), spec(1, 0), spec(2, 0), spec(0, 0), spec(0, 0), spec(0, 0)]
        args = (qkv_c, qkv_c, qkv_c, dy_c, y_c, lse_c)
    out = spec(0, 0)
    shp = jax.ShapeDtypeStruct((d, S // d, GRP_W), F32)
    return pl.pallas_call(
        body, name=name, grid=(d, nblk), in_specs=in_specs, out_specs=[out, out, out], out_shape=[shp, shp, shp],
        compiler_params=_cp(("parallel", "parallel")),
    )(*args)


_SQRT_HALF = 0.7071067811865476
_INV_SQRT_2PI = 0.3989422804014327


def _gelu(z):
    return 0.5 * z * (1.0 + lax.erf(z * _SQRT_HALF))


def _gelu_grad(z):
    return 0.5 * (1.0 + lax.erf(z * _SQRT_HALF)) + z * (jnp.exp(-0.5 * z * z) * _INV_SQRT_2PI)


def _tril_mask():
    t = lax.broadcasted_iota(jnp.int32, (BLK, BLK), 0)
    s = lax.broadcasted_iota(jnp.int32, (BLK, BLK), 1)
    return s <= t


def _groups(t):
    return jnp.stack([t[:, g * BLK:(g + 1) * BLK] for g in range(8)])


def _ungroup(t3):
    return jnp.concatenate([t3[g] for g in range(8)], axis=1)


def _group_bias(bs_ref):
    return jnp.stack([bs_ref[:, g:g + 1] for g in range(8)])


def _gmlp_fwd(z, ln_g, ln_b, w_s, b_s_t, name):
    def body(z_ref, g_ref, b_ref, ws_ref, bs_ref, y_ref, yt_ref):
        zg = _gelu(z_ref[...])
        u = zg[:, :GW]
        xh, _ = _ln_stats(zg[:, GW:])
        vn = (xh * g_ref[...] + b_ref[...]).astype(BF16)
        wt = jnp.where(_tril_mask(), ws_ref[...], 0.0).astype(BF16)
        yv = u * _ungroup(_bdot(wt, _groups(vn)) + _group_bias(bs_ref))
        y_ref[...] = yv.astype(BF16)
        yt_ref[...] = yv.T.astype(BF16)

    vec = pl.BlockSpec((1, GW), lambda n: (0, 0))
    return pl.pallas_call(
        body, name=name, grid=(NBLK,),
        in_specs=[pl.BlockSpec((BLK, 2 * GW), lambda n: (n, 0)), vec, vec,
                  pl.BlockSpec((8, BLK, BLK), lambda n: (0, 0, 0)), pl.BlockSpec((BLK, 8), lambda n: (0, 0))],
        out_specs=[pl.BlockSpec((BLK, GW), lambda n: (n, 0)), pl.BlockSpec((GW, BLK), lambda n: (0, n))],
        out_shape=[jax.ShapeDtypeStruct((S, GW), BF16), jax.ShapeDtypeStruct((GW, S), BF16)],
        compiler_params=_cp(("parallel",)),
    )(z, ln_g, ln_b, w_s, b_s_t)


def _gmlp_bwd(z, dy, ln_g, ln_b, w_s, b_s_t, name):
    def body(z_ref, dy_ref, g_ref, b_ref, ws_ref, bs_ref, dz_ref, dws_ref, dbs_ref, dg_ref, db_ref, dvn_ref):
        n = pl.program_id(0)
        zv = z_ref[...]
        zg = _gelu(zv)
        u = zg[:, :GW]
        xh, rstd = _ln_stats(zg[:, GW:])
        vn = (xh * g_ref[...] + b_ref[...]).astype(BF16)
        tril = _tril_mask()

        @pl.when(n == 0)
        def _():
            dws_ref[...] = jnp.zeros_like(dws_ref)
            dbs_ref[...] = jnp.zeros_like(dbs_ref)
            dg_ref[...] = jnp.zeros_like(dg_ref)
            db_ref[...] = jnp.zeros_like(db_ref)

        wt = jnp.where(tril, ws_ref[...], 0.0).astype(BF16)
        vn3 = _groups(vn)
        dyv = dy_ref[...]
        mixed = _ungroup(_bdot(wt, vn3) + _group_bias(bs_ref))
        dz_ref[:, :GW] = (dyv * mixed * _gelu_grad(zv[:, :GW])).astype(BF16)
        dmix3 = _groups(dyv * u)
        dmb = dmix3.astype(BF16)
        dws_ref[...] += jnp.where(tril, _bdot_nt(dmb, vn3), 0.0)
        dbsum = jnp.sum(dmix3, axis=-1, keepdims=True)
        for gg in range(8):
            dbs_ref[:, gg:gg + 1] += dbsum[gg]
        dvn_ref[...] = _ungroup(_bdot_tn(wt, dmb))

        dvn = dvn_ref[...]
        dg_ref[...] += jnp.sum(dvn * xh, axis=0, keepdims=True)
        db_ref[...] += jnp.sum(dvn, axis=0, keepdims=True)
        dvg = _ln_dx(dvn * g_ref[...], xh, rstd)
        dz_ref[:, GW:] = (dvg * _gelu_grad(zv[:, GW:])).astype(BF16)

    vec = pl.BlockSpec((1, GW), lambda n: (0, 0))
    ws = pl.BlockSpec((8, BLK, BLK), lambda n: (0, 0, 0))
    bs = pl.BlockSpec((BLK, 8), lambda n: (0, 0))
    return pl.pallas_call(
        body, name=name, grid=(NBLK,),
        in_specs=[pl.BlockSpec((BLK, 2 * GW), lambda n: (n, 0)), pl.BlockSpec((BLK, GW), lambda n: (n, 0)),
                  vec, vec, ws, bs],
        out_specs=[pl.BlockSpec((BLK, 2 * GW), lambda n: (n, 0)), ws, bs, vec, vec],
        out_shape=[jax.ShapeDtypeStruct((S, 2 * GW), BF16), jax.ShapeDtypeStruct((8, BLK, BLK), F32),
                   jax.ShapeDtypeStruct((BLK, 8), F32), jax.ShapeDtypeStruct((1, GW), F32),
                   jax.ShapeDtypeStruct((1, GW), F32)],
        scratch_shapes=[pltpu.VMEM((BLK, GW), F32)],
        compiler_params=_cp(("arbitrary",)),
    )(z, dy, ln_g, ln_b, w_s, b_s_t)


def _merge_fwd(a, b, gl, b_gates, name):
    tm = 512

    def body(a_ref, b_ref, g0_ref, g1_ref, bg_ref, o_ref, ot_ref):
        g0 = jax.nn.sigmoid(g0_ref[...] + bg_ref[:, :D])
        g1 = jax.nn.sigmoid(g1_ref[...] + bg_ref[:, D:])
        mg = g0 * a_ref[...] + g1 * b_ref[...]
        o_ref[...] = mg.astype(BF16)
        ot_ref[...] = mg.T.astype(BF16)

    row = pl.BlockSpec((tm, D), lambda i: (i, 0))
    return pl.pallas_call(
        body, name=name, grid=(S // tm,),
        in_specs=[row, row, row, pl.BlockSpec((tm, D), lambda i: (i, 1)), pl.BlockSpec((1, 2 * D), lambda i: (0, 0))],
        out_specs=[row, pl.BlockSpec((D, tm), lambda i: (0, i))],
        out_shape=[jax.ShapeDtypeStruct((S, D), BF16), jax.ShapeDtypeStruct((D, S), BF16)],
        compiler_params=_cp(("parallel",)),
    )(a, b, gl, gl, b_gates)


def _merge_bwd(dm, a, b, gl, b_gates, name):
    tm = 512

    def body(dm_ref, a_ref, b_ref, g0_ref, g1_ref, bg_ref, da_ref, db_ref, dgl_ref, dbg_ref):
        i = pl.program_id(0)
        dmv = dm_ref[...]
        g0 = jax.nn.sigmoid(g0_ref[...] + bg_ref[:, :D])
        g1 = jax.nn.sigmoid(g1_ref[...] + bg_ref[:, D:])
        da_ref[...] = (dmv * g0).astype(BF16)
        db_ref[...] = (dmv * g1).astype(BF16)
        d0 = dmv * a_ref[...] * g0 * (1.0 - g0)
        d1 = dmv * b_ref[...] * g1 * (1.0 - g1)
        dgl_ref[:, :D] = d0.astype(BF16)
        dgl_ref[:, D:] = d1.astype(BF16)
        s0 = jnp.sum(d0, axis=0, keepdims=True)
        s1 = jnp.sum(d1, axis=0, keepdims=True)

        @pl.when(i == 0)
        def _():
            dbg_ref[:, :D] = s0
            dbg_ref[:, D:] = s1

        @pl.when(i > 0)
        def _():
            dbg_ref[:, :D] += s0
            dbg_ref[:, D:] += s1

    row = pl.BlockSpec((tm, D), lambda i: (i, 0))
    wide = pl.BlockSpec((tm, 2 * D), lambda i: (i, 0))
    bg = pl.BlockSpec((1, 2 * D), lambda i: (0, 0))
    return pl.pallas_call(
        body, name=name, grid=(S // tm,),
        in_specs=[row, row, row, row, pl.BlockSpec((tm, D), lambda i: (i, 1)), bg],
        out_specs=[row, row, wide, bg],
        out_shape=[jax.ShapeDtypeStruct((S, D), BF16), jax.ShapeDtypeStruct((S, D), BF16),
                   jax.ShapeDtypeStruct((S, 2 * D), BF16), jax.ShapeDtypeStruct((1, 2 * D), F32)],
        compiler_params=_cp(("arbitrary",)),
    )(dm, a, b, gl, gl, b_gates)


def _adam_math(w, g, m, v):
    m2 = ADAM_B1 * m + (1.0 - ADAM_B1) * g
    v2 = ADAM_B2 * v + (1.0 - ADAM_B2) * (g * g)
    m_hat = m2 / (1.0 - ADAM_B1 ** ADAM_STEP)
    v_hat = v2 / (1.0 - ADAM_B2 ** ADAM_STEP)
    delta = -ADAM_LR * (m_hat / (jnp.sqrt(v_hat) + ADAM_EPS) + ADAM_WD * w)
    return delta, m2, v2


def _pick_rows(rows, cols, unit=16, budget=2 * MIB):
    best = unit
    for t in range(unit, rows + 1, unit):
        if rows % t == 0 and t * cols * 4 <= budget:
            best = t
    assert rows % best == 0
    return best


def _adamw(w, g, m, v, name):
    r, c = w.shape
    tr = _pick_rows(r, c, unit=8)

    def body(w_ref, g_ref, m_ref, v_ref, go_ref, d_ref, mo_ref, vo_ref):
        gv = g_ref[...]
        delta, m2, v2 = _adam_math(w_ref[...], gv, m_ref[...], v_ref[...])
        go_ref[...] = gv
        d_ref[...] = delta
        mo_ref[...] = m2
        vo_ref[...] = v2

    blk = pl.BlockSpec((tr, c), lambda i: (i, 0))
    shp = jax.ShapeDtypeStruct((r, c), F32)
    return pl.pallas_call(
        body, name=name, grid=(r // tr,), in_specs=[blk] * 4, out_specs=[blk] * 4, out_shape=[shp] * 4,
        compiler_params=_cp(("parallel",)),
    )(*[pltpu.with_memory_space_constraint(t, pltpu.HBM) for t in (w, g, m, v)])


def _small_sum_adamw(parts, own, pos, w, m, v, name):
    tr = 48

    def body(pos_ref, p_ref, own_ref, w_ref, m_ref, v_ref, g_ref, d_ref, mo_ref, vo_ref):
        me = 2 * pos_ref[1] + pos_ref[0]
        gv = None
        for k in range(8):
            term = jnp.where(me == k, own_ref[...], p_ref[k])
            gv = term if gv is None else gv + term
        delta, m2, v2 = _adam_math(w_ref[...], gv, m_ref[...], v_ref[...])
        g_ref[...] = gv
        d_ref[...] = delta
        mo_ref[...] = m2
        vo_ref[...] = v2

    blk = pl.BlockSpec((tr, D), lambda i, p: (i, 0))
    shp = jax.ShapeDtypeStruct((SMALL_ROWS, D), F32)
    return pl.pallas_call(
        body, name=name,
        grid_spec=pltpu.PrefetchScalarGridSpec(
            num_scalar_prefetch=1, grid=(SMALL_ROWS // tr,),
            in_specs=[pl.BlockSpec((8, tr, D), lambda i, p: (0, i, 0)), blk, blk, blk, blk],
            out_specs=[blk] * 4),
        out_shape=[shp] * 4,
        compiler_params=_cp(("parallel",)),
    )(pos, parts, own, w, m, v)


ANY = pl.BlockSpec(memory_space=pl.ANY)


def _in_hbm(arrays):
    return [pltpu.with_memory_space_constraint(a, pltpu.HBM) for a in arrays]


def _mesh_pos():
    x, y, c = lax.axis_index("x"), lax.axis_index("y"), lax.axis_index("c")
    chips = [(1 - x, y), (x, 1 - y), (1 - x, 1 - y)]
    return x, y, c, chips


def _place_shard(w, kind, pos, name):
    r, c = w.shape
    tr = _pick_rows(r, c)

    def body(pos_ref, w_ref, o_ref):
        o_ref[...] = w_ref[...].astype(BF16)

    if kind == "stack":
        o_spec = pl.BlockSpec((None, tr, c), lambda i, p: (p[1], i, 0))
        shape = (NSH, r, c)
    else:
        o_spec = pl.BlockSpec((tr, c), lambda i, p: (i, p[1]))
        shape = (r, NSH * c)
    return pl.pallas_call(
        body, name=name,
        grid_spec=pltpu.PrefetchScalarGridSpec(
            num_scalar_prefetch=1, grid=(r // tr,),
            in_specs=[pl.BlockSpec((tr, c), lambda i, p: (i, 0))], out_specs=o_spec),
        out_shape=pltpu.HBM(shape, BF16),
        compiler_params=_cp(("parallel",)),
    )(pos, pltpu.with_memory_space_constraint(w, pltpu.HBM))


SEM = pl.BlockSpec(memory_space=pltpu.SEMAPHORE)
SPLIT_COPY = pltpu.CompilerParams(has_side_effects=pltpu.SideEffectType.DATAFLOW_SIDE_EFFECTING)


def _shard_window(ref, kind, j, h, dims):
    r, c = dims
    rows = pl.ds(pl.multiple_of(h * (r // 2), 16), r // 2)
    if kind == "stack":
        return ref.at[j, rows, :]
    return ref.at[rows, pl.ds(pl.multiple_of(j * c, 128), c)]


def _ici_copy(ref, kind, dims, j, c, sems, idx, to):
    win = _shard_window(ref, kind, j, c, dims)
    return pltpu.make_async_remote_copy(src_ref=win, dst_ref=win, send_sem=sems[0].at[idx], recv_sem=sems[1].at[idx],
                                        device_id=to, device_id_type=MESH_T)


def _both_copy(ref, kind, dims, a, k, chip, half, tc, sc, sems):
    win = _shard_window(ref, kind, half[0], half[1], dims)
    return pltpu.make_async_remote_copy(src_ref=win, dst_ref=win, send_sem=sems[0].at[6 * a + 2 * k + tc],
                                        recv_sem=sems[1].at[6 * a + 2 * k + sc],
                                        device_id=(chip[0], chip[1], tc), device_id_type=MESH_T)


def _gather_start(fulls, kinds, dims, after, name, both=False):
    n, na = len(fulls), len(after)
    per = 6 if both else 3

    def body(*refs):
        outs = refs[n + na:2 * n + na]
        send_sems, recv_sems, token = refs[2 * n + na:]
        x, y, c, chips = _mesh_pos()
        for a in range(n):
            for k, chip in enumerate(chips):
                if both:
                    for tc in range(2):
                        _both_copy(outs[a], kinds[a], dims[a], a, k, chip, (2 * x + y, c), tc, c,
                                   (send_sems, recv_sems)).start()
                else:
                    _ici_copy(outs[a], kinds[a], dims[a], 2 * x + y, c, (send_sems, recv_sems), 3 * a + k,
                              (chip[0], chip[1], c)).start()
        token[...] = jnp.zeros_like(token)

    res = pl.pallas_call(
        body, name=name, in_specs=[ANY] * (n + na),
        out_specs=[ANY] * n + [SEM, SEM, pl.BlockSpec(memory_space=pltpu.VMEM)],
        out_shape=[pltpu.HBM(f.shape, BF16) for f in fulls]
        + [pltpu.SemaphoreType.DMA((per * n,)), pltpu.SemaphoreType.DMA((per * n,)),
           jax.ShapeDtypeStruct((8, 128), F32)],
        input_output_aliases={i: i for i in range(n)},
        compiler_params=SPLIT_COPY,
    )(*_in_hbm(fulls), *after)
    return res[:n], res[n], res[n + 1], res[n + 2]


def _gather_wait(fulls, send_sems, recv_sems, kinds, dims, after, name, both=False):
    n, na = len(fulls), len(after)

    def body(*refs):
        ssem, rsem = refs[n], refs[n + 1]
        outs = refs[n + 2 + na:]
        x, y, c, chips = _mesh_pos()
        for a in range(n):
            for k, chip in enumerate(chips):
                if both:
                    for oc in range(2):
                        _both_copy(outs[a], kinds[a], dims[a], a, k, chip, (2 * x + y, c), oc, c,
                                   (ssem, rsem)).wait_send()
                        _both_copy(outs[a], kinds[a], dims[a], a, k, chip, (2 * chip[0] + chip[1], oc), c, oc,
                                   (ssem, rsem)).wait_recv()
                    continue
                to = (chip[0], chip[1], c)
                _ici_copy(outs[a], kinds[a], dims[a], 2 * x + y, c, (ssem, rsem), 3 * a + k, to).wait_send()
                _ici_copy(outs[a], kinds[a], dims[a], 2 * chip[0] + chip[1], c, (ssem, rsem), 3 * a + k, to).wait_recv()

    return pl.pallas_call(
        body, name=name, in_specs=[ANY] * n + [SEM, SEM] + [ANY] * na, out_specs=[ANY] * n,
        out_shape=[pltpu.HBM(f.shape, BF16) for f in fulls],
        input_output_aliases={i: i for i in range(n)},
        compiler_params=SPLIT_COPY,
    )(*_in_hbm(fulls), send_sems, recv_sems, *after)


def _gather_forward(fulls, kinds, dims, name):
    n = len(fulls)

    def body(*refs):
        outs = refs[n:2 * n]
        sems = refs[2 * n:]
        x, y, c, chips = _mesh_pos()
        sib = (x, y, 1 - c)
        cps = []
        for a in range(n):
            for k, chip in enumerate(chips):
                cp = _ici_copy(outs[a], kinds[a], dims[a], 2 * chip[0] + chip[1], c, sems, 3 * a + k, sib)
                cp.start()
                cps.append(cp)
        for a in range(n):
            for k, chip in enumerate(chips):
                _ici_copy(outs[a], kinds[a], dims[a], 2 * chip[0] + chip[1], 1 - c, sems, 3 * a + k, sib).wait_recv()
        for cp in cps:
            cp.wait_send()

    return pl.pallas_call(
        body, name=name, in_specs=[ANY] * n, out_specs=[ANY] * n,
        out_shape=[pltpu.HBM(f.shape, BF16) for f in fulls],
        input_output_aliases={i: i for i in range(n)},
        scratch_shapes=[pltpu.SemaphoreType.DMA((3 * n,)), pltpu.SemaphoreType.DMA((3 * n,))],
    )(*_in_hbm(fulls))


def _pair_copy(src, land, a, x, y, c, sems):
    return pltpu.make_async_remote_copy(
        src_ref=src.at[1 - c], dst_ref=land, send_sem=sems[0].at[a], recv_sem=sems[1].at[a],
        device_id=(x, y, 1 - c), device_id_type=MESH_T)


def _pair_start(grads, lands, name):
    n = len(grads)

    def body(*refs):
        srcs, dsts = refs[2 * n:3 * n], refs[3 * n:4 * n]
        send_sems, recv_sems, token = refs[4 * n:]
        x, y, c, _ = _mesh_pos()
        for a in range(n):
            _pair_copy(srcs[a], dsts[a], a, x, y, c, (send_sems, recv_sems)).start()
        token[...] = jnp.zeros_like(token)

    res = pl.pallas_call(
        body, name=name, in_specs=[ANY] * (2 * n),
        out_specs=[ANY] * (2 * n) + [SEM, SEM, pl.BlockSpec(memory_space=pltpu.VMEM)],
        out_shape=[pltpu.HBM(g.shape, F32) for g in grads]
        + [pltpu.HBM(l.shape, F32) for l in lands]
        + [pltpu.SemaphoreType.DMA((n,)), pltpu.SemaphoreType.DMA((n,)), jax.ShapeDtypeStruct((8, 128), F32)],
        input_output_aliases={i: i for i in range(2 * n)},
        compiler_params=SPLIT_COPY,
    )(*_in_hbm(grads), *_in_hbm(lands))
    return res[:n], res[n:2 * n], res[2 * n], res[2 * n + 1], res[2 * n + 2]


def _pair_wait(grads, lands, send_sems, recv_sems, after, name):
    n, na = len(grads), len(after)

    def body(*refs):
        ssem, rsem = refs[2 * n], refs[2 * n + 1]
        outs = refs[2 * n + 2 + na:]
        x, y, c, _ = _mesh_pos()
        for a in range(n):
            cp = _pair_copy(outs[a], outs[n + a], a, x, y, c, (ssem, rsem))
            cp.wait_send()
            cp.wait_recv()

    res = pl.pallas_call(
        body, name=name, in_specs=[ANY] * (2 * n) + [SEM, SEM] + [ANY] * na, out_specs=[ANY] * (2 * n),
        out_shape=[pltpu.HBM(g.shape, F32) for g in grads]
        + [pltpu.HBM(l.shape, F32) for l in lands],
        input_output_aliases={i: i for i in range(2 * n)},
        compiler_params=SPLIT_COPY,
    )(*_in_hbm(grads), *_in_hbm(lands), send_sems, recv_sems, *after)
    return res[:n], res[n:]


def _pair_sum(g, recv, pos, name):
    _, _, rh, c = g.shape
    tr = _pick_rows(rh, c)

    def body(pos_ref, g_ref, r_ref, o_ref):
        o_ref[...] = (g_ref[...] + r_ref[...]).astype(BF16)

    return pl.pallas_call(
        body, name=name,
        grid_spec=pltpu.PrefetchScalarGridSpec(
            num_scalar_prefetch=1, grid=(3, rh // tr),
            in_specs=[pl.BlockSpec((None, None, tr, c), lambda k, r, p: (p[0], p[2 + k], r, 0)),
                      pl.BlockSpec((None, tr, c), lambda k, r, p: (p[2 + k], r, 0))],
            out_specs=pl.BlockSpec((None, tr, c), lambda k, r, p: (k, r, 0))),
        out_shape=pltpu.HBM((3, rh, c), BF16),
        compiler_params=_cp(("parallel", "parallel")),
    )(pos, *_in_hbm([g, recv]))


def _pair_sum_group(gs, recvs, pos, name):
    n = len(gs)
    _, _, rh, c = gs[0].shape

    def body(pos_ref, *refs):
        a = pl.program_id(0)
        for t in range(n):
            @pl.when(a == t)
            def _(t=t):
                refs[2 * n + t][...] = (refs[t][...] + refs[n + t][...]).astype(BF16)

    def slot(t, a, k):
        return jnp.where(a == t, k, jnp.where(a < t, 0, 2))

    g_specs = [pl.BlockSpec((None, None, rh, c), lambda a, k, p, t=t: (p[0], p[2 + slot(t, a, k)], 0, 0))
               for t in range(n)]
    r_specs = [pl.BlockSpec((None, rh, c), lambda a, k, p, t=t: (p[2 + slot(t, a, k)], 0, 0)) for t in range(n)]
    o_specs = [pl.BlockSpec((None, rh, c), lambda a, k, p, t=t: (slot(t, a, k), 0, 0)) for t in range(n)]
    return pl.pallas_call(
        body, name=name,
        grid_spec=pltpu.PrefetchScalarGridSpec(num_scalar_prefetch=1, grid=(n, 3), in_specs=g_specs + r_specs,
                                               out_specs=o_specs),
        out_shape=[pltpu.HBM((3, rh, c), BF16)] * n,
        compiler_params=_cp(("arbitrary", "arbitrary")),
    )(pos, *_in_hbm(list(gs) + list(recvs)))


def _chip_copy(src, land, a, k, chip, c, sems):
    return pltpu.make_async_remote_copy(
        src_ref=src.at[k], dst_ref=land.at[k], send_sem=sems[0].at[3 * a + k],
        recv_sem=sems[1].at[3 * a + k], device_id=(chip[0], chip[1], c), device_id_type=MESH_T)


def _chip_start(psums, lands, name):
    n = len(psums)

    def body(*refs):
        srcs, dsts = refs[2 * n:3 * n], refs[3 * n:4 * n]
        send_sems, recv_sems, token = refs[4 * n:]
        x, y, c, chips = _mesh_pos()
        for a in range(n):
            for k, chip in enumerate(chips):
                _chip_copy(srcs[a], dsts[a], a, k, chip, c, (send_sems, recv_sems)).start()
        token[...] = jnp.zeros_like(token)

    res = pl.pallas_call(
        body, name=name, in_specs=[ANY] * (2 * n),
        out_specs=[ANY] * (2 * n) + [SEM, SEM, pl.BlockSpec(memory_space=pltpu.VMEM)],
        out_shape=[pltpu.HBM(p.shape, BF16) for p in psums]
        + [pltpu.HBM(l.shape, BF16) for l in lands]
        + [pltpu.SemaphoreType.DMA((3 * n,)), pltpu.SemaphoreType.DMA((3 * n,)), jax.ShapeDtypeStruct((8, 128), F32)],
        input_output_aliases={i: i for i in range(2 * n)},
        compiler_params=SPLIT_COPY,
    )(*_in_hbm(psums), *_in_hbm(lands))
    return res[:n], res[n:2 * n], res[2 * n], res[2 * n + 1], res[2 * n + 2]


def _chip_wait(psums, lands, send_sems, recv_sems, after, name):
    n, na = len(psums), len(after)

    def body(*refs):
        ssem, rsem = refs[2 * n], refs[2 * n + 1]
        outs = refs[2 * n + 2 + na:]
        srcs, dsts = outs[:n], outs[n:]
        x, y, c, chips = _mesh_pos()
        for a in range(n):
            for k, chip in enumerate(chips):
                cp = _chip_copy(srcs[a], dsts[a], a, k, chip, c, (ssem, rsem))
                cp.wait_send()
                cp.wait_recv()

    res = pl.pallas_call(
        body, name=name, in_specs=[ANY] * (2 * n) + [SEM, SEM] + [ANY] * na, out_specs=[ANY] * (2 * n),
        out_shape=[pltpu.HBM(p.shape, BF16) for p in psums]
        + [pltpu.HBM(l.shape, BF16) for l in lands],
        input_output_aliases={i: i for i in range(2 * n)},
        compiler_params=SPLIT_COPY,
    )(*_in_hbm(psums), *_in_hbm(lands), send_sems, recv_sems, *after)
    return res[n:]


def _owner_sum(g, recv_a, recv_b, pos, name):
    _, _, rh, c = g.shape
    tr = _pick_rows(rh, c)

    def body(pos_ref, g_ref, ra_ref, rb_ref, o_ref):
        acc = g_ref[...] + ra_ref[...]
        for k in range(3):
            acc = acc + rb_ref[k].astype(F32)
        o_ref[...] = acc

    return pl.pallas_call(
        body, name=name,
        grid_spec=pltpu.PrefetchScalarGridSpec(
            num_scalar_prefetch=1, grid=(rh // tr,),
            in_specs=[pl.BlockSpec((None, None, tr, c), lambda r, p: (p[0], p[1], r, 0)),
                      pl.BlockSpec((None, tr, c), lambda r, p: (p[1], r, 0)),
                      pl.BlockSpec((3, tr, c), lambda r, p: (0, r, 0))],
            out_specs=pl.BlockSpec((None, tr, c), lambda r, p: (p[0], r, 0))),
        out_shape=pltpu.HBM((2, rh, c), F32),
        compiler_params=_cp(("parallel",)),
    )(pos, *_in_hbm([g, recv_a, recv_b]))


def _sibling_allgather(halves, name):
    n = len(halves)

    def body(*refs):
        outs = refs[n:2 * n]
        send_sems, recv_sems = refs[2 * n:]
        x, y, c, _ = _mesh_pos()
        cps = []
        for a in range(n):
            cp = pltpu.make_async_remote_copy(
                src_ref=outs[a].at[c], dst_ref=outs[a].at[c], send_sem=send_sems.at[a], recv_sem=recv_sems.at[a],
                device_id=(x, y, 1 - c), device_id_type=MESH_T)
            cp.start()
            cps.append(cp)
        for a in range(n):
            cps[a].wait_send()
            pltpu.make_async_remote_copy(
                src_ref=outs[a].at[1 - c], dst_ref=outs[a].at[1 - c], send_sem=send_sems.at[a],
                recv_sem=recv_sems.at[a], device_id=(x, y, 1 - c), device_id_type=MESH_T).wait_recv()

    return pl.pallas_call(
        body, name=name, in_specs=[ANY] * n, out_specs=[ANY] * n,
        out_shape=[pltpu.HBM(h.shape, F32) for h in halves],
        input_output_aliases={i: i for i in range(n)},
        scratch_shapes=[pltpu.SemaphoreType.DMA((n,)), pltpu.SemaphoreType.DMA((n,))],
    )(*_in_hbm(halves))


def _peers(x, y, c):
    rel = [(0, 0, 1), (0, 1, 0), (0, 1, 1), (1, 0, 0), (1, 0, 1), (1, 1, 0), (1, 1, 1)]
    return [((1 - x) if dx else x, (1 - y) if dy else y, (1 - c) if dc else c) for dx, dy, dc in rel]


def _small_copy(src, land, k, peer, slot, sems):
    return pltpu.make_async_remote_copy(src_ref=src, dst_ref=land.at[slot], send_sem=sems[0].at[k],
                                        recv_sem=sems[1].at[k], device_id=peer, device_id_type=MESH_T)


def _small_start(part, land, name):
    def body(p_in, l_in, p_ref, l_ref, send_sems, recv_sems, token):
        x, y, c, _ = _mesh_pos()
        for k, peer in enumerate(_peers(x, y, c)):
            _small_copy(p_ref, l_ref, k, peer, 4 * x + 2 * y + c, (send_sems, recv_sems)).start()
        token[...] = jnp.zeros_like(token)

    return pl.pallas_call(
        body, name=name, in_specs=[ANY, ANY],
        out_specs=[ANY, ANY, SEM, SEM, pl.BlockSpec(memory_space=pltpu.VMEM)],
        out_shape=[pltpu.HBM(part.shape, F32), pltpu.HBM(land.shape, F32), pltpu.SemaphoreType.DMA((7,)),
                   pltpu.SemaphoreType.DMA((7,)), jax.ShapeDtypeStruct((8, 128), F32)],
        input_output_aliases={0: 0, 1: 1},
        compiler_params=SPLIT_COPY,
    )(*_in_hbm([part, land]))


def _small_wait(part, land, send_sems, recv_sems, after, name):
    na = len(after)

    def body(*refs):
        ssem, rsem = refs[2], refs[3]
        p_ref, l_ref = refs[4 + na:]
        x, y, c, _ = _mesh_pos()
        for k, peer in enumerate(_peers(x, y, c)):
            cp = _small_copy(p_ref, l_ref, k, peer, 4 * peer[0] + 2 * peer[1] + peer[2], (ssem, rsem))
            cp.wait_send()
            cp.wait_recv()

    return pl.pallas_call(
        body, name=name, in_specs=[ANY, ANY, SEM, SEM] + [ANY] * na, out_specs=[ANY, ANY],
        out_shape=[pltpu.HBM(part.shape, F32), pltpu.HBM(land.shape, F32)],
        input_output_aliases={0: 0, 1: 1},
        compiler_params=SPLIT_COPY,
    )(*_in_hbm([part, land]), send_sems, recv_sems, *after)


def _pack_small(ln1_g, ln1_b, gln_g, gln_b, ln2_g, ln2_b, ln3_g, ln3_b, b_gates, b_s, w_s):
    rows = [ln1_g, ln1_b, gln_g, gln_b, ln2_g, ln2_b, ln3_g, ln3_b]
    rows = [r.reshape(1, D) for r in rows] + [b_gates.reshape(2, D), b_s.reshape(1, D), jnp.zeros((5, D), F32),
                                             w_s.reshape(128, D)]
    return jnp.concatenate(rows, axis=0)


def _unpack_small(p):
    out = [p[i:i + 1] for i in range(8)]
    return out + [p[8:10].reshape(1, 2 * D), p[10:11].reshape(1, 8, BLK), p[16:144].reshape(1, 8, BLK, BLK)]


GROUPS = (("f1g", "f1u", "f1d"), ("w_in",), ("w_ab", "w_gb", "w_out"), ("f2g", "f2u", "f2d"))
LATE_GROUPS = (2, 3)


def _local_step(x, pos_f, target, P, weights_of, grads_ready, flush, small_ready):
    invf = ROPE_THETA ** (-jnp.arange(0, DH, 2, dtype=F32) / DH)
    invf = jnp.tile(invf, 4).reshape(1, 128)
    b_s_t = P["gmlp_b_s"].T

    W = dict(weights_of(0, []))
    h1b, xh1, rstd1, a1, b1, h1t = _ffn_fwd(x, W["f1g"], W["f1u"], W["f1d"], P["ln1_g"], P["ln1_b"], "ffn1_fwd",
                                                emit_t=True)
    W.update(weights_of(1, [h1b]))
    qkv_c = _proj_qkv_rope(h1b, W["w_in"], pos_f, invf, "proj_qkv_rope")
    z = _matmul(h1b, W["w_in"], "nn", "proj_z", n=2 * GW, b_col0=3 * ATT_W, tm=S, tn=512)
    gl = _matmul(h1b, W["w_in"], "nn", "proj_gates", n=2 * D, b_col0=3 * ATT_W + 2 * GW, tm=S, tn=512)
    og = [_attn_fwd(gi, qkv_c[gi], "attn_fwd_g%d" % gi) for gi in range(NG)]
    y_attn, y_attn_t, lse = _attn_combine([o for o, _ in og], [l for _, l in og], "attn_combine")
    y_gmlp, y_gmlp_t = _gmlp_fwd(z, P["gmlp_ln_g"], P["gmlp_ln_b"], P["gmlp_w_s"], b_s_t, "gmlp_fwd")
    W.update(weights_of(2, [y_gmlp]))
    br_a = _matmul(y_attn, W["w_ab"], "nn", "branch_attn", n=D, tm=1024, tn=D)
    br_b = _matmul(y_gmlp, W["w_gb"], "nn", "branch_gmlp", n=D, tm=1024, tn=D)
    merged, merged_t = _merge_fwd(br_a, br_b, gl, P["b_gates"], "merge_fwd")
    h2, h2b, xh2, rstd2 = _resid_ln(xh1, P["ln1_g"], P["ln1_b"], merged, W["w_out"], P["ln2_g"], P["ln2_b"],
                                    "mix_resid_ln2")
    W.update(weights_of(3, [h2b]))
    dr3, a2, b2, dg3, db3, loss = _ffn_fwd(h2, W["f2g"], W["f2u"], W["f2d"], P["ln3_g"], P["ln3_b"],
                                           "ffn2_fwd_loss", target=target)

    g_f2g, g_f2u, g_f2d, dh2 = _ffn_bwd(dr3, h2b, a2, b2, W["f2g"], W["f2u"], W["f2d"], "ffn2_bwd")
    tok = grads_ready(3, dict(f2g=g_f2g, f2u=g_f2u, f2d=g_f2d))
    dr2, dg2, db2 = _ln_bwd(dh2, xh2, rstd2, P["ln2_g"], "ln2_bwd", after=tok)
    g_wout = _wgrad(merged_t, dr2, 128, D, "dw_out", row_sharded=True)
    dmerged = _matmul(dr2, W["w_out"], "nt", "dmerged", n=D, tm=1024, tn=D)
    dab, dbb, dglb, dbg = _merge_bwd(dmerged, br_a, br_b, gl, P["b_gates"], "merge_bwd")
    tok = flush([dab])
    g_wab = _wgrad(y_attn_t, dab, GRP_W // 2, 256, "dw_attn_branch", row_sharded=False, after=tok)
    g_wgb = _wgrad(y_gmlp_t, dbb, 128, D, "dw_gmlp_branch", row_sharded=True)
    tok = grads_ready(2, dict(w_ab=g_wab, w_gb=g_wgb, w_out=g_wout))
    dy_attn = _matmul(dab, W["w_ab"], "nt", "dy_attn", n=GRP_W, tm=1024, tn=GRP_W, after=tok)
    dy_gmlp = _matmul(dbb, W["w_gb"], "nt", "dy_gmlp", n=GW, tm=1024, tn=GW)
    dzb, dws, dbs_t, dgln_g, dgln_b = _gmlp_bwd(z, dy_gmlp, P["gmlp_ln_g"], P["gmlp_ln_b"], P["gmlp_w_s"], b_s_t,
                                                 "gmlp_bwd")
    cls = _class_order([dy_attn, y_attn, lse], "attn_class_order")
    dqkv_c = []
    for gi in range(NG):
        dy_c, y_c, lse_c = [t[None] if gi == 0 else cls[2 * a + gi - 1] for a, t in enumerate((dy_attn, y_attn, lse))]
        dqkv_c.append(_attn_bwd(gi, qkv_c[gi], dy_c, y_c, lse_c, "attn_bwd_g%d" % gi))
    dproj = _rope_bwd(dqkv_c, dzb, dglb, pos_f, invf, "rope_bwd")
    tok = flush([dproj])
    g_win = _wgrad(h1t, dproj, D // 2, IN_SH, "dw_in", row_sharded=False, after=tok)
    tok = grads_ready(1, dict(w_in=g_win))
    dr1, dg1, db1 = _dh1_ln_bwd(dproj, W["w_in"], dr2, xh1, rstd1, P["ln1_g"], "dh1_ln1_bwd", after=tok)
    tok = flush([dr1])
    tok = tok + small_ready(_pack_small(dg1, db1, dgln_g, dgln_b, dg2, db2, dg3, db3, dbg, dbs_t.T, dws))
    g_f1g, g_f1u, g_f1d, dx = _ffn_bwd(dr1, x.astype(BF16), a1, b1, W["f1g"], W["f1u"], W["f1d"], "ffn1_bwd",
                                       after=tok)
    grads_ready(0, dict(f1g=g_f1g, f1u=g_f1u, f1d=g_f1d))
    flush([dx])
    return loss, dx


TRANSPOSED = ("f1g", "f1u", "f2g", "f2u")
KIND = dict(f1g="stack", f1u="stack", f1d="stack", w_in="col", w_ab="col", w_gb="stack", w_out="stack",
            f2g="stack", f2u="stack", f2d="stack")


def kernel(x, positions, ffn1_w_gate, ffn1_w_up, ffn1_w_down, ln1_g, ln1_b, w_in, b_gates, gmlp_ln_g, gmlp_ln_b, gmlp_w_s, gmlp_b_s, w_attn_branch, w_gmlp_branch, w_out, ln2_g, ln2_b, ffn2_w_gate, ffn2_w_up, ffn2_w_down, ln3_g, ln3_b, loss_target, m_ffn1_w_gate, m_ffn1_w_up, m_ffn1_w_down, m_ln1_g, m_ln1_b, m_w_in, m_b_gates, m_gmlp_ln_g, m_gmlp_ln_b, m_gmlp_w_s, m_gmlp_b_s, m_w_attn_branch, m_w_gmlp_branch, m_w_out, m_ln2_g, m_ln2_b, m_ffn2_w_gate, m_ffn2_w_up, m_ffn2_w_down, m_ln3_g, m_ln3_b, v_ffn1_w_gate, v_ffn1_w_up, v_ffn1_w_down, v_ln1_g, v_ln1_b, v_w_in, v_b_gates, v_gmlp_ln_g, v_gmlp_ln_b, v_gmlp_w_s, v_gmlp_b_s, v_w_attn_branch, v_w_gmlp_branch, v_w_out, v_ln2_g, v_ln2_b, v_ffn2_w_gate, v_ffn2_w_up, v_ffn2_w_down, v_ln3_g, v_ln3_b):
    cx, cy, cc = lax.axis_index("x"), lax.axis_index("y"), lax.axis_index("c")
    pos = jnp.stack([cc, 2 * cx + cy, 2 * (1 - cx) + cy, 2 * cx + 1 - cy, 2 * (1 - cx) + 1 - cy]).astype(jnp.int32)

    w_sh = dict(f1g=ffn1_w_gate, f1u=ffn1_w_up, f1d=ffn1_w_down, w_in=w_in, w_ab=w_attn_branch,
                w_gb=w_gmlp_branch, w_out=w_out, f2g=ffn2_w_gate, f2u=ffn2_w_up, f2d=ffn2_w_down)
    m_sh = dict(f1g=m_ffn1_w_gate, f1u=m_ffn1_w_up, f1d=m_ffn1_w_down, w_in=m_w_in, w_ab=m_w_attn_branch,
                w_gb=m_w_gmlp_branch, w_out=m_w_out, f2g=m_ffn2_w_gate, f2u=m_ffn2_w_up, f2d=m_ffn2_w_down)
    v_sh = dict(f1g=v_ffn1_w_gate, f1u=v_ffn1_w_up, f1d=v_ffn1_w_down, w_in=v_w_in, w_ab=v_w_attn_branch,
                w_gb=v_w_gmlp_branch, w_out=v_w_out, f2g=v_ffn2_w_gate, f2u=v_ffn2_w_up, f2d=v_ffn2_w_down)
    w_sh = {k: (v[0].T if k in TRANSPOSED else v[0]) for k, v in w_sh.items()}
    m_sh = {k: (v[0].T if k in TRANSPOSED else v[0]) for k, v in m_sh.items()}
    v_sh = {k: (v[0].T if k in TRANSPOSED else v[0]) for k, v in v_sh.items()}

    started, tokens = [], []
    for gi, names in enumerate(GROUPS):
        placed = [_place_shard(w_sh[k], KIND[k], pos, "place_" + k) for k in names]
        fulls, ssem, rsem, token = _gather_start(placed, [KIND[k] for k in names], [w_sh[k].shape for k in names],
                                                 tokens[-1:], "gather_start_g%d" % gi, both=gi in LATE_GROUPS)
        started.append((fulls, ssem, rsem))
        tokens.append(token)

    def weights_of(gi, after):
        names = GROUPS[gi]
        kinds, dims = [KIND[k] for k in names], [w_sh[k].shape for k in names]
        fulls, ssem, rsem = started[gi]
        fulls = _gather_wait(fulls, ssem, rsem, kinds, dims, list(after) + (tokens if gi == 0 else []),
                             "gather_wait_g%d" % gi, both=gi in LATE_GROUPS)
        if gi not in LATE_GROUPS:
            fulls = _gather_forward(fulls, kinds, dims, "gather_forward_g%d" % gi)
        return {k: (f.reshape(D, D) if k in ("w_gb", "w_out") else f) for k, f in zip(names, fulls)}

    pending, inflight = [], {}

    def grads_ready(gi, gd):
        grads = [gd[k] for k in GROUPS[gi]]
        lands = [lax.empty(g.shape[1:], F32) for g in grads]
        grads, lands, ssem, rsem, token = _pair_start(grads, lands, "rs_pair_start_g%d" % gi)
        pending.append((gi, grads, lands, ssem, rsem))
        return [token]

    def flush(after):
        gi, grads, lands, ssem, rsem = pending.pop()
        names = GROUPS[gi]
        grads, recv_a = _pair_wait(grads, lands, ssem, rsem, after, "rs_pair_wait_g%d" % gi)
        if len(names) > 1 and len({g.shape for g in grads}) == 1:
            psums = _pair_sum_group(grads, recv_a, pos, "rs_pair_sum_g%d" % gi)
        else:
            psums = [_pair_sum(g, r, pos, "rs_pair_sum_" + k) for g, r, k in zip(grads, recv_a, names)]
        lands = [lax.empty((3,) + p.shape[1:], BF16) for p in psums]
        psums, lands, ssem, rsem, token = _chip_start(psums, lands, "rs_chip_start_g%d" % gi)
        inflight[gi] = (grads, recv_a, psums, lands, ssem, rsem, token)
        return [token]

    P = dict(ln1_g=ln1_g, ln1_b=ln1_b, ln2_g=ln2_g, ln2_b=ln2_b, ln3_g=ln3_g, ln3_b=ln3_b, b_gates=b_gates,
             gmlp_ln_g=gmlp_ln_g, gmlp_ln_b=gmlp_ln_b, gmlp_w_s=gmlp_w_s[0], gmlp_b_s=gmlp_b_s[0])
    pos_f = positions.reshape(S, 1).astype(F32)
    small_state = []

    def small_ready(packed):
        land = jnp.zeros((8, SMALL_ROWS, D), F32)
        packed, land, ssem, rsem, token = _small_start(packed, land, "small_start")
        small_state.append((packed, land, ssem, rsem))
        return [token]

    loss_part, dx = _local_step(x[0], pos_f, loss_target[0], P, weights_of, grads_ready, flush, small_ready)
    loss = lax.psum(loss_part[0, 0], ("x", "y", "c"))

    g_out, d_out, m_out, v_out = {}, {}, {}, {}

    def finish(gis, after, tag):
        names, halves = [], []
        for gi in gis:
            grads, recv_a, psums, lands, ssem, rsem, token = inflight[gi]
            recv_b = _chip_wait(psums, lands, ssem, rsem, after + [inflight[0][6]], "rs_chip_wait_g%d" % gi)
            halves += [_owner_sum(g, ra, rb, pos, "rs_owner_sum_" + k)
                       for g, ra, rb, k in zip(grads, recv_a, recv_b, GROUPS[gi])]
            names += GROUPS[gi]
            after = halves[-1:]
        reduced = _sibling_allgather(halves, "rs_sibling_allgather_" + tag)
        for k, gfull in zip(names, reduced):
            res = _adamw(w_sh[k], gfull.reshape(w_sh[k].shape), m_sh[k], v_sh[k], "adamw_" + k)
            after = [res[1]]
            if k in TRANSPOSED:
                res = [r.T for r in res]
            g_out[k], d_out[k], m_out[k], v_out[k] = [r[None] for r in res]
        return after

    after = finish((3, 2, 1), [], "g321")

    small, parts = _small_wait(*small_state[0], after, "small_wait")
    sp = (ln1_g, ln1_b, gmlp_ln_g, gmlp_ln_b, ln2_g, ln2_b, ln3_g, ln3_b, b_gates, gmlp_b_s, gmlp_w_s)
    sm = (m_ln1_g, m_ln1_b, m_gmlp_ln_g, m_gmlp_ln_b, m_ln2_g, m_ln2_b, m_ln3_g, m_ln3_b, m_b_gates, m_gmlp_b_s,
          m_gmlp_w_s)
    sv = (v_ln1_g, v_ln1_b, v_gmlp_ln_g, v_gmlp_ln_b, v_ln2_g, v_ln2_b, v_ln3_g, v_ln3_b, v_b_gates, v_gmlp_b_s,
          v_gmlp_w_s)
    sg, sd, smn, svn = _small_sum_adamw(parts, small, pos, _pack_small(*sp), _pack_small(*sm), _pack_small(*sv),
                                        "small_adamw")
    names = ("ln1_g", "ln1_b", "gmlp_ln_g", "gmlp_ln_b", "ln2_g", "ln2_b", "ln3_g", "ln3_b", "b_gates", "gmlp_b_s",
             "gmlp_w_s")
    for dst, packed in ((g_out, sg), (d_out, sd), (m_out, smn), (v_out, svn)):
        for nm, val in zip(names, _unpack_small(packed)):
            dst[nm] = val
    finish((0,), [sg], "g0")

    order = ("f1g", "f1u", "f1d", "ln1_g", "ln1_b", "w_in", "b_gates", "gmlp_ln_g", "gmlp_ln_b", "gmlp_w_s", "gmlp_b_s",
             "w_ab", "w_gb", "w_out", "ln2_g", "ln2_b", "f2g", "f2u", "f2d", "ln3_g", "ln3_b")
    outs = [loss, dx[None]]
    for dst in (g_out, d_out, m_out, v_out):
        outs += [dst[k] for k in order]
    return tuple(outs)
```

```python
import jax
import jax.numpy as jnp
from jax import lax
from jax.experimental import pallas as pl
from jax.experimental.pallas import tpu as pltpu

F32 = jnp.float32
BF16 = jnp.bfloat16

S = 2048
D = 1024
NSH = 4
FSH = 704
ATT_W = 1536
GRP_W = 512
NG = 3
NH = 8
DH = 64
BLK = 128
NBLK = S // BLK
GW = 1024
IN_W = 8704
IN_SH = IN_W // NSH
ALPHA = 2.0 ** 0.25
LN_EPS = 1e-5
ROPE_THETA = 10000.0
DILATIONS = (1, 4, 16)
ADAM_LR, ADAM_B1, ADAM_B2, ADAM_EPS, ADAM_WD, ADAM_STEP = 0.001, 0.9, 0.999, 1e-08, 0.01, 10
SMALL_ROWS = 144
EPI_ROWS = 256
MESH_T = pl.DeviceIdType.MESH
MIB = 1024 * 1024
NEG_INF = float("-inf")


def _cp(sem, vmem_mib=48):
    return pltpu.CompilerParams(dimension_semantics=sem, vmem_limit_bytes=vmem_mib * MIB)


def _ln_stats(r):
    mu = jnp.mean(r, axis=-1, keepdims=True)
    xc = r - mu
    var = jnp.mean(xc * xc, axis=-1, keepdims=True)
    rstd = lax.rsqrt(var + LN_EPS)
    return xc * rstd, rstd


def _ln_dx(dxh, xh, rstd):
    m1 = jnp.mean(dxh, axis=-1, keepdims=True)
    m2 = jnp.mean(dxh * xh, axis=-1, keepdims=True)
    return rstd * (dxh - m1 - xh * m2)


def _dot_nt(a, b):
    return lax.dot_general(a, b, (((1,), (1,)), ((), ())), preferred_element_type=F32)


def _dot_tn(a, b):
    return lax.dot_general(a, b, (((0,), (0,)), ((), ())), preferred_element_type=F32)


def _dot(a, b):
    return jnp.dot(a, b, preferred_element_type=F32)


def _ffn_fwd(xin, wgt, wut, wd, ln_g, ln_b, name, emit_t=False, target=None):
    with_loss = target is not None
    tm = 1024

    def body(x_ref, wg_ref, wu_ref, wd_ref, g_ref, b_ref, *rest):
        if with_loss:
            t_ref, dr_ref, a_ref, bb_ref, dg_ref, db_ref, loss_ref, acc_ref = rest
        elif emit_t:
            hb_ref, xh_ref, rstd_ref, a_ref, bb_ref, ht_ref, acc_ref = rest
        else:
            hb_ref, xh_ref, rstd_ref, a_ref, bb_ref, acc_ref = rest
        i = pl.program_id(0)
        j = pl.program_id(1)
        xb = x_ref[...].astype(BF16)
        a = _dot_nt(xb, wg_ref[...])
        b = _dot_nt(xb, wu_ref[...])
        a_ref[...] = a.astype(BF16)
        bb_ref[...] = b.astype(BF16)
        s = (a * jax.nn.sigmoid(a)) * b
        f = _dot(s.astype(BF16), wd_ref[...])

        @pl.when(j == 0)
        def _():
            acc_ref[...] = f

        @pl.when(j > 0)
        def _():
            acc_ref[...] += f

        if with_loss:
            @pl.when(jnp.logical_and(j == NSH - 1, i == 0))
            def _():
                dg_ref[...] = jnp.zeros_like(dg_ref)
                db_ref[...] = jnp.zeros_like(db_ref)
                loss_ref[...] = jnp.zeros_like(loss_ref)

        @pl.when(j == NSH - 1)
        def _():
            for c0 in range(0, tm, EPI_ROWS):
                rows = slice(c0, c0 + EPI_ROWS)
                r = ALPHA * x_ref[rows, :] + 0.5 * acc_ref[rows, :]
                xh, rstd = _ln_stats(r)
                h = xh * g_ref[...] + b_ref[...]
                if with_loss:
                    err = h - t_ref[rows, :]
                    dy = err * (1.0 / D)
                    dr_ref[rows, :] = _ln_dx(dy * g_ref[...], xh, rstd)
                    dg_ref[...] += jnp.sum(dy * xh, axis=0, keepdims=True)
                    db_ref[...] += jnp.sum(dy, axis=0, keepdims=True)
                    part = 0.5 * jnp.sum(jnp.mean(err * err, axis=-1, keepdims=True), axis=0, keepdims=True)
                    loss_ref[...] += jnp.broadcast_to(part, (8, 128))
                else:
                    hb_ref[rows, :] = h.astype(BF16)
                    xh_ref[rows, :] = xh
                    rstd_ref[rows, :] = rstd
                    if emit_t:
                        ht_ref[:, rows] = h.T.astype(BF16)

    row = pl.BlockSpec((tm, D), lambda i, j: (i, 0))
    vec = pl.BlockSpec((1, D), lambda i, j: (0, 0))
    wsp = pl.BlockSpec((None, FSH, D), lambda i, j: (j, 0, 0))
    ab = pl.BlockSpec((None, tm, FSH), lambda i, j: (j, i, 0))
    ab_shape = jax.ShapeDtypeStruct((NSH, S, FSH), BF16)
    in_specs, args = [row, wsp, wsp, wsp, vec, vec], (xin, wgt, wut, wd, ln_g, ln_b)
    if with_loss:
        in_specs, args = in_specs + [row], args + (target,)
        out_specs = [row, ab, ab, vec, vec, pl.BlockSpec((8, 128), lambda i, j: (0, 0))]
        out_shape = [jax.ShapeDtypeStruct((S, D), F32), ab_shape, ab_shape, jax.ShapeDtypeStruct((1, D), F32),
                     jax.ShapeDtypeStruct((1, D), F32), jax.ShapeDtypeStruct((8, 128), F32)]
    else:
        out_specs = [row, row, pl.BlockSpec((tm, 1), lambda i, j: (i, 0)), ab, ab]
        out_shape = [jax.ShapeDtypeStruct((S, D), BF16), jax.ShapeDtypeStruct((S, D), F32),
                     jax.ShapeDtypeStruct((S, 1), F32), ab_shape, ab_shape]
        if emit_t:
            out_specs.append(pl.BlockSpec((D, tm), lambda i, j: (0, i)))
            out_shape.append(jax.ShapeDtypeStruct((D, S), BF16))
    return pl.pallas_call(
        body, name=name, grid=(S // tm, NSH), in_specs=in_specs, out_specs=out_specs, out_shape=out_shape,
        scratch_shapes=[pltpu.VMEM((tm, D), F32)],
        compiler_params=_cp(("arbitrary" if with_loss else "parallel", "arbitrary"), vmem_mib=56),
    )(*args)


def _ffn_bwd(dr, xin_b, a, b, wgt, wut, wd, name, after=()):
    tm = 512
    ni = S // tm
    hr = FSH // 2

    def body(dr_ref, a_ref, b_ref, wg_ref, wu_ref, wd_ref, x_hbm, *rest):
        dwg_hbm, dwu_hbm, dwd_hbm, dx_hbm, dx_acc, da_all, db_all, s_all, df_all, x_all, res_buf, sems = rest[len(after):]
        j = pl.program_id(0)
        i = pl.program_id(1)
        rows = pl.ds(pl.multiple_of(i * tm, tm), tm)

        @pl.when(jnp.logical_and(j == 0, i == 0))
        def _():
            cp = pltpu.make_async_copy(x_hbm, x_all, sems.at[0])
            cp.start()
            cp.wait()

        drv = dr_ref[...]
        df = (0.5 * drv).astype(BF16)

        @pl.when(j == 0)
        def _():
            df_all[rows, :] = df

        ds = jnp.concatenate([_dot_nt(df, wd_ref[0:384, :]), _dot_nt(df, wd_ref[384:FSH, :])], axis=1)
        av = a_ref[...].astype(F32)
        bv = b_ref[...].astype(F32)
        sig = jax.nn.sigmoid(av)
        sl = av * sig
        da = (ds * bv * (sig * (1.0 + av * (1.0 - sig)))).astype(BF16)
        db = (ds * sl).astype(BF16)
        da_all[rows, :] = da
        db_all[rows, :] = db
        s_all[rows, :] = (sl * bv).astype(BF16)
        dx = _dot(da, wg_ref[...]) + _dot(db, wu_ref[...])

        @pl.when(j == 0)
        def _():
            dx_acc[rows, :] = ALPHA * drv + dx

        @pl.when(j > 0)
        def _():
            dx_acc[rows, :] += dx

        @pl.when(i == ni - 1)
        def _():
            copies = []
            for n, (lhs, rhs, out) in enumerate(((da_all, x_all, dwg_hbm), (db_all, x_all, dwu_hbm),
                                                 (s_all, df_all, dwd_hbm))):
                slot = n % 2
                if n >= 2:
                    for cp in copies[2 * (n - 2): 2 * (n - 2) + 2]:
                        cp.wait()
                res_buf[slot] = _dot_tn(lhs[...], rhs[...])
                for h in range(2):
                    cp = pltpu.make_async_copy(res_buf.at[slot, pl.ds(h * hr, hr), :], out.at[h, j],
                                               sems.at[1 + 2 * slot + h])
                    cp.start()
                    copies.append(cp)
            for cp in copies[2:]:
                cp.wait()

        @pl.when(jnp.logical_and(j == NSH - 1, i == ni - 1))
        def _():
            cp = pltpu.make_async_copy(dx_acc, dx_hbm, sems.at[0])
            cp.start()
            cp.wait()

    row = pl.BlockSpec((tm, D), lambda j, i: (i, 0))
    wsp = pl.BlockSpec((None, FSH, D), lambda j, i: (j, 0, 0))
    ab = pl.BlockSpec((None, tm, FSH), lambda j, i: (j, i, 0))
    dwshape = jax.ShapeDtypeStruct((2, NSH, hr, D), F32)
    return pl.pallas_call(
        body, name=name, grid=(NSH, ni),
        in_specs=[row, ab, ab, wsp, wsp, wsp, ANY] + [ANY] * len(after),
        out_specs=[ANY, ANY, ANY, ANY],
        out_shape=[dwshape, dwshape, dwshape, jax.ShapeDtypeStruct((S, D), F32)],
        scratch_shapes=[pltpu.VMEM((S, D), F32), pltpu.VMEM((S, FSH), BF16), pltpu.VMEM((S, FSH), BF16),
                        pltpu.VMEM((S, FSH), BF16), pltpu.VMEM((S, D), BF16), pltpu.VMEM((S, D), BF16),
                        pltpu.VMEM((2, FSH, D), F32), pltpu.SemaphoreType.DMA((5,))],
        compiler_params=_cp(("arbitrary", "arbitrary"), vmem_mib=58),
    )(dr, a, b, wgt, wut, wd, xin_b, *after)


def _matmul(a, b, mode, name, *, n, tm, tn, b_col0=0, after=()):
    m, k = a.shape
    assert m % tm == 0 and n % tn == 0 and b_col0 % tn == 0
    off = b_col0 // tn
    na = len(after)

    def body(*refs):
        a_ref, b_ref, o_ref = refs[na:]
        av = a_ref[...].astype(BF16)
        o_ref[...] = _dot(av, b_ref[...]) if mode == "nn" else _dot_nt(av, b_ref[...])

    if mode == "nn":
        b_spec = pl.BlockSpec((k, tn), lambda i, j: (0, j + off))
    else:
        b_spec = pl.BlockSpec((tn, k), lambda i, j: (j, 0))
    return pl.pallas_call(
        body, name=name, grid=(m // tm, n // tn),
        in_specs=[pl.BlockSpec(memory_space=pl.ANY)] * na + [pl.BlockSpec((tm, k), lambda i, j: (i, 0)), b_spec],
        out_specs=pl.BlockSpec((tm, tn), lambda i, j: (i, j)),
        out_shape=jax.ShapeDtypeStruct((m, n), F32),
        compiler_params=_cp(("parallel", "parallel")),
    )(*after, a, b)


def _wgrad(xt, y, rh, c, name, row_sharded, after=()):
    na = len(after)
    if row_sharded:
        tc = 512

        def body(x_ref, y_ref, *rest):
            o_ref = rest[na]
            res = _dot(x_ref[...], y_ref[...].astype(BF16))
            for j in range(NSH):
                for h in range(2):
                    o_ref[h, j] = res[(2 * j + h) * rh:(2 * j + h + 1) * rh, :]

        grid = (c // tc,)
        in_specs = [pl.BlockSpec((2 * NSH * rh, S), lambda g: (0, 0)), pl.BlockSpec((S, tc), lambda g: (0, g))]
        out_specs = pl.BlockSpec((2, NSH, rh, tc), lambda g: (0, 0, 0, g))
        sem = ("parallel",)
    else:
        def body(x_ref, y_ref, *rest):
            rest[na][...] = _dot(x_ref[...], y_ref[...].astype(BF16))

        grid = (2, NSH)
        in_specs = [pl.BlockSpec((rh, S), lambda h, j: (h, 0)), pl.BlockSpec((S, c), lambda h, j: (0, j))]
        out_specs = pl.BlockSpec((None, None, rh, c), lambda h, j: (h, j, 0, 0))
        sem = ("parallel", "parallel")
    return pl.pallas_call(
        body, name=name, grid=grid, in_specs=in_specs + [pl.BlockSpec(memory_space=pl.ANY)] * na, out_specs=out_specs,
        out_shape=jax.ShapeDtypeStruct((2, NSH, rh, c), F32),
        compiler_params=_cp(sem, vmem_mib=56),
    )(xt, y, *after)


def _resid_ln(res_xh, res_g, res_b, a, w, ln_g, ln_b, name):
    tm = 512

    def body(rx_ref, rg_ref, rb_ref, a_ref, w_ref, g_ref, b_ref, h_ref, hb_ref, xh_ref, rstd_ref):
        r = ALPHA * (rx_ref[...] * rg_ref[...] + rb_ref[...]) + _dot(a_ref[...], w_ref[...])
        xh, rstd = _ln_stats(r)
        h = xh * g_ref[...] + b_ref[...]
        h_ref[...] = h
        hb_ref[...] = h.astype(BF16)
        xh_ref[...] = xh
        rstd_ref[...] = rstd

    row = pl.BlockSpec((tm, D), lambda i: (i, 0))
    vec = pl.BlockSpec((1, D), lambda i: (0, 0))
    return pl.pallas_call(
        body, name=name, grid=(S // tm,),
        in_specs=[row, vec, vec, row, pl.BlockSpec((D, D), lambda i: (0, 0)), vec, vec],
        out_specs=[row, row, row, pl.BlockSpec((tm, 1), lambda i: (i, 0))],
        out_shape=[jax.ShapeDtypeStruct((S, D), F32), jax.ShapeDtypeStruct((S, D), BF16),
                   jax.ShapeDtypeStruct((S, D), F32), jax.ShapeDtypeStruct((S, 1), F32)],
        compiler_params=_cp(("parallel",)),
    )(res_xh, res_g, res_b, a, w, ln_g, ln_b)


def _dh1_ln_bwd(dproj, w_in, dr2, xh, rstd, ln_g, name, after=()):
    tm, tk, ch = 1024, IN_SH, EPI_ROWS
    nk = IN_W // tk
    na = len(after)

    def body(*refs):
        a_ref, b_ref, add_ref, xh_ref, rstd_ref, g_ref, dr_ref, dg_ref, db_ref, acc_ref = refs[na:]
        i = pl.program_id(0)
        k = pl.program_id(1)
        p = _dot_nt(a_ref[...], b_ref[...])

        @pl.when(k == 0)
        def _():
            acc_ref[...] = p

        @pl.when(k > 0)
        def _():
            acc_ref[...] += p

        @pl.when(jnp.logical_and(k == nk - 1, i == 0))
        def _():
            dg_ref[...] = jnp.zeros_like(dg_ref)
            db_ref[...] = jnp.zeros_like(db_ref)

        @pl.when(k == nk - 1)
        def _():
            for c0 in range(0, tm, ch):
                rows = slice(c0, c0 + ch)
                dy = acc_ref[rows, :] + ALPHA * add_ref[rows, :]
                xhv = xh_ref[rows, :]
                dr_ref[rows, :] = _ln_dx(dy * g_ref[...], xhv, rstd_ref[rows, :])
                dg_ref[...] += jnp.sum(dy * xhv, axis=0, keepdims=True)
                db_ref[...] += jnp.sum(dy, axis=0, keepdims=True)

    row = pl.BlockSpec((tm, D), lambda i, k: (i, 0))
    vec = pl.BlockSpec((1, D), lambda i, k: (0, 0))
    return pl.pallas_call(
        body, name=name, grid=(S // tm, nk),
        in_specs=[pl.BlockSpec(memory_space=pl.ANY)] * na
        + [pl.BlockSpec((tm, tk), lambda i, k: (i, k)), pl.BlockSpec((D, tk), lambda i, k: (0, k)), row, row,
           pl.BlockSpec((tm, 1), lambda i, k: (i, 0)), vec],
        out_specs=[row, vec, vec],
        out_shape=[jax.ShapeDtypeStruct((S, D), F32), jax.ShapeDtypeStruct((1, D), F32),
                   jax.ShapeDtypeStruct((1, D), F32)],
        scratch_shapes=[pltpu.VMEM((tm, D), F32)],
        compiler_params=_cp(("arbitrary", "arbitrary"), vmem_mib=56),
    )(*after, dproj, w_in, dr2, xh, rstd, ln_g)


def _ln_bwd(dout, xh, rstd, ln_g, name, after=()):
    tm = 512
    na = len(after)

    def body(*refs):
        y_ref, xh_ref, rstd_ref, g_ref, dr_ref, dg_ref, db_ref = refs[na:]
        dy = y_ref[...]
        i = pl.program_id(0)
        xh = xh_ref[...]
        dr_ref[...] = _ln_dx(dy * g_ref[...], xh, rstd_ref[...])
        dg = jnp.sum(dy * xh, axis=0, keepdims=True)
        db = jnp.sum(dy, axis=0, keepdims=True)

        @pl.when(i == 0)
        def _():
            dg_ref[...] = dg
            db_ref[...] = db

        @pl.when(i > 0)
        def _():
            dg_ref[...] += dg
            db_ref[...] += db

    row = pl.BlockSpec((tm, D), lambda i: (i, 0))
    vec = pl.BlockSpec((1, D), lambda i: (0, 0))
    return pl.pallas_call(
        body, name=name, grid=(S // tm,),
        in_specs=[pl.BlockSpec(memory_space=pl.ANY)] * na + [row, row, pl.BlockSpec((tm, 1), lambda i: (i, 0)), vec],
        out_specs=[row, vec, vec],
        out_shape=[jax.ShapeDtypeStruct((S, D), F32), jax.ShapeDtypeStruct((1, D), F32),
                   jax.ShapeDtypeStruct((1, D), F32)],
        compiler_params=_cp(("arbitrary",)),
    )(*after, dout, xh, rstd, ln_g)


ROPE_TM = 256


def _rope_tables(pos_ref, invf_ref, sign):
    ang = pos_ref[...] * invf_ref[...]
    lane = lax.broadcasted_iota(jnp.int32, ang.shape, 1)
    first = (lane % DH) < (DH // 2)
    sinv = jnp.sin(ang) * sign
    return first, jnp.cos(ang), jnp.where(first, -sinv, sinv)


def _rotate(x, first, cosf, sinf):
    return x * cosf + jnp.where(first, pltpu.roll(x, 96, 1), pltpu.roll(x, 32, 1)) * sinf


def _proj_qkv_rope(hb, w_in, pos_f, invf, name):
    tm = 2 * ROPE_TM

    def body(h_ref, w_ref, pos_ref, invf_ref, o0_ref, o1_ref, o2_ref, buf_ref):
        rot = pl.program_id(1) < 2
        first, cosf, sinf = _rope_tables(pos_ref, invf_ref, 1.0)
        cosf = jnp.where(rot, cosf, 1.0)
        sinf = jnp.where(rot, sinf, 0.0)
        acc = _dot(h_ref[...], w_ref[...])
        for gi, (d, o_ref) in enumerate(zip(DILATIONS, (o0_ref, o1_ref, o2_ref))):
            for ch in range(GRP_W // 128):
                cols = slice(ch * 128, (ch + 1) * 128)
                x = _rotate(acc[:, gi * GRP_W + ch * 128: gi * GRP_W + (ch + 1) * 128], first, cosf, sinf)
                if d == 1:
                    o_ref[0, :, cols] = x.astype(BF16)
                else:
                    buf_ref[...] = x
                    for r in range(d):
                        o_ref[r, :, cols] = buf_ref[pl.ds(r, tm // d, stride=d), :].astype(BF16)

    return pl.pallas_call(
        body, name=name, grid=(S // tm, 3),
        in_specs=[pl.BlockSpec((tm, D), lambda i, s: (i, 0)), pl.BlockSpec((D, ATT_W), lambda i, s: (0, s)),
                  pl.BlockSpec((tm, 1), lambda i, s: (i, 0)), pl.BlockSpec((1, 128), lambda i, s: (0, 0))],
        out_specs=[pl.BlockSpec((d, tm // d, GRP_W), lambda i, s: (0, i, s)) for d in DILATIONS],
        out_shape=[jax.ShapeDtypeStruct((d, S // d, 3 * GRP_W), BF16) for d in DILATIONS],
        scratch_shapes=[pltpu.VMEM((tm, 128), F32)],
        compiler_params=_cp(("parallel", "parallel")),
    )(hb, w_in, pos_f, invf)


def _rope_bwd(dqkv_c, dz, dgl, pos_f, invf, name):
    tm = ROPE_TM

    def body(*refs):
        g_refs, (dz_ref, dgl_ref, pos_ref, invf_ref, o_ref, buf_ref) = refs[:9], refs[9:]
        o_ref[:, 3 * ATT_W:3 * ATT_W + 2 * GW] = dz_ref[...]
        o_ref[:, 3 * ATT_W + 2 * GW:IN_W] = dgl_ref[...]
        first, cosf, sinf = _rope_tables(pos_ref, invf_ref, -1.0)
        for sec in range(3):
            for gi, d in enumerate(DILATIONS):
                g_ref = g_refs[3 * gi + sec]
                for ch in range(GRP_W // 128):
                    cols = slice(ch * 128, (ch + 1) * 128)
                    if d == 1:
                        x = g_ref[0, :, cols]
                    else:
                        for r in range(d):
                            buf_ref[pl.ds(r, tm // d, stride=d), :] = g_ref[r, :, cols]
                        x = buf_ref[...]
                    if sec < 2:
                        x = _rotate(x, first, cosf, sinf)
                    dst = sec * ATT_W + gi * GRP_W + ch * 128
                    o_ref[:, dst:dst + 128] = x.astype(BF16)

    g_specs = [pl.BlockSpec((d, tm // d, GRP_W), lambda i: (0, i, 0)) for d in DILATIONS for _ in range(3)]
    return pl.pallas_call(
        body, name=name, grid=(S // tm,),
        in_specs=g_specs + [pl.BlockSpec((tm, 2 * GW), lambda i: (i, 0)), pl.BlockSpec((tm, 2 * D), lambda i: (i, 0)),
                            pl.BlockSpec((tm, 1), lambda i: (i, 0)), pl.BlockSpec((1, 128), lambda i: (0, 0))],
        out_specs=pl.BlockSpec((tm, IN_W), lambda i: (i, 0)),
        out_shape=jax.ShapeDtypeStruct((S, IN_W), BF16),
        scratch_shapes=[pltpu.VMEM((tm, 128), F32)],
        compiler_params=_cp(("parallel",)),
    )(*[g for grp in dqkv_c for g in grp], dz, dgl, pos_f, invf)


def _class_order(ts, name):
    tm = ROPE_TM
    n = len(ts)

    def body(*refs):
        buf_ref = refs[3 * n]
        for a in range(n):
            for ch in range(GRP_W // 128):
                cols = slice(ch * 128, (ch + 1) * 128)
                buf_ref[...] = refs[a][:, cols]
                for b, d in enumerate(DILATIONS[1:]):
                    for r in range(d):
                        refs[n + 2 * a + b][r, :, cols] = buf_ref[pl.ds(r, tm // d, stride=d), :]

    return pl.pallas_call(
        body, name=name, grid=(S // tm,),
        in_specs=[pl.BlockSpec((tm, GRP_W), lambda i: (i, 0))] * n,
        out_specs=[pl.BlockSpec((d, tm // d, GRP_W), lambda i: (0, i, 0)) for _ in range(n) for d in DILATIONS[1:]],
        out_shape=[jax.ShapeDtypeStruct((d, S // d, GRP_W), F32) for _ in range(n) for d in DILATIONS[1:]],
        scratch_shapes=[pltpu.VMEM((tm, 128), F32)],
        compiler_params=_cp(("parallel",)),
    )(*ts)


def _own_lanes(h):
    return (lax.broadcasted_iota(jnp.int32, (1, 2 * DH), 1) // DH) == (h % 2)


def _heads(ref):
    out = []
    for h in range(NH):
        pair = ref[:, (h // 2) * 2 * DH:(h // 2 + 1) * 2 * DH]
        out.append(jnp.where(_own_lanes(h), pair, jnp.zeros_like(pair)))
    return jnp.stack(out)


def _unheads(t3):
    return jnp.concatenate([t3[2 * p] + t3[2 * p + 1] for p in range(NH // 2)], axis=1)


def _bdot_nt(a, b):
    return lax.dot_general(a, b, (((2,), (2,)), ((0,), (0,))), preferred_element_type=F32)


def _bdot(a, b):
    return lax.dot_general(a, b, (((2,), (1,)), ((0,), (0,))), preferred_element_type=F32)


def _bdot_tn(a, b):
    return lax.dot_general(a, b, (((1,), (1,)), ((0,), (0,))), preferred_element_type=F32)


def _attn_fwd(gi, qkv_c, name):
    d = DILATIONS[gi]
    nblk = S // d // BLK

    def body(*refs):
        if nblk > 1:
            q_ref, kc_ref, kp_ref, vc_ref, vp_ref, o_ref, lse_ref = refs
            has_prev = pl.program_id(1) != 0
        else:
            q_ref, kc_ref, vc_ref, o_ref, lse_ref = refs
        qi = lax.broadcasted_iota(jnp.int32, (NH, BLK, BLK), 1)
        kj = lax.broadcasted_iota(jnp.int32, (NH, BLK, BLK), 2)
        q = _heads(q_ref)
        sc = jnp.where(kj <= qi, _bdot_nt(q, _heads(kc_ref)) * 0.125, NEG_INF)
        m = jnp.max(sc, axis=-1, keepdims=True)
        if nblk > 1:
            mask_p = jnp.logical_and(kj >= qi, has_prev)
            sp = jnp.where(mask_p, _bdot_nt(q, _heads(kp_ref)) * 0.125, NEG_INF)
            m = jnp.maximum(m, jnp.max(sp, axis=-1, keepdims=True))
        pc = jnp.exp(sc - m)
        l = jnp.sum(pc, axis=-1, keepdims=True)
        o = _bdot(pc.astype(BF16), _heads(vc_ref))
        if nblk > 1:
            pp = jnp.exp(sp - m)
            l = l + jnp.sum(pp, axis=-1, keepdims=True)
            o = o + _bdot(pp.astype(BF16), _heads(vp_ref))
        o_ref[...] = _unheads(o / l)
        lse = jnp.broadcast_to(m + jnp.log(l), (NH, BLK, 2 * DH))
        lse_ref[...] = _unheads(jnp.stack([jnp.where(_own_lanes(h), lse[h], 0.0) for h in range(NH)]))

    def cur(sec):
        return pl.BlockSpec((None, BLK, GRP_W), lambda r, n: (r, n, sec))

    def prev(sec):
        return pl.BlockSpec((None, BLK, GRP_W), lambda r, n: (r, jnp.maximum(n - 1, 0), sec))

    out = pl.BlockSpec((None, BLK, GRP_W), lambda r, n: (r, n, 0))
    shp = jax.ShapeDtypeStruct((d, S // d, GRP_W), F32)
    if nblk > 1:
        in_specs, args = [cur(0), cur(1), prev(1), cur(2), prev(2)], (qkv_c,) * 5
    else:
        in_specs, args = [cur(0), cur(1), cur(2)], (qkv_c,) * 3
    return pl.pallas_call(
        body, name=name, grid=(d, nblk), in_specs=in_specs, out_specs=[out, out], out_shape=[shp, shp],
        compiler_params=_cp(("parallel", "parallel")),
    )(*args)


def _attn_combine(os, lses, name):
    tm = ROPE_TM

    def body(o0_ref, o1_ref, o2_ref, l0_ref, l1_ref, l2_ref, y_ref, yt_ref, l_ref, buf_ref):
        def token_order(ref, d, cols, slot):
            if d == 1:
                return ref[0, :, cols]
            for r in range(d):
                buf_ref[slot, pl.ds(r, tm // d, stride=d), :] = ref[r, :, cols]
            return buf_ref[slot]

        for ch in range(GRP_W // 128):
            cols = slice(ch * 128, (ch + 1) * 128)
            o = [token_order(ref, d, cols, k) for k, (ref, d) in enumerate(zip((o0_ref, o1_ref, o2_ref), DILATIONS))]
            ls = [token_order(ref, d, cols, 3 + k)
                  for k, (ref, d) in enumerate(zip((l0_ref, l1_ref, l2_ref), DILATIONS))]
            m = jnp.maximum(jnp.maximum(ls[0], ls[1]), ls[2])
            e = [jnp.exp(l - m) for l in ls]
            den = e[0] + e[1] + e[2]
            y = (e[0] * o[0] + e[1] * o[1] + e[2] * o[2]) / den
            y_ref[:, cols] = y
            yt_ref[cols, :] = y.T.astype(BF16)
            l_ref[:, cols] = m + jnp.log(den)

    blk = pl.BlockSpec((tm, GRP_W), lambda i: (i, 0))
    cls = [pl.BlockSpec((d, tm // d, GRP_W), lambda i: (0, i, 0)) for d in DILATIONS]
    shp = jax.ShapeDtypeStruct((S, GRP_W), F32)
    return pl.pallas_call(
        body, name=name, grid=(S // tm,), in_specs=cls + cls,
        out_specs=[blk, pl.BlockSpec((GRP_W, tm), lambda i: (0, i)), blk],
        out_shape=[shp, jax.ShapeDtypeStruct((GRP_W, S), BF16), shp],
        scratch_shapes=[pltpu.VMEM((6, tm, 128), F32)],
        compiler_params=_cp(("parallel",)),
    )(*os, *lses)


def _attn_bwd(gi, qkv_c, dy_c, y_c, lse_c, name):
    d = DILATIONS[gi]
    nblk = S // d // BLK

    def body(*refs):
        if nblk > 1:
            (q_ref, qn_ref, k_ref, kp_ref, v_ref, vp_ref, dy_ref, dyn_ref, y_ref, yn_ref, l_ref, ln_ref,
             dq_ref, dk_ref, dv_ref) = refs
            n = pl.program_id(1)
            has_prev = n != 0
            has_next = n != nblk - 1
        else:
            q_ref, k_ref, v_ref, dy_ref, y_ref, l_ref, dq_ref, dk_ref, dv_ref = refs
        qi = lax.broadcasted_iota(jnp.int32, (NH, BLK, BLK), 1)
        kj = lax.broadcasted_iota(jnp.int32, (NH, BLK, BLK), 2)

        def lse_col(ref):
            return jnp.stack([ref[:, h * DH:h * DH + 1] for h in range(NH)])

        q, k, v = _heads(q_ref), _heads(k_ref), _heads(v_ref)
        dy = _heads(dy_ref)
        dd = jnp.sum(dy * _heads(y_ref), axis=-1, keepdims=True)
        lcol = lse_col(l_ref)
        dyb = dy.astype(BF16)
        p = jnp.exp(jnp.where(kj <= qi, _bdot_nt(q, k) * 0.125, NEG_INF) - lcol)
        ds = (p * (_bdot_nt(dyb, v) - dd)).astype(BF16)
        dq = _bdot(ds, k)
        dk = _bdot_tn(ds, q)
        dv = _bdot_tn(p.astype(BF16), dyb)
        if nblk > 1:
            qn, kpv, vpv = _heads(qn_ref), _heads(kp_ref), _heads(vp_ref)
            dyn = _heads(dyn_ref)
            ddn = jnp.sum(dyn * _heads(yn_ref), axis=-1, keepdims=True)
            lncol = lse_col(ln_ref)
            dynb = dyn.astype(BF16)
            mask_p = jnp.logical_and(kj >= qi, has_prev)
            pp = jnp.exp(jnp.where(mask_p, _bdot_nt(q, kpv) * 0.125, NEG_INF) - lcol)
            dsp = (pp * (_bdot_nt(dyb, vpv) - dd)).astype(BF16)
            dq = dq + _bdot(dsp, kpv)
            mask_n = jnp.logical_and(kj >= qi, has_next)
            pn = jnp.exp(jnp.where(mask_n, _bdot_nt(qn, k) * 0.125, NEG_INF) - lncol)
            dsn = (pn * (_bdot_nt(dynb, v) - ddn)).astype(BF16)
            dk = dk + _bdot_tn(dsn, qn)
            dv = dv + _bdot_tn(pn.astype(BF16), dynb)
        dq_ref[...] = _unheads(dq) * 0.125
        dk_ref[...] = _unheads(dk) * 0.125
        dv_ref[...] = _unheads(dv)

    def spec(sec, shift):
        def idx(r, n):
            return (r, jnp.clip(n + shift, 0, nblk - 1), sec)
        return pl.BlockSpec((None, BLK, GRP_W), idx)

    if nblk > 1:
        in_specs = [spec(0, 0), spec(0, 1), spec(1, 0), spec(1, -1), spec(2, 0), spec(2, -1),
                    spec(0, 0), spec(0, 1), spec(0, 0), spec(0, 1), spec(0, 0), spec(0, 1)]
        args = (qkv_c,) * 6 + (dy_c, dy_c, y_c, y_c, lse_c, lse_c)
    else:
        in_specs = [spec(0, 0), spec(1, 0), spec(2, 0), spec(0, 0), spec(0, 0), spec(0, 0)]
        args = (qkv_c, qkv_c, qkv_c, dy_c, y_c, lse_c)
    out = spec(0, 0)
    shp = jax.ShapeDtypeStruct((d, S // d, GRP_W), F32)
    return pl.pallas_call(
        body, name=name, grid=(d, nblk), in_specs=in_specs, out_specs=[out, out, out], out_shape=[shp, shp, shp],
        compiler_params=_cp(("parallel", "parallel")),
    )(*args)


_SQRT_HALF = 0.7071067811865476
_INV_SQRT_2PI = 0.3989422804014327


def _gelu(z):
    return 0.5 * z * (1.0 + lax.erf(z * _SQRT_HALF))


def _gelu_grad(z):
    return 0.5 * (1.0 + lax.erf(z * _SQRT_HALF)) + z * (jnp.exp(-0.5 * z * z) * _INV_SQRT_2PI)


def _tril_mask():
    t = lax.broadcasted_iota(jnp.int32, (BLK, BLK), 0)
    s = lax.broadcasted_iota(jnp.int32, (BLK, BLK), 1)
    return s <= t


def _groups(t):
    return jnp.stack([t[:, g * BLK:(g + 1) * BLK] for g in range(8)])


def _ungroup(t3):
    return jnp.concatenate([t3[g] for g in range(8)], axis=1)


def _group_bias(bs_ref):
    return jnp.stack([bs_ref[:, g:g + 1] for g in range(8)])


def _gmlp_fwd(z, ln_g, ln_b, w_s, b_s_t, name):
    def body(z_ref, g_ref, b_ref, ws_ref, bs_ref, y_ref, yt_ref):
        zg = _gelu(z_ref[...])
        u = zg[:, :GW]
        xh, _ = _ln_stats(zg[:, GW:])
        vn = (xh * g_ref[...] + b_ref[...]).astype(BF16)
        wt = jnp.where(_tril_mask(), ws_ref[...], 0.0).astype(BF16)
        yv = u * _ungroup(_bdot(wt, _groups(vn)) + _group_bias(bs_ref))
        y_ref[...] = yv.astype(BF16)
        yt_ref[...] = yv.T.astype(BF16)

    vec = pl.BlockSpec((1, GW), lambda n: (0, 0))
    return pl.pallas_call(
        body, name=name, grid=(NBLK,),
        in_specs=[pl.BlockSpec((BLK, 2 * GW), lambda n: (n, 0)), vec, vec,
                  pl.BlockSpec((8, BLK, BLK), lambda n: (0, 0, 0)), pl.BlockSpec((BLK, 8), lambda n: (0, 0))],
        out_specs=[pl.BlockSpec((BLK, GW), lambda n: (n, 0)), pl.BlockSpec((GW, BLK), lambda n: (0, n))],
        out_shape=[jax.ShapeDtypeStruct((S, GW), BF16), jax.ShapeDtypeStruct((GW, S), BF16)],
        compiler_params=_cp(("parallel",)),
    )(z, ln_g, ln_b, w_s, b_s_t)


def _gmlp_bwd(z, dy, ln_g, ln_b, w_s, b_s_t, name):
    def body(z_ref, dy_ref, g_ref, b_ref, ws_ref, bs_ref, dz_ref, dws_ref, dbs_ref, dg_ref, db_ref, dvn_ref):
        n = pl.program_id(0)
        zv = z_ref[...]
        zg = _gelu(zv)
        u = zg[:, :GW]
        xh, rstd = _ln_stats(zg[:, GW:])
        vn = (xh * g_ref[...] + b_ref[...]).astype(BF16)
        tril = _tril_mask()

        @pl.when(n == 0)
        def _():
            dws_ref[...] = jnp.zeros_like(dws_ref)
            dbs_ref[...] = jnp.zeros_like(dbs_ref)
            dg_ref[...] = jnp.zeros_like(dg_ref)
            db_ref[...] = jnp.zeros_like(db_ref)

        wt = jnp.where(tril, ws_ref[...], 0.0).astype(BF16)
        vn3 = _groups(vn)
        dyv = dy_ref[...]
        mixed = _ungroup(_bdot(wt, vn3) + _group_bias(bs_ref))
        dz_ref[:, :GW] = (dyv * mixed * _gelu_grad(zv[:, :GW])).astype(BF16)
        dmix3 = _groups(dyv * u)
        dmb = dmix3.astype(BF16)
        dws_ref[...] += jnp.where(tril, _bdot_nt(dmb, vn3), 0.0)
        dbsum = jnp.sum(dmix3, axis=-1, keepdims=True)
        for gg in range(8):
            dbs_ref[:, gg:gg + 1] += dbsum[gg]
        dvn_ref[...] = _ungroup(_bdot_tn(wt, dmb))

        dvn = dvn_ref[...]
        dg_ref[...] += jnp.sum(dvn * xh, axis=0, keepdims=True)
        db_ref[...] += jnp.sum(dvn, axis=0, keepdims=True)
        dvg = _ln_dx(dvn * g_ref[...], xh, rstd)
        dz_ref[:, GW:] = (dvg * _gelu_grad(zv[:, GW:])).astype(BF16)

    vec = pl.BlockSpec((1, GW), lambda n: (0, 0))
    ws = pl.BlockSpec((8, BLK, BLK), lambda n: (0, 0, 0))
    bs = pl.BlockSpec((BLK, 8), lambda n: (0, 0))
    return pl.pallas_call(
        body, name=name, grid=(NBLK,),
        in_specs=[pl.BlockSpec((BLK, 2 * GW), lambda n: (n, 0)), pl.BlockSpec((BLK, GW), lambda n: (n, 0)),
                  vec, vec, ws, bs],
        out_specs=[pl.BlockSpec((BLK, 2 * GW), lambda n: (n, 0)), ws, bs, vec, vec],
        out_shape=[jax.ShapeDtypeStruct((S, 2 * GW), BF16), jax.ShapeDtypeStruct((8, BLK, BLK), F32),
                   jax.ShapeDtypeStruct((BLK, 8), F32), jax.ShapeDtypeStruct((1, GW), F32),
                   jax.ShapeDtypeStruct((1, GW), F32)],
        scratch_shapes=[pltpu.VMEM((BLK, GW), F32)],
        compiler_params=_cp(("arbitrary",)),
    )(z, dy, ln_g, ln_b, w_s, b_s_t)


def _merge_fwd(a, b, gl, b_gates, name):
    tm = 512

    def body(a_ref, b_ref, g0_ref, g1_ref, bg_ref, o_ref, ot_ref):
        g0 = jax.nn.sigmoid(g0_ref[...] + bg_ref[:, :D])
        g1 = jax.nn.sigmoid(g1_ref[...] + bg_ref[:, D:])
        mg = g0 * a_ref[...] + g1 * b_ref[...]
        o_ref[...] = mg.astype(BF16)
        ot_ref[...] = mg.T.astype(BF16)

    row = pl.BlockSpec((tm, D), lambda i: (i, 0))
    return pl.pallas_call(
        body, name=name, grid=(S // tm,),
        in_specs=[row, row, row, pl.BlockSpec((tm, D), lambda i: (i, 1)), pl.BlockSpec((1, 2 * D), lambda i: (0, 0))],
        out_specs=[row, pl.BlockSpec((D, tm), lambda i: (0, i))],
        out_shape=[jax.ShapeDtypeStruct((S, D), BF16), jax.ShapeDtypeStruct((D, S), BF16)],
        compiler_params=_cp(("parallel",)),
    )(a, b, gl, gl, b_gates)


def _merge_bwd(dm, a, b, gl, b_gates, name):
    tm = 512

    def body(dm_ref, a_ref, b_ref, g0_ref, g1_ref, bg_ref, da_ref, db_ref, dgl_ref, dbg_ref):
        i = pl.program_id(0)
        dmv = dm_ref[...]
        g0 = jax.nn.sigmoid(g0_ref[...] + bg_ref[:, :D])
        g1 = jax.nn.sigmoid(g1_ref[...] + bg_ref[:, D:])
        da_ref[...] = (dmv * g0).astype(BF16)
        db_ref[...] = (dmv * g1).astype(BF16)
        d0 = dmv * a_ref[...] * g0 * (1.0 - g0)
        d1 = dmv * b_ref[...] * g1 * (1.0 - g1)
        dgl_ref[:, :D] = d0.astype(BF16)
        dgl_ref[:, D:] = d1.astype(BF16)
        s0 = jnp.sum(d0, axis=0, keepdims=True)
        s1 = jnp.sum(d1, axis=0, keepdims=True)

        @pl.when(i == 0)
        def _():
            dbg_ref[:, :D] = s0
            dbg_ref[:, D:] = s1

        @pl.when(i > 0)
        def _():
            dbg_ref[:, :D] += s0
            dbg_ref[:, D:] += s1

    row = pl.BlockSpec((tm, D), lambda i: (i, 0))
    wide = pl.BlockSpec((tm, 2 * D), lambda i: (i, 0))
    bg = pl.BlockSpec((1, 2 * D), lambda i: (0, 0))
    return pl.pallas_call(
        body, name=name, grid=(S // tm,),
        in_specs=[row, row, row, row, pl.BlockSpec((tm, D), lambda i: (i, 1)), bg],
        out_specs=[row, row, wide, bg],
        out_shape=[jax.ShapeDtypeStruct((S, D), BF16), jax.ShapeDtypeStruct((S, D), BF16),
                   jax.ShapeDtypeStruct((S, 2 * D), BF16), jax.ShapeDtypeStruct((1, 2 * D), F32)],
        compiler_params=_cp(("arbitrary",)),
    )(dm, a, b, gl, gl, b_gates)


def _adam_math(w, g, m, v):
    m2 = ADAM_B1 * m + (1.0 - ADAM_B1) * g
    v2 = ADAM_B2 * v + (1.0 - ADAM_B2) * (g * g)
    m_hat = m2 / (1.0 - ADAM_B1 ** ADAM_STEP)
    v_hat = v2 / (1.0 - ADAM_B2 ** ADAM_STEP)
    delta = -ADAM_LR * (m_hat / (jnp.sqrt(v_hat) + ADAM_EPS) + ADAM_WD * w)
    return delta, m2, v2


def _pick_rows(rows, cols, unit=16, budget=2 * MIB):
    best = unit
    for t in range(unit, rows + 1, unit):
        if rows % t == 0 and t * cols * 4 <= budget:
            best = t
    assert rows % best == 0
    return best


def _adamw(w, g, m, v, name):
    r, c = w.shape
    tr = _pick_rows(r, c, unit=8)

    def body(w_ref, g_ref, m_ref, v_ref, go_ref, d_ref, mo_ref, vo_ref):
        gv = g_ref[...]
        delta, m2, v2 = _adam_math(w_ref[...], gv, m_ref[...], v_ref[...])
        go_ref[...] = gv
        d_ref[...] = delta
        mo_ref[...] = m2
        vo_ref[...] = v2

    blk = pl.BlockSpec((tr, c), lambda i: (i, 0))
    shp = jax.ShapeDtypeStruct((r, c), F32)
    return pl.pallas_call(
        body, name=name, grid=(r // tr,), in_specs=[blk] * 4, out_specs=[blk] * 4, out_shape=[shp] * 4,
        compiler_params=_cp(("parallel",)),
    )(*[pltpu.with_memory_space_constraint(t, pltpu.HBM) for t in (w, g, m, v)])


def _small_sum_adamw(parts, own, pos, w, m, v, name):
    tr = 48

    def body(pos_ref, p_ref, own_ref, w_ref, m_ref, v_ref, g_ref, d_ref, mo_ref, vo_ref):
        me = 2 * pos_ref[1] + pos_ref[0]
        gv = None
        for k in range(8):
            term = jnp.where(me == k, own_ref[...], p_ref[k])
            gv = term if gv is None else gv + term
        delta, m2, v2 = _adam_math(w_ref[...], gv, m_ref[...], v_ref[...])
        g_ref[...] = gv
        d_ref[...] = delta
        mo_ref[...] = m2
        vo_ref[...] = v2

    blk = pl.BlockSpec((tr, D), lambda i, p: (i, 0))
    shp = jax.ShapeDtypeStruct((SMALL_ROWS, D), F32)
    return pl.pallas_call(
        body, name=name,
        grid_spec=pltpu.PrefetchScalarGridSpec(
            num_scalar_prefetch=1, grid=(SMALL_ROWS // tr,),
            in_specs=[pl.BlockSpec((8, tr, D), lambda i, p: (0, i, 0)), blk, blk, blk, blk],
            out_specs=[blk] * 4),
        out_shape=[shp] * 4,
        compiler_params=_cp(("parallel",)),
    )(pos, parts, own, w, m, v)


ANY = pl.BlockSpec(memory_space=pl.ANY)


def _in_hbm(arrays):
    return [pltpu.with_memory_space_constraint(a, pltpu.HBM) for a in arrays]


def _mesh_pos():
    x, y, c = lax.axis_index("x"), lax.axis_index("y"), lax.axis_index("c")
    chips = [(1 - x, y), (x, 1 - y), (1 - x, 1 - y)]
    return x, y, c, chips


def _place_shard(w, kind, pos, name):
    r, c = w.shape
    tr = _pick_rows(r, c)

    def body(pos_ref, w_ref, o_ref):
        o_ref[...] = w_ref[...].astype(BF16)

    if kind == "stack":
        o_spec = pl.BlockSpec((None, tr, c), lambda i, p: (p[1], i, 0))
        shape = (NSH, r, c)
    else:
        o_spec = pl.BlockSpec((tr, c), lambda i, p: (i, p[1]))
        shape = (r, NSH * c)
    return pl.pallas_call(
        body, name=name,
        grid_spec=pltpu.PrefetchScalarGridSpec(
            num_scalar_prefetch=1, grid=(r // tr,),
            in_specs=[pl.BlockSpec((tr, c), lambda i, p: (i, 0))], out_specs=o_spec),
        out_shape=pltpu.HBM(shape, BF16),
        compiler_params=_cp(("parallel",)),
    )(pos, pltpu.with_memory_space_constraint(w, pltpu.HBM))


SEM = pl.BlockSpec(memory_space=pltpu.SEMAPHORE)
SPLIT_COPY = pltpu.CompilerParams(has_side_effects=pltpu.SideEffectType.DATAFLOW_SIDE_EFFECTING)


def _shard_window(ref, kind, j, h, dims):
    r, c = dims
    rows = pl.ds(pl.multiple_of(h * (r // 2), 16), r // 2)
    if kind == "stack":
        return ref.at[j, rows, :]
    return ref.at[rows, pl.ds(pl.multiple_of(j * c, 128), c)]


def _ici_copy(ref, kind, dims, j, c, sems, idx, to):
    win = _shard_window(ref, kind, j, c, dims)
    return pltpu.make_async_remote_copy(src_ref=win, dst_ref=win, send_sem=sems[0].at[idx], recv_sem=sems[1].at[idx],
                                        device_id=to, device_id_type=MESH_T)


def _both_copy(ref, kind, dims, a, k, chip, half, tc, sc, sems):
    win = _shard_window(ref, kind, half[0], half[1], dims)
    return pltpu.make_async_remote_copy(src_ref=win, dst_ref=win, send_sem=sems[0].at[6 * a + 2 * k + tc],
                                        recv_sem=sems[1].at[6 * a + 2 * k + sc],
                                        device_id=(chip[0], chip[1], tc), device_id_type=MESH_T)


def _gather_start(fulls, kinds, dims, after, name, both=False):
    n, na = len(fulls), len(after)
    per = 6 if both else 3

    def body(*refs):
        outs = refs[n + na:2 * n + na]
        send_sems, recv_sems, token = refs[2 * n + na:]
        x, y, c, chips = _mesh_pos()
        for a in range(n):
            for k, chip in enumerate(chips):
                if both:
                    for tc in range(2):
                        _both_copy(outs[a], kinds[a], dims[a], a, k, chip, (2 * x + y, c), tc, c,
                                   (send_sems, recv_sems)).start()
                else:
                    _ici_copy(outs[a], kinds[a], dims[a], 2 * x + y, c, (send_sems, recv_sems), 3 * a + k,
                              (chip[0], chip[1], c)).start()
        token[...] = jnp.zeros_like(token)

    res = pl.pallas_call(
        body, name=name, in_specs=[ANY] * (n + na),
        out_specs=[ANY] * n + [SEM, SEM, pl.BlockSpec(memory_space=pltpu.VMEM)],
        out_shape=[pltpu.HBM(f.shape, BF16) for f in fulls]
        + [pltpu.SemaphoreType.DMA((per * n,)), pltpu.SemaphoreType.DMA((per * n,)),
           jax.ShapeDtypeStruct((8, 128), F32)],
        input_output_aliases={i: i for i in range(n)},
        compiler_params=SPLIT_COPY,
    )(*_in_hbm(fulls), *after)
    return res[:n], res[n], res[n + 1], res[n + 2]


def _gather_wait(fulls, send_sems, recv_sems, kinds, dims, after, name, both=False):
    n, na = len(fulls), len(after)

    def body(*refs):
        ssem, rsem = refs[n], refs[n + 1]
        outs = refs[n + 2 + na:]
        x, y, c, chips = _mesh_pos()
        for a in range(n):
            for k, chip in enumerate(chips):
                if both:
                    for oc in range(2):
                        _both_copy(outs[a], kinds[a], dims[a], a, k, chip, (2 * x + y, c), oc, c,
                                   (ssem, rsem)).wait_send()
                        _both_copy(outs[a], kinds[a], dims[a], a, k, chip, (2 * chip[0] + chip[1], oc), c, oc,
                                   (ssem, rsem)).wait_recv()
                    continue
                to = (chip[0], chip[1], c)
                _ici_copy(outs[a], kinds[a], dims[a], 2 * x + y, c, (ssem, rsem), 3 * a + k, to).wait_send()
                _ici_copy(outs[a], kinds[a], dims[a], 2 * chip[0] + chip[1], c, (ssem, rsem), 3 * a + k, to).wait_recv()

    return pl.pallas_call(
        body, name=name, in_specs=[ANY] * n + [SEM, SEM] + [ANY] * na, out_specs=[ANY] * n,
        out_shape=[pltpu.HBM(f.shape, BF16) for f in fulls],
        input_output_aliases={i: i for i in range(n)},
        compiler_params=SPLIT_COPY,
    )(*_in_hbm(fulls), send_sems, recv_sems, *after)


def _gather_forward(fulls, kinds, dims, name):
    n = len(fulls)

    def body(*refs):
        outs = refs[n:2 * n]
        sems = refs[2 * n:]
        x, y, c, chips = _mesh_pos()
        sib = (x, y, 1 - c)
        cps = []
        for a in range(n):
            for k, chip in enumerate(chips):
                cp = _ici_copy(outs[a], kinds[a], dims[a], 2 * chip[0] + chip[1], c, sems, 3 * a + k, sib)
                cp.start()
                cps.append(cp)
        for a in range(n):
            for k, chip in enumerate(chips):
                _ici_copy(outs[a], kinds[a], dims[a], 2 * chip[0] + chip[1], 1 - c, sems, 3 * a + k, sib).wait_recv()
        for cp in cps:
            cp.wait_send()

    return pl.pallas_call(
        body, name=name, in_specs=[ANY] * n, out_specs=[ANY] * n,
        out_shape=[pltpu.HBM(f.shape, BF16) for f in fulls],
        input_output_aliases={i: i for i in range(n)},
        scratch_shapes=[pltpu.SemaphoreType.DMA((3 * n,)), pltpu.SemaphoreType.DMA((3 * n,))],
    )(*_in_hbm(fulls))


def _pair_copy(src, land, a, x, y, c, sems):
    return pltpu.make_async_remote_copy(
        src_ref=src.at[1 - c], dst_ref=land, send_sem=sems[0].at[a], recv_sem=sems[1].at[a],
        device_id=(x, y, 1 - c), device_id_type=MESH_T)


def _pair_start(grads, lands, name):
    n = len(grads)

    def body(*refs):
        srcs, dsts = refs[2 * n:3 * n], refs[3 * n:4 * n]
        send_sems, recv_sems, token = refs[4 * n:]
        x, y, c, _ = _mesh_pos()
        for a in range(n):
            _pair_copy(srcs[a], dsts[a], a, x, y, c, (send_sems, recv_sems)).start()
        token[...] = jnp.zeros_like(token)

    res = pl.pallas_call(
        body, name=name, in_specs=[ANY] * (2 * n),
        out_specs=[ANY] * (2 * n) + [SEM, SEM, pl.BlockSpec(memory_space=pltpu.VMEM)],
        out_shape=[pltpu.HBM(g.shape, F32) for g in grads]
        + [pltpu.HBM(l.shape, F32) for l in lands]
        + [pltpu.SemaphoreType.DMA((n,)), pltpu.SemaphoreType.DMA((n,)), jax.ShapeDtypeStruct((8, 128), F32)],
        input_output_aliases={i: i for i in range(2 * n)},
        compiler_params=SPLIT_COPY,
    )(*_in_hbm(grads), *_in_hbm(lands))
    return res[:n], res[n:2 * n], res[2 * n], res[2 * n + 1], res[2 * n + 2]


def _pair_wait(grads, lands, send_sems, recv_sems, after, name):
    n, na = len(grads), len(after)

    def body(*refs):
        ssem, rsem = refs[2 * n], refs[2 * n + 1]
        outs = refs[2 * n + 2 + na:]
        x, y, c, _ = _mesh_pos()
        for a in range(n):
            cp = _pair_copy(outs[a], outs[n + a], a, x, y, c, (ssem, rsem))
            cp.wait_send()
            cp.wait_recv()

    res = pl.pallas_call(
        body, name=name, in_specs=[ANY] * (2 * n) + [SEM, SEM] + [ANY] * na, out_specs=[ANY] * (2 * n),
        out_shape=[pltpu.HBM(g.shape, F32) for g in grads]
        + [pltpu.HBM(l.shape, F32) for l in lands],
        input_output_aliases={i: i for i in range(2 * n)},
        compiler_params=SPLIT_COPY,
    )(*_in_hbm(grads), *_in_hbm(lands), send_sems, recv_sems, *after)
    return res[:n], res[n:]


def _pair_sum(g, recv, pos, name):
    _, _, rh, c = g.shape
    tr = _pick_rows(rh, c)

    def body(pos_ref, g_ref, r_ref, o_ref):
        o_ref[...] = (g_ref[...] + r_ref[...]).astype(BF16)

    return pl.pallas_call(
        body, name=name,
        grid_spec=pltpu.PrefetchScalarGridSpec(
            num_scalar_prefetch=1, grid=(3, rh // tr),
            in_specs=[pl.BlockSpec((None, None, tr, c), lambda k, r, p: (p[0], p[2 + k], r, 0)),
                      pl.BlockSpec((None, tr, c), lambda k, r, p: (p[2 + k], r, 0))],
            out_specs=pl.BlockSpec((None, tr, c), lambda k, r, p: (k, r, 0))),
        out_shape=pltpu.HBM((3, rh, c), BF16),
        compiler_params=_cp(("parallel", "parallel")),
    )(pos, *_in_hbm([g, recv]))


def _pair_sum_group(gs, recvs, pos, name):
    n = len(gs)
    _, _, rh, c = gs[0].shape

    def body(pos_ref, *refs):
        a = pl.program_id(0)
        for t in range(n):
            @pl.when(a == t)
            def _(t=t):
                refs[2 * n + t][...] = (refs[t][...] + refs[n + t][...]).astype(BF16)

    def slot(t, a, k):
        return jnp.where(a == t, k, jnp.where(a < t, 0, 2))

    g_specs = [pl.BlockSpec((None, None, rh, c), lambda a, k, p, t=t: (p[0], p[2 + slot(t, a, k)], 0, 0))
               for t in range(n)]
    r_specs = [pl.BlockSpec((None, rh, c), lambda a, k, p, t=t: (p[2 + slot(t, a, k)], 0, 0)) for t in range(n)]
    o_specs = [pl.BlockSpec((None, rh, c), lambda a, k, p, t=t: (slot(t, a, k), 0, 0)) for t in range(n)]
    return pl.pallas_call(
        body, name=name,
        grid_spec=pltpu.PrefetchScalarGridSpec(num_scalar_prefetch=1, grid=(n, 3), in_specs=g_specs + r_specs,
                                               out_specs=o_specs),
        out_shape=[pltpu.HBM((3, rh, c), BF16)] * n,
        compiler_params=_cp(("arbitrary", "arbitrary")),
    )(pos, *_in_hbm(list(gs) + list(recvs)))


def _chip_copy(src, land, a, k, chip, c, sems):
    return pltpu.make_async_remote_copy(
        src_ref=src.at[k], dst_ref=land.at[k], send_sem=sems[0].at[3 * a + k],
        recv_sem=sems[1].at[3 * a + k], device_id=(chip[0], chip[1], c), device_id_type=MESH_T)


def _chip_start(psums, lands, name):
    n = len(psums)

    def body(*refs):
        srcs, dsts = refs[2 * n:3 * n], refs[3 * n:4 * n]
        send_sems, recv_sems, token = refs[4 * n:]
        x, y, c, chips = _mesh_pos()
        for a in range(n):
            for k, chip in enumerate(chips):
                _chip_copy(srcs[a], dsts[a], a, k, chip, c, (send_sems, recv_sems)).start()
        token[...] = jnp.zeros_like(token)

    res = pl.pallas_call(
        body, name=name, in_specs=[ANY] * (2 * n),
        out_specs=[ANY] * (2 * n) + [SEM, SEM, pl.BlockSpec(memory_space=pltpu.VMEM)],
        out_shape=[pltpu.HBM(p.shape, BF16) for p in psums]
        + [pltpu.HBM(l.shape, BF16) for l in lands]
        + [pltpu.SemaphoreType.DMA((3 * n,)), pltpu.SemaphoreType.DMA((3 * n,)), jax.ShapeDtypeStruct((8, 128), F32)],
        input_output_aliases={i: i for i in range(2 * n)},
        compiler_params=SPLIT_COPY,
    )(*_in_hbm(psums), *_in_hbm(lands))
    return res[:n], res[n:2 * n], res[2 * n], res[2 * n + 1], res[2 * n + 2]


def _chip_wait(psums, lands, send_sems, recv_sems, after, name):
    n, na = len(psums), len(after)

    def body(*refs):
        ssem, rsem = refs[2 * n], refs[2 * n + 1]
        outs = refs[2 * n + 2 + na:]
        srcs, dsts = outs[:n], outs[n:]
        x, y, c, chips = _mesh_pos()
        for a in range(n):
            for k, chip in enumerate(chips):
                cp = _chip_copy(srcs[a], dsts[a], a, k, chip, c, (ssem, rsem))
                cp.wait_send()
                cp.wait_recv()

    res = pl.pallas_call(
        body, name=name, in_specs=[ANY] * (2 * n) + [SEM, SEM] + [ANY] * na, out_specs=[ANY] * (2 * n),
        out_shape=[pltpu.HBM(p.shape, BF16) for p in psums]
        + [pltpu.HBM(l.shape, BF16) for l in lands],
        input_output_aliases={i: i for i in range(2 * n)},
        compiler_params=SPLIT_COPY,
    )(*_in_hbm(psums), *_in_hbm(lands), send_sems, recv_sems, *after)
    return res[n:]


def _owner_sum(g, recv_a, recv_b, pos, name):
    _, _, rh, c = g.shape
    tr = _pick_rows(rh // 2, c)

    def body(pos_ref, g_ref, ra_ref, rb_ref, o_ref):
        acc = g_ref[...] + ra_ref[...]
        for k in range(3):
            acc = acc + rb_ref[k].astype(F32)
        o_ref[...] = acc

    return pl.pallas_call(
        body, name=name,
        grid_spec=pltpu.PrefetchScalarGridSpec(
            num_scalar_prefetch=1, grid=(rh // tr,),
            in_specs=[pl.BlockSpec((None, None, tr, c), lambda r, p: (p[0], p[1], r, 0)),
                      pl.BlockSpec((None, tr, c), lambda r, p: (p[1], r, 0)),
                      pl.BlockSpec((3, tr, c), lambda r, p: (0, r, 0))],
            out_specs=pl.BlockSpec((None, tr, c), lambda r, p: (p[0], r, 0))),
        out_shape=pltpu.HBM((2, rh, c), F32),
        compiler_params=_cp(("parallel",)),
    )(pos, *_in_hbm([g, recv_a, recv_b]))


def _sibling_allgather(halves, name):
    n = len(halves)

    def body(*refs):
        outs = refs[n:2 * n]
        send_sems, recv_sems = refs[2 * n:]
        x, y, c, _ = _mesh_pos()
        cps = []
        for a in range(n):
            cp = pltpu.make_async_remote_copy(
                src_ref=outs[a].at[c], dst_ref=outs[a].at[c], send_sem=send_sems.at[a], recv_sem=recv_sems.at[a],
                device_id=(x, y, 1 - c), device_id_type=MESH_T)
            cp.start()
            cps.append(cp)
        for a in range(n):
            cps[a].wait_send()
            pltpu.make_async_remote_copy(
                src_ref=outs[a].at[1 - c], dst_ref=outs[a].at[1 - c], send_sem=send_sems.at[a],
                recv_sem=recv_sems.at[a], device_id=(x, y, 1 - c), device_id_type=MESH_T).wait_recv()

    return pl.pallas_call(
        body, name=name, in_specs=[ANY] * n, out_specs=[ANY] * n,
        out_shape=[pltpu.HBM(h.shape, F32) for h in halves],
        input_output_aliases={i: i for i in range(n)},
        scratch_shapes=[pltpu.SemaphoreType.DMA((n,)), pltpu.SemaphoreType.DMA((n,))],
    )(*_in_hbm(halves))


def _peers(x, y, c):
    rel = [(0, 0, 1), (0, 1, 0), (0, 1, 1), (1, 0, 0), (1, 0, 1), (1, 1, 0), (1, 1, 1)]
    return [((1 - x) if dx else x, (1 - y) if dy else y, (1 - c) if dc else c) for dx, dy, dc in rel]


def _small_copy(src, land, k, peer, slot, sems):
    return pltpu.make_async_remote_copy(src_ref=src, dst_ref=land.at[slot], send_sem=sems[0].at[k],
                                        recv_sem=sems[1].at[k], device_id=peer, device_id_type=MESH_T)


def _small_start(part, land, name):
    def body(p_in, l_in, p_ref, l_ref, send_sems, recv_sems, token):
        x, y, c, _ = _mesh_pos()
        for k, peer in enumerate(_peers(x, y, c)):
            _small_copy(p_ref, l_ref, k, peer, 4 * x + 2 * y + c, (send_sems, recv_sems)).start()
        token[...] = jnp.zeros_like(token)

    return pl.pallas_call(
        body, name=name, in_specs=[ANY, ANY],
        out_specs=[ANY, ANY, SEM, SEM, pl.BlockSpec(memory_space=pltpu.VMEM)],
        out_shape=[pltpu.HBM(part.shape, F32), pltpu.HBM(land.shape, F32), pltpu.SemaphoreType.DMA((7,)),
                   pltpu.SemaphoreType.DMA((7,)), jax.ShapeDtypeStruct((8, 128), F32)],
        input_output_aliases={0: 0, 1: 1},
        compiler_params=SPLIT_COPY,
    )(*_in_hbm([part, land]))


def _small_wait(part, land, send_sems, recv_sems, after, name):
    na = len(after)

    def body(*refs):
        ssem, rsem = refs[2], refs[3]
        p_ref, l_ref = refs[4 + na:]
        x, y, c, _ = _mesh_pos()
        for k, peer in enumerate(_peers(x, y, c)):
            cp = _small_copy(p_ref, l_ref, k, peer, 4 * peer[0] + 2 * peer[1] + peer[2], (ssem, rsem))
            cp.wait_send()
            cp.wait_recv()

    return pl.pallas_call(
        body, name=name, in_specs=[ANY, ANY, SEM, SEM] + [ANY] * na, out_specs=[ANY, ANY],
        out_shape=[pltpu.HBM(part.shape, F32), pltpu.HBM(land.shape, F32)],
        input_output_aliases={0: 0, 1: 1},
        compiler_params=SPLIT_COPY,
    )(*_in_hbm([part, land]), send_sems, recv_sems, *after)


def _pack_small(ln1_g, ln1_b, gln_g, gln_b, ln2_g, ln2_b, ln3_g, ln3_b, b_gates, b_s, w_s):
    rows = [ln1_g, ln1_b, gln_g, gln_b, ln2_g, ln2_b, ln3_g, ln3_b]
    rows = [r.reshape(1, D) for r in rows] + [b_gates.reshape(2, D), b_s.reshape(1, D), jnp.zeros((5, D), F32),
                                             w_s.reshape(128, D)]
    return jnp.concatenate(rows, axis=0)


def _unpack_small(p):
    out = [p[i:i + 1] for i in range(8)]
    return out + [p[8:10].reshape(1, 2 * D), p[10:11].reshape(1, 8, BLK), p[16:144].reshape(1, 8, BLK, BLK)]


GROUPS = (("f1g", "f1u", "f1d"), ("w_in",), ("w_ab", "w_gb", "w_out"), ("f2g", "f2u", "f2d"))
LATE_GROUPS = (2, 3)


def _local_step(x, pos_f, target, P, weights_of, grads_ready, flush, small_ready):
    invf = ROPE_THETA ** (-jnp.arange(0, DH, 2, dtype=F32) / DH)
    invf = jnp.tile(invf, 4).reshape(1, 128)
    b_s_t = P["gmlp_b_s"].T

    W = dict(weights_of(0, []))
    h1b, xh1, rstd1, a1, b1, h1t = _ffn_fwd(x, W["f1g"], W["f1u"], W["f1d"], P["ln1_g"], P["ln1_b"], "ffn1_fwd",
                                                emit_t=True)
    W.update(weights_of(1, [h1b]))
    qkv_c = _proj_qkv_rope(h1b, W["w_in"], pos_f, invf, "proj_qkv_rope")
    z = _matmul(h1b, W["w_in"], "nn", "proj_z", n=2 * GW, b_col0=3 * ATT_W, tm=S, tn=512)
    gl = _matmul(h1b, W["w_in"], "nn", "proj_gates", n=2 * D, b_col0=3 * ATT_W + 2 * GW, tm=S, tn=512)
    og = [_attn_fwd(gi, qkv_c[gi], "attn_fwd_g%d" % gi) for gi in range(NG)]
    y_attn, y_attn_t, lse = _attn_combine([o for o, _ in og], [l for _, l in og], "attn_combine")
    y_gmlp, y_gmlp_t = _gmlp_fwd(z, P["gmlp_ln_g"], P["gmlp_ln_b"], P["gmlp_w_s"], b_s_t, "gmlp_fwd")
    W.update(weights_of(2, [y_gmlp]))
    br_a = _matmul(y_attn, W["w_ab"], "nn", "branch_attn", n=D, tm=1024, tn=D)
    br_b = _matmul(y_gmlp, W["w_gb"], "nn", "branch_gmlp", n=D, tm=1024, tn=D)
    merged, merged_t = _merge_fwd(br_a, br_b, gl, P["b_gates"], "merge_fwd")
    h2, h2b, xh2, rstd2 = _resid_ln(xh1, P["ln1_g"], P["ln1_b"], merged, W["w_out"], P["ln2_g"], P["ln2_b"],
                                    "mix_resid_ln2")
    W.update(weights_of(3, [h2b]))
    dr3, a2, b2, dg3, db3, loss = _ffn_fwd(h2, W["f2g"], W["f2u"], W["f2d"], P["ln3_g"], P["ln3_b"],
                                           "ffn2_fwd_loss", target=target)

    g_f2g, g_f2u, g_f2d, dh2 = _ffn_bwd(dr3, h2b, a2, b2, W["f2g"], W["f2u"], W["f2d"], "ffn2_bwd")
    tok = grads_ready(3, dict(f2g=g_f2g, f2u=g_f2u, f2d=g_f2d))
    dr2, dg2, db2 = _ln_bwd(dh2, xh2, rstd2, P["ln2_g"], "ln2_bwd", after=tok)
    g_wout = _wgrad(merged_t, dr2, 128, D, "dw_out", row_sharded=True)
    dmerged = _matmul(dr2, W["w_out"], "nt", "dmerged", n=D, tm=1024, tn=D)
    dab, dbb, dglb, dbg = _merge_bwd(dmerged, br_a, br_b, gl, P["b_gates"], "merge_bwd")
    tok = flush([dab])
    g_wab = _wgrad(y_attn_t, dab, GRP_W // 2, 256, "dw_attn_branch", row_sharded=False, after=tok)
    g_wgb = _wgrad(y_gmlp_t, dbb, 128, D, "dw_gmlp_branch", row_sharded=True)
    tok = grads_ready(2, dict(w_ab=g_wab, w_gb=g_wgb, w_out=g_wout))
    dy_attn = _matmul(dab, W["w_ab"], "nt", "dy_attn", n=GRP_W, tm=1024, tn=GRP_W, after=tok)
    dy_gmlp = _matmul(dbb, W["w_gb"], "nt", "dy_gmlp", n=GW, tm=1024, tn=GW)
    dzb, dws, dbs_t, dgln_g, dgln_b = _gmlp_bwd(z, dy_gmlp, P["gmlp_ln_g"], P["gmlp_ln_b"], P["gmlp_w_s"], b_s_t,
                                                 "gmlp_bwd")
    cls = _class_order([dy_attn, y_attn, lse], "attn_class_order")
    dqkv_c = []
    for gi in range(NG):
        dy_c, y_c, lse_c = [t[None] if gi == 0 else cls[2 * a + gi - 1] for a, t in enumerate((dy_attn, y_attn, lse))]
        dqkv_c.append(_attn_bwd(gi, qkv_c[gi], dy_c, y_c, lse_c, "attn_bwd_g%d" % gi))
    dproj = _rope_bwd(dqkv_c, dzb, dglb, pos_f, invf, "rope_bwd")
    tok = flush([dproj])
    g_win = _wgrad(h1t, dproj, D // 2, IN_SH, "dw_in", row_sharded=False, after=tok)
    tok = grads_ready(1, dict(w_in=g_win))
    dr1, dg1, db1 = _dh1_ln_bwd(dproj, W["w_in"], dr2, xh1, rstd1, P["ln1_g"], "dh1_ln1_bwd", after=tok)
    tok = flush([dr1])
    tok = tok + small_ready(_pack_small(dg1, db1, dgln_g, dgln_b, dg2, db2, dg3, db3, dbg, dbs_t.T, dws))
    g_f1g, g_f1u, g_f1d, dx = _ffn_bwd(dr1, x.astype(BF16), a1, b1, W["f1g"], W["f1u"], W["f1d"], "ffn1_bwd",
                                       after=tok)
    grads_ready(0, dict(f1g=g_f1g, f1u=g_f1u, f1d=g_f1d))
    flush([dx])
    return loss, dx


TRANSPOSED = ("f1g", "f1u", "f2g", "f2u")
KIND = dict(f1g="stack", f1u="stack", f1d="stack", w_in="col", w_ab="col", w_gb="stack", w_out="stack",
            f2g="stack", f2u="stack", f2d="stack")


def kernel(x, positions, ffn1_w_gate, ffn1_w_up, ffn1_w_down, ln1_g, ln1_b, w_in, b_gates, gmlp_ln_g, gmlp_ln_b, gmlp_w_s, gmlp_b_s, w_attn_branch, w_gmlp_branch, w_out, ln2_g, ln2_b, ffn2_w_gate, ffn2_w_up, ffn2_w_down, ln3_g, ln3_b, loss_target, m_ffn1_w_gate, m_ffn1_w_up, m_ffn1_w_down, m_ln1_g, m_ln1_b, m_w_in, m_b_gates, m_gmlp_ln_g, m_gmlp_ln_b, m_gmlp_w_s, m_gmlp_b_s, m_w_attn_branch, m_w_gmlp_branch, m_w_out, m_ln2_g, m_ln2_b, m_ffn2_w_gate, m_ffn2_w_up, m_ffn2_w_down, m_ln3_g, m_ln3_b, v_ffn1_w_gate, v_ffn1_w_up, v_ffn1_w_down, v_ln1_g, v_ln1_b, v_w_in, v_b_gates, v_gmlp_ln_g, v_gmlp_ln_b, v_gmlp_w_s, v_gmlp_b_s, v_w_attn_branch, v_w_gmlp_branch, v_w_out, v_ln2_g, v_ln2_b, v_ffn2_w_gate, v_ffn2_w_up, v_ffn2_w_down, v_ln3_g, v_ln3_b):
    cx, cy, cc = lax.axis_index("x"), lax.axis_index("y"), lax.axis_index("c")
    pos = jnp.stack([cc, 2 * cx + cy, 2 * (1 - cx) + cy, 2 * cx + 1 - cy, 2 * (1 - cx) + 1 - cy]).astype(jnp.int32)

    w_sh = dict(f1g=ffn1_w_gate, f1u=ffn1_w_up, f1d=ffn1_w_down, w_in=w_in, w_ab=w_attn_branch,
                w_gb=w_gmlp_branch, w_out=w_out, f2g=ffn2_w_gate, f2u=ffn2_w_up, f2d=ffn2_w_down)
    m_sh = dict(f1g=m_ffn1_w_gate, f1u=m_ffn1_w_up, f1d=m_ffn1_w_down, w_in=m_w_in, w_ab=m_w_attn_branch,
                w_gb=m_w_gmlp_branch, w_out=m_w_out, f2g=m_ffn2_w_gate, f2u=m_ffn2_w_up, f2d=m_ffn2_w_down)
    v_sh = dict(f1g=v_ffn1_w_gate, f1u=v_ffn1_w_up, f1d=v_ffn1_w_down, w_in=v_w_in, w_ab=v_w_attn_branch,
                w_gb=v_w_gmlp_branch, w_out=v_w_out, f2g=v_ffn2_w_gate, f2u=v_ffn2_w_up, f2d=v_ffn2_w_down)
    w_sh = {k: (v[0].T if k in TRANSPOSED else v[0]) for k, v in w_sh.items()}
    m_sh = {k: (v[0].T if k in TRANSPOSED else v[0]) for k, v in m_sh.items()}
    v_sh = {k: (v[0].T if k in TRANSPOSED else v[0]) for k, v in v_sh.items()}

    started, tokens = [], []
    for gi, names in enumerate(GROUPS):
        placed = [_place_shard(w_sh[k], KIND[k], pos, "place_" + k) for k in names]
        fulls, ssem, rsem, token = _gather_start(placed, [KIND[k] for k in names], [w_sh[k].shape for k in names],
                                                 tokens[-1:], "gather_start_g%d" % gi, both=gi in LATE_GROUPS)
        started.append((fulls, ssem, rsem))
        tokens.append(token)

    def weights_of(gi, after):
        names = GROUPS[gi]
        kinds, dims = [KIND[k] for k in names], [w_sh[k].shape for k in names]
        fulls, ssem, rsem = started[gi]
        fulls = _gather_wait(fulls, ssem, rsem, kinds, dims, list(after) + (tokens if gi == 0 else []),
                             "gather_wait_g%d" % gi, both=gi in LATE_GROUPS)
        if gi not in LATE_GROUPS:
            fulls = _gather_forward(fulls, kinds, dims, "gather_forward_g%d" % gi)
        return {k: (f.reshape(D, D) if k in ("w_gb", "w_out") else f) for k, f in zip(names, fulls)}

    pending, inflight = [], {}

    def grads_ready(gi, gd):
        grads = [gd[k] for k in GROUPS[gi]]
        lands = [lax.empty(g.shape[1:], F32) for g in grads]
        grads, lands, ssem, rsem, token = _pair_start(grads, lands, "rs_pair_start_g%d" % gi)
        pending.append((gi, grads, lands, ssem, rsem))
        return [token]

    def flush(after):
        gi, grads, lands, ssem, rsem = pending.pop()
        names = GROUPS[gi]
        grads, recv_a = _pair_wait(grads, lands, ssem, rsem, after, "rs_pair_wait_g%d" % gi)
        if len(names) > 1 and len({g.shape for g in grads}) == 1:
            psums = _pair_sum_group(grads, recv_a, pos, "rs_pair_sum_g%d" % gi)
        else:
            psums = [_pair_sum(g, r, pos, "rs_pair_sum_" + k) for g, r, k in zip(grads, recv_a, names)]
        lands = [lax.empty((3,) + p.shape[1:], BF16) for p in psums]
        psums, lands, ssem, rsem, token = _chip_start(psums, lands, "rs_chip_start_g%d" % gi)
        inflight[gi] = (grads, recv_a, psums, lands, ssem, rsem, token)
        return [token]

    P = dict(ln1_g=ln1_g, ln1_b=ln1_b, ln2_g=ln2_g, ln2_b=ln2_b, ln3_g=ln3_g, ln3_b=ln3_b, b_gates=b_gates,
             gmlp_ln_g=gmlp_ln_g, gmlp_ln_b=gmlp_ln_b, gmlp_w_s=gmlp_w_s[0], gmlp_b_s=gmlp_b_s[0])
    pos_f = positions.reshape(S, 1).astype(F32)
    small_state = []

    def small_ready(packed):
        land = jnp.zeros((8, SMALL_ROWS, D), F32)
        packed, land, ssem, rsem, token = _small_start(packed, land, "small_start")
        small_state.append((packed, land, ssem, rsem))
        return [token]

    loss_part, dx = _local_step(x[0], pos_f, loss_target[0], P, weights_of, grads_ready, flush, small_ready)
    loss = lax.psum(loss_part[0, 0], ("x", "y", "c"))

    g_out, d_out, m_out, v_out = {}, {}, {}, {}

    def finish(gis, after, tag):
        names, halves = [], []
        for gi in gis:
            grads, recv_a, psums, lands, ssem, rsem, token = inflight[gi]
            recv_b = _chip_wait(psums, lands, ssem, rsem, after + [inflight[0][6]], "rs_chip_wait_g%d" % gi)
            halves += [_owner_sum(g, ra, rb, pos, "rs_owner_sum_" + k)
                       for g, ra, rb, k in zip(grads, recv_a, recv_b, GROUPS[gi])]
            names += GROUPS[gi]
            after = halves[-1:]
        reduced = _sibling_allgather(halves, "rs_sibling_allgather_" + tag)
        for k, gfull in zip(names, reduced):
            res = _adamw(w_sh[k], gfull.reshape(w_sh[k].shape), m_sh[k], v_sh[k], "adamw_" + k)
            after = [res[1]]
            if k in TRANSPOSED:
                res = [r.T for r in res]
            g_out[k], d_out[k], m_out[k], v_out[k] = [r[None] for r in res]
        return after

    after = finish((3, 2, 1), [], "g321")

    small, parts = _small_wait(*small_state[0], after, "small_wait")
    sp = (ln1_g, ln1_b, gmlp_ln_g, gmlp_ln_b, ln2_g, ln2_b, ln3_g, ln3_b, b_gates, gmlp_b_s, gmlp_w_s)
    sm = (m_ln1_g, m_ln1_b, m_gmlp_ln_g, m_gmlp_ln_b, m_ln2_g, m_ln2_b, m_ln3_g, m_ln3_b, m_b_gates, m_gmlp_b_s,
          m_gmlp_w_s)
    sv = (v_ln1_g, v_ln1_b, v_gmlp_ln_g, v_gmlp_ln_b, v_ln2_g, v_ln2_b, v_ln3_g, v_ln3_b, v_b_gates, v_gmlp_b_s,
          v_gmlp_w_s)
    sg, sd, smn, svn = _small_sum_adamw(parts, small, pos, _pack_small(*sp), _pack_small(*sm), _pack_small(*sv),
                                        "small_adamw")
    names = ("ln1_g", "ln1_b", "gmlp_ln_g", "gmlp_ln_b", "ln2_g", "ln2_b", "ln3_g", "ln3_b", "b_gates", "gmlp_b_s",
             "gmlp_w_s")
    for dst, packed in ((g_out, sg), (d_out, sd), (m_out, smn), (v_out, svn)):
        for nm, val in zip(names, _unpack_small(packed)):
            dst[nm] = val
    finish((0,), [sg], "g0")

    order = ("f1g", "f1u", "f1d", "ln1_g", "ln1_b", "w_in", "b_gates", "gmlp_ln_g", "gmlp_ln_b", "gmlp_w_s", "gmlp_b_s",
             "w_ab", "w_gb", "w_out", "ln2_g", "ln2_b", "f2g", "f2u", "f2d", "ln3_g", "ln3_b")
    outs = [loss, dx[None]]
    for dst in (g_out, d_out, m_out, v_out):
        outs += [dst[k] for k in order]
    return tuple(outs)
```

```python
import jax
import jax.numpy as jnp
from jax import lax
from jax.experimental import pallas as pl
from jax.experimental.pallas import tpu as pltpu

F32 = jnp.float32
BF16 = jnp.bfloat16

S = 2048
D = 1024
NSH = 4
FSH = 704
ATT_W = 1536
GRP_W = 512
NG = 3
NH = 8
DH = 64
BLK = 128
NBLK = S // BLK
GW = 1024
IN_W = 8704
IN_SH = IN_W // NSH
ALPHA = 2.0 ** 0.25
LN_EPS = 1e-5
ROPE_THETA = 10000.0
DILATIONS = (1, 4, 16)
ADAM_LR, ADAM_B1, ADAM_B2, ADAM_EPS, ADAM_WD, ADAM_STEP = 0.001, 0.9, 0.999, 1e-08, 0.01, 10
SMALL_ROWS = 144
EPI_ROWS = 256
MESH_T = pl.DeviceIdType.MESH
MIB = 1024 * 1024
NEG_INF = float("-inf")


def _cp(sem, vmem_mib=48):
    return pltpu.CompilerParams(dimension_semantics=sem, vmem_limit_bytes=vmem_mib * MIB)


def _ln_stats(r):
    mu = jnp.mean(r, axis=-1, keepdims=True)
    xc = r - mu
    var = jnp.mean(xc * xc, axis=-1, keepdims=True)
    rstd = lax.rsqrt(var + LN_EPS)
    return xc * rstd, rstd


def _ln_dx(dxh, xh, rstd):
    m1 = jnp.mean(dxh, axis=-1, keepdims=True)
    m2 = jnp.mean(dxh * xh, axis=-1, keepdims=True)
    return rstd * (dxh - m1 - xh * m2)


def _dot_nt(a, b):
    return lax.dot_general(a, b, (((1,), (1,)), ((), ())), preferred_element_type=F32)


def _dot_tn(a, b):
    return lax.dot_general(a, b, (((0,), (0,)), ((), ())), preferred_element_type=F32)


def _dot(a, b):
    return jnp.dot(a, b, preferred_element_type=F32)


def _ffn_fwd(xin, wgt, wut, wd, ln_g, ln_b, name, emit_t=False, target=None):
    with_loss = target is not None
    tm = 1024

    def body(x_ref, wg_ref, wu_ref, wd_ref, g_ref, b_ref, *rest):
        if with_loss:
            t_ref, dr_ref, a_ref, bb_ref, dg_ref, db_ref, loss_ref, acc_ref = rest
        elif emit_t:
            hb_ref, xh_ref, rstd_ref, a_ref, bb_ref, ht_ref, acc_ref = rest
        else:
            hb_ref, xh_ref, rstd_ref, a_ref, bb_ref, acc_ref = rest
        i = pl.program_id(0)
        j = pl.program_id(1)
        xb = x_ref[...].astype(BF16)
        a = _dot_nt(xb, wg_ref[...])
        b = _dot_nt(xb, wu_ref[...])
        a_ref[...] = a.astype(BF16)
        bb_ref[...] = b.astype(BF16)
        s = (a * jax.nn.sigmoid(a)) * b
        f = _dot(s.astype(BF16), wd_ref[...])

        @pl.when(j == 0)
        def _():
            acc_ref[...] = f

        @pl.when(j > 0)
        def _():
            acc_ref[...] += f

        if with_loss:
            @pl.when(jnp.logical_and(j == NSH - 1, i == 0))
            def _():
                dg_ref[...] = jnp.zeros_like(dg_ref)
                db_ref[...] = jnp.zeros_like(db_ref)
                loss_ref[...] = jnp.zeros_like(loss_ref)

        @pl.when(j == NSH - 1)
        def _():
            for c0 in range(0, tm, EPI_ROWS):
                rows = slice(c0, c0 + EPI_ROWS)
                r = ALPHA * x_ref[rows, :] + 0.5 * acc_ref[rows, :]
                xh, rstd = _ln_stats(r)
                h = xh * g_ref[...] + b_ref[...]
                if with_loss:
                    err = h - t_ref[rows, :]
                    dy = err * (1.0 / D)
                    dr_ref[rows, :] = _ln_dx(dy * g_ref[...], xh, rstd)
                    dg_ref[...] += jnp.sum(dy * xh, axis=0, keepdims=True)
                    db_ref[...] += jnp.sum(dy, axis=0, keepdims=True)
                    part = 0.5 * jnp.sum(jnp.mean(err * err, axis=-1, keepdims=True), axis=0, keepdims=True)
                    loss_ref[...] += jnp.broadcast_to(part, (8, 128))
                else:
                    hb_ref[rows, :] = h.astype(BF16)
                    xh_ref[rows, :] = xh
                    rstd_ref[rows, :] = rstd
                    if emit_t:
                        ht_ref[:, rows] = h.T.astype(BF16)

    row = pl.BlockSpec((tm, D), lambda i, j: (i, 0))
    vec = pl.BlockSpec((1, D), lambda i, j: (0, 0))
    wsp = pl.BlockSpec((None, FSH, D), lambda i, j: (j, 0, 0))
    ab = pl.BlockSpec((None, tm, FSH), lambda i, j: (j, i, 0))
    ab_shape = jax.ShapeDtypeStruct((NSH, S, FSH), BF16)
    in_specs, args = [row, wsp, wsp, wsp, vec, vec], (xin, wgt, wut, wd, ln_g, ln_b)
    if with_loss:
        in_specs, args = in_specs + [row], args + (target,)
        out_specs = [row, ab, ab, vec, vec, pl.BlockSpec((8, 128), lambda i, j: (0, 0))]
        out_shape = [jax.ShapeDtypeStruct((S, D), F32), ab_shape, ab_shape, jax.ShapeDtypeStruct((1, D), F32),
                     jax.ShapeDtypeStruct((1, D), F32), jax.ShapeDtypeStruct((8, 128), F32)]
    else:
        out_specs = [row, row, pl.BlockSpec((tm, 1), lambda i, j: (i, 0)), ab, ab]
        out_shape = [jax.ShapeDtypeStruct((S, D), BF16), jax.ShapeDtypeStruct((S, D), F32),
                     jax.ShapeDtypeStruct((S, 1), F32), ab_shape, ab_shape]
        if emit_t:
            out_specs.append(pl.BlockSpec((D, tm), lambda i, j: (0, i)))
            out_shape.append(jax.ShapeDtypeStruct((D, S), BF16))
    return pl.pallas_call(
        body, name=name, grid=(S // tm, NSH), in_specs=in_specs, out_specs=out_specs, out_shape=out_shape,
        scratch_shapes=[pltpu.VMEM((tm, D), F32)],
        compiler_params=_cp(("arbitrary" if with_loss else "parallel", "arbitrary"), vmem_mib=56),
    )(*args)


def _ffn_bwd(dr, xin_b, a, b, wgt, wut, wd, name, after=()):
    tm = 512
    ni = S // tm
    hr = FSH // 2

    def body(dr_ref, a_ref, b_ref, wg_ref, wu_ref, wd_ref, x_hbm, *rest):
        dwg_hbm, dwu_hbm, dwd_hbm, dx_hbm, dx_acc, da_all, db_all, s_all, df_all, x_all, res_buf, sems = rest[len(after):]
        j = pl.program_id(0)
        i = pl.program_id(1)
        rows = pl.ds(pl.multiple_of(i * tm, tm), tm)

        @pl.when(jnp.logical_and(j == 0, i == 0))
        def _():
            cp = pltpu.make_async_copy(x_hbm, x_all, sems.at[0])
            cp.start()
            cp.wait()

        drv = dr_ref[...]
        df = (0.5 * drv).astype(BF16)

        @pl.when(j == 0)
        def _():
            df_all[rows, :] = df

        ds = jnp.concatenate([_dot_nt(df, wd_ref[0:384, :]), _dot_nt(df, wd_ref[384:FSH, :])], axis=1)
        av = a_ref[...].astype(F32)
        bv = b_ref[...].astype(F32)
        sig = jax.nn.sigmoid(av)
        sl = av * sig
        da = (ds * bv * (sig * (1.0 + av * (1.0 - sig)))).astype(BF16)
        db = (ds * sl).astype(BF16)
        da_all[rows, :] = da
        db_all[rows, :] = db
        s_all[rows, :] = (sl * bv).astype(BF16)
        dx = _dot(da, wg_ref[...]) + _dot(db, wu_ref[...])

        @pl.when(j == 0)
        def _():
            dx_acc[rows, :] = ALPHA * drv + dx

        @pl.when(j > 0)
        def _():
            dx_acc[rows, :] += dx

        @pl.when(i == ni - 1)
        def _():
            copies = []
            for n, (lhs, rhs, out) in enumerate(((da_all, x_all, dwg_hbm), (db_all, x_all, dwu_hbm),
                                                 (s_all, df_all, dwd_hbm))):
                slot = n % 2
                if n >= 2:
                    for cp in copies[2 * (n - 2): 2 * (n - 2) + 2]:
                        cp.wait()
                res_buf[slot] = _dot_tn(lhs[...], rhs[...])
                for h in range(2):
                    cp = pltpu.make_async_copy(res_buf.at[slot, pl.ds(h * hr, hr), :], out.at[h, j],
                                               sems.at[1 + 2 * slot + h])
                    cp.start()
                    copies.append(cp)
            for cp in copies[2:]:
                cp.wait()

        @pl.when(jnp.logical_and(j == NSH - 1, i == ni - 1))
        def _():
            cp = pltpu.make_async_copy(dx_acc, dx_hbm, sems.at[0])
            cp.start()
            cp.wait()

    row = pl.BlockSpec((tm, D), lambda j, i: (i, 0))
    wsp = pl.BlockSpec((None, FSH, D), lambda j, i: (j, 0, 0))
    ab = pl.BlockSpec((None, tm, FSH), lambda j, i: (j, i, 0))
    dwshape = jax.ShapeDtypeStruct((2, NSH, hr, D), F32)
    return pl.pallas_call(
        body, name=name, grid=(NSH, ni),
        in_specs=[row, ab, ab, wsp, wsp, wsp, ANY] + [ANY] * len(after),
        out_specs=[ANY, ANY, ANY, ANY],
        out_shape=[dwshape, dwshape, dwshape, jax.ShapeDtypeStruct((S, D), F32)],
        scratch_shapes=[pltpu.VMEM((S, D), F32), pltpu.VMEM((S, FSH), BF16), pltpu.VMEM((S, FSH), BF16),
                        pltpu.VMEM((S, FSH), BF16), pltpu.VMEM((S, D), BF16), pltpu.VMEM((S, D), BF16),
                        pltpu.VMEM((2, FSH, D), F32), pltpu.SemaphoreType.DMA((5,))],
        compiler_params=_cp(("arbitrary", "arbitrary"), vmem_mib=58),
    )(dr, a, b, wgt, wut, wd, xin_b, *after)


def _matmul(a, b, mode, name, *, n, tm, tn, b_col0=0, after=()):
    m, k = a.shape
    assert m % tm == 0 and n % tn == 0 and b_col0 % tn == 0
    off = b_col0 // tn
    na = len(after)

    def body(*refs):
        a_ref, b_ref, o_ref = refs[na:]
        av = a_ref[...].astype(BF16)
        o_ref[...] = _dot(av, b_ref[...]) if mode == "nn" else _dot_nt(av, b_ref[...])

    if mode == "nn":
        b_spec = pl.BlockSpec((k, tn), lambda i, j: (0, j + off))
    else:
        b_spec = pl.BlockSpec((tn, k), lambda i, j: (j, 0))
    return pl.pallas_call(
        body, name=name, grid=(m // tm, n // tn),
        in_specs=[pl.BlockSpec(memory_space=pl.ANY)] * na + [pl.BlockSpec((tm, k), lambda i, j: (i, 0)), b_spec],
        out_specs=pl.BlockSpec((tm, tn), lambda i, j: (i, j)),
        out_shape=jax.ShapeDtypeStruct((m, n), F32),
        compiler_params=_cp(("parallel", "parallel")),
    )(*after, a, b)


def _wgrad(xt, y, rh, c, name, row_sharded, after=()):
    na = len(after)
    if row_sharded:
        tc = 512

        def body(x_ref, y_ref, *rest):
            o_ref = rest[na]
            res = _dot(x_ref[...], y_ref[...].astype(BF16))
            for j in range(NSH):
                for h in range(2):
                    o_ref[h, j] = res[(2 * j + h) * rh:(2 * j + h + 1) * rh, :]

        grid = (c // tc,)
        in_specs = [pl.BlockSpec((2 * NSH * rh, S), lambda g: (0, 0)), pl.BlockSpec((S, tc), lambda g: (0, g))]
        out_specs = pl.BlockSpec((2, NSH, rh, tc), lambda g: (0, 0, 0, g))
        sem = ("parallel",)
    else:
        def body(x_ref, y_ref, *rest):
            rest[na][...] = _dot(x_ref[...], y_ref[...].astype(BF16))

        grid = (NSH, 2)
        in_specs = [pl.BlockSpec((rh, S), lambda j, h: (h, 0)), pl.BlockSpec((S, c), lambda j, h: (0, j))]
        out_specs = pl.BlockSpec((None, None, rh, c), lambda j, h: (h, j, 0, 0))
        sem = ("parallel", "parallel")
    return pl.pallas_call(
        body, name=name, grid=grid, in_specs=in_specs + [pl.BlockSpec(memory_space=pl.ANY)] * na, out_specs=out_specs,
        out_shape=jax.ShapeDtypeStruct((2, NSH, rh, c), F32),
        compiler_params=_cp(sem, vmem_mib=56),
    )(xt, y, *after)


def _resid_ln(res_xh, res_g, res_b, a, w, ln_g, ln_b, name):
    tm = 512

    def body(rx_ref, rg_ref, rb_ref, a_ref, w_ref, g_ref, b_ref, h_ref, hb_ref, xh_ref, rstd_ref):
        r = ALPHA * (rx_ref[...] * rg_ref[...] + rb_ref[...]) + _dot(a_ref[...], w_ref[...])
        xh, rstd = _ln_stats(r)
        h = xh * g_ref[...] + b_ref[...]
        h_ref[...] = h
        hb_ref[...] = h.astype(BF16)
        xh_ref[...] = xh
        rstd_ref[...] = rstd

    row = pl.BlockSpec((tm, D), lambda i: (i, 0))
    vec = pl.BlockSpec((1, D), lambda i: (0, 0))
    return pl.pallas_call(
        body, name=name, grid=(S // tm,),
        in_specs=[row, vec, vec, row, pl.BlockSpec((D, D), lambda i: (0, 0)), vec, vec],
        out_specs=[row, row, row, pl.BlockSpec((tm, 1), lambda i: (i, 0))],
        out_shape=[jax.ShapeDtypeStruct((S, D), F32), jax.ShapeDtypeStruct((S, D), BF16),
                   jax.ShapeDtypeStruct((S, D), F32), jax.ShapeDtypeStruct((S, 1), F32)],
        compiler_params=_cp(("parallel",)),
    )(res_xh, res_g, res_b, a, w, ln_g, ln_b)


def _dh1_ln_bwd(dproj, w_in, dr2, xh, rstd, ln_g, name, after=()):
    tm, tk, ch = 1024, IN_SH, EPI_ROWS
    nk = IN_W // tk
    na = len(after)

    def body(*refs):
        a_ref, b_ref, add_ref, xh_ref, rstd_ref, g_ref, dr_ref, dg_ref, db_ref, acc_ref = refs[na:]
        i = pl.program_id(0)
        k = pl.program_id(1)
        p = _dot_nt(a_ref[...], b_ref[...])

        @pl.when(k == 0)
        def _():
            acc_ref[...] = p

        @pl.when(k > 0)
        def _():
            acc_ref[...] += p

        @pl.when(jnp.logical_and(k == nk - 1, i == 0))
        def _():
            dg_ref[...] = jnp.zeros_like(dg_ref)
            db_ref[...] = jnp.zeros_like(db_ref)

        @pl.when(k == nk - 1)
        def _():
            for c0 in range(0, tm, ch):
                rows = slice(c0, c0 + ch)
                dy = acc_ref[rows, :] + ALPHA * add_ref[rows, :]
                xhv = xh_ref[rows, :]
                dr_ref[rows, :] = _ln_dx(dy * g_ref[...], xhv, rstd_ref[rows, :])
                dg_ref[...] += jnp.sum(dy * xhv, axis=0, keepdims=True)
                db_ref[...] += jnp.sum(dy, axis=0, keepdims=True)

    row = pl.BlockSpec((tm, D), lambda i, k: (i, 0))
    vec = pl.BlockSpec((1, D), lambda i, k: (0, 0))
    return pl.pallas_call(
        body, name=name, grid=(S // tm, nk),
        in_specs=[pl.BlockSpec(memory_space=pl.ANY)] * na
        + [pl.BlockSpec((tm, tk), lambda i, k: (i, k)), pl.BlockSpec((D, tk), lambda i, k: (0, k)), row, row,
           pl.BlockSpec((tm, 1), lambda i, k: (i, 0)), vec],
        out_specs=[row, vec, vec],
        out_shape=[jax.ShapeDtypeStruct((S, D), F32), jax.ShapeDtypeStruct((1, D), F32),
                   jax.ShapeDtypeStruct((1, D), F32)],
        scratch_shapes=[pltpu.VMEM((tm, D), F32)],
        compiler_params=_cp(("arbitrary", "arbitrary"), vmem_mib=56),
    )(*after, dproj, w_in, dr2, xh, rstd, ln_g)


def _ln_bwd(dout, xh, rstd, ln_g, name, after=()):
    tm = 512
    na = len(after)

    def body(*refs):
        y_ref, xh_ref, rstd_ref, g_ref, dr_ref, dg_ref, db_ref = refs[na:]
        dy = y_ref[...]
        i = pl.program_id(0)
        xh = xh_ref[...]
        dr_ref[...] = _ln_dx(dy * g_ref[...], xh, rstd_ref[...])
        dg = jnp.sum(dy * xh, axis=0, keepdims=True)
        db = jnp.sum(dy, axis=0, keepdims=True)

        @pl.when(i == 0)
        def _():
            dg_ref[...] = dg
            db_ref[...] = db

        @pl.when(i > 0)
        def _():
            dg_ref[...] += dg
            db_ref[...] += db

    row = pl.BlockSpec((tm, D), lambda i: (i, 0))
    vec = pl.BlockSpec((1, D), lambda i: (0, 0))
    return pl.pallas_call(
        body, name=name, grid=(S // tm,),
        in_specs=[pl.BlockSpec(memory_space=pl.ANY)] * na + [row, row, pl.BlockSpec((tm, 1), lambda i: (i, 0)), vec],
        out_specs=[row, vec, vec],
        out_shape=[jax.ShapeDtypeStruct((S, D), F32), jax.ShapeDtypeStruct((1, D), F32),
                   jax.ShapeDtypeStruct((1, D), F32)],
        compiler_params=_cp(("arbitrary",)),
    )(*after, dout, xh, rstd, ln_g)


ROPE_TM = 256


def _rope_tables(pos_ref, invf_ref, sign):
    ang = pos_ref[...] * invf_ref[...]
    lane = lax.broadcasted_iota(jnp.int32, ang.shape, 1)
    first = (lane % DH) < (DH // 2)
    sinv = jnp.sin(ang) * sign
    return first, jnp.cos(ang), jnp.where(first, -sinv, sinv)


def _rotate(x, first, cosf, sinf):
    return x * cosf + jnp.where(first, pltpu.roll(x, 96, 1), pltpu.roll(x, 32, 1)) * sinf


def _proj_qkv_rope(hb, w_in, pos_f, invf, name):
    tm = 2 * ROPE_TM

    def body(h_ref, w_ref, pos_ref, invf_ref, o0_ref, o1_ref, o2_ref, buf_ref):
        rot = pl.program_id(1) < 2
        first, cosf, sinf = _rope_tables(pos_ref, invf_ref, 1.0)
        cosf = jnp.where(rot, cosf, 1.0)
        sinf = jnp.where(rot, sinf, 0.0)
        acc = _dot(h_ref[...], w_ref[...])
        for gi, (d, o_ref) in enumerate(zip(DILATIONS, (o0_ref, o1_ref, o2_ref))):
            for ch in range(GRP_W // 128):
                cols = slice(ch * 128, (ch + 1) * 128)
                x = _rotate(acc[:, gi * GRP_W + ch * 128: gi * GRP_W + (ch + 1) * 128], first, cosf, sinf)
                if d == 1:
                    o_ref[0, :, cols] = x.astype(BF16)
                else:
                    buf_ref[...] = x
                    for r in range(d):
                        o_ref[r, :, cols] = buf_ref[pl.ds(r, tm // d, stride=d), :].astype(BF16)

    return pl.pallas_call(
        body, name=name, grid=(S // tm, 3),
        in_specs=[pl.BlockSpec((tm, D), lambda i, s: (i, 0)), pl.BlockSpec((D, ATT_W), lambda i, s: (0, s)),
                  pl.BlockSpec((tm, 1), lambda i, s: (i, 0)), pl.BlockSpec((1, 128), lambda i, s: (0, 0))],
        out_specs=[pl.BlockSpec((d, tm // d, GRP_W), lambda i, s: (0, i, s)) for d in DILATIONS],
        out_shape=[jax.ShapeDtypeStruct((d, S // d, 3 * GRP_W), BF16) for d in DILATIONS],
        scratch_shapes=[pltpu.VMEM((tm, 128), F32)],
        compiler_params=_cp(("parallel", "parallel")),
    )(hb, w_in, pos_f, invf)


def _rope_bwd(dqkv_c, dz, dgl, pos_f, invf, name):
    tm = ROPE_TM

    def body(*refs):
        g_refs, (dz_ref, dgl_ref, pos_ref, invf_ref, o_ref, buf_ref) = refs[:9], refs[9:]
        o_ref[:, 3 * ATT_W:3 * ATT_W + 2 * GW] = dz_ref[...]
        o_ref[:, 3 * ATT_W + 2 * GW:IN_W] = dgl_ref[...]
        first, cosf, sinf = _rope_tables(pos_ref, invf_ref, -1.0)
        for sec in range(3):
            for gi, d in enumerate(DILATIONS):
                g_ref = g_refs[3 * gi + sec]
                for ch in range(GRP_W // 128):
                    cols = slice(ch * 128, (ch + 1) * 128)
                    if d == 1:
                        x = g_ref[0, :, cols]
                    else:
                        for r in range(d):
                            buf_ref[pl.ds(r, tm // d, stride=d), :] = g_ref[r, :, cols]
                        x = buf_ref[...]
                    if sec < 2:
                        x = _rotate(x, first, cosf, sinf)
                    dst = sec * ATT_W + gi * GRP_W + ch * 128
                    o_ref[:, dst:dst + 128] = x.astype(BF16)

    g_specs = [pl.BlockSpec((d, tm // d, GRP_W), lambda i: (0, i, 0)) for d in DILATIONS for _ in range(3)]
    return pl.pallas_call(
        body, name=name, grid=(S // tm,),
        in_specs=g_specs + [pl.BlockSpec((tm, 2 * GW), lambda i: (i, 0)), pl.BlockSpec((tm, 2 * D), lambda i: (i, 0)),
                            pl.BlockSpec((tm, 1), lambda i: (i, 0)), pl.BlockSpec((1, 128), lambda i: (0, 0))],
        out_specs=pl.BlockSpec((tm, IN_W), lambda i: (i, 0)),
        out_shape=jax.ShapeDtypeStruct((S, IN_W), BF16),
        scratch_shapes=[pltpu.VMEM((tm, 128), F32)],
        compiler_params=_cp(("parallel",)),
    )(*[g for grp in dqkv_c for g in grp], dz, dgl, pos_f, invf)


def _class_order(ts, name):
    tm = ROPE_TM
    n = len(ts)

    def body(*refs):
        buf_ref = refs[3 * n]
        for a in range(n):
            for ch in range(GRP_W // 128):
                cols = slice(ch * 128, (ch + 1) * 128)
                buf_ref[...] = refs[a][:, cols]
                for b, d in enumerate(DILATIONS[1:]):
                    for r in range(d):
                        refs[n + 2 * a + b][r, :, cols] = buf_ref[pl.ds(r, tm // d, stride=d), :]

    return pl.pallas_call(
        body, name=name, grid=(S // tm,),
        in_specs=[pl.BlockSpec((tm, GRP_W), lambda i: (i, 0))] * n,
        out_specs=[pl.BlockSpec((d, tm // d, GRP_W), lambda i: (0, i, 0)) for _ in range(n) for d in DILATIONS[1:]],
        out_shape=[jax.ShapeDtypeStruct((d, S // d, GRP_W), F32) for _ in range(n) for d in DILATIONS[1:]],
        scratch_shapes=[pltpu.VMEM((tm, 128), F32)],
        compiler_params=_cp(("parallel",)),
    )(*ts)


def _own_lanes(h):
    return (lax.broadcasted_iota(jnp.int32, (1, 2 * DH), 1) // DH) == (h % 2)


def _heads(ref):
    out = []
    for h in range(NH):
        pair = ref[:, (h // 2) * 2 * DH:(h // 2 + 1) * 2 * DH]
        out.append(jnp.where(_own_lanes(h), pair, jnp.zeros_like(pair)))
    return jnp.stack(out)


def _unheads(t3):
    return jnp.concatenate([t3[2 * p] + t3[2 * p + 1] for p in range(NH // 2)], axis=1)


def _bdot_nt(a, b):
    return lax.dot_general(a, b, (((2,), (2,)), ((0,), (0,))), preferred_element_type=F32)


def _bdot(a, b):
    return lax.dot_general(a, b, (((2,), (1,)), ((0,), (0,))), preferred_element_type=F32)


def _bdot_tn(a, b):
    return lax.dot_general(a, b, (((1,), (1,)), ((0,), (0,))), preferred_element_type=F32)


def _attn_fwd(gi, qkv_c, name):
    d = DILATIONS[gi]
    nblk = S // d // BLK

    def body(*refs):
        if nblk > 1:
            q_ref, kc_ref, kp_ref, vc_ref, vp_ref, o_ref, lse_ref = refs
            has_prev = pl.program_id(1) != 0
        else:
            q_ref, kc_ref, vc_ref, o_ref, lse_ref = refs
        qi = lax.broadcasted_iota(jnp.int32, (NH, BLK, BLK), 1)
        kj = lax.broadcasted_iota(jnp.int32, (NH, BLK, BLK), 2)
        q = _heads(q_ref)
        sc = jnp.where(kj <= qi, _bdot_nt(q, _heads(kc_ref)) * 0.125, NEG_INF)
        m = jnp.max(sc, axis=-1, keepdims=True)
        if nblk > 1:
            mask_p = jnp.logical_and(kj >= qi, has_prev)
            sp = jnp.where(mask_p, _bdot_nt(q, _heads(kp_ref)) * 0.125, NEG_INF)
            m = jnp.maximum(m, jnp.max(sp, axis=-1, keepdims=True))
        pc = jnp.exp(sc - m)
        l = jnp.sum(pc, axis=-1, keepdims=True)
        o = _bdot(pc.astype(BF16), _heads(vc_ref))
        if nblk > 1:
            pp = jnp.exp(sp - m)
            l = l + jnp.sum(pp, axis=-1, keepdims=True)
            o = o + _bdot(pp.astype(BF16), _heads(vp_ref))
        o_ref[...] = _unheads(o / l)
        lse = jnp.broadcast_to(m + jnp.log(l), (NH, BLK, 2 * DH))
        lse_ref[...] = _unheads(jnp.stack([jnp.where(_own_lanes(h), lse[h], 0.0) for h in range(NH)]))

    def cur(sec):
        return pl.BlockSpec((None, BLK, GRP_W), lambda r, n: (r, n, sec))

    def prev(sec):
        return pl.BlockSpec((None, BLK, GRP_W), lambda r, n: (r, jnp.maximum(n - 1, 0), sec))

    out = pl.BlockSpec((None, BLK, GRP_W), lambda r, n: (r, n, 0))
    shp = jax.ShapeDtypeStruct((d, S // d, GRP_W), F32)
    if nblk > 1:
        in_specs, args = [cur(0), cur(1), prev(1), cur(2), prev(2)], (qkv_c,) * 5
    else:
        in_specs, args = [cur(0), cur(1), cur(2)], (qkv_c,) * 3
    return pl.pallas_call(
        body, name=name, grid=(d, nblk), in_specs=in_specs, out_specs=[out, out], out_shape=[shp, shp],
        compiler_params=_cp(("parallel", "parallel")),
    )(*args)


def _attn_combine(os, lses, name):
    tm = ROPE_TM

    def body(o0_ref, o1_ref, o2_ref, l0_ref, l1_ref, l2_ref, y_ref, yt_ref, l_ref, buf_ref):
        def token_order(ref, d, cols, slot):
            if d == 1:
                return ref[0, :, cols]
            for r in range(d):
                buf_ref[slot, pl.ds(r, tm // d, stride=d), :] = ref[r, :, cols]
            return buf_ref[slot]

        for ch in range(GRP_W // 128):
            cols = slice(ch * 128, (ch + 1) * 128)
            o = [token_order(ref, d, cols, k) for k, (ref, d) in enumerate(zip((o0_ref, o1_ref, o2_ref), DILATIONS))]
            ls = [token_order(ref, d, cols, 3 + k)
                  for k, (ref, d) in enumerate(zip((l0_ref, l1_ref, l2_ref), DILATIONS))]
            m = jnp.maximum(jnp.maximum(ls[0], ls[1]), ls[2])
            e = [jnp.exp(l - m) for l in ls]
            den = e[0] + e[1] + e[2]
            y = (e[0] * o[0] + e[1] * o[1] + e[2] * o[2]) / den
            y_ref[:, cols] = y
            yt_ref[cols, :] = y.T.astype(BF16)
            l_ref[:, cols] = m + jnp.log(den)

    blk = pl.BlockSpec((tm, GRP_W), lambda i: (i, 0))
    cls = [pl.BlockSpec((d, tm // d, GRP_W), lambda i: (0, i, 0)) for d in DILATIONS]
    shp = jax.ShapeDtypeStruct((S, GRP_W), F32)
    return pl.pallas_call(
        body, name=name, grid=(S // tm,), in_specs=cls + cls,
        out_specs=[blk, pl.BlockSpec((GRP_W, tm), lambda i: (0, i)), blk],
        out_shape=[shp, jax.ShapeDtypeStruct((GRP_W, S), BF16), shp],
        scratch_shapes=[pltpu.VMEM((6, tm, 128), F32)],
        compiler_params=_cp(("parallel",)),
    )(*os, *lses)


def _attn_bwd(gi, qkv_c, dy_c, y_c, lse_c, name):
    d = DILATIONS[gi]
    nblk = S // d // BLK

    def body(*refs):
        if nblk > 1:
            (q_ref, qn_ref, k_ref, kp_ref, v_ref, vp_ref, dy_ref, dyn_ref, y_ref, yn_ref, l_ref, ln_ref,
             dq_ref, dk_ref, dv_ref) = refs
            n = pl.program_id(1)
            has_prev = n != 0
            has_next = n != nblk - 1
        else:
            q_ref, k_ref, v_ref, dy_ref, y_ref, l_ref, dq_ref, dk_ref, dv_ref = refs
        qi = lax.broadcasted_iota(jnp.int32, (NH, BLK, BLK), 1)
        kj = lax.broadcasted_iota(jnp.int32, (NH, BLK, BLK), 2)

        def lse_col(ref):
            return jnp.stack([ref[:, h * DH:h * DH + 1] for h in range(NH)])

        q, k, v = _heads(q_ref), _heads(k_ref), _heads(v_ref)
        dy = _heads(dy_ref)
        dd = jnp.sum(dy * _heads(y_ref), axis=-1, keepdims=True)
        lcol = lse_col(l_ref)
        dyb = dy.astype(BF16)
        p = jnp.exp(jnp.where(kj <= qi, _bdot_nt(q, k) * 0.125, NEG_INF) - lcol)
        ds = (p * (_bdot_nt(dyb, v) - dd)).astype(BF16)
        dq = _bdot(ds, k)
        dk = _bdot_tn(ds, q)
        dv = _bdot_tn(p.astype(BF16), dyb)
        if nblk > 1:
            qn, kpv, vpv = _heads(qn_ref), _heads(kp_ref), _heads(vp_ref)
            dyn = _heads(dyn_ref)
            ddn = jnp.sum(dyn * _heads(yn_ref), axis=-1, keepdims=True)
            lncol = lse_col(ln_ref)
            dynb = dyn.astype(BF16)
            mask_p = jnp.logical_and(kj >= qi, has_prev)
            pp = jnp.exp(jnp.where(mask_p, _bdot_nt(q, kpv) * 0.125, NEG_INF) - lcol)
            dsp = (pp * (_bdot_nt(dyb, vpv) - dd)).astype(BF16)
            dq = dq + _bdot(dsp, kpv)
            mask_n = jnp.logical_and(kj >= qi, has_next)
            pn = jnp.exp(jnp.where(mask_n, _bdot_nt(qn, k) * 0.125, NEG_INF) - lncol)
            dsn = (pn * (_bdot_nt(dynb, v) - ddn)).astype(BF16)
            dk = dk + _bdot_tn(dsn, qn)
            dv = dv + _bdot_tn(pn.astype(BF16), dynb)
        dq_ref[...] = _unheads(dq) * 0.125
        dk_ref[...] = _unheads(dk) * 0.125
        dv_ref[...] = _unheads(dv)

    def spec(sec, shift):
        def idx(r, n):
            return (r, jnp.clip(n + shift, 0, nblk - 1), sec)
        return pl.BlockSpec((None, BLK, GRP_W), idx)

    if nblk > 1:
        in_specs = [spec(0, 0), spec(0, 1), spec(1, 0), spec(1, -1), spec(2, 0), spec(2, -1),
                    spec(0, 0), spec(0, 1), spec(0, 0), spec(0, 1), spec(0, 0), spec(0, 1)]
        args = (qkv_c,) * 6 + (dy_c, dy_c, y_c, y_c, lse_c, lse_c)
    else:
        in_specs = [spec(0, 0), spec(1, 0), spec(2, 0), spec(0, 0), spec(0, 0), spec(0, 0)]
        args = (qkv_c, qkv_c, qkv_c, dy_c, y_c, lse_c)
    out = spec(0, 0)
    shp = jax.ShapeDtypeStruct((d, S // d, GRP_W), F32)
    return pl.pallas_call(
        body, name=name, grid=(d, nblk), in_specs=in_specs, out_specs=[out, out, out], out_shape=[shp, shp, shp],
        compiler_params=_cp(("parallel", "parallel")),
    )(*args)


_SQRT_HALF = 0.7071067811865476
_INV_SQRT_2PI = 0.3989422804014327


def _gelu(z):
    return 0.5 * z * (1.0 + lax.erf(z * _SQRT_HALF))


def _gelu_grad(z):
    return 0.5 * (1.0 + lax.erf(z * _SQRT_HALF)) + z * (jnp.exp(-0.5 * z * z) * _INV_SQRT_2PI)


def _tril_mask():
    t = lax.broadcasted_iota(jnp.int32, (BLK, BLK), 0)
    s = lax.broadcasted_iota(jnp.int32, (BLK, BLK), 1)
    return s <= t


def _groups(t):
    return jnp.stack([t[:, g * BLK:(g + 1) * BLK] for g in range(8)])


def _ungroup(t3):
    return jnp.concatenate([t3[g] for g in range(8)], axis=1)


def _group_bias(bs_ref):
    return jnp.stack([bs_ref[:, g:g + 1] for g in range(8)])


def _gmlp_fwd(z, ln_g, ln_b, w_s, b_s_t, name):
    def body(z_ref, g_ref, b_ref, ws_ref, bs_ref, y_ref, yt_ref):
        zg = _gelu(z_ref[...])
        u = zg[:, :GW]
        xh, _ = _ln_stats(zg[:, GW:])
        vn = (xh * g_ref[...] + b_ref[...]).astype(BF16)
        wt = jnp.where(_tril_mask(), ws_ref[...], 0.0).astype(BF16)
        yv = u * _ungroup(_bdot(wt, _groups(vn)) + _group_bias(bs_ref))
        y_ref[...] = yv.astype(BF16)
        yt_ref[...] = yv.T.astype(BF16)

    vec = pl.BlockSpec((1, GW), lambda n: (0, 0))
    return pl.pallas_call(
        body, name=name, grid=(NBLK,),
        in_specs=[pl.BlockSpec((BLK, 2 * GW), lambda n: (n, 0)), vec, vec,
                  pl.BlockSpec((8, BLK, BLK), lambda n: (0, 0, 0)), pl.BlockSpec((BLK, 8), lambda n: (0, 0))],
        out_specs=[pl.BlockSpec((BLK, GW), lambda n: (n, 0)), pl.BlockSpec((GW, BLK), lambda n: (0, n))],
        out_shape=[jax.ShapeDtypeStruct((S, GW), BF16), jax.ShapeDtypeStruct((GW, S), BF16)],
        compiler_params=_cp(("parallel",)),
    )(z, ln_g, ln_b, w_s, b_s_t)


def _gmlp_bwd(z, dy, ln_g, ln_b, w_s, b_s_t, name):
    def body(z_ref, dy_ref, g_ref, b_ref, ws_ref, bs_ref, dz_ref, dws_ref, dbs_ref, dg_ref, db_ref, dvn_ref):
        n = pl.program_id(0)
        zv = z_ref[...]
        zg = _gelu(zv)
        u = zg[:, :GW]
        xh, rstd = _ln_stats(zg[:, GW:])
        vn = (xh * g_ref[...] + b_ref[...]).astype(BF16)
        tril = _tril_mask()

        @pl.when(n == 0)
        def _():
            dws_ref[...] = jnp.zeros_like(dws_ref)
            dbs_ref[...] = jnp.zeros_like(dbs_ref)
            dg_ref[...] = jnp.zeros_like(dg_ref)
            db_ref[...] = jnp.zeros_like(db_ref)

        wt = jnp.where(tril, ws_ref[...], 0.0).astype(BF16)
        vn3 = _groups(vn)
        dyv = dy_ref[...]
        mixed = _ungroup(_bdot(wt, vn3) + _group_bias(bs_ref))
        dz_ref[:, :GW] = (dyv * mixed * _gelu_grad(zv[:, :GW])).astype(BF16)
        dmix3 = _groups(dyv * u)
        dmb = dmix3.astype(BF16)
        dws_ref[...] += jnp.where(tril, _bdot_nt(dmb, vn3), 0.0)
        dbsum = jnp.sum(dmix3, axis=-1, keepdims=True)
        for gg in range(8):
            dbs_ref[:, gg:gg + 1] += dbsum[gg]
        dvn_ref[...] = _ungroup(_bdot_tn(wt, dmb))

        dvn = dvn_ref[...]
        dg_ref[...] += jnp.sum(dvn * xh, axis=0, keepdims=True)
        db_ref[...] += jnp.sum(dvn, axis=0, keepdims=True)
        dvg = _ln_dx(dvn * g_ref[...], xh, rstd)
        dz_ref[:, GW:] = (dvg * _gelu_grad(zv[:, GW:])).astype(BF16)

    vec = pl.BlockSpec((1, GW), lambda n: (0, 0))
    ws = pl.BlockSpec((8, BLK, BLK), lambda n: (0, 0, 0))
    bs = pl.BlockSpec((BLK, 8), lambda n: (0, 0))
    return pl.pallas_call(
        body, name=name, grid=(NBLK,),
        in_specs=[pl.BlockSpec((BLK, 2 * GW), lambda n: (n, 0)), pl.BlockSpec((BLK, GW), lambda n: (n, 0)),
                  vec, vec, ws, bs],
        out_specs=[pl.BlockSpec((BLK, 2 * GW), lambda n: (n, 0)), ws, bs, vec, vec],
        out_shape=[jax.ShapeDtypeStruct((S, 2 * GW), BF16), jax.ShapeDtypeStruct((8, BLK, BLK), F32),
                   jax.ShapeDtypeStruct((BLK, 8), F32), jax.ShapeDtypeStruct((1, GW), F32),
                   jax.ShapeDtypeStruct((1, GW), F32)],
        scratch_shapes=[pltpu.VMEM((BLK, GW), F32)],
        compiler_params=_cp(("arbitrary",)),
    )(z, dy, ln_g, ln_b, w_s, b_s_t)


def _merge_fwd(a, b, gl, b_gates, name):
    tm = 512

    def body(a_ref, b_ref, g0_ref, g1_ref, bg_ref, o_ref, ot_ref):
        g0 = jax.nn.sigmoid(g0_ref[...] + bg_ref[:, :D])
        g1 = jax.nn.sigmoid(g1_ref[...] + bg_ref[:, D:])
        mg = g0 * a_ref[...] + g1 * b_ref[...]
        o_ref[...] = mg.astype(BF16)
        ot_ref[...] = mg.T.astype(BF16)

    row = pl.BlockSpec((tm, D), lambda i: (i, 0))
    return pl.pallas_call(
        body, name=name, grid=(S // tm,),
        in_specs=[row, row, row, pl.BlockSpec((tm, D), lambda i: (i, 1)), pl.BlockSpec((1, 2 * D), lambda i: (0, 0))],
        out_specs=[row, pl.BlockSpec((D, tm), lambda i: (0, i))],
        out_shape=[jax.ShapeDtypeStruct((S, D), BF16), jax.ShapeDtypeStruct((D, S), BF16)],
        compiler_params=_cp(("parallel",)),
    )(a, b, gl, gl, b_gates)


def _merge_bwd(dm, a, b, gl, b_gates, name):
    tm = 512

    def body(dm_ref, a_ref, b_ref, g0_ref, g1_ref, bg_ref, da_ref, db_ref, dgl_ref, dbg_ref):
        i = pl.program_id(0)
        dmv = dm_ref[...]
        g0 = jax.nn.sigmoid(g0_ref[...] + bg_ref[:, :D])
        g1 = jax.nn.sigmoid(g1_ref[...] + bg_ref[:, D:])
        da_ref[...] = (dmv * g0).astype(BF16)
        db_ref[...] = (dmv * g1).astype(BF16)
        d0 = dmv * a_ref[...] * g0 * (1.0 - g0)
        d1 = dmv * b_ref[...] * g1 * (1.0 - g1)
        dgl_ref[:, :D] = d0.astype(BF16)
        dgl_ref[:, D:] = d1.astype(BF16)
        s0 = jnp.sum(d0, axis=0, keepdims=True)
        s1 = jnp.sum(d1, axis=0, keepdims=True)

        @pl.when(i == 0)
        def _():
            dbg_ref[:, :D] = s0
            dbg_ref[:, D:] = s1

        @pl.when(i > 0)
        def _():
            dbg_ref[:, :D] += s0
            dbg_ref[:, D:] += s1

    row = pl.BlockSpec((tm, D), lambda i: (i, 0))
    wide = pl.BlockSpec((tm, 2 * D), lambda i: (i, 0))
    bg = pl.BlockSpec((1, 2 * D), lambda i: (0, 0))
    return pl.pallas_call(
        body, name=name, grid=(S // tm,),
        in_specs=[row, row, row, row, pl.BlockSpec((tm, D), lambda i: (i, 1)), bg],
        out_specs=[row, row, wide, bg],
        out_shape=[jax.ShapeDtypeStruct((S, D), BF16), jax.ShapeDtypeStruct((S, D), BF16),
                   jax.ShapeDtypeStruct((S, 2 * D), BF16), jax.ShapeDtypeStruct((1, 2 * D), F32)],
        compiler_params=_cp(("arbitrary",)),
    )(dm, a, b, gl, gl, b_gates)


def _adam_math(w, g, m, v):
    m2 = ADAM_B1 * m + (1.0 - ADAM_B1) * g
    v2 = ADAM_B2 * v + (1.0 - ADAM_B2) * (g * g)
    m_hat = m2 / (1.0 - ADAM_B1 ** ADAM_STEP)
    v_hat = v2 / (1.0 - ADAM_B2 ** ADAM_STEP)
    delta = -ADAM_LR * (m_hat / (jnp.sqrt(v_hat) + ADAM_EPS) + ADAM_WD * w)
    return delta, m2, v2


def _pick_rows(rows, cols, unit=16, budget=2 * MIB):
    best = unit
    for t in range(unit, rows + 1, unit):
        if rows % t == 0 and t * cols * 4 <= budget:
            best = t
    assert rows % best == 0
    return best


def _adamw(w, g, m, v, name):
    r, c = w.shape
    tr = _pick_rows(r, c, unit=8)

    def body(w_ref, g_ref, m_ref, v_ref, go_ref, d_ref, mo_ref, vo_ref):
        gv = g_ref[...]
        delta, m2, v2 = _adam_math(w_ref[...], gv, m_ref[...], v_ref[...])
        go_ref[...] = gv
        d_ref[...] = delta
        mo_ref[...] = m2
        vo_ref[...] = v2

    blk = pl.BlockSpec((tr, c), lambda i: (i, 0))
    shp = jax.ShapeDtypeStruct((r, c), F32)
    return pl.pallas_call(
        body, name=name, grid=(r // tr,), in_specs=[blk] * 4, out_specs=[blk] * 4, out_shape=[shp] * 4,
        compiler_params=_cp(("parallel",)),
    )(*[pltpu.with_memory_space_constraint(t, pltpu.HBM) for t in (w, g, m, v)])


def _small_sum_adamw(parts, own, pos, w, m, v, name):
    tr = 48

    def body(pos_ref, p_ref, own_ref, w_ref, m_ref, v_ref, g_ref, d_ref, mo_ref, vo_ref):
        me = 2 * pos_ref[1] + pos_ref[0]
        gv = None
        for k in range(8):
            term = jnp.where(me == k, own_ref[...], p_ref[k])
            gv = term if gv is None else gv + term
        delta, m2, v2 = _adam_math(w_ref[...], gv, m_ref[...], v_ref[...])
        g_ref[...] = gv
        d_ref[...] = delta
        mo_ref[...] = m2
        vo_ref[...] = v2

    blk = pl.BlockSpec((tr, D), lambda i, p: (i, 0))
    shp = jax.ShapeDtypeStruct((SMALL_ROWS, D), F32)
    return pl.pallas_call(
        body, name=name,
        grid_spec=pltpu.PrefetchScalarGridSpec(
            num_scalar_prefetch=1, grid=(SMALL_ROWS // tr,),
            in_specs=[pl.BlockSpec((8, tr, D), lambda i, p: (0, i, 0)), blk, blk, blk, blk],
            out_specs=[blk] * 4),
        out_shape=[shp] * 4,
        compiler_params=_cp(("parallel",)),
    )(pos, parts, own, w, m, v)


ANY = pl.BlockSpec(memory_space=pl.ANY)


def _in_hbm(arrays):
    return [pltpu.with_memory_space_constraint(a, pltpu.HBM) for a in arrays]


def _mesh_pos():
    x, y, c = lax.axis_index("x"), lax.axis_index("y"), lax.axis_index("c")
    chips = [(1 - x, y), (x, 1 - y), (1 - x, 1 - y)]
    return x, y, c, chips


def _place_shard(w, kind, pos, name):
    r, c = w.shape
    tr = _pick_rows(r, c)

    def body(pos_ref, w_ref, o_ref):
        o_ref[...] = w_ref[...].astype(BF16)

    if kind == "stack":
        o_spec = pl.BlockSpec((None, tr, c), lambda i, p: (p[1], i, 0))
        shape = (NSH, r, c)
    else:
        o_spec = pl.BlockSpec((tr, c), lambda i, p: (i, p[1]))
        shape = (r, NSH * c)
    return pl.pallas_call(
        body, name=name,
        grid_spec=pltpu.PrefetchScalarGridSpec(
            num_scalar_prefetch=1, grid=(r // tr,),
            in_specs=[pl.BlockSpec((tr, c), lambda i, p: (i, 0))], out_specs=o_spec),
        out_shape=pltpu.HBM(shape, BF16),
        compiler_params=_cp(("parallel",)),
    )(pos, pltpu.with_memory_space_constraint(w, pltpu.HBM))


SEM = pl.BlockSpec(memory_space=pltpu.SEMAPHORE)
SPLIT_COPY = pltpu.CompilerParams(has_side_effects=pltpu.SideEffectType.DATAFLOW_SIDE_EFFECTING)


def _shard_window(ref, kind, j, h, dims):
    r, c = dims
    rows = pl.ds(pl.multiple_of(h * (r // 2), 16), r // 2)
    if kind == "stack":
        return ref.at[j, rows, :]
    return ref.at[rows, pl.ds(pl.multiple_of(j * c, 128), c)]


def _ici_copy(ref, kind, dims, j, c, sems, idx, to):
    win = _shard_window(ref, kind, j, c, dims)
    return pltpu.make_async_remote_copy(src_ref=win, dst_ref=win, send_sem=sems[0].at[idx], recv_sem=sems[1].at[idx],
                                        device_id=to, device_id_type=MESH_T)


def _both_copy(ref, kind, dims, a, k, chip, half, tc, sc, sems):
    win = _shard_window(ref, kind, half[0], half[1], dims)
    return pltpu.make_async_remote_copy(src_ref=win, dst_ref=win, send_sem=sems[0].at[6 * a + 2 * k + tc],
                                        recv_sem=sems[1].at[6 * a + 2 * k + sc],
                                        device_id=(chip[0], chip[1], tc), device_id_type=MESH_T)


def _gather_start(fulls, kinds, dims, after, name, both=False):
    n, na = len(fulls), len(after)
    per = 6 if both else 3

    def body(*refs):
        outs = refs[n + na:2 * n + na]
        send_sems, recv_sems, token = refs[2 * n + na:]
        x, y, c, chips = _mesh_pos()
        for a in range(n):
            for k, chip in enumerate(chips):
                if both:
                    for tc in range(2):
                        _both_copy(outs[a], kinds[a], dims[a], a, k, chip, (2 * x + y, c), tc, c,
                                   (send_sems, recv_sems)).start()
                else:
                    _ici_copy(outs[a], kinds[a], dims[a], 2 * x + y, c, (send_sems, recv_sems), 3 * a + k,
                              (chip[0], chip[1], c)).start()
        token[...] = jnp.zeros_like(token)

    res = pl.pallas_call(
        body, name=name, in_specs=[ANY] * (n + na),
        out_specs=[ANY] * n + [SEM, SEM, pl.BlockSpec(memory_space=pltpu.VMEM)],
        out_shape=[pltpu.HBM(f.shape, BF16) for f in fulls]
        + [pltpu.SemaphoreType.DMA((per * n,)), pltpu.SemaphoreType.DMA((per * n,)),
           jax.ShapeDtypeStruct((8, 128), F32)],
        input_output_aliases={i: i for i in range(n)},
        compiler_params=SPLIT_COPY,
    )(*_in_hbm(fulls), *after)
    return res[:n], res[n], res[n + 1], res[n + 2]


def _gather_wait(fulls, send_sems, recv_sems, kinds, dims, after, name, both=False):
    n, na = len(fulls), len(after)

    def body(*refs):
        ssem, rsem = refs[n], refs[n + 1]
        outs = refs[n + 2 + na:]
        x, y, c, chips = _mesh_pos()
        for a in range(n):
            for k, chip in enumerate(chips):
                if both:
                    for oc in range(2):
                        _both_copy(outs[a], kinds[a], dims[a], a, k, chip, (2 * x + y, c), oc, c,
                                   (ssem, rsem)).wait_send()
                        _both_copy(outs[a], kinds[a], dims[a], a, k, chip, (2 * chip[0] + chip[1], oc), c, oc,
                                   (ssem, rsem)).wait_recv()
                    continue
                to = (chip[0], chip[1], c)
                _ici_copy(outs[a], kinds[a], dims[a], 2 * x + y, c, (ssem, rsem), 3 * a + k, to).wait_send()
                _ici_copy(outs[a], kinds[a], dims[a], 2 * chip[0] + chip[1], c, (ssem, rsem), 3 * a + k, to).wait_recv()

    return pl.pallas_call(
        body, name=name, in_specs=[ANY] * n + [SEM, SEM] + [ANY] * na, out_specs=[ANY] * n,
        out_shape=[pltpu.HBM(f.shape, BF16) for f in fulls],
        input_output_aliases={i: i for i in range(n)},
        compiler_params=SPLIT_COPY,
    )(*_in_hbm(fulls), send_sems, recv_sems, *after)


def _gather_forward(fulls, kinds, dims, name):
    n = len(fulls)

    def body(*refs):
        outs = refs[n:2 * n]
        sems = refs[2 * n:]
        x, y, c, chips = _mesh_pos()
        sib = (x, y, 1 - c)
        cps = []
        for a in range(n):
            for k, chip in enumerate(chips):
                cp = _ici_copy(outs[a], kinds[a], dims[a], 2 * chip[0] + chip[1], c, sems, 3 * a + k, sib)
                cp.start()
                cps.append(cp)
        for a in range(n):
            for k, chip in enumerate(chips):
                _ici_copy(outs[a], kinds[a], dims[a], 2 * chip[0] + chip[1], 1 - c, sems, 3 * a + k, sib).wait_recv()
        for cp in cps:
            cp.wait_send()

    return pl.pallas_call(
        body, name=name, in_specs=[ANY] * n, out_specs=[ANY] * n,
        out_shape=[pltpu.HBM(f.shape, BF16) for f in fulls],
        input_output_aliases={i: i for i in range(n)},
        scratch_shapes=[pltpu.SemaphoreType.DMA((3 * n,)), pltpu.SemaphoreType.DMA((3 * n,))],
    )(*_in_hbm(fulls))


def _pair_copy(src, land, a, x, y, c, sems):
    return pltpu.make_async_remote_copy(
        src_ref=src.at[1 - c], dst_ref=land, send_sem=sems[0].at[a], recv_sem=sems[1].at[a],
        device_id=(x, y, 1 - c), device_id_type=MESH_T)


def _pair_start(grads, lands, name):
    n = len(grads)

    def body(*refs):
        srcs, dsts = refs[2 * n:3 * n], refs[3 * n:4 * n]
        send_sems, recv_sems, token = refs[4 * n:]
        x, y, c, _ = _mesh_pos()
        for a in range(n):
            _pair_copy(srcs[a], dsts[a], a, x, y, c, (send_sems, recv_sems)).start()
        token[...] = jnp.zeros_like(token)

    res = pl.pallas_call(
        body, name=name, in_specs=[ANY] * (2 * n),
        out_specs=[ANY] * (2 * n) + [SEM, SEM, pl.BlockSpec(memory_space=pltpu.VMEM)],
        out_shape=[pltpu.HBM(g.shape, F32) for g in grads]
        + [pltpu.HBM(l.shape, F32) for l in lands]
        + [pltpu.SemaphoreType.DMA((n,)), pltpu.SemaphoreType.DMA((n,)), jax.ShapeDtypeStruct((8, 128), F32)],
        input_output_aliases={i: i for i in range(2 * n)},
        compiler_params=SPLIT_COPY,
    )(*_in_hbm(grads), *_in_hbm(lands))
    return res[:n], res[n:2 * n], res[2 * n], res[2 * n + 1], res[2 * n + 2]


def _pair_wait(grads, lands, send_sems, recv_sems, after, name):
    n, na = len(grads), len(after)

    def body(*refs):
        ssem, rsem = refs[2 * n], refs[2 * n + 1]
        outs = refs[2 * n + 2 + na:]
        x, y, c, _ = _mesh_pos()
        for a in range(n):
            cp = _pair_copy(outs[a], outs[n + a], a, x, y, c, (ssem, rsem))
            cp.wait_send()
            cp.wait_recv()

    res = pl.pallas_call(
        body, name=name, in_specs=[ANY] * (2 * n) + [SEM, SEM] + [ANY] * na, out_specs=[ANY] * (2 * n),
        out_shape=[pltpu.HBM(g.shape, F32) for g in grads]
        + [pltpu.HBM(l.shape, F32) for l in lands],
        input_output_aliases={i: i for i in range(2 * n)},
        compiler_params=SPLIT_COPY,
    )(*_in_hbm(grads), *_in_hbm(lands), send_sems, recv_sems, *after)
    return res[:n], res[n:]


def _pair_sum(g, recv, pos, name):
    _, _, rh, c = g.shape
    tr = _pick_rows(rh, c)

    def body(pos_ref, g_ref, r_ref, o_ref):
        o_ref[...] = (g_ref[...] + r_ref[...]).astype(BF16)

    return pl.pallas_call(
        body, name=name,
        grid_spec=pltpu.PrefetchScalarGridSpec(
            num_scalar_prefetch=1, grid=(3, rh // tr),
            in_specs=[pl.BlockSpec((None, None, tr, c), lambda k, r, p: (p[0], p[2 + k], r, 0)),
                      pl.BlockSpec((None, tr, c), lambda k, r, p: (p[2 + k], r, 0))],
            out_specs=pl.BlockSpec((None, tr, c), lambda k, r, p: (k, r, 0))),
        out_shape=pltpu.HBM((3, rh, c), BF16),
        compiler_params=_cp(("parallel", "parallel")),
    )(pos, *_in_hbm([g, recv]))


def _pair_sum_group(gs, recvs, pos, name):
    n = len(gs)
    _, _, rh, c = gs[0].shape

    def body(pos_ref, *refs):
        a = pl.program_id(0)
        for t in range(n):
            @pl.when(a == t)
            def _(t=t):
                refs[2 * n + t][...] = (refs[t][...] + refs[n + t][...]).astype(BF16)

    def slot(t, a, k):
        return jnp.where(a == t, k, jnp.where(a < t, 0, 2))

    g_specs = [pl.BlockSpec((None, None, rh, c), lambda a, k, p, t=t: (p[0], p[2 + slot(t, a, k)], 0, 0))
               for t in range(n)]
    r_specs = [pl.BlockSpec((None, rh, c), lambda a, k, p, t=t: (p[2 + slot(t, a, k)], 0, 0)) for t in range(n)]
    o_specs = [pl.BlockSpec((None, rh, c), lambda a, k, p, t=t: (slot(t, a, k), 0, 0)) for t in range(n)]
    return pl.pallas_call(
        body, name=name,
        grid_spec=pltpu.PrefetchScalarGridSpec(num_scalar_prefetch=1, grid=(n, 3), in_specs=g_specs + r_specs,
                                               out_specs=o_specs),
        out_shape=[pltpu.HBM((3, rh, c), BF16)] * n,
        compiler_params=_cp(("arbitrary", "arbitrary")),
    )(pos, *_in_hbm(list(gs) + list(recvs)))


def _chip_copy(src, land, a, k, chip, c, sems):
    return pltpu.make_async_remote_copy(
        src_ref=src.at[k], dst_ref=land.at[k], send_sem=sems[0].at[3 * a + k],
        recv_sem=sems[1].at[3 * a + k], device_id=(chip[0], chip[1], c), device_id_type=MESH_T)


def _chip_start(psums, lands, name):
    n = len(psums)

    def body(*refs):
        srcs, dsts = refs[2 * n:3 * n], refs[3 * n:4 * n]
        send_sems, recv_sems, token = refs[4 * n:]
        x, y, c, chips = _mesh_pos()
        for a in range(n):
            for k, chip in enumerate(chips):
                _chip_copy(srcs[a], dsts[a], a, k, chip, c, (send_sems, recv_sems)).start()
        token[...] = jnp.zeros_like(token)

    res = pl.pallas_call(
        body, name=name, in_specs=[ANY] * (2 * n),
        out_specs=[ANY] * (2 * n) + [SEM, SEM, pl.BlockSpec(memory_space=pltpu.VMEM)],
        out_shape=[pltpu.HBM(p.shape, BF16) for p in psums]
        + [pltpu.HBM(l.shape, BF16) for l in lands]
        + [pltpu.SemaphoreType.DMA((3 * n,)), pltpu.SemaphoreType.DMA((3 * n,)), jax.ShapeDtypeStruct((8, 128), F32)],
        input_output_aliases={i: i for i in range(2 * n)},
        compiler_params=SPLIT_COPY,
    )(*_in_hbm(psums), *_in_hbm(lands))
    return res[:n], res[n:2 * n], res[2 * n], res[2 * n + 1], res[2 * n + 2]


def _chip_wait(psums, lands, send_sems, recv_sems, after, name):
    n, na = len(psums), len(after)

    def body(*refs):
        ssem, rsem = refs[2 * n], refs[2 * n + 1]
        outs = refs[2 * n + 2 + na:]
        srcs, dsts = outs[:n], outs[n:]
        x, y, c, chips = _mesh_pos()
        for a in range(n):
            for k, chip in enumerate(chips):
                cp = _chip_copy(srcs[a], dsts[a], a, k, chip, c, (ssem, rsem))
                cp.wait_send()
                cp.wait_recv()

    res = pl.pallas_call(
        body, name=name, in_specs=[ANY] * (2 * n) + [SEM, SEM] + [ANY] * na, out_specs=[ANY] * (2 * n),
        out_shape=[pltpu.HBM(p.shape, BF16) for p in psums]
        + [pltpu.HBM(l.shape, BF16) for l in lands],
        input_output_aliases={i: i for i in range(2 * n)},
        compiler_params=SPLIT_COPY,
    )(*_in_hbm(psums), *_in_hbm(lands), send_sems, recv_sems, *after)
    return res[n:]


def _owner_sum(g, recv_a, recv_b, pos, name):
    _, _, rh, c = g.shape
    tr = _pick_rows(rh, c)

    def body(pos_ref, g_ref, ra_ref, rb_ref, o_ref):
        acc = g_ref[...] + ra_ref[...]
        for k in range(3):
            acc = acc + rb_ref[k].astype(F32)
        o_ref[...] = acc

    return pl.pallas_call(
        body, name=name,
        grid_spec=pltpu.PrefetchScalarGridSpec(
            num_scalar_prefetch=1, grid=(rh // tr,),
            in_specs=[pl.BlockSpec((None, None, tr, c), lambda r, p: (p[0], p[1], r, 0)),
                      pl.BlockSpec((None, tr, c), lambda r, p: (p[1], r, 0)),
                      pl.BlockSpec((3, tr, c), lambda r, p: (0, r, 0))],
            out_specs=pl.BlockSpec((None, tr, c), lambda r, p: (p[0], r, 0))),
        out_shape=pltpu.HBM((2, rh, c), F32),
        compiler_params=_cp(("parallel",)),
    )(pos, *_in_hbm([g, recv_a, recv_b]))


def _sibling_allgather(halves, name):
    n = len(halves)

    def body(*refs):
        outs = refs[n:2 * n]
        send_sems, recv_sems = refs[2 * n:]
        x, y, c, _ = _mesh_pos()
        cps = []
        for a in range(n):
            cp = pltpu.make_async_remote_copy(
                src_ref=outs[a].at[c], dst_ref=outs[a].at[c], send_sem=send_sems.at[a], recv_sem=recv_sems.at[a],
                device_id=(x, y, 1 - c), device_id_type=MESH_T)
            cp.start()
            cps.append(cp)
        for a in range(n):
            cps[a].wait_send()
            pltpu.make_async_remote_copy(
                src_ref=outs[a].at[1 - c], dst_ref=outs[a].at[1 - c], send_sem=send_sems.at[a],
                recv_sem=recv_sems.at[a], device_id=(x, y, 1 - c), device_id_type=MESH_T).wait_recv()

    return pl.pallas_call(
        body, name=name, in_specs=[ANY] * n, out_specs=[ANY] * n,
        out_shape=[pltpu.HBM(h.shape, F32) for h in halves],
        input_output_aliases={i: i for i in range(n)},
        scratch_shapes=[pltpu.SemaphoreType.DMA((n,)), pltpu.SemaphoreType.DMA((n,))],
    )(*_in_hbm(halves))


def _peers(x, y, c):
    rel = [(0, 0, 1), (0, 1, 0), (0, 1, 1), (1, 0, 0), (1, 0, 1), (1, 1, 0), (1, 1, 1)]
    return [((1 - x) if dx else x, (1 - y) if dy else y, (1 - c) if dc else c) for dx, dy, dc in rel]


def _small_copy(src, land, k, peer, slot, sems):
    return pltpu.make_async_remote_copy(src_ref=src, dst_ref=land.at[slot], send_sem=sems[0].at[k],
                                        recv_sem=sems[1].at[k], device_id=peer, device_id_type=MESH_T)


def _small_start(part, land, name):
    def body(p_in, l_in, p_ref, l_ref, send_sems, recv_sems, token):
        x, y, c, _ = _mesh_pos()
        for k, peer in enumerate(_peers(x, y, c)):
            _small_copy(p_ref, l_ref, k, peer, 4 * x + 2 * y + c, (send_sems, recv_sems)).start()
        token[...] = jnp.zeros_like(token)

    return pl.pallas_call(
        body, name=name, in_specs=[ANY, ANY],
        out_specs=[ANY, ANY, SEM, SEM, pl.BlockSpec(memory_space=pltpu.VMEM)],
        out_shape=[pltpu.HBM(part.shape, F32), pltpu.HBM(land.shape, F32), pltpu.SemaphoreType.DMA((7,)),
                   pltpu.SemaphoreType.DMA((7,)), jax.ShapeDtypeStruct((8, 128), F32)],
        input_output_aliases={0: 0, 1: 1},
        compiler_params=SPLIT_COPY,
    )(*_in_hbm([part, land]))


def _small_wait(part, land, send_sems, recv_sems, after, name):
    na = len(after)

    def body(*refs):
        ssem, rsem = refs[2], refs[3]
        p_ref, l_ref = refs[4 + na:]
        x, y, c, _ = _mesh_pos()
        for k, peer in enumerate(_peers(x, y, c)):
            cp = _small_copy(p_ref, l_ref, k, peer, 4 * peer[0] + 2 * peer[1] + peer[2], (ssem, rsem))
            cp.wait_send()
            cp.wait_recv()

    return pl.pallas_call(
        body, name=name, in_specs=[ANY, ANY, SEM, SEM] + [ANY] * na, out_specs=[ANY, ANY],
        out_shape=[pltpu.HBM(part.shape, F32), pltpu.HBM(land.shape, F32)],
        input_output_aliases={0: 0, 1: 1},
        compiler_params=SPLIT_COPY,
    )(*_in_hbm([part, land]), send_sems, recv_sems, *after)


def _pack_small(ln1_g, ln1_b, gln_g, gln_b, ln2_g, ln2_b, ln3_g, ln3_b, b_gates, b_s, w_s):
    rows = [ln1_g, ln1_b, gln_g, gln_b, ln2_g, ln2_b, ln3_g, ln3_b]
    rows = [r.reshape(1, D) for r in rows] + [b_gates.reshape(2, D), b_s.reshape(1, D), jnp.zeros((5, D), F32),
                                             w_s.reshape(128, D)]
    return jnp.concatenate(rows, axis=0)


def _unpack_small(p):
    out = [p[i:i + 1] for i in range(8)]
    return out + [p[8:10].reshape(1, 2 * D), p[10:11].reshape(1, 8, BLK), p[16:144].reshape(1, 8, BLK, BLK)]


GROUPS = (("f1g", "f1u", "f1d"), ("w_in",), ("w_ab", "w_gb", "w_out"), ("f2g", "f2u", "f2d"))
LATE_GROUPS = (2, 3)


def _local_step(x, pos_f, target, P, weights_of, grads_ready, flush, small_ready):
    invf = ROPE_THETA ** (-jnp.arange(0, DH, 2, dtype=F32) / DH)
    invf = jnp.tile(invf, 4).reshape(1, 128)
    b_s_t = P["gmlp_b_s"].T

    W = dict(weights_of(0, []))
    h1b, xh1, rstd1, a1, b1, h1t = _ffn_fwd(x, W["f1g"], W["f1u"], W["f1d"], P["ln1_g"], P["ln1_b"], "ffn1_fwd",
                                                emit_t=True)
    W.update(weights_of(1, [h1b]))
    qkv_c = _proj_qkv_rope(h1b, W["w_in"], pos_f, invf, "proj_qkv_rope")
    z = _matmul(h1b, W["w_in"], "nn", "proj_z", n=2 * GW, b_col0=3 * ATT_W, tm=S, tn=512)
    gl = _matmul(h1b, W["w_in"], "nn", "proj_gates", n=2 * D, b_col0=3 * ATT_W + 2 * GW, tm=S, tn=512)
    og = [_attn_fwd(gi, qkv_c[gi], "attn_fwd_g%d" % gi) for gi in range(NG)]
    y_attn, y_attn_t, lse = _attn_combine([o for o, _ in og], [l for _, l in og], "attn_combine")
    y_gmlp, y_gmlp_t = _gmlp_fwd(z, P["gmlp_ln_g"], P["gmlp_ln_b"], P["gmlp_w_s"], b_s_t, "gmlp_fwd")
    W.update(weights_of(2, [y_gmlp]))
    br_a = _matmul(y_attn, W["w_ab"], "nn", "branch_attn", n=D, tm=1024, tn=D)
    br_b = _matmul(y_gmlp, W["w_gb"], "nn", "branch_gmlp", n=D, tm=1024, tn=D)
    merged, merged_t = _merge_fwd(br_a, br_b, gl, P["b_gates"], "merge_fwd")
    h2, h2b, xh2, rstd2 = _resid_ln(xh1, P["ln1_g"], P["ln1_b"], merged, W["w_out"], P["ln2_g"], P["ln2_b"],
                                    "mix_resid_ln2")
    W.update(weights_of(3, [h2b]))
    dr3, a2, b2, dg3, db3, loss = _ffn_fwd(h2, W["f2g"], W["f2u"], W["f2d"], P["ln3_g"], P["ln3_b"],
                                           "ffn2_fwd_loss", target=target)

    g_f2g, g_f2u, g_f2d, dh2 = _ffn_bwd(dr3, h2b, a2, b2, W["f2g"], W["f2u"], W["f2d"], "ffn2_bwd")
    tok = grads_ready(3, dict(f2g=g_f2g, f2u=g_f2u, f2d=g_f2d))
    dr2, dg2, db2 = _ln_bwd(dh2, xh2, rstd2, P["ln2_g"], "ln2_bwd", after=tok)
    g_wout = _wgrad(merged_t, dr2, 128, D, "dw_out", row_sharded=True)
    dmerged = _matmul(dr2, W["w_out"], "nt", "dmerged", n=D, tm=1024, tn=D)
    dab, dbb, dglb, dbg = _merge_bwd(dmerged, br_a, br_b, gl, P["b_gates"], "merge_bwd")
    tok = flush([dab])
    g_wab = _wgrad(y_attn_t, dab, GRP_W // 2, 256, "dw_attn_branch", row_sharded=False, after=tok)
    g_wgb = _wgrad(y_gmlp_t, dbb, 128, D, "dw_gmlp_branch", row_sharded=True)
    tok = grads_ready(2, dict(w_ab=g_wab, w_gb=g_wgb, w_out=g_wout))
    dy_attn = _matmul(dab, W["w_ab"], "nt", "dy_attn", n=GRP_W, tm=1024, tn=GRP_W, after=tok)
    dy_gmlp = _matmul(dbb, W["w_gb"], "nt", "dy_gmlp", n=GW, tm=1024, tn=GW)
    dzb, dws, dbs_t, dgln_g, dgln_b = _gmlp_bwd(z, dy_gmlp, P["gmlp_ln_g"], P["gmlp_ln_b"], P["gmlp_w_s"], b_s_t,
                                                 "gmlp_bwd")
    cls = _class_order([dy_attn, y_attn, lse], "attn_class_order")
    dqkv_c = []
    for gi in range(NG):
        dy_c, y_c, lse_c = [t[None] if gi == 0 else cls[2 * a + gi - 1] for a, t in enumerate((dy_attn, y_attn, lse))]
        dqkv_c.append(_attn_bwd(gi, qkv_c[gi], dy_c, y_c, lse_c, "attn_bwd_g%d" % gi))
    dproj = _rope_bwd(dqkv_c, dzb, dglb, pos_f, invf, "rope_bwd")
    tok = flush([dproj])
    g_win = _wgrad(h1t, dproj, D // 2, IN_SH, "dw_in", row_sharded=False, after=tok)
    tok = grads_ready(1, dict(w_in=g_win))
    dr1, dg1, db1 = _dh1_ln_bwd(dproj, W["w_in"], dr2, xh1, rstd1, P["ln1_g"], "dh1_ln1_bwd", after=tok)
    tok = flush([dr1])
    tok = tok + small_ready(_pack_small(dg1, db1, dgln_g, dgln_b, dg2, db2, dg3, db3, dbg, dbs_t.T, dws))
    g_f1g, g_f1u, g_f1d, dx = _ffn_bwd(dr1, x.astype(BF16), a1, b1, W["f1g"], W["f1u"], W["f1d"], "ffn1_bwd",
                                       after=tok)
    grads_ready(0, dict(f1g=g_f1g, f1u=g_f1u, f1d=g_f1d))
    flush([dx])
    return loss, dx


TRANSPOSED = ("f1g", "f1u", "f2g", "f2u")
KIND = dict(f1g="stack", f1u="stack", f1d="stack", w_in="col", w_ab="col", w_gb="stack", w_out="stack",
            f2g="stack", f2u="stack", f2d="stack")


def kernel(x, positions, ffn1_w_gate, ffn1_w_up, ffn1_w_down, ln1_g, ln1_b, w_in, b_gates, gmlp_ln_g, gmlp_ln_b, gmlp_w_s, gmlp_b_s, w_attn_branch, w_gmlp_branch, w_out, ln2_g, ln2_b, ffn2_w_gate, ffn2_w_up, ffn2_w_down, ln3_g, ln3_b, loss_target, m_ffn1_w_gate, m_ffn1_w_up, m_ffn1_w_down, m_ln1_g, m_ln1_b, m_w_in, m_b_gates, m_gmlp_ln_g, m_gmlp_ln_b, m_gmlp_w_s, m_gmlp_b_s, m_w_attn_branch, m_w_gmlp_branch, m_w_out, m_ln2_g, m_ln2_b, m_ffn2_w_gate, m_ffn2_w_up, m_ffn2_w_down, m_ln3_g, m_ln3_b, v_ffn1_w_gate, v_ffn1_w_up, v_ffn1_w_down, v_ln1_g, v_ln1_b, v_w_in, v_b_gates, v_gmlp_ln_g, v_gmlp_ln_b, v_gmlp_w_s, v_gmlp_b_s, v_w_attn_branch, v_w_gmlp_branch, v_w_out, v_ln2_g, v_ln2_b, v_ffn2_w_gate, v_ffn2_w_up, v_ffn2_w_down, v_ln3_g, v_ln3_b):
    cx, cy, cc = lax.axis_index("x"), lax.axis_index("y"), lax.axis_index("c")
    pos = jnp.stack([cc, 2 * cx + cy, 2 * (1 - cx) + cy, 2 * cx + 1 - cy, 2 * (1 - cx) + 1 - cy]).astype(jnp.int32)

    w_sh = dict(f1g=ffn1_w_gate, f1u=ffn1_w_up, f1d=ffn1_w_down, w_in=w_in, w_ab=w_attn_branch,
                w_gb=w_gmlp_branch, w_out=w_out, f2g=ffn2_w_gate, f2u=ffn2_w_up, f2d=ffn2_w_down)
    m_sh = dict(f1g=m_ffn1_w_gate, f1u=m_ffn1_w_up, f1d=m_ffn1_w_down, w_in=m_w_in, w_ab=m_w_attn_branch,
                w_gb=m_w_gmlp_branch, w_out=m_w_out, f2g=m_ffn2_w_gate, f2u=m_ffn2_w_up, f2d=m_ffn2_w_down)
    v_sh = dict(f1g=v_ffn1_w_gate, f1u=v_ffn1_w_up, f1d=v_ffn1_w_down, w_in=v_w_in, w_ab=v_w_attn_branch,
                w_gb=v_w_gmlp_branch, w_out=v_w_out, f2g=v_ffn2_w_gate, f2u=v_ffn2_w_up, f2d=v_ffn2_w_down)
    w_sh = {k: (v[0].T if k in TRANSPOSED else v[0]) for k, v in w_sh.items()}
    m_sh = {k: (v[0].T if k in TRANSPOSED else v[0]) for k, v in m_sh.items()}
    v_sh = {k: (v[0].T if k in TRANSPOSED else v[0]) for k, v in v_sh.items()}

    started, tokens = [], []
    for gi, names in enumerate(GROUPS):
        placed = [_place_shard(w_sh[k], KIND[k], pos, "place_" + k) for k in names]
        fulls, ssem, rsem, token = _gather_start(placed, [KIND[k] for k in names], [w_sh[k].shape for k in names],
                                                 tokens[-1:], "gather_start_g%d" % gi, both=gi in LATE_GROUPS)
        started.append((fulls, ssem, rsem))
        tokens.append(token)

    def weights_of(gi, after):
        names = GROUPS[gi]
        kinds, dims = [KIND[k] for k in names], [w_sh[k].shape for k in names]
        fulls, ssem, rsem = started[gi]
        fulls = _gather_wait(fulls, ssem, rsem, kinds, dims, list(after) + (tokens if gi == 0 else []),
                             "gather_wait_g%d" % gi, both=gi in LATE_GROUPS)
        if gi not in LATE_GROUPS:
            fulls = _gather_forward(fulls, kinds, dims, "gather_forward_g%d" % gi)
        return {k: (f.reshape(D, D) if k in ("w_gb", "w_out") else f) for k, f in zip(names, fulls)}

    pending, inflight = [], {}

    def grads_ready(gi, gd):
        grads = [gd[k] for k in GROUPS[gi]]
        lands = [lax.empty(g.shape[1:], F32) for g in grads]
        grads, lands, ssem, rsem, token = _pair_start(grads, lands, "rs_pair_start_g%d" % gi)
        pending.append((gi, grads, lands, ssem, rsem))
        return [token]

    def flush(after):
        gi, grads, lands, ssem, rsem = pending.pop()
        names = GROUPS[gi]
        grads, recv_a = _pair_wait(grads, lands, ssem, rsem, after, "rs_pair_wait_g%d" % gi)
        if len(names) > 1 and len({g.shape for g in grads}) == 1:
            psums = _pair_sum_group(grads, recv_a, pos, "rs_pair_sum_g%d" % gi)
        else:
            psums = [_pair_sum(g, r, pos, "rs_pair_sum_" + k) for g, r, k in zip(grads, recv_a, names)]
        lands = [lax.empty((3,) + p.shape[1:], BF16) for p in psums]
        psums, lands, ssem, rsem, token = _chip_start(psums, lands, "rs_chip_start_g%d" % gi)
        inflight[gi] = (grads, recv_a, psums, lands, ssem, rsem, token)
        return [token]

    P = dict(ln1_g=ln1_g, ln1_b=ln1_b, ln2_g=ln2_g, ln2_b=ln2_b, ln3_g=ln3_g, ln3_b=ln3_b, b_gates=b_gates,
             gmlp_ln_g=gmlp_ln_g, gmlp_ln_b=gmlp_ln_b, gmlp_w_s=gmlp_w_s[0], gmlp_b_s=gmlp_b_s[0])
    pos_f = positions.reshape(S, 1).astype(F32)
    small_state = []

    def small_ready(packed):
        land = jnp.zeros((8, SMALL_ROWS, D), F32)
        packed, land, ssem, rsem, token = _small_start(packed, land, "small_start")
        small_state.append((packed, land, ssem, rsem))
        return [token]

    loss_part, dx = _local_step(x[0], pos_f, loss_target[0], P, weights_of, grads_ready, flush, small_ready)
    loss = lax.psum(loss_part[0, 0], ("x", "y", "c"))

    g_out, d_out, m_out, v_out = {}, {}, {}, {}

    def finish(gis, after, tag):
        names, halves = [], []
        for gi in gis:
            grads, recv_a, psums, lands, ssem, rsem, token = inflight[gi]
            recv_b = _chip_wait(psums, lands, ssem, rsem, after + [inflight[0][6]], "rs_chip_wait_g%d" % gi)
            halves += [_owner_sum(g, ra, rb, pos, "rs_owner_sum_" + k)
                       for g, ra, rb, k in zip(grads, recv_a, recv_b, GROUPS[gi])]
            names += GROUPS[gi]
            after = halves[-1:]
        reduced = _sibling_allgather(halves, "rs_sibling_allgather_" + tag)
        for k, gfull in zip(names, reduced):
            res = _adamw(w_sh[k], gfull.reshape(w_sh[k].shape), m_sh[k], v_sh[k], "adamw_" + k)
            after = [res[1]]
            if k in TRANSPOSED:
                res = [r.T for r in res]
            g_out[k], d_out[k], m_out[k], v_out[k] = [r[None] for r in res]
        return after

    after = finish((3, 2, 1), [], "g321")

    small, parts = _small_wait(*small_state[0], after, "small_wait")
    sp = (ln1_g, ln1_b, gmlp_ln_g, gmlp_ln_b, ln2_g, ln2_b, ln3_g, ln3_b, b_gates, gmlp_b_s, gmlp_w_s)
    sm = (m_ln1_g, m_ln1_b, m_gmlp_ln_g, m_gmlp_ln_b, m_ln2_g, m_ln2_b, m_ln3_g, m_ln3_b, m_b_gates, m_gmlp_b_s,
          m_gmlp_w_s)
    sv = (v_ln1_g, v_ln1_b, v_gmlp_ln_g, v_gmlp_ln_b, v_ln2_g, v_ln2_b, v_ln3_g, v_ln3_b, v_b_gates, v_gmlp_b_s,
          v_gmlp_w_s)
    sg, sd, smn, svn = _small_sum_adamw(parts, small, pos, _pack_small(*sp), _pack_small(*sm), _pack_small(*sv),
                                        "small_adamw")
    names = ("ln1_g", "ln1_b", "gmlp_ln_g", "gmlp_ln_b", "ln2_g", "ln2_b", "ln3_g", "ln3_b", "b_gates", "gmlp_b_s",
             "gmlp_w_s")
    for dst, packed in ((g_out, sg), (d_out, sd), (m_out, smn), (v_out, svn)):
        for nm, val in zip(names, _unpack_small(packed)):
            dst[nm] = val
    finish((0,), [sg], "g0")

    order = ("f1g", "f1u", "f1d", "ln1_g", "ln1_b", "w_in", "b_gates", "gmlp_ln_g", "gmlp_ln_b", "gmlp_w_s", "gmlp_b_s",
             "w_ab", "w_gb", "w_out", "ln2_g", "ln2_b", "f2g", "f2u", "f2d", "ln3_g", "ln3_b")
    outs = [loss, dx[None]]
    for dst in (g_out, d_out, m_out, v_out):
        outs += [dst[k] for k in order]
    return tuple(outs)
```

```python
import jax
import jax.numpy as jnp
from jax import lax
from jax.experimental import pallas as pl
from jax.experimental.pallas import tpu as pltpu

F32 = jnp.float32
BF16 = jnp.bfloat16

S = 2048
D = 1024
NSH = 4
FSH = 704
ATT_W = 1536
GRP_W = 512
NG = 3
NH = 8
DH = 64
BLK = 128
NBLK = S // BLK
GW = 1024
IN_W = 8704
IN_SH = IN_W // NSH
ALPHA = 2.0 ** 0.25
LN_EPS = 1e-5
ROPE_THETA = 10000.0
DILATIONS = (1, 4, 16)
ADAM_LR, ADAM_B1, ADAM_B2, ADAM_EPS, ADAM_WD, ADAM_STEP = 0.001, 0.9, 0.999, 1e-08, 0.01, 10
SMALL_ROWS = 144
EPI_ROWS = 256
MESH_T = pl.DeviceIdType.MESH
MIB = 1024 * 1024
NEG_INF = float("-inf")


def _cp(sem, vmem_mib=48):
    return pltpu.CompilerParams(dimension_semantics=sem, vmem_limit_bytes=vmem_mib * MIB)


def _ln_stats(r):
    mu = jnp.mean(r, axis=-1, keepdims=True)
    xc = r - mu
    var = jnp.mean(xc * xc, axis=-1, keepdims=True)
    rstd = lax.rsqrt(var + LN_EPS)
    return xc * rstd, rstd


def _ln_dx(dxh, xh, rstd):
    m1 = jnp.mean(dxh, axis=-1, keepdims=True)
    m2 = jnp.mean(dxh * xh, axis=-1, keepdims=True)
    return rstd * (dxh - m1 - xh * m2)


def _dot_nt(a, b):
    return lax.dot_general(a, b, (((1,), (1,)), ((), ())), preferred_element_type=F32)


def _dot_tn(a, b):
    return lax.dot_general(a, b, (((0,), (0,)), ((), ())), preferred_element_type=F32)


def _dot(a, b):
    return jnp.dot(a, b, preferred_element_type=F32)


def _ffn_fwd(xin, wgt, wut, wd, ln_g, ln_b, name, emit_t=False, target=None):
    with_loss = target is not None
    tm = 1024

    def body(x_ref, wg_ref, wu_ref, wd_ref, g_ref, b_ref, *rest):
        if with_loss:
            t_ref, dr_ref, a_ref, bb_ref, dg_ref, db_ref, loss_ref, acc_ref = rest
        elif emit_t:
            hb_ref, xh_ref, rstd_ref, a_ref, bb_ref, ht_ref, acc_ref = rest
        else:
            hb_ref, xh_ref, rstd_ref, a_ref, bb_ref, acc_ref = rest
        i = pl.program_id(0)
        j = pl.program_id(1)
        xb = x_ref[...].astype(BF16)
        a = _dot_nt(xb, wg_ref[...])
        b = _dot_nt(xb, wu_ref[...])
        a_ref[...] = a.astype(BF16)
        bb_ref[...] = b.astype(BF16)
        s = (a * jax.nn.sigmoid(a)) * b
        f = _dot(s.astype(BF16), wd_ref[...])

        @pl.when(j == 0)
        def _():
            acc_ref[...] = f

        @pl.when(j > 0)
        def _():
            acc_ref[...] += f

        if with_loss:
            @pl.when(jnp.logical_and(j == NSH - 1, i == 0))
            def _():
                dg_ref[...] = jnp.zeros_like(dg_ref)
                db_ref[...] = jnp.zeros_like(db_ref)
                loss_ref[...] = jnp.zeros_like(loss_ref)

        @pl.when(j == NSH - 1)
        def _():
            for c0 in range(0, tm, EPI_ROWS):
                rows = slice(c0, c0 + EPI_ROWS)
                r = ALPHA * x_ref[rows, :] + 0.5 * acc_ref[rows, :]
                xh, rstd = _ln_stats(r)
                h = xh * g_ref[...] + b_ref[...]
                if with_loss:
                    err = h - t_ref[rows, :]
                    dy = err * (1.0 / D)
                    dr_ref[rows, :] = _ln_dx(dy * g_ref[...], xh, rstd)
                    dg_ref[...] += jnp.sum(dy * xh, axis=0, keepdims=True)
                    db_ref[...] += jnp.sum(dy, axis=0, keepdims=True)
                    part = 0.5 * jnp.sum(jnp.mean(err * err, axis=-1, keepdims=True), axis=0, keepdims=True)
                    loss_ref[...] += jnp.broadcast_to(part, (8, 128))
                else:
                    hb_ref[rows, :] = h.astype(BF16)
                    xh_ref[rows, :] = xh
                    rstd_ref[rows, :] = rstd
                    if emit_t:
                        ht_ref[:, rows] = h.T.astype(BF16)

    row = pl.BlockSpec((tm, D), lambda i, j: (i, 0))
    vec = pl.BlockSpec((1, D), lambda i, j: (0, 0))
    wsp = pl.BlockSpec((None, FSH, D), lambda i, j: (j, 0, 0))
    ab = pl.BlockSpec((None, tm, FSH), lambda i, j: (j, i, 0))
    ab_shape = jax.ShapeDtypeStruct((NSH, S, FSH), BF16)
    in_specs, args = [row, wsp, wsp, wsp, vec, vec], (xin, wgt, wut, wd, ln_g, ln_b)
    if with_loss:
        in_specs, args = in_specs + [row], args + (target,)
        out_specs = [row, ab, ab, vec, vec, pl.BlockSpec((8, 128), lambda i, j: (0, 0))]
        out_shape = [jax.ShapeDtypeStruct((S, D), F32), ab_shape, ab_shape, jax.ShapeDtypeStruct((1, D), F32),
                     jax.ShapeDtypeStruct((1, D), F32), jax.ShapeDtypeStruct((8, 128), F32)]
    else:
        out_specs = [row, row, pl.BlockSpec((tm, 1), lambda i, j: (i, 0)), ab, ab]
        out_shape = [jax.ShapeDtypeStruct((S, D), BF16), jax.ShapeDtypeStruct((S, D), F32),
                     jax.ShapeDtypeStruct((S, 1), F32), ab_shape, ab_shape]
        if emit_t:
            out_specs.append(pl.BlockSpec((D, tm), lambda i, j: (0, i)))
            out_shape.append(jax.ShapeDtypeStruct((D, S), BF16))
    return pl.pallas_call(
        body, name=name, grid=(S // tm, NSH), in_specs=in_specs, out_specs=out_specs, out_shape=out_shape,
        scratch_shapes=[pltpu.VMEM((tm, D), F32)],
        compiler_params=_cp(("arbitrary" if with_loss else "parallel", "arbitrary"), vmem_mib=56),
    )(*args)


def _ffn_bwd(dr, xin_b, a, b, wgt, wut, wd, name, after=()):
    tm = 512
    ni = S // tm
    hr = FSH // 2

    def body(dr_ref, a_ref, b_ref, wg_ref, wu_ref, wd_ref, x_hbm, *rest):
        dwg_hbm, dwu_hbm, dwd_hbm, dx_hbm, dx_acc, da_all, db_all, s_all, df_all, x_all, res_buf, sems = rest[len(after):]
        j = pl.program_id(0)
        i = pl.program_id(1)
        rows = pl.ds(pl.multiple_of(i * tm, tm), tm)

        @pl.when(jnp.logical_and(j == 0, i == 0))
        def _():
            cp = pltpu.make_async_copy(x_hbm, x_all, sems.at[0])
            cp.start()
            cp.wait()

        drv = dr_ref[...]
        df = (0.5 * drv).astype(BF16)

        @pl.when(j == 0)
        def _():
            df_all[rows, :] = df

        ds = jnp.concatenate([_dot_nt(df, wd_ref[0:384, :]), _dot_nt(df, wd_ref[384:FSH, :])], axis=1)
        av = a_ref[...].astype(F32)
        bv = b_ref[...].astype(F32)
        sig = jax.nn.sigmoid(av)
        sl = av * sig
        da = (ds * bv * (sig * (1.0 + av * (1.0 - sig)))).astype(BF16)
        db = (ds * sl).astype(BF16)
        da_all[rows, :] = da
        db_all[rows, :] = db
        s_all[rows, :] = (sl * bv).astype(BF16)
        dx = _dot(da, wg_ref[...]) + _dot(db, wu_ref[...])

        @pl.when(j == 0)
        def _():
            dx_acc[rows, :] = ALPHA * drv + dx

        @pl.when(j > 0)
        def _():
            dx_acc[rows, :] += dx

        @pl.when(i == ni - 1)
        def _():
            copies = []
            for n, (lhs, rhs, out) in enumerate(((da_all, x_all, dwg_hbm), (db_all, x_all, dwu_hbm),
                                                 (s_all, df_all, dwd_hbm))):
                slot = n % 2
                if n >= 2:
                    for cp in copies[2 * (n - 2): 2 * (n - 2) + 2]:
                        cp.wait()
                res_buf[slot] = _dot_tn(lhs[...], rhs[...])
                for h in range(2):
                    cp = pltpu.make_async_copy(res_buf.at[slot, pl.ds(h * hr, hr), :], out.at[h, j],
                                               sems.at[1 + 2 * slot + h])
                    cp.start()
                    copies.append(cp)
            for cp in copies[2:]:
                cp.wait()

        @pl.when(jnp.logical_and(j == NSH - 1, i == ni - 1))
        def _():
            cp = pltpu.make_async_copy(dx_acc, dx_hbm, sems.at[0])
            cp.start()
            cp.wait()

    row = pl.BlockSpec((tm, D), lambda j, i: (i, 0))
    wsp = pl.BlockSpec((None, FSH, D), lambda j, i: (j, 0, 0))
    ab = pl.BlockSpec((None, tm, FSH), lambda j, i: (j, i, 0))
    dwshape = jax.ShapeDtypeStruct((2, NSH, hr, D), F32)
    return pl.pallas_call(
        body, name=name, grid=(NSH, ni),
        in_specs=[row, ab, ab, wsp, wsp, wsp, ANY] + [ANY] * len(after),
        out_specs=[ANY, ANY, ANY, ANY],
        out_shape=[dwshape, dwshape, dwshape, jax.ShapeDtypeStruct((S, D), F32)],
        scratch_shapes=[pltpu.VMEM((S, D), F32), pltpu.VMEM((S, FSH), BF16), pltpu.VMEM((S, FSH), BF16),
                        pltpu.VMEM((S, FSH), BF16), pltpu.VMEM((S, D), BF16), pltpu.VMEM((S, D), BF16),
                        pltpu.VMEM((2, FSH, D), F32), pltpu.SemaphoreType.DMA((5,))],
        compiler_params=_cp(("arbitrary", "arbitrary"), vmem_mib=58),
    )(dr, a, b, wgt, wut, wd, xin_b, *after)


def _matmul(a, b, mode, name, *, n, tm, tn, b_col0=0, after=()):
    m, k = a.shape
    assert m % tm == 0 and n % tn == 0 and b_col0 % tn == 0
    off = b_col0 // tn
    na = len(after)

    def body(*refs):
        a_ref, b_ref, o_ref = refs[na:]
        av = a_ref[...].astype(BF16)
        o_ref[...] = _dot(av, b_ref[...]) if mode == "nn" else _dot_nt(av, b_ref[...])

    if mode == "nn":
        b_spec = pl.BlockSpec((k, tn), lambda i, j: (0, j + off))
    else:
        b_spec = pl.BlockSpec((tn, k), lambda i, j: (j, 0))
    return pl.pallas_call(
        body, name=name, grid=(m // tm, n // tn),
        in_specs=[pl.BlockSpec(memory_space=pl.ANY)] * na + [pl.BlockSpec((tm, k), lambda i, j: (i, 0)), b_spec],
        out_specs=pl.BlockSpec((tm, tn), lambda i, j: (i, j)),
        out_shape=jax.ShapeDtypeStruct((m, n), F32),
        compiler_params=_cp(("parallel", "parallel")),
    )(*after, a, b)


def _wgrad(xt, y, rh, c, name, row_sharded, after=()):
    na = len(after)
    if row_sharded:
        tc = 512

        def body(x_ref, y_ref, *rest):
            o_ref = rest[na]
            res = _dot(x_ref[...], y_ref[...].astype(BF16))
            for j in range(NSH):
                for h in range(2):
                    o_ref[h, j] = res[(2 * j + h) * rh:(2 * j + h + 1) * rh, :]

        grid = (c // tc,)
        in_specs = [pl.BlockSpec((2 * NSH * rh, S), lambda g: (0, 0)), pl.BlockSpec((S, tc), lambda g: (0, g))]
        out_specs = pl.BlockSpec((2, NSH, rh, tc), lambda g: (0, 0, 0, g))
        sem = ("parallel",)
    else:
        def body(x_ref, y_ref, *rest):
            rest[na][...] = _dot(x_ref[...], y_ref[...].astype(BF16))

        grid = (2, NSH)
        in_specs = [pl.BlockSpec((rh, S), lambda h, j: (h, 0)), pl.BlockSpec((S, c), lambda h, j: (0, j))]
        out_specs = pl.BlockSpec((None, None, rh, c), lambda h, j: (h, j, 0, 0))
        sem = ("parallel", "parallel")
    return pl.pallas_call(
        body, name=name, grid=grid, in_specs=in_specs + [pl.BlockSpec(memory_space=pl.ANY)] * na, out_specs=out_specs,
        out_shape=jax.ShapeDtypeStruct((2, NSH, rh, c), F32),
        compiler_params=_cp(sem, vmem_mib=56),
    )(xt, y, *after)


def _resid_ln(res_xh, res_g, res_b, a, w, ln_g, ln_b, name):
    tm = 512

    def body(rx_ref, rg_ref, rb_ref, a_ref, w_ref, g_ref, b_ref, h_ref, hb_ref, xh_ref, rstd_ref):
        r = ALPHA * (rx_ref[...] * rg_ref[...] + rb_ref[...]) + _dot(a_ref[...], w_ref[...])
        xh, rstd = _ln_stats(r)
        h = xh * g_ref[...] + b_ref[...]
        h_ref[...] = h
        hb_ref[...] = h.astype(BF16)
        xh_ref[...] = xh
        rstd_ref[...] = rstd

    row = pl.BlockSpec((tm, D), lambda i: (i, 0))
    vec = pl.BlockSpec((1, D), lambda i: (0, 0))
    return pl.pallas_call(
        body, name=name, grid=(S // tm,),
        in_specs=[row, vec, vec, row, pl.BlockSpec((D, D), lambda i: (0, 0)), vec, vec],
        out_specs=[row, row, row, pl.BlockSpec((tm, 1), lambda i: (i, 0))],
        out_shape=[jax.ShapeDtypeStruct((S, D), F32), jax.ShapeDtypeStruct((S, D), BF16),
                   jax.ShapeDtypeStruct((S, D), F32), jax.ShapeDtypeStruct((S, 1), F32)],
        compiler_params=_cp(("parallel",)),
    )(res_xh, res_g, res_b, a, w, ln_g, ln_b)


def _dh1_ln_bwd(dproj, w_in, dr2, xh, rstd, ln_g, name, after=()):
    tm, tk, ch = 1024, IN_SH, EPI_ROWS
    nk = IN_W // tk
    na = len(after)

    def body(*refs):
        a_ref, b_ref, add_ref, xh_ref, rstd_ref, g_ref, dr_ref, dg_ref, db_ref, acc_ref = refs[na:]
        i = pl.program_id(0)
        k = pl.program_id(1)
        p = _dot_nt(a_ref[...], b_ref[...])

        @pl.when(k == 0)
        def _():
            acc_ref[...] = p

        @pl.when(k > 0)
        def _():
            acc_ref[...] += p

        @pl.when(jnp.logical_and(k == nk - 1, i == 0))
        def _():
            dg_ref[...] = jnp.zeros_like(dg_ref)
            db_ref[...] = jnp.zeros_like(db_ref)

        @pl.when(k == nk - 1)
        def _():
            for c0 in range(0, tm, ch):
                rows = slice(c0, c0 + ch)
                dy = acc_ref[rows, :] + ALPHA * add_ref[rows, :]
                xhv = xh_ref[rows, :]
                dr_ref[rows, :] = _ln_dx(dy * g_ref[...], xhv, rstd_ref[rows, :])
                dg_ref[...] += jnp.sum(dy * xhv, axis=0, keepdims=True)
                db_ref[...] += jnp.sum(dy, axis=0, keepdims=True)

    row = pl.BlockSpec((tm, D), lambda i, k: (i, 0))
    vec = pl.BlockSpec((1, D), lambda i, k: (0, 0))
    return pl.pallas_call(
        body, name=name, grid=(S // tm, nk),
        in_specs=[pl.BlockSpec(memory_space=pl.ANY)] * na
        + [pl.BlockSpec((tm, tk), lambda i, k: (i, k)), pl.BlockSpec((D, tk), lambda i, k: (0, k)), row, row,
           pl.BlockSpec((tm, 1), lambda i, k: (i, 0)), vec],
        out_specs=[row, vec, vec],
        out_shape=[jax.ShapeDtypeStruct((S, D), F32), jax.ShapeDtypeStruct((1, D), F32),
                   jax.ShapeDtypeStruct((1, D), F32)],
        scratch_shapes=[pltpu.VMEM((tm, D), F32)],
        compiler_params=_cp(("arbitrary", "arbitrary"), vmem_mib=56),
    )(*after, dproj, w_in, dr2, xh, rstd, ln_g)


def _ln_bwd(dout, xh, rstd, ln_g, name, after=()):
    tm = 512
    na = len(after)

    def body(*refs):
        y_ref, xh_ref, rstd_ref, g_ref, dr_ref, dg_ref, db_ref = refs[na:]
        dy = y_ref[...]
        i = pl.program_id(0)
        xh = xh_ref[...]
        dr_ref[...] = _ln_dx(dy * g_ref[...], xh, rstd_ref[...])
        dg = jnp.sum(dy * xh, axis=0, keepdims=True)
        db = jnp.sum(dy, axis=0, keepdims=True)

        @pl.when(i == 0)
        def _():
            dg_ref[...] = dg
            db_ref[...] = db

        @pl.when(i > 0)
        def _():
            dg_ref[...] += dg
            db_ref[...] += db

    row = pl.BlockSpec((tm, D), lambda i: (i, 0))
    vec = pl.BlockSpec((1, D), lambda i: (0, 0))
    return pl.pallas_call(
        body, name=name, grid=(S // tm,),
        in_specs=[pl.BlockSpec(memory_space=pl.ANY)] * na + [row, row, pl.BlockSpec((tm, 1), lambda i: (i, 0)), vec],
        out_specs=[row, vec, vec],
        out_shape=[jax.ShapeDtypeStruct((S, D), F32), jax.ShapeDtypeStruct((1, D), F32),
                   jax.ShapeDtypeStruct((1, D), F32)],
        compiler_params=_cp(("arbitrary",)),
    )(*after, dout, xh, rstd, ln_g)


ROPE_TM = 256


def _rope_tables(pos_ref, invf_ref, sign):
    ang = pos_ref[...] * invf_ref[...]
    lane = lax.broadcasted_iota(jnp.int32, ang.shape, 1)
    first = (lane % DH) < (DH // 2)
    sinv = jnp.sin(ang) * sign
    return first, jnp.cos(ang), jnp.where(first, -sinv, sinv)


def _rotate(x, first, cosf, sinf):
    return x * cosf + jnp.where(first, pltpu.roll(x, 96, 1), pltpu.roll(x, 32, 1)) * sinf


def _proj_qkv_rope(hb, w_in, pos_f, invf, name):
    tm = 2 * ROPE_TM

    def body(h_ref, w_ref, pos_ref, invf_ref, o0_ref, o1_ref, o2_ref, buf_ref):
        rot = pl.program_id(1) < 2
        first, cosf, sinf = _rope_tables(pos_ref, invf_ref, 1.0)
        cosf = jnp.where(rot, cosf, 1.0)
        sinf = jnp.where(rot, sinf, 0.0)
        acc = _dot(h_ref[...], w_ref[...])
        for gi, (d, o_ref) in enumerate(zip(DILATIONS, (o0_ref, o1_ref, o2_ref))):
            for ch in range(GRP_W // 128):
                cols = slice(ch * 128, (ch + 1) * 128)
                x = _rotate(acc[:, gi * GRP_W + ch * 128: gi * GRP_W + (ch + 1) * 128], first, cosf, sinf)
                if d == 1:
                    o_ref[0, :, cols] = x.astype(BF16)
                else:
                    buf_ref[...] = x
                    for r in range(d):
                        o_ref[r, :, cols] = buf_ref[pl.ds(r, tm // d, stride=d), :].astype(BF16)

    return pl.pallas_call(
        body, name=name, grid=(S // tm, 3),
        in_specs=[pl.BlockSpec((tm, D), lambda i, s: (i, 0)), pl.BlockSpec((D, ATT_W), lambda i, s: (0, s)),
                  pl.BlockSpec((tm, 1), lambda i, s: (i, 0)), pl.BlockSpec((1, 128), lambda i, s: (0, 0))],
        out_specs=[pl.BlockSpec((d, tm // d, GRP_W), lambda i, s: (0, i, s)) for d in DILATIONS],
        out_shape=[jax.ShapeDtypeStruct((d, S // d, 3 * GRP_W), BF16) for d in DILATIONS],
        scratch_shapes=[pltpu.VMEM((tm, 128), F32)],
        compiler_params=_cp(("parallel", "parallel")),
    )(hb, w_in, pos_f, invf)


def _rope_bwd(dqkv_c, dz, dgl, pos_f, invf, name):
    tm = ROPE_TM

    def body(*refs):
        g_refs, (dz_ref, dgl_ref, pos_ref, invf_ref, o_ref, buf_ref) = refs[:9], refs[9:]
        o_ref[:, 3 * ATT_W:3 * ATT_W + 2 * GW] = dz_ref[...]
        o_ref[:, 3 * ATT_W + 2 * GW:IN_W] = dgl_ref[...]
        first, cosf, sinf = _rope_tables(pos_ref, invf_ref, -1.0)
        for sec in range(3):
            for gi, d in enumerate(DILATIONS):
                g_ref = g_refs[3 * gi + sec]
                for ch in range(GRP_W // 128):
                    cols = slice(ch * 128, (ch + 1) * 128)
                    if d == 1:
                        x = g_ref[0, :, cols]
                    else:
                        for r in range(d):
                            buf_ref[pl.ds(r, tm // d, stride=d), :] = g_ref[r, :, cols]
                        x = buf_ref[...]
                    if sec < 2:
                        x = _rotate(x, first, cosf, sinf)
                    dst = sec * ATT_W + gi * GRP_W + ch * 128
                    o_ref[:, dst:dst + 128] = x.astype(BF16)

    g_specs = [pl.BlockSpec((d, tm // d, GRP_W), lambda i: (0, i, 0)) for d in DILATIONS for _ in range(3)]
    return pl.pallas_call(
        body, name=name, grid=(S // tm,),
        in_specs=g_specs + [pl.BlockSpec((tm, 2 * GW), lambda i: (i, 0)), pl.BlockSpec((tm, 2 * D), lambda i: (i, 0)),
                            pl.BlockSpec((tm, 1), lambda i: (i, 0)), pl.BlockSpec((1, 128), lambda i: (0, 0))],
        out_specs=pl.BlockSpec((tm, IN_W), lambda i: (i, 0)),
        out_shape=jax.ShapeDtypeStruct((S, IN_W), BF16),
        scratch_shapes=[pltpu.VMEM((tm, 128), F32)],
        compiler_params=_cp(("parallel",)),
    )(*[g for grp in dqkv_c for g in grp], dz, dgl, pos_f, invf)


def _class_order(ts, name):
    tm = ROPE_TM
    n = len(ts)

    def body(*refs):
        buf_ref = refs[3 * n]
        for a in range(n):
            for ch in range(GRP_W // 128):
                cols = slice(ch * 128, (ch + 1) * 128)
                buf_ref[...] = refs[a][:, cols]
                for b, d in enumerate(DILATIONS[1:]):
                    for r in range(d):
                        refs[n + 2 * a + b][r, :, cols] = buf_ref[pl.ds(r, tm // d, stride=d), :]

    return pl.pallas_call(
        body, name=name, grid=(S // tm,),
        in_specs=[pl.BlockSpec((tm, GRP_W), lambda i: (i, 0))] * n,
        out_specs=[pl.BlockSpec((d, tm // d, GRP_W), lambda i: (0, i, 0)) for _ in range(n) for d in DILATIONS[1:]],
        out_shape=[jax.ShapeDtypeStruct((d, S // d, GRP_W), F32) for _ in range(n) for d in DILATIONS[1:]],
        scratch_shapes=[pltpu.VMEM((tm, 128), F32)],
        compiler_params=_cp(("parallel",)),
    )(*ts)


def _own_lanes(h):
    return (lax.broadcasted_iota(jnp.int32, (1, 2 * DH), 1) // DH) == (h % 2)


def _heads(ref):
    out = []
    for h in range(NH):
        pair = ref[:, (h // 2) * 2 * DH:(h // 2 + 1) * 2 * DH]
        out.append(jnp.where(_own_lanes(h), pair, jnp.zeros_like(pair)))
    return jnp.stack(out)


def _unheads(t3):
    return jnp.concatenate([t3[2 * p] + t3[2 * p + 1] for p in range(NH // 2)], axis=1)


def _bdot_nt(a, b):
    return lax.dot_general(a, b, (((2,), (2,)), ((0,), (0,))), preferred_element_type=F32)


def _bdot(a, b):
    return lax.dot_general(a, b, (((2,), (1,)), ((0,), (0,))), preferred_element_type=F32)


def _bdot_tn(a, b):
    return lax.dot_general(a, b, (((1,), (1,)), ((0,), (0,))), preferred_element_type=F32)


def _attn_fwd(gi, qkv_c, name):
    d = DILATIONS[gi]
    nblk = S // d // BLK

    def body(*refs):
        if nblk > 1:
            q_ref, kc_ref, kp_ref, vc_ref, vp_ref, o_ref, lse_ref = refs
            has_prev = pl.program_id(1) != 0
        else:
            q_ref, kc_ref, vc_ref, o_ref, lse_ref = refs
        qi = lax.broadcasted_iota(jnp.int32, (NH, BLK, BLK), 1)
        kj = lax.broadcasted_iota(jnp.int32, (NH, BLK, BLK), 2)
        q = _heads(q_ref)
        sc = jnp.where(kj <= qi, _bdot_nt(q, _heads(kc_ref)) * 0.125, NEG_INF)
        m = jnp.max(sc, axis=-1, keepdims=True)
        if nblk > 1:
            mask_p = jnp.logical_and(kj >= qi, has_prev)
            sp = jnp.where(mask_p, _bdot_nt(q, _heads(kp_ref)) * 0.125, NEG_INF)
            m = jnp.maximum(m, jnp.max(sp, axis=-1, keepdims=True))
        pc = jnp.exp(sc - m)
        l = jnp.sum(pc, axis=-1, keepdims=True)
        o = _bdot(pc.astype(BF16), _heads(vc_ref))
        if nblk > 1:
            pp = jnp.exp(sp - m)
            l = l + jnp.sum(pp, axis=-1, keepdims=True)
            o = o + _bdot(pp.astype(BF16), _heads(vp_ref))
        o_ref[...] = _unheads(o / l)
        lse = jnp.broadcast_to(m + jnp.log(l), (NH, BLK, 2 * DH))
        lse_ref[...] = _unheads(jnp.stack([jnp.where(_own_lanes(h), lse[h], 0.0) for h in range(NH)]))

    def cur(sec):
        return pl.BlockSpec((None, BLK, GRP_W), lambda r, n: (r, n, sec))

    def prev(sec):
        return pl.BlockSpec((None, BLK, GRP_W), lambda r, n: (r, jnp.maximum(n - 1, 0), sec))

    out = pl.BlockSpec((None, BLK, GRP_W), lambda r, n: (r, n, 0))
    shp = jax.ShapeDtypeStruct((d, S // d, GRP_W), F32)
    if nblk > 1:
        in_specs, args = [cur(0), cur(1), prev(1), cur(2), prev(2)], (qkv_c,) * 5
    else:
        in_specs, args = [cur(0), cur(1), cur(2)], (qkv_c,) * 3
    return pl.pallas_call(
        body, name=name, grid=(d, nblk), in_specs=in_specs, out_specs=[out, out], out_shape=[shp, shp],
        compiler_params=_cp(("parallel", "parallel")),
    )(*args)


def _attn_combine(os, lses, name):
    tm = ROPE_TM

    def body(o0_ref, o1_ref, o2_ref, l0_ref, l1_ref, l2_ref, y_ref, yt_ref, l_ref, buf_ref):
        def token_order(ref, d, cols, slot):
            if d == 1:
                return ref[0, :, cols]
            for r in range(d):
                buf_ref[slot, pl.ds(r, tm // d, stride=d), :] = ref[r, :, cols]
            return buf_ref[slot]

        for ch in range(GRP_W // 128):
            cols = slice(ch * 128, (ch + 1) * 128)
            o = [token_order(ref, d, cols, k) for k, (ref, d) in enumerate(zip((o0_ref, o1_ref, o2_ref), DILATIONS))]
            ls = [token_order(ref, d, cols, 3 + k)
                  for k, (ref, d) in enumerate(zip((l0_ref, l1_ref, l2_ref), DILATIONS))]
            m = jnp.maximum(jnp.maximum(ls[0], ls[1]), ls[2])
            e = [jnp.exp(l - m) for l in ls]
            den = e[0] + e[1] + e[2]
            y = (e[0] * o[0] + e[1] * o[1] + e[2] * o[2]) / den
            y_ref[:, cols] = y
            yt_ref[cols, :] = y.T.astype(BF16)
            l_ref[:, cols] = m + jnp.log(den)

    blk = pl.BlockSpec((tm, GRP_W), lambda i: (i, 0))
    cls = [pl.BlockSpec((d, tm // d, GRP_W), lambda i: (0, i, 0)) for d in DILATIONS]
    shp = jax.ShapeDtypeStruct((S, GRP_W), F32)
    return pl.pallas_call(
        body, name=name, grid=(S // tm,), in_specs=cls + cls,
        out_specs=[blk, pl.BlockSpec((GRP_W, tm), lambda i: (0, i)), blk],
        out_shape=[shp, jax.ShapeDtypeStruct((GRP_W, S), BF16), shp],
        scratch_shapes=[pltpu.VMEM((6, tm, 128), F32)],
        compiler_params=_cp(("parallel",)),
    )(*os, *lses)


def _attn_bwd(gi, qkv_c, dy_c, y_c, lse_c, name):
    d = DILATIONS[gi]
    nblk = S // d // BLK

    def body(*refs):
        if nblk > 1:
            (q_ref, qn_ref, k_ref, kp_ref, v_ref, vp_ref, dy_ref, dyn_ref, y_ref, yn_ref, l_ref, ln_ref,
             dq_ref, dk_ref, dv_ref) = refs
            n = pl.program_id(1)
            has_prev = n != 0
            has_next = n != nblk - 1
        else:
            q_ref, k_ref, v_ref, dy_ref, y_ref, l_ref, dq_ref, dk_ref, dv_ref = refs
        qi = lax.broadcasted_iota(jnp.int32, (NH, BLK, BLK), 1)
        kj = lax.broadcasted_iota(jnp.int32, (NH, BLK, BLK), 2)

        def lse_col(ref):
            return jnp.stack([ref[:, h * DH:h * DH + 1] for h in range(NH)])

        q, k, v = _heads(q_ref), _heads(k_ref), _heads(v_ref)
        dy = _heads(dy_ref)
        dd = jnp.sum(dy * _heads(y_ref), axis=-1, keepdims=True)
        lcol = lse_col(l_ref)
        dyb = dy.astype(BF16)
        p = jnp.exp(jnp.where(kj <= qi, _bdot_nt(q, k) * 0.125, NEG_INF) - lcol)
        ds = (p * (_bdot_nt(dyb, v) - dd)).astype(BF16)
        dq = _bdot(ds, k)
        dk = _bdot_tn(ds, q)
        dv = _bdot_tn(p.astype(BF16), dyb)
        if nblk > 1:
            qn, kpv, vpv = _heads(qn_ref), _heads(kp_ref), _heads(vp_ref)
            dyn = _heads(dyn_ref)
            ddn = jnp.sum(dyn * _heads(yn_ref), axis=-1, keepdims=True)
            lncol = lse_col(ln_ref)
            dynb = dyn.astype(BF16)
            mask_p = jnp.logical_and(kj >= qi, has_prev)
            pp = jnp.exp(jnp.where(mask_p, _bdot_nt(q, kpv) * 0.125, NEG_INF) - lcol)
            dsp = (pp * (_bdot_nt(dyb, vpv) - dd)).astype(BF16)
            dq = dq + _bdot(dsp, kpv)
            mask_n = jnp.logical_and(kj >= qi, has_next)
            pn = jnp.exp(jnp.where(mask_n, _bdot_nt(qn, k) * 0.125, NEG_INF) - lncol)
            dsn = (pn * (_bdot_nt(dynb, v) - ddn)).astype(BF16)
            dk = dk + _bdot_tn(dsn, qn)
            dv = dv + _bdot_tn(pn.astype(BF16), dynb)
        dq_ref[...] = _unheads(dq) * 0.125
        dk_ref[...] = _unheads(dk) * 0.125
        dv_ref[...] = _unheads(dv)

    def spec(sec, shift):
        def idx(r, n):
            return (r, jnp.clip(n + shift, 0, nblk - 1), sec)
        return pl.BlockSpec((None, BLK, GRP_W), idx)

    if nblk > 1:
        in_specs = [spec(0, 0), spec(0, 1), spec(1, 0), spec(1, -1), spec(2, 0), spec(2, -1),
                    spec(0, 0), spec(0, 1), spec(0, 0), spec(0, 1), spec(0, 0), spec(0, 1)]
        args = (qkv_c,) * 6 + (dy_c, dy_c, y_c, y_c, lse_c, lse_c)
    else:
        in_specs = [spec(0, 0), spec(1, 0), spec(2, 0), spec(0, 0), spec(0, 0), spec(0, 0)]
        args = (qkv_c, qkv_c, qkv_c, dy_c, y_c, lse_c)
    out = spec(0, 0)
    shp = jax.ShapeDtypeStruct((d, S // d, GRP_W), F32)
    return pl.pallas_call(
        body, name=name, grid=(d, nblk), in_specs=in_specs, out_specs=[out, out, out], out_shape=[shp, shp, shp],
        compiler_params=_cp(("parallel", "parallel")),
    )(*args)


_SQRT_HALF = 0.7071067811865476
_INV_SQRT_2PI = 0.3989422804014327


def _gelu(z):
    return 0.5 * z * (1.0 + lax.erf(z * _SQRT_HALF))


def _gelu_grad(z):
    return 0.5 * (1.0 + lax.erf(z * _SQRT_HALF)) + z * (jnp.exp(-0.5 * z * z) * _INV_SQRT_2PI)


def _tril_mask():
    t = lax.broadcasted_iota(jnp.int32, (BLK, BLK), 0)
    s = lax.broadcasted_iota(jnp.int32, (BLK, BLK), 1)
    return s <= t


def _groups(t):
    return jnp.stack([t[:, g * BLK:(g + 1) * BLK] for g in range(8)])


def _ungroup(t3):
    return jnp.concatenate([t3[g] for g in range(8)], axis=1)


def _group_bias(bs_ref):
    return jnp.stack([bs_ref[:, g:g + 1] for g in range(8)])


def _gmlp_fwd(z, ln_g, ln_b, w_s, b_s_t, name):
    def body(z_ref, g_ref, b_ref, ws_ref, bs_ref, y_ref, yt_ref):
        zg = _gelu(z_ref[...])
        u = zg[:, :GW]
        xh, _ = _ln_stats(zg[:, GW:])
        vn = (xh * g_ref[...] + b_ref[...]).astype(BF16)
        wt = jnp.where(_tril_mask(), ws_ref[...], 0.0).astype(BF16)
        yv = u * _ungroup(_bdot(wt, _groups(vn)) + _group_bias(bs_ref))
        y_ref[...] = yv.astype(BF16)
        yt_ref[...] = yv.T.astype(BF16)

    vec = pl.BlockSpec((1, GW), lambda n: (0, 0))
    return pl.pallas_call(
        body, name=name, grid=(NBLK,),
        in_specs=[pl.BlockSpec((BLK, 2 * GW), lambda n: (n, 0)), vec, vec,
                  pl.BlockSpec((8, BLK, BLK), lambda n: (0, 0, 0)), pl.BlockSpec((BLK, 8), lambda n: (0, 0))],
        out_specs=[pl.BlockSpec((BLK, GW), lambda n: (n, 0)), pl.BlockSpec((GW, BLK), lambda n: (0, n))],
        out_shape=[jax.ShapeDtypeStruct((S, GW), BF16), jax.ShapeDtypeStruct((GW, S), BF16)],
        compiler_params=_cp(("parallel",)),
    )(z, ln_g, ln_b, w_s, b_s_t)


def _gmlp_bwd(z, dy, ln_g, ln_b, w_s, b_s_t, name):
    def body(z_ref, dy_ref, g_ref, b_ref, ws_ref, bs_ref, dz_ref, dws_ref, dbs_ref, dg_ref, db_ref, dvn_ref):
        n = pl.program_id(0)
        zv = z_ref[...]
        zg = _gelu(zv)
        u = zg[:, :GW]
        xh, rstd = _ln_stats(zg[:, GW:])
        vn = (xh * g_ref[...] + b_ref[...]).astype(BF16)
        tril = _tril_mask()

        @pl.when(n == 0)
        def _():
            dws_ref[...] = jnp.zeros_like(dws_ref)
            dbs_ref[...] = jnp.zeros_like(dbs_ref)
            dg_ref[...] = jnp.zeros_like(dg_ref)
            db_ref[...] = jnp.zeros_like(db_ref)

        wt = jnp.where(tril, ws_ref[...], 0.0).astype(BF16)
        vn3 = _groups(vn)
        dyv = dy_ref[...]
        mixed = _ungroup(_bdot(wt, vn3) + _group_bias(bs_ref))
        dz_ref[:, :GW] = (dyv * mixed * _gelu_grad(zv[:, :GW])).astype(BF16)
        dmix3 = _groups(dyv * u)
        dmb = dmix3.astype(BF16)
        dws_ref[...] += jnp.where(tril, _bdot_nt(dmb, vn3), 0.0)
        dbsum = jnp.sum(dmix3, axis=-1, keepdims=True)
        for gg in range(8):
            dbs_ref[:, gg:gg + 1] += dbsum[gg]
        dvn_ref[...] = _ungroup(_bdot_tn(wt, dmb))

        dvn = dvn_ref[...]
        dg_ref[...] += jnp.sum(dvn * xh, axis=0, keepdims=True)
        db_ref[...] += jnp.sum(dvn, axis=0, keepdims=True)
        dvg = _ln_dx(dvn * g_ref[...], xh, rstd)
        dz_ref[:, GW:] = (dvg * _gelu_grad(zv[:, GW:])).astype(BF16)

    vec = pl.BlockSpec((1, GW), lambda n: (0, 0))
    ws = pl.BlockSpec((8, BLK, BLK), lambda n: (0, 0, 0))
    bs = pl.BlockSpec((BLK, 8), lambda n: (0, 0))
    return pl.pallas_call(
        body, name=name, grid=(NBLK,),
        in_specs=[pl.BlockSpec((BLK, 2 * GW), lambda n: (n, 0)), pl.BlockSpec((BLK, GW), lambda n: (n, 0)),
                  vec, vec, ws, bs],
        out_specs=[pl.BlockSpec((BLK, 2 * GW), lambda n: (n, 0)), ws, bs, vec, vec],
        out_shape=[jax.ShapeDtypeStruct((S, 2 * GW), BF16), jax.ShapeDtypeStruct((8, BLK, BLK), F32),
                   jax.ShapeDtypeStruct((BLK, 8), F32), jax.ShapeDtypeStruct((1, GW), F32),
                   jax.ShapeDtypeStruct((1, GW), F32)],
        scratch_shapes=[pltpu.VMEM((BLK, GW), F32)],
        compiler_params=_cp(("arbitrary",)),
    )(z, dy, ln_g, ln_b, w_s, b_s_t)


def _merge_fwd(a, b, gl, b_gates, name):
    tm = 512

    def body(a_ref, b_ref, g0_ref, g1_ref, bg_ref, o_ref, ot_ref):
        g0 = jax.nn.sigmoid(g0_ref[...] + bg_ref[:, :D])
        g1 = jax.nn.sigmoid(g1_ref[...] + bg_ref[:, D:])
        mg = g0 * a_ref[...] + g1 * b_ref[...]
        o_ref[...] = mg.astype(BF16)
        ot_ref[...] = mg.T.astype(BF16)

    row = pl.BlockSpec((tm, D), lambda i: (i, 0))
    return pl.pallas_call(
        body, name=name, grid=(S // tm,),
        in_specs=[row, row, row, pl.BlockSpec((tm, D), lambda i: (i, 1)), pl.BlockSpec((1, 2 * D), lambda i: (0, 0))],
        out_specs=[row, pl.BlockSpec((D, tm), lambda i: (0, i))],
        out_shape=[jax.ShapeDtypeStruct((S, D), BF16), jax.ShapeDtypeStruct((D, S), BF16)],
        compiler_params=_cp(("parallel",)),
    )(a, b, gl, gl, b_gates)


def _merge_bwd(dm, a, b, gl, b_gates, name):
    tm = 512

    def body(dm_ref, a_ref, b_ref, g0_ref, g1_ref, bg_ref, da_ref, db_ref, dgl_ref, dbg_ref):
        i = pl.program_id(0)
        dmv = dm_ref[...]
        g0 = jax.nn.sigmoid(g0_ref[...] + bg_ref[:, :D])
        g1 = jax.nn.sigmoid(g1_ref[...] + bg_ref[:, D:])
        da_ref[...] = (dmv * g0).astype(BF16)
        db_ref[...] = (dmv * g1).astype(BF16)
        d0 = dmv * a_ref[...] * g0 * (1.0 - g0)
        d1 = dmv * b_ref[...] * g1 * (1.0 - g1)
        dgl_ref[:, :D] = d0.astype(BF16)
        dgl_ref[:, D:] = d1.astype(BF16)
        s0 = jnp.sum(d0, axis=0, keepdims=True)
        s1 = jnp.sum(d1, axis=0, keepdims=True)

        @pl.when(i == 0)
        def _():
            dbg_ref[:, :D] = s0
            dbg_ref[:, D:] = s1

        @pl.when(i > 0)
        def _():
            dbg_ref[:, :D] += s0
            dbg_ref[:, D:] += s1

    row = pl.BlockSpec((tm, D), lambda i: (i, 0))
    wide = pl.BlockSpec((tm, 2 * D), lambda i: (i, 0))
    bg = pl.BlockSpec((1, 2 * D), lambda i: (0, 0))
    return pl.pallas_call(
        body, name=name, grid=(S // tm,),
        in_specs=[row, row, row, row, pl.BlockSpec((tm, D), lambda i: (i, 1)), bg],
        out_specs=[row, row, wide, bg],
        out_shape=[jax.ShapeDtypeStruct((S, D), BF16), jax.ShapeDtypeStruct((S, D), BF16),
                   jax.ShapeDtypeStruct((S, 2 * D), BF16), jax.ShapeDtypeStruct((1, 2 * D), F32)],
        compiler_params=_cp(("arbitrary",)),
    )(dm, a, b, gl, gl, b_gates)


def _adam_math(w, g, m, v):
    m2 = ADAM_B1 * m + (1.0 - ADAM_B1) * g
    v2 = ADAM_B2 * v + (1.0 - ADAM_B2) * (g * g)
    m_hat = m2 / (1.0 - ADAM_B1 ** ADAM_STEP)
    v_hat = v2 / (1.0 - ADAM_B2 ** ADAM_STEP)
    delta = -ADAM_LR * (m_hat / (jnp.sqrt(v_hat) + ADAM_EPS) + ADAM_WD * w)
    return delta, m2, v2


def _pick_rows(rows, cols, unit=16, budget=2 * MIB):
    best = unit
    for t in range(unit, rows + 1, unit):
        if rows % t == 0 and t * cols * 4 <= budget:
            best = t
    assert rows % best == 0
    return best


def _adamw(w, g, m, v, name):
    r, c = w.shape
    tr = _pick_rows(r, c, unit=8, budget=MIB)

    def body(w_ref, g_ref, m_ref, v_ref, go_ref, d_ref, mo_ref, vo_ref):
        gv = g_ref[...]
        delta, m2, v2 = _adam_math(w_ref[...], gv, m_ref[...], v_ref[...])
        go_ref[...] = gv
        d_ref[...] = delta
        mo_ref[...] = m2
        vo_ref[...] = v2

    blk = pl.BlockSpec((tr, c), lambda i: (i, 0))
    shp = jax.ShapeDtypeStruct((r, c), F32)
    return pl.pallas_call(
        body, name=name, grid=(r // tr,), in_specs=[blk] * 4, out_specs=[blk] * 4, out_shape=[shp] * 4,
        compiler_params=_cp(("parallel",)),
    )(*[pltpu.with_memory_space_constraint(t, pltpu.HBM) for t in (w, g, m, v)])


def _small_sum_adamw(parts, own, pos, w, m, v, name):
    tr = 48

    def body(pos_ref, p_ref, own_ref, w_ref, m_ref, v_ref, g_ref, d_ref, mo_ref, vo_ref):
        me = 2 * pos_ref[1] + pos_ref[0]
        gv = None
        for k in range(8):
            term = jnp.where(me == k, own_ref[...], p_ref[k])
            gv = term if gv is None else gv + term
        delta, m2, v2 = _adam_math(w_ref[...], gv, m_ref[...], v_ref[...])
        g_ref[...] = gv
        d_ref[...] = delta
        mo_ref[...] = m2
        vo_ref[...] = v2

    blk = pl.BlockSpec((tr, D), lambda i, p: (i, 0))
    shp = jax.ShapeDtypeStruct((SMALL_ROWS, D), F32)
    return pl.pallas_call(
        body, name=name,
        grid_spec=pltpu.PrefetchScalarGridSpec(
            num_scalar_prefetch=1, grid=(SMALL_ROWS // tr,),
            in_specs=[pl.BlockSpec((8, tr, D), lambda i, p: (0, i, 0)), blk, blk, blk, blk],
            out_specs=[blk] * 4),
        out_shape=[shp] * 4,
        compiler_params=_cp(("parallel",)),
    )(pos, parts, own, w, m, v)


ANY = pl.BlockSpec(memory_space=pl.ANY)


def _in_hbm(arrays):
    return [pltpu.with_memory_space_constraint(a, pltpu.HBM) for a in arrays]


def _mesh_pos():
    x, y, c = lax.axis_index("x"), lax.axis_index("y"), lax.axis_index("c")
    chips = [(1 - x, y), (x, 1 - y), (1 - x, 1 - y)]
    return x, y, c, chips


def _place_shard(w, kind, pos, name):
    r, c = w.shape
    tr = _pick_rows(r, c)

    def body(pos_ref, w_ref, o_ref):
        o_ref[...] = w_ref[...].astype(BF16)

    if kind == "stack":
        o_spec = pl.BlockSpec((None, tr, c), lambda i, p: (p[1], i, 0))
        shape = (NSH, r, c)
    else:
        o_spec = pl.BlockSpec((tr, c), lambda i, p: (i, p[1]))
        shape = (r, NSH * c)
    return pl.pallas_call(
        body, name=name,
        grid_spec=pltpu.PrefetchScalarGridSpec(
            num_scalar_prefetch=1, grid=(r // tr,),
            in_specs=[pl.BlockSpec((tr, c), lambda i, p: (i, 0))], out_specs=o_spec),
        out_shape=pltpu.HBM(shape, BF16),
        compiler_params=_cp(("parallel",)),
    )(pos, pltpu.with_memory_space_constraint(w, pltpu.HBM))


SEM = pl.BlockSpec(memory_space=pltpu.SEMAPHORE)
SPLIT_COPY = pltpu.CompilerParams(has_side_effects=pltpu.SideEffectType.DATAFLOW_SIDE_EFFECTING)


def _shard_window(ref, kind, j, h, dims):
    r, c = dims
    rows = pl.ds(pl.multiple_of(h * (r // 2), 16), r // 2)
    if kind == "stack":
        return ref.at[j, rows, :]
    return ref.at[rows, pl.ds(pl.multiple_of(j * c, 128), c)]


def _ici_copy(ref, kind, dims, j, c, sems, idx, to):
    win = _shard_window(ref, kind, j, c, dims)
    return pltpu.make_async_remote_copy(src_ref=win, dst_ref=win, send_sem=sems[0].at[idx], recv_sem=sems[1].at[idx],
                                        device_id=to, device_id_type=MESH_T)


def _both_copy(ref, kind, dims, a, k, chip, half, tc, sc, sems):
    win = _shard_window(ref, kind, half[0], half[1], dims)
    return pltpu.make_async_remote_copy(src_ref=win, dst_ref=win, send_sem=sems[0].at[6 * a + 2 * k + tc],
                                        recv_sem=sems[1].at[6 * a + 2 * k + sc],
                                        device_id=(chip[0], chip[1], tc), device_id_type=MESH_T)


def _gather_start(fulls, kinds, dims, after, name, both=False):
    n, na = len(fulls), len(after)
    per = 6 if both else 3

    def body(*refs):
        outs = refs[n + na:2 * n + na]
        send_sems, recv_sems, token = refs[2 * n + na:]
        x, y, c, chips = _mesh_pos()
        for a in range(n):
            for k, chip in enumerate(chips):
                if both:
                    for tc in range(2):
                        _both_copy(outs[a], kinds[a], dims[a], a, k, chip, (2 * x + y, c), tc, c,
                                   (send_sems, recv_sems)).start()
                else:
                    _ici_copy(outs[a], kinds[a], dims[a], 2 * x + y, c, (send_sems, recv_sems), 3 * a + k,
                              (chip[0], chip[1], c)).start()
        token[...] = jnp.zeros_like(token)

    res = pl.pallas_call(
        body, name=name, in_specs=[ANY] * (n + na),
        out_specs=[ANY] * n + [SEM, SEM, pl.BlockSpec(memory_space=pltpu.VMEM)],
        out_shape=[pltpu.HBM(f.shape, BF16) for f in fulls]
        + [pltpu.SemaphoreType.DMA((per * n,)), pltpu.SemaphoreType.DMA((per * n,)),
           jax.ShapeDtypeStruct((8, 128), F32)],
        input_output_aliases={i: i for i in range(n)},
        compiler_params=SPLIT_COPY,
    )(*_in_hbm(fulls), *after)
    return res[:n], res[n], res[n + 1], res[n + 2]


def _gather_wait(fulls, send_sems, recv_sems, kinds, dims, after, name, both=False):
    n, na = len(fulls), len(after)

    def body(*refs):
        ssem, rsem = refs[n], refs[n + 1]
        outs = refs[n + 2 + na:]
        x, y, c, chips = _mesh_pos()
        for a in range(n):
            for k, chip in enumerate(chips):
                if both:
                    for oc in range(2):
                        _both_copy(outs[a], kinds[a], dims[a], a, k, chip, (2 * x + y, c), oc, c,
                                   (ssem, rsem)).wait_send()
                        _both_copy(outs[a], kinds[a], dims[a], a, k, chip, (2 * chip[0] + chip[1], oc), c, oc,
                                   (ssem, rsem)).wait_recv()
                    continue
                to = (chip[0], chip[1], c)
                _ici_copy(outs[a], kinds[a], dims[a], 2 * x + y, c, (ssem, rsem), 3 * a + k, to).wait_send()
                _ici_copy(outs[a], kinds[a], dims[a], 2 * chip[0] + chip[1], c, (ssem, rsem), 3 * a + k, to).wait_recv()

    return pl.pallas_call(
        body, name=name, in_specs=[ANY] * n + [SEM, SEM] + [ANY] * na, out_specs=[ANY] * n,
        out_shape=[pltpu.HBM(f.shape, BF16) for f in fulls],
        input_output_aliases={i: i for i in range(n)},
        compiler_params=SPLIT_COPY,
    )(*_in_hbm(fulls), send_sems, recv_sems, *after)


def _gather_forward(fulls, kinds, dims, name):
    n = len(fulls)

    def body(*refs):
        outs = refs[n:2 * n]
        sems = refs[2 * n:]
        x, y, c, chips = _mesh_pos()
        sib = (x, y, 1 - c)
        cps = []
        for a in range(n):
            for k, chip in enumerate(chips):
                cp = _ici_copy(outs[a], kinds[a], dims[a], 2 * chip[0] + chip[1], c, sems, 3 * a + k, sib)
                cp.start()
                cps.append(cp)
        for a in range(n):
            for k, chip in enumerate(chips):
                _ici_copy(outs[a], kinds[a], dims[a], 2 * chip[0] + chip[1], 1 - c, sems, 3 * a + k, sib).wait_recv()
        for cp in cps:
            cp.wait_send()

    return pl.pallas_call(
        body, name=name, in_specs=[ANY] * n, out_specs=[ANY] * n,
        out_shape=[pltpu.HBM(f.shape, BF16) for f in fulls],
        input_output_aliases={i: i for i in range(n)},
        scratch_shapes=[pltpu.SemaphoreType.DMA((3 * n,)), pltpu.SemaphoreType.DMA((3 * n,))],
    )(*_in_hbm(fulls))


def _pair_copy(src, land, a, x, y, c, sems):
    return pltpu.make_async_remote_copy(
        src_ref=src.at[1 - c], dst_ref=land, send_sem=sems[0].at[a], recv_sem=sems[1].at[a],
        device_id=(x, y, 1 - c), device_id_type=MESH_T)


def _pair_start(grads, lands, name):
    n = len(grads)

    def body(*refs):
        srcs, dsts = refs[2 * n:3 * n], refs[3 * n:4 * n]
        send_sems, recv_sems, token = refs[4 * n:]
        x, y, c, _ = _mesh_pos()
        for a in range(n):
            _pair_copy(srcs[a], dsts[a], a, x, y, c, (send_sems, recv_sems)).start()
        token[...] = jnp.zeros_like(token)

    res = pl.pallas_call(
        body, name=name, in_specs=[ANY] * (2 * n),
        out_specs=[ANY] * (2 * n) + [SEM, SEM, pl.BlockSpec(memory_space=pltpu.VMEM)],
        out_shape=[pltpu.HBM(g.shape, F32) for g in grads]
        + [pltpu.HBM(l.shape, F32) for l in lands]
        + [pltpu.SemaphoreType.DMA((n,)), pltpu.SemaphoreType.DMA((n,)), jax.ShapeDtypeStruct((8, 128), F32)],
        input_output_aliases={i: i for i in range(2 * n)},
        compiler_params=SPLIT_COPY,
    )(*_in_hbm(grads), *_in_hbm(lands))
    return res[:n], res[n:2 * n], res[2 * n], res[2 * n + 1], res[2 * n + 2]


def _pair_wait(grads, lands, send_sems, recv_sems, after, name):
    n, na = len(grads), len(after)

    def body(*refs):
        ssem, rsem = refs[2 * n], refs[2 * n + 1]
        outs = refs[2 * n + 2 + na:]
        x, y, c, _ = _mesh_pos()
        for a in range(n):
            cp = _pair_copy(outs[a], outs[n + a], a, x, y, c, (ssem, rsem))
            cp.wait_send()
            cp.wait_recv()

    res = pl.pallas_call(
        body, name=name, in_specs=[ANY] * (2 * n) + [SEM, SEM] + [ANY] * na, out_specs=[ANY] * (2 * n),
        out_shape=[pltpu.HBM(g.shape, F32) for g in grads]
        + [pltpu.HBM(l.shape, F32) for l in lands],
        input_output_aliases={i: i for i in range(2 * n)},
        compiler_params=SPLIT_COPY,
    )(*_in_hbm(grads), *_in_hbm(lands), send_sems, recv_sems, *after)
    return res[:n], res[n:]


def _pair_sum(g, recv, pos, name):
    _, _, rh, c = g.shape
    tr = _pick_rows(rh, c)

    def body(pos_ref, g_ref, r_ref, o_ref):
        o_ref[...] = (g_ref[...] + r_ref[...]).astype(BF16)

    return pl.pallas_call(
        body, name=name,
        grid_spec=pltpu.PrefetchScalarGridSpec(
            num_scalar_prefetch=1, grid=(3, rh // tr),
            in_specs=[pl.BlockSpec((None, None, tr, c), lambda k, r, p: (p[0], p[2 + k], r, 0)),
                      pl.BlockSpec((None, tr, c), lambda k, r, p: (p[2 + k], r, 0))],
            out_specs=pl.BlockSpec((None, tr, c), lambda k, r, p: (k, r, 0))),
        out_shape=pltpu.HBM((3, rh, c), BF16),
        compiler_params=_cp(("parallel", "parallel")),
    )(pos, *_in_hbm([g, recv]))


def _pair_sum_group(gs, recvs, pos, name):
    n = len(gs)
    _, _, rh, c = gs[0].shape

    def body(pos_ref, *refs):
        a = pl.program_id(0)
        for t in range(n):
            @pl.when(a == t)
            def _(t=t):
                refs[2 * n + t][...] = (refs[t][...] + refs[n + t][...]).astype(BF16)

    def slot(t, a, k):
        return jnp.where(a == t, k, jnp.where(a < t, 0, 2))

    g_specs = [pl.BlockSpec((None, None, rh, c), lambda a, k, p, t=t: (p[0], p[2 + slot(t, a, k)], 0, 0))
               for t in range(n)]
    r_specs = [pl.BlockSpec((None, rh, c), lambda a, k, p, t=t: (p[2 + slot(t, a, k)], 0, 0)) for t in range(n)]
    o_specs = [pl.BlockSpec((None, rh, c), lambda a, k, p, t=t: (slot(t, a, k), 0, 0)) for t in range(n)]
    return pl.pallas_call(
        body, name=name,
        grid_spec=pltpu.PrefetchScalarGridSpec(num_scalar_prefetch=1, grid=(n, 3), in_specs=g_specs + r_specs,
                                               out_specs=o_specs),
        out_shape=[pltpu.HBM((3, rh, c), BF16)] * n,
        compiler_params=_cp(("arbitrary", "arbitrary")),
    )(pos, *_in_hbm(list(gs) + list(recvs)))


def _chip_copy(src, land, a, k, chip, c, sems):
    return pltpu.make_async_remote_copy(
        src_ref=src.at[k], dst_ref=land.at[k], send_sem=sems[0].at[3 * a + k],
        recv_sem=sems[1].at[3 * a + k], device_id=(chip[0], chip[1], c), device_id_type=MESH_T)


def _chip_start(psums, lands, name):
    n = len(psums)

    def body(*refs):
        srcs, dsts = refs[2 * n:3 * n], refs[3 * n:4 * n]
        send_sems, recv_sems, token = refs[4 * n:]
        x, y, c, chips = _mesh_pos()
        for a in range(n):
            for k, chip in enumerate(chips):
                _chip_copy(srcs[a], dsts[a], a, k, chip, c, (send_sems, recv_sems)).start()
        token[...] = jnp.zeros_like(token)

    res = pl.pallas_call(
        body, name=name, in_specs=[ANY] * (2 * n),
        out_specs=[ANY] * (2 * n) + [SEM, SEM, pl.BlockSpec(memory_space=pltpu.VMEM)],
        out_shape=[pltpu.HBM(p.shape, BF16) for p in psums]
        + [pltpu.HBM(l.shape, BF16) for l in lands]
        + [pltpu.SemaphoreType.DMA((3 * n,)), pltpu.SemaphoreType.DMA((3 * n,)), jax.ShapeDtypeStruct((8, 128), F32)],
        input_output_aliases={i: i for i in range(2 * n)},
        compiler_params=SPLIT_COPY,
    )(*_in_hbm(psums), *_in_hbm(lands))
    return res[:n], res[n:2 * n], res[2 * n], res[2 * n + 1], res[2 * n + 2]


def _chip_wait(psums, lands, send_sems, recv_sems, after, name):
    n, na = len(psums), len(after)

    def body(*refs):
        ssem, rsem = refs[2 * n], refs[2 * n + 1]
        outs = refs[2 * n + 2 + na:]
        srcs, dsts = outs[:n], outs[n:]
        x, y, c, chips = _mesh_pos()
        for a in range(n):
            for k, chip in enumerate(chips):
                cp = _chip_copy(srcs[a], dsts[a], a, k, chip, c, (ssem, rsem))
                cp.wait_send()
                cp.wait_recv()

    res = pl.pallas_call(
        body, name=name, in_specs=[ANY] * (2 * n) + [SEM, SEM] + [ANY] * na, out_specs=[ANY] * (2 * n),
        out_shape=[pltpu.HBM(p.shape, BF16) for p in psums]
        + [pltpu.HBM(l.shape, BF16) for l in lands],
        input_output_aliases={i: i for i in range(2 * n)},
        compiler_params=SPLIT_COPY,
    )(*_in_hbm(psums), *_in_hbm(lands), send_sems, recv_sems, *after)
    return res[n:]


def _owner_sum(g, recv_a, recv_b, pos, name):
    _, _, rh, c = g.shape
    tr = _pick_rows(rh, c)

    def body(pos_ref, g_ref, ra_ref, rb_ref, o_ref):
        acc = g_ref[...] + ra_ref[...]
        for k in range(3):
            acc = acc + rb_ref[k].astype(F32)
        o_ref[...] = acc

    return pl.pallas_call(
        body, name=name,
        grid_spec=pltpu.PrefetchScalarGridSpec(
            num_scalar_prefetch=1, grid=(rh // tr,),
            in_specs=[pl.BlockSpec((None, None, tr, c), lambda r, p: (p[0], p[1], r, 0)),
                      pl.BlockSpec((None, tr, c), lambda r, p: (p[1], r, 0)),
                      pl.BlockSpec((3, tr, c), lambda r, p: (0, r, 0))],
            out_specs=pl.BlockSpec((None, tr, c), lambda r, p: (p[0], r, 0))),
        out_shape=pltpu.HBM((2, rh, c), F32),
        compiler_params=_cp(("parallel",)),
    )(pos, *_in_hbm([g, recv_a, recv_b]))


def _sibling_allgather(halves, name):
    n = len(halves)

    def body(*refs):
        outs = refs[n:2 * n]
        send_sems, recv_sems = refs[2 * n:]
        x, y, c, _ = _mesh_pos()
        cps = []
        for a in range(n):
            cp = pltpu.make_async_remote_copy(
                src_ref=outs[a].at[c], dst_ref=outs[a].at[c], send_sem=send_sems.at[a], recv_sem=recv_sems.at[a],
                device_id=(x, y, 1 - c), device_id_type=MESH_T)
            cp.start()
            cps.append(cp)
        for a in range(n):
            cps[a].wait_send()
            pltpu.make_async_remote_copy(
                src_ref=outs[a].at[1 - c], dst_ref=outs[a].at[1 - c], send_sem=send_sems.at[a],
                recv_sem=recv_sems.at[a], device_id=(x, y, 1 - c), device_id_type=MESH_T).wait_recv()

    return pl.pallas_call(
        body, name=name, in_specs=[ANY] * n, out_specs=[ANY] * n,
        out_shape=[pltpu.HBM(h.shape, F32) for h in halves],
        input_output_aliases={i: i for i in range(n)},
        scratch_shapes=[pltpu.SemaphoreType.DMA((n,)), pltpu.SemaphoreType.DMA((n,))],
    )(*_in_hbm(halves))


def _peers(x, y, c):
    rel = [(0, 0, 1), (0, 1, 0), (0, 1, 1), (1, 0, 0), (1, 0, 1), (1, 1, 0), (1, 1, 1)]
    return [((1 - x) if dx else x, (1 - y) if dy else y, (1 - c) if dc else c) for dx, dy, dc in rel]


def _small_copy(src, land, k, peer, slot, sems):
    return pltpu.make_async_remote_copy(src_ref=src, dst_ref=land.at[slot], send_sem=sems[0].at[k],
                                        recv_sem=sems[1].at[k], device_id=peer, device_id_type=MESH_T)


def _small_start(part, land, name):
    def body(p_in, l_in, p_ref, l_ref, send_sems, recv_sems, token):
        x, y, c, _ = _mesh_pos()
        for k, peer in enumerate(_peers(x, y, c)):
            _small_copy(p_ref, l_ref, k, peer, 4 * x + 2 * y + c, (send_sems, recv_sems)).start()
        token[...] = jnp.zeros_like(token)

    return pl.pallas_call(
        body, name=name, in_specs=[ANY, ANY],
        out_specs=[ANY, ANY, SEM, SEM, pl.BlockSpec(memory_space=pltpu.VMEM)],
        out_shape=[pltpu.HBM(part.shape, F32), pltpu.HBM(land.shape, F32), pltpu.SemaphoreType.DMA((7,)),
                   pltpu.SemaphoreType.DMA((7,)), jax.ShapeDtypeStruct((8, 128), F32)],
        input_output_aliases={0: 0, 1: 1},
        compiler_params=SPLIT_COPY,
    )(*_in_hbm([part, land]))


def _small_wait(part, land, send_sems, recv_sems, after, name):
    na = len(after)

    def body(*refs):
        ssem, rsem = refs[2], refs[3]
        p_ref, l_ref = refs[4 + na:]
        x, y, c, _ = _mesh_pos()
        for k, peer in enumerate(_peers(x, y, c)):
            cp = _small_copy(p_ref, l_ref, k, peer, 4 * peer[0] + 2 * peer[1] + peer[2], (ssem, rsem))
            cp.wait_send()
            cp.wait_recv()

    return pl.pallas_call(
        body, name=name, in_specs=[ANY, ANY, SEM, SEM] + [ANY] * na, out_specs=[ANY, ANY],
        out_shape=[pltpu.HBM(part.shape, F32), pltpu.HBM(land.shape, F32)],
        input_output_aliases={0: 0, 1: 1},
        compiler_params=SPLIT_COPY,
    )(*_in_hbm([part, land]), send_sems, recv_sems, *after)


def _pack_small(ln1_g, ln1_b, gln_g, gln_b, ln2_g, ln2_b, ln3_g, ln3_b, b_gates, b_s, w_s):
    rows = [ln1_g, ln1_b, gln_g, gln_b, ln2_g, ln2_b, ln3_g, ln3_b]
    rows = [r.reshape(1, D) for r in rows] + [b_gates.reshape(2, D), b_s.reshape(1, D), jnp.zeros((5, D), F32),
                                             w_s.reshape(128, D)]
    return jnp.concatenate(rows, axis=0)


def _unpack_small(p):
    out = [p[i:i + 1] for i in range(8)]
    return out + [p[8:10].reshape(1, 2 * D), p[10:11].reshape(1, 8, BLK), p[16:144].reshape(1, 8, BLK, BLK)]


GROUPS = (("f1g", "f1u", "f1d"), ("w_in",), ("w_ab", "w_gb", "w_out"), ("f2g", "f2u", "f2d"))
LATE_GROUPS = (2, 3)


def _local_step(x, pos_f, target, P, weights_of, grads_ready, flush, small_ready):
    invf = ROPE_THETA ** (-jnp.arange(0, DH, 2, dtype=F32) / DH)
    invf = jnp.tile(invf, 4).reshape(1, 128)
    b_s_t = P["gmlp_b_s"].T

    W = dict(weights_of(0, []))
    h1b, xh1, rstd1, a1, b1, h1t = _ffn_fwd(x, W["f1g"], W["f1u"], W["f1d"], P["ln1_g"], P["ln1_b"], "ffn1_fwd",
                                                emit_t=True)
    W.update(weights_of(1, [h1b]))
    qkv_c = _proj_qkv_rope(h1b, W["w_in"], pos_f, invf, "proj_qkv_rope")
    z = _matmul(h1b, W["w_in"], "nn", "proj_z", n=2 * GW, b_col0=3 * ATT_W, tm=S, tn=512)
    gl = _matmul(h1b, W["w_in"], "nn", "proj_gates", n=2 * D, b_col0=3 * ATT_W + 2 * GW, tm=S, tn=512)
    og = [_attn_fwd(gi, qkv_c[gi], "attn_fwd_g%d" % gi) for gi in range(NG)]
    y_attn, y_attn_t, lse = _attn_combine([o for o, _ in og], [l for _, l in og], "attn_combine")
    y_gmlp, y_gmlp_t = _gmlp_fwd(z, P["gmlp_ln_g"], P["gmlp_ln_b"], P["gmlp_w_s"], b_s_t, "gmlp_fwd")
    W.update(weights_of(2, [y_gmlp]))
    br_a = _matmul(y_attn, W["w_ab"], "nn", "branch_attn", n=D, tm=1024, tn=D)
    br_b = _matmul(y_gmlp, W["w_gb"], "nn", "branch_gmlp", n=D, tm=1024, tn=D)
    merged, merged_t = _merge_fwd(br_a, br_b, gl, P["b_gates"], "merge_fwd")
    h2, h2b, xh2, rstd2 = _resid_ln(xh1, P["ln1_g"], P["ln1_b"], merged, W["w_out"], P["ln2_g"], P["ln2_b"],
                                    "mix_resid_ln2")
    W.update(weights_of(3, [h2b]))
    dr3, a2, b2, dg3, db3, loss = _ffn_fwd(h2, W["f2g"], W["f2u"], W["f2d"], P["ln3_g"], P["ln3_b"],
                                           "ffn2_fwd_loss", target=target)

    g_f2g, g_f2u, g_f2d, dh2 = _ffn_bwd(dr3, h2b, a2, b2, W["f2g"], W["f2u"], W["f2d"], "ffn2_bwd")
    tok = grads_ready(3, dict(f2g=g_f2g, f2u=g_f2u, f2d=g_f2d))
    dr2, dg2, db2 = _ln_bwd(dh2, xh2, rstd2, P["ln2_g"], "ln2_bwd", after=tok)
    g_wout = _wgrad(merged_t, dr2, 128, D, "dw_out", row_sharded=True)
    dmerged = _matmul(dr2, W["w_out"], "nt", "dmerged", n=D, tm=1024, tn=D)
    dab, dbb, dglb, dbg = _merge_bwd(dmerged, br_a, br_b, gl, P["b_gates"], "merge_bwd")
    tok = flush([dab])
    g_wab = _wgrad(y_attn_t, dab, GRP_W // 2, 256, "dw_attn_branch", row_sharded=False, after=tok)
    g_wgb = _wgrad(y_gmlp_t, dbb, 128, D, "dw_gmlp_branch", row_sharded=True)
    tok = grads_ready(2, dict(w_ab=g_wab, w_gb=g_wgb, w_out=g_wout))
    dy_attn = _matmul(dab, W["w_ab"], "nt", "dy_attn", n=GRP_W, tm=1024, tn=GRP_W, after=tok)
    dy_gmlp = _matmul(dbb, W["w_gb"], "nt", "dy_gmlp", n=GW, tm=1024, tn=GW)
    dzb, dws, dbs_t, dgln_g, dgln_b = _gmlp_bwd(z, dy_gmlp, P["gmlp_ln_g"], P["gmlp_ln_b"], P["gmlp_w_s"], b_s_t,
                                                 "gmlp_bwd")
    cls = _class_order([dy_attn, y_attn, lse], "attn_class_order")
    dqkv_c = []
    for gi in range(NG):
        dy_c, y_c, lse_c = [t[None] if gi == 0 else cls[2 * a + gi - 1] for a, t in enumerate((dy_attn, y_attn, lse))]
        dqkv_c.append(_attn_bwd(gi, qkv_c[gi], dy_c, y_c, lse_c, "attn_bwd_g%d" % gi))
    dproj = _rope_bwd(dqkv_c, dzb, dglb, pos_f, invf, "rope_bwd")
    tok = flush([dproj])
    g_win = _wgrad(h1t, dproj, D // 2, IN_SH, "dw_in", row_sharded=False, after=tok)
    tok = grads_ready(1, dict(w_in=g_win))
    dr1, dg1, db1 = _dh1_ln_bwd(dproj, W["w_in"], dr2, xh1, rstd1, P["ln1_g"], "dh1_ln1_bwd", after=tok)
    tok = flush([dr1])
    tok = tok + small_ready(_pack_small(dg1, db1, dgln_g, dgln_b, dg2, db2, dg3, db3, dbg, dbs_t.T, dws))
    g_f1g, g_f1u, g_f1d, dx = _ffn_bwd(dr1, x.astype(BF16), a1, b1, W["f1g"], W["f1u"], W["f1d"], "ffn1_bwd",
                                       after=tok)
    grads_ready(0, dict(f1g=g_f1g, f1u=g_f1u, f1d=g_f1d))
    flush([dx])
    return loss, dx


TRANSPOSED = ("f1g", "f1u", "f2g", "f2u")
KIND = dict(f1g="stack", f1u="stack", f1d="stack", w_in="col", w_ab="col", w_gb="stack", w_out="stack",
            f2g="stack", f2u="stack", f2d="stack")


def kernel(x, positions, ffn1_w_gate, ffn1_w_up, ffn1_w_down, ln1_g, ln1_b, w_in, b_gates, gmlp_ln_g, gmlp_ln_b, gmlp_w_s, gmlp_b_s, w_attn_branch, w_gmlp_branch, w_out, ln2_g, ln2_b, ffn2_w_gate, ffn2_w_up, ffn2_w_down, ln3_g, ln3_b, loss_target, m_ffn1_w_gate, m_ffn1_w_up, m_ffn1_w_down, m_ln1_g, m_ln1_b, m_w_in, m_b_gates, m_gmlp_ln_g, m_gmlp_ln_b, m_gmlp_w_s, m_gmlp_b_s, m_w_attn_branch, m_w_gmlp_branch, m_w_out, m_ln2_g, m_ln2_b, m_ffn2_w_gate, m_ffn2_w_up, m_ffn2_w_down, m_ln3_g, m_ln3_b, v_ffn1_w_gate, v_ffn1_w_up, v_ffn1_w_down, v_ln1_g, v_ln1_b, v_w_in, v_b_gates, v_gmlp_ln_g, v_gmlp_ln_b, v_gmlp_w_s, v_gmlp_b_s, v_w_attn_branch, v_w_gmlp_branch, v_w_out, v_ln2_g, v_ln2_b, v_ffn2_w_gate, v_ffn2_w_up, v_ffn2_w_down, v_ln3_g, v_ln3_b):
    cx, cy, cc = lax.axis_index("x"), lax.axis_index("y"), lax.axis_index("c")
    pos = jnp.stack([cc, 2 * cx + cy, 2 * (1 - cx) + cy, 2 * cx + 1 - cy, 2 * (1 - cx) + 1 - cy]).astype(jnp.int32)

    w_sh = dict(f1g=ffn1_w_gate, f1u=ffn1_w_up, f1d=ffn1_w_down, w_in=w_in, w_ab=w_attn_branch,
                w_gb=w_gmlp_branch, w_out=w_out, f2g=ffn2_w_gate, f2u=ffn2_w_up, f2d=ffn2_w_down)
    m_sh = dict(f1g=m_ffn1_w_gate, f1u=m_ffn1_w_up, f1d=m_ffn1_w_down, w_in=m_w_in, w_ab=m_w_attn_branch,
                w_gb=m_w_gmlp_branch, w_out=m_w_out, f2g=m_ffn2_w_gate, f2u=m_ffn2_w_up, f2d=m_ffn2_w_down)
    v_sh = dict(f1g=v_ffn1_w_gate, f1u=v_ffn1_w_up, f1d=v_ffn1_w_down, w_in=v_w_in, w_ab=v_w_attn_branch,
                w_gb=v_w_gmlp_branch, w_out=v_w_out, f2g=v_ffn2_w_gate, f2u=v_ffn2_w_up, f2d=v_ffn2_w_down)
    w_sh = {k: (v[0].T if k in TRANSPOSED else v[0]) for k, v in w_sh.items()}
    m_sh = {k: (v[0].T if k in TRANSPOSED else v[0]) for k, v in m_sh.items()}
    v_sh = {k: (v[0].T if k in TRANSPOSED else v[0]) for k, v in v_sh.items()}

    started, tokens = [], []
    for gi, names in enumerate(GROUPS):
        placed = [_place_shard(w_sh[k], KIND[k], pos, "place_" + k) for k in names]
        fulls, ssem, rsem, token = _gather_start(placed, [KIND[k] for k in names], [w_sh[k].shape for k in names],
                                                 tokens[-1:], "gather_start_g%d" % gi, both=gi in LATE_GROUPS)
        started.append((fulls, ssem, rsem))
        tokens.append(token)

    def weights_of(gi, after):
        names = GROUPS[gi]
        kinds, dims = [KIND[k] for k in names], [w_sh[k].shape for k in names]
        fulls, ssem, rsem = started[gi]
        fulls = _gather_wait(fulls, ssem, rsem, kinds, dims, list(after) + (tokens if gi == 0 else []),
                             "gather_wait_g%d" % gi, both=gi in LATE_GROUPS)
        if gi not in LATE_GROUPS:
            fulls = _gather_forward(fulls, kinds, dims, "gather_forward_g%d" % gi)
        return {k: (f.reshape(D, D) if k in ("w_gb", "w_out") else f) for k, f in zip(names, fulls)}

    pending, inflight = [], {}

    def grads_ready(gi, gd):
        grads = [gd[k] for k in GROUPS[gi]]
        lands = [lax.empty(g.shape[1:], F32) for g in grads]
        grads, lands, ssem, rsem, token = _pair_start(grads, lands, "rs_pair_start_g%d" % gi)
        pending.append((gi, grads, lands, ssem, rsem))
        return [token]

    def flush(after):
        gi, grads, lands, ssem, rsem = pending.pop()
        names = GROUPS[gi]
        grads, recv_a = _pair_wait(grads, lands, ssem, rsem, after, "rs_pair_wait_g%d" % gi)
        if len(names) > 1 and len({g.shape for g in grads}) == 1:
            psums = _pair_sum_group(grads, recv_a, pos, "rs_pair_sum_g%d" % gi)
        else:
            psums = [_pair_sum(g, r, pos, "rs_pair_sum_" + k) for g, r, k in zip(grads, recv_a, names)]
        lands = [lax.empty((3,) + p.shape[1:], BF16) for p in psums]
        psums, lands, ssem, rsem, token = _chip_start(psums, lands, "rs_chip_start_g%d" % gi)
        inflight[gi] = (grads, recv_a, psums, lands, ssem, rsem, token)
        return [token]

    P = dict(ln1_g=ln1_g, ln1_b=ln1_b, ln2_g=ln2_g, ln2_b=ln2_b, ln3_g=ln3_g, ln3_b=ln3_b, b_gates=b_gates,
             gmlp_ln_g=gmlp_ln_g, gmlp_ln_b=gmlp_ln_b, gmlp_w_s=gmlp_w_s[0], gmlp_b_s=gmlp_b_s[0])
    pos_f = positions.reshape(S, 1).astype(F32)
    small_state = []

    def small_ready(packed):
        land = jnp.zeros((8, SMALL_ROWS, D), F32)
        packed, land, ssem, rsem, token = _small_start(packed, land, "small_start")
        small_state.append((packed, land, ssem, rsem))
        return [token]

    loss_part, dx = _local_step(x[0], pos_f, loss_target[0], P, weights_of, grads_ready, flush, small_ready)
    loss = lax.psum(loss_part[0, 0], ("x", "y", "c"))

    g_out, d_out, m_out, v_out = {}, {}, {}, {}

    def finish(gis, after, tag):
        names, halves = [], []
        for gi in gis:
            grads, recv_a, psums, lands, ssem, rsem, token = inflight[gi]
            recv_b = _chip_wait(psums, lands, ssem, rsem, after + [inflight[0][6]], "rs_chip_wait_g%d" % gi)
            halves += [_owner_sum(g, ra, rb, pos, "rs_owner_sum_" + k)
                       for g, ra, rb, k in zip(grads, recv_a, recv_b, GROUPS[gi])]
            names += GROUPS[gi]
            after = halves[-1:]
        reduced = _sibling_allgather(halves, "rs_sibling_allgather_" + tag)
        for k, gfull in zip(names, reduced):
            res = _adamw(w_sh[k], gfull.reshape(w_sh[k].shape), m_sh[k], v_sh[k], "adamw_" + k)
            after = [res[1]]
            if k in TRANSPOSED:
                res = [r.T for r in res]
            g_out[k], d_out[k], m_out[k], v_out[k] = [r[None] for r in res]
        return after

    after = finish((3, 2, 1), [], "g321")

    small, parts = _small_wait(*small_state[0], after, "small_wait")
    sp = (ln1_g, ln1_b, gmlp_ln_g, gmlp_ln_b, ln2_g, ln2_b, ln3_g, ln3_b, b_gates, gmlp_b_s, gmlp_w_s)
    sm = (m_ln1_g, m_ln1_b, m_gmlp_ln_g, m_gmlp_ln_b, m_ln2_g, m_ln2_b, m_ln3_g, m_ln3_b, m_b_gates, m_gmlp_b_s,
          m_gmlp_w_s)
    sv = (v_ln1_g, v_ln1_b, v_gmlp_ln_g, v_gmlp_ln_b, v_ln2_g, v_ln2_b, v_ln3_g, v_ln3_b, v_b_gates, v_gmlp_b_s,
          v_gmlp_w_s)
    sg, sd, smn, svn = _small_sum_adamw(parts, small, pos, _pack_small(*sp), _pack_small(*sm), _pack_small(*sv),
                                        "small_adamw")
    names = ("ln1_g", "ln1_b", "gmlp_ln_g", "gmlp_ln_b", "ln2_g", "ln2_b", "ln3_g", "ln3_b", "b_gates", "gmlp_b_s",
             "gmlp_w_s")
    for dst, packed in ((g_out, sg), (d_out, sd), (m_out, smn), (v_out, svn)):
        for nm, val in zip(names, _unpack_small(packed)):
            dst[nm] = val
    finish((0,), [sg], "g0")

    order = ("f1g", "f1u", "f1d", "ln1_g", "ln1_b", "w_in", "b_gates", "gmlp_ln_g", "gmlp_ln_b", "gmlp_w_s", "gmlp_b_s",
             "w_ab", "w_gb", "w_out", "ln2_g", "ln2_b", "f2g", "f2u", "f2d", "ln3_g", "ln3_b")
    outs = [loss, dx[None]]
    for dst in (g_out, d_out, m_out, v_out):
        outs += [dst[k] for k in order]
    return tuple(outs)
```
